```python
import math
import jax
import jax.numpy as jnp
from jax import lax
import numpy as np

D_MODEL = 1024
BATCH = 8
SEQ = 2048
DEPTH = 2

N_META = 16
BLOCK_Q = 128
MIX_WIDTH = D_MODEL // 2
V_HEAD_DIM = 64
MLA_HEADS = MIX_WIDTH // V_HEAD_DIM
QK_NOPE_DIM = 64
QK_ROPE_DIM = 32
QK_HEAD_DIM = QK_NOPE_DIM + QK_ROPE_DIM
Q_LORA_RANK = 3 * D_MODEL // 8
KV_LORA_RANK = D_MODEL // 4
ROPE_BASE = 10000.0
CONV_WIDTH = MIX_WIDTH
CONV_K = 3
S5_WIDTH = MIX_WIDTH
S5_GROUP = 16
S5_GROUPS = S5_WIDTH // S5_GROUP
S5_STATE = 64
N_BRANCH = 3
D_FF = 128 * ((8 * D_MODEL // 3 + 127) // 128)
ALPHA = (2.0 * DEPTH) ** 0.25
BETA = (8.0 * DEPTH) ** -0.25
LN_EPS = 1e-5
RMS_EPS = 1e-6
IN_SPLITS = (Q_LORA_RANK, KV_LORA_RANK, QK_ROPE_DIM, CONV_WIDTH, CONV_WIDTH, CONV_WIDTH, S5_WIDTH, N_BRANCH * D_MODEL)
D_IN = sum(IN_SPLITS)

kernel_name = "hybrid_mla_shortconv_s5_deepnorm_macaron"


def layer_norm(x, g, b):
    xf = x.astype(jnp.float32)
    mu = jnp.mean(xf, axis=-1, keepdims=True)
    var = jnp.mean(jnp.square(xf - mu), axis=-1, keepdims=True)
    y = (xf - mu) * lax.rsqrt(var + LN_EPS) * g.astype(jnp.float32) + b.astype(jnp.float32)
    return y.astype(x.dtype)


def rms_norm(x, g):
    xf = x.astype(jnp.float32)
    y = xf * lax.rsqrt(jnp.mean(jnp.square(xf), axis=-1, keepdims=True) + RMS_EPS) * g.astype(jnp.float32)
    return y.astype(x.dtype)


def swiglu(x, w_gate, w_up, w_down):
    return (jax.nn.silu(x @ w_gate) * (x @ w_up)) @ w_down


def rope(x, pos):
    d = x.shape[-1]
    inv_freq = ROPE_BASE ** (-jnp.arange(0, d, 2, dtype=jnp.float32) / d)
    ang = pos.astype(jnp.float32)[:, None] * inv_freq[None, :]
    cos = jnp.cos(ang)[None, :, None, :]
    sin = jnp.sin(ang)[None, :, None, :]
    xf = x.astype(jnp.float32)
    x1, x2 = xf[..., : d // 2], xf[..., d // 2:]
    return jnp.concatenate([x1 * cos - x2 * sin, x2 * cos + x1 * sin], axis=-1).astype(x.dtype)


def mla_branch(c_q_raw, c_kv_raw, k_rope_raw, q_norm_g, w_uq, kv_norm_g, w_ukv, w_out):
    B, L, _ = c_q_raw.shape
    pos = jnp.arange(L)
    q = (rms_norm(c_q_raw, q_norm_g) @ w_uq).reshape(B, L, MLA_HEADS, QK_HEAD_DIM)
    kv = (rms_norm(c_kv_raw, kv_norm_g) @ w_ukv).reshape(B, L, MLA_HEADS, QK_NOPE_DIM + V_HEAD_DIM)
    q = jnp.concatenate([q[..., :QK_NOPE_DIM], rope(q[..., QK_NOPE_DIM:], pos)], axis=-1)
    k_rope = rope(k_rope_raw[:, :, None, :], pos)
    k = jnp.concatenate([kv[..., :QK_NOPE_DIM], jnp.broadcast_to(k_rope, (B, L, MLA_HEADS, QK_ROPE_DIM))], axis=-1)
    v = kv[..., QK_NOPE_DIM:]
    pad = (-L) % BLOCK_Q
    padw = ((0, 0), (pad, 0), (0, 0), (0, 0))
    q, k, v = jnp.pad(q, padw), jnp.pad(k, padw), jnp.pad(v, padw)
    n_blocks = (L + pad) // BLOCK_Q
    scale = QK_HEAD_DIM ** -0.5
    outs = []
    for i in range(n_blocks):
        kend = (i + 1) * BLOCK_Q
        qs = q[:, i * BLOCK_Q: kend]
        s = jnp.einsum('bqhd,bkhd->bhqk', qs, k[:, :kend]).astype(jnp.float32) * scale
        qi = i * BLOCK_Q + jnp.arange(BLOCK_Q)
        ki = jnp.arange(kend)
        mask = (ki[None, :] <= qi[:, None]) & (ki[None, :] >= pad)
        s = jnp.where(mask[None, None], s, -1e30)
        p = jax.nn.softmax(s, axis=-1).astype(v.dtype)
        outs.append(jnp.einsum('bhqk,bkhd->bqhd', p, v[:, :kend]))
    o = jnp.concatenate(outs, axis=1)[:, pad:]
    return o.reshape(B, L, MLA_HEADS * V_HEAD_DIM) @ w_out


def short_conv_branch(xbar, b_gate, c_gate, conv_w, conv_b, w_out):
    L = xbar.shape[1]
    u = c_gate * xbar
    up = jnp.pad(u, ((0, 0), (CONV_K - 1, 0), (0, 0)))
    y = conv_b + sum(conv_w[j] * up[:, j: j + L] for j in range(CONV_K))
    return (b_gate * y) @ w_out


def _ssm_combine(e1, e2):
    a1r, a1i, b1r, b1i = e1
    a2r, a2i, b2r, b2i = e2
    ar = a2r * a1r - a2i * a1i
    ai = a2r * a1i + a2i * a1r
    br = a2r * b1r - a2i * b1i + b2r
    bi = a2r * b1i + a2i * b1r + b2i
    return (ar, ai, br, bi)


def s5_branch(u, a_re, a_im, log_dt, b_re, b_im, c_re, c_im, d, w_glu, b_glu, w_out):
    B, L, _ = u.shape
    f32 = jnp.float32
    uf = u.astype(f32).reshape(B, L, S5_GROUPS, S5_GROUP)
    a_re, a_im = a_re.astype(f32), a_im.astype(f32)
    dt = jnp.exp(log_dt.astype(f32))[:, None]
    mag = jnp.exp(dt * a_re)
    ab_re, ab_im = mag * jnp.cos(dt * a_im), mag * jnp.sin(dt * a_im)
    den = a_re * a_re + a_im * a_im
    nr, ni = ab_re - 1.0, ab_im
    coef_re = (nr * a_re + ni * a_im) / den
    coef_im = (ni * a_re - nr * a_im) / den
    b_re, b_im = b_re.astype(f32), b_im.astype(f32)
    bb_re = coef_re[..., None] * b_re - coef_im[..., None] * b_im
    bb_im = coef_re[..., None] * b_im + coef_im[..., None] * b_re
    bu_re = jnp.einsum('gnh,blgh->blgn', bb_re, uf)
    bu_im = jnp.einsum('gnh,blgh->blgn', bb_im, uf)
    shp = (1, L, S5_GROUPS, S5_STATE)
    elems = (jnp.broadcast_to(ab_re[None, None], shp), jnp.broadcast_to(ab_im[None, None], shp), bu_re, bu_im)
    _, _, xr, xi = lax.associative_scan(_ssm_combine, elems, axis=1)
    y = (jnp.einsum('ghn,blgn->blgh', c_re.astype(f32), xr)
         - jnp.einsum('ghn,blgn->blgh', c_im.astype(f32), xi)
         + d.astype(f32) * uf)
    y = jax.nn.gelu(y.reshape(B, L, S5_WIDTH).astype(u.dtype))
    y = y * jax.nn.sigmoid(y @ w_glu + b_glu)
    return y @ w_out


def hybrid_layer(x, ffn1_w_gate, ffn1_w_up, ffn1_w_down, ln1_g, ln1_b, w_in,
                 mla_q_norm_g, mla_w_uq, mla_kv_norm_g, mla_w_ukv, mla_w_o,
                 conv_w, conv_b, conv_w_out,
                 s5_a_re, s5_a_im, s5_log_dt, s5_b_re, s5_b_im, s5_c_re, s5_c_im, s5_d, s5_w_glu, s5_b_glu, s5_w_out,
                 w_o, ln2_g, ln2_b, ffn2_w_gate, ffn2_w_up, ffn2_w_down, ln3_g, ln3_b):
    B, L, _ = x.shape
    x = layer_norm(ALPHA * x + 0.5 * swiglu(x, ffn1_w_gate, ffn1_w_up, ffn1_w_down), ln1_g, ln1_b)
    proj = x @ w_in
    c_q, c_kv, k_rope, xbar, b_gate, c_gate, u_s5, gates = jnp.split(proj, np.cumsum(IN_SPLITS)[:-1].tolist(), axis=-1)
    y_a = mla_branch(c_q, c_kv, k_rope, mla_q_norm_g, mla_w_uq, mla_kv_norm_g, mla_w_ukv, mla_w_o)
    y_b = short_conv_branch(xbar, b_gate, c_gate, conv_w, conv_b, conv_w_out)
    y_c = s5_branch(u_s5, s5_a_re, s5_a_im, s5_log_dt, s5_b_re, s5_b_im, s5_c_re, s5_c_im, s5_d, s5_w_glu, s5_b_glu, s5_w_out)
    g = jax.nn.sigmoid(gates.reshape(B, L, N_BRANCH, D_MODEL))
    mixed = g[:, :, 0] * y_a + g[:, :, 1] * y_b + g[:, :, 2] * y_c
    x = layer_norm(ALPHA * x + mixed @ w_o, ln2_g, ln2_b)
    x = layer_norm(ALPHA * x + 0.5 * swiglu(x, ffn2_w_gate, ffn2_w_up, ffn2_w_down), ln3_g, ln3_b)
    return x


def _fwd_setup_inputs(seed: int = 0) -> dict:
    key = jax.random.key(seed)
    keys = iter(jax.random.split(key, 48))
    f32 = jnp.float32

    def nrm(shape, scale):
        return jax.random.normal(next(keys), shape, f32) * scale

    def gain(shape):
        return 1.0 + nrm(shape, 0.01)

    Dp = DEPTH
    n_idx = jnp.arange(S5_STATE, dtype=f32)
    inp = {}
    inp["x"] = nrm((BATCH, SEQ, D_MODEL), 1.0)
    inp["meta"] = nrm((N_META, D_MODEL), 1.0)
    inp["ffn1_w_gate"] = nrm((Dp, D_MODEL, D_FF), D_MODEL ** -0.5)
    inp["ffn1_w_up"] = nrm((Dp, D_MODEL, D_FF), D_MODEL ** -0.5)
    inp["ffn1_w_down"] = nrm((Dp, D_FF, D_MODEL), BETA * D_FF ** -0.5)
    inp["ln1_g"] = gain((Dp, D_MODEL))
    inp["ln1_b"] = nrm((Dp, D_MODEL), 0.01)
    inp["w_in"] = nrm((Dp, D_MODEL, D_IN), D_MODEL ** -0.5)
    inp["mla_q_norm_g"] = gain((Dp, Q_LORA_RANK))
    inp["mla_w_uq"] = nrm((Dp, Q_LORA_RANK, MLA_HEADS * QK_HEAD_DIM), Q_LORA_RANK ** -0.5)
    inp["mla_kv_norm_g"] = gain((Dp, KV_LORA_RANK))
    inp["mla_w_ukv"] = nrm((Dp, KV_LORA_RANK, MLA_HEADS * (QK_NOPE_DIM + V_HEAD_DIM)), KV_LORA_RANK ** -0.5)
    inp["mla_w_o"] = nrm((Dp, MLA_HEADS * V_HEAD_DIM, D_MODEL), (MLA_HEADS * V_HEAD_DIM) ** -0.5)
    inp["conv_w"] = nrm((Dp, CONV_K, CONV_WIDTH), CONV_K ** -0.5)
    inp["conv_b"] = nrm((Dp, CONV_WIDTH), 0.01)
    inp["conv_w_out"] = nrm((Dp, CONV_WIDTH, D_MODEL), CONV_WIDTH ** -0.5)
    inp["s5_a_re"] = -0.5 * jnp.exp(nrm((Dp, S5_GROUPS, S5_STATE), 0.01))
    inp["s5_a_im"] = math.pi * n_idx + nrm((Dp, S5_GROUPS, S5_STATE), 0.01)
    inp["s5_log_dt"] = jax.random.uniform(next(keys), (Dp, S5_GROUPS), f32, math.log(1e-3), math.log(1e-1))
    inp["s5_b_re"] = nrm((Dp, S5_GROUPS, S5_STATE, S5_GROUP), (2 * S5_GROUP) ** -0.5)
    inp["s5_b_im"] = nrm((Dp, S5_GROUPS, S5_STATE, S5_GROUP), (2 * S5_GROUP) ** -0.5)
    inp["s5_c_re"] = nrm((Dp, S5_GROUPS, S5_GROUP, S5_STATE), (2 * S5_STATE) ** -0.5)
    inp["s5_c_im"] = nrm((Dp, S5_GROUPS, S5_GROUP, S5_STATE), (2 * S5_STATE) ** -0.5)
    inp["s5_d"] = nrm((Dp, S5_GROUPS, S5_GROUP), 1.0)
    inp["s5_w_glu"] = nrm((Dp, S5_WIDTH, S5_WIDTH), S5_WIDTH ** -0.5)
    inp["s5_b_glu"] = nrm((Dp, S5_WIDTH), 0.01)
    inp["s5_w_out"] = nrm((Dp, S5_WIDTH, D_MODEL), S5_WIDTH ** -0.5)
    inp["w_o"] = nrm((Dp, D_MODEL, D_MODEL), BETA * D_MODEL ** -0.5)
    inp["ln2_g"] = gain((Dp, D_MODEL))
    inp["ln2_b"] = nrm((Dp, D_MODEL), 0.01)
    inp["ffn2_w_gate"] = nrm((Dp, D_MODEL, D_FF), D_MODEL ** -0.5)
    inp["ffn2_w_up"] = nrm((Dp, D_MODEL, D_FF), D_MODEL ** -0.5)
    inp["ffn2_w_down"] = nrm((Dp, D_FF, D_MODEL), BETA * D_FF ** -0.5)
    inp["ln3_g"] = gain((Dp, D_MODEL))
    inp["ln3_b"] = nrm((Dp, D_MODEL), 0.01)
    return inp


def _fwd_reference(x, meta, ffn1_w_gate, ffn1_w_up, ffn1_w_down, ln1_g, ln1_b, w_in,
              mla_q_norm_g, mla_w_uq, mla_kv_norm_g, mla_w_ukv, mla_w_o,
              conv_w, conv_b, conv_w_out,
              s5_a_re, s5_a_im, s5_log_dt, s5_b_re, s5_b_im, s5_c_re, s5_c_im, s5_d, s5_w_glu, s5_b_glu, s5_w_out,
              w_o, ln2_g, ln2_b, ffn2_w_gate, ffn2_w_up, ffn2_w_down, ln3_g, ln3_b):
    B = x.shape[0]
    h = jnp.concatenate([jnp.broadcast_to(meta[None].astype(x.dtype), (B, N_META, D_MODEL)), x], axis=1)
    for i in range(DEPTH):
        h = hybrid_layer(h, ffn1_w_gate[i], ffn1_w_up[i], ffn1_w_down[i], ln1_g[i], ln1_b[i], w_in[i],
                         mla_q_norm_g[i], mla_w_uq[i], mla_kv_norm_g[i], mla_w_ukv[i], mla_w_o[i],
                         conv_w[i], conv_b[i], conv_w_out[i],
                         s5_a_re[i], s5_a_im[i], s5_log_dt[i], s5_b_re[i], s5_b_im[i], s5_c_re[i], s5_c_im[i],
                         s5_d[i], s5_w_glu[i], s5_b_glu[i], s5_w_out[i],
                         w_o[i], ln2_g[i], ln2_b[i], ffn2_w_gate[i], ffn2_w_up[i], ffn2_w_down[i], ln3_g[i], ln3_b[i])
    return h[:, N_META:]


import jax as _jax
import jax.numpy as _jnp

TWIN_FORMAT = 'train_step'
FWD_PARAMS = ['x', 'meta', 'ffn1_w_gate', 'ffn1_w_up', 'ffn1_w_down', 'ln1_g', 'ln1_b', 'w_in', 'mla_q_norm_g', 'mla_w_uq', 'mla_kv_norm_g', 'mla_w_ukv', 'mla_w_o', 'conv_w', 'conv_b', 'conv_w_out', 's5_a_re', 's5_a_im', 's5_log_dt', 's5_b_re', 's5_b_im', 's5_c_re', 's5_c_im', 's5_d', 's5_w_glu', 's5_b_glu', 's5_w_out', 'w_o', 'ln2_g', 'ln2_b', 'ffn2_w_gate', 'ffn2_w_up', 'ffn2_w_down', 'ln3_g', 'ln3_b']
TWIN_WEIGHTS = ['meta', 'ffn1_w_gate', 'ffn1_w_up', 'ffn1_w_down', 'ln1_g', 'ln1_b', 'w_in', 'mla_q_norm_g', 'mla_w_uq', 'mla_kv_norm_g', 'mla_w_ukv', 'mla_w_o', 'conv_w', 'conv_b', 'conv_w_out', 's5_a_re', 's5_a_im', 's5_log_dt', 's5_b_re', 's5_b_im', 's5_c_re', 's5_c_im', 's5_d', 's5_w_glu', 's5_b_glu', 's5_w_out', 'w_o', 'ln2_g', 'ln2_b', 'ffn2_w_gate', 'ffn2_w_up', 'ffn2_w_down', 'ln3_g', 'ln3_b']
TWIN_DIFF_INPUT = 'x'
TWIN_INPUTS = ['x', 'meta', 'ffn1_w_gate', 'ffn1_w_up', 'ffn1_w_down', 'ln1_g', 'ln1_b', 'w_in', 'mla_q_norm_g', 'mla_w_uq', 'mla_kv_norm_g', 'mla_w_ukv', 'mla_w_o', 'conv_w', 'conv_b', 'conv_w_out', 's5_a_re', 's5_a_im', 's5_log_dt', 's5_b_re', 's5_b_im', 's5_c_re', 's5_c_im', 's5_d', 's5_w_glu', 's5_b_glu', 's5_w_out', 'w_o', 'ln2_g', 'ln2_b', 'ffn2_w_gate', 'ffn2_w_up', 'ffn2_w_down', 'ln3_g', 'ln3_b', 'loss_target', 'm_meta', 'm_ffn1_w_gate', 'm_ffn1_w_up', 'm_ffn1_w_down', 'm_ln1_g', 'm_ln1_b', 'm_w_in', 'm_mla_q_norm_g', 'm_mla_w_uq', 'm_mla_kv_norm_g', 'm_mla_w_ukv', 'm_mla_w_o', 'm_conv_w', 'm_conv_b', 'm_conv_w_out', 'm_s5_a_re', 'm_s5_a_im', 'm_s5_log_dt', 'm_s5_b_re', 'm_s5_b_im', 'm_s5_c_re', 'm_s5_c_im', 'm_s5_d', 'm_s5_w_glu', 'm_s5_b_glu', 'm_s5_w_out', 'm_w_o', 'm_ln2_g', 'm_ln2_b', 'm_ffn2_w_gate', 'm_ffn2_w_up', 'm_ffn2_w_down', 'm_ln3_g', 'm_ln3_b', 'v_meta', 'v_ffn1_w_gate', 'v_ffn1_w_up', 'v_ffn1_w_down', 'v_ln1_g', 'v_ln1_b', 'v_w_in', 'v_mla_q_norm_g', 'v_mla_w_uq', 'v_mla_kv_norm_g', 'v_mla_w_ukv', 'v_mla_w_o', 'v_conv_w', 'v_conv_b', 'v_conv_w_out', 'v_s5_a_re', 'v_s5_a_im', 'v_s5_log_dt', 'v_s5_b_re', 'v_s5_b_im', 'v_s5_c_re', 'v_s5_c_im', 'v_s5_d', 'v_s5_w_glu', 'v_s5_b_glu', 'v_s5_w_out', 'v_w_o', 'v_ln2_g', 'v_ln2_b', 'v_ffn2_w_gate', 'v_ffn2_w_up', 'v_ffn2_w_down', 'v_ln3_g', 'v_ln3_b']
TWIN_OUTPUTS = ['loss', 'grad_x', 'grad_meta', 'grad_ffn1_w_gate', 'grad_ffn1_w_up', 'grad_ffn1_w_down', 'grad_ln1_g', 'grad_ln1_b', 'grad_w_in', 'grad_mla_q_norm_g', 'grad_mla_w_uq', 'grad_mla_kv_norm_g', 'grad_mla_w_ukv', 'grad_mla_w_o', 'grad_conv_w', 'grad_conv_b', 'grad_conv_w_out', 'grad_s5_a_re', 'grad_s5_a_im', 'grad_s5_log_dt', 'grad_s5_b_re', 'grad_s5_b_im', 'grad_s5_c_re', 'grad_s5_c_im', 'grad_s5_d', 'grad_s5_w_glu', 'grad_s5_b_glu', 'grad_s5_w_out', 'grad_w_o', 'grad_ln2_g', 'grad_ln2_b', 'grad_ffn2_w_gate', 'grad_ffn2_w_up', 'grad_ffn2_w_down', 'grad_ln3_g', 'grad_ln3_b', 'delta_meta', 'delta_ffn1_w_gate', 'delta_ffn1_w_up', 'delta_ffn1_w_down', 'delta_ln1_g', 'delta_ln1_b', 'delta_w_in', 'delta_mla_q_norm_g', 'delta_mla_w_uq', 'delta_mla_kv_norm_g', 'delta_mla_w_ukv', 'delta_mla_w_o', 'delta_conv_w', 'delta_conv_b', 'delta_conv_w_out', 'delta_s5_a_re', 'delta_s5_a_im', 'delta_s5_log_dt', 'delta_s5_b_re', 'delta_s5_b_im', 'delta_s5_c_re', 'delta_s5_c_im', 'delta_s5_d', 'delta_s5_w_glu', 'delta_s5_b_glu', 'delta_s5_w_out', 'delta_w_o', 'delta_ln2_g', 'delta_ln2_b', 'delta_ffn2_w_gate', 'delta_ffn2_w_up', 'delta_ffn2_w_down', 'delta_ln3_g', 'delta_ln3_b', 'new_m_meta', 'new_m_ffn1_w_gate', 'new_m_ffn1_w_up', 'new_m_ffn1_w_down', 'new_m_ln1_g', 'new_m_ln1_b', 'new_m_w_in', 'new_m_mla_q_norm_g', 'new_m_mla_w_uq', 'new_m_mla_kv_norm_g', 'new_m_mla_w_ukv', 'new_m_mla_w_o', 'new_m_conv_w', 'new_m_conv_b', 'new_m_conv_w_out', 'new_m_s5_a_re', 'new_m_s5_a_im', 'new_m_s5_log_dt', 'new_m_s5_b_re', 'new_m_s5_b_im', 'new_m_s5_c_re', 'new_m_s5_c_im', 'new_m_s5_d', 'new_m_s5_w_glu', 'new_m_s5_b_glu', 'new_m_s5_w_out', 'new_m_w_o', 'new_m_ln2_g', 'new_m_ln2_b', 'new_m_ffn2_w_gate', 'new_m_ffn2_w_up', 'new_m_ffn2_w_down', 'new_m_ln3_g', 'new_m_ln3_b', 'new_v_meta', 'new_v_ffn1_w_gate', 'new_v_ffn1_w_up', 'new_v_ffn1_w_down', 'new_v_ln1_g', 'new_v_ln1_b', 'new_v_w_in', 'new_v_mla_q_norm_g', 'new_v_mla_w_uq', 'new_v_mla_kv_norm_g', 'new_v_mla_w_ukv', 'new_v_mla_w_o', 'new_v_conv_w', 'new_v_conv_b', 'new_v_conv_w_out', 'new_v_s5_a_re', 'new_v_s5_a_im', 'new_v_s5_log_dt', 'new_v_s5_b_re', 'new_v_s5_b_im', 'new_v_s5_c_re', 'new_v_s5_c_im', 'new_v_s5_d', 'new_v_s5_w_glu', 'new_v_s5_b_glu', 'new_v_s5_w_out', 'new_v_w_o', 'new_v_ln2_g', 'new_v_ln2_b', 'new_v_ffn2_w_gate', 'new_v_ffn2_w_up', 'new_v_ffn2_w_down', 'new_v_ln3_g', 'new_v_ln3_b']
TWIN_LEAF_KINDS = {'loss': 'loss', 'grad_x': 'grad_x', 'grad_meta': 'grad_w', 'grad_ffn1_w_gate': 'grad_w', 'grad_ffn1_w_up': 'grad_w', 'grad_ffn1_w_down': 'grad_w', 'grad_ln1_g': 'grad_w', 'grad_ln1_b': 'grad_w', 'grad_w_in': 'grad_w', 'grad_mla_q_norm_g': 'grad_w', 'grad_mla_w_uq': 'grad_w', 'grad_mla_kv_norm_g': 'grad_w', 'grad_mla_w_ukv': 'grad_w', 'grad_mla_w_o': 'grad_w', 'grad_conv_w': 'grad_w', 'grad_conv_b': 'grad_w', 'grad_conv_w_out': 'grad_w', 'grad_s5_a_re': 'grad_w', 'grad_s5_a_im': 'grad_w', 'grad_s5_log_dt': 'grad_w', 'grad_s5_b_re': 'grad_w', 'grad_s5_b_im': 'grad_w', 'grad_s5_c_re': 'grad_w', 'grad_s5_c_im': 'grad_w', 'grad_s5_d': 'grad_w', 'grad_s5_w_glu': 'grad_w', 'grad_s5_b_glu': 'grad_w', 'grad_s5_w_out': 'grad_w', 'grad_w_o': 'grad_w', 'grad_ln2_g': 'grad_w', 'grad_ln2_b': 'grad_w', 'grad_ffn2_w_gate': 'grad_w', 'grad_ffn2_w_up': 'grad_w', 'grad_ffn2_w_down': 'grad_w', 'grad_ln3_g': 'grad_w', 'grad_ln3_b': 'grad_w', 'delta_meta': 'delta_w', 'delta_ffn1_w_gate': 'delta_w', 'delta_ffn1_w_up': 'delta_w', 'delta_ffn1_w_down': 'delta_w', 'delta_ln1_g': 'delta_w', 'delta_ln1_b': 'delta_w', 'delta_w_in': 'delta_w', 'delta_mla_q_norm_g': 'delta_w', 'delta_mla_w_uq': 'delta_w', 'delta_mla_kv_norm_g': 'delta_w', 'delta_mla_w_ukv': 'delta_w', 'delta_mla_w_o': 'delta_w', 'delta_conv_w': 'delta_w', 'delta_conv_b': 'delta_w', 'delta_conv_w_out': 'delta_w', 'delta_s5_a_re': 'delta_w', 'delta_s5_a_im': 'delta_w', 'delta_s5_log_dt': 'delta_w', 'delta_s5_b_re': 'delta_w', 'delta_s5_b_im': 'delta_w', 'delta_s5_c_re': 'delta_w', 'delta_s5_c_im': 'delta_w', 'delta_s5_d': 'delta_w', 'delta_s5_w_glu': 'delta_w', 'delta_s5_b_glu': 'delta_w', 'delta_s5_w_out': 'delta_w', 'delta_w_o': 'delta_w', 'delta_ln2_g': 'delta_w', 'delta_ln2_b': 'delta_w', 'delta_ffn2_w_gate': 'delta_w', 'delta_ffn2_w_up': 'delta_w', 'delta_ffn2_w_down': 'delta_w', 'delta_ln3_g': 'delta_w', 'delta_ln3_b': 'delta_w', 'new_m_meta': 'new_m', 'new_m_ffn1_w_gate': 'new_m', 'new_m_ffn1_w_up': 'new_m', 'new_m_ffn1_w_down': 'new_m', 'new_m_ln1_g': 'new_m', 'new_m_ln1_b': 'new_m', 'new_m_w_in': 'new_m', 'new_m_mla_q_norm_g': 'new_m', 'new_m_mla_w_uq': 'new_m', 'new_m_mla_kv_norm_g': 'new_m', 'new_m_mla_w_ukv': 'new_m', 'new_m_mla_w_o': 'new_m', 'new_m_conv_w': 'new_m', 'new_m_conv_b': 'new_m', 'new_m_conv_w_out': 'new_m', 'new_m_s5_a_re': 'new_m', 'new_m_s5_a_im': 'new_m', 'new_m_s5_log_dt': 'new_m', 'new_m_s5_b_re': 'new_m', 'new_m_s5_b_im': 'new_m', 'new_m_s5_c_re': 'new_m', 'new_m_s5_c_im': 'new_m', 'new_m_s5_d': 'new_m', 'new_m_s5_w_glu': 'new_m', 'new_m_s5_b_glu': 'new_m', 'new_m_s5_w_out': 'new_m', 'new_m_w_o': 'new_m', 'new_m_ln2_g': 'new_m', 'new_m_ln2_b': 'new_m', 'new_m_ffn2_w_gate': 'new_m', 'new_m_ffn2_w_up': 'new_m', 'new_m_ffn2_w_down': 'new_m', 'new_m_ln3_g': 'new_m', 'new_m_ln3_b': 'new_m', 'new_v_meta': 'new_v', 'new_v_ffn1_w_gate': 'new_v', 'new_v_ffn1_w_up': 'new_v', 'new_v_ffn1_w_down': 'new_v', 'new_v_ln1_g': 'new_v', 'new_v_ln1_b': 'new_v', 'new_v_w_in': 'new_v', 'new_v_mla_q_norm_g': 'new_v', 'new_v_mla_w_uq': 'new_v', 'new_v_mla_kv_norm_g': 'new_v', 'new_v_mla_w_ukv': 'new_v', 'new_v_mla_w_o': 'new_v', 'new_v_conv_w': 'new_v', 'new_v_conv_b': 'new_v', 'new_v_conv_w_out': 'new_v', 'new_v_s5_a_re': 'new_v', 'new_v_s5_a_im': 'new_v', 'new_v_s5_log_dt': 'new_v', 'new_v_s5_b_re': 'new_v', 'new_v_s5_b_im': 'new_v', 'new_v_s5_c_re': 'new_v', 'new_v_s5_c_im': 'new_v', 'new_v_s5_d': 'new_v', 'new_v_s5_w_glu': 'new_v', 'new_v_s5_b_glu': 'new_v', 'new_v_s5_w_out': 'new_v', 'new_v_w_o': 'new_v', 'new_v_ln2_g': 'new_v', 'new_v_ln2_b': 'new_v', 'new_v_ffn2_w_gate': 'new_v', 'new_v_ffn2_w_up': 'new_v', 'new_v_ffn2_w_down': 'new_v', 'new_v_ln3_g': 'new_v', 'new_v_ln3_b': 'new_v'}


def _forward(args):
    return _fwd_reference(*[args[k] for k in FWD_PARAMS])


def _output_shape():
    out = _jax.eval_shape(lambda: _forward(_fwd_setup_inputs(0)))
    return out.shape, out.dtype

N_MICROBATCH = 1
ADAM_LR = 0.001
ADAM_B1 = 0.9
ADAM_B2 = 0.999
ADAM_EPS = 1e-08
ADAM_WD = 0.01
ADAM_STEP = 10
PER_EXAMPLE_BATCH_AXIS = {'x': 0, 'loss_target': 0}
SHARED_INPUTS = []
_WEIGHT_DTYPES = {'meta': _jnp.float32, 'ffn1_w_gate': _jnp.float32, 'ffn1_w_up': _jnp.float32, 'ffn1_w_down': _jnp.float32, 'ln1_g': _jnp.float32, 'ln1_b': _jnp.float32, 'w_in': _jnp.float32, 'mla_q_norm_g': _jnp.float32, 'mla_w_uq': _jnp.float32, 'mla_kv_norm_g': _jnp.float32, 'mla_w_ukv': _jnp.float32, 'mla_w_o': _jnp.float32, 'conv_w': _jnp.float32, 'conv_b': _jnp.float32, 'conv_w_out': _jnp.float32, 's5_a_re': _jnp.float32, 's5_a_im': _jnp.float32, 's5_log_dt': _jnp.float32, 's5_b_re': _jnp.float32, 's5_b_im': _jnp.float32, 's5_c_re': _jnp.float32, 's5_c_im': _jnp.float32, 's5_d': _jnp.float32, 's5_w_glu': _jnp.float32, 's5_b_glu': _jnp.float32, 's5_w_out': _jnp.float32, 'w_o': _jnp.float32, 'ln2_g': _jnp.float32, 'ln2_b': _jnp.float32, 'ffn2_w_gate': _jnp.float32, 'ffn2_w_up': _jnp.float32, 'ffn2_w_down': _jnp.float32, 'ln3_g': _jnp.float32, 'ln3_b': _jnp.float32}
MOMENT_SCALE = {'meta': 1.346732e-03, 'ffn1_w_gate': 8.748648e-03, 'ffn1_w_up': 8.476085e-03, 'ffn1_w_down': 2.813079e-02, 'ln1_g': 2.603893e-01, 'ln1_b': 1.655241e-01, 'w_in': 1.880089e-02, 'mla_q_norm_g': 6.421094e-03, 'mla_w_uq': 4.566108e-03, 'mla_kv_norm_g': 1.111747e-02, 'mla_w_ukv': 5.450406e-03, 'mla_w_o': 4.318570e-03, 'conv_w': 3.433752e-02, 'conv_b': 3.535607e-02, 'conv_w_out': 2.425857e-02, 's5_a_re': 6.249089e-04, 's5_a_im': 6.713774e-04, 's5_log_dt': 3.427460e-01, 's5_b_re': 4.428750e-04, 's5_b_im': 4.342216e-04, 's5_c_re': 8.462167e-04, 's5_c_im': 8.669128e-04, 's5_d': 1.380641e-02, 's5_w_glu': 3.496247e-03, 's5_b_glu': 5.442566e-03, 's5_w_out': 8.767453e-03, 'w_o': 5.222454e-02, 'ln2_g': 2.699366e-01, 'ln2_b': 1.659061e-01, 'ffn2_w_gate': 8.366583e-03, 'ffn2_w_up': 8.094040e-03, 'ffn2_w_down': 2.683887e-02, 'ln3_g': 1.130982e+01, 'ln3_b': 4.918372e-01}


def _to_microbatches(a, axis):
    t = _jnp.moveaxis(a, axis, 0)
    t = t.reshape((N_MICROBATCH, t.shape[0] // N_MICROBATCH) + t.shape[1:])
    return _jnp.moveaxis(t, 1, axis + 1)


def setup_inputs(seed: int = 0) -> dict:
    inp = _fwd_setup_inputs(seed)
    key = _jax.random.fold_in(_jax.random.key(seed), 7919)
    shape, _ = _output_shape()
    out = dict(inp)
    out["loss_target"] = _jax.random.normal(_jax.random.fold_in(key, 0), shape, _jnp.float32)
    for i, name in enumerate(TWIN_WEIGHTS):
        w = inp[name].astype(_jnp.float32)
        if MOMENT_SCALE is None:
            s = _jnp.sqrt(_jnp.mean(_jnp.square(w)) + 1e-30)
        else:
            s = MOMENT_SCALE[name]
        km, kv = _jax.random.split(_jax.random.fold_in(key, i + 1))
        out[name] = w
        out["m_" + name] = s * _jax.random.normal(km, w.shape, _jnp.float32)
        out["v_" + name] = (s * s) * _jax.random.uniform(kv, w.shape, _jnp.float32, 0.5, 1.5)
    if N_MICROBATCH > 1:
        for name, axis in PER_EXAMPLE_BATCH_AXIS.items():
            out[name] = _to_microbatches(out[name], axis)
    return {'x': out['x'], 'meta': out['meta'], 'ffn1_w_gate': out['ffn1_w_gate'], 'ffn1_w_up': out['ffn1_w_up'], 'ffn1_w_down': out['ffn1_w_down'], 'ln1_g': out['ln1_g'], 'ln1_b': out['ln1_b'], 'w_in': out['w_in'], 'mla_q_norm_g': out['mla_q_norm_g'], 'mla_w_uq': out['mla_w_uq'], 'mla_kv_norm_g': out['mla_kv_norm_g'], 'mla_w_ukv': out['mla_w_ukv'], 'mla_w_o': out['mla_w_o'], 'conv_w': out['conv_w'], 'conv_b': out['conv_b'], 'conv_w_out': out['conv_w_out'], 's5_a_re': out['s5_a_re'], 's5_a_im': out['s5_a_im'], 's5_log_dt': out['s5_log_dt'], 's5_b_re': out['s5_b_re'], 's5_b_im': out['s5_b_im'], 's5_c_re': out['s5_c_re'], 's5_c_im': out['s5_c_im'], 's5_d': out['s5_d'], 's5_w_glu': out['s5_w_glu'], 's5_b_glu': out['s5_b_glu'], 's5_w_out': out['s5_w_out'], 'w_o': out['w_o'], 'ln2_g': out['ln2_g'], 'ln2_b': out['ln2_b'], 'ffn2_w_gate': out['ffn2_w_gate'], 'ffn2_w_up': out['ffn2_w_up'], 'ffn2_w_down': out['ffn2_w_down'], 'ln3_g': out['ln3_g'], 'ln3_b': out['ln3_b'], 'loss_target': out['loss_target'], 'm_meta': out['m_meta'], 'm_ffn1_w_gate': out['m_ffn1_w_gate'], 'm_ffn1_w_up': out['m_ffn1_w_up'], 'm_ffn1_w_down': out['m_ffn1_w_down'], 'm_ln1_g': out['m_ln1_g'], 'm_ln1_b': out['m_ln1_b'], 'm_w_in': out['m_w_in'], 'm_mla_q_norm_g': out['m_mla_q_norm_g'], 'm_mla_w_uq': out['m_mla_w_uq'], 'm_mla_kv_norm_g': out['m_mla_kv_norm_g'], 'm_mla_w_ukv': out['m_mla_w_ukv'], 'm_mla_w_o': out['m_mla_w_o'], 'm_conv_w': out['m_conv_w'], 'm_conv_b': out['m_conv_b'], 'm_conv_w_out': out['m_conv_w_out'], 'm_s5_a_re': out['m_s5_a_re'], 'm_s5_a_im': out['m_s5_a_im'], 'm_s5_log_dt': out['m_s5_log_dt'], 'm_s5_b_re': out['m_s5_b_re'], 'm_s5_b_im': out['m_s5_b_im'], 'm_s5_c_re': out['m_s5_c_re'], 'm_s5_c_im': out['m_s5_c_im'], 'm_s5_d': out['m_s5_d'], 'm_s5_w_glu': out['m_s5_w_glu'], 'm_s5_b_glu': out['m_s5_b_glu'], 'm_s5_w_out': out['m_s5_w_out'], 'm_w_o': out['m_w_o'], 'm_ln2_g': out['m_ln2_g'], 'm_ln2_b': out['m_ln2_b'], 'm_ffn2_w_gate': out['m_ffn2_w_gate'], 'm_ffn2_w_up': out['m_ffn2_w_up'], 'm_ffn2_w_down': out['m_ffn2_w_down'], 'm_ln3_g': out['m_ln3_g'], 'm_ln3_b': out['m_ln3_b'], 'v_meta': out['v_meta'], 'v_ffn1_w_gate': out['v_ffn1_w_gate'], 'v_ffn1_w_up': out['v_ffn1_w_up'], 'v_ffn1_w_down': out['v_ffn1_w_down'], 'v_ln1_g': out['v_ln1_g'], 'v_ln1_b': out['v_ln1_b'], 'v_w_in': out['v_w_in'], 'v_mla_q_norm_g': out['v_mla_q_norm_g'], 'v_mla_w_uq': out['v_mla_w_uq'], 'v_mla_kv_norm_g': out['v_mla_kv_norm_g'], 'v_mla_w_ukv': out['v_mla_w_ukv'], 'v_mla_w_o': out['v_mla_w_o'], 'v_conv_w': out['v_conv_w'], 'v_conv_b': out['v_conv_b'], 'v_conv_w_out': out['v_conv_w_out'], 'v_s5_a_re': out['v_s5_a_re'], 'v_s5_a_im': out['v_s5_a_im'], 'v_s5_log_dt': out['v_s5_log_dt'], 'v_s5_b_re': out['v_s5_b_re'], 'v_s5_b_im': out['v_s5_b_im'], 'v_s5_c_re': out['v_s5_c_re'], 'v_s5_c_im': out['v_s5_c_im'], 'v_s5_d': out['v_s5_d'], 'v_s5_w_glu': out['v_s5_w_glu'], 'v_s5_b_glu': out['v_s5_b_glu'], 'v_s5_w_out': out['v_s5_w_out'], 'v_w_o': out['v_w_o'], 'v_ln2_g': out['v_ln2_g'], 'v_ln2_b': out['v_ln2_b'], 'v_ffn2_w_gate': out['v_ffn2_w_gate'], 'v_ffn2_w_up': out['v_ffn2_w_up'], 'v_ffn2_w_down': out['v_ffn2_w_down'], 'v_ln3_g': out['v_ln3_g'], 'v_ln3_b': out['v_ln3_b']}


def _loss(weights, diff, rest, loss_target):
    with _jax.named_scope("forward"):
        args = {**rest, TWIN_DIFF_INPUT: diff, **{k: w.astype(_WEIGHT_DTYPES[k]) for k, w in weights.items()}}
        y = _forward(args)
    with _jax.named_scope("loss_head"):
        err = _jnp.square(y.astype(_jnp.float32) - loss_target)
        return 0.5 * _jnp.sum(_jnp.mean(err, axis=-1)) if err.ndim else 0.5 * err


def _adamw(w, g, m, v):
    m = ADAM_B1 * m + (1.0 - ADAM_B1) * g
    v = ADAM_B2 * v + (1.0 - ADAM_B2) * _jnp.square(g)
    m_hat = m / (1.0 - ADAM_B1 ** ADAM_STEP)
    v_hat = v / (1.0 - ADAM_B2 ** ADAM_STEP)
    delta = -ADAM_LR * (m_hat / (_jnp.sqrt(v_hat) + ADAM_EPS) + ADAM_WD * w)
    return delta, m, v


def reference(x, meta, ffn1_w_gate, ffn1_w_up, ffn1_w_down, ln1_g, ln1_b, w_in, mla_q_norm_g, mla_w_uq, mla_kv_norm_g, mla_w_ukv, mla_w_o, conv_w, conv_b, conv_w_out, s5_a_re, s5_a_im, s5_log_dt, s5_b_re, s5_b_im, s5_c_re, s5_c_im, s5_d, s5_w_glu, s5_b_glu, s5_w_out, w_o, ln2_g, ln2_b, ffn2_w_gate, ffn2_w_up, ffn2_w_down, ln3_g, ln3_b, loss_target, m_meta, m_ffn1_w_gate, m_ffn1_w_up, m_ffn1_w_down, m_ln1_g, m_ln1_b, m_w_in, m_mla_q_norm_g, m_mla_w_uq, m_mla_kv_norm_g, m_mla_w_ukv, m_mla_w_o, m_conv_w, m_conv_b, m_conv_w_out, m_s5_a_re, m_s5_a_im, m_s5_log_dt, m_s5_b_re, m_s5_b_im, m_s5_c_re, m_s5_c_im, m_s5_d, m_s5_w_glu, m_s5_b_glu, m_s5_w_out, m_w_o, m_ln2_g, m_ln2_b, m_ffn2_w_gate, m_ffn2_w_up, m_ffn2_w_down, m_ln3_g, m_ln3_b, v_meta, v_ffn1_w_gate, v_ffn1_w_up, v_ffn1_w_down, v_ln1_g, v_ln1_b, v_w_in, v_mla_q_norm_g, v_mla_w_uq, v_mla_kv_norm_g, v_mla_w_ukv, v_mla_w_o, v_conv_w, v_conv_b, v_conv_w_out, v_s5_a_re, v_s5_a_im, v_s5_log_dt, v_s5_b_re, v_s5_b_im, v_s5_c_re, v_s5_c_im, v_s5_d, v_s5_w_glu, v_s5_b_glu, v_s5_w_out, v_w_o, v_ln2_g, v_ln2_b, v_ffn2_w_gate, v_ffn2_w_up, v_ffn2_w_down, v_ln3_g, v_ln3_b):
    given = dict(x=x, meta=meta, ffn1_w_gate=ffn1_w_gate, ffn1_w_up=ffn1_w_up, ffn1_w_down=ffn1_w_down, ln1_g=ln1_g, ln1_b=ln1_b, w_in=w_in, mla_q_norm_g=mla_q_norm_g, mla_w_uq=mla_w_uq, mla_kv_norm_g=mla_kv_norm_g, mla_w_ukv=mla_w_ukv, mla_w_o=mla_w_o, conv_w=conv_w, conv_b=conv_b, conv_w_out=conv_w_out, s5_a_re=s5_a_re, s5_a_im=s5_a_im, s5_log_dt=s5_log_dt, s5_b_re=s5_b_re, s5_b_im=s5_b_im, s5_c_re=s5_c_re, s5_c_im=s5_c_im, s5_d=s5_d, s5_w_glu=s5_w_glu, s5_b_glu=s5_b_glu, s5_w_out=s5_w_out, w_o=w_o, ln2_g=ln2_g, ln2_b=ln2_b, ffn2_w_gate=ffn2_w_gate, ffn2_w_up=ffn2_w_up, ffn2_w_down=ffn2_w_down, ln3_g=ln3_g, ln3_b=ln3_b, loss_target=loss_target, m_meta=m_meta, m_ffn1_w_gate=m_ffn1_w_gate, m_ffn1_w_up=m_ffn1_w_up, m_ffn1_w_down=m_ffn1_w_down, m_ln1_g=m_ln1_g, m_ln1_b=m_ln1_b, m_w_in=m_w_in, m_mla_q_norm_g=m_mla_q_norm_g, m_mla_w_uq=m_mla_w_uq, m_mla_kv_norm_g=m_mla_kv_norm_g, m_mla_w_ukv=m_mla_w_ukv, m_mla_w_o=m_mla_w_o, m_conv_w=m_conv_w, m_conv_b=m_conv_b, m_conv_w_out=m_conv_w_out, m_s5_a_re=m_s5_a_re, m_s5_a_im=m_s5_a_im, m_s5_log_dt=m_s5_log_dt, m_s5_b_re=m_s5_b_re, m_s5_b_im=m_s5_b_im, m_s5_c_re=m_s5_c_re, m_s5_c_im=m_s5_c_im, m_s5_d=m_s5_d, m_s5_w_glu=m_s5_w_glu, m_s5_b_glu=m_s5_b_glu, m_s5_w_out=m_s5_w_out, m_w_o=m_w_o, m_ln2_g=m_ln2_g, m_ln2_b=m_ln2_b, m_ffn2_w_gate=m_ffn2_w_gate, m_ffn2_w_up=m_ffn2_w_up, m_ffn2_w_down=m_ffn2_w_down, m_ln3_g=m_ln3_g, m_ln3_b=m_ln3_b, v_meta=v_meta, v_ffn1_w_gate=v_ffn1_w_gate, v_ffn1_w_up=v_ffn1_w_up, v_ffn1_w_down=v_ffn1_w_down, v_ln1_g=v_ln1_g, v_ln1_b=v_ln1_b, v_w_in=v_w_in, v_mla_q_norm_g=v_mla_q_norm_g, v_mla_w_uq=v_mla_w_uq, v_mla_kv_norm_g=v_mla_kv_norm_g, v_mla_w_ukv=v_mla_w_ukv, v_mla_w_o=v_mla_w_o, v_conv_w=v_conv_w, v_conv_b=v_conv_b, v_conv_w_out=v_conv_w_out, v_s5_a_re=v_s5_a_re, v_s5_a_im=v_s5_a_im, v_s5_log_dt=v_s5_log_dt, v_s5_b_re=v_s5_b_re, v_s5_b_im=v_s5_b_im, v_s5_c_re=v_s5_c_re, v_s5_c_im=v_s5_c_im, v_s5_d=v_s5_d, v_s5_w_glu=v_s5_w_glu, v_s5_b_glu=v_s5_b_glu, v_s5_w_out=v_s5_w_out, v_w_o=v_w_o, v_ln2_g=v_ln2_g, v_ln2_b=v_ln2_b, v_ffn2_w_gate=v_ffn2_w_gate, v_ffn2_w_up=v_ffn2_w_up, v_ffn2_w_down=v_ffn2_w_down, v_ln3_g=v_ln3_g, v_ln3_b=v_ln3_b)
    weights = {n: given[n] for n in TWIN_WEIGHTS}
    shared = {n: given[n] for n in SHARED_INPUTS}
    per_example = {n: given[n] for n in ['x']}
    grad_fn = _jax.value_and_grad(_loss, argnums=(0, 1))

    def one_microbatch(ex, loss_target):
        ex = dict(ex)
        diff = ex.pop(TWIN_DIFF_INPUT)
        return grad_fn(weights, diff, {**shared, **ex}, loss_target)

    if N_MICROBATCH == 1:
        loss, (grad_w, grad_x) = one_microbatch(per_example, given["loss_target"])
    else:
        def body(carry, xs):
            loss_sum, grad_sum = carry
            l_k, (gw_k, gx_k) = one_microbatch(xs[0], xs[1])
            with _jax.named_scope("update"):
                return (loss_sum + l_k, _jax.tree.map(_jnp.add, grad_sum, gw_k)), gx_k

        init = (_jnp.zeros((), _jnp.float32), _jax.tree.map(_jnp.zeros_like, weights))
        (loss, grad_w), grad_x = _jax.lax.scan(body, init, (per_example, given["loss_target"]))
    with _jax.named_scope("update"):
        delta_w, new_m, new_v = {}, {}, {}
        for n in TWIN_WEIGHTS:
            delta_w[n], new_m[n], new_v[n] = _adamw(weights[n], grad_w[n], given["m_" + n], given["v_" + n])
    return (loss, grad_x, *[grad_w[n] for n in TWIN_WEIGHTS], *[delta_w[n] for n in TWIN_WEIGHTS],
            *[new_m[n] for n in TWIN_WEIGHTS], *[new_v[n] for n in TWIN_WEIGHTS])
```

```python
import functools
import math

import numpy as np
import jax
import jax.numpy as jnp
from jax import lax
from jax.experimental import pallas as pl
from jax.experimental.pallas import tpu as pltpu

F32 = jnp.float32
BF16 = jnp.bfloat16

D_MODEL = 1024
DEPTH = 2
N_META = 16
PAD = 112
X0 = PAD + N_META
N_HEADS = 8
D_NOPE = 64
D_ROPE = 32
D_V = 64
Q_RANK = 384
KV_RANK = 256
MIX = 512
S5_GROUPS = 32
S5_GROUP = 16
S5_STATE = 64
S5_LANES = S5_GROUPS * S5_STATE
D_FF = 2816
N_SHARD = 4
FF_SHARD = D_FF // N_SHARD
D_IN = 5792
P_IN = 6144
ALPHA = (2.0 * DEPTH) ** 0.25
LN_EPS = 1e-5
RMS_EPS = 1e-6
ATT_SCALE = (D_NOPE + D_ROPE) ** -0.5
ROPE_BASE = 10000.0
ADAM_LR, ADAM_B1, ADAM_B2, ADAM_EPS, ADAM_WD, ADAM_STEP = 0.001, 0.9, 0.999, 1e-08, 0.01, 10
SCAN_CHUNK = 128
VMEM_LIMIT = 52 * 2 ** 20
MESH = pl.DeviceIdType.MESH


def _cparams(**kw):
    return pltpu.CompilerParams(vmem_limit_bytes=VMEM_LIMIT, **kw)


def _tile(n):
    if n <= 1088:
        return n
    for t in (1024, 544, 512, 272, 256, 128):
        if n % t == 0:
            return t
    return n


def _row_tile(lp):
    for t in (544, 272, 128):
        if lp % t == 0:
            return t
    return lp


def _sigmoid(x):
    return 1.0 / (1.0 + jnp.exp(-x))


_GELU_C = math.sqrt(2.0 / math.pi)


def _gelu(x):
    return 0.5 * x * (1.0 + jnp.tanh(_GELU_C * (x + 0.044715 * x * x * x)))


def _gelu_grad(x):
    t = jnp.tanh(_GELU_C * (x + 0.044715 * x * x * x))
    return 0.5 * (1.0 + t) + 0.5 * x * (1.0 - t * t) * _GELU_C * (1.0 + 3.0 * 0.044715 * x * x)


def _dot(a, b, ca, cb, precision=None):
    return lax.dot_general(a, b, (((ca,), (cb,)), ((), ())), preferred_element_type=F32, precision=precision)


def matmul(a, b, *, name, ta=False, tb=False, ab='n', bb='n', res=None, res_scale=1.0, scale=1.0, out_dtype=F32):
    if ta:
        _, K, M = a.shape
    else:
        _, M, K = a.shape
    if tb:
        _, N, K2 = b.shape
    else:
        _, K2, N = b.shape
    assert K == K2, (a.shape, b.shape)
    n_out = max(a.shape[0] if ab == 'o' else 1, b.shape[0] if bb == 'o' else 1)
    n_red = max(a.shape[0] if ab == 'r' else 1, b.shape[0] if bb == 'r' else 1)
    tm, tn = _tile(M), _tile(N)
    tk = K if K <= 2304 else _tile(K)
    nkt = K // tk
    n_steps = n_red * nkt

    def bsel(mode, o, r):
        if mode == 'o':
            return o
        if mode == 'r':
            return r // nkt if nkt > 1 else r
        return 0

    def ksel(r):
        if nkt == 1:
            return 0
        return r % nkt if n_red > 1 else r

    a_map = (lambda o, i, j, r: (bsel(ab, o, r), ksel(r), i)) if ta else (lambda o, i, j, r: (bsel(ab, o, r), i, ksel(r)))
    b_map = (lambda o, i, j, r: (bsel(bb, o, r), j, ksel(r))) if tb else (lambda o, i, j, r: (bsel(bb, o, r), ksel(r), j))
    o_map = lambda o, i, j, r: (o, i, j)
    in_specs = [pl.BlockSpec((None, tk, tm) if ta else (None, tm, tk), a_map),
                pl.BlockSpec((None, tn, tk) if tb else (None, tk, tn), b_map)]
    operands = [a, b]
    if res is not None:
        in_specs.append(pl.BlockSpec((None, tm, tn), o_map))
        operands.append(res)
    has_res = res is not None

    def body(*refs):
        a_ref, b_ref = refs[0], refs[1]
        res_ref = refs[2] if has_res else None
        o_ref = refs[3] if has_res else refs[2]
        part = _dot(a_ref[...].astype(BF16), b_ref[...].astype(BF16), 0 if ta else 1, 1 if tb else 0)

        def finish(acc):
            v = acc if scale == 1.0 else acc * scale
            if has_res:
                v = v + res_scale * res_ref[...].astype(F32)
            o_ref[...] = v.astype(o_ref.dtype)

        if n_steps == 1:
            finish(part)
        else:
            acc_ref = refs[-1]
            r = pl.program_id(3)

            @pl.when(r == 0)
            def _():
                acc_ref[...] = part

            @pl.when(r > 0)
            def _():
                acc_ref[...] += part

            @pl.when(r == n_steps - 1)
            def _():
                finish(acc_ref[...])

    return pl.pallas_call(
        body, name=name,
        grid=(n_out, M // tm, N // tn, n_steps),
        in_specs=in_specs,
        out_specs=pl.BlockSpec((None, tm, tn), o_map),
        out_shape=jax.ShapeDtypeStruct((n_out, M, N), out_dtype),
        scratch_shapes=[pltpu.VMEM((tm, tn), F32)] if n_steps > 1 else [],
        compiler_params=_cparams(),
    )(*operands)


def rowwise(fn, rows, pars, outs, accs=(), *, name, lp):
    tm = _row_tile(lp)
    n_rows, n_pars, n_outs, n_accs = len(rows), len(pars), len(outs), len(accs)
    in_specs = [pl.BlockSpec((tm, w), functools.partial(lambda i, cb: (i, cb), cb=cb)) for _, w, cb in rows]
    in_specs += [pl.BlockSpec(p.shape, functools.partial(lambda i, nd: (0,) * nd, nd=p.ndim)) for p in pars]
    out_specs = [pl.BlockSpec((tm, w), lambda i: (i, 0)) for w, _ in outs]
    out_specs += [pl.BlockSpec(s, functools.partial(lambda i, nd: (0,) * nd, nd=len(s))) for s, _ in accs]
    out_shape = [jax.ShapeDtypeStruct((lp, w), dt) for w, dt in outs]
    out_shape += [jax.ShapeDtypeStruct(s, dt) for s, dt in accs]

    def body(*refs):
        i = pl.program_id(0)
        rv = [r[...] for r in refs[:n_rows]]
        pv = [r[...] for r in refs[n_rows:n_rows + n_pars]]
        o_refs = refs[n_rows + n_pars:n_rows + n_pars + n_outs]
        a_refs = refs[n_rows + n_pars + n_outs:]
        ov, av = fn(i * tm, rv, pv)
        for r, v in zip(o_refs, ov):
            r[...] = v.astype(r.dtype)
        if n_accs:
            @pl.when(i == 0)
            def _():
                for r, v in zip(a_refs, av):
                    r[...] = v.astype(r.dtype)

            @pl.when(i > 0)
            def _():
                for r, v in zip(a_refs, av):
                    r[...] += v.astype(r.dtype)

    res = pl.pallas_call(
        body, name=name, grid=(lp // tm,), in_specs=in_specs, out_specs=out_specs, out_shape=out_shape,
        compiler_params=_cparams(),
    )(*[r[0] for r in rows], *pars)
    return res


def _row_mask(row0, shape):
    return (row0 + lax.broadcasted_iota(jnp.int32, shape, 0)) >= PAD


def ffn_up(hb, wg, wu, lp):
    tm = _row_tile(lp)

    def body(h_ref, wg_ref, wu_ref, ab_ref, hid_ref):
        h = h_ref[...]
        a = _dot(h, wg_ref[...], 1, 0)
        b = _dot(h, wu_ref[...], 1, 0)
        ab_ref[0] = a
        ab_ref[1] = b
        hid_ref[...] = (a * _sigmoid(a) * b).astype(BF16)

    wspec = pl.BlockSpec((None, D_MODEL, FF_SHARD), lambda j, i: (j, 0, 0))
    return pl.pallas_call(
        body, name="ffn_up", grid=(N_SHARD, lp // tm),
        in_specs=[pl.BlockSpec((tm, D_MODEL), lambda j, i: (i, 0)), wspec, wspec],
        out_specs=[pl.BlockSpec((None, 2, tm, FF_SHARD), lambda j, i: (j, 0, i, 0)),
                   pl.BlockSpec((None, tm, FF_SHARD), lambda j, i: (j, i, 0))],
        out_shape=[jax.ShapeDtypeStruct((N_SHARD, 2, lp, FF_SHARD), F32),
                   jax.ShapeDtypeStruct((N_SHARD, lp, FF_SHARD), BF16)],
        compiler_params=_cparams(),
    )(hb, wg, wu)


def _layer_norm(z, g, b):
    mu = jnp.mean(z, axis=-1, keepdims=True)
    zc = z - mu
    var = jnp.mean(zc * zc, axis=-1, keepdims=True)
    return zc * lax.rsqrt(var + LN_EPS) * g + b


def mm_res_ln(a, w, res, g, b, *, scale, name, lp):
    n_red, _, K = a.shape
    tm = _row_tile(lp)

    def body(a_ref, w_ref, res_ref, g_ref, b_ref, z_ref, h_ref, hb_ref, acc_ref):
        r = pl.program_id(1)
        part = _dot(a_ref[...].astype(BF16), w_ref[...], 1, 0)

        @pl.when(r == 0)
        def _():
            acc_ref[...] = part

        @pl.when(r > 0)
        def _():
            acc_ref[...] += part

        @pl.when(r == n_red - 1)
        def _():
            z = ALPHA * res_ref[...] + scale * acc_ref[...]
            z_ref[...] = z
            hn = _layer_norm(z, g_ref[...], b_ref[...])
            h_ref[...] = hn
            hb_ref[...] = hn.astype(BF16)

    row = pl.BlockSpec((tm, D_MODEL), lambda i, r: (i, 0))
    par = pl.BlockSpec((1, D_MODEL), lambda i, r: (0, 0))
    return pl.pallas_call(
        body, name=name, grid=(lp // tm, n_red),
        in_specs=[pl.BlockSpec((None, tm, K), lambda i, r: (r, i, 0)),
                  pl.BlockSpec((None, K, D_MODEL), lambda i, r: (r, 0, 0)), row, par, par],
        out_specs=[row, row, row],
        out_shape=[jax.ShapeDtypeStruct((lp, D_MODEL), F32), jax.ShapeDtypeStruct((lp, D_MODEL), F32),
                   jax.ShapeDtypeStruct((lp, D_MODEL), BF16)],
        scratch_shapes=[pltpu.VMEM((tm, D_MODEL), F32)],
        compiler_params=_cparams(),
    )(a, w, res, g, b)


def ln_bwd(dh, z, g, *, fscale, name, lp):
    def fn(row0, rv, pv):
        dh_, z_ = rv
        g_, = pv
        mu = jnp.mean(z_, axis=-1, keepdims=True)
        zc = z_ - mu
        rstd = lax.rsqrt(jnp.mean(zc * zc, axis=-1, keepdims=True) + LN_EPS)
        xh = zc * rstd
        dxh = dh_ * g_
        m1 = jnp.mean(dxh, axis=-1, keepdims=True)
        m2 = jnp.mean(dxh * xh, axis=-1, keepdims=True)
        dz = rstd * (dxh - m1 - xh * m2)
        return ((dz, fscale * dz),
                (jnp.sum(dh_ * xh, axis=0, keepdims=True), jnp.sum(dh_, axis=0, keepdims=True)))

    return rowwise(fn, [(dh, D_MODEL, 0), (z, D_MODEL, 0)], [g], [(D_MODEL, F32), (D_MODEL, BF16)],
                   [((1, D_MODEL), F32), ((1, D_MODEL), F32)], name=name, lp=lp)


def ffn_down_bwd(dfb, wd, ab, lp):
    tm = _row_tile(lp)

    def body(df_ref, w_ref, ab_ref, da_ref, db_ref):
        dhid = _dot(df_ref[...], w_ref[...], 1, 1)
        a = ab_ref[0]
        b = ab_ref[1]
        sg = _sigmoid(a)
        da_ref[...] = (dhid * b * (sg * (1.0 + a * (1.0 - sg)))).astype(BF16)
        db_ref[...] = (dhid * (a * sg)).astype(BF16)

    ospec = pl.BlockSpec((None, tm, FF_SHARD), lambda j, i: (j, i, 0))
    return pl.pallas_call(
        body, name="ffn_down_bwd", grid=(N_SHARD, lp // tm),
        in_specs=[pl.BlockSpec((tm, D_MODEL), lambda j, i: (i, 0)),
                  pl.BlockSpec((None, FF_SHARD, D_MODEL), lambda j, i: (j, 0, 0)),
                  pl.BlockSpec((None, 2, tm, FF_SHARD), lambda j, i: (j, 0, i, 0))],
        out_specs=[ospec, ospec],
        out_shape=[jax.ShapeDtypeStruct((N_SHARD, lp, FF_SHARD), BF16)] * 2,
        compiler_params=_cparams(),
    )(dfb, wd, ab)


def ffn_dx(da, db, wg, wu, dz, lp):
    tm = _row_tile(lp)

    def body(da_ref, db_ref, wg_ref, wu_ref, dz_ref, o_ref, acc_ref):
        j = pl.program_id(1)
        part = _dot(da_ref[...], wg_ref[...], 1, 1) + _dot(db_ref[...], wu_ref[...], 1, 1)

        @pl.when(j == 0)
        def _():
            acc_ref[...] = part

        @pl.when(j > 0)
        def _():
            acc_ref[...] += part

        @pl.when(j == N_SHARD - 1)
        def _():
            o_ref[...] = acc_ref[...] + ALPHA * dz_ref[...]

    aspec = pl.BlockSpec((None, tm, FF_SHARD), lambda i, j: (j, i, 0))
    wspec = pl.BlockSpec((None, D_MODEL, FF_SHARD), lambda i, j: (j, 0, 0))
    row = pl.BlockSpec((tm, D_MODEL), lambda i, j: (i, 0))
    return pl.pallas_call(
        body, name="ffn_dx", grid=(lp // tm, N_SHARD), in_specs=[aspec, aspec, wspec, wspec, row], out_specs=row,
        out_shape=jax.ShapeDtypeStruct((lp, D_MODEL), F32), scratch_shapes=[pltpu.VMEM((tm, D_MODEL), F32)],
        compiler_params=_cparams(),
    )(da, db, wg, wu, dz)


def ffn_fwd(h, hb, wg, wu, wd, g, b, lp):
    ab, hid = ffn_up(hb, wg, wu, lp)
    z, hn, hnb = mm_res_ln(hid, wd, h, g, b, scale=0.5, name="ffn_down_ln", lp=lp)
    return hn, hnb, dict(hb=hb, ab=ab, hid=hid, z=z)


def ffn_bwd(dh, sv, wg, wu, wd, g, lp):
    dz, dfb, dg, db = ln_bwd(dh, sv['z'], g, fscale=0.5, name="ffn_ln_bwd", lp=lp)
    da, dbb = ffn_down_bwd(dfb, wd, sv['ab'], lp)
    d_wd = matmul(sv['hid'], dfb[None], ta=True, ab='o', name="ffn_dwd")
    d_wg = matmul(sv['hb'][None], da, ta=True, bb='o', name="ffn_dwg")
    d_wu = matmul(sv['hb'][None], dbb, ta=True, bb='o', name="ffn_dwu")
    dh_in = ffn_dx(da, dbb, wg, wu, dz, lp)
    return dh_in, dict(wg=d_wg, wu=d_wu, wd=d_wd, ln_g=dg, ln_b=db)


def _rope_tables(lp):
    pos = np.arange(lp, dtype=np.float32) - PAD
    inv = ROPE_BASE ** (-np.arange(0, D_ROPE, 2, dtype=np.float32) / D_ROPE)
    ang = pos[:, None] * inv[None, :]
    cos = np.concatenate([np.cos(ang), np.cos(ang)], axis=1).astype(np.float32)
    sin = np.concatenate([np.sin(ang), np.sin(ang)], axis=1).astype(np.float32)
    rot = np.zeros((D_ROPE, D_ROPE), np.float32)
    half = D_ROPE // 2
    for j in range(half):
        rot[j + half, j] = -1.0
        rot[j, j + half] = 1.0
    return jnp.asarray(cos), jnp.asarray(sin), jnp.asarray(rot)


def _rot(x, rot):
    return _dot(x, rot, 1, 0, precision=lax.Precision.HIGHEST)


def _rms(x, g):
    r = lax.rsqrt(jnp.mean(x * x, axis=-1, keepdims=True) + RMS_EPS)
    return x * r * g


def mla_prep(proj, cos, sin, rot, qg, kvg, lp):
    def fn(row0, rv, pv):
        cq, krb, ckv, c, s = rv
        qg_, kvg_, rot_ = pv
        kr = krb[:, :D_ROPE]
        return ((_rms(cq, qg_), _rms(ckv, kvg_), kr * c + _rot(kr, rot_) * s), ())

    return rowwise(fn, [(proj, Q_RANK, 0), (proj, 128, 3), (proj, KV_RANK, 2), (cos, D_ROPE, 0), (sin, D_ROPE, 0)],
                   [qg, kvg, rot], [(Q_RANK, BF16), (KV_RANK, BF16), (D_ROPE, BF16)], name="mla_prep", lp=lp)


def mla_heads(cqn, ckvn, cos, sin, rot, wqn, wqr, wkn, wv, lp):
    tm = _row_tile(lp)

    def body(cq_ref, ckv_ref, c_ref, s_ref, rot_ref, wqn_ref, wqr_ref, wkn_ref, wv_ref, qn_ref, qr_ref, kn_ref, v_ref):
        cq = cq_ref[...]
        ckv = ckv_ref[...]
        qn_ref[...] = _dot(cq, wqn_ref[...], 1, 0).astype(BF16)
        qr = _dot(cq, wqr_ref[...], 1, 0)
        qr_ref[...] = (qr * c_ref[...] + _rot(qr, rot_ref[...]) * s_ref[...]).astype(BF16)
        kn_ref[...] = _dot(ckv, wkn_ref[...], 1, 0).astype(BF16)
        v_ref[...] = _dot(ckv, wv_ref[...], 1, 0).astype(BF16)

    def row(w):
        return pl.BlockSpec((tm, w), lambda h, i: (i, 0))

    def wspec(k, n):
        return pl.BlockSpec((None, k, n), lambda h, i: (h, 0, 0))

    def ospec(n):
        return pl.BlockSpec((None, tm, n), lambda h, i: (h, i, 0))

    return pl.pallas_call(
        body, name="mla_heads", grid=(N_HEADS, lp // tm),
        in_specs=[row(Q_RANK), row(KV_RANK), row(D_ROPE), row(D_ROPE),
                  pl.BlockSpec((D_ROPE, D_ROPE), lambda h, i: (0, 0)),
                  wspec(Q_RANK, D_NOPE), wspec(Q_RANK, D_ROPE), wspec(KV_RANK, D_NOPE), wspec(KV_RANK, D_V)],
        out_specs=[ospec(D_NOPE), ospec(D_ROPE), ospec(D_NOPE), ospec(D_V)],
        out_shape=[jax.ShapeDtypeStruct((N_HEADS, lp, D_NOPE), BF16), jax.ShapeDtypeStruct((N_HEADS, lp, D_ROPE), BF16),
                   jax.ShapeDtypeStruct((N_HEADS, lp, D_NOPE), BF16), jax.ShapeDtypeStruct((N_HEADS, lp, D_V), BF16)],
        compiler_params=_cparams(),
    )(cqn, ckvn, cos, sin, rot, wqn, wqr, wkn, wv)


def _att_probs(qn, qr, kn, kr, row0, tq, lp):
    s = (_dot(qn, kn, 1, 1) + _dot(qr, kr, 1, 1)) * ATT_SCALE
    qi = row0 + lax.broadcasted_iota(jnp.int32, (tq, lp), 0)
    ki = lax.broadcasted_iota(jnp.int32, (tq, lp), 1)
    s = jnp.where((ki <= qi) & (ki >= PAD), s, -1e30)
    p = jnp.exp(s - jnp.max(s, axis=-1, keepdims=True))
    return p / jnp.sum(p, axis=-1, keepdims=True)


def _att_specs(tq, lp):
    def qspec(n):
        return pl.BlockSpec((None, tq, n), lambda h, i: (h, i, 0))

    def kspec(n):
        return pl.BlockSpec((None, lp, n), lambda h, i: (h, 0, 0))

    return qspec, kspec, pl.BlockSpec((lp, D_ROPE), lambda h, i: (0, 0))


def attn_fwd(qn, qr, kn, v, kr, lp):
    tq = BLOCK_Q = 128
    qspec, kspec, krspec = _att_specs(tq, lp)

    def body(qn_ref, qr_ref, kn_ref, v_ref, kr_ref, o_ref):
        p = _att_probs(qn_ref[...], qr_ref[...], kn_ref[...], kr_ref[...], pl.program_id(1) * tq, tq, lp)
        o_ref[...] = _dot(p.astype(BF16), v_ref[...], 1, 0).astype(BF16)

    return pl.pallas_call(
        body, name="attn_fwd", grid=(N_HEADS, lp // tq),
        in_specs=[qspec(D_NOPE), qspec(D_ROPE), kspec(D_NOPE), kspec(D_V), krspec],
        out_specs=qspec(D_V), out_shape=jax.ShapeDtypeStruct((N_HEADS, lp, D_V), BF16),
        compiler_params=_cparams(),
    )(qn, qr, kn, v, kr)


def attn_bwd(qn, qr, kn, v, kr, do, lp):
    tq = 128
    qspec, kspec, krspec = _att_specs(tq, lp)

    def body(qn_ref, qr_ref, kn_ref, v_ref, kr_ref, do_ref, dqn_ref, dqr_ref, dkn_ref, dv_ref, dkr_ref):
        h, i = pl.program_id(0), pl.program_id(1)
        qn_, qr_, kn_, v_, kr_, do_ = qn_ref[...], qr_ref[...], kn_ref[...], v_ref[...], kr_ref[...], do_ref[...]
        p = _att_probs(qn_, qr_, kn_, kr_, i * tq, tq, lp)
        dp = _dot(do_, v_, 1, 1)
        delta = jnp.sum(p * dp, axis=-1, keepdims=True)
        ds = (p * (dp - delta) * ATT_SCALE).astype(BF16)
        dqn_ref[...] = _dot(ds, kn_, 1, 0).astype(BF16)
        dqr_ref[...] = _dot(ds, kr_, 1, 0)
        dkn = _dot(ds, qn_, 0, 0)
        dv = _dot(p.astype(BF16), do_, 0, 0)
        dkr = _dot(ds, qr_, 0, 0)

        @pl.when(i == 0)
        def _():
            dkn_ref[...] = dkn
            dv_ref[...] = dv

        @pl.when(i > 0)
        def _():
            dkn_ref[...] += dkn
            dv_ref[...] += dv

        @pl.when((i == 0) & (h == 0))
        def _():
            dkr_ref[...] = dkr

        @pl.when((i > 0) | (h > 0))
        def _():
            dkr_ref[...] += dkr

    return pl.pallas_call(
        body, name="attn_bwd", grid=(N_HEADS, lp // tq),
        in_specs=[qspec(D_NOPE), qspec(D_ROPE), kspec(D_NOPE), kspec(D_V), krspec, qspec(D_V)],
        out_specs=[qspec(D_NOPE), qspec(D_ROPE), kspec(D_NOPE), kspec(D_V), krspec],
        out_shape=[jax.ShapeDtypeStruct((N_HEADS, lp, D_NOPE), BF16), jax.ShapeDtypeStruct((N_HEADS, lp, D_ROPE), F32),
                   jax.ShapeDtypeStruct((N_HEADS, lp, D_NOPE), F32), jax.ShapeDtypeStruct((N_HEADS, lp, D_V), F32),
                   jax.ShapeDtypeStruct((lp, D_ROPE), F32)],
        compiler_params=_cparams(),
    )(qn, qr, kn, v, kr, do)


def mla_heads_bwd(dqn, dqr, dkn, dv, cos, sin, rot, wqn, wqr, wkn, wv, lp):
    tm = _row_tile(lp)

    def body(dqn_ref, dqr_ref, dkn_ref, dv_ref, c_ref, s_ref, rot_ref, wqn_ref, wqr_ref, wkn_ref, wv_ref,
             dcq_ref, dckv_ref, dqrp_ref):
        h = pl.program_id(1)
        dqr_ = dqr_ref[...]
        dqrp = (dqr_ * c_ref[...] - _rot(dqr_ * s_ref[...], rot_ref[...])).astype(BF16)
        dqrp_ref[...] = dqrp
        dcq = _dot(dqn_ref[...], wqn_ref[...], 1, 1) + _dot(dqrp, wqr_ref[...], 1, 1)
        dckv = _dot(dkn_ref[...].astype(BF16), wkn_ref[...], 1, 1) + _dot(dv_ref[...].astype(BF16), wv_ref[...], 1, 1)

        @pl.when(h == 0)
        def _():
            dcq_ref[...] = dcq
            dckv_ref[...] = dckv

        @pl.when(h > 0)
        def _():
            dcq_ref[...] += dcq
            dckv_ref[...] += dckv

    def hspec(n):
        return pl.BlockSpec((None, tm, n), lambda i, h: (h, i, 0))

    def row(w):
        return pl.BlockSpec((tm, w), lambda i, h: (i, 0))

    def wspec(k, n):
        return pl.BlockSpec((None, k, n), lambda i, h: (h, 0, 0))

    return pl.pallas_call(
        body, name="mla_heads_bwd", grid=(lp // tm, N_HEADS),
        in_specs=[hspec(D_NOPE), hspec(D_ROPE), hspec(D_NOPE), hspec(D_V), row(D_ROPE), row(D_ROPE),
                  pl.BlockSpec((D_ROPE, D_ROPE), lambda i, h: (0, 0)),
                  wspec(Q_RANK, D_NOPE), wspec(Q_RANK, D_ROPE), wspec(KV_RANK, D_NOPE), wspec(KV_RANK, D_V)],
        out_specs=[row(Q_RANK), row(KV_RANK), hspec(D_ROPE)],
        out_shape=[jax.ShapeDtypeStruct((lp, Q_RANK), F32), jax.ShapeDtypeStruct((lp, KV_RANK), F32),
                   jax.ShapeDtypeStruct((N_HEADS, lp, D_ROPE), BF16)],
        compiler_params=_cparams(),
    )(dqn, dqr, dkn, dv, cos, sin, rot, wqn, wqr, wkn, wv)


def _rms_bwd(dy, x, g):
    r = lax.rsqrt(jnp.mean(x * x, axis=-1, keepdims=True) + RMS_EPS)
    n = x * r
    dn = dy * g
    dx = r * (dn - n * jnp.mean(dn * n, axis=-1, keepdims=True))
    return dx, jnp.sum(dy * n, axis=0, keepdims=True)


def mla_prep_bwd(dcq, dckv, dkr, proj, cos, sin, rot, qg, kvg, lp):
    def fn(row0, rv, pv):
        dcq_, dckv_, dkr_, cq, ckv, c, s = rv
        qg_, kvg_, rot_ = pv
        dxq, dgq = _rms_bwd(dcq_, cq, qg_)
        dxkv, dgkv = _rms_bwd(dckv_, ckv, kvg_)
        dkr_raw = dkr_ * c - _rot(dkr_ * s, rot_)
        return ((dxq, dxkv, dkr_raw), (dgq, dgkv))

    return rowwise(fn, [(dcq, Q_RANK, 0), (dckv, KV_RANK, 0), (dkr, D_ROPE, 0), (proj, Q_RANK, 0), (proj, KV_RANK, 2),
                        (cos, D_ROPE, 0), (sin, D_ROPE, 0)], [qg, kvg, rot],
                   [(Q_RANK, BF16), (KV_RANK, BF16), (D_ROPE, BF16)], [((1, Q_RANK), F32), ((1, KV_RANK), F32)],
                   name="mla_prep_bwd", lp=lp)


def _shift_down(x, d, rows):
    return jnp.where(rows >= d, pltpu.roll(x, d, 0), 0.0)


def _shift_up(x, d, rows, n):
    return jnp.where(rows < n - d, pltpu.roll(x, n - d, 0), 0.0)


_CONV_W = 128
_XB, _BG, _CG = 1024 // _CONV_W, 1536 // _CONV_W, 2048 // _CONV_W


def _conv_specs(lp):
    def pspec(base):
        return pl.BlockSpec((lp, _CONV_W), functools.partial(lambda c, base: (0, base + c), base=base))

    col = pl.BlockSpec((lp, _CONV_W), lambda c: (0, c))
    wspec = pl.BlockSpec((3, _CONV_W), lambda c: (0, c))
    bspec = pl.BlockSpec((1, _CONV_W), lambda c: (0, c))
    return pspec, col, wspec, bspec


def _conv_core(xbar, cg, w, bias, lp):
    rows = lax.broadcasted_iota(jnp.int32, (lp, _CONV_W), 0)
    u = jnp.where(rows >= PAD, cg * xbar, 0.0)
    u1 = _shift_down(u, 1, rows)
    u2 = _shift_down(u, 2, rows)
    y = bias + w[0:1] * u2 + w[1:2] * u1 + w[2:3] * u
    return rows, u, u1, u2, y


def conv_fwd(proj, w, bias, lp):
    pspec, col, wspec, bspec = _conv_specs(lp)

    def body(x_ref, b_ref, c_ref, w_ref, bias_ref, v_ref):
        _, _, _, _, y = _conv_core(x_ref[...], c_ref[...], w_ref[...], bias_ref[...], lp)
        v_ref[...] = (b_ref[...] * y).astype(BF16)

    return pl.pallas_call(
        body, name="conv_fwd", grid=(MIX // _CONV_W,),
        in_specs=[pspec(_XB), pspec(_BG), pspec(_CG), wspec, bspec], out_specs=col,
        out_shape=jax.ShapeDtypeStruct((lp, MIX), BF16), compiler_params=_cparams(),
    )(proj, proj, proj, w, bias)


def conv_bwd(dv, proj, w, bias, lp):
    pspec, col, wspec, bspec = _conv_specs(lp)

    def body(dv_ref, x_ref, b_ref, c_ref, w_ref, bias_ref, dx_ref, db_ref, dc_ref, dw_ref, dbias_ref):
        xbar, cg, w_ = x_ref[...], c_ref[...], w_ref[...]
        rows, u, u1, u2, y = _conv_core(xbar, cg, w_, bias_ref[...], lp)
        dv_ = dv_ref[...]
        db_ref[...] = (dv_ * y).astype(BF16)
        dy = dv_ * b_ref[...]
        dbias_ref[...] = jnp.sum(dy, axis=0, keepdims=True)
        dw_ref[0:1, :] = jnp.sum(dy * u2, axis=0, keepdims=True)
        dw_ref[1:2, :] = jnp.sum(dy * u1, axis=0, keepdims=True)
        dw_ref[2:3, :] = jnp.sum(dy * u, axis=0, keepdims=True)
        du = w_[2:3] * dy + w_[1:2] * _shift_up(dy, 1, rows, lp) + w_[0:1] * _shift_up(dy, 2, rows, lp)
        du = jnp.where(rows >= PAD, du, 0.0)
        dc_ref[...] = (du * xbar).astype(BF16)
        dx_ref[...] = (du * cg).astype(BF16)

    return pl.pallas_call(
        body, name="conv_bwd", grid=(MIX // _CONV_W,),
        in_specs=[col, pspec(_XB), pspec(_BG), pspec(_CG), wspec, bspec],
        out_specs=[col, col, col, wspec, bspec],
        out_shape=[jax.ShapeDtypeStruct((lp, MIX), BF16)] * 3 + [jax.ShapeDtypeStruct((3, MIX), F32),
                                                                jax.ShapeDtypeStruct((1, MIX), F32)],
        compiler_params=_cparams(),
    )(dv, proj, proj, proj, w, bias)


def _s5_disc(a_re, a_im, log_dt, b_re, b_im):
    dt = jnp.exp(log_dt)
    mag = jnp.exp(dt * a_re)
    ab_re, ab_im = mag * jnp.cos(dt * a_im), mag * jnp.sin(dt * a_im)
    den = a_re * a_re + a_im * a_im
    nr, ni = ab_re - 1.0, ab_im
    coef_re = (nr * a_re + ni * a_im) / den
    coef_im = (ni * a_re - nr * a_im) / den
    return ab_re, ab_im, coef_re * b_re - coef_im * b_im, coef_re * b_im + coef_im * b_re


_S5_ROWS = S5_GROUPS * S5_GROUP


def s5_prep(a_re, a_im, log_dt, b_re, b_im):
    def body(ar, ai, ld, br, bi, o0, o1, o2, o3):
        for o, v in zip((o0, o1, o2, o3), _s5_disc(ar[...], ai[...], ld[...], br[...], bi[...])):
            o[...] = v

    return pl.pallas_call(body, name="s5_prep",
                          out_shape=[jax.ShapeDtypeStruct((_S5_ROWS, S5_STATE), F32)] * 4)(a_re, a_im, log_dt, b_re, b_im)


def s5_prep_bwd(a_re, a_im, log_dt, b_re, b_im, d_ab_re, d_ab_im, d_bb_re, d_bb_im, sel):
    def body(ar, ai, ld, br, bi, g0, g1, g2, g3, sel_ref, da_re, da_im, dld, dbr, dbi):
        _, vjp = jax.vjp(_s5_disc, ar[...], ai[...], ld[...], br[...], bi[...])
        c_ar, c_ai, c_ld, c_br, c_bi = vjp((g0[...], g1[...], g2[...], g3[...]))
        s = sel_ref[...]
        hi = lax.Precision.HIGHEST
        da_re[...] = _dot(s, c_ar, 1, 0, precision=hi)
        da_im[...] = _dot(s, c_ai, 1, 0, precision=hi)
        dld[...] = jnp.sum(_dot(s, c_ld, 1, 0, precision=hi), axis=-1, keepdims=True)
        dbr[...] = c_br
        dbi[...] = c_bi

    g = jax.ShapeDtypeStruct((S5_GROUPS, S5_STATE), F32)
    full = jax.ShapeDtypeStruct((_S5_ROWS, S5_STATE), F32)
    return pl.pallas_call(body, name="s5_prep_bwd",
                          out_shape=[g, g, jax.ShapeDtypeStruct((S5_GROUPS, 1), F32), full, full],
                          )(a_re, a_im, log_dt, b_re, b_im, d_ab_re, d_ab_im, d_bb_re, d_bb_im, sel)


_SCAN_W = 128
_SCAN_STEPS = int(math.log2(SCAN_CHUNK))


def _cmul(ar, ai, br, bi):
    return ar * br - ai * bi, ar * bi + ai * br


def _scan_powers(ar, ai, reverse):
    pw = [(ar, ai)]
    for _ in range(_SCAN_STEPS):
        pw.append(_cmul(*pw[-1], *pw[-1]))
    rows = lax.broadcasted_iota(jnp.int32, (SCAN_CHUNK, ar.shape[-1]), 0)
    tr = jnp.broadcast_to(ar, rows.shape)
    ti = jnp.broadcast_to(ai, rows.shape)
    for k in range(_SCAN_STEPS):
        d = 2 ** k
        if reverse:
            live = rows < SCAN_CHUNK - d
            mr, mi = _cmul(tr, ti, _shift_up(tr, d, rows, SCAN_CHUNK), _shift_up(ti, d, rows, SCAN_CHUNK))
        else:
            live = rows >= d
            mr, mi = _cmul(tr, ti, _shift_down(tr, d, rows), _shift_down(ti, d, rows))
        tr = jnp.where(live, mr, tr)
        ti = jnp.where(live, mi, ti)
    return pw, rows, tr, ti


def s5_scan(bu, ab_re, ab_im, lp):
    n_chunks = lp // SCAN_CHUNK

    def body(bu_ref, ar_ref, ai_ref, s_ref):
        ar, ai = ar_ref[...], ai_ref[...]
        pw, rows, tr, ti = _scan_powers(ar, ai, False)

        def chunk(ci, carry):
            cr, cim = carry
            r0 = pl.multiple_of(ci * SCAN_CHUNK, SCAN_CHUNK)
            xr = bu_ref[0, pl.ds(r0, SCAN_CHUNK), :]
            xi = bu_ref[1, pl.ds(r0, SCAN_CHUNK), :]
            for k in range(_SCAN_STEPS):
                d = 2 ** k
                mr, mi = _cmul(pw[k][0], pw[k][1], _shift_down(xr, d, rows), _shift_down(xi, d, rows))
                xr, xi = xr + mr, xi + mi
            mr, mi = _cmul(tr, ti, cr, cim)
            xr, xi = xr + mr, xi + mi
            s_ref[0, pl.ds(r0, SCAN_CHUNK), :] = xr
            s_ref[1, pl.ds(r0, SCAN_CHUNK), :] = xi
            return xr[SCAN_CHUNK - 1:SCAN_CHUNK, :], xi[SCAN_CHUNK - 1:SCAN_CHUNK, :]

        zero = jnp.zeros((1, _SCAN_W), F32)
        lax.fori_loop(0, n_chunks, chunk, (zero, zero))

    spec = pl.BlockSpec((2, lp, _SCAN_W), lambda c: (0, 0, c))
    aspec = pl.BlockSpec((1, _SCAN_W), lambda c: (0, c))
    return pl.pallas_call(
        body, name="s5_scan", grid=(S5_LANES // _SCAN_W,), in_specs=[spec, aspec, aspec], out_specs=spec,
        out_shape=jax.ShapeDtypeStruct((2, lp, S5_LANES), F32), compiler_params=_cparams(),
    )(bu, ab_re, ab_im)


def s5_scan_bwd(ds, s, ab_re, ab_im, lp):
    n_chunks = lp // SCAN_CHUNK

    def body(ds_ref, s_ref, ar_ref, ai_ref, g_ref, da_ref):
        ar, ai = ar_ref[...], -ai_ref[...]
        pw, rows, tr, ti = _scan_powers(ar, ai, True)

        def chunk(k, carry):
            cr, cim, dar, dai = carry
            ci = n_chunks - 1 - k
            r0 = pl.multiple_of(ci * SCAN_CHUNK, SCAN_CHUNK)
            xr = ds_ref[0, pl.ds(r0, SCAN_CHUNK), :]
            xi = ds_ref[1, pl.ds(r0, SCAN_CHUNK), :]
            for j in range(_SCAN_STEPS):
                d = 2 ** j
                mr, mi = _cmul(pw[j][0], pw[j][1], _shift_up(xr, d, rows, SCAN_CHUNK), _shift_up(xi, d, rows, SCAN_CHUNK))
                xr, xi = xr + mr, xi + mi
            mr, mi = _cmul(tr, ti, cr, cim)
            xr, xi = xr + mr, xi + mi
            g_ref[0, pl.ds(r0, SCAN_CHUNK), :] = xr
            g_ref[1, pl.ds(r0, SCAN_CHUNK), :] = xi
            prev0 = pl.multiple_of(jnp.maximum(r0 - 8, 0), 8)
            live = (ci > 0).astype(F32)
            pr = s_ref[0, pl.ds(prev0, 8), :][7:8, :] * live
            pim = s_ref[1, pl.ds(prev0, 8), :][7:8, :] * live
            sr = s_ref[0, pl.ds(r0, SCAN_CHUNK), :]
            si = s_ref[1, pl.ds(r0, SCAN_CHUNK), :]
            sr = jnp.where(rows >= 1, pltpu.roll(sr, 1, 0), pr)
            si = jnp.where(rows >= 1, pltpu.roll(si, 1, 0), pim)
            dar = dar + jnp.sum(xr * sr + xi * si, axis=0, keepdims=True)
            dai = dai + jnp.sum(xi * sr - xr * si, axis=0, keepdims=True)
            return xr[0:1, :], xi[0:1, :], dar, dai

        zero = jnp.zeros((1, _SCAN_W), F32)
        _, _, dar, dai = lax.fori_loop(0, n_chunks, chunk, (zero, zero, zero, zero))
        da_ref[0] = dar
        da_ref[1] = dai

    spec = pl.BlockSpec((2, lp, _SCAN_W), lambda c: (0, 0, c))
    aspec = pl.BlockSpec((1, _SCAN_W), lambda c: (0, c))
    return pl.pallas_call(
        body, name="s5_scan_bwd", grid=(S5_LANES // _SCAN_W,), in_specs=[spec, spec, aspec, aspec],
        out_specs=[spec, pl.BlockSpec((2, 1, _SCAN_W), lambda c: (0, 0, c))],
        out_shape=[jax.ShapeDtypeStruct((2, lp, S5_LANES), F32), jax.ShapeDtypeStruct((2, 1, S5_LANES), F32)],
        compiler_params=_cparams(),
    )(ds, s, ab_re, ab_im)


def _blockdiag(x):
    g, r, c = x.shape
    eye = jnp.eye(g, dtype=x.dtype)
    return (x[:, :, None, :] * eye[:, None, :, None]).reshape(g * r, g * c)


def _blockdiag_extract(m, g, r, c):
    return jnp.einsum('grgc->grc', m.reshape(g, r, g, c))


def s5_u(proj, lp):
    def fn(row0, rv, pv):
        u, = rv
        return ((jnp.where(_row_mask(row0, u.shape), u, 0.0),), ())

    return rowwise(fn, [(proj, MIX, 5)], [], [(MIX, BF16)], name="s5_u", lp=lp)[0]


def s5_y(ys, proj, d, lp):
    def fn(row0, rv, pv):
        ys_, u = rv
        y = ys_ + pv[0] * u
        return ((y, _gelu(y)), ())

    return rowwise(fn, [(ys, MIX, 0), (proj, MIX, 5)], [d], [(MIX, F32), (MIX, BF16)], name="s5_y", lp=lp)


def s5_glu(z, y, b, lp):
    def fn(row0, rv, pv):
        z_, y_ = rv
        return ((_gelu(y_) * _sigmoid(z_ + pv[0]),), ())

    return rowwise(fn, [(z, MIX, 0), (y, MIX, 0)], [b], [(MIX, BF16)], name="s5_glu", lp=lp)[0]


def s5_glu_bwd(dgl, z, y, b, lp):
    def fn(row0, rv, pv):
        dgl_, z_, y_ = rv
        sg = _sigmoid(z_ + pv[0])
        dz = dgl_ * _gelu(y_) * sg * (1.0 - sg)
        return ((dgl_ * sg, dz), (jnp.sum(dz, axis=0, keepdims=True),))

    return rowwise(fn, [(dgl, MIX, 0), (z, MIX, 0), (y, MIX, 0)], [b], [(MIX, F32), (MIX, BF16)], [((1, MIX), F32)],
                   name="s5_glu_bwd", lp=lp)


def s5_y_bwd(dyg, y, proj, d, lp):
    def fn(row0, rv, pv):
        dyg_, y_, u = rv
        dy = dyg_ * _gelu_grad(y_)
        return ((dy, dy * pv[0]), (jnp.sum(dy * u, axis=0, keepdims=True),))

    return rowwise(fn, [(dyg, MIX, 0), (y, MIX, 0), (proj, MIX, 5)], [d], [(MIX, BF16), (MIX, F32)], [((1, MIX), F32)],
                   name="s5_y_bwd", lp=lp)


def s5_du(du, lp):
    def fn(row0, rv, pv):
        return ((jnp.where(_row_mask(row0, rv[0].shape), rv[0], 0.0),), ())

    return rowwise(fn, [(du, MIX, 0)], [], [(MIX, BF16)], name="s5_du", lp=lp)[0]


def merge_fwd(proj, ya, yb, yc, lp):
    def fn(row0, rv, pv):
        g0, g1, g2, a, b, c = rv
        return ((_sigmoid(g0) * a + _sigmoid(g1) * b + _sigmoid(g2) * c,), ())

    return rowwise(fn, [(proj, D_MODEL, 3), (proj, D_MODEL, 4), (proj, D_MODEL, 5), (ya, D_MODEL, 0), (yb, D_MODEL, 0),
                        (yc, D_MODEL, 0)], [], [(D_MODEL, BF16)], name="merge_fwd", lp=lp)[0]


def merge_bwd(dmix, proj, ya, yb, yc, lp):
    def fn(row0, rv, pv):
        dm, g0, g1, g2, a, b, c = rv
        outs_y, outs_g = [], []
        for g, yv in ((g0, a), (g1, b), (g2, c)):
            sg = _sigmoid(g)
            outs_y.append(dm * sg)
            outs_g.append(dm * yv * sg * (1.0 - sg))
        return (tuple(outs_y) + tuple(outs_g), ())

    return rowwise(fn, [(dmix, D_MODEL, 0), (proj, D_MODEL, 3), (proj, D_MODEL, 4), (proj, D_MODEL, 5),
                        (ya, D_MODEL, 0), (yb, D_MODEL, 0), (yc, D_MODEL, 0)], [], [(D_MODEL, BF16)] * 6,
                   name="merge_bwd", lp=lp)


def loss_head(h, tgt, lp):
    def fn(row0, rv, pv):
        h_, t_ = rv
        live = (row0 + lax.broadcasted_iota(jnp.int32, h_.shape, 0)) >= X0
        diff = jnp.where(live, h_ - t_, 0.0)
        ssq = jnp.sum(jnp.sum(diff * diff, axis=1, keepdims=True), axis=0, keepdims=True)
        return ((diff * (1.0 / D_MODEL),), (ssq * (0.5 / D_MODEL),))

    return rowwise(fn, [(h, D_MODEL, 0), (tgt, D_MODEL, 0)], [], [(D_MODEL, F32)], [((1, 1), F32)], name="loss_head", lp=lp)


def _s5_consts(W):
    ab_re_rep, ab_im_rep, bb_re, bb_im = s5_prep(W['s5_a_re'], W['s5_a_im'], W['s5_log_dt'], W['s5_b_re'], W['s5_b_im'])
    pick = lambda t: t.reshape(S5_GROUPS, S5_GROUP, S5_STATE)[:, 0].reshape(1, S5_LANES)
    bb = jnp.stack([_blockdiag(bb_re.reshape(S5_GROUPS, S5_GROUP, S5_STATE)),
                    _blockdiag(bb_im.reshape(S5_GROUPS, S5_GROUP, S5_STATE))]).astype(BF16)
    return pick(ab_re_rep), pick(ab_im_rep), bb


def layer_fwd(h, hb, W, tabs, lp):
    cos, sin, rot = tabs
    h1, h1b, sv1 = ffn_fwd(h, hb, W['wg1'], W['wu1'], W['wd1'], W['ln1_g'], W['ln1_b'], lp)
    proj = matmul(h1b[None], W['w_in'][None], name="proj")[0]
    cqn, ckvn, kr = mla_prep(proj, cos, sin, rot, W['q_norm_g'], W['kv_norm_g'], lp)
    qn, qr, kn, v = mla_heads(cqn, ckvn, cos, sin, rot, W['wqn'], W['wqr'], W['wkn'], W['wv'], lp)
    o = attn_fwd(qn, qr, kn, v, kr, lp)
    ya = matmul(o, W['mla_wo'], ab='r', bb='r', name="mla_out")[0]
    vconv = conv_fwd(proj, W['conv_w'], W['conv_b'], lp)
    yb = matmul(vconv[None], W['conv_wout'][None], name="conv_out")[0]
    ub = s5_u(proj, lp)
    ab_re, ab_im, bb = _s5_consts(W)
    bu = matmul(ub[None], bb, bb='o', name="s5_bu")
    s = s5_scan(bu, ab_re, ab_im, lp)
    ys = matmul(s, W['s5_ct'], ab='r', bb='r', name="s5_cs")[0]
    y, ygb = s5_y(ys, proj, W['s5_d'], lp)
    zg = matmul(ygb[None], W['s5_wglu'][None], name="s5_glu_mm")[0]
    glb = s5_glu(zg, y, W['s5_b_glu'], lp)
    yc = matmul(glb[None], W['s5_wout'][None], name="s5_out")[0]
    mixed = merge_fwd(proj, ya, yb, yc, lp)
    z2, h2, h2b = mm_res_ln(mixed[None], W['w_o'][None], h1, W['ln2_g'], W['ln2_b'], scale=1.0, name="wo_ln", lp=lp)
    h3, h3b, sv3 = ffn_fwd(h2, h2b, W['wg2'], W['wu2'], W['wd2'], W['ln3_g'], W['ln3_b'], lp)
    sv = dict(sv1=sv1, sv3=sv3, h1b=h1b, proj=proj, cqn=cqn, ckvn=ckvn, kr=kr, qn=qn, qr=qr, kn=kn, v=v, o=o, ya=ya,
              vconv=vconv, yb=yb, ub=ub, ab_re=ab_re, ab_im=ab_im, bb=bb, s=s, y=y, ygb=ygb, zg=zg, glb=glb, yc=yc,
              mixed=mixed, z2=z2)
    return h3, h3b, sv


def layer_bwd(dh3, sv, W, tabs, lp):
    cos, sin, rot = tabs
    proj = sv['proj']
    G = {}
    dh2, g3 = ffn_bwd(dh3, sv['sv3'], W['wg2'], W['wu2'], W['wd2'], W['ln3_g'], lp)
    G.update(wg2=g3['wg'], wu2=g3['wu'], wd2=g3['wd'], ln3_g=g3['ln_g'], ln3_b=g3['ln_b'])
    dz2, dz2b, G['ln2_g'], G['ln2_b'] = ln_bwd(dh2, sv['z2'], W['ln2_g'], fscale=1.0, name="wo_ln_bwd", lp=lp)
    dmix = matmul(dz2b[None], W['w_o'][None], tb=True, name="wo_dx")[0]
    G['w_o'] = matmul(sv['mixed'][None], dz2b[None], ta=True, name="wo_dw")[0]
    dya, dyb, dyc, dg0, dg1, dg2 = merge_bwd(dmix, proj, sv['ya'], sv['yb'], sv['yc'], lp)
    dgl = matmul(dyc[None], W['s5_wout'][None], tb=True, name="s5_out_dx")[0]
    G['s5_wout'] = matmul(sv['glb'][None], dyc[None], ta=True, name="s5_out_dw")[0]
    t1, dzb, G['s5_b_glu'] = s5_glu_bwd(dgl, sv['zg'], sv['y'], W['s5_b_glu'], lp)
    dyg = matmul(dzb[None], W['s5_wglu'][None], tb=True, res=t1[None], name="s5_glu_dx")[0]
    G['s5_wglu'] = matmul(sv['ygb'][None], dzb[None], ta=True, name="s5_glu_dw")[0]
    dyb_, du_d, G['s5_d'] = s5_y_bwd(dyg, sv['y'], proj, W['s5_d'], lp)
    ds = matmul(dyb_[None], W['s5_ct'], tb=True, bb='o', name="s5_cs_dx")
    G['s5_ct'] = matmul(sv['s'], dyb_[None], ta=True, ab='o', name="s5_cs_dw")
    g_adj, d_ab = s5_scan_bwd(ds, sv['s'], sv['ab_re'], sv['ab_im'], lp)
    du = matmul(g_adj, sv['bb'], tb=True, ab='r', bb='r', res=du_d[None], name="s5_bu_dx")[0]
    d_bb = matmul(sv['ub'][None], g_adj, ta=True, bb='o', name="s5_bu_dw")
    du_b = s5_du(du, lp)
    onehot = (jnp.arange(S5_GROUP) == 0).astype(F32)
    spread = lambda t: (t.reshape(S5_GROUPS, 1, S5_STATE) * onehot[None, :, None]).reshape(_S5_ROWS, S5_STATE)
    take = lambda t: _blockdiag_extract(t, S5_GROUPS, S5_GROUP, S5_STATE).reshape(_S5_ROWS, S5_STATE)
    sel = jnp.kron(jnp.eye(S5_GROUPS, dtype=F32), jnp.ones((1, S5_GROUP), F32))
    (G['s5_a_re'], G['s5_a_im'], G['s5_log_dt'], G['s5_b_re'], G['s5_b_im']) = s5_prep_bwd(
        W['s5_a_re'], W['s5_a_im'], W['s5_log_dt'], W['s5_b_re'], W['s5_b_im'],
        spread(d_ab[0]), spread(d_ab[1]), take(d_bb[0]), take(d_bb[1]), sel)
    dv = matmul(dyb[None], W['conv_wout'][None], tb=True, name="conv_out_dx")[0]
    G['conv_wout'] = matmul(sv['vconv'][None], dyb[None], ta=True, name="conv_out_dw")[0]
    dxbar, dbg, dcg, G['conv_w'], G['conv_b'] = conv_bwd(dv, proj, W['conv_w'], W['conv_b'], lp)
    do = matmul(dya[None], W['mla_wo'], tb=True, bb='o', out_dtype=BF16, name="mla_out_dx")
    G['mla_wo'] = matmul(sv['o'], dya[None], ta=True, ab='o', name="mla_out_dw")
    dqn, dqr, dkn, dvv, dkr = attn_bwd(sv['qn'], sv['qr'], sv['kn'], sv['v'], sv['kr'], do, lp)
    dcq, dckv, dqrp = mla_heads_bwd(dqn, dqr, dkn, dvv, cos, sin, rot, W['wqn'], W['wqr'], W['wkn'], W['wv'], lp)
    G['wqn'] = matmul(sv['cqn'][None], dqn, ta=True, bb='o', name="mla_dwqn")
    G['wqr'] = matmul(sv['cqn'][None], dqrp, ta=True, bb='o', name="mla_dwqr")
    G['wkn'] = matmul(sv['ckvn'][None], dkn, ta=True, bb='o', name="mla_dwkn")
    G['wv'] = matmul(sv['ckvn'][None], dvv, ta=True, bb='o', name="mla_dwv")
    dcq_raw, dckv_raw, dkr_raw, G['q_norm_g'], G['kv_norm_g'] = mla_prep_bwd(
        dcq, dckv, dkr, proj, cos, sin, rot, W['q_norm_g'], W['kv_norm_g'], lp)
    zeros = lambda n: jnp.zeros((lp, n), BF16)
    dproj = jnp.concatenate([dcq_raw, dkr_raw, zeros(96), dckv_raw, zeros(256), dxbar, dbg, dcg, du_b, dg0, dg1, dg2], axis=1)
    dh1 = matmul(dproj[None], W['w_in'][None], tb=True, res=dz2[None], res_scale=ALPHA, name="proj_dx")[0]
    G['w_in'] = matmul(sv['h1b'][None], dproj[None], ta=True, name="proj_dw")[0]
    dh0, g1 = ffn_bwd(dh1, sv['sv1'], W['wg1'], W['wu1'], W['wd1'], W['ln1_g'], lp)
    G.update(wg1=g1['wg'], wu1=g1['wu'], wd1=g1['wd'], ln1_g=g1['ln_g'], ln1_b=g1['ln_b'])
    return dh0, G


def _nat_cols(st):
    return jnp.transpose(st, (1, 0, 2)).reshape(st.shape[1], -1)


def _shard_cols(nat):
    k, n = nat.shape
    return jnp.transpose(nat.reshape(k, N_SHARD, n // N_SHARD), (1, 0, 2))


def _win_pad(w):
    z = lambda n: jnp.zeros((w.shape[0], n), w.dtype)
    return jnp.concatenate([w[:, 0:384], w[:, 640:672], z(96), w[:, 384:640], z(256), w[:, 672:]], axis=1)


def _win_unpad(wp):
    return jnp.concatenate([wp[:, 0:384], wp[:, 512:768], wp[:, 384:416], wp[:, 1024:]], axis=1)


_BIG = [('ffn1_w_gate', 1), ('ffn1_w_up', 1), ('ffn1_w_down', 0), ('w_in', 1), ('mla_w_uq', 1), ('mla_w_ukv', 1),
        ('mla_w_o', 1), ('conv_w_out', 1), ('s5_w_glu', 0), ('s5_w_out', 1), ('w_o', 0),
        ('ffn2_w_gate', 1), ('ffn2_w_up', 1), ('ffn2_w_down', 0)]
_REPL = ['ln1_g', 'ln1_b', 'mla_q_norm_g', 'mla_kv_norm_g', 'conv_b', 's5_a_re', 's5_a_im', 's5_log_dt', 's5_b_re',
         's5_b_im', 's5_c_re', 's5_c_im', 's5_d', 's5_b_glu', 'ln2_g', 'ln2_b', 'ln3_g', 'ln3_b']


def compute_weights(st, small):
    W = {}
    for t in ('1', '2'):
        W['wg' + t], W['wu' + t] = st['ffn%s_w_gate' % t], st['ffn%s_w_up' % t]
        W['wd' + t] = st['ffn%s_w_down' % t]
    W['w_in'] = _win_pad(_nat_cols(st['w_in']))
    uq = jnp.transpose(_nat_cols(st['mla_w_uq']).reshape(Q_RANK, N_HEADS, D_NOPE + D_ROPE), (1, 0, 2))
    W['wqn'], W['wqr'] = uq[:, :, :D_NOPE], uq[:, :, D_NOPE:]
    ukv = jnp.transpose(_nat_cols(st['mla_w_ukv']).reshape(KV_RANK, N_HEADS, D_NOPE + D_V), (1, 0, 2))
    W['wkn'], W['wv'] = ukv[:, :, :D_NOPE], ukv[:, :, D_NOPE:]
    W['mla_wo'] = _nat_cols(st['mla_w_o']).reshape(N_HEADS, D_V, D_MODEL)
    W['conv_wout'] = _nat_cols(st['conv_w_out'])
    W['s5_wglu'] = st['s5_w_glu'].reshape(MIX, MIX)
    W['s5_wout'] = _nat_cols(st['s5_w_out'])
    W['w_o'] = st['w_o'].reshape(D_MODEL, D_MODEL)
    W['conv_w'] = small['conv_w']
    for n in ('ln1_g', 'ln1_b', 'ln2_g', 'ln2_b', 'ln3_g', 'ln3_b', 'conv_b', 's5_b_glu'):
        W[n] = small[n].reshape(1, -1)
    W['q_norm_g'] = small['mla_q_norm_g'].reshape(1, -1)
    W['kv_norm_g'] = small['mla_kv_norm_g'].reshape(1, -1)
    W['s5_d'] = small['s5_d'].reshape(1, MIX)
    rep = lambda t: jnp.repeat(t, S5_GROUP, axis=0)
    W['s5_a_re'], W['s5_a_im'] = rep(small['s5_a_re']), rep(small['s5_a_im'])
    W['s5_log_dt'] = jnp.broadcast_to(rep(small['s5_log_dt'].reshape(S5_GROUPS, 1)), (_S5_ROWS, S5_STATE))
    tr = lambda t: jnp.transpose(t, (0, 2, 1)).reshape(_S5_ROWS, S5_STATE)
    W['s5_b_re'], W['s5_b_im'] = tr(small['s5_b_re']), tr(small['s5_b_im'])
    ct = lambda t: _blockdiag(jnp.transpose(t, (0, 2, 1)))
    W['s5_ct'] = jnp.stack([ct(small['s5_c_re']), -ct(small['s5_c_im'])]).astype(BF16)
    return W


def shard_grads(G):
    R = reference_grads(G, ffn=False)
    S = {n: (_shard_of(R[n], a) if n in R else None) for n, a in _BIG}
    for t in ('1', '2'):
        S['ffn%s_w_gate' % t] = G['wg' + t]
        S['ffn%s_w_up' % t] = G['wu' + t]
        S['ffn%s_w_down' % t] = G['wd' + t]
    S['conv_w'] = _shard_cols(R['conv_w'])
    for n in _REPL:
        S[n] = R[n]
    return S


def reference_grads(G, ffn=True):
    R = {}
    for t in ('1', '2') if ffn else ():
        R['ffn%s_w_gate' % t] = _nat_cols(G['wg' + t])
        R['ffn%s_w_up' % t] = _nat_cols(G['wu' + t])
        R['ffn%s_w_down' % t] = G['wd' + t].reshape(D_FF, D_MODEL)
    R['w_in'] = _win_unpad(G['w_in'])
    R['mla_w_uq'] = jnp.transpose(jnp.concatenate([G['wqn'], G['wqr']], axis=2), (1, 0, 2)).reshape(Q_RANK, -1)
    R['mla_w_ukv'] = jnp.transpose(jnp.concatenate([G['wkn'], G['wv']], axis=2), (1, 0, 2)).reshape(KV_RANK, -1)
    R['mla_w_o'] = G['mla_wo'].reshape(N_HEADS * D_V, D_MODEL)
    R['conv_w'], R['conv_w_out'] = G['conv_w'], G['conv_wout']
    R['s5_w_glu'], R['s5_w_out'], R['w_o'] = G['s5_wglu'], G['s5_wout'], G['w_o']
    for n in ('ln1_g', 'ln1_b', 'ln2_g', 'ln2_b', 'ln3_g', 'ln3_b', 'conv_b', 's5_b_glu'):
        R[n] = G[n].reshape(-1)
    R['mla_q_norm_g'], R['mla_kv_norm_g'] = G['q_norm_g'].reshape(-1), G['kv_norm_g'].reshape(-1)
    R['s5_d'] = G['s5_d'].reshape(S5_GROUPS, S5_GROUP)
    R['s5_a_re'], R['s5_a_im'], R['s5_log_dt'] = G['s5_a_re'], G['s5_a_im'], G['s5_log_dt'].reshape(-1)
    untr = lambda t: jnp.transpose(t.reshape(S5_GROUPS, S5_GROUP, S5_STATE), (0, 2, 1))
    R['s5_b_re'], R['s5_b_im'] = untr(G['s5_b_re']), untr(G['s5_b_im'])
    unct = lambda t: jnp.transpose(_blockdiag_extract(t, S5_GROUPS, S5_STATE, S5_GROUP), (0, 2, 1))
    R['s5_c_re'], R['s5_c_im'] = unct(G['s5_ct'][0]), -unct(G['s5_ct'][1])
    return R


def local_step(x2d, tgt2d, meta, Ws):
    lp = x2d.shape[0] + X0
    tabs = _rope_tables(lp)
    h = jnp.concatenate([jnp.zeros((PAD, D_MODEL), F32), meta, x2d], axis=0)
    hb = h.astype(BF16)
    saved = []
    for W in Ws:
        h, hb, sv = layer_fwd(h, hb, W, tabs, lp)
        saved.append(sv)
    tgt = jnp.concatenate([jnp.zeros((X0, D_MODEL), F32), tgt2d], axis=0)
    dh, loss = loss_head(h, tgt, lp)
    grads = [None] * len(Ws)
    for li in reversed(range(len(Ws))):
        dh, grads[li] = layer_bwd(dh, saved[li], Ws[li], tabs, lp)
    return loss, dh[X0:], dh[PAD:X0], grads


_ANY = pl.BlockSpec(memory_space=pl.ANY)
LANES = 1024


def _half_rows(n):
    rows = -(-n // (2 * LANES))
    q = 512 if rows > 512 else 16
    return -(-rows // q) * q


def _place():
    x, y, c = lax.axis_index("x"), lax.axis_index("y"), lax.axis_index("c")
    chips = [(1 - x, y), (x, 1 - y), (1 - x, 1 - y)]
    return x, y, c, chips


def all_gather_halves(src):
    def body(src_ref, out_ref, send_sems, recv_sems, local_sem):
        x, y, c, chips = _place()
        me = 2 * x + y
        sibling = (x, y, 1 - c)

        def copy(k, src, chip_idx, half, to):
            return pltpu.make_async_remote_copy(src_ref=src, dst_ref=out_ref.at[chip_idx, half], send_sem=send_sems.at[k],
                                                recv_sem=recv_sems.at[k], device_id=to, device_id_type=MESH)

        mine = pltpu.make_async_copy(src_ref, out_ref.at[me], local_sem)
        mine.start()
        first = [copy(j, src_ref.at[c], me, c, (*chip, c)) for j, chip in enumerate(chips)]
        for cp in first:
            cp.start()
        passed = []
        for j, chip in enumerate(chips):
            idx = 2 * chip[0] + chip[1]
            copy(j, src_ref.at[c], idx, c, sibling).wait_recv()
            cp = copy(3 + j, out_ref.at[idx, c], idx, c, sibling)
            cp.start()
            passed.append(cp)
        for j, chip in enumerate(chips):
            copy(3 + j, src_ref.at[c], 2 * chip[0] + chip[1], 1 - c, sibling).wait_recv()
        for cp in first + passed:
            cp.wait_send()
        mine.wait()

    return pl.pallas_call(
        body, name="all_gather_weights", in_specs=[_ANY], out_specs=_ANY,
        out_shape=jax.ShapeDtypeStruct((N_SHARD,) + src.shape, src.dtype),
        scratch_shapes=[pltpu.SemaphoreType.DMA((6,)), pltpu.SemaphoreType.DMA((6,)), pltpu.SemaphoreType.DMA],
    )(src)


def pair_swap(gs):
    n = len(gs)

    def body(*refs):
        g_refs, r_refs, send_sems, recv_sems = refs[:n], refs[n:2 * n], refs[2 * n], refs[2 * n + 1]
        x, y, c, _ = _place()
        copies = []
        for k in range(n):
            for j in range(N_SHARD):
                copies.append(pltpu.make_async_remote_copy(
                    src_ref=g_refs[k].at[j, 1 - c], dst_ref=r_refs[k].at[j], send_sem=send_sems.at[k * N_SHARD + j],
                    recv_sem=recv_sems.at[k * N_SHARD + j], device_id=(x, y, 1 - c), device_id_type=MESH))
        for cp in copies:
            cp.start()
        for cp in copies:
            cp.wait()

    return pl.pallas_call(
        body, name="grad_pair_swap", in_specs=[_ANY] * n, out_specs=[_ANY] * n,
        out_shape=[jax.ShapeDtypeStruct((N_SHARD,) + g.shape[2:], g.dtype) for g in gs],
        scratch_shapes=[pltpu.SemaphoreType.DMA((n * N_SHARD,)), pltpu.SemaphoreType.DMA((n * N_SHARD,))],
    )(*gs)


def _flat_tile(rows):
    return 512 if rows % 512 == 0 else rows


def pair_add(g, r, cidx, out_dtype, name):
    rows = g.shape[2]
    tr = _flat_tile(rows)

    def body(c_ref, g_ref, r_ref, o_ref):
        o_ref[...] = (g_ref[...] + r_ref[...]).astype(out_dtype)

    return pl.pallas_call(
        body, name=name,
        grid_spec=pltpu.PrefetchScalarGridSpec(
            num_scalar_prefetch=1, grid=(N_SHARD, rows // tr),
            in_specs=[pl.BlockSpec((None, None, tr, LANES), lambda j, i, c: (j, c[0], i, 0)),
                      pl.BlockSpec((None, tr, LANES), lambda j, i, c: (j, i, 0))],
            out_specs=pl.BlockSpec((None, tr, LANES), lambda j, i, c: (j, i, 0))),
        out_shape=jax.ShapeDtypeStruct((N_SHARD, rows, LANES), out_dtype),
        compiler_params=_cparams(),
    )(cidx, g, r)


def chip_scatter(ps):
    n = len(ps)

    def body(*refs):
        p_refs, r_refs, send_sems, recv_sems, local_sems = refs[:n], refs[n:2 * n], refs[2 * n], refs[2 * n + 1], refs[2 * n + 2]
        x, y, c, chips = _place()
        me = 2 * x + y
        local, copies = [], []
        for k in range(n):
            cp = pltpu.make_async_copy(p_refs[k].at[me], r_refs[k].at[me], local_sems.at[k])
            cp.start()
            local.append(cp)
            for j, chip in enumerate(chips):
                copies.append(pltpu.make_async_remote_copy(
                    src_ref=p_refs[k].at[2 * chip[0] + chip[1]], dst_ref=r_refs[k].at[me], send_sem=send_sems.at[k * 3 + j],
                    recv_sem=recv_sems.at[k * 3 + j], device_id=(*chip, c), device_id_type=MESH))
        for cp in copies:
            cp.start()
        for cp in copies:
            cp.wait()
        for cp in local:
            cp.wait()

    return pl.pallas_call(
        body, name="grad_chip_scatter", in_specs=[_ANY] * n, out_specs=[_ANY] * n,
        out_shape=[jax.ShapeDtypeStruct(p.shape, p.dtype) for p in ps],
        scratch_shapes=[pltpu.SemaphoreType.DMA((n * 3,)), pltpu.SemaphoreType.DMA((n * 3,)), pltpu.SemaphoreType.DMA((n,))],
    )(*ps)


def sum_slots(r, name):
    rows = r.shape[1]
    tr = _flat_tile(rows)

    def body(r_ref, o_ref):
        o_ref[...] = ((r_ref[0].astype(F32) + r_ref[1].astype(F32)) + r_ref[2].astype(F32)) + r_ref[3].astype(F32)

    return pl.pallas_call(
        body, name=name, grid=(rows // tr,),
        in_specs=[pl.BlockSpec((N_SHARD, tr, LANES), lambda i: (0, i, 0))],
        out_specs=pl.BlockSpec((tr, LANES), lambda i: (i, 0)),
        out_shape=jax.ShapeDtypeStruct((rows, LANES), F32), compiler_params=_cparams(),
    )(r)


def pair_gather(hs):
    n = len(hs)

    def body(*refs):
        h_refs, f_refs, send_sems, recv_sems, local_sems = refs[:n], refs[n:2 * n], refs[2 * n], refs[2 * n + 1], refs[2 * n + 2]
        x, y, c, _ = _place()
        local, copies = [], []
        for k in range(n):
            cp = pltpu.make_async_copy(h_refs[k], f_refs[k].at[c], local_sems.at[k])
            cp.start()
            local.append(cp)
            copies.append(pltpu.make_async_remote_copy(
                src_ref=h_refs[k], dst_ref=f_refs[k].at[c], send_sem=send_sems.at[k], recv_sem=recv_sems.at[k],
                device_id=(x, y, 1 - c), device_id_type=MESH))
        for cp in copies:
            cp.start()
        for cp in copies:
            cp.wait()
        for cp in local:
            cp.wait()

    return pl.pallas_call(
        body, name="grad_pair_gather", in_specs=[_ANY] * n, out_specs=[_ANY] * n,
        out_shape=[jax.ShapeDtypeStruct((2,) + h.shape, h.dtype) for h in hs],
        scratch_shapes=[pltpu.SemaphoreType.DMA((n,)), pltpu.SemaphoreType.DMA((n,)), pltpu.SemaphoreType.DMA((n,))],
    )(*hs)


def reduce_scatter(g_big, g_small, cidx):
    r_big, r_small = pair_swap([g_big, g_small])
    p_big = pair_add(g_big, r_big, cidx, BF16, "grad_pair_add")
    p_small = pair_add(g_small, r_small, cidx, F32, "grad_pair_add_small")
    q_big, q_small = chip_scatter([p_big, p_small])
    h_big = sum_slots(q_big, "grad_chip_sum")
    h_small = sum_slots(q_small, "grad_chip_sum_small")
    f_big, f_small = pair_gather([h_big, h_small])
    return f_big.reshape(-1, LANES), f_small.reshape(-1, LANES)


def _rows_of(c, half):
    return pl.ds(pl.multiple_of(c * half, 8), half)


def all_gather_shards(srcs, exact):
    n, m = len(srcs), len(exact)
    halves = [s.shape[0] // 2 for s in srcs]

    def body(*refs):
        s_refs, e_refs = refs[:n], refs[n:n + m]
        o_refs, eo_refs = refs[n + m:2 * n + m], refs[2 * n + m:2 * n + 2 * m]
        send, recv, esend, erecv, lsem = refs[2 * n + 2 * m:]
        x, y, c, chips = _place()
        me = 2 * x + y
        sibling = (x, y, 1 - c)
        local = [pltpu.make_async_copy(s_refs[k], o_refs[k].at[me], lsem.at[k]) for k in range(n)]
        local += [pltpu.make_async_copy(e_refs[k], eo_refs[k].at[me], lsem.at[n + k]) for k in range(m)]
        for cp in local:
            cp.start()

        def copy(k, s, src, idx, half_c, to):
            return pltpu.make_async_remote_copy(
                src_ref=src, dst_ref=o_refs[k].at[idx, _rows_of(half_c, halves[k])], send_sem=send.at[6 * k + s],
                recv_sem=recv.at[6 * k + s], device_id=to, device_id_type=MESH)

        def ecopy(k, j, idx, to):
            return pltpu.make_async_remote_copy(src_ref=e_refs[k], dst_ref=eo_refs[k].at[idx], send_sem=esend.at[3 * k + j],
                                                recv_sem=erecv.at[3 * k + j], device_id=to, device_id_type=MESH)

        sends = []
        for k in range(n):
            mine = s_refs[k].at[_rows_of(c, halves[k])]
            sends += [copy(k, j, mine, me, c, (*chip, c)) for j, chip in enumerate(chips)]
        for k in range(m):
            sends += [ecopy(k, j, me, (*chip, c)) for j, chip in enumerate(chips)]
        for cp in sends:
            cp.start()
        for j, chip in enumerate(chips):
            idx = 2 * chip[0] + chip[1]
            for k in range(n):
                landed = o_refs[k].at[idx, _rows_of(c, halves[k])]
                copy(k, j, landed, idx, c, sibling).wait_recv()
                fwd = copy(k, 3 + j, landed, idx, c, sibling)
                fwd.start()
                sends.append(fwd)
        for j, chip in enumerate(chips):
            idx = 2 * chip[0] + chip[1]
            for k in range(n):
                copy(k, 3 + j, s_refs[k].at[_rows_of(c, halves[k])], idx, 1 - c, sibling).wait_recv()
            for k in range(m):
                ecopy(k, j, idx, sibling).wait_recv()
        for cp in sends:
            cp.wait_send()
        for cp in local:
            cp.wait()

    outs = pl.pallas_call(
        body, name="all_gather_weights", in_specs=[_ANY] * (n + m), out_specs=[_ANY] * (n + m),
        out_shape=[jax.ShapeDtypeStruct((N_SHARD,) + a.shape, a.dtype) for a in list(srcs) + list(exact)],
        scratch_shapes=[pltpu.SemaphoreType.DMA((6 * n,)), pltpu.SemaphoreType.DMA((6 * n,)),
                        pltpu.SemaphoreType.DMA((3 * m,)), pltpu.SemaphoreType.DMA((3 * m,)),
                        pltpu.SemaphoreType.DMA((n + m,))],
    )(*srcs, *exact)
    return outs[:n], outs[n:]


def rs_pair_swap(gs):
    n = len(gs)

    def body(*refs):
        g_refs, r_refs, send, recv = refs[:n], refs[n:2 * n], refs[2 * n], refs[2 * n + 1]
        x, y, c, _ = _place()
        copies = [pltpu.make_async_remote_copy(
            src_ref=g_refs[k].at[pl.ds(0, N_SHARD), _rows_of(1 - c, gs[k].shape[1] // 2)], dst_ref=r_refs[k],
            send_sem=send.at[k], recv_sem=recv.at[k], device_id=(x, y, 1 - c), device_id_type=MESH) for k in range(n)]
        for cp in copies:
            cp.start()
        for cp in copies:
            cp.wait()

    return pl.pallas_call(
        body, name="grad_pair_swap", in_specs=[_ANY] * n, out_specs=[_ANY] * n,
        out_shape=[jax.ShapeDtypeStruct((N_SHARD, g.shape[1] // 2, g.shape[2]), g.dtype) for g in gs],
        scratch_shapes=[pltpu.SemaphoreType.DMA((n,)), pltpu.SemaphoreType.DMA((n,))],
    )(*gs)


def _group_tile(half, n_cols, n_arrays):
    budget = (20 * 2 ** 20) // (6 * n_arrays)
    fits = [t for t in range(8, half + 1, 8) if half % t == 0 and t * n_cols * 4 <= budget]
    return max(fits) if fits else 8


def rs_pair_add(gs, rs, cidx, out_dtype, name):
    n = len(gs)
    _, K, cols = gs[0].shape
    half = K // 2
    tr = _group_tile(half, cols, n)
    nb = half // tr

    def body(c_ref, *refs):
        for g_ref, r_ref, o_ref in zip(refs[:n], refs[n:2 * n], refs[2 * n:]):
            o_ref[...] = (g_ref[...] + r_ref[...]).astype(out_dtype)

    gspec = pl.BlockSpec((None, tr, cols), lambda j, i, c: (j, c[0] * nb + i, 0))
    rspec = pl.BlockSpec((None, tr, cols), lambda j, i, c: (j, i, 0))
    return pl.pallas_call(
        body, name=name,
        grid_spec=pltpu.PrefetchScalarGridSpec(num_scalar_prefetch=1, grid=(N_SHARD, nb), in_specs=[gspec] * n + [rspec] * n,
                                               out_specs=[rspec] * n),
        out_shape=[jax.ShapeDtypeStruct((N_SHARD, half, cols), out_dtype)] * n,
        compiler_params=_cparams(),
    )(cidx, *gs, *rs)


def rs_chip_scatter(ps):
    n = len(ps)

    def body(*refs):
        p_refs, q_refs, send, recv, lsem = refs[:n], refs[n:2 * n], refs[2 * n], refs[2 * n + 1], refs[2 * n + 2]
        x, y, c, chips = _place()
        me = 2 * x + y
        local = [pltpu.make_async_copy(p_refs[k].at[me], q_refs[k].at[me], lsem.at[k]) for k in range(n)]
        copies = [pltpu.make_async_remote_copy(
            src_ref=p_refs[k].at[2 * chip[0] + chip[1]], dst_ref=q_refs[k].at[me], send_sem=send.at[3 * k + j],
            recv_sem=recv.at[3 * k + j], device_id=(*chip, c), device_id_type=MESH)
            for k in range(n) for j, chip in enumerate(chips)]
        for cp in local + copies:
            cp.start()
        for cp in copies:
            cp.wait()
        for cp in local:
            cp.wait()

    return pl.pallas_call(
        body, name="grad_chip_scatter", in_specs=[_ANY] * n, out_specs=[_ANY] * n,
        out_shape=[jax.ShapeDtypeStruct(p.shape, p.dtype) for p in ps],
        scratch_shapes=[pltpu.SemaphoreType.DMA((3 * n,)), pltpu.SemaphoreType.DMA((3 * n,)), pltpu.SemaphoreType.DMA((n,))],
    )(*ps)


def rs_chip_sum(qs, cidx, name):
    n = len(qs)
    _, half, cols = qs[0].shape
    tr = _group_tile(half, cols, n)
    nb = half // tr

    def body(c_ref, *refs):
        for q_ref, o_ref in zip(refs[:n], refs[n:]):
            o_ref[...] = ((q_ref[0].astype(F32) + q_ref[1].astype(F32)) + q_ref[2].astype(F32)) + q_ref[3].astype(F32)

    return pl.pallas_call(
        body, name=name,
        grid_spec=pltpu.PrefetchScalarGridSpec(
            num_scalar_prefetch=1, grid=(nb,),
            in_specs=[pl.BlockSpec((N_SHARD, tr, cols), lambda i, c: (0, i, 0))] * n,
            out_specs=[pl.BlockSpec((tr, cols), lambda i, c: (c[0] * nb + i, 0))] * n),
        out_shape=[jax.ShapeDtypeStruct((2 * half, cols), F32)] * n,
        compiler_params=_cparams(),
    )(cidx, *qs)


def rs_pair_gather(fs):
    n = len(fs)

    def body(*refs):
        f_refs, send, recv = refs[n:2 * n], refs[2 * n], refs[2 * n + 1]
        x, y, c, _ = _place()
        copies = []
        for k in range(n):
            rows = f_refs[k].at[_rows_of(c, fs[k].shape[0] // 2)]
            copies.append(pltpu.make_async_remote_copy(src_ref=rows, dst_ref=rows, send_sem=send.at[k], recv_sem=recv.at[k],
                                                       device_id=(x, y, 1 - c), device_id_type=MESH))
        for cp in copies:
            cp.start()
        for cp in copies:
            cp.wait()

    return pl.pallas_call(
        body, name="grad_pair_gather", in_specs=[_ANY] * n, out_specs=[_ANY] * n,
        out_shape=[jax.ShapeDtypeStruct(f.shape, f.dtype) for f in fs],
        input_output_aliases={k: k for k in range(n)},
        scratch_shapes=[pltpu.SemaphoreType.DMA((n,)), pltpu.SemaphoreType.DMA((n,))],
    )(*fs)


def reduce_scatter_shards(gs, wire, cidx):
    n = len(gs)
    rs = rs_pair_swap(gs)
    groups = {}
    for k in range(n):
        groups.setdefault((gs[k].shape, jnp.dtype(wire[k]).name), []).append(k)
    ps = [None] * n
    for gi, ((_, _), ks) in enumerate(groups.items()):
        outs = rs_pair_add([gs[k] for k in ks], [rs[k] for k in ks], cidx, wire[ks[0]], "grad_pair_add_%d" % gi)
        for k, o in zip(ks, outs):
            ps[k] = o
    qs = rs_chip_scatter(ps)
    fs = [None] * n
    for gi, ((_, _), ks) in enumerate(groups.items()):
        outs = rs_chip_sum([qs[k] for k in ks], cidx, "grad_chip_sum_%d" % gi)
        for k, o in zip(ks, outs):
            fs[k] = o
    return rs_pair_gather(fs)


def _flat_view(shape):
    n = int(np.prod(shape))
    if n <= 2 ** 20 and n % LANES == 0:
        return (n // LANES, LANES)
    if n <= 2 ** 20 and n % 128 == 0:
        return (n // 128, 128)
    return (n // shape[-1], shape[-1])


def adamw(w, g, m, v, name):
    shape = w.shape
    rows, cols = _flat_view(shape)
    tr = rows
    if rows * cols * 4 > 2 ** 21:
        tr = max(t for t in range(8, rows, 8) if rows % t == 0 and t * cols * 4 <= 2 ** 21)

    def body(w_ref, g_ref, m_ref, v_ref, d_ref, nm_ref, nv_ref):
        g_ = g_ref[...]
        m_new = ADAM_B1 * m_ref[...] + (1.0 - ADAM_B1) * g_
        v_new = ADAM_B2 * v_ref[...] + (1.0 - ADAM_B2) * (g_ * g_)
        m_hat = m_new / (1.0 - ADAM_B1 ** ADAM_STEP)
        v_hat = v_new / (1.0 - ADAM_B2 ** ADAM_STEP)
        d_ref[...] = -ADAM_LR * (m_hat / (jnp.sqrt(v_hat) + ADAM_EPS) + ADAM_WD * w_ref[...])
        nm_ref[...] = m_new
        nv_ref[...] = v_new

    spec = pl.BlockSpec((tr, cols), lambda i: (i, 0))
    outs = pl.pallas_call(
        body, name=name, grid=(rows // tr,), in_specs=[spec] * 4, out_specs=[spec] * 3,
        out_shape=[jax.ShapeDtypeStruct((rows, cols), F32)] * 3, compiler_params=_cparams(),
    )(*[t.reshape(rows, cols) for t in (w, g, m, v)])
    return [o.reshape(shape) for o in outs]


_WEIGHTS = ['meta', 'ffn1_w_gate', 'ffn1_w_up', 'ffn1_w_down', 'ln1_g', 'ln1_b', 'w_in', 'mla_q_norm_g', 'mla_w_uq',
            'mla_kv_norm_g', 'mla_w_ukv', 'mla_w_o', 'conv_w', 'conv_b', 'conv_w_out', 's5_a_re', 's5_a_im', 's5_log_dt',
            's5_b_re', 's5_b_im', 's5_c_re', 's5_c_im', 's5_d', 's5_w_glu', 's5_b_glu', 's5_w_out', 'w_o', 'ln2_g', 'ln2_b',
            'ffn2_w_gate', 'ffn2_w_up', 'ffn2_w_down', 'ln3_g', 'ln3_b']


def _pad_to(flat, n):
    return jnp.concatenate([flat, jnp.zeros((n - flat.shape[0],), flat.dtype)])


def _shard_of(full, axis):
    if axis == 1:
        return _shard_cols(full)
    return full.reshape(N_SHARD, full.shape[0] // N_SHARD, full.shape[1])


_FFN_KEY = {'gate': 'wg', 'up': 'wu', 'down': 'wd'}


def _step(env):
    w = {n: env[n] for n in _WEIGHTS}
    mom = {n: env['m_' + n] for n in _WEIGHTS}
    var = {n: env['v_' + n] for n in _WEIGHTS}
    cidx = lax.axis_index("c").astype(jnp.int32).reshape(1)
    chip = 2 * lax.axis_index("x") + lax.axis_index("y")
    big_names = [n for n, _ in _BIG]
    nb = len(big_names)

    srcs = [w[n][li].astype(BF16) for li in range(DEPTH) for n in big_names]
    gathered, (conv_w_st, meta_st) = all_gather_shards(srcs, [w['conv_w'], w['meta']])
    meta_full = _nat_cols(meta_st)
    Ws = []
    for li in range(DEPTH):
        small = {n: w[n][li] for n in _REPL}
        small['conv_w'] = _nat_cols(conv_w_st[:, li])
        Ws.append(compute_weights({n: gathered[li * nb + i] for i, n in enumerate(big_names)}, small))

    loss_part, dx, dmeta, grads = local_step(env['x'][0], env['loss_target'][0], meta_full, Ws)
    loss = lax.psum(loss_part[0, 0], ("x", "y", "c"))
    full = [reference_grads(g, ffn=False) for g in grads]

    def shard(li, n, axis):
        if n.startswith('ffn'):
            return grads[li][_FFN_KEY[n.split('_')[-1]] + n[3]]
        return _shard_of(full[li][n], axis)

    gs = [shard(li, n, a) for li in range(DEPTH) for n, a in _BIG]
    s_parts = [jnp.stack([full[li][n] for li in range(DEPTH)]).reshape(-1) for n in _REPL + ['conv_w']] + [dmeta.reshape(-1)]
    s_sizes = [int(p.shape[0]) for p in s_parts]
    s_rows = -(-sum(s_sizes) // (16 * LANES)) * 16
    g_small = _pad_to(jnp.concatenate(s_parts), s_rows * LANES).reshape(1, s_rows, LANES)
    g_small = jnp.broadcast_to(g_small, (N_SHARD, s_rows, LANES))
    red = reduce_scatter_shards(gs + [g_small], [BF16] * len(gs) + [F32], cidx)
    f_small = red[-1].reshape(-1)

    grad = {n: jnp.stack([red[li * nb + i] for li in range(DEPTH)]) for i, n in enumerate(big_names)}
    off = 0
    for n, sz in zip(_REPL + ['conv_w', 'meta'], s_sizes):
        grad[n] = f_small[off:off + sz]
        off += sz
    for n in _REPL:
        grad[n] = grad[n].reshape(w[n].shape)
    cw = grad['conv_w'].reshape(DEPTH, 3, MIX)
    grad['conv_w'] = lax.dynamic_slice_in_dim(cw, chip * (MIX // N_SHARD), MIX // N_SHARD, axis=2)
    gm = grad['meta'].reshape(N_META, D_MODEL)
    grad['meta'] = lax.dynamic_slice_in_dim(gm, chip * (D_MODEL // N_SHARD), D_MODEL // N_SHARD, axis=1)

    delta, new_m, new_v = {}, {}, {}
    for n in _WEIGHTS:
        delta[n], new_m[n], new_v[n] = adamw(w[n], grad[n], mom[n], var[n], "adamw_" + n)
    return (loss, dx[None], *[grad[n] for n in _WEIGHTS], *[delta[n] for n in _WEIGHTS],
            *[new_m[n] for n in _WEIGHTS], *[new_v[n] for n in _WEIGHTS])


def kernel(x, meta, ffn1_w_gate, ffn1_w_up, ffn1_w_down, ln1_g, ln1_b, w_in, mla_q_norm_g, mla_w_uq, mla_kv_norm_g, mla_w_ukv, mla_w_o, conv_w, conv_b, conv_w_out, s5_a_re, s5_a_im, s5_log_dt, s5_b_re, s5_b_im, s5_c_re, s5_c_im, s5_d, s5_w_glu, s5_b_glu, s5_w_out, w_o, ln2_g, ln2_b, ffn2_w_gate, ffn2_w_up, ffn2_w_down, ln3_g, ln3_b, loss_target, m_meta, m_ffn1_w_gate, m_ffn1_w_up, m_ffn1_w_down, m_ln1_g, m_ln1_b, m_w_in, m_mla_q_norm_g, m_mla_w_uq, m_mla_kv_norm_g, m_mla_w_ukv, m_mla_w_o, m_conv_w, m_conv_b, m_conv_w_out, m_s5_a_re, m_s5_a_im, m_s5_log_dt, m_s5_b_re, m_s5_b_im, m_s5_c_re, m_s5_c_im, m_s5_d, m_s5_w_glu, m_s5_b_glu, m_s5_w_out, m_w_o, m_ln2_g, m_ln2_b, m_ffn2_w_gate, m_ffn2_w_up, m_ffn2_w_down, m_ln3_g, m_ln3_b, v_meta, v_ffn1_w_gate, v_ffn1_w_up, v_ffn1_w_down, v_ln1_g, v_ln1_b, v_w_in, v_mla_q_norm_g, v_mla_w_uq, v_mla_kv_norm_g, v_mla_w_ukv, v_mla_w_o, v_conv_w, v_conv_b, v_conv_w_out, v_s5_a_re, v_s5_a_im, v_s5_log_dt, v_s5_b_re, v_s5_b_im, v_s5_c_re, v_s5_c_im, v_s5_d, v_s5_w_glu, v_s5_b_glu, v_s5_w_out, v_w_o, v_ln2_g, v_ln2_b, v_ffn2_w_gate, v_ffn2_w_up, v_ffn2_w_down, v_ln3_g, v_ln3_b):
    return _step(dict(locals()))


def _unused_packed_step(env):
    w = {n: env[n] for n in _WEIGHTS}
    mom = {n: env['m_' + n] for n in _WEIGHTS}
    var = {n: env['v_' + n] for n in _WEIGHTS}
    cidx = lax.axis_index("c").astype(jnp.int32).reshape(1)
    big_names = [n for n, _ in _BIG]

    exact = jnp.concatenate([conv_w.reshape(-1), meta.reshape(-1)])
    parts = [w[n].astype(BF16).reshape(-1) for n in big_names] + [lax.bitcast_convert_type(exact, BF16).reshape(-1)]
    sizes = [int(p.shape[0]) for p in parts]
    hr = _half_rows(sum(sizes))
    packed = _pad_to(jnp.concatenate(parts), 2 * hr * LANES).reshape(2, hr, LANES)
    gathered = all_gather_halves(packed).reshape(N_SHARD, -1)
    st, off = {}, 0
    for n, sz in zip(big_names, sizes[:-1]):
        st[n] = gathered[:, off:off + sz].reshape((N_SHARD,) + w[n].shape)
        off += sz
    exact_all = lax.bitcast_convert_type(gathered[:, off:off + sizes[-1]].reshape(N_SHARD, -1, 2), F32)
    n_cw = conv_w.size
    conv_w_st = exact_all[:, :n_cw].reshape((N_SHARD,) + conv_w.shape)
    meta_full = _nat_cols(exact_all[:, n_cw:].reshape((N_SHARD,) + meta.shape))

    Ws = []
    for li in range(DEPTH):
        small = {n: w[n][li] for n in _REPL}
        small['conv_w'] = _nat_cols(conv_w_st[:, li])
        Ws.append(compute_weights({n: st[n][:, li] for n in big_names}, small))

    loss_part, dx, dmeta, grads = local_step(x[0], loss_target[0], meta_full, Ws)
    loss = lax.psum(loss_part[0, 0], ("x", "y", "c"))
    ref_grads = [shard_grads(g) for g in grads]

    def shard_stack(n):
        return jnp.stack([ref_grads[li][n] for li in range(DEPTH)], axis=1)

    g_parts = [shard_stack(n).reshape(N_SHARD, -1) for n, _ in _BIG]
    g_parts.append(shard_stack('conv_w').reshape(N_SHARD, -1))
    g_parts.append(_shard_cols(dmeta).reshape(N_SHARD, -1))
    g_sizes = [int(p.shape[1]) for p in g_parts]
    ghr = _half_rows(sum(g_sizes))
    g_big = jnp.concatenate(g_parts + [jnp.zeros((N_SHARD, 2 * ghr * LANES - sum(g_sizes)), F32)], axis=1)
    g_big = g_big.reshape(N_SHARD, 2, ghr, LANES)
    s_parts = [jnp.stack([ref_grads[li][n] for li in range(DEPTH)]).reshape(-1) for n in _REPL]
    s_sizes = [int(p.shape[0]) for p in s_parts]
    shr = _half_rows(sum(s_sizes))
    g_small = _pad_to(jnp.concatenate(s_parts), 2 * shr * LANES)
    g_small = jnp.broadcast_to(g_small.reshape(1, 2, shr, LANES), (N_SHARD, 2, shr, LANES))
    f_big, f_small = reduce_scatter(g_big, g_small, cidx)
    f_big, f_small = f_big.reshape(-1), f_small.reshape(-1)

    grad = {}
    off = 0
    for (n, _), sz in zip(_BIG + [('conv_w', 1), ('meta', 1)], g_sizes):
        grad[n] = f_big[off:off + sz].reshape(w[n].shape)
        off += sz
    off = 0
    for n, sz in zip(_REPL, s_sizes):
        grad[n] = f_small[off:off + sz].reshape(w[n].shape)
        off += sz

    delta, new_m, new_v = {}, {}, {}
    for n in _WEIGHTS:
        delta[n], new_m[n], new_v[n] = adamw(w[n], grad[n], mom[n], var[n], "adamw_" + n)
    return (loss, dx[None], *[grad[n] for n in _WEIGHTS], *[delta[n] for n in _WEIGHTS],
            *[new_m[n] for n in _WEIGHTS], *[new_v[n] for n in _WEIGHTS])
```

```python
import functools
import math

import numpy as np
import jax
import jax.numpy as jnp
from jax import lax
from jax.experimental import pallas as pl
from jax.experimental.pallas import tpu as pltpu

F32 = jnp.float32
BF16 = jnp.bfloat16

D_MODEL = 1024
DEPTH = 2
N_META = 16
PAD = 112
X0 = PAD + N_META
N_HEADS = 8
D_NOPE = 64
D_ROPE = 32
D_V = 64
Q_RANK = 384
KV_RANK = 256
MIX = 512
S5_GROUPS = 32
S5_GROUP = 16
S5_STATE = 64
S5_LANES = S5_GROUPS * S5_STATE
D_FF = 2816
N_SHARD = 4
FF_SHARD = D_FF // N_SHARD
D_IN = 5792
P_IN = 6144
ALPHA = (2.0 * DEPTH) ** 0.25
LN_EPS = 1e-5
RMS_EPS = 1e-6
ATT_SCALE = (D_NOPE + D_ROPE) ** -0.5
ROPE_BASE = 10000.0
ADAM_LR, ADAM_B1, ADAM_B2, ADAM_EPS, ADAM_WD, ADAM_STEP = 0.001, 0.9, 0.999, 1e-08, 0.01, 10
SCAN_CHUNK = 128
VMEM_LIMIT = 52 * 2 ** 20
WGRAD = BF16
MESH = pl.DeviceIdType.MESH


def _cparams(**kw):
    return pltpu.CompilerParams(vmem_limit_bytes=VMEM_LIMIT, **kw)


def _tile(n):
    if n <= 1088:
        return n
    for t in (1024, 544, 512, 272, 256, 128):
        if n % t == 0:
            return t
    return n


def _row_tile(lp):
    for t in (544, 272, 128):
        if lp % t == 0:
            return t
    return lp


def _sigmoid(x):
    return 1.0 / (1.0 + jnp.exp(-x))


_GELU_C = math.sqrt(2.0 / math.pi)


def _gelu(x):
    return 0.5 * x * (1.0 + jnp.tanh(_GELU_C * (x + 0.044715 * x * x * x)))


def _gelu_grad(x):
    t = jnp.tanh(_GELU_C * (x + 0.044715 * x * x * x))
    return 0.5 * (1.0 + t) + 0.5 * x * (1.0 - t * t) * _GELU_C * (1.0 + 3.0 * 0.044715 * x * x)


def _dot(a, b, ca, cb, precision=None):
    return lax.dot_general(a, b, (((ca,), (cb,)), ((), ())), preferred_element_type=F32, precision=precision)


def matmul(a, b, *, name, ta=False, tb=False, ab='n', bb='n', res=None, res_scale=1.0, scale=1.0, out_dtype=F32):
    if ta:
        _, K, M = a.shape
    else:
        _, M, K = a.shape
    if tb:
        _, N, K2 = b.shape
    else:
        _, K2, N = b.shape
    assert K == K2, (a.shape, b.shape)
    n_out = max(a.shape[0] if ab == 'o' else 1, b.shape[0] if bb == 'o' else 1)
    n_red = max(a.shape[0] if ab == 'r' else 1, b.shape[0] if bb == 'r' else 1)
    tm, tn = _tile(M), _tile(N)
    tk = K if K <= 2304 else _tile(K)
    nkt = K // tk
    n_steps = n_red * nkt

    def bsel(mode, o, r):
        if mode == 'o':
            return o
        if mode == 'r':
            return r // nkt if nkt > 1 else r
        return 0

    def ksel(r):
        if nkt == 1:
            return 0
        return r % nkt if n_red > 1 else r

    a_map = (lambda o, i, j, r: (bsel(ab, o, r), ksel(r), i)) if ta else (lambda o, i, j, r: (bsel(ab, o, r), i, ksel(r)))
    b_map = (lambda o, i, j, r: (bsel(bb, o, r), j, ksel(r))) if tb else (lambda o, i, j, r: (bsel(bb, o, r), ksel(r), j))
    o_map = lambda o, i, j, r: (o, i, j)
    in_specs = [pl.BlockSpec((None, tk, tm) if ta else (None, tm, tk), a_map),
                pl.BlockSpec((None, tn, tk) if tb else (None, tk, tn), b_map)]
    operands = [a, b]
    if res is not None:
        in_specs.append(pl.BlockSpec((None, tm, tn), o_map))
        operands.append(res)
    has_res = res is not None

    def body(*refs):
        a_ref, b_ref = refs[0], refs[1]
        res_ref = refs[2] if has_res else None
        o_ref = refs[3] if has_res else refs[2]
        part = _dot(a_ref[...].astype(BF16), b_ref[...].astype(BF16), 0 if ta else 1, 1 if tb else 0)

        def finish(acc):
            v = acc if scale == 1.0 else acc * scale
            if has_res:
                v = v + res_scale * res_ref[...].astype(F32)
            o_ref[...] = v.astype(o_ref.dtype)

        if n_steps == 1:
            finish(part)
        else:
            acc_ref = refs[-1]
            r = pl.program_id(3)

            @pl.when(r == 0)
            def _():
                acc_ref[...] = part

            @pl.when(r > 0)
            def _():
                acc_ref[...] += part

            @pl.when(r == n_steps - 1)
            def _():
                finish(acc_ref[...])

    return pl.pallas_call(
        body, name=name,
        grid=(n_out, M // tm, N // tn, n_steps),
        in_specs=in_specs,
        out_specs=pl.BlockSpec((None, tm, tn), o_map),
        out_shape=jax.ShapeDtypeStruct((n_out, M, N), out_dtype),
        scratch_shapes=[pltpu.VMEM((tm, tn), F32)] if n_steps > 1 else [],
        compiler_params=_cparams(),
    )(*operands)


def rowwise(fn, rows, pars, outs, accs=(), *, name, lp):
    tm = _row_tile(lp)
    n_rows, n_pars, n_outs, n_accs = len(rows), len(pars), len(outs), len(accs)
    in_specs = [pl.BlockSpec((tm, w), functools.partial(lambda i, cb: (i, cb), cb=cb)) for _, w, cb in rows]
    in_specs += [pl.BlockSpec(p.shape, functools.partial(lambda i, nd: (0,) * nd, nd=p.ndim)) for p in pars]
    out_specs = [pl.BlockSpec((tm, w), lambda i: (i, 0)) for w, _ in outs]
    out_specs += [pl.BlockSpec(s, functools.partial(lambda i, nd: (0,) * nd, nd=len(s))) for s, _ in accs]
    out_shape = [jax.ShapeDtypeStruct((lp, w), dt) for w, dt in outs]
    out_shape += [jax.ShapeDtypeStruct(s, dt) for s, dt in accs]

    def body(*refs):
        i = pl.program_id(0)
        rv = [r[...] for r in refs[:n_rows]]
        pv = [r[...] for r in refs[n_rows:n_rows + n_pars]]
        o_refs = refs[n_rows + n_pars:n_rows + n_pars + n_outs]
        a_refs = refs[n_rows + n_pars + n_outs:]
        ov, av = fn(i * tm, rv, pv)
        for r, v in zip(o_refs, ov):
            r[...] = v.astype(r.dtype)
        if n_accs:
            @pl.when(i == 0)
            def _():
                for r, v in zip(a_refs, av):
                    r[...] = v.astype(r.dtype)

            @pl.when(i > 0)
            def _():
                for r, v in zip(a_refs, av):
                    r[...] += v.astype(r.dtype)

    res = pl.pallas_call(
        body, name=name, grid=(lp // tm,), in_specs=in_specs, out_specs=out_specs, out_shape=out_shape,
        compiler_params=_cparams(),
    )(*[r[0] for r in rows], *pars)
    return res


def _row_mask(row0, shape):
    return (row0 + lax.broadcasted_iota(jnp.int32, shape, 0)) >= PAD


def ffn_up(hb, wg, wu, lp):
    tm = _row_tile(lp)

    def body(h_ref, wg_ref, wu_ref, ab_ref, hid_ref):
        h = h_ref[...]
        a = _dot(h, wg_ref[...], 1, 0)
        b = _dot(h, wu_ref[...], 1, 0)
        ab_ref[0] = a
        ab_ref[1] = b
        hid_ref[...] = (a * _sigmoid(a) * b).astype(BF16)

    wspec = pl.BlockSpec((None, D_MODEL, FF_SHARD), lambda j, i: (j, 0, 0))
    return pl.pallas_call(
        body, name="ffn_up", grid=(N_SHARD, lp // tm),
        in_specs=[pl.BlockSpec((tm, D_MODEL), lambda j, i: (i, 0)), wspec, wspec],
        out_specs=[pl.BlockSpec((None, 2, tm, FF_SHARD), lambda j, i: (j, 0, i, 0)),
                   pl.BlockSpec((None, tm, FF_SHARD), lambda j, i: (j, i, 0))],
        out_shape=[jax.ShapeDtypeStruct((N_SHARD, 2, lp, FF_SHARD), F32),
                   jax.ShapeDtypeStruct((N_SHARD, lp, FF_SHARD), BF16)],
        compiler_params=_cparams(),
    )(hb, wg, wu)


def _layer_norm(z, g, b):
    mu = jnp.mean(z, axis=-1, keepdims=True)
    zc = z - mu
    var = jnp.mean(zc * zc, axis=-1, keepdims=True)
    return zc * lax.rsqrt(var + LN_EPS) * g + b


def mm_res_ln(a, w, res, g, b, *, scale, name, lp):
    n_red, _, K = a.shape
    tm = _row_tile(lp)

    def body(a_ref, w_ref, res_ref, g_ref, b_ref, z_ref, h_ref, hb_ref, acc_ref):
        r = pl.program_id(1)
        part = _dot(a_ref[...].astype(BF16), w_ref[...], 1, 0)

        @pl.when(r == 0)
        def _():
            acc_ref[...] = part

        @pl.when(r > 0)
        def _():
            acc_ref[...] += part

        @pl.when(r == n_red - 1)
        def _():
            z = ALPHA * res_ref[...] + scale * acc_ref[...]
            z_ref[...] = z
            hn = _layer_norm(z, g_ref[...], b_ref[...])
            h_ref[...] = hn
            hb_ref[...] = hn.astype(BF16)

    row = pl.BlockSpec((tm, D_MODEL), lambda i, r: (i, 0))
    par = pl.BlockSpec((1, D_MODEL), lambda i, r: (0, 0))
    return pl.pallas_call(
        body, name=name, grid=(lp // tm, n_red),
        in_specs=[pl.BlockSpec((None, tm, K), lambda i, r: (r, i, 0)),
                  pl.BlockSpec((None, K, D_MODEL), lambda i, r: (r, 0, 0)), row, par, par],
        out_specs=[row, row, row],
        out_shape=[jax.ShapeDtypeStruct((lp, D_MODEL), F32), jax.ShapeDtypeStruct((lp, D_MODEL), F32),
                   jax.ShapeDtypeStruct((lp, D_MODEL), BF16)],
        scratch_shapes=[pltpu.VMEM((tm, D_MODEL), F32)],
        compiler_params=_cparams(),
    )(a, w, res, g, b)


def ln_bwd(dh, z, g, *, fscale, name, lp):
    def fn(row0, rv, pv):
        dh_, z_ = rv
        g_, = pv
        mu = jnp.mean(z_, axis=-1, keepdims=True)
        zc = z_ - mu
        rstd = lax.rsqrt(jnp.mean(zc * zc, axis=-1, keepdims=True) + LN_EPS)
        xh = zc * rstd
        dxh = dh_ * g_
        m1 = jnp.mean(dxh, axis=-1, keepdims=True)
        m2 = jnp.mean(dxh * xh, axis=-1, keepdims=True)
        dz = rstd * (dxh - m1 - xh * m2)
        return ((dz, fscale * dz),
                (jnp.sum(dh_ * xh, axis=0, keepdims=True), jnp.sum(dh_, axis=0, keepdims=True)))

    return rowwise(fn, [(dh, D_MODEL, 0), (z, D_MODEL, 0)], [g], [(D_MODEL, F32), (D_MODEL, BF16)],
                   [((1, D_MODEL), F32), ((1, D_MODEL), F32)], name=name, lp=lp)


def ffn_down_bwd(dfb, wd, ab, lp):
    tm = _row_tile(lp)

    def body(df_ref, w_ref, ab_ref, da_ref, db_ref):
        dhid = _dot(df_ref[...], w_ref[...], 1, 1)
        a = ab_ref[0]
        b = ab_ref[1]
        sg = _sigmoid(a)
        da_ref[...] = (dhid * b * (sg * (1.0 + a * (1.0 - sg)))).astype(BF16)
        db_ref[...] = (dhid * (a * sg)).astype(BF16)

    ospec = pl.BlockSpec((None, tm, FF_SHARD), lambda j, i: (j, i, 0))
    return pl.pallas_call(
        body, name="ffn_down_bwd", grid=(N_SHARD, lp // tm),
        in_specs=[pl.BlockSpec((tm, D_MODEL), lambda j, i: (i, 0)),
                  pl.BlockSpec((None, FF_SHARD, D_MODEL), lambda j, i: (j, 0, 0)),
                  pl.BlockSpec((None, 2, tm, FF_SHARD), lambda j, i: (j, 0, i, 0))],
        out_specs=[ospec, ospec],
        out_shape=[jax.ShapeDtypeStruct((N_SHARD, lp, FF_SHARD), BF16)] * 2,
        compiler_params=_cparams(),
    )(dfb, wd, ab)


def ffn_dx(da, db, wg, wu, dz, lp):
    tm = _row_tile(lp)

    def body(da_ref, db_ref, wg_ref, wu_ref, dz_ref, o_ref, acc_ref):
        j = pl.program_id(1)
        part = _dot(da_ref[...], wg_ref[...], 1, 1) + _dot(db_ref[...], wu_ref[...], 1, 1)

        @pl.when(j == 0)
        def _():
            acc_ref[...] = part

        @pl.when(j > 0)
        def _():
            acc_ref[...] += part

        @pl.when(j == N_SHARD - 1)
        def _():
            o_ref[...] = acc_ref[...] + ALPHA * dz_ref[...]

    aspec = pl.BlockSpec((None, tm, FF_SHARD), lambda i, j: (j, i, 0))
    wspec = pl.BlockSpec((None, D_MODEL, FF_SHARD), lambda i, j: (j, 0, 0))
    row = pl.BlockSpec((tm, D_MODEL), lambda i, j: (i, 0))
    return pl.pallas_call(
        body, name="ffn_dx", grid=(lp // tm, N_SHARD), in_specs=[aspec, aspec, wspec, wspec, row], out_specs=row,
        out_shape=jax.ShapeDtypeStruct((lp, D_MODEL), F32), scratch_shapes=[pltpu.VMEM((tm, D_MODEL), F32)],
        compiler_params=_cparams(),
    )(da, db, wg, wu, dz)


def ffn_fwd(h, hb, wg, wu, wd, g, b, lp):
    ab, hid = ffn_up(hb, wg, wu, lp)
    z, hn, hnb = mm_res_ln(hid, wd, h, g, b, scale=0.5, name="ffn_down_ln", lp=lp)
    return hn, hnb, dict(hb=hb, ab=ab, hid=hid, z=z)


def ffn_bwd(dh, sv, wg, wu, wd, g, lp):
    dz, dfb, dg, db = ln_bwd(dh, sv['z'], g, fscale=0.5, name="ffn_ln_bwd", lp=lp)
    da, dbb = ffn_down_bwd(dfb, wd, sv['ab'], lp)
    d_wd = matmul(sv['hid'], dfb[None], ta=True, ab='o', out_dtype=WGRAD, name="ffn_dwd")
    d_wg = matmul(sv['hb'][None], da, ta=True, bb='o', out_dtype=WGRAD, name="ffn_dwg")
    d_wu = matmul(sv['hb'][None], dbb, ta=True, bb='o', out_dtype=WGRAD, name="ffn_dwu")
    dh_in = ffn_dx(da, dbb, wg, wu, dz, lp)
    return dh_in, dict(wg=d_wg, wu=d_wu, wd=d_wd, ln_g=dg, ln_b=db)


def _rope_tables(lp):
    pos = np.arange(lp, dtype=np.float32) - PAD
    inv = ROPE_BASE ** (-np.arange(0, D_ROPE, 2, dtype=np.float32) / D_ROPE)
    ang = pos[:, None] * inv[None, :]
    cos = np.concatenate([np.cos(ang), np.cos(ang)], axis=1).astype(np.float32)
    sin = np.concatenate([np.sin(ang), np.sin(ang)], axis=1).astype(np.float32)
    rot = np.zeros((D_ROPE, D_ROPE), np.float32)
    half = D_ROPE // 2
    for j in range(half):
        rot[j + half, j] = -1.0
        rot[j, j + half] = 1.0
    return jnp.asarray(cos), jnp.asarray(sin), jnp.asarray(rot)


def _rot(x, rot):
    return _dot(x, rot, 1, 0, precision=lax.Precision.HIGHEST)


def _rms(x, g):
    r = lax.rsqrt(jnp.mean(x * x, axis=-1, keepdims=True) + RMS_EPS)
    return x * r * g


def mla_prep(proj, cos, sin, rot, qg, kvg, lp):
    def fn(row0, rv, pv):
        cq, krb, ckv, c, s = rv
        qg_, kvg_, rot_ = pv
        kr = krb[:, :D_ROPE]
        return ((_rms(cq, qg_), _rms(ckv, kvg_), kr * c + _rot(kr, rot_) * s), ())

    return rowwise(fn, [(proj, Q_RANK, 0), (proj, 128, 3), (proj, KV_RANK, 2), (cos, D_ROPE, 0), (sin, D_ROPE, 0)],
                   [qg, kvg, rot], [(Q_RANK, BF16), (KV_RANK, BF16), (D_ROPE, BF16)], name="mla_prep", lp=lp)


def mla_heads(cqn, ckvn, cos, sin, rot, wqn, wqr, wkn, wv, lp):
    tm = _row_tile(lp)

    def body(cq_ref, ckv_ref, c_ref, s_ref, rot_ref, wqn_ref, wqr_ref, wkn_ref, wv_ref, qn_ref, qr_ref, kn_ref, v_ref):
        cq = cq_ref[...]
        ckv = ckv_ref[...]
        qn_ref[...] = _dot(cq, wqn_ref[...], 1, 0).astype(BF16)
        qr = _dot(cq, wqr_ref[...], 1, 0)
        qr_ref[...] = (qr * c_ref[...] + _rot(qr, rot_ref[...]) * s_ref[...]).astype(BF16)
        kn_ref[...] = _dot(ckv, wkn_ref[...], 1, 0).astype(BF16)
        v_ref[...] = _dot(ckv, wv_ref[...], 1, 0).astype(BF16)

    def row(w):
        return pl.BlockSpec((tm, w), lambda h, i: (i, 0))

    def wspec(k, n):
        return pl.BlockSpec((None, k, n), lambda h, i: (h, 0, 0))

    def ospec(n):
        return pl.BlockSpec((None, tm, n), lambda h, i: (h, i, 0))

    return pl.pallas_call(
        body, name="mla_heads", grid=(N_HEADS, lp // tm),
        in_specs=[row(Q_RANK), row(KV_RANK), row(D_ROPE), row(D_ROPE),
                  pl.BlockSpec((D_ROPE, D_ROPE), lambda h, i: (0, 0)),
                  wspec(Q_RANK, D_NOPE), wspec(Q_RANK, D_ROPE), wspec(KV_RANK, D_NOPE), wspec(KV_RANK, D_V)],
        out_specs=[ospec(D_NOPE), ospec(D_ROPE), ospec(D_NOPE), ospec(D_V)],
        out_shape=[jax.ShapeDtypeStruct((N_HEADS, lp, D_NOPE), BF16), jax.ShapeDtypeStruct((N_HEADS, lp, D_ROPE), BF16),
                   jax.ShapeDtypeStruct((N_HEADS, lp, D_NOPE), BF16), jax.ShapeDtypeStruct((N_HEADS, lp, D_V), BF16)],
        compiler_params=_cparams(),
    )(cqn, ckvn, cos, sin, rot, wqn, wqr, wkn, wv)


def _att_probs(qn, qr, kn, kr, row0, tq, lp):
    s = (_dot(qn, kn, 1, 1) + _dot(qr, kr, 1, 1)) * ATT_SCALE
    qi = row0 + lax.broadcasted_iota(jnp.int32, (tq, lp), 0)
    ki = lax.broadcasted_iota(jnp.int32, (tq, lp), 1)
    s = jnp.where((ki <= qi) & (ki >= PAD), s, -1e30)
    p = jnp.exp(s - jnp.max(s, axis=-1, keepdims=True))
    return p / jnp.sum(p, axis=-1, keepdims=True)


def _att_specs(tq, lp):
    def qspec(n):
        return pl.BlockSpec((None, tq, n), lambda h, i: (h, i, 0))

    def kspec(n):
        return pl.BlockSpec((None, lp, n), lambda h, i: (h, 0, 0))

    return qspec, kspec, pl.BlockSpec((lp, D_ROPE), lambda h, i: (0, 0))


def _att_tile(lp):
    return 272 if lp % 272 == 0 else 128


def attn_fwd(qn, qr, kn, v, kr, lp):
    tq = _att_tile(lp)
    qspec, kspec, krspec = _att_specs(tq, lp)

    def body(qn_ref, qr_ref, kn_ref, v_ref, kr_ref, o_ref):
        i = pl.program_id(1)
        for k in range(lp // tq):
            @pl.when(i == k)
            def _(k=k):
                ke = (k + 1) * tq
                p = _att_probs(qn_ref[...], qr_ref[...], kn_ref[0:ke, :], kr_ref[0:ke, :], k * tq, tq, ke)
                o_ref[...] = _dot(p.astype(BF16), v_ref[0:ke, :], 1, 0).astype(BF16)

    return pl.pallas_call(
        body, name="attn_fwd", grid=(N_HEADS, lp // tq),
        in_specs=[qspec(D_NOPE), qspec(D_ROPE), kspec(D_NOPE), kspec(D_V), krspec],
        out_specs=qspec(D_V), out_shape=jax.ShapeDtypeStruct((N_HEADS, lp, D_V), BF16),
        compiler_params=_cparams(),
    )(qn, qr, kn, v, kr)


def attn_bwd(qn, qr, kn, v, kr, do, lp):
    tq = _att_tile(lp)
    qspec, kspec, krspec = _att_specs(tq, lp)

    def body(qn_ref, qr_ref, kn_ref, v_ref, kr_ref, do_ref, dqn_ref, dqr_ref, dkn_ref, dv_ref, dkr_ref):
        h, i = pl.program_id(0), pl.program_id(1)

        @pl.when(i == 0)
        def _():
            dkn_ref[...] = jnp.zeros_like(dkn_ref)
            dv_ref[...] = jnp.zeros_like(dv_ref)

        @pl.when((i == 0) & (h == 0))
        def _():
            dkr_ref[...] = jnp.zeros_like(dkr_ref)

        for k in range(lp // tq):
            @pl.when(i == k)
            def _(k=k):
                ke = (k + 1) * tq
                qn_, qr_, do_ = qn_ref[...], qr_ref[...], do_ref[...]
                kn_, v_, kr_ = kn_ref[0:ke, :], v_ref[0:ke, :], kr_ref[0:ke, :]
                p = _att_probs(qn_, qr_, kn_, kr_, k * tq, tq, ke)
                dp = _dot(do_, v_, 1, 1)
                delta = jnp.sum(p * dp, axis=-1, keepdims=True)
                ds = (p * (dp - delta) * ATT_SCALE).astype(BF16)
                dqn_ref[...] = _dot(ds, kn_, 1, 0).astype(BF16)
                dqr_ref[...] = _dot(ds, kr_, 1, 0)
                dkn_ref[0:ke, :] += _dot(ds, qn_, 0, 0)
                dv_ref[0:ke, :] += _dot(p.astype(BF16), do_, 0, 0)
                dkr_ref[0:ke, :] += _dot(ds, qr_, 0, 0)

    return pl.pallas_call(
        body, name="attn_bwd", grid=(N_HEADS, lp // tq),
        in_specs=[qspec(D_NOPE), qspec(D_ROPE), kspec(D_NOPE), kspec(D_V), krspec, qspec(D_V)],
        out_specs=[qspec(D_NOPE), qspec(D_ROPE), kspec(D_NOPE), kspec(D_V), krspec],
        out_shape=[jax.ShapeDtypeStruct((N_HEADS, lp, D_NOPE), BF16), jax.ShapeDtypeStruct((N_HEADS, lp, D_ROPE), F32),
                   jax.ShapeDtypeStruct((N_HEADS, lp, D_NOPE), F32), jax.ShapeDtypeStruct((N_HEADS, lp, D_V), F32),
                   jax.ShapeDtypeStruct((lp, D_ROPE), F32)],
        compiler_params=_cparams(),
    )(qn, qr, kn, v, kr, do)


def mla_heads_bwd(dqn, dqr, dkn, dv, cos, sin, rot, wqn, wqr, wkn, wv, lp):
    tm = _row_tile(lp)

    def body(dqn_ref, dqr_ref, dkn_ref, dv_ref, c_ref, s_ref, rot_ref, wqn_ref, wqr_ref, wkn_ref, wv_ref,
             dcq_ref, dckv_ref, dqrp_ref):
        h = pl.program_id(1)
        dqr_ = dqr_ref[...]
        dqrp = (dqr_ * c_ref[...] - _rot(dqr_ * s_ref[...], rot_ref[...])).astype(BF16)
        dqrp_ref[...] = dqrp
        dcq = _dot(dqn_ref[...], wqn_ref[...], 1, 1) + _dot(dqrp, wqr_ref[...], 1, 1)
        dckv = _dot(dkn_ref[...].astype(BF16), wkn_ref[...], 1, 1) + _dot(dv_ref[...].astype(BF16), wv_ref[...], 1, 1)

        @pl.when(h == 0)
        def _():
            dcq_ref[...] = dcq
            dckv_ref[...] = dckv

        @pl.when(h > 0)
        def _():
            dcq_ref[...] += dcq
            dckv_ref[...] += dckv

    def hspec(n):
        return pl.BlockSpec((None, tm, n), lambda i, h: (h, i, 0))

    def row(w):
        return pl.BlockSpec((tm, w), lambda i, h: (i, 0))

    def wspec(k, n):
        return pl.BlockSpec((None, k, n), lambda i, h: (h, 0, 0))

    return pl.pallas_call(
        body, name="mla_heads_bwd", grid=(lp // tm, N_HEADS),
        in_specs=[hspec(D_NOPE), hspec(D_ROPE), hspec(D_NOPE), hspec(D_V), row(D_ROPE), row(D_ROPE),
                  pl.BlockSpec((D_ROPE, D_ROPE), lambda i, h: (0, 0)),
                  wspec(Q_RANK, D_NOPE), wspec(Q_RANK, D_ROPE), wspec(KV_RANK, D_NOPE), wspec(KV_RANK, D_V)],
        out_specs=[row(Q_RANK), row(KV_RANK), hspec(D_ROPE)],
        out_shape=[jax.ShapeDtypeStruct((lp, Q_RANK), F32), jax.ShapeDtypeStruct((lp, KV_RANK), F32),
                   jax.ShapeDtypeStruct((N_HEADS, lp, D_ROPE), BF16)],
        compiler_params=_cparams(),
    )(dqn, dqr, dkn, dv, cos, sin, rot, wqn, wqr, wkn, wv)


def _rms_bwd(dy, x, g):
    r = lax.rsqrt(jnp.mean(x * x, axis=-1, keepdims=True) + RMS_EPS)
    n = x * r
    dn = dy * g
    dx = r * (dn - n * jnp.mean(dn * n, axis=-1, keepdims=True))
    return dx, jnp.sum(dy * n, axis=0, keepdims=True)


def mla_prep_bwd(dcq, dckv, dkr, proj, cos, sin, rot, qg, kvg, lp):
    def fn(row0, rv, pv):
        dcq_, dckv_, dkr_, cq, ckv, c, s = rv
        qg_, kvg_, rot_ = pv
        dxq, dgq = _rms_bwd(dcq_, cq, qg_)
        dxkv, dgkv = _rms_bwd(dckv_, ckv, kvg_)
        dkr_raw = dkr_ * c - _rot(dkr_ * s, rot_)
        return ((dxq, dxkv, dkr_raw), (dgq, dgkv))

    return rowwise(fn, [(dcq, Q_RANK, 0), (dckv, KV_RANK, 0), (dkr, D_ROPE, 0), (proj, Q_RANK, 0), (proj, KV_RANK, 2),
                        (cos, D_ROPE, 0), (sin, D_ROPE, 0)], [qg, kvg, rot],
                   [(Q_RANK, BF16), (KV_RANK, BF16), (D_ROPE, BF16)], [((1, Q_RANK), F32), ((1, KV_RANK), F32)],
                   name="mla_prep_bwd", lp=lp)


def _shift_down(x, d, rows):
    return jnp.where(rows >= d, pltpu.roll(x, d, 0), 0.0)


def _shift_up(x, d, rows, n):
    return jnp.where(rows < n - d, pltpu.roll(x, n - d, 0), 0.0)


_CONV_W = 128
_XB, _BG, _CG = 1024 // _CONV_W, 1536 // _CONV_W, 2048 // _CONV_W


def _conv_specs(lp):
    def pspec(base):
        return pl.BlockSpec((lp, _CONV_W), functools.partial(lambda c, base: (0, base + c), base=base))

    col = pl.BlockSpec((lp, _CONV_W), lambda c: (0, c))
    wspec = pl.BlockSpec((3, _CONV_W), lambda c: (0, c))
    bspec = pl.BlockSpec((1, _CONV_W), lambda c: (0, c))
    return pspec, col, wspec, bspec


def _conv_core(xbar, cg, w, bias, lp):
    rows = lax.broadcasted_iota(jnp.int32, (lp, _CONV_W), 0)
    u = jnp.where(rows >= PAD, cg * xbar, 0.0)
    u1 = _shift_down(u, 1, rows)
    u2 = _shift_down(u, 2, rows)
    y = bias + w[0:1] * u2 + w[1:2] * u1 + w[2:3] * u
    return rows, u, u1, u2, y


def conv_fwd(proj, w, bias, lp):
    pspec, col, wspec, bspec = _conv_specs(lp)

    def body(x_ref, b_ref, c_ref, w_ref, bias_ref, v_ref):
        _, _, _, _, y = _conv_core(x_ref[...], c_ref[...], w_ref[...], bias_ref[...], lp)
        v_ref[...] = (b_ref[...] * y).astype(BF16)

    return pl.pallas_call(
        body, name="conv_fwd", grid=(MIX // _CONV_W,),
        in_specs=[pspec(_XB), pspec(_BG), pspec(_CG), wspec, bspec], out_specs=col,
        out_shape=jax.ShapeDtypeStruct((lp, MIX), BF16), compiler_params=_cparams(),
    )(proj, proj, proj, w, bias)


def conv_bwd(dv, proj, w, bias, lp):
    pspec, col, wspec, bspec = _conv_specs(lp)

    def body(dv_ref, x_ref, b_ref, c_ref, w_ref, bias_ref, dx_ref, db_ref, dc_ref, dw_ref, dbias_ref):
        xbar, cg, w_ = x_ref[...], c_ref[...], w_ref[...]
        rows, u, u1, u2, y = _conv_core(xbar, cg, w_, bias_ref[...], lp)
        dv_ = dv_ref[...]
        db_ref[...] = (dv_ * y).astype(BF16)
        dy = dv_ * b_ref[...]
        dbias_ref[...] = jnp.sum(dy, axis=0, keepdims=True)
        dw_ref[0:1, :] = jnp.sum(dy * u2, axis=0, keepdims=True)
        dw_ref[1:2, :] = jnp.sum(dy * u1, axis=0, keepdims=True)
        dw_ref[2:3, :] = jnp.sum(dy * u, axis=0, keepdims=True)
        du = w_[2:3] * dy + w_[1:2] * _shift_up(dy, 1, rows, lp) + w_[0:1] * _shift_up(dy, 2, rows, lp)
        du = jnp.where(rows >= PAD, du, 0.0)
        dc_ref[...] = (du * xbar).astype(BF16)
        dx_ref[...] = (du * cg).astype(BF16)

    return pl.pallas_call(
        body, name="conv_bwd", grid=(MIX // _CONV_W,),
        in_specs=[col, pspec(_XB), pspec(_BG), pspec(_CG), wspec, bspec],
        out_specs=[col, col, col, wspec, bspec],
        out_shape=[jax.ShapeDtypeStruct((lp, MIX), BF16)] * 3 + [jax.ShapeDtypeStruct((3, MIX), F32),
                                                                jax.ShapeDtypeStruct((1, MIX), F32)],
        compiler_params=_cparams(),
    )(dv, proj, proj, proj, w, bias)


def _s5_disc(a_re, a_im, log_dt, b_re, b_im):
    dt = jnp.exp(log_dt)
    mag = jnp.exp(dt * a_re)
    ab_re, ab_im = mag * jnp.cos(dt * a_im), mag * jnp.sin(dt * a_im)
    den = a_re * a_re + a_im * a_im
    nr, ni = ab_re - 1.0, ab_im
    coef_re = (nr * a_re + ni * a_im) / den
    coef_im = (ni * a_re - nr * a_im) / den
    return ab_re, ab_im, coef_re * b_re - coef_im * b_im, coef_re * b_im + coef_im * b_re


_S5_ROWS = S5_GROUPS * S5_GROUP


def s5_prep(a_re, a_im, log_dt, b_re, b_im):
    def body(ar, ai, ld, br, bi, o0, o1, o2, o3):
        for o, v in zip((o0, o1, o2, o3), _s5_disc(ar[...], ai[...], ld[...], br[...], bi[...])):
            o[...] = v

    return pl.pallas_call(body, name="s5_prep",
                          out_shape=[jax.ShapeDtypeStruct((_S5_ROWS, S5_STATE), F32)] * 4)(a_re, a_im, log_dt, b_re, b_im)


def s5_prep_bwd(a_re, a_im, log_dt, b_re, b_im, d_ab_re, d_ab_im, d_bb_re, d_bb_im, sel):
    def body(ar, ai, ld, br, bi, g0, g1, g2, g3, sel_ref, da_re, da_im, dld, dbr, dbi):
        _, vjp = jax.vjp(_s5_disc, ar[...], ai[...], ld[...], br[...], bi[...])
        c_ar, c_ai, c_ld, c_br, c_bi = vjp((g0[...], g1[...], g2[...], g3[...]))
        s = sel_ref[...]
        hi = lax.Precision.HIGHEST
        da_re[...] = _dot(s, c_ar, 1, 0, precision=hi)
        da_im[...] = _dot(s, c_ai, 1, 0, precision=hi)
        dld[...] = jnp.sum(_dot(s, c_ld, 1, 0, precision=hi), axis=-1, keepdims=True)
        dbr[...] = c_br
        dbi[...] = c_bi

    g = jax.ShapeDtypeStruct((S5_GROUPS, S5_STATE), F32)
    full = jax.ShapeDtypeStruct((_S5_ROWS, S5_STATE), F32)
    return pl.pallas_call(body, name="s5_prep_bwd",
                          out_shape=[g, g, jax.ShapeDtypeStruct((S5_GROUPS, 1), F32), full, full],
                          )(a_re, a_im, log_dt, b_re, b_im, d_ab_re, d_ab_im, d_bb_re, d_bb_im, sel)


_SCAN_W = 128
_SCAN_STEPS = int(math.log2(SCAN_CHUNK))


def _cmul(ar, ai, br, bi):
    return ar * br - ai * bi, ar * bi + ai * br


def _scan_powers(ar, ai, reverse):
    pw = [(ar, ai)]
    for _ in range(_SCAN_STEPS):
        pw.append(_cmul(*pw[-1], *pw[-1]))
    rows = lax.broadcasted_iota(jnp.int32, (SCAN_CHUNK, ar.shape[-1]), 0)
    tr = jnp.broadcast_to(ar, rows.shape)
    ti = jnp.broadcast_to(ai, rows.shape)
    for k in range(_SCAN_STEPS):
        d = 2 ** k
        if reverse:
            live = rows < SCAN_CHUNK - d
            mr, mi = _cmul(tr, ti, _shift_up(tr, d, rows, SCAN_CHUNK), _shift_up(ti, d, rows, SCAN_CHUNK))
        else:
            live = rows >= d
            mr, mi = _cmul(tr, ti, _shift_down(tr, d, rows), _shift_down(ti, d, rows))
        tr = jnp.where(live, mr, tr)
        ti = jnp.where(live, mi, ti)
    return pw, rows, tr, ti


def s5_scan(bu, ab_re, ab_im, lp):
    n_chunks = lp // SCAN_CHUNK

    def body(bu_ref, ar_ref, ai_ref, s_ref):
        ar, ai = ar_ref[...], ai_ref[...]
        pw, rows, tr, ti = _scan_powers(ar, ai, False)

        def chunk(ci, carry):
            cr, cim = carry
            r0 = pl.multiple_of(ci * SCAN_CHUNK, SCAN_CHUNK)
            xr = bu_ref[0, pl.ds(r0, SCAN_CHUNK), :]
            xi = bu_ref[1, pl.ds(r0, SCAN_CHUNK), :]
            for k in range(_SCAN_STEPS):
                d = 2 ** k
                mr, mi = _cmul(pw[k][0], pw[k][1], _shift_down(xr, d, rows), _shift_down(xi, d, rows))
                xr, xi = xr + mr, xi + mi
            mr, mi = _cmul(tr, ti, cr, cim)
            xr, xi = xr + mr, xi + mi
            s_ref[0, pl.ds(r0, SCAN_CHUNK), :] = xr
            s_ref[1, pl.ds(r0, SCAN_CHUNK), :] = xi
            return xr[SCAN_CHUNK - 1:SCAN_CHUNK, :], xi[SCAN_CHUNK - 1:SCAN_CHUNK, :]

        zero = jnp.zeros((1, _SCAN_W), F32)
        lax.fori_loop(0, n_chunks, chunk, (zero, zero))

    spec = pl.BlockSpec((2, lp, _SCAN_W), lambda c: (0, 0, c))
    aspec = pl.BlockSpec((1, _SCAN_W), lambda c: (0, c))
    return pl.pallas_call(
        body, name="s5_scan", grid=(S5_LANES // _SCAN_W,), in_specs=[spec, aspec, aspec], out_specs=spec,
        out_shape=jax.ShapeDtypeStruct((2, lp, S5_LANES), F32), compiler_params=_cparams(),
    )(bu, ab_re, ab_im)


def s5_scan_bwd(ds, s, ab_re, ab_im, lp):
    n_chunks = lp // SCAN_CHUNK

    def body(ds_ref, s_ref, ar_ref, ai_ref, g_ref, da_ref):
        ar, ai = ar_ref[...], -ai_ref[...]
        pw, rows, tr, ti = _scan_powers(ar, ai, True)

        def chunk(k, carry):
            cr, cim, dar, dai = carry
            ci = n_chunks - 1 - k
            r0 = pl.multiple_of(ci * SCAN_CHUNK, SCAN_CHUNK)
            xr = ds_ref[0, pl.ds(r0, SCAN_CHUNK), :]
            xi = ds_ref[1, pl.ds(r0, SCAN_CHUNK), :]
            for j in range(_SCAN_STEPS):
                d = 2 ** j
                mr, mi = _cmul(pw[j][0], pw[j][1], _shift_up(xr, d, rows, SCAN_CHUNK), _shift_up(xi, d, rows, SCAN_CHUNK))
                xr, xi = xr + mr, xi + mi
            mr, mi = _cmul(tr, ti, cr, cim)
            xr, xi = xr + mr, xi + mi
            g_ref[0, pl.ds(r0, SCAN_CHUNK), :] = xr
            g_ref[1, pl.ds(r0, SCAN_CHUNK), :] = xi
            prev0 = pl.multiple_of(jnp.maximum(r0 - 8, 0), 8)
            live = (ci > 0).astype(F32)
            pr = s_ref[0, pl.ds(prev0, 8), :][7:8, :] * live
            pim = s_ref[1, pl.ds(prev0, 8), :][7:8, :] * live
            sr = s_ref[0, pl.ds(r0, SCAN_CHUNK), :]
            si = s_ref[1, pl.ds(r0, SCAN_CHUNK), :]
            sr = jnp.where(rows >= 1, pltpu.roll(sr, 1, 0), pr)
            si = jnp.where(rows >= 1, pltpu.roll(si, 1, 0), pim)
            dar = dar + jnp.sum(xr * sr + xi * si, axis=0, keepdims=True)
            dai = dai + jnp.sum(xi * sr - xr * si, axis=0, keepdims=True)
            return xr[0:1, :], xi[0:1, :], dar, dai

        zero = jnp.zeros((1, _SCAN_W), F32)
        _, _, dar, dai = lax.fori_loop(0, n_chunks, chunk, (zero, zero, zero, zero))
        da_ref[0] = dar
        da_ref[1] = dai

    spec = pl.BlockSpec((2, lp, _SCAN_W), lambda c: (0, 0, c))
    aspec = pl.BlockSpec((1, _SCAN_W), lambda c: (0, c))
    return pl.pallas_call(
        body, name="s5_scan_bwd", grid=(S5_LANES // _SCAN_W,), in_specs=[spec, spec, aspec, aspec],
        out_specs=[spec, pl.BlockSpec((2, 1, _SCAN_W), lambda c: (0, 0, c))],
        out_shape=[jax.ShapeDtypeStruct((2, lp, S5_LANES), F32), jax.ShapeDtypeStruct((2, 1, S5_LANES), F32)],
        compiler_params=_cparams(),
    )(ds, s, ab_re, ab_im)


def _blockdiag(x):
    g, r, c = x.shape
    eye = jnp.eye(g, dtype=x.dtype)
    return (x[:, :, None, :] * eye[:, None, :, None]).reshape(g * r, g * c)


def _blockdiag_extract(m, g, r, c):
    return jnp.einsum('grgc->grc', m.reshape(g, r, g, c))


def s5_u(proj, lp):
    def fn(row0, rv, pv):
        u, = rv
        return ((jnp.where(_row_mask(row0, u.shape), u, 0.0),), ())

    return rowwise(fn, [(proj, MIX, 5)], [], [(MIX, BF16)], name="s5_u", lp=lp)[0]


def s5_y(ys, proj, d, lp):
    def fn(row0, rv, pv):
        ys_, u = rv
        y = ys_ + pv[0] * u
        return ((y, _gelu(y)), ())

    return rowwise(fn, [(ys, MIX, 0), (proj, MIX, 5)], [d], [(MIX, F32), (MIX, BF16)], name="s5_y", lp=lp)


def s5_glu(z, y, b, lp):
    def fn(row0, rv, pv):
        z_, y_ = rv
        return ((_gelu(y_) * _sigmoid(z_ + pv[0]),), ())

    return rowwise(fn, [(z, MIX, 0), (y, MIX, 0)], [b], [(MIX, BF16)], name="s5_glu", lp=lp)[0]


def s5_glu_bwd(dgl, z, y, b, lp):
    def fn(row0, rv, pv):
        dgl_, z_, y_ = rv
        sg = _sigmoid(z_ + pv[0])
        dz = dgl_ * _gelu(y_) * sg * (1.0 - sg)
        return ((dgl_ * sg, dz), (jnp.sum(dz, axis=0, keepdims=True),))

    return rowwise(fn, [(dgl, MIX, 0), (z, MIX, 0), (y, MIX, 0)], [b], [(MIX, F32), (MIX, BF16)], [((1, MIX), F32)],
                   name="s5_glu_bwd", lp=lp)


def s5_y_bwd(dyg, y, proj, d, lp):
    def fn(row0, rv, pv):
        dyg_, y_, u = rv
        dy = dyg_ * _gelu_grad(y_)
        return ((dy, dy * pv[0]), (jnp.sum(dy * u, axis=0, keepdims=True),))

    return rowwise(fn, [(dyg, MIX, 0), (y, MIX, 0), (proj, MIX, 5)], [d], [(MIX, BF16), (MIX, F32)], [((1, MIX), F32)],
                   name="s5_y_bwd", lp=lp)


def s5_du(du, lp):
    def fn(row0, rv, pv):
        return ((jnp.where(_row_mask(row0, rv[0].shape), rv[0], 0.0),), ())

    return rowwise(fn, [(du, MIX, 0)], [], [(MIX, BF16)], name="s5_du", lp=lp)[0]


def merge_fwd(proj, ya, yb, yc, lp):
    def fn(row0, rv, pv):
        g0, g1, g2, a, b, c = rv
        return ((_sigmoid(g0) * a + _sigmoid(g1) * b + _sigmoid(g2) * c,), ())

    return rowwise(fn, [(proj, D_MODEL, 3), (proj, D_MODEL, 4), (proj, D_MODEL, 5), (ya, D_MODEL, 0), (yb, D_MODEL, 0),
                        (yc, D_MODEL, 0)], [], [(D_MODEL, BF16)], name="merge_fwd", lp=lp)[0]


def merge_bwd(dmix, proj, ya, yb, yc, lp):
    def fn(row0, rv, pv):
        dm, g0, g1, g2, a, b, c = rv
        outs_y, outs_g = [], []
        for g, yv in ((g0, a), (g1, b), (g2, c)):
            sg = _sigmoid(g)
            outs_y.append(dm * sg)
            outs_g.append(dm * yv * sg * (1.0 - sg))
        return (tuple(outs_y) + tuple(outs_g), ())

    return rowwise(fn, [(dmix, D_MODEL, 0), (proj, D_MODEL, 3), (proj, D_MODEL, 4), (proj, D_MODEL, 5),
                        (ya, D_MODEL, 0), (yb, D_MODEL, 0), (yc, D_MODEL, 0)], [], [(D_MODEL, BF16)] * 6,
                   name="merge_bwd", lp=lp)


def loss_head(h, tgt, lp):
    def fn(row0, rv, pv):
        h_, t_ = rv
        live = (row0 + lax.broadcasted_iota(jnp.int32, h_.shape, 0)) >= X0
        diff = jnp.where(live, h_ - t_, 0.0)
        ssq = jnp.sum(jnp.sum(diff * diff, axis=1, keepdims=True), axis=0, keepdims=True)
        return ((diff * (1.0 / D_MODEL),), (ssq * (0.5 / D_MODEL),))

    return rowwise(fn, [(h, D_MODEL, 0), (tgt, D_MODEL, 0)], [], [(D_MODEL, F32)], [((1, 1), F32)], name="loss_head", lp=lp)


def _s5_consts(W):
    ab_re_rep, ab_im_rep, bb_re, bb_im = s5_prep(W['s5_a_re'], W['s5_a_im'], W['s5_log_dt'], W['s5_b_re'], W['s5_b_im'])
    pick = lambda t: t.reshape(S5_GROUPS, S5_GROUP, S5_STATE)[:, 0].reshape(1, S5_LANES)
    bb = jnp.stack([_blockdiag(bb_re.reshape(S5_GROUPS, S5_GROUP, S5_STATE)),
                    _blockdiag(bb_im.reshape(S5_GROUPS, S5_GROUP, S5_STATE))]).astype(BF16)
    return pick(ab_re_rep), pick(ab_im_rep), bb


def layer_fwd(h, hb, W, tabs, lp):
    cos, sin, rot = tabs
    h1, h1b, sv1 = ffn_fwd(h, hb, W['wg1'], W['wu1'], W['wd1'], W['ln1_g'], W['ln1_b'], lp)
    proj = matmul(h1b[None], W['w_in'][None], name="proj")[0]
    cqn, ckvn, kr = mla_prep(proj, cos, sin, rot, W['q_norm_g'], W['kv_norm_g'], lp)
    qn, qr, kn, v = mla_heads(cqn, ckvn, cos, sin, rot, W['wqn'], W['wqr'], W['wkn'], W['wv'], lp)
    o = attn_fwd(qn, qr, kn, v, kr, lp)
    ya = matmul(o, W['mla_wo'], ab='r', bb='r', name="mla_out")[0]
    vconv = conv_fwd(proj, W['conv_w'], W['conv_b'], lp)
    yb = matmul(vconv[None], W['conv_wout'][None], name="conv_out")[0]
    ub = s5_u(proj, lp)
    ab_re, ab_im, bb = _s5_consts(W)
    bu = matmul(ub[None], bb, bb='o', name="s5_bu")
    s = s5_scan(bu, ab_re, ab_im, lp)
    ys = matmul(s, W['s5_ct'], ab='r', bb='r', name="s5_cs")[0]
    y, ygb = s5_y(ys, proj, W['s5_d'], lp)
    zg = matmul(ygb[None], W['s5_wglu'][None], name="s5_glu_mm")[0]
    glb = s5_glu(zg, y, W['s5_b_glu'], lp)
    yc = matmul(glb[None], W['s5_wout'][None], name="s5_out")[0]
    mixed = merge_fwd(proj, ya, yb, yc, lp)
    z2, h2, h2b = mm_res_ln(mixed[None], W['w_o'][None], h1, W['ln2_g'], W['ln2_b'], scale=1.0, name="wo_ln", lp=lp)
    h3, h3b, sv3 = ffn_fwd(h2, h2b, W['wg2'], W['wu2'], W['wd2'], W['ln3_g'], W['ln3_b'], lp)
    sv = dict(sv1=sv1, sv3=sv3, h1b=h1b, proj=proj, cqn=cqn, ckvn=ckvn, kr=kr, qn=qn, qr=qr, kn=kn, v=v, o=o, ya=ya,
              vconv=vconv, yb=yb, ub=ub, ab_re=ab_re, ab_im=ab_im, bb=bb, s=s, y=y, ygb=ygb, zg=zg, glb=glb, yc=yc,
              mixed=mixed, z2=z2)
    return h3, h3b, sv


def layer_bwd(dh3, sv, W, tabs, lp):
    cos, sin, rot = tabs
    proj = sv['proj']
    G = {}
    dh2, g3 = ffn_bwd(dh3, sv['sv3'], W['wg2'], W['wu2'], W['wd2'], W['ln3_g'], lp)
    G.update(wg2=g3['wg'], wu2=g3['wu'], wd2=g3['wd'], ln3_g=g3['ln_g'], ln3_b=g3['ln_b'])
    dz2, dz2b, G['ln2_g'], G['ln2_b'] = ln_bwd(dh2, sv['z2'], W['ln2_g'], fscale=1.0, name="wo_ln_bwd", lp=lp)
    dmix = matmul(dz2b[None], W['w_o'][None], tb=True, name="wo_dx")[0]
    G['w_o'] = matmul(sv['mixed'][None], dz2b[None], ta=True, out_dtype=WGRAD, name="wo_dw")[0]
    dya, dyb, dyc, dg0, dg1, dg2 = merge_bwd(dmix, proj, sv['ya'], sv['yb'], sv['yc'], lp)
    dgl = matmul(dyc[None], W['s5_wout'][None], tb=True, name="s5_out_dx")[0]
    G['s5_wout'] = matmul(sv['glb'][None], dyc[None], ta=True, out_dtype=WGRAD, name="s5_out_dw")[0]
    t1, dzb, G['s5_b_glu'] = s5_glu_bwd(dgl, sv['zg'], sv['y'], W['s5_b_glu'], lp)
    dyg = matmul(dzb[None], W['s5_wglu'][None], tb=True, res=t1[None], name="s5_glu_dx")[0]
    G['s5_wglu'] = matmul(sv['ygb'][None], dzb[None], ta=True, out_dtype=WGRAD, name="s5_glu_dw")[0]
    dyb_, du_d, G['s5_d'] = s5_y_bwd(dyg, sv['y'], proj, W['s5_d'], lp)
    ds = matmul(dyb_[None], W['s5_ct'], tb=True, bb='o', name="s5_cs_dx")
    G['s5_ct'] = matmul(sv['s'], dyb_[None], ta=True, ab='o', name="s5_cs_dw")
    g_adj, d_ab = s5_scan_bwd(ds, sv['s'], sv['ab_re'], sv['ab_im'], lp)
    du = matmul(g_adj, sv['bb'], tb=True, ab='r', bb='r', res=du_d[None], name="s5_bu_dx")[0]
    d_bb = matmul(sv['ub'][None], g_adj, ta=True, bb='o', name="s5_bu_dw")
    du_b = s5_du(du, lp)
    onehot = (jnp.arange(S5_GROUP) == 0).astype(F32)
    spread = lambda t: (t.reshape(S5_GROUPS, 1, S5_STATE) * onehot[None, :, None]).reshape(_S5_ROWS, S5_STATE)
    take = lambda t: _blockdiag_extract(t, S5_GROUPS, S5_GROUP, S5_STATE).reshape(_S5_ROWS, S5_STATE)
    sel = jnp.kron(jnp.eye(S5_GROUPS, dtype=F32), jnp.ones((1, S5_GROUP), F32))
    (G['s5_a_re'], G['s5_a_im'], G['s5_log_dt'], G['s5_b_re'], G['s5_b_im']) = s5_prep_bwd(
        W['s5_a_re'], W['s5_a_im'], W['s5_log_dt'], W['s5_b_re'], W['s5_b_im'],
        spread(d_ab[0]), spread(d_ab[1]), take(d_bb[0]), take(d_bb[1]), sel)
    dv = matmul(dyb[None], W['conv_wout'][None], tb=True, name="conv_out_dx")[0]
    G['conv_wout'] = matmul(sv['vconv'][None], dyb[None], ta=True, out_dtype=WGRAD, name="conv_out_dw")[0]
    dxbar, dbg, dcg, G['conv_w'], G['conv_b'] = conv_bwd(dv, proj, W['conv_w'], W['conv_b'], lp)
    do = matmul(dya[None], W['mla_wo'], tb=True, bb='o', out_dtype=BF16, name="mla_out_dx")
    G['mla_wo'] = matmul(sv['o'], dya[None], ta=True, ab='o', out_dtype=WGRAD, name="mla_out_dw")
    dqn, dqr, dkn, dvv, dkr = attn_bwd(sv['qn'], sv['qr'], sv['kn'], sv['v'], sv['kr'], do, lp)
    dcq, dckv, dqrp = mla_heads_bwd(dqn, dqr, dkn, dvv, cos, sin, rot, W['wqn'], W['wqr'], W['wkn'], W['wv'], lp)
    G['wqn'] = matmul(sv['cqn'][None], dqn, ta=True, bb='o', out_dtype=WGRAD, name="mla_dwqn")
    G['wqr'] = matmul(sv['cqn'][None], dqrp, ta=True, bb='o', out_dtype=WGRAD, name="mla_dwqr")
    G['wkn'] = matmul(sv['ckvn'][None], dkn, ta=True, bb='o', out_dtype=WGRAD, name="mla_dwkn")
    G['wv'] = matmul(sv['ckvn'][None], dvv, ta=True, bb='o', out_dtype=WGRAD, name="mla_dwv")
    dcq_raw, dckv_raw, dkr_raw, G['q_norm_g'], G['kv_norm_g'] = mla_prep_bwd(
        dcq, dckv, dkr, proj, cos, sin, rot, W['q_norm_g'], W['kv_norm_g'], lp)
    zeros = lambda n: jnp.zeros((lp, n), BF16)
    dproj = jnp.concatenate([dcq_raw, dkr_raw, zeros(96), dckv_raw, zeros(256), dxbar, dbg, dcg, du_b, dg0, dg1, dg2], axis=1)
    dh1 = matmul(dproj[None], W['w_in'][None], tb=True, res=dz2[None], res_scale=ALPHA, name="proj_dx")[0]
    G['w_in'] = matmul(sv['h1b'][None], dproj[None], ta=True, out_dtype=WGRAD, name="proj_dw")[0]
    dh0, g1 = ffn_bwd(dh1, sv['sv1'], W['wg1'], W['wu1'], W['wd1'], W['ln1_g'], lp)
    G.update(wg1=g1['wg'], wu1=g1['wu'], wd1=g1['wd'], ln1_g=g1['ln_g'], ln1_b=g1['ln_b'])
    return dh0, G


def _nat_cols(st):
    return jnp.transpose(st, (1, 0, 2)).reshape(st.shape[1], -1)


def _shard_cols(nat):
    k, n = nat.shape
    return jnp.transpose(nat.reshape(k, N_SHARD, n // N_SHARD), (1, 0, 2))


def _win_pad(w):
    z = lambda n: jnp.zeros((w.shape[0], n), w.dtype)
    return jnp.concatenate([w[:, 0:384], w[:, 640:672], z(96), w[:, 384:640], z(256), w[:, 672:]], axis=1)


def _win_unpad(wp):
    return jnp.concatenate([wp[:, 0:384], wp[:, 512:768], wp[:, 384:416], wp[:, 1024:]], axis=1)


_BIG = [('ffn1_w_gate', 1), ('ffn1_w_up', 1), ('ffn1_w_down', 0), ('w_in', 1), ('mla_w_uq', 1), ('mla_w_ukv', 1),
        ('mla_w_o', 1), ('conv_w_out', 1), ('s5_w_glu', 0), ('s5_w_out', 1), ('w_o', 0),
        ('ffn2_w_gate', 1), ('ffn2_w_up', 1), ('ffn2_w_down', 0)]
_REPL = ['ln1_g', 'ln1_b', 'mla_q_norm_g', 'mla_kv_norm_g', 'conv_b', 's5_a_re', 's5_a_im', 's5_log_dt', 's5_b_re',
         's5_b_im', 's5_c_re', 's5_c_im', 's5_d', 's5_b_glu', 'ln2_g', 'ln2_b', 'ln3_g', 'ln3_b']


def compute_weights(st, small):
    W = {}
    for t in ('1', '2'):
        W['wg' + t], W['wu' + t] = st['ffn%s_w_gate' % t], st['ffn%s_w_up' % t]
        W['wd' + t] = st['ffn%s_w_down' % t]
    W['w_in'] = _win_pad(_nat_cols(st['w_in']))
    uq = jnp.transpose(_nat_cols(st['mla_w_uq']).reshape(Q_RANK, N_HEADS, D_NOPE + D_ROPE), (1, 0, 2))
    W['wqn'], W['wqr'] = uq[:, :, :D_NOPE], uq[:, :, D_NOPE:]
    ukv = jnp.transpose(_nat_cols(st['mla_w_ukv']).reshape(KV_RANK, N_HEADS, D_NOPE + D_V), (1, 0, 2))
    W['wkn'], W['wv'] = ukv[:, :, :D_NOPE], ukv[:, :, D_NOPE:]
    W['mla_wo'] = _nat_cols(st['mla_w_o']).reshape(N_HEADS, D_V, D_MODEL)
    W['conv_wout'] = _nat_cols(st['conv_w_out'])
    W['s5_wglu'] = st['s5_w_glu'].reshape(MIX, MIX)
    W['s5_wout'] = _nat_cols(st['s5_w_out'])
    W['w_o'] = st['w_o'].reshape(D_MODEL, D_MODEL)
    W['conv_w'] = small['conv_w']
    for n in ('ln1_g', 'ln1_b', 'ln2_g', 'ln2_b', 'ln3_g', 'ln3_b', 'conv_b', 's5_b_glu'):
        W[n] = small[n].reshape(1, -1)
    W['q_norm_g'] = small['mla_q_norm_g'].reshape(1, -1)
    W['kv_norm_g'] = small['mla_kv_norm_g'].reshape(1, -1)
    W['s5_d'] = small['s5_d'].reshape(1, MIX)
    rep = lambda t: jnp.repeat(t, S5_GROUP, axis=0)
    W['s5_a_re'], W['s5_a_im'] = rep(small['s5_a_re']), rep(small['s5_a_im'])
    W['s5_log_dt'] = jnp.broadcast_to(rep(small['s5_log_dt'].reshape(S5_GROUPS, 1)), (_S5_ROWS, S5_STATE))
    tr = lambda t: jnp.transpose(t, (0, 2, 1)).reshape(_S5_ROWS, S5_STATE)
    W['s5_b_re'], W['s5_b_im'] = tr(small['s5_b_re']), tr(small['s5_b_im'])
    ct = lambda t: _blockdiag(jnp.transpose(t, (0, 2, 1)))
    W['s5_ct'] = jnp.stack([ct(small['s5_c_re']), -ct(small['s5_c_im'])]).astype(BF16)
    return W


def shard_grads(G):
    R = reference_grads(G, ffn=False)
    S = {n: (_shard_of(R[n], a) if n in R else None) for n, a in _BIG}
    for t in ('1', '2'):
        S['ffn%s_w_gate' % t] = G['wg' + t]
        S['ffn%s_w_up' % t] = G['wu' + t]
        S['ffn%s_w_down' % t] = G['wd' + t]
    S['conv_w'] = _shard_cols(R['conv_w'])
    for n in _REPL:
        S[n] = R[n]
    return S


def reference_grads(G, ffn=True):
    R = {}
    for t in ('1', '2') if ffn else ():
        R['ffn%s_w_gate' % t] = _nat_cols(G['wg' + t])
        R['ffn%s_w_up' % t] = _nat_cols(G['wu' + t])
        R['ffn%s_w_down' % t] = G['wd' + t].reshape(D_FF, D_MODEL)
    R['w_in'] = _win_unpad(G['w_in'])
    R['mla_w_uq'] = jnp.transpose(jnp.concatenate([G['wqn'], G['wqr']], axis=2), (1, 0, 2)).reshape(Q_RANK, -1)
    R['mla_w_ukv'] = jnp.transpose(jnp.concatenate([G['wkn'], G['wv']], axis=2), (1, 0, 2)).reshape(KV_RANK, -1)
    R['mla_w_o'] = G['mla_wo'].reshape(N_HEADS * D_V, D_MODEL)
    R['conv_w'], R['conv_w_out'] = G['conv_w'], G['conv_wout']
    R['s5_w_glu'], R['s5_w_out'], R['w_o'] = G['s5_wglu'], G['s5_wout'], G['w_o']
    for n in ('ln1_g', 'ln1_b', 'ln2_g', 'ln2_b', 'ln3_g', 'ln3_b', 'conv_b', 's5_b_glu'):
        R[n] = G[n].reshape(-1)
    R['mla_q_norm_g'], R['mla_kv_norm_g'] = G['q_norm_g'].reshape(-1), G['kv_norm_g'].reshape(-1)
    R['s5_d'] = G['s5_d'].reshape(S5_GROUPS, S5_GROUP)
    R['s5_a_re'], R['s5_a_im'], R['s5_log_dt'] = G['s5_a_re'], G['s5_a_im'], G['s5_log_dt'].reshape(-1)
    untr = lambda t: jnp.transpose(t.reshape(S5_GROUPS, S5_GROUP, S5_STATE), (0, 2, 1))
    R['s5_b_re'], R['s5_b_im'] = untr(G['s5_b_re']), untr(G['s5_b_im'])
    unct = lambda t: jnp.transpose(_blockdiag_extract(t, S5_GROUPS, S5_STATE, S5_GROUP), (0, 2, 1))
    R['s5_c_re'], R['s5_c_im'] = unct(G['s5_ct'][0]), -unct(G['s5_ct'][1])
    return R


def local_step(x2d, tgt2d, meta, Ws):
    lp = x2d.shape[0] + X0
    tabs = _rope_tables(lp)
    h = jnp.concatenate([jnp.zeros((PAD, D_MODEL), F32), meta, x2d], axis=0)
    hb = h.astype(BF16)
    saved = []
    for W in Ws:
        h, hb, sv = layer_fwd(h, hb, W, tabs, lp)
        saved.append(sv)
    tgt = jnp.concatenate([jnp.zeros((X0, D_MODEL), F32), tgt2d], axis=0)
    dh, loss = loss_head(h, tgt, lp)
    grads = [None] * len(Ws)
    for li in reversed(range(len(Ws))):
        dh, grads[li] = layer_bwd(dh, saved[li], Ws[li], tabs, lp)
    return loss, dh[X0:], dh[PAD:X0], grads


_ANY = pl.BlockSpec(memory_space=pl.ANY)
LANES = 1024


def _half_rows(n):
    rows = -(-n // (2 * LANES))
    q = 512 if rows > 512 else 16
    return -(-rows // q) * q


def _place():
    x, y, c = lax.axis_index("x"), lax.axis_index("y"), lax.axis_index("c")
    chips = [(1 - x, y), (x, 1 - y), (1 - x, 1 - y)]
    return x, y, c, chips


def all_gather_halves(src):
    def body(src_ref, out_ref, send_sems, recv_sems, local_sem):
        x, y, c, chips = _place()
        me = 2 * x + y
        sibling = (x, y, 1 - c)

        def copy(k, src, chip_idx, half, to):
            return pltpu.make_async_remote_copy(src_ref=src, dst_ref=out_ref.at[chip_idx, half], send_sem=send_sems.at[k],
                                                recv_sem=recv_sems.at[k], device_id=to, device_id_type=MESH)

        mine = pltpu.make_async_copy(src_ref, out_ref.at[me], local_sem)
        mine.start()
        first = [copy(j, src_ref.at[c], me, c, (*chip, c)) for j, chip in enumerate(chips)]
        for cp in first:
            cp.start()
        passed = []
        for j, chip in enumerate(chips):
            idx = 2 * chip[0] + chip[1]
            copy(j, src_ref.at[c], idx, c, sibling).wait_recv()
            cp = copy(3 + j, out_ref.at[idx, c], idx, c, sibling)
            cp.start()
            passed.append(cp)
        for j, chip in enumerate(chips):
            copy(3 + j, src_ref.at[c], 2 * chip[0] + chip[1], 1 - c, sibling).wait_recv()
        for cp in first + passed:
            cp.wait_send()
        mine.wait()

    return pl.pallas_call(
        body, name="all_gather_weights", in_specs=[_ANY], out_specs=_ANY,
        out_shape=jax.ShapeDtypeStruct((N_SHARD,) + src.shape, src.dtype),
        scratch_shapes=[pltpu.SemaphoreType.DMA((6,)), pltpu.SemaphoreType.DMA((6,)), pltpu.SemaphoreType.DMA],
    )(src)


def pair_swap(gs):
    n = len(gs)

    def body(*refs):
        g_refs, r_refs, send_sems, recv_sems = refs[:n], refs[n:2 * n], refs[2 * n], refs[2 * n + 1]
        x, y, c, _ = _place()
        copies = []
        for k in range(n):
            for j in range(N_SHARD):
                copies.append(pltpu.make_async_remote_copy(
                    src_ref=g_refs[k].at[j, 1 - c], dst_ref=r_refs[k].at[j], send_sem=send_sems.at[k * N_SHARD + j],
                    recv_sem=recv_sems.at[k * N_SHARD + j], device_id=(x, y, 1 - c), device_id_type=MESH))
        for cp in copies:
            cp.start()
        for cp in copies:
            cp.wait()

    return pl.pallas_call(
        body, name="grad_pair_swap", in_specs=[_ANY] * n, out_specs=[_ANY] * n,
        out_shape=[jax.ShapeDtypeStruct((N_SHARD,) + g.shape[2:], g.dtype) for g in gs],
        scratch_shapes=[pltpu.SemaphoreType.DMA((n * N_SHARD,)), pltpu.SemaphoreType.DMA((n * N_SHARD,))],
    )(*gs)


def _flat_tile(rows):
    return 512 if rows % 512 == 0 else rows


def pair_add(g, r, cidx, out_dtype, name):
    rows = g.shape[2]
    tr = _flat_tile(rows)

    def body(c_ref, g_ref, r_ref, o_ref):
        o_ref[...] = (g_ref[...] + r_ref[...]).astype(out_dtype)

    return pl.pallas_call(
        body, name=name,
        grid_spec=pltpu.PrefetchScalarGridSpec(
            num_scalar_prefetch=1, grid=(N_SHARD, rows // tr),
            in_specs=[pl.BlockSpec((None, None, tr, LANES), lambda j, i, c: (j, c[0], i, 0)),
                      pl.BlockSpec((None, tr, LANES), lambda j, i, c: (j, i, 0))],
            out_specs=pl.BlockSpec((None, tr, LANES), lambda j, i, c: (j, i, 0))),
        out_shape=jax.ShapeDtypeStruct((N_SHARD, rows, LANES), out_dtype),
        compiler_params=_cparams(),
    )(cidx, g, r)


def chip_scatter(ps):
    n = len(ps)

    def body(*refs):
        p_refs, r_refs, send_sems, recv_sems, local_sems = refs[:n], refs[n:2 * n], refs[2 * n], refs[2 * n + 1], refs[2 * n + 2]
        x, y, c, chips = _place()
        me = 2 * x + y
        local, copies = [], []
        for k in range(n):
            cp = pltpu.make_async_copy(p_refs[k].at[me], r_refs[k].at[me], local_sems.at[k])
            cp.start()
            local.append(cp)
            for j, chip in enumerate(chips):
                copies.append(pltpu.make_async_remote_copy(
                    src_ref=p_refs[k].at[2 * chip[0] + chip[1]], dst_ref=r_refs[k].at[me], send_sem=send_sems.at[k * 3 + j],
                    recv_sem=recv_sems.at[k * 3 + j], device_id=(*chip, c), device_id_type=MESH))
        for cp in copies:
            cp.start()
        for cp in copies:
            cp.wait()
        for cp in local:
            cp.wait()

    return pl.pallas_call(
        body, name="grad_chip_scatter", in_specs=[_ANY] * n, out_specs=[_ANY] * n,
        out_shape=[jax.ShapeDtypeStruct(p.shape, p.dtype) for p in ps],
        scratch_shapes=[pltpu.SemaphoreType.DMA((n * 3,)), pltpu.SemaphoreType.DMA((n * 3,)), pltpu.SemaphoreType.DMA((n,))],
    )(*ps)


def sum_slots(r, name):
    rows = r.shape[1]
    tr = _flat_tile(rows)

    def body(r_ref, o_ref):
        o_ref[...] = ((r_ref[0].astype(F32) + r_ref[1].astype(F32)) + r_ref[2].astype(F32)) + r_ref[3].astype(F32)

    return pl.pallas_call(
        body, name=name, grid=(rows // tr,),
        in_specs=[pl.BlockSpec((N_SHARD, tr, LANES), lambda i: (0, i, 0))],
        out_specs=pl.BlockSpec((tr, LANES), lambda i: (i, 0)),
        out_shape=jax.ShapeDtypeStruct((rows, LANES), F32), compiler_params=_cparams(),
    )(r)


def pair_gather(hs):
    n = len(hs)

    def body(*refs):
        h_refs, f_refs, send_sems, recv_sems, local_sems = refs[:n], refs[n:2 * n], refs[2 * n], refs[2 * n + 1], refs[2 * n + 2]
        x, y, c, _ = _place()
        local, copies = [], []
        for k in range(n):
            cp = pltpu.make_async_copy(h_refs[k], f_refs[k].at[c], local_sems.at[k])
            cp.start()
            local.append(cp)
            copies.append(pltpu.make_async_remote_copy(
                src_ref=h_refs[k], dst_ref=f_refs[k].at[c], send_sem=send_sems.at[k], recv_sem=recv_sems.at[k],
                device_id=(x, y, 1 - c), device_id_type=MESH))
        for cp in copies:
            cp.start()
        for cp in copies:
            cp.wait()
        for cp in local:
            cp.wait()

    return pl.pallas_call(
        body, name="grad_pair_gather", in_specs=[_ANY] * n, out_specs=[_ANY] * n,
        out_shape=[jax.ShapeDtypeStruct((2,) + h.shape, h.dtype) for h in hs],
        scratch_shapes=[pltpu.SemaphoreType.DMA((n,)), pltpu.SemaphoreType.DMA((n,)), pltpu.SemaphoreType.DMA((n,))],
    )(*hs)


def reduce_scatter(g_big, g_small, cidx):
    r_big, r_small = pair_swap([g_big, g_small])
    p_big = pair_add(g_big, r_big, cidx, BF16, "grad_pair_add")
    p_small = pair_add(g_small, r_small, cidx, F32, "grad_pair_add_small")
    q_big, q_small = chip_scatter([p_big, p_small])
    h_big = sum_slots(q_big, "grad_chip_sum")
    h_small = sum_slots(q_small, "grad_chip_sum_small")
    f_big, f_small = pair_gather([h_big, h_small])
    return f_big.reshape(-1, LANES), f_small.reshape(-1, LANES)


def _rows_of(c, half):
    return pl.ds(pl.multiple_of(c * half, 8), half)


def all_gather_shards(srcs, exact):
    n, m = len(srcs), len(exact)
    halves = [s.shape[0] // 2 for s in srcs]

    def body(*refs):
        s_refs, e_refs = refs[:n], refs[n:n + m]
        o_refs, eo_refs = refs[n + m:2 * n + m], refs[2 * n + m:2 * n + 2 * m]
        send, recv, esend, erecv, osend, orecv, lsem = refs[2 * n + 2 * m:]
        x, y, c, chips = _place()
        me = 2 * x + y
        sibling = (x, y, 1 - c)
        own = [pltpu.make_async_remote_copy(src_ref=s_refs[k], dst_ref=o_refs[k].at[me], send_sem=osend.at[k],
                                            recv_sem=orecv.at[k], device_id=sibling, device_id_type=MESH) for k in range(n)]
        local = [pltpu.make_async_copy(e_refs[k], eo_refs[k].at[me], lsem.at[k]) for k in range(m)]
        for cp in own + local:
            cp.start()

        def copy(k, s, src, idx, half_c, to):
            return pltpu.make_async_remote_copy(
                src_ref=src, dst_ref=o_refs[k].at[idx, _rows_of(half_c, halves[k])], send_sem=send.at[6 * k + s],
                recv_sem=recv.at[6 * k + s], device_id=to, device_id_type=MESH)

        def ecopy(k, j, idx, to):
            return pltpu.make_async_remote_copy(src_ref=e_refs[k], dst_ref=eo_refs[k].at[idx], send_sem=esend.at[3 * k + j],
                                                recv_sem=erecv.at[3 * k + j], device_id=to, device_id_type=MESH)

        sends = []
        for k in range(n):
            mine = s_refs[k].at[_rows_of(c, halves[k])]
            sends += [copy(k, j, mine, me, c, (*chip, c)) for j, chip in enumerate(chips)]
        for k in range(m):
            sends += [ecopy(k, j, me, (*chip, c)) for j, chip in enumerate(chips)]
        for cp in sends:
            cp.start()
        for j, chip in enumerate(chips):
            idx = 2 * chip[0] + chip[1]
            for k in range(n):
                landed = o_refs[k].at[idx, _rows_of(c, halves[k])]
                copy(k, j, landed, idx, c, sibling).wait_recv()
                fwd = copy(k, 3 + j, landed, idx, c, sibling)
                fwd.start()
                sends.append(fwd)
        for j, chip in enumerate(chips):
            idx = 2 * chip[0] + chip[1]
            for k in range(n):
                copy(k, 3 + j, s_refs[k].at[_rows_of(c, halves[k])], idx, 1 - c, sibling).wait_recv()
            for k in range(m):
                ecopy(k, j, idx, sibling).wait_recv()
        for cp in sends:
            cp.wait_send()
        for cp in own + local:
            cp.wait()

    outs = pl.pallas_call(
        body, name="all_gather_weights", in_specs=[_ANY] * (n + m), out_specs=[_ANY] * (n + m),
        out_shape=[jax.ShapeDtypeStruct((N_SHARD,) + a.shape, a.dtype) for a in list(srcs) + list(exact)],
        scratch_shapes=[pltpu.SemaphoreType.DMA((6 * n,)), pltpu.SemaphoreType.DMA((6 * n,)),
                        pltpu.SemaphoreType.DMA((3 * m,)), pltpu.SemaphoreType.DMA((3 * m,)),
                        pltpu.SemaphoreType.DMA((n,)), pltpu.SemaphoreType.DMA((n,)), pltpu.SemaphoreType.DMA((m,))],
    )(*srcs, *exact)
    return outs[:n], outs[n:]


def rs_pair_swap(gs):
    n = len(gs)

    def body(*refs):
        g_refs, r_refs, send, recv = refs[:n], refs[n:2 * n], refs[2 * n], refs[2 * n + 1]
        x, y, c, _ = _place()
        copies = [pltpu.make_async_remote_copy(
            src_ref=g_refs[k].at[pl.ds(0, N_SHARD), _rows_of(1 - c, gs[k].shape[1] // 2)], dst_ref=r_refs[k],
            send_sem=send.at[k], recv_sem=recv.at[k], device_id=(x, y, 1 - c), device_id_type=MESH) for k in range(n)]
        for cp in copies:
            cp.start()
        for cp in copies:
            cp.wait()

    return pl.pallas_call(
        body, name="grad_pair_swap", in_specs=[_ANY] * n, out_specs=[_ANY] * n,
        out_shape=[jax.ShapeDtypeStruct((N_SHARD, g.shape[1] // 2, g.shape[2]), g.dtype) for g in gs],
        scratch_shapes=[pltpu.SemaphoreType.DMA((n,)), pltpu.SemaphoreType.DMA((n,))],
    )(*gs)


def _group_tile(half, n_cols, n_arrays):
    budget = (20 * 2 ** 20) // (6 * n_arrays)
    fits = [t for t in range(8, half + 1, 8) if half % t == 0 and t * n_cols * 4 <= budget]
    return max(fits) if fits else 8


def rs_pair_add(gs, rs, cidx, out_dtype, name):
    n = len(gs)
    _, K, cols = gs[0].shape
    half = K // 2
    tr = _group_tile(half, cols, n)
    nb = half // tr

    def body(c_ref, *refs):
        for g_ref, r_ref, o_ref in zip(refs[:n], refs[n:2 * n], refs[2 * n:]):
            o_ref[...] = (g_ref[...].astype(F32) + r_ref[...].astype(F32)).astype(out_dtype)

    gspec = pl.BlockSpec((None, tr, cols), lambda j, i, c: (j, c[0] * nb + i, 0))
    rspec = pl.BlockSpec((None, tr, cols), lambda j, i, c: (j, i, 0))
    return pl.pallas_call(
        body, name=name,
        grid_spec=pltpu.PrefetchScalarGridSpec(num_scalar_prefetch=1, grid=(N_SHARD, nb), in_specs=[gspec] * n + [rspec] * n,
                                               out_specs=[rspec] * n),
        out_shape=[jax.ShapeDtypeStruct((N_SHARD, half, cols), out_dtype)] * n,
        compiler_params=_cparams(),
    )(cidx, *gs, *rs)


def rs_chip_scatter(ps):
    n = len(ps)

    def body(*refs):
        p_refs, q_refs, send, recv, lsem = refs[:n], refs[n:2 * n], refs[2 * n], refs[2 * n + 1], refs[2 * n + 2]
        x, y, c, chips = _place()
        me = 2 * x + y
        local = [pltpu.make_async_copy(p_refs[k].at[me], q_refs[k].at[me], lsem.at[k]) for k in range(n)]
        copies = [pltpu.make_async_remote_copy(
            src_ref=p_refs[k].at[2 * chip[0] + chip[1]], dst_ref=q_refs[k].at[me], send_sem=send.at[3 * k + j],
            recv_sem=recv.at[3 * k + j], device_id=(*chip, c), device_id_type=MESH)
            for k in range(n) for j, chip in enumerate(chips)]
        for cp in local + copies:
            cp.start()
        for cp in copies:
            cp.wait()
        for cp in local:
            cp.wait()

    return pl.pallas_call(
        body, name="grad_chip_scatter", in_specs=[_ANY] * n, out_specs=[_ANY] * n,
        out_shape=[jax.ShapeDtypeStruct(p.shape, p.dtype) for p in ps],
        scratch_shapes=[pltpu.SemaphoreType.DMA((3 * n,)), pltpu.SemaphoreType.DMA((3 * n,)), pltpu.SemaphoreType.DMA((n,))],
    )(*ps)


def rs_chip_sum(qs, cidx, name):
    n = len(qs)
    _, half, cols = qs[0].shape
    tr = _group_tile(half, cols, n)
    nb = half // tr

    def body(c_ref, *refs):
        for q_ref, o_ref in zip(refs[:n], refs[n:]):
            o_ref[...] = ((q_ref[0].astype(F32) + q_ref[1].astype(F32)) + q_ref[2].astype(F32)) + q_ref[3].astype(F32)

    return pl.pallas_call(
        body, name=name,
        grid_spec=pltpu.PrefetchScalarGridSpec(
            num_scalar_prefetch=1, grid=(nb,),
            in_specs=[pl.BlockSpec((N_SHARD, tr, cols), lambda i, c: (0, i, 0))] * n,
            out_specs=[pl.BlockSpec((tr, cols), lambda i, c: (c[0] * nb + i, 0))] * n),
        out_shape=[jax.ShapeDtypeStruct((2 * half, cols), F32)] * n,
        compiler_params=_cparams(),
    )(cidx, *qs)


def rs_pair_gather(fs):
    n = len(fs)

    def body(*refs):
        f_refs, send, recv = refs[n:2 * n], refs[2 * n], refs[2 * n + 1]
        x, y, c, _ = _place()
        copies = []
        for k in range(n):
            rows = f_refs[k].at[_rows_of(c, fs[k].shape[0] // 2)]
            copies.append(pltpu.make_async_remote_copy(src_ref=rows, dst_ref=rows, send_sem=send.at[k], recv_sem=recv.at[k],
                                                       device_id=(x, y, 1 - c), device_id_type=MESH))
        for cp in copies:
            cp.start()
        for cp in copies:
            cp.wait()

    return pl.pallas_call(
        body, name="grad_pair_gather", in_specs=[_ANY] * n, out_specs=[_ANY] * n,
        out_shape=[jax.ShapeDtypeStruct(f.shape, f.dtype) for f in fs],
        input_output_aliases={k: k for k in range(n)},
        scratch_shapes=[pltpu.SemaphoreType.DMA((n,)), pltpu.SemaphoreType.DMA((n,))],
    )(*fs)


def reduce_scatter_shards(gs, wire, cidx):
    n = len(gs)
    rs = rs_pair_swap(gs)
    groups = {}
    for k in range(n):
        groups.setdefault((gs[k].shape, jnp.dtype(wire[k]).name), []).append(k)
    ps = [None] * n
    for gi, ((_, _), ks) in enumerate(groups.items()):
        outs = rs_pair_add([gs[k] for k in ks], [rs[k] for k in ks], cidx, wire[ks[0]], "grad_pair_add_%d" % gi)
        for k, o in zip(ks, outs):
            ps[k] = o
    qs = rs_chip_scatter(ps)
    fs = [None] * n
    for gi, ((_, _), ks) in enumerate(groups.items()):
        outs = rs_chip_sum([qs[k] for k in ks], cidx, "grad_chip_sum_%d" % gi)
        for k, o in zip(ks, outs):
            fs[k] = o
    return rs_pair_gather(fs)


def _flat_view(shape):
    n = int(np.prod(shape))
    if n <= 2 ** 20 and n % LANES == 0:
        return (n // LANES, LANES)
    if n <= 2 ** 20 and n % 128 == 0:
        return (n // 128, 128)
    return (n // shape[-1], shape[-1])


def adamw(w, g, m, v, name):
    shape = w.shape
    rows, cols = _flat_view(shape)
    tr = rows
    if rows * cols * 4 > 2 ** 21:
        tr = max(t for t in range(8, rows, 8) if rows % t == 0 and t * cols * 4 <= 2 ** 21)

    def body(w_ref, g_ref, m_ref, v_ref, d_ref, nm_ref, nv_ref):
        g_ = g_ref[...]
        m_new = ADAM_B1 * m_ref[...] + (1.0 - ADAM_B1) * g_
        v_new = ADAM_B2 * v_ref[...] + (1.0 - ADAM_B2) * (g_ * g_)
        m_hat = m_new / (1.0 - ADAM_B1 ** ADAM_STEP)
        v_hat = v_new / (1.0 - ADAM_B2 ** ADAM_STEP)
        d_ref[...] = -ADAM_LR * (m_hat / (jnp.sqrt(v_hat) + ADAM_EPS) + ADAM_WD * w_ref[...])
        nm_ref[...] = m_new
        nv_ref[...] = v_new

    spec = pl.BlockSpec((tr, cols), lambda i: (i, 0))
    outs = pl.pallas_call(
        body, name=name, grid=(rows // tr,), in_specs=[spec] * 4, out_specs=[spec] * 3,
        out_shape=[jax.ShapeDtypeStruct((rows, cols), F32)] * 3, compiler_params=_cparams(),
    )(*[t.reshape(rows, cols) for t in (w, g, m, v)])
    return [o.reshape(shape) for o in outs]


_WEIGHTS = ['meta', 'ffn1_w_gate', 'ffn1_w_up', 'ffn1_w_down', 'ln1_g', 'ln1_b', 'w_in', 'mla_q_norm_g', 'mla_w_uq',
            'mla_kv_norm_g', 'mla_w_ukv', 'mla_w_o', 'conv_w', 'conv_b', 'conv_w_out', 's5_a_re', 's5_a_im', 's5_log_dt',
            's5_b_re', 's5_b_im', 's5_c_re', 's5_c_im', 's5_d', 's5_w_glu', 's5_b_glu', 's5_w_out', 'w_o', 'ln2_g', 'ln2_b',
            'ffn2_w_gate', 'ffn2_w_up', 'ffn2_w_down', 'ln3_g', 'ln3_b']


def _pad_to(flat, n):
    return jnp.concatenate([flat, jnp.zeros((n - flat.shape[0],), flat.dtype)])


def _shard_of(full, axis):
    if axis == 1:
        return _shard_cols(full)
    return full.reshape(N_SHARD, full.shape[0] // N_SHARD, full.shape[1])


_FFN_KEY = {'gate': 'wg', 'up': 'wu', 'down': 'wd'}


def _step(env):
    w = {n: env[n] for n in _WEIGHTS}
    mom = {n: env['m_' + n] for n in _WEIGHTS}
    var = {n: env['v_' + n] for n in _WEIGHTS}
    cidx = lax.axis_index("c").astype(jnp.int32).reshape(1)
    chip = 2 * lax.axis_index("x") + lax.axis_index("y")
    big_names = [n for n, _ in _BIG]
    nb = len(big_names)

    srcs = [w[n][li].astype(BF16) for li in range(DEPTH) for n in big_names]
    gathered, (conv_w_st, meta_st) = all_gather_shards(srcs, [w['conv_w'], w['meta']])
    meta_full = _nat_cols(meta_st)
    Ws = []
    for li in range(DEPTH):
        small = {n: w[n][li] for n in _REPL}
        small['conv_w'] = _nat_cols(conv_w_st[:, li])
        Ws.append(compute_weights({n: gathered[li * nb + i] for i, n in enumerate(big_names)}, small))

    loss_part, dx, dmeta, grads = local_step(env['x'][0], env['loss_target'][0], meta_full, Ws)
    loss = lax.psum(loss_part[0, 0], ("x", "y", "c"))
    full = [reference_grads(g, ffn=False) for g in grads]

    def shard(li, n, axis):
        if n.startswith('ffn'):
            return grads[li][_FFN_KEY[n.split('_')[-1]] + n[3]]
        return _shard_of(full[li][n], axis)

    gs = [shard(li, n, a) for li in range(DEPTH) for n, a in _BIG]
    s_parts = [jnp.stack([full[li][n] for li in range(DEPTH)]).reshape(-1) for n in _REPL + ['conv_w']] + [dmeta.reshape(-1)]
    s_sizes = [int(p.shape[0]) for p in s_parts]
    s_rows = -(-sum(s_sizes) // (16 * LANES)) * 16
    g_small = _pad_to(jnp.concatenate(s_parts), s_rows * LANES).reshape(1, s_rows, LANES)
    g_small = jnp.broadcast_to(g_small, (N_SHARD, s_rows, LANES))
    red = reduce_scatter_shards(gs + [g_small], [BF16] * len(gs) + [F32], cidx)
    f_small = red[-1].reshape(-1)

    grad = {n: jnp.stack([red[li * nb + i] for li in range(DEPTH)]) for i, n in enumerate(big_names)}
    off = 0
    for n, sz in zip(_REPL + ['conv_w', 'meta'], s_sizes):
        grad[n] = f_small[off:off + sz]
        off += sz
    for n in _REPL:
        grad[n] = grad[n].reshape(w[n].shape)
    cw = grad['conv_w'].reshape(DEPTH, 3, MIX)
    grad['conv_w'] = lax.dynamic_slice_in_dim(cw, chip * (MIX // N_SHARD), MIX // N_SHARD, axis=2)
    gm = grad['meta'].reshape(N_META, D_MODEL)
    grad['meta'] = lax.dynamic_slice_in_dim(gm, chip * (D_MODEL // N_SHARD), D_MODEL // N_SHARD, axis=1)

    delta, new_m, new_v = {}, {}, {}
    for n in _WEIGHTS:
        delta[n], new_m[n], new_v[n] = adamw(w[n], grad[n], mom[n], var[n], "adamw_" + n)
    return (loss, dx[None], *[grad[n] for n in _WEIGHTS], *[delta[n] for n in _WEIGHTS],
            *[new_m[n] for n in _WEIGHTS], *[new_v[n] for n in _WEIGHTS])


def kernel(x, meta, ffn1_w_gate, ffn1_w_up, ffn1_w_down, ln1_g, ln1_b, w_in, mla_q_norm_g, mla_w_uq, mla_kv_norm_g, mla_w_ukv, mla_w_o, conv_w, conv_b, conv_w_out, s5_a_re, s5_a_im, s5_log_dt, s5_b_re, s5_b_im, s5_c_re, s5_c_im, s5_d, s5_w_glu, s5_b_glu, s5_w_out, w_o, ln2_g, ln2_b, ffn2_w_gate, ffn2_w_up, ffn2_w_down, ln3_g, ln3_b, loss_target, m_meta, m_ffn1_w_gate, m_ffn1_w_up, m_ffn1_w_down, m_ln1_g, m_ln1_b, m_w_in, m_mla_q_norm_g, m_mla_w_uq, m_mla_kv_norm_g, m_mla_w_ukv, m_mla_w_o, m_conv_w, m_conv_b, m_conv_w_out, m_s5_a_re, m_s5_a_im, m_s5_log_dt, m_s5_b_re, m_s5_b_im, m_s5_c_re, m_s5_c_im, m_s5_d, m_s5_w_glu, m_s5_b_glu, m_s5_w_out, m_w_o, m_ln2_g, m_ln2_b, m_ffn2_w_gate, m_ffn2_w_up, m_ffn2_w_down, m_ln3_g, m_ln3_b, v_meta, v_ffn1_w_gate, v_ffn1_w_up, v_ffn1_w_down, v_ln1_g, v_ln1_b, v_w_in, v_mla_q_norm_g, v_mla_w_uq, v_mla_kv_norm_g, v_mla_w_ukv, v_mla_w_o, v_conv_w, v_conv_b, v_conv_w_out, v_s5_a_re, v_s5_a_im, v_s5_log_dt, v_s5_b_re, v_s5_b_im, v_s5_c_re, v_s5_c_im, v_s5_d, v_s5_w_glu, v_s5_b_glu, v_s5_w_out, v_w_o, v_ln2_g, v_ln2_b, v_ffn2_w_gate, v_ffn2_w_up, v_ffn2_w_down, v_ln3_g, v_ln3_b):
    return _step(dict(locals()))


def _unused_packed_step(env):
    w = {n: env[n] for n in _WEIGHTS}
    mom = {n: env['m_' + n] for n in _WEIGHTS}
    var = {n: env['v_' + n] for n in _WEIGHTS}
    cidx = lax.axis_index("c").astype(jnp.int32).reshape(1)
    big_names = [n for n, _ in _BIG]

    exact = jnp.concatenate([conv_w.reshape(-1), meta.reshape(-1)])
    parts = [w[n].astype(BF16).reshape(-1) for n in big_names] + [lax.bitcast_convert_type(exact, BF16).reshape(-1)]
    sizes = [int(p.shape[0]) for p in parts]
    hr = _half_rows(sum(sizes))
    packed = _pad_to(jnp.concatenate(parts), 2 * hr * LANES).reshape(2, hr, LANES)
    gathered = all_gather_halves(packed).reshape(N_SHARD, -1)
    st, off = {}, 0
    for n, sz in zip(big_names, sizes[:-1]):
        st[n] = gathered[:, off:off + sz].reshape((N_SHARD,) + w[n].shape)
        off += sz
    exact_all = lax.bitcast_convert_type(gathered[:, off:off + sizes[-1]].reshape(N_SHARD, -1, 2), F32)
    n_cw = conv_w.size
    conv_w_st = exact_all[:, :n_cw].reshape((N_SHARD,) + conv_w.shape)
    meta_full = _nat_cols(exact_all[:, n_cw:].reshape((N_SHARD,) + meta.shape))

    Ws = []
    for li in range(DEPTH):
        small = {n: w[n][li] for n in _REPL}
        small['conv_w'] = _nat_cols(conv_w_st[:, li])
        Ws.append(compute_weights({n: st[n][:, li] for n in big_names}, small))

    loss_part, dx, dmeta, grads = local_step(x[0], loss_target[0], meta_full, Ws)
    loss = lax.psum(loss_part[0, 0], ("x", "y", "c"))
    ref_grads = [shard_grads(g) for g in grads]

    def shard_stack(n):
        return jnp.stack([ref_grads[li][n] for li in range(DEPTH)], axis=1)

    g_parts = [shard_stack(n).reshape(N_SHARD, -1) for n, _ in _BIG]
    g_parts.append(shard_stack('conv_w').reshape(N_SHARD, -1))
    g_parts.append(_shard_cols(dmeta).reshape(N_SHARD, -1))
    g_sizes = [int(p.shape[1]) for p in g_parts]
    ghr = _half_rows(sum(g_sizes))
    g_big = jnp.concatenate(g_parts + [jnp.zeros((N_SHARD, 2 * ghr * LANES - sum(g_sizes)), F32)], axis=1)
    g_big = g_big.reshape(N_SHARD, 2, ghr, LANES)
    s_parts = [jnp.stack([ref_grads[li][n] for li in range(DEPTH)]).reshape(-1) for n in _REPL]
    s_sizes = [int(p.shape[0]) for p in s_parts]
    shr = _half_rows(sum(s_sizes))
    g_small = _pad_to(jnp.concatenate(s_parts), 2 * shr * LANES)
    g_small = jnp.broadcast_to(g_small.reshape(1, 2, shr, LANES), (N_SHARD, 2, shr, LANES))
    f_big, f_small = reduce_scatter(g_big, g_small, cidx)
    f_big, f_small = f_big.reshape(-1), f_small.reshape(-1)

    grad = {}
    off = 0
    for (n, _), sz in zip(_BIG + [('conv_w', 1), ('meta', 1)], g_sizes):
        grad[n] = f_big[off:off + sz].reshape(w[n].shape)
        off += sz
    off = 0
    for n, sz in zip(_REPL, s_sizes):
        grad[n] = f_small[off:off + sz].reshape(w[n].shape)
        off += sz

    delta, new_m, new_v = {}, {}, {}
    for n in _WEIGHTS:
        delta[n], new_m[n], new_v[n] = adamw(w[n], grad[n], mom[n], var[n], "adamw_" + n)
    return (loss, dx[None], *[grad[n] for n in _WEIGHTS], *[delta[n] for n in _WEIGHTS],
            *[new_m[n] for n in _WEIGHTS], *[new_v[n] for n in _WEIGHTS])
```

```python
import functools
import math

import numpy as np
import jax
import jax.numpy as jnp
from jax import lax
from jax.experimental import pallas as pl
from jax.experimental.pallas import tpu as pltpu

F32 = jnp.float32
BF16 = jnp.bfloat16

D_MODEL = 1024
DEPTH = 2
N_META = 16
PAD = 112
X0 = PAD + N_META
N_HEADS = 8
D_NOPE = 64
D_ROPE = 32
D_V = 64
Q_RANK = 384
KV_RANK = 256
MIX = 512
S5_GROUPS = 32
S5_GROUP = 16
S5_STATE = 64
S5_LANES = S5_GROUPS * S5_STATE
D_FF = 2816
N_SHARD = 4
FF_SHARD = D_FF // N_SHARD
D_IN = 5792
P_IN = 6144
ALPHA = (2.0 * DEPTH) ** 0.25
LN_EPS = 1e-5
RMS_EPS = 1e-6
ATT_SCALE = (D_NOPE + D_ROPE) ** -0.5
ROPE_BASE = 10000.0
ADAM_LR, ADAM_B1, ADAM_B2, ADAM_EPS, ADAM_WD, ADAM_STEP = 0.001, 0.9, 0.999, 1e-08, 0.01, 10
SCAN_CHUNK = 128
VMEM_LIMIT = 52 * 2 ** 20
WGRAD = BF16
MESH = pl.DeviceIdType.MESH


def _cparams(**kw):
    return pltpu.CompilerParams(vmem_limit_bytes=VMEM_LIMIT, **kw)


def _tile(n):
    if n <= 1088:
        return n
    for t in (1024, 544, 512, 272, 256, 128):
        if n % t == 0:
            return t
    return n


def _row_tile(lp):
    for t in (544, 272, 128):
        if lp % t == 0:
            return t
    return lp


def _sigmoid(x):
    return 1.0 / (1.0 + jnp.exp(-x))


_GELU_C = math.sqrt(2.0 / math.pi)


def _gelu(x):
    return 0.5 * x * (1.0 + jnp.tanh(_GELU_C * (x + 0.044715 * x * x * x)))


def _gelu_grad(x):
    t = jnp.tanh(_GELU_C * (x + 0.044715 * x * x * x))
    return 0.5 * (1.0 + t) + 0.5 * x * (1.0 - t * t) * _GELU_C * (1.0 + 3.0 * 0.044715 * x * x)


def _dot(a, b, ca, cb, precision=None):
    return lax.dot_general(a, b, (((ca,), (cb,)), ((), ())), preferred_element_type=F32, precision=precision)


def matmul(a, b, *, name, ta=False, tb=False, ab='n', bb='n', res=None, res_scale=1.0, scale=1.0, out_dtype=F32):
    if ta:
        _, K, M = a.shape
    else:
        _, M, K = a.shape
    if tb:
        _, N, K2 = b.shape
    else:
        _, K2, N = b.shape
    assert K == K2, (a.shape, b.shape)
    n_out = max(a.shape[0] if ab == 'o' else 1, b.shape[0] if bb == 'o' else 1)
    n_red = max(a.shape[0] if ab == 'r' else 1, b.shape[0] if bb == 'r' else 1)
    tm, tn = _tile(M), _tile(N)
    tk = K if K <= 2304 else _tile(K)
    nkt = K // tk
    n_steps = n_red * nkt

    def bsel(mode, o, r):
        if mode == 'o':
            return o
        if mode == 'r':
            return r // nkt if nkt > 1 else r
        return 0

    def ksel(r):
        if nkt == 1:
            return 0
        return r % nkt if n_red > 1 else r

    a_map = (lambda o, i, j, r: (bsel(ab, o, r), ksel(r), i)) if ta else (lambda o, i, j, r: (bsel(ab, o, r), i, ksel(r)))
    b_map = (lambda o, i, j, r: (bsel(bb, o, r), j, ksel(r))) if tb else (lambda o, i, j, r: (bsel(bb, o, r), ksel(r), j))
    o_map = lambda o, i, j, r: (o, i, j)
    in_specs = [pl.BlockSpec((None, tk, tm) if ta else (None, tm, tk), a_map),
                pl.BlockSpec((None, tn, tk) if tb else (None, tk, tn), b_map)]
    operands = [a, b]
    if res is not None:
        in_specs.append(pl.BlockSpec((None, tm, tn), o_map))
        operands.append(res)
    has_res = res is not None

    def body(*refs):
        a_ref, b_ref = refs[0], refs[1]
        res_ref = refs[2] if has_res else None
        o_ref = refs[3] if has_res else refs[2]
        part = _dot(a_ref[...].astype(BF16), b_ref[...].astype(BF16), 0 if ta else 1, 1 if tb else 0)

        def finish(acc):
            v = acc if scale == 1.0 else acc * scale
            if has_res:
                v = v + res_scale * res_ref[...].astype(F32)
            o_ref[...] = v.astype(o_ref.dtype)

        if n_steps == 1:
            finish(part)
        else:
            acc_ref = refs[-1]
            r = pl.program_id(3)

            @pl.when(r == 0)
            def _():
                acc_ref[...] = part

            @pl.when(r > 0)
            def _():
                acc_ref[...] += part

            @pl.when(r == n_steps - 1)
            def _():
                finish(acc_ref[...])

    return pl.pallas_call(
        body, name=name,
        grid=(n_out, M // tm, N // tn, n_steps),
        in_specs=in_specs,
        out_specs=pl.BlockSpec((None, tm, tn), o_map),
        out_shape=jax.ShapeDtypeStruct((n_out, M, N), out_dtype),
        scratch_shapes=[pltpu.VMEM((tm, tn), F32)] if n_steps > 1 else [],
        compiler_params=_cparams(),
    )(*operands)


def rowwise(fn, rows, pars, outs, accs=(), *, name, lp):
    tm = _row_tile(lp)
    n_rows, n_pars, n_outs, n_accs = len(rows), len(pars), len(outs), len(accs)
    in_specs = [pl.BlockSpec((tm, w), functools.partial(lambda i, cb: (i, cb), cb=cb)) for _, w, cb in rows]
    in_specs += [pl.BlockSpec(p.shape, functools.partial(lambda i, nd: (0,) * nd, nd=p.ndim)) for p in pars]
    out_specs = [pl.BlockSpec((tm, w), lambda i: (i, 0)) for w, _ in outs]
    out_specs += [pl.BlockSpec(s, functools.partial(lambda i, nd: (0,) * nd, nd=len(s))) for s, _ in accs]
    out_shape = [jax.ShapeDtypeStruct((lp, w), dt) for w, dt in outs]
    out_shape += [jax.ShapeDtypeStruct(s, dt) for s, dt in accs]

    def body(*refs):
        i = pl.program_id(0)
        rv = [r[...] for r in refs[:n_rows]]
        pv = [r[...] for r in refs[n_rows:n_rows + n_pars]]
        o_refs = refs[n_rows + n_pars:n_rows + n_pars + n_outs]
        a_refs = refs[n_rows + n_pars + n_outs:]
        ov, av = fn(i * tm, rv, pv)
        for r, v in zip(o_refs, ov):
            r[...] = v.astype(r.dtype)
        if n_accs:
            @pl.when(i == 0)
            def _():
                for r, v in zip(a_refs, av):
                    r[...] = v.astype(r.dtype)

            @pl.when(i > 0)
            def _():
                for r, v in zip(a_refs, av):
                    r[...] += v.astype(r.dtype)

    res = pl.pallas_call(
        body, name=name, grid=(lp // tm,), in_specs=in_specs, out_specs=out_specs, out_shape=out_shape,
        compiler_params=_cparams(),
    )(*[r[0] for r in rows], *pars)
    return res


def _row_mask(row0, shape):
    return (row0 + lax.broadcasted_iota(jnp.int32, shape, 0)) >= PAD


def ffn_up(hb, wg, wu, lp):
    tm = _row_tile(lp)

    def body(h_ref, wg_ref, wu_ref, ab_ref, hid_ref):
        h = h_ref[...]
        a = _dot(h, wg_ref[...], 1, 0)
        b = _dot(h, wu_ref[...], 1, 0)
        ab_ref[0] = a
        ab_ref[1] = b
        hid_ref[...] = (a * _sigmoid(a) * b).astype(BF16)

    wspec = pl.BlockSpec((None, D_MODEL, FF_SHARD), lambda j, i: (j, 0, 0))
    return pl.pallas_call(
        body, name="ffn_up", grid=(N_SHARD, lp // tm),
        in_specs=[pl.BlockSpec((tm, D_MODEL), lambda j, i: (i, 0)), wspec, wspec],
        out_specs=[pl.BlockSpec((None, 2, tm, FF_SHARD), lambda j, i: (j, 0, i, 0)),
                   pl.BlockSpec((None, tm, FF_SHARD), lambda j, i: (j, i, 0))],
        out_shape=[jax.ShapeDtypeStruct((N_SHARD, 2, lp, FF_SHARD), F32),
                   jax.ShapeDtypeStruct((N_SHARD, lp, FF_SHARD), BF16)],
        compiler_params=_cparams(),
    )(hb, wg, wu)


def _layer_norm(z, g, b):
    mu = jnp.mean(z, axis=-1, keepdims=True)
    zc = z - mu
    var = jnp.mean(zc * zc, axis=-1, keepdims=True)
    return zc * lax.rsqrt(var + LN_EPS) * g + b


def mm_res_ln(a, w, res, g, b, *, scale, name, lp):
    n_red, _, K = a.shape
    tm = _row_tile(lp)

    def body(a_ref, w_ref, res_ref, g_ref, b_ref, z_ref, h_ref, hb_ref, acc_ref):
        r = pl.program_id(1)
        part = _dot(a_ref[...].astype(BF16), w_ref[...], 1, 0)

        @pl.when(r == 0)
        def _():
            acc_ref[...] = part

        @pl.when(r > 0)
        def _():
            acc_ref[...] += part

        @pl.when(r == n_red - 1)
        def _():
            z = ALPHA * res_ref[...] + scale * acc_ref[...]
            z_ref[...] = z
            hn = _layer_norm(z, g_ref[...], b_ref[...])
            h_ref[...] = hn
            hb_ref[...] = hn.astype(BF16)

    row = pl.BlockSpec((tm, D_MODEL), lambda i, r: (i, 0))
    par = pl.BlockSpec((1, D_MODEL), lambda i, r: (0, 0))
    return pl.pallas_call(
        body, name=name, grid=(lp // tm, n_red),
        in_specs=[pl.BlockSpec((None, tm, K), lambda i, r: (r, i, 0)),
                  pl.BlockSpec((None, K, D_MODEL), lambda i, r: (r, 0, 0)), row, par, par],
        out_specs=[row, row, row],
        out_shape=[jax.ShapeDtypeStruct((lp, D_MODEL), F32), jax.ShapeDtypeStruct((lp, D_MODEL), F32),
                   jax.ShapeDtypeStruct((lp, D_MODEL), BF16)],
        scratch_shapes=[pltpu.VMEM((tm, D_MODEL), F32)],
        compiler_params=_cparams(),
    )(a, w, res, g, b)


def ln_bwd(dh, z, g, *, fscale, name, lp):
    def fn(row0, rv, pv):
        dh_, z_ = rv
        g_, = pv
        mu = jnp.mean(z_, axis=-1, keepdims=True)
        zc = z_ - mu
        rstd = lax.rsqrt(jnp.mean(zc * zc, axis=-1, keepdims=True) + LN_EPS)
        xh = zc * rstd
        dxh = dh_ * g_
        m1 = jnp.mean(dxh, axis=-1, keepdims=True)
        m2 = jnp.mean(dxh * xh, axis=-1, keepdims=True)
        dz = rstd * (dxh - m1 - xh * m2)
        return ((dz, fscale * dz),
                (jnp.sum(dh_ * xh, axis=0, keepdims=True), jnp.sum(dh_, axis=0, keepdims=True)))

    return rowwise(fn, [(dh, D_MODEL, 0), (z, D_MODEL, 0)], [g], [(D_MODEL, F32), (D_MODEL, BF16)],
                   [((1, D_MODEL), F32), ((1, D_MODEL), F32)], name=name, lp=lp)


def ffn_down_bwd(dfb, wd, ab, lp):
    tm = _row_tile(lp)

    def body(df_ref, w_ref, ab_ref, da_ref, db_ref):
        dhid = _dot(df_ref[...], w_ref[...], 1, 1)
        a = ab_ref[0]
        b = ab_ref[1]
        sg = _sigmoid(a)
        da_ref[...] = (dhid * b * (sg * (1.0 + a * (1.0 - sg)))).astype(BF16)
        db_ref[...] = (dhid * (a * sg)).astype(BF16)

    ospec = pl.BlockSpec((None, tm, FF_SHARD), lambda j, i: (j, i, 0))
    return pl.pallas_call(
        body, name="ffn_down_bwd", grid=(N_SHARD, lp // tm),
        in_specs=[pl.BlockSpec((tm, D_MODEL), lambda j, i: (i, 0)),
                  pl.BlockSpec((None, FF_SHARD, D_MODEL), lambda j, i: (j, 0, 0)),
                  pl.BlockSpec((None, 2, tm, FF_SHARD), lambda j, i: (j, 0, i, 0))],
        out_specs=[ospec, ospec],
        out_shape=[jax.ShapeDtypeStruct((N_SHARD, lp, FF_SHARD), BF16)] * 2,
        compiler_params=_cparams(),
    )(dfb, wd, ab)


def ffn_dx(da, db, wg, wu, dz, lp):
    tm = _row_tile(lp)

    def body(da_ref, db_ref, wg_ref, wu_ref, dz_ref, o_ref, acc_ref):
        j = pl.program_id(1)
        part = _dot(da_ref[...], wg_ref[...], 1, 1) + _dot(db_ref[...], wu_ref[...], 1, 1)

        @pl.when(j == 0)
        def _():
            acc_ref[...] = part

        @pl.when(j > 0)
        def _():
            acc_ref[...] += part

        @pl.when(j == N_SHARD - 1)
        def _():
            o_ref[...] = acc_ref[...] + ALPHA * dz_ref[...]

    aspec = pl.BlockSpec((None, tm, FF_SHARD), lambda i, j: (j, i, 0))
    wspec = pl.BlockSpec((None, D_MODEL, FF_SHARD), lambda i, j: (j, 0, 0))
    row = pl.BlockSpec((tm, D_MODEL), lambda i, j: (i, 0))
    return pl.pallas_call(
        body, name="ffn_dx", grid=(lp // tm, N_SHARD), in_specs=[aspec, aspec, wspec, wspec, row], out_specs=row,
        out_shape=jax.ShapeDtypeStruct((lp, D_MODEL), F32), scratch_shapes=[pltpu.VMEM((tm, D_MODEL), F32)],
        compiler_params=_cparams(),
    )(da, db, wg, wu, dz)


def ffn_fwd(h, hb, wg, wu, wd, g, b, lp):
    ab, hid = ffn_up(hb, wg, wu, lp)
    z, hn, hnb = mm_res_ln(hid, wd, h, g, b, scale=0.5, name="ffn_down_ln", lp=lp)
    return hn, hnb, dict(hb=hb, ab=ab, hid=hid, z=z)


def ffn_bwd(dh, sv, wg, wu, wd, g, lp):
    dz, dfb, dg, db = ln_bwd(dh, sv['z'], g, fscale=0.5, name="ffn_ln_bwd", lp=lp)
    da, dbb = ffn_down_bwd(dfb, wd, sv['ab'], lp)
    d_wd = matmul(sv['hid'], dfb[None], ta=True, ab='o', out_dtype=WGRAD, name="ffn_dwd")
    d_wg = matmul(sv['hb'][None], da, ta=True, bb='o', out_dtype=WGRAD, name="ffn_dwg")
    d_wu = matmul(sv['hb'][None], dbb, ta=True, bb='o', out_dtype=WGRAD, name="ffn_dwu")
    dh_in = ffn_dx(da, dbb, wg, wu, dz, lp)
    return dh_in, dict(wg=d_wg, wu=d_wu, wd=d_wd, ln_g=dg, ln_b=db)


def _rope_tables(lp):
    pos = np.arange(lp, dtype=np.float32) - PAD
    inv = ROPE_BASE ** (-np.arange(0, D_ROPE, 2, dtype=np.float32) / D_ROPE)
    ang = pos[:, None] * inv[None, :]
    cos = np.concatenate([np.cos(ang), np.cos(ang)], axis=1).astype(np.float32)
    sin = np.concatenate([np.sin(ang), np.sin(ang)], axis=1).astype(np.float32)
    rot = np.zeros((D_ROPE, D_ROPE), np.float32)
    half = D_ROPE // 2
    for j in range(half):
        rot[j + half, j] = -1.0
        rot[j, j + half] = 1.0
    return jnp.asarray(cos), jnp.asarray(sin), jnp.asarray(rot)


def _rot(x, rot):
    return _dot(x, rot, 1, 0, precision=lax.Precision.HIGHEST)


def _rms(x, g):
    r = lax.rsqrt(jnp.mean(x * x, axis=-1, keepdims=True) + RMS_EPS)
    return x * r * g


def mla_prep(proj, cos, sin, rot, qg, kvg, lp):
    def fn(row0, rv, pv):
        cq, krb, ckv, c, s = rv
        qg_, kvg_, rot_ = pv
        kr = krb[:, :D_ROPE]
        return ((_rms(cq, qg_), _rms(ckv, kvg_), kr * c + _rot(kr, rot_) * s), ())

    return rowwise(fn, [(proj, Q_RANK, 0), (proj, 128, 3), (proj, KV_RANK, 2), (cos, D_ROPE, 0), (sin, D_ROPE, 0)],
                   [qg, kvg, rot], [(Q_RANK, BF16), (KV_RANK, BF16), (D_ROPE, BF16)], name="mla_prep", lp=lp)


def mla_heads(cqn, ckvn, cos, sin, rot, wqn, wqr, wkn, wv, lp):
    tm = _row_tile(lp)

    def body(cq_ref, ckv_ref, c_ref, s_ref, rot_ref, wqn_ref, wqr_ref, wkn_ref, wv_ref, qn_ref, qr_ref, kn_ref, v_ref):
        cq = cq_ref[...]
        ckv = ckv_ref[...]
        qn_ref[...] = _dot(cq, wqn_ref[...], 1, 0).astype(BF16)
        qr = _dot(cq, wqr_ref[...], 1, 0)
        qr_ref[...] = (qr * c_ref[...] + _rot(qr, rot_ref[...]) * s_ref[...]).astype(BF16)
        kn_ref[...] = _dot(ckv, wkn_ref[...], 1, 0).astype(BF16)
        v_ref[...] = _dot(ckv, wv_ref[...], 1, 0).astype(BF16)

    def row(w):
        return pl.BlockSpec((tm, w), lambda h, i: (i, 0))

    def wspec(k, n):
        return pl.BlockSpec((None, k, n), lambda h, i: (h, 0, 0))

    def ospec(n):
        return pl.BlockSpec((None, tm, n), lambda h, i: (h, i, 0))

    return pl.pallas_call(
        body, name="mla_heads", grid=(N_HEADS, lp // tm),
        in_specs=[row(Q_RANK), row(KV_RANK), row(D_ROPE), row(D_ROPE),
                  pl.BlockSpec((D_ROPE, D_ROPE), lambda h, i: (0, 0)),
                  wspec(Q_RANK, D_NOPE), wspec(Q_RANK, D_ROPE), wspec(KV_RANK, D_NOPE), wspec(KV_RANK, D_V)],
        out_specs=[ospec(D_NOPE), ospec(D_ROPE), ospec(D_NOPE), ospec(D_V)],
        out_shape=[jax.ShapeDtypeStruct((N_HEADS, lp, D_NOPE), BF16), jax.ShapeDtypeStruct((N_HEADS, lp, D_ROPE), BF16),
                   jax.ShapeDtypeStruct((N_HEADS, lp, D_NOPE), BF16), jax.ShapeDtypeStruct((N_HEADS, lp, D_V), BF16)],
        compiler_params=_cparams(),
    )(cqn, ckvn, cos, sin, rot, wqn, wqr, wkn, wv)


def _att_probs(qn, qr, kn, kr, row0, tq, lp):
    s = (_dot(qn, kn, 1, 1) + _dot(qr, kr, 1, 1)) * ATT_SCALE
    qi = row0 + lax.broadcasted_iota(jnp.int32, (tq, lp), 0)
    ki = lax.broadcasted_iota(jnp.int32, (tq, lp), 1)
    s = jnp.where((ki <= qi) & (ki >= PAD), s, -1e30)
    p = jnp.exp(s - jnp.max(s, axis=-1, keepdims=True))
    return p / jnp.sum(p, axis=-1, keepdims=True)


def _att_specs(tq, lp):
    def qspec(n):
        return pl.BlockSpec((None, tq, n), lambda h, i: (h, i, 0))

    def kspec(n):
        return pl.BlockSpec((None, lp, n), lambda h, i: (h, 0, 0))

    return qspec, kspec, pl.BlockSpec((lp, D_ROPE), lambda h, i: (0, 0))


def _att_tile(lp):
    return 272 if lp % 272 == 0 else 128


def attn_fwd(qn, qr, kn, v, kr, lp):
    tq = _att_tile(lp)
    qspec, kspec, krspec = _att_specs(tq, lp)

    def body(qn_ref, qr_ref, kn_ref, v_ref, kr_ref, o_ref):
        i = pl.program_id(1)
        for k in range(lp // tq):
            @pl.when(i == k)
            def _(k=k):
                ke = (k + 1) * tq
                p = _att_probs(qn_ref[...], qr_ref[...], kn_ref[0:ke, :], kr_ref[0:ke, :], k * tq, tq, ke)
                o_ref[...] = _dot(p.astype(BF16), v_ref[0:ke, :], 1, 0).astype(BF16)

    return pl.pallas_call(
        body, name="attn_fwd", grid=(N_HEADS, lp // tq),
        in_specs=[qspec(D_NOPE), qspec(D_ROPE), kspec(D_NOPE), kspec(D_V), krspec],
        out_specs=qspec(D_V), out_shape=jax.ShapeDtypeStruct((N_HEADS, lp, D_V), BF16),
        compiler_params=_cparams(),
    )(qn, qr, kn, v, kr)


def attn_bwd(qn, qr, kn, v, kr, do, lp):
    tq = _att_tile(lp)
    qspec, kspec, krspec = _att_specs(tq, lp)

    def body(qn_ref, qr_ref, kn_ref, v_ref, kr_ref, do_ref, dqn_ref, dqr_ref, dkn_ref, dv_ref, dkr_ref):
        h, i = pl.program_id(0), pl.program_id(1)

        @pl.when(i == 0)
        def _():
            dkn_ref[...] = jnp.zeros_like(dkn_ref)
            dv_ref[...] = jnp.zeros_like(dv_ref)

        @pl.when((i == 0) & (h == 0))
        def _():
            dkr_ref[...] = jnp.zeros_like(dkr_ref)

        for k in range(lp // tq):
            @pl.when(i == k)
            def _(k=k):
                ke = (k + 1) * tq
                qn_, qr_, do_ = qn_ref[...], qr_ref[...], do_ref[...]
                kn_, v_, kr_ = kn_ref[0:ke, :], v_ref[0:ke, :], kr_ref[0:ke, :]
                p = _att_probs(qn_, qr_, kn_, kr_, k * tq, tq, ke)
                dp = _dot(do_, v_, 1, 1)
                delta = jnp.sum(p * dp, axis=-1, keepdims=True)
                ds = (p * (dp - delta) * ATT_SCALE).astype(BF16)
                dqn_ref[...] = _dot(ds, kn_, 1, 0).astype(BF16)
                dqr_ref[...] = _dot(ds, kr_, 1, 0)
                dkn_ref[0:ke, :] += _dot(ds, qn_, 0, 0)
                dv_ref[0:ke, :] += _dot(p.astype(BF16), do_, 0, 0)
                dkr_ref[0:ke, :] += _dot(ds, qr_, 0, 0)

    return pl.pallas_call(
        body, name="attn_bwd", grid=(N_HEADS, lp // tq),
        in_specs=[qspec(D_NOPE), qspec(D_ROPE), kspec(D_NOPE), kspec(D_V), krspec, qspec(D_V)],
        out_specs=[qspec(D_NOPE), qspec(D_ROPE), kspec(D_NOPE), kspec(D_V), krspec],
        out_shape=[jax.ShapeDtypeStruct((N_HEADS, lp, D_NOPE), BF16), jax.ShapeDtypeStruct((N_HEADS, lp, D_ROPE), F32),
                   jax.ShapeDtypeStruct((N_HEADS, lp, D_NOPE), F32), jax.ShapeDtypeStruct((N_HEADS, lp, D_V), F32),
                   jax.ShapeDtypeStruct((lp, D_ROPE), F32)],
        compiler_params=_cparams(),
    )(qn, qr, kn, v, kr, do)


def mla_heads_bwd(dqn, dqr, dkn, dv, cos, sin, rot, wqn, wqr, wkn, wv, lp):
    tm = _row_tile(lp)

    def body(dqn_ref, dqr_ref, dkn_ref, dv_ref, c_ref, s_ref, rot_ref, wqn_ref, wqr_ref, wkn_ref, wv_ref,
             dcq_ref, dckv_ref, dqrp_ref):
        h = pl.program_id(1)
        dqr_ = dqr_ref[...]
        dqrp = (dqr_ * c_ref[...] - _rot(dqr_ * s_ref[...], rot_ref[...])).astype(BF16)
        dqrp_ref[...] = dqrp
        dcq = _dot(dqn_ref[...], wqn_ref[...], 1, 1) + _dot(dqrp, wqr_ref[...], 1, 1)
        dckv = _dot(dkn_ref[...].astype(BF16), wkn_ref[...], 1, 1) + _dot(dv_ref[...].astype(BF16), wv_ref[...], 1, 1)

        @pl.when(h == 0)
        def _():
            dcq_ref[...] = dcq
            dckv_ref[...] = dckv

        @pl.when(h > 0)
        def _():
            dcq_ref[...] += dcq
            dckv_ref[...] += dckv

    def hspec(n):
        return pl.BlockSpec((None, tm, n), lambda i, h: (h, i, 0))

    def row(w):
        return pl.BlockSpec((tm, w), lambda i, h: (i, 0))

    def wspec(k, n):
        return pl.BlockSpec((None, k, n), lambda i, h: (h, 0, 0))

    return pl.pallas_call(
        body, name="mla_heads_bwd", grid=(lp // tm, N_HEADS),
        in_specs=[hspec(D_NOPE), hspec(D_ROPE), hspec(D_NOPE), hspec(D_V), row(D_ROPE), row(D_ROPE),
                  pl.BlockSpec((D_ROPE, D_ROPE), lambda i, h: (0, 0)),
                  wspec(Q_RANK, D_NOPE), wspec(Q_RANK, D_ROPE), wspec(KV_RANK, D_NOPE), wspec(KV_RANK, D_V)],
        out_specs=[row(Q_RANK), row(KV_RANK), hspec(D_ROPE)],
        out_shape=[jax.ShapeDtypeStruct((lp, Q_RANK), F32), jax.ShapeDtypeStruct((lp, KV_RANK), F32),
                   jax.ShapeDtypeStruct((N_HEADS, lp, D_ROPE), BF16)],
        compiler_params=_cparams(),
    )(dqn, dqr, dkn, dv, cos, sin, rot, wqn, wqr, wkn, wv)


def _rms_bwd(dy, x, g):
    r = lax.rsqrt(jnp.mean(x * x, axis=-1, keepdims=True) + RMS_EPS)
    n = x * r
    dn = dy * g
    dx = r * (dn - n * jnp.mean(dn * n, axis=-1, keepdims=True))
    return dx, jnp.sum(dy * n, axis=0, keepdims=True)


def mla_prep_bwd(dcq, dckv, dkr, proj, cos, sin, rot, qg, kvg, lp):
    def fn(row0, rv, pv):
        dcq_, dckv_, dkr_, cq, ckv, c, s = rv
        qg_, kvg_, rot_ = pv
        dxq, dgq = _rms_bwd(dcq_, cq, qg_)
        dxkv, dgkv = _rms_bwd(dckv_, ckv, kvg_)
        dkr_raw = dkr_ * c - _rot(dkr_ * s, rot_)
        return ((dxq, dxkv, dkr_raw), (dgq, dgkv))

    return rowwise(fn, [(dcq, Q_RANK, 0), (dckv, KV_RANK, 0), (dkr, D_ROPE, 0), (proj, Q_RANK, 0), (proj, KV_RANK, 2),
                        (cos, D_ROPE, 0), (sin, D_ROPE, 0)], [qg, kvg, rot],
                   [(Q_RANK, BF16), (KV_RANK, BF16), (D_ROPE, BF16)], [((1, Q_RANK), F32), ((1, KV_RANK), F32)],
                   name="mla_prep_bwd", lp=lp)


def _shift_down(x, d, rows):
    return jnp.where(rows >= d, pltpu.roll(x, d, 0), 0.0)


def _shift_up(x, d, rows, n):
    return jnp.where(rows < n - d, pltpu.roll(x, n - d, 0), 0.0)


_CONV_W = 128
_XB, _BG, _CG = 1024 // _CONV_W, 1536 // _CONV_W, 2048 // _CONV_W


def _conv_specs(lp):
    def pspec(base):
        return pl.BlockSpec((lp, _CONV_W), functools.partial(lambda c, base: (0, base + c), base=base))

    col = pl.BlockSpec((lp, _CONV_W), lambda c: (0, c))
    wspec = pl.BlockSpec((3, _CONV_W), lambda c: (0, c))
    bspec = pl.BlockSpec((1, _CONV_W), lambda c: (0, c))
    return pspec, col, wspec, bspec


def _conv_core(xbar, cg, w, bias, lp):
    rows = lax.broadcasted_iota(jnp.int32, (lp, _CONV_W), 0)
    u = jnp.where(rows >= PAD, cg * xbar, 0.0)
    u1 = _shift_down(u, 1, rows)
    u2 = _shift_down(u, 2, rows)
    y = bias + w[0:1] * u2 + w[1:2] * u1 + w[2:3] * u
    return rows, u, u1, u2, y


def conv_fwd(proj, w, bias, lp):
    pspec, col, wspec, bspec = _conv_specs(lp)

    def body(x_ref, b_ref, c_ref, w_ref, bias_ref, v_ref):
        _, _, _, _, y = _conv_core(x_ref[...], c_ref[...], w_ref[...], bias_ref[...], lp)
        v_ref[...] = (b_ref[...] * y).astype(BF16)

    return pl.pallas_call(
        body, name="conv_fwd", grid=(MIX // _CONV_W,),
        in_specs=[pspec(_XB), pspec(_BG), pspec(_CG), wspec, bspec], out_specs=col,
        out_shape=jax.ShapeDtypeStruct((lp, MIX), BF16), compiler_params=_cparams(),
    )(proj, proj, proj, w, bias)


def conv_bwd(dv, proj, w, bias, lp):
    pspec, col, wspec, bspec = _conv_specs(lp)

    def body(dv_ref, x_ref, b_ref, c_ref, w_ref, bias_ref, dx_ref, db_ref, dc_ref, dw_ref, dbias_ref):
        xbar, cg, w_ = x_ref[...], c_ref[...], w_ref[...]
        rows, u, u1, u2, y = _conv_core(xbar, cg, w_, bias_ref[...], lp)
        dv_ = dv_ref[...]
        db_ref[...] = (dv_ * y).astype(BF16)
        dy = dv_ * b_ref[...]
        dbias_ref[...] = jnp.sum(dy, axis=0, keepdims=True)
        dw_ref[0:1, :] = jnp.sum(dy * u2, axis=0, keepdims=True)
        dw_ref[1:2, :] = jnp.sum(dy * u1, axis=0, keepdims=True)
        dw_ref[2:3, :] = jnp.sum(dy * u, axis=0, keepdims=True)
        du = w_[2:3] * dy + w_[1:2] * _shift_up(dy, 1, rows, lp) + w_[0:1] * _shift_up(dy, 2, rows, lp)
        du = jnp.where(rows >= PAD, du, 0.0)
        dc_ref[...] = (du * xbar).astype(BF16)
        dx_ref[...] = (du * cg).astype(BF16)

    return pl.pallas_call(
        body, name="conv_bwd", grid=(MIX // _CONV_W,),
        in_specs=[col, pspec(_XB), pspec(_BG), pspec(_CG), wspec, bspec],
        out_specs=[col, col, col, wspec, bspec],
        out_shape=[jax.ShapeDtypeStruct((lp, MIX), BF16)] * 3 + [jax.ShapeDtypeStruct((3, MIX), F32),
                                                                jax.ShapeDtypeStruct((1, MIX), F32)],
        compiler_params=_cparams(),
    )(dv, proj, proj, proj, w, bias)


def _s5_disc(a_re, a_im, log_dt, b_re, b_im):
    dt = jnp.exp(log_dt)
    mag = jnp.exp(dt * a_re)
    ab_re, ab_im = mag * jnp.cos(dt * a_im), mag * jnp.sin(dt * a_im)
    den = a_re * a_re + a_im * a_im
    nr, ni = ab_re - 1.0, ab_im
    coef_re = (nr * a_re + ni * a_im) / den
    coef_im = (ni * a_re - nr * a_im) / den
    return ab_re, ab_im, coef_re * b_re - coef_im * b_im, coef_re * b_im + coef_im * b_re


_S5_ROWS = S5_GROUPS * S5_GROUP


def s5_prep(a_re, a_im, log_dt, b_re, b_im):
    def body(ar, ai, ld, br, bi, o0, o1, o2, o3):
        for o, v in zip((o0, o1, o2, o3), _s5_disc(ar[...], ai[...], ld[...], br[...], bi[...])):
            o[...] = v

    return pl.pallas_call(body, name="s5_prep",
                          out_shape=[jax.ShapeDtypeStruct((_S5_ROWS, S5_STATE), F32)] * 4)(a_re, a_im, log_dt, b_re, b_im)


def s5_prep_bwd(a_re, a_im, log_dt, b_re, b_im, d_ab_re, d_ab_im, d_bb_re, d_bb_im, sel):
    def body(ar, ai, ld, br, bi, g0, g1, g2, g3, sel_ref, da_re, da_im, dld, dbr, dbi):
        _, vjp = jax.vjp(_s5_disc, ar[...], ai[...], ld[...], br[...], bi[...])
        c_ar, c_ai, c_ld, c_br, c_bi = vjp((g0[...], g1[...], g2[...], g3[...]))
        s = sel_ref[...]
        hi = lax.Precision.HIGHEST
        da_re[...] = _dot(s, c_ar, 1, 0, precision=hi)
        da_im[...] = _dot(s, c_ai, 1, 0, precision=hi)
        dld[...] = jnp.sum(_dot(s, c_ld, 1, 0, precision=hi), axis=-1, keepdims=True)
        dbr[...] = c_br
        dbi[...] = c_bi

    g = jax.ShapeDtypeStruct((S5_GROUPS, S5_STATE), F32)
    full = jax.ShapeDtypeStruct((_S5_ROWS, S5_STATE), F32)
    return pl.pallas_call(body, name="s5_prep_bwd",
                          out_shape=[g, g, jax.ShapeDtypeStruct((S5_GROUPS, 1), F32), full, full],
                          )(a_re, a_im, log_dt, b_re, b_im, d_ab_re, d_ab_im, d_bb_re, d_bb_im, sel)


_SCAN_W = 128
_SCAN_STEPS = int(math.log2(SCAN_CHUNK))


def _cmul(ar, ai, br, bi):
    return ar * br - ai * bi, ar * bi + ai * br


def _scan_powers(ar, ai, reverse):
    pw = [(ar, ai)]
    for _ in range(_SCAN_STEPS):
        pw.append(_cmul(*pw[-1], *pw[-1]))
    rows = lax.broadcasted_iota(jnp.int32, (SCAN_CHUNK, ar.shape[-1]), 0)
    tr = jnp.broadcast_to(ar, rows.shape)
    ti = jnp.broadcast_to(ai, rows.shape)
    for k in range(_SCAN_STEPS):
        d = 2 ** k
        if reverse:
            live = rows < SCAN_CHUNK - d
            mr, mi = _cmul(tr, ti, _shift_up(tr, d, rows, SCAN_CHUNK), _shift_up(ti, d, rows, SCAN_CHUNK))
        else:
            live = rows >= d
            mr, mi = _cmul(tr, ti, _shift_down(tr, d, rows), _shift_down(ti, d, rows))
        tr = jnp.where(live, mr, tr)
        ti = jnp.where(live, mi, ti)
    return pw, rows, tr, ti


def s5_scan(bu, ab_re, ab_im, lp):
    n_chunks = lp // SCAN_CHUNK

    def body(bu_ref, ar_ref, ai_ref, s_ref):
        ar, ai = ar_ref[...], ai_ref[...]
        pw, rows, tr, ti = _scan_powers(ar, ai, False)

        def chunk(ci, carry):
            cr, cim = carry
            r0 = pl.multiple_of(ci * SCAN_CHUNK, SCAN_CHUNK)
            xr = bu_ref[0, pl.ds(r0, SCAN_CHUNK), :]
            xi = bu_ref[1, pl.ds(r0, SCAN_CHUNK), :]
            for k in range(_SCAN_STEPS):
                d = 2 ** k
                mr, mi = _cmul(pw[k][0], pw[k][1], _shift_down(xr, d, rows), _shift_down(xi, d, rows))
                xr, xi = xr + mr, xi + mi
            mr, mi = _cmul(tr, ti, cr, cim)
            xr, xi = xr + mr, xi + mi
            s_ref[0, pl.ds(r0, SCAN_CHUNK), :] = xr
            s_ref[1, pl.ds(r0, SCAN_CHUNK), :] = xi
            return xr[SCAN_CHUNK - 1:SCAN_CHUNK, :], xi[SCAN_CHUNK - 1:SCAN_CHUNK, :]

        zero = jnp.zeros((1, _SCAN_W), F32)
        lax.fori_loop(0, n_chunks, chunk, (zero, zero))

    spec = pl.BlockSpec((2, lp, _SCAN_W), lambda c: (0, 0, c))
    aspec = pl.BlockSpec((1, _SCAN_W), lambda c: (0, c))
    return pl.pallas_call(
        body, name="s5_scan", grid=(S5_LANES // _SCAN_W,), in_specs=[spec, aspec, aspec], out_specs=spec,
        out_shape=jax.ShapeDtypeStruct((2, lp, S5_LANES), F32), compiler_params=_cparams(),
    )(bu, ab_re, ab_im)


def s5_scan_bwd(ds, s, ab_re, ab_im, lp):
    n_chunks = lp // SCAN_CHUNK

    def body(ds_ref, s_ref, ar_ref, ai_ref, g_ref, da_ref):
        ar, ai = ar_ref[...], -ai_ref[...]
        pw, rows, tr, ti = _scan_powers(ar, ai, True)

        def chunk(k, carry):
            cr, cim, dar, dai = carry
            ci = n_chunks - 1 - k
            r0 = pl.multiple_of(ci * SCAN_CHUNK, SCAN_CHUNK)
            xr = ds_ref[0, pl.ds(r0, SCAN_CHUNK), :]
            xi = ds_ref[1, pl.ds(r0, SCAN_CHUNK), :]
            for j in range(_SCAN_STEPS):
                d = 2 ** j
                mr, mi = _cmul(pw[j][0], pw[j][1], _shift_up(xr, d, rows, SCAN_CHUNK), _shift_up(xi, d, rows, SCAN_CHUNK))
                xr, xi = xr + mr, xi + mi
            mr, mi = _cmul(tr, ti, cr, cim)
            xr, xi = xr + mr, xi + mi
            g_ref[0, pl.ds(r0, SCAN_CHUNK), :] = xr
            g_ref[1, pl.ds(r0, SCAN_CHUNK), :] = xi
            prev0 = pl.multiple_of(jnp.maximum(r0 - 8, 0), 8)
            live = (ci > 0).astype(F32)
            pr = s_ref[0, pl.ds(prev0, 8), :][7:8, :] * live
            pim = s_ref[1, pl.ds(prev0, 8), :][7:8, :] * live
            sr = s_ref[0, pl.ds(r0, SCAN_CHUNK), :]
            si = s_ref[1, pl.ds(r0, SCAN_CHUNK), :]
            sr = jnp.where(rows >= 1, pltpu.roll(sr, 1, 0), pr)
            si = jnp.where(rows >= 1, pltpu.roll(si, 1, 0), pim)
            dar = dar + jnp.sum(xr * sr + xi * si, axis=0, keepdims=True)
            dai = dai + jnp.sum(xi * sr - xr * si, axis=0, keepdims=True)
            return xr[0:1, :], xi[0:1, :], dar, dai

        zero = jnp.zeros((1, _SCAN_W), F32)
        _, _, dar, dai = lax.fori_loop(0, n_chunks, chunk, (zero, zero, zero, zero))
        da_ref[0] = dar
        da_ref[1] = dai

    spec = pl.BlockSpec((2, lp, _SCAN_W), lambda c: (0, 0, c))
    aspec = pl.BlockSpec((1, _SCAN_W), lambda c: (0, c))
    return pl.pallas_call(
        body, name="s5_scan_bwd", grid=(S5_LANES // _SCAN_W,), in_specs=[spec, spec, aspec, aspec],
        out_specs=[spec, pl.BlockSpec((2, 1, _SCAN_W), lambda c: (0, 0, c))],
        out_shape=[jax.ShapeDtypeStruct((2, lp, S5_LANES), F32), jax.ShapeDtypeStruct((2, 1, S5_LANES), F32)],
        compiler_params=_cparams(),
    )(ds, s, ab_re, ab_im)


def _blockdiag(x):
    g, r, c = x.shape
    eye = jnp.eye(g, dtype=x.dtype)
    return (x[:, :, None, :] * eye[:, None, :, None]).reshape(g * r, g * c)


def _blockdiag_extract(m, g, r, c):
    return jnp.einsum('grgc->grc', m.reshape(g, r, g, c))


def s5_u(proj, lp):
    def fn(row0, rv, pv):
        u, = rv
        return ((jnp.where(_row_mask(row0, u.shape), u, 0.0),), ())

    return rowwise(fn, [(proj, MIX, 5)], [], [(MIX, BF16)], name="s5_u", lp=lp)[0]


def s5_y(ys, proj, d, lp):
    def fn(row0, rv, pv):
        ys_, u = rv
        y = ys_ + pv[0] * u
        return ((y, _gelu(y)), ())

    return rowwise(fn, [(ys, MIX, 0), (proj, MIX, 5)], [d], [(MIX, F32), (MIX, BF16)], name="s5_y", lp=lp)


def s5_glu(z, y, b, lp):
    def fn(row0, rv, pv):
        z_, y_ = rv
        return ((_gelu(y_) * _sigmoid(z_ + pv[0]),), ())

    return rowwise(fn, [(z, MIX, 0), (y, MIX, 0)], [b], [(MIX, BF16)], name="s5_glu", lp=lp)[0]


def s5_glu_bwd(dgl, z, y, b, lp):
    def fn(row0, rv, pv):
        dgl_, z_, y_ = rv
        sg = _sigmoid(z_ + pv[0])
        dz = dgl_ * _gelu(y_) * sg * (1.0 - sg)
        return ((dgl_ * sg, dz), (jnp.sum(dz, axis=0, keepdims=True),))

    return rowwise(fn, [(dgl, MIX, 0), (z, MIX, 0), (y, MIX, 0)], [b], [(MIX, F32), (MIX, BF16)], [((1, MIX), F32)],
                   name="s5_glu_bwd", lp=lp)


def s5_y_bwd(dyg, y, proj, d, lp):
    def fn(row0, rv, pv):
        dyg_, y_, u = rv
        dy = dyg_ * _gelu_grad(y_)
        return ((dy, dy * pv[0]), (jnp.sum(dy * u, axis=0, keepdims=True),))

    return rowwise(fn, [(dyg, MIX, 0), (y, MIX, 0), (proj, MIX, 5)], [d], [(MIX, BF16), (MIX, F32)], [((1, MIX), F32)],
                   name="s5_y_bwd", lp=lp)


def s5_du(du, lp):
    def fn(row0, rv, pv):
        return ((jnp.where(_row_mask(row0, rv[0].shape), rv[0], 0.0),), ())

    return rowwise(fn, [(du, MIX, 0)], [], [(MIX, BF16)], name="s5_du", lp=lp)[0]


def merge_fwd(proj, ya, yb, yc, lp):
    def fn(row0, rv, pv):
        g0, g1, g2, a, b, c = rv
        return ((_sigmoid(g0) * a + _sigmoid(g1) * b + _sigmoid(g2) * c,), ())

    return rowwise(fn, [(proj, D_MODEL, 3), (proj, D_MODEL, 4), (proj, D_MODEL, 5), (ya, D_MODEL, 0), (yb, D_MODEL, 0),
                        (yc, D_MODEL, 0)], [], [(D_MODEL, BF16)], name="merge_fwd", lp=lp)[0]


def merge_bwd(dmix, proj, ya, yb, yc, lp):
    def fn(row0, rv, pv):
        dm, g0, g1, g2, a, b, c = rv
        outs_y, outs_g = [], []
        for g, yv in ((g0, a), (g1, b), (g2, c)):
            sg = _sigmoid(g)
            outs_y.append(dm * sg)
            outs_g.append(dm * yv * sg * (1.0 - sg))
        return (tuple(outs_y) + tuple(outs_g), ())

    return rowwise(fn, [(dmix, D_MODEL, 0), (proj, D_MODEL, 3), (proj, D_MODEL, 4), (proj, D_MODEL, 5),
                        (ya, D_MODEL, 0), (yb, D_MODEL, 0), (yc, D_MODEL, 0)], [], [(D_MODEL, BF16)] * 6,
                   name="merge_bwd", lp=lp)


def loss_head(h, tgt, lp):
    def fn(row0, rv, pv):
        h_, t_ = rv
        live = (row0 + lax.broadcasted_iota(jnp.int32, h_.shape, 0)) >= X0
        diff = jnp.where(live, h_ - t_, 0.0)
        ssq = jnp.sum(jnp.sum(diff * diff, axis=1, keepdims=True), axis=0, keepdims=True)
        return ((diff * (1.0 / D_MODEL),), (ssq * (0.5 / D_MODEL),))

    return rowwise(fn, [(h, D_MODEL, 0), (tgt, D_MODEL, 0)], [], [(D_MODEL, F32)], [((1, 1), F32)], name="loss_head", lp=lp)


def _s5_consts(W):
    ab_re_rep, ab_im_rep, bb_re, bb_im = s5_prep(W['s5_a_re'], W['s5_a_im'], W['s5_log_dt'], W['s5_b_re'], W['s5_b_im'])
    pick = lambda t: t.reshape(S5_GROUPS, S5_GROUP, S5_STATE)[:, 0].reshape(1, S5_LANES)
    bb = jnp.stack([_blockdiag(bb_re.reshape(S5_GROUPS, S5_GROUP, S5_STATE)),
                    _blockdiag(bb_im.reshape(S5_GROUPS, S5_GROUP, S5_STATE))]).astype(BF16)
    return pick(ab_re_rep), pick(ab_im_rep), bb


def layer_fwd(h, hb, W, tabs, lp):
    cos, sin, rot = tabs
    h1, h1b, sv1 = ffn_fwd(h, hb, W['wg1'], W['wu1'], W['wd1'], W['ln1_g'], W['ln1_b'], lp)
    proj = matmul(h1b[None], W['w_in'][None], name="proj")[0]
    cqn, ckvn, kr = mla_prep(proj, cos, sin, rot, W['q_norm_g'], W['kv_norm_g'], lp)
    qn, qr, kn, v = mla_heads(cqn, ckvn, cos, sin, rot, W['wqn'], W['wqr'], W['wkn'], W['wv'], lp)
    o = attn_fwd(qn, qr, kn, v, kr, lp)
    ya = matmul(o, W['mla_wo'], ab='r', bb='r', name="mla_out")[0]
    vconv = conv_fwd(proj, W['conv_w'], W['conv_b'], lp)
    yb = matmul(vconv[None], W['conv_wout'][None], name="conv_out")[0]
    ub = s5_u(proj, lp)
    ab_re, ab_im, bb = _s5_consts(W)
    bu = matmul(ub[None], bb, bb='o', name="s5_bu")
    s = s5_scan(bu, ab_re, ab_im, lp)
    ys = matmul(s, W['s5_ct'], ab='r', bb='r', name="s5_cs")[0]
    y, ygb = s5_y(ys, proj, W['s5_d'], lp)
    zg = matmul(ygb[None], W['s5_wglu'][None], name="s5_glu_mm")[0]
    glb = s5_glu(zg, y, W['s5_b_glu'], lp)
    yc = matmul(glb[None], W['s5_wout'][None], name="s5_out")[0]
    mixed = merge_fwd(proj, ya, yb, yc, lp)
    z2, h2, h2b = mm_res_ln(mixed[None], W['w_o'][None], h1, W['ln2_g'], W['ln2_b'], scale=1.0, name="wo_ln", lp=lp)
    h3, h3b, sv3 = ffn_fwd(h2, h2b, W['wg2'], W['wu2'], W['wd2'], W['ln3_g'], W['ln3_b'], lp)
    sv = dict(sv1=sv1, sv3=sv3, h1b=h1b, proj=proj, cqn=cqn, ckvn=ckvn, kr=kr, qn=qn, qr=qr, kn=kn, v=v, o=o, ya=ya,
              vconv=vconv, yb=yb, ub=ub, ab_re=ab_re, ab_im=ab_im, bb=bb, s=s, y=y, ygb=ygb, zg=zg, glb=glb, yc=yc,
              mixed=mixed, z2=z2)
    return h3, h3b, sv


def layer_bwd(dh3, sv, W, tabs, lp):
    cos, sin, rot = tabs
    proj = sv['proj']
    G = {}
    dh2, g3 = ffn_bwd(dh3, sv['sv3'], W['wg2'], W['wu2'], W['wd2'], W['ln3_g'], lp)
    G.update(wg2=g3['wg'], wu2=g3['wu'], wd2=g3['wd'], ln3_g=g3['ln_g'], ln3_b=g3['ln_b'])
    dz2, dz2b, G['ln2_g'], G['ln2_b'] = ln_bwd(dh2, sv['z2'], W['ln2_g'], fscale=1.0, name="wo_ln_bwd", lp=lp)
    dmix = matmul(dz2b[None], W['w_o'][None], tb=True, name="wo_dx")[0]
    G['w_o'] = matmul(sv['mixed'][None], dz2b[None], ta=True, out_dtype=WGRAD, name="wo_dw")[0]
    dya, dyb, dyc, dg0, dg1, dg2 = merge_bwd(dmix, proj, sv['ya'], sv['yb'], sv['yc'], lp)
    dgl = matmul(dyc[None], W['s5_wout'][None], tb=True, name="s5_out_dx")[0]
    G['s5_wout'] = matmul(sv['glb'][None], dyc[None], ta=True, out_dtype=WGRAD, name="s5_out_dw")[0]
    t1, dzb, G['s5_b_glu'] = s5_glu_bwd(dgl, sv['zg'], sv['y'], W['s5_b_glu'], lp)
    dyg = matmul(dzb[None], W['s5_wglu'][None], tb=True, res=t1[None], name="s5_glu_dx")[0]
    G['s5_wglu'] = matmul(sv['ygb'][None], dzb[None], ta=True, out_dtype=WGRAD, name="s5_glu_dw")[0]
    dyb_, du_d, G['s5_d'] = s5_y_bwd(dyg, sv['y'], proj, W['s5_d'], lp)
    ds = matmul(dyb_[None], W['s5_ct'], tb=True, bb='o', name="s5_cs_dx")
    G['s5_ct'] = matmul(sv['s'], dyb_[None], ta=True, ab='o', name="s5_cs_dw")
    g_adj, d_ab = s5_scan_bwd(ds, sv['s'], sv['ab_re'], sv['ab_im'], lp)
    du = matmul(g_adj, sv['bb'], tb=True, ab='r', bb='r', res=du_d[None], name="s5_bu_dx")[0]
    d_bb = matmul(sv['ub'][None], g_adj, ta=True, bb='o', name="s5_bu_dw")
    du_b = s5_du(du, lp)
    onehot = (jnp.arange(S5_GROUP) == 0).astype(F32)
    spread = lambda t: (t.reshape(S5_GROUPS, 1, S5_STATE) * onehot[None, :, None]).reshape(_S5_ROWS, S5_STATE)
    take = lambda t: _blockdiag_extract(t, S5_GROUPS, S5_GROUP, S5_STATE).reshape(_S5_ROWS, S5_STATE)
    sel = jnp.kron(jnp.eye(S5_GROUPS, dtype=F32), jnp.ones((1, S5_GROUP), F32))
    (G['s5_a_re'], G['s5_a_im'], G['s5_log_dt'], G['s5_b_re'], G['s5_b_im']) = s5_prep_bwd(
        W['s5_a_re'], W['s5_a_im'], W['s5_log_dt'], W['s5_b_re'], W['s5_b_im'],
        spread(d_ab[0]), spread(d_ab[1]), take(d_bb[0]), take(d_bb[1]), sel)
    dv = matmul(dyb[None], W['conv_wout'][None], tb=True, name="conv_out_dx")[0]
    G['conv_wout'] = matmul(sv['vconv'][None], dyb[None], ta=True, out_dtype=WGRAD, name="conv_out_dw")[0]
    dxbar, dbg, dcg, G['conv_w'], G['conv_b'] = conv_bwd(dv, proj, W['conv_w'], W['conv_b'], lp)
    do = matmul(dya[None], W['mla_wo'], tb=True, bb='o', out_dtype=BF16, name="mla_out_dx")
    G['mla_wo'] = matmul(sv['o'], dya[None], ta=True, ab='o', out_dtype=WGRAD, name="mla_out_dw")
    dqn, dqr, dkn, dvv, dkr = attn_bwd(sv['qn'], sv['qr'], sv['kn'], sv['v'], sv['kr'], do, lp)
    dcq, dckv, dqrp = mla_heads_bwd(dqn, dqr, dkn, dvv, cos, sin, rot, W['wqn'], W['wqr'], W['wkn'], W['wv'], lp)
    G['wqn'] = matmul(sv['cqn'][None], dqn, ta=True, bb='o', out_dtype=WGRAD, name="mla_dwqn")
    G['wqr'] = matmul(sv['cqn'][None], dqrp, ta=True, bb='o', out_dtype=WGRAD, name="mla_dwqr")
    G['wkn'] = matmul(sv['ckvn'][None], dkn, ta=True, bb='o', out_dtype=WGRAD, name="mla_dwkn")
    G['wv'] = matmul(sv['ckvn'][None], dvv, ta=True, bb='o', out_dtype=WGRAD, name="mla_dwv")
    dcq_raw, dckv_raw, dkr_raw, G['q_norm_g'], G['kv_norm_g'] = mla_prep_bwd(
        dcq, dckv, dkr, proj, cos, sin, rot, W['q_norm_g'], W['kv_norm_g'], lp)
    zeros = lambda n: jnp.zeros((lp, n), BF16)
    dproj = jnp.concatenate([dcq_raw, dkr_raw, zeros(96), dckv_raw, zeros(256), dxbar, dbg, dcg, du_b, dg0, dg1, dg2], axis=1)
    dh1 = matmul(dproj[None], W['w_in'][None], tb=True, res=dz2[None], res_scale=ALPHA, name="proj_dx")[0]
    G['w_in'] = matmul(sv['h1b'][None], dproj[None], ta=True, out_dtype=WGRAD, name="proj_dw")[0]
    dh0, g1 = ffn_bwd(dh1, sv['sv1'], W['wg1'], W['wu1'], W['wd1'], W['ln1_g'], lp)
    G.update(wg1=g1['wg'], wu1=g1['wu'], wd1=g1['wd'], ln1_g=g1['ln_g'], ln1_b=g1['ln_b'])
    return dh0, G


def _nat_cols(st):
    return jnp.transpose(st, (1, 0, 2)).reshape(st.shape[1], -1)


def _shard_cols(nat):
    k, n = nat.shape
    return jnp.transpose(nat.reshape(k, N_SHARD, n // N_SHARD), (1, 0, 2))


def _win_pad(w):
    z = lambda n: jnp.zeros((w.shape[0], n), w.dtype)
    return jnp.concatenate([w[:, 0:384], w[:, 640:672], z(96), w[:, 384:640], z(256), w[:, 672:]], axis=1)


def _win_unpad(wp):
    return jnp.concatenate([wp[:, 0:384], wp[:, 512:768], wp[:, 384:416], wp[:, 1024:]], axis=1)


_BIG = [('ffn1_w_gate', 1), ('ffn1_w_up', 1), ('ffn1_w_down', 0), ('w_in', 1), ('mla_w_uq', 1), ('mla_w_ukv', 1),
        ('mla_w_o', 1), ('conv_w_out', 1), ('s5_w_glu', 0), ('s5_w_out', 1), ('w_o', 0),
        ('ffn2_w_gate', 1), ('ffn2_w_up', 1), ('ffn2_w_down', 0)]
_REPL = ['ln1_g', 'ln1_b', 'mla_q_norm_g', 'mla_kv_norm_g', 'conv_b', 's5_a_re', 's5_a_im', 's5_log_dt', 's5_b_re',
         's5_b_im', 's5_c_re', 's5_c_im', 's5_d', 's5_b_glu', 'ln2_g', 'ln2_b', 'ln3_g', 'ln3_b']


def compute_weights(st, small):
    W = {}
    for t in ('1', '2'):
        W['wg' + t], W['wu' + t] = st['ffn%s_w_gate' % t], st['ffn%s_w_up' % t]
        W['wd' + t] = st['ffn%s_w_down' % t]
    W['w_in'] = _win_pad(_nat_cols(st['w_in']))
    uq = jnp.transpose(_nat_cols(st['mla_w_uq']).reshape(Q_RANK, N_HEADS, D_NOPE + D_ROPE), (1, 0, 2))
    W['wqn'], W['wqr'] = uq[:, :, :D_NOPE], uq[:, :, D_NOPE:]
    ukv = jnp.transpose(_nat_cols(st['mla_w_ukv']).reshape(KV_RANK, N_HEADS, D_NOPE + D_V), (1, 0, 2))
    W['wkn'], W['wv'] = ukv[:, :, :D_NOPE], ukv[:, :, D_NOPE:]
    W['mla_wo'] = _nat_cols(st['mla_w_o']).reshape(N_HEADS, D_V, D_MODEL)
    W['conv_wout'] = _nat_cols(st['conv_w_out'])
    W['s5_wglu'] = st['s5_w_glu'].reshape(MIX, MIX)
    W['s5_wout'] = _nat_cols(st['s5_w_out'])
    W['w_o'] = st['w_o'].reshape(D_MODEL, D_MODEL)
    W['conv_w'] = small['conv_w']
    for n in ('ln1_g', 'ln1_b', 'ln2_g', 'ln2_b', 'ln3_g', 'ln3_b', 'conv_b', 's5_b_glu'):
        W[n] = small[n].reshape(1, -1)
    W['q_norm_g'] = small['mla_q_norm_g'].reshape(1, -1)
    W['kv_norm_g'] = small['mla_kv_norm_g'].reshape(1, -1)
    W['s5_d'] = small['s5_d'].reshape(1, MIX)
    rep = lambda t: jnp.repeat(t, S5_GROUP, axis=0)
    W['s5_a_re'], W['s5_a_im'] = rep(small['s5_a_re']), rep(small['s5_a_im'])
    W['s5_log_dt'] = jnp.broadcast_to(rep(small['s5_log_dt'].reshape(S5_GROUPS, 1)), (_S5_ROWS, S5_STATE))
    tr = lambda t: jnp.transpose(t, (0, 2, 1)).reshape(_S5_ROWS, S5_STATE)
    W['s5_b_re'], W['s5_b_im'] = tr(small['s5_b_re']), tr(small['s5_b_im'])
    ct = lambda t: _blockdiag(jnp.transpose(t, (0, 2, 1)))
    W['s5_ct'] = jnp.stack([ct(small['s5_c_re']), -ct(small['s5_c_im'])]).astype(BF16)
    return W


def shard_grads(G):
    R = reference_grads(G, ffn=False)
    S = {n: (_shard_of(R[n], a) if n in R else None) for n, a in _BIG}
    for t in ('1', '2'):
        S['ffn%s_w_gate' % t] = G['wg' + t]
        S['ffn%s_w_up' % t] = G['wu' + t]
        S['ffn%s_w_down' % t] = G['wd' + t]
    S['conv_w'] = _shard_cols(R['conv_w'])
    for n in _REPL:
        S[n] = R[n]
    return S


def reference_grads(G, ffn=True):
    R = {}
    for t in ('1', '2') if ffn else ():
        R['ffn%s_w_gate' % t] = _nat_cols(G['wg' + t])
        R['ffn%s_w_up' % t] = _nat_cols(G['wu' + t])
        R['ffn%s_w_down' % t] = G['wd' + t].reshape(D_FF, D_MODEL)
    R['w_in'] = _win_unpad(G['w_in'])
    R['mla_w_uq'] = jnp.transpose(jnp.concatenate([G['wqn'], G['wqr']], axis=2), (1, 0, 2)).reshape(Q_RANK, -1)
    R['mla_w_ukv'] = jnp.transpose(jnp.concatenate([G['wkn'], G['wv']], axis=2), (1, 0, 2)).reshape(KV_RANK, -1)
    R['mla_w_o'] = G['mla_wo'].reshape(N_HEADS * D_V, D_MODEL)
    R['conv_w'], R['conv_w_out'] = G['conv_w'], G['conv_wout']
    R['s5_w_glu'], R['s5_w_out'], R['w_o'] = G['s5_wglu'], G['s5_wout'], G['w_o']
    for n in ('ln1_g', 'ln1_b', 'ln2_g', 'ln2_b', 'ln3_g', 'ln3_b', 'conv_b', 's5_b_glu'):
        R[n] = G[n].reshape(-1)
    R['mla_q_norm_g'], R['mla_kv_norm_g'] = G['q_norm_g'].reshape(-1), G['kv_norm_g'].reshape(-1)
    R['s5_d'] = G['s5_d'].reshape(S5_GROUPS, S5_GROUP)
    R['s5_a_re'], R['s5_a_im'], R['s5_log_dt'] = G['s5_a_re'], G['s5_a_im'], G['s5_log_dt'].reshape(-1)
    untr = lambda t: jnp.transpose(t.reshape(S5_GROUPS, S5_GROUP, S5_STATE), (0, 2, 1))
    R['s5_b_re'], R['s5_b_im'] = untr(G['s5_b_re']), untr(G['s5_b_im'])
    unct = lambda t: jnp.transpose(_blockdiag_extract(t, S5_GROUPS, S5_STATE, S5_GROUP), (0, 2, 1))
    R['s5_c_re'], R['s5_c_im'] = unct(G['s5_ct'][0]), -unct(G['s5_ct'][1])
    return R


def local_step(x2d, tgt2d, meta, Ws):
    lp = x2d.shape[0] + X0
    tabs = _rope_tables(lp)
    h = jnp.concatenate([jnp.zeros((PAD, D_MODEL), F32), meta, x2d], axis=0)
    hb = h.astype(BF16)
    saved = []
    for W in Ws:
        h, hb, sv = layer_fwd(h, hb, W, tabs, lp)
        saved.append(sv)
    tgt = jnp.concatenate([jnp.zeros((X0, D_MODEL), F32), tgt2d], axis=0)
    dh, loss = loss_head(h, tgt, lp)
    grads = [None] * len(Ws)
    for li in reversed(range(len(Ws))):
        dh, grads[li] = layer_bwd(dh, saved[li], Ws[li], tabs, lp)
    return loss, dh[X0:], dh[PAD:X0], grads


_ANY = pl.BlockSpec(memory_space=pl.ANY)
LANES = 1024


def _half_rows(n):
    rows = -(-n // (2 * LANES))
    q = 512 if rows > 512 else 16
    return -(-rows // q) * q


def _place():
    x, y, c = lax.axis_index("x"), lax.axis_index("y"), lax.axis_index("c")
    chips = [(1 - x, y), (x, 1 - y), (1 - x, 1 - y)]
    return x, y, c, chips


def all_gather_halves(src):
    def body(src_ref, out_ref, send_sems, recv_sems, local_sem):
        x, y, c, chips = _place()
        me = 2 * x + y
        sibling = (x, y, 1 - c)

        def copy(k, src, chip_idx, half, to):
            return pltpu.make_async_remote_copy(src_ref=src, dst_ref=out_ref.at[chip_idx, half], send_sem=send_sems.at[k],
                                                recv_sem=recv_sems.at[k], device_id=to, device_id_type=MESH)

        mine = pltpu.make_async_copy(src_ref, out_ref.at[me], local_sem)
        mine.start()
        first = [copy(j, src_ref.at[c], me, c, (*chip, c)) for j, chip in enumerate(chips)]
        for cp in first:
            cp.start()
        passed = []
        for j, chip in enumerate(chips):
            idx = 2 * chip[0] + chip[1]
            copy(j, src_ref.at[c], idx, c, sibling).wait_recv()
            cp = copy(3 + j, out_ref.at[idx, c], idx, c, sibling)
            cp.start()
            passed.append(cp)
        for j, chip in enumerate(chips):
            copy(3 + j, src_ref.at[c], 2 * chip[0] + chip[1], 1 - c, sibling).wait_recv()
        for cp in first + passed:
            cp.wait_send()
        mine.wait()

    return pl.pallas_call(
        body, name="all_gather_weights", in_specs=[_ANY], out_specs=_ANY,
        out_shape=jax.ShapeDtypeStruct((N_SHARD,) + src.shape, src.dtype),
        scratch_shapes=[pltpu.SemaphoreType.DMA((6,)), pltpu.SemaphoreType.DMA((6,)), pltpu.SemaphoreType.DMA],
    )(src)


def pair_swap(gs):
    n = len(gs)

    def body(*refs):
        g_refs, r_refs, send_sems, recv_sems = refs[:n], refs[n:2 * n], refs[2 * n], refs[2 * n + 1]
        x, y, c, _ = _place()
        copies = []
        for k in range(n):
            for j in range(N_SHARD):
                copies.append(pltpu.make_async_remote_copy(
                    src_ref=g_refs[k].at[j, 1 - c], dst_ref=r_refs[k].at[j], send_sem=send_sems.at[k * N_SHARD + j],
                    recv_sem=recv_sems.at[k * N_SHARD + j], device_id=(x, y, 1 - c), device_id_type=MESH))
        for cp in copies:
            cp.start()
        for cp in copies:
            cp.wait()

    return pl.pallas_call(
        body, name="grad_pair_swap", in_specs=[_ANY] * n, out_specs=[_ANY] * n,
        out_shape=[jax.ShapeDtypeStruct((N_SHARD,) + g.shape[2:], g.dtype) for g in gs],
        scratch_shapes=[pltpu.SemaphoreType.DMA((n * N_SHARD,)), pltpu.SemaphoreType.DMA((n * N_SHARD,))],
    )(*gs)


def _flat_tile(rows):
    return 512 if rows % 512 == 0 else rows


def pair_add(g, r, cidx, out_dtype, name):
    rows = g.shape[2]
    tr = _flat_tile(rows)

    def body(c_ref, g_ref, r_ref, o_ref):
        o_ref[...] = (g_ref[...] + r_ref[...]).astype(out_dtype)

    return pl.pallas_call(
        body, name=name,
        grid_spec=pltpu.PrefetchScalarGridSpec(
            num_scalar_prefetch=1, grid=(N_SHARD, rows // tr),
            in_specs=[pl.BlockSpec((None, None, tr, LANES), lambda j, i, c: (j, c[0], i, 0)),
                      pl.BlockSpec((None, tr, LANES), lambda j, i, c: (j, i, 0))],
            out_specs=pl.BlockSpec((None, tr, LANES), lambda j, i, c: (j, i, 0))),
        out_shape=jax.ShapeDtypeStruct((N_SHARD, rows, LANES), out_dtype),
        compiler_params=_cparams(),
    )(cidx, g, r)


def chip_scatter(ps):
    n = len(ps)

    def body(*refs):
        p_refs, r_refs, send_sems, recv_sems, local_sems = refs[:n], refs[n:2 * n], refs[2 * n], refs[2 * n + 1], refs[2 * n + 2]
        x, y, c, chips = _place()
        me = 2 * x + y
        local, copies = [], []
        for k in range(n):
            cp = pltpu.make_async_copy(p_refs[k].at[me], r_refs[k].at[me], local_sems.at[k])
            cp.start()
            local.append(cp)
            for j, chip in enumerate(chips):
                copies.append(pltpu.make_async_remote_copy(
                    src_ref=p_refs[k].at[2 * chip[0] + chip[1]], dst_ref=r_refs[k].at[me], send_sem=send_sems.at[k * 3 + j],
                    recv_sem=recv_sems.at[k * 3 + j], device_id=(*chip, c), device_id_type=MESH))
        for cp in copies:
            cp.start()
        for cp in copies:
            cp.wait()
        for cp in local:
            cp.wait()

    return pl.pallas_call(
        body, name="grad_chip_scatter", in_specs=[_ANY] * n, out_specs=[_ANY] * n,
        out_shape=[jax.ShapeDtypeStruct(p.shape, p.dtype) for p in ps],
        scratch_shapes=[pltpu.SemaphoreType.DMA((n * 3,)), pltpu.SemaphoreType.DMA((n * 3,)), pltpu.SemaphoreType.DMA((n,))],
    )(*ps)


def sum_slots(r, name):
    rows = r.shape[1]
    tr = _flat_tile(rows)

    def body(r_ref, o_ref):
        o_ref[...] = ((r_ref[0].astype(F32) + r_ref[1].astype(F32)) + r_ref[2].astype(F32)) + r_ref[3].astype(F32)

    return pl.pallas_call(
        body, name=name, grid=(rows // tr,),
        in_specs=[pl.BlockSpec((N_SHARD, tr, LANES), lambda i: (0, i, 0))],
        out_specs=pl.BlockSpec((tr, LANES), lambda i: (i, 0)),
        out_shape=jax.ShapeDtypeStruct((rows, LANES), F32), compiler_params=_cparams(),
    )(r)


def pair_gather(hs):
    n = len(hs)

    def body(*refs):
        h_refs, f_refs, send_sems, recv_sems, local_sems = refs[:n], refs[n:2 * n], refs[2 * n], refs[2 * n + 1], refs[2 * n + 2]
        x, y, c, _ = _place()
        local, copies = [], []
        for k in range(n):
            cp = pltpu.make_async_copy(h_refs[k], f_refs[k].at[c], local_sems.at[k])
            cp.start()
            local.append(cp)
            copies.append(pltpu.make_async_remote_copy(
                src_ref=h_refs[k], dst_ref=f_refs[k].at[c], send_sem=send_sems.at[k], recv_sem=recv_sems.at[k],
                device_id=(x, y, 1 - c), device_id_type=MESH))
        for cp in copies:
            cp.start()
        for cp in copies:
            cp.wait()
        for cp in local:
            cp.wait()

    return pl.pallas_call(
        body, name="grad_pair_gather", in_specs=[_ANY] * n, out_specs=[_ANY] * n,
        out_shape=[jax.ShapeDtypeStruct((2,) + h.shape, h.dtype) for h in hs],
        scratch_shapes=[pltpu.SemaphoreType.DMA((n,)), pltpu.SemaphoreType.DMA((n,)), pltpu.SemaphoreType.DMA((n,))],
    )(*hs)


def reduce_scatter(g_big, g_small, cidx):
    r_big, r_small = pair_swap([g_big, g_small])
    p_big = pair_add(g_big, r_big, cidx, BF16, "grad_pair_add")
    p_small = pair_add(g_small, r_small, cidx, F32, "grad_pair_add_small")
    q_big, q_small = chip_scatter([p_big, p_small])
    h_big = sum_slots(q_big, "grad_chip_sum")
    h_small = sum_slots(q_small, "grad_chip_sum_small")
    f_big, f_small = pair_gather([h_big, h_small])
    return f_big.reshape(-1, LANES), f_small.reshape(-1, LANES)


def _rows_of(c, half):
    return pl.ds(pl.multiple_of(c * half, 8), half)


def all_gather_shards(srcs, exact):
    n, m = len(srcs), len(exact)
    halves = [s.shape[0] // 2 for s in srcs]

    def body(*refs):
        s_refs, e_refs = refs[:n], refs[n:n + m]
        o_refs, eo_refs = refs[n + m:2 * n + m], refs[2 * n + m:2 * n + 2 * m]
        send, recv, esend, erecv, osend, orecv, lsem = refs[2 * n + 2 * m:]
        x, y, c, chips = _place()
        me = 2 * x + y
        sibling = (x, y, 1 - c)
        own = [pltpu.make_async_remote_copy(src_ref=s_refs[k], dst_ref=o_refs[k].at[me], send_sem=osend.at[k],
                                            recv_sem=orecv.at[k], device_id=sibling, device_id_type=MESH) for k in range(n)]
        local = [pltpu.make_async_copy(e_refs[k], eo_refs[k].at[me], lsem.at[k]) for k in range(m)]
        for cp in own + local:
            cp.start()

        def copy(k, s, src, idx, half_c, to):
            return pltpu.make_async_remote_copy(
                src_ref=src, dst_ref=o_refs[k].at[idx, _rows_of(half_c, halves[k])], send_sem=send.at[6 * k + s],
                recv_sem=recv.at[6 * k + s], device_id=to, device_id_type=MESH)

        def ecopy(k, j, idx, to):
            return pltpu.make_async_remote_copy(src_ref=e_refs[k], dst_ref=eo_refs[k].at[idx], send_sem=esend.at[3 * k + j],
                                                recv_sem=erecv.at[3 * k + j], device_id=to, device_id_type=MESH)

        sends = []
        for k in range(n):
            mine = s_refs[k].at[_rows_of(c, halves[k])]
            sends += [copy(k, j, mine, me, c, (*chip, c)) for j, chip in enumerate(chips)]
        for k in range(m):
            sends += [ecopy(k, j, me, (*chip, c)) for j, chip in enumerate(chips)]
        for cp in sends:
            cp.start()
        for j, chip in enumerate(chips):
            idx = 2 * chip[0] + chip[1]
            for k in range(n):
                landed = o_refs[k].at[idx, _rows_of(c, halves[k])]
                copy(k, j, landed, idx, c, sibling).wait_recv()
                fwd = copy(k, 3 + j, landed, idx, c, sibling)
                fwd.start()
                sends.append(fwd)
        for j, chip in enumerate(chips):
            idx = 2 * chip[0] + chip[1]
            for k in range(n):
                copy(k, 3 + j, s_refs[k].at[_rows_of(c, halves[k])], idx, 1 - c, sibling).wait_recv()
            for k in range(m):
                ecopy(k, j, idx, sibling).wait_recv()
        for cp in sends:
            cp.wait_send()
        for cp in own + local:
            cp.wait()

    outs = pl.pallas_call(
        body, name="all_gather_weights", in_specs=[_ANY] * (n + m), out_specs=[_ANY] * (n + m),
        out_shape=[jax.ShapeDtypeStruct((N_SHARD,) + a.shape, a.dtype) for a in list(srcs) + list(exact)],
        scratch_shapes=[pltpu.SemaphoreType.DMA((6 * n,)), pltpu.SemaphoreType.DMA((6 * n,)),
                        pltpu.SemaphoreType.DMA((3 * m,)), pltpu.SemaphoreType.DMA((3 * m,)),
                        pltpu.SemaphoreType.DMA((n,)), pltpu.SemaphoreType.DMA((n,)), pltpu.SemaphoreType.DMA((m,))],
    )(*srcs, *exact)
    return outs[:n], outs[n:]


def rs_pair_swap(gs):
    n = len(gs)

    def body(*refs):
        g_refs, r_refs, send, recv = refs[:n], refs[n:2 * n], refs[2 * n], refs[2 * n + 1]
        x, y, c, _ = _place()
        copies = [pltpu.make_async_remote_copy(
            src_ref=g_refs[k].at[pl.ds(0, N_SHARD), _rows_of(1 - c, gs[k].shape[1] // 2)], dst_ref=r_refs[k],
            send_sem=send.at[k], recv_sem=recv.at[k], device_id=(x, y, 1 - c), device_id_type=MESH) for k in range(n)]
        for cp in copies:
            cp.start()
        for cp in copies:
            cp.wait()

    return pl.pallas_call(
        body, name="grad_pair_swap", in_specs=[_ANY] * n, out_specs=[_ANY] * n,
        out_shape=[jax.ShapeDtypeStruct((N_SHARD, g.shape[1] // 2, g.shape[2]), g.dtype) for g in gs],
        scratch_shapes=[pltpu.SemaphoreType.DMA((n,)), pltpu.SemaphoreType.DMA((n,))],
    )(*gs)


def _group_tile(half, n_cols, n_arrays):
    budget = (20 * 2 ** 20) // (6 * n_arrays)
    fits = [t for t in range(8, half + 1, 8) if half % t == 0 and t * n_cols * 4 <= budget]
    return max(fits) if fits else 8


def rs_pair_add(gs, rs, cidx, out_dtype, name):
    n = len(gs)
    _, K, cols = gs[0].shape
    half = K // 2
    tr = _group_tile(half, cols, n)
    nb = half // tr

    def body(c_ref, *refs):
        for g_ref, r_ref, o_ref in zip(refs[:n], refs[n:2 * n], refs[2 * n:]):
            o_ref[...] = (g_ref[...].astype(F32) + r_ref[...].astype(F32)).astype(out_dtype)

    gspec = pl.BlockSpec((None, tr, cols), lambda j, i, c: (j, c[0] * nb + i, 0))
    rspec = pl.BlockSpec((None, tr, cols), lambda j, i, c: (j, i, 0))
    return pl.pallas_call(
        body, name=name,
        grid_spec=pltpu.PrefetchScalarGridSpec(num_scalar_prefetch=1, grid=(N_SHARD, nb), in_specs=[gspec] * n + [rspec] * n,
                                               out_specs=[rspec] * n),
        out_shape=[jax.ShapeDtypeStruct((N_SHARD, half, cols), out_dtype)] * n,
        compiler_params=_cparams(),
    )(cidx, *gs, *rs)


def rs_chip_scatter(ps):
    n = len(ps)

    def body(*refs):
        p_refs, q_refs, send, recv, lsem = refs[:n], refs[n:2 * n], refs[2 * n], refs[2 * n + 1], refs[2 * n + 2]
        x, y, c, chips = _place()
        me = 2 * x + y
        local = [pltpu.make_async_copy(p_refs[k].at[me], q_refs[k].at[me], lsem.at[k]) for k in range(n)]
        copies = [pltpu.make_async_remote_copy(
            src_ref=p_refs[k].at[2 * chip[0] + chip[1]], dst_ref=q_refs[k].at[me], send_sem=send.at[3 * k + j],
            recv_sem=recv.at[3 * k + j], device_id=(*chip, c), device_id_type=MESH)
            for k in range(n) for j, chip in enumerate(chips)]
        for cp in local + copies:
            cp.start()
        for cp in copies:
            cp.wait()
        for cp in local:
            cp.wait()

    return pl.pallas_call(
        body, name="grad_chip_scatter", in_specs=[_ANY] * n, out_specs=[_ANY] * n,
        out_shape=[jax.ShapeDtypeStruct(p.shape, p.dtype) for p in ps],
        scratch_shapes=[pltpu.SemaphoreType.DMA((3 * n,)), pltpu.SemaphoreType.DMA((3 * n,)), pltpu.SemaphoreType.DMA((n,))],
    )(*ps)


def rs_chip_sum(qs, nl, cidx, name):
    n = len(qs)
    _, half, cols = qs[0].shape
    tr = _group_tile(half, cols, n)
    nb = half // tr

    def body(c_ref, *refs):
        for k, q_ref in enumerate(refs[:n]):
            o_ref = refs[n + k // nl]
            o_ref[k % nl] = ((q_ref[0].astype(F32) + q_ref[1].astype(F32)) + q_ref[2].astype(F32)) + q_ref[3].astype(F32)

    return pl.pallas_call(
        body, name=name,
        grid_spec=pltpu.PrefetchScalarGridSpec(
            num_scalar_prefetch=1, grid=(nb,),
            in_specs=[pl.BlockSpec((N_SHARD, tr, cols), lambda i, c: (0, i, 0))] * n,
            out_specs=[pl.BlockSpec((nl, tr, cols), lambda i, c: (0, c[0] * nb + i, 0))] * (n // nl)),
        out_shape=[jax.ShapeDtypeStruct((nl, 2 * half, cols), F32)] * (n // nl),
        compiler_params=_cparams(),
    )(cidx, *qs)


def rs_pair_gather(fs):
    n = len(fs)

    def body(*refs):
        f_refs, send, recv = refs[n:2 * n], refs[2 * n], refs[2 * n + 1]
        x, y, c, _ = _place()
        copies = []
        for k in range(n):
            rows = f_refs[k].at[pl.ds(0, fs[k].shape[0]), _rows_of(c, fs[k].shape[1] // 2)]
            copies.append(pltpu.make_async_remote_copy(src_ref=rows, dst_ref=rows, send_sem=send.at[k], recv_sem=recv.at[k],
                                                       device_id=(x, y, 1 - c), device_id_type=MESH))
        for cp in copies:
            cp.start()
        for cp in copies:
            cp.wait()

    return pl.pallas_call(
        body, name="grad_pair_gather", in_specs=[_ANY] * n, out_specs=[_ANY] * n,
        out_shape=[jax.ShapeDtypeStruct(f.shape, f.dtype) for f in fs],
        input_output_aliases={k: k for k in range(n)},
        scratch_shapes=[pltpu.SemaphoreType.DMA((n,)), pltpu.SemaphoreType.DMA((n,))],
    )(*fs)


def reduce_scatter_shards(items, wire_of, cidx):
    nl = [len(it) for it in items]
    gs = [g for it in items for g in it]
    wire = [wire_of[i] for i, it in enumerate(items) for _ in it]
    first = np.cumsum([0] + nl)
    rs = rs_pair_swap(gs)
    groups = {}
    for i, it in enumerate(items):
        groups.setdefault((it[0].shape, len(it), jnp.dtype(wire_of[i]).name), []).append(i)
    ps = [None] * len(gs)
    for gi, ids in enumerate(groups.values()):
        ks = [first[i] + l for i in ids for l in range(nl[i])]
        outs = rs_pair_add([gs[k] for k in ks], [rs[k] for k in ks], cidx, wire[ks[0]], "grad_pair_add_%d" % gi)
        for k, o in zip(ks, outs):
            ps[k] = o
    qs = rs_chip_scatter(ps)
    fs = [None] * len(items)
    for gi, ids in enumerate(groups.values()):
        ks = [first[i] + l for i in ids for l in range(nl[i])]
        outs = rs_chip_sum([qs[k] for k in ks], nl[ids[0]], cidx, "grad_chip_sum_%d" % gi)
        for i, o in zip(ids, outs):
            fs[i] = o
    return rs_pair_gather(fs)


def _flat_view(shape):
    n = int(np.prod(shape))
    if n <= 2 ** 20 and n % LANES == 0:
        return (n // LANES, LANES)
    if n <= 2 ** 20 and n % 128 == 0:
        return (n // 128, 128)
    return (n // shape[-1], shape[-1])


def adamw(w, g, m, v, name):
    shape = w.shape
    if w.ndim == 2:
        block, grid, index = shape, (1,), (lambda i: (0, 0))
    else:
        slab = shape[2:]
        unit = 4 * int(np.prod(slab[:-2] or (1,))) * (-(-slab[-1] // 128) * 128)
        if len(slab) >= 2:
            unit *= -(-slab[-2] // 8) * 8
        k = shape[1]
        tr = k
        if k * unit > 2 ** 21:
            tr = max(t for t in range(8, k, 8) if k % t == 0 and t * unit <= 2 ** 21)
        block, grid = (None, tr) + tuple(slab), (shape[0], k // tr)
        index = lambda l, i: (l, i) + (0,) * len(slab)

    def body(w_ref, g_ref, m_ref, v_ref, d_ref, nm_ref, nv_ref):
        g_ = g_ref[...]
        m_new = ADAM_B1 * m_ref[...] + (1.0 - ADAM_B1) * g_
        v_new = ADAM_B2 * v_ref[...] + (1.0 - ADAM_B2) * (g_ * g_)
        m_hat = m_new / (1.0 - ADAM_B1 ** ADAM_STEP)
        v_hat = v_new / (1.0 - ADAM_B2 ** ADAM_STEP)
        d_ref[...] = -ADAM_LR * (m_hat / (jnp.sqrt(v_hat) + ADAM_EPS) + ADAM_WD * w_ref[...])
        nm_ref[...] = m_new
        nv_ref[...] = v_new

    spec = pl.BlockSpec(block, index)
    return pl.pallas_call(
        body, name=name, grid=grid, in_specs=[spec] * 4, out_specs=[spec] * 3,
        out_shape=[jax.ShapeDtypeStruct(shape, F32)] * 3, compiler_params=_cparams(),
    )(w, g, m, v)


_WEIGHTS = ['meta', 'ffn1_w_gate', 'ffn1_w_up', 'ffn1_w_down', 'ln1_g', 'ln1_b', 'w_in', 'mla_q_norm_g', 'mla_w_uq',
            'mla_kv_norm_g', 'mla_w_ukv', 'mla_w_o', 'conv_w', 'conv_b', 'conv_w_out', 's5_a_re', 's5_a_im', 's5_log_dt',
            's5_b_re', 's5_b_im', 's5_c_re', 's5_c_im', 's5_d', 's5_w_glu', 's5_b_glu', 's5_w_out', 'w_o', 'ln2_g', 'ln2_b',
            'ffn2_w_gate', 'ffn2_w_up', 'ffn2_w_down', 'ln3_g', 'ln3_b']


def _pad_to(flat, n):
    return jnp.concatenate([flat, jnp.zeros((n - flat.shape[0],), flat.dtype)])


def _shard_of(full, axis):
    if axis == 1:
        return _shard_cols(full)
    return full.reshape(N_SHARD, full.shape[0] // N_SHARD, full.shape[1])


_FFN_KEY = {'gate': 'wg', 'up': 'wu', 'down': 'wd'}


def _step(env):
    w = {n: env[n] for n in _WEIGHTS}
    mom = {n: env['m_' + n] for n in _WEIGHTS}
    var = {n: env['v_' + n] for n in _WEIGHTS}
    cidx = lax.axis_index("c").astype(jnp.int32).reshape(1)
    chip = 2 * lax.axis_index("x") + lax.axis_index("y")
    big_names = [n for n, _ in _BIG]
    nb = len(big_names)

    srcs = [w[n][li].astype(BF16) for li in range(DEPTH) for n in big_names]
    gathered, (conv_w_st, meta_st) = all_gather_shards(srcs, [w['conv_w'], w['meta']])
    meta_full = _nat_cols(meta_st)
    Ws = []
    for li in range(DEPTH):
        small = {n: w[n][li] for n in _REPL}
        small['conv_w'] = _nat_cols(conv_w_st[:, li])
        Ws.append(compute_weights({n: gathered[li * nb + i] for i, n in enumerate(big_names)}, small))

    loss_part, dx, dmeta, grads = local_step(env['x'][0], env['loss_target'][0], meta_full, Ws)
    loss = lax.psum(loss_part[0, 0], ("x", "y", "c"))
    full = [reference_grads(g, ffn=False) for g in grads]

    def shard(li, n, axis):
        if n.startswith('ffn'):
            return grads[li][_FFN_KEY[n.split('_')[-1]] + n[3]]
        return _shard_of(full[li][n], axis)

    items = [[shard(li, n, a) for li in range(DEPTH)] for n, a in _BIG]
    s_parts = [jnp.stack([full[li][n] for li in range(DEPTH)]).reshape(-1) for n in _REPL + ['conv_w']] + [dmeta.reshape(-1)]
    s_sizes = [int(p.shape[0]) for p in s_parts]
    s_rows = -(-sum(s_sizes) // (16 * LANES)) * 16
    g_small = _pad_to(jnp.concatenate(s_parts), s_rows * LANES).reshape(1, s_rows, LANES)
    g_small = jnp.broadcast_to(g_small, (N_SHARD, s_rows, LANES))
    red = reduce_scatter_shards(items + [[g_small]], [BF16] * nb + [F32], cidx)
    f_small = red[-1].reshape(-1)

    grad = dict(zip(big_names, red[:nb]))
    off = 0
    for n, sz in zip(_REPL + ['conv_w', 'meta'], s_sizes):
        grad[n] = f_small[off:off + sz]
        off += sz
    for n in _REPL:
        grad[n] = grad[n].reshape(w[n].shape)
    cw = grad['conv_w'].reshape(DEPTH, 3, MIX)
    grad['conv_w'] = lax.dynamic_slice_in_dim(cw, chip * (MIX // N_SHARD), MIX // N_SHARD, axis=2)
    gm = grad['meta'].reshape(N_META, D_MODEL)
    grad['meta'] = lax.dynamic_slice_in_dim(gm, chip * (D_MODEL // N_SHARD), D_MODEL // N_SHARD, axis=1)

    delta, new_m, new_v = {}, {}, {}
    for n in _WEIGHTS:
        delta[n], new_m[n], new_v[n] = adamw(w[n], grad[n], mom[n], var[n], "adamw_" + n)
    return (loss, dx[None], *[grad[n] for n in _WEIGHTS], *[delta[n] for n in _WEIGHTS],
            *[new_m[n] for n in _WEIGHTS], *[new_v[n] for n in _WEIGHTS])


def kernel(x, meta, ffn1_w_gate, ffn1_w_up, ffn1_w_down, ln1_g, ln1_b, w_in, mla_q_norm_g, mla_w_uq, mla_kv_norm_g, mla_w_ukv, mla_w_o, conv_w, conv_b, conv_w_out, s5_a_re, s5_a_im, s5_log_dt, s5_b_re, s5_b_im, s5_c_re, s5_c_im, s5_d, s5_w_glu, s5_b_glu, s5_w_out, w_o, ln2_g, ln2_b, ffn2_w_gate, ffn2_w_up, ffn2_w_down, ln3_g, ln3_b, loss_target, m_meta, m_ffn1_w_gate, m_ffn1_w_up, m_ffn1_w_down, m_ln1_g, m_ln1_b, m_w_in, m_mla_q_norm_g, m_mla_w_uq, m_mla_kv_norm_g, m_mla_w_ukv, m_mla_w_o, m_conv_w, m_conv_b, m_conv_w_out, m_s5_a_re, m_s5_a_im, m_s5_log_dt, m_s5_b_re, m_s5_b_im, m_s5_c_re, m_s5_c_im, m_s5_d, m_s5_w_glu, m_s5_b_glu, m_s5_w_out, m_w_o, m_ln2_g, m_ln2_b, m_ffn2_w_gate, m_ffn2_w_up, m_ffn2_w_down, m_ln3_g, m_ln3_b, v_meta, v_ffn1_w_gate, v_ffn1_w_up, v_ffn1_w_down, v_ln1_g, v_ln1_b, v_w_in, v_mla_q_norm_g, v_mla_w_uq, v_mla_kv_norm_g, v_mla_w_ukv, v_mla_w_o, v_conv_w, v_conv_b, v_conv_w_out, v_s5_a_re, v_s5_a_im, v_s5_log_dt, v_s5_b_re, v_s5_b_im, v_s5_c_re, v_s5_c_im, v_s5_d, v_s5_w_glu, v_s5_b_glu, v_s5_w_out, v_w_o, v_ln2_g, v_ln2_b, v_ffn2_w_gate, v_ffn2_w_up, v_ffn2_w_down, v_ln3_g, v_ln3_b):
    return _step(dict(locals()))


def _unused_packed_step(env):
    w = {n: env[n] for n in _WEIGHTS}
    mom = {n: env['m_' + n] for n in _WEIGHTS}
    var = {n: env['v_' + n] for n in _WEIGHTS}
    cidx = lax.axis_index("c").astype(jnp.int32).reshape(1)
    big_names = [n for n, _ in _BIG]

    exact = jnp.concatenate([conv_w.reshape(-1), meta.reshape(-1)])
    parts = [w[n].astype(BF16).reshape(-1) for n in big_names] + [lax.bitcast_convert_type(exact, BF16).reshape(-1)]
    sizes = [int(p.shape[0]) for p in parts]
    hr = _half_rows(sum(sizes))
    packed = _pad_to(jnp.concatenate(parts), 2 * hr * LANES).reshape(2, hr, LANES)
    gathered = all_gather_halves(packed).reshape(N_SHARD, -1)
    st, off = {}, 0
    for n, sz in zip(big_names, sizes[:-1]):
        st[n] = gathered[:, off:off + sz].reshape((N_SHARD,) + w[n].shape)
        off += sz
    exact_all = lax.bitcast_convert_type(gathered[:, off:off + sizes[-1]].reshape(N_SHARD, -1, 2), F32)
    n_cw = conv_w.size
    conv_w_st = exact_all[:, :n_cw].reshape((N_SHARD,) + conv_w.shape)
    meta_full = _nat_cols(exact_all[:, n_cw:].reshape((N_SHARD,) + meta.shape))

    Ws = []
    for li in range(DEPTH):
        small = {n: w[n][li] for n in _REPL}
        small['conv_w'] = _nat_cols(conv_w_st[:, li])
        Ws.append(compute_weights({n: st[n][:, li] for n in big_names}, small))

    loss_part, dx, dmeta, grads = local_step(x[0], loss_target[0], meta_full, Ws)
    loss = lax.psum(loss_part[0, 0], ("x", "y", "c"))
    ref_grads = [shard_grads(g) for g in grads]

    def shard_stack(n):
        return jnp.stack([ref_grads[li][n] for li in range(DEPTH)], axis=1)

    g_parts = [shard_stack(n).reshape(N_SHARD, -1) for n, _ in _BIG]
    g_parts.append(shard_stack('conv_w').reshape(N_SHARD, -1))
    g_parts.append(_shard_cols(dmeta).reshape(N_SHARD, -1))
    g_sizes = [int(p.shape[1]) for p in g_parts]
    ghr = _half_rows(sum(g_sizes))
    g_big = jnp.concatenate(g_parts + [jnp.zeros((N_SHARD, 2 * ghr * LANES - sum(g_sizes)), F32)], axis=1)
    g_big = g_big.reshape(N_SHARD, 2, ghr, LANES)
    s_parts = [jnp.stack([ref_grads[li][n] for li in range(DEPTH)]).reshape(-1) for n in _REPL]
    s_sizes = [int(p.shape[0]) for p in s_parts]
    shr = _half_rows(sum(s_sizes))
    g_small = _pad_to(jnp.concatenate(s_parts), 2 * shr * LANES)
    g_small = jnp.broadcast_to(g_small.reshape(1, 2, shr, LANES), (N_SHARD, 2, shr, LANES))
    f_big, f_small = reduce_scatter(g_big, g_small, cidx)
    f_big, f_small = f_big.reshape(-1), f_small.reshape(-1)

    grad = {}
    off = 0
    for (n, _), sz in zip(_BIG + [('conv_w', 1), ('meta', 1)], g_sizes):
        grad[n] = f_big[off:off + sz].reshape(w[n].shape)
        off += sz
    off = 0
    for n, sz in zip(_REPL, s_sizes):
        grad[n] = f_small[off:off + sz].reshape(w[n].shape)
        off += sz

    delta, new_m, new_v = {}, {}, {}
    for n in _WEIGHTS:
        delta[n], new_m[n], new_v[n] = adamw(w[n], grad[n], mom[n], var[n], "adamw_" + n)
    return (loss, dx[None], *[grad[n] for n in _WEIGHTS], *[delta[n] for n in _WEIGHTS],
            *[new_m[n] for n in _WEIGHTS], *[new_v[n] for n in _WEIGHTS])
```

```python
import functools
import math

import numpy as np
import jax
import jax.numpy as jnp
from jax import lax
from jax.experimental import pallas as pl
from jax.experimental.pallas import tpu as pltpu

F32 = jnp.float32
BF16 = jnp.bfloat16

D_MODEL = 1024
DEPTH = 2
N_META = 16
PAD = 112
X0 = PAD + N_META
N_HEADS = 8
D_NOPE = 64
D_ROPE = 32
D_V = 64
Q_RANK = 384
KV_RANK = 256
MIX = 512
S5_GROUPS = 32
S5_GROUP = 16
S5_STATE = 64
S5_LANES = S5_GROUPS * S5_STATE
D_FF = 2816
N_SHARD = 4
FF_SHARD = D_FF // N_SHARD
D_IN = 5792
P_IN = 6144
ALPHA = (2.0 * DEPTH) ** 0.25
LN_EPS = 1e-5
RMS_EPS = 1e-6
ATT_SCALE = (D_NOPE + D_ROPE) ** -0.5
ROPE_BASE = 10000.0
ADAM_LR, ADAM_B1, ADAM_B2, ADAM_EPS, ADAM_WD, ADAM_STEP = 0.001, 0.9, 0.999, 1e-08, 0.01, 10
SCAN_CHUNK = 128
VMEM_LIMIT = 52 * 2 ** 20
WGRAD = BF16
MESH = pl.DeviceIdType.MESH


def _cparams(**kw):
    return pltpu.CompilerParams(vmem_limit_bytes=VMEM_LIMIT, **kw)


def _tile(n):
    if n <= 1088:
        return n
    for t in (1024, 544, 512, 272, 256, 128):
        if n % t == 0:
            return t
    return n


def _row_tile(lp):
    for t in (544, 272, 128):
        if lp % t == 0:
            return t
    return lp


def _sigmoid(x):
    return 1.0 / (1.0 + jnp.exp(-x))


_GELU_C = math.sqrt(2.0 / math.pi)


def _gelu(x):
    return 0.5 * x * (1.0 + jnp.tanh(_GELU_C * (x + 0.044715 * x * x * x)))


def _gelu_grad(x):
    t = jnp.tanh(_GELU_C * (x + 0.044715 * x * x * x))
    return 0.5 * (1.0 + t) + 0.5 * x * (1.0 - t * t) * _GELU_C * (1.0 + 3.0 * 0.044715 * x * x)


def _dot(a, b, ca, cb, precision=None):
    return lax.dot_general(a, b, (((ca,), (cb,)), ((), ())), preferred_element_type=F32, precision=precision)


def matmul(a, b, *, name, ta=False, tb=False, ab='n', bb='n', res=None, res_scale=1.0, scale=1.0, out_dtype=F32):
    if ta:
        _, K, M = a.shape
    else:
        _, M, K = a.shape
    if tb:
        _, N, K2 = b.shape
    else:
        _, K2, N = b.shape
    assert K == K2, (a.shape, b.shape)
    n_out = max(a.shape[0] if ab == 'o' else 1, b.shape[0] if bb == 'o' else 1)
    n_red = max(a.shape[0] if ab == 'r' else 1, b.shape[0] if bb == 'r' else 1)
    tm, tn = _tile(M), _tile(N)
    tk = K if K <= 2304 else _tile(K)
    nkt = K // tk
    n_steps = n_red * nkt

    def bsel(mode, o, r):
        if mode == 'o':
            return o
        if mode == 'r':
            return r // nkt if nkt > 1 else r
        return 0

    def ksel(r):
        if nkt == 1:
            return 0
        return r % nkt if n_red > 1 else r

    a_map = (lambda o, i, j, r: (bsel(ab, o, r), ksel(r), i)) if ta else (lambda o, i, j, r: (bsel(ab, o, r), i, ksel(r)))
    b_map = (lambda o, i, j, r: (bsel(bb, o, r), j, ksel(r))) if tb else (lambda o, i, j, r: (bsel(bb, o, r), ksel(r), j))
    o_map = lambda o, i, j, r: (o, i, j)
    in_specs = [pl.BlockSpec((None, tk, tm) if ta else (None, tm, tk), a_map),
                pl.BlockSpec((None, tn, tk) if tb else (None, tk, tn), b_map)]
    operands = [a, b]
    if res is not None:
        in_specs.append(pl.BlockSpec((None, tm, tn), o_map))
        operands.append(res)
    has_res = res is not None

    def body(*refs):
        a_ref, b_ref = refs[0], refs[1]
        res_ref = refs[2] if has_res else None
        o_ref = refs[3] if has_res else refs[2]
        part = _dot(a_ref[...].astype(BF16), b_ref[...].astype(BF16), 0 if ta else 1, 1 if tb else 0)

        def finish(acc):
            v = acc if scale == 1.0 else acc * scale
            if has_res:
                v = v + res_scale * res_ref[...].astype(F32)
            o_ref[...] = v.astype(o_ref.dtype)

        if n_steps == 1:
            finish(part)
        else:
            acc_ref = refs[-1]
            r = pl.program_id(3)

            @pl.when(r == 0)
            def _():
                acc_ref[...] = part

            @pl.when(r > 0)
            def _():
                acc_ref[...] += part

            @pl.when(r == n_steps - 1)
            def _():
                finish(acc_ref[...])

    return pl.pallas_call(
        body, name=name,
        grid=(n_out, M // tm, N // tn, n_steps),
        in_specs=in_specs,
        out_specs=pl.BlockSpec((None, tm, tn), o_map),
        out_shape=jax.ShapeDtypeStruct((n_out, M, N), out_dtype),
        scratch_shapes=[pltpu.VMEM((tm, tn), F32)] if n_steps > 1 else [],
        compiler_params=_cparams(),
    )(*operands)


def rowwise(fn, rows, pars, outs, accs=(), *, name, lp):
    tm = _row_tile(lp)
    n_rows, n_pars, n_outs, n_accs = len(rows), len(pars), len(outs), len(accs)
    in_specs = [pl.BlockSpec((tm, w), functools.partial(lambda i, cb: (i, cb), cb=cb)) for _, w, cb in rows]
    in_specs += [pl.BlockSpec(p.shape, functools.partial(lambda i, nd: (0,) * nd, nd=p.ndim)) for p in pars]
    out_specs = [pl.BlockSpec((tm, w), lambda i: (i, 0)) for w, _ in outs]
    out_specs += [pl.BlockSpec(s, functools.partial(lambda i, nd: (0,) * nd, nd=len(s))) for s, _ in accs]
    out_shape = [jax.ShapeDtypeStruct((lp, w), dt) for w, dt in outs]
    out_shape += [jax.ShapeDtypeStruct(s, dt) for s, dt in accs]

    def body(*refs):
        i = pl.program_id(0)
        rv = [r[...] for r in refs[:n_rows]]
        pv = [r[...] for r in refs[n_rows:n_rows + n_pars]]
        o_refs = refs[n_rows + n_pars:n_rows + n_pars + n_outs]
        a_refs = refs[n_rows + n_pars + n_outs:]
        ov, av = fn(i * tm, rv, pv)
        for r, v in zip(o_refs, ov):
            r[...] = v.astype(r.dtype)
        if n_accs:
            @pl.when(i == 0)
            def _():
                for r, v in zip(a_refs, av):
                    r[...] = v.astype(r.dtype)

            @pl.when(i > 0)
            def _():
                for r, v in zip(a_refs, av):
                    r[...] += v.astype(r.dtype)

    res = pl.pallas_call(
        body, name=name, grid=(lp // tm,), in_specs=in_specs, out_specs=out_specs, out_shape=out_shape,
        compiler_params=_cparams(),
    )(*[r[0] for r in rows], *pars)
    return res


def _row_mask(row0, shape):
    return (row0 + lax.broadcasted_iota(jnp.int32, shape, 0)) >= PAD


def ffn_up(hb, wg, wu, lp):
    tm = _row_tile(lp)

    def body(h_ref, wg_ref, wu_ref, ab_ref, hid_ref):
        h = h_ref[...]
        a = _dot(h, wg_ref[...], 1, 0)
        b = _dot(h, wu_ref[...], 1, 0)
        ab_ref[0] = a
        ab_ref[1] = b
        hid_ref[...] = (a * _sigmoid(a) * b).astype(BF16)

    wspec = pl.BlockSpec((None, D_MODEL, FF_SHARD), lambda j, i: (j, 0, 0))
    return pl.pallas_call(
        body, name="ffn_up", grid=(N_SHARD, lp // tm),
        in_specs=[pl.BlockSpec((tm, D_MODEL), lambda j, i: (i, 0)), wspec, wspec],
        out_specs=[pl.BlockSpec((None, 2, tm, FF_SHARD), lambda j, i: (j, 0, i, 0)),
                   pl.BlockSpec((None, tm, FF_SHARD), lambda j, i: (j, i, 0))],
        out_shape=[jax.ShapeDtypeStruct((N_SHARD, 2, lp, FF_SHARD), F32),
                   jax.ShapeDtypeStruct((N_SHARD, lp, FF_SHARD), BF16)],
        compiler_params=_cparams(),
    )(hb, wg, wu)


def _layer_norm(z, g, b):
    mu = jnp.mean(z, axis=-1, keepdims=True)
    zc = z - mu
    var = jnp.mean(zc * zc, axis=-1, keepdims=True)
    return zc * lax.rsqrt(var + LN_EPS) * g + b


def mm_res_ln(a, w, res, g, b, *, scale, name, lp):
    n_red, _, K = a.shape
    tm = _row_tile(lp)

    def body(a_ref, w_ref, res_ref, g_ref, b_ref, z_ref, h_ref, hb_ref, acc_ref):
        r = pl.program_id(1)
        part = _dot(a_ref[...].astype(BF16), w_ref[...], 1, 0)

        @pl.when(r == 0)
        def _():
            acc_ref[...] = part

        @pl.when(r > 0)
        def _():
            acc_ref[...] += part

        @pl.when(r == n_red - 1)
        def _():
            z = ALPHA * res_ref[...] + scale * acc_ref[...]
            z_ref[...] = z
            hn = _layer_norm(z, g_ref[...], b_ref[...])
            h_ref[...] = hn
            hb_ref[...] = hn.astype(BF16)

    row = pl.BlockSpec((tm, D_MODEL), lambda i, r: (i, 0))
    par = pl.BlockSpec((1, D_MODEL), lambda i, r: (0, 0))
    return pl.pallas_call(
        body, name=name, grid=(lp // tm, n_red),
        in_specs=[pl.BlockSpec((None, tm, K), lambda i, r: (r, i, 0)),
                  pl.BlockSpec((None, K, D_MODEL), lambda i, r: (r, 0, 0)), row, par, par],
        out_specs=[row, row, row],
        out_shape=[jax.ShapeDtypeStruct((lp, D_MODEL), F32), jax.ShapeDtypeStruct((lp, D_MODEL), F32),
                   jax.ShapeDtypeStruct((lp, D_MODEL), BF16)],
        scratch_shapes=[pltpu.VMEM((tm, D_MODEL), F32)],
        compiler_params=_cparams(),
    )(a, w, res, g, b)


def ln_bwd(dh, z, g, *, fscale, name, lp):
    def fn(row0, rv, pv):
        dh_, z_ = rv
        g_, = pv
        mu = jnp.mean(z_, axis=-1, keepdims=True)
        zc = z_ - mu
        rstd = lax.rsqrt(jnp.mean(zc * zc, axis=-1, keepdims=True) + LN_EPS)
        xh = zc * rstd
        dxh = dh_ * g_
        m1 = jnp.mean(dxh, axis=-1, keepdims=True)
        m2 = jnp.mean(dxh * xh, axis=-1, keepdims=True)
        dz = rstd * (dxh - m1 - xh * m2)
        return ((dz, fscale * dz),
                (jnp.sum(dh_ * xh, axis=0, keepdims=True), jnp.sum(dh_, axis=0, keepdims=True)))

    return rowwise(fn, [(dh, D_MODEL, 0), (z, D_MODEL, 0)], [g], [(D_MODEL, F32), (D_MODEL, BF16)],
                   [((1, D_MODEL), F32), ((1, D_MODEL), F32)], name=name, lp=lp)


def ffn_down_bwd(dfb, wd, ab, lp):
    tm = _row_tile(lp)

    def body(df_ref, w_ref, ab_ref, da_ref, db_ref):
        dhid = _dot(df_ref[...], w_ref[...], 1, 1)
        a = ab_ref[0]
        b = ab_ref[1]
        sg = _sigmoid(a)
        da_ref[...] = (dhid * b * (sg * (1.0 + a * (1.0 - sg)))).astype(BF16)
        db_ref[...] = (dhid * (a * sg)).astype(BF16)

    ospec = pl.BlockSpec((None, tm, FF_SHARD), lambda j, i: (j, i, 0))
    return pl.pallas_call(
        body, name="ffn_down_bwd", grid=(N_SHARD, lp // tm),
        in_specs=[pl.BlockSpec((tm, D_MODEL), lambda j, i: (i, 0)),
                  pl.BlockSpec((None, FF_SHARD, D_MODEL), lambda j, i: (j, 0, 0)),
                  pl.BlockSpec((None, 2, tm, FF_SHARD), lambda j, i: (j, 0, i, 0))],
        out_specs=[ospec, ospec],
        out_shape=[jax.ShapeDtypeStruct((N_SHARD, lp, FF_SHARD), BF16)] * 2,
        compiler_params=_cparams(),
    )(dfb, wd, ab)


def ffn_dx(da, db, wg, wu, dz, lp):
    tm = _row_tile(lp)

    def body(da_ref, db_ref, wg_ref, wu_ref, dz_ref, o_ref, acc_ref):
        j = pl.program_id(1)
        part = _dot(da_ref[...], wg_ref[...], 1, 1) + _dot(db_ref[...], wu_ref[...], 1, 1)

        @pl.when(j == 0)
        def _():
            acc_ref[...] = part

        @pl.when(j > 0)
        def _():
            acc_ref[...] += part

        @pl.when(j == N_SHARD - 1)
        def _():
            o_ref[...] = acc_ref[...] + ALPHA * dz_ref[...]

    aspec = pl.BlockSpec((None, tm, FF_SHARD), lambda i, j: (j, i, 0))
    wspec = pl.BlockSpec((None, D_MODEL, FF_SHARD), lambda i, j: (j, 0, 0))
    row = pl.BlockSpec((tm, D_MODEL), lambda i, j: (i, 0))
    return pl.pallas_call(
        body, name="ffn_dx", grid=(lp // tm, N_SHARD), in_specs=[aspec, aspec, wspec, wspec, row], out_specs=row,
        out_shape=jax.ShapeDtypeStruct((lp, D_MODEL), F32), scratch_shapes=[pltpu.VMEM((tm, D_MODEL), F32)],
        compiler_params=_cparams(),
    )(da, db, wg, wu, dz)


def ffn_fwd(h, hb, wg, wu, wd, g, b, lp):
    ab, hid = ffn_up(hb, wg, wu, lp)
    z, hn, hnb = mm_res_ln(hid, wd, h, g, b, scale=0.5, name="ffn_down_ln", lp=lp)
    return hn, hnb, dict(hb=hb, ab=ab, hid=hid, z=z)


def ffn_bwd(dh, sv, wg, wu, wd, g, lp):
    dz, dfb, dg, db = ln_bwd(dh, sv['z'], g, fscale=0.5, name="ffn_ln_bwd", lp=lp)
    da, dbb = ffn_down_bwd(dfb, wd, sv['ab'], lp)
    d_wd = matmul(sv['hid'], dfb[None], ta=True, ab='o', out_dtype=WGRAD, name="ffn_dwd")
    d_wg = matmul(sv['hb'][None], da, ta=True, bb='o', out_dtype=WGRAD, name="ffn_dwg")
    d_wu = matmul(sv['hb'][None], dbb, ta=True, bb='o', out_dtype=WGRAD, name="ffn_dwu")
    dh_in = ffn_dx(da, dbb, wg, wu, dz, lp)
    return dh_in, dict(wg=d_wg, wu=d_wu, wd=d_wd, ln_g=dg, ln_b=db)


def _rope_tables(lp):
    pos = np.arange(lp, dtype=np.float32) - PAD
    inv = ROPE_BASE ** (-np.arange(0, D_ROPE, 2, dtype=np.float32) / D_ROPE)
    ang = pos[:, None] * inv[None, :]
    cos = np.concatenate([np.cos(ang), np.cos(ang)], axis=1).astype(np.float32)
    sin = np.concatenate([np.sin(ang), np.sin(ang)], axis=1).astype(np.float32)
    rot = np.zeros((D_ROPE, D_ROPE), np.float32)
    half = D_ROPE // 2
    for j in range(half):
        rot[j + half, j] = -1.0
        rot[j, j + half] = 1.0
    return jnp.asarray(cos), jnp.asarray(sin), jnp.asarray(rot)


def _rot(x, rot):
    return _dot(x, rot, 1, 0, precision=lax.Precision.HIGHEST)


def _rms(x, g):
    r = lax.rsqrt(jnp.mean(x * x, axis=-1, keepdims=True) + RMS_EPS)
    return x * r * g


def mla_prep(proj, cos, sin, rot, qg, kvg, lp):
    def fn(row0, rv, pv):
        cq, krb, ckv, c, s = rv
        qg_, kvg_, rot_ = pv
        kr = krb[:, :D_ROPE]
        return ((_rms(cq, qg_), _rms(ckv, kvg_), kr * c + _rot(kr, rot_) * s), ())

    return rowwise(fn, [(proj, Q_RANK, 0), (proj, 128, 3), (proj, KV_RANK, 2), (cos, D_ROPE, 0), (sin, D_ROPE, 0)],
                   [qg, kvg, rot], [(Q_RANK, BF16), (KV_RANK, BF16), (D_ROPE, BF16)], name="mla_prep", lp=lp)


def mla_heads(cqn, ckvn, cos, sin, rot, wqn, wqr, wkn, wv, lp):
    tm = _row_tile(lp)

    def body(cq_ref, ckv_ref, c_ref, s_ref, rot_ref, wqn_ref, wqr_ref, wkn_ref, wv_ref, qn_ref, qr_ref, kn_ref, v_ref):
        cq = cq_ref[...]
        ckv = ckv_ref[...]
        qn_ref[...] = _dot(cq, wqn_ref[...], 1, 0).astype(BF16)
        qr = _dot(cq, wqr_ref[...], 1, 0)
        qr_ref[...] = (qr * c_ref[...] + _rot(qr, rot_ref[...]) * s_ref[...]).astype(BF16)
        kn_ref[...] = _dot(ckv, wkn_ref[...], 1, 0).astype(BF16)
        v_ref[...] = _dot(ckv, wv_ref[...], 1, 0).astype(BF16)

    def row(w):
        return pl.BlockSpec((tm, w), lambda h, i: (i, 0))

    def wspec(k, n):
        return pl.BlockSpec((None, k, n), lambda h, i: (h, 0, 0))

    def ospec(n):
        return pl.BlockSpec((None, tm, n), lambda h, i: (h, i, 0))

    return pl.pallas_call(
        body, name="mla_heads", grid=(N_HEADS, lp // tm),
        in_specs=[row(Q_RANK), row(KV_RANK), row(D_ROPE), row(D_ROPE),
                  pl.BlockSpec((D_ROPE, D_ROPE), lambda h, i: (0, 0)),
                  wspec(Q_RANK, D_NOPE), wspec(Q_RANK, D_ROPE), wspec(KV_RANK, D_NOPE), wspec(KV_RANK, D_V)],
        out_specs=[ospec(D_NOPE), ospec(D_ROPE), ospec(D_NOPE), ospec(D_V)],
        out_shape=[jax.ShapeDtypeStruct((N_HEADS, lp, D_NOPE), BF16), jax.ShapeDtypeStruct((N_HEADS, lp, D_ROPE), BF16),
                   jax.ShapeDtypeStruct((N_HEADS, lp, D_NOPE), BF16), jax.ShapeDtypeStruct((N_HEADS, lp, D_V), BF16)],
        compiler_params=_cparams(),
    )(cqn, ckvn, cos, sin, rot, wqn, wqr, wkn, wv)


def _att_probs(qn, qr, kn, kr, row0, tq, lp):
    s = (_dot(qn, kn, 1, 1) + _dot(qr, kr, 1, 1)) * ATT_SCALE
    qi = row0 + lax.broadcasted_iota(jnp.int32, (tq, lp), 0)
    ki = lax.broadcasted_iota(jnp.int32, (tq, lp), 1)
    s = jnp.where((ki <= qi) & (ki >= PAD), s, -1e30)
    p = jnp.exp(s - jnp.max(s, axis=-1, keepdims=True))
    return p / jnp.sum(p, axis=-1, keepdims=True)


def _att_specs(tq, lp):
    def qspec(n):
        return pl.BlockSpec((None, tq, n), lambda h, i: (h, i, 0))

    def kspec(n):
        return pl.BlockSpec((None, lp, n), lambda h, i: (h, 0, 0))

    return qspec, kspec, pl.BlockSpec((lp, D_ROPE), lambda h, i: (0, 0))


def _att_tile(lp):
    return 272 if lp % 272 == 0 else 128


def attn_fwd(qn, qr, kn, v, kr, lp):
    tq = _att_tile(lp)
    qspec, kspec, krspec = _att_specs(tq, lp)

    def body(qn_ref, qr_ref, kn_ref, v_ref, kr_ref, o_ref):
        i = pl.program_id(1)
        for k in range(lp // tq):
            @pl.when(i == k)
            def _(k=k):
                ke = (k + 1) * tq
                p = _att_probs(qn_ref[...], qr_ref[...], kn_ref[0:ke, :], kr_ref[0:ke, :], k * tq, tq, ke)
                o_ref[...] = _dot(p.astype(BF16), v_ref[0:ke, :], 1, 0).astype(BF16)

    return pl.pallas_call(
        body, name="attn_fwd", grid=(N_HEADS, lp // tq),
        in_specs=[qspec(D_NOPE), qspec(D_ROPE), kspec(D_NOPE), kspec(D_V), krspec],
        out_specs=qspec(D_V), out_shape=jax.ShapeDtypeStruct((N_HEADS, lp, D_V), BF16),
        compiler_params=_cparams(),
    )(qn, qr, kn, v, kr)


def attn_bwd(qn, qr, kn, v, kr, do, lp):
    tq = _att_tile(lp)
    qspec, kspec, krspec = _att_specs(tq, lp)

    def body(qn_ref, qr_ref, kn_ref, v_ref, kr_ref, do_ref, dqn_ref, dqr_ref, dkn_ref, dv_ref, dkr_ref):
        h, i = pl.program_id(0), pl.program_id(1)

        @pl.when(i == 0)
        def _():
            dkn_ref[...] = jnp.zeros_like(dkn_ref)
            dv_ref[...] = jnp.zeros_like(dv_ref)

        @pl.when((i == 0) & (h == 0))
        def _():
            dkr_ref[...] = jnp.zeros_like(dkr_ref)

        for k in range(lp // tq):
            @pl.when(i == k)
            def _(k=k):
                ke = (k + 1) * tq
                qn_, qr_, do_ = qn_ref[...], qr_ref[...], do_ref[...]
                kn_, v_, kr_ = kn_ref[0:ke, :], v_ref[0:ke, :], kr_ref[0:ke, :]
                p = _att_probs(qn_, qr_, kn_, kr_, k * tq, tq, ke)
                dp = _dot(do_, v_, 1, 1)
                delta = jnp.sum(p * dp, axis=-1, keepdims=True)
                ds = (p * (dp - delta) * ATT_SCALE).astype(BF16)
                dqn_ref[...] = _dot(ds, kn_, 1, 0).astype(BF16)
                dqr_ref[...] = _dot(ds, kr_, 1, 0)
                dkn_ref[0:ke, :] += _dot(ds, qn_, 0, 0)
                dv_ref[0:ke, :] += _dot(p.astype(BF16), do_, 0, 0)
                dkr_ref[0:ke, :] += _dot(ds, qr_, 0, 0)

    return pl.pallas_call(
        body, name="attn_bwd", grid=(N_HEADS, lp // tq),
        in_specs=[qspec(D_NOPE), qspec(D_ROPE), kspec(D_NOPE), kspec(D_V), krspec, qspec(D_V)],
        out_specs=[qspec(D_NOPE), qspec(D_ROPE), kspec(D_NOPE), kspec(D_V), krspec],
        out_shape=[jax.ShapeDtypeStruct((N_HEADS, lp, D_NOPE), BF16), jax.ShapeDtypeStruct((N_HEADS, lp, D_ROPE), F32),
                   jax.ShapeDtypeStruct((N_HEADS, lp, D_NOPE), F32), jax.ShapeDtypeStruct((N_HEADS, lp, D_V), F32),
                   jax.ShapeDtypeStruct((lp, D_ROPE), F32)],
        compiler_params=_cparams(),
    )(qn, qr, kn, v, kr, do)


def mla_heads_bwd(dqn, dqr, dkn, dv, cos, sin, rot, wqn, wqr, wkn, wv, lp):
    tm = _row_tile(lp)

    def body(dqn_ref, dqr_ref, dkn_ref, dv_ref, c_ref, s_ref, rot_ref, wqn_ref, wqr_ref, wkn_ref, wv_ref,
             dcq_ref, dckv_ref, dqrp_ref):
        h = pl.program_id(1)
        dqr_ = dqr_ref[...]
        dqrp = (dqr_ * c_ref[...] - _rot(dqr_ * s_ref[...], rot_ref[...])).astype(BF16)
        dqrp_ref[...] = dqrp
        dcq = _dot(dqn_ref[...], wqn_ref[...], 1, 1) + _dot(dqrp, wqr_ref[...], 1, 1)
        dckv = _dot(dkn_ref[...].astype(BF16), wkn_ref[...], 1, 1) + _dot(dv_ref[...].astype(BF16), wv_ref[...], 1, 1)

        @pl.when(h == 0)
        def _():
            dcq_ref[...] = dcq
            dckv_ref[...] = dckv

        @pl.when(h > 0)
        def _():
            dcq_ref[...] += dcq
            dckv_ref[...] += dckv

    def hspec(n):
        return pl.BlockSpec((None, tm, n), lambda i, h: (h, i, 0))

    def row(w):
        return pl.BlockSpec((tm, w), lambda i, h: (i, 0))

    def wspec(k, n):
        return pl.BlockSpec((None, k, n), lambda i, h: (h, 0, 0))

    return pl.pallas_call(
        body, name="mla_heads_bwd", grid=(lp // tm, N_HEADS),
        in_specs=[hspec(D_NOPE), hspec(D_ROPE), hspec(D_NOPE), hspec(D_V), row(D_ROPE), row(D_ROPE),
                  pl.BlockSpec((D_ROPE, D_ROPE), lambda i, h: (0, 0)),
                  wspec(Q_RANK, D_NOPE), wspec(Q_RANK, D_ROPE), wspec(KV_RANK, D_NOPE), wspec(KV_RANK, D_V)],
        out_specs=[row(Q_RANK), row(KV_RANK), hspec(D_ROPE)],
        out_shape=[jax.ShapeDtypeStruct((lp, Q_RANK), F32), jax.ShapeDtypeStruct((lp, KV_RANK), F32),
                   jax.ShapeDtypeStruct((N_HEADS, lp, D_ROPE), BF16)],
        compiler_params=_cparams(),
    )(dqn, dqr, dkn, dv, cos, sin, rot, wqn, wqr, wkn, wv)


def _rms_bwd(dy, x, g):
    r = lax.rsqrt(jnp.mean(x * x, axis=-1, keepdims=True) + RMS_EPS)
    n = x * r
    dn = dy * g
    dx = r * (dn - n * jnp.mean(dn * n, axis=-1, keepdims=True))
    return dx, jnp.sum(dy * n, axis=0, keepdims=True)


def mla_prep_bwd(dcq, dckv, dkr, proj, cos, sin, rot, qg, kvg, lp):
    def fn(row0, rv, pv):
        dcq_, dckv_, dkr_, cq, ckv, c, s = rv
        qg_, kvg_, rot_ = pv
        dxq, dgq = _rms_bwd(dcq_, cq, qg_)
        dxkv, dgkv = _rms_bwd(dckv_, ckv, kvg_)
        dkr_raw = dkr_ * c - _rot(dkr_ * s, rot_)
        return ((dxq, dxkv, dkr_raw), (dgq, dgkv))

    return rowwise(fn, [(dcq, Q_RANK, 0), (dckv, KV_RANK, 0), (dkr, D_ROPE, 0), (proj, Q_RANK, 0), (proj, KV_RANK, 2),
                        (cos, D_ROPE, 0), (sin, D_ROPE, 0)], [qg, kvg, rot],
                   [(Q_RANK, BF16), (KV_RANK, BF16), (D_ROPE, BF16)], [((1, Q_RANK), F32), ((1, KV_RANK), F32)],
                   name="mla_prep_bwd", lp=lp)


def _shift_down(x, d, rows):
    return jnp.where(rows >= d, pltpu.roll(x, d, 0), 0.0)


def _shift_up(x, d, rows, n):
    return jnp.where(rows < n - d, pltpu.roll(x, n - d, 0), 0.0)


_CONV_W = 128
_XB, _BG, _CG = 1024 // _CONV_W, 1536 // _CONV_W, 2048 // _CONV_W


def _conv_specs(lp):
    def pspec(base):
        return pl.BlockSpec((lp, _CONV_W), functools.partial(lambda c, base: (0, base + c), base=base))

    col = pl.BlockSpec((lp, _CONV_W), lambda c: (0, c))
    wspec = pl.BlockSpec((3, _CONV_W), lambda c: (0, c))
    bspec = pl.BlockSpec((1, _CONV_W), lambda c: (0, c))
    return pspec, col, wspec, bspec


def _conv_core(xbar, cg, w, bias, lp):
    rows = lax.broadcasted_iota(jnp.int32, (lp, _CONV_W), 0)
    u = jnp.where(rows >= PAD, cg * xbar, 0.0)
    u1 = _shift_down(u, 1, rows)
    u2 = _shift_down(u, 2, rows)
    y = bias + w[0:1] * u2 + w[1:2] * u1 + w[2:3] * u
    return rows, u, u1, u2, y


def conv_fwd(proj, w, bias, lp):
    pspec, col, wspec, bspec = _conv_specs(lp)

    def body(x_ref, b_ref, c_ref, w_ref, bias_ref, v_ref):
        _, _, _, _, y = _conv_core(x_ref[...], c_ref[...], w_ref[...], bias_ref[...], lp)
        v_ref[...] = (b_ref[...] * y).astype(BF16)

    return pl.pallas_call(
        body, name="conv_fwd", grid=(MIX // _CONV_W,),
        in_specs=[pspec(_XB), pspec(_BG), pspec(_CG), wspec, bspec], out_specs=col,
        out_shape=jax.ShapeDtypeStruct((lp, MIX), BF16), compiler_params=_cparams(),
    )(proj, proj, proj, w, bias)


def conv_bwd(dv, proj, w, bias, lp):
    pspec, col, wspec, bspec = _conv_specs(lp)

    def body(dv_ref, x_ref, b_ref, c_ref, w_ref, bias_ref, dx_ref, db_ref, dc_ref, dw_ref, dbias_ref):
        xbar, cg, w_ = x_ref[...], c_ref[...], w_ref[...]
        rows, u, u1, u2, y = _conv_core(xbar, cg, w_, bias_ref[...], lp)
        dv_ = dv_ref[...]
        db_ref[...] = (dv_ * y).astype(BF16)
        dy = dv_ * b_ref[...]
        dbias_ref[...] = jnp.sum(dy, axis=0, keepdims=True)
        dw_ref[0:1, :] = jnp.sum(dy * u2, axis=0, keepdims=True)
        dw_ref[1:2, :] = jnp.sum(dy * u1, axis=0, keepdims=True)
        dw_ref[2:3, :] = jnp.sum(dy * u, axis=0, keepdims=True)
        du = w_[2:3] * dy + w_[1:2] * _shift_up(dy, 1, rows, lp) + w_[0:1] * _shift_up(dy, 2, rows, lp)
        du = jnp.where(rows >= PAD, du, 0.0)
        dc_ref[...] = (du * xbar).astype(BF16)
        dx_ref[...] = (du * cg).astype(BF16)

    return pl.pallas_call(
        body, name="conv_bwd", grid=(MIX // _CONV_W,),
        in_specs=[col, pspec(_XB), pspec(_BG), pspec(_CG), wspec, bspec],
        out_specs=[col, col, col, wspec, bspec],
        out_shape=[jax.ShapeDtypeStruct((lp, MIX), BF16)] * 3 + [jax.ShapeDtypeStruct((3, MIX), F32),
                                                                jax.ShapeDtypeStruct((1, MIX), F32)],
        compiler_params=_cparams(),
    )(dv, proj, proj, proj, w, bias)


def _s5_disc(a_re, a_im, log_dt, b_re, b_im):
    dt = jnp.exp(log_dt)
    mag = jnp.exp(dt * a_re)
    ab_re, ab_im = mag * jnp.cos(dt * a_im), mag * jnp.sin(dt * a_im)
    den = a_re * a_re + a_im * a_im
    nr, ni = ab_re - 1.0, ab_im
    coef_re = (nr * a_re + ni * a_im) / den
    coef_im = (ni * a_re - nr * a_im) / den
    return ab_re, ab_im, coef_re * b_re - coef_im * b_im, coef_re * b_im + coef_im * b_re


_S5_ROWS = S5_GROUPS * S5_GROUP


def s5_prep(a_re, a_im, log_dt, b_re, b_im):
    def body(ar, ai, ld, br, bi, o0, o1, o2, o3):
        for o, v in zip((o0, o1, o2, o3), _s5_disc(ar[...], ai[...], ld[...], br[...], bi[...])):
            o[...] = v

    return pl.pallas_call(body, name="s5_prep",
                          out_shape=[jax.ShapeDtypeStruct((_S5_ROWS, S5_STATE), F32)] * 4)(a_re, a_im, log_dt, b_re, b_im)


def s5_prep_bwd(a_re, a_im, log_dt, b_re, b_im, d_ab_re, d_ab_im, d_bb_re, d_bb_im, sel):
    def body(ar, ai, ld, br, bi, g0, g1, g2, g3, sel_ref, da_re, da_im, dld, dbr, dbi):
        _, vjp = jax.vjp(_s5_disc, ar[...], ai[...], ld[...], br[...], bi[...])
        c_ar, c_ai, c_ld, c_br, c_bi = vjp((g0[...], g1[...], g2[...], g3[...]))
        s = sel_ref[...]
        hi = lax.Precision.HIGHEST
        da_re[...] = _dot(s, c_ar, 1, 0, precision=hi)
        da_im[...] = _dot(s, c_ai, 1, 0, precision=hi)
        dld[...] = jnp.sum(_dot(s, c_ld, 1, 0, precision=hi), axis=-1, keepdims=True)
        dbr[...] = c_br
        dbi[...] = c_bi

    g = jax.ShapeDtypeStruct((S5_GROUPS, S5_STATE), F32)
    full = jax.ShapeDtypeStruct((_S5_ROWS, S5_STATE), F32)
    return pl.pallas_call(body, name="s5_prep_bwd",
                          out_shape=[g, g, jax.ShapeDtypeStruct((S5_GROUPS, 1), F32), full, full],
                          )(a_re, a_im, log_dt, b_re, b_im, d_ab_re, d_ab_im, d_bb_re, d_bb_im, sel)


_SCAN_W = 128
_SCAN_STEPS = int(math.log2(SCAN_CHUNK))


def _cmul(ar, ai, br, bi):
    return ar * br - ai * bi, ar * bi + ai * br


def _scan_powers(ar, ai, reverse):
    pw = [(ar, ai)]
    for _ in range(_SCAN_STEPS):
        pw.append(_cmul(*pw[-1], *pw[-1]))
    rows = lax.broadcasted_iota(jnp.int32, (SCAN_CHUNK, ar.shape[-1]), 0)
    tr = jnp.broadcast_to(ar, rows.shape)
    ti = jnp.broadcast_to(ai, rows.shape)
    for k in range(_SCAN_STEPS):
        d = 2 ** k
        if reverse:
            live = rows < SCAN_CHUNK - d
            mr, mi = _cmul(tr, ti, _shift_up(tr, d, rows, SCAN_CHUNK), _shift_up(ti, d, rows, SCAN_CHUNK))
        else:
            live = rows >= d
            mr, mi = _cmul(tr, ti, _shift_down(tr, d, rows), _shift_down(ti, d, rows))
        tr = jnp.where(live, mr, tr)
        ti = jnp.where(live, mi, ti)
    return pw, rows, tr, ti


def s5_scan(bu, ab_re, ab_im, lp):
    n_chunks = lp // SCAN_CHUNK

    def body(bu_ref, ar_ref, ai_ref, s_ref):
        ar, ai = ar_ref[...], ai_ref[...]
        pw, rows, tr, ti = _scan_powers(ar, ai, False)

        def chunk(ci, carry):
            cr, cim = carry
            r0 = pl.multiple_of(ci * SCAN_CHUNK, SCAN_CHUNK)
            xr = bu_ref[0, pl.ds(r0, SCAN_CHUNK), :]
            xi = bu_ref[1, pl.ds(r0, SCAN_CHUNK), :]
            for k in range(_SCAN_STEPS):
                d = 2 ** k
                mr, mi = _cmul(pw[k][0], pw[k][1], _shift_down(xr, d, rows), _shift_down(xi, d, rows))
                xr, xi = xr + mr, xi + mi
            mr, mi = _cmul(tr, ti, cr, cim)
            xr, xi = xr + mr, xi + mi
            s_ref[0, pl.ds(r0, SCAN_CHUNK), :] = xr
            s_ref[1, pl.ds(r0, SCAN_CHUNK), :] = xi
            return xr[SCAN_CHUNK - 1:SCAN_CHUNK, :], xi[SCAN_CHUNK - 1:SCAN_CHUNK, :]

        zero = jnp.zeros((1, _SCAN_W), F32)
        lax.fori_loop(0, n_chunks, chunk, (zero, zero))

    spec = pl.BlockSpec((2, lp, _SCAN_W), lambda c: (0, 0, c))
    aspec = pl.BlockSpec((1, _SCAN_W), lambda c: (0, c))
    return pl.pallas_call(
        body, name="s5_scan", grid=(S5_LANES // _SCAN_W,), in_specs=[spec, aspec, aspec], out_specs=spec,
        out_shape=jax.ShapeDtypeStruct((2, lp, S5_LANES), F32), compiler_params=_cparams(),
    )(bu, ab_re, ab_im)


def s5_scan_bwd(ds, s, ab_re, ab_im, lp):
    n_chunks = lp // SCAN_CHUNK

    def body(ds_ref, s_ref, ar_ref, ai_ref, g_ref, da_ref):
        ar, ai = ar_ref[...], -ai_ref[...]
        pw, rows, tr, ti = _scan_powers(ar, ai, True)

        def chunk(k, carry):
            cr, cim, dar, dai = carry
            ci = n_chunks - 1 - k
            r0 = pl.multiple_of(ci * SCAN_CHUNK, SCAN_CHUNK)
            xr = ds_ref[0, pl.ds(r0, SCAN_CHUNK), :]
            xi = ds_ref[1, pl.ds(r0, SCAN_CHUNK), :]
            for j in range(_SCAN_STEPS):
                d = 2 ** j
                mr, mi = _cmul(pw[j][0], pw[j][1], _shift_up(xr, d, rows, SCAN_CHUNK), _shift_up(xi, d, rows, SCAN_CHUNK))
                xr, xi = xr + mr, xi + mi
            mr, mi = _cmul(tr, ti, cr, cim)
            xr, xi = xr + mr, xi + mi
            g_ref[0, pl.ds(r0, SCAN_CHUNK), :] = xr
            g_ref[1, pl.ds(r0, SCAN_CHUNK), :] = xi
            prev0 = pl.multiple_of(jnp.maximum(r0 - 8, 0), 8)
            live = (ci > 0).astype(F32)
            pr = s_ref[0, pl.ds(prev0, 8), :][7:8, :] * live
            pim = s_ref[1, pl.ds(prev0, 8), :][7:8, :] * live
            sr = s_ref[0, pl.ds(r0, SCAN_CHUNK), :]
            si = s_ref[1, pl.ds(r0, SCAN_CHUNK), :]
            sr = jnp.where(rows >= 1, pltpu.roll(sr, 1, 0), pr)
            si = jnp.where(rows >= 1, pltpu.roll(si, 1, 0), pim)
            dar = dar + jnp.sum(xr * sr + xi * si, axis=0, keepdims=True)
            dai = dai + jnp.sum(xi * sr - xr * si, axis=0, keepdims=True)
            return xr[0:1, :], xi[0:1, :], dar, dai

        zero = jnp.zeros((1, _SCAN_W), F32)
        _, _, dar, dai = lax.fori_loop(0, n_chunks, chunk, (zero, zero, zero, zero))
        da_ref[0] = dar
        da_ref[1] = dai

    spec = pl.BlockSpec((2, lp, _SCAN_W), lambda c: (0, 0, c))
    aspec = pl.BlockSpec((1, _SCAN_W), lambda c: (0, c))
    return pl.pallas_call(
        body, name="s5_scan_bwd", grid=(S5_LANES // _SCAN_W,), in_specs=[spec, spec, aspec, aspec],
        out_specs=[spec, pl.BlockSpec((2, 1, _SCAN_W), lambda c: (0, 0, c))],
        out_shape=[jax.ShapeDtypeStruct((2, lp, S5_LANES), F32), jax.ShapeDtypeStruct((2, 1, S5_LANES), F32)],
        compiler_params=_cparams(),
    )(ds, s, ab_re, ab_im)


def _blockdiag(x):
    g, r, c = x.shape
    eye = jnp.eye(g, dtype=x.dtype)
    return (x[:, :, None, :] * eye[:, None, :, None]).reshape(g * r, g * c)


def _blockdiag_extract(m, g, r, c):
    return jnp.einsum('grgc->grc', m.reshape(g, r, g, c))


def s5_u(proj, lp):
    def fn(row0, rv, pv):
        u, = rv
        return ((jnp.where(_row_mask(row0, u.shape), u, 0.0),), ())

    return rowwise(fn, [(proj, MIX, 5)], [], [(MIX, BF16)], name="s5_u", lp=lp)[0]


def s5_y(ys, proj, d, lp):
    def fn(row0, rv, pv):
        ys_, u = rv
        y = ys_ + pv[0] * u
        return ((y, _gelu(y)), ())

    return rowwise(fn, [(ys, MIX, 0), (proj, MIX, 5)], [d], [(MIX, F32), (MIX, BF16)], name="s5_y", lp=lp)


def s5_glu(z, y, b, lp):
    def fn(row0, rv, pv):
        z_, y_ = rv
        return ((_gelu(y_) * _sigmoid(z_ + pv[0]),), ())

    return rowwise(fn, [(z, MIX, 0), (y, MIX, 0)], [b], [(MIX, BF16)], name="s5_glu", lp=lp)[0]


def s5_glu_bwd(dgl, z, y, b, lp):
    def fn(row0, rv, pv):
        dgl_, z_, y_ = rv
        sg = _sigmoid(z_ + pv[0])
        dz = dgl_ * _gelu(y_) * sg * (1.0 - sg)
        return ((dgl_ * sg, dz), (jnp.sum(dz, axis=0, keepdims=True),))

    return rowwise(fn, [(dgl, MIX, 0), (z, MIX, 0), (y, MIX, 0)], [b], [(MIX, F32), (MIX, BF16)], [((1, MIX), F32)],
                   name="s5_glu_bwd", lp=lp)


def s5_y_bwd(dyg, y, proj, d, lp):
    def fn(row0, rv, pv):
        dyg_, y_, u = rv
        dy = dyg_ * _gelu_grad(y_)
        return ((dy, dy * pv[0]), (jnp.sum(dy * u, axis=0, keepdims=True),))

    return rowwise(fn, [(dyg, MIX, 0), (y, MIX, 0), (proj, MIX, 5)], [d], [(MIX, BF16), (MIX, F32)], [((1, MIX), F32)],
                   name="s5_y_bwd", lp=lp)


def s5_du(du, lp):
    def fn(row0, rv, pv):
        return ((jnp.where(_row_mask(row0, rv[0].shape), rv[0], 0.0),), ())

    return rowwise(fn, [(du, MIX, 0)], [], [(MIX, BF16)], name="s5_du", lp=lp)[0]


def merge_fwd(proj, ya, yb, yc, lp):
    def fn(row0, rv, pv):
        g0, g1, g2, a, b, c = rv
        return ((_sigmoid(g0) * a + _sigmoid(g1) * b + _sigmoid(g2) * c,), ())

    return rowwise(fn, [(proj, D_MODEL, 3), (proj, D_MODEL, 4), (proj, D_MODEL, 5), (ya, D_MODEL, 0), (yb, D_MODEL, 0),
                        (yc, D_MODEL, 0)], [], [(D_MODEL, BF16)], name="merge_fwd", lp=lp)[0]


def merge_bwd(dmix, proj, ya, yb, yc, lp):
    def fn(row0, rv, pv):
        dm, g0, g1, g2, a, b, c = rv
        outs_y, outs_g = [], []
        for g, yv in ((g0, a), (g1, b), (g2, c)):
            sg = _sigmoid(g)
            outs_y.append(dm * sg)
            outs_g.append(dm * yv * sg * (1.0 - sg))
        return (tuple(outs_y) + tuple(outs_g), ())

    return rowwise(fn, [(dmix, D_MODEL, 0), (proj, D_MODEL, 3), (proj, D_MODEL, 4), (proj, D_MODEL, 5),
                        (ya, D_MODEL, 0), (yb, D_MODEL, 0), (yc, D_MODEL, 0)], [], [(D_MODEL, BF16)] * 6,
                   name="merge_bwd", lp=lp)


def loss_head(h, tgt, lp):
    def fn(row0, rv, pv):
        h_, t_ = rv
        live = (row0 + lax.broadcasted_iota(jnp.int32, h_.shape, 0)) >= X0
        diff = jnp.where(live, h_ - t_, 0.0)
        ssq = jnp.sum(jnp.sum(diff * diff, axis=1, keepdims=True), axis=0, keepdims=True)
        return ((diff * (1.0 / D_MODEL),), (ssq * (0.5 / D_MODEL),))

    return rowwise(fn, [(h, D_MODEL, 0), (tgt, D_MODEL, 0)], [], [(D_MODEL, F32)], [((1, 1), F32)], name="loss_head", lp=lp)


def _s5_consts(W):
    ab_re_rep, ab_im_rep, bb_re, bb_im = s5_prep(W['s5_a_re'], W['s5_a_im'], W['s5_log_dt'], W['s5_b_re'], W['s5_b_im'])
    pick = lambda t: t.reshape(S5_GROUPS, S5_GROUP, S5_STATE)[:, 0].reshape(1, S5_LANES)
    bb = jnp.stack([_blockdiag(bb_re.reshape(S5_GROUPS, S5_GROUP, S5_STATE)),
                    _blockdiag(bb_im.reshape(S5_GROUPS, S5_GROUP, S5_STATE))]).astype(BF16)
    return pick(ab_re_rep), pick(ab_im_rep), bb


def layer_fwd(h, hb, W, tabs, lp):
    cos, sin, rot = tabs
    h1, h1b, sv1 = ffn_fwd(h, hb, W['wg1'], W['wu1'], W['wd1'], W['ln1_g'], W['ln1_b'], lp)
    proj = matmul(h1b[None], W['w_in'][None], name="proj")[0]
    cqn, ckvn, kr = mla_prep(proj, cos, sin, rot, W['q_norm_g'], W['kv_norm_g'], lp)
    qn, qr, kn, v = mla_heads(cqn, ckvn, cos, sin, rot, W['wqn'], W['wqr'], W['wkn'], W['wv'], lp)
    o = attn_fwd(qn, qr, kn, v, kr, lp)
    ya = matmul(o, W['mla_wo'], ab='r', bb='r', name="mla_out")[0]
    vconv = conv_fwd(proj, W['conv_w'], W['conv_b'], lp)
    yb = matmul(vconv[None], W['conv_wout'][None], name="conv_out")[0]
    ub = s5_u(proj, lp)
    ab_re, ab_im, bb = _s5_consts(W)
    bu = matmul(ub[None], bb, bb='o', name="s5_bu")
    s = s5_scan(bu, ab_re, ab_im, lp)
    ys = matmul(s, W['s5_ct'], ab='r', bb='r', name="s5_cs")[0]
    y, ygb = s5_y(ys, proj, W['s5_d'], lp)
    zg = matmul(ygb[None], W['s5_wglu'][None], name="s5_glu_mm")[0]
    glb = s5_glu(zg, y, W['s5_b_glu'], lp)
    yc = matmul(glb[None], W['s5_wout'][None], name="s5_out")[0]
    mixed = merge_fwd(proj, ya, yb, yc, lp)
    z2, h2, h2b = mm_res_ln(mixed[None], W['w_o'][None], h1, W['ln2_g'], W['ln2_b'], scale=1.0, name="wo_ln", lp=lp)
    h3, h3b, sv3 = ffn_fwd(h2, h2b, W['wg2'], W['wu2'], W['wd2'], W['ln3_g'], W['ln3_b'], lp)
    sv = dict(sv1=sv1, sv3=sv3, h1b=h1b, proj=proj, cqn=cqn, ckvn=ckvn, kr=kr, qn=qn, qr=qr, kn=kn, v=v, o=o, ya=ya,
              vconv=vconv, yb=yb, ub=ub, ab_re=ab_re, ab_im=ab_im, bb=bb, s=s, y=y, ygb=ygb, zg=zg, glb=glb, yc=yc,
              mixed=mixed, z2=z2)
    return h3, h3b, sv


def layer_bwd(dh3, sv, W, tabs, lp):
    cos, sin, rot = tabs
    proj = sv['proj']
    G = {}
    dh2, g3 = ffn_bwd(dh3, sv['sv3'], W['wg2'], W['wu2'], W['wd2'], W['ln3_g'], lp)
    G.update(wg2=g3['wg'], wu2=g3['wu'], wd2=g3['wd'], ln3_g=g3['ln_g'], ln3_b=g3['ln_b'])
    dz2, dz2b, G['ln2_g'], G['ln2_b'] = ln_bwd(dh2, sv['z2'], W['ln2_g'], fscale=1.0, name="wo_ln_bwd", lp=lp)
    dmix = matmul(dz2b[None], W['w_o'][None], tb=True, name="wo_dx")[0]
    G['w_o'] = matmul(sv['mixed'][None], dz2b[None], ta=True, out_dtype=WGRAD, name="wo_dw")[0]
    dya, dyb, dyc, dg0, dg1, dg2 = merge_bwd(dmix, proj, sv['ya'], sv['yb'], sv['yc'], lp)
    dgl = matmul(dyc[None], W['s5_wout'][None], tb=True, name="s5_out_dx")[0]
    G['s5_wout'] = matmul(sv['glb'][None], dyc[None], ta=True, out_dtype=WGRAD, name="s5_out_dw")[0]
    t1, dzb, G['s5_b_glu'] = s5_glu_bwd(dgl, sv['zg'], sv['y'], W['s5_b_glu'], lp)
    dyg = matmul(dzb[None], W['s5_wglu'][None], tb=True, res=t1[None], name="s5_glu_dx")[0]
    G['s5_wglu'] = matmul(sv['ygb'][None], dzb[None], ta=True, out_dtype=WGRAD, name="s5_glu_dw")[0]
    dyb_, du_d, G['s5_d'] = s5_y_bwd(dyg, sv['y'], proj, W['s5_d'], lp)
    ds = matmul(dyb_[None], W['s5_ct'], tb=True, bb='o', name="s5_cs_dx")
    G['s5_ct'] = matmul(sv['s'], dyb_[None], ta=True, ab='o', name="s5_cs_dw")
    g_adj, d_ab = s5_scan_bwd(ds, sv['s'], sv['ab_re'], sv['ab_im'], lp)
    du = matmul(g_adj, sv['bb'], tb=True, ab='r', bb='r', res=du_d[None], name="s5_bu_dx")[0]
    d_bb = matmul(sv['ub'][None], g_adj, ta=True, bb='o', name="s5_bu_dw")
    du_b = s5_du(du, lp)
    onehot = (jnp.arange(S5_GROUP) == 0).astype(F32)
    spread = lambda t: (t.reshape(S5_GROUPS, 1, S5_STATE) * onehot[None, :, None]).reshape(_S5_ROWS, S5_STATE)
    take = lambda t: _blockdiag_extract(t, S5_GROUPS, S5_GROUP, S5_STATE).reshape(_S5_ROWS, S5_STATE)
    sel = jnp.kron(jnp.eye(S5_GROUPS, dtype=F32), jnp.ones((1, S5_GROUP), F32))
    (G['s5_a_re'], G['s5_a_im'], G['s5_log_dt'], G['s5_b_re'], G['s5_b_im']) = s5_prep_bwd(
        W['s5_a_re'], W['s5_a_im'], W['s5_log_dt'], W['s5_b_re'], W['s5_b_im'],
        spread(d_ab[0]), spread(d_ab[1]), take(d_bb[0]), take(d_bb[1]), sel)
    dv = matmul(dyb[None], W['conv_wout'][None], tb=True, name="conv_out_dx")[0]
    G['conv_wout'] = matmul(sv['vconv'][None], dyb[None], ta=True, out_dtype=WGRAD, name="conv_out_dw")[0]
    dxbar, dbg, dcg, G['conv_w'], G['conv_b'] = conv_bwd(dv, proj, W['conv_w'], W['conv_b'], lp)
    do = matmul(dya[None], W['mla_wo'], tb=True, bb='o', out_dtype=BF16, name="mla_out_dx")
    G['mla_wo'] = matmul(sv['o'], dya[None], ta=True, ab='o', out_dtype=WGRAD, name="mla_out_dw")
    dqn, dqr, dkn, dvv, dkr = attn_bwd(sv['qn'], sv['qr'], sv['kn'], sv['v'], sv['kr'], do, lp)
    dcq, dckv, dqrp = mla_heads_bwd(dqn, dqr, dkn, dvv, cos, sin, rot, W['wqn'], W['wqr'], W['wkn'], W['wv'], lp)
    G['wqn'] = matmul(sv['cqn'][None], dqn, ta=True, bb='o', out_dtype=WGRAD, name="mla_dwqn")
    G['wqr'] = matmul(sv['cqn'][None], dqrp, ta=True, bb='o', out_dtype=WGRAD, name="mla_dwqr")
    G['wkn'] = matmul(sv['ckvn'][None], dkn, ta=True, bb='o', out_dtype=WGRAD, name="mla_dwkn")
    G['wv'] = matmul(sv['ckvn'][None], dvv, ta=True, bb='o', out_dtype=WGRAD, name="mla_dwv")
    dcq_raw, dckv_raw, dkr_raw, G['q_norm_g'], G['kv_norm_g'] = mla_prep_bwd(
        dcq, dckv, dkr, proj, cos, sin, rot, W['q_norm_g'], W['kv_norm_g'], lp)
    zeros = lambda n: jnp.zeros((lp, n), BF16)
    dproj = jnp.concatenate([dcq_raw, dkr_raw, zeros(96), dckv_raw, zeros(256), dxbar, dbg, dcg, du_b, dg0, dg1, dg2], axis=1)
    dh1 = matmul(dproj[None], W['w_in'][None], tb=True, res=dz2[None], res_scale=ALPHA, name="proj_dx")[0]
    G['w_in'] = matmul(sv['h1b'][None], dproj[None], ta=True, out_dtype=WGRAD, name="proj_dw")[0]
    dh0, g1 = ffn_bwd(dh1, sv['sv1'], W['wg1'], W['wu1'], W['wd1'], W['ln1_g'], lp)
    G.update(wg1=g1['wg'], wu1=g1['wu'], wd1=g1['wd'], ln1_g=g1['ln_g'], ln1_b=g1['ln_b'])
    return dh0, G


def _nat_cols(st):
    return jnp.transpose(st, (1, 0, 2)).reshape(st.shape[1], -1)


def _shard_cols(nat):
    k, n = nat.shape
    return jnp.transpose(nat.reshape(k, N_SHARD, n // N_SHARD), (1, 0, 2))


def _win_pad(w):
    z = lambda n: jnp.zeros((w.shape[0], n), w.dtype)
    return jnp.concatenate([w[:, 0:384], w[:, 640:672], z(96), w[:, 384:640], z(256), w[:, 672:]], axis=1)


def _win_unpad(wp):
    return jnp.concatenate([wp[:, 0:384], wp[:, 512:768], wp[:, 384:416], wp[:, 1024:]], axis=1)


_BIG = [('ffn1_w_gate', 1), ('ffn1_w_up', 1), ('ffn1_w_down', 0), ('w_in', 1), ('mla_w_uq', 1), ('mla_w_ukv', 1),
        ('mla_w_o', 1), ('conv_w_out', 1), ('s5_w_glu', 0), ('s5_w_out', 1), ('w_o', 0),
        ('ffn2_w_gate', 1), ('ffn2_w_up', 1), ('ffn2_w_down', 0)]
_REPL = ['ln1_g', 'ln1_b', 'mla_q_norm_g', 'mla_kv_norm_g', 'conv_b', 's5_a_re', 's5_a_im', 's5_log_dt', 's5_b_re',
         's5_b_im', 's5_c_re', 's5_c_im', 's5_d', 's5_b_glu', 'ln2_g', 'ln2_b', 'ln3_g', 'ln3_b']


def compute_weights(st, small):
    W = {}
    for t in ('1', '2'):
        W['wg' + t], W['wu' + t] = st['ffn%s_w_gate' % t], st['ffn%s_w_up' % t]
        W['wd' + t] = st['ffn%s_w_down' % t]
    W['w_in'] = _win_pad(_nat_cols(st['w_in']))
    uq = jnp.transpose(_nat_cols(st['mla_w_uq']).reshape(Q_RANK, N_HEADS, D_NOPE + D_ROPE), (1, 0, 2))
    W['wqn'], W['wqr'] = uq[:, :, :D_NOPE], uq[:, :, D_NOPE:]
    ukv = jnp.transpose(_nat_cols(st['mla_w_ukv']).reshape(KV_RANK, N_HEADS, D_NOPE + D_V), (1, 0, 2))
    W['wkn'], W['wv'] = ukv[:, :, :D_NOPE], ukv[:, :, D_NOPE:]
    W['mla_wo'] = _nat_cols(st['mla_w_o']).reshape(N_HEADS, D_V, D_MODEL)
    W['conv_wout'] = _nat_cols(st['conv_w_out'])
    W['s5_wglu'] = st['s5_w_glu'].reshape(MIX, MIX)
    W['s5_wout'] = _nat_cols(st['s5_w_out'])
    W['w_o'] = st['w_o'].reshape(D_MODEL, D_MODEL)
    W['conv_w'] = small['conv_w']
    for n in ('ln1_g', 'ln1_b', 'ln2_g', 'ln2_b', 'ln3_g', 'ln3_b', 'conv_b', 's5_b_glu'):
        W[n] = small[n].reshape(1, -1)
    W['q_norm_g'] = small['mla_q_norm_g'].reshape(1, -1)
    W['kv_norm_g'] = small['mla_kv_norm_g'].reshape(1, -1)
    W['s5_d'] = small['s5_d'].reshape(1, MIX)
    rep = lambda t: jnp.repeat(t, S5_GROUP, axis=0)
    W['s5_a_re'], W['s5_a_im'] = rep(small['s5_a_re']), rep(small['s5_a_im'])
    W['s5_log_dt'] = jnp.broadcast_to(rep(small['s5_log_dt'].reshape(S5_GROUPS, 1)), (_S5_ROWS, S5_STATE))
    tr = lambda t: jnp.transpose(t, (0, 2, 1)).reshape(_S5_ROWS, S5_STATE)
    W['s5_b_re'], W['s5_b_im'] = tr(small['s5_b_re']), tr(small['s5_b_im'])
    ct = lambda t: _blockdiag(jnp.transpose(t, (0, 2, 1)))
    W['s5_ct'] = jnp.stack([ct(small['s5_c_re']), -ct(small['s5_c_im'])]).astype(BF16)
    return W


def shard_grads(G):
    R = reference_grads(G, ffn=False)
    S = {n: (_shard_of(R[n], a) if n in R else None) for n, a in _BIG}
    for t in ('1', '2'):
        S['ffn%s_w_gate' % t] = G['wg' + t]
        S['ffn%s_w_up' % t] = G['wu' + t]
        S['ffn%s_w_down' % t] = G['wd' + t]
    S['conv_w'] = _shard_cols(R['conv_w'])
    for n in _REPL:
        S[n] = R[n]
    return S


def reference_grads(G, ffn=True):
    R = {}
    for t in ('1', '2') if ffn else ():
        R['ffn%s_w_gate' % t] = _nat_cols(G['wg' + t])
        R['ffn%s_w_up' % t] = _nat_cols(G['wu' + t])
        R['ffn%s_w_down' % t] = G['wd' + t].reshape(D_FF, D_MODEL)
    R['w_in'] = _win_unpad(G['w_in'])
    R['mla_w_uq'] = jnp.transpose(jnp.concatenate([G['wqn'], G['wqr']], axis=2), (1, 0, 2)).reshape(Q_RANK, -1)
    R['mla_w_ukv'] = jnp.transpose(jnp.concatenate([G['wkn'], G['wv']], axis=2), (1, 0, 2)).reshape(KV_RANK, -1)
    R['mla_w_o'] = G['mla_wo'].reshape(N_HEADS * D_V, D_MODEL)
    R['conv_w'], R['conv_w_out'] = G['conv_w'], G['conv_wout']
    R['s5_w_glu'], R['s5_w_out'], R['w_o'] = G['s5_wglu'], G['s5_wout'], G['w_o']
    for n in ('ln1_g', 'ln1_b', 'ln2_g', 'ln2_b', 'ln3_g', 'ln3_b', 'conv_b', 's5_b_glu'):
        R[n] = G[n].reshape(-1)
    R['mla_q_norm_g'], R['mla_kv_norm_g'] = G['q_norm_g'].reshape(-1), G['kv_norm_g'].reshape(-1)
    R['s5_d'] = G['s5_d'].reshape(S5_GROUPS, S5_GROUP)
    R['s5_a_re'], R['s5_a_im'], R['s5_log_dt'] = G['s5_a_re'], G['s5_a_im'], G['s5_log_dt'].reshape(-1)
    untr = lambda t: jnp.transpose(t.reshape(S5_GROUPS, S5_GROUP, S5_STATE), (0, 2, 1))
    R['s5_b_re'], R['s5_b_im'] = untr(G['s5_b_re']), untr(G['s5_b_im'])
    unct = lambda t: jnp.transpose(_blockdiag_extract(t, S5_GROUPS, S5_STATE, S5_GROUP), (0, 2, 1))
    R['s5_c_re'], R['s5_c_im'] = unct(G['s5_ct'][0]), -unct(G['s5_ct'][1])
    return R


def local_step(x2d, tgt2d, meta, Ws):
    lp = x2d.shape[0] + X0
    tabs = _rope_tables(lp)
    h = jnp.concatenate([jnp.zeros((PAD, D_MODEL), F32), meta, x2d], axis=0)
    hb = h.astype(BF16)
    saved = []
    for W in Ws:
        h, hb, sv = layer_fwd(h, hb, W, tabs, lp)
        saved.append(sv)
    tgt = jnp.concatenate([jnp.zeros((X0, D_MODEL), F32), tgt2d], axis=0)
    dh, loss = loss_head(h, tgt, lp)
    grads = [None] * len(Ws)
    for li in reversed(range(len(Ws))):
        dh, grads[li] = layer_bwd(dh, saved[li], Ws[li], tabs, lp)
    return loss, dh[X0:], dh[PAD:X0], grads


_ANY = pl.BlockSpec(memory_space=pl.ANY)
LANES = 1024


def _half_rows(n):
    rows = -(-n // (2 * LANES))
    q = 512 if rows > 512 else 16
    return -(-rows // q) * q


def _place():
    x, y, c = lax.axis_index("x"), lax.axis_index("y"), lax.axis_index("c")
    chips = [(1 - x, y), (x, 1 - y), (1 - x, 1 - y)]
    return x, y, c, chips


def all_gather_halves(src):
    def body(src_ref, out_ref, send_sems, recv_sems, local_sem):
        x, y, c, chips = _place()
        me = 2 * x + y
        sibling = (x, y, 1 - c)

        def copy(k, src, chip_idx, half, to):
            return pltpu.make_async_remote_copy(src_ref=src, dst_ref=out_ref.at[chip_idx, half], send_sem=send_sems.at[k],
                                                recv_sem=recv_sems.at[k], device_id=to, device_id_type=MESH)

        mine = pltpu.make_async_copy(src_ref, out_ref.at[me], local_sem)
        mine.start()
        first = [copy(j, src_ref.at[c], me, c, (*chip, c)) for j, chip in enumerate(chips)]
        for cp in first:
            cp.start()
        passed = []
        for j, chip in enumerate(chips):
            idx = 2 * chip[0] + chip[1]
            copy(j, src_ref.at[c], idx, c, sibling).wait_recv()
            cp = copy(3 + j, out_ref.at[idx, c], idx, c, sibling)
            cp.start()
            passed.append(cp)
        for j, chip in enumerate(chips):
            copy(3 + j, src_ref.at[c], 2 * chip[0] + chip[1], 1 - c, sibling).wait_recv()
        for cp in first + passed:
            cp.wait_send()
        mine.wait()

    return pl.pallas_call(
        body, name="all_gather_weights", in_specs=[_ANY], out_specs=_ANY,
        out_shape=jax.ShapeDtypeStruct((N_SHARD,) + src.shape, src.dtype),
        scratch_shapes=[pltpu.SemaphoreType.DMA((6,)), pltpu.SemaphoreType.DMA((6,)), pltpu.SemaphoreType.DMA],
    )(src)


def pair_swap(gs):
    n = len(gs)

    def body(*refs):
        g_refs, r_refs, send_sems, recv_sems = refs[:n], refs[n:2 * n], refs[2 * n], refs[2 * n + 1]
        x, y, c, _ = _place()
        copies = []
        for k in range(n):
            for j in range(N_SHARD):
                copies.append(pltpu.make_async_remote_copy(
                    src_ref=g_refs[k].at[j, 1 - c], dst_ref=r_refs[k].at[j], send_sem=send_sems.at[k * N_SHARD + j],
                    recv_sem=recv_sems.at[k * N_SHARD + j], device_id=(x, y, 1 - c), device_id_type=MESH))
        for cp in copies:
            cp.start()
        for cp in copies:
            cp.wait()

    return pl.pallas_call(
        body, name="grad_pair_swap", in_specs=[_ANY] * n, out_specs=[_ANY] * n,
        out_shape=[jax.ShapeDtypeStruct((N_SHARD,) + g.shape[2:], g.dtype) for g in gs],
        scratch_shapes=[pltpu.SemaphoreType.DMA((n * N_SHARD,)), pltpu.SemaphoreType.DMA((n * N_SHARD,))],
    )(*gs)


def _flat_tile(rows):
    return 512 if rows % 512 == 0 else rows


def pair_add(g, r, cidx, out_dtype, name):
    rows = g.shape[2]
    tr = _flat_tile(rows)

    def body(c_ref, g_ref, r_ref, o_ref):
        o_ref[...] = (g_ref[...] + r_ref[...]).astype(out_dtype)

    return pl.pallas_call(
        body, name=name,
        grid_spec=pltpu.PrefetchScalarGridSpec(
            num_scalar_prefetch=1, grid=(N_SHARD, rows // tr),
            in_specs=[pl.BlockSpec((None, None, tr, LANES), lambda j, i, c: (j, c[0], i, 0)),
                      pl.BlockSpec((None, tr, LANES), lambda j, i, c: (j, i, 0))],
            out_specs=pl.BlockSpec((None, tr, LANES), lambda j, i, c: (j, i, 0))),
        out_shape=jax.ShapeDtypeStruct((N_SHARD, rows, LANES), out_dtype),
        compiler_params=_cparams(),
    )(cidx, g, r)


def chip_scatter(ps):
    n = len(ps)

    def body(*refs):
        p_refs, r_refs, send_sems, recv_sems, local_sems = refs[:n], refs[n:2 * n], refs[2 * n], refs[2 * n + 1], refs[2 * n + 2]
        x, y, c, chips = _place()
        me = 2 * x + y
        local, copies = [], []
        for k in range(n):
            cp = pltpu.make_async_copy(p_refs[k].at[me], r_refs[k].at[me], local_sems.at[k])
            cp.start()
            local.append(cp)
            for j, chip in enumerate(chips):
                copies.append(pltpu.make_async_remote_copy(
                    src_ref=p_refs[k].at[2 * chip[0] + chip[1]], dst_ref=r_refs[k].at[me], send_sem=send_sems.at[k * 3 + j],
                    recv_sem=recv_sems.at[k * 3 + j], device_id=(*chip, c), device_id_type=MESH))
        for cp in copies:
            cp.start()
        for cp in copies:
            cp.wait()
        for cp in local:
            cp.wait()

    return pl.pallas_call(
        body, name="grad_chip_scatter", in_specs=[_ANY] * n, out_specs=[_ANY] * n,
        out_shape=[jax.ShapeDtypeStruct(p.shape, p.dtype) for p in ps],
        scratch_shapes=[pltpu.SemaphoreType.DMA((n * 3,)), pltpu.SemaphoreType.DMA((n * 3,)), pltpu.SemaphoreType.DMA((n,))],
    )(*ps)


def sum_slots(r, name):
    rows = r.shape[1]
    tr = _flat_tile(rows)

    def body(r_ref, o_ref):
        o_ref[...] = ((r_ref[0].astype(F32) + r_ref[1].astype(F32)) + r_ref[2].astype(F32)) + r_ref[3].astype(F32)

    return pl.pallas_call(
        body, name=name, grid=(rows // tr,),
        in_specs=[pl.BlockSpec((N_SHARD, tr, LANES), lambda i: (0, i, 0))],
        out_specs=pl.BlockSpec((tr, LANES), lambda i: (i, 0)),
        out_shape=jax.ShapeDtypeStruct((rows, LANES), F32), compiler_params=_cparams(),
    )(r)


def pair_gather(hs):
    n = len(hs)

    def body(*refs):
        h_refs, f_refs, send_sems, recv_sems, local_sems = refs[:n], refs[n:2 * n], refs[2 * n], refs[2 * n + 1], refs[2 * n + 2]
        x, y, c, _ = _place()
        local, copies = [], []
        for k in range(n):
            cp = pltpu.make_async_copy(h_refs[k], f_refs[k].at[c], local_sems.at[k])
            cp.start()
            local.append(cp)
            copies.append(pltpu.make_async_remote_copy(
                src_ref=h_refs[k], dst_ref=f_refs[k].at[c], send_sem=send_sems.at[k], recv_sem=recv_sems.at[k],
                device_id=(x, y, 1 - c), device_id_type=MESH))
        for cp in copies:
            cp.start()
        for cp in copies:
            cp.wait()
        for cp in local:
            cp.wait()

    return pl.pallas_call(
        body, name="grad_pair_gather", in_specs=[_ANY] * n, out_specs=[_ANY] * n,
        out_shape=[jax.ShapeDtypeStruct((2,) + h.shape, h.dtype) for h in hs],
        scratch_shapes=[pltpu.SemaphoreType.DMA((n,)), pltpu.SemaphoreType.DMA((n,)), pltpu.SemaphoreType.DMA((n,))],
    )(*hs)


def reduce_scatter(g_big, g_small, cidx):
    r_big, r_small = pair_swap([g_big, g_small])
    p_big = pair_add(g_big, r_big, cidx, BF16, "grad_pair_add")
    p_small = pair_add(g_small, r_small, cidx, F32, "grad_pair_add_small")
    q_big, q_small = chip_scatter([p_big, p_small])
    h_big = sum_slots(q_big, "grad_chip_sum")
    h_small = sum_slots(q_small, "grad_chip_sum_small")
    f_big, f_small = pair_gather([h_big, h_small])
    return f_big.reshape(-1, LANES), f_small.reshape(-1, LANES)


def _rows_of(c, half):
    return pl.ds(pl.multiple_of(c * half, 8), half)


def all_gather_shards(srcs, exact):
    n, m = len(srcs), len(exact)
    halves = [s.shape[0] // 2 for s in srcs]

    def body(*refs):
        s_refs, e_refs = refs[:n], refs[n:n + m]
        o_refs, eo_refs = refs[n + m:2 * n + m], refs[2 * n + m:2 * n + 2 * m]
        send, recv, esend, erecv, osend, orecv, lsem = refs[2 * n + 2 * m:]
        x, y, c, chips = _place()
        me = 2 * x + y
        sibling = (x, y, 1 - c)
        own = [pltpu.make_async_remote_copy(src_ref=s_refs[k], dst_ref=o_refs[k].at[me], send_sem=osend.at[k],
                                            recv_sem=orecv.at[k], device_id=sibling, device_id_type=MESH) for k in range(n)]
        local = [pltpu.make_async_copy(e_refs[k], eo_refs[k].at[me], lsem.at[k]) for k in range(m)]
        for cp in own + local:
            cp.start()

        def copy(k, s, src, idx, half_c, to):
            return pltpu.make_async_remote_copy(
                src_ref=src, dst_ref=o_refs[k].at[idx, _rows_of(half_c, halves[k])], send_sem=send.at[6 * k + s],
                recv_sem=recv.at[6 * k + s], device_id=to, device_id_type=MESH)

        def ecopy(k, j, idx, to):
            return pltpu.make_async_remote_copy(src_ref=e_refs[k], dst_ref=eo_refs[k].at[idx], send_sem=esend.at[3 * k + j],
                                                recv_sem=erecv.at[3 * k + j], device_id=to, device_id_type=MESH)

        sends = []
        for k in range(n):
            mine = s_refs[k].at[_rows_of(c, halves[k])]
            sends += [copy(k, j, mine, me, c, (*chip, c)) for j, chip in enumerate(chips)]
        for k in range(m):
            sends += [ecopy(k, j, me, (*chip, c)) for j, chip in enumerate(chips)]
        for cp in sends:
            cp.start()
        for j, chip in enumerate(chips):
            idx = 2 * chip[0] + chip[1]
            for k in range(n):
                landed = o_refs[k].at[idx, _rows_of(c, halves[k])]
                copy(k, j, landed, idx, c, sibling).wait_recv()
                fwd = copy(k, 3 + j, landed, idx, c, sibling)
                fwd.start()
                sends.append(fwd)
        for j, chip in enumerate(chips):
            idx = 2 * chip[0] + chip[1]
            for k in range(n):
                copy(k, 3 + j, s_refs[k].at[_rows_of(c, halves[k])], idx, 1 - c, sibling).wait_recv()
            for k in range(m):
                ecopy(k, j, idx, sibling).wait_recv()
        for cp in sends:
            cp.wait_send()
        for cp in own + local:
            cp.wait()

    outs = pl.pallas_call(
        body, name="all_gather_weights", in_specs=[_ANY] * (n + m), out_specs=[_ANY] * (n + m),
        out_shape=[jax.ShapeDtypeStruct((N_SHARD,) + a.shape, a.dtype) for a in list(srcs) + list(exact)],
        scratch_shapes=[pltpu.SemaphoreType.DMA((6 * n,)), pltpu.SemaphoreType.DMA((6 * n,)),
                        pltpu.SemaphoreType.DMA((3 * m,)), pltpu.SemaphoreType.DMA((3 * m,)),
                        pltpu.SemaphoreType.DMA((n,)), pltpu.SemaphoreType.DMA((n,)), pltpu.SemaphoreType.DMA((m,))],
    )(*srcs, *exact)
    return outs[:n], outs[n:]


def rs_pair_swap(gs):
    n = len(gs)

    def body(*refs):
        g_refs, r_refs, send, recv = refs[:n], refs[n:2 * n], refs[2 * n], refs[2 * n + 1]
        x, y, c, _ = _place()
        copies = [pltpu.make_async_remote_copy(
            src_ref=g_refs[k].at[pl.ds(0, N_SHARD), _rows_of(1 - c, gs[k].shape[1] // 2)], dst_ref=r_refs[k],
            send_sem=send.at[k], recv_sem=recv.at[k], device_id=(x, y, 1 - c), device_id_type=MESH) for k in range(n)]
        for cp in copies:
            cp.start()
        for cp in copies:
            cp.wait()

    return pl.pallas_call(
        body, name="grad_pair_swap", in_specs=[_ANY] * n, out_specs=[_ANY] * n,
        out_shape=[jax.ShapeDtypeStruct((N_SHARD, g.shape[1] // 2, g.shape[2]), g.dtype) for g in gs],
        scratch_shapes=[pltpu.SemaphoreType.DMA((n,)), pltpu.SemaphoreType.DMA((n,))],
    )(*gs)


def _group_tile(half, n_cols, n_arrays):
    budget = (20 * 2 ** 20) // (6 * n_arrays)
    fits = [t for t in range(8, half + 1, 8) if half % t == 0 and t * n_cols * 4 <= budget]
    return max(fits) if fits else 8


def rs_pair_add(gs, rs, cidx, out_dtype, name):
    n = len(gs)
    _, K, cols = gs[0].shape
    half = K // 2
    tr = _group_tile(half, cols, n)
    nb = half // tr

    def body(c_ref, *refs):
        for g_ref, r_ref, o_ref in zip(refs[:n], refs[n:2 * n], refs[2 * n:]):
            o_ref[...] = (g_ref[...].astype(F32) + r_ref[...].astype(F32)).astype(out_dtype)

    gspec = pl.BlockSpec((None, tr, cols), lambda j, i, c: (j, c[0] * nb + i, 0))
    rspec = pl.BlockSpec((None, tr, cols), lambda j, i, c: (j, i, 0))
    return pl.pallas_call(
        body, name=name,
        grid_spec=pltpu.PrefetchScalarGridSpec(num_scalar_prefetch=1, grid=(N_SHARD, nb), in_specs=[gspec] * n + [rspec] * n,
                                               out_specs=[rspec] * n),
        out_shape=[jax.ShapeDtypeStruct((N_SHARD, half, cols), out_dtype)] * n,
        compiler_params=_cparams(),
    )(cidx, *gs, *rs)


def rs_chip_scatter(ps):
    n = len(ps)

    def body(*refs):
        p_refs, q_refs, send, recv, lsem = refs[:n], refs[n:2 * n], refs[2 * n], refs[2 * n + 1], refs[2 * n + 2]
        x, y, c, chips = _place()
        me = 2 * x + y
        local = [pltpu.make_async_copy(p_refs[k].at[me], q_refs[k].at[me], lsem.at[k]) for k in range(n)]
        copies = [pltpu.make_async_remote_copy(
            src_ref=p_refs[k].at[2 * chip[0] + chip[1]], dst_ref=q_refs[k].at[me], send_sem=send.at[3 * k + j],
            recv_sem=recv.at[3 * k + j], device_id=(*chip, c), device_id_type=MESH)
            for k in range(n) for j, chip in enumerate(chips)]
        for cp in local + copies:
            cp.start()
        for cp in copies:
            cp.wait()
        for cp in local:
            cp.wait()

    return pl.pallas_call(
        body, name="grad_chip_scatter", in_specs=[_ANY] * n, out_specs=[_ANY] * n,
        out_shape=[jax.ShapeDtypeStruct(p.shape, p.dtype) for p in ps],
        scratch_shapes=[pltpu.SemaphoreType.DMA((3 * n,)), pltpu.SemaphoreType.DMA((3 * n,)), pltpu.SemaphoreType.DMA((n,))],
    )(*ps)


def rs_chip_sum(qs, nl, cidx, name):
    n = len(qs)
    _, half, cols = qs[0].shape
    tr = _group_tile(half, cols, n)
    nb = half // tr

    def body(c_ref, *refs):
        for k, q_ref in enumerate(refs[:n]):
            o_ref = refs[n + k // nl]
            o_ref[k % nl] = ((q_ref[0].astype(F32) + q_ref[1].astype(F32)) + q_ref[2].astype(F32)) + q_ref[3].astype(F32)

    return pl.pallas_call(
        body, name=name,
        grid_spec=pltpu.PrefetchScalarGridSpec(
            num_scalar_prefetch=1, grid=(nb,),
            in_specs=[pl.BlockSpec((N_SHARD, tr, cols), lambda i, c: (0, i, 0))] * n,
            out_specs=[pl.BlockSpec((nl, tr, cols), lambda i, c: (0, c[0] * nb + i, 0))] * (n // nl)),
        out_shape=[jax.ShapeDtypeStruct((nl, 2 * half, cols), F32)] * (n // nl),
        compiler_params=_cparams(),
    )(cidx, *qs)


def rs_pair_gather(fs):
    n = len(fs)

    def body(*refs):
        f_refs, send, recv = refs[n:2 * n], refs[2 * n], refs[2 * n + 1]
        x, y, c, _ = _place()
        copies = []
        for k in range(n):
            rows = f_refs[k].at[pl.ds(0, fs[k].shape[0]), _rows_of(c, fs[k].shape[1] // 2)]
            copies.append(pltpu.make_async_remote_copy(src_ref=rows, dst_ref=rows, send_sem=send.at[k], recv_sem=recv.at[k],
                                                       device_id=(x, y, 1 - c), device_id_type=MESH))
        for cp in copies:
            cp.start()
        for cp in copies:
            cp.wait()

    return pl.pallas_call(
        body, name="grad_pair_gather", in_specs=[_ANY] * n, out_specs=[_ANY] * n,
        out_shape=[jax.ShapeDtypeStruct(f.shape, f.dtype) for f in fs],
        input_output_aliases={k: k for k in range(n)},
        scratch_shapes=[pltpu.SemaphoreType.DMA((n,)), pltpu.SemaphoreType.DMA((n,))],
    )(*fs)


_HBM = pl.BlockSpec(memory_space=pltpu.HBM)
_SEM = pl.BlockSpec(memory_space=pltpu.SEMAPHORE)
_EFFECT = pltpu.SideEffectType.DATAFLOW_SIDE_EFFECTING


def _in_hbm(a):
    return pltpu.with_memory_space_constraint(a, pltpu.HBM)


def split_start(name, srcs, lands, after, copies_fn, n_copies):
    n = len(srcs)

    def body(*refs):
        for cp in copies_fn(refs[:n], refs[n:2 * n], refs[2 * n + 1], refs[2 * n + 2]):
            cp.start()
        refs[-1][...] = jnp.zeros_like(refs[-1])

    bufs = list(srcs) + list(lands)
    outs = pl.pallas_call(
        body, name=name,
        out_shape=(pltpu.SemaphoreType.DMA((n_copies,)), pltpu.SemaphoreType.DMA((n_copies,)),
                   *[pltpu.HBM(a.shape, a.dtype) for a in bufs], jax.ShapeDtypeStruct((8, 128), F32)),
        in_specs=[_HBM] * (2 * n) + [_ANY],
        out_specs=(_SEM, _SEM, *[_HBM] * (2 * n), pl.BlockSpec(memory_space=pltpu.VMEM)),
        input_output_aliases={i: 2 + i for i in range(2 * n)},
        compiler_params=pltpu.CompilerParams(has_side_effects=_EFFECT),
    )(*[_in_hbm(a) for a in bufs], after)
    return outs[0], outs[1], outs[2:2 + n], outs[2 + n:2 + 2 * n], outs[-1]


def split_wait(name, send, recv, srcs, lands, after, copies_fn):
    n = len(srcs)

    def body(*refs):
        copies = copies_fn(refs[:n], refs[n:2 * n], refs[2 * n], refs[2 * n + 1])
        for cp in copies:
            cp.wait_send()
        for cp in copies:
            cp.wait_recv()

    bufs = list(srcs) + list(lands)
    outs = pl.pallas_call(
        body, name=name, out_shape=tuple(pltpu.HBM(a.shape, a.dtype) for a in bufs),
        in_specs=[_HBM] * (2 * n) + [_SEM, _SEM, _ANY], out_specs=tuple([_HBM] * (2 * n)),
        input_output_aliases={i: i for i in range(2 * n)},
        compiler_params=pltpu.CompilerParams(has_side_effects=_EFFECT),
    )(*bufs, send, recv, after)
    return list(outs[n:])


def _gather_copies(s_refs, l_refs, send, recv):
    x, y, c, chips = _place()
    me = 2 * x + y
    out = []
    for k, (s, l) in enumerate(zip(s_refs, l_refs)):
        rows = _rows_of(c, s.shape[0] // 2)
        for j, chip in enumerate(chips):
            out.append(pltpu.make_async_remote_copy(src_ref=s.at[rows], dst_ref=l.at[me, rows], send_sem=send.at[4 * k + j],
                                                    recv_sem=recv.at[4 * k + j], device_id=(*chip, c), device_id_type=MESH))
        out.append(pltpu.make_async_remote_copy(src_ref=s, dst_ref=l.at[me], send_sem=send.at[4 * k + 3],
                                                recv_sem=recv.at[4 * k + 3], device_id=(x, y, 1 - c), device_id_type=MESH))
    return out


def _scatter_copies(s_refs, l_refs, send, recv):
    x, y, c, chips = _place()
    me = 2 * x + y
    return [pltpu.make_async_remote_copy(src_ref=s.at[2 * chip[0] + chip[1]], dst_ref=l.at[me], send_sem=send.at[3 * k + j],
                                         recv_sem=recv.at[3 * k + j], device_id=(*chip, c), device_id_type=MESH)
            for k, (s, l) in enumerate(zip(s_refs, l_refs)) for j, chip in enumerate(chips)]


def gather_forward(lands):
    n = len(lands)

    def body(*refs):
        l_refs, send, recv = refs[n:2 * n], refs[2 * n], refs[2 * n + 1]
        x, y, c, chips = _place()
        copies = []
        for k in range(n):
            rows = _rows_of(c, lands[k].shape[1] // 2)
            for j, chip in enumerate(chips):
                part = l_refs[k].at[2 * chip[0] + chip[1], rows]
                copies.append(pltpu.make_async_remote_copy(src_ref=part, dst_ref=part, send_sem=send.at[3 * k + j],
                                                           recv_sem=recv.at[3 * k + j], device_id=(x, y, 1 - c),
                                                           device_id_type=MESH))
        for cp in copies:
            cp.start()
        for cp in copies:
            cp.wait()

    return pl.pallas_call(
        body, name="gather_forward", in_specs=[_ANY] * n, out_specs=[_ANY] * n,
        out_shape=[jax.ShapeDtypeStruct(a.shape, a.dtype) for a in lands],
        input_output_aliases={k: k for k in range(n)},
        scratch_shapes=[pltpu.SemaphoreType.DMA((3 * n,)), pltpu.SemaphoreType.DMA((3 * n,))],
    )(*lands)


def rs_partials(gs, wire, cidx, tag):
    rs = rs_pair_swap(gs)
    groups = {}
    for k, g in enumerate(gs):
        groups.setdefault((g.shape, jnp.dtype(wire[k]).name), []).append(k)
    ps = [None] * len(gs)
    for gi, ks in enumerate(groups.values()):
        outs = rs_pair_add([gs[k] for k in ks], [rs[k] for k in ks], cidx, wire[ks[0]], "grad_pair_add_%s%d" % (tag, gi))
        for k, o in zip(ks, outs):
            ps[k] = o
    return ps


def rs_finish(items):
    cidx = lax.axis_index("c").astype(jnp.int32).reshape(1)
    groups = {}
    for i, it in enumerate(items):
        groups.setdefault((it[0].shape, len(it), it[0].dtype.name), []).append(i)
    fs = [None] * len(items)
    for gi, ids in enumerate(groups.values()):
        outs = rs_chip_sum([q for i in ids for q in items[i]], len(items[ids[0]]), cidx, "grad_chip_sum_%d" % gi)
        for i, o in zip(ids, outs):
            fs[i] = o
    return rs_pair_gather(fs)


def reduce_scatter_shards(items, wire_of, cidx):
    nl = [len(it) for it in items]
    gs = [g for it in items for g in it]
    wire = [wire_of[i] for i, it in enumerate(items) for _ in it]
    first = np.cumsum([0] + nl)
    rs = rs_pair_swap(gs)
    groups = {}
    for i, it in enumerate(items):
        groups.setdefault((it[0].shape, len(it), jnp.dtype(wire_of[i]).name), []).append(i)
    ps = [None] * len(gs)
    for gi, ids in enumerate(groups.values()):
        ks = [first[i] + l for i in ids for l in range(nl[i])]
        outs = rs_pair_add([gs[k] for k in ks], [rs[k] for k in ks], cidx, wire[ks[0]], "grad_pair_add_%d" % gi)
        for k, o in zip(ks, outs):
            ps[k] = o
    qs = rs_chip_scatter(ps)
    fs = [None] * len(items)
    for gi, ids in enumerate(groups.values()):
        ks = [first[i] + l for i in ids for l in range(nl[i])]
        outs = rs_chip_sum([qs[k] for k in ks], nl[ids[0]], cidx, "grad_chip_sum_%d" % gi)
        for i, o in zip(ids, outs):
            fs[i] = o
    return rs_pair_gather(fs)


def _flat_view(shape):
    n = int(np.prod(shape))
    if n <= 2 ** 20 and n % LANES == 0:
        return (n // LANES, LANES)
    if n <= 2 ** 20 and n % 128 == 0:
        return (n // 128, 128)
    return (n // shape[-1], shape[-1])


def adamw(w, g, m, v, name):
    shape = w.shape
    if w.ndim == 2:
        block, grid, index = shape, (1,), (lambda i: (0, 0))
    else:
        slab = shape[2:]
        unit = 4 * int(np.prod(slab[:-2] or (1,))) * (-(-slab[-1] // 128) * 128)
        if len(slab) >= 2:
            unit *= -(-slab[-2] // 8) * 8
        k = shape[1]
        tr = k
        if k * unit > 2 ** 21:
            tr = max(t for t in range(8, k, 8) if k % t == 0 and t * unit <= 2 ** 21)
        block, grid = (None, tr) + tuple(slab), (shape[0], k // tr)
        index = lambda l, i: (l, i) + (0,) * len(slab)

    def body(w_ref, g_ref, m_ref, v_ref, d_ref, nm_ref, nv_ref):
        g_ = g_ref[...]
        m_new = ADAM_B1 * m_ref[...] + (1.0 - ADAM_B1) * g_
        v_new = ADAM_B2 * v_ref[...] + (1.0 - ADAM_B2) * (g_ * g_)
        m_hat = m_new / (1.0 - ADAM_B1 ** ADAM_STEP)
        v_hat = v_new / (1.0 - ADAM_B2 ** ADAM_STEP)
        d_ref[...] = -ADAM_LR * (m_hat / (jnp.sqrt(v_hat) + ADAM_EPS) + ADAM_WD * w_ref[...])
        nm_ref[...] = m_new
        nv_ref[...] = v_new

    spec = pl.BlockSpec(block, index)
    return pl.pallas_call(
        body, name=name, grid=grid, in_specs=[spec] * 4, out_specs=[spec] * 3,
        out_shape=[jax.ShapeDtypeStruct(shape, F32)] * 3, compiler_params=_cparams(),
    )(w, g, m, v)


_WEIGHTS = ['meta', 'ffn1_w_gate', 'ffn1_w_up', 'ffn1_w_down', 'ln1_g', 'ln1_b', 'w_in', 'mla_q_norm_g', 'mla_w_uq',
            'mla_kv_norm_g', 'mla_w_ukv', 'mla_w_o', 'conv_w', 'conv_b', 'conv_w_out', 's5_a_re', 's5_a_im', 's5_log_dt',
            's5_b_re', 's5_b_im', 's5_c_re', 's5_c_im', 's5_d', 's5_w_glu', 's5_b_glu', 's5_w_out', 'w_o', 'ln2_g', 'ln2_b',
            'ffn2_w_gate', 'ffn2_w_up', 'ffn2_w_down', 'ln3_g', 'ln3_b']


def _pad_to(flat, n):
    return jnp.concatenate([flat, jnp.zeros((n - flat.shape[0],), flat.dtype)])


def _shard_of(full, axis):
    if axis == 1:
        return _shard_cols(full)
    return full.reshape(N_SHARD, full.shape[0] // N_SHARD, full.shape[1])


_FFN_KEY = {'gate': 'wg', 'up': 'wu', 'down': 'wd'}


def _step(env):
    w = {n: env[n] for n in _WEIGHTS}
    mom = {n: env['m_' + n] for n in _WEIGHTS}
    var = {n: env['v_' + n] for n in _WEIGHTS}
    cidx = lax.axis_index("c").astype(jnp.int32).reshape(1)
    chip = 2 * lax.axis_index("x") + lax.axis_index("y")
    big_names = [n for n, _ in _BIG]
    nb = len(big_names)

    srcs = [[w[n][li].astype(BF16) for n in big_names] for li in range(DEPTH)]
    gathered0, (conv_w_st, meta_st) = all_gather_shards(srcs[0], [w['conv_w'], w['meta']])
    lands = [lax.empty((N_SHARD,) + s.shape, BF16) for s in srcs[1]]
    g_send, g_recv, srcs_t, lands_t, token = split_start("gather1_start", srcs[1], lands, gathered0[0], _gather_copies, 4 * nb)

    def layer_weights(li, st):
        small = {n: w[n][li] for n in _REPL}
        small['conv_w'] = _nat_cols(conv_w_st[:, li])
        return compute_weights(dict(zip(big_names, st)), small)

    x2d = env['x'][0]
    lp = x2d.shape[0] + X0
    tabs = _rope_tables(lp)
    h = jnp.concatenate([jnp.zeros((PAD, D_MODEL), F32), _nat_cols(meta_st), x2d], axis=0) + token[0, 0]
    W0 = layer_weights(0, gathered0)
    h, hb, sv0 = layer_fwd(h, h.astype(BF16), W0, tabs, lp)
    lands1 = split_wait("gather1_wait", g_send, g_recv, srcs_t, lands_t, h, _gather_copies)
    W1 = layer_weights(1, gather_forward(lands1))
    h, hb, sv1 = layer_fwd(h, hb, W1, tabs, lp)
    tgt = jnp.concatenate([jnp.zeros((X0, D_MODEL), F32), env['loss_target'][0]], axis=0)
    dh, loss_part = loss_head(h, tgt, lp)
    loss = lax.psum(loss_part[0, 0], ("x", "y", "c"))

    def shards(G):
        full = reference_grads(G, ffn=False)
        return [G[_FFN_KEY[n.split('_')[-1]] + n[3]] if n.startswith('ffn') else _shard_of(full[n], a) for n, a in _BIG], full

    dh, G1 = layer_bwd(dh, sv1, W1, tabs, lp)
    gs1, full1 = shards(G1)
    p1 = rs_partials(gs1, [BF16] * nb, cidx, "b")
    q1 = [lax.dynamic_update_slice_in_dim(jnp.zeros_like(p), lax.dynamic_slice_in_dim(p, chip, 1, axis=0), chip, axis=0)
          for p in p1]
    s_send, s_recv, p1_t, q1_t, token2 = split_start("scatter1_start", p1, q1, dh, _scatter_copies, 3 * nb)
    dh, G0 = layer_bwd(dh + token2[0, 0], sv0, W0, tabs, lp)
    q1 = split_wait("scatter1_wait", s_send, s_recv, p1_t, q1_t, dh, _scatter_copies)
    gs0, full0 = shards(G0)
    full = [full0, full1]

    s_parts = [jnp.stack([full[li][n] for li in range(DEPTH)]).reshape(-1) for n in _REPL + ['conv_w']]
    s_parts.append(dh[PAD:X0].reshape(-1))
    s_sizes = [int(p.shape[0]) for p in s_parts]
    s_rows = -(-sum(s_sizes) // (16 * LANES)) * 16
    g_small = _pad_to(jnp.concatenate(s_parts), s_rows * LANES).reshape(1, s_rows, LANES)
    g_small = jnp.broadcast_to(g_small, (N_SHARD, s_rows, LANES))
    q0 = rs_chip_scatter(rs_partials(gs0 + [g_small], [BF16] * nb + [F32], cidx, "a"))
    red = rs_finish([[q0[i], q1[i]] for i in range(nb)] + [[q0[nb]]])
    f_small = red[-1].reshape(-1)

    grad = dict(zip(big_names, red[:nb]))
    off = 0
    for n, sz in zip(_REPL + ['conv_w', 'meta'], s_sizes):
        grad[n] = f_small[off:off + sz]
        off += sz
    for n in _REPL:
        grad[n] = grad[n].reshape(w[n].shape)
    cw = grad['conv_w'].reshape(DEPTH, 3, MIX)
    grad['conv_w'] = lax.dynamic_slice_in_dim(cw, chip * (MIX // N_SHARD), MIX // N_SHARD, axis=2)
    gm = grad['meta'].reshape(N_META, D_MODEL)
    grad['meta'] = lax.dynamic_slice_in_dim(gm, chip * (D_MODEL // N_SHARD), D_MODEL // N_SHARD, axis=1)

    delta, new_m, new_v = {}, {}, {}
    for n in _WEIGHTS:
        delta[n], new_m[n], new_v[n] = adamw(w[n], grad[n], mom[n], var[n], "adamw_" + n)
    return (loss, dh[X0:][None], *[grad[n] for n in _WEIGHTS], *[delta[n] for n in _WEIGHTS],
            *[new_m[n] for n in _WEIGHTS], *[new_v[n] for n in _WEIGHTS])


def _step_sync(env):
    w = {n: env[n] for n in _WEIGHTS}
    mom = {n: env['m_' + n] for n in _WEIGHTS}
    var = {n: env['v_' + n] for n in _WEIGHTS}
    cidx = lax.axis_index("c").astype(jnp.int32).reshape(1)
    chip = 2 * lax.axis_index("x") + lax.axis_index("y")
    big_names = [n for n, _ in _BIG]
    nb = len(big_names)

    srcs = [w[n][li].astype(BF16) for li in range(DEPTH) for n in big_names]
    gathered, (conv_w_st, meta_st) = all_gather_shards(srcs, [w['conv_w'], w['meta']])
    meta_full = _nat_cols(meta_st)
    Ws = []
    for li in range(DEPTH):
        small = {n: w[n][li] for n in _REPL}
        small['conv_w'] = _nat_cols(conv_w_st[:, li])
        Ws.append(compute_weights({n: gathered[li * nb + i] for i, n in enumerate(big_names)}, small))

    loss_part, dx, dmeta, grads = local_step(env['x'][0], env['loss_target'][0], meta_full, Ws)
    loss = lax.psum(loss_part[0, 0], ("x", "y", "c"))
    full = [reference_grads(g, ffn=False) for g in grads]

    def shard(li, n, axis):
        if n.startswith('ffn'):
            return grads[li][_FFN_KEY[n.split('_')[-1]] + n[3]]
        return _shard_of(full[li][n], axis)

    items = [[shard(li, n, a) for li in range(DEPTH)] for n, a in _BIG]
    s_parts = [jnp.stack([full[li][n] for li in range(DEPTH)]).reshape(-1) for n in _REPL + ['conv_w']] + [dmeta.reshape(-1)]
    s_sizes = [int(p.shape[0]) for p in s_parts]
    s_rows = -(-sum(s_sizes) // (16 * LANES)) * 16
    g_small = _pad_to(jnp.concatenate(s_parts), s_rows * LANES).reshape(1, s_rows, LANES)
    g_small = jnp.broadcast_to(g_small, (N_SHARD, s_rows, LANES))
    red = reduce_scatter_shards(items + [[g_small]], [BF16] * nb + [F32], cidx)
    f_small = red[-1].reshape(-1)

    grad = dict(zip(big_names, red[:nb]))
    off = 0
    for n, sz in zip(_REPL + ['conv_w', 'meta'], s_sizes):
        grad[n] = f_small[off:off + sz]
        off += sz
    for n in _REPL:
        grad[n] = grad[n].reshape(w[n].shape)
    cw = grad['conv_w'].reshape(DEPTH, 3, MIX)
    grad['conv_w'] = lax.dynamic_slice_in_dim(cw, chip * (MIX // N_SHARD), MIX // N_SHARD, axis=2)
    gm = grad['meta'].reshape(N_META, D_MODEL)
    grad['meta'] = lax.dynamic_slice_in_dim(gm, chip * (D_MODEL // N_SHARD), D_MODEL // N_SHARD, axis=1)

    delta, new_m, new_v = {}, {}, {}
    for n in _WEIGHTS:
        delta[n], new_m[n], new_v[n] = adamw(w[n], grad[n], mom[n], var[n], "adamw_" + n)
    return (loss, dx[None], *[grad[n] for n in _WEIGHTS], *[delta[n] for n in _WEIGHTS],
            *[new_m[n] for n in _WEIGHTS], *[new_v[n] for n in _WEIGHTS])


def kernel(x, meta, ffn1_w_gate, ffn1_w_up, ffn1_w_down, ln1_g, ln1_b, w_in, mla_q_norm_g, mla_w_uq, mla_kv_norm_g, mla_w_ukv, mla_w_o, conv_w, conv_b, conv_w_out, s5_a_re, s5_a_im, s5_log_dt, s5_b_re, s5_b_im, s5_c_re, s5_c_im, s5_d, s5_w_glu, s5_b_glu, s5_w_out, w_o, ln2_g, ln2_b, ffn2_w_gate, ffn2_w_up, ffn2_w_down, ln3_g, ln3_b, loss_target, m_meta, m_ffn1_w_gate, m_ffn1_w_up, m_ffn1_w_down, m_ln1_g, m_ln1_b, m_w_in, m_mla_q_norm_g, m_mla_w_uq, m_mla_kv_norm_g, m_mla_w_ukv, m_mla_w_o, m_conv_w, m_conv_b, m_conv_w_out, m_s5_a_re, m_s5_a_im, m_s5_log_dt, m_s5_b_re, m_s5_b_im, m_s5_c_re, m_s5_c_im, m_s5_d, m_s5_w_glu, m_s5_b_glu, m_s5_w_out, m_w_o, m_ln2_g, m_ln2_b, m_ffn2_w_gate, m_ffn2_w_up, m_ffn2_w_down, m_ln3_g, m_ln3_b, v_meta, v_ffn1_w_gate, v_ffn1_w_up, v_ffn1_w_down, v_ln1_g, v_ln1_b, v_w_in, v_mla_q_norm_g, v_mla_w_uq, v_mla_kv_norm_g, v_mla_w_ukv, v_mla_w_o, v_conv_w, v_conv_b, v_conv_w_out, v_s5_a_re, v_s5_a_im, v_s5_log_dt, v_s5_b_re, v_s5_b_im, v_s5_c_re, v_s5_c_im, v_s5_d, v_s5_w_glu, v_s5_b_glu, v_s5_w_out, v_w_o, v_ln2_g, v_ln2_b, v_ffn2_w_gate, v_ffn2_w_up, v_ffn2_w_down, v_ln3_g, v_ln3_b):
    return _step(dict(locals()))


def _unused_packed_step(env):
    w = {n: env[n] for n in _WEIGHTS}
    mom = {n: env['m_' + n] for n in _WEIGHTS}
    var = {n: env['v_' + n] for n in _WEIGHTS}
    cidx = lax.axis_index("c").astype(jnp.int32).reshape(1)
    big_names = [n for n, _ in _BIG]

    exact = jnp.concatenate([conv_w.reshape(-1), meta.reshape(-1)])
    parts = [w[n].astype(BF16).reshape(-1) for n in big_names] + [lax.bitcast_convert_type(exact, BF16).reshape(-1)]
    sizes = [int(p.shape[0]) for p in parts]
    hr = _half_rows(sum(sizes))
    packed = _pad_to(jnp.concatenate(parts), 2 * hr * LANES).reshape(2, hr, LANES)
    gathered = all_gather_halves(packed).reshape(N_SHARD, -1)
    st, off = {}, 0
    for n, sz in zip(big_names, sizes[:-1]):
        st[n] = gathered[:, off:off + sz].reshape((N_SHARD,) + w[n].shape)
        off += sz
    exact_all = lax.bitcast_convert_type(gathered[:, off:off + sizes[-1]].reshape(N_SHARD, -1, 2), F32)
    n_cw = conv_w.size
    conv_w_st = exact_all[:, :n_cw].reshape((N_SHARD,) + conv_w.shape)
    meta_full = _nat_cols(exact_all[:, n_cw:].reshape((N_SHARD,) + meta.shape))

    Ws = []
    for li in range(DEPTH):
        small = {n: w[n][li] for n in _REPL}
        small['conv_w'] = _nat_cols(conv_w_st[:, li])
        Ws.append(compute_weights({n: st[n][:, li] for n in big_names}, small))

    loss_part, dx, dmeta, grads = local_step(x[0], loss_target[0], meta_full, Ws)
    loss = lax.psum(loss_part[0, 0], ("x", "y", "c"))
    ref_grads = [shard_grads(g) for g in grads]

    def shard_stack(n):
        return jnp.stack([ref_grads[li][n] for li in range(DEPTH)], axis=1)

    g_parts = [shard_stack(n).reshape(N_SHARD, -1) for n, _ in _BIG]
    g_parts.append(shard_stack('conv_w').reshape(N_SHARD, -1))
    g_parts.append(_shard_cols(dmeta).reshape(N_SHARD, -1))
    g_sizes = [int(p.shape[1]) for p in g_parts]
    ghr = _half_rows(sum(g_sizes))
    g_big = jnp.concatenate(g_parts + [jnp.zeros((N_SHARD, 2 * ghr * LANES - sum(g_sizes)), F32)], axis=1)
    g_big = g_big.reshape(N_SHARD, 2, ghr, LANES)
    s_parts = [jnp.stack([ref_grads[li][n] for li in range(DEPTH)]).reshape(-1) for n in _REPL]
    s_sizes = [int(p.shape[0]) for p in s_parts]
    shr = _half_rows(sum(s_sizes))
    g_small = _pad_to(jnp.concatenate(s_parts), 2 * shr * LANES)
    g_small = jnp.broadcast_to(g_small.reshape(1, 2, shr, LANES), (N_SHARD, 2, shr, LANES))
    f_big, f_small = reduce_scatter(g_big, g_small, cidx)
    f_big, f_small = f_big.reshape(-1), f_small.reshape(-1)

    grad = {}
    off = 0
    for (n, _), sz in zip(_BIG + [('conv_w', 1), ('meta', 1)], g_sizes):
        grad[n] = f_big[off:off + sz].reshape(w[n].shape)
        off += sz
    off = 0
    for n, sz in zip(_REPL, s_sizes):
        grad[n] = f_small[off:off + sz].reshape(w[n].shape)
        off += sz

    delta, new_m, new_v = {}, {}, {}
    for n in _WEIGHTS:
        delta[n], new_m[n], new_v[n] = adamw(w[n], grad[n], mom[n], var[n], "adamw_" + n)
    return (loss, dx[None], *[grad[n] for n in _WEIGHTS], *[delta[n] for n in _WEIGHTS],
            *[new_m[n] for n in _WEIGHTS], *[new_v[n] for n in _WEIGHTS])
```

```python
import functools
import math

import numpy as np
import jax
import jax.numpy as jnp
from jax import lax
from jax.experimental import pallas as pl
from jax.experimental.pallas import tpu as pltpu

F32 = jnp.float32
BF16 = jnp.bfloat16

D_MODEL = 1024
DEPTH = 2
N_META = 16
PAD = 112
X0 = PAD + N_META
N_HEADS = 8
D_NOPE = 64
D_ROPE = 32
D_V = 64
Q_RANK = 384
KV_RANK = 256
MIX = 512
S5_GROUPS = 32
S5_GROUP = 16
S5_STATE = 64
S5_LANES = S5_GROUPS * S5_STATE
D_FF = 2816
N_SHARD = 4
FF_SHARD = D_FF // N_SHARD
D_IN = 5792
P_IN = 6144
ALPHA = (2.0 * DEPTH) ** 0.25
LN_EPS = 1e-5
RMS_EPS = 1e-6
ATT_SCALE = (D_NOPE + D_ROPE) ** -0.5
ROPE_BASE = 10000.0
ADAM_LR, ADAM_B1, ADAM_B2, ADAM_EPS, ADAM_WD, ADAM_STEP = 0.001, 0.9, 0.999, 1e-08, 0.01, 10
SCAN_CHUNK = 128
VMEM_LIMIT = 52 * 2 ** 20
WGRAD = BF16
MESH = pl.DeviceIdType.MESH


def _cparams(**kw):
    return pltpu.CompilerParams(vmem_limit_bytes=VMEM_LIMIT, **kw)


def _tile(n):
    if n <= 1088:
        return n
    for t in (1024, 544, 512, 272, 256, 128):
        if n % t == 0:
            return t
    return n


def _row_tile(lp):
    for t in (544, 272, 128):
        if lp % t == 0:
            return t
    return lp


def _sigmoid(x):
    return 1.0 / (1.0 + jnp.exp(-x))


_GELU_C = math.sqrt(2.0 / math.pi)


def _gelu(x):
    return 0.5 * x * (1.0 + jnp.tanh(_GELU_C * (x + 0.044715 * x * x * x)))


def _gelu_grad(x):
    t = jnp.tanh(_GELU_C * (x + 0.044715 * x * x * x))
    return 0.5 * (1.0 + t) + 0.5 * x * (1.0 - t * t) * _GELU_C * (1.0 + 3.0 * 0.044715 * x * x)


def _dot(a, b, ca, cb, precision=None):
    return lax.dot_general(a, b, (((ca,), (cb,)), ((), ())), preferred_element_type=F32, precision=precision)


def matmul(a, b, *, name, ta=False, tb=False, ab='n', bb='n', res=None, res_scale=1.0, scale=1.0, out_dtype=F32):
    if ta:
        _, K, M = a.shape
    else:
        _, M, K = a.shape
    if tb:
        _, N, K2 = b.shape
    else:
        _, K2, N = b.shape
    assert K == K2, (a.shape, b.shape)
    n_out = max(a.shape[0] if ab == 'o' else 1, b.shape[0] if bb == 'o' else 1)
    n_red = max(a.shape[0] if ab == 'r' else 1, b.shape[0] if bb == 'r' else 1)
    tm, tn = _tile(M), _tile(N)
    tk = K if K <= 2304 else _tile(K)
    nkt = K // tk
    n_steps = n_red * nkt

    def bsel(mode, o, r):
        if mode == 'o':
            return o
        if mode == 'r':
            return r // nkt if nkt > 1 else r
        return 0

    def ksel(r):
        if nkt == 1:
            return 0
        return r % nkt if n_red > 1 else r

    a_map = (lambda o, i, j, r: (bsel(ab, o, r), ksel(r), i)) if ta else (lambda o, i, j, r: (bsel(ab, o, r), i, ksel(r)))
    b_map = (lambda o, i, j, r: (bsel(bb, o, r), j, ksel(r))) if tb else (lambda o, i, j, r: (bsel(bb, o, r), ksel(r), j))
    o_map = lambda o, i, j, r: (o, i, j)
    in_specs = [pl.BlockSpec((None, tk, tm) if ta else (None, tm, tk), a_map),
                pl.BlockSpec((None, tn, tk) if tb else (None, tk, tn), b_map)]
    operands = [a, b]
    if res is not None:
        in_specs.append(pl.BlockSpec((None, tm, tn), o_map))
        operands.append(res)
    has_res = res is not None

    def body(*refs):
        a_ref, b_ref = refs[0], refs[1]
        res_ref = refs[2] if has_res else None
        o_ref = refs[3] if has_res else refs[2]
        part = _dot(a_ref[...].astype(BF16), b_ref[...].astype(BF16), 0 if ta else 1, 1 if tb else 0)

        def finish(acc):
            v = acc if scale == 1.0 else acc * scale
            if has_res:
                v = v + res_scale * res_ref[...].astype(F32)
            o_ref[...] = v.astype(o_ref.dtype)

        if n_steps == 1:
            finish(part)
        else:
            acc_ref = refs[-1]
            r = pl.program_id(3)

            @pl.when(r == 0)
            def _():
                acc_ref[...] = part

            @pl.when(r > 0)
            def _():
                acc_ref[...] += part

            @pl.when(r == n_steps - 1)
            def _():
                finish(acc_ref[...])

    return pl.pallas_call(
        body, name=name,
        grid=(n_out, M // tm, N // tn, n_steps),
        in_specs=in_specs,
        out_specs=pl.BlockSpec((None, tm, tn), o_map),
        out_shape=jax.ShapeDtypeStruct((n_out, M, N), out_dtype),
        scratch_shapes=[pltpu.VMEM((tm, tn), F32)] if n_steps > 1 else [],
        compiler_params=_cparams(),
    )(*operands)


def rowwise(fn, rows, pars, outs, accs=(), *, name, lp):
    tm = _row_tile(lp)
    n_rows, n_pars, n_outs, n_accs = len(rows), len(pars), len(outs), len(accs)
    in_specs = [pl.BlockSpec((tm, w), functools.partial(lambda i, cb: (i, cb), cb=cb)) for _, w, cb in rows]
    in_specs += [pl.BlockSpec(p.shape, functools.partial(lambda i, nd: (0,) * nd, nd=p.ndim)) for p in pars]
    out_specs = [pl.BlockSpec((tm, w), lambda i: (i, 0)) for w, _ in outs]
    out_specs += [pl.BlockSpec(s, functools.partial(lambda i, nd: (0,) * nd, nd=len(s))) for s, _ in accs]
    out_shape = [jax.ShapeDtypeStruct((lp, w), dt) for w, dt in outs]
    out_shape += [jax.ShapeDtypeStruct(s, dt) for s, dt in accs]

    def body(*refs):
        i = pl.program_id(0)
        rv = [r[...] for r in refs[:n_rows]]
        pv = [r[...] for r in refs[n_rows:n_rows + n_pars]]
        o_refs = refs[n_rows + n_pars:n_rows + n_pars + n_outs]
        a_refs = refs[n_rows + n_pars + n_outs:]
        ov, av = fn(i * tm, rv, pv)
        for r, v in zip(o_refs, ov):
            r[...] = v.astype(r.dtype)
        if n_accs:
            @pl.when(i == 0)
            def _():
                for r, v in zip(a_refs, av):
                    r[...] = v.astype(r.dtype)

            @pl.when(i > 0)
            def _():
                for r, v in zip(a_refs, av):
                    r[...] += v.astype(r.dtype)

    res = pl.pallas_call(
        body, name=name, grid=(lp // tm,), in_specs=in_specs, out_specs=out_specs, out_shape=out_shape,
        compiler_params=_cparams(),
    )(*[r[0] for r in rows], *pars)
    return res


def _row_mask(row0, shape):
    return (row0 + lax.broadcasted_iota(jnp.int32, shape, 0)) >= PAD


def ffn_up(hb, wg, wu, lp):
    tm = _row_tile(lp)

    def body(h_ref, wg_ref, wu_ref, ab_ref, hid_ref):
        h = h_ref[...]
        a = _dot(h, wg_ref[...], 1, 1)
        b = _dot(h, wu_ref[...], 1, 1)
        ab_ref[0] = a
        ab_ref[1] = b
        hid_ref[...] = (a * _sigmoid(a) * b).astype(BF16)

    wspec = pl.BlockSpec((None, FF_SHARD, D_MODEL), lambda j, i: (j, 0, 0))
    return pl.pallas_call(
        body, name="ffn_up", grid=(N_SHARD, lp // tm),
        in_specs=[pl.BlockSpec((tm, D_MODEL), lambda j, i: (i, 0)), wspec, wspec],
        out_specs=[pl.BlockSpec((None, 2, tm, FF_SHARD), lambda j, i: (j, 0, i, 0)),
                   pl.BlockSpec((None, tm, FF_SHARD), lambda j, i: (j, i, 0))],
        out_shape=[jax.ShapeDtypeStruct((N_SHARD, 2, lp, FF_SHARD), F32),
                   jax.ShapeDtypeStruct((N_SHARD, lp, FF_SHARD), BF16)],
        compiler_params=_cparams(),
    )(hb, wg, wu)


def _layer_norm(z, g, b):
    mu = jnp.mean(z, axis=-1, keepdims=True)
    zc = z - mu
    var = jnp.mean(zc * zc, axis=-1, keepdims=True)
    return zc * lax.rsqrt(var + LN_EPS) * g + b


def mm_res_ln(a, w, res, g, b, *, scale, name, lp):
    n_red, _, K = a.shape
    tm = _row_tile(lp)

    def body(a_ref, w_ref, res_ref, g_ref, b_ref, z_ref, h_ref, hb_ref, acc_ref):
        r = pl.program_id(1)
        part = _dot(a_ref[...].astype(BF16), w_ref[...], 1, 0)

        @pl.when(r == 0)
        def _():
            acc_ref[...] = part

        @pl.when(r > 0)
        def _():
            acc_ref[...] += part

        @pl.when(r == n_red - 1)
        def _():
            z = ALPHA * res_ref[...] + scale * acc_ref[...]
            z_ref[...] = z
            hn = _layer_norm(z, g_ref[...], b_ref[...])
            h_ref[...] = hn
            hb_ref[...] = hn.astype(BF16)

    row = pl.BlockSpec((tm, D_MODEL), lambda i, r: (i, 0))
    par = pl.BlockSpec((1, D_MODEL), lambda i, r: (0, 0))
    return pl.pallas_call(
        body, name=name, grid=(lp // tm, n_red),
        in_specs=[pl.BlockSpec((None, tm, K), lambda i, r: (r, i, 0)),
                  pl.BlockSpec((None, K, D_MODEL), lambda i, r: (r, 0, 0)), row, par, par],
        out_specs=[row, row, row],
        out_shape=[jax.ShapeDtypeStruct((lp, D_MODEL), F32), jax.ShapeDtypeStruct((lp, D_MODEL), F32),
                   jax.ShapeDtypeStruct((lp, D_MODEL), BF16)],
        scratch_shapes=[pltpu.VMEM((tm, D_MODEL), F32)],
        compiler_params=_cparams(),
    )(a, w, res, g, b)


def ln_bwd(dh, z, g, *, fscale, name, lp):
    def fn(row0, rv, pv):
        dh_, z_ = rv
        g_, = pv
        mu = jnp.mean(z_, axis=-1, keepdims=True)
        zc = z_ - mu
        rstd = lax.rsqrt(jnp.mean(zc * zc, axis=-1, keepdims=True) + LN_EPS)
        xh = zc * rstd
        dxh = dh_ * g_
        m1 = jnp.mean(dxh, axis=-1, keepdims=True)
        m2 = jnp.mean(dxh * xh, axis=-1, keepdims=True)
        dz = rstd * (dxh - m1 - xh * m2)
        return ((dz, fscale * dz),
                (jnp.sum(dh_ * xh, axis=0, keepdims=True), jnp.sum(dh_, axis=0, keepdims=True)))

    return rowwise(fn, [(dh, D_MODEL, 0), (z, D_MODEL, 0)], [g], [(D_MODEL, F32), (D_MODEL, BF16)],
                   [((1, D_MODEL), F32), ((1, D_MODEL), F32)], name=name, lp=lp)


def ffn_down_bwd(dfb, wd, ab, lp):
    tm = _row_tile(lp)

    def body(df_ref, w_ref, ab_ref, da_ref, db_ref):
        dhid = _dot(df_ref[...], w_ref[...], 1, 1)
        a = ab_ref[0]
        b = ab_ref[1]
        sg = _sigmoid(a)
        da_ref[...] = (dhid * b * (sg * (1.0 + a * (1.0 - sg)))).astype(BF16)
        db_ref[...] = (dhid * (a * sg)).astype(BF16)

    ospec = pl.BlockSpec((None, tm, FF_SHARD), lambda j, i: (j, i, 0))
    return pl.pallas_call(
        body, name="ffn_down_bwd", grid=(N_SHARD, lp // tm),
        in_specs=[pl.BlockSpec((tm, D_MODEL), lambda j, i: (i, 0)),
                  pl.BlockSpec((None, FF_SHARD, D_MODEL), lambda j, i: (j, 0, 0)),
                  pl.BlockSpec((None, 2, tm, FF_SHARD), lambda j, i: (j, 0, i, 0))],
        out_specs=[ospec, ospec],
        out_shape=[jax.ShapeDtypeStruct((N_SHARD, lp, FF_SHARD), BF16)] * 2,
        compiler_params=_cparams(),
    )(dfb, wd, ab)


def ffn_dx(da, db, wg, wu, dz, lp):
    tm = _row_tile(lp)

    def body(da_ref, db_ref, wg_ref, wu_ref, dz_ref, o_ref, acc_ref):
        j = pl.program_id(1)
        part = _dot(da_ref[...], wg_ref[...], 1, 0) + _dot(db_ref[...], wu_ref[...], 1, 0)

        @pl.when(j == 0)
        def _():
            acc_ref[...] = part

        @pl.when(j > 0)
        def _():
            acc_ref[...] += part

        @pl.when(j == N_SHARD - 1)
        def _():
            o_ref[...] = acc_ref[...] + ALPHA * dz_ref[...]

    aspec = pl.BlockSpec((None, tm, FF_SHARD), lambda i, j: (j, i, 0))
    wspec = pl.BlockSpec((None, FF_SHARD, D_MODEL), lambda i, j: (j, 0, 0))
    row = pl.BlockSpec((tm, D_MODEL), lambda i, j: (i, 0))
    return pl.pallas_call(
        body, name="ffn_dx", grid=(lp // tm, N_SHARD), in_specs=[aspec, aspec, wspec, wspec, row], out_specs=row,
        out_shape=jax.ShapeDtypeStruct((lp, D_MODEL), F32), scratch_shapes=[pltpu.VMEM((tm, D_MODEL), F32)],
        compiler_params=_cparams(),
    )(da, db, wg, wu, dz)


def ffn_fwd(h, hb, wg, wu, wd, g, b, lp):
    ab, hid = ffn_up(hb, wg, wu, lp)
    z, hn, hnb = mm_res_ln(hid, wd, h, g, b, scale=0.5, name="ffn_down_ln", lp=lp)
    return hn, hnb, dict(hb=hb, ab=ab, hid=hid, z=z)


def ffn_bwd(dh, sv, wg, wu, wd, g, lp):
    dz, dfb, dg, db = ln_bwd(dh, sv['z'], g, fscale=0.5, name="ffn_ln_bwd", lp=lp)
    da, dbb = ffn_down_bwd(dfb, wd, sv['ab'], lp)
    d_wd = matmul(sv['hid'], dfb[None], ta=True, ab='o', out_dtype=WGRAD, name="ffn_dwd")
    d_wg = matmul(da, sv['hb'][None], ta=True, ab='o', out_dtype=WGRAD, name="ffn_dwg")
    d_wu = matmul(dbb, sv['hb'][None], ta=True, ab='o', out_dtype=WGRAD, name="ffn_dwu")
    dh_in = ffn_dx(da, dbb, wg, wu, dz, lp)
    return dh_in, dict(wg=d_wg, wu=d_wu, wd=d_wd, ln_g=dg, ln_b=db)


def _rope_tables(lp):
    pos = np.arange(lp, dtype=np.float32) - PAD
    inv = ROPE_BASE ** (-np.arange(0, D_ROPE, 2, dtype=np.float32) / D_ROPE)
    ang = pos[:, None] * inv[None, :]
    cos = np.concatenate([np.cos(ang), np.cos(ang)], axis=1).astype(np.float32)
    sin = np.concatenate([np.sin(ang), np.sin(ang)], axis=1).astype(np.float32)
    rot = np.zeros((D_ROPE, D_ROPE), np.float32)
    half = D_ROPE // 2
    for j in range(half):
        rot[j + half, j] = -1.0
        rot[j, j + half] = 1.0
    return jnp.asarray(cos), jnp.asarray(sin), jnp.asarray(rot)


def _rot(x, rot):
    return _dot(x, rot, 1, 0, precision=lax.Precision.HIGHEST)


def _rms(x, g):
    r = lax.rsqrt(jnp.mean(x * x, axis=-1, keepdims=True) + RMS_EPS)
    return x * r * g


def mla_prep(proj, cos, sin, rot, qg, kvg, lp):
    def fn(row0, rv, pv):
        cq, krb, ckv, c, s = rv
        qg_, kvg_, rot_ = pv
        kr = krb[:, :D_ROPE]
        return ((_rms(cq, qg_), _rms(ckv, kvg_), kr * c + _rot(kr, rot_) * s), ())

    return rowwise(fn, [(proj, Q_RANK, 0), (proj, 128, 3), (proj, KV_RANK, 2), (cos, D_ROPE, 0), (sin, D_ROPE, 0)],
                   [qg, kvg, rot], [(Q_RANK, BF16), (KV_RANK, BF16), (D_ROPE, BF16)], name="mla_prep", lp=lp)


def mla_heads(cqn, ckvn, cos, sin, rot, wqn, wqr, wkn, wv, lp):
    tm = _row_tile(lp)

    def body(cq_ref, ckv_ref, c_ref, s_ref, rot_ref, wqn_ref, wqr_ref, wkn_ref, wv_ref, qn_ref, qr_ref, kn_ref, v_ref):
        cq = cq_ref[...]
        ckv = ckv_ref[...]
        qn_ref[...] = _dot(cq, wqn_ref[...], 1, 0).astype(BF16)
        qr = _dot(cq, wqr_ref[...], 1, 0)
        qr_ref[...] = (qr * c_ref[...] + _rot(qr, rot_ref[...]) * s_ref[...]).astype(BF16)
        kn_ref[...] = _dot(ckv, wkn_ref[...], 1, 0).astype(BF16)
        v_ref[...] = _dot(ckv, wv_ref[...], 1, 0).astype(BF16)

    def row(w):
        return pl.BlockSpec((tm, w), lambda h, i: (i, 0))

    def wspec(k, n):
        return pl.BlockSpec((None, k, n), lambda h, i: (h, 0, 0))

    def ospec(n):
        return pl.BlockSpec((None, tm, n), lambda h, i: (h, i, 0))

    return pl.pallas_call(
        body, name="mla_heads", grid=(N_HEADS, lp // tm),
        in_specs=[row(Q_RANK), row(KV_RANK), row(D_ROPE), row(D_ROPE),
                  pl.BlockSpec((D_ROPE, D_ROPE), lambda h, i: (0, 0)),
                  wspec(Q_RANK, D_NOPE), wspec(Q_RANK, D_ROPE), wspec(KV_RANK, D_NOPE), wspec(KV_RANK, D_V)],
        out_specs=[ospec(D_NOPE), ospec(D_ROPE), ospec(D_NOPE), ospec(D_V)],
        out_shape=[jax.ShapeDtypeStruct((N_HEADS, lp, D_NOPE), BF16), jax.ShapeDtypeStruct((N_HEADS, lp, D_ROPE), BF16),
                   jax.ShapeDtypeStruct((N_HEADS, lp, D_NOPE), BF16), jax.ShapeDtypeStruct((N_HEADS, lp, D_V), BF16)],
        compiler_params=_cparams(),
    )(cqn, ckvn, cos, sin, rot, wqn, wqr, wkn, wv)


def _att_probs(qn, qr, kn, kr, row0, tq, lp):
    s = (_dot(qn, kn, 1, 1) + _dot(qr, kr, 1, 1)) * ATT_SCALE
    qi = row0 + lax.broadcasted_iota(jnp.int32, (tq, lp), 0)
    ki = lax.broadcasted_iota(jnp.int32, (tq, lp), 1)
    s = jnp.where((ki <= qi) & (ki >= PAD), s, -1e30)
    p = jnp.exp(s - jnp.max(s, axis=-1, keepdims=True))
    return p / jnp.sum(p, axis=-1, keepdims=True)


def _att_specs(tq, lp):
    def qspec(n):
        return pl.BlockSpec((None, tq, n), lambda h, i: (h, i, 0))

    def kspec(n):
        return pl.BlockSpec((None, lp, n), lambda h, i: (h, 0, 0))

    return qspec, kspec, pl.BlockSpec((lp, D_ROPE), lambda h, i: (0, 0))


def _att_tile(lp):
    return 272 if lp % 272 == 0 else 128


def attn_fwd(qn, qr, kn, v, kr, lp):
    tq = _att_tile(lp)
    qspec, kspec, krspec = _att_specs(tq, lp)

    def body(qn_ref, qr_ref, kn_ref, v_ref, kr_ref, o_ref):
        i = pl.program_id(1)
        for k in range(lp // tq):
            @pl.when(i == k)
            def _(k=k):
                ke = (k + 1) * tq
                p = _att_probs(qn_ref[...], qr_ref[...], kn_ref[0:ke, :], kr_ref[0:ke, :], k * tq, tq, ke)
                o_ref[...] = _dot(p.astype(BF16), v_ref[0:ke, :], 1, 0).astype(BF16)

    return pl.pallas_call(
        body, name="attn_fwd", grid=(N_HEADS, lp // tq),
        in_specs=[qspec(D_NOPE), qspec(D_ROPE), kspec(D_NOPE), kspec(D_V), krspec],
        out_specs=qspec(D_V), out_shape=jax.ShapeDtypeStruct((N_HEADS, lp, D_V), BF16),
        compiler_params=_cparams(),
    )(qn, qr, kn, v, kr)


def attn_bwd(qn, qr, kn, v, kr, do, lp):
    tq = _att_tile(lp)
    qspec, kspec, krspec = _att_specs(tq, lp)

    def body(qn_ref, qr_ref, kn_ref, v_ref, kr_ref, do_ref, dqn_ref, dqr_ref, dkn_ref, dv_ref, dkr_ref):
        h, i = pl.program_id(0), pl.program_id(1)

        @pl.when(i == 0)
        def _():
            dkn_ref[...] = jnp.zeros_like(dkn_ref)
            dv_ref[...] = jnp.zeros_like(dv_ref)

        @pl.when((i == 0) & (h == 0))
        def _():
            dkr_ref[...] = jnp.zeros_like(dkr_ref)

        for k in range(lp // tq):
            @pl.when(i == k)
            def _(k=k):
                ke = (k + 1) * tq
                qn_, qr_, do_ = qn_ref[...], qr_ref[...], do_ref[...]
                kn_, v_, kr_ = kn_ref[0:ke, :], v_ref[0:ke, :], kr_ref[0:ke, :]
                p = _att_probs(qn_, qr_, kn_, kr_, k * tq, tq, ke)
                dp = _dot(do_, v_, 1, 1)
                delta = jnp.sum(p * dp, axis=-1, keepdims=True)
                ds = (p * (dp - delta) * ATT_SCALE).astype(BF16)
                dqn_ref[...] = _dot(ds, kn_, 1, 0).astype(BF16)
                dqr_ref[...] = _dot(ds, kr_, 1, 0)
                dkn_ref[0:ke, :] += _dot(ds, qn_, 0, 0)
                dv_ref[0:ke, :] += _dot(p.astype(BF16), do_, 0, 0)
                dkr_ref[0:ke, :] += _dot(ds, qr_, 0, 0)

    return pl.pallas_call(
        body, name="attn_bwd", grid=(N_HEADS, lp // tq),
        in_specs=[qspec(D_NOPE), qspec(D_ROPE), kspec(D_NOPE), kspec(D_V), krspec, qspec(D_V)],
        out_specs=[qspec(D_NOPE), qspec(D_ROPE), kspec(D_NOPE), kspec(D_V), krspec],
        out_shape=[jax.ShapeDtypeStruct((N_HEADS, lp, D_NOPE), BF16), jax.ShapeDtypeStruct((N_HEADS, lp, D_ROPE), F32),
                   jax.ShapeDtypeStruct((N_HEADS, lp, D_NOPE), F32), jax.ShapeDtypeStruct((N_HEADS, lp, D_V), F32),
                   jax.ShapeDtypeStruct((lp, D_ROPE), F32)],
        compiler_params=_cparams(),
    )(qn, qr, kn, v, kr, do)


def mla_heads_bwd(dqn, dqr, dkn, dv, cos, sin, rot, wqn, wqr, wkn, wv, lp):
    tm = _row_tile(lp)

    def body(dqn_ref, dqr_ref, dkn_ref, dv_ref, c_ref, s_ref, rot_ref, wqn_ref, wqr_ref, wkn_ref, wv_ref,
             dcq_ref, dckv_ref, dqrp_ref):
        h = pl.program_id(1)
        dqr_ = dqr_ref[...]
        dqrp = (dqr_ * c_ref[...] - _rot(dqr_ * s_ref[...], rot_ref[...])).astype(BF16)
        dqrp_ref[...] = dqrp
        dcq = _dot(dqn_ref[...], wqn_ref[...], 1, 1) + _dot(dqrp, wqr_ref[...], 1, 1)
        dckv = _dot(dkn_ref[...].astype(BF16), wkn_ref[...], 1, 1) + _dot(dv_ref[...].astype(BF16), wv_ref[...], 1, 1)

        @pl.when(h == 0)
        def _():
            dcq_ref[...] = dcq
            dckv_ref[...] = dckv

        @pl.when(h > 0)
        def _():
            dcq_ref[...] += dcq
            dckv_ref[...] += dckv

    def hspec(n):
        return pl.BlockSpec((None, tm, n), lambda i, h: (h, i, 0))

    def row(w):
        return pl.BlockSpec((tm, w), lambda i, h: (i, 0))

    def wspec(k, n):
        return pl.BlockSpec((None, k, n), lambda i, h: (h, 0, 0))

    return pl.pallas_call(
        body, name="mla_heads_bwd", grid=(lp // tm, N_HEADS),
        in_specs=[hspec(D_NOPE), hspec(D_ROPE), hspec(D_NOPE), hspec(D_V), row(D_ROPE), row(D_ROPE),
                  pl.BlockSpec((D_ROPE, D_ROPE), lambda i, h: (0, 0)),
                  wspec(Q_RANK, D_NOPE), wspec(Q_RANK, D_ROPE), wspec(KV_RANK, D_NOPE), wspec(KV_RANK, D_V)],
        out_specs=[row(Q_RANK), row(KV_RANK), hspec(D_ROPE)],
        out_shape=[jax.ShapeDtypeStruct((lp, Q_RANK), F32), jax.ShapeDtypeStruct((lp, KV_RANK), F32),
                   jax.ShapeDtypeStruct((N_HEADS, lp, D_ROPE), BF16)],
        compiler_params=_cparams(),
    )(dqn, dqr, dkn, dv, cos, sin, rot, wqn, wqr, wkn, wv)


def _rms_bwd(dy, x, g):
    r = lax.rsqrt(jnp.mean(x * x, axis=-1, keepdims=True) + RMS_EPS)
    n = x * r
    dn = dy * g
    dx = r * (dn - n * jnp.mean(dn * n, axis=-1, keepdims=True))
    return dx, jnp.sum(dy * n, axis=0, keepdims=True)


def mla_prep_bwd(dcq, dckv, dkr, proj, cos, sin, rot, qg, kvg, lp):
    def fn(row0, rv, pv):
        dcq_, dckv_, dkr_, cq, ckv, c, s = rv
        qg_, kvg_, rot_ = pv
        dxq, dgq = _rms_bwd(dcq_, cq, qg_)
        dxkv, dgkv = _rms_bwd(dckv_, ckv, kvg_)
        dkr_raw = dkr_ * c - _rot(dkr_ * s, rot_)
        return ((dxq, dxkv, dkr_raw), (dgq, dgkv))

    return rowwise(fn, [(dcq, Q_RANK, 0), (dckv, KV_RANK, 0), (dkr, D_ROPE, 0), (proj, Q_RANK, 0), (proj, KV_RANK, 2),
                        (cos, D_ROPE, 0), (sin, D_ROPE, 0)], [qg, kvg, rot],
                   [(Q_RANK, BF16), (KV_RANK, BF16), (D_ROPE, BF16)], [((1, Q_RANK), F32), ((1, KV_RANK), F32)],
                   name="mla_prep_bwd", lp=lp)


def _shift_down(x, d, rows):
    return jnp.where(rows >= d, pltpu.roll(x, d, 0), 0.0)


def _shift_up(x, d, rows, n):
    return jnp.where(rows < n - d, pltpu.roll(x, n - d, 0), 0.0)


_CONV_W = 128
_XB, _BG, _CG = 1024 // _CONV_W, 1536 // _CONV_W, 2048 // _CONV_W


def _conv_specs(lp):
    def pspec(base):
        return pl.BlockSpec((lp, _CONV_W), functools.partial(lambda c, base: (0, base + c), base=base))

    col = pl.BlockSpec((lp, _CONV_W), lambda c: (0, c))
    wspec = pl.BlockSpec((3, _CONV_W), lambda c: (0, c))
    bspec = pl.BlockSpec((1, _CONV_W), lambda c: (0, c))
    return pspec, col, wspec, bspec


def _conv_core(xbar, cg, w, bias, lp):
    rows = lax.broadcasted_iota(jnp.int32, (lp, _CONV_W), 0)
    u = jnp.where(rows >= PAD, cg * xbar, 0.0)
    u1 = _shift_down(u, 1, rows)
    u2 = _shift_down(u, 2, rows)
    y = bias + w[0:1] * u2 + w[1:2] * u1 + w[2:3] * u
    return rows, u, u1, u2, y


def conv_fwd(proj, w, bias, lp):
    pspec, col, wspec, bspec = _conv_specs(lp)

    def body(x_ref, b_ref, c_ref, w_ref, bias_ref, v_ref):
        _, _, _, _, y = _conv_core(x_ref[...], c_ref[...], w_ref[...], bias_ref[...], lp)
        v_ref[...] = (b_ref[...] * y).astype(BF16)

    return pl.pallas_call(
        body, name="conv_fwd", grid=(MIX // _CONV_W,),
        in_specs=[pspec(_XB), pspec(_BG), pspec(_CG), wspec, bspec], out_specs=col,
        out_shape=jax.ShapeDtypeStruct((lp, MIX), BF16), compiler_params=_cparams(),
    )(proj, proj, proj, w, bias)


def conv_bwd(dv, proj, w, bias, lp):
    pspec, col, wspec, bspec = _conv_specs(lp)

    def body(dv_ref, x_ref, b_ref, c_ref, w_ref, bias_ref, dx_ref, db_ref, dc_ref, dw_ref, dbias_ref):
        xbar, cg, w_ = x_ref[...], c_ref[...], w_ref[...]
        rows, u, u1, u2, y = _conv_core(xbar, cg, w_, bias_ref[...], lp)
        dv_ = dv_ref[...]
        db_ref[...] = (dv_ * y).astype(BF16)
        dy = dv_ * b_ref[...]
        dbias_ref[...] = jnp.sum(dy, axis=0, keepdims=True)
        dw_ref[0:1, :] = jnp.sum(dy * u2, axis=0, keepdims=True)
        dw_ref[1:2, :] = jnp.sum(dy * u1, axis=0, keepdims=True)
        dw_ref[2:3, :] = jnp.sum(dy * u, axis=0, keepdims=True)
        du = w_[2:3] * dy + w_[1:2] * _shift_up(dy, 1, rows, lp) + w_[0:1] * _shift_up(dy, 2, rows, lp)
        du = jnp.where(rows >= PAD, du, 0.0)
        dc_ref[...] = (du * xbar).astype(BF16)
        dx_ref[...] = (du * cg).astype(BF16)

    return pl.pallas_call(
        body, name="conv_bwd", grid=(MIX // _CONV_W,),
        in_specs=[col, pspec(_XB), pspec(_BG), pspec(_CG), wspec, bspec],
        out_specs=[col, col, col, wspec, bspec],
        out_shape=[jax.ShapeDtypeStruct((lp, MIX), BF16)] * 3 + [jax.ShapeDtypeStruct((3, MIX), F32),
                                                                jax.ShapeDtypeStruct((1, MIX), F32)],
        compiler_params=_cparams(),
    )(dv, proj, proj, proj, w, bias)


def _s5_disc(a_re, a_im, log_dt, b_re, b_im):
    dt = jnp.exp(log_dt)
    mag = jnp.exp(dt * a_re)
    ab_re, ab_im = mag * jnp.cos(dt * a_im), mag * jnp.sin(dt * a_im)
    den = a_re * a_re + a_im * a_im
    nr, ni = ab_re - 1.0, ab_im
    coef_re = (nr * a_re + ni * a_im) / den
    coef_im = (ni * a_re - nr * a_im) / den
    return ab_re, ab_im, coef_re * b_re - coef_im * b_im, coef_re * b_im + coef_im * b_re


_S5_ROWS = S5_GROUPS * S5_GROUP


def s5_prep(a_re, a_im, log_dt, b_re, b_im):
    def body(ar, ai, ld, br, bi, o0, o1, o2, o3):
        for o, v in zip((o0, o1, o2, o3), _s5_disc(ar[...], ai[...], ld[...], br[...], bi[...])):
            o[...] = v

    return pl.pallas_call(body, name="s5_prep",
                          out_shape=[jax.ShapeDtypeStruct((_S5_ROWS, S5_STATE), F32)] * 4)(a_re, a_im, log_dt, b_re, b_im)


def s5_prep_bwd(a_re, a_im, log_dt, b_re, b_im, d_ab_re, d_ab_im, d_bb_re, d_bb_im, sel):
    def body(ar, ai, ld, br, bi, g0, g1, g2, g3, sel_ref, da_re, da_im, dld, dbr, dbi):
        _, vjp = jax.vjp(_s5_disc, ar[...], ai[...], ld[...], br[...], bi[...])
        c_ar, c_ai, c_ld, c_br, c_bi = vjp((g0[...], g1[...], g2[...], g3[...]))
        s = sel_ref[...]
        hi = lax.Precision.HIGHEST
        da_re[...] = _dot(s, c_ar, 1, 0, precision=hi)
        da_im[...] = _dot(s, c_ai, 1, 0, precision=hi)
        dld[...] = jnp.sum(_dot(s, c_ld, 1, 0, precision=hi), axis=-1, keepdims=True)
        dbr[...] = c_br
        dbi[...] = c_bi

    g = jax.ShapeDtypeStruct((S5_GROUPS, S5_STATE), F32)
    full = jax.ShapeDtypeStruct((_S5_ROWS, S5_STATE), F32)
    return pl.pallas_call(body, name="s5_prep_bwd",
                          out_shape=[g, g, jax.ShapeDtypeStruct((S5_GROUPS, 1), F32), full, full],
                          )(a_re, a_im, log_dt, b_re, b_im, d_ab_re, d_ab_im, d_bb_re, d_bb_im, sel)


_SCAN_W = 128
_SCAN_STEPS = int(math.log2(SCAN_CHUNK))


def _cmul(ar, ai, br, bi):
    return ar * br - ai * bi, ar * bi + ai * br


def _scan_powers(ar, ai, reverse):
    pw = [(ar, ai)]
    for _ in range(_SCAN_STEPS):
        pw.append(_cmul(*pw[-1], *pw[-1]))
    rows = lax.broadcasted_iota(jnp.int32, (SCAN_CHUNK, ar.shape[-1]), 0)
    tr = jnp.broadcast_to(ar, rows.shape)
    ti = jnp.broadcast_to(ai, rows.shape)
    for k in range(_SCAN_STEPS):
        d = 2 ** k
        if reverse:
            live = rows < SCAN_CHUNK - d
            mr, mi = _cmul(tr, ti, _shift_up(tr, d, rows, SCAN_CHUNK), _shift_up(ti, d, rows, SCAN_CHUNK))
        else:
            live = rows >= d
            mr, mi = _cmul(tr, ti, _shift_down(tr, d, rows), _shift_down(ti, d, rows))
        tr = jnp.where(live, mr, tr)
        ti = jnp.where(live, mi, ti)
    return pw, rows, tr, ti


def s5_scan(bu, ab_re, ab_im, lp):
    n_chunks = lp // SCAN_CHUNK

    def body(bu_ref, ar_ref, ai_ref, s_ref):
        ar, ai = ar_ref[...], ai_ref[...]
        pw, rows, tr, ti = _scan_powers(ar, ai, False)

        def chunk(ci, carry):
            cr, cim = carry
            r0 = pl.multiple_of(ci * SCAN_CHUNK, SCAN_CHUNK)
            xr = bu_ref[0, pl.ds(r0, SCAN_CHUNK), :]
            xi = bu_ref[1, pl.ds(r0, SCAN_CHUNK), :]
            for k in range(_SCAN_STEPS):
                d = 2 ** k
                mr, mi = _cmul(pw[k][0], pw[k][1], _shift_down(xr, d, rows), _shift_down(xi, d, rows))
                xr, xi = xr + mr, xi + mi
            mr, mi = _cmul(tr, ti, cr, cim)
            xr, xi = xr + mr, xi + mi
            s_ref[0, pl.ds(r0, SCAN_CHUNK), :] = xr
            s_ref[1, pl.ds(r0, SCAN_CHUNK), :] = xi
            return xr[SCAN_CHUNK - 1:SCAN_CHUNK, :], xi[SCAN_CHUNK - 1:SCAN_CHUNK, :]

        zero = jnp.zeros((1, _SCAN_W), F32)
        lax.fori_loop(0, n_chunks, chunk, (zero, zero))

    spec = pl.BlockSpec((2, lp, _SCAN_W), lambda c: (0, 0, c))
    aspec = pl.BlockSpec((1, _SCAN_W), lambda c: (0, c))
    return pl.pallas_call(
        body, name="s5_scan", grid=(S5_LANES // _SCAN_W,), in_specs=[spec, aspec, aspec], out_specs=spec,
        out_shape=jax.ShapeDtypeStruct((2, lp, S5_LANES), F32), compiler_params=_cparams(),
    )(bu, ab_re, ab_im)


def s5_scan_bwd(ds, s, ab_re, ab_im, lp):
    n_chunks = lp // SCAN_CHUNK

    def body(ds_ref, s_ref, ar_ref, ai_ref, g_ref, da_ref):
        ar, ai = ar_ref[...], -ai_ref[...]
        pw, rows, tr, ti = _scan_powers(ar, ai, True)

        def chunk(k, carry):
            cr, cim, dar, dai = carry
            ci = n_chunks - 1 - k
            r0 = pl.multiple_of(ci * SCAN_CHUNK, SCAN_CHUNK)
            xr = ds_ref[0, pl.ds(r0, SCAN_CHUNK), :]
            xi = ds_ref[1, pl.ds(r0, SCAN_CHUNK), :]
            for j in range(_SCAN_STEPS):
                d = 2 ** j
                mr, mi = _cmul(pw[j][0], pw[j][1], _shift_up(xr, d, rows, SCAN_CHUNK), _shift_up(xi, d, rows, SCAN_CHUNK))
                xr, xi = xr + mr, xi + mi
            mr, mi = _cmul(tr, ti, cr, cim)
            xr, xi = xr + mr, xi + mi
            g_ref[0, pl.ds(r0, SCAN_CHUNK), :] = xr
            g_ref[1, pl.ds(r0, SCAN_CHUNK), :] = xi
            prev0 = pl.multiple_of(jnp.maximum(r0 - 8, 0), 8)
            live = (ci > 0).astype(F32)
            pr = s_ref[0, pl.ds(prev0, 8), :][7:8, :] * live
            pim = s_ref[1, pl.ds(prev0, 8), :][7:8, :] * live
            sr = s_ref[0, pl.ds(r0, SCAN_CHUNK), :]
            si = s_ref[1, pl.ds(r0, SCAN_CHUNK), :]
            sr = jnp.where(rows >= 1, pltpu.roll(sr, 1, 0), pr)
            si = jnp.where(rows >= 1, pltpu.roll(si, 1, 0), pim)
            dar = dar + jnp.sum(xr * sr + xi * si, axis=0, keepdims=True)
            dai = dai + jnp.sum(xi * sr - xr * si, axis=0, keepdims=True)
            return xr[0:1, :], xi[0:1, :], dar, dai

        zero = jnp.zeros((1, _SCAN_W), F32)
        _, _, dar, dai = lax.fori_loop(0, n_chunks, chunk, (zero, zero, zero, zero))
        da_ref[0] = dar
        da_ref[1] = dai

    spec = pl.BlockSpec((2, lp, _SCAN_W), lambda c: (0, 0, c))
    aspec = pl.BlockSpec((1, _SCAN_W), lambda c: (0, c))
    return pl.pallas_call(
        body, name="s5_scan_bwd", grid=(S5_LANES // _SCAN_W,), in_specs=[spec, spec, aspec, aspec],
        out_specs=[spec, pl.BlockSpec((2, 1, _SCAN_W), lambda c: (0, 0, c))],
        out_shape=[jax.ShapeDtypeStruct((2, lp, S5_LANES), F32), jax.ShapeDtypeStruct((2, 1, S5_LANES), F32)],
        compiler_params=_cparams(),
    )(ds, s, ab_re, ab_im)


def _blockdiag(x):
    g, r, c = x.shape
    eye = jnp.eye(g, dtype=x.dtype)
    return (x[:, :, None, :] * eye[:, None, :, None]).reshape(g * r, g * c)


def _blockdiag_extract(m, g, r, c):
    return jnp.einsum('grgc->grc', m.reshape(g, r, g, c))


def s5_u(proj, lp):
    def fn(row0, rv, pv):
        u, = rv
        return ((jnp.where(_row_mask(row0, u.shape), u, 0.0),), ())

    return rowwise(fn, [(proj, MIX, 5)], [], [(MIX, BF16)], name="s5_u", lp=lp)[0]


def s5_y(ys, proj, d, lp):
    def fn(row0, rv, pv):
        ys_, u = rv
        y = ys_ + pv[0] * u
        return ((y, _gelu(y)), ())

    return rowwise(fn, [(ys, MIX, 0), (proj, MIX, 5)], [d], [(MIX, F32), (MIX, BF16)], name="s5_y", lp=lp)


def s5_glu(z, y, b, lp):
    def fn(row0, rv, pv):
        z_, y_ = rv
        return ((_gelu(y_) * _sigmoid(z_ + pv[0]),), ())

    return rowwise(fn, [(z, MIX, 0), (y, MIX, 0)], [b], [(MIX, BF16)], name="s5_glu", lp=lp)[0]


def s5_glu_bwd(dgl, z, y, b, lp):
    def fn(row0, rv, pv):
        dgl_, z_, y_ = rv
        sg = _sigmoid(z_ + pv[0])
        dz = dgl_ * _gelu(y_) * sg * (1.0 - sg)
        return ((dgl_ * sg, dz), (jnp.sum(dz, axis=0, keepdims=True),))

    return rowwise(fn, [(dgl, MIX, 0), (z, MIX, 0), (y, MIX, 0)], [b], [(MIX, F32), (MIX, BF16)], [((1, MIX), F32)],
                   name="s5_glu_bwd", lp=lp)


def s5_y_bwd(dyg, y, proj, d, lp):
    def fn(row0, rv, pv):
        dyg_, y_, u = rv
        dy = dyg_ * _gelu_grad(y_)
        return ((dy, dy * pv[0]), (jnp.sum(dy * u, axis=0, keepdims=True),))

    return rowwise(fn, [(dyg, MIX, 0), (y, MIX, 0), (proj, MIX, 5)], [d], [(MIX, BF16), (MIX, F32)], [((1, MIX), F32)],
                   name="s5_y_bwd", lp=lp)


def s5_du(du, lp):
    def fn(row0, rv, pv):
        return ((jnp.where(_row_mask(row0, rv[0].shape), rv[0], 0.0),), ())

    return rowwise(fn, [(du, MIX, 0)], [], [(MIX, BF16)], name="s5_du", lp=lp)[0]


def merge_fwd(proj, ya, yb, yc, lp):
    def fn(row0, rv, pv):
        g0, g1, g2, a, b, c = rv
        return ((_sigmoid(g0) * a + _sigmoid(g1) * b + _sigmoid(g2) * c,), ())

    return rowwise(fn, [(proj, D_MODEL, 3), (proj, D_MODEL, 4), (proj, D_MODEL, 5), (ya, D_MODEL, 0), (yb, D_MODEL, 0),
                        (yc, D_MODEL, 0)], [], [(D_MODEL, BF16)], name="merge_fwd", lp=lp)[0]


def merge_bwd(dmix, proj, ya, yb, yc, lp):
    def fn(row0, rv, pv):
        dm, g0, g1, g2, a, b, c = rv
        outs_y, outs_g = [], []
        for g, yv in ((g0, a), (g1, b), (g2, c)):
            sg = _sigmoid(g)
            outs_y.append(dm * sg)
            outs_g.append(dm * yv * sg * (1.0 - sg))
        return (tuple(outs_y) + tuple(outs_g), ())

    return rowwise(fn, [(dmix, D_MODEL, 0), (proj, D_MODEL, 3), (proj, D_MODEL, 4), (proj, D_MODEL, 5),
                        (ya, D_MODEL, 0), (yb, D_MODEL, 0), (yc, D_MODEL, 0)], [], [(D_MODEL, BF16)] * 6,
                   name="merge_bwd", lp=lp)


def loss_head(h, tgt, lp):
    def fn(row0, rv, pv):
        h_, t_ = rv
        live = (row0 + lax.broadcasted_iota(jnp.int32, h_.shape, 0)) >= X0
        diff = jnp.where(live, h_ - t_, 0.0)
        ssq = jnp.sum(jnp.sum(diff * diff, axis=1, keepdims=True), axis=0, keepdims=True)
        return ((diff * (1.0 / D_MODEL),), (ssq * (0.5 / D_MODEL),))

    return rowwise(fn, [(h, D_MODEL, 0), (tgt, D_MODEL, 0)], [], [(D_MODEL, F32)], [((1, 1), F32)], name="loss_head", lp=lp)


def _s5_consts(W):
    ab_re_rep, ab_im_rep, bb_re, bb_im = s5_prep(W['s5_a_re'], W['s5_a_im'], W['s5_log_dt'], W['s5_b_re'], W['s5_b_im'])
    pick = lambda t: t.reshape(S5_GROUPS, S5_GROUP, S5_STATE)[:, 0].reshape(1, S5_LANES)
    bb = jnp.stack([_blockdiag(bb_re.reshape(S5_GROUPS, S5_GROUP, S5_STATE)),
                    _blockdiag(bb_im.reshape(S5_GROUPS, S5_GROUP, S5_STATE))]).astype(BF16)
    return pick(ab_re_rep), pick(ab_im_rep), bb


def layer_fwd(h, hb, W, tabs, lp):
    cos, sin, rot = tabs
    h1, h1b, sv1 = ffn_fwd(h, hb, W['wg1'], W['wu1'], W['wd1'], W['ln1_g'], W['ln1_b'], lp)
    proj = matmul(h1b[None], W['w_in'][None], tb=True, name="proj")[0]
    cqn, ckvn, kr = mla_prep(proj, cos, sin, rot, W['q_norm_g'], W['kv_norm_g'], lp)
    qn, qr, kn, v = mla_heads(cqn, ckvn, cos, sin, rot, W['wqn'], W['wqr'], W['wkn'], W['wv'], lp)
    o = attn_fwd(qn, qr, kn, v, kr, lp)
    ya = matmul(o, W['mla_wo'], ab='r', bb='r', name="mla_out")[0]
    vconv = conv_fwd(proj, W['conv_w'], W['conv_b'], lp)
    yb = matmul(vconv[None], W['conv_wout'][None], name="conv_out")[0]
    ub = s5_u(proj, lp)
    ab_re, ab_im, bb = _s5_consts(W)
    bu = matmul(ub[None], bb, bb='o', name="s5_bu")
    s = s5_scan(bu, ab_re, ab_im, lp)
    ys = matmul(s, W['s5_ct'], ab='r', bb='r', name="s5_cs")[0]
    y, ygb = s5_y(ys, proj, W['s5_d'], lp)
    zg = matmul(ygb[None], W['s5_wglu'][None], name="s5_glu_mm")[0]
    glb = s5_glu(zg, y, W['s5_b_glu'], lp)
    yc = matmul(glb[None], W['s5_wout'][None], name="s5_out")[0]
    mixed = merge_fwd(proj, ya, yb, yc, lp)
    z2, h2, h2b = mm_res_ln(mixed[None], W['w_o'][None], h1, W['ln2_g'], W['ln2_b'], scale=1.0, name="wo_ln", lp=lp)
    h3, h3b, sv3 = ffn_fwd(h2, h2b, W['wg2'], W['wu2'], W['wd2'], W['ln3_g'], W['ln3_b'], lp)
    sv = dict(sv1=sv1, sv3=sv3, h1b=h1b, proj=proj, cqn=cqn, ckvn=ckvn, kr=kr, qn=qn, qr=qr, kn=kn, v=v, o=o, ya=ya,
              vconv=vconv, yb=yb, ub=ub, ab_re=ab_re, ab_im=ab_im, bb=bb, s=s, y=y, ygb=ygb, zg=zg, glb=glb, yc=yc,
              mixed=mixed, z2=z2)
    return h3, h3b, sv


def layer_bwd(dh3, sv, W, tabs, lp):
    cos, sin, rot = tabs
    proj = sv['proj']
    G = {}
    dh2, g3 = ffn_bwd(dh3, sv['sv3'], W['wg2'], W['wu2'], W['wd2'], W['ln3_g'], lp)
    G.update(wg2=g3['wg'], wu2=g3['wu'], wd2=g3['wd'], ln3_g=g3['ln_g'], ln3_b=g3['ln_b'])
    dz2, dz2b, G['ln2_g'], G['ln2_b'] = ln_bwd(dh2, sv['z2'], W['ln2_g'], fscale=1.0, name="wo_ln_bwd", lp=lp)
    dmix = matmul(dz2b[None], W['w_o'][None], tb=True, name="wo_dx")[0]
    G['w_o'] = matmul(sv['mixed'][None], dz2b[None], ta=True, out_dtype=WGRAD, name="wo_dw")[0]
    dya, dyb, dyc, dg0, dg1, dg2 = merge_bwd(dmix, proj, sv['ya'], sv['yb'], sv['yc'], lp)
    dgl = matmul(dyc[None], W['s5_wout'][None], tb=True, name="s5_out_dx")[0]
    G['s5_wout'] = matmul(sv['glb'][None], dyc[None], ta=True, out_dtype=WGRAD, name="s5_out_dw")[0]
    t1, dzb, G['s5_b_glu'] = s5_glu_bwd(dgl, sv['zg'], sv['y'], W['s5_b_glu'], lp)
    dyg = matmul(dzb[None], W['s5_wglu'][None], tb=True, res=t1[None], name="s5_glu_dx")[0]
    G['s5_wglu'] = matmul(sv['ygb'][None], dzb[None], ta=True, out_dtype=WGRAD, name="s5_glu_dw")[0]
    dyb_, du_d, G['s5_d'] = s5_y_bwd(dyg, sv['y'], proj, W['s5_d'], lp)
    ds = matmul(dyb_[None], W['s5_ct'], tb=True, bb='o', name="s5_cs_dx")
    G['s5_ct'] = matmul(sv['s'], dyb_[None], ta=True, ab='o', name="s5_cs_dw")
    g_adj, d_ab = s5_scan_bwd(ds, sv['s'], sv['ab_re'], sv['ab_im'], lp)
    du = matmul(g_adj, sv['bb'], tb=True, ab='r', bb='r', res=du_d[None], name="s5_bu_dx")[0]
    d_bb = matmul(sv['ub'][None], g_adj, ta=True, bb='o', name="s5_bu_dw")
    du_b = s5_du(du, lp)
    onehot = (jnp.arange(S5_GROUP) == 0).astype(F32)
    spread = lambda t: (t.reshape(S5_GROUPS, 1, S5_STATE) * onehot[None, :, None]).reshape(_S5_ROWS, S5_STATE)
    take = lambda t: _blockdiag_extract(t, S5_GROUPS, S5_GROUP, S5_STATE).reshape(_S5_ROWS, S5_STATE)
    sel = jnp.kron(jnp.eye(S5_GROUPS, dtype=F32), jnp.ones((1, S5_GROUP), F32))
    (G['s5_a_re'], G['s5_a_im'], G['s5_log_dt'], G['s5_b_re'], G['s5_b_im']) = s5_prep_bwd(
        W['s5_a_re'], W['s5_a_im'], W['s5_log_dt'], W['s5_b_re'], W['s5_b_im'],
        spread(d_ab[0]), spread(d_ab[1]), take(d_bb[0]), take(d_bb[1]), sel)
    dv = matmul(dyb[None], W['conv_wout'][None], tb=True, name="conv_out_dx")[0]
    G['conv_wout'] = matmul(sv['vconv'][None], dyb[None], ta=True, out_dtype=WGRAD, name="conv_out_dw")[0]
    dxbar, dbg, dcg, G['conv_w'], G['conv_b'] = conv_bwd(dv, proj, W['conv_w'], W['conv_b'], lp)
    do = matmul(dya[None], W['mla_wo'], tb=True, bb='o', out_dtype=BF16, name="mla_out_dx")
    G['mla_wo'] = matmul(sv['o'], dya[None], ta=True, ab='o', out_dtype=WGRAD, name="mla_out_dw")
    dqn, dqr, dkn, dvv, dkr = attn_bwd(sv['qn'], sv['qr'], sv['kn'], sv['v'], sv['kr'], do, lp)
    dcq, dckv, dqrp = mla_heads_bwd(dqn, dqr, dkn, dvv, cos, sin, rot, W['wqn'], W['wqr'], W['wkn'], W['wv'], lp)
    G['wqn'] = matmul(sv['cqn'][None], dqn, ta=True, bb='o', out_dtype=WGRAD, name="mla_dwqn")
    G['wqr'] = matmul(sv['cqn'][None], dqrp, ta=True, bb='o', out_dtype=WGRAD, name="mla_dwqr")
    G['wkn'] = matmul(sv['ckvn'][None], dkn, ta=True, bb='o', out_dtype=WGRAD, name="mla_dwkn")
    G['wv'] = matmul(sv['ckvn'][None], dvv, ta=True, bb='o', out_dtype=WGRAD, name="mla_dwv")
    dcq_raw, dckv_raw, dkr_raw, G['q_norm_g'], G['kv_norm_g'] = mla_prep_bwd(
        dcq, dckv, dkr, proj, cos, sin, rot, W['q_norm_g'], W['kv_norm_g'], lp)
    zeros = lambda n: jnp.zeros((lp, n), BF16)
    dproj = jnp.concatenate([dcq_raw, dkr_raw, zeros(96), dckv_raw, zeros(256), dxbar, dbg, dcg, du_b, dg0, dg1, dg2], axis=1)
    dh1 = matmul(dproj[None], W['w_in'][None], res=dz2[None], res_scale=ALPHA, name="proj_dx")[0]
    G['w_in'] = matmul(dproj[None], sv['h1b'][None], ta=True, out_dtype=WGRAD, name="proj_dw")[0]
    dh0, g1 = ffn_bwd(dh1, sv['sv1'], W['wg1'], W['wu1'], W['wd1'], W['ln1_g'], lp)
    G.update(wg1=g1['wg'], wu1=g1['wu'], wd1=g1['wd'], ln1_g=g1['ln_g'], ln1_b=g1['ln_b'])
    return dh0, G


def _nat_cols(st):
    return jnp.transpose(st, (1, 0, 2)).reshape(st.shape[1], -1)


def _shard_cols(nat):
    k, n = nat.shape
    return jnp.transpose(nat.reshape(k, N_SHARD, n // N_SHARD), (1, 0, 2))


def _win_pad(wt):
    z = lambda n: jnp.zeros((n, wt.shape[1]), wt.dtype)
    return jnp.concatenate([wt[0:384], wt[640:672], z(96), wt[384:640], z(256), wt[672:]], axis=0)


def _win_unpad(wp):
    return jnp.concatenate([wp[0:384], wp[512:768], wp[384:416], wp[1024:]], axis=0)


_BIG = [('ffn1_w_gate', 'T'), ('ffn1_w_up', 'T'), ('ffn1_w_down', 0), ('w_in', 'T'), ('mla_w_uq', 1), ('mla_w_ukv', 1),
        ('mla_w_o', 1), ('conv_w_out', 1), ('s5_w_glu', 0), ('s5_w_out', 1), ('w_o', 0),
        ('ffn2_w_gate', 'T'), ('ffn2_w_up', 'T'), ('ffn2_w_down', 0)]
_REPL = ['ln1_g', 'ln1_b', 'mla_q_norm_g', 'mla_kv_norm_g', 'conv_b', 's5_a_re', 's5_a_im', 's5_log_dt', 's5_b_re',
         's5_b_im', 's5_c_re', 's5_c_im', 's5_d', 's5_b_glu', 'ln2_g', 'ln2_b', 'ln3_g', 'ln3_b']


def compute_weights(st, small):
    W = {}
    for t in ('1', '2'):
        W['wg' + t], W['wu' + t] = st['ffn%s_w_gate' % t], st['ffn%s_w_up' % t]
        W['wd' + t] = st['ffn%s_w_down' % t]
    W['w_in'] = _win_pad(st['w_in'].reshape(D_IN, D_MODEL))
    uq = jnp.transpose(_nat_cols(st['mla_w_uq']).reshape(Q_RANK, N_HEADS, D_NOPE + D_ROPE), (1, 0, 2))
    W['wqn'], W['wqr'] = uq[:, :, :D_NOPE], uq[:, :, D_NOPE:]
    ukv = jnp.transpose(_nat_cols(st['mla_w_ukv']).reshape(KV_RANK, N_HEADS, D_NOPE + D_V), (1, 0, 2))
    W['wkn'], W['wv'] = ukv[:, :, :D_NOPE], ukv[:, :, D_NOPE:]
    W['mla_wo'] = _nat_cols(st['mla_w_o']).reshape(N_HEADS, D_V, D_MODEL)
    W['conv_wout'] = _nat_cols(st['conv_w_out'])
    W['s5_wglu'] = st['s5_w_glu'].reshape(MIX, MIX)
    W['s5_wout'] = _nat_cols(st['s5_w_out'])
    W['w_o'] = st['w_o'].reshape(D_MODEL, D_MODEL)
    W['conv_w'] = small['conv_w']
    for n in ('ln1_g', 'ln1_b', 'ln2_g', 'ln2_b', 'ln3_g', 'ln3_b', 'conv_b', 's5_b_glu'):
        W[n] = small[n].reshape(1, -1)
    W['q_norm_g'] = small['mla_q_norm_g'].reshape(1, -1)
    W['kv_norm_g'] = small['mla_kv_norm_g'].reshape(1, -1)
    W['s5_d'] = small['s5_d'].reshape(1, MIX)
    rep = lambda t: jnp.repeat(t, S5_GROUP, axis=0)
    W['s5_a_re'], W['s5_a_im'] = rep(small['s5_a_re']), rep(small['s5_a_im'])
    W['s5_log_dt'] = jnp.broadcast_to(rep(small['s5_log_dt'].reshape(S5_GROUPS, 1)), (_S5_ROWS, S5_STATE))
    tr = lambda t: jnp.transpose(t, (0, 2, 1)).reshape(_S5_ROWS, S5_STATE)
    W['s5_b_re'], W['s5_b_im'] = tr(small['s5_b_re']), tr(small['s5_b_im'])
    ct = lambda t: _blockdiag(jnp.transpose(t, (0, 2, 1)))
    W['s5_ct'] = jnp.stack([ct(small['s5_c_re']), -ct(small['s5_c_im'])]).astype(BF16)
    return W


def shard_grads(G):
    R = reference_grads(G, ffn=False)
    S = {n: (_shard_of(R[n], a) if n in R else None) for n, a in _BIG}
    for t in ('1', '2'):
        S['ffn%s_w_gate' % t] = G['wg' + t]
        S['ffn%s_w_up' % t] = G['wu' + t]
        S['ffn%s_w_down' % t] = G['wd' + t]
    S['conv_w'] = _shard_cols(R['conv_w'])
    for n in _REPL:
        S[n] = R[n]
    return S


def reference_grads(G, ffn=True):
    R = {}
    for t in ('1', '2') if ffn else ():
        R['ffn%s_w_gate' % t] = G['wg' + t].reshape(D_FF, D_MODEL).T
        R['ffn%s_w_up' % t] = G['wu' + t].reshape(D_FF, D_MODEL).T
        R['ffn%s_w_down' % t] = G['wd' + t].reshape(D_FF, D_MODEL)
    R['w_in_t'] = _win_unpad(G['w_in'])
    if ffn:
        R['w_in'] = R['w_in_t'].T
    R['mla_w_uq'] = jnp.transpose(jnp.concatenate([G['wqn'], G['wqr']], axis=2), (1, 0, 2)).reshape(Q_RANK, -1)
    R['mla_w_ukv'] = jnp.transpose(jnp.concatenate([G['wkn'], G['wv']], axis=2), (1, 0, 2)).reshape(KV_RANK, -1)
    R['mla_w_o'] = G['mla_wo'].reshape(N_HEADS * D_V, D_MODEL)
    R['conv_w'], R['conv_w_out'] = G['conv_w'], G['conv_wout']
    R['s5_w_glu'], R['s5_w_out'], R['w_o'] = G['s5_wglu'], G['s5_wout'], G['w_o']
    for n in ('ln1_g', 'ln1_b', 'ln2_g', 'ln2_b', 'ln3_g', 'ln3_b', 'conv_b', 's5_b_glu'):
        R[n] = G[n].reshape(-1)
    R['mla_q_norm_g'], R['mla_kv_norm_g'] = G['q_norm_g'].reshape(-1), G['kv_norm_g'].reshape(-1)
    R['s5_d'] = G['s5_d'].reshape(S5_GROUPS, S5_GROUP)
    R['s5_a_re'], R['s5_a_im'], R['s5_log_dt'] = G['s5_a_re'], G['s5_a_im'], G['s5_log_dt'].reshape(-1)
    untr = lambda t: jnp.transpose(t.reshape(S5_GROUPS, S5_GROUP, S5_STATE), (0, 2, 1))
    R['s5_b_re'], R['s5_b_im'] = untr(G['s5_b_re']), untr(G['s5_b_im'])
    unct = lambda t: jnp.transpose(_blockdiag_extract(t, S5_GROUPS, S5_STATE, S5_GROUP), (0, 2, 1))
    R['s5_c_re'], R['s5_c_im'] = unct(G['s5_ct'][0]), -unct(G['s5_ct'][1])
    return R


def local_step(x2d, tgt2d, meta, Ws):
    lp = x2d.shape[0] + X0
    tabs = _rope_tables(lp)
    h = jnp.concatenate([jnp.zeros((PAD, D_MODEL), F32), meta, x2d], axis=0)
    hb = h.astype(BF16)
    saved = []
    for W in Ws:
        h, hb, sv = layer_fwd(h, hb, W, tabs, lp)
        saved.append(sv)
    tgt = jnp.concatenate([jnp.zeros((X0, D_MODEL), F32), tgt2d], axis=0)
    dh, loss = loss_head(h, tgt, lp)
    grads = [None] * len(Ws)
    for li in reversed(range(len(Ws))):
        dh, grads[li] = layer_bwd(dh, saved[li], Ws[li], tabs, lp)
    return loss, dh[X0:], dh[PAD:X0], grads


_ANY = pl.BlockSpec(memory_space=pl.ANY)
LANES = 1024


def _half_rows(n):
    rows = -(-n // (2 * LANES))
    q = 512 if rows > 512 else 16
    return -(-rows // q) * q


def _place():
    x, y, c = lax.axis_index("x"), lax.axis_index("y"), lax.axis_index("c")
    chips = [(1 - x, y), (x, 1 - y), (1 - x, 1 - y)]
    return x, y, c, chips


def all_gather_halves(src):
    def body(src_ref, out_ref, send_sems, recv_sems, local_sem):
        x, y, c, chips = _place()
        me = 2 * x + y
        sibling = (x, y, 1 - c)

        def copy(k, src, chip_idx, half, to):
            return pltpu.make_async_remote_copy(src_ref=src, dst_ref=out_ref.at[chip_idx, half], send_sem=send_sems.at[k],
                                                recv_sem=recv_sems.at[k], device_id=to, device_id_type=MESH)

        mine = pltpu.make_async_copy(src_ref, out_ref.at[me], local_sem)
        mine.start()
        first = [copy(j, src_ref.at[c], me, c, (*chip, c)) for j, chip in enumerate(chips)]
        for cp in first:
            cp.start()
        passed = []
        for j, chip in enumerate(chips):
            idx = 2 * chip[0] + chip[1]
            copy(j, src_ref.at[c], idx, c, sibling).wait_recv()
            cp = copy(3 + j, out_ref.at[idx, c], idx, c, sibling)
            cp.start()
            passed.append(cp)
        for j, chip in enumerate(chips):
            copy(3 + j, src_ref.at[c], 2 * chip[0] + chip[1], 1 - c, sibling).wait_recv()
        for cp in first + passed:
            cp.wait_send()
        mine.wait()

    return pl.pallas_call(
        body, name="all_gather_weights", in_specs=[_ANY], out_specs=_ANY,
        out_shape=jax.ShapeDtypeStruct((N_SHARD,) + src.shape, src.dtype),
        scratch_shapes=[pltpu.SemaphoreType.DMA((6,)), pltpu.SemaphoreType.DMA((6,)), pltpu.SemaphoreType.DMA],
    )(src)


def pair_swap(gs):
    n = len(gs)

    def body(*refs):
        g_refs, r_refs, send_sems, recv_sems = refs[:n], refs[n:2 * n], refs[2 * n], refs[2 * n + 1]
        x, y, c, _ = _place()
        copies = []
        for k in range(n):
            for j in range(N_SHARD):
                copies.append(pltpu.make_async_remote_copy(
                    src_ref=g_refs[k].at[j, 1 - c], dst_ref=r_refs[k].at[j], send_sem=send_sems.at[k * N_SHARD + j],
                    recv_sem=recv_sems.at[k * N_SHARD + j], device_id=(x, y, 1 - c), device_id_type=MESH))
        for cp in copies:
            cp.start()
        for cp in copies:
            cp.wait()

    return pl.pallas_call(
        body, name="grad_pair_swap", in_specs=[_ANY] * n, out_specs=[_ANY] * n,
        out_shape=[jax.ShapeDtypeStruct((N_SHARD,) + g.shape[2:], g.dtype) for g in gs],
        scratch_shapes=[pltpu.SemaphoreType.DMA((n * N_SHARD,)), pltpu.SemaphoreType.DMA((n * N_SHARD,))],
    )(*gs)


def _flat_tile(rows):
    return 512 if rows % 512 == 0 else rows


def pair_add(g, r, cidx, out_dtype, name):
    rows = g.shape[2]
    tr = _flat_tile(rows)

    def body(c_ref, g_ref, r_ref, o_ref):
        o_ref[...] = (g_ref[...] + r_ref[...]).astype(out_dtype)

    return pl.pallas_call(
        body, name=name,
        grid_spec=pltpu.PrefetchScalarGridSpec(
            num_scalar_prefetch=1, grid=(N_SHARD, rows // tr),
            in_specs=[pl.BlockSpec((None, None, tr, LANES), lambda j, i, c: (j, c[0], i, 0)),
                      pl.BlockSpec((None, tr, LANES), lambda j, i, c: (j, i, 0))],
            out_specs=pl.BlockSpec((None, tr, LANES), lambda j, i, c: (j, i, 0))),
        out_shape=jax.ShapeDtypeStruct((N_SHARD, rows, LANES), out_dtype),
        compiler_params=_cparams(),
    )(cidx, g, r)


def chip_scatter(ps):
    n = len(ps)

    def body(*refs):
        p_refs, r_refs, send_sems, recv_sems, local_sems = refs[:n], refs[n:2 * n], refs[2 * n], refs[2 * n + 1], refs[2 * n + 2]
        x, y, c, chips = _place()
        me = 2 * x + y
        local, copies = [], []
        for k in range(n):
            cp = pltpu.make_async_copy(p_refs[k].at[me], r_refs[k].at[me], local_sems.at[k])
            cp.start()
            local.append(cp)
            for j, chip in enumerate(chips):
                copies.append(pltpu.make_async_remote_copy(
                    src_ref=p_refs[k].at[2 * chip[0] + chip[1]], dst_ref=r_refs[k].at[me], send_sem=send_sems.at[k * 3 + j],
                    recv_sem=recv_sems.at[k * 3 + j], device_id=(*chip, c), device_id_type=MESH))
        for cp in copies:
            cp.start()
        for cp in copies:
            cp.wait()
        for cp in local:
            cp.wait()

    return pl.pallas_call(
        body, name="grad_chip_scatter", in_specs=[_ANY] * n, out_specs=[_ANY] * n,
        out_shape=[jax.ShapeDtypeStruct(p.shape, p.dtype) for p in ps],
        scratch_shapes=[pltpu.SemaphoreType.DMA((n * 3,)), pltpu.SemaphoreType.DMA((n * 3,)), pltpu.SemaphoreType.DMA((n,))],
    )(*ps)


def sum_slots(r, name):
    rows = r.shape[1]
    tr = _flat_tile(rows)

    def body(r_ref, o_ref):
        o_ref[...] = ((r_ref[0].astype(F32) + r_ref[1].astype(F32)) + r_ref[2].astype(F32)) + r_ref[3].astype(F32)

    return pl.pallas_call(
        body, name=name, grid=(rows // tr,),
        in_specs=[pl.BlockSpec((N_SHARD, tr, LANES), lambda i: (0, i, 0))],
        out_specs=pl.BlockSpec((tr, LANES), lambda i: (i, 0)),
        out_shape=jax.ShapeDtypeStruct((rows, LANES), F32), compiler_params=_cparams(),
    )(r)


def pair_gather(hs):
    n = len(hs)

    def body(*refs):
        h_refs, f_refs, send_sems, recv_sems, local_sems = refs[:n], refs[n:2 * n], refs[2 * n], refs[2 * n + 1], refs[2 * n + 2]
        x, y, c, _ = _place()
        local, copies = [], []
        for k in range(n):
            cp = pltpu.make_async_copy(h_refs[k], f_refs[k].at[c], local_sems.at[k])
            cp.start()
            local.append(cp)
            copies.append(pltpu.make_async_remote_copy(
                src_ref=h_refs[k], dst_ref=f_refs[k].at[c], send_sem=send_sems.at[k], recv_sem=recv_sems.at[k],
                device_id=(x, y, 1 - c), device_id_type=MESH))
        for cp in copies:
            cp.start()
        for cp in copies:
            cp.wait()
        for cp in local:
            cp.wait()

    return pl.pallas_call(
        body, name="grad_pair_gather", in_specs=[_ANY] * n, out_specs=[_ANY] * n,
        out_shape=[jax.ShapeDtypeStruct((2,) + h.shape, h.dtype) for h in hs],
        scratch_shapes=[pltpu.SemaphoreType.DMA((n,)), pltpu.SemaphoreType.DMA((n,)), pltpu.SemaphoreType.DMA((n,))],
    )(*hs)


def reduce_scatter(g_big, g_small, cidx):
    r_big, r_small = pair_swap([g_big, g_small])
    p_big = pair_add(g_big, r_big, cidx, BF16, "grad_pair_add")
    p_small = pair_add(g_small, r_small, cidx, F32, "grad_pair_add_small")
    q_big, q_small = chip_scatter([p_big, p_small])
    h_big = sum_slots(q_big, "grad_chip_sum")
    h_small = sum_slots(q_small, "grad_chip_sum_small")
    f_big, f_small = pair_gather([h_big, h_small])
    return f_big.reshape(-1, LANES), f_small.reshape(-1, LANES)


def _rows_of(c, half):
    return pl.ds(pl.multiple_of(c * half, 8), half)


def all_gather_shards(srcs, exact):
    n, m = len(srcs), len(exact)
    halves = [s.shape[0] // 2 for s in srcs]

    def body(*refs):
        s_refs, e_refs = refs[:n], refs[n:n + m]
        o_refs, eo_refs = refs[n + m:2 * n + m], refs[2 * n + m:2 * n + 2 * m]
        send, recv, esend, erecv, osend, orecv, lsem = refs[2 * n + 2 * m:]
        x, y, c, chips = _place()
        me = 2 * x + y
        sibling = (x, y, 1 - c)
        own = [pltpu.make_async_remote_copy(src_ref=s_refs[k], dst_ref=o_refs[k].at[me], send_sem=osend.at[k],
                                            recv_sem=orecv.at[k], device_id=sibling, device_id_type=MESH) for k in range(n)]
        local = [pltpu.make_async_copy(e_refs[k], eo_refs[k].at[me], lsem.at[k]) for k in range(m)]
        for cp in own + local:
            cp.start()

        def copy(k, s, src, idx, half_c, to):
            return pltpu.make_async_remote_copy(
                src_ref=src, dst_ref=o_refs[k].at[idx, _rows_of(half_c, halves[k])], send_sem=send.at[6 * k + s],
                recv_sem=recv.at[6 * k + s], device_id=to, device_id_type=MESH)

        def ecopy(k, j, idx, to):
            return pltpu.make_async_remote_copy(src_ref=e_refs[k], dst_ref=eo_refs[k].at[idx], send_sem=esend.at[3 * k + j],
                                                recv_sem=erecv.at[3 * k + j], device_id=to, device_id_type=MESH)

        sends = []
        for k in range(n):
            mine = s_refs[k].at[_rows_of(c, halves[k])]
            sends += [copy(k, j, mine, me, c, (*chip, c)) for j, chip in enumerate(chips)]
        for k in range(m):
            sends += [ecopy(k, j, me, (*chip, c)) for j, chip in enumerate(chips)]
        for cp in sends:
            cp.start()
        for j, chip in enumerate(chips):
            idx = 2 * chip[0] + chip[1]
            for k in range(n):
                landed = o_refs[k].at[idx, _rows_of(c, halves[k])]
                copy(k, j, landed, idx, c, sibling).wait_recv()
                fwd = copy(k, 3 + j, landed, idx, c, sibling)
                fwd.start()
                sends.append(fwd)
        for j, chip in enumerate(chips):
            idx = 2 * chip[0] + chip[1]
            for k in range(n):
                copy(k, 3 + j, s_refs[k].at[_rows_of(c, halves[k])], idx, 1 - c, sibling).wait_recv()
            for k in range(m):
                ecopy(k, j, idx, sibling).wait_recv()
        for cp in sends:
            cp.wait_send()
        for cp in own + local:
            cp.wait()

    outs = pl.pallas_call(
        body, name="all_gather_weights", in_specs=[_ANY] * (n + m), out_specs=[_ANY] * (n + m),
        out_shape=[jax.ShapeDtypeStruct((N_SHARD,) + a.shape, a.dtype) for a in list(srcs) + list(exact)],
        scratch_shapes=[pltpu.SemaphoreType.DMA((6 * n,)), pltpu.SemaphoreType.DMA((6 * n,)),
                        pltpu.SemaphoreType.DMA((3 * m,)), pltpu.SemaphoreType.DMA((3 * m,)),
                        pltpu.SemaphoreType.DMA((n,)), pltpu.SemaphoreType.DMA((n,)), pltpu.SemaphoreType.DMA((m,))],
    )(*srcs, *exact)
    return outs[:n], outs[n:]


def rs_pair_swap(gs):
    n = len(gs)

    def body(*refs):
        g_refs, r_refs, send, recv = refs[:n], refs[n:2 * n], refs[2 * n], refs[2 * n + 1]
        x, y, c, _ = _place()
        copies = [pltpu.make_async_remote_copy(
            src_ref=g_refs[k].at[pl.ds(0, N_SHARD), _rows_of(1 - c, gs[k].shape[1] // 2)], dst_ref=r_refs[k],
            send_sem=send.at[k], recv_sem=recv.at[k], device_id=(x, y, 1 - c), device_id_type=MESH) for k in range(n)]
        for cp in copies:
            cp.start()
        for cp in copies:
            cp.wait()

    return pl.pallas_call(
        body, name="grad_pair_swap", in_specs=[_ANY] * n, out_specs=[_ANY] * n,
        out_shape=[jax.ShapeDtypeStruct((N_SHARD, g.shape[1] // 2, g.shape[2]), g.dtype) for g in gs],
        scratch_shapes=[pltpu.SemaphoreType.DMA((n,)), pltpu.SemaphoreType.DMA((n,))],
    )(*gs)


def _group_tile(half, n_cols, n_arrays):
    budget = (20 * 2 ** 20) // (6 * n_arrays)
    fits = [t for t in range(8, half + 1, 8) if half % t == 0 and t * n_cols * 4 <= budget]
    return max(fits) if fits else 8


def rs_pair_add(gs, rs, cidx, out_dtype, name):
    n = len(gs)
    _, K, cols = gs[0].shape
    half = K // 2
    tr = _group_tile(half, cols, n)
    nb = half // tr

    def body(c_ref, *refs):
        for g_ref, r_ref, o_ref in zip(refs[:n], refs[n:2 * n], refs[2 * n:]):
            o_ref[...] = (g_ref[...].astype(F32) + r_ref[...].astype(F32)).astype(out_dtype)

    gspec = pl.BlockSpec((None, tr, cols), lambda j, i, c: (j, c[0] * nb + i, 0))
    rspec = pl.BlockSpec((None, tr, cols), lambda j, i, c: (j, i, 0))
    return pl.pallas_call(
        body, name=name,
        grid_spec=pltpu.PrefetchScalarGridSpec(num_scalar_prefetch=1, grid=(N_SHARD, nb), in_specs=[gspec] * n + [rspec] * n,
                                               out_specs=[rspec] * n),
        out_shape=[jax.ShapeDtypeStruct((N_SHARD, half, cols), out_dtype)] * n,
        compiler_params=_cparams(),
    )(cidx, *gs, *rs)


def rs_chip_scatter(ps):
    n = len(ps)

    def body(*refs):
        p_refs, q_refs, send, recv, lsem = refs[:n], refs[n:2 * n], refs[2 * n], refs[2 * n + 1], refs[2 * n + 2]
        x, y, c, chips = _place()
        me = 2 * x + y
        local = [pltpu.make_async_copy(p_refs[k].at[me], q_refs[k].at[me], lsem.at[k]) for k in range(n)]
        copies = [pltpu.make_async_remote_copy(
            src_ref=p_refs[k].at[2 * chip[0] + chip[1]], dst_ref=q_refs[k].at[me], send_sem=send.at[3 * k + j],
            recv_sem=recv.at[3 * k + j], device_id=(*chip, c), device_id_type=MESH)
            for k in range(n) for j, chip in enumerate(chips)]
        for cp in local + copies:
            cp.start()
        for cp in copies:
            cp.wait()
        for cp in local:
            cp.wait()

    return pl.pallas_call(
        body, name="grad_chip_scatter", in_specs=[_ANY] * n, out_specs=[_ANY] * n,
        out_shape=[jax.ShapeDtypeStruct(p.shape, p.dtype) for p in ps],
        scratch_shapes=[pltpu.SemaphoreType.DMA((3 * n,)), pltpu.SemaphoreType.DMA((3 * n,)), pltpu.SemaphoreType.DMA((n,))],
    )(*ps)


def rs_chip_sum(qs, nl, cidx, name):
    n = len(qs)
    _, half, cols = qs[0].shape
    tr = _group_tile(half, cols, n)
    nb = half // tr

    def body(c_ref, *refs):
        for k, q_ref in enumerate(refs[:n]):
            o_ref = refs[n + k // nl]
            o_ref[k % nl] = ((q_ref[0].astype(F32) + q_ref[1].astype(F32)) + q_ref[2].astype(F32)) + q_ref[3].astype(F32)

    return pl.pallas_call(
        body, name=name,
        grid_spec=pltpu.PrefetchScalarGridSpec(
            num_scalar_prefetch=1, grid=(nb,),
            in_specs=[pl.BlockSpec((N_SHARD, tr, cols), lambda i, c: (0, i, 0))] * n,
            out_specs=[pl.BlockSpec((nl, tr, cols), lambda i, c: (0, c[0] * nb + i, 0))] * (n // nl)),
        out_shape=[jax.ShapeDtypeStruct((nl, 2 * half, cols), F32)] * (n // nl),
        compiler_params=_cparams(),
    )(cidx, *qs)


def rs_pair_gather(fs):
    n = len(fs)

    def body(*refs):
        f_refs, send, recv = refs[n:2 * n], refs[2 * n], refs[2 * n + 1]
        x, y, c, _ = _place()
        copies = []
        for k in range(n):
            rows = f_refs[k].at[pl.ds(0, fs[k].shape[0]), _rows_of(c, fs[k].shape[1] // 2)]
            copies.append(pltpu.make_async_remote_copy(src_ref=rows, dst_ref=rows, send_sem=send.at[k], recv_sem=recv.at[k],
                                                       device_id=(x, y, 1 - c), device_id_type=MESH))
        for cp in copies:
            cp.start()
        for cp in copies:
            cp.wait()

    return pl.pallas_call(
        body, name="grad_pair_gather", in_specs=[_ANY] * n, out_specs=[_ANY] * n,
        out_shape=[jax.ShapeDtypeStruct(f.shape, f.dtype) for f in fs],
        input_output_aliases={k: k for k in range(n)},
        scratch_shapes=[pltpu.SemaphoreType.DMA((n,)), pltpu.SemaphoreType.DMA((n,))],
    )(*fs)


_HBM = pl.BlockSpec(memory_space=pltpu.HBM)
_SEM = pl.BlockSpec(memory_space=pltpu.SEMAPHORE)
_EFFECT = pltpu.SideEffectType.DATAFLOW_SIDE_EFFECTING


def _in_hbm(a):
    return pltpu.with_memory_space_constraint(a, pltpu.HBM)


def split_start(name, srcs, lands, after, copies_fn, n_copies):
    n = len(srcs)

    def body(*refs):
        for cp in copies_fn(refs[:n], refs[n:2 * n], refs[2 * n + 1], refs[2 * n + 2]):
            cp.start()
        refs[-1][...] = jnp.zeros_like(refs[-1])

    bufs = list(srcs) + list(lands)
    outs = pl.pallas_call(
        body, name=name,
        out_shape=(pltpu.SemaphoreType.DMA((n_copies,)), pltpu.SemaphoreType.DMA((n_copies,)),
                   *[pltpu.HBM(a.shape, a.dtype) for a in bufs], jax.ShapeDtypeStruct((8, 128), F32)),
        in_specs=[_HBM] * (2 * n) + [_ANY],
        out_specs=(_SEM, _SEM, *[_HBM] * (2 * n), pl.BlockSpec(memory_space=pltpu.VMEM)),
        input_output_aliases={i: 2 + i for i in range(2 * n)},
        compiler_params=pltpu.CompilerParams(has_side_effects=_EFFECT),
    )(*[_in_hbm(a) for a in bufs], after)
    return outs[0], outs[1], outs[2:2 + n], outs[2 + n:2 + 2 * n], outs[-1]


def split_wait(name, send, recv, srcs, lands, after, copies_fn):
    n = len(srcs)

    def body(*refs):
        copies = copies_fn(refs[:n], refs[n:2 * n], refs[2 * n], refs[2 * n + 1])
        for cp in copies:
            cp.wait_send()
        for cp in copies:
            cp.wait_recv()

    bufs = list(srcs) + list(lands)
    outs = pl.pallas_call(
        body, name=name, out_shape=tuple(pltpu.HBM(a.shape, a.dtype) for a in bufs),
        in_specs=[_HBM] * (2 * n) + [_SEM, _SEM, _ANY], out_specs=tuple([_HBM] * (2 * n)),
        input_output_aliases={i: i for i in range(2 * n)},
        compiler_params=pltpu.CompilerParams(has_side_effects=_EFFECT),
    )(*bufs, send, recv, after)
    return list(outs[n:])


def _gather_copies(s_refs, l_refs, send, recv):
    x, y, c, chips = _place()
    me = 2 * x + y
    out = []
    for k, (s, l) in enumerate(zip(s_refs, l_refs)):
        rows = _rows_of(c, s.shape[0] // 2)
        for j, chip in enumerate(chips):
            out.append(pltpu.make_async_remote_copy(src_ref=s.at[rows], dst_ref=l.at[me, rows], send_sem=send.at[4 * k + j],
                                                    recv_sem=recv.at[4 * k + j], device_id=(*chip, c), device_id_type=MESH))
        out.append(pltpu.make_async_remote_copy(src_ref=s, dst_ref=l.at[me], send_sem=send.at[4 * k + 3],
                                                recv_sem=recv.at[4 * k + 3], device_id=(x, y, 1 - c), device_id_type=MESH))
    return out


def _scatter_copies(s_refs, l_refs, send, recv):
    x, y, c, chips = _place()
    me = 2 * x + y
    return [pltpu.make_async_remote_copy(src_ref=s.at[2 * chip[0] + chip[1]], dst_ref=l.at[me], send_sem=send.at[3 * k + j],
                                         recv_sem=recv.at[3 * k + j], device_id=(*chip, c), device_id_type=MESH)
            for k, (s, l) in enumerate(zip(s_refs, l_refs)) for j, chip in enumerate(chips)]


def gather_forward(lands):
    n = len(lands)

    def body(*refs):
        l_refs, send, recv = refs[n:2 * n], refs[2 * n], refs[2 * n + 1]
        x, y, c, chips = _place()
        copies = []
        for k in range(n):
            rows = _rows_of(c, lands[k].shape[1] // 2)
            for j, chip in enumerate(chips):
                part = l_refs[k].at[2 * chip[0] + chip[1], rows]
                copies.append(pltpu.make_async_remote_copy(src_ref=part, dst_ref=part, send_sem=send.at[3 * k + j],
                                                           recv_sem=recv.at[3 * k + j], device_id=(x, y, 1 - c),
                                                           device_id_type=MESH))
        for cp in copies:
            cp.start()
        for cp in copies:
            cp.wait()

    return pl.pallas_call(
        body, name="gather_forward", in_specs=[_ANY] * n, out_specs=[_ANY] * n,
        out_shape=[jax.ShapeDtypeStruct(a.shape, a.dtype) for a in lands],
        input_output_aliases={k: k for k in range(n)},
        scratch_shapes=[pltpu.SemaphoreType.DMA((3 * n,)), pltpu.SemaphoreType.DMA((3 * n,))],
    )(*lands)


def rs_partials(gs, wire, cidx, tag):
    rs = rs_pair_swap(gs)
    groups = {}
    for k, g in enumerate(gs):
        groups.setdefault((g.shape, jnp.dtype(wire[k]).name), []).append(k)
    ps = [None] * len(gs)
    for gi, ks in enumerate(groups.values()):
        outs = rs_pair_add([gs[k] for k in ks], [rs[k] for k in ks], cidx, wire[ks[0]], "grad_pair_add_%s%d" % (tag, gi))
        for k, o in zip(ks, outs):
            ps[k] = o
    return ps


def rs_finish(items):
    cidx = lax.axis_index("c").astype(jnp.int32).reshape(1)
    groups = {}
    for i, it in enumerate(items):
        groups.setdefault((it[0].shape, len(it), it[0].dtype.name), []).append(i)
    fs = [None] * len(items)
    for gi, ids in enumerate(groups.values()):
        outs = rs_chip_sum([q for i in ids for q in items[i]], len(items[ids[0]]), cidx, "grad_chip_sum_%d" % gi)
        for i, o in zip(ids, outs):
            fs[i] = o
    return rs_pair_gather(fs)


def reduce_scatter_shards(items, wire_of, cidx):
    nl = [len(it) for it in items]
    gs = [g for it in items for g in it]
    wire = [wire_of[i] for i, it in enumerate(items) for _ in it]
    first = np.cumsum([0] + nl)
    rs = rs_pair_swap(gs)
    groups = {}
    for i, it in enumerate(items):
        groups.setdefault((it[0].shape, len(it), jnp.dtype(wire_of[i]).name), []).append(i)
    ps = [None] * len(gs)
    for gi, ids in enumerate(groups.values()):
        ks = [first[i] + l for i in ids for l in range(nl[i])]
        outs = rs_pair_add([gs[k] for k in ks], [rs[k] for k in ks], cidx, wire[ks[0]], "grad_pair_add_%d" % gi)
        for k, o in zip(ks, outs):
            ps[k] = o
    qs = rs_chip_scatter(ps)
    fs = [None] * len(items)
    for gi, ids in enumerate(groups.values()):
        ks = [first[i] + l for i in ids for l in range(nl[i])]
        outs = rs_chip_sum([qs[k] for k in ks], nl[ids[0]], cidx, "grad_chip_sum_%d" % gi)
        for i, o in zip(ids, outs):
            fs[i] = o
    return rs_pair_gather(fs)


def _flat_view(shape):
    n = int(np.prod(shape))
    if n <= 2 ** 20 and n % LANES == 0:
        return (n // LANES, LANES)
    if n <= 2 ** 20 and n % 128 == 0:
        return (n // 128, 128)
    return (n // shape[-1], shape[-1])


def adamw(w, g, m, v, name):
    shape = w.shape
    if w.ndim == 2:
        block, grid, index = shape, (1,), (lambda i: (0, 0))
    else:
        slab = shape[2:]
        unit = 4 * int(np.prod(slab[:-2] or (1,))) * (-(-slab[-1] // 128) * 128)
        if len(slab) >= 2:
            unit *= -(-slab[-2] // 8) * 8
        k = shape[1]
        tr = k
        if k * unit > 2 ** 21:
            tr = max(t for t in range(8, k, 8) if k % t == 0 and t * unit <= 2 ** 21)
        block, grid = (None, tr) + tuple(slab), (shape[0], k // tr)
        index = lambda l, i: (l, i) + (0,) * len(slab)
        if tr < min(k, 64) and len(slab) == 1:
            tc = max(t for t in range(128, slab[0] + 1, 128) if slab[0] % t == 0 and k * t * 4 <= 2 ** 21)
            block, grid = (None, k, tc), (shape[0], slab[0] // tc)
            index = lambda l, i: (l, 0, i)

    def body(w_ref, g_ref, m_ref, v_ref, d_ref, nm_ref, nv_ref):
        g_ = g_ref[...]
        m_new = ADAM_B1 * m_ref[...] + (1.0 - ADAM_B1) * g_
        v_new = ADAM_B2 * v_ref[...] + (1.0 - ADAM_B2) * (g_ * g_)
        m_hat = m_new / (1.0 - ADAM_B1 ** ADAM_STEP)
        v_hat = v_new / (1.0 - ADAM_B2 ** ADAM_STEP)
        d_ref[...] = -ADAM_LR * (m_hat / (jnp.sqrt(v_hat) + ADAM_EPS) + ADAM_WD * w_ref[...])
        nm_ref[...] = m_new
        nv_ref[...] = v_new

    spec = pl.BlockSpec(block, index)
    return pl.pallas_call(
        body, name=name, grid=grid, in_specs=[spec] * 4, out_specs=[spec] * 3,
        out_shape=[jax.ShapeDtypeStruct(shape, F32)] * 3, compiler_params=_cparams(),
    )(w, g, m, v)


_WEIGHTS = ['meta', 'ffn1_w_gate', 'ffn1_w_up', 'ffn1_w_down', 'ln1_g', 'ln1_b', 'w_in', 'mla_q_norm_g', 'mla_w_uq',
            'mla_kv_norm_g', 'mla_w_ukv', 'mla_w_o', 'conv_w', 'conv_b', 'conv_w_out', 's5_a_re', 's5_a_im', 's5_log_dt',
            's5_b_re', 's5_b_im', 's5_c_re', 's5_c_im', 's5_d', 's5_w_glu', 's5_b_glu', 's5_w_out', 'w_o', 'ln2_g', 'ln2_b',
            'ffn2_w_gate', 'ffn2_w_up', 'ffn2_w_down', 'ln3_g', 'ln3_b']


def _pad_to(flat, n):
    return jnp.concatenate([flat, jnp.zeros((n - flat.shape[0],), flat.dtype)])


def _shard_of(full, axis):
    if axis == 1:
        return _shard_cols(full)
    if axis == 'T':
        return full.T.reshape(N_SHARD, full.shape[1] // N_SHARD, full.shape[0])
    return full.reshape(N_SHARD, full.shape[0] // N_SHARD, full.shape[1])


_FFN_KEY = {'gate': 'wg', 'up': 'wu', 'down': 'wd'}


def _pad_rows(a, axis):
    k = a.shape[axis]
    extra = -k % 32
    if not extra:
        return a
    return jnp.pad(a, [(0, extra) if d == axis else (0, 0) for d in range(a.ndim)])


def _step(env):
    w = {n: env[n] for n in _WEIGHTS}
    mom = {n: env['m_' + n] for n in _WEIGHTS}
    var = {n: env['v_' + n] for n in _WEIGHTS}
    cidx = lax.axis_index("c").astype(jnp.int32).reshape(1)
    chip = 2 * lax.axis_index("x") + lax.axis_index("y")
    big_names = [n for n, _ in _BIG]
    nb = len(big_names)

    kept_t = [n for n, a in _BIG if a == 'T']
    own = {n: (jnp.swapaxes(w[n], 1, 2) if n in kept_t else w[n]) for n in big_names}
    srcs = [[_pad_rows(own[n][li].astype(BF16), 0) for n in big_names] for li in range(DEPTH)]
    gathered0, (conv_w_st, meta_st) = all_gather_shards(srcs[0], [w['conv_w'], w['meta']])
    lands = [lax.empty((N_SHARD,) + s.shape, BF16) for s in srcs[1]]
    g_send, g_recv, srcs_t, lands_t, token = split_start("gather1_start", srcs[1], lands, gathered0[0], _gather_copies, 4 * nb)

    def layer_weights(li, st):
        small = {n: w[n][li] for n in _REPL}
        small['conv_w'] = _nat_cols(conv_w_st[:, li])
        return compute_weights({n: a[:, :own[n].shape[1]] for n, a in zip(big_names, st)}, small)

    x2d = env['x'][0]
    lp = x2d.shape[0] + X0
    tabs = _rope_tables(lp)
    h = jnp.concatenate([jnp.zeros((PAD, D_MODEL), F32), _nat_cols(meta_st), x2d], axis=0) + token[0, 0]
    W0 = layer_weights(0, gathered0)
    h, hb, sv0 = layer_fwd(h, h.astype(BF16), W0, tabs, lp)
    lands1 = split_wait("gather1_wait", g_send, g_recv, srcs_t, lands_t, h, _gather_copies)
    W1 = layer_weights(1, gather_forward(lands1))
    h, hb, sv1 = layer_fwd(h, hb, W1, tabs, lp)
    tgt = jnp.concatenate([jnp.zeros((X0, D_MODEL), F32), env['loss_target'][0]], axis=0)
    dh, loss_part = loss_head(h, tgt, lp)
    loss = lax.psum(loss_part[0, 0], ("x", "y", "c"))

    def shards(G):
        full = reference_grads(G, ffn=False)

        def one(n, a):
            if n.startswith('ffn'):
                return G[_FFN_KEY[n.split('_')[-1]] + n[3]]
            if n == 'w_in':
                return full['w_in_t'].reshape(N_SHARD, D_IN // N_SHARD, D_MODEL)
            return _shard_of(full[n], a)

        return [_pad_rows(one(n, a), 1) for n, a in _BIG], full

    dh, G1 = layer_bwd(dh, sv1, W1, tabs, lp)
    gs1, full1 = shards(G1)
    p1 = rs_partials(gs1, [BF16] * nb, cidx, "b")
    q1 = [lax.dynamic_update_slice_in_dim(jnp.zeros_like(p), lax.dynamic_slice_in_dim(p, chip, 1, axis=0), chip, axis=0)
          for p in p1]
    s_send, s_recv, p1_t, q1_t, token2 = split_start("scatter1_start", p1, q1, dh, _scatter_copies, 3 * nb)
    dh, G0 = layer_bwd(dh + token2[0, 0], sv0, W0, tabs, lp)
    q1 = split_wait("scatter1_wait", s_send, s_recv, p1_t, q1_t, dh, _scatter_copies)
    gs0, full0 = shards(G0)
    full = [full0, full1]

    s_parts = [jnp.stack([full[li][n] for li in range(DEPTH)]).reshape(-1) for n in _REPL + ['conv_w']]
    s_parts.append(dh[PAD:X0].reshape(-1))
    s_sizes = [int(p.shape[0]) for p in s_parts]
    s_rows = -(-sum(s_sizes) // (16 * LANES)) * 16
    g_small = _pad_to(jnp.concatenate(s_parts), s_rows * LANES).reshape(1, s_rows, LANES)
    g_small = jnp.broadcast_to(g_small, (N_SHARD, s_rows, LANES))
    q0 = rs_chip_scatter(rs_partials(gs0 + [g_small], [BF16] * nb + [F32], cidx, "a"))
    red = rs_finish([[q0[i], q1[i]] for i in range(nb)] + [[q0[nb]]])
    f_small = red[-1].reshape(-1)

    grad = {n: r[:, :own[n].shape[1]] for n, r in zip(big_names, red[:nb])}
    off = 0
    for n, sz in zip(_REPL + ['conv_w', 'meta'], s_sizes):
        grad[n] = f_small[off:off + sz]
        off += sz
    for n in _REPL:
        grad[n] = grad[n].reshape(w[n].shape)
    cw = grad['conv_w'].reshape(DEPTH, 3, MIX)
    grad['conv_w'] = lax.dynamic_slice_in_dim(cw, chip * (MIX // N_SHARD), MIX // N_SHARD, axis=2)
    gm = grad['meta'].reshape(N_META, D_MODEL)
    grad['meta'] = lax.dynamic_slice_in_dim(gm, chip * (D_MODEL // N_SHARD), D_MODEL // N_SHARD, axis=1)

    delta, new_m, new_v = {}, {}, {}
    for n in _WEIGHTS:
        if n in kept_t:
            outs = adamw(own[n], grad[n], jnp.swapaxes(mom[n], 1, 2), jnp.swapaxes(var[n], 1, 2), "adamw_" + n)
            grad[n], delta[n], new_m[n], new_v[n] = [jnp.swapaxes(t, 1, 2) for t in [grad[n]] + list(outs)]
        else:
            delta[n], new_m[n], new_v[n] = adamw(w[n], grad[n], mom[n], var[n], "adamw_" + n)
    return (loss, dh[X0:][None], *[grad[n] for n in _WEIGHTS], *[delta[n] for n in _WEIGHTS],
            *[new_m[n] for n in _WEIGHTS], *[new_v[n] for n in _WEIGHTS])


def _step_sync(env):
    w = {n: env[n] for n in _WEIGHTS}
    mom = {n: env['m_' + n] for n in _WEIGHTS}
    var = {n: env['v_' + n] for n in _WEIGHTS}
    cidx = lax.axis_index("c").astype(jnp.int32).reshape(1)
    chip = 2 * lax.axis_index("x") + lax.axis_index("y")
    big_names = [n for n, _ in _BIG]
    nb = len(big_names)

    srcs = [w[n][li].astype(BF16) for li in range(DEPTH) for n in big_names]
    gathered, (conv_w_st, meta_st) = all_gather_shards(srcs, [w['conv_w'], w['meta']])
    meta_full = _nat_cols(meta_st)
    Ws = []
    for li in range(DEPTH):
        small = {n: w[n][li] for n in _REPL}
        small['conv_w'] = _nat_cols(conv_w_st[:, li])
        Ws.append(compute_weights({n: gathered[li * nb + i] for i, n in enumerate(big_names)}, small))

    loss_part, dx, dmeta, grads = local_step(env['x'][0], env['loss_target'][0], meta_full, Ws)
    loss = lax.psum(loss_part[0, 0], ("x", "y", "c"))
    full = [reference_grads(g, ffn=False) for g in grads]

    def shard(li, n, axis):
        if n.startswith('ffn'):
            return grads[li][_FFN_KEY[n.split('_')[-1]] + n[3]]
        return _shard_of(full[li][n], axis)

    items = [[shard(li, n, a) for li in range(DEPTH)] for n, a in _BIG]
    s_parts = [jnp.stack([full[li][n] for li in range(DEPTH)]).reshape(-1) for n in _REPL + ['conv_w']] + [dmeta.reshape(-1)]
    s_sizes = [int(p.shape[0]) for p in s_parts]
    s_rows = -(-sum(s_sizes) // (16 * LANES)) * 16
    g_small = _pad_to(jnp.concatenate(s_parts), s_rows * LANES).reshape(1, s_rows, LANES)
    g_small = jnp.broadcast_to(g_small, (N_SHARD, s_rows, LANES))
    red = reduce_scatter_shards(items + [[g_small]], [BF16] * nb + [F32], cidx)
    f_small = red[-1].reshape(-1)

    grad = dict(zip(big_names, red[:nb]))
    off = 0
    for n, sz in zip(_REPL + ['conv_w', 'meta'], s_sizes):
        grad[n] = f_small[off:off + sz]
        off += sz
    for n in _REPL:
        grad[n] = grad[n].reshape(w[n].shape)
    cw = grad['conv_w'].reshape(DEPTH, 3, MIX)
    grad['conv_w'] = lax.dynamic_slice_in_dim(cw, chip * (MIX // N_SHARD), MIX // N_SHARD, axis=2)
    gm = grad['meta'].reshape(N_META, D_MODEL)
    grad['meta'] = lax.dynamic_slice_in_dim(gm, chip * (D_MODEL // N_SHARD), D_MODEL // N_SHARD, axis=1)

    delta, new_m, new_v = {}, {}, {}
    for n in _WEIGHTS:
        delta[n], new_m[n], new_v[n] = adamw(w[n], grad[n], mom[n], var[n], "adamw_" + n)
    return (loss, dx[None], *[grad[n] for n in _WEIGHTS], *[delta[n] for n in _WEIGHTS],
            *[new_m[n] for n in _WEIGHTS], *[new_v[n] for n in _WEIGHTS])


def kernel(x, meta, ffn1_w_gate, ffn1_w_up, ffn1_w_down, ln1_g, ln1_b, w_in, mla_q_norm_g, mla_w_uq, mla_kv_norm_g, mla_w_ukv, mla_w_o, conv_w, conv_b, conv_w_out, s5_a_re, s5_a_im, s5_log_dt, s5_b_re, s5_b_im, s5_c_re, s5_c_im, s5_d, s5_w_glu, s5_b_glu, s5_w_out, w_o, ln2_g, ln2_b, ffn2_w_gate, ffn2_w_up, ffn2_w_down, ln3_g, ln3_b, loss_target, m_meta, m_ffn1_w_gate, m_ffn1_w_up, m_ffn1_w_down, m_ln1_g, m_ln1_b, m_w_in, m_mla_q_norm_g, m_mla_w_uq, m_mla_kv_norm_g, m_mla_w_ukv, m_mla_w_o, m_conv_w, m_conv_b, m_conv_w_out, m_s5_a_re, m_s5_a_im, m_s5_log_dt, m_s5_b_re, m_s5_b_im, m_s5_c_re, m_s5_c_im, m_s5_d, m_s5_w_glu, m_s5_b_glu, m_s5_w_out, m_w_o, m_ln2_g, m_ln2_b, m_ffn2_w_gate, m_ffn2_w_up, m_ffn2_w_down, m_ln3_g, m_ln3_b, v_meta, v_ffn1_w_gate, v_ffn1_w_up, v_ffn1_w_down, v_ln1_g, v_ln1_b, v_w_in, v_mla_q_norm_g, v_mla_w_uq, v_mla_kv_norm_g, v_mla_w_ukv, v_mla_w_o, v_conv_w, v_conv_b, v_conv_w_out, v_s5_a_re, v_s5_a_im, v_s5_log_dt, v_s5_b_re, v_s5_b_im, v_s5_c_re, v_s5_c_im, v_s5_d, v_s5_w_glu, v_s5_b_glu, v_s5_w_out, v_w_o, v_ln2_g, v_ln2_b, v_ffn2_w_gate, v_ffn2_w_up, v_ffn2_w_down, v_ln3_g, v_ln3_b):
    return _step(dict(locals()))


def _unused_packed_step(env):
    w = {n: env[n] for n in _WEIGHTS}
    mom = {n: env['m_' + n] for n in _WEIGHTS}
    var = {n: env['v_' + n] for n in _WEIGHTS}
    cidx = lax.axis_index("c").astype(jnp.int32).reshape(1)
    big_names = [n for n, _ in _BIG]

    exact = jnp.concatenate([conv_w.reshape(-1), meta.reshape(-1)])
    parts = [w[n].astype(BF16).reshape(-1) for n in big_names] + [lax.bitcast_convert_type(exact, BF16).reshape(-1)]
    sizes = [int(p.shape[0]) for p in parts]
    hr = _half_rows(sum(sizes))
    packed = _pad_to(jnp.concatenate(parts), 2 * hr * LANES).reshape(2, hr, LANES)
    gathered = all_gather_halves(packed).reshape(N_SHARD, -1)
    st, off = {}, 0
    for n, sz in zip(big_names, sizes[:-1]):
        st[n] = gathered[:, off:off + sz].reshape((N_SHARD,) + w[n].shape)
        off += sz
    exact_all = lax.bitcast_convert_type(gathered[:, off:off + sizes[-1]].reshape(N_SHARD, -1, 2), F32)
    n_cw = conv_w.size
    conv_w_st = exact_all[:, :n_cw].reshape((N_SHARD,) + conv_w.shape)
    meta_full = _nat_cols(exact_all[:, n_cw:].reshape((N_SHARD,) + meta.shape))

    Ws = []
    for li in range(DEPTH):
        small = {n: w[n][li] for n in _REPL}
        small['conv_w'] = _nat_cols(conv_w_st[:, li])
        Ws.append(compute_weights({n: st[n][:, li] for n in big_names}, small))

    loss_part, dx, dmeta, grads = local_step(x[0], loss_target[0], meta_full, Ws)
    loss = lax.psum(loss_part[0, 0], ("x", "y", "c"))
    ref_grads = [shard_grads(g) for g in grads]

    def shard_stack(n):
        return jnp.stack([ref_grads[li][n] for li in range(DEPTH)], axis=1)

    g_parts = [shard_stack(n).reshape(N_SHARD, -1) for n, _ in _BIG]
    g_parts.append(shard_stack('conv_w').reshape(N_SHARD, -1))
    g_parts.append(_shard_cols(dmeta).reshape(N_SHARD, -1))
    g_sizes = [int(p.shape[1]) for p in g_parts]
    ghr = _half_rows(sum(g_sizes))
    g_big = jnp.concatenate(g_parts + [jnp.zeros((N_SHARD, 2 * ghr * LANES - sum(g_sizes)), F32)], axis=1)
    g_big = g_big.reshape(N_SHARD, 2, ghr, LANES)
    s_parts = [jnp.stack([ref_grads[li][n] for li in range(DEPTH)]).reshape(-1) for n in _REPL]
    s_sizes = [int(p.shape[0]) for p in s_parts]
    shr = _half_rows(sum(s_sizes))
    g_small = _pad_to(jnp.concatenate(s_parts), 2 * shr * LANES)
    g_small = jnp.broadcast_to(g_small.reshape(1, 2, shr, LANES), (N_SHARD, 2, shr, LANES))
    f_big, f_small = reduce_scatter(g_big, g_small, cidx)
    f_big, f_small = f_big.reshape(-1), f_small.reshape(-1)

    grad = {}
    off = 0
    for (n, _), sz in zip(_BIG + [('conv_w', 1), ('meta', 1)], g_sizes):
        grad[n] = f_big[off:off + sz].reshape(w[n].shape)
        off += sz
    off = 0
    for n, sz in zip(_REPL, s_sizes):
        grad[n] = f_small[off:off + sz].reshape(w[n].shape)
        off += sz

    delta, new_m, new_v = {}, {}, {}
    for n in _WEIGHTS:
        delta[n], new_m[n], new_v[n] = adamw(w[n], grad[n], mom[n], var[n], "adamw_" + n)
    return (loss, dx[None], *[grad[n] for n in _WEIGHTS], *[delta[n] for n in _WEIGHTS],
            *[new_m[n] for n in _WEIGHTS], *[new_v[n] for n in _WEIGHTS])
```

```python
import functools
import math

import numpy as np
import jax
import jax.numpy as jnp
from jax import lax
from jax.experimental import pallas as pl
from jax.experimental.pallas import tpu as pltpu

F32 = jnp.float32
BF16 = jnp.bfloat16

D_MODEL = 1024
DEPTH = 2
N_META = 16
PAD = 112
X0 = PAD + N_META
N_HEADS = 8
D_NOPE = 64
D_ROPE = 32
D_V = 64
Q_RANK = 384
KV_RANK = 256
MIX = 512
S5_GROUPS = 32
S5_GROUP = 16
S5_STATE = 64
S5_LANES = S5_GROUPS * S5_STATE
D_FF = 2816
N_SHARD = 4
FF_SHARD = D_FF // N_SHARD
D_IN = 5792
P_IN = 6144
ALPHA = (2.0 * DEPTH) ** 0.25
LN_EPS = 1e-5
RMS_EPS = 1e-6
ATT_SCALE = (D_NOPE + D_ROPE) ** -0.5
ROPE_BASE = 10000.0
ADAM_LR, ADAM_B1, ADAM_B2, ADAM_EPS, ADAM_WD, ADAM_STEP = 0.001, 0.9, 0.999, 1e-08, 0.01, 10
SCAN_CHUNK = 128
VMEM_LIMIT = 52 * 2 ** 20
WGRAD = BF16
MESH = pl.DeviceIdType.MESH


def _cparams(**kw):
    return pltpu.CompilerParams(vmem_limit_bytes=VMEM_LIMIT, **kw)


def _tile(n):
    if n <= 1088:
        return n
    for t in (1024, 544, 512, 272, 256, 128):
        if n % t == 0:
            return t
    return n


def _row_tile(lp):
    for t in (544, 272, 128):
        if lp % t == 0:
            return t
    return lp


def _sigmoid(x):
    return 1.0 / (1.0 + jnp.exp(-x))


_GELU_C = math.sqrt(2.0 / math.pi)


def _gelu(x):
    return 0.5 * x * (1.0 + jnp.tanh(_GELU_C * (x + 0.044715 * x * x * x)))


def _gelu_grad(x):
    t = jnp.tanh(_GELU_C * (x + 0.044715 * x * x * x))
    return 0.5 * (1.0 + t) + 0.5 * x * (1.0 - t * t) * _GELU_C * (1.0 + 3.0 * 0.044715 * x * x)


def _dot(a, b, ca, cb, precision=None):
    return lax.dot_general(a, b, (((ca,), (cb,)), ((), ())), preferred_element_type=F32, precision=precision)


def matmul(a, b, *, name, ta=False, tb=False, ab='n', bb='n', res=None, res_scale=1.0, scale=1.0, out_dtype=F32):
    if ta:
        _, K, M = a.shape
    else:
        _, M, K = a.shape
    if tb:
        _, N, K2 = b.shape
    else:
        _, K2, N = b.shape
    assert K == K2, (a.shape, b.shape)
    n_out = max(a.shape[0] if ab == 'o' else 1, b.shape[0] if bb == 'o' else 1)
    n_red = max(a.shape[0] if ab == 'r' else 1, b.shape[0] if bb == 'r' else 1)
    tm, tn = _tile(M), _tile(N)
    tk = K if K <= 2304 else _tile(K)
    nkt = K // tk
    n_steps = n_red * nkt

    def bsel(mode, o, r):
        if mode == 'o':
            return o
        if mode == 'r':
            return r // nkt if nkt > 1 else r
        return 0

    def ksel(r):
        if nkt == 1:
            return 0
        return r % nkt if n_red > 1 else r

    a_map = (lambda o, i, j, r: (bsel(ab, o, r), ksel(r), i)) if ta else (lambda o, i, j, r: (bsel(ab, o, r), i, ksel(r)))
    b_map = (lambda o, i, j, r: (bsel(bb, o, r), j, ksel(r))) if tb else (lambda o, i, j, r: (bsel(bb, o, r), ksel(r), j))
    o_map = lambda o, i, j, r: (o, i, j)
    in_specs = [pl.BlockSpec((None, tk, tm) if ta else (None, tm, tk), a_map),
                pl.BlockSpec((None, tn, tk) if tb else (None, tk, tn), b_map)]
    operands = [a, b]
    if res is not None:
        in_specs.append(pl.BlockSpec((None, tm, tn), o_map))
        operands.append(res)
    has_res = res is not None

    def body(*refs):
        a_ref, b_ref = refs[0], refs[1]
        res_ref = refs[2] if has_res else None
        o_ref = refs[3] if has_res else refs[2]
        part = _dot(a_ref[...].astype(BF16), b_ref[...].astype(BF16), 0 if ta else 1, 1 if tb else 0)

        def finish(acc):
            v = acc if scale == 1.0 else acc * scale
            if has_res:
                v = v + res_scale * res_ref[...].astype(F32)
            o_ref[...] = v.astype(o_ref.dtype)

        if n_steps == 1:
            finish(part)
        else:
            acc_ref = refs[-1]
            r = pl.program_id(3)

            @pl.when(r == 0)
            def _():
                acc_ref[...] = part

            @pl.when(r > 0)
            def _():
                acc_ref[...] += part

            @pl.when(r == n_steps - 1)
            def _():
                finish(acc_ref[...])

    return pl.pallas_call(
        body, name=name,
        grid=(n_out, M // tm, N // tn, n_steps),
        in_specs=in_specs,
        out_specs=pl.BlockSpec((None, tm, tn), o_map),
        out_shape=jax.ShapeDtypeStruct((n_out, M, N), out_dtype),
        scratch_shapes=[pltpu.VMEM((tm, tn), F32)] if n_steps > 1 else [],
        compiler_params=_cparams(),
    )(*operands)


def rowwise(fn, rows, pars, outs, accs=(), *, name, lp):
    tm = _row_tile(lp)
    n_rows, n_pars, n_outs, n_accs = len(rows), len(pars), len(outs), len(accs)
    in_specs = [pl.BlockSpec((tm, w), functools.partial(lambda i, cb: (i, cb), cb=cb)) for _, w, cb in rows]
    in_specs += [pl.BlockSpec(p.shape, functools.partial(lambda i, nd: (0,) * nd, nd=p.ndim)) for p in pars]
    out_specs = [pl.BlockSpec((tm, w), lambda i: (i, 0)) for w, _ in outs]
    out_specs += [pl.BlockSpec(s, functools.partial(lambda i, nd: (0,) * nd, nd=len(s))) for s, _ in accs]
    out_shape = [jax.ShapeDtypeStruct((lp, w), dt) for w, dt in outs]
    out_shape += [jax.ShapeDtypeStruct(s, dt) for s, dt in accs]

    def body(*refs):
        i = pl.program_id(0)
        rv = [r[...] for r in refs[:n_rows]]
        pv = [r[...] for r in refs[n_rows:n_rows + n_pars]]
        o_refs = refs[n_rows + n_pars:n_rows + n_pars + n_outs]
        a_refs = refs[n_rows + n_pars + n_outs:]
        ov, av = fn(i * tm, rv, pv)
        for r, v in zip(o_refs, ov):
            r[...] = v.astype(r.dtype)
        if n_accs:
            @pl.when(i == 0)
            def _():
                for r, v in zip(a_refs, av):
                    r[...] = v.astype(r.dtype)

            @pl.when(i > 0)
            def _():
                for r, v in zip(a_refs, av):
                    r[...] += v.astype(r.dtype)

    res = pl.pallas_call(
        body, name=name, grid=(lp // tm,), in_specs=in_specs, out_specs=out_specs, out_shape=out_shape,
        compiler_params=_cparams(),
    )(*[r[0] for r in rows], *pars)
    return res


def _row_mask(row0, shape):
    return (row0 + lax.broadcasted_iota(jnp.int32, shape, 0)) >= PAD


def ffn_up(hb, wg, wu, lp):
    tm = _row_tile(lp)

    def body(h_ref, wg_ref, wu_ref, ab_ref, hid_ref):
        h = h_ref[...]
        a = _dot(h, wg_ref[...], 1, 1)
        b = _dot(h, wu_ref[...], 1, 1)
        ab_ref[0] = a.astype(BF16)
        ab_ref[1] = b.astype(BF16)
        hid_ref[...] = (a * _sigmoid(a) * b).astype(BF16)

    wspec = pl.BlockSpec((None, FF_SHARD, D_MODEL), lambda j, i: (j, 0, 0))
    return pl.pallas_call(
        body, name="ffn_up", grid=(N_SHARD, lp // tm),
        in_specs=[pl.BlockSpec((tm, D_MODEL), lambda j, i: (i, 0)), wspec, wspec],
        out_specs=[pl.BlockSpec((None, 2, tm, FF_SHARD), lambda j, i: (j, 0, i, 0)),
                   pl.BlockSpec((None, tm, FF_SHARD), lambda j, i: (j, i, 0))],
        out_shape=[jax.ShapeDtypeStruct((N_SHARD, 2, lp, FF_SHARD), BF16),
                   jax.ShapeDtypeStruct((N_SHARD, lp, FF_SHARD), BF16)],
        compiler_params=_cparams(),
    )(hb, wg, wu)


def _layer_norm(z, g, b):
    mu = jnp.mean(z, axis=-1, keepdims=True)
    zc = z - mu
    var = jnp.mean(zc * zc, axis=-1, keepdims=True)
    return zc * lax.rsqrt(var + LN_EPS) * g + b


def mm_res_ln(a, w, res, g, b, *, scale, name, lp):
    n_red, _, K = a.shape
    tm = _row_tile(lp)

    def body(a_ref, w_ref, res_ref, g_ref, b_ref, z_ref, h_ref, hb_ref, acc_ref):
        r = pl.program_id(1)
        part = _dot(a_ref[...].astype(BF16), w_ref[...], 1, 0)

        @pl.when(r == 0)
        def _():
            acc_ref[...] = part

        @pl.when(r > 0)
        def _():
            acc_ref[...] += part

        @pl.when(r == n_red - 1)
        def _():
            z = ALPHA * res_ref[...] + scale * acc_ref[...]
            z_ref[...] = z
            hn = _layer_norm(z, g_ref[...], b_ref[...])
            h_ref[...] = hn
            hb_ref[...] = hn.astype(BF16)

    row = pl.BlockSpec((tm, D_MODEL), lambda i, r: (i, 0))
    par = pl.BlockSpec((1, D_MODEL), lambda i, r: (0, 0))
    return pl.pallas_call(
        body, name=name, grid=(lp // tm, n_red),
        in_specs=[pl.BlockSpec((None, tm, K), lambda i, r: (r, i, 0)),
                  pl.BlockSpec((None, K, D_MODEL), lambda i, r: (r, 0, 0)), row, par, par],
        out_specs=[row, row, row],
        out_shape=[jax.ShapeDtypeStruct((lp, D_MODEL), F32), jax.ShapeDtypeStruct((lp, D_MODEL), F32),
                   jax.ShapeDtypeStruct((lp, D_MODEL), BF16)],
        scratch_shapes=[pltpu.VMEM((tm, D_MODEL), F32)],
        compiler_params=_cparams(),
    )(a, w, res, g, b)


def ln_bwd(dh, z, g, *, fscale, name, lp):
    def fn(row0, rv, pv):
        dh_, z_ = rv
        g_, = pv
        mu = jnp.mean(z_, axis=-1, keepdims=True)
        zc = z_ - mu
        rstd = lax.rsqrt(jnp.mean(zc * zc, axis=-1, keepdims=True) + LN_EPS)
        xh = zc * rstd
        dxh = dh_ * g_
        m1 = jnp.mean(dxh, axis=-1, keepdims=True)
        m2 = jnp.mean(dxh * xh, axis=-1, keepdims=True)
        dz = rstd * (dxh - m1 - xh * m2)
        return ((dz, fscale * dz),
                (jnp.sum(dh_ * xh, axis=0, keepdims=True), jnp.sum(dh_, axis=0, keepdims=True)))

    return rowwise(fn, [(dh, D_MODEL, 0), (z, D_MODEL, 0)], [g], [(D_MODEL, F32), (D_MODEL, BF16)],
                   [((1, D_MODEL), F32), ((1, D_MODEL), F32)], name=name, lp=lp)


def ffn_down_bwd(dfb, wd, ab, lp):
    tm = _row_tile(lp)

    def body(df_ref, w_ref, ab_ref, da_ref, db_ref):
        dhid = _dot(df_ref[...], w_ref[...], 1, 1)
        a = ab_ref[0].astype(F32)
        b = ab_ref[1].astype(F32)
        sg = _sigmoid(a)
        da_ref[...] = (dhid * b * (sg * (1.0 + a * (1.0 - sg)))).astype(BF16)
        db_ref[...] = (dhid * (a * sg)).astype(BF16)

    ospec = pl.BlockSpec((None, tm, FF_SHARD), lambda j, i: (j, i, 0))
    return pl.pallas_call(
        body, name="ffn_down_bwd", grid=(N_SHARD, lp // tm),
        in_specs=[pl.BlockSpec((tm, D_MODEL), lambda j, i: (i, 0)),
                  pl.BlockSpec((None, FF_SHARD, D_MODEL), lambda j, i: (j, 0, 0)),
                  pl.BlockSpec((None, 2, tm, FF_SHARD), lambda j, i: (j, 0, i, 0))],
        out_specs=[ospec, ospec],
        out_shape=[jax.ShapeDtypeStruct((N_SHARD, lp, FF_SHARD), BF16)] * 2,
        compiler_params=_cparams(),
    )(dfb, wd, ab)


def ffn_dx(da, db, wg, wu, dz, lp):
    tm = _row_tile(lp)

    def body(da_ref, db_ref, wg_ref, wu_ref, dz_ref, o_ref, acc_ref):
        j = pl.program_id(1)
        part = _dot(da_ref[...], wg_ref[...], 1, 0) + _dot(db_ref[...], wu_ref[...], 1, 0)

        @pl.when(j == 0)
        def _():
            acc_ref[...] = part

        @pl.when(j > 0)
        def _():
            acc_ref[...] += part

        @pl.when(j == N_SHARD - 1)
        def _():
            o_ref[...] = acc_ref[...] + ALPHA * dz_ref[...]

    aspec = pl.BlockSpec((None, tm, FF_SHARD), lambda i, j: (j, i, 0))
    wspec = pl.BlockSpec((None, FF_SHARD, D_MODEL), lambda i, j: (j, 0, 0))
    row = pl.BlockSpec((tm, D_MODEL), lambda i, j: (i, 0))
    return pl.pallas_call(
        body, name="ffn_dx", grid=(lp // tm, N_SHARD), in_specs=[aspec, aspec, wspec, wspec, row], out_specs=row,
        out_shape=jax.ShapeDtypeStruct((lp, D_MODEL), F32), scratch_shapes=[pltpu.VMEM((tm, D_MODEL), F32)],
        compiler_params=_cparams(),
    )(da, db, wg, wu, dz)


def ffn_fwd(h, hb, wg, wu, wd, g, b, lp):
    ab, hid = ffn_up(hb, wg, wu, lp)
    z, hn, hnb = mm_res_ln(hid, wd, h, g, b, scale=0.5, name="ffn_down_ln", lp=lp)
    return hn, hnb, dict(hb=hb, ab=ab, hid=hid, z=z)


def ffn_bwd(dh, sv, wg, wu, wd, g, lp):
    dz, dfb, dg, db = ln_bwd(dh, sv['z'], g, fscale=0.5, name="ffn_ln_bwd", lp=lp)
    da, dbb = ffn_down_bwd(dfb, wd, sv['ab'], lp)
    d_wd = matmul(sv['hid'], dfb[None], ta=True, ab='o', out_dtype=WGRAD, name="ffn_dwd")
    d_wg = matmul(da, sv['hb'][None], ta=True, ab='o', out_dtype=WGRAD, name="ffn_dwg")
    d_wu = matmul(dbb, sv['hb'][None], ta=True, ab='o', out_dtype=WGRAD, name="ffn_dwu")
    dh_in = ffn_dx(da, dbb, wg, wu, dz, lp)
    return dh_in, dict(wg=d_wg, wu=d_wu, wd=d_wd, ln_g=dg, ln_b=db)


def _rope_tables(lp):
    pos = np.arange(lp, dtype=np.float32) - PAD
    inv = ROPE_BASE ** (-np.arange(0, D_ROPE, 2, dtype=np.float32) / D_ROPE)
    ang = pos[:, None] * inv[None, :]
    cos = np.concatenate([np.cos(ang), np.cos(ang)], axis=1).astype(np.float32)
    sin = np.concatenate([np.sin(ang), np.sin(ang)], axis=1).astype(np.float32)
    rot = np.zeros((D_ROPE, D_ROPE), np.float32)
    half = D_ROPE // 2
    for j in range(half):
        rot[j + half, j] = -1.0
        rot[j, j + half] = 1.0
    return jnp.asarray(cos), jnp.asarray(sin), jnp.asarray(rot)


def _rot(x, rot):
    return _dot(x, rot, 1, 0, precision=lax.Precision.HIGHEST)


def _rms(x, g):
    r = lax.rsqrt(jnp.mean(x * x, axis=-1, keepdims=True) + RMS_EPS)
    return x * r * g


def mla_prep(proj, cos, sin, rot, qg, kvg, lp):
    def fn(row0, rv, pv):
        cq, krb, ckv, c, s = rv
        qg_, kvg_, rot_ = pv
        kr = krb[:, :D_ROPE]
        return ((_rms(cq, qg_), _rms(ckv, kvg_), kr * c + _rot(kr, rot_) * s), ())

    return rowwise(fn, [(proj, Q_RANK, 0), (proj, 128, 3), (proj, KV_RANK, 2), (cos, D_ROPE, 0), (sin, D_ROPE, 0)],
                   [qg, kvg, rot], [(Q_RANK, BF16), (KV_RANK, BF16), (D_ROPE, BF16)], name="mla_prep", lp=lp)


def mla_heads(cqn, ckvn, cos, sin, rot, wqn, wqr, wkn, wv, lp):
    tm = _row_tile(lp)

    def body(cq_ref, ckv_ref, c_ref, s_ref, rot_ref, wqn_ref, wqr_ref, wkn_ref, wv_ref, qn_ref, qr_ref, kn_ref, v_ref):
        cq = cq_ref[...]
        ckv = ckv_ref[...]
        qn_ref[...] = _dot(cq, wqn_ref[...], 1, 0).astype(BF16)
        qr = _dot(cq, wqr_ref[...], 1, 0)
        qr_ref[...] = (qr * c_ref[...] + _rot(qr, rot_ref[...]) * s_ref[...]).astype(BF16)
        kn_ref[...] = _dot(ckv, wkn_ref[...], 1, 0).astype(BF16)
        v_ref[...] = _dot(ckv, wv_ref[...], 1, 0).astype(BF16)

    def row(w):
        return pl.BlockSpec((tm, w), lambda h, i: (i, 0))

    def wspec(k, n):
        return pl.BlockSpec((None, k, n), lambda h, i: (h, 0, 0))

    def ospec(n):
        return pl.BlockSpec((None, tm, n), lambda h, i: (h, i, 0))

    return pl.pallas_call(
        body, name="mla_heads", grid=(N_HEADS, lp // tm),
        in_specs=[row(Q_RANK), row(KV_RANK), row(D_ROPE), row(D_ROPE),
                  pl.BlockSpec((D_ROPE, D_ROPE), lambda h, i: (0, 0)),
                  wspec(Q_RANK, D_NOPE), wspec(Q_RANK, D_ROPE), wspec(KV_RANK, D_NOPE), wspec(KV_RANK, D_V)],
        out_specs=[ospec(D_NOPE), ospec(D_ROPE), ospec(D_NOPE), ospec(D_V)],
        out_shape=[jax.ShapeDtypeStruct((N_HEADS, lp, D_NOPE), BF16), jax.ShapeDtypeStruct((N_HEADS, lp, D_ROPE), BF16),
                   jax.ShapeDtypeStruct((N_HEADS, lp, D_NOPE), BF16), jax.ShapeDtypeStruct((N_HEADS, lp, D_V), BF16)],
        compiler_params=_cparams(),
    )(cqn, ckvn, cos, sin, rot, wqn, wqr, wkn, wv)


def _att_probs(qn, qr, kn, kr, row0, tq, lp):
    s = (_dot(qn, kn, 1, 1) + _dot(qr, kr, 1, 1)) * ATT_SCALE
    qi = row0 + lax.broadcasted_iota(jnp.int32, (tq, lp), 0)
    ki = lax.broadcasted_iota(jnp.int32, (tq, lp), 1)
    s = jnp.where((ki <= qi) & (ki >= PAD), s, -1e30)
    p = jnp.exp(s - jnp.max(s, axis=-1, keepdims=True))
    return p / jnp.sum(p, axis=-1, keepdims=True)


def _att_specs(tq, lp):
    def qspec(n):
        return pl.BlockSpec((None, tq, n), lambda h, i: (h, i, 0))

    def kspec(n):
        return pl.BlockSpec((None, lp, n), lambda h, i: (h, 0, 0))

    return qspec, kspec, pl.BlockSpec((lp, D_ROPE), lambda h, i: (0, 0))


def _att_tile(lp):
    return 272 if lp % 272 == 0 else 128


def attn_fwd(qn, qr, kn, v, kr, lp):
    tq = _att_tile(lp)
    qspec, kspec, krspec = _att_specs(tq, lp)

    def body(qn_ref, qr_ref, kn_ref, v_ref, kr_ref, o_ref):
        i = pl.program_id(1)
        for k in range(lp // tq):
            @pl.when(i == k)
            def _(k=k):
                ke = (k + 1) * tq
                p = _att_probs(qn_ref[...], qr_ref[...], kn_ref[0:ke, :], kr_ref[0:ke, :], k * tq, tq, ke)
                o_ref[...] = _dot(p.astype(BF16), v_ref[0:ke, :], 1, 0).astype(BF16)

    return pl.pallas_call(
        body, name="attn_fwd", grid=(N_HEADS, lp // tq),
        in_specs=[qspec(D_NOPE), qspec(D_ROPE), kspec(D_NOPE), kspec(D_V), krspec],
        out_specs=qspec(D_V), out_shape=jax.ShapeDtypeStruct((N_HEADS, lp, D_V), BF16),
        compiler_params=_cparams(),
    )(qn, qr, kn, v, kr)


def attn_bwd(qn, qr, kn, v, kr, do, lp):
    tq = _att_tile(lp)
    qspec, kspec, krspec = _att_specs(tq, lp)

    def body(qn_ref, qr_ref, kn_ref, v_ref, kr_ref, do_ref, dqn_ref, dqr_ref, dkn_ref, dv_ref, dkr_ref):
        h, i = pl.program_id(0), pl.program_id(1)

        @pl.when(i == 0)
        def _():
            dkn_ref[...] = jnp.zeros_like(dkn_ref)
            dv_ref[...] = jnp.zeros_like(dv_ref)

        @pl.when((i == 0) & (h == 0))
        def _():
            dkr_ref[...] = jnp.zeros_like(dkr_ref)

        for k in range(lp // tq):
            @pl.when(i == k)
            def _(k=k):
                ke = (k + 1) * tq
                qn_, qr_, do_ = qn_ref[...], qr_ref[...], do_ref[...]
                kn_, v_, kr_ = kn_ref[0:ke, :], v_ref[0:ke, :], kr_ref[0:ke, :]
                p = _att_probs(qn_, qr_, kn_, kr_, k * tq, tq, ke)
                dp = _dot(do_, v_, 1, 1)
                delta = jnp.sum(p * dp, axis=-1, keepdims=True)
                ds = (p * (dp - delta) * ATT_SCALE).astype(BF16)
                dqn_ref[...] = _dot(ds, kn_, 1, 0).astype(BF16)
                dqr_ref[...] = _dot(ds, kr_, 1, 0)
                dkn_ref[0:ke, :] += _dot(ds, qn_, 0, 0)
                dv_ref[0:ke, :] += _dot(p.astype(BF16), do_, 0, 0)
                dkr_ref[0:ke, :] += _dot(ds, qr_, 0, 0)

    return pl.pallas_call(
        body, name="attn_bwd", grid=(N_HEADS, lp // tq),
        in_specs=[qspec(D_NOPE), qspec(D_ROPE), kspec(D_NOPE), kspec(D_V), krspec, qspec(D_V)],
        out_specs=[qspec(D_NOPE), qspec(D_ROPE), kspec(D_NOPE), kspec(D_V), krspec],
        out_shape=[jax.ShapeDtypeStruct((N_HEADS, lp, D_NOPE), BF16), jax.ShapeDtypeStruct((N_HEADS, lp, D_ROPE), F32),
                   jax.ShapeDtypeStruct((N_HEADS, lp, D_NOPE), F32), jax.ShapeDtypeStruct((N_HEADS, lp, D_V), F32),
                   jax.ShapeDtypeStruct((lp, D_ROPE), F32)],
        compiler_params=_cparams(),
    )(qn, qr, kn, v, kr, do)


def mla_heads_bwd(dqn, dqr, dkn, dv, cos, sin, rot, wqn, wqr, wkn, wv, lp):
    tm = _row_tile(lp)

    def body(dqn_ref, dqr_ref, dkn_ref, dv_ref, c_ref, s_ref, rot_ref, wqn_ref, wqr_ref, wkn_ref, wv_ref,
             dcq_ref, dckv_ref, dqrp_ref):
        h = pl.program_id(1)
        dqr_ = dqr_ref[...]
        dqrp = (dqr_ * c_ref[...] - _rot(dqr_ * s_ref[...], rot_ref[...])).astype(BF16)
        dqrp_ref[...] = dqrp
        dcq = _dot(dqn_ref[...], wqn_ref[...], 1, 1) + _dot(dqrp, wqr_ref[...], 1, 1)
        dckv = _dot(dkn_ref[...].astype(BF16), wkn_ref[...], 1, 1) + _dot(dv_ref[...].astype(BF16), wv_ref[...], 1, 1)

        @pl.when(h == 0)
        def _():
            dcq_ref[...] = dcq
            dckv_ref[...] = dckv

        @pl.when(h > 0)
        def _():
            dcq_ref[...] += dcq
            dckv_ref[...] += dckv

    def hspec(n):
        return pl.BlockSpec((None, tm, n), lambda i, h: (h, i, 0))

    def row(w):
        return pl.BlockSpec((tm, w), lambda i, h: (i, 0))

    def wspec(k, n):
        return pl.BlockSpec((None, k, n), lambda i, h: (h, 0, 0))

    return pl.pallas_call(
        body, name="mla_heads_bwd", grid=(lp // tm, N_HEADS),
        in_specs=[hspec(D_NOPE), hspec(D_ROPE), hspec(D_NOPE), hspec(D_V), row(D_ROPE), row(D_ROPE),
                  pl.BlockSpec((D_ROPE, D_ROPE), lambda i, h: (0, 0)),
                  wspec(Q_RANK, D_NOPE), wspec(Q_RANK, D_ROPE), wspec(KV_RANK, D_NOPE), wspec(KV_RANK, D_V)],
        out_specs=[row(Q_RANK), row(KV_RANK), hspec(D_ROPE)],
        out_shape=[jax.ShapeDtypeStruct((lp, Q_RANK), F32), jax.ShapeDtypeStruct((lp, KV_RANK), F32),
                   jax.ShapeDtypeStruct((N_HEADS, lp, D_ROPE), BF16)],
        compiler_params=_cparams(),
    )(dqn, dqr, dkn, dv, cos, sin, rot, wqn, wqr, wkn, wv)


def _rms_bwd(dy, x, g):
    r = lax.rsqrt(jnp.mean(x * x, axis=-1, keepdims=True) + RMS_EPS)
    n = x * r
    dn = dy * g
    dx = r * (dn - n * jnp.mean(dn * n, axis=-1, keepdims=True))
    return dx, jnp.sum(dy * n, axis=0, keepdims=True)


def mla_prep_bwd(dcq, dckv, dkr, proj, cos, sin, rot, qg, kvg, lp):
    def fn(row0, rv, pv):
        dcq_, dckv_, dkr_, cq, ckv, c, s = rv
        qg_, kvg_, rot_ = pv
        dxq, dgq = _rms_bwd(dcq_, cq, qg_)
        dxkv, dgkv = _rms_bwd(dckv_, ckv, kvg_)
        dkr_raw = dkr_ * c - _rot(dkr_ * s, rot_)
        return ((dxq, dxkv, dkr_raw), (dgq, dgkv))

    return rowwise(fn, [(dcq, Q_RANK, 0), (dckv, KV_RANK, 0), (dkr, D_ROPE, 0), (proj, Q_RANK, 0), (proj, KV_RANK, 2),
                        (cos, D_ROPE, 0), (sin, D_ROPE, 0)], [qg, kvg, rot],
                   [(Q_RANK, BF16), (KV_RANK, BF16), (D_ROPE, BF16)], [((1, Q_RANK), F32), ((1, KV_RANK), F32)],
                   name="mla_prep_bwd", lp=lp)


def _shift_down(x, d, rows):
    return jnp.where(rows >= d, pltpu.roll(x, d, 0), 0.0)


def _shift_up(x, d, rows, n):
    return jnp.where(rows < n - d, pltpu.roll(x, n - d, 0), 0.0)


_CONV_W = 128
_XB, _BG, _CG = 1024 // _CONV_W, 1536 // _CONV_W, 2048 // _CONV_W


def _conv_specs(lp):
    def pspec(base):
        return pl.BlockSpec((lp, _CONV_W), functools.partial(lambda c, base: (0, base + c), base=base))

    col = pl.BlockSpec((lp, _CONV_W), lambda c: (0, c))
    wspec = pl.BlockSpec((3, _CONV_W), lambda c: (0, c))
    bspec = pl.BlockSpec((1, _CONV_W), lambda c: (0, c))
    return pspec, col, wspec, bspec


def _conv_core(xbar, cg, w, bias, lp):
    rows = lax.broadcasted_iota(jnp.int32, (lp, _CONV_W), 0)
    u = jnp.where(rows >= PAD, cg * xbar, 0.0)
    u1 = _shift_down(u, 1, rows)
    u2 = _shift_down(u, 2, rows)
    y = bias + w[0:1] * u2 + w[1:2] * u1 + w[2:3] * u
    return rows, u, u1, u2, y


def conv_fwd(proj, w, bias, lp):
    pspec, col, wspec, bspec = _conv_specs(lp)

    def body(x_ref, b_ref, c_ref, w_ref, bias_ref, v_ref):
        _, _, _, _, y = _conv_core(x_ref[...], c_ref[...], w_ref[...], bias_ref[...], lp)
        v_ref[...] = (b_ref[...] * y).astype(BF16)

    return pl.pallas_call(
        body, name="conv_fwd", grid=(MIX // _CONV_W,),
        in_specs=[pspec(_XB), pspec(_BG), pspec(_CG), wspec, bspec], out_specs=col,
        out_shape=jax.ShapeDtypeStruct((lp, MIX), BF16), compiler_params=_cparams(),
    )(proj, proj, proj, w, bias)


def conv_bwd(dv, proj, w, bias, lp):
    pspec, col, wspec, bspec = _conv_specs(lp)

    def body(dv_ref, x_ref, b_ref, c_ref, w_ref, bias_ref, dx_ref, db_ref, dc_ref, dw_ref, dbias_ref):
        xbar, cg, w_ = x_ref[...], c_ref[...], w_ref[...]
        rows, u, u1, u2, y = _conv_core(xbar, cg, w_, bias_ref[...], lp)
        dv_ = dv_ref[...]
        db_ref[...] = (dv_ * y).astype(BF16)
        dy = dv_ * b_ref[...]
        dbias_ref[...] = jnp.sum(dy, axis=0, keepdims=True)
        dw_ref[0:1, :] = jnp.sum(dy * u2, axis=0, keepdims=True)
        dw_ref[1:2, :] = jnp.sum(dy * u1, axis=0, keepdims=True)
        dw_ref[2:3, :] = jnp.sum(dy * u, axis=0, keepdims=True)
        du = w_[2:3] * dy + w_[1:2] * _shift_up(dy, 1, rows, lp) + w_[0:1] * _shift_up(dy, 2, rows, lp)
        du = jnp.where(rows >= PAD, du, 0.0)
        dc_ref[...] = (du * xbar).astype(BF16)
        dx_ref[...] = (du * cg).astype(BF16)

    return pl.pallas_call(
        body, name="conv_bwd", grid=(MIX // _CONV_W,),
        in_specs=[col, pspec(_XB), pspec(_BG), pspec(_CG), wspec, bspec],
        out_specs=[col, col, col, wspec, bspec],
        out_shape=[jax.ShapeDtypeStruct((lp, MIX), BF16)] * 3 + [jax.ShapeDtypeStruct((3, MIX), F32),
                                                                jax.ShapeDtypeStruct((1, MIX), F32)],
        compiler_params=_cparams(),
    )(dv, proj, proj, proj, w, bias)


def _s5_disc(a_re, a_im, log_dt, b_re, b_im):
    dt = jnp.exp(log_dt)
    mag = jnp.exp(dt * a_re)
    ab_re, ab_im = mag * jnp.cos(dt * a_im), mag * jnp.sin(dt * a_im)
    den = a_re * a_re + a_im * a_im
    nr, ni = ab_re - 1.0, ab_im
    coef_re = (nr * a_re + ni * a_im) / den
    coef_im = (ni * a_re - nr * a_im) / den
    return ab_re, ab_im, coef_re * b_re - coef_im * b_im, coef_re * b_im + coef_im * b_re


_S5_ROWS = S5_GROUPS * S5_GROUP


def s5_prep(a_re, a_im, log_dt, b_re, b_im):
    def body(ar, ai, ld, br, bi, o0, o1, o2, o3):
        for o, v in zip((o0, o1, o2, o3), _s5_disc(ar[...], ai[...], ld[...], br[...], bi[...])):
            o[...] = v

    return pl.pallas_call(body, name="s5_prep",
                          out_shape=[jax.ShapeDtypeStruct((_S5_ROWS, S5_STATE), F32)] * 4)(a_re, a_im, log_dt, b_re, b_im)


def s5_prep_bwd(a_re, a_im, log_dt, b_re, b_im, d_ab_re, d_ab_im, d_bb_re, d_bb_im, sel):
    def body(ar, ai, ld, br, bi, g0, g1, g2, g3, sel_ref, da_re, da_im, dld, dbr, dbi):
        _, vjp = jax.vjp(_s5_disc, ar[...], ai[...], ld[...], br[...], bi[...])
        c_ar, c_ai, c_ld, c_br, c_bi = vjp((g0[...], g1[...], g2[...], g3[...]))
        s = sel_ref[...]
        hi = lax.Precision.HIGHEST
        da_re[...] = _dot(s, c_ar, 1, 0, precision=hi)
        da_im[...] = _dot(s, c_ai, 1, 0, precision=hi)
        dld[...] = jnp.sum(_dot(s, c_ld, 1, 0, precision=hi), axis=-1, keepdims=True)
        dbr[...] = c_br
        dbi[...] = c_bi

    g = jax.ShapeDtypeStruct((S5_GROUPS, S5_STATE), F32)
    full = jax.ShapeDtypeStruct((_S5_ROWS, S5_STATE), F32)
    return pl.pallas_call(body, name="s5_prep_bwd",
                          out_shape=[g, g, jax.ShapeDtypeStruct((S5_GROUPS, 1), F32), full, full],
                          )(a_re, a_im, log_dt, b_re, b_im, d_ab_re, d_ab_im, d_bb_re, d_bb_im, sel)


_SCAN_W = 128
_SCAN_STEPS = int(math.log2(SCAN_CHUNK))


def _cmul(ar, ai, br, bi):
    return ar * br - ai * bi, ar * bi + ai * br


def _scan_powers(ar, ai, reverse):
    pw = [(ar, ai)]
    for _ in range(_SCAN_STEPS):
        pw.append(_cmul(*pw[-1], *pw[-1]))
    rows = lax.broadcasted_iota(jnp.int32, (SCAN_CHUNK, ar.shape[-1]), 0)
    tr = jnp.broadcast_to(ar, rows.shape)
    ti = jnp.broadcast_to(ai, rows.shape)
    for k in range(_SCAN_STEPS):
        d = 2 ** k
        if reverse:
            live = rows < SCAN_CHUNK - d
            mr, mi = _cmul(tr, ti, _shift_up(tr, d, rows, SCAN_CHUNK), _shift_up(ti, d, rows, SCAN_CHUNK))
        else:
            live = rows >= d
            mr, mi = _cmul(tr, ti, _shift_down(tr, d, rows), _shift_down(ti, d, rows))
        tr = jnp.where(live, mr, tr)
        ti = jnp.where(live, mi, ti)
    return pw, rows, tr, ti


def s5_scan(bu, ab_re, ab_im, lp):
    n_chunks = lp // SCAN_CHUNK

    def body(bu_ref, ar_ref, ai_ref, s_ref):
        ar, ai = ar_ref[...], ai_ref[...]
        pw, rows, tr, ti = _scan_powers(ar, ai, False)

        def chunk(ci, carry):
            cr, cim = carry
            r0 = pl.multiple_of(ci * SCAN_CHUNK, SCAN_CHUNK)
            xr = bu_ref[0, pl.ds(r0, SCAN_CHUNK), :]
            xi = bu_ref[1, pl.ds(r0, SCAN_CHUNK), :]
            for k in range(_SCAN_STEPS):
                d = 2 ** k
                mr, mi = _cmul(pw[k][0], pw[k][1], _shift_down(xr, d, rows), _shift_down(xi, d, rows))
                xr, xi = xr + mr, xi + mi
            mr, mi = _cmul(tr, ti, cr, cim)
            xr, xi = xr + mr, xi + mi
            s_ref[0, pl.ds(r0, SCAN_CHUNK), :] = xr
            s_ref[1, pl.ds(r0, SCAN_CHUNK), :] = xi
            return xr[SCAN_CHUNK - 1:SCAN_CHUNK, :], xi[SCAN_CHUNK - 1:SCAN_CHUNK, :]

        zero = jnp.zeros((1, _SCAN_W), F32)
        lax.fori_loop(0, n_chunks, chunk, (zero, zero))

    spec = pl.BlockSpec((2, lp, _SCAN_W), lambda c: (0, 0, c))
    aspec = pl.BlockSpec((1, _SCAN_W), lambda c: (0, c))
    return pl.pallas_call(
        body, name="s5_scan", grid=(S5_LANES // _SCAN_W,), in_specs=[spec, aspec, aspec], out_specs=spec,
        out_shape=jax.ShapeDtypeStruct((2, lp, S5_LANES), F32), compiler_params=_cparams(),
    )(bu, ab_re, ab_im)


def s5_scan_bwd(ds, s, ab_re, ab_im, lp):
    n_chunks = lp // SCAN_CHUNK

    def body(ds_ref, s_ref, ar_ref, ai_ref, g_ref, da_ref):
        ar, ai = ar_ref[...], -ai_ref[...]
        pw, rows, tr, ti = _scan_powers(ar, ai, True)

        def chunk(k, carry):
            cr, cim, dar, dai = carry
            ci = n_chunks - 1 - k
            r0 = pl.multiple_of(ci * SCAN_CHUNK, SCAN_CHUNK)
            xr = ds_ref[0, pl.ds(r0, SCAN_CHUNK), :]
            xi = ds_ref[1, pl.ds(r0, SCAN_CHUNK), :]
            for j in range(_SCAN_STEPS):
                d = 2 ** j
                mr, mi = _cmul(pw[j][0], pw[j][1], _shift_up(xr, d, rows, SCAN_CHUNK), _shift_up(xi, d, rows, SCAN_CHUNK))
                xr, xi = xr + mr, xi + mi
            mr, mi = _cmul(tr, ti, cr, cim)
            xr, xi = xr + mr, xi + mi
            g_ref[0, pl.ds(r0, SCAN_CHUNK), :] = xr
            g_ref[1, pl.ds(r0, SCAN_CHUNK), :] = xi
            prev0 = pl.multiple_of(jnp.maximum(r0 - 8, 0), 8)
            live = (ci > 0).astype(F32)
            pr = s_ref[0, pl.ds(prev0, 8), :][7:8, :] * live
            pim = s_ref[1, pl.ds(prev0, 8), :][7:8, :] * live
            sr = s_ref[0, pl.ds(r0, SCAN_CHUNK), :]
            si = s_ref[1, pl.ds(r0, SCAN_CHUNK), :]
            sr = jnp.where(rows >= 1, pltpu.roll(sr, 1, 0), pr)
            si = jnp.where(rows >= 1, pltpu.roll(si, 1, 0), pim)
            dar = dar + jnp.sum(xr * sr + xi * si, axis=0, keepdims=True)
            dai = dai + jnp.sum(xi * sr - xr * si, axis=0, keepdims=True)
            return xr[0:1, :], xi[0:1, :], dar, dai

        zero = jnp.zeros((1, _SCAN_W), F32)
        _, _, dar, dai = lax.fori_loop(0, n_chunks, chunk, (zero, zero, zero, zero))
        da_ref[0] = dar
        da_ref[1] = dai

    spec = pl.BlockSpec((2, lp, _SCAN_W), lambda c: (0, 0, c))
    aspec = pl.BlockSpec((1, _SCAN_W), lambda c: (0, c))
    return pl.pallas_call(
        body, name="s5_scan_bwd", grid=(S5_LANES // _SCAN_W,), in_specs=[spec, spec, aspec, aspec],
        out_specs=[spec, pl.BlockSpec((2, 1, _SCAN_W), lambda c: (0, 0, c))],
        out_shape=[jax.ShapeDtypeStruct((2, lp, S5_LANES), F32), jax.ShapeDtypeStruct((2, 1, S5_LANES), F32)],
        compiler_params=_cparams(),
    )(ds, s, ab_re, ab_im)


S5_BLOCKS = 4
_S5_PER = S5_GROUPS // S5_BLOCKS


def _blockdiag(x):
    _, r, c = x.shape
    eye = jnp.eye(_S5_PER, dtype=x.dtype)
    x = x.reshape(S5_BLOCKS, _S5_PER, r, c)
    return (x[:, :, :, None, :] * eye[None, :, None, :, None]).reshape(S5_BLOCKS, _S5_PER * r, _S5_PER * c)


def _blockdiag_extract(m, r, c):
    return jnp.einsum('qgrgc->qgrc', m.reshape(S5_BLOCKS, _S5_PER, r, _S5_PER, c)).reshape(S5_GROUPS, r, c)


def bd_matmul(a, w, *, w_t, reduce, res=None, name):
    _, M, _ = a.shape
    n_w, _, k1, k2 = w.shape
    ka, kout = (k2, k1) if w_t else (k1, k2)
    tm = _row_tile(M)
    n_out, n_red = (1, n_w) if reduce else (n_w, 1)
    has_res = res is not None

    def body(*refs):
        a_ref, w_ref = refs[0], refs[1]
        o_ref = refs[3] if has_res else refs[2]
        part = _dot(a_ref[...].astype(BF16), w_ref[...], 1, 1 if w_t else 0)
        if n_red == 1:
            o_ref[...] = part.astype(o_ref.dtype)
        else:
            acc_ref = refs[-1]
            r = pl.program_id(3)

            @pl.when(r == 0)
            def _():
                acc_ref[...] = part

            @pl.when(r == n_red - 1)
            def _():
                tot = acc_ref[...] + part
                o_ref[...] = (tot + refs[2][...] if has_res else tot).astype(o_ref.dtype)

    if reduce:
        a_map, w_map = (lambda o, i, q, r: (r, i, q)), (lambda o, i, q, r: (r, q, 0, 0))
    else:
        a_map, w_map = (lambda o, i, q, r: (0, i, q)), (lambda o, i, q, r: (o, q, 0, 0))
    o_map = lambda o, i, q, r: (o, i, q)
    in_specs = [pl.BlockSpec((None, tm, ka), a_map), pl.BlockSpec((None, None, k1, k2), w_map)]
    operands = [a, w]
    if has_res:
        in_specs.append(pl.BlockSpec((None, tm, kout), o_map))
        operands.append(res)
    return pl.pallas_call(
        body, name=name, grid=(n_out, M // tm, S5_BLOCKS, n_red), in_specs=in_specs,
        out_specs=pl.BlockSpec((None, tm, kout), o_map),
        out_shape=jax.ShapeDtypeStruct((n_out, M, S5_BLOCKS * kout), F32),
        scratch_shapes=[pltpu.VMEM((tm, kout), F32)] if n_red > 1 else [],
        compiler_params=_cparams(),
    )(*operands)


def bd_outer(a, b, name):
    na, M, wa = a.shape
    nb_, _, wb = b.shape
    ka, kb = wa // S5_BLOCKS, wb // S5_BLOCKS
    n_out = max(na, nb_)

    def body(a_ref, b_ref, o_ref):
        o_ref[...] = _dot(a_ref[...].astype(BF16), b_ref[...].astype(BF16), 0, 0)

    return pl.pallas_call(
        body, name=name, grid=(n_out, S5_BLOCKS),
        in_specs=[pl.BlockSpec((None, M, ka), (lambda o, q: (o, 0, q)) if na > 1 else (lambda o, q: (0, 0, q))),
                  pl.BlockSpec((None, M, kb), (lambda o, q: (o, 0, q)) if nb_ > 1 else (lambda o, q: (0, 0, q)))],
        out_specs=pl.BlockSpec((None, None, ka, kb), lambda o, q: (o, q, 0, 0)),
        out_shape=jax.ShapeDtypeStruct((n_out, S5_BLOCKS, ka, kb), F32), compiler_params=_cparams(),
    )(a, b)


def s5_u(proj, lp):
    def fn(row0, rv, pv):
        u, = rv
        return ((jnp.where(_row_mask(row0, u.shape), u, 0.0),), ())

    return rowwise(fn, [(proj, MIX, 5)], [], [(MIX, BF16)], name="s5_u", lp=lp)[0]


def s5_y(ys, proj, d, lp):
    def fn(row0, rv, pv):
        ys_, u = rv
        y = ys_ + pv[0] * u
        return ((y, _gelu(y)), ())

    return rowwise(fn, [(ys, MIX, 0), (proj, MIX, 5)], [d], [(MIX, F32), (MIX, BF16)], name="s5_y", lp=lp)


def s5_glu(z, y, b, lp):
    def fn(row0, rv, pv):
        z_, y_ = rv
        return ((_gelu(y_) * _sigmoid(z_ + pv[0]),), ())

    return rowwise(fn, [(z, MIX, 0), (y, MIX, 0)], [b], [(MIX, BF16)], name="s5_glu", lp=lp)[0]


def s5_glu_bwd(dgl, z, y, b, lp):
    def fn(row0, rv, pv):
        dgl_, z_, y_ = rv
        sg = _sigmoid(z_ + pv[0])
        dz = dgl_ * _gelu(y_) * sg * (1.0 - sg)
        return ((dgl_ * sg, dz), (jnp.sum(dz, axis=0, keepdims=True),))

    return rowwise(fn, [(dgl, MIX, 0), (z, MIX, 0), (y, MIX, 0)], [b], [(MIX, F32), (MIX, BF16)], [((1, MIX), F32)],
                   name="s5_glu_bwd", lp=lp)


def s5_y_bwd(dyg, y, proj, d, lp):
    def fn(row0, rv, pv):
        dyg_, y_, u = rv
        dy = dyg_ * _gelu_grad(y_)
        return ((dy, dy * pv[0]), (jnp.sum(dy * u, axis=0, keepdims=True),))

    return rowwise(fn, [(dyg, MIX, 0), (y, MIX, 0), (proj, MIX, 5)], [d], [(MIX, BF16), (MIX, F32)], [((1, MIX), F32)],
                   name="s5_y_bwd", lp=lp)


def s5_du(du, lp):
    def fn(row0, rv, pv):
        return ((jnp.where(_row_mask(row0, rv[0].shape), rv[0], 0.0),), ())

    return rowwise(fn, [(du, MIX, 0)], [], [(MIX, BF16)], name="s5_du", lp=lp)[0]


def merge_fwd(proj, ya, yb, yc, lp):
    def fn(row0, rv, pv):
        g0, g1, g2, a, b, c = rv
        return ((_sigmoid(g0) * a + _sigmoid(g1) * b + _sigmoid(g2) * c,), ())

    return rowwise(fn, [(proj, D_MODEL, 3), (proj, D_MODEL, 4), (proj, D_MODEL, 5), (ya, D_MODEL, 0), (yb, D_MODEL, 0),
                        (yc, D_MODEL, 0)], [], [(D_MODEL, BF16)], name="merge_fwd", lp=lp)[0]


def merge_bwd(dmix, proj, ya, yb, yc, lp):
    def fn(row0, rv, pv):
        dm, g0, g1, g2, a, b, c = rv
        outs_y, outs_g = [], []
        for g, yv in ((g0, a), (g1, b), (g2, c)):
            sg = _sigmoid(g)
            outs_y.append(dm * sg)
            outs_g.append(dm * yv * sg * (1.0 - sg))
        return (tuple(outs_y) + tuple(outs_g), ())

    return rowwise(fn, [(dmix, D_MODEL, 0), (proj, D_MODEL, 3), (proj, D_MODEL, 4), (proj, D_MODEL, 5),
                        (ya, D_MODEL, 0), (yb, D_MODEL, 0), (yc, D_MODEL, 0)], [], [(D_MODEL, BF16)] * 6,
                   name="merge_bwd", lp=lp)


def loss_head(h, tgt, lp):
    def fn(row0, rv, pv):
        h_, t_ = rv
        live = (row0 + lax.broadcasted_iota(jnp.int32, h_.shape, 0)) >= X0
        diff = jnp.where(live, h_ - t_, 0.0)
        ssq = jnp.sum(jnp.sum(diff * diff, axis=1, keepdims=True), axis=0, keepdims=True)
        return ((diff * (1.0 / D_MODEL),), (ssq * (0.5 / D_MODEL),))

    return rowwise(fn, [(h, D_MODEL, 0), (tgt, D_MODEL, 0)], [], [(D_MODEL, F32)], [((1, 1), F32)], name="loss_head", lp=lp)


def _s5_consts(W):
    ab_re_rep, ab_im_rep, bb_re, bb_im = s5_prep(W['s5_a_re'], W['s5_a_im'], W['s5_log_dt'], W['s5_b_re'], W['s5_b_im'])
    pick = lambda t: t.reshape(S5_GROUPS, S5_GROUP, S5_STATE)[:, 0].reshape(1, S5_LANES)
    bb = jnp.stack([_blockdiag(bb_re.reshape(S5_GROUPS, S5_GROUP, S5_STATE)),
                    _blockdiag(bb_im.reshape(S5_GROUPS, S5_GROUP, S5_STATE))]).astype(BF16)
    return pick(ab_re_rep), pick(ab_im_rep), bb


def layer_fwd(h, hb, W, tabs, lp):
    cos, sin, rot = tabs
    h1, h1b, sv1 = ffn_fwd(h, hb, W['wg1'], W['wu1'], W['wd1'], W['ln1_g'], W['ln1_b'], lp)
    proj = matmul(h1b[None], W['w_in'][None], tb=True, name="proj")[0]
    cqn, ckvn, kr = mla_prep(proj, cos, sin, rot, W['q_norm_g'], W['kv_norm_g'], lp)
    qn, qr, kn, v = mla_heads(cqn, ckvn, cos, sin, rot, W['wqn'], W['wqr'], W['wkn'], W['wv'], lp)
    o = attn_fwd(qn, qr, kn, v, kr, lp)
    ya = matmul(o, W['mla_wo'], ab='r', bb='r', name="mla_out")[0]
    vconv = conv_fwd(proj, W['conv_w'], W['conv_b'], lp)
    yb = matmul(vconv[None], W['conv_wout'][None], name="conv_out")[0]
    ub = s5_u(proj, lp)
    ab_re, ab_im, bb = _s5_consts(W)
    bu = bd_matmul(ub[None], bb, w_t=False, reduce=False, name="s5_bu")
    s = s5_scan(bu, ab_re, ab_im, lp)
    ys = bd_matmul(s, W['s5_ct'], w_t=False, reduce=True, name="s5_cs")[0]
    y, ygb = s5_y(ys, proj, W['s5_d'], lp)
    zg = matmul(ygb[None], W['s5_wglu'][None], name="s5_glu_mm")[0]
    glb = s5_glu(zg, y, W['s5_b_glu'], lp)
    yc = matmul(glb[None], W['s5_wout'][None], name="s5_out")[0]
    mixed = merge_fwd(proj, ya, yb, yc, lp)
    z2, h2, h2b = mm_res_ln(mixed[None], W['w_o'][None], h1, W['ln2_g'], W['ln2_b'], scale=1.0, name="wo_ln", lp=lp)
    h3, h3b, sv3 = ffn_fwd(h2, h2b, W['wg2'], W['wu2'], W['wd2'], W['ln3_g'], W['ln3_b'], lp)
    sv = dict(sv1=sv1, sv3=sv3, h1b=h1b, proj=proj, cqn=cqn, ckvn=ckvn, kr=kr, qn=qn, qr=qr, kn=kn, v=v, o=o, ya=ya,
              vconv=vconv, yb=yb, ub=ub, ab_re=ab_re, ab_im=ab_im, bb=bb, s=s, y=y, ygb=ygb, zg=zg, glb=glb, yc=yc,
              mixed=mixed, z2=z2)
    return h3, h3b, sv


def layer_bwd(dh3, sv, W, tabs, lp):
    cos, sin, rot = tabs
    proj = sv['proj']
    G = {}
    dh2, g3 = ffn_bwd(dh3, sv['sv3'], W['wg2'], W['wu2'], W['wd2'], W['ln3_g'], lp)
    G.update(wg2=g3['wg'], wu2=g3['wu'], wd2=g3['wd'], ln3_g=g3['ln_g'], ln3_b=g3['ln_b'])
    dz2, dz2b, G['ln2_g'], G['ln2_b'] = ln_bwd(dh2, sv['z2'], W['ln2_g'], fscale=1.0, name="wo_ln_bwd", lp=lp)
    dmix = matmul(dz2b[None], W['w_o'][None], tb=True, name="wo_dx")[0]
    G['w_o'] = matmul(sv['mixed'][None], dz2b[None], ta=True, out_dtype=WGRAD, name="wo_dw")[0]
    dya, dyb, dyc, dg0, dg1, dg2 = merge_bwd(dmix, proj, sv['ya'], sv['yb'], sv['yc'], lp)
    dgl = matmul(dyc[None], W['s5_wout'][None], tb=True, name="s5_out_dx")[0]
    G['s5_wout'] = matmul(sv['glb'][None], dyc[None], ta=True, out_dtype=WGRAD, name="s5_out_dw")[0]
    t1, dzb, G['s5_b_glu'] = s5_glu_bwd(dgl, sv['zg'], sv['y'], W['s5_b_glu'], lp)
    dyg = matmul(dzb[None], W['s5_wglu'][None], tb=True, res=t1[None], name="s5_glu_dx")[0]
    G['s5_wglu'] = matmul(sv['ygb'][None], dzb[None], ta=True, out_dtype=WGRAD, name="s5_glu_dw")[0]
    dyb_, du_d, G['s5_d'] = s5_y_bwd(dyg, sv['y'], proj, W['s5_d'], lp)
    ds = bd_matmul(dyb_[None], W['s5_ct'], w_t=True, reduce=False, name="s5_cs_dx")
    G['s5_ct'] = bd_outer(sv['s'], dyb_[None], "s5_cs_dw")
    g_adj, d_ab = s5_scan_bwd(ds, sv['s'], sv['ab_re'], sv['ab_im'], lp)
    du = bd_matmul(g_adj, sv['bb'], w_t=True, reduce=True, res=du_d[None], name="s5_bu_dx")[0]
    d_bb = bd_outer(sv['ub'][None], g_adj, "s5_bu_dw")
    du_b = s5_du(du, lp)
    onehot = (jnp.arange(S5_GROUP) == 0).astype(F32)
    spread = lambda t: (t.reshape(S5_GROUPS, 1, S5_STATE) * onehot[None, :, None]).reshape(_S5_ROWS, S5_STATE)
    take = lambda t: _blockdiag_extract(t, S5_GROUP, S5_STATE).reshape(_S5_ROWS, S5_STATE)
    sel = jnp.kron(jnp.eye(S5_GROUPS, dtype=F32), jnp.ones((1, S5_GROUP), F32))
    (G['s5_a_re'], G['s5_a_im'], G['s5_log_dt'], G['s5_b_re'], G['s5_b_im']) = s5_prep_bwd(
        W['s5_a_re'], W['s5_a_im'], W['s5_log_dt'], W['s5_b_re'], W['s5_b_im'],
        spread(d_ab[0]), spread(d_ab[1]), take(d_bb[0]), take(d_bb[1]), sel)
    dv = matmul(dyb[None], W['conv_wout'][None], tb=True, name="conv_out_dx")[0]
    G['conv_wout'] = matmul(sv['vconv'][None], dyb[None], ta=True, out_dtype=WGRAD, name="conv_out_dw")[0]
    dxbar, dbg, dcg, G['conv_w'], G['conv_b'] = conv_bwd(dv, proj, W['conv_w'], W['conv_b'], lp)
    do = matmul(dya[None], W['mla_wo'], tb=True, bb='o', out_dtype=BF16, name="mla_out_dx")
    G['mla_wo'] = matmul(sv['o'], dya[None], ta=True, ab='o', out_dtype=WGRAD, name="mla_out_dw")
    dqn, dqr, dkn, dvv, dkr = attn_bwd(sv['qn'], sv['qr'], sv['kn'], sv['v'], sv['kr'], do, lp)
    dcq, dckv, dqrp = mla_heads_bwd(dqn, dqr, dkn, dvv, cos, sin, rot, W['wqn'], W['wqr'], W['wkn'], W['wv'], lp)
    G['wqn'] = matmul(sv['cqn'][None], dqn, ta=True, bb='o', out_dtype=WGRAD, name="mla_dwqn")
    G['wqr'] = matmul(sv['cqn'][None], dqrp, ta=True, bb='o', out_dtype=WGRAD, name="mla_dwqr")
    G['wkn'] = matmul(sv['ckvn'][None], dkn, ta=True, bb='o', out_dtype=WGRAD, name="mla_dwkn")
    G['wv'] = matmul(sv['ckvn'][None], dvv, ta=True, bb='o', out_dtype=WGRAD, name="mla_dwv")
    dcq_raw, dckv_raw, dkr_raw, G['q_norm_g'], G['kv_norm_g'] = mla_prep_bwd(
        dcq, dckv, dkr, proj, cos, sin, rot, W['q_norm_g'], W['kv_norm_g'], lp)
    zeros = lambda n: jnp.zeros((lp, n), BF16)
    dproj = jnp.concatenate([dcq_raw, dkr_raw, zeros(96), dckv_raw, zeros(256), dxbar, dbg, dcg, du_b, dg0, dg1, dg2], axis=1)
    dh1 = matmul(dproj[None], W['w_in'][None], res=dz2[None], res_scale=ALPHA, name="proj_dx")[0]
    G['w_in'] = matmul(dproj[None], sv['h1b'][None], ta=True, out_dtype=WGRAD, name="proj_dw")[0]
    dh0, g1 = ffn_bwd(dh1, sv['sv1'], W['wg1'], W['wu1'], W['wd1'], W['ln1_g'], lp)
    G.update(wg1=g1['wg'], wu1=g1['wu'], wd1=g1['wd'], ln1_g=g1['ln_g'], ln1_b=g1['ln_b'])
    return dh0, G


def _nat_cols(st):
    return jnp.transpose(st, (1, 0, 2)).reshape(st.shape[1], -1)


def _shard_cols(nat):
    k, n = nat.shape
    return jnp.transpose(nat.reshape(k, N_SHARD, n // N_SHARD), (1, 0, 2))


def _win_pad(wt):
    z = lambda n: jnp.zeros((n, wt.shape[1]), wt.dtype)
    return jnp.concatenate([wt[0:384], wt[640:672], z(96), wt[384:640], z(256), wt[672:]], axis=0)


def _win_unpad(wp):
    return jnp.concatenate([wp[0:384], wp[512:768], wp[384:416], wp[1024:]], axis=0)


_BIG = [('ffn1_w_gate', 'T'), ('ffn1_w_up', 'T'), ('ffn1_w_down', 0), ('w_in', 'T'), ('mla_w_uq', 1), ('mla_w_ukv', 1),
        ('mla_w_o', 1), ('conv_w_out', 1), ('s5_w_glu', 0), ('s5_w_out', 1), ('w_o', 0),
        ('ffn2_w_gate', 'T'), ('ffn2_w_up', 'T'), ('ffn2_w_down', 0)]
_REPL = ['ln1_g', 'ln1_b', 'mla_q_norm_g', 'mla_kv_norm_g', 'conv_b', 's5_a_re', 's5_a_im', 's5_log_dt', 's5_b_re',
         's5_b_im', 's5_c_re', 's5_c_im', 's5_d', 's5_b_glu', 'ln2_g', 'ln2_b', 'ln3_g', 'ln3_b']


def compute_weights(st, small):
    W = {}
    for t in ('1', '2'):
        W['wg' + t], W['wu' + t] = st['ffn%s_w_gate' % t], st['ffn%s_w_up' % t]
        W['wd' + t] = st['ffn%s_w_down' % t]
    W['w_in'] = _win_pad(st['w_in'].reshape(D_IN, D_MODEL))
    uq = jnp.transpose(_nat_cols(st['mla_w_uq']).reshape(Q_RANK, N_HEADS, D_NOPE + D_ROPE), (1, 0, 2))
    W['wqn'], W['wqr'] = uq[:, :, :D_NOPE], uq[:, :, D_NOPE:]
    ukv = jnp.transpose(_nat_cols(st['mla_w_ukv']).reshape(KV_RANK, N_HEADS, D_NOPE + D_V), (1, 0, 2))
    W['wkn'], W['wv'] = ukv[:, :, :D_NOPE], ukv[:, :, D_NOPE:]
    W['mla_wo'] = _nat_cols(st['mla_w_o']).reshape(N_HEADS, D_V, D_MODEL)
    W['conv_wout'] = _nat_cols(st['conv_w_out'])
    W['s5_wglu'] = st['s5_w_glu'].reshape(MIX, MIX)
    W['s5_wout'] = _nat_cols(st['s5_w_out'])
    W['w_o'] = st['w_o'].reshape(D_MODEL, D_MODEL)
    W['conv_w'] = small['conv_w']
    for n in ('ln1_g', 'ln1_b', 'ln2_g', 'ln2_b', 'ln3_g', 'ln3_b', 'conv_b', 's5_b_glu'):
        W[n] = small[n].reshape(1, -1)
    W['q_norm_g'] = small['mla_q_norm_g'].reshape(1, -1)
    W['kv_norm_g'] = small['mla_kv_norm_g'].reshape(1, -1)
    W['s5_d'] = small['s5_d'].reshape(1, MIX)
    rep = lambda t: jnp.repeat(t, S5_GROUP, axis=0)
    W['s5_a_re'], W['s5_a_im'] = rep(small['s5_a_re']), rep(small['s5_a_im'])
    W['s5_log_dt'] = jnp.broadcast_to(rep(small['s5_log_dt'].reshape(S5_GROUPS, 1)), (_S5_ROWS, S5_STATE))
    tr = lambda t: jnp.transpose(t, (0, 2, 1)).reshape(_S5_ROWS, S5_STATE)
    W['s5_b_re'], W['s5_b_im'] = tr(small['s5_b_re']), tr(small['s5_b_im'])
    ct = lambda t: _blockdiag(jnp.transpose(t, (0, 2, 1)))
    W['s5_ct'] = jnp.stack([ct(small['s5_c_re']), -ct(small['s5_c_im'])]).astype(BF16)
    return W


def reference_grads(G, ffn=True):
    R = {}
    for t in ('1', '2') if ffn else ():
        R['ffn%s_w_gate' % t] = G['wg' + t].reshape(D_FF, D_MODEL).T
        R['ffn%s_w_up' % t] = G['wu' + t].reshape(D_FF, D_MODEL).T
        R['ffn%s_w_down' % t] = G['wd' + t].reshape(D_FF, D_MODEL)
    R['w_in_t'] = _win_unpad(G['w_in'])
    if ffn:
        R['w_in'] = R['w_in_t'].T
    R['mla_w_uq'] = jnp.transpose(jnp.concatenate([G['wqn'], G['wqr']], axis=2), (1, 0, 2)).reshape(Q_RANK, -1)
    R['mla_w_ukv'] = jnp.transpose(jnp.concatenate([G['wkn'], G['wv']], axis=2), (1, 0, 2)).reshape(KV_RANK, -1)
    R['mla_w_o'] = G['mla_wo'].reshape(N_HEADS * D_V, D_MODEL)
    R['conv_w'], R['conv_w_out'] = G['conv_w'], G['conv_wout']
    R['s5_w_glu'], R['s5_w_out'], R['w_o'] = G['s5_wglu'], G['s5_wout'], G['w_o']
    for n in ('ln1_g', 'ln1_b', 'ln2_g', 'ln2_b', 'ln3_g', 'ln3_b', 'conv_b', 's5_b_glu'):
        R[n] = G[n].reshape(-1)
    R['mla_q_norm_g'], R['mla_kv_norm_g'] = G['q_norm_g'].reshape(-1), G['kv_norm_g'].reshape(-1)
    R['s5_d'] = G['s5_d'].reshape(S5_GROUPS, S5_GROUP)
    R['s5_a_re'], R['s5_a_im'], R['s5_log_dt'] = G['s5_a_re'], G['s5_a_im'], G['s5_log_dt'].reshape(-1)
    untr = lambda t: jnp.transpose(t.reshape(S5_GROUPS, S5_GROUP, S5_STATE), (0, 2, 1))
    R['s5_b_re'], R['s5_b_im'] = untr(G['s5_b_re']), untr(G['s5_b_im'])
    unct = lambda t: jnp.transpose(_blockdiag_extract(t, S5_STATE, S5_GROUP), (0, 2, 1))
    R['s5_c_re'], R['s5_c_im'] = unct(G['s5_ct'][0]), -unct(G['s5_ct'][1])
    return R


_ANY = pl.BlockSpec(memory_space=pl.ANY)
LANES = 1024


def _place():
    x, y, c = lax.axis_index("x"), lax.axis_index("y"), lax.axis_index("c")
    chips = [(1 - x, y), (x, 1 - y), (1 - x, 1 - y)]
    return x, y, c, chips


def _rows_of(c, half):
    return pl.ds(pl.multiple_of(c * half, 8), half)


def all_gather_shards(srcs, exact):
    n, m = len(srcs), len(exact)
    halves = [s.shape[0] // 2 for s in srcs]

    def body(*refs):
        s_refs, e_refs = refs[:n], refs[n:n + m]
        o_refs, eo_refs = refs[n + m:2 * n + m], refs[2 * n + m:2 * n + 2 * m]
        send, recv, esend, erecv, osend, orecv, lsem = refs[2 * n + 2 * m:]
        x, y, c, chips = _place()
        me = 2 * x + y
        sibling = (x, y, 1 - c)
        own = [pltpu.make_async_remote_copy(src_ref=s_refs[k], dst_ref=o_refs[k].at[me], send_sem=osend.at[k],
                                            recv_sem=orecv.at[k], device_id=sibling, device_id_type=MESH) for k in range(n)]
        local = [pltpu.make_async_copy(e_refs[k], eo_refs[k].at[me], lsem.at[k]) for k in range(m)]
        for cp in own + local:
            cp.start()

        def copy(k, s, src, idx, half_c, to):
            return pltpu.make_async_remote_copy(
                src_ref=src, dst_ref=o_refs[k].at[idx, _rows_of(half_c, halves[k])], send_sem=send.at[6 * k + s],
                recv_sem=recv.at[6 * k + s], device_id=to, device_id_type=MESH)

        def ecopy(k, j, idx, to):
            return pltpu.make_async_remote_copy(src_ref=e_refs[k], dst_ref=eo_refs[k].at[idx], send_sem=esend.at[3 * k + j],
                                                recv_sem=erecv.at[3 * k + j], device_id=to, device_id_type=MESH)

        sends = []
        for k in range(n):
            mine = s_refs[k].at[_rows_of(c, halves[k])]
            sends += [copy(k, j, mine, me, c, (*chip, c)) for j, chip in enumerate(chips)]
        for k in range(m):
            sends += [ecopy(k, j, me, (*chip, c)) for j, chip in enumerate(chips)]
        for cp in sends:
            cp.start()
        for j, chip in enumerate(chips):
            idx = 2 * chip[0] + chip[1]
            for k in range(n):
                landed = o_refs[k].at[idx, _rows_of(c, halves[k])]
                copy(k, j, landed, idx, c, sibling).wait_recv()
                fwd = copy(k, 3 + j, landed, idx, c, sibling)
                fwd.start()
                sends.append(fwd)
        for j, chip in enumerate(chips):
            idx = 2 * chip[0] + chip[1]
            for k in range(n):
                copy(k, 3 + j, s_refs[k].at[_rows_of(c, halves[k])], idx, 1 - c, sibling).wait_recv()
            for k in range(m):
                ecopy(k, j, idx, sibling).wait_recv()
        for cp in sends:
            cp.wait_send()
        for cp in own + local:
            cp.wait()

    outs = pl.pallas_call(
        body, name="all_gather_weights", in_specs=[_ANY] * (n + m), out_specs=[_ANY] * (n + m),
        out_shape=[jax.ShapeDtypeStruct((N_SHARD,) + a.shape, a.dtype) for a in list(srcs) + list(exact)],
        scratch_shapes=[pltpu.SemaphoreType.DMA((6 * n,)), pltpu.SemaphoreType.DMA((6 * n,)),
                        pltpu.SemaphoreType.DMA((3 * m,)), pltpu.SemaphoreType.DMA((3 * m,)),
                        pltpu.SemaphoreType.DMA((n,)), pltpu.SemaphoreType.DMA((n,)), pltpu.SemaphoreType.DMA((m,))],
    )(*srcs, *exact)
    return outs[:n], outs[n:]


def rs_pair_swap(gs):
    n = len(gs)

    def body(*refs):
        g_refs, r_refs, send, recv = refs[:n], refs[n:2 * n], refs[2 * n], refs[2 * n + 1]
        x, y, c, _ = _place()
        copies = [pltpu.make_async_remote_copy(
            src_ref=g_refs[k].at[pl.ds(0, N_SHARD), _rows_of(1 - c, gs[k].shape[1] // 2)], dst_ref=r_refs[k],
            send_sem=send.at[k], recv_sem=recv.at[k], device_id=(x, y, 1 - c), device_id_type=MESH) for k in range(n)]
        for cp in copies:
            cp.start()
        for cp in copies:
            cp.wait()

    return pl.pallas_call(
        body, name="grad_pair_swap", in_specs=[_ANY] * n, out_specs=[_ANY] * n,
        out_shape=[jax.ShapeDtypeStruct((N_SHARD, g.shape[1] // 2, g.shape[2]), g.dtype) for g in gs],
        scratch_shapes=[pltpu.SemaphoreType.DMA((n,)), pltpu.SemaphoreType.DMA((n,))],
    )(*gs)


def _group_tile(half, n_cols, n_arrays):
    budget = (20 * 2 ** 20) // (6 * n_arrays)
    fits = [t for t in range(8, half + 1, 8) if half % t == 0 and t * n_cols * 4 <= budget]
    return max(fits) if fits else 8


def rs_pair_add(gs, rs, cidx, out_dtype, name):
    n = len(gs)
    _, K, cols = gs[0].shape
    half = K // 2
    tr = _group_tile(half, cols, n)
    nb = half // tr

    def body(c_ref, *refs):
        for g_ref, r_ref, o_ref in zip(refs[:n], refs[n:2 * n], refs[2 * n:]):
            o_ref[...] = (g_ref[...].astype(F32) + r_ref[...].astype(F32)).astype(out_dtype)

    gspec = pl.BlockSpec((None, tr, cols), lambda j, i, c: (j, c[0] * nb + i, 0))
    rspec = pl.BlockSpec((None, tr, cols), lambda j, i, c: (j, i, 0))
    return pl.pallas_call(
        body, name=name,
        grid_spec=pltpu.PrefetchScalarGridSpec(num_scalar_prefetch=1, grid=(N_SHARD, nb), in_specs=[gspec] * n + [rspec] * n,
                                               out_specs=[rspec] * n),
        out_shape=[jax.ShapeDtypeStruct((N_SHARD, half, cols), out_dtype)] * n,
        compiler_params=_cparams(),
    )(cidx, *gs, *rs)


def rs_chip_scatter(ps):
    n = len(ps)

    def body(*refs):
        p_refs, q_refs, send, recv, lsem = refs[:n], refs[n:2 * n], refs[2 * n], refs[2 * n + 1], refs[2 * n + 2]
        x, y, c, chips = _place()
        me = 2 * x + y
        local = [pltpu.make_async_copy(p_refs[k].at[me], q_refs[k].at[me], lsem.at[k]) for k in range(n)]
        copies = [pltpu.make_async_remote_copy(
            src_ref=p_refs[k].at[2 * chip[0] + chip[1]], dst_ref=q_refs[k].at[me], send_sem=send.at[3 * k + j],
            recv_sem=recv.at[3 * k + j], device_id=(*chip, c), device_id_type=MESH)
            for k in range(n) for j, chip in enumerate(chips)]
        for cp in local + copies:
            cp.start()
        for cp in copies:
            cp.wait()
        for cp in local:
            cp.wait()

    return pl.pallas_call(
        body, name="grad_chip_scatter", in_specs=[_ANY] * n, out_specs=[_ANY] * n,
        out_shape=[jax.ShapeDtypeStruct(p.shape, p.dtype) for p in ps],
        scratch_shapes=[pltpu.SemaphoreType.DMA((3 * n,)), pltpu.SemaphoreType.DMA((3 * n,)), pltpu.SemaphoreType.DMA((n,))],
    )(*ps)


def rs_chip_sum(qs, nl, cidx, name):
    n = len(qs)
    _, half, cols = qs[0].shape
    tr = _group_tile(half, cols, n)
    nb = half // tr

    def body(c_ref, *refs):
        for k, q_ref in enumerate(refs[:n]):
            o_ref = refs[n + k // nl]
            o_ref[k % nl] = ((q_ref[0].astype(F32) + q_ref[1].astype(F32)) + q_ref[2].astype(F32)) + q_ref[3].astype(F32)

    return pl.pallas_call(
        body, name=name,
        grid_spec=pltpu.PrefetchScalarGridSpec(
            num_scalar_prefetch=1, grid=(nb,),
            in_specs=[pl.BlockSpec((N_SHARD, tr, cols), lambda i, c: (0, i, 0))] * n,
            out_specs=[pl.BlockSpec((nl, tr, cols), lambda i, c: (0, c[0] * nb + i, 0))] * (n // nl)),
        out_shape=[jax.ShapeDtypeStruct((nl, 2 * half, cols), F32)] * (n // nl),
        compiler_params=_cparams(),
    )(cidx, *qs)


def rs_pair_gather(fs):
    n = len(fs)

    def body(*refs):
        f_refs, send, recv = refs[n:2 * n], refs[2 * n], refs[2 * n + 1]
        x, y, c, _ = _place()
        copies = []
        for k in range(n):
            rows = f_refs[k].at[pl.ds(0, fs[k].shape[0]), _rows_of(c, fs[k].shape[1] // 2)]
            copies.append(pltpu.make_async_remote_copy(src_ref=rows, dst_ref=rows, send_sem=send.at[k], recv_sem=recv.at[k],
                                                       device_id=(x, y, 1 - c), device_id_type=MESH))
        for cp in copies:
            cp.start()
        for cp in copies:
            cp.wait()

    return pl.pallas_call(
        body, name="grad_pair_gather", in_specs=[_ANY] * n, out_specs=[_ANY] * n,
        out_shape=[jax.ShapeDtypeStruct(f.shape, f.dtype) for f in fs],
        input_output_aliases={k: k for k in range(n)},
        scratch_shapes=[pltpu.SemaphoreType.DMA((n,)), pltpu.SemaphoreType.DMA((n,))],
    )(*fs)


_HBM = pl.BlockSpec(memory_space=pltpu.HBM)
_SEM = pl.BlockSpec(memory_space=pltpu.SEMAPHORE)
_EFFECT = pltpu.SideEffectType.DATAFLOW_SIDE_EFFECTING


def _in_hbm(a):
    return pltpu.with_memory_space_constraint(a, pltpu.HBM)


def split_start(name, srcs, lands, after, copies_fn, n_copies):
    n = len(srcs)

    def body(*refs):
        for cp in copies_fn(refs[:n], refs[n:2 * n], refs[2 * n + 1], refs[2 * n + 2]):
            cp.start()
        refs[-1][...] = jnp.zeros_like(refs[-1])

    bufs = list(srcs) + list(lands)
    outs = pl.pallas_call(
        body, name=name,
        out_shape=(pltpu.SemaphoreType.DMA((n_copies,)), pltpu.SemaphoreType.DMA((n_copies,)),
                   *[pltpu.HBM(a.shape, a.dtype) for a in bufs], jax.ShapeDtypeStruct((8, 128), F32)),
        in_specs=[_HBM] * (2 * n) + [_ANY],
        out_specs=(_SEM, _SEM, *[_HBM] * (2 * n), pl.BlockSpec(memory_space=pltpu.VMEM)),
        input_output_aliases={i: 2 + i for i in range(2 * n)},
        compiler_params=pltpu.CompilerParams(has_side_effects=_EFFECT),
    )(*[_in_hbm(a) for a in bufs], after)
    return outs[0], outs[1], outs[2:2 + n], outs[2 + n:2 + 2 * n], outs[-1]


def split_wait(name, send, recv, srcs, lands, after, copies_fn):
    n = len(srcs)

    def body(*refs):
        copies = copies_fn(refs[:n], refs[n:2 * n], refs[2 * n], refs[2 * n + 1])
        for cp in copies:
            cp.wait_send()
        for cp in copies:
            cp.wait_recv()

    bufs = list(srcs) + list(lands)
    outs = pl.pallas_call(
        body, name=name, out_shape=tuple(pltpu.HBM(a.shape, a.dtype) for a in bufs),
        in_specs=[_HBM] * (2 * n) + [_SEM, _SEM, _ANY], out_specs=tuple([_HBM] * (2 * n)),
        input_output_aliases={i: i for i in range(2 * n)},
        compiler_params=pltpu.CompilerParams(has_side_effects=_EFFECT),
    )(*bufs, send, recv, after)
    return list(outs[n:])


def _gather_copies(s_refs, l_refs, send, recv):
    x, y, c, chips = _place()
    me = 2 * x + y
    out = []
    for k, (s, l) in enumerate(zip(s_refs, l_refs)):
        rows = _rows_of(c, s.shape[0] // 2)
        for j, chip in enumerate(chips):
            out.append(pltpu.make_async_remote_copy(src_ref=s.at[rows], dst_ref=l.at[me, rows], send_sem=send.at[4 * k + j],
                                                    recv_sem=recv.at[4 * k + j], device_id=(*chip, c), device_id_type=MESH))
        out.append(pltpu.make_async_remote_copy(src_ref=s, dst_ref=l.at[me], send_sem=send.at[4 * k + 3],
                                                recv_sem=recv.at[4 * k + 3], device_id=(x, y, 1 - c), device_id_type=MESH))
    return out


def _scatter_copies(s_refs, l_refs, send, recv):
    x, y, c, chips = _place()
    me = 2 * x + y
    return [pltpu.make_async_remote_copy(src_ref=s.at[2 * chip[0] + chip[1]], dst_ref=l.at[me], send_sem=send.at[3 * k + j],
                                         recv_sem=recv.at[3 * k + j], device_id=(*chip, c), device_id_type=MESH)
            for k, (s, l) in enumerate(zip(s_refs, l_refs)) for j, chip in enumerate(chips)]


def gather_forward(lands):
    n = len(lands)

    def body(*refs):
        l_refs, send, recv = refs[n:2 * n], refs[2 * n], refs[2 * n + 1]
        x, y, c, chips = _place()
        copies = []
        for k in range(n):
            rows = _rows_of(c, lands[k].shape[1] // 2)
            for j, chip in enumerate(chips):
                part = l_refs[k].at[2 * chip[0] + chip[1], rows]
                copies.append(pltpu.make_async_remote_copy(src_ref=part, dst_ref=part, send_sem=send.at[3 * k + j],
                                                           recv_sem=recv.at[3 * k + j], device_id=(x, y, 1 - c),
                                                           device_id_type=MESH))
        for cp in copies:
            cp.start()
        for cp in copies:
            cp.wait()

    return pl.pallas_call(
        body, name="gather_forward", in_specs=[_ANY] * n, out_specs=[_ANY] * n,
        out_shape=[jax.ShapeDtypeStruct(a.shape, a.dtype) for a in lands],
        input_output_aliases={k: k for k in range(n)},
        scratch_shapes=[pltpu.SemaphoreType.DMA((3 * n,)), pltpu.SemaphoreType.DMA((3 * n,))],
    )(*lands)


def rs_partials(gs, wire, cidx, tag):
    rs = rs_pair_swap(gs)
    groups = {}
    for k, g in enumerate(gs):
        groups.setdefault((g.shape, jnp.dtype(wire[k]).name), []).append(k)
    ps = [None] * len(gs)
    for gi, ks in enumerate(groups.values()):
        outs = rs_pair_add([gs[k] for k in ks], [rs[k] for k in ks], cidx, wire[ks[0]], "grad_pair_add_%s%d" % (tag, gi))
        for k, o in zip(ks, outs):
            ps[k] = o
    return ps


def rs_finish(items):
    cidx = lax.axis_index("c").astype(jnp.int32).reshape(1)
    groups = {}
    for i, it in enumerate(items):
        groups.setdefault((it[0].shape, len(it), it[0].dtype.name), []).append(i)
    fs = [None] * len(items)
    for gi, ids in enumerate(groups.values()):
        outs = rs_chip_sum([q for i in ids for q in items[i]], len(items[ids[0]]), cidx, "grad_chip_sum_%d" % gi)
        for i, o in zip(ids, outs):
            fs[i] = o
    return rs_pair_gather(fs)


def adamw(w, g, m, v, name):
    shape = w.shape
    if w.ndim == 2:
        block, grid, index = shape, (1,), (lambda i: (0, 0))
    else:
        slab = shape[2:]
        unit = 4 * int(np.prod(slab[:-2] or (1,))) * (-(-slab[-1] // 128) * 128)
        if len(slab) >= 2:
            unit *= -(-slab[-2] // 8) * 8
        k = shape[1]
        tr = k
        if k * unit > 2 ** 21:
            tr = max(t for t in range(8, k, 8) if k % t == 0 and t * unit <= 2 ** 21)
        block, grid = (None, tr) + tuple(slab), (shape[0], k // tr)
        index = lambda l, i: (l, i) + (0,) * len(slab)
        if tr < min(k, 64) and len(slab) == 1:
            tc = max(t for t in range(128, slab[0] + 1, 128) if slab[0] % t == 0 and k * t * 4 <= 2 ** 21)
            block, grid = (None, k, tc), (shape[0], slab[0] // tc)
            index = lambda l, i: (l, 0, i)

    def body(w_ref, g_ref, m_ref, v_ref, d_ref, nm_ref, nv_ref):
        g_ = g_ref[...]
        m_new = ADAM_B1 * m_ref[...] + (1.0 - ADAM_B1) * g_
        v_new = ADAM_B2 * v_ref[...] + (1.0 - ADAM_B2) * (g_ * g_)
        m_hat = m_new / (1.0 - ADAM_B1 ** ADAM_STEP)
        v_hat = v_new / (1.0 - ADAM_B2 ** ADAM_STEP)
        d_ref[...] = -ADAM_LR * (m_hat / (jnp.sqrt(v_hat) + ADAM_EPS) + ADAM_WD * w_ref[...])
        nm_ref[...] = m_new
        nv_ref[...] = v_new

    spec = pl.BlockSpec(block, index)
    return pl.pallas_call(
        body, name=name, grid=grid, in_specs=[spec] * 4, out_specs=[spec] * 3,
        out_shape=[jax.ShapeDtypeStruct(shape, F32)] * 3, compiler_params=_cparams(),
    )(w, g, m, v)


_WEIGHTS = ['meta', 'ffn1_w_gate', 'ffn1_w_up', 'ffn1_w_down', 'ln1_g', 'ln1_b', 'w_in', 'mla_q_norm_g', 'mla_w_uq',
            'mla_kv_norm_g', 'mla_w_ukv', 'mla_w_o', 'conv_w', 'conv_b', 'conv_w_out', 's5_a_re', 's5_a_im', 's5_log_dt',
            's5_b_re', 's5_b_im', 's5_c_re', 's5_c_im', 's5_d', 's5_w_glu', 's5_b_glu', 's5_w_out', 'w_o', 'ln2_g', 'ln2_b',
            'ffn2_w_gate', 'ffn2_w_up', 'ffn2_w_down', 'ln3_g', 'ln3_b']


def _pad_to(flat, n):
    return jnp.concatenate([flat, jnp.zeros((n - flat.shape[0],), flat.dtype)])


def _shard_of(full, axis):
    if axis == 1:
        return _shard_cols(full)
    if axis == 'T':
        return full.T.reshape(N_SHARD, full.shape[1] // N_SHARD, full.shape[0])
    return full.reshape(N_SHARD, full.shape[0] // N_SHARD, full.shape[1])


_FFN_KEY = {'gate': 'wg', 'up': 'wu', 'down': 'wd'}


def _pad_rows(a, axis):
    k = a.shape[axis]
    extra = -k % 32
    if not extra:
        return a
    return jnp.pad(a, [(0, extra) if d == axis else (0, 0) for d in range(a.ndim)])


def _step(env):
    w = {n: env[n] for n in _WEIGHTS}
    mom = {n: env['m_' + n] for n in _WEIGHTS}
    var = {n: env['v_' + n] for n in _WEIGHTS}
    cidx = lax.axis_index("c").astype(jnp.int32).reshape(1)
    chip = 2 * lax.axis_index("x") + lax.axis_index("y")
    big_names = [n for n, _ in _BIG]
    nb = len(big_names)

    kept_t = [n for n, a in _BIG if a == 'T']
    own = {n: (jnp.swapaxes(w[n], 1, 2) if n in kept_t else w[n]) for n in big_names}
    srcs = [[_pad_rows(own[n][li].astype(BF16), 0) for n in big_names] for li in range(DEPTH)]
    gathered0, (conv_w_st, meta_st) = all_gather_shards(srcs[0], [w['conv_w'], w['meta']])
    lands = [lax.empty((N_SHARD,) + s.shape, BF16) for s in srcs[1]]
    g_send, g_recv, srcs_t, lands_t, token = split_start("gather1_start", srcs[1], lands, gathered0[0], _gather_copies, 4 * nb)

    def layer_weights(li, st):
        small = {n: w[n][li] for n in _REPL}
        small['conv_w'] = _nat_cols(conv_w_st[:, li])
        return compute_weights({n: a[:, :own[n].shape[1]] for n, a in zip(big_names, st)}, small)

    x2d = env['x'][0]
    lp = x2d.shape[0] + X0
    tabs = _rope_tables(lp)
    h = jnp.concatenate([jnp.zeros((PAD, D_MODEL), F32), _nat_cols(meta_st), x2d], axis=0) + token[0, 0]
    W0 = layer_weights(0, gathered0)
    h, hb, sv0 = layer_fwd(h, h.astype(BF16), W0, tabs, lp)
    lands1 = split_wait("gather1_wait", g_send, g_recv, srcs_t, lands_t, h, _gather_copies)
    W1 = layer_weights(1, gather_forward(lands1))
    h, hb, sv1 = layer_fwd(h, hb, W1, tabs, lp)
    tgt = jnp.concatenate([jnp.zeros((X0, D_MODEL), F32), env['loss_target'][0]], axis=0)
    dh, loss_part = loss_head(h, tgt, lp)
    loss = lax.psum(loss_part[0, 0], ("x", "y", "c"))

    def shards(G):
        full = reference_grads(G, ffn=False)

        def one(n, a):
            if n.startswith('ffn'):
                return G[_FFN_KEY[n.split('_')[-1]] + n[3]]
            if n == 'w_in':
                return full['w_in_t'].reshape(N_SHARD, D_IN // N_SHARD, D_MODEL)
            return _shard_of(full[n], a)

        return [_pad_rows(one(n, a), 1) for n, a in _BIG], full

    dh, G1 = layer_bwd(dh, sv1, W1, tabs, lp)
    gs1, full1 = shards(G1)
    p1 = rs_partials(gs1, [BF16] * nb, cidx, "b")
    q1 = [lax.dynamic_update_slice_in_dim(jnp.zeros_like(p), lax.dynamic_slice_in_dim(p, chip, 1, axis=0), chip, axis=0)
          for p in p1]
    s_send, s_recv, p1_t, q1_t, token2 = split_start("scatter1_start", p1, q1, dh, _scatter_copies, 3 * nb)
    dh, G0 = layer_bwd(dh + token2[0, 0], sv0, W0, tabs, lp)
    q1 = split_wait("scatter1_wait", s_send, s_recv, p1_t, q1_t, dh, _scatter_copies)
    gs0, full0 = shards(G0)
    full = [full0, full1]

    s_parts = [jnp.stack([full[li][n] for li in range(DEPTH)]).reshape(-1) for n in _REPL + ['conv_w']]
    s_parts.append(dh[PAD:X0].reshape(-1))
    s_sizes = [int(p.shape[0]) for p in s_parts]
    s_rows = -(-sum(s_sizes) // (16 * LANES)) * 16
    g_small = _pad_to(jnp.concatenate(s_parts), s_rows * LANES).reshape(1, s_rows, LANES)
    g_small = jnp.broadcast_to(g_small, (N_SHARD, s_rows, LANES))
    q0 = rs_chip_scatter(rs_partials(gs0 + [g_small], [BF16] * nb + [F32], cidx, "a"))
    red = rs_finish([[q0[i], q1[i]] for i in range(nb)] + [[q0[nb]]])
    f_small = red[-1].reshape(-1)

    grad = {n: r[:, :own[n].shape[1]] for n, r in zip(big_names, red[:nb])}
    off = 0
    for n, sz in zip(_REPL + ['conv_w', 'meta'], s_sizes):
        grad[n] = f_small[off:off + sz]
        off += sz
    for n in _REPL:
        grad[n] = grad[n].reshape(w[n].shape)
    cw = grad['conv_w'].reshape(DEPTH, 3, MIX)
    grad['conv_w'] = lax.dynamic_slice_in_dim(cw, chip * (MIX // N_SHARD), MIX // N_SHARD, axis=2)
    gm = grad['meta'].reshape(N_META, D_MODEL)
    grad['meta'] = lax.dynamic_slice_in_dim(gm, chip * (D_MODEL // N_SHARD), D_MODEL // N_SHARD, axis=1)

    delta, new_m, new_v = {}, {}, {}
    for n in _WEIGHTS:
        if n in kept_t:
            outs = adamw(own[n], grad[n], jnp.swapaxes(mom[n], 1, 2), jnp.swapaxes(var[n], 1, 2), "adamw_" + n)
            grad[n], delta[n], new_m[n], new_v[n] = [jnp.swapaxes(t, 1, 2) for t in [grad[n]] + list(outs)]
        else:
            delta[n], new_m[n], new_v[n] = adamw(w[n], grad[n], mom[n], var[n], "adamw_" + n)
    return (loss, dh[X0:][None], *[grad[n] for n in _WEIGHTS], *[delta[n] for n in _WEIGHTS],
            *[new_m[n] for n in _WEIGHTS], *[new_v[n] for n in _WEIGHTS])


def kernel(x, meta, ffn1_w_gate, ffn1_w_up, ffn1_w_down, ln1_g, ln1_b, w_in, mla_q_norm_g, mla_w_uq, mla_kv_norm_g, mla_w_ukv, mla_w_o, conv_w, conv_b, conv_w_out, s5_a_re, s5_a_im, s5_log_dt, s5_b_re, s5_b_im, s5_c_re, s5_c_im, s5_d, s5_w_glu, s5_b_glu, s5_w_out, w_o, ln2_g, ln2_b, ffn2_w_gate, ffn2_w_up, ffn2_w_down, ln3_g, ln3_b, loss_target, m_meta, m_ffn1_w_gate, m_ffn1_w_up, m_ffn1_w_down, m_ln1_g, m_ln1_b, m_w_in, m_mla_q_norm_g, m_mla_w_uq, m_mla_kv_norm_g, m_mla_w_ukv, m_mla_w_o, m_conv_w, m_conv_b, m_conv_w_out, m_s5_a_re, m_s5_a_im, m_s5_log_dt, m_s5_b_re, m_s5_b_im, m_s5_c_re, m_s5_c_im, m_s5_d, m_s5_w_glu, m_s5_b_glu, m_s5_w_out, m_w_o, m_ln2_g, m_ln2_b, m_ffn2_w_gate, m_ffn2_w_up, m_ffn2_w_down, m_ln3_g, m_ln3_b, v_meta, v_ffn1_w_gate, v_ffn1_w_up, v_ffn1_w_down, v_ln1_g, v_ln1_b, v_w_in, v_mla_q_norm_g, v_mla_w_uq, v_mla_kv_norm_g, v_mla_w_ukv, v_mla_w_o, v_conv_w, v_conv_b, v_conv_w_out, v_s5_a_re, v_s5_a_im, v_s5_log_dt, v_s5_b_re, v_s5_b_im, v_s5_c_re, v_s5_c_im, v_s5_d, v_s5_w_glu, v_s5_b_glu, v_s5_w_out, v_w_o, v_ln2_g, v_ln2_b, v_ffn2_w_gate, v_ffn2_w_up, v_ffn2_w_down, v_ln3_g, v_ln3_b):
    return _step(dict(locals()))
```

```python
import functools
import math

import numpy as np
import jax
import jax.numpy as jnp
from jax import lax
from jax.experimental import pallas as pl
from jax.experimental.pallas import tpu as pltpu

F32 = jnp.float32
BF16 = jnp.bfloat16

D_MODEL = 1024
DEPTH = 2
N_META = 16
PAD = 112
X0 = PAD + N_META
N_HEADS = 8
D_NOPE = 64
D_ROPE = 32
D_V = 64
Q_RANK = 384
KV_RANK = 256
MIX = 512
S5_GROUPS = 32
S5_GROUP = 16
S5_STATE = 64
S5_LANES = S5_GROUPS * S5_STATE
D_FF = 2816
N_SHARD = 4
FF_SHARD = D_FF // N_SHARD
D_IN = 5792
P_IN = 6144
ALPHA = (2.0 * DEPTH) ** 0.25
LN_EPS = 1e-5
RMS_EPS = 1e-6
ATT_SCALE = (D_NOPE + D_ROPE) ** -0.5
ROPE_BASE = 10000.0
ADAM_LR, ADAM_B1, ADAM_B2, ADAM_EPS, ADAM_WD, ADAM_STEP = 0.001, 0.9, 0.999, 1e-08, 0.01, 10
SCAN_CHUNK = 128
VMEM_LIMIT = 52 * 2 ** 20
WGRAD = BF16
MESH = pl.DeviceIdType.MESH


def _cparams(**kw):
    return pltpu.CompilerParams(vmem_limit_bytes=VMEM_LIMIT, **kw)


def _tile(n):
    if n <= 1088:
        return n
    for t in (1024, 544, 512, 272, 256, 128):
        if n % t == 0:
            return t
    return n


def _row_tile(lp):
    for t in (544, 272, 128):
        if lp % t == 0:
            return t
    return lp


def _sigmoid(x):
    return 1.0 / (1.0 + jnp.exp(-x))


_GELU_C = math.sqrt(2.0 / math.pi)


def _gelu(x):
    return 0.5 * x * (1.0 + jnp.tanh(_GELU_C * (x + 0.044715 * x * x * x)))


def _gelu_grad(x):
    t = jnp.tanh(_GELU_C * (x + 0.044715 * x * x * x))
    return 0.5 * (1.0 + t) + 0.5 * x * (1.0 - t * t) * _GELU_C * (1.0 + 3.0 * 0.044715 * x * x)


def _dot(a, b, ca, cb, precision=None):
    return lax.dot_general(a, b, (((ca,), (cb,)), ((), ())), preferred_element_type=F32, precision=precision)


def matmul(a, b, *, name, ta=False, tb=False, ab='n', bb='n', res=None, res_scale=1.0, scale=1.0, out_dtype=F32):
    if ta:
        _, K, M = a.shape
    else:
        _, M, K = a.shape
    if tb:
        _, N, K2 = b.shape
    else:
        _, K2, N = b.shape
    assert K == K2, (a.shape, b.shape)
    n_out = max(a.shape[0] if ab == 'o' else 1, b.shape[0] if bb == 'o' else 1)
    n_red = max(a.shape[0] if ab == 'r' else 1, b.shape[0] if bb == 'r' else 1)
    tm, tn = _tile(M), _tile(N)
    tk = K if K <= 2304 else _tile(K)
    nkt = K // tk
    n_steps = n_red * nkt

    def bsel(mode, o, r):
        if mode == 'o':
            return o
        if mode == 'r':
            return r // nkt if nkt > 1 else r
        return 0

    def ksel(r):
        if nkt == 1:
            return 0
        return r % nkt if n_red > 1 else r

    a_map = (lambda o, i, j, r: (bsel(ab, o, r), ksel(r), i)) if ta else (lambda o, i, j, r: (bsel(ab, o, r), i, ksel(r)))
    b_map = (lambda o, i, j, r: (bsel(bb, o, r), j, ksel(r))) if tb else (lambda o, i, j, r: (bsel(bb, o, r), ksel(r), j))
    o_map = lambda o, i, j, r: (o, i, j)
    in_specs = [pl.BlockSpec((None, tk, tm) if ta else (None, tm, tk), a_map),
                pl.BlockSpec((None, tn, tk) if tb else (None, tk, tn), b_map)]
    operands = [a, b]
    if res is not None:
        in_specs.append(pl.BlockSpec((None, tm, tn), o_map))
        operands.append(res)
    has_res = res is not None

    def body(*refs):
        a_ref, b_ref = refs[0], refs[1]
        res_ref = refs[2] if has_res else None
        o_ref = refs[3] if has_res else refs[2]
        part = _dot(a_ref[...].astype(BF16), b_ref[...].astype(BF16), 0 if ta else 1, 1 if tb else 0)

        def finish(acc):
            v = acc if scale == 1.0 else acc * scale
            if has_res:
                v = v + res_scale * res_ref[...].astype(F32)
            o_ref[...] = v.astype(o_ref.dtype)

        if n_steps == 1:
            finish(part)
        else:
            acc_ref = refs[-1]
            r = pl.program_id(3)

            @pl.when(r == 0)
            def _():
                acc_ref[...] = part

            @pl.when(r > 0)
            def _():
                acc_ref[...] += part

            @pl.when(r == n_steps - 1)
            def _():
                finish(acc_ref[...])

    return pl.pallas_call(
        body, name=name,
        grid=(n_out, M // tm, N // tn, n_steps),
        in_specs=in_specs,
        out_specs=pl.BlockSpec((None, tm, tn), o_map),
        out_shape=jax.ShapeDtypeStruct((n_out, M, N), out_dtype),
        scratch_shapes=[pltpu.VMEM((tm, tn), F32)] if n_steps > 1 else [],
        compiler_params=_cparams(),
    )(*operands)


def rowwise(fn, rows, pars, outs, accs=(), *, name, lp):
    tm = _row_tile(lp)
    n_rows, n_pars, n_outs, n_accs = len(rows), len(pars), len(outs), len(accs)
    in_specs = [pl.BlockSpec((tm, w), functools.partial(lambda i, cb: (i, cb), cb=cb)) for _, w, cb in rows]
    in_specs += [pl.BlockSpec(p.shape, functools.partial(lambda i, nd: (0,) * nd, nd=p.ndim)) for p in pars]
    out_specs = [pl.BlockSpec((tm, w), lambda i: (i, 0)) for w, _ in outs]
    out_specs += [pl.BlockSpec(s, functools.partial(lambda i, nd: (0,) * nd, nd=len(s))) for s, _ in accs]
    out_shape = [jax.ShapeDtypeStruct((lp, w), dt) for w, dt in outs]
    out_shape += [jax.ShapeDtypeStruct(s, dt) for s, dt in accs]

    def body(*refs):
        i = pl.program_id(0)
        rv = [r[...] for r in refs[:n_rows]]
        pv = [r[...] for r in refs[n_rows:n_rows + n_pars]]
        o_refs = refs[n_rows + n_pars:n_rows + n_pars + n_outs]
        a_refs = refs[n_rows + n_pars + n_outs:]
        ov, av = fn(i * tm, rv, pv)
        for r, v in zip(o_refs, ov):
            r[...] = v.astype(r.dtype)
        if n_accs:
            @pl.when(i == 0)
            def _():
                for r, v in zip(a_refs, av):
                    r[...] = v.astype(r.dtype)

            @pl.when(i > 0)
            def _():
                for r, v in zip(a_refs, av):
                    r[...] += v.astype(r.dtype)

    res = pl.pallas_call(
        body, name=name, grid=(lp // tm,), in_specs=in_specs, out_specs=out_specs, out_shape=out_shape,
        compiler_params=_cparams(),
    )(*[r[0] for r in rows], *pars)
    return res


def _row_mask(row0, shape):
    return (row0 + lax.broadcasted_iota(jnp.int32, shape, 0)) >= PAD


def ffn_up(hb, wg, wu, lp):
    tm = _row_tile(lp)

    def body(h_ref, wg_ref, wu_ref, ab_ref, hid_ref):
        h = h_ref[...]
        a = _dot(h, wg_ref[...], 1, 1)
        b = _dot(h, wu_ref[...], 1, 1)
        ab_ref[0] = a.astype(BF16)
        ab_ref[1] = b.astype(BF16)
        hid_ref[...] = (a * _sigmoid(a) * b).astype(BF16)

    wspec = pl.BlockSpec((None, FF_SHARD, D_MODEL), lambda j, i: (j, 0, 0))
    return pl.pallas_call(
        body, name="ffn_up", grid=(N_SHARD, lp // tm),
        in_specs=[pl.BlockSpec((tm, D_MODEL), lambda j, i: (i, 0)), wspec, wspec],
        out_specs=[pl.BlockSpec((None, 2, tm, FF_SHARD), lambda j, i: (j, 0, i, 0)),
                   pl.BlockSpec((None, tm, FF_SHARD), lambda j, i: (j, i, 0))],
        out_shape=[jax.ShapeDtypeStruct((N_SHARD, 2, lp, FF_SHARD), BF16),
                   jax.ShapeDtypeStruct((N_SHARD, lp, FF_SHARD), BF16)],
        compiler_params=_cparams(),
    )(hb, wg, wu)


def _layer_norm(z, g, b):
    mu = jnp.mean(z, axis=-1, keepdims=True)
    zc = z - mu
    var = jnp.mean(zc * zc, axis=-1, keepdims=True)
    return zc * lax.rsqrt(var + LN_EPS) * g + b


def mm_res_ln(a, w, res, g, b, *, scale, name, lp):
    n_red, _, K = a.shape
    tm = _row_tile(lp)

    def body(a_ref, w_ref, res_ref, g_ref, b_ref, z_ref, h_ref, hb_ref, acc_ref):
        r = pl.program_id(1)
        part = _dot(a_ref[...].astype(BF16), w_ref[...], 1, 0)

        @pl.when(r == 0)
        def _():
            acc_ref[...] = part

        @pl.when(r > 0)
        def _():
            acc_ref[...] += part

        @pl.when(r == n_red - 1)
        def _():
            z = ALPHA * res_ref[...] + scale * acc_ref[...]
            z_ref[...] = z
            hn = _layer_norm(z, g_ref[...], b_ref[...])
            h_ref[...] = hn
            hb_ref[...] = hn.astype(BF16)

    row = pl.BlockSpec((tm, D_MODEL), lambda i, r: (i, 0))
    par = pl.BlockSpec((1, D_MODEL), lambda i, r: (0, 0))
    return pl.pallas_call(
        body, name=name, grid=(lp // tm, n_red),
        in_specs=[pl.BlockSpec((None, tm, K), lambda i, r: (r, i, 0)),
                  pl.BlockSpec((None, K, D_MODEL), lambda i, r: (r, 0, 0)), row, par, par],
        out_specs=[row, row, row],
        out_shape=[jax.ShapeDtypeStruct((lp, D_MODEL), F32), jax.ShapeDtypeStruct((lp, D_MODEL), F32),
                   jax.ShapeDtypeStruct((lp, D_MODEL), BF16)],
        scratch_shapes=[pltpu.VMEM((tm, D_MODEL), F32)],
        compiler_params=_cparams(),
    )(a, w, res, g, b)


def ln_bwd(dh, z, g, *, fscale, name, lp):
    def fn(row0, rv, pv):
        dh_, z_ = rv
        g_, = pv
        mu = jnp.mean(z_, axis=-1, keepdims=True)
        zc = z_ - mu
        rstd = lax.rsqrt(jnp.mean(zc * zc, axis=-1, keepdims=True) + LN_EPS)
        xh = zc * rstd
        dxh = dh_ * g_
        m1 = jnp.mean(dxh, axis=-1, keepdims=True)
        m2 = jnp.mean(dxh * xh, axis=-1, keepdims=True)
        dz = rstd * (dxh - m1 - xh * m2)
        return ((dz, fscale * dz),
                (jnp.sum(dh_ * xh, axis=0, keepdims=True), jnp.sum(dh_, axis=0, keepdims=True)))

    return rowwise(fn, [(dh, D_MODEL, 0), (z, D_MODEL, 0)], [g], [(D_MODEL, F32), (D_MODEL, BF16)],
                   [((1, D_MODEL), F32), ((1, D_MODEL), F32)], name=name, lp=lp)


def ffn_down_bwd(dfb, wd, ab, lp):
    tm = _row_tile(lp)

    def body(df_ref, w_ref, ab_ref, da_ref, db_ref):
        dhid = _dot(df_ref[...], w_ref[...], 1, 1)
        a = ab_ref[0].astype(F32)
        b = ab_ref[1].astype(F32)
        sg = _sigmoid(a)
        da_ref[...] = (dhid * b * (sg * (1.0 + a * (1.0 - sg)))).astype(BF16)
        db_ref[...] = (dhid * (a * sg)).astype(BF16)

    ospec = pl.BlockSpec((None, tm, FF_SHARD), lambda j, i: (j, i, 0))
    return pl.pallas_call(
        body, name="ffn_down_bwd", grid=(N_SHARD, lp // tm),
        in_specs=[pl.BlockSpec((tm, D_MODEL), lambda j, i: (i, 0)),
                  pl.BlockSpec((None, FF_SHARD, D_MODEL), lambda j, i: (j, 0, 0)),
                  pl.BlockSpec((None, 2, tm, FF_SHARD), lambda j, i: (j, 0, i, 0))],
        out_specs=[ospec, ospec],
        out_shape=[jax.ShapeDtypeStruct((N_SHARD, lp, FF_SHARD), BF16)] * 2,
        compiler_params=_cparams(),
    )(dfb, wd, ab)


def ffn_dx(da, db, wg, wu, dz, lp):
    tm = _row_tile(lp)

    def body(da_ref, db_ref, wg_ref, wu_ref, dz_ref, o_ref, acc_ref):
        j = pl.program_id(1)
        part = _dot(da_ref[...], wg_ref[...], 1, 0) + _dot(db_ref[...], wu_ref[...], 1, 0)

        @pl.when(j == 0)
        def _():
            acc_ref[...] = part

        @pl.when(j > 0)
        def _():
            acc_ref[...] += part

        @pl.when(j == N_SHARD - 1)
        def _():
            o_ref[...] = acc_ref[...] + ALPHA * dz_ref[...]

    aspec = pl.BlockSpec((None, tm, FF_SHARD), lambda i, j: (j, i, 0))
    wspec = pl.BlockSpec((None, FF_SHARD, D_MODEL), lambda i, j: (j, 0, 0))
    row = pl.BlockSpec((tm, D_MODEL), lambda i, j: (i, 0))
    return pl.pallas_call(
        body, name="ffn_dx", grid=(lp // tm, N_SHARD), in_specs=[aspec, aspec, wspec, wspec, row], out_specs=row,
        out_shape=jax.ShapeDtypeStruct((lp, D_MODEL), F32), scratch_shapes=[pltpu.VMEM((tm, D_MODEL), F32)],
        compiler_params=_cparams(),
    )(da, db, wg, wu, dz)


def ffn_fwd(h, hb, wg, wu, wd, g, b, lp):
    ab, hid = ffn_up(hb, wg, wu, lp)
    z, hn, hnb = mm_res_ln(hid, wd, h, g, b, scale=0.5, name="ffn_down_ln", lp=lp)
    return hn, hnb, dict(hb=hb, ab=ab, hid=hid, z=z)


def ffn_bwd(dh, sv, wg, wu, wd, g, lp):
    dz, dfb, dg, db = ln_bwd(dh, sv['z'], g, fscale=0.5, name="ffn_ln_bwd", lp=lp)
    da, dbb = ffn_down_bwd(dfb, wd, sv['ab'], lp)
    d_wd = matmul(sv['hid'], dfb[None], ta=True, ab='o', out_dtype=WGRAD, name="ffn_dwd")
    d_wg = matmul(da, sv['hb'][None], ta=True, ab='o', out_dtype=WGRAD, name="ffn_dwg")
    d_wu = matmul(dbb, sv['hb'][None], ta=True, ab='o', out_dtype=WGRAD, name="ffn_dwu")
    dh_in = ffn_dx(da, dbb, wg, wu, dz, lp)
    return dh_in, dict(wg=d_wg, wu=d_wu, wd=d_wd, ln_g=dg, ln_b=db)


def _rope_tables(lp):
    pos = np.arange(lp, dtype=np.float32) - PAD
    inv = ROPE_BASE ** (-np.arange(0, D_ROPE, 2, dtype=np.float32) / D_ROPE)
    ang = pos[:, None] * inv[None, :]
    cos = np.concatenate([np.cos(ang), np.cos(ang)], axis=1).astype(np.float32)
    sin = np.concatenate([np.sin(ang), np.sin(ang)], axis=1).astype(np.float32)
    rot = np.zeros((D_ROPE, D_ROPE), np.float32)
    half = D_ROPE // 2
    for j in range(half):
        rot[j + half, j] = -1.0
        rot[j, j + half] = 1.0
    return jnp.asarray(cos), jnp.asarray(sin), jnp.asarray(rot)


def _rot(x, rot):
    return _dot(x, rot, 1, 0, precision=lax.Precision.HIGHEST)


def _rms(x, g):
    r = lax.rsqrt(jnp.mean(x * x, axis=-1, keepdims=True) + RMS_EPS)
    return x * r * g


def mla_prep(proj, cos, sin, rot, qg, kvg, lp):
    def fn(row0, rv, pv):
        cq, krb, ckv, c, s = rv
        qg_, kvg_, rot_ = pv
        kr = krb[:, :D_ROPE]
        return ((_rms(cq, qg_), _rms(ckv, kvg_), kr * c + _rot(kr, rot_) * s), ())

    return rowwise(fn, [(proj, Q_RANK, 0), (proj, 128, 3), (proj, KV_RANK, 2), (cos, D_ROPE, 0), (sin, D_ROPE, 0)],
                   [qg, kvg, rot], [(Q_RANK, BF16), (KV_RANK, BF16), (D_ROPE, BF16)], name="mla_prep", lp=lp)


def mla_heads(cqn, ckvn, cos, sin, rot, wqn, wqr, wkn, wv, lp):
    tm = _row_tile(lp)

    def body(cq_ref, ckv_ref, c_ref, s_ref, rot_ref, wqn_ref, wqr_ref, wkn_ref, wv_ref, qn_ref, qr_ref, kn_ref, v_ref):
        cq = cq_ref[...]
        ckv = ckv_ref[...]
        qn_ref[...] = _dot(cq, wqn_ref[...], 1, 0).astype(BF16)
        qr = _dot(cq, wqr_ref[...], 1, 0)
        qr_ref[...] = (qr * c_ref[...] + _rot(qr, rot_ref[...]) * s_ref[...]).astype(BF16)
        kn_ref[...] = _dot(ckv, wkn_ref[...], 1, 0).astype(BF16)
        v_ref[...] = _dot(ckv, wv_ref[...], 1, 0).astype(BF16)

    def row(w):
        return pl.BlockSpec((tm, w), lambda h, i: (i, 0))

    def wspec(k, n):
        return pl.BlockSpec((None, k, n), lambda h, i: (h, 0, 0))

    def ospec(n):
        return pl.BlockSpec((None, tm, n), lambda h, i: (h, i, 0))

    return pl.pallas_call(
        body, name="mla_heads", grid=(N_HEADS, lp // tm),
        in_specs=[row(Q_RANK), row(KV_RANK), row(D_ROPE), row(D_ROPE),
                  pl.BlockSpec((D_ROPE, D_ROPE), lambda h, i: (0, 0)),
                  wspec(Q_RANK, D_NOPE), wspec(Q_RANK, D_ROPE), wspec(KV_RANK, D_NOPE), wspec(KV_RANK, D_V)],
        out_specs=[ospec(D_NOPE), ospec(D_ROPE), ospec(D_NOPE), ospec(D_V)],
        out_shape=[jax.ShapeDtypeStruct((N_HEADS, lp, D_NOPE), BF16), jax.ShapeDtypeStruct((N_HEADS, lp, D_ROPE), BF16),
                   jax.ShapeDtypeStruct((N_HEADS, lp, D_NOPE), BF16), jax.ShapeDtypeStruct((N_HEADS, lp, D_V), BF16)],
        compiler_params=_cparams(),
    )(cqn, ckvn, cos, sin, rot, wqn, wqr, wkn, wv)


def _att_probs(qn, qr, kn, kr, row0, tq, lp):
    s = (_dot(qn, kn, 1, 1) + _dot(qr, kr, 1, 1)) * ATT_SCALE
    qi = row0 + lax.broadcasted_iota(jnp.int32, (tq, lp), 0)
    ki = lax.broadcasted_iota(jnp.int32, (tq, lp), 1)
    s = jnp.where((ki <= qi) & (ki >= PAD), s, -1e30)
    p = jnp.exp(s - jnp.max(s, axis=-1, keepdims=True))
    return p / jnp.sum(p, axis=-1, keepdims=True)


def _att_specs(tq, lp):
    def qspec(n):
        return pl.BlockSpec((None, tq, n), lambda h, i: (h, i, 0))

    def kspec(n):
        return pl.BlockSpec((None, lp, n), lambda h, i: (h, 0, 0))

    return qspec, kspec, pl.BlockSpec((lp, D_ROPE), lambda h, i: (0, 0))


def _att_tile(lp):
    return 272 if lp % 272 == 0 else 128


def attn_fwd(qn, qr, kn, v, kr, lp):
    tq = _att_tile(lp)
    qspec, kspec, krspec = _att_specs(tq, lp)

    def body(qn_ref, qr_ref, kn_ref, v_ref, kr_ref, o_ref):
        i = pl.program_id(1)
        for k in range(lp // tq):
            @pl.when(i == k)
            def _(k=k):
                ke = (k + 1) * tq
                p = _att_probs(qn_ref[...], qr_ref[...], kn_ref[0:ke, :], kr_ref[0:ke, :], k * tq, tq, ke)
                o_ref[...] = _dot(p.astype(BF16), v_ref[0:ke, :], 1, 0).astype(BF16)

    return pl.pallas_call(
        body, name="attn_fwd", grid=(N_HEADS, lp // tq),
        in_specs=[qspec(D_NOPE), qspec(D_ROPE), kspec(D_NOPE), kspec(D_V), krspec],
        out_specs=qspec(D_V), out_shape=jax.ShapeDtypeStruct((N_HEADS, lp, D_V), BF16),
        compiler_params=_cparams(),
    )(qn, qr, kn, v, kr)


def attn_bwd(qn, qr, kn, v, kr, do, lp):
    tq = _att_tile(lp)
    qspec, kspec, krspec = _att_specs(tq, lp)

    def body(qn_ref, qr_ref, kn_ref, v_ref, kr_ref, do_ref, dqn_ref, dqr_ref, dkn_ref, dv_ref, dkr_ref):
        h, i = pl.program_id(0), pl.program_id(1)

        @pl.when(i == 0)
        def _():
            dkn_ref[...] = jnp.zeros_like(dkn_ref)
            dv_ref[...] = jnp.zeros_like(dv_ref)

        @pl.when((i == 0) & (h == 0))
        def _():
            dkr_ref[...] = jnp.zeros_like(dkr_ref)

        for k in range(lp // tq):
            @pl.when(i == k)
            def _(k=k):
                ke = (k + 1) * tq
                qn_, qr_, do_ = qn_ref[...], qr_ref[...], do_ref[...]
                kn_, v_, kr_ = kn_ref[0:ke, :], v_ref[0:ke, :], kr_ref[0:ke, :]
                p = _att_probs(qn_, qr_, kn_, kr_, k * tq, tq, ke)
                dp = _dot(do_, v_, 1, 1)
                delta = jnp.sum(p * dp, axis=-1, keepdims=True)
                ds = (p * (dp - delta) * ATT_SCALE).astype(BF16)
                dqn_ref[...] = _dot(ds, kn_, 1, 0).astype(BF16)
                dqr_ref[...] = _dot(ds, kr_, 1, 0)
                dkn_ref[0:ke, :] += _dot(ds, qn_, 0, 0)
                dv_ref[0:ke, :] += _dot(p.astype(BF16), do_, 0, 0)
                dkr_ref[0:ke, :] += _dot(ds, qr_, 0, 0)

    return pl.pallas_call(
        body, name="attn_bwd", grid=(N_HEADS, lp // tq),
        in_specs=[qspec(D_NOPE), qspec(D_ROPE), kspec(D_NOPE), kspec(D_V), krspec, qspec(D_V)],
        out_specs=[qspec(D_NOPE), qspec(D_ROPE), kspec(D_NOPE), kspec(D_V), krspec],
        out_shape=[jax.ShapeDtypeStruct((N_HEADS, lp, D_NOPE), BF16), jax.ShapeDtypeStruct((N_HEADS, lp, D_ROPE), F32),
                   jax.ShapeDtypeStruct((N_HEADS, lp, D_NOPE), F32), jax.ShapeDtypeStruct((N_HEADS, lp, D_V), F32),
                   jax.ShapeDtypeStruct((lp, D_ROPE), F32)],
        compiler_params=_cparams(),
    )(qn, qr, kn, v, kr, do)


def mla_heads_bwd(dqn, dqr, dkn, dv, cos, sin, rot, wqn, wqr, wkn, wv, lp):
    tm = _row_tile(lp)

    def body(dqn_ref, dqr_ref, dkn_ref, dv_ref, c_ref, s_ref, rot_ref, wqn_ref, wqr_ref, wkn_ref, wv_ref,
             dcq_ref, dckv_ref, dqrp_ref):
        h = pl.program_id(1)
        dqr_ = dqr_ref[...]
        dqrp = (dqr_ * c_ref[...] - _rot(dqr_ * s_ref[...], rot_ref[...])).astype(BF16)
        dqrp_ref[...] = dqrp
        dcq = _dot(dqn_ref[...], wqn_ref[...], 1, 1) + _dot(dqrp, wqr_ref[...], 1, 1)
        dckv = _dot(dkn_ref[...].astype(BF16), wkn_ref[...], 1, 1) + _dot(dv_ref[...].astype(BF16), wv_ref[...], 1, 1)

        @pl.when(h == 0)
        def _():
            dcq_ref[...] = dcq
            dckv_ref[...] = dckv

        @pl.when(h > 0)
        def _():
            dcq_ref[...] += dcq
            dckv_ref[...] += dckv

    def hspec(n):
        return pl.BlockSpec((None, tm, n), lambda i, h: (h, i, 0))

    def row(w):
        return pl.BlockSpec((tm, w), lambda i, h: (i, 0))

    def wspec(k, n):
        return pl.BlockSpec((None, k, n), lambda i, h: (h, 0, 0))

    return pl.pallas_call(
        body, name="mla_heads_bwd", grid=(lp // tm, N_HEADS),
        in_specs=[hspec(D_NOPE), hspec(D_ROPE), hspec(D_NOPE), hspec(D_V), row(D_ROPE), row(D_ROPE),
                  pl.BlockSpec((D_ROPE, D_ROPE), lambda i, h: (0, 0)),
                  wspec(Q_RANK, D_NOPE), wspec(Q_RANK, D_ROPE), wspec(KV_RANK, D_NOPE), wspec(KV_RANK, D_V)],
        out_specs=[row(Q_RANK), row(KV_RANK), hspec(D_ROPE)],
        out_shape=[jax.ShapeDtypeStruct((lp, Q_RANK), F32), jax.ShapeDtypeStruct((lp, KV_RANK), F32),
                   jax.ShapeDtypeStruct((N_HEADS, lp, D_ROPE), BF16)],
        compiler_params=_cparams(),
    )(dqn, dqr, dkn, dv, cos, sin, rot, wqn, wqr, wkn, wv)


def _rms_bwd(dy, x, g):
    r = lax.rsqrt(jnp.mean(x * x, axis=-1, keepdims=True) + RMS_EPS)
    n = x * r
    dn = dy * g
    dx = r * (dn - n * jnp.mean(dn * n, axis=-1, keepdims=True))
    return dx, jnp.sum(dy * n, axis=0, keepdims=True)


def mla_prep_bwd(dcq, dckv, dkr, proj, cos, sin, rot, qg, kvg, lp):
    def fn(row0, rv, pv):
        dcq_, dckv_, dkr_, cq, ckv, c, s = rv
        qg_, kvg_, rot_ = pv
        dxq, dgq = _rms_bwd(dcq_, cq, qg_)
        dxkv, dgkv = _rms_bwd(dckv_, ckv, kvg_)
        dkr_raw = dkr_ * c - _rot(dkr_ * s, rot_)
        return ((dxq, dxkv, dkr_raw), (dgq, dgkv))

    return rowwise(fn, [(dcq, Q_RANK, 0), (dckv, KV_RANK, 0), (dkr, D_ROPE, 0), (proj, Q_RANK, 0), (proj, KV_RANK, 2),
                        (cos, D_ROPE, 0), (sin, D_ROPE, 0)], [qg, kvg, rot],
                   [(Q_RANK, BF16), (KV_RANK, BF16), (D_ROPE, BF16)], [((1, Q_RANK), F32), ((1, KV_RANK), F32)],
                   name="mla_prep_bwd", lp=lp)


def _shift_down(x, d, rows):
    return jnp.where(rows >= d, pltpu.roll(x, d, 0), 0.0)


def _shift_up(x, d, rows, n):
    return jnp.where(rows < n - d, pltpu.roll(x, n - d, 0), 0.0)


_CONV_W = 128
_XB, _BG, _CG = 1024 // _CONV_W, 1536 // _CONV_W, 2048 // _CONV_W


def _conv_specs(lp):
    def pspec(base):
        return pl.BlockSpec((lp, _CONV_W), functools.partial(lambda c, base: (0, base + c), base=base))

    col = pl.BlockSpec((lp, _CONV_W), lambda c: (0, c))
    wspec = pl.BlockSpec((3, _CONV_W), lambda c: (0, c))
    bspec = pl.BlockSpec((1, _CONV_W), lambda c: (0, c))
    return pspec, col, wspec, bspec


def _conv_core(xbar, cg, w, bias, lp):
    rows = lax.broadcasted_iota(jnp.int32, (lp, _CONV_W), 0)
    u = jnp.where(rows >= PAD, cg * xbar, 0.0)
    u1 = _shift_down(u, 1, rows)
    u2 = _shift_down(u, 2, rows)
    y = bias + w[0:1] * u2 + w[1:2] * u1 + w[2:3] * u
    return rows, u, u1, u2, y


def conv_fwd(proj, w, bias, lp):
    pspec, col, wspec, bspec = _conv_specs(lp)

    def body(x_ref, b_ref, c_ref, w_ref, bias_ref, v_ref):
        _, _, _, _, y = _conv_core(x_ref[...], c_ref[...], w_ref[...], bias_ref[...], lp)
        v_ref[...] = (b_ref[...] * y).astype(BF16)

    return pl.pallas_call(
        body, name="conv_fwd", grid=(MIX // _CONV_W,),
        in_specs=[pspec(_XB), pspec(_BG), pspec(_CG), wspec, bspec], out_specs=col,
        out_shape=jax.ShapeDtypeStruct((lp, MIX), BF16), compiler_params=_cparams(),
    )(proj, proj, proj, w, bias)


def conv_bwd(dv, proj, w, bias, lp):
    pspec, col, wspec, bspec = _conv_specs(lp)

    def body(dv_ref, x_ref, b_ref, c_ref, w_ref, bias_ref, dx_ref, db_ref, dc_ref, dw_ref, dbias_ref):
        xbar, cg, w_ = x_ref[...], c_ref[...], w_ref[...]
        rows, u, u1, u2, y = _conv_core(xbar, cg, w_, bias_ref[...], lp)
        dv_ = dv_ref[...]
        db_ref[...] = (dv_ * y).astype(BF16)
        dy = dv_ * b_ref[...]
        dbias_ref[...] = jnp.sum(dy, axis=0, keepdims=True)
        dw_ref[0:1, :] = jnp.sum(dy * u2, axis=0, keepdims=True)
        dw_ref[1:2, :] = jnp.sum(dy * u1, axis=0, keepdims=True)
        dw_ref[2:3, :] = jnp.sum(dy * u, axis=0, keepdims=True)
        du = w_[2:3] * dy + w_[1:2] * _shift_up(dy, 1, rows, lp) + w_[0:1] * _shift_up(dy, 2, rows, lp)
        du = jnp.where(rows >= PAD, du, 0.0)
        dc_ref[...] = (du * xbar).astype(BF16)
        dx_ref[...] = (du * cg).astype(BF16)

    return pl.pallas_call(
        body, name="conv_bwd", grid=(MIX // _CONV_W,),
        in_specs=[col, pspec(_XB), pspec(_BG), pspec(_CG), wspec, bspec],
        out_specs=[col, col, col, wspec, bspec],
        out_shape=[jax.ShapeDtypeStruct((lp, MIX), BF16)] * 3 + [jax.ShapeDtypeStruct((3, MIX), F32),
                                                                jax.ShapeDtypeStruct((1, MIX), F32)],
        compiler_params=_cparams(),
    )(dv, proj, proj, proj, w, bias)


def _s5_disc(a_re, a_im, log_dt, b_re, b_im):
    dt = jnp.exp(log_dt)
    mag = jnp.exp(dt * a_re)
    ab_re, ab_im = mag * jnp.cos(dt * a_im), mag * jnp.sin(dt * a_im)
    den = a_re * a_re + a_im * a_im
    nr, ni = ab_re - 1.0, ab_im
    coef_re = (nr * a_re + ni * a_im) / den
    coef_im = (ni * a_re - nr * a_im) / den
    return ab_re, ab_im, coef_re * b_re - coef_im * b_im, coef_re * b_im + coef_im * b_re


_S5_ROWS = S5_GROUPS * S5_GROUP


def s5_prep(a_re, a_im, log_dt, b_re, b_im):
    def body(ar, ai, ld, br, bi, o0, o1, o2, o3):
        for o, v in zip((o0, o1, o2, o3), _s5_disc(ar[...], ai[...], ld[...], br[...], bi[...])):
            o[...] = v

    return pl.pallas_call(body, name="s5_prep",
                          out_shape=[jax.ShapeDtypeStruct((_S5_ROWS, S5_STATE), F32)] * 4)(a_re, a_im, log_dt, b_re, b_im)


def s5_prep_bwd(a_re, a_im, log_dt, b_re, b_im, d_ab_re, d_ab_im, d_bb_re, d_bb_im, sel):
    def body(ar, ai, ld, br, bi, g0, g1, g2, g3, sel_ref, da_re, da_im, dld, dbr, dbi):
        _, vjp = jax.vjp(_s5_disc, ar[...], ai[...], ld[...], br[...], bi[...])
        c_ar, c_ai, c_ld, c_br, c_bi = vjp((g0[...], g1[...], g2[...], g3[...]))
        s = sel_ref[...]
        hi = lax.Precision.HIGHEST
        da_re[...] = _dot(s, c_ar, 1, 0, precision=hi)
        da_im[...] = _dot(s, c_ai, 1, 0, precision=hi)
        dld[...] = jnp.sum(_dot(s, c_ld, 1, 0, precision=hi), axis=-1, keepdims=True)
        dbr[...] = c_br
        dbi[...] = c_bi

    g = jax.ShapeDtypeStruct((S5_GROUPS, S5_STATE), F32)
    full = jax.ShapeDtypeStruct((_S5_ROWS, S5_STATE), F32)
    return pl.pallas_call(body, name="s5_prep_bwd",
                          out_shape=[g, g, jax.ShapeDtypeStruct((S5_GROUPS, 1), F32), full, full],
                          )(a_re, a_im, log_dt, b_re, b_im, d_ab_re, d_ab_im, d_bb_re, d_bb_im, sel)


_SCAN_W = 128
_SCAN_STEPS = int(math.log2(SCAN_CHUNK))


def _cmul(ar, ai, br, bi):
    return ar * br - ai * bi, ar * bi + ai * br


def _scan_powers(ar, ai, reverse):
    pw = [(ar, ai)]
    for _ in range(_SCAN_STEPS):
        pw.append(_cmul(*pw[-1], *pw[-1]))
    rows = lax.broadcasted_iota(jnp.int32, (SCAN_CHUNK, ar.shape[-1]), 0)
    tr = jnp.broadcast_to(ar, rows.shape)
    ti = jnp.broadcast_to(ai, rows.shape)
    for k in range(_SCAN_STEPS):
        d = 2 ** k
        if reverse:
            live = rows < SCAN_CHUNK - d
            mr, mi = _cmul(tr, ti, _shift_up(tr, d, rows, SCAN_CHUNK), _shift_up(ti, d, rows, SCAN_CHUNK))
        else:
            live = rows >= d
            mr, mi = _cmul(tr, ti, _shift_down(tr, d, rows), _shift_down(ti, d, rows))
        tr = jnp.where(live, mr, tr)
        ti = jnp.where(live, mi, ti)
    return pw, rows, tr, ti


def s5_scan(bu, ab_re, ab_im, lp):
    n_chunks = lp // SCAN_CHUNK

    def body(bu_ref, ar_ref, ai_ref, s_ref):
        ar, ai = ar_ref[...], ai_ref[...]
        pw, rows, tr, ti = _scan_powers(ar, ai, False)

        def chunk(ci, carry):
            cr, cim = carry
            r0 = pl.multiple_of(ci * SCAN_CHUNK, SCAN_CHUNK)
            xr = bu_ref[0, pl.ds(r0, SCAN_CHUNK), :]
            xi = bu_ref[1, pl.ds(r0, SCAN_CHUNK), :]
            for k in range(_SCAN_STEPS):
                d = 2 ** k
                mr, mi = _cmul(pw[k][0], pw[k][1], _shift_down(xr, d, rows), _shift_down(xi, d, rows))
                xr, xi = xr + mr, xi + mi
            mr, mi = _cmul(tr, ti, cr, cim)
            xr, xi = xr + mr, xi + mi
            s_ref[0, pl.ds(r0, SCAN_CHUNK), :] = xr
            s_ref[1, pl.ds(r0, SCAN_CHUNK), :] = xi
            return xr[SCAN_CHUNK - 1:SCAN_CHUNK, :], xi[SCAN_CHUNK - 1:SCAN_CHUNK, :]

        zero = jnp.zeros((1, _SCAN_W), F32)
        lax.fori_loop(0, n_chunks, chunk, (zero, zero))

    spec = pl.BlockSpec((2, lp, _SCAN_W), lambda c: (0, 0, c))
    aspec = pl.BlockSpec((1, _SCAN_W), lambda c: (0, c))
    return pl.pallas_call(
        body, name="s5_scan", grid=(S5_LANES // _SCAN_W,), in_specs=[spec, aspec, aspec], out_specs=spec,
        out_shape=jax.ShapeDtypeStruct((2, lp, S5_LANES), F32), compiler_params=_cparams(),
    )(bu, ab_re, ab_im)


def s5_scan_bwd(ds, s, ab_re, ab_im, lp):
    n_chunks = lp // SCAN_CHUNK

    def body(ds_ref, s_ref, ar_ref, ai_ref, g_ref, da_ref):
        ar, ai = ar_ref[...], -ai_ref[...]
        pw, rows, tr, ti = _scan_powers(ar, ai, True)

        def chunk(k, carry):
            cr, cim, dar, dai = carry
            ci = n_chunks - 1 - k
            r0 = pl.multiple_of(ci * SCAN_CHUNK, SCAN_CHUNK)
            xr = ds_ref[0, pl.ds(r0, SCAN_CHUNK), :]
            xi = ds_ref[1, pl.ds(r0, SCAN_CHUNK), :]
            for j in range(_SCAN_STEPS):
                d = 2 ** j
                mr, mi = _cmul(pw[j][0], pw[j][1], _shift_up(xr, d, rows, SCAN_CHUNK), _shift_up(xi, d, rows, SCAN_CHUNK))
                xr, xi = xr + mr, xi + mi
            mr, mi = _cmul(tr, ti, cr, cim)
            xr, xi = xr + mr, xi + mi
            g_ref[0, pl.ds(r0, SCAN_CHUNK), :] = xr
            g_ref[1, pl.ds(r0, SCAN_CHUNK), :] = xi
            prev0 = pl.multiple_of(jnp.maximum(r0 - 8, 0), 8)
            live = (ci > 0).astype(F32)
            pr = s_ref[0, pl.ds(prev0, 8), :][7:8, :] * live
            pim = s_ref[1, pl.ds(prev0, 8), :][7:8, :] * live
            sr = s_ref[0, pl.ds(r0, SCAN_CHUNK), :]
            si = s_ref[1, pl.ds(r0, SCAN_CHUNK), :]
            sr = jnp.where(rows >= 1, pltpu.roll(sr, 1, 0), pr)
            si = jnp.where(rows >= 1, pltpu.roll(si, 1, 0), pim)
            dar = dar + jnp.sum(xr * sr + xi * si, axis=0, keepdims=True)
            dai = dai + jnp.sum(xi * sr - xr * si, axis=0, keepdims=True)
            return xr[0:1, :], xi[0:1, :], dar, dai

        zero = jnp.zeros((1, _SCAN_W), F32)
        _, _, dar, dai = lax.fori_loop(0, n_chunks, chunk, (zero, zero, zero, zero))
        da_ref[0] = dar
        da_ref[1] = dai

    spec = pl.BlockSpec((2, lp, _SCAN_W), lambda c: (0, 0, c))
    aspec = pl.BlockSpec((1, _SCAN_W), lambda c: (0, c))
    return pl.pallas_call(
        body, name="s5_scan_bwd", grid=(S5_LANES // _SCAN_W,), in_specs=[spec, spec, aspec, aspec],
        out_specs=[spec, pl.BlockSpec((2, 1, _SCAN_W), lambda c: (0, 0, c))],
        out_shape=[jax.ShapeDtypeStruct((2, lp, S5_LANES), F32), jax.ShapeDtypeStruct((2, 1, S5_LANES), F32)],
        compiler_params=_cparams(),
    )(ds, s, ab_re, ab_im)


S5_BLOCKS = 4
_S5_PER = S5_GROUPS // S5_BLOCKS


def _blockdiag(x):
    _, r, c = x.shape
    eye = jnp.eye(_S5_PER, dtype=x.dtype)
    x = x.reshape(S5_BLOCKS, _S5_PER, r, c)
    return (x[:, :, :, None, :] * eye[None, :, None, :, None]).reshape(S5_BLOCKS, _S5_PER * r, _S5_PER * c)


def _blockdiag_extract(m, r, c):
    return jnp.einsum('qgrgc->qgrc', m.reshape(S5_BLOCKS, _S5_PER, r, _S5_PER, c)).reshape(S5_GROUPS, r, c)


def bd_matmul(a, w, *, w_t, reduce, res=None, name):
    _, M, _ = a.shape
    n_w, _, k1, k2 = w.shape
    ka, kout = (k2, k1) if w_t else (k1, k2)
    tm = _row_tile(M)
    n_out, n_red = (1, n_w) if reduce else (n_w, 1)
    has_res = res is not None

    assert n_red <= 2

    def body(*refs):
        a_ref, w_ref = refs[0], refs[1]
        o_ref = refs[3] if has_res else refs[2]
        for q in range(S5_BLOCKS):
            cols = slice(q * kout, (q + 1) * kout)
            part = _dot(a_ref[:, q * ka:(q + 1) * ka].astype(BF16), w_ref[q], 1, 1 if w_t else 0)
            if n_red == 1:
                o_ref[:, cols] = part
            else:
                acc_ref = refs[-1]

                @pl.when(pl.program_id(2) == 0)
                def _():
                    acc_ref[:, cols] = part

                @pl.when(pl.program_id(2) == 1)
                def _():
                    tot = acc_ref[:, cols] + part
                    o_ref[:, cols] = tot + refs[2][:, cols] if has_res else tot

    if reduce:
        a_map, w_map = (lambda o, i, r: (r, i, 0)), (lambda o, i, r: (r, 0, 0, 0))
    else:
        a_map, w_map = (lambda o, i, r: (0, i, 0)), (lambda o, i, r: (o, 0, 0, 0))
    o_map = lambda o, i, r: (o, i, 0)
    in_specs = [pl.BlockSpec((None, tm, S5_BLOCKS * ka), a_map), pl.BlockSpec((None, S5_BLOCKS, k1, k2), w_map)]
    operands = [a, w]
    if has_res:
        in_specs.append(pl.BlockSpec((None, tm, S5_BLOCKS * kout), o_map))
        operands.append(res)
    return pl.pallas_call(
        body, name=name, grid=(n_out, M // tm, n_red), in_specs=in_specs,
        out_specs=pl.BlockSpec((None, tm, S5_BLOCKS * kout), o_map),
        out_shape=jax.ShapeDtypeStruct((n_out, M, S5_BLOCKS * kout), F32),
        scratch_shapes=[pltpu.VMEM((tm, S5_BLOCKS * kout), F32)] if n_red > 1 else [],
        compiler_params=_cparams(),
    )(*operands)


def bd_outer(a, b, name):
    na, M, wa = a.shape
    nb_, _, wb = b.shape
    ka, kb = wa // S5_BLOCKS, wb // S5_BLOCKS
    n_out = max(na, nb_)

    def body(a_ref, b_ref, o_ref):
        o_ref[...] = _dot(a_ref[...].astype(BF16), b_ref[...].astype(BF16), 0, 0)

    return pl.pallas_call(
        body, name=name, grid=(n_out, S5_BLOCKS),
        in_specs=[pl.BlockSpec((None, M, ka), (lambda o, q: (o, 0, q)) if na > 1 else (lambda o, q: (0, 0, q))),
                  pl.BlockSpec((None, M, kb), (lambda o, q: (o, 0, q)) if nb_ > 1 else (lambda o, q: (0, 0, q)))],
        out_specs=pl.BlockSpec((None, None, ka, kb), lambda o, q: (o, q, 0, 0)),
        out_shape=jax.ShapeDtypeStruct((n_out, S5_BLOCKS, ka, kb), F32), compiler_params=_cparams(),
    )(a, b)


def s5_u(proj, lp):
    def fn(row0, rv, pv):
        u, = rv
        return ((jnp.where(_row_mask(row0, u.shape), u, 0.0),), ())

    return rowwise(fn, [(proj, MIX, 5)], [], [(MIX, BF16)], name="s5_u", lp=lp)[0]


def s5_y(ys, proj, d, lp):
    def fn(row0, rv, pv):
        ys_, u = rv
        y = ys_ + pv[0] * u
        return ((y, _gelu(y)), ())

    return rowwise(fn, [(ys, MIX, 0), (proj, MIX, 5)], [d], [(MIX, F32), (MIX, BF16)], name="s5_y", lp=lp)


def s5_glu(z, y, b, lp):
    def fn(row0, rv, pv):
        z_, y_ = rv
        return ((_gelu(y_) * _sigmoid(z_ + pv[0]),), ())

    return rowwise(fn, [(z, MIX, 0), (y, MIX, 0)], [b], [(MIX, BF16)], name="s5_glu", lp=lp)[0]


def s5_glu_bwd(dgl, z, y, b, lp):
    def fn(row0, rv, pv):
        dgl_, z_, y_ = rv
        sg = _sigmoid(z_ + pv[0])
        dz = dgl_ * _gelu(y_) * sg * (1.0 - sg)
        return ((dgl_ * sg, dz), (jnp.sum(dz, axis=0, keepdims=True),))

    return rowwise(fn, [(dgl, MIX, 0), (z, MIX, 0), (y, MIX, 0)], [b], [(MIX, F32), (MIX, BF16)], [((1, MIX), F32)],
                   name="s5_glu_bwd", lp=lp)


def s5_y_bwd(dyg, y, proj, d, lp):
    def fn(row0, rv, pv):
        dyg_, y_, u = rv
        dy = dyg_ * _gelu_grad(y_)
        return ((dy, dy * pv[0]), (jnp.sum(dy * u, axis=0, keepdims=True),))

    return rowwise(fn, [(dyg, MIX, 0), (y, MIX, 0), (proj, MIX, 5)], [d], [(MIX, BF16), (MIX, F32)], [((1, MIX), F32)],
                   name="s5_y_bwd", lp=lp)


def s5_du(du, lp):
    def fn(row0, rv, pv):
        return ((jnp.where(_row_mask(row0, rv[0].shape), rv[0], 0.0),), ())

    return rowwise(fn, [(du, MIX, 0)], [], [(MIX, BF16)], name="s5_du", lp=lp)[0]


def merge_fwd(proj, ya, yb, yc, lp):
    def fn(row0, rv, pv):
        g0, g1, g2, a, b, c = rv
        return ((_sigmoid(g0) * a + _sigmoid(g1) * b + _sigmoid(g2) * c,), ())

    return rowwise(fn, [(proj, D_MODEL, 3), (proj, D_MODEL, 4), (proj, D_MODEL, 5), (ya, D_MODEL, 0), (yb, D_MODEL, 0),
                        (yc, D_MODEL, 0)], [], [(D_MODEL, BF16)], name="merge_fwd", lp=lp)[0]


def merge_bwd(dmix, proj, ya, yb, yc, lp):
    def fn(row0, rv, pv):
        dm, g0, g1, g2, a, b, c = rv
        outs_y, outs_g = [], []
        for g, yv in ((g0, a), (g1, b), (g2, c)):
            sg = _sigmoid(g)
            outs_y.append(dm * sg)
            outs_g.append(dm * yv * sg * (1.0 - sg))
        return (tuple(outs_y) + tuple(outs_g), ())

    return rowwise(fn, [(dmix, D_MODEL, 0), (proj, D_MODEL, 3), (proj, D_MODEL, 4), (proj, D_MODEL, 5),
                        (ya, D_MODEL, 0), (yb, D_MODEL, 0), (yc, D_MODEL, 0)], [], [(D_MODEL, BF16)] * 6,
                   name="merge_bwd", lp=lp)


def loss_head(h, tgt, lp):
    def fn(row0, rv, pv):
        h_, t_ = rv
        live = (row0 + lax.broadcasted_iota(jnp.int32, h_.shape, 0)) >= X0
        diff = jnp.where(live, h_ - t_, 0.0)
        ssq = jnp.sum(jnp.sum(diff * diff, axis=1, keepdims=True), axis=0, keepdims=True)
        return ((diff * (1.0 / D_MODEL),), (ssq * (0.5 / D_MODEL),))

    return rowwise(fn, [(h, D_MODEL, 0), (tgt, D_MODEL, 0)], [], [(D_MODEL, F32)], [((1, 1), F32)], name="loss_head", lp=lp)


def _s5_consts(W):
    ab_re_rep, ab_im_rep, bb_re, bb_im = s5_prep(W['s5_a_re'], W['s5_a_im'], W['s5_log_dt'], W['s5_b_re'], W['s5_b_im'])
    pick = lambda t: t.reshape(S5_GROUPS, S5_GROUP, S5_STATE)[:, 0].reshape(1, S5_LANES)
    bb = jnp.stack([_blockdiag(bb_re.reshape(S5_GROUPS, S5_GROUP, S5_STATE)),
                    _blockdiag(bb_im.reshape(S5_GROUPS, S5_GROUP, S5_STATE))]).astype(BF16)
    return pick(ab_re_rep), pick(ab_im_rep), bb


def layer_fwd(h, hb, W, tabs, lp, ffn1=None):
    cos, sin, rot = tabs
    h1, h1b, sv1 = ffn1 if ffn1 is not None else ffn_fwd(h, hb, W['wg1'], W['wu1'], W['wd1'], W['ln1_g'], W['ln1_b'], lp)
    proj = matmul(h1b[None], W['w_in'][None], tb=True, name="proj")[0]
    cqn, ckvn, kr = mla_prep(proj, cos, sin, rot, W['q_norm_g'], W['kv_norm_g'], lp)
    qn, qr, kn, v = mla_heads(cqn, ckvn, cos, sin, rot, W['wqn'], W['wqr'], W['wkn'], W['wv'], lp)
    o = attn_fwd(qn, qr, kn, v, kr, lp)
    ya = matmul(o, W['mla_wo'], ab='r', bb='r', name="mla_out")[0]
    vconv = conv_fwd(proj, W['conv_w'], W['conv_b'], lp)
    yb = matmul(vconv[None], W['conv_wout'][None], name="conv_out")[0]
    ub = s5_u(proj, lp)
    ab_re, ab_im, bb = _s5_consts(W)
    bu = bd_matmul(ub[None], bb, w_t=False, reduce=False, name="s5_bu")
    s = s5_scan(bu, ab_re, ab_im, lp)
    ys = bd_matmul(s, W['s5_ct'], w_t=False, reduce=True, name="s5_cs")[0]
    y, ygb = s5_y(ys, proj, W['s5_d'], lp)
    zg = matmul(ygb[None], W['s5_wglu'][None], name="s5_glu_mm")[0]
    glb = s5_glu(zg, y, W['s5_b_glu'], lp)
    yc = matmul(glb[None], W['s5_wout'][None], name="s5_out")[0]
    mixed = merge_fwd(proj, ya, yb, yc, lp)
    z2, h2, h2b = mm_res_ln(mixed[None], W['w_o'][None], h1, W['ln2_g'], W['ln2_b'], scale=1.0, name="wo_ln", lp=lp)
    h3, h3b, sv3 = ffn_fwd(h2, h2b, W['wg2'], W['wu2'], W['wd2'], W['ln3_g'], W['ln3_b'], lp)
    sv = dict(sv1=sv1, sv3=sv3, h1b=h1b, proj=proj, cqn=cqn, ckvn=ckvn, kr=kr, qn=qn, qr=qr, kn=kn, v=v, o=o, ya=ya,
              vconv=vconv, yb=yb, ub=ub, ab_re=ab_re, ab_im=ab_im, bb=bb, s=s, y=y, ygb=ygb, zg=zg, glb=glb, yc=yc,
              mixed=mixed, z2=z2)
    return h3, h3b, sv


def layer_bwd(dh3, sv, W, tabs, lp, ffn1=True):
    cos, sin, rot = tabs
    proj = sv['proj']
    G = {}
    dh2, g3 = ffn_bwd(dh3, sv['sv3'], W['wg2'], W['wu2'], W['wd2'], W['ln3_g'], lp)
    G.update(wg2=g3['wg'], wu2=g3['wu'], wd2=g3['wd'], ln3_g=g3['ln_g'], ln3_b=g3['ln_b'])
    dz2, dz2b, G['ln2_g'], G['ln2_b'] = ln_bwd(dh2, sv['z2'], W['ln2_g'], fscale=1.0, name="wo_ln_bwd", lp=lp)
    dmix = matmul(dz2b[None], W['w_o'][None], tb=True, name="wo_dx")[0]
    G['w_o'] = matmul(sv['mixed'][None], dz2b[None], ta=True, out_dtype=WGRAD, name="wo_dw")[0]
    dya, dyb, dyc, dg0, dg1, dg2 = merge_bwd(dmix, proj, sv['ya'], sv['yb'], sv['yc'], lp)
    dgl = matmul(dyc[None], W['s5_wout'][None], tb=True, name="s5_out_dx")[0]
    G['s5_wout'] = matmul(sv['glb'][None], dyc[None], ta=True, out_dtype=WGRAD, name="s5_out_dw")[0]
    t1, dzb, G['s5_b_glu'] = s5_glu_bwd(dgl, sv['zg'], sv['y'], W['s5_b_glu'], lp)
    dyg = matmul(dzb[None], W['s5_wglu'][None], tb=True, res=t1[None], name="s5_glu_dx")[0]
    G['s5_wglu'] = matmul(sv['ygb'][None], dzb[None], ta=True, out_dtype=WGRAD, name="s5_glu_dw")[0]
    dyb_, du_d, G['s5_d'] = s5_y_bwd(dyg, sv['y'], proj, W['s5_d'], lp)
    ds = bd_matmul(dyb_[None], W['s5_ct'], w_t=True, reduce=False, name="s5_cs_dx")
    G['s5_ct'] = bd_outer(sv['s'], dyb_[None], "s5_cs_dw")
    g_adj, d_ab = s5_scan_bwd(ds, sv['s'], sv['ab_re'], sv['ab_im'], lp)
    du = bd_matmul(g_adj, sv['bb'], w_t=True, reduce=True, res=du_d[None], name="s5_bu_dx")[0]
    d_bb = bd_outer(sv['ub'][None], g_adj, "s5_bu_dw")
    du_b = s5_du(du, lp)
    onehot = (jnp.arange(S5_GROUP) == 0).astype(F32)
    spread = lambda t: (t.reshape(S5_GROUPS, 1, S5_STATE) * onehot[None, :, None]).reshape(_S5_ROWS, S5_STATE)
    take = lambda t: _blockdiag_extract(t, S5_GROUP, S5_STATE).reshape(_S5_ROWS, S5_STATE)
    sel = jnp.kron(jnp.eye(S5_GROUPS, dtype=F32), jnp.ones((1, S5_GROUP), F32))
    (G['s5_a_re'], G['s5_a_im'], G['s5_log_dt'], G['s5_b_re'], G['s5_b_im']) = s5_prep_bwd(
        W['s5_a_re'], W['s5_a_im'], W['s5_log_dt'], W['s5_b_re'], W['s5_b_im'],
        spread(d_ab[0]), spread(d_ab[1]), take(d_bb[0]), take(d_bb[1]), sel)
    dv = matmul(dyb[None], W['conv_wout'][None], tb=True, name="conv_out_dx")[0]
    G['conv_wout'] = matmul(sv['vconv'][None], dyb[None], ta=True, out_dtype=WGRAD, name="conv_out_dw")[0]
    dxbar, dbg, dcg, G['conv_w'], G['conv_b'] = conv_bwd(dv, proj, W['conv_w'], W['conv_b'], lp)
    do = matmul(dya[None], W['mla_wo'], tb=True, bb='o', out_dtype=BF16, name="mla_out_dx")
    G['mla_wo'] = matmul(sv['o'], dya[None], ta=True, ab='o', out_dtype=WGRAD, name="mla_out_dw")
    dqn, dqr, dkn, dvv, dkr = attn_bwd(sv['qn'], sv['qr'], sv['kn'], sv['v'], sv['kr'], do, lp)
    dcq, dckv, dqrp = mla_heads_bwd(dqn, dqr, dkn, dvv, cos, sin, rot, W['wqn'], W['wqr'], W['wkn'], W['wv'], lp)
    G['wqn'] = matmul(sv['cqn'][None], dqn, ta=True, bb='o', out_dtype=WGRAD, name="mla_dwqn")
    G['wqr'] = matmul(sv['cqn'][None], dqrp, ta=True, bb='o', out_dtype=WGRAD, name="mla_dwqr")
    G['wkn'] = matmul(sv['ckvn'][None], dkn, ta=True, bb='o', out_dtype=WGRAD, name="mla_dwkn")
    G['wv'] = matmul(sv['ckvn'][None], dvv, ta=True, bb='o', out_dtype=WGRAD, name="mla_dwv")
    dcq_raw, dckv_raw, dkr_raw, G['q_norm_g'], G['kv_norm_g'] = mla_prep_bwd(
        dcq, dckv, dkr, proj, cos, sin, rot, W['q_norm_g'], W['kv_norm_g'], lp)
    zeros = lambda n: jnp.zeros((lp, n), BF16)
    dproj = jnp.concatenate([dcq_raw, dkr_raw, zeros(96), dckv_raw, zeros(256), dxbar, dbg, dcg, du_b, dg0, dg1, dg2], axis=1)
    dh1 = matmul(dproj[None], W['w_in'][None], res=dz2[None], res_scale=ALPHA, name="proj_dx")[0]
    G['w_in'] = matmul(dproj[None], sv['h1b'][None], ta=True, out_dtype=WGRAD, name="proj_dw")[0]
    if not ffn1:
        return dh1, G
    dh0, g1 = ffn_bwd(dh1, sv['sv1'], W['wg1'], W['wu1'], W['wd1'], W['ln1_g'], lp)
    G.update(wg1=g1['wg'], wu1=g1['wu'], wd1=g1['wd'], ln1_g=g1['ln_g'], ln1_b=g1['ln_b'])
    return dh0, G


def _nat_cols(st):
    return jnp.transpose(st, (1, 0, 2)).reshape(st.shape[1], -1)


def _shard_cols(nat):
    k, n = nat.shape
    return jnp.transpose(nat.reshape(k, N_SHARD, n // N_SHARD), (1, 0, 2))


def _win_pad(wt):
    z = lambda n: jnp.zeros((n, wt.shape[1]), wt.dtype)
    return jnp.concatenate([wt[0:384], wt[640:672], z(96), wt[384:640], z(256), wt[672:]], axis=0)


def _win_unpad(wp):
    return jnp.concatenate([wp[0:384], wp[512:768], wp[384:416], wp[1024:]], axis=0)


_BIG = [('ffn1_w_gate', 'T'), ('ffn1_w_up', 'T'), ('ffn1_w_down', 0), ('w_in', 'T'), ('mla_w_uq', 1), ('mla_w_ukv', 1),
        ('mla_w_o', 1), ('conv_w_out', 1), ('s5_w_glu', 0), ('s5_w_out', 1), ('w_o', 0),
        ('ffn2_w_gate', 'T'), ('ffn2_w_up', 'T'), ('ffn2_w_down', 0)]
_REPL = ['ln1_g', 'ln1_b', 'mla_q_norm_g', 'mla_kv_norm_g', 'conv_b', 's5_a_re', 's5_a_im', 's5_log_dt', 's5_b_re',
         's5_b_im', 's5_c_re', 's5_c_im', 's5_d', 's5_b_glu', 'ln2_g', 'ln2_b', 'ln3_g', 'ln3_b']


def compute_weights(st, small):
    W = {}
    for t in ('1', '2'):
        if 'ffn%s_w_gate' % t in st:
            W['wg' + t], W['wu' + t], W['wd' + t] = (st['ffn%s_w_%s' % (t, p)] for p in ('gate', 'up', 'down'))
    if 'w_in' in st:
        W.update(_mixer_weights(st))
    if small is not None:
        W.update(_small_weights(small))
    return W


def _mixer_weights(st):
    W = {}
    W['w_in'] = _win_pad(st['w_in'].reshape(D_IN, D_MODEL))
    uq = jnp.transpose(_nat_cols(st['mla_w_uq']).reshape(Q_RANK, N_HEADS, D_NOPE + D_ROPE), (1, 0, 2))
    W['wqn'], W['wqr'] = uq[:, :, :D_NOPE], uq[:, :, D_NOPE:]
    ukv = jnp.transpose(_nat_cols(st['mla_w_ukv']).reshape(KV_RANK, N_HEADS, D_NOPE + D_V), (1, 0, 2))
    W['wkn'], W['wv'] = ukv[:, :, :D_NOPE], ukv[:, :, D_NOPE:]
    W['mla_wo'] = _nat_cols(st['mla_w_o']).reshape(N_HEADS, D_V, D_MODEL)
    W['conv_wout'] = _nat_cols(st['conv_w_out'])
    W['s5_wglu'] = st['s5_w_glu'].reshape(MIX, MIX)
    W['s5_wout'] = _nat_cols(st['s5_w_out'])
    W['w_o'] = st['w_o'].reshape(D_MODEL, D_MODEL)
    return W


def _small_weights(small):
    W = {}
    W['conv_w'] = small['conv_w']
    for n in ('ln1_g', 'ln1_b', 'ln2_g', 'ln2_b', 'ln3_g', 'ln3_b', 'conv_b', 's5_b_glu'):
        W[n] = small[n].reshape(1, -1)
    W['q_norm_g'] = small['mla_q_norm_g'].reshape(1, -1)
    W['kv_norm_g'] = small['mla_kv_norm_g'].reshape(1, -1)
    W['s5_d'] = small['s5_d'].reshape(1, MIX)
    rep = lambda t: jnp.repeat(t, S5_GROUP, axis=0)
    W['s5_a_re'], W['s5_a_im'] = rep(small['s5_a_re']), rep(small['s5_a_im'])
    W['s5_log_dt'] = jnp.broadcast_to(rep(small['s5_log_dt'].reshape(S5_GROUPS, 1)), (_S5_ROWS, S5_STATE))
    tr = lambda t: jnp.transpose(t, (0, 2, 1)).reshape(_S5_ROWS, S5_STATE)
    W['s5_b_re'], W['s5_b_im'] = tr(small['s5_b_re']), tr(small['s5_b_im'])
    ct = lambda t: _blockdiag(jnp.transpose(t, (0, 2, 1)))
    W['s5_ct'] = jnp.stack([ct(small['s5_c_re']), -ct(small['s5_c_im'])]).astype(BF16)
    return W


def reference_grads(G, ffn=True):
    R = {}
    for t in ('1', '2') if ffn else ():
        R['ffn%s_w_gate' % t] = G['wg' + t].reshape(D_FF, D_MODEL).T
        R['ffn%s_w_up' % t] = G['wu' + t].reshape(D_FF, D_MODEL).T
        R['ffn%s_w_down' % t] = G['wd' + t].reshape(D_FF, D_MODEL)
    R['w_in_t'] = _win_unpad(G['w_in'])
    if ffn:
        R['w_in'] = R['w_in_t'].T
    R['mla_w_uq'] = jnp.transpose(jnp.concatenate([G['wqn'], G['wqr']], axis=2), (1, 0, 2)).reshape(Q_RANK, -1)
    R['mla_w_ukv'] = jnp.transpose(jnp.concatenate([G['wkn'], G['wv']], axis=2), (1, 0, 2)).reshape(KV_RANK, -1)
    R['mla_w_o'] = G['mla_wo'].reshape(N_HEADS * D_V, D_MODEL)
    R['conv_w'], R['conv_w_out'] = G['conv_w'], G['conv_wout']
    R['s5_w_glu'], R['s5_w_out'], R['w_o'] = G['s5_wglu'], G['s5_wout'], G['w_o']
    for n in ('ln1_g', 'ln1_b', 'ln2_g', 'ln2_b', 'ln3_g', 'ln3_b', 'conv_b', 's5_b_glu'):
        if n in G:
            R[n] = G[n].reshape(-1)
    R['mla_q_norm_g'], R['mla_kv_norm_g'] = G['q_norm_g'].reshape(-1), G['kv_norm_g'].reshape(-1)
    R['s5_d'] = G['s5_d'].reshape(S5_GROUPS, S5_GROUP)
    R['s5_a_re'], R['s5_a_im'], R['s5_log_dt'] = G['s5_a_re'], G['s5_a_im'], G['s5_log_dt'].reshape(-1)
    untr = lambda t: jnp.transpose(t.reshape(S5_GROUPS, S5_GROUP, S5_STATE), (0, 2, 1))
    R['s5_b_re'], R['s5_b_im'] = untr(G['s5_b_re']), untr(G['s5_b_im'])
    unct = lambda t: jnp.transpose(_blockdiag_extract(t, S5_STATE, S5_GROUP), (0, 2, 1))
    R['s5_c_re'], R['s5_c_im'] = unct(G['s5_ct'][0]), -unct(G['s5_ct'][1])
    return R


_ANY = pl.BlockSpec(memory_space=pl.ANY)
LANES = 1024


def _place():
    x, y, c = lax.axis_index("x"), lax.axis_index("y"), lax.axis_index("c")
    chips = [(1 - x, y), (x, 1 - y), (1 - x, 1 - y)]
    return x, y, c, chips


def _rows_of(c, half):
    return pl.ds(pl.multiple_of(c * half, 8), half)


def all_gather_shards(srcs, exact):
    n, m = len(srcs), len(exact)
    halves = [s.shape[0] // 2 for s in srcs]

    def body(*refs):
        s_refs, e_refs = refs[:n], refs[n:n + m]
        o_refs, eo_refs = refs[n + m:2 * n + m], refs[2 * n + m:2 * n + 2 * m]
        send, recv, esend, erecv, osend, orecv, lsem = refs[2 * n + 2 * m:]
        x, y, c, chips = _place()
        me = 2 * x + y
        sibling = (x, y, 1 - c)
        own = [pltpu.make_async_remote_copy(src_ref=s_refs[k], dst_ref=o_refs[k].at[me], send_sem=osend.at[k],
                                            recv_sem=orecv.at[k], device_id=sibling, device_id_type=MESH) for k in range(n)]
        local = [pltpu.make_async_copy(e_refs[k], eo_refs[k].at[me], lsem.at[k]) for k in range(m)]
        for cp in own + local:
            cp.start()

        def copy(k, s, src, idx, half_c, to):
            return pltpu.make_async_remote_copy(
                src_ref=src, dst_ref=o_refs[k].at[idx, _rows_of(half_c, halves[k])], send_sem=send.at[6 * k + s],
                recv_sem=recv.at[6 * k + s], device_id=to, device_id_type=MESH)

        def ecopy(k, j, idx, to):
            return pltpu.make_async_remote_copy(src_ref=e_refs[k], dst_ref=eo_refs[k].at[idx], send_sem=esend.at[3 * k + j],
                                                recv_sem=erecv.at[3 * k + j], device_id=to, device_id_type=MESH)

        sends = []
        for k in range(n):
            mine = s_refs[k].at[_rows_of(c, halves[k])]
            sends += [copy(k, j, mine, me, c, (*chip, c)) for j, chip in enumerate(chips)]
        for k in range(m):
            sends += [ecopy(k, j, me, (*chip, c)) for j, chip in enumerate(chips)]
        for cp in sends:
            cp.start()
        for j, chip in enumerate(chips):
            idx = 2 * chip[0] + chip[1]
            for k in range(n):
                landed = o_refs[k].at[idx, _rows_of(c, halves[k])]
                copy(k, j, landed, idx, c, sibling).wait_recv()
                fwd = copy(k, 3 + j, landed, idx, c, sibling)
                fwd.start()
                sends.append(fwd)
        for j, chip in enumerate(chips):
            idx = 2 * chip[0] + chip[1]
            for k in range(n):
                copy(k, 3 + j, s_refs[k].at[_rows_of(c, halves[k])], idx, 1 - c, sibling).wait_recv()
            for k in range(m):
                ecopy(k, j, idx, sibling).wait_recv()
        for cp in sends:
            cp.wait_send()
        for cp in own + local:
            cp.wait()

    outs = pl.pallas_call(
        body, name="all_gather_weights", in_specs=[_ANY] * (n + m), out_specs=[_ANY] * (n + m),
        out_shape=[jax.ShapeDtypeStruct((N_SHARD,) + a.shape, a.dtype) for a in list(srcs) + list(exact)],
        scratch_shapes=[pltpu.SemaphoreType.DMA((6 * n,)), pltpu.SemaphoreType.DMA((6 * n,)),
                        pltpu.SemaphoreType.DMA((3 * m,)), pltpu.SemaphoreType.DMA((3 * m,)),
                        pltpu.SemaphoreType.DMA((n,)), pltpu.SemaphoreType.DMA((n,)), pltpu.SemaphoreType.DMA((m,))],
    )(*srcs, *exact)
    return outs[:n], outs[n:]


def rs_pair_swap(gs):
    n = len(gs)

    def body(*refs):
        g_refs, r_refs, send, recv = refs[:n], refs[n:2 * n], refs[2 * n], refs[2 * n + 1]
        x, y, c, _ = _place()
        copies = [pltpu.make_async_remote_copy(
            src_ref=g_refs[k].at[pl.ds(0, N_SHARD), _rows_of(1 - c, gs[k].shape[1] // 2)], dst_ref=r_refs[k],
            send_sem=send.at[k], recv_sem=recv.at[k], device_id=(x, y, 1 - c), device_id_type=MESH) for k in range(n)]
        for cp in copies:
            cp.start()
        for cp in copies:
            cp.wait()

    return pl.pallas_call(
        body, name="grad_pair_swap", in_specs=[_ANY] * n, out_specs=[_ANY] * n,
        out_shape=[jax.ShapeDtypeStruct((N_SHARD, g.shape[1] // 2, g.shape[2]), g.dtype) for g in gs],
        scratch_shapes=[pltpu.SemaphoreType.DMA((n,)), pltpu.SemaphoreType.DMA((n,))],
    )(*gs)


def _group_tile(half, n_cols, n_arrays):
    budget = (20 * 2 ** 20) // (6 * n_arrays)
    fits = [t for t in range(8, half + 1, 8) if half % t == 0 and t * n_cols * 4 <= budget]
    return max(fits) if fits else 8


def rs_pair_add(gs, rs, cidx, out_dtype, name):
    n = len(gs)
    _, K, cols = gs[0].shape
    half = K // 2
    tr = _group_tile(half, cols, n)
    nb = half // tr

    def body(c_ref, *refs):
        for g_ref, r_ref, o_ref in zip(refs[:n], refs[n:2 * n], refs[2 * n:]):
            o_ref[...] = (g_ref[...].astype(F32) + r_ref[...].astype(F32)).astype(out_dtype)

    gspec = pl.BlockSpec((None, tr, cols), lambda j, i, c: (j, c[0] * nb + i, 0))
    rspec = pl.BlockSpec((None, tr, cols), lambda j, i, c: (j, i, 0))
    return pl.pallas_call(
        body, name=name,
        grid_spec=pltpu.PrefetchScalarGridSpec(num_scalar_prefetch=1, grid=(N_SHARD, nb), in_specs=[gspec] * n + [rspec] * n,
                                               out_specs=[rspec] * n),
        out_shape=[jax.ShapeDtypeStruct((N_SHARD, half, cols), out_dtype)] * n,
        compiler_params=_cparams(),
    )(cidx, *gs, *rs)


def rs_chip_scatter(ps):
    n = len(ps)

    def body(*refs):
        p_refs, q_refs, send, recv, lsem = refs[:n], refs[n:2 * n], refs[2 * n], refs[2 * n + 1], refs[2 * n + 2]
        x, y, c, chips = _place()
        me = 2 * x + y
        local = [pltpu.make_async_copy(p_refs[k].at[me], q_refs[k].at[me], lsem.at[k]) for k in range(n)]
        copies = [pltpu.make_async_remote_copy(
            src_ref=p_refs[k].at[2 * chip[0] + chip[1]], dst_ref=q_refs[k].at[me], send_sem=send.at[3 * k + j],
            recv_sem=recv.at[3 * k + j], device_id=(*chip, c), device_id_type=MESH)
            for k in range(n) for j, chip in enumerate(chips)]
        for cp in local + copies:
            cp.start()
        for cp in copies:
            cp.wait()
        for cp in local:
            cp.wait()

    return pl.pallas_call(
        body, name="grad_chip_scatter", in_specs=[_ANY] * n, out_specs=[_ANY] * n,
        out_shape=[jax.ShapeDtypeStruct(p.shape, p.dtype) for p in ps],
        scratch_shapes=[pltpu.SemaphoreType.DMA((3 * n,)), pltpu.SemaphoreType.DMA((3 * n,)), pltpu.SemaphoreType.DMA((n,))],
    )(*ps)


def rs_chip_sum(qs, nl, cidx, name):
    n = len(qs)
    _, half, cols = qs[0].shape
    tr = _group_tile(half, cols, n)
    nb = half // tr

    def body(c_ref, *refs):
        for k, q_ref in enumerate(refs[:n]):
            o_ref = refs[n + k // nl]
            o_ref[k % nl] = ((q_ref[0].astype(F32) + q_ref[1].astype(F32)) + q_ref[2].astype(F32)) + q_ref[3].astype(F32)

    return pl.pallas_call(
        body, name=name,
        grid_spec=pltpu.PrefetchScalarGridSpec(
            num_scalar_prefetch=1, grid=(nb,),
            in_specs=[pl.BlockSpec((N_SHARD, tr, cols), lambda i, c: (0, i, 0))] * n,
            out_specs=[pl.BlockSpec((nl, tr, cols), lambda i, c: (0, c[0] * nb + i, 0))] * (n // nl)),
        out_shape=[jax.ShapeDtypeStruct((nl, 2 * half, cols), F32)] * (n // nl),
        compiler_params=_cparams(),
    )(cidx, *qs)


def rs_pair_gather(fs):
    n = len(fs)

    def body(*refs):
        f_refs, send, recv = refs[n:2 * n], refs[2 * n], refs[2 * n + 1]
        x, y, c, _ = _place()
        copies = []
        for k in range(n):
            rows = f_refs[k].at[pl.ds(0, fs[k].shape[0]), _rows_of(c, fs[k].shape[1] // 2)]
            copies.append(pltpu.make_async_remote_copy(src_ref=rows, dst_ref=rows, send_sem=send.at[k], recv_sem=recv.at[k],
                                                       device_id=(x, y, 1 - c), device_id_type=MESH))
        for cp in copies:
            cp.start()
        for cp in copies:
            cp.wait()

    return pl.pallas_call(
        body, name="grad_pair_gather", in_specs=[_ANY] * n, out_specs=[_ANY] * n,
        out_shape=[jax.ShapeDtypeStruct(f.shape, f.dtype) for f in fs],
        input_output_aliases={k: k for k in range(n)},
        scratch_shapes=[pltpu.SemaphoreType.DMA((n,)), pltpu.SemaphoreType.DMA((n,))],
    )(*fs)


_HBM = pl.BlockSpec(memory_space=pltpu.HBM)
_SEM = pl.BlockSpec(memory_space=pltpu.SEMAPHORE)
_EFFECT = pltpu.SideEffectType.DATAFLOW_SIDE_EFFECTING


def _in_hbm(a):
    return pltpu.with_memory_space_constraint(a, pltpu.HBM)


def split_start(name, srcs, lands, after, copies_fn, n_copies):
    n = len(srcs)

    def body(*refs):
        for cp in copies_fn(refs[:n], refs[n:2 * n], refs[2 * n + 1], refs[2 * n + 2]):
            cp.start()
        refs[-1][...] = jnp.zeros_like(refs[-1])

    bufs = list(srcs) + list(lands)
    outs = pl.pallas_call(
        body, name=name,
        out_shape=(pltpu.SemaphoreType.DMA((n_copies,)), pltpu.SemaphoreType.DMA((n_copies,)),
                   *[pltpu.HBM(a.shape, a.dtype) for a in bufs], jax.ShapeDtypeStruct((8, 128), F32)),
        in_specs=[_HBM] * (2 * n) + [_ANY],
        out_specs=(_SEM, _SEM, *[_HBM] * (2 * n), pl.BlockSpec(memory_space=pltpu.VMEM)),
        input_output_aliases={i: 2 + i for i in range(2 * n)},
        compiler_params=pltpu.CompilerParams(has_side_effects=_EFFECT),
    )(*[_in_hbm(a) for a in bufs], after)
    return outs[0], outs[1], outs[2:2 + n], outs[2 + n:2 + 2 * n], outs[-1]


def split_wait(name, send, recv, srcs, lands, after, copies_fn, which=None):
    n = len(srcs)

    def body(*refs):
        copies = copies_fn(refs[:n], refs[n:2 * n], refs[2 * n], refs[2 * n + 1])
        per = len(copies) // n
        if which is not None:
            copies = [cp for k in which for cp in copies[k * per:(k + 1) * per]]
        for cp in copies:
            cp.wait_send()
        for cp in copies:
            cp.wait_recv()

    bufs = list(srcs) + list(lands)
    outs = pl.pallas_call(
        body, name=name, out_shape=tuple(pltpu.HBM(a.shape, a.dtype) for a in bufs),
        in_specs=[_HBM] * (2 * n) + [_SEM, _SEM, _ANY], out_specs=tuple([_HBM] * (2 * n)),
        input_output_aliases={i: i for i in range(2 * n)},
        compiler_params=pltpu.CompilerParams(has_side_effects=_EFFECT),
    )(*bufs, send, recv, after)
    return list(outs[:n]), list(outs[n:])


def _gather_copies(s_refs, l_refs, send, recv):
    x, y, c, chips = _place()
    me = 2 * x + y
    out = []
    for k, (s, l) in enumerate(zip(s_refs, l_refs)):
        rows = _rows_of(c, s.shape[0] // 2)
        for j, chip in enumerate(chips):
            out.append(pltpu.make_async_remote_copy(src_ref=s.at[rows], dst_ref=l.at[me, rows], send_sem=send.at[4 * k + j],
                                                    recv_sem=recv.at[4 * k + j], device_id=(*chip, c), device_id_type=MESH))
        out.append(pltpu.make_async_remote_copy(src_ref=s, dst_ref=l.at[me], send_sem=send.at[4 * k + 3],
                                                recv_sem=recv.at[4 * k + 3], device_id=(x, y, 1 - c), device_id_type=MESH))
    return out


def _scatter_copies(s_refs, l_refs, send, recv):
    x, y, c, chips = _place()
    me = 2 * x + y
    return [pltpu.make_async_remote_copy(src_ref=s.at[2 * chip[0] + chip[1]], dst_ref=l.at[me], send_sem=send.at[3 * k + j],
                                         recv_sem=recv.at[3 * k + j], device_id=(*chip, c), device_id_type=MESH)
            for k, (s, l) in enumerate(zip(s_refs, l_refs)) for j, chip in enumerate(chips)]


def gather_forward(lands):
    n = len(lands)

    def body(*refs):
        l_refs, send, recv = refs[n:2 * n], refs[2 * n], refs[2 * n + 1]
        x, y, c, chips = _place()
        copies = []
        for k in range(n):
            rows = _rows_of(c, lands[k].shape[1] // 2)
            for j, chip in enumerate(chips):
                part = l_refs[k].at[2 * chip[0] + chip[1], rows]
                copies.append(pltpu.make_async_remote_copy(src_ref=part, dst_ref=part, send_sem=send.at[3 * k + j],
                                                           recv_sem=recv.at[3 * k + j], device_id=(x, y, 1 - c),
                                                           device_id_type=MESH))
        for cp in copies:
            cp.start()
        for cp in copies:
            cp.wait()

    return pl.pallas_call(
        body, name="gather_forward", in_specs=[_ANY] * n, out_specs=[_ANY] * n,
        out_shape=[jax.ShapeDtypeStruct(a.shape, a.dtype) for a in lands],
        input_output_aliases={k: k for k in range(n)},
        scratch_shapes=[pltpu.SemaphoreType.DMA((3 * n,)), pltpu.SemaphoreType.DMA((3 * n,))],
    )(*lands)


def rs_partials(gs, wire, cidx, tag):
    rs = rs_pair_swap(gs)
    groups = {}
    for k, g in enumerate(gs):
        groups.setdefault((g.shape, jnp.dtype(wire[k]).name), []).append(k)
    ps = [None] * len(gs)
    for gi, ks in enumerate(groups.values()):
        outs = rs_pair_add([gs[k] for k in ks], [rs[k] for k in ks], cidx, wire[ks[0]], "grad_pair_add_%s%d" % (tag, gi))
        for k, o in zip(ks, outs):
            ps[k] = o
    return ps


def rs_finish(items):
    cidx = lax.axis_index("c").astype(jnp.int32).reshape(1)
    groups = {}
    for i, it in enumerate(items):
        groups.setdefault((it[0].shape, len(it), it[0].dtype.name), []).append(i)
    fs = [None] * len(items)
    for gi, ids in enumerate(groups.values()):
        outs = rs_chip_sum([q for i in ids for q in items[i]], len(items[ids[0]]), cidx, "grad_chip_sum_%d" % gi)
        for i, o in zip(ids, outs):
            fs[i] = o
    return rs_pair_gather(fs)


def adamw(w, g, m, v, name):
    shape = w.shape
    if w.ndim == 2:
        block, grid, index = shape, (1,), (lambda i: (0, 0))
    else:
        slab = shape[2:]
        unit = 4 * int(np.prod(slab[:-2] or (1,))) * (-(-slab[-1] // 128) * 128)
        if len(slab) >= 2:
            unit *= -(-slab[-2] // 8) * 8
        k = shape[1]
        tr = k
        if k * unit > 2 ** 21:
            tr = max(t for t in range(8, k, 8) if k % t == 0 and t * unit <= 2 ** 21)
        block, grid = (None, tr) + tuple(slab), (shape[0], k // tr)
        index = lambda l, i: (l, i) + (0,) * len(slab)
        if tr < min(k, 64) and len(slab) == 1:
            tc = max(t for t in range(128, slab[0] + 1, 128) if slab[0] % t == 0 and k * t * 4 <= 2 ** 21)
            block, grid = (None, k, tc), (shape[0], slab[0] // tc)
            index = lambda l, i: (l, 0, i)

    def body(w_ref, g_ref, m_ref, v_ref, d_ref, nm_ref, nv_ref):
        g_ = g_ref[...]
        m_new = ADAM_B1 * m_ref[...] + (1.0 - ADAM_B1) * g_
        v_new = ADAM_B2 * v_ref[...] + (1.0 - ADAM_B2) * (g_ * g_)
        m_hat = m_new / (1.0 - ADAM_B1 ** ADAM_STEP)
        v_hat = v_new / (1.0 - ADAM_B2 ** ADAM_STEP)
        d_ref[...] = -ADAM_LR * (m_hat / (jnp.sqrt(v_hat) + ADAM_EPS) + ADAM_WD * w_ref[...])
        nm_ref[...] = m_new
        nv_ref[...] = v_new

    spec = pl.BlockSpec(block, index)
    return pl.pallas_call(
        body, name=name, grid=grid, in_specs=[spec] * 4, out_specs=[spec] * 3,
        out_shape=[jax.ShapeDtypeStruct(shape, F32)] * 3, compiler_params=_cparams(),
    )(w, g, m, v)


_WEIGHTS = ['meta', 'ffn1_w_gate', 'ffn1_w_up', 'ffn1_w_down', 'ln1_g', 'ln1_b', 'w_in', 'mla_q_norm_g', 'mla_w_uq',
            'mla_kv_norm_g', 'mla_w_ukv', 'mla_w_o', 'conv_w', 'conv_b', 'conv_w_out', 's5_a_re', 's5_a_im', 's5_log_dt',
            's5_b_re', 's5_b_im', 's5_c_re', 's5_c_im', 's5_d', 's5_w_glu', 's5_b_glu', 's5_w_out', 'w_o', 'ln2_g', 'ln2_b',
            'ffn2_w_gate', 'ffn2_w_up', 'ffn2_w_down', 'ln3_g', 'ln3_b']


def _pad_to(flat, n):
    return jnp.concatenate([flat, jnp.zeros((n - flat.shape[0],), flat.dtype)])


def _shard_of(full, axis):
    if axis == 1:
        return _shard_cols(full)
    if axis == 'T':
        return full.T.reshape(N_SHARD, full.shape[1] // N_SHARD, full.shape[0])
    return full.reshape(N_SHARD, full.shape[0] // N_SHARD, full.shape[1])


_FFN_KEY = {'gate': 'wg', 'up': 'wu', 'down': 'wd'}


def _pad_rows(a, axis):
    k = a.shape[axis]
    extra = -k % 32
    if not extra:
        return a
    return jnp.pad(a, [(0, extra) if d == axis else (0, 0) for d in range(a.ndim)])


def _step(env):
    w = {n: env[n] for n in _WEIGHTS}
    mom = {n: env['m_' + n] for n in _WEIGHTS}
    var = {n: env['v_' + n] for n in _WEIGHTS}
    cidx = lax.axis_index("c").astype(jnp.int32).reshape(1)
    chip = 2 * lax.axis_index("x") + lax.axis_index("y")
    big_names = [n for n, _ in _BIG]
    nb = len(big_names)

    kept_t = [n for n, a in _BIG if a == 'T']
    own = {n: (jnp.swapaxes(w[n], 1, 2) if n in kept_t else w[n]) for n in big_names}
    first = [n for n in big_names if n.startswith('ffn1')]
    rest = [n for n in big_names if not n.startswith('ffn1')]
    nr = len(rest)
    src = lambda n, li: _pad_rows(own[n][li].astype(BF16), 0)
    gathered_first, (conv_w_st, meta_st) = all_gather_shards([src(n, 0) for n in first], [w['conv_w'], w['meta']])
    later = [src(n, 0) for n in rest] + [src(n, 1) for n in big_names]
    lands = [lax.empty((N_SHARD,) + s.shape, BF16) for s in later]
    g_send, g_recv, later_t, lands_t, token = split_start("gather_start", later, lands, gathered_first[0], _gather_copies,
                                                          4 * len(later))

    def weights_of(names, st, li, with_small):
        small = None
        if with_small:
            small = {n: w[n][li] for n in _REPL}
            small['conv_w'] = _nat_cols(conv_w_st[:, li])
        return compute_weights({n: a[:, :own[n].shape[1]] for n, a in zip(names, st)}, small)

    x2d = env['x'][0]
    lp = x2d.shape[0] + X0
    tabs = _rope_tables(lp)
    h = jnp.concatenate([jnp.zeros((PAD, D_MODEL), F32), _nat_cols(meta_st), x2d], axis=0) + token[0, 0]
    W0 = weights_of(first, gathered_first, 0, True)
    ffn1 = ffn_fwd(h, h.astype(BF16), W0['wg1'], W0['wu1'], W0['wd1'], W0['ln1_g'], W0['ln1_b'], lp)
    later_t, lands_t = split_wait("gather0_wait", g_send, g_recv, later_t, lands_t, ffn1[0], _gather_copies, range(nr))
    W0.update(weights_of(rest, gather_forward(lands_t[:nr]), 0, False))
    h, hb, sv0 = layer_fwd(None, None, W0, tabs, lp, ffn1=ffn1)
    _, lands_t = split_wait("gather1_wait", g_send, g_recv, later_t, lands_t, h, _gather_copies, range(nr, len(later)))
    W1 = weights_of(big_names, gather_forward(lands_t[nr:]), 1, True)
    h, hb, sv1 = layer_fwd(h, hb, W1, tabs, lp)
    tgt = jnp.concatenate([jnp.zeros((X0, D_MODEL), F32), env['loss_target'][0]], axis=0)
    dh, loss_part = loss_head(h, tgt, lp)
    loss = lax.psum(loss_part[0, 0], ("x", "y", "c"))

    def shards(G, names):
        full = reference_grads(G, ffn=False)

        def one(n, a):
            if n.startswith('ffn'):
                return G[_FFN_KEY[n.split('_')[-1]] + n[3]]
            if n == 'w_in':
                return full['w_in_t'].reshape(N_SHARD, D_IN // N_SHARD, D_MODEL)
            return _shard_of(full[n], a)

        return [_pad_rows(one(n, a), 1) for n, a in _BIG if n in names]

    def scatter_start(name, ps, after):
        qs = [lax.dynamic_update_slice_in_dim(jnp.zeros_like(p), lax.dynamic_slice_in_dim(p, chip, 1, axis=0), chip, axis=0)
              for p in ps]
        return split_start(name, ps, qs, after, _scatter_copies, 3 * len(ps))

    dh, G1 = layer_bwd(dh, sv1, W1, tabs, lp)
    p1 = rs_partials(shards(G1, big_names), [BF16] * nb, cidx, "b")
    s1_send, s1_recv, p1_t, q1_t, token1 = scatter_start("scatter1_start", p1, dh)
    dh, G0 = layer_bwd(dh + token1[0, 0], sv0, W0, tabs, lp, ffn1=False)
    p0 = rs_partials(shards(G0, rest), [BF16] * nr, cidx, "c")
    s0_send, s0_recv, p0_t, q0_t, token0 = scatter_start("scatter0_start", p0, dh)
    dh, g1 = ffn_bwd(dh + token0[0, 0], sv0['sv1'], W0['wg1'], W0['wu1'], W0['wd1'], W0['ln1_g'], lp)
    G0.update(wg1=g1['wg'], wu1=g1['wu'], wd1=g1['wd'], ln1_g=g1['ln_g'], ln1_b=g1['ln_b'])
    _, q1 = split_wait("scatter1_wait", s1_send, s1_recv, p1_t, q1_t, dh, _scatter_copies)
    _, q0_rest = split_wait("scatter0_wait", s0_send, s0_recv, p0_t, q0_t, dh, _scatter_copies)
    full = [reference_grads(G0, ffn=False), reference_grads(G1, ffn=False)]

    s_parts = [jnp.stack([full[li][n] for li in range(DEPTH)]).reshape(-1) for n in _REPL + ['conv_w']]
    s_parts.append(dh[PAD:X0].reshape(-1))
    s_sizes = [int(p.shape[0]) for p in s_parts]
    s_rows = -(-sum(s_sizes) // (16 * LANES)) * 16
    g_small = _pad_to(jnp.concatenate(s_parts), s_rows * LANES).reshape(1, s_rows, LANES)
    g_small = jnp.broadcast_to(g_small, (N_SHARD, s_rows, LANES))
    q0_first = rs_chip_scatter(rs_partials(shards(G0, first) + [g_small], [BF16] * len(first) + [F32], cidx, "a"))
    q0 = dict(zip(first + rest, list(q0_first[:-1]) + list(q0_rest)))
    red = rs_finish([[q0[n], q] for n, q in zip(big_names, q1)] + [[q0_first[-1]]])
    f_small = red[-1].reshape(-1)

    grad = {n: r[:, :own[n].shape[1]] for n, r in zip(big_names, red[:nb])}
    off = 0
    for n, sz in zip(_REPL + ['conv_w', 'meta'], s_sizes):
        grad[n] = f_small[off:off + sz]
        off += sz
    for n in _REPL:
        grad[n] = grad[n].reshape(w[n].shape)
    cw = grad['conv_w'].reshape(DEPTH, 3, MIX)
    grad['conv_w'] = lax.dynamic_slice_in_dim(cw, chip * (MIX // N_SHARD), MIX // N_SHARD, axis=2)
    gm = grad['meta'].reshape(N_META, D_MODEL)
    grad['meta'] = lax.dynamic_slice_in_dim(gm, chip * (D_MODEL // N_SHARD), D_MODEL // N_SHARD, axis=1)

    delta, new_m, new_v = {}, {}, {}
    for n in _WEIGHTS:
        if n in kept_t:
            outs = adamw(own[n], grad[n], jnp.swapaxes(mom[n], 1, 2), jnp.swapaxes(var[n], 1, 2), "adamw_" + n)
            grad[n], delta[n], new_m[n], new_v[n] = [jnp.swapaxes(t, 1, 2) for t in [grad[n]] + list(outs)]
        else:
            delta[n], new_m[n], new_v[n] = adamw(w[n], grad[n], mom[n], var[n], "adamw_" + n)
    return (loss, dh[X0:][None], *[grad[n] for n in _WEIGHTS], *[delta[n] for n in _WEIGHTS],
            *[new_m[n] for n in _WEIGHTS], *[new_v[n] for n in _WEIGHTS])


def kernel(x, meta, ffn1_w_gate, ffn1_w_up, ffn1_w_down, ln1_g, ln1_b, w_in, mla_q_norm_g, mla_w_uq, mla_kv_norm_g, mla_w_ukv, mla_w_o, conv_w, conv_b, conv_w_out, s5_a_re, s5_a_im, s5_log_dt, s5_b_re, s5_b_im, s5_c_re, s5_c_im, s5_d, s5_w_glu, s5_b_glu, s5_w_out, w_o, ln2_g, ln2_b, ffn2_w_gate, ffn2_w_up, ffn2_w_down, ln3_g, ln3_b, loss_target, m_meta, m_ffn1_w_gate, m_ffn1_w_up, m_ffn1_w_down, m_ln1_g, m_ln1_b, m_w_in, m_mla_q_norm_g, m_mla_w_uq, m_mla_kv_norm_g, m_mla_w_ukv, m_mla_w_o, m_conv_w, m_conv_b, m_conv_w_out, m_s5_a_re, m_s5_a_im, m_s5_log_dt, m_s5_b_re, m_s5_b_im, m_s5_c_re, m_s5_c_im, m_s5_d, m_s5_w_glu, m_s5_b_glu, m_s5_w_out, m_w_o, m_ln2_g, m_ln2_b, m_ffn2_w_gate, m_ffn2_w_up, m_ffn2_w_down, m_ln3_g, m_ln3_b, v_meta, v_ffn1_w_gate, v_ffn1_w_up, v_ffn1_w_down, v_ln1_g, v_ln1_b, v_w_in, v_mla_q_norm_g, v_mla_w_uq, v_mla_kv_norm_g, v_mla_w_ukv, v_mla_w_o, v_conv_w, v_conv_b, v_conv_w_out, v_s5_a_re, v_s5_a_im, v_s5_log_dt, v_s5_b_re, v_s5_b_im, v_s5_c_re, v_s5_c_im, v_s5_d, v_s5_w_glu, v_s5_b_glu, v_s5_w_out, v_w_o, v_ln2_g, v_ln2_b, v_ffn2_w_gate, v_ffn2_w_up, v_ffn2_w_down, v_ln3_g, v_ln3_b):
    return _step(dict(locals()))
```

```python
import functools
import math

import numpy as np
import jax
import jax.numpy as jnp
from jax import lax
from jax.experimental import pallas as pl
from jax.experimental.pallas import tpu as pltpu

F32 = jnp.float32
BF16 = jnp.bfloat16

D_MODEL = 1024
DEPTH = 2
N_META = 16
PAD = 112
X0 = PAD + N_META
N_HEADS = 8
D_NOPE = 64
D_ROPE = 32
D_V = 64
Q_RANK = 384
KV_RANK = 256
MIX = 512
S5_GROUPS = 32
S5_GROUP = 16
S5_STATE = 64
S5_LANES = S5_GROUPS * S5_STATE
D_FF = 2816
N_SHARD = 4
FF_SHARD = D_FF // N_SHARD
D_IN = 5792
P_IN = 6144
ALPHA = (2.0 * DEPTH) ** 0.25
LN_EPS = 1e-5
RMS_EPS = 1e-6
ATT_SCALE = (D_NOPE + D_ROPE) ** -0.5
ROPE_BASE = 10000.0
ADAM_LR, ADAM_B1, ADAM_B2, ADAM_EPS, ADAM_WD, ADAM_STEP = 0.001, 0.9, 0.999, 1e-08, 0.01, 10
SCAN_CHUNK = 128
VMEM_LIMIT = 52 * 2 ** 20
WGRAD = BF16
MESH = pl.DeviceIdType.MESH


def _cparams(**kw):
    return pltpu.CompilerParams(vmem_limit_bytes=VMEM_LIMIT, **kw)


def _tile(n):
    if n <= 1088:
        return n
    for t in (1024, 544, 512, 272, 256, 128):
        if n % t == 0:
            return t
    return n


def _row_tile(lp):
    for t in (544, 272, 128):
        if lp % t == 0:
            return t
    return lp


def _sigmoid(x):
    return 1.0 / (1.0 + jnp.exp(-x))


_GELU_C = math.sqrt(2.0 / math.pi)


def _gelu(x):
    return 0.5 * x * (1.0 + jnp.tanh(_GELU_C * (x + 0.044715 * x * x * x)))


def _gelu_grad(x):
    t = jnp.tanh(_GELU_C * (x + 0.044715 * x * x * x))
    return 0.5 * (1.0 + t) + 0.5 * x * (1.0 - t * t) * _GELU_C * (1.0 + 3.0 * 0.044715 * x * x)


def _dot(a, b, ca, cb, precision=None):
    return lax.dot_general(a, b, (((ca,), (cb,)), ((), ())), preferred_element_type=F32, precision=precision)


def matmul(a, b, *, name, ta=False, tb=False, ab='n', bb='n', res=None, res_scale=1.0, scale=1.0, out_dtype=F32):
    if ta:
        _, K, M = a.shape
    else:
        _, M, K = a.shape
    if tb:
        _, N, K2 = b.shape
    else:
        _, K2, N = b.shape
    assert K == K2, (a.shape, b.shape)
    n_out = max(a.shape[0] if ab == 'o' else 1, b.shape[0] if bb == 'o' else 1)
    n_red = max(a.shape[0] if ab == 'r' else 1, b.shape[0] if bb == 'r' else 1)
    tm, tn = _tile(M), _tile(N)
    tk = K if K <= 2304 else _tile(K)
    nkt = K // tk
    n_steps = n_red * nkt

    def bsel(mode, o, r):
        if mode == 'o':
            return o
        if mode == 'r':
            return r // nkt if nkt > 1 else r
        return 0

    def ksel(r):
        if nkt == 1:
            return 0
        return r % nkt if n_red > 1 else r

    a_map = (lambda o, i, j, r: (bsel(ab, o, r), ksel(r), i)) if ta else (lambda o, i, j, r: (bsel(ab, o, r), i, ksel(r)))
    b_map = (lambda o, i, j, r: (bsel(bb, o, r), j, ksel(r))) if tb else (lambda o, i, j, r: (bsel(bb, o, r), ksel(r), j))
    o_map = lambda o, i, j, r: (o, i, j)
    in_specs = [pl.BlockSpec((None, tk, tm) if ta else (None, tm, tk), a_map),
                pl.BlockSpec((None, tn, tk) if tb else (None, tk, tn), b_map)]
    operands = [a, b]
    if res is not None:
        in_specs.append(pl.BlockSpec((None, tm, tn), o_map))
        operands.append(res)
    has_res = res is not None

    def body(*refs):
        a_ref, b_ref = refs[0], refs[1]
        res_ref = refs[2] if has_res else None
        o_ref = refs[3] if has_res else refs[2]
        part = _dot(a_ref[...].astype(BF16), b_ref[...].astype(BF16), 0 if ta else 1, 1 if tb else 0)

        def finish(acc):
            v = acc if scale == 1.0 else acc * scale
            if has_res:
                v = v + res_scale * res_ref[...].astype(F32)
            o_ref[...] = v.astype(o_ref.dtype)

        if n_steps == 1:
            finish(part)
        else:
            acc_ref = refs[-1]
            r = pl.program_id(3)

            @pl.when(r == 0)
            def _():
                acc_ref[...] = part

            @pl.when(r > 0)
            def _():
                acc_ref[...] += part

            @pl.when(r == n_steps - 1)
            def _():
                finish(acc_ref[...])

    return pl.pallas_call(
        body, name=name,
        grid=(n_out, M // tm, N // tn, n_steps),
        in_specs=in_specs,
        out_specs=pl.BlockSpec((None, tm, tn), o_map),
        out_shape=jax.ShapeDtypeStruct((n_out, M, N), out_dtype),
        scratch_shapes=[pltpu.VMEM((tm, tn), F32)] if n_steps > 1 else [],
        compiler_params=_cparams(),
    )(*operands)


def rowwise(fn, rows, pars, outs, accs=(), *, name, lp):
    tm = _row_tile(lp)
    n_rows, n_pars, n_outs, n_accs = len(rows), len(pars), len(outs), len(accs)
    in_specs = [pl.BlockSpec((tm, w), functools.partial(lambda i, cb: (i, cb), cb=cb)) for _, w, cb in rows]
    in_specs += [pl.BlockSpec(p.shape, functools.partial(lambda i, nd: (0,) * nd, nd=p.ndim)) for p in pars]
    out_specs = [pl.BlockSpec((tm, w), lambda i: (i, 0)) for w, _ in outs]
    out_specs += [pl.BlockSpec(s, functools.partial(lambda i, nd: (0,) * nd, nd=len(s))) for s, _ in accs]
    out_shape = [jax.ShapeDtypeStruct((lp, w), dt) for w, dt in outs]
    out_shape += [jax.ShapeDtypeStruct(s, dt) for s, dt in accs]

    def body(*refs):
        i = pl.program_id(0)
        rv = [r[...] for r in refs[:n_rows]]
        pv = [r[...] for r in refs[n_rows:n_rows + n_pars]]
        o_refs = refs[n_rows + n_pars:n_rows + n_pars + n_outs]
        a_refs = refs[n_rows + n_pars + n_outs:]
        ov, av = fn(i * tm, rv, pv)
        for r, v in zip(o_refs, ov):
            r[...] = v.astype(r.dtype)
        if n_accs:
            @pl.when(i == 0)
            def _():
                for r, v in zip(a_refs, av):
                    r[...] = v.astype(r.dtype)

            @pl.when(i > 0)
            def _():
                for r, v in zip(a_refs, av):
                    r[...] += v.astype(r.dtype)

    res = pl.pallas_call(
        body, name=name, grid=(lp // tm,), in_specs=in_specs, out_specs=out_specs, out_shape=out_shape,
        compiler_params=_cparams(),
    )(*[r[0] for r in rows], *pars)
    return res


def _row_mask(row0, shape):
    return (row0 + lax.broadcasted_iota(jnp.int32, shape, 0)) >= PAD


def ffn_up(hb, wg, wu, lp):
    tm = _row_tile(lp)

    def body(h_ref, wg_ref, wu_ref, ab_ref, hid_ref):
        h = h_ref[...]
        a = _dot(h, wg_ref[...], 1, 1)
        b = _dot(h, wu_ref[...], 1, 1)
        ab_ref[0] = a.astype(BF16)
        ab_ref[1] = b.astype(BF16)
        hid_ref[...] = (a * _sigmoid(a) * b).astype(BF16)

    wspec = pl.BlockSpec((None, FF_SHARD, D_MODEL), lambda j, i: (j, 0, 0))
    return pl.pallas_call(
        body, name="ffn_up", grid=(N_SHARD, lp // tm),
        in_specs=[pl.BlockSpec((tm, D_MODEL), lambda j, i: (i, 0)), wspec, wspec],
        out_specs=[pl.BlockSpec((None, 2, tm, FF_SHARD), lambda j, i: (j, 0, i, 0)),
                   pl.BlockSpec((None, tm, FF_SHARD), lambda j, i: (j, i, 0))],
        out_shape=[jax.ShapeDtypeStruct((N_SHARD, 2, lp, FF_SHARD), BF16),
                   jax.ShapeDtypeStruct((N_SHARD, lp, FF_SHARD), BF16)],
        compiler_params=_cparams(),
    )(hb, wg, wu)


def _layer_norm(z, g, b):
    mu = jnp.mean(z, axis=-1, keepdims=True)
    zc = z - mu
    var = jnp.mean(zc * zc, axis=-1, keepdims=True)
    return zc * lax.rsqrt(var + LN_EPS) * g + b


def mm_res_ln(a, w, res, g, b, *, scale, name, lp):
    n_red, _, K = a.shape
    tm = _row_tile(lp)

    def body(a_ref, w_ref, res_ref, g_ref, b_ref, z_ref, h_ref, hb_ref, acc_ref):
        r = pl.program_id(1)
        part = _dot(a_ref[...].astype(BF16), w_ref[...], 1, 0)

        @pl.when(r == 0)
        def _():
            acc_ref[...] = part

        @pl.when(r > 0)
        def _():
            acc_ref[...] += part

        @pl.when(r == n_red - 1)
        def _():
            z = ALPHA * res_ref[...] + scale * acc_ref[...]
            z_ref[...] = z
            hn = _layer_norm(z, g_ref[...], b_ref[...])
            h_ref[...] = hn
            hb_ref[...] = hn.astype(BF16)

    row = pl.BlockSpec((tm, D_MODEL), lambda i, r: (i, 0))
    par = pl.BlockSpec((1, D_MODEL), lambda i, r: (0, 0))
    return pl.pallas_call(
        body, name=name, grid=(lp // tm, n_red),
        in_specs=[pl.BlockSpec((None, tm, K), lambda i, r: (r, i, 0)),
                  pl.BlockSpec((None, K, D_MODEL), lambda i, r: (r, 0, 0)), row, par, par],
        out_specs=[row, row, row],
        out_shape=[jax.ShapeDtypeStruct((lp, D_MODEL), F32), jax.ShapeDtypeStruct((lp, D_MODEL), F32),
                   jax.ShapeDtypeStruct((lp, D_MODEL), BF16)],
        scratch_shapes=[pltpu.VMEM((tm, D_MODEL), F32)],
        compiler_params=_cparams(),
    )(a, w, res, g, b)


def ln_bwd(dh, z, g, *, fscale, name, lp):
    def fn(row0, rv, pv):
        dh_, z_ = rv
        g_, = pv
        mu = jnp.mean(z_, axis=-1, keepdims=True)
        zc = z_ - mu
        rstd = lax.rsqrt(jnp.mean(zc * zc, axis=-1, keepdims=True) + LN_EPS)
        xh = zc * rstd
        dxh = dh_ * g_
        m1 = jnp.mean(dxh, axis=-1, keepdims=True)
        m2 = jnp.mean(dxh * xh, axis=-1, keepdims=True)
        dz = rstd * (dxh - m1 - xh * m2)
        return ((dz, fscale * dz),
                (jnp.sum(dh_ * xh, axis=0, keepdims=True), jnp.sum(dh_, axis=0, keepdims=True)))

    return rowwise(fn, [(dh, D_MODEL, 0), (z, D_MODEL, 0)], [g], [(D_MODEL, F32), (D_MODEL, BF16)],
                   [((1, D_MODEL), F32), ((1, D_MODEL), F32)], name=name, lp=lp)


def ffn_down_bwd(dfb, wd, ab, lp):
    tm = _row_tile(lp)

    def body(df_ref, w_ref, ab_ref, da_ref, db_ref):
        dhid = _dot(df_ref[...], w_ref[...], 1, 1)
        a = ab_ref[0].astype(F32)
        b = ab_ref[1].astype(F32)
        sg = _sigmoid(a)
        da_ref[...] = (dhid * b * (sg * (1.0 + a * (1.0 - sg)))).astype(BF16)
        db_ref[...] = (dhid * (a * sg)).astype(BF16)

    ospec = pl.BlockSpec((None, tm, FF_SHARD), lambda j, i: (j, i, 0))
    return pl.pallas_call(
        body, name="ffn_down_bwd", grid=(N_SHARD, lp // tm),
        in_specs=[pl.BlockSpec((tm, D_MODEL), lambda j, i: (i, 0)),
                  pl.BlockSpec((None, FF_SHARD, D_MODEL), lambda j, i: (j, 0, 0)),
                  pl.BlockSpec((None, 2, tm, FF_SHARD), lambda j, i: (j, 0, i, 0))],
        out_specs=[ospec, ospec],
        out_shape=[jax.ShapeDtypeStruct((N_SHARD, lp, FF_SHARD), BF16)] * 2,
        compiler_params=_cparams(),
    )(dfb, wd, ab)


def ffn_dx(da, db, wg, wu, dz, lp):
    tm = _row_tile(lp)

    def body(da_ref, db_ref, wg_ref, wu_ref, dz_ref, o_ref, acc_ref):
        j = pl.program_id(1)
        part = _dot(da_ref[...], wg_ref[...], 1, 0) + _dot(db_ref[...], wu_ref[...], 1, 0)

        @pl.when(j == 0)
        def _():
            acc_ref[...] = part

        @pl.when(j > 0)
        def _():
            acc_ref[...] += part

        @pl.when(j == N_SHARD - 1)
        def _():
            o_ref[...] = acc_ref[...] + ALPHA * dz_ref[...]

    aspec = pl.BlockSpec((None, tm, FF_SHARD), lambda i, j: (j, i, 0))
    wspec = pl.BlockSpec((None, FF_SHARD, D_MODEL), lambda i, j: (j, 0, 0))
    row = pl.BlockSpec((tm, D_MODEL), lambda i, j: (i, 0))
    return pl.pallas_call(
        body, name="ffn_dx", grid=(lp // tm, N_SHARD), in_specs=[aspec, aspec, wspec, wspec, row], out_specs=row,
        out_shape=jax.ShapeDtypeStruct((lp, D_MODEL), F32), scratch_shapes=[pltpu.VMEM((tm, D_MODEL), F32)],
        compiler_params=_cparams(),
    )(da, db, wg, wu, dz)


def ffn_fwd(h, hb, wg, wu, wd, g, b, lp):
    ab, hid = ffn_up(hb, wg, wu, lp)
    z, hn, hnb = mm_res_ln(hid, wd, h, g, b, scale=0.5, name="ffn_down_ln", lp=lp)
    return hn, hnb, dict(hb=hb, ab=ab, hid=hid, z=z)


def ffn_bwd(dh, sv, wg, wu, wd, g, lp):
    dz, dfb, dg, db = ln_bwd(dh, sv['z'], g, fscale=0.5, name="ffn_ln_bwd", lp=lp)
    da, dbb = ffn_down_bwd(dfb, wd, sv['ab'], lp)
    d_wd = matmul(sv['hid'], dfb[None], ta=True, ab='o', out_dtype=WGRAD, name="ffn_dwd")
    d_wg = matmul(da, sv['hb'][None], ta=True, ab='o', out_dtype=WGRAD, name="ffn_dwg")
    d_wu = matmul(dbb, sv['hb'][None], ta=True, ab='o', out_dtype=WGRAD, name="ffn_dwu")
    dh_in = ffn_dx(da, dbb, wg, wu, dz, lp)
    return dh_in, dict(wg=d_wg, wu=d_wu, wd=d_wd, ln_g=dg, ln_b=db)


def _rope_tables(lp):
    pos = np.arange(lp, dtype=np.float32) - PAD
    inv = ROPE_BASE ** (-np.arange(0, D_ROPE, 2, dtype=np.float32) / D_ROPE)
    ang = pos[:, None] * inv[None, :]
    cos = np.concatenate([np.cos(ang), np.cos(ang)], axis=1).astype(np.float32)
    sin = np.concatenate([np.sin(ang), np.sin(ang)], axis=1).astype(np.float32)
    rot = np.zeros((D_ROPE, D_ROPE), np.float32)
    half = D_ROPE // 2
    for j in range(half):
        rot[j + half, j] = -1.0
        rot[j, j + half] = 1.0
    return jnp.asarray(cos), jnp.asarray(sin), jnp.asarray(rot)


def _rot(x, rot):
    return _dot(x, rot, 1, 0, precision=lax.Precision.HIGHEST)


def _rms(x, g):
    r = lax.rsqrt(jnp.mean(x * x, axis=-1, keepdims=True) + RMS_EPS)
    return x * r * g


def mla_prep(proj, cos, sin, rot, qg, kvg, lp):
    def fn(row0, rv, pv):
        cq, krb, ckv, c, s = rv
        qg_, kvg_, rot_ = pv
        kr = krb[:, :D_ROPE]
        return ((_rms(cq, qg_), _rms(ckv, kvg_), kr * c + _rot(kr, rot_) * s), ())

    return rowwise(fn, [(proj, Q_RANK, 0), (proj, 128, 3), (proj, KV_RANK, 2), (cos, D_ROPE, 0), (sin, D_ROPE, 0)],
                   [qg, kvg, rot], [(Q_RANK, BF16), (KV_RANK, BF16), (D_ROPE, BF16)], name="mla_prep", lp=lp)


def mla_heads(cqn, ckvn, cos, sin, rot, wqn, wqr, wkn, wv, lp):
    tm = _row_tile(lp)

    def body(cq_ref, ckv_ref, c_ref, s_ref, rot_ref, wqn_ref, wqr_ref, wkn_ref, wv_ref, qn_ref, qr_ref, kn_ref, v_ref):
        cq = cq_ref[...]
        ckv = ckv_ref[...]
        qn_ref[...] = _dot(cq, wqn_ref[...], 1, 0).astype(BF16)
        qr = _dot(cq, wqr_ref[...], 1, 0)
        qr_ref[...] = (qr * c_ref[...] + _rot(qr, rot_ref[...]) * s_ref[...]).astype(BF16)
        kn_ref[...] = _dot(ckv, wkn_ref[...], 1, 0).astype(BF16)
        v_ref[...] = _dot(ckv, wv_ref[...], 1, 0).astype(BF16)

    def row(w):
        return pl.BlockSpec((tm, w), lambda h, i: (i, 0))

    def wspec(k, n):
        return pl.BlockSpec((None, k, n), lambda h, i: (h, 0, 0))

    def ospec(n):
        return pl.BlockSpec((None, tm, n), lambda h, i: (h, i, 0))

    return pl.pallas_call(
        body, name="mla_heads", grid=(N_HEADS, lp // tm),
        in_specs=[row(Q_RANK), row(KV_RANK), row(D_ROPE), row(D_ROPE),
                  pl.BlockSpec((D_ROPE, D_ROPE), lambda h, i: (0, 0)),
                  wspec(Q_RANK, D_NOPE), wspec(Q_RANK, D_ROPE), wspec(KV_RANK, D_NOPE), wspec(KV_RANK, D_V)],
        out_specs=[ospec(D_NOPE), ospec(D_ROPE), ospec(D_NOPE), ospec(D_V)],
        out_shape=[jax.ShapeDtypeStruct((N_HEADS, lp, D_NOPE), BF16), jax.ShapeDtypeStruct((N_HEADS, lp, D_ROPE), BF16),
                   jax.ShapeDtypeStruct((N_HEADS, lp, D_NOPE), BF16), jax.ShapeDtypeStruct((N_HEADS, lp, D_V), BF16)],
        compiler_params=_cparams(),
    )(cqn, ckvn, cos, sin, rot, wqn, wqr, wkn, wv)


def _att_probs(qn, qr, kn, kr, row0, tq, lp):
    s = (_dot(qn, kn, 1, 1) + _dot(qr, kr, 1, 1)) * ATT_SCALE
    qi = row0 + lax.broadcasted_iota(jnp.int32, (tq, lp), 0)
    ki = lax.broadcasted_iota(jnp.int32, (tq, lp), 1)
    s = jnp.where((ki <= qi) & (ki >= PAD), s, -1e30)
    p = jnp.exp(s - jnp.max(s, axis=-1, keepdims=True))
    return p / jnp.sum(p, axis=-1, keepdims=True)


def _att_specs(tq, lp):
    def qspec(n):
        return pl.BlockSpec((None, tq, n), lambda h, i: (h, i, 0))

    def kspec(n):
        return pl.BlockSpec((None, lp, n), lambda h, i: (h, 0, 0))

    return qspec, kspec, pl.BlockSpec((lp, D_ROPE), lambda h, i: (0, 0))


def _att_tiles(lp):
    tiles, r = [(0, X0)], X0
    while r < lp:
        tiles.append((r, 256))
        r += 256
    assert r == lp
    return tiles


def attn_fwd(qn, qr, kn, v, kr, lp):
    _, kspec, krspec = _att_specs(0, lp)

    def body(qn_ref, qr_ref, kn_ref, v_ref, kr_ref, o_ref):
        for r0, rows in _att_tiles(lp):
            ke, q = r0 + rows, slice(r0, r0 + rows)
            p = _att_probs(qn_ref[q, :], qr_ref[q, :], kn_ref[0:ke, :], kr_ref[0:ke, :], r0, rows, ke)
            o_ref[q, :] = _dot(p.astype(BF16), v_ref[0:ke, :], 1, 0).astype(BF16)

    return pl.pallas_call(
        body, name="attn_fwd", grid=(N_HEADS, 1),
        in_specs=[kspec(D_NOPE), kspec(D_ROPE), kspec(D_NOPE), kspec(D_V), krspec],
        out_specs=kspec(D_V), out_shape=jax.ShapeDtypeStruct((N_HEADS, lp, D_V), BF16),
        compiler_params=_cparams(),
    )(qn, qr, kn, v, kr)


def attn_bwd(qn, qr, kn, v, kr, do, lp):
    _, kspec, krspec = _att_specs(0, lp)

    def body(qn_ref, qr_ref, kn_ref, v_ref, kr_ref, do_ref, dqn_ref, dqr_ref, dkn_ref, dv_ref, dkr_ref):
        dkn_ref[...] = jnp.zeros_like(dkn_ref)
        dv_ref[...] = jnp.zeros_like(dv_ref)

        @pl.when(pl.program_id(0) == 0)
        def _():
            dkr_ref[...] = jnp.zeros_like(dkr_ref)

        for r0, rows in _att_tiles(lp):
            ke, q = r0 + rows, slice(r0, r0 + rows)
            qn_, qr_, do_ = qn_ref[q, :], qr_ref[q, :], do_ref[q, :]
            kn_, v_, kr_ = kn_ref[0:ke, :], v_ref[0:ke, :], kr_ref[0:ke, :]
            p = _att_probs(qn_, qr_, kn_, kr_, r0, rows, ke)
            dp = _dot(do_, v_, 1, 1)
            delta = jnp.sum(p * dp, axis=-1, keepdims=True)
            ds = (p * (dp - delta) * ATT_SCALE).astype(BF16)
            dqn_ref[q, :] = _dot(ds, kn_, 1, 0).astype(BF16)
            dqr_ref[q, :] = _dot(ds, kr_, 1, 0)
            dkn_ref[0:ke, :] += _dot(ds, qn_, 0, 0)
            dv_ref[0:ke, :] += _dot(p.astype(BF16), do_, 0, 0)
            dkr_ref[0:ke, :] += _dot(ds, qr_, 0, 0)

    return pl.pallas_call(
        body, name="attn_bwd", grid=(N_HEADS, 1),
        in_specs=[kspec(D_NOPE), kspec(D_ROPE), kspec(D_NOPE), kspec(D_V), krspec, kspec(D_V)],
        out_specs=[kspec(D_NOPE), kspec(D_ROPE), kspec(D_NOPE), kspec(D_V), krspec],
        out_shape=[jax.ShapeDtypeStruct((N_HEADS, lp, D_NOPE), BF16), jax.ShapeDtypeStruct((N_HEADS, lp, D_ROPE), F32),
                   jax.ShapeDtypeStruct((N_HEADS, lp, D_NOPE), F32), jax.ShapeDtypeStruct((N_HEADS, lp, D_V), F32),
                   jax.ShapeDtypeStruct((lp, D_ROPE), F32)],
        compiler_params=_cparams(),
    )(qn, qr, kn, v, kr, do)


def mla_heads_bwd(dqn, dqr, dkn, dv, cos, sin, rot, wqn, wqr, wkn, wv, lp):
    tm = _row_tile(lp)

    def body(dqn_ref, dqr_ref, dkn_ref, dv_ref, c_ref, s_ref, rot_ref, wqn_ref, wqr_ref, wkn_ref, wv_ref,
             dcq_ref, dckv_ref, dqrp_ref):
        h = pl.program_id(1)
        dqr_ = dqr_ref[...]
        dqrp = (dqr_ * c_ref[...] - _rot(dqr_ * s_ref[...], rot_ref[...])).astype(BF16)
        dqrp_ref[...] = dqrp
        dcq = _dot(dqn_ref[...], wqn_ref[...], 1, 1) + _dot(dqrp, wqr_ref[...], 1, 1)
        dckv = _dot(dkn_ref[...].astype(BF16), wkn_ref[...], 1, 1) + _dot(dv_ref[...].astype(BF16), wv_ref[...], 1, 1)

        @pl.when(h == 0)
        def _():
            dcq_ref[...] = dcq
            dckv_ref[...] = dckv

        @pl.when(h > 0)
        def _():
            dcq_ref[...] += dcq
            dckv_ref[...] += dckv

    def hspec(n):
        return pl.BlockSpec((None, tm, n), lambda i, h: (h, i, 0))

    def row(w):
        return pl.BlockSpec((tm, w), lambda i, h: (i, 0))

    def wspec(k, n):
        return pl.BlockSpec((None, k, n), lambda i, h: (h, 0, 0))

    return pl.pallas_call(
        body, name="mla_heads_bwd", grid=(lp // tm, N_HEADS),
        in_specs=[hspec(D_NOPE), hspec(D_ROPE), hspec(D_NOPE), hspec(D_V), row(D_ROPE), row(D_ROPE),
                  pl.BlockSpec((D_ROPE, D_ROPE), lambda i, h: (0, 0)),
                  wspec(Q_RANK, D_NOPE), wspec(Q_RANK, D_ROPE), wspec(KV_RANK, D_NOPE), wspec(KV_RANK, D_V)],
        out_specs=[row(Q_RANK), row(KV_RANK), hspec(D_ROPE)],
        out_shape=[jax.ShapeDtypeStruct((lp, Q_RANK), F32), jax.ShapeDtypeStruct((lp, KV_RANK), F32),
                   jax.ShapeDtypeStruct((N_HEADS, lp, D_ROPE), BF16)],
        compiler_params=_cparams(),
    )(dqn, dqr, dkn, dv, cos, sin, rot, wqn, wqr, wkn, wv)


def _rms_bwd(dy, x, g):
    r = lax.rsqrt(jnp.mean(x * x, axis=-1, keepdims=True) + RMS_EPS)
    n = x * r
    dn = dy * g
    dx = r * (dn - n * jnp.mean(dn * n, axis=-1, keepdims=True))
    return dx, jnp.sum(dy * n, axis=0, keepdims=True)


def mla_prep_bwd(dcq, dckv, dkr, proj, cos, sin, rot, qg, kvg, lp):
    def fn(row0, rv, pv):
        dcq_, dckv_, dkr_, cq, ckv, c, s = rv
        qg_, kvg_, rot_ = pv
        dxq, dgq = _rms_bwd(dcq_, cq, qg_)
        dxkv, dgkv = _rms_bwd(dckv_, ckv, kvg_)
        dkr_raw = dkr_ * c - _rot(dkr_ * s, rot_)
        return ((dxq, dxkv, dkr_raw), (dgq, dgkv))

    return rowwise(fn, [(dcq, Q_RANK, 0), (dckv, KV_RANK, 0), (dkr, D_ROPE, 0), (proj, Q_RANK, 0), (proj, KV_RANK, 2),
                        (cos, D_ROPE, 0), (sin, D_ROPE, 0)], [qg, kvg, rot],
                   [(Q_RANK, BF16), (KV_RANK, BF16), (D_ROPE, BF16)], [((1, Q_RANK), F32), ((1, KV_RANK), F32)],
                   name="mla_prep_bwd", lp=lp)


def _shift_down(x, d, rows):
    return jnp.where(rows >= d, pltpu.roll(x, d, 0), 0.0)


def _shift_up(x, d, rows, n):
    return jnp.where(rows < n - d, pltpu.roll(x, n - d, 0), 0.0)


_CONV_W = 128
_XB, _BG, _CG = 1024 // _CONV_W, 1536 // _CONV_W, 2048 // _CONV_W


def _conv_specs(lp):
    def pspec(base):
        return pl.BlockSpec((lp, _CONV_W), functools.partial(lambda c, base: (0, base + c), base=base))

    col = pl.BlockSpec((lp, _CONV_W), lambda c: (0, c))
    wspec = pl.BlockSpec((3, _CONV_W), lambda c: (0, c))
    bspec = pl.BlockSpec((1, _CONV_W), lambda c: (0, c))
    return pspec, col, wspec, bspec


def _conv_core(xbar, cg, w, bias, lp):
    rows = lax.broadcasted_iota(jnp.int32, (lp, _CONV_W), 0)
    u = jnp.where(rows >= PAD, cg * xbar, 0.0)
    u1 = _shift_down(u, 1, rows)
    u2 = _shift_down(u, 2, rows)
    y = bias + w[0:1] * u2 + w[1:2] * u1 + w[2:3] * u
    return rows, u, u1, u2, y


def conv_fwd(proj, w, bias, lp):
    pspec, col, wspec, bspec = _conv_specs(lp)

    def body(x_ref, b_ref, c_ref, w_ref, bias_ref, v_ref):
        _, _, _, _, y = _conv_core(x_ref[...], c_ref[...], w_ref[...], bias_ref[...], lp)
        v_ref[...] = (b_ref[...] * y).astype(BF16)

    return pl.pallas_call(
        body, name="conv_fwd", grid=(MIX // _CONV_W,),
        in_specs=[pspec(_XB), pspec(_BG), pspec(_CG), wspec, bspec], out_specs=col,
        out_shape=jax.ShapeDtypeStruct((lp, MIX), BF16), compiler_params=_cparams(),
    )(proj, proj, proj, w, bias)


def conv_bwd(dv, proj, w, bias, lp):
    pspec, col, wspec, bspec = _conv_specs(lp)

    def body(dv_ref, x_ref, b_ref, c_ref, w_ref, bias_ref, dx_ref, db_ref, dc_ref, dw_ref, dbias_ref):
        xbar, cg, w_ = x_ref[...], c_ref[...], w_ref[...]
        rows, u, u1, u2, y = _conv_core(xbar, cg, w_, bias_ref[...], lp)
        dv_ = dv_ref[...]
        db_ref[...] = (dv_ * y).astype(BF16)
        dy = dv_ * b_ref[...]
        dbias_ref[...] = jnp.sum(dy, axis=0, keepdims=True)
        dw_ref[0:1, :] = jnp.sum(dy * u2, axis=0, keepdims=True)
        dw_ref[1:2, :] = jnp.sum(dy * u1, axis=0, keepdims=True)
        dw_ref[2:3, :] = jnp.sum(dy * u, axis=0, keepdims=True)
        du = w_[2:3] * dy + w_[1:2] * _shift_up(dy, 1, rows, lp) + w_[0:1] * _shift_up(dy, 2, rows, lp)
        du = jnp.where(rows >= PAD, du, 0.0)
        dc_ref[...] = (du * xbar).astype(BF16)
        dx_ref[...] = (du * cg).astype(BF16)

    return pl.pallas_call(
        body, name="conv_bwd", grid=(MIX // _CONV_W,),
        in_specs=[col, pspec(_XB), pspec(_BG), pspec(_CG), wspec, bspec],
        out_specs=[col, col, col, wspec, bspec],
        out_shape=[jax.ShapeDtypeStruct((lp, MIX), BF16)] * 3 + [jax.ShapeDtypeStruct((3, MIX), F32),
                                                                jax.ShapeDtypeStruct((1, MIX), F32)],
        compiler_params=_cparams(),
    )(dv, proj, proj, proj, w, bias)


def _s5_disc(a_re, a_im, log_dt, b_re, b_im):
    dt = jnp.exp(log_dt)
    mag = jnp.exp(dt * a_re)
    ab_re, ab_im = mag * jnp.cos(dt * a_im), mag * jnp.sin(dt * a_im)
    den = a_re * a_re + a_im * a_im
    nr, ni = ab_re - 1.0, ab_im
    coef_re = (nr * a_re + ni * a_im) / den
    coef_im = (ni * a_re - nr * a_im) / den
    return ab_re, ab_im, coef_re * b_re - coef_im * b_im, coef_re * b_im + coef_im * b_re


_S5_ROWS = S5_GROUPS * S5_GROUP


def s5_prep(a_re, a_im, log_dt, b_re, b_im):
    def body(ar, ai, ld, br, bi, o0, o1, o2, o3):
        for o, v in zip((o0, o1, o2, o3), _s5_disc(ar[...], ai[...], ld[...], br[...], bi[...])):
            o[...] = v

    return pl.pallas_call(body, name="s5_prep",
                          out_shape=[jax.ShapeDtypeStruct((_S5_ROWS, S5_STATE), F32)] * 4)(a_re, a_im, log_dt, b_re, b_im)


def s5_prep_bwd(a_re, a_im, log_dt, b_re, b_im, d_ab_re, d_ab_im, d_bb_re, d_bb_im, sel):
    def body(ar, ai, ld, br, bi, g0, g1, g2, g3, sel_ref, da_re, da_im, dld, dbr, dbi):
        _, vjp = jax.vjp(_s5_disc, ar[...], ai[...], ld[...], br[...], bi[...])
        c_ar, c_ai, c_ld, c_br, c_bi = vjp((g0[...], g1[...], g2[...], g3[...]))
        s = sel_ref[...]
        hi = lax.Precision.HIGHEST
        da_re[...] = _dot(s, c_ar, 1, 0, precision=hi)
        da_im[...] = _dot(s, c_ai, 1, 0, precision=hi)
        dld[...] = jnp.sum(_dot(s, c_ld, 1, 0, precision=hi), axis=-1, keepdims=True)
        dbr[...] = c_br
        dbi[...] = c_bi

    g = jax.ShapeDtypeStruct((S5_GROUPS, S5_STATE), F32)
    full = jax.ShapeDtypeStruct((_S5_ROWS, S5_STATE), F32)
    return pl.pallas_call(body, name="s5_prep_bwd",
                          out_shape=[g, g, jax.ShapeDtypeStruct((S5_GROUPS, 1), F32), full, full],
                          )(a_re, a_im, log_dt, b_re, b_im, d_ab_re, d_ab_im, d_bb_re, d_bb_im, sel)


_SCAN_W = 128
_SCAN_STEPS = int(math.log2(SCAN_CHUNK))


def _cmul(ar, ai, br, bi):
    return ar * br - ai * bi, ar * bi + ai * br


def _scan_powers(ar, ai, reverse):
    pw = [(ar, ai)]
    for _ in range(_SCAN_STEPS):
        pw.append(_cmul(*pw[-1], *pw[-1]))
    rows = lax.broadcasted_iota(jnp.int32, (SCAN_CHUNK, ar.shape[-1]), 0)
    tr = jnp.broadcast_to(ar, rows.shape)
    ti = jnp.broadcast_to(ai, rows.shape)
    for k in range(_SCAN_STEPS):
        d = 2 ** k
        if reverse:
            live = rows < SCAN_CHUNK - d
            mr, mi = _cmul(tr, ti, _shift_up(tr, d, rows, SCAN_CHUNK), _shift_up(ti, d, rows, SCAN_CHUNK))
        else:
            live = rows >= d
            mr, mi = _cmul(tr, ti, _shift_down(tr, d, rows), _shift_down(ti, d, rows))
        tr = jnp.where(live, mr, tr)
        ti = jnp.where(live, mi, ti)
    return pw, rows, tr, ti


def s5_scan(bu, ab_re, ab_im, lp):
    n_chunks = lp // SCAN_CHUNK

    def body(bu_ref, ar_ref, ai_ref, s_ref):
        ar, ai = ar_ref[...], ai_ref[...]
        pw, rows, tr, ti = _scan_powers(ar, ai, False)

        def chunk(ci, carry):
            cr, cim = carry
            r0 = pl.multiple_of(ci * SCAN_CHUNK, SCAN_CHUNK)
            xr = bu_ref[0, pl.ds(r0, SCAN_CHUNK), :]
            xi = bu_ref[1, pl.ds(r0, SCAN_CHUNK), :]
            for k in range(_SCAN_STEPS):
                d = 2 ** k
                mr, mi = _cmul(pw[k][0], pw[k][1], _shift_down(xr, d, rows), _shift_down(xi, d, rows))
                xr, xi = xr + mr, xi + mi
            mr, mi = _cmul(tr, ti, cr, cim)
            xr, xi = xr + mr, xi + mi
            s_ref[0, pl.ds(r0, SCAN_CHUNK), :] = xr
            s_ref[1, pl.ds(r0, SCAN_CHUNK), :] = xi
            return xr[SCAN_CHUNK - 1:SCAN_CHUNK, :], xi[SCAN_CHUNK - 1:SCAN_CHUNK, :]

        zero = jnp.zeros((1, _SCAN_W), F32)
        lax.fori_loop(0, n_chunks, chunk, (zero, zero))

    spec = pl.BlockSpec((2, lp, _SCAN_W), lambda c: (0, 0, c))
    aspec = pl.BlockSpec((1, _SCAN_W), lambda c: (0, c))
    return pl.pallas_call(
        body, name="s5_scan", grid=(S5_LANES // _SCAN_W,), in_specs=[spec, aspec, aspec], out_specs=spec,
        out_shape=jax.ShapeDtypeStruct((2, lp, S5_LANES), F32), compiler_params=_cparams(),
    )(bu, ab_re, ab_im)


def s5_scan_bwd(ds, s, ab_re, ab_im, lp):
    n_chunks = lp // SCAN_CHUNK

    def body(ds_ref, s_ref, ar_ref, ai_ref, g_ref, da_ref):
        ar, ai = ar_ref[...], -ai_ref[...]
        pw, rows, tr, ti = _scan_powers(ar, ai, True)

        def chunk(k, carry):
            cr, cim, dar, dai = carry
            ci = n_chunks - 1 - k
            r0 = pl.multiple_of(ci * SCAN_CHUNK, SCAN_CHUNK)
            xr = ds_ref[0, pl.ds(r0, SCAN_CHUNK), :]
            xi = ds_ref[1, pl.ds(r0, SCAN_CHUNK), :]
            for j in range(_SCAN_STEPS):
                d = 2 ** j
                mr, mi = _cmul(pw[j][0], pw[j][1], _shift_up(xr, d, rows, SCAN_CHUNK), _shift_up(xi, d, rows, SCAN_CHUNK))
                xr, xi = xr + mr, xi + mi
            mr, mi = _cmul(tr, ti, cr, cim)
            xr, xi = xr + mr, xi + mi
            g_ref[0, pl.ds(r0, SCAN_CHUNK), :] = xr
            g_ref[1, pl.ds(r0, SCAN_CHUNK), :] = xi
            prev0 = pl.multiple_of(jnp.maximum(r0 - 8, 0), 8)
            live = (ci > 0).astype(F32)
            pr = s_ref[0, pl.ds(prev0, 8), :][7:8, :] * live
            pim = s_ref[1, pl.ds(prev0, 8), :][7:8, :] * live
            sr = s_ref[0, pl.ds(r0, SCAN_CHUNK), :]
            si = s_ref[1, pl.ds(r0, SCAN_CHUNK), :]
            sr = jnp.where(rows >= 1, pltpu.roll(sr, 1, 0), pr)
            si = jnp.where(rows >= 1, pltpu.roll(si, 1, 0), pim)
            dar = dar + jnp.sum(xr * sr + xi * si, axis=0, keepdims=True)
            dai = dai + jnp.sum(xi * sr - xr * si, axis=0, keepdims=True)
            return xr[0:1, :], xi[0:1, :], dar, dai

        zero = jnp.zeros((1, _SCAN_W), F32)
        _, _, dar, dai = lax.fori_loop(0, n_chunks, chunk, (zero, zero, zero, zero))
        da_ref[0] = dar
        da_ref[1] = dai

    spec = pl.BlockSpec((2, lp, _SCAN_W), lambda c: (0, 0, c))
    aspec = pl.BlockSpec((1, _SCAN_W), lambda c: (0, c))
    return pl.pallas_call(
        body, name="s5_scan_bwd", grid=(S5_LANES // _SCAN_W,), in_specs=[spec, spec, aspec, aspec],
        out_specs=[spec, pl.BlockSpec((2, 1, _SCAN_W), lambda c: (0, 0, c))],
        out_shape=[jax.ShapeDtypeStruct((2, lp, S5_LANES), F32), jax.ShapeDtypeStruct((2, 1, S5_LANES), F32)],
        compiler_params=_cparams(),
    )(ds, s, ab_re, ab_im)


S5_BLOCKS = 4
_S5_PER = S5_GROUPS // S5_BLOCKS


def _blockdiag(x):
    _, r, c = x.shape
    eye = jnp.eye(_S5_PER, dtype=x.dtype)
    x = x.reshape(S5_BLOCKS, _S5_PER, r, c)
    return (x[:, :, :, None, :] * eye[None, :, None, :, None]).reshape(S5_BLOCKS, _S5_PER * r, _S5_PER * c)


def _blockdiag_extract(m, r, c):
    return jnp.einsum('qgrgc->qgrc', m.reshape(S5_BLOCKS, _S5_PER, r, _S5_PER, c)).reshape(S5_GROUPS, r, c)


def bd_matmul(a, w, *, w_t, reduce, res=None, name):
    _, M, _ = a.shape
    n_w, _, k1, k2 = w.shape
    ka, kout = (k2, k1) if w_t else (k1, k2)
    tm = _row_tile(M)
    n_out, n_red = (1, n_w) if reduce else (n_w, 1)
    has_res = res is not None

    assert n_red <= 2

    def body(*refs):
        a_ref, w_ref = refs[0], refs[1]
        o_ref = refs[3] if has_res else refs[2]
        for q in range(S5_BLOCKS):
            cols = slice(q * kout, (q + 1) * kout)
            part = _dot(a_ref[:, q * ka:(q + 1) * ka].astype(BF16), w_ref[q], 1, 1 if w_t else 0)
            if n_red == 1:
                o_ref[:, cols] = part
            else:
                acc_ref = refs[-1]

                @pl.when(pl.program_id(2) == 0)
                def _():
                    acc_ref[:, cols] = part

                @pl.when(pl.program_id(2) == 1)
                def _():
                    tot = acc_ref[:, cols] + part
                    o_ref[:, cols] = tot + refs[2][:, cols] if has_res else tot

    if reduce:
        a_map, w_map = (lambda o, i, r: (r, i, 0)), (lambda o, i, r: (r, 0, 0, 0))
    else:
        a_map, w_map = (lambda o, i, r: (0, i, 0)), (lambda o, i, r: (o, 0, 0, 0))
    o_map = lambda o, i, r: (o, i, 0)
    in_specs = [pl.BlockSpec((None, tm, S5_BLOCKS * ka), a_map), pl.BlockSpec((None, S5_BLOCKS, k1, k2), w_map)]
    operands = [a, w]
    if has_res:
        in_specs.append(pl.BlockSpec((None, tm, S5_BLOCKS * kout), o_map))
        operands.append(res)
    return pl.pallas_call(
        body, name=name, grid=(n_out, M // tm, n_red), in_specs=in_specs,
        out_specs=pl.BlockSpec((None, tm, S5_BLOCKS * kout), o_map),
        out_shape=jax.ShapeDtypeStruct((n_out, M, S5_BLOCKS * kout), F32),
        scratch_shapes=[pltpu.VMEM((tm, S5_BLOCKS * kout), F32)] if n_red > 1 else [],
        compiler_params=_cparams(),
    )(*operands)


def bd_outer(a, b, name):
    na, M, wa = a.shape
    nb_, _, wb = b.shape
    ka, kb = wa // S5_BLOCKS, wb // S5_BLOCKS
    n_out = max(na, nb_)

    def body(a_ref, b_ref, o_ref):
        o_ref[...] = _dot(a_ref[...].astype(BF16), b_ref[...].astype(BF16), 0, 0)

    return pl.pallas_call(
        body, name=name, grid=(n_out, S5_BLOCKS),
        in_specs=[pl.BlockSpec((None, M, ka), (lambda o, q: (o, 0, q)) if na > 1 else (lambda o, q: (0, 0, q))),
                  pl.BlockSpec((None, M, kb), (lambda o, q: (o, 0, q)) if nb_ > 1 else (lambda o, q: (0, 0, q)))],
        out_specs=pl.BlockSpec((None, None, ka, kb), lambda o, q: (o, q, 0, 0)),
        out_shape=jax.ShapeDtypeStruct((n_out, S5_BLOCKS, ka, kb), F32), compiler_params=_cparams(),
    )(a, b)


def s5_u(proj, lp):
    def fn(row0, rv, pv):
        u, = rv
        return ((jnp.where(_row_mask(row0, u.shape), u, 0.0),), ())

    return rowwise(fn, [(proj, MIX, 5)], [], [(MIX, BF16)], name="s5_u", lp=lp)[0]


def s5_y(ys, proj, d, lp):
    def fn(row0, rv, pv):
        ys_, u = rv
        y = ys_ + pv[0] * u
        return ((y, _gelu(y)), ())

    return rowwise(fn, [(ys, MIX, 0), (proj, MIX, 5)], [d], [(MIX, F32), (MIX, BF16)], name="s5_y", lp=lp)


def s5_glu(z, y, b, lp):
    def fn(row0, rv, pv):
        z_, y_ = rv
        return ((_gelu(y_) * _sigmoid(z_ + pv[0]),), ())

    return rowwise(fn, [(z, MIX, 0), (y, MIX, 0)], [b], [(MIX, BF16)], name="s5_glu", lp=lp)[0]


def s5_glu_bwd(dgl, z, y, b, lp):
    def fn(row0, rv, pv):
        dgl_, z_, y_ = rv
        sg = _sigmoid(z_ + pv[0])
        dz = dgl_ * _gelu(y_) * sg * (1.0 - sg)
        return ((dgl_ * sg, dz), (jnp.sum(dz, axis=0, keepdims=True),))

    return rowwise(fn, [(dgl, MIX, 0), (z, MIX, 0), (y, MIX, 0)], [b], [(MIX, F32), (MIX, BF16)], [((1, MIX), F32)],
                   name="s5_glu_bwd", lp=lp)


def s5_y_bwd(dyg, y, proj, d, lp):
    def fn(row0, rv, pv):
        dyg_, y_, u = rv
        dy = dyg_ * _gelu_grad(y_)
        return ((dy, dy * pv[0]), (jnp.sum(dy * u, axis=0, keepdims=True),))

    return rowwise(fn, [(dyg, MIX, 0), (y, MIX, 0), (proj, MIX, 5)], [d], [(MIX, BF16), (MIX, F32)], [((1, MIX), F32)],
                   name="s5_y_bwd", lp=lp)


def s5_du(du, lp):
    def fn(row0, rv, pv):
        return ((jnp.where(_row_mask(row0, rv[0].shape), rv[0], 0.0),), ())

    return rowwise(fn, [(du, MIX, 0)], [], [(MIX, BF16)], name="s5_du", lp=lp)[0]


def merge_fwd(proj, ya, yb, yc, lp):
    def fn(row0, rv, pv):
        g0, g1, g2, a, b, c = rv
        return ((_sigmoid(g0) * a + _sigmoid(g1) * b + _sigmoid(g2) * c,), ())

    return rowwise(fn, [(proj, D_MODEL, 3), (proj, D_MODEL, 4), (proj, D_MODEL, 5), (ya, D_MODEL, 0), (yb, D_MODEL, 0),
                        (yc, D_MODEL, 0)], [], [(D_MODEL, BF16)], name="merge_fwd", lp=lp)[0]


def merge_bwd(dmix, proj, ya, yb, yc, lp):
    def fn(row0, rv, pv):
        dm, g0, g1, g2, a, b, c = rv
        outs_y, outs_g = [], []
        for g, yv in ((g0, a), (g1, b), (g2, c)):
            sg = _sigmoid(g)
            outs_y.append(dm * sg)
            outs_g.append(dm * yv * sg * (1.0 - sg))
        return (tuple(outs_y) + tuple(outs_g), ())

    return rowwise(fn, [(dmix, D_MODEL, 0), (proj, D_MODEL, 3), (proj, D_MODEL, 4), (proj, D_MODEL, 5),
                        (ya, D_MODEL, 0), (yb, D_MODEL, 0), (yc, D_MODEL, 0)], [], [(D_MODEL, BF16)] * 6,
                   name="merge_bwd", lp=lp)


def loss_head(h, tgt, lp):
    def fn(row0, rv, pv):
        h_, t_ = rv
        live = (row0 + lax.broadcasted_iota(jnp.int32, h_.shape, 0)) >= X0
        diff = jnp.where(live, h_ - t_, 0.0)
        ssq = jnp.sum(jnp.sum(diff * diff, axis=1, keepdims=True), axis=0, keepdims=True)
        return ((diff * (1.0 / D_MODEL),), (ssq * (0.5 / D_MODEL),))

    return rowwise(fn, [(h, D_MODEL, 0), (tgt, D_MODEL, 0)], [], [(D_MODEL, F32)], [((1, 1), F32)], name="loss_head", lp=lp)


def _s5_consts(W):
    ab_re_rep, ab_im_rep, bb_re, bb_im = s5_prep(W['s5_a_re'], W['s5_a_im'], W['s5_log_dt'], W['s5_b_re'], W['s5_b_im'])
    pick = lambda t: t.reshape(S5_GROUPS, S5_GROUP, S5_STATE)[:, 0].reshape(1, S5_LANES)
    bb = jnp.stack([_blockdiag(bb_re.reshape(S5_GROUPS, S5_GROUP, S5_STATE)),
                    _blockdiag(bb_im.reshape(S5_GROUPS, S5_GROUP, S5_STATE))]).astype(BF16)
    return pick(ab_re_rep), pick(ab_im_rep), bb


def layer_fwd(h, hb, W, tabs, lp, ffn1=None):
    cos, sin, rot = tabs
    h1, h1b, sv1 = ffn1 if ffn1 is not None else ffn_fwd(h, hb, W['wg1'], W['wu1'], W['wd1'], W['ln1_g'], W['ln1_b'], lp)
    proj = matmul(h1b[None], W['w_in'][None], tb=True, name="proj")[0]
    cqn, ckvn, kr = mla_prep(proj, cos, sin, rot, W['q_norm_g'], W['kv_norm_g'], lp)
    qn, qr, kn, v = mla_heads(cqn, ckvn, cos, sin, rot, W['wqn'], W['wqr'], W['wkn'], W['wv'], lp)
    o = attn_fwd(qn, qr, kn, v, kr, lp)
    ya = matmul(o, W['mla_wo'], ab='r', bb='r', name="mla_out")[0]
    vconv = conv_fwd(proj, W['conv_w'], W['conv_b'], lp)
    yb = matmul(vconv[None], W['conv_wout'][None], name="conv_out")[0]
    ub = s5_u(proj, lp)
    ab_re, ab_im, bb = _s5_consts(W)
    bu = bd_matmul(ub[None], bb, w_t=False, reduce=False, name="s5_bu")
    s = s5_scan(bu, ab_re, ab_im, lp)
    ys = bd_matmul(s, W['s5_ct'], w_t=False, reduce=True, name="s5_cs")[0]
    y, ygb = s5_y(ys, proj, W['s5_d'], lp)
    zg = matmul(ygb[None], W['s5_wglu'][None], name="s5_glu_mm")[0]
    glb = s5_glu(zg, y, W['s5_b_glu'], lp)
    yc = matmul(glb[None], W['s5_wout'][None], name="s5_out")[0]
    mixed = merge_fwd(proj, ya, yb, yc, lp)
    z2, h2, h2b = mm_res_ln(mixed[None], W['w_o'][None], h1, W['ln2_g'], W['ln2_b'], scale=1.0, name="wo_ln", lp=lp)
    h3, h3b, sv3 = ffn_fwd(h2, h2b, W['wg2'], W['wu2'], W['wd2'], W['ln3_g'], W['ln3_b'], lp)
    sv = dict(sv1=sv1, sv3=sv3, h1b=h1b, proj=proj, cqn=cqn, ckvn=ckvn, kr=kr, qn=qn, qr=qr, kn=kn, v=v, o=o, ya=ya,
              vconv=vconv, yb=yb, ub=ub, ab_re=ab_re, ab_im=ab_im, bb=bb, s=s, y=y, ygb=ygb, zg=zg, glb=glb, yc=yc,
              mixed=mixed, z2=z2)
    return h3, h3b, sv


def layer_bwd(dh3, sv, W, tabs, lp, ffn1=True):
    cos, sin, rot = tabs
    proj = sv['proj']
    G = {}
    dh2, g3 = ffn_bwd(dh3, sv['sv3'], W['wg2'], W['wu2'], W['wd2'], W['ln3_g'], lp)
    G.update(wg2=g3['wg'], wu2=g3['wu'], wd2=g3['wd'], ln3_g=g3['ln_g'], ln3_b=g3['ln_b'])
    dz2, dz2b, G['ln2_g'], G['ln2_b'] = ln_bwd(dh2, sv['z2'], W['ln2_g'], fscale=1.0, name="wo_ln_bwd", lp=lp)
    dmix = matmul(dz2b[None], W['w_o'][None], tb=True, name="wo_dx")[0]
    G['w_o'] = matmul(sv['mixed'][None], dz2b[None], ta=True, out_dtype=WGRAD, name="wo_dw")[0]
    dya, dyb, dyc, dg0, dg1, dg2 = merge_bwd(dmix, proj, sv['ya'], sv['yb'], sv['yc'], lp)
    dgl = matmul(dyc[None], W['s5_wout'][None], tb=True, name="s5_out_dx")[0]
    G['s5_wout'] = matmul(sv['glb'][None], dyc[None], ta=True, out_dtype=WGRAD, name="s5_out_dw")[0]
    t1, dzb, G['s5_b_glu'] = s5_glu_bwd(dgl, sv['zg'], sv['y'], W['s5_b_glu'], lp)
    dyg = matmul(dzb[None], W['s5_wglu'][None], tb=True, res=t1[None], name="s5_glu_dx")[0]
    G['s5_wglu'] = matmul(sv['ygb'][None], dzb[None], ta=True, out_dtype=WGRAD, name="s5_glu_dw")[0]
    dyb_, du_d, G['s5_d'] = s5_y_bwd(dyg, sv['y'], proj, W['s5_d'], lp)
    ds = bd_matmul(dyb_[None], W['s5_ct'], w_t=True, reduce=False, name="s5_cs_dx")
    G['s5_ct'] = bd_outer(sv['s'], dyb_[None], "s5_cs_dw")
    g_adj, d_ab = s5_scan_bwd(ds, sv['s'], sv['ab_re'], sv['ab_im'], lp)
    du = bd_matmul(g_adj, sv['bb'], w_t=True, reduce=True, res=du_d[None], name="s5_bu_dx")[0]
    d_bb = bd_outer(sv['ub'][None], g_adj, "s5_bu_dw")
    du_b = s5_du(du, lp)
    onehot = (jnp.arange(S5_GROUP) == 0).astype(F32)
    spread = lambda t: (t.reshape(S5_GROUPS, 1, S5_STATE) * onehot[None, :, None]).reshape(_S5_ROWS, S5_STATE)
    take = lambda t: _blockdiag_extract(t, S5_GROUP, S5_STATE).reshape(_S5_ROWS, S5_STATE)
    sel = jnp.kron(jnp.eye(S5_GROUPS, dtype=F32), jnp.ones((1, S5_GROUP), F32))
    (G['s5_a_re'], G['s5_a_im'], G['s5_log_dt'], G['s5_b_re'], G['s5_b_im']) = s5_prep_bwd(
        W['s5_a_re'], W['s5_a_im'], W['s5_log_dt'], W['s5_b_re'], W['s5_b_im'],
        spread(d_ab[0]), spread(d_ab[1]), take(d_bb[0]), take(d_bb[1]), sel)
    dv = matmul(dyb[None], W['conv_wout'][None], tb=True, name="conv_out_dx")[0]
    G['conv_wout'] = matmul(sv['vconv'][None], dyb[None], ta=True, out_dtype=WGRAD, name="conv_out_dw")[0]
    dxbar, dbg, dcg, G['conv_w'], G['conv_b'] = conv_bwd(dv, proj, W['conv_w'], W['conv_b'], lp)
    do = matmul(dya[None], W['mla_wo'], tb=True, bb='o', out_dtype=BF16, name="mla_out_dx")
    G['mla_wo'] = matmul(sv['o'], dya[None], ta=True, ab='o', out_dtype=WGRAD, name="mla_out_dw")
    dqn, dqr, dkn, dvv, dkr = attn_bwd(sv['qn'], sv['qr'], sv['kn'], sv['v'], sv['kr'], do, lp)
    dcq, dckv, dqrp = mla_heads_bwd(dqn, dqr, dkn, dvv, cos, sin, rot, W['wqn'], W['wqr'], W['wkn'], W['wv'], lp)
    G['wqn'] = matmul(sv['cqn'][None], dqn, ta=True, bb='o', out_dtype=WGRAD, name="mla_dwqn")
    G['wqr'] = matmul(sv['cqn'][None], dqrp, ta=True, bb='o', out_dtype=WGRAD, name="mla_dwqr")
    G['wkn'] = matmul(sv['ckvn'][None], dkn, ta=True, bb='o', out_dtype=WGRAD, name="mla_dwkn")
    G['wv'] = matmul(sv['ckvn'][None], dvv, ta=True, bb='o', out_dtype=WGRAD, name="mla_dwv")
    dcq_raw, dckv_raw, dkr_raw, G['q_norm_g'], G['kv_norm_g'] = mla_prep_bwd(
        dcq, dckv, dkr, proj, cos, sin, rot, W['q_norm_g'], W['kv_norm_g'], lp)
    zeros = lambda n: jnp.zeros((lp, n), BF16)
    dproj = jnp.concatenate([dcq_raw, dkr_raw, zeros(96), dckv_raw, zeros(256), dxbar, dbg, dcg, du_b, dg0, dg1, dg2], axis=1)
    dh1 = matmul(dproj[None], W['w_in'][None], res=dz2[None], res_scale=ALPHA, name="proj_dx")[0]
    G['w_in'] = matmul(dproj[None], sv['h1b'][None], ta=True, out_dtype=WGRAD, name="proj_dw")[0]
    if not ffn1:
        return dh1, G
    dh0, g1 = ffn_bwd(dh1, sv['sv1'], W['wg1'], W['wu1'], W['wd1'], W['ln1_g'], lp)
    G.update(wg1=g1['wg'], wu1=g1['wu'], wd1=g1['wd'], ln1_g=g1['ln_g'], ln1_b=g1['ln_b'])
    return dh0, G


def _nat_cols(st):
    return jnp.transpose(st, (1, 0, 2)).reshape(st.shape[1], -1)


def _shard_cols(nat):
    k, n = nat.shape
    return jnp.transpose(nat.reshape(k, N_SHARD, n // N_SHARD), (1, 0, 2))


def _win_pad(wt):
    z = lambda n: jnp.zeros((n, wt.shape[1]), wt.dtype)
    return jnp.concatenate([wt[0:384], wt[640:672], z(96), wt[384:640], z(256), wt[672:]], axis=0)


def _win_unpad(wp):
    return jnp.concatenate([wp[0:384], wp[512:768], wp[384:416], wp[1024:]], axis=0)


_BIG = [('ffn1_w_gate', 'T'), ('ffn1_w_up', 'T'), ('ffn1_w_down', 0), ('w_in', 'T'), ('mla_w_uq', 1), ('mla_w_ukv', 1),
        ('mla_w_o', 1), ('conv_w_out', 1), ('s5_w_glu', 0), ('s5_w_out', 1), ('w_o', 0),
        ('ffn2_w_gate', 'T'), ('ffn2_w_up', 'T'), ('ffn2_w_down', 0)]
_REPL = ['ln1_g', 'ln1_b', 'mla_q_norm_g', 'mla_kv_norm_g', 'conv_b', 's5_a_re', 's5_a_im', 's5_log_dt', 's5_b_re',
         's5_b_im', 's5_c_re', 's5_c_im', 's5_d', 's5_b_glu', 'ln2_g', 'ln2_b', 'ln3_g', 'ln3_b']


def compute_weights(st, small):
    W = {}
    for t in ('1', '2'):
        if 'ffn%s_w_gate' % t in st:
            W['wg' + t], W['wu' + t], W['wd' + t] = (st['ffn%s_w_%s' % (t, p)] for p in ('gate', 'up', 'down'))
    if 'w_in' in st:
        W.update(_mixer_weights(st))
    if small is not None:
        W.update(_small_weights(small))
    return W


def _mixer_weights(st):
    W = {}
    W['w_in'] = _win_pad(st['w_in'].reshape(D_IN, D_MODEL))
    uq = jnp.transpose(_nat_cols(st['mla_w_uq']).reshape(Q_RANK, N_HEADS, D_NOPE + D_ROPE), (1, 0, 2))
    W['wqn'], W['wqr'] = uq[:, :, :D_NOPE], uq[:, :, D_NOPE:]
    ukv = jnp.transpose(_nat_cols(st['mla_w_ukv']).reshape(KV_RANK, N_HEADS, D_NOPE + D_V), (1, 0, 2))
    W['wkn'], W['wv'] = ukv[:, :, :D_NOPE], ukv[:, :, D_NOPE:]
    W['mla_wo'] = _nat_cols(st['mla_w_o']).reshape(N_HEADS, D_V, D_MODEL)
    W['conv_wout'] = _nat_cols(st['conv_w_out'])
    W['s5_wglu'] = st['s5_w_glu'].reshape(MIX, MIX)
    W['s5_wout'] = _nat_cols(st['s5_w_out'])
    W['w_o'] = st['w_o'].reshape(D_MODEL, D_MODEL)
    return W


def _small_weights(small):
    W = {}
    W['conv_w'] = small['conv_w']
    for n in ('ln1_g', 'ln1_b', 'ln2_g', 'ln2_b', 'ln3_g', 'ln3_b', 'conv_b', 's5_b_glu'):
        W[n] = small[n].reshape(1, -1)
    W['q_norm_g'] = small['mla_q_norm_g'].reshape(1, -1)
    W['kv_norm_g'] = small['mla_kv_norm_g'].reshape(1, -1)
    W['s5_d'] = small['s5_d'].reshape(1, MIX)
    rep = lambda t: jnp.repeat(t, S5_GROUP, axis=0)
    W['s5_a_re'], W['s5_a_im'] = rep(small['s5_a_re']), rep(small['s5_a_im'])
    W['s5_log_dt'] = jnp.broadcast_to(rep(small['s5_log_dt'].reshape(S5_GROUPS, 1)), (_S5_ROWS, S5_STATE))
    tr = lambda t: jnp.transpose(t, (0, 2, 1)).reshape(_S5_ROWS, S5_STATE)
    W['s5_b_re'], W['s5_b_im'] = tr(small['s5_b_re']), tr(small['s5_b_im'])
    ct = lambda t: _blockdiag(jnp.transpose(t, (0, 2, 1)))
    W['s5_ct'] = jnp.stack([ct(small['s5_c_re']), -ct(small['s5_c_im'])]).astype(BF16)
    return W


def reference_grads(G, ffn=True):
    R = {}
    for t in ('1', '2') if ffn else ():
        R['ffn%s_w_gate' % t] = G['wg' + t].reshape(D_FF, D_MODEL).T
        R['ffn%s_w_up' % t] = G['wu' + t].reshape(D_FF, D_MODEL).T
        R['ffn%s_w_down' % t] = G['wd' + t].reshape(D_FF, D_MODEL)
    R['w_in_t'] = _win_unpad(G['w_in'])
    if ffn:
        R['w_in'] = R['w_in_t'].T
    R['mla_w_uq'] = jnp.transpose(jnp.concatenate([G['wqn'], G['wqr']], axis=2), (1, 0, 2)).reshape(Q_RANK, -1)
    R['mla_w_ukv'] = jnp.transpose(jnp.concatenate([G['wkn'], G['wv']], axis=2), (1, 0, 2)).reshape(KV_RANK, -1)
    R['mla_w_o'] = G['mla_wo'].reshape(N_HEADS * D_V, D_MODEL)
    R['conv_w'], R['conv_w_out'] = G['conv_w'], G['conv_wout']
    R['s5_w_glu'], R['s5_w_out'], R['w_o'] = G['s5_wglu'], G['s5_wout'], G['w_o']
    for n in ('ln1_g', 'ln1_b', 'ln2_g', 'ln2_b', 'ln3_g', 'ln3_b', 'conv_b', 's5_b_glu'):
        if n in G:
            R[n] = G[n].reshape(-1)
    R['mla_q_norm_g'], R['mla_kv_norm_g'] = G['q_norm_g'].reshape(-1), G['kv_norm_g'].reshape(-1)
    R['s5_d'] = G['s5_d'].reshape(S5_GROUPS, S5_GROUP)
    R['s5_a_re'], R['s5_a_im'], R['s5_log_dt'] = G['s5_a_re'], G['s5_a_im'], G['s5_log_dt'].reshape(-1)
    untr = lambda t: jnp.transpose(t.reshape(S5_GROUPS, S5_GROUP, S5_STATE), (0, 2, 1))
    R['s5_b_re'], R['s5_b_im'] = untr(G['s5_b_re']), untr(G['s5_b_im'])
    unct = lambda t: jnp.transpose(_blockdiag_extract(t, S5_STATE, S5_GROUP), (0, 2, 1))
    R['s5_c_re'], R['s5_c_im'] = unct(G['s5_ct'][0]), -unct(G['s5_ct'][1])
    return R


_ANY = pl.BlockSpec(memory_space=pl.ANY)
LANES = 1024


def _place():
    x, y, c = lax.axis_index("x"), lax.axis_index("y"), lax.axis_index("c")
    chips = [(1 - x, y), (x, 1 - y), (1 - x, 1 - y)]
    return x, y, c, chips


def _rows_of(c, half):
    return pl.ds(pl.multiple_of(c * half, 8), half)


def all_gather_shards(srcs, exact):
    n, m = len(srcs), len(exact)
    halves = [s.shape[0] // 2 for s in srcs]

    def body(*refs):
        s_refs, e_refs = refs[:n], refs[n:n + m]
        o_refs, eo_refs = refs[n + m:2 * n + m], refs[2 * n + m:2 * n + 2 * m]
        send, recv, esend, erecv, osend, orecv, lsem = refs[2 * n + 2 * m:]
        x, y, c, chips = _place()
        me = 2 * x + y
        sibling = (x, y, 1 - c)
        own = [pltpu.make_async_remote_copy(src_ref=s_refs[k], dst_ref=o_refs[k].at[me], send_sem=osend.at[k],
                                            recv_sem=orecv.at[k], device_id=sibling, device_id_type=MESH) for k in range(n)]
        local = [pltpu.make_async_copy(e_refs[k], eo_refs[k].at[me], lsem.at[k]) for k in range(m)]
        for cp in own + local:
            cp.start()

        def copy(k, s, src, idx, half_c, to):
            return pltpu.make_async_remote_copy(
                src_ref=src, dst_ref=o_refs[k].at[idx, _rows_of(half_c, halves[k])], send_sem=send.at[6 * k + s],
                recv_sem=recv.at[6 * k + s], device_id=to, device_id_type=MESH)

        def ecopy(k, j, idx, to):
            return pltpu.make_async_remote_copy(src_ref=e_refs[k], dst_ref=eo_refs[k].at[idx], send_sem=esend.at[3 * k + j],
                                                recv_sem=erecv.at[3 * k + j], device_id=to, device_id_type=MESH)

        sends = []
        for k in range(n):
            mine = s_refs[k].at[_rows_of(c, halves[k])]
            sends += [copy(k, j, mine, me, c, (*chip, c)) for j, chip in enumerate(chips)]
        for k in range(m):
            sends += [ecopy(k, j, me, (*chip, c)) for j, chip in enumerate(chips)]
        for cp in sends:
            cp.start()
        for j, chip in enumerate(chips):
            idx = 2 * chip[0] + chip[1]
            for k in range(n):
                landed = o_refs[k].at[idx, _rows_of(c, halves[k])]
                copy(k, j, landed, idx, c, sibling).wait_recv()
                fwd = copy(k, 3 + j, landed, idx, c, sibling)
                fwd.start()
                sends.append(fwd)
        for j, chip in enumerate(chips):
            idx = 2 * chip[0] + chip[1]
            for k in range(n):
                copy(k, 3 + j, s_refs[k].at[_rows_of(c, halves[k])], idx, 1 - c, sibling).wait_recv()
            for k in range(m):
                ecopy(k, j, idx, sibling).wait_recv()
        for cp in sends:
            cp.wait_send()
        for cp in own + local:
            cp.wait()

    outs = pl.pallas_call(
        body, name="all_gather_weights", in_specs=[_ANY] * (n + m), out_specs=[_ANY] * (n + m),
        out_shape=[jax.ShapeDtypeStruct((N_SHARD,) + a.shape, a.dtype) for a in list(srcs) + list(exact)],
        scratch_shapes=[pltpu.SemaphoreType.DMA((6 * n,)), pltpu.SemaphoreType.DMA((6 * n,)),
                        pltpu.SemaphoreType.DMA((3 * m,)), pltpu.SemaphoreType.DMA((3 * m,)),
                        pltpu.SemaphoreType.DMA((n,)), pltpu.SemaphoreType.DMA((n,)), pltpu.SemaphoreType.DMA((m,))],
    )(*srcs, *exact)
    return outs[:n], outs[n:]


def rs_pair_swap(gs):
    n = len(gs)

    def body(*refs):
        g_refs, r_refs, send, recv = refs[:n], refs[n:2 * n], refs[2 * n], refs[2 * n + 1]
        x, y, c, _ = _place()
        copies = [pltpu.make_async_remote_copy(
            src_ref=g_refs[k].at[pl.ds(0, N_SHARD), _rows_of(1 - c, gs[k].shape[1] // 2)], dst_ref=r_refs[k],
            send_sem=send.at[k], recv_sem=recv.at[k], device_id=(x, y, 1 - c), device_id_type=MESH) for k in range(n)]
        for cp in copies:
            cp.start()
        for cp in copies:
            cp.wait()

    return pl.pallas_call(
        body, name="grad_pair_swap", in_specs=[_ANY] * n, out_specs=[_ANY] * n,
        out_shape=[jax.ShapeDtypeStruct((N_SHARD, g.shape[1] // 2, g.shape[2]), g.dtype) for g in gs],
        scratch_shapes=[pltpu.SemaphoreType.DMA((n,)), pltpu.SemaphoreType.DMA((n,))],
    )(*gs)


def _group_tile(half, n_cols, n_arrays):
    budget = (20 * 2 ** 20) // (6 * n_arrays)
    fits = [t for t in range(8, half + 1, 8) if half % t == 0 and t * n_cols * 4 <= budget]
    return max(fits) if fits else 8


def rs_pair_add(gs, rs, cidx, out_dtype, name):
    n = len(gs)
    _, K, cols = gs[0].shape
    half = K // 2
    tr = _group_tile(half, cols, n)
    nb = half // tr

    def body(c_ref, *refs):
        for g_ref, r_ref, o_ref in zip(refs[:n], refs[n:2 * n], refs[2 * n:]):
            o_ref[...] = (g_ref[...].astype(F32) + r_ref[...].astype(F32)).astype(out_dtype)

    gspec = pl.BlockSpec((None, tr, cols), lambda j, i, c: (j, c[0] * nb + i, 0))
    rspec = pl.BlockSpec((None, tr, cols), lambda j, i, c: (j, i, 0))
    return pl.pallas_call(
        body, name=name,
        grid_spec=pltpu.PrefetchScalarGridSpec(num_scalar_prefetch=1, grid=(N_SHARD, nb), in_specs=[gspec] * n + [rspec] * n,
                                               out_specs=[rspec] * n),
        out_shape=[jax.ShapeDtypeStruct((N_SHARD, half, cols), out_dtype)] * n,
        compiler_params=_cparams(),
    )(cidx, *gs, *rs)


def rs_chip_scatter(ps):
    n = len(ps)

    def body(*refs):
        p_refs, q_refs, send, recv, lsem = refs[:n], refs[n:2 * n], refs[2 * n], refs[2 * n + 1], refs[2 * n + 2]
        x, y, c, chips = _place()
        me = 2 * x + y
        local = [pltpu.make_async_copy(p_refs[k].at[me], q_refs[k].at[me], lsem.at[k]) for k in range(n)]
        copies = [pltpu.make_async_remote_copy(
            src_ref=p_refs[k].at[2 * chip[0] + chip[1]], dst_ref=q_refs[k].at[me], send_sem=send.at[3 * k + j],
            recv_sem=recv.at[3 * k + j], device_id=(*chip, c), device_id_type=MESH)
            for k in range(n) for j, chip in enumerate(chips)]
        for cp in local + copies:
            cp.start()
        for cp in copies:
            cp.wait()
        for cp in local:
            cp.wait()

    return pl.pallas_call(
        body, name="grad_chip_scatter", in_specs=[_ANY] * n, out_specs=[_ANY] * n,
        out_shape=[jax.ShapeDtypeStruct(p.shape, p.dtype) for p in ps],
        scratch_shapes=[pltpu.SemaphoreType.DMA((3 * n,)), pltpu.SemaphoreType.DMA((3 * n,)), pltpu.SemaphoreType.DMA((n,))],
    )(*ps)


def rs_chip_sum(qs, nl, cidx, name):
    n = len(qs)
    _, half, cols = qs[0].shape
    tr = _group_tile(half, cols, n)
    nb = half // tr

    def body(c_ref, *refs):
        for k, q_ref in enumerate(refs[:n]):
            o_ref = refs[n + k // nl]
            o_ref[k % nl] = ((q_ref[0].astype(F32) + q_ref[1].astype(F32)) + q_ref[2].astype(F32)) + q_ref[3].astype(F32)

    return pl.pallas_call(
        body, name=name,
        grid_spec=pltpu.PrefetchScalarGridSpec(
            num_scalar_prefetch=1, grid=(nb,),
            in_specs=[pl.BlockSpec((N_SHARD, tr, cols), lambda i, c: (0, i, 0))] * n,
            out_specs=[pl.BlockSpec((nl, tr, cols), lambda i, c: (0, c[0] * nb + i, 0))] * (n // nl)),
        out_shape=[jax.ShapeDtypeStruct((nl, 2 * half, cols), F32)] * (n // nl),
        compiler_params=_cparams(),
    )(cidx, *qs)


def rs_pair_gather(fs):
    n = len(fs)

    def body(*refs):
        f_refs, send, recv = refs[n:2 * n], refs[2 * n], refs[2 * n + 1]
        x, y, c, _ = _place()
        copies = []
        for k in range(n):
            rows = f_refs[k].at[pl.ds(0, fs[k].shape[0]), _rows_of(c, fs[k].shape[1] // 2)]
            copies.append(pltpu.make_async_remote_copy(src_ref=rows, dst_ref=rows, send_sem=send.at[k], recv_sem=recv.at[k],
                                                       device_id=(x, y, 1 - c), device_id_type=MESH))
        for cp in copies:
            cp.start()
        for cp in copies:
            cp.wait()

    return pl.pallas_call(
        body, name="grad_pair_gather", in_specs=[_ANY] * n, out_specs=[_ANY] * n,
        out_shape=[jax.ShapeDtypeStruct(f.shape, f.dtype) for f in fs],
        input_output_aliases={k: k for k in range(n)},
        scratch_shapes=[pltpu.SemaphoreType.DMA((n,)), pltpu.SemaphoreType.DMA((n,))],
    )(*fs)


_HBM = pl.BlockSpec(memory_space=pltpu.HBM)
_SEM = pl.BlockSpec(memory_space=pltpu.SEMAPHORE)
_EFFECT = pltpu.SideEffectType.DATAFLOW_SIDE_EFFECTING


def _in_hbm(a):
    return pltpu.with_memory_space_constraint(a, pltpu.HBM)


def split_start(name, srcs, lands, after, copies_fn, n_copies):
    n = len(srcs)

    def body(*refs):
        for cp in copies_fn(refs[:n], refs[n:2 * n], refs[2 * n + 1], refs[2 * n + 2]):
            cp.start()
        refs[-1][...] = jnp.zeros_like(refs[-1])

    bufs = list(srcs) + list(lands)
    outs = pl.pallas_call(
        body, name=name,
        out_shape=(pltpu.SemaphoreType.DMA((n_copies,)), pltpu.SemaphoreType.DMA((n_copies,)),
                   *[pltpu.HBM(a.shape, a.dtype) for a in bufs], jax.ShapeDtypeStruct((8, 128), F32)),
        in_specs=[_HBM] * (2 * n) + [_ANY],
        out_specs=(_SEM, _SEM, *[_HBM] * (2 * n), pl.BlockSpec(memory_space=pltpu.VMEM)),
        input_output_aliases={i: 2 + i for i in range(2 * n)},
        compiler_params=pltpu.CompilerParams(has_side_effects=_EFFECT),
    )(*[_in_hbm(a) for a in bufs], after)
    return outs[0], outs[1], outs[2:2 + n], outs[2 + n:2 + 2 * n], outs[-1]


def split_wait(name, send, recv, srcs, lands, after, copies_fn, which=None):
    n = len(srcs)

    def body(*refs):
        copies = copies_fn(refs[:n], refs[n:2 * n], refs[2 * n], refs[2 * n + 1])
        per = len(copies) // n
        if which is not None:
            copies = [cp for k in which for cp in copies[k * per:(k + 1) * per]]
        for cp in copies:
            cp.wait_send()
        for cp in copies:
            cp.wait_recv()

    bufs = list(srcs) + list(lands)
    outs = pl.pallas_call(
        body, name=name, out_shape=tuple(pltpu.HBM(a.shape, a.dtype) for a in bufs),
        in_specs=[_HBM] * (2 * n) + [_SEM, _SEM, _ANY], out_specs=tuple([_HBM] * (2 * n)),
        input_output_aliases={i: i for i in range(2 * n)},
        compiler_params=pltpu.CompilerParams(has_side_effects=_EFFECT),
    )(*bufs, send, recv, after)
    return list(outs[:n]), list(outs[n:])


def _gather_copies(s_refs, l_refs, send, recv):
    x, y, c, chips = _place()
    me = 2 * x + y
    out = []
    for k, (s, l) in enumerate(zip(s_refs, l_refs)):
        rows = _rows_of(c, s.shape[0] // 2)
        for j, chip in enumerate(chips):
            out.append(pltpu.make_async_remote_copy(src_ref=s.at[rows], dst_ref=l.at[me, rows], send_sem=send.at[4 * k + j],
                                                    recv_sem=recv.at[4 * k + j], device_id=(*chip, c), device_id_type=MESH))
        out.append(pltpu.make_async_remote_copy(src_ref=s, dst_ref=l.at[me], send_sem=send.at[4 * k + 3],
                                                recv_sem=recv.at[4 * k + 3], device_id=(x, y, 1 - c), device_id_type=MESH))
    return out


def _scatter_copies(s_refs, l_refs, send, recv):
    x, y, c, chips = _place()
    me = 2 * x + y
    return [pltpu.make_async_remote_copy(src_ref=s.at[2 * chip[0] + chip[1]], dst_ref=l.at[me], send_sem=send.at[3 * k + j],
                                         recv_sem=recv.at[3 * k + j], device_id=(*chip, c), device_id_type=MESH)
            for k, (s, l) in enumerate(zip(s_refs, l_refs)) for j, chip in enumerate(chips)]


def gather_forward(lands):
    n = len(lands)

    def body(*refs):
        l_refs, send, recv = refs[n:2 * n], refs[2 * n], refs[2 * n + 1]
        x, y, c, chips = _place()
        copies = []
        for k in range(n):
            rows = _rows_of(c, lands[k].shape[1] // 2)
            for j, chip in enumerate(chips):
                part = l_refs[k].at[2 * chip[0] + chip[1], rows]
                copies.append(pltpu.make_async_remote_copy(src_ref=part, dst_ref=part, send_sem=send.at[3 * k + j],
                                                           recv_sem=recv.at[3 * k + j], device_id=(x, y, 1 - c),
                                                           device_id_type=MESH))
        for cp in copies:
            cp.start()
        for cp in copies:
            cp.wait()

    return pl.pallas_call(
        body, name="gather_forward", in_specs=[_ANY] * n, out_specs=[_ANY] * n,
        out_shape=[jax.ShapeDtypeStruct(a.shape, a.dtype) for a in lands],
        input_output_aliases={k: k for k in range(n)},
        scratch_shapes=[pltpu.SemaphoreType.DMA((3 * n,)), pltpu.SemaphoreType.DMA((3 * n,))],
    )(*lands)


def rs_partials(gs, wire, cidx, tag):
    rs = rs_pair_swap(gs)
    groups = {}
    for k, g in enumerate(gs):
        groups.setdefault((g.shape, jnp.dtype(wire[k]).name), []).append(k)
    ps = [None] * len(gs)
    for gi, ks in enumerate(groups.values()):
        outs = rs_pair_add([gs[k] for k in ks], [rs[k] for k in ks], cidx, wire[ks[0]], "grad_pair_add_%s%d" % (tag, gi))
        for k, o in zip(ks, outs):
            ps[k] = o
    return ps


def rs_finish(items):
    cidx = lax.axis_index("c").astype(jnp.int32).reshape(1)
    groups = {}
    for i, it in enumerate(items):
        groups.setdefault((it[0].shape, len(it), it[0].dtype.name), []).append(i)
    fs = [None] * len(items)
    for gi, ids in enumerate(groups.values()):
        outs = rs_chip_sum([q for i in ids for q in items[i]], len(items[ids[0]]), cidx, "grad_chip_sum_%d" % gi)
        for i, o in zip(ids, outs):
            fs[i] = o
    return rs_pair_gather(fs)


def adamw(w, g, m, v, name):
    shape = w.shape
    if w.ndim == 2:
        block, grid, index = shape, (1,), (lambda i: (0, 0))
    else:
        slab = shape[2:]
        unit = 4 * int(np.prod(slab[:-2] or (1,))) * (-(-slab[-1] // 128) * 128)
        if len(slab) >= 2:
            unit *= -(-slab[-2] // 8) * 8
        k = shape[1]
        tr = k
        if k * unit > 2 ** 21:
            tr = max(t for t in range(8, k, 8) if k % t == 0 and t * unit <= 2 ** 21)
        block, grid = (None, tr) + tuple(slab), (shape[0], k // tr)
        index = lambda l, i: (l, i) + (0,) * len(slab)
        if tr < min(k, 64) and len(slab) == 1:
            tc = max(t for t in range(128, slab[0] + 1, 128) if slab[0] % t == 0 and k * t * 4 <= 2 ** 21)
            block, grid = (None, k, tc), (shape[0], slab[0] // tc)
            index = lambda l, i: (l, 0, i)

    def body(w_ref, g_ref, m_ref, v_ref, d_ref, nm_ref, nv_ref):
        g_ = g_ref[...]
        m_new = ADAM_B1 * m_ref[...] + (1.0 - ADAM_B1) * g_
        v_new = ADAM_B2 * v_ref[...] + (1.0 - ADAM_B2) * (g_ * g_)
        m_hat = m_new / (1.0 - ADAM_B1 ** ADAM_STEP)
        v_hat = v_new / (1.0 - ADAM_B2 ** ADAM_STEP)
        d_ref[...] = -ADAM_LR * (m_hat / (jnp.sqrt(v_hat) + ADAM_EPS) + ADAM_WD * w_ref[...])
        nm_ref[...] = m_new
        nv_ref[...] = v_new

    spec = pl.BlockSpec(block, index)
    return pl.pallas_call(
        body, name=name, grid=grid, in_specs=[spec] * 4, out_specs=[spec] * 3,
        out_shape=[jax.ShapeDtypeStruct(shape, F32)] * 3, compiler_params=_cparams(),
    )(w, g, m, v)


_WEIGHTS = ['meta', 'ffn1_w_gate', 'ffn1_w_up', 'ffn1_w_down', 'ln1_g', 'ln1_b', 'w_in', 'mla_q_norm_g', 'mla_w_uq',
            'mla_kv_norm_g', 'mla_w_ukv', 'mla_w_o', 'conv_w', 'conv_b', 'conv_w_out', 's5_a_re', 's5_a_im', 's5_log_dt',
            's5_b_re', 's5_b_im', 's5_c_re', 's5_c_im', 's5_d', 's5_w_glu', 's5_b_glu', 's5_w_out', 'w_o', 'ln2_g', 'ln2_b',
            'ffn2_w_gate', 'ffn2_w_up', 'ffn2_w_down', 'ln3_g', 'ln3_b']


def _pad_to(flat, n):
    return jnp.concatenate([flat, jnp.zeros((n - flat.shape[0],), flat.dtype)])


def _shard_of(full, axis):
    if axis == 1:
        return _shard_cols(full)
    if axis == 'T':
        return full.T.reshape(N_SHARD, full.shape[1] // N_SHARD, full.shape[0])
    return full.reshape(N_SHARD, full.shape[0] // N_SHARD, full.shape[1])


_FFN_KEY = {'gate': 'wg', 'up': 'wu', 'down': 'wd'}


def _pad_rows(a, axis):
    k = a.shape[axis]
    extra = -k % 32
    if not extra:
        return a
    return jnp.pad(a, [(0, extra) if d == axis else (0, 0) for d in range(a.ndim)])


def _step(env):
    w = {n: env[n] for n in _WEIGHTS}
    mom = {n: env['m_' + n] for n in _WEIGHTS}
    var = {n: env['v_' + n] for n in _WEIGHTS}
    cidx = lax.axis_index("c").astype(jnp.int32).reshape(1)
    chip = 2 * lax.axis_index("x") + lax.axis_index("y")
    big_names = [n for n, _ in _BIG]
    nb = len(big_names)

    kept_t = [n for n, a in _BIG if a == 'T']
    own = {n: (jnp.swapaxes(w[n], 1, 2) if n in kept_t else w[n]) for n in big_names}
    first = [n for n in big_names if n.startswith('ffn1')]
    rest = [n for n in big_names if not n.startswith('ffn1')]
    nr = len(rest)
    src = lambda n, li: _pad_rows(own[n][li].astype(BF16), 0)
    gathered_first, (conv_w_st, meta_st) = all_gather_shards([src(n, 0) for n in first], [w['conv_w'], w['meta']])
    later = [src(n, 0) for n in rest] + [src(n, 1) for n in big_names]
    lands = [lax.empty((N_SHARD,) + s.shape, BF16) for s in later]
    g_send, g_recv, later_t, lands_t, token = split_start("gather_start", later, lands, gathered_first[0], _gather_copies,
                                                          4 * len(later))

    def weights_of(names, st, li, with_small):
        small = None
        if with_small:
            small = {n: w[n][li] for n in _REPL}
            small['conv_w'] = _nat_cols(conv_w_st[:, li])
        return compute_weights({n: a[:, :own[n].shape[1]] for n, a in zip(names, st)}, small)

    x2d = env['x'][0]
    lp = x2d.shape[0] + X0
    tabs = _rope_tables(lp)
    h = jnp.concatenate([jnp.zeros((PAD, D_MODEL), F32), _nat_cols(meta_st), x2d], axis=0) + token[0, 0]
    W0 = weights_of(first, gathered_first, 0, True)
    ffn1 = ffn_fwd(h, h.astype(BF16), W0['wg1'], W0['wu1'], W0['wd1'], W0['ln1_g'], W0['ln1_b'], lp)
    later_t, lands_t = split_wait("gather0_wait", g_send, g_recv, later_t, lands_t, ffn1[0], _gather_copies, range(nr))
    W0.update(weights_of(rest, gather_forward(lands_t[:nr]), 0, False))
    h, hb, sv0 = layer_fwd(None, None, W0, tabs, lp, ffn1=ffn1)
    _, lands_t = split_wait("gather1_wait", g_send, g_recv, later_t, lands_t, h, _gather_copies, range(nr, len(later)))
    W1 = weights_of(big_names, gather_forward(lands_t[nr:]), 1, True)
    h, hb, sv1 = layer_fwd(h, hb, W1, tabs, lp)
    tgt = jnp.concatenate([jnp.zeros((X0, D_MODEL), F32), env['loss_target'][0]], axis=0)
    dh, loss_part = loss_head(h, tgt, lp)
    loss = lax.psum(loss_part[0, 0], ("x", "y", "c"))

    def shards(G, names):
        full = reference_grads(G, ffn=False)

        def one(n, a):
            if n.startswith('ffn'):
                return G[_FFN_KEY[n.split('_')[-1]] + n[3]]
            if n == 'w_in':
                return full['w_in_t'].reshape(N_SHARD, D_IN // N_SHARD, D_MODEL)
            return _shard_of(full[n], a)

        return [_pad_rows(one(n, a), 1) for n, a in _BIG if n in names]

    def scatter_start(name, ps, after):
        qs = [lax.dynamic_update_slice_in_dim(jnp.zeros_like(p), lax.dynamic_slice_in_dim(p, chip, 1, axis=0), chip, axis=0)
              for p in ps]
        return split_start(name, ps, qs, after, _scatter_copies, 3 * len(ps))

    dh, G1 = layer_bwd(dh, sv1, W1, tabs, lp)
    p1 = rs_partials(shards(G1, big_names), [BF16] * nb, cidx, "b")
    s1_send, s1_recv, p1_t, q1_t, token1 = scatter_start("scatter1_start", p1, dh)
    dh, G0 = layer_bwd(dh + token1[0, 0], sv0, W0, tabs, lp, ffn1=False)
    p0 = rs_partials(shards(G0, rest), [BF16] * nr, cidx, "c")
    s0_send, s0_recv, p0_t, q0_t, token0 = scatter_start("scatter0_start", p0, dh)
    dh, g1 = ffn_bwd(dh + token0[0, 0], sv0['sv1'], W0['wg1'], W0['wu1'], W0['wd1'], W0['ln1_g'], lp)
    G0.update(wg1=g1['wg'], wu1=g1['wu'], wd1=g1['wd'], ln1_g=g1['ln_g'], ln1_b=g1['ln_b'])
    _, q1 = split_wait("scatter1_wait", s1_send, s1_recv, p1_t, q1_t, dh, _scatter_copies)
    _, q0_rest = split_wait("scatter0_wait", s0_send, s0_recv, p0_t, q0_t, dh, _scatter_copies)
    full = [reference_grads(G0, ffn=False), reference_grads(G1, ffn=False)]

    s_parts = [jnp.stack([full[li][n] for li in range(DEPTH)]).reshape(-1) for n in _REPL + ['conv_w']]
    s_parts.append(dh[PAD:X0].reshape(-1))
    s_sizes = [int(p.shape[0]) for p in s_parts]
    s_rows = -(-sum(s_sizes) // (16 * LANES)) * 16
    g_small = _pad_to(jnp.concatenate(s_parts), s_rows * LANES).reshape(1, s_rows, LANES)
    g_small = jnp.broadcast_to(g_small, (N_SHARD, s_rows, LANES))
    q0_first = rs_chip_scatter(rs_partials(shards(G0, first) + [g_small], [BF16] * len(first) + [F32], cidx, "a"))
    q0 = dict(zip(first + rest, list(q0_first[:-1]) + list(q0_rest)))
    red = rs_finish([[q0[n], q] for n, q in zip(big_names, q1)] + [[q0_first[-1]]])
    f_small = red[-1].reshape(-1)

    grad = {n: r[:, :own[n].shape[1]] for n, r in zip(big_names, red[:nb])}
    off = 0
    for n, sz in zip(_REPL + ['conv_w', 'meta'], s_sizes):
        grad[n] = f_small[off:off + sz]
        off += sz
    for n in _REPL:
        grad[n] = grad[n].reshape(w[n].shape)
    cw = grad['conv_w'].reshape(DEPTH, 3, MIX)
    grad['conv_w'] = lax.dynamic_slice_in_dim(cw, chip * (MIX // N_SHARD), MIX // N_SHARD, axis=2)
    gm = grad['meta'].reshape(N_META, D_MODEL)
    grad['meta'] = lax.dynamic_slice_in_dim(gm, chip * (D_MODEL // N_SHARD), D_MODEL // N_SHARD, axis=1)

    delta, new_m, new_v = {}, {}, {}
    for n in _WEIGHTS:
        if n in kept_t:
            outs = adamw(own[n], grad[n], jnp.swapaxes(mom[n], 1, 2), jnp.swapaxes(var[n], 1, 2), "adamw_" + n)
            grad[n], delta[n], new_m[n], new_v[n] = [jnp.swapaxes(t, 1, 2) for t in [grad[n]] + list(outs)]
        else:
            delta[n], new_m[n], new_v[n] = adamw(w[n], grad[n], mom[n], var[n], "adamw_" + n)
    return (loss, dh[X0:][None], *[grad[n] for n in _WEIGHTS], *[delta[n] for n in _WEIGHTS],
            *[new_m[n] for n in _WEIGHTS], *[new_v[n] for n in _WEIGHTS])


def kernel(x, meta, ffn1_w_gate, ffn1_w_up, ffn1_w_down, ln1_g, ln1_b, w_in, mla_q_norm_g, mla_w_uq, mla_kv_norm_g, mla_w_ukv, mla_w_o, conv_w, conv_b, conv_w_out, s5_a_re, s5_a_im, s5_log_dt, s5_b_re, s5_b_im, s5_c_re, s5_c_im, s5_d, s5_w_glu, s5_b_glu, s5_w_out, w_o, ln2_g, ln2_b, ffn2_w_gate, ffn2_w_up, ffn2_w_down, ln3_g, ln3_b, loss_target, m_meta, m_ffn1_w_gate, m_ffn1_w_up, m_ffn1_w_down, m_ln1_g, m_ln1_b, m_w_in, m_mla_q_norm_g, m_mla_w_uq, m_mla_kv_norm_g, m_mla_w_ukv, m_mla_w_o, m_conv_w, m_conv_b, m_conv_w_out, m_s5_a_re, m_s5_a_im, m_s5_log_dt, m_s5_b_re, m_s5_b_im, m_s5_c_re, m_s5_c_im, m_s5_d, m_s5_w_glu, m_s5_b_glu, m_s5_w_out, m_w_o, m_ln2_g, m_ln2_b, m_ffn2_w_gate, m_ffn2_w_up, m_ffn2_w_down, m_ln3_g, m_ln3_b, v_meta, v_ffn1_w_gate, v_ffn1_w_up, v_ffn1_w_down, v_ln1_g, v_ln1_b, v_w_in, v_mla_q_norm_g, v_mla_w_uq, v_mla_kv_norm_g, v_mla_w_ukv, v_mla_w_o, v_conv_w, v_conv_b, v_conv_w_out, v_s5_a_re, v_s5_a_im, v_s5_log_dt, v_s5_b_re, v_s5_b_im, v_s5_c_re, v_s5_c_im, v_s5_d, v_s5_w_glu, v_s5_b_glu, v_s5_w_out, v_w_o, v_ln2_g, v_ln2_b, v_ffn2_w_gate, v_ffn2_w_up, v_ffn2_w_down, v_ln3_g, v_ln3_b):
    return _step(dict(locals()))
```

```python
import functools
import math

import numpy as np
import jax
import jax.numpy as jnp
from jax import lax
from jax.experimental import pallas as pl
from jax.experimental.pallas import tpu as pltpu

F32 = jnp.float32
BF16 = jnp.bfloat16

D_MODEL = 1024
DEPTH = 2
N_META = 16
PAD = 112
X0 = PAD + N_META
N_HEADS = 8
D_NOPE = 64
D_ROPE = 32
D_V = 64
Q_RANK = 384
KV_RANK = 256
MIX = 512
S5_GROUPS = 32
S5_GROUP = 16
S5_STATE = 64
S5_LANES = S5_GROUPS * S5_STATE
D_FF = 2816
N_SHARD = 4
FF_SHARD = D_FF // N_SHARD
D_IN = 5792
P_IN = 6144
ALPHA = (2.0 * DEPTH) ** 0.25
LN_EPS = 1e-5
RMS_EPS = 1e-6
ATT_SCALE = (D_NOPE + D_ROPE) ** -0.5
ROPE_BASE = 10000.0
ADAM_LR, ADAM_B1, ADAM_B2, ADAM_EPS, ADAM_WD, ADAM_STEP = 0.001, 0.9, 0.999, 1e-08, 0.01, 10
SCAN_CHUNK = 128
VMEM_LIMIT = 52 * 2 ** 20
WGRAD = BF16
MESH = pl.DeviceIdType.MESH


def _cparams(**kw):
    return pltpu.CompilerParams(vmem_limit_bytes=VMEM_LIMIT, **kw)


def _tile(n):
    if n <= 1088:
        return n
    for t in (1024, 544, 512, 272, 256, 128):
        if n % t == 0:
            return t
    return n


def _row_tile(lp):
    for t in (544, 272, 128):
        if lp % t == 0:
            return t
    return lp


def _sigmoid(x):
    return 1.0 / (1.0 + jnp.exp(-x))


_GELU_C = math.sqrt(2.0 / math.pi)


def _gelu(x):
    return 0.5 * x * (1.0 + jnp.tanh(_GELU_C * (x + 0.044715 * x * x * x)))


def _gelu_grad(x):
    t = jnp.tanh(_GELU_C * (x + 0.044715 * x * x * x))
    return 0.5 * (1.0 + t) + 0.5 * x * (1.0 - t * t) * _GELU_C * (1.0 + 3.0 * 0.044715 * x * x)


def _dot(a, b, ca, cb, precision=None):
    return lax.dot_general(a, b, (((ca,), (cb,)), ((), ())), preferred_element_type=F32, precision=precision)


def matmul(a, b, *, name, ta=False, tb=False, ab='n', bb='n', res=None, res_scale=1.0, scale=1.0, out_dtype=F32):
    if ta:
        _, K, M = a.shape
    else:
        _, M, K = a.shape
    if tb:
        _, N, K2 = b.shape
    else:
        _, K2, N = b.shape
    assert K == K2, (a.shape, b.shape)
    n_out = max(a.shape[0] if ab == 'o' else 1, b.shape[0] if bb == 'o' else 1)
    n_red = max(a.shape[0] if ab == 'r' else 1, b.shape[0] if bb == 'r' else 1)
    tm, tn = _tile(M), _tile(N)
    tk = K if K <= 2304 else _tile(K)
    nkt = K // tk
    n_steps = n_red * nkt

    def bsel(mode, o, r):
        if mode == 'o':
            return o
        if mode == 'r':
            return r // nkt if nkt > 1 else r
        return 0

    def ksel(r):
        if nkt == 1:
            return 0
        return r % nkt if n_red > 1 else r

    a_map = (lambda o, i, j, r: (bsel(ab, o, r), ksel(r), i)) if ta else (lambda o, i, j, r: (bsel(ab, o, r), i, ksel(r)))
    b_map = (lambda o, i, j, r: (bsel(bb, o, r), j, ksel(r))) if tb else (lambda o, i, j, r: (bsel(bb, o, r), ksel(r), j))
    o_map = lambda o, i, j, r: (o, i, j)
    in_specs = [pl.BlockSpec((None, tk, tm) if ta else (None, tm, tk), a_map),
                pl.BlockSpec((None, tn, tk) if tb else (None, tk, tn), b_map)]
    operands = [a, b]
    if res is not None:
        in_specs.append(pl.BlockSpec((None, tm, tn), o_map))
        operands.append(res)
    has_res = res is not None

    def body(*refs):
        a_ref, b_ref = refs[0], refs[1]
        res_ref = refs[2] if has_res else None
        o_ref = refs[3] if has_res else refs[2]
        part = _dot(a_ref[...].astype(BF16), b_ref[...].astype(BF16), 0 if ta else 1, 1 if tb else 0)

        def finish(acc):
            v = acc if scale == 1.0 else acc * scale
            if has_res:
                v = v + res_scale * res_ref[...].astype(F32)
            o_ref[...] = v.astype(o_ref.dtype)

        if n_steps == 1:
            finish(part)
        else:
            acc_ref = refs[-1]
            r = pl.program_id(3)

            @pl.when(r == 0)
            def _():
                acc_ref[...] = part

            @pl.when(r > 0)
            def _():
                acc_ref[...] += part

            @pl.when(r == n_steps - 1)
            def _():
                finish(acc_ref[...])

    return pl.pallas_call(
        body, name=name,
        grid=(n_out, M // tm, N // tn, n_steps),
        in_specs=in_specs,
        out_specs=pl.BlockSpec((None, tm, tn), o_map),
        out_shape=jax.ShapeDtypeStruct((n_out, M, N), out_dtype),
        scratch_shapes=[pltpu.VMEM((tm, tn), F32)] if n_steps > 1 else [],
        compiler_params=_cparams(),
    )(*operands)


def rowwise(fn, rows, pars, outs, accs=(), *, name, lp):
    tm = _row_tile(lp)
    n_rows, n_pars, n_outs, n_accs = len(rows), len(pars), len(outs), len(accs)
    in_specs = [pl.BlockSpec((tm, w), functools.partial(lambda i, cb: (i, cb), cb=cb)) for _, w, cb in rows]
    in_specs += [pl.BlockSpec(p.shape, functools.partial(lambda i, nd: (0,) * nd, nd=p.ndim)) for p in pars]
    out_specs = [pl.BlockSpec((tm, w), lambda i: (i, 0)) for w, _ in outs]
    out_specs += [pl.BlockSpec(s, functools.partial(lambda i, nd: (0,) * nd, nd=len(s))) for s, _ in accs]
    out_shape = [jax.ShapeDtypeStruct((lp, w), dt) for w, dt in outs]
    out_shape += [jax.ShapeDtypeStruct(s, dt) for s, dt in accs]

    def body(*refs):
        i = pl.program_id(0)
        rv = [r[...] for r in refs[:n_rows]]
        pv = [r[...] for r in refs[n_rows:n_rows + n_pars]]
        o_refs = refs[n_rows + n_pars:n_rows + n_pars + n_outs]
        a_refs = refs[n_rows + n_pars + n_outs:]
        ov, av = fn(i * tm, rv, pv)
        for r, v in zip(o_refs, ov):
            r[...] = v.astype(r.dtype)
        if n_accs:
            @pl.when(i == 0)
            def _():
                for r, v in zip(a_refs, av):
                    r[...] = v.astype(r.dtype)

            @pl.when(i > 0)
            def _():
                for r, v in zip(a_refs, av):
                    r[...] += v.astype(r.dtype)

    res = pl.pallas_call(
        body, name=name, grid=(lp // tm,), in_specs=in_specs, out_specs=out_specs, out_shape=out_shape,
        compiler_params=_cparams(),
    )(*[r[0] for r in rows], *pars)
    return res


def _row_mask(row0, shape):
    return (row0 + lax.broadcasted_iota(jnp.int32, shape, 0)) >= PAD


def ffn_up(hb, wg, wu, lp):
    tm = _row_tile(lp)

    def body(h_ref, wg_ref, wu_ref, ab_ref, hid_ref):
        h = h_ref[...]
        a = _dot(h, wg_ref[...], 1, 1)
        b = _dot(h, wu_ref[...], 1, 1)
        ab_ref[0] = a.astype(BF16)
        ab_ref[1] = b.astype(BF16)
        hid_ref[...] = (a * _sigmoid(a) * b).astype(BF16)

    wspec = pl.BlockSpec((None, FF_SHARD, D_MODEL), lambda j, i: (j, 0, 0))
    return pl.pallas_call(
        body, name="ffn_up", grid=(N_SHARD, lp // tm),
        in_specs=[pl.BlockSpec((tm, D_MODEL), lambda j, i: (i, 0)), wspec, wspec],
        out_specs=[pl.BlockSpec((None, 2, tm, FF_SHARD), lambda j, i: (j, 0, i, 0)),
                   pl.BlockSpec((None, tm, FF_SHARD), lambda j, i: (j, i, 0))],
        out_shape=[jax.ShapeDtypeStruct((N_SHARD, 2, lp, FF_SHARD), BF16),
                   jax.ShapeDtypeStruct((N_SHARD, lp, FF_SHARD), BF16)],
        compiler_params=_cparams(),
    )(hb, wg, wu)


def _layer_norm(z, g, b):
    mu = jnp.mean(z, axis=-1, keepdims=True)
    zc = z - mu
    var = jnp.mean(zc * zc, axis=-1, keepdims=True)
    return zc * lax.rsqrt(var + LN_EPS) * g + b


def mm_res_ln(a, w, res, g, b, *, scale, name, lp):
    n_red, _, K = a.shape
    tm = _row_tile(lp)

    def body(a_ref, w_ref, res_ref, g_ref, b_ref, z_ref, h_ref, hb_ref, acc_ref):
        r = pl.program_id(1)
        part = _dot(a_ref[...].astype(BF16), w_ref[...], 1, 0)

        @pl.when(r == 0)
        def _():
            acc_ref[...] = part

        @pl.when(r > 0)
        def _():
            acc_ref[...] += part

        @pl.when(r == n_red - 1)
        def _():
            z = ALPHA * res_ref[...] + scale * acc_ref[...]
            z_ref[...] = z
            hn = _layer_norm(z, g_ref[...], b_ref[...])
            h_ref[...] = hn
            hb_ref[...] = hn.astype(BF16)

    row = pl.BlockSpec((tm, D_MODEL), lambda i, r: (i, 0))
    par = pl.BlockSpec((1, D_MODEL), lambda i, r: (0, 0))
    return pl.pallas_call(
        body, name=name, grid=(lp // tm, n_red),
        in_specs=[pl.BlockSpec((None, tm, K), lambda i, r: (r, i, 0)),
                  pl.BlockSpec((None, K, D_MODEL), lambda i, r: (r, 0, 0)), row, par, par],
        out_specs=[row, row, row],
        out_shape=[jax.ShapeDtypeStruct((lp, D_MODEL), F32), jax.ShapeDtypeStruct((lp, D_MODEL), F32),
                   jax.ShapeDtypeStruct((lp, D_MODEL), BF16)],
        scratch_shapes=[pltpu.VMEM((tm, D_MODEL), F32)],
        compiler_params=_cparams(),
    )(a, w, res, g, b)


def ln_bwd(dh, z, g, *, fscale, name, lp):
    def fn(row0, rv, pv):
        dh_, z_ = rv
        g_, = pv
        mu = jnp.mean(z_, axis=-1, keepdims=True)
        zc = z_ - mu
        rstd = lax.rsqrt(jnp.mean(zc * zc, axis=-1, keepdims=True) + LN_EPS)
        xh = zc * rstd
        dxh = dh_ * g_
        m1 = jnp.mean(dxh, axis=-1, keepdims=True)
        m2 = jnp.mean(dxh * xh, axis=-1, keepdims=True)
        dz = rstd * (dxh - m1 - xh * m2)
        return ((dz, fscale * dz),
                (jnp.sum(dh_ * xh, axis=0, keepdims=True), jnp.sum(dh_, axis=0, keepdims=True)))

    return rowwise(fn, [(dh, D_MODEL, 0), (z, D_MODEL, 0)], [g], [(D_MODEL, F32), (D_MODEL, BF16)],
                   [((1, D_MODEL), F32), ((1, D_MODEL), F32)], name=name, lp=lp)


def ffn_down_bwd(dfb, wd, ab, lp):
    tm = _row_tile(lp)

    def body(df_ref, w_ref, ab_ref, da_ref, db_ref):
        dhid = _dot(df_ref[...], w_ref[...], 1, 1)
        a = ab_ref[0].astype(F32)
        b = ab_ref[1].astype(F32)
        sg = _sigmoid(a)
        da_ref[...] = (dhid * b * (sg * (1.0 + a * (1.0 - sg)))).astype(BF16)
        db_ref[...] = (dhid * (a * sg)).astype(BF16)

    ospec = pl.BlockSpec((None, tm, FF_SHARD), lambda j, i: (j, i, 0))
    return pl.pallas_call(
        body, name="ffn_down_bwd", grid=(N_SHARD, lp // tm),
        in_specs=[pl.BlockSpec((tm, D_MODEL), lambda j, i: (i, 0)),
                  pl.BlockSpec((None, FF_SHARD, D_MODEL), lambda j, i: (j, 0, 0)),
                  pl.BlockSpec((None, 2, tm, FF_SHARD), lambda j, i: (j, 0, i, 0))],
        out_specs=[ospec, ospec],
        out_shape=[jax.ShapeDtypeStruct((N_SHARD, lp, FF_SHARD), BF16)] * 2,
        compiler_params=_cparams(),
    )(dfb, wd, ab)


def ffn_dx(da, db, wg, wu, dz, lp):
    tm = _row_tile(lp)

    def body(da_ref, db_ref, wg_ref, wu_ref, dz_ref, o_ref, acc_ref):
        j = pl.program_id(1)
        part = _dot(da_ref[...], wg_ref[...], 1, 0) + _dot(db_ref[...], wu_ref[...], 1, 0)

        @pl.when(j == 0)
        def _():
            acc_ref[...] = part

        @pl.when(j > 0)
        def _():
            acc_ref[...] += part

        @pl.when(j == N_SHARD - 1)
        def _():
            o_ref[...] = acc_ref[...] + ALPHA * dz_ref[...]

    aspec = pl.BlockSpec((None, tm, FF_SHARD), lambda i, j: (j, i, 0))
    wspec = pl.BlockSpec((None, FF_SHARD, D_MODEL), lambda i, j: (j, 0, 0))
    row = pl.BlockSpec((tm, D_MODEL), lambda i, j: (i, 0))
    return pl.pallas_call(
        body, name="ffn_dx", grid=(lp // tm, N_SHARD), in_specs=[aspec, aspec, wspec, wspec, row], out_specs=row,
        out_shape=jax.ShapeDtypeStruct((lp, D_MODEL), F32), scratch_shapes=[pltpu.VMEM((tm, D_MODEL), F32)],
        compiler_params=_cparams(),
    )(da, db, wg, wu, dz)


def ffn_fwd(h, hb, wg, wu, wd, g, b, lp):
    ab, hid = ffn_up(hb, wg, wu, lp)
    z, hn, hnb = mm_res_ln(hid, wd, h, g, b, scale=0.5, name="ffn_down_ln", lp=lp)
    return hn, hnb, dict(hb=hb, ab=ab, hid=hid, z=z)


def ffn_bwd(dh, sv, wg, wu, wd, g, lp):
    dz, dfb, dg, db = ln_bwd(dh, sv['z'], g, fscale=0.5, name="ffn_ln_bwd", lp=lp)
    da, dbb = ffn_down_bwd(dfb, wd, sv['ab'], lp)
    d_wd = matmul(sv['hid'], dfb[None], ta=True, ab='o', out_dtype=WGRAD, name="ffn_dwd")
    d_wg = matmul(da, sv['hb'][None], ta=True, ab='o', out_dtype=WGRAD, name="ffn_dwg")
    d_wu = matmul(dbb, sv['hb'][None], ta=True, ab='o', out_dtype=WGRAD, name="ffn_dwu")
    dh_in = ffn_dx(da, dbb, wg, wu, dz, lp)
    return dh_in, dict(wg=d_wg, wu=d_wu, wd=d_wd, ln_g=dg, ln_b=db)


def _rope_tables(lp):
    pos = np.arange(lp, dtype=np.float32) - PAD
    inv = ROPE_BASE ** (-np.arange(0, D_ROPE, 2, dtype=np.float32) / D_ROPE)
    ang = pos[:, None] * inv[None, :]
    cos = np.concatenate([np.cos(ang), np.cos(ang)], axis=1).astype(np.float32)
    sin = np.concatenate([np.sin(ang), np.sin(ang)], axis=1).astype(np.float32)
    rot = np.zeros((D_ROPE, D_ROPE), np.float32)
    half = D_ROPE // 2
    for j in range(half):
        rot[j + half, j] = -1.0
        rot[j, j + half] = 1.0
    return jnp.asarray(cos), jnp.asarray(sin), jnp.asarray(rot)


def _rot(x, rot):
    return _dot(x, rot, 1, 0, precision=lax.Precision.HIGHEST)


def _rms(x, g):
    r = lax.rsqrt(jnp.mean(x * x, axis=-1, keepdims=True) + RMS_EPS)
    return x * r * g


def mla_prep(proj, cos, sin, rot, qg, kvg, lp):
    def fn(row0, rv, pv):
        cq, krb, ckv, c, s = rv
        qg_, kvg_, rot_ = pv
        kr = krb[:, :D_ROPE]
        return ((_rms(cq, qg_), _rms(ckv, kvg_), kr * c + _rot(kr, rot_) * s), ())

    return rowwise(fn, [(proj, Q_RANK, 0), (proj, 128, 3), (proj, KV_RANK, 2), (cos, D_ROPE, 0), (sin, D_ROPE, 0)],
                   [qg, kvg, rot], [(Q_RANK, BF16), (KV_RANK, BF16), (D_ROPE, BF16)], name="mla_prep", lp=lp)


def mla_heads(cqn, ckvn, cos, sin, rot, wqn, wqr, wkn, wv, lp):
    tm = _row_tile(lp)

    def body(cq_ref, ckv_ref, c_ref, s_ref, rot_ref, wqn_ref, wqr_ref, wkn_ref, wv_ref, qn_ref, qr_ref, kn_ref, v_ref):
        cq = cq_ref[...]
        ckv = ckv_ref[...]
        qn_ref[...] = _dot(cq, wqn_ref[...], 1, 0).astype(BF16)
        qr = _dot(cq, wqr_ref[...], 1, 0)
        qr_ref[...] = (qr * c_ref[...] + _rot(qr, rot_ref[...]) * s_ref[...]).astype(BF16)
        kn_ref[...] = _dot(ckv, wkn_ref[...], 1, 0).astype(BF16)
        v_ref[...] = _dot(ckv, wv_ref[...], 1, 0).astype(BF16)

    def row(w):
        return pl.BlockSpec((tm, w), lambda h, i: (i, 0))

    def wspec(k, n):
        return pl.BlockSpec((None, k, n), lambda h, i: (h, 0, 0))

    def ospec(n):
        return pl.BlockSpec((None, tm, n), lambda h, i: (h, i, 0))

    return pl.pallas_call(
        body, name="mla_heads", grid=(N_HEADS, lp // tm),
        in_specs=[row(Q_RANK), row(KV_RANK), row(D_ROPE), row(D_ROPE),
                  pl.BlockSpec((D_ROPE, D_ROPE), lambda h, i: (0, 0)),
                  wspec(Q_RANK, D_NOPE), wspec(Q_RANK, D_ROPE), wspec(KV_RANK, D_NOPE), wspec(KV_RANK, D_V)],
        out_specs=[ospec(D_NOPE), ospec(D_ROPE), ospec(D_NOPE), ospec(D_V)],
        out_shape=[jax.ShapeDtypeStruct((N_HEADS, lp, D_NOPE), BF16), jax.ShapeDtypeStruct((N_HEADS, lp, D_ROPE), BF16),
                   jax.ShapeDtypeStruct((N_HEADS, lp, D_NOPE), BF16), jax.ShapeDtypeStruct((N_HEADS, lp, D_V), BF16)],
        compiler_params=_cparams(),
    )(cqn, ckvn, cos, sin, rot, wqn, wqr, wkn, wv)


def _att_probs(qn, qr, kn, kr, row0, tq, lp):
    s = (_dot(qn, kn, 1, 1) + _dot(qr, kr, 1, 1)) * ATT_SCALE
    qi = row0 + lax.broadcasted_iota(jnp.int32, (tq, lp), 0)
    ki = lax.broadcasted_iota(jnp.int32, (tq, lp), 1)
    s = jnp.where((ki <= qi) & (ki >= PAD), s, -1e30)
    p = jnp.exp(s - jnp.max(s, axis=-1, keepdims=True))
    return p / jnp.sum(p, axis=-1, keepdims=True)


def _att_specs(lp):
    def hspec(n):
        return pl.BlockSpec((None, lp, n), lambda h, i: (h, 0, 0))

    return hspec, pl.BlockSpec((lp, D_ROPE), lambda h, i: (0, 0))


def _att_tiles(lp):
    tiles, r = [(0, X0)], X0
    while r < lp:
        tiles.append((r, 256))
        r += 256
    assert r == lp
    return tiles


def attn_fwd(qn, qr, kn, v, kr, lp):
    kspec, krspec = _att_specs(lp)

    def body(qn_ref, qr_ref, kn_ref, v_ref, kr_ref, o_ref):
        for r0, rows in _att_tiles(lp):
            ke, q = r0 + rows, slice(r0, r0 + rows)
            p = _att_probs(qn_ref[q, :], qr_ref[q, :], kn_ref[0:ke, :], kr_ref[0:ke, :], r0, rows, ke)
            o_ref[q, :] = _dot(p.astype(BF16), v_ref[0:ke, :], 1, 0).astype(BF16)

    return pl.pallas_call(
        body, name="attn_fwd", grid=(N_HEADS, 1),
        in_specs=[kspec(D_NOPE), kspec(D_ROPE), kspec(D_NOPE), kspec(D_V), krspec],
        out_specs=kspec(D_V), out_shape=jax.ShapeDtypeStruct((N_HEADS, lp, D_V), BF16),
        compiler_params=_cparams(),
    )(qn, qr, kn, v, kr)


def attn_bwd(qn, qr, kn, v, kr, do, lp):
    kspec, krspec = _att_specs(lp)

    def body(qn_ref, qr_ref, kn_ref, v_ref, kr_ref, do_ref, dqn_ref, dqr_ref, dkn_ref, dv_ref, dkr_ref):
        dkn_ref[...] = jnp.zeros_like(dkn_ref)
        dv_ref[...] = jnp.zeros_like(dv_ref)

        @pl.when(pl.program_id(0) == 0)
        def _():
            dkr_ref[...] = jnp.zeros_like(dkr_ref)

        for r0, rows in _att_tiles(lp):
            ke, q = r0 + rows, slice(r0, r0 + rows)
            qn_, qr_, do_ = qn_ref[q, :], qr_ref[q, :], do_ref[q, :]
            kn_, v_, kr_ = kn_ref[0:ke, :], v_ref[0:ke, :], kr_ref[0:ke, :]
            p = _att_probs(qn_, qr_, kn_, kr_, r0, rows, ke)
            dp = _dot(do_, v_, 1, 1)
            delta = jnp.sum(p * dp, axis=-1, keepdims=True)
            ds = (p * (dp - delta) * ATT_SCALE).astype(BF16)
            dqn_ref[q, :] = _dot(ds, kn_, 1, 0).astype(BF16)
            dqr_ref[q, :] = _dot(ds, kr_, 1, 0)
            dkn_ref[0:ke, :] += _dot(ds, qn_, 0, 0)
            dv_ref[0:ke, :] += _dot(p.astype(BF16), do_, 0, 0)
            dkr_ref[0:ke, :] += _dot(ds, qr_, 0, 0)

    return pl.pallas_call(
        body, name="attn_bwd", grid=(N_HEADS, 1),
        in_specs=[kspec(D_NOPE), kspec(D_ROPE), kspec(D_NOPE), kspec(D_V), krspec, kspec(D_V)],
        out_specs=[kspec(D_NOPE), kspec(D_ROPE), kspec(D_NOPE), kspec(D_V), krspec],
        out_shape=[jax.ShapeDtypeStruct((N_HEADS, lp, D_NOPE), BF16), jax.ShapeDtypeStruct((N_HEADS, lp, D_ROPE), F32),
                   jax.ShapeDtypeStruct((N_HEADS, lp, D_NOPE), F32), jax.ShapeDtypeStruct((N_HEADS, lp, D_V), F32),
                   jax.ShapeDtypeStruct((lp, D_ROPE), F32)],
        compiler_params=_cparams(),
    )(qn, qr, kn, v, kr, do)


def mla_heads_bwd(dqn, dqr, dkn, dv, cos, sin, rot, wqn, wqr, wkn, wv, lp):
    tm = _row_tile(lp)

    def body(dqn_ref, dqr_ref, dkn_ref, dv_ref, c_ref, s_ref, rot_ref, wqn_ref, wqr_ref, wkn_ref, wv_ref,
             dcq_ref, dckv_ref, dqrp_ref):
        h = pl.program_id(1)
        dqr_ = dqr_ref[...]
        dqrp = (dqr_ * c_ref[...] - _rot(dqr_ * s_ref[...], rot_ref[...])).astype(BF16)
        dqrp_ref[...] = dqrp
        dcq = _dot(dqn_ref[...], wqn_ref[...], 1, 1) + _dot(dqrp, wqr_ref[...], 1, 1)
        dckv = _dot(dkn_ref[...].astype(BF16), wkn_ref[...], 1, 1) + _dot(dv_ref[...].astype(BF16), wv_ref[...], 1, 1)

        @pl.when(h == 0)
        def _():
            dcq_ref[...] = dcq
            dckv_ref[...] = dckv

        @pl.when(h > 0)
        def _():
            dcq_ref[...] += dcq
            dckv_ref[...] += dckv

    def hspec(n):
        return pl.BlockSpec((None, tm, n), lambda i, h: (h, i, 0))

    def row(w):
        return pl.BlockSpec((tm, w), lambda i, h: (i, 0))

    def wspec(k, n):
        return pl.BlockSpec((None, k, n), lambda i, h: (h, 0, 0))

    return pl.pallas_call(
        body, name="mla_heads_bwd", grid=(lp // tm, N_HEADS),
        in_specs=[hspec(D_NOPE), hspec(D_ROPE), hspec(D_NOPE), hspec(D_V), row(D_ROPE), row(D_ROPE),
                  pl.BlockSpec((D_ROPE, D_ROPE), lambda i, h: (0, 0)),
                  wspec(Q_RANK, D_NOPE), wspec(Q_RANK, D_ROPE), wspec(KV_RANK, D_NOPE), wspec(KV_RANK, D_V)],
        out_specs=[row(Q_RANK), row(KV_RANK), hspec(D_ROPE)],
        out_shape=[jax.ShapeDtypeStruct((lp, Q_RANK), F32), jax.ShapeDtypeStruct((lp, KV_RANK), F32),
                   jax.ShapeDtypeStruct((N_HEADS, lp, D_ROPE), BF16)],
        compiler_params=_cparams(),
    )(dqn, dqr, dkn, dv, cos, sin, rot, wqn, wqr, wkn, wv)


def _rms_bwd(dy, x, g):
    r = lax.rsqrt(jnp.mean(x * x, axis=-1, keepdims=True) + RMS_EPS)
    n = x * r
    dn = dy * g
    dx = r * (dn - n * jnp.mean(dn * n, axis=-1, keepdims=True))
    return dx, jnp.sum(dy * n, axis=0, keepdims=True)


def mla_prep_bwd(dcq, dckv, dkr, proj, cos, sin, rot, qg, kvg, lp):
    def fn(row0, rv, pv):
        dcq_, dckv_, dkr_, cq, ckv, c, s = rv
        qg_, kvg_, rot_ = pv
        dxq, dgq = _rms_bwd(dcq_, cq, qg_)
        dxkv, dgkv = _rms_bwd(dckv_, ckv, kvg_)
        dkr_raw = dkr_ * c - _rot(dkr_ * s, rot_)
        return ((dxq, dxkv, dkr_raw), (dgq, dgkv))

    return rowwise(fn, [(dcq, Q_RANK, 0), (dckv, KV_RANK, 0), (dkr, D_ROPE, 0), (proj, Q_RANK, 0), (proj, KV_RANK, 2),
                        (cos, D_ROPE, 0), (sin, D_ROPE, 0)], [qg, kvg, rot],
                   [(Q_RANK, BF16), (KV_RANK, BF16), (D_ROPE, BF16)], [((1, Q_RANK), F32), ((1, KV_RANK), F32)],
                   name="mla_prep_bwd", lp=lp)


def _shift_down(x, d, rows):
    return jnp.where(rows >= d, pltpu.roll(x, d, 0), 0.0)


def _shift_up(x, d, rows, n):
    return jnp.where(rows < n - d, pltpu.roll(x, n - d, 0), 0.0)


_CONV_W = 128
_XB, _BG, _CG = 1024 // _CONV_W, 1536 // _CONV_W, 2048 // _CONV_W


def _conv_specs(lp):
    def pspec(base):
        return pl.BlockSpec((lp, _CONV_W), functools.partial(lambda c, base: (0, base + c), base=base))

    col = pl.BlockSpec((lp, _CONV_W), lambda c: (0, c))
    wspec = pl.BlockSpec((3, _CONV_W), lambda c: (0, c))
    bspec = pl.BlockSpec((1, _CONV_W), lambda c: (0, c))
    return pspec, col, wspec, bspec


def _conv_core(xbar, cg, w, bias, lp):
    rows = lax.broadcasted_iota(jnp.int32, (lp, _CONV_W), 0)
    u = jnp.where(rows >= PAD, cg * xbar, 0.0)
    u1 = _shift_down(u, 1, rows)
    u2 = _shift_down(u, 2, rows)
    y = bias + w[0:1] * u2 + w[1:2] * u1 + w[2:3] * u
    return rows, u, u1, u2, y


def conv_fwd(proj, w, bias, lp):
    pspec, col, wspec, bspec = _conv_specs(lp)

    def body(x_ref, b_ref, c_ref, w_ref, bias_ref, v_ref):
        _, _, _, _, y = _conv_core(x_ref[...], c_ref[...], w_ref[...], bias_ref[...], lp)
        v_ref[...] = (b_ref[...] * y).astype(BF16)

    return pl.pallas_call(
        body, name="conv_fwd", grid=(MIX // _CONV_W,),
        in_specs=[pspec(_XB), pspec(_BG), pspec(_CG), wspec, bspec], out_specs=col,
        out_shape=jax.ShapeDtypeStruct((lp, MIX), BF16), compiler_params=_cparams(),
    )(proj, proj, proj, w, bias)


def conv_bwd(dv, proj, w, bias, lp):
    pspec, col, wspec, bspec = _conv_specs(lp)

    def body(dv_ref, x_ref, b_ref, c_ref, w_ref, bias_ref, dx_ref, db_ref, dc_ref, dw_ref, dbias_ref):
        xbar, cg, w_ = x_ref[...], c_ref[...], w_ref[...]
        rows, u, u1, u2, y = _conv_core(xbar, cg, w_, bias_ref[...], lp)
        dv_ = dv_ref[...]
        db_ref[...] = (dv_ * y).astype(BF16)
        dy = dv_ * b_ref[...]
        dbias_ref[...] = jnp.sum(dy, axis=0, keepdims=True)
        dw_ref[0:1, :] = jnp.sum(dy * u2, axis=0, keepdims=True)
        dw_ref[1:2, :] = jnp.sum(dy * u1, axis=0, keepdims=True)
        dw_ref[2:3, :] = jnp.sum(dy * u, axis=0, keepdims=True)
        du = w_[2:3] * dy + w_[1:2] * _shift_up(dy, 1, rows, lp) + w_[0:1] * _shift_up(dy, 2, rows, lp)
        du = jnp.where(rows >= PAD, du, 0.0)
        dc_ref[...] = (du * xbar).astype(BF16)
        dx_ref[...] = (du * cg).astype(BF16)

    return pl.pallas_call(
        body, name="conv_bwd", grid=(MIX // _CONV_W,),
        in_specs=[col, pspec(_XB), pspec(_BG), pspec(_CG), wspec, bspec],
        out_specs=[col, col, col, wspec, bspec],
        out_shape=[jax.ShapeDtypeStruct((lp, MIX), BF16)] * 3 + [jax.ShapeDtypeStruct((3, MIX), F32),
                                                                jax.ShapeDtypeStruct((1, MIX), F32)],
        compiler_params=_cparams(),
    )(dv, proj, proj, proj, w, bias)


def _s5_disc(a_re, a_im, log_dt, b_re, b_im):
    dt = jnp.exp(log_dt)
    mag = jnp.exp(dt * a_re)
    ab_re, ab_im = mag * jnp.cos(dt * a_im), mag * jnp.sin(dt * a_im)
    den = a_re * a_re + a_im * a_im
    nr, ni = ab_re - 1.0, ab_im
    coef_re = (nr * a_re + ni * a_im) / den
    coef_im = (ni * a_re - nr * a_im) / den
    return ab_re, ab_im, coef_re * b_re - coef_im * b_im, coef_re * b_im + coef_im * b_re


_S5_ROWS = S5_GROUPS * S5_GROUP


def s5_prep(a_re, a_im, log_dt, b_re, b_im):
    def body(ar, ai, ld, br, bi, o0, o1, o2, o3):
        for o, v in zip((o0, o1, o2, o3), _s5_disc(ar[...], ai[...], ld[...], br[...], bi[...])):
            o[...] = v

    return pl.pallas_call(body, name="s5_prep",
                          out_shape=[jax.ShapeDtypeStruct((_S5_ROWS, S5_STATE), F32)] * 4)(a_re, a_im, log_dt, b_re, b_im)


def s5_prep_bwd(a_re, a_im, log_dt, b_re, b_im, d_ab_re, d_ab_im, d_bb_re, d_bb_im, sel):
    def body(ar, ai, ld, br, bi, g0, g1, g2, g3, sel_ref, da_re, da_im, dld, dbr, dbi):
        _, vjp = jax.vjp(_s5_disc, ar[...], ai[...], ld[...], br[...], bi[...])
        c_ar, c_ai, c_ld, c_br, c_bi = vjp((g0[...], g1[...], g2[...], g3[...]))
        s = sel_ref[...]
        hi = lax.Precision.HIGHEST
        da_re[...] = _dot(s, c_ar, 1, 0, precision=hi)
        da_im[...] = _dot(s, c_ai, 1, 0, precision=hi)
        dld[...] = jnp.sum(_dot(s, c_ld, 1, 0, precision=hi), axis=-1, keepdims=True)
        dbr[...] = c_br
        dbi[...] = c_bi

    g = jax.ShapeDtypeStruct((S5_GROUPS, S5_STATE), F32)
    full = jax.ShapeDtypeStruct((_S5_ROWS, S5_STATE), F32)
    return pl.pallas_call(body, name="s5_prep_bwd",
                          out_shape=[g, g, jax.ShapeDtypeStruct((S5_GROUPS, 1), F32), full, full],
                          )(a_re, a_im, log_dt, b_re, b_im, d_ab_re, d_ab_im, d_bb_re, d_bb_im, sel)


_SCAN_W = 128
_SCAN_STEPS = int(math.log2(SCAN_CHUNK))


def _cmul(ar, ai, br, bi):
    return ar * br - ai * bi, ar * bi + ai * br


def _scan_powers(ar, ai, reverse):
    pw = [(ar, ai)]
    for _ in range(_SCAN_STEPS):
        pw.append(_cmul(*pw[-1], *pw[-1]))
    rows = lax.broadcasted_iota(jnp.int32, (SCAN_CHUNK, ar.shape[-1]), 0)
    tr = jnp.broadcast_to(ar, rows.shape)
    ti = jnp.broadcast_to(ai, rows.shape)
    for k in range(_SCAN_STEPS):
        d = 2 ** k
        if reverse:
            live = rows < SCAN_CHUNK - d
            mr, mi = _cmul(tr, ti, _shift_up(tr, d, rows, SCAN_CHUNK), _shift_up(ti, d, rows, SCAN_CHUNK))
        else:
            live = rows >= d
            mr, mi = _cmul(tr, ti, _shift_down(tr, d, rows), _shift_down(ti, d, rows))
        tr = jnp.where(live, mr, tr)
        ti = jnp.where(live, mi, ti)
    return pw, rows, tr, ti


def s5_scan(bu, ab_re, ab_im, lp):
    n_chunks = lp // SCAN_CHUNK

    def body(bu_ref, ar_ref, ai_ref, s_ref):
        ar, ai = ar_ref[...], ai_ref[...]
        pw, rows, tr, ti = _scan_powers(ar, ai, False)

        def chunk(ci, carry):
            cr, cim = carry
            r0 = pl.multiple_of(ci * SCAN_CHUNK, SCAN_CHUNK)
            xr = bu_ref[0, pl.ds(r0, SCAN_CHUNK), :]
            xi = bu_ref[1, pl.ds(r0, SCAN_CHUNK), :]
            for k in range(_SCAN_STEPS):
                d = 2 ** k
                mr, mi = _cmul(pw[k][0], pw[k][1], _shift_down(xr, d, rows), _shift_down(xi, d, rows))
                xr, xi = xr + mr, xi + mi
            mr, mi = _cmul(tr, ti, cr, cim)
            xr, xi = xr + mr, xi + mi
            s_ref[0, pl.ds(r0, SCAN_CHUNK), :] = xr
            s_ref[1, pl.ds(r0, SCAN_CHUNK), :] = xi
            return xr[SCAN_CHUNK - 1:SCAN_CHUNK, :], xi[SCAN_CHUNK - 1:SCAN_CHUNK, :]

        zero = jnp.zeros((1, _SCAN_W), F32)
        lax.fori_loop(0, n_chunks, chunk, (zero, zero))

    spec = pl.BlockSpec((2, lp, _SCAN_W), lambda c: (0, 0, c))
    aspec = pl.BlockSpec((1, _SCAN_W), lambda c: (0, c))
    return pl.pallas_call(
        body, name="s5_scan", grid=(S5_LANES // _SCAN_W,), in_specs=[spec, aspec, aspec], out_specs=spec,
        out_shape=jax.ShapeDtypeStruct((2, lp, S5_LANES), F32), compiler_params=_cparams(),
    )(bu, ab_re, ab_im)


def s5_scan_bwd(ds, s, ab_re, ab_im, lp):
    n_chunks = lp // SCAN_CHUNK

    def body(ds_ref, s_ref, ar_ref, ai_ref, g_ref, da_ref):
        ar, ai = ar_ref[...], -ai_ref[...]
        pw, rows, tr, ti = _scan_powers(ar, ai, True)

        def chunk(k, carry):
            cr, cim, dar, dai = carry
            ci = n_chunks - 1 - k
            r0 = pl.multiple_of(ci * SCAN_CHUNK, SCAN_CHUNK)
            xr = ds_ref[0, pl.ds(r0, SCAN_CHUNK), :]
            xi = ds_ref[1, pl.ds(r0, SCAN_CHUNK), :]
            for j in range(_SCAN_STEPS):
                d = 2 ** j
                mr, mi = _cmul(pw[j][0], pw[j][1], _shift_up(xr, d, rows, SCAN_CHUNK), _shift_up(xi, d, rows, SCAN_CHUNK))
                xr, xi = xr + mr, xi + mi
            mr, mi = _cmul(tr, ti, cr, cim)
            xr, xi = xr + mr, xi + mi
            g_ref[0, pl.ds(r0, SCAN_CHUNK), :] = xr
            g_ref[1, pl.ds(r0, SCAN_CHUNK), :] = xi
            prev0 = pl.multiple_of(jnp.maximum(r0 - 8, 0), 8)
            live = (ci > 0).astype(F32)
            pr = s_ref[0, pl.ds(prev0, 8), :][7:8, :] * live
            pim = s_ref[1, pl.ds(prev0, 8), :][7:8, :] * live
            sr = s_ref[0, pl.ds(r0, SCAN_CHUNK), :]
            si = s_ref[1, pl.ds(r0, SCAN_CHUNK), :]
            sr = jnp.where(rows >= 1, pltpu.roll(sr, 1, 0), pr)
            si = jnp.where(rows >= 1, pltpu.roll(si, 1, 0), pim)
            dar = dar + jnp.sum(xr * sr + xi * si, axis=0, keepdims=True)
            dai = dai + jnp.sum(xi * sr - xr * si, axis=0, keepdims=True)
            return xr[0:1, :], xi[0:1, :], dar, dai

        zero = jnp.zeros((1, _SCAN_W), F32)
        _, _, dar, dai = lax.fori_loop(0, n_chunks, chunk, (zero, zero, zero, zero))
        da_ref[0] = dar
        da_ref[1] = dai

    spec = pl.BlockSpec((2, lp, _SCAN_W), lambda c: (0, 0, c))
    aspec = pl.BlockSpec((1, _SCAN_W), lambda c: (0, c))
    return pl.pallas_call(
        body, name="s5_scan_bwd", grid=(S5_LANES // _SCAN_W,), in_specs=[spec, spec, aspec, aspec],
        out_specs=[spec, pl.BlockSpec((2, 1, _SCAN_W), lambda c: (0, 0, c))],
        out_shape=[jax.ShapeDtypeStruct((2, lp, S5_LANES), F32), jax.ShapeDtypeStruct((2, 1, S5_LANES), F32)],
        compiler_params=_cparams(),
    )(ds, s, ab_re, ab_im)


S5_BLOCKS = 4
_S5_PER = S5_GROUPS // S5_BLOCKS


def _blockdiag(x):
    _, r, c = x.shape
    eye = jnp.eye(_S5_PER, dtype=x.dtype)
    x = x.reshape(S5_BLOCKS, _S5_PER, r, c)
    return (x[:, :, :, None, :] * eye[None, :, None, :, None]).reshape(S5_BLOCKS, _S5_PER * r, _S5_PER * c)


def _blockdiag_extract(m, r, c):
    return jnp.einsum('qgrgc->qgrc', m.reshape(S5_BLOCKS, _S5_PER, r, _S5_PER, c)).reshape(S5_GROUPS, r, c)


def bd_matmul(a, w, *, w_t, reduce, res=None, name):
    _, M, _ = a.shape
    n_w, _, k1, k2 = w.shape
    ka, kout = (k2, k1) if w_t else (k1, k2)
    tm = _row_tile(M)
    n_out, n_red = (1, n_w) if reduce else (n_w, 1)
    has_res = res is not None

    assert n_red <= 2

    def body(*refs):
        a_ref, w_ref = refs[0], refs[1]
        o_ref = refs[3] if has_res else refs[2]
        for q in range(S5_BLOCKS):
            cols = slice(q * kout, (q + 1) * kout)
            part = _dot(a_ref[:, q * ka:(q + 1) * ka].astype(BF16), w_ref[q], 1, 1 if w_t else 0)
            if n_red == 1:
                o_ref[:, cols] = part
            else:
                acc_ref = refs[-1]

                @pl.when(pl.program_id(2) == 0)
                def _():
                    acc_ref[:, cols] = part

                @pl.when(pl.program_id(2) == 1)
                def _():
                    tot = acc_ref[:, cols] + part
                    o_ref[:, cols] = tot + refs[2][:, cols] if has_res else tot

    if reduce:
        a_map, w_map = (lambda o, i, r: (r, i, 0)), (lambda o, i, r: (r, 0, 0, 0))
    else:
        a_map, w_map = (lambda o, i, r: (0, i, 0)), (lambda o, i, r: (o, 0, 0, 0))
    o_map = lambda o, i, r: (o, i, 0)
    in_specs = [pl.BlockSpec((None, tm, S5_BLOCKS * ka), a_map), pl.BlockSpec((None, S5_BLOCKS, k1, k2), w_map)]
    operands = [a, w]
    if has_res:
        in_specs.append(pl.BlockSpec((None, tm, S5_BLOCKS * kout), o_map))
        operands.append(res)
    return pl.pallas_call(
        body, name=name, grid=(n_out, M // tm, n_red), in_specs=in_specs,
        out_specs=pl.BlockSpec((None, tm, S5_BLOCKS * kout), o_map),
        out_shape=jax.ShapeDtypeStruct((n_out, M, S5_BLOCKS * kout), F32),
        scratch_shapes=[pltpu.VMEM((tm, S5_BLOCKS * kout), F32)] if n_red > 1 else [],
        compiler_params=_cparams(),
    )(*operands)


def bd_outer(a, b, name):
    na, M, wa = a.shape
    nb_, _, wb = b.shape
    ka, kb = wa // S5_BLOCKS, wb // S5_BLOCKS
    n_out = max(na, nb_)

    def body(a_ref, b_ref, o_ref):
        o_ref[...] = _dot(a_ref[...].astype(BF16), b_ref[...].astype(BF16), 0, 0)

    return pl.pallas_call(
        body, name=name, grid=(n_out, S5_BLOCKS),
        in_specs=[pl.BlockSpec((None, M, ka), (lambda o, q: (o, 0, q)) if na > 1 else (lambda o, q: (0, 0, q))),
                  pl.BlockSpec((None, M, kb), (lambda o, q: (o, 0, q)) if nb_ > 1 else (lambda o, q: (0, 0, q)))],
        out_specs=pl.BlockSpec((None, None, ka, kb), lambda o, q: (o, q, 0, 0)),
        out_shape=jax.ShapeDtypeStruct((n_out, S5_BLOCKS, ka, kb), F32), compiler_params=_cparams(),
    )(a, b)


def s5_u(proj, lp):
    def fn(row0, rv, pv):
        u, = rv
        return ((jnp.where(_row_mask(row0, u.shape), u, 0.0),), ())

    return rowwise(fn, [(proj, MIX, 5)], [], [(MIX, BF16)], name="s5_u", lp=lp)[0]


def s5_y(ys, proj, d, lp):
    def fn(row0, rv, pv):
        ys_, u = rv
        y = ys_ + pv[0] * u
        return ((y, _gelu(y)), ())

    return rowwise(fn, [(ys, MIX, 0), (proj, MIX, 5)], [d], [(MIX, F32), (MIX, BF16)], name="s5_y", lp=lp)


def s5_glu(z, y, b, lp):
    def fn(row0, rv, pv):
        z_, y_ = rv
        return ((_gelu(y_) * _sigmoid(z_ + pv[0]),), ())

    return rowwise(fn, [(z, MIX, 0), (y, MIX, 0)], [b], [(MIX, BF16)], name="s5_glu", lp=lp)[0]


def s5_glu_bwd(dgl, z, y, b, lp):
    def fn(row0, rv, pv):
        dgl_, z_, y_ = rv
        sg = _sigmoid(z_ + pv[0])
        dz = dgl_ * _gelu(y_) * sg * (1.0 - sg)
        return ((dgl_ * sg, dz), (jnp.sum(dz, axis=0, keepdims=True),))

    return rowwise(fn, [(dgl, MIX, 0), (z, MIX, 0), (y, MIX, 0)], [b], [(MIX, F32), (MIX, BF16)], [((1, MIX), F32)],
                   name="s5_glu_bwd", lp=lp)


def s5_y_bwd(dyg, y, proj, d, lp):
    def fn(row0, rv, pv):
        dyg_, y_, u = rv
        dy = dyg_ * _gelu_grad(y_)
        return ((dy, dy * pv[0]), (jnp.sum(dy * u, axis=0, keepdims=True),))

    return rowwise(fn, [(dyg, MIX, 0), (y, MIX, 0), (proj, MIX, 5)], [d], [(MIX, BF16), (MIX, F32)], [((1, MIX), F32)],
                   name="s5_y_bwd", lp=lp)


def s5_du(du, lp):
    def fn(row0, rv, pv):
        return ((jnp.where(_row_mask(row0, rv[0].shape), rv[0], 0.0),), ())

    return rowwise(fn, [(du, MIX, 0)], [], [(MIX, BF16)], name="s5_du", lp=lp)[0]


def merge_fwd(proj, ya, yb, yc, lp):
    def fn(row0, rv, pv):
        g0, g1, g2, a, b, c = rv
        return ((_sigmoid(g0) * a + _sigmoid(g1) * b + _sigmoid(g2) * c,), ())

    return rowwise(fn, [(proj, D_MODEL, 3), (proj, D_MODEL, 4), (proj, D_MODEL, 5), (ya, D_MODEL, 0), (yb, D_MODEL, 0),
                        (yc, D_MODEL, 0)], [], [(D_MODEL, BF16)], name="merge_fwd", lp=lp)[0]


def merge_bwd(dmix, proj, ya, yb, yc, lp):
    def fn(row0, rv, pv):
        dm, g0, g1, g2, a, b, c = rv
        outs_y, outs_g = [], []
        for g, yv in ((g0, a), (g1, b), (g2, c)):
            sg = _sigmoid(g)
            outs_y.append(dm * sg)
            outs_g.append(dm * yv * sg * (1.0 - sg))
        return (tuple(outs_y) + tuple(outs_g), ())

    return rowwise(fn, [(dmix, D_MODEL, 0), (proj, D_MODEL, 3), (proj, D_MODEL, 4), (proj, D_MODEL, 5),
                        (ya, D_MODEL, 0), (yb, D_MODEL, 0), (yc, D_MODEL, 0)], [], [(D_MODEL, BF16)] * 6,
                   name="merge_bwd", lp=lp)


def loss_head(h, tgt, lp):
    def fn(row0, rv, pv):
        h_, t_ = rv
        live = (row0 + lax.broadcasted_iota(jnp.int32, h_.shape, 0)) >= X0
        diff = jnp.where(live, h_ - t_, 0.0)
        ssq = jnp.sum(jnp.sum(diff * diff, axis=1, keepdims=True), axis=0, keepdims=True)
        return ((diff * (1.0 / D_MODEL),), (ssq * (0.5 / D_MODEL),))

    return rowwise(fn, [(h, D_MODEL, 0), (tgt, D_MODEL, 0)], [], [(D_MODEL, F32)], [((1, 1), F32)], name="loss_head", lp=lp)


def _s5_consts(W):
    ab_re_rep, ab_im_rep, bb_re, bb_im = s5_prep(W['s5_a_re'], W['s5_a_im'], W['s5_log_dt'], W['s5_b_re'], W['s5_b_im'])
    pick = lambda t: t.reshape(S5_GROUPS, S5_GROUP, S5_STATE)[:, 0].reshape(1, S5_LANES)
    bb = jnp.stack([_blockdiag(bb_re.reshape(S5_GROUPS, S5_GROUP, S5_STATE)),
                    _blockdiag(bb_im.reshape(S5_GROUPS, S5_GROUP, S5_STATE))]).astype(BF16)
    return pick(ab_re_rep), pick(ab_im_rep), bb


def layer_fwd(h, hb, W, tabs, lp, ffn1=None, ffn2=True):
    cos, sin, rot = tabs
    h1, h1b, sv1 = ffn1 if ffn1 is not None else ffn_fwd(h, hb, W['wg1'], W['wu1'], W['wd1'], W['ln1_g'], W['ln1_b'], lp)
    proj = matmul(h1b[None], W['w_in'][None], tb=True, name="proj")[0]
    cqn, ckvn, kr = mla_prep(proj, cos, sin, rot, W['q_norm_g'], W['kv_norm_g'], lp)
    qn, qr, kn, v = mla_heads(cqn, ckvn, cos, sin, rot, W['wqn'], W['wqr'], W['wkn'], W['wv'], lp)
    o = attn_fwd(qn, qr, kn, v, kr, lp)
    ya = matmul(o, W['mla_wo'], ab='r', bb='r', name="mla_out")[0]
    vconv = conv_fwd(proj, W['conv_w'], W['conv_b'], lp)
    yb = matmul(vconv[None], W['conv_wout'][None], name="conv_out")[0]
    ub = s5_u(proj, lp)
    ab_re, ab_im, bb = _s5_consts(W)
    bu = bd_matmul(ub[None], bb, w_t=False, reduce=False, name="s5_bu")
    s = s5_scan(bu, ab_re, ab_im, lp)
    ys = bd_matmul(s, W['s5_ct'], w_t=False, reduce=True, name="s5_cs")[0]
    y, ygb = s5_y(ys, proj, W['s5_d'], lp)
    zg = matmul(ygb[None], W['s5_wglu'][None], name="s5_glu_mm")[0]
    glb = s5_glu(zg, y, W['s5_b_glu'], lp)
    yc = matmul(glb[None], W['s5_wout'][None], name="s5_out")[0]
    mixed = merge_fwd(proj, ya, yb, yc, lp)
    z2, h2, h2b = mm_res_ln(mixed[None], W['w_o'][None], h1, W['ln2_g'], W['ln2_b'], scale=1.0, name="wo_ln", lp=lp)
    sv = dict(sv1=sv1, h1b=h1b, proj=proj, cqn=cqn, ckvn=ckvn, kr=kr, qn=qn, qr=qr, kn=kn, v=v, o=o, ya=ya,
              vconv=vconv, yb=yb, ub=ub, ab_re=ab_re, ab_im=ab_im, bb=bb, s=s, y=y, ygb=ygb, zg=zg, glb=glb, yc=yc,
              mixed=mixed, z2=z2)
    if not ffn2:
        return h2, h2b, sv
    h3, h3b, sv['sv3'] = ffn_fwd(h2, h2b, W['wg2'], W['wu2'], W['wd2'], W['ln3_g'], W['ln3_b'], lp)
    return h3, h3b, sv


def layer_bwd(dh3, sv, W, tabs, lp, ffn1=True, ffn2=True):
    cos, sin, rot = tabs
    proj = sv['proj']
    G = {}
    dh2 = dh3
    if ffn2:
        dh2, g3 = ffn_bwd(dh3, sv['sv3'], W['wg2'], W['wu2'], W['wd2'], W['ln3_g'], lp)
        G.update(wg2=g3['wg'], wu2=g3['wu'], wd2=g3['wd'], ln3_g=g3['ln_g'], ln3_b=g3['ln_b'])
    dz2, dz2b, G['ln2_g'], G['ln2_b'] = ln_bwd(dh2, sv['z2'], W['ln2_g'], fscale=1.0, name="wo_ln_bwd", lp=lp)
    dmix = matmul(dz2b[None], W['w_o'][None], tb=True, name="wo_dx")[0]
    G['w_o'] = matmul(sv['mixed'][None], dz2b[None], ta=True, out_dtype=WGRAD, name="wo_dw")[0]
    dya, dyb, dyc, dg0, dg1, dg2 = merge_bwd(dmix, proj, sv['ya'], sv['yb'], sv['yc'], lp)
    dgl = matmul(dyc[None], W['s5_wout'][None], tb=True, name="s5_out_dx")[0]
    G['s5_wout'] = matmul(sv['glb'][None], dyc[None], ta=True, out_dtype=WGRAD, name="s5_out_dw")[0]
    t1, dzb, G['s5_b_glu'] = s5_glu_bwd(dgl, sv['zg'], sv['y'], W['s5_b_glu'], lp)
    dyg = matmul(dzb[None], W['s5_wglu'][None], tb=True, res=t1[None], name="s5_glu_dx")[0]
    G['s5_wglu'] = matmul(sv['ygb'][None], dzb[None], ta=True, out_dtype=WGRAD, name="s5_glu_dw")[0]
    dyb_, du_d, G['s5_d'] = s5_y_bwd(dyg, sv['y'], proj, W['s5_d'], lp)
    ds = bd_matmul(dyb_[None], W['s5_ct'], w_t=True, reduce=False, name="s5_cs_dx")
    G['s5_ct'] = bd_outer(sv['s'], dyb_[None], "s5_cs_dw")
    g_adj, d_ab = s5_scan_bwd(ds, sv['s'], sv['ab_re'], sv['ab_im'], lp)
    du = bd_matmul(g_adj, sv['bb'], w_t=True, reduce=True, res=du_d[None], name="s5_bu_dx")[0]
    d_bb = bd_outer(sv['ub'][None], g_adj, "s5_bu_dw")
    du_b = s5_du(du, lp)
    onehot = (jnp.arange(S5_GROUP) == 0).astype(F32)
    spread = lambda t: (t.reshape(S5_GROUPS, 1, S5_STATE) * onehot[None, :, None]).reshape(_S5_ROWS, S5_STATE)
    take = lambda t: _blockdiag_extract(t, S5_GROUP, S5_STATE).reshape(_S5_ROWS, S5_STATE)
    sel = jnp.kron(jnp.eye(S5_GROUPS, dtype=F32), jnp.ones((1, S5_GROUP), F32))
    (G['s5_a_re'], G['s5_a_im'], G['s5_log_dt'], G['s5_b_re'], G['s5_b_im']) = s5_prep_bwd(
        W['s5_a_re'], W['s5_a_im'], W['s5_log_dt'], W['s5_b_re'], W['s5_b_im'],
        spread(d_ab[0]), spread(d_ab[1]), take(d_bb[0]), take(d_bb[1]), sel)
    dv = matmul(dyb[None], W['conv_wout'][None], tb=True, name="conv_out_dx")[0]
    G['conv_wout'] = matmul(sv['vconv'][None], dyb[None], ta=True, out_dtype=WGRAD, name="conv_out_dw")[0]
    dxbar, dbg, dcg, G['conv_w'], G['conv_b'] = conv_bwd(dv, proj, W['conv_w'], W['conv_b'], lp)
    do = matmul(dya[None], W['mla_wo'], tb=True, bb='o', out_dtype=BF16, name="mla_out_dx")
    G['mla_wo'] = matmul(sv['o'], dya[None], ta=True, ab='o', out_dtype=WGRAD, name="mla_out_dw")
    dqn, dqr, dkn, dvv, dkr = attn_bwd(sv['qn'], sv['qr'], sv['kn'], sv['v'], sv['kr'], do, lp)
    dcq, dckv, dqrp = mla_heads_bwd(dqn, dqr, dkn, dvv, cos, sin, rot, W['wqn'], W['wqr'], W['wkn'], W['wv'], lp)
    G['wqn'] = matmul(sv['cqn'][None], dqn, ta=True, bb='o', out_dtype=WGRAD, name="mla_dwqn")
    G['wqr'] = matmul(sv['cqn'][None], dqrp, ta=True, bb='o', out_dtype=WGRAD, name="mla_dwqr")
    G['wkn'] = matmul(sv['ckvn'][None], dkn, ta=True, bb='o', out_dtype=WGRAD, name="mla_dwkn")
    G['wv'] = matmul(sv['ckvn'][None], dvv, ta=True, bb='o', out_dtype=WGRAD, name="mla_dwv")
    dcq_raw, dckv_raw, dkr_raw, G['q_norm_g'], G['kv_norm_g'] = mla_prep_bwd(
        dcq, dckv, dkr, proj, cos, sin, rot, W['q_norm_g'], W['kv_norm_g'], lp)
    zeros = lambda n: jnp.zeros((lp, n), BF16)
    dproj = jnp.concatenate([dcq_raw, dkr_raw, zeros(96), dckv_raw, zeros(256), dxbar, dbg, dcg, du_b, dg0, dg1, dg2], axis=1)
    dh1 = matmul(dproj[None], W['w_in'][None], res=dz2[None], res_scale=ALPHA, name="proj_dx")[0]
    G['w_in'] = matmul(dproj[None], sv['h1b'][None], ta=True, out_dtype=WGRAD, name="proj_dw")[0]
    if not ffn1:
        return dh1, G
    dh0, g1 = ffn_bwd(dh1, sv['sv1'], W['wg1'], W['wu1'], W['wd1'], W['ln1_g'], lp)
    G.update(wg1=g1['wg'], wu1=g1['wu'], wd1=g1['wd'], ln1_g=g1['ln_g'], ln1_b=g1['ln_b'])
    return dh0, G


def _nat_cols(st):
    return jnp.transpose(st, (1, 0, 2)).reshape(st.shape[1], -1)


def _shard_cols(nat):
    k, n = nat.shape
    return jnp.transpose(nat.reshape(k, N_SHARD, n // N_SHARD), (1, 0, 2))


def _win_pad(wt):
    z = lambda n: jnp.zeros((n, wt.shape[1]), wt.dtype)
    return jnp.concatenate([wt[0:384], wt[640:672], z(96), wt[384:640], z(256), wt[672:]], axis=0)


def _win_unpad(wp):
    return jnp.concatenate([wp[0:384], wp[512:768], wp[384:416], wp[1024:]], axis=0)


_BIG = [('ffn1_w_gate', 'T'), ('ffn1_w_up', 'T'), ('ffn1_w_down', 0), ('w_in', 'T'), ('mla_w_uq', 1), ('mla_w_ukv', 1),
        ('mla_w_o', 1), ('conv_w_out', 1), ('s5_w_glu', 0), ('s5_w_out', 1), ('w_o', 0),
        ('ffn2_w_gate', 'T'), ('ffn2_w_up', 'T'), ('ffn2_w_down', 0)]
_REPL = ['ln1_g', 'ln1_b', 'mla_q_norm_g', 'mla_kv_norm_g', 'conv_b', 's5_a_re', 's5_a_im', 's5_log_dt', 's5_b_re',
         's5_b_im', 's5_c_re', 's5_c_im', 's5_d', 's5_b_glu', 'ln2_g', 'ln2_b', 'ln3_g', 'ln3_b']


def compute_weights(st, small):
    W = {}
    for t in ('1', '2'):
        if 'ffn%s_w_gate' % t in st:
            W['wg' + t], W['wu' + t], W['wd' + t] = (st['ffn%s_w_%s' % (t, p)] for p in ('gate', 'up', 'down'))
    if 'w_in' in st:
        W.update(_mixer_weights(st))
    if small is not None:
        W.update(_small_weights(small))
    return W


def _mixer_weights(st):
    W = {}
    W['w_in'] = _win_pad(st['w_in'].reshape(D_IN, D_MODEL))
    uq = jnp.transpose(_nat_cols(st['mla_w_uq']).reshape(Q_RANK, N_HEADS, D_NOPE + D_ROPE), (1, 0, 2))
    W['wqn'], W['wqr'] = uq[:, :, :D_NOPE], uq[:, :, D_NOPE:]
    ukv = jnp.transpose(_nat_cols(st['mla_w_ukv']).reshape(KV_RANK, N_HEADS, D_NOPE + D_V), (1, 0, 2))
    W['wkn'], W['wv'] = ukv[:, :, :D_NOPE], ukv[:, :, D_NOPE:]
    W['mla_wo'] = _nat_cols(st['mla_w_o']).reshape(N_HEADS, D_V, D_MODEL)
    W['conv_wout'] = _nat_cols(st['conv_w_out'])
    W['s5_wglu'] = st['s5_w_glu'].reshape(MIX, MIX)
    W['s5_wout'] = _nat_cols(st['s5_w_out'])
    W['w_o'] = st['w_o'].reshape(D_MODEL, D_MODEL)
    return W


def _small_weights(small):
    W = {}
    W['conv_w'] = small['conv_w']
    for n in ('ln1_g', 'ln1_b', 'ln2_g', 'ln2_b', 'ln3_g', 'ln3_b', 'conv_b', 's5_b_glu'):
        W[n] = small[n].reshape(1, -1)
    W['q_norm_g'] = small['mla_q_norm_g'].reshape(1, -1)
    W['kv_norm_g'] = small['mla_kv_norm_g'].reshape(1, -1)
    W['s5_d'] = small['s5_d'].reshape(1, MIX)
    rep = lambda t: jnp.repeat(t, S5_GROUP, axis=0)
    W['s5_a_re'], W['s5_a_im'] = rep(small['s5_a_re']), rep(small['s5_a_im'])
    W['s5_log_dt'] = jnp.broadcast_to(rep(small['s5_log_dt'].reshape(S5_GROUPS, 1)), (_S5_ROWS, S5_STATE))
    tr = lambda t: jnp.transpose(t, (0, 2, 1)).reshape(_S5_ROWS, S5_STATE)
    W['s5_b_re'], W['s5_b_im'] = tr(small['s5_b_re']), tr(small['s5_b_im'])
    ct = lambda t: _blockdiag(jnp.transpose(t, (0, 2, 1)))
    W['s5_ct'] = jnp.stack([ct(small['s5_c_re']), -ct(small['s5_c_im'])]).astype(BF16)
    return W


def reference_grads(G, ffn=True):
    R = {}
    for t in ('1', '2') if ffn else ():
        R['ffn%s_w_gate' % t] = G['wg' + t].reshape(D_FF, D_MODEL).T
        R['ffn%s_w_up' % t] = G['wu' + t].reshape(D_FF, D_MODEL).T
        R['ffn%s_w_down' % t] = G['wd' + t].reshape(D_FF, D_MODEL)
    R['w_in_t'] = _win_unpad(G['w_in'])
    if ffn:
        R['w_in'] = R['w_in_t'].T
    R['mla_w_uq'] = jnp.transpose(jnp.concatenate([G['wqn'], G['wqr']], axis=2), (1, 0, 2)).reshape(Q_RANK, -1)
    R['mla_w_ukv'] = jnp.transpose(jnp.concatenate([G['wkn'], G['wv']], axis=2), (1, 0, 2)).reshape(KV_RANK, -1)
    R['mla_w_o'] = G['mla_wo'].reshape(N_HEADS * D_V, D_MODEL)
    R['conv_w'], R['conv_w_out'] = G['conv_w'], G['conv_wout']
    R['s5_w_glu'], R['s5_w_out'], R['w_o'] = G['s5_wglu'], G['s5_wout'], G['w_o']
    for n in ('ln1_g', 'ln1_b', 'ln2_g', 'ln2_b', 'ln3_g', 'ln3_b', 'conv_b', 's5_b_glu'):
        if n in G:
            R[n] = G[n].reshape(-1)
    R['mla_q_norm_g'], R['mla_kv_norm_g'] = G['q_norm_g'].reshape(-1), G['kv_norm_g'].reshape(-1)
    R['s5_d'] = G['s5_d'].reshape(S5_GROUPS, S5_GROUP)
    R['s5_a_re'], R['s5_a_im'], R['s5_log_dt'] = G['s5_a_re'], G['s5_a_im'], G['s5_log_dt'].reshape(-1)
    untr = lambda t: jnp.transpose(t.reshape(S5_GROUPS, S5_GROUP, S5_STATE), (0, 2, 1))
    R['s5_b_re'], R['s5_b_im'] = untr(G['s5_b_re']), untr(G['s5_b_im'])
    unct = lambda t: jnp.transpose(_blockdiag_extract(t, S5_STATE, S5_GROUP), (0, 2, 1))
    R['s5_c_re'], R['s5_c_im'] = unct(G['s5_ct'][0]), -unct(G['s5_ct'][1])
    return R


_ANY = pl.BlockSpec(memory_space=pl.ANY)
LANES = 1024


def _place():
    x, y, c = lax.axis_index("x"), lax.axis_index("y"), lax.axis_index("c")
    chips = [(1 - x, y), (x, 1 - y), (1 - x, 1 - y)]
    return x, y, c, chips


def _rows_of(c, half):
    return pl.ds(pl.multiple_of(c * half, 8), half)


def all_gather_shards(srcs, exact):
    n, m = len(srcs), len(exact)
    halves = [s.shape[0] // 2 for s in srcs]

    def body(*refs):
        s_refs, e_refs = refs[:n], refs[n:n + m]
        o_refs, eo_refs = refs[n + m:2 * n + m], refs[2 * n + m:2 * n + 2 * m]
        send, recv, esend, erecv, osend, orecv, lsem = refs[2 * n + 2 * m:]
        x, y, c, chips = _place()
        me = 2 * x + y
        sibling = (x, y, 1 - c)
        own = [pltpu.make_async_remote_copy(src_ref=s_refs[k], dst_ref=o_refs[k].at[me], send_sem=osend.at[k],
                                            recv_sem=orecv.at[k], device_id=sibling, device_id_type=MESH) for k in range(n)]
        local = [pltpu.make_async_copy(e_refs[k], eo_refs[k].at[me], lsem.at[k]) for k in range(m)]
        for cp in own + local:
            cp.start()

        def copy(k, s, src, idx, half_c, to):
            return pltpu.make_async_remote_copy(
                src_ref=src, dst_ref=o_refs[k].at[idx, _rows_of(half_c, halves[k])], send_sem=send.at[6 * k + s],
                recv_sem=recv.at[6 * k + s], device_id=to, device_id_type=MESH)

        def ecopy(k, j, idx, to):
            return pltpu.make_async_remote_copy(src_ref=e_refs[k], dst_ref=eo_refs[k].at[idx], send_sem=esend.at[3 * k + j],
                                                recv_sem=erecv.at[3 * k + j], device_id=to, device_id_type=MESH)

        sends = []
        for k in range(n):
            mine = s_refs[k].at[_rows_of(c, halves[k])]
            sends += [copy(k, j, mine, me, c, (*chip, c)) for j, chip in enumerate(chips)]
        for k in range(m):
            sends += [ecopy(k, j, me, (*chip, c)) for j, chip in enumerate(chips)]
        for cp in sends:
            cp.start()
        for j, chip in enumerate(chips):
            idx = 2 * chip[0] + chip[1]
            for k in range(n):
                landed = o_refs[k].at[idx, _rows_of(c, halves[k])]
                copy(k, j, landed, idx, c, sibling).wait_recv()
                fwd = copy(k, 3 + j, landed, idx, c, sibling)
                fwd.start()
                sends.append(fwd)
        for j, chip in enumerate(chips):
            idx = 2 * chip[0] + chip[1]
            for k in range(n):
                copy(k, 3 + j, s_refs[k].at[_rows_of(c, halves[k])], idx, 1 - c, sibling).wait_recv()
            for k in range(m):
                ecopy(k, j, idx, sibling).wait_recv()
        for cp in sends:
            cp.wait_send()
        for cp in own + local:
            cp.wait()

    outs = pl.pallas_call(
        body, name="all_gather_weights", in_specs=[_ANY] * (n + m), out_specs=[_ANY] * (n + m),
        out_shape=[jax.ShapeDtypeStruct((N_SHARD,) + a.shape, a.dtype) for a in list(srcs) + list(exact)],
        scratch_shapes=[pltpu.SemaphoreType.DMA((6 * n,)), pltpu.SemaphoreType.DMA((6 * n,)),
                        pltpu.SemaphoreType.DMA((3 * m,)), pltpu.SemaphoreType.DMA((3 * m,)),
                        pltpu.SemaphoreType.DMA((n,)), pltpu.SemaphoreType.DMA((n,)), pltpu.SemaphoreType.DMA((m,))],
    )(*srcs, *exact)
    return outs[:n], outs[n:]


def rs_pair_swap(gs):
    n = len(gs)

    def body(*refs):
        g_refs, r_refs, send, recv = refs[:n], refs[n:2 * n], refs[2 * n], refs[2 * n + 1]
        x, y, c, _ = _place()
        copies = [pltpu.make_async_remote_copy(
            src_ref=g_refs[k].at[pl.ds(0, N_SHARD), _rows_of(1 - c, gs[k].shape[1] // 2)], dst_ref=r_refs[k],
            send_sem=send.at[k], recv_sem=recv.at[k], device_id=(x, y, 1 - c), device_id_type=MESH) for k in range(n)]
        for cp in copies:
            cp.start()
        for cp in copies:
            cp.wait()

    return pl.pallas_call(
        body, name="grad_pair_swap", in_specs=[_ANY] * n, out_specs=[_ANY] * n,
        out_shape=[jax.ShapeDtypeStruct((N_SHARD, g.shape[1] // 2, g.shape[2]), g.dtype) for g in gs],
        scratch_shapes=[pltpu.SemaphoreType.DMA((n,)), pltpu.SemaphoreType.DMA((n,))],
    )(*gs)


def _group_tile(half, n_cols, n_arrays):
    budget = (20 * 2 ** 20) // (6 * n_arrays)
    fits = [t for t in range(8, half + 1, 8) if half % t == 0 and t * n_cols * 4 <= budget]
    return max(fits) if fits else 8


def rs_pair_add(gs, rs, cidx, out_dtype, name):
    n = len(gs)
    _, K, cols = gs[0].shape
    half = K // 2
    tr = _group_tile(half, cols, n)
    nb = half // tr

    def body(c_ref, *refs):
        for g_ref, r_ref, o_ref in zip(refs[:n], refs[n:2 * n], refs[2 * n:]):
            o_ref[...] = (g_ref[...].astype(F32) + r_ref[...].astype(F32)).astype(out_dtype)

    gspec = pl.BlockSpec((None, tr, cols), lambda j, i, c: (j, c[0] * nb + i, 0))
    rspec = pl.BlockSpec((None, tr, cols), lambda j, i, c: (j, i, 0))
    return pl.pallas_call(
        body, name=name,
        grid_spec=pltpu.PrefetchScalarGridSpec(num_scalar_prefetch=1, grid=(N_SHARD, nb), in_specs=[gspec] * n + [rspec] * n,
                                               out_specs=[rspec] * n),
        out_shape=[jax.ShapeDtypeStruct((N_SHARD, half, cols), out_dtype)] * n,
        compiler_params=_cparams(),
    )(cidx, *gs, *rs)


def rs_chip_scatter(ps):
    n = len(ps)

    def body(*refs):
        p_refs, q_refs, send, recv, lsem = refs[:n], refs[n:2 * n], refs[2 * n], refs[2 * n + 1], refs[2 * n + 2]
        x, y, c, chips = _place()
        me = 2 * x + y
        local = [pltpu.make_async_copy(p_refs[k].at[me], q_refs[k].at[me], lsem.at[k]) for k in range(n)]
        copies = [pltpu.make_async_remote_copy(
            src_ref=p_refs[k].at[2 * chip[0] + chip[1]], dst_ref=q_refs[k].at[me], send_sem=send.at[3 * k + j],
            recv_sem=recv.at[3 * k + j], device_id=(*chip, c), device_id_type=MESH)
            for k in range(n) for j, chip in enumerate(chips)]
        for cp in local + copies:
            cp.start()
        for cp in copies:
            cp.wait()
        for cp in local:
            cp.wait()

    return pl.pallas_call(
        body, name="grad_chip_scatter", in_specs=[_ANY] * n, out_specs=[_ANY] * n,
        out_shape=[jax.ShapeDtypeStruct(p.shape, p.dtype) for p in ps],
        scratch_shapes=[pltpu.SemaphoreType.DMA((3 * n,)), pltpu.SemaphoreType.DMA((3 * n,)), pltpu.SemaphoreType.DMA((n,))],
    )(*ps)


def rs_chip_sum(qs, nl, cidx, name):
    n = len(qs)
    _, half, cols = qs[0].shape
    tr = _group_tile(half, cols, n)
    nb = half // tr

    def body(c_ref, *refs):
        for k, q_ref in enumerate(refs[:n]):
            o_ref = refs[n + k // nl]
            o_ref[k % nl] = ((q_ref[0].astype(F32) + q_ref[1].astype(F32)) + q_ref[2].astype(F32)) + q_ref[3].astype(F32)

    return pl.pallas_call(
        body, name=name,
        grid_spec=pltpu.PrefetchScalarGridSpec(
            num_scalar_prefetch=1, grid=(nb,),
            in_specs=[pl.BlockSpec((N_SHARD, tr, cols), lambda i, c: (0, i, 0))] * n,
            out_specs=[pl.BlockSpec((nl, tr, cols), lambda i, c: (0, c[0] * nb + i, 0))] * (n // nl)),
        out_shape=[jax.ShapeDtypeStruct((nl, 2 * half, cols), F32)] * (n // nl),
        compiler_params=_cparams(),
    )(cidx, *qs)


def rs_pair_gather(fs):
    n = len(fs)

    def body(*refs):
        f_refs, send, recv = refs[n:2 * n], refs[2 * n], refs[2 * n + 1]
        x, y, c, _ = _place()
        copies = []
        for k in range(n):
            rows = f_refs[k].at[pl.ds(0, fs[k].shape[0]), _rows_of(c, fs[k].shape[1] // 2)]
            copies.append(pltpu.make_async_remote_copy(src_ref=rows, dst_ref=rows, send_sem=send.at[k], recv_sem=recv.at[k],
                                                       device_id=(x, y, 1 - c), device_id_type=MESH))
        for cp in copies:
            cp.start()
        for cp in copies:
            cp.wait()

    return pl.pallas_call(
        body, name="grad_pair_gather", in_specs=[_ANY] * n, out_specs=[_ANY] * n,
        out_shape=[jax.ShapeDtypeStruct(f.shape, f.dtype) for f in fs],
        input_output_aliases={k: k for k in range(n)},
        scratch_shapes=[pltpu.SemaphoreType.DMA((n,)), pltpu.SemaphoreType.DMA((n,))],
    )(*fs)


_HBM = pl.BlockSpec(memory_space=pltpu.HBM)
_SEM = pl.BlockSpec(memory_space=pltpu.SEMAPHORE)
_EFFECT = pltpu.SideEffectType.DATAFLOW_SIDE_EFFECTING


def _in_hbm(a):
    return pltpu.with_memory_space_constraint(a, pltpu.HBM)


def split_start(name, srcs, lands, after, copies_fn, n_copies):
    n = len(srcs)

    def body(*refs):
        for cp in copies_fn(refs[:n], refs[n:2 * n], refs[2 * n + 1], refs[2 * n + 2]):
            cp.start()
        refs[-1][...] = jnp.zeros_like(refs[-1])

    bufs = list(srcs) + list(lands)
    outs = pl.pallas_call(
        body, name=name,
        out_shape=(pltpu.SemaphoreType.DMA((n_copies,)), pltpu.SemaphoreType.DMA((n_copies,)),
                   *[pltpu.HBM(a.shape, a.dtype) for a in bufs], jax.ShapeDtypeStruct((8, 128), F32)),
        in_specs=[_HBM] * (2 * n) + [_ANY],
        out_specs=(_SEM, _SEM, *[_HBM] * (2 * n), pl.BlockSpec(memory_space=pltpu.VMEM)),
        input_output_aliases={i: 2 + i for i in range(2 * n)},
        compiler_params=pltpu.CompilerParams(has_side_effects=_EFFECT),
    )(*[_in_hbm(a) for a in bufs], after)
    return outs[0], outs[1], outs[2:2 + n], outs[2 + n:2 + 2 * n], outs[-1]


def split_wait(name, send, recv, srcs, lands, after, copies_fn, which=None):
    n = len(srcs)

    def body(*refs):
        copies = copies_fn(refs[:n], refs[n:2 * n], refs[2 * n], refs[2 * n + 1], which)
        for cp in copies:
            cp.wait_send()
        for cp in copies:
            cp.wait_recv()

    bufs = list(srcs) + list(lands)
    outs = pl.pallas_call(
        body, name=name, out_shape=tuple(pltpu.HBM(a.shape, a.dtype) for a in bufs),
        in_specs=[_HBM] * (2 * n) + [_SEM, _SEM, _ANY], out_specs=tuple([_HBM] * (2 * n)),
        input_output_aliases={i: i for i in range(2 * n)},
        compiler_params=pltpu.CompilerParams(has_side_effects=_EFFECT),
    )(*bufs, send, recv, after)
    return list(outs[:n]), list(outs[n:])


def _gather_copies(s_refs, l_refs, send, recv, which=None):
    x, y, c, chips = _place()
    me = 2 * x + y
    out = []
    for k in (range(len(s_refs)) if which is None else which):
        s, l = s_refs[k], l_refs[k]
        rows = _rows_of(c, s.shape[0] // 2)
        for j, chip in enumerate(chips):
            out.append(pltpu.make_async_remote_copy(src_ref=s.at[rows], dst_ref=l.at[me, rows], send_sem=send.at[4 * k + j],
                                                    recv_sem=recv.at[4 * k + j], device_id=(*chip, c), device_id_type=MESH))
        out.append(pltpu.make_async_remote_copy(src_ref=s, dst_ref=l.at[me], send_sem=send.at[4 * k + 3],
                                                recv_sem=recv.at[4 * k + 3], device_id=(x, y, 1 - c), device_id_type=MESH))
    return out


def _scatter_copies(s_refs, l_refs, send, recv, which=None):
    x, y, c, chips = _place()
    me = 2 * x + y
    return [pltpu.make_async_remote_copy(src_ref=s.at[2 * chip[0] + chip[1]], dst_ref=l.at[me], send_sem=send.at[3 * k + j],
                                         recv_sem=recv.at[3 * k + j], device_id=(*chip, c), device_id_type=MESH)
            for k, (s, l) in enumerate(zip(s_refs, l_refs)) for j, chip in enumerate(chips)]


def gather_forward(lands, name):
    n = len(lands)

    def body(*refs):
        l_refs, send, recv = refs[n:2 * n], refs[2 * n], refs[2 * n + 1]
        x, y, c, chips = _place()
        copies = []
        for k in range(n):
            rows = _rows_of(c, lands[k].shape[1] // 2)
            for j, chip in enumerate(chips):
                part = l_refs[k].at[2 * chip[0] + chip[1], rows]
                copies.append(pltpu.make_async_remote_copy(src_ref=part, dst_ref=part, send_sem=send.at[3 * k + j],
                                                           recv_sem=recv.at[3 * k + j], device_id=(x, y, 1 - c),
                                                           device_id_type=MESH))
        for cp in copies:
            cp.start()
        for cp in copies:
            cp.wait()

    return pl.pallas_call(
        body, name=name, in_specs=[_ANY] * n, out_specs=[_ANY] * n,
        out_shape=[jax.ShapeDtypeStruct(a.shape, a.dtype) for a in lands],
        input_output_aliases={k: k for k in range(n)},
        scratch_shapes=[pltpu.SemaphoreType.DMA((3 * n,)), pltpu.SemaphoreType.DMA((3 * n,))],
    )(*lands)


def rs_partials(gs, wire, cidx, tag):
    rs = rs_pair_swap(gs)
    groups = {}
    for k, g in enumerate(gs):
        groups.setdefault((g.shape, jnp.dtype(wire[k]).name), []).append(k)
    ps = [None] * len(gs)
    for gi, ks in enumerate(groups.values()):
        outs = rs_pair_add([gs[k] for k in ks], [rs[k] for k in ks], cidx, wire[ks[0]], "grad_pair_add_%s%d" % (tag, gi))
        for k, o in zip(ks, outs):
            ps[k] = o
    return ps


def rs_finish(items):
    cidx = lax.axis_index("c").astype(jnp.int32).reshape(1)
    groups = {}
    for i, it in enumerate(items):
        groups.setdefault((it[0].shape, len(it), it[0].dtype.name), []).append(i)
    fs = [None] * len(items)
    for gi, ids in enumerate(groups.values()):
        outs = rs_chip_sum([q for i in ids for q in items[i]], len(items[ids[0]]), cidx, "grad_chip_sum_%d" % gi)
        for i, o in zip(ids, outs):
            fs[i] = o
    return rs_pair_gather(fs)


def adamw(w, g, m, v, name):
    shape = w.shape
    if w.ndim == 2:
        block, grid, index = shape, (1,), (lambda i: (0, 0))
    else:
        slab = shape[2:]
        unit = 4 * int(np.prod(slab[:-2] or (1,))) * (-(-slab[-1] // 128) * 128)
        if len(slab) >= 2:
            unit *= -(-slab[-2] // 8) * 8
        k = shape[1]
        tr = k
        if k * unit > 2 ** 21:
            tr = max(t for t in range(8, k, 8) if k % t == 0 and t * unit <= 2 ** 21)
        block, grid = (None, tr) + tuple(slab), (shape[0], k // tr)
        index = lambda l, i: (l, i) + (0,) * len(slab)
        if tr < min(k, 64) and len(slab) == 1:
            tc = max(t for t in range(128, slab[0] + 1, 128) if slab[0] % t == 0 and k * t * 4 <= 2 ** 21)
            block, grid = (None, k, tc), (shape[0], slab[0] // tc)
            index = lambda l, i: (l, 0, i)

    def body(w_ref, g_ref, m_ref, v_ref, d_ref, nm_ref, nv_ref):
        g_ = g_ref[...]
        m_new = ADAM_B1 * m_ref[...] + (1.0 - ADAM_B1) * g_
        v_new = ADAM_B2 * v_ref[...] + (1.0 - ADAM_B2) * (g_ * g_)
        m_hat = m_new / (1.0 - ADAM_B1 ** ADAM_STEP)
        v_hat = v_new / (1.0 - ADAM_B2 ** ADAM_STEP)
        d_ref[...] = -ADAM_LR * (m_hat / (jnp.sqrt(v_hat) + ADAM_EPS) + ADAM_WD * w_ref[...])
        nm_ref[...] = m_new
        nv_ref[...] = v_new

    spec = pl.BlockSpec(block, index)
    return pl.pallas_call(
        body, name=name, grid=grid, in_specs=[spec] * 4, out_specs=[spec] * 3,
        out_shape=[jax.ShapeDtypeStruct(shape, F32)] * 3, compiler_params=_cparams(),
    )(w, g, m, v)


_WEIGHTS = ['meta', 'ffn1_w_gate', 'ffn1_w_up', 'ffn1_w_down', 'ln1_g', 'ln1_b', 'w_in', 'mla_q_norm_g', 'mla_w_uq',
            'mla_kv_norm_g', 'mla_w_ukv', 'mla_w_o', 'conv_w', 'conv_b', 'conv_w_out', 's5_a_re', 's5_a_im', 's5_log_dt',
            's5_b_re', 's5_b_im', 's5_c_re', 's5_c_im', 's5_d', 's5_w_glu', 's5_b_glu', 's5_w_out', 'w_o', 'ln2_g', 'ln2_b',
            'ffn2_w_gate', 'ffn2_w_up', 'ffn2_w_down', 'ln3_g', 'ln3_b']


def _pad_to(flat, n):
    return jnp.concatenate([flat, jnp.zeros((n - flat.shape[0],), flat.dtype)])


def _shard_of(full, axis):
    if axis == 1:
        return _shard_cols(full)
    if axis == 'T':
        return full.T.reshape(N_SHARD, full.shape[1] // N_SHARD, full.shape[0])
    return full.reshape(N_SHARD, full.shape[0] // N_SHARD, full.shape[1])


_FFN_KEY = {'gate': 'wg', 'up': 'wu', 'down': 'wd'}


def _pad_rows(a, axis):
    k = a.shape[axis]
    extra = -k % 32
    if not extra:
        return a
    return jnp.pad(a, [(0, extra) if d == axis else (0, 0) for d in range(a.ndim)])


def _step(env):
    w = {n: env[n] for n in _WEIGHTS}
    mom = {n: env['m_' + n] for n in _WEIGHTS}
    var = {n: env['v_' + n] for n in _WEIGHTS}
    cidx = lax.axis_index("c").astype(jnp.int32).reshape(1)
    chip = 2 * lax.axis_index("x") + lax.axis_index("y")
    big_names = [n for n, _ in _BIG]
    nb = len(big_names)

    kept_t = [n for n, a in _BIG if a == 'T']
    own = {n: (jnp.swapaxes(w[n], 1, 2) if n in kept_t else w[n]) for n in big_names}
    first = [n for n in big_names if n.startswith('ffn1')]
    mix = [n for n in big_names if not n.startswith('ffn')]
    last = [n for n in big_names if n.startswith('ffn2')]
    rest = mix + last
    nm, nr = len(mix), len(mix) + len(last)
    src = lambda n, li: _pad_rows(own[n][li].astype(BF16), 0)
    gathered_first, (conv_w_st, meta_st) = all_gather_shards([src(n, 0) for n in first], [w['conv_w'], w['meta']])
    later = [src(n, 0) for n in rest] + [src(n, 1) for n in big_names]
    lands = [lax.empty((N_SHARD,) + s.shape, BF16) for s in later]
    g_send, g_recv, later_t, lands_t, token = split_start("gather_start", later, lands, gathered_first[0], _gather_copies,
                                                          4 * len(later))

    def weights_of(names, st, li, with_small):
        small = None
        if with_small:
            small = {n: w[n][li] for n in _REPL}
            small['conv_w'] = _nat_cols(conv_w_st[:, li])
        return compute_weights({n: a[:, :own[n].shape[1]] for n, a in zip(names, st)}, small)

    x2d = env['x'][0]
    lp = x2d.shape[0] + X0
    tabs = _rope_tables(lp)
    h = jnp.concatenate([jnp.zeros((PAD, D_MODEL), F32), _nat_cols(meta_st), x2d], axis=0) + token[0, 0]
    W0 = weights_of(first, gathered_first, 0, True)
    ffn1 = ffn_fwd(h, h.astype(BF16), W0['wg1'], W0['wu1'], W0['wd1'], W0['ln1_g'], W0['ln1_b'], lp)
    later_t, lands_t = split_wait("gather0_wait", g_send, g_recv, later_t, lands_t, ffn1[0], _gather_copies, range(nm))
    W0.update(weights_of(mix, gather_forward(lands_t[:nm], "gather0_forward"), 0, False))
    h, hb, sv0 = layer_fwd(None, None, W0, tabs, lp, ffn1=ffn1, ffn2=False)
    later_t, lands_t = split_wait("gather0b_wait", g_send, g_recv, later_t, lands_t, h, _gather_copies, range(nm, nr))
    W0.update(weights_of(last, gather_forward(lands_t[nm:nr], "gather0b_forward"), 0, False))
    h, hb, sv0['sv3'] = ffn_fwd(h, hb, W0['wg2'], W0['wu2'], W0['wd2'], W0['ln3_g'], W0['ln3_b'], lp)
    _, lands_t = split_wait("gather1_wait", g_send, g_recv, later_t, lands_t, h, _gather_copies, range(nr, len(later)))
    W1 = weights_of(big_names, gather_forward(lands_t[nr:], "gather1_forward"), 1, True)
    h, hb, sv1 = layer_fwd(h, hb, W1, tabs, lp)
    tgt = jnp.concatenate([jnp.zeros((X0, D_MODEL), F32), env['loss_target'][0]], axis=0)
    dh, loss_part = loss_head(h, tgt, lp)
    loss = lax.psum(loss_part[0, 0], ("x", "y", "c"))

    def shards(G, names):
        full = None if all(n.startswith('ffn') for n in names) else reference_grads(G, ffn=False)

        def one(n, a):
            if n.startswith('ffn'):
                return G[_FFN_KEY[n.split('_')[-1]] + n[3]]
            if n == 'w_in':
                return full['w_in_t'].reshape(N_SHARD, D_IN // N_SHARD, D_MODEL)
            return _shard_of(full[n], a)

        return [_pad_rows(one(n, a), 1) for n, a in _BIG if n in names]

    def scatter_start(name, ps, after):
        qs = [lax.dynamic_update_slice_in_dim(jnp.zeros_like(p), lax.dynamic_slice_in_dim(p, chip, 1, axis=0), chip, axis=0)
              for p in ps]
        return split_start(name, ps, qs, after, _scatter_copies, 3 * len(ps))

    dh, G1 = layer_bwd(dh, sv1, W1, tabs, lp)
    p1 = rs_partials(shards(G1, big_names), [BF16] * nb, cidx, "b")
    s1_send, s1_recv, p1_t, q1_t, token1 = scatter_start("scatter1_start", p1, dh)
    dh, g3 = ffn_bwd(dh + token1[0, 0], sv0['sv3'], W0['wg2'], W0['wu2'], W0['wd2'], W0['ln3_g'], lp)
    G0 = dict(wg2=g3['wg'], wu2=g3['wu'], wd2=g3['wd'])
    p0l = rs_partials(shards(G0, last), [BF16] * len(last), cidx, "c")
    sl_send, sl_recv, p0l_t, q0l_t, token0l = scatter_start("scatter0b_start", p0l, dh)
    dh, Gm = layer_bwd(dh + token0l[0, 0], sv0, W0, tabs, lp, ffn1=False, ffn2=False)
    G0.update(Gm, ln3_g=g3['ln_g'], ln3_b=g3['ln_b'])
    p0m = rs_partials(shards(G0, mix), [BF16] * nm, cidx, "d")
    sm_send, sm_recv, p0m_t, q0m_t, token0m = scatter_start("scatter0_start", p0m, dh)
    dh, g1 = ffn_bwd(dh + token0m[0, 0], sv0['sv1'], W0['wg1'], W0['wu1'], W0['wd1'], W0['ln1_g'], lp)
    G0.update(wg1=g1['wg'], wu1=g1['wu'], wd1=g1['wd'], ln1_g=g1['ln_g'], ln1_b=g1['ln_b'])
    _, q1 = split_wait("scatter1_wait", s1_send, s1_recv, p1_t, q1_t, dh, _scatter_copies)
    _, q0_last = split_wait("scatter0b_wait", sl_send, sl_recv, p0l_t, q0l_t, dh, _scatter_copies)
    _, q0_mix = split_wait("scatter0_wait", sm_send, sm_recv, p0m_t, q0m_t, dh, _scatter_copies)
    q0_rest = list(q0_mix) + list(q0_last)
    full = [reference_grads(G0, ffn=False), reference_grads(G1, ffn=False)]

    s_parts = [jnp.stack([full[li][n] for li in range(DEPTH)]).reshape(-1) for n in _REPL + ['conv_w']]
    s_parts.append(dh[PAD:X0].reshape(-1))
    s_sizes = [int(p.shape[0]) for p in s_parts]
    s_rows = -(-sum(s_sizes) // (16 * LANES)) * 16
    g_small = _pad_to(jnp.concatenate(s_parts), s_rows * LANES).reshape(1, s_rows, LANES)
    g_small = jnp.broadcast_to(g_small, (N_SHARD, s_rows, LANES))
    q0_first = rs_chip_scatter(rs_partials(shards(G0, first) + [g_small], [BF16] * len(first) + [F32], cidx, "a"))
    q0 = dict(zip(first + rest, list(q0_first[:-1]) + list(q0_rest)))
    red = rs_finish([[q0[n], q] for n, q in zip(big_names, q1)] + [[q0_first[-1]]])
    f_small = red[-1].reshape(-1)

    grad = {n: r[:, :own[n].shape[1]] for n, r in zip(big_names, red[:nb])}
    off = 0
    for n, sz in zip(_REPL + ['conv_w', 'meta'], s_sizes):
        grad[n] = f_small[off:off + sz]
        off += sz
    for n in _REPL:
        grad[n] = grad[n].reshape(w[n].shape)
    cw = grad['conv_w'].reshape(DEPTH, 3, MIX)
    grad['conv_w'] = lax.dynamic_slice_in_dim(cw, chip * (MIX // N_SHARD), MIX // N_SHARD, axis=2)
    gm = grad['meta'].reshape(N_META, D_MODEL)
    grad['meta'] = lax.dynamic_slice_in_dim(gm, chip * (D_MODEL // N_SHARD), D_MODEL // N_SHARD, axis=1)

    delta, new_m, new_v = {}, {}, {}
    for n in _WEIGHTS:
        if n in kept_t:
            outs = adamw(own[n], grad[n], jnp.swapaxes(mom[n], 1, 2), jnp.swapaxes(var[n], 1, 2), "adamw_" + n)
            grad[n], delta[n], new_m[n], new_v[n] = [jnp.swapaxes(t, 1, 2) for t in [grad[n]] + list(outs)]
        else:
            delta[n], new_m[n], new_v[n] = adamw(w[n], grad[n], mom[n], var[n], "adamw_" + n)
    return (loss, dh[X0:][None], *[grad[n] for n in _WEIGHTS], *[delta[n] for n in _WEIGHTS],
            *[new_m[n] for n in _WEIGHTS], *[new_v[n] for n in _WEIGHTS])


def kernel(x, meta, ffn1_w_gate, ffn1_w_up, ffn1_w_down, ln1_g, ln1_b, w_in, mla_q_norm_g, mla_w_uq, mla_kv_norm_g, mla_w_ukv, mla_w_o, conv_w, conv_b, conv_w_out, s5_a_re, s5_a_im, s5_log_dt, s5_b_re, s5_b_im, s5_c_re, s5_c_im, s5_d, s5_w_glu, s5_b_glu, s5_w_out, w_o, ln2_g, ln2_b, ffn2_w_gate, ffn2_w_up, ffn2_w_down, ln3_g, ln3_b, loss_target, m_meta, m_ffn1_w_gate, m_ffn1_w_up, m_ffn1_w_down, m_ln1_g, m_ln1_b, m_w_in, m_mla_q_norm_g, m_mla_w_uq, m_mla_kv_norm_g, m_mla_w_ukv, m_mla_w_o, m_conv_w, m_conv_b, m_conv_w_out, m_s5_a_re, m_s5_a_im, m_s5_log_dt, m_s5_b_re, m_s5_b_im, m_s5_c_re, m_s5_c_im, m_s5_d, m_s5_w_glu, m_s5_b_glu, m_s5_w_out, m_w_o, m_ln2_g, m_ln2_b, m_ffn2_w_gate, m_ffn2_w_up, m_ffn2_w_down, m_ln3_g, m_ln3_b, v_meta, v_ffn1_w_gate, v_ffn1_w_up, v_ffn1_w_down, v_ln1_g, v_ln1_b, v_w_in, v_mla_q_norm_g, v_mla_w_uq, v_mla_kv_norm_g, v_mla_w_ukv, v_mla_w_o, v_conv_w, v_conv_b, v_conv_w_out, v_s5_a_re, v_s5_a_im, v_s5_log_dt, v_s5_b_re, v_s5_b_im, v_s5_c_re, v_s5_c_im, v_s5_d, v_s5_w_glu, v_s5_b_glu, v_s5_w_out, v_w_o, v_ln2_g, v_ln2_b, v_ffn2_w_gate, v_ffn2_w_up, v_ffn2_w_down, v_ln3_g, v_ln3_b):
    return _step(dict(locals()))
```

```python
import functools
import math

import numpy as np
import jax
import jax.numpy as jnp
from jax import lax
from jax.experimental import pallas as pl
from jax.experimental.pallas import tpu as pltpu

F32 = jnp.float32
BF16 = jnp.bfloat16

D_MODEL = 1024
DEPTH = 2
N_META = 16
PAD = 112
X0 = PAD + N_META
N_HEADS = 8
D_NOPE = 64
D_ROPE = 32
D_V = 64
Q_RANK = 384
KV_RANK = 256
MIX = 512
S5_GROUPS = 32
S5_GROUP = 16
S5_STATE = 64
S5_LANES = S5_GROUPS * S5_STATE
D_FF = 2816
N_SHARD = 4
FF_SHARD = D_FF // N_SHARD
D_IN = 5792
P_IN = 6144
ALPHA = (2.0 * DEPTH) ** 0.25
LN_EPS = 1e-5
RMS_EPS = 1e-6
ATT_SCALE = (D_NOPE + D_ROPE) ** -0.5
ROPE_BASE = 10000.0
ADAM_LR, ADAM_B1, ADAM_B2, ADAM_EPS, ADAM_WD, ADAM_STEP = 0.001, 0.9, 0.999, 1e-08, 0.01, 10
SCAN_CHUNK = 128
VMEM_LIMIT = 52 * 2 ** 20
WGRAD = BF16
MESH = pl.DeviceIdType.MESH


def _cparams(**kw):
    return pltpu.CompilerParams(vmem_limit_bytes=VMEM_LIMIT, **kw)


def _tile(n):
    if n <= 1088:
        return n
    for t in (1024, 544, 512, 272, 256, 128):
        if n % t == 0:
            return t
    return n


def _row_tile(lp):
    for t in (544, 272, 128):
        if lp % t == 0:
            return t
    return lp


def _sigmoid(x):
    return 1.0 / (1.0 + jnp.exp(-x))


_GELU_C = math.sqrt(2.0 / math.pi)


def _gelu(x):
    return 0.5 * x * (1.0 + jnp.tanh(_GELU_C * (x + 0.044715 * x * x * x)))


def _gelu_grad(x):
    t = jnp.tanh(_GELU_C * (x + 0.044715 * x * x * x))
    return 0.5 * (1.0 + t) + 0.5 * x * (1.0 - t * t) * _GELU_C * (1.0 + 3.0 * 0.044715 * x * x)


def _dot(a, b, ca, cb, precision=None):
    return lax.dot_general(a, b, (((ca,), (cb,)), ((), ())), preferred_element_type=F32, precision=precision)


def matmul(a, b, *, name, ta=False, tb=False, ab='n', bb='n', res=None, res_scale=1.0, scale=1.0, out_dtype=F32):
    if ta:
        _, K, M = a.shape
    else:
        _, M, K = a.shape
    if tb:
        _, N, K2 = b.shape
    else:
        _, K2, N = b.shape
    assert K == K2, (a.shape, b.shape)
    n_out = max(a.shape[0] if ab == 'o' else 1, b.shape[0] if bb == 'o' else 1)
    n_red = max(a.shape[0] if ab == 'r' else 1, b.shape[0] if bb == 'r' else 1)
    tm, tn = _tile(M), _tile(N)
    tk = K if K <= 2304 else _tile(K)
    nkt = K // tk
    n_steps = n_red * nkt

    def bsel(mode, o, r):
        if mode == 'o':
            return o
        if mode == 'r':
            return r // nkt if nkt > 1 else r
        return 0

    def ksel(r):
        if nkt == 1:
            return 0
        return r % nkt if n_red > 1 else r

    a_map = (lambda o, i, j, r: (bsel(ab, o, r), ksel(r), i)) if ta else (lambda o, i, j, r: (bsel(ab, o, r), i, ksel(r)))
    b_map = (lambda o, i, j, r: (bsel(bb, o, r), j, ksel(r))) if tb else (lambda o, i, j, r: (bsel(bb, o, r), ksel(r), j))
    o_map = lambda o, i, j, r: (o, i, j)
    in_specs = [pl.BlockSpec((None, tk, tm) if ta else (None, tm, tk), a_map),
                pl.BlockSpec((None, tn, tk) if tb else (None, tk, tn), b_map)]
    operands = [a, b]
    if res is not None:
        in_specs.append(pl.BlockSpec((None, tm, tn), o_map))
        operands.append(res)
    has_res = res is not None

    def body(*refs):
        a_ref, b_ref = refs[0], refs[1]
        res_ref = refs[2] if has_res else None
        o_ref = refs[3] if has_res else refs[2]
        part = _dot(a_ref[...].astype(BF16), b_ref[...].astype(BF16), 0 if ta else 1, 1 if tb else 0)

        def finish(acc):
            v = acc if scale == 1.0 else acc * scale
            if has_res:
                v = v + res_scale * res_ref[...].astype(F32)
            o_ref[...] = v.astype(o_ref.dtype)

        if n_steps == 1:
            finish(part)
        else:
            acc_ref = refs[-1]
            r = pl.program_id(3)

            @pl.when(r == 0)
            def _():
                acc_ref[...] = part

            @pl.when(r > 0)
            def _():
                acc_ref[...] += part

            @pl.when(r == n_steps - 1)
            def _():
                finish(acc_ref[...])

    return pl.pallas_call(
        body, name=name,
        grid=(n_out, M // tm, N // tn, n_steps),
        in_specs=in_specs,
        out_specs=pl.BlockSpec((None, tm, tn), o_map),
        out_shape=jax.ShapeDtypeStruct((n_out, M, N), out_dtype),
        scratch_shapes=[pltpu.VMEM((tm, tn), F32)] if n_steps > 1 else [],
        compiler_params=_cparams(),
    )(*operands)


def rowwise(fn, rows, pars, outs, accs=(), *, name, lp):
    tm = _row_tile(lp)
    n_rows, n_pars, n_outs, n_accs = len(rows), len(pars), len(outs), len(accs)
    in_specs = [pl.BlockSpec((tm, w), functools.partial(lambda i, cb: (i, cb), cb=cb)) for _, w, cb in rows]
    in_specs += [pl.BlockSpec(p.shape, functools.partial(lambda i, nd: (0,) * nd, nd=p.ndim)) for p in pars]
    out_specs = [pl.BlockSpec((tm, w), lambda i: (i, 0)) for w, _ in outs]
    out_specs += [pl.BlockSpec(s, functools.partial(lambda i, nd: (0,) * nd, nd=len(s))) for s, _ in accs]
    out_shape = [jax.ShapeDtypeStruct((lp, w), dt) for w, dt in outs]
    out_shape += [jax.ShapeDtypeStruct(s, dt) for s, dt in accs]

    def body(*refs):
        i = pl.program_id(0)
        rv = [r[...] for r in refs[:n_rows]]
        pv = [r[...] for r in refs[n_rows:n_rows + n_pars]]
        o_refs = refs[n_rows + n_pars:n_rows + n_pars + n_outs]
        a_refs = refs[n_rows + n_pars + n_outs:]
        ov, av = fn(i * tm, rv, pv)
        for r, v in zip(o_refs, ov):
            r[...] = v.astype(r.dtype)
        if n_accs:
            @pl.when(i == 0)
            def _():
                for r, v in zip(a_refs, av):
                    r[...] = v.astype(r.dtype)

            @pl.when(i > 0)
            def _():
                for r, v in zip(a_refs, av):
                    r[...] += v.astype(r.dtype)

    res = pl.pallas_call(
        body, name=name, grid=(lp // tm,), in_specs=in_specs, out_specs=out_specs, out_shape=out_shape,
        compiler_params=_cparams(),
    )(*[r[0] for r in rows], *pars)
    return res


def _row_mask(row0, shape):
    return (row0 + lax.broadcasted_iota(jnp.int32, shape, 0)) >= PAD


def ffn_up(hb, wg, wu, lp):
    tm = _row_tile(lp)

    def body(h_ref, wg_ref, wu_ref, ab_ref, hid_ref):
        h = h_ref[...]
        a = _dot(h, wg_ref[...], 1, 1)
        b = _dot(h, wu_ref[...], 1, 1)
        ab_ref[0] = a.astype(BF16)
        ab_ref[1] = b.astype(BF16)
        hid_ref[...] = (a * _sigmoid(a) * b).astype(BF16)

    wspec = pl.BlockSpec((None, FF_SHARD, D_MODEL), lambda j, i: (j, 0, 0))
    return pl.pallas_call(
        body, name="ffn_up", grid=(N_SHARD, lp // tm),
        in_specs=[pl.BlockSpec((tm, D_MODEL), lambda j, i: (i, 0)), wspec, wspec],
        out_specs=[pl.BlockSpec((None, 2, tm, FF_SHARD), lambda j, i: (j, 0, i, 0)),
                   pl.BlockSpec((None, tm, FF_SHARD), lambda j, i: (j, i, 0))],
        out_shape=[jax.ShapeDtypeStruct((N_SHARD, 2, lp, FF_SHARD), BF16),
                   jax.ShapeDtypeStruct((N_SHARD, lp, FF_SHARD), BF16)],
        compiler_params=_cparams(),
    )(hb, wg, wu)


def _layer_norm(z, g, b):
    mu = jnp.mean(z, axis=-1, keepdims=True)
    zc = z - mu
    var = jnp.mean(zc * zc, axis=-1, keepdims=True)
    return zc * lax.rsqrt(var + LN_EPS) * g + b


def mm_res_ln(a, w, res, g, b, *, scale, name, lp):
    n_red, _, K = a.shape
    tm = _row_tile(lp)

    def body(a_ref, w_ref, res_ref, g_ref, b_ref, z_ref, h_ref, hb_ref, acc_ref):
        r = pl.program_id(1)
        part = _dot(a_ref[...].astype(BF16), w_ref[...], 1, 0)

        @pl.when(r == 0)
        def _():
            acc_ref[...] = part

        @pl.when(r > 0)
        def _():
            acc_ref[...] += part

        @pl.when(r == n_red - 1)
        def _():
            z = ALPHA * res_ref[...] + scale * acc_ref[...]
            z_ref[...] = z
            hn = _layer_norm(z, g_ref[...], b_ref[...])
            h_ref[...] = hn
            hb_ref[...] = hn.astype(BF16)

    row = pl.BlockSpec((tm, D_MODEL), lambda i, r: (i, 0))
    par = pl.BlockSpec((1, D_MODEL), lambda i, r: (0, 0))
    return pl.pallas_call(
        body, name=name, grid=(lp // tm, n_red),
        in_specs=[pl.BlockSpec((None, tm, K), lambda i, r: (r, i, 0)),
                  pl.BlockSpec((None, K, D_MODEL), lambda i, r: (r, 0, 0)), row, par, par],
        out_specs=[row, row, row],
        out_shape=[jax.ShapeDtypeStruct((lp, D_MODEL), F32), jax.ShapeDtypeStruct((lp, D_MODEL), F32),
                   jax.ShapeDtypeStruct((lp, D_MODEL), BF16)],
        scratch_shapes=[pltpu.VMEM((tm, D_MODEL), F32)],
        compiler_params=_cparams(),
    )(a, w, res, g, b)


def ln_bwd(dh, z, g, *, fscale, name, lp):
    def fn(row0, rv, pv):
        dh_, z_ = rv
        g_, = pv
        mu = jnp.mean(z_, axis=-1, keepdims=True)
        zc = z_ - mu
        rstd = lax.rsqrt(jnp.mean(zc * zc, axis=-1, keepdims=True) + LN_EPS)
        xh = zc * rstd
        dxh = dh_ * g_
        m1 = jnp.mean(dxh, axis=-1, keepdims=True)
        m2 = jnp.mean(dxh * xh, axis=-1, keepdims=True)
        dz = rstd * (dxh - m1 - xh * m2)
        return ((dz, fscale * dz),
                (jnp.sum(dh_ * xh, axis=0, keepdims=True), jnp.sum(dh_, axis=0, keepdims=True)))

    return rowwise(fn, [(dh, D_MODEL, 0), (z, D_MODEL, 0)], [g], [(D_MODEL, F32), (D_MODEL, BF16)],
                   [((1, D_MODEL), F32), ((1, D_MODEL), F32)], name=name, lp=lp)


def ffn_down_bwd(dfb, wd, ab, lp):
    tm = _row_tile(lp)

    def body(df_ref, w_ref, ab_ref, da_ref, db_ref):
        dhid = _dot(df_ref[...], w_ref[...], 1, 1)
        a = ab_ref[0].astype(F32)
        b = ab_ref[1].astype(F32)
        sg = _sigmoid(a)
        da_ref[...] = (dhid * b * (sg * (1.0 + a * (1.0 - sg)))).astype(BF16)
        db_ref[...] = (dhid * (a * sg)).astype(BF16)

    ospec = pl.BlockSpec((None, tm, FF_SHARD), lambda j, i: (j, i, 0))
    return pl.pallas_call(
        body, name="ffn_down_bwd", grid=(N_SHARD, lp // tm),
        in_specs=[pl.BlockSpec((tm, D_MODEL), lambda j, i: (i, 0)),
                  pl.BlockSpec((None, FF_SHARD, D_MODEL), lambda j, i: (j, 0, 0)),
                  pl.BlockSpec((None, 2, tm, FF_SHARD), lambda j, i: (j, 0, i, 0))],
        out_specs=[ospec, ospec],
        out_shape=[jax.ShapeDtypeStruct((N_SHARD, lp, FF_SHARD), BF16)] * 2,
        compiler_params=_cparams(),
    )(dfb, wd, ab)


def ffn_dx(da, db, wg, wu, dz, lp):
    tm = _row_tile(lp)

    def body(da_ref, db_ref, wg_ref, wu_ref, dz_ref, o_ref, acc_ref):
        j = pl.program_id(1)
        part = _dot(da_ref[...], wg_ref[...], 1, 0) + _dot(db_ref[...], wu_ref[...], 1, 0)

        @pl.when(j == 0)
        def _():
            acc_ref[...] = part

        @pl.when(j > 0)
        def _():
            acc_ref[...] += part

        @pl.when(j == N_SHARD - 1)
        def _():
            o_ref[...] = acc_ref[...] + ALPHA * dz_ref[...]

    aspec = pl.BlockSpec((None, tm, FF_SHARD), lambda i, j: (j, i, 0))
    wspec = pl.BlockSpec((None, FF_SHARD, D_MODEL), lambda i, j: (j, 0, 0))
    row = pl.BlockSpec((tm, D_MODEL), lambda i, j: (i, 0))
    return pl.pallas_call(
        body, name="ffn_dx", grid=(lp // tm, N_SHARD), in_specs=[aspec, aspec, wspec, wspec, row], out_specs=row,
        out_shape=jax.ShapeDtypeStruct((lp, D_MODEL), F32), scratch_shapes=[pltpu.VMEM((tm, D_MODEL), F32)],
        compiler_params=_cparams(),
    )(da, db, wg, wu, dz)


def ffn_fwd(h, hb, wg, wu, wd, g, b, lp):
    ab, hid = ffn_up(hb, wg, wu, lp)
    z, hn, hnb = mm_res_ln(hid, wd, h, g, b, scale=0.5, name="ffn_down_ln", lp=lp)
    return hn, hnb, dict(hb=hb, ab=ab, hid=hid, z=z)


def ffn_bwd(dh, sv, wg, wu, wd, g, lp):
    dz, dfb, dg, db = ln_bwd(dh, sv['z'], g, fscale=0.5, name="ffn_ln_bwd", lp=lp)
    da, dbb = ffn_down_bwd(dfb, wd, sv['ab'], lp)
    d_wd = matmul(sv['hid'], dfb[None], ta=True, ab='o', out_dtype=WGRAD, name="ffn_dwd")
    d_wg = matmul(da, sv['hb'][None], ta=True, ab='o', out_dtype=WGRAD, name="ffn_dwg")
    d_wu = matmul(dbb, sv['hb'][None], ta=True, ab='o', out_dtype=WGRAD, name="ffn_dwu")
    dh_in = ffn_dx(da, dbb, wg, wu, dz, lp)
    return dh_in, dict(wg=d_wg, wu=d_wu, wd=d_wd, ln_g=dg, ln_b=db)


def _rope_tables(lp):
    pos = np.arange(lp, dtype=np.float32) - PAD
    inv = ROPE_BASE ** (-np.arange(0, D_ROPE, 2, dtype=np.float32) / D_ROPE)
    ang = pos[:, None] * inv[None, :]
    cos = np.concatenate([np.cos(ang), np.cos(ang)], axis=1).astype(np.float32)
    sin = np.concatenate([np.sin(ang), np.sin(ang)], axis=1).astype(np.float32)
    rot = np.zeros((D_ROPE, D_ROPE), np.float32)
    half = D_ROPE // 2
    for j in range(half):
        rot[j + half, j] = -1.0
        rot[j, j + half] = 1.0
    return jnp.asarray(cos), jnp.asarray(sin), jnp.asarray(rot)


def _rot(x, rot):
    return _dot(x, rot, 1, 0, precision=lax.Precision.HIGHEST)


def _rms(x, g):
    r = lax.rsqrt(jnp.mean(x * x, axis=-1, keepdims=True) + RMS_EPS)
    return x * r * g


def mla_prep(proj, cos, sin, rot, qg, kvg, lp):
    def fn(row0, rv, pv):
        cq, krb, ckv, c, s = rv
        qg_, kvg_, rot_ = pv
        kr = krb[:, :D_ROPE]
        return ((_rms(cq, qg_), _rms(ckv, kvg_), kr * c + _rot(kr, rot_) * s), ())

    return rowwise(fn, [(proj, Q_RANK, 0), (proj, 128, 3), (proj, KV_RANK, 2), (cos, D_ROPE, 0), (sin, D_ROPE, 0)],
                   [qg, kvg, rot], [(Q_RANK, BF16), (KV_RANK, BF16), (D_ROPE, BF16)], name="mla_prep", lp=lp)


def mla_heads(cqn, ckvn, cos, sin, rot, wqn, wqr, wkn, wv, lp):
    tm = _row_tile(lp)

    def body(cq_ref, ckv_ref, c_ref, s_ref, rot_ref, wqn_ref, wqr_ref, wkn_ref, wv_ref, qn_ref, qr_ref, kn_ref, v_ref):
        cq = cq_ref[...]
        ckv = ckv_ref[...]
        qn_ref[...] = _dot(cq, wqn_ref[...], 1, 0).astype(BF16)
        qr = _dot(cq, wqr_ref[...], 1, 0)
        qr_ref[...] = (qr * c_ref[...] + _rot(qr, rot_ref[...]) * s_ref[...]).astype(BF16)
        kn_ref[...] = _dot(ckv, wkn_ref[...], 1, 0).astype(BF16)
        v_ref[...] = _dot(ckv, wv_ref[...], 1, 0).astype(BF16)

    def row(w):
        return pl.BlockSpec((tm, w), lambda h, i: (i, 0))

    def wspec(k, n):
        return pl.BlockSpec((None, k, n), lambda h, i: (h, 0, 0))

    def ospec(n):
        return pl.BlockSpec((None, tm, n), lambda h, i: (h, i, 0))

    return pl.pallas_call(
        body, name="mla_heads", grid=(N_HEADS, lp // tm),
        in_specs=[row(Q_RANK), row(KV_RANK), row(D_ROPE), row(D_ROPE),
                  pl.BlockSpec((D_ROPE, D_ROPE), lambda h, i: (0, 0)),
                  wspec(Q_RANK, D_NOPE), wspec(Q_RANK, D_ROPE), wspec(KV_RANK, D_NOPE), wspec(KV_RANK, D_V)],
        out_specs=[ospec(D_NOPE), ospec(D_ROPE), ospec(D_NOPE), ospec(D_V)],
        out_shape=[jax.ShapeDtypeStruct((N_HEADS, lp, D_NOPE), BF16), jax.ShapeDtypeStruct((N_HEADS, lp, D_ROPE), BF16),
                   jax.ShapeDtypeStruct((N_HEADS, lp, D_NOPE), BF16), jax.ShapeDtypeStruct((N_HEADS, lp, D_V), BF16)],
        compiler_params=_cparams(),
    )(cqn, ckvn, cos, sin, rot, wqn, wqr, wkn, wv)


def _att_probs(qn, qr, kn, kr, row0, tq, lp):
    s = (_dot(qn, kn, 1, 1) + _dot(qr, kr, 1, 1)) * ATT_SCALE
    qi = row0 + lax.broadcasted_iota(jnp.int32, (tq, lp), 0)
    ki = lax.broadcasted_iota(jnp.int32, (tq, lp), 1)
    s = jnp.where((ki <= qi) & (ki >= PAD), s, -1e30)
    p = jnp.exp(s - jnp.max(s, axis=-1, keepdims=True))
    return p / jnp.sum(p, axis=-1, keepdims=True)


def _att_specs(lp):
    def hspec(n):
        return pl.BlockSpec((None, lp, n), lambda h, i: (h, 0, 0))

    return hspec, pl.BlockSpec((lp, D_ROPE), lambda h, i: (0, 0))


def _att_tiles(lp):
    tiles, r = [(0, X0)], X0
    while r < lp:
        tiles.append((r, 256))
        r += 256
    assert r == lp
    return tiles


def attn_fwd(qn, qr, kn, v, kr, lp):
    kspec, krspec = _att_specs(lp)

    def body(qn_ref, qr_ref, kn_ref, v_ref, kr_ref, o_ref):
        for r0, rows in _att_tiles(lp):
            ke, q = r0 + rows, slice(r0, r0 + rows)
            p = _att_probs(qn_ref[q, :], qr_ref[q, :], kn_ref[0:ke, :], kr_ref[0:ke, :], r0, rows, ke)
            o_ref[q, :] = _dot(p.astype(BF16), v_ref[0:ke, :], 1, 0).astype(BF16)

    return pl.pallas_call(
        body, name="attn_fwd", grid=(N_HEADS, 1),
        in_specs=[kspec(D_NOPE), kspec(D_ROPE), kspec(D_NOPE), kspec(D_V), krspec],
        out_specs=kspec(D_V), out_shape=jax.ShapeDtypeStruct((N_HEADS, lp, D_V), BF16),
        compiler_params=_cparams(),
    )(qn, qr, kn, v, kr)


def attn_bwd(qn, qr, kn, v, kr, do, lp):
    kspec, krspec = _att_specs(lp)

    def body(qn_ref, qr_ref, kn_ref, v_ref, kr_ref, do_ref, dqn_ref, dqr_ref, dkn_ref, dv_ref, dkr_ref):
        dkn_ref[...] = jnp.zeros_like(dkn_ref)
        dv_ref[...] = jnp.zeros_like(dv_ref)

        @pl.when(pl.program_id(0) == 0)
        def _():
            dkr_ref[...] = jnp.zeros_like(dkr_ref)

        for r0, rows in _att_tiles(lp):
            ke, q = r0 + rows, slice(r0, r0 + rows)
            qn_, qr_, do_ = qn_ref[q, :], qr_ref[q, :], do_ref[q, :]
            kn_, v_, kr_ = kn_ref[0:ke, :], v_ref[0:ke, :], kr_ref[0:ke, :]
            p = _att_probs(qn_, qr_, kn_, kr_, r0, rows, ke)
            dp = _dot(do_, v_, 1, 1)
            delta = jnp.sum(p * dp, axis=-1, keepdims=True)
            ds = (p * (dp - delta) * ATT_SCALE).astype(BF16)
            dqn_ref[q, :] = _dot(ds, kn_, 1, 0).astype(BF16)
            dqr_ref[q, :] = _dot(ds, kr_, 1, 0)
            dkn_ref[0:ke, :] += _dot(ds, qn_, 0, 0)
            dv_ref[0:ke, :] += _dot(p.astype(BF16), do_, 0, 0)
            dkr_ref[0:ke, :] += _dot(ds, qr_, 0, 0)

    return pl.pallas_call(
        body, name="attn_bwd", grid=(N_HEADS, 1),
        in_specs=[kspec(D_NOPE), kspec(D_ROPE), kspec(D_NOPE), kspec(D_V), krspec, kspec(D_V)],
        out_specs=[kspec(D_NOPE), kspec(D_ROPE), kspec(D_NOPE), kspec(D_V), krspec],
        out_shape=[jax.ShapeDtypeStruct((N_HEADS, lp, D_NOPE), BF16), jax.ShapeDtypeStruct((N_HEADS, lp, D_ROPE), F32),
                   jax.ShapeDtypeStruct((N_HEADS, lp, D_NOPE), F32), jax.ShapeDtypeStruct((N_HEADS, lp, D_V), F32),
                   jax.ShapeDtypeStruct((lp, D_ROPE), F32)],
        compiler_params=_cparams(),
    )(qn, qr, kn, v, kr, do)


def mla_heads_bwd(dqn, dqr, dkn, dv, cos, sin, rot, wqn, wqr, wkn, wv, lp):
    tm = _row_tile(lp)

    def body(dqn_ref, dqr_ref, dkn_ref, dv_ref, c_ref, s_ref, rot_ref, wqn_ref, wqr_ref, wkn_ref, wv_ref,
             dcq_ref, dckv_ref, dqrp_ref):
        h = pl.program_id(1)
        dqr_ = dqr_ref[...]
        dqrp = (dqr_ * c_ref[...] - _rot(dqr_ * s_ref[...], rot_ref[...])).astype(BF16)
        dqrp_ref[...] = dqrp
        dcq = _dot(dqn_ref[...], wqn_ref[...], 1, 1) + _dot(dqrp, wqr_ref[...], 1, 1)
        dckv = _dot(dkn_ref[...].astype(BF16), wkn_ref[...], 1, 1) + _dot(dv_ref[...].astype(BF16), wv_ref[...], 1, 1)

        @pl.when(h == 0)
        def _():
            dcq_ref[...] = dcq
            dckv_ref[...] = dckv

        @pl.when(h > 0)
        def _():
            dcq_ref[...] += dcq
            dckv_ref[...] += dckv

    def hspec(n):
        return pl.BlockSpec((None, tm, n), lambda i, h: (h, i, 0))

    def row(w):
        return pl.BlockSpec((tm, w), lambda i, h: (i, 0))

    def wspec(k, n):
        return pl.BlockSpec((None, k, n), lambda i, h: (h, 0, 0))

    return pl.pallas_call(
        body, name="mla_heads_bwd", grid=(lp // tm, N_HEADS),
        in_specs=[hspec(D_NOPE), hspec(D_ROPE), hspec(D_NOPE), hspec(D_V), row(D_ROPE), row(D_ROPE),
                  pl.BlockSpec((D_ROPE, D_ROPE), lambda i, h: (0, 0)),
                  wspec(Q_RANK, D_NOPE), wspec(Q_RANK, D_ROPE), wspec(KV_RANK, D_NOPE), wspec(KV_RANK, D_V)],
        out_specs=[row(Q_RANK), row(KV_RANK), hspec(D_ROPE)],
        out_shape=[jax.ShapeDtypeStruct((lp, Q_RANK), F32), jax.ShapeDtypeStruct((lp, KV_RANK), F32),
                   jax.ShapeDtypeStruct((N_HEADS, lp, D_ROPE), BF16)],
        compiler_params=_cparams(),
    )(dqn, dqr, dkn, dv, cos, sin, rot, wqn, wqr, wkn, wv)


def _rms_bwd(dy, x, g):
    r = lax.rsqrt(jnp.mean(x * x, axis=-1, keepdims=True) + RMS_EPS)
    n = x * r
    dn = dy * g
    dx = r * (dn - n * jnp.mean(dn * n, axis=-1, keepdims=True))
    return dx, jnp.sum(dy * n, axis=0, keepdims=True)


def mla_prep_bwd(dcq, dckv, dkr, proj, cos, sin, rot, qg, kvg, lp):
    def fn(row0, rv, pv):
        dcq_, dckv_, dkr_, cq, ckv, c, s = rv
        qg_, kvg_, rot_ = pv
        dxq, dgq = _rms_bwd(dcq_, cq, qg_)
        dxkv, dgkv = _rms_bwd(dckv_, ckv, kvg_)
        dkr_raw = dkr_ * c - _rot(dkr_ * s, rot_)
        return ((dxq, dxkv, dkr_raw), (dgq, dgkv))

    return rowwise(fn, [(dcq, Q_RANK, 0), (dckv, KV_RANK, 0), (dkr, D_ROPE, 0), (proj, Q_RANK, 0), (proj, KV_RANK, 2),
                        (cos, D_ROPE, 0), (sin, D_ROPE, 0)], [qg, kvg, rot],
                   [(Q_RANK, BF16), (KV_RANK, BF16), (D_ROPE, BF16)], [((1, Q_RANK), F32), ((1, KV_RANK), F32)],
                   name="mla_prep_bwd", lp=lp)


def _shift_down(x, d, rows):
    return jnp.where(rows >= d, pltpu.roll(x, d, 0), 0.0)


def _shift_up(x, d, rows, n):
    return jnp.where(rows < n - d, pltpu.roll(x, n - d, 0), 0.0)


_CONV_W = 128
_XB, _BG, _CG = 1024 // _CONV_W, 1536 // _CONV_W, 2048 // _CONV_W


def _conv_specs(lp):
    def pspec(base):
        return pl.BlockSpec((lp, _CONV_W), functools.partial(lambda c, base: (0, base + c), base=base))

    col = pl.BlockSpec((lp, _CONV_W), lambda c: (0, c))
    wspec = pl.BlockSpec((3, _CONV_W), lambda c: (0, c))
    bspec = pl.BlockSpec((1, _CONV_W), lambda c: (0, c))
    return pspec, col, wspec, bspec


def _conv_core(xbar, cg, w, bias, lp):
    rows = lax.broadcasted_iota(jnp.int32, (lp, _CONV_W), 0)
    u = jnp.where(rows >= PAD, cg * xbar, 0.0)
    u1 = _shift_down(u, 1, rows)
    u2 = _shift_down(u, 2, rows)
    y = bias + w[0:1] * u2 + w[1:2] * u1 + w[2:3] * u
    return rows, u, u1, u2, y


def conv_fwd(proj, w, bias, lp):
    pspec, col, wspec, bspec = _conv_specs(lp)

    def body(x_ref, b_ref, c_ref, w_ref, bias_ref, v_ref):
        _, _, _, _, y = _conv_core(x_ref[...], c_ref[...], w_ref[...], bias_ref[...], lp)
        v_ref[...] = (b_ref[...] * y).astype(BF16)

    return pl.pallas_call(
        body, name="conv_fwd", grid=(MIX // _CONV_W,),
        in_specs=[pspec(_XB), pspec(_BG), pspec(_CG), wspec, bspec], out_specs=col,
        out_shape=jax.ShapeDtypeStruct((lp, MIX), BF16), compiler_params=_cparams(),
    )(proj, proj, proj, w, bias)


def conv_bwd(dv, proj, w, bias, lp):
    pspec, col, wspec, bspec = _conv_specs(lp)

    def body(dv_ref, x_ref, b_ref, c_ref, w_ref, bias_ref, dx_ref, db_ref, dc_ref, dw_ref, dbias_ref):
        xbar, cg, w_ = x_ref[...], c_ref[...], w_ref[...]
        rows, u, u1, u2, y = _conv_core(xbar, cg, w_, bias_ref[...], lp)
        dv_ = dv_ref[...]
        db_ref[...] = (dv_ * y).astype(BF16)
        dy = dv_ * b_ref[...]
        dbias_ref[...] = jnp.sum(dy, axis=0, keepdims=True)
        dw_ref[0:1, :] = jnp.sum(dy * u2, axis=0, keepdims=True)
        dw_ref[1:2, :] = jnp.sum(dy * u1, axis=0, keepdims=True)
        dw_ref[2:3, :] = jnp.sum(dy * u, axis=0, keepdims=True)
        du = w_[2:3] * dy + w_[1:2] * _shift_up(dy, 1, rows, lp) + w_[0:1] * _shift_up(dy, 2, rows, lp)
        du = jnp.where(rows >= PAD, du, 0.0)
        dc_ref[...] = (du * xbar).astype(BF16)
        dx_ref[...] = (du * cg).astype(BF16)

    return pl.pallas_call(
        body, name="conv_bwd", grid=(MIX // _CONV_W,),
        in_specs=[col, pspec(_XB), pspec(_BG), pspec(_CG), wspec, bspec],
        out_specs=[col, col, col, wspec, bspec],
        out_shape=[jax.ShapeDtypeStruct((lp, MIX), BF16)] * 3 + [jax.ShapeDtypeStruct((3, MIX), F32),
                                                                jax.ShapeDtypeStruct((1, MIX), F32)],
        compiler_params=_cparams(),
    )(dv, proj, proj, proj, w, bias)


def _s5_disc(a_re, a_im, log_dt, b_re, b_im):
    dt = jnp.exp(log_dt)
    mag = jnp.exp(dt * a_re)
    ab_re, ab_im = mag * jnp.cos(dt * a_im), mag * jnp.sin(dt * a_im)
    den = a_re * a_re + a_im * a_im
    nr, ni = ab_re - 1.0, ab_im
    coef_re = (nr * a_re + ni * a_im) / den
    coef_im = (ni * a_re - nr * a_im) / den
    return ab_re, ab_im, coef_re * b_re - coef_im * b_im, coef_re * b_im + coef_im * b_re


_S5_ROWS = S5_GROUPS * S5_GROUP


def s5_prep(a_re, a_im, log_dt, b_re, b_im):
    def body(ar, ai, ld, br, bi, o0, o1, o2, o3):
        for o, v in zip((o0, o1, o2, o3), _s5_disc(ar[...], ai[...], ld[...], br[...], bi[...])):
            o[...] = v

    return pl.pallas_call(body, name="s5_prep",
                          out_shape=[jax.ShapeDtypeStruct((_S5_ROWS, S5_STATE), F32)] * 4)(a_re, a_im, log_dt, b_re, b_im)


def s5_prep_bwd(a_re, a_im, log_dt, b_re, b_im, d_ab_re, d_ab_im, d_bb_re, d_bb_im, sel):
    def body(ar, ai, ld, br, bi, g0, g1, g2, g3, sel_ref, da_re, da_im, dld, dbr, dbi):
        _, vjp = jax.vjp(_s5_disc, ar[...], ai[...], ld[...], br[...], bi[...])
        c_ar, c_ai, c_ld, c_br, c_bi = vjp((g0[...], g1[...], g2[...], g3[...]))
        s = sel_ref[...]
        hi = lax.Precision.HIGHEST
        da_re[...] = _dot(s, c_ar, 1, 0, precision=hi)
        da_im[...] = _dot(s, c_ai, 1, 0, precision=hi)
        dld[...] = jnp.sum(_dot(s, c_ld, 1, 0, precision=hi), axis=-1, keepdims=True)
        dbr[...] = c_br
        dbi[...] = c_bi

    g = jax.ShapeDtypeStruct((S5_GROUPS, S5_STATE), F32)
    full = jax.ShapeDtypeStruct((_S5_ROWS, S5_STATE), F32)
    return pl.pallas_call(body, name="s5_prep_bwd",
                          out_shape=[g, g, jax.ShapeDtypeStruct((S5_GROUPS, 1), F32), full, full],
                          )(a_re, a_im, log_dt, b_re, b_im, d_ab_re, d_ab_im, d_bb_re, d_bb_im, sel)


_SCAN_W = 128
_SCAN_STEPS = int(math.log2(SCAN_CHUNK))


def _cmul(ar, ai, br, bi):
    return ar * br - ai * bi, ar * bi + ai * br


def _scan_powers(ar, ai, reverse):
    pw = [(ar, ai)]
    for _ in range(_SCAN_STEPS):
        pw.append(_cmul(*pw[-1], *pw[-1]))
    rows = lax.broadcasted_iota(jnp.int32, (SCAN_CHUNK, ar.shape[-1]), 0)
    tr = jnp.broadcast_to(ar, rows.shape)
    ti = jnp.broadcast_to(ai, rows.shape)
    for k in range(_SCAN_STEPS):
        d = 2 ** k
        if reverse:
            live = rows < SCAN_CHUNK - d
            mr, mi = _cmul(tr, ti, _shift_up(tr, d, rows, SCAN_CHUNK), _shift_up(ti, d, rows, SCAN_CHUNK))
        else:
            live = rows >= d
            mr, mi = _cmul(tr, ti, _shift_down(tr, d, rows), _shift_down(ti, d, rows))
        tr = jnp.where(live, mr, tr)
        ti = jnp.where(live, mi, ti)
    return pw, rows, tr, ti


def s5_scan(bu, ab_re, ab_im, lp):
    n_chunks = lp // SCAN_CHUNK

    def body(bu_ref, ar_ref, ai_ref, s_ref):
        ar, ai = ar_ref[...], ai_ref[...]
        pw, rows, tr, ti = _scan_powers(ar, ai, False)

        def chunk(ci, carry):
            cr, cim = carry
            r0 = pl.multiple_of(ci * SCAN_CHUNK, SCAN_CHUNK)
            xr = bu_ref[0, pl.ds(r0, SCAN_CHUNK), :]
            xi = bu_ref[1, pl.ds(r0, SCAN_CHUNK), :]
            for k in range(_SCAN_STEPS):
                d = 2 ** k
                mr, mi = _cmul(pw[k][0], pw[k][1], _shift_down(xr, d, rows), _shift_down(xi, d, rows))
                xr, xi = xr + mr, xi + mi
            mr, mi = _cmul(tr, ti, cr, cim)
            xr, xi = xr + mr, xi + mi
            s_ref[0, pl.ds(r0, SCAN_CHUNK), :] = xr
            s_ref[1, pl.ds(r0, SCAN_CHUNK), :] = xi
            return xr[SCAN_CHUNK - 1:SCAN_CHUNK, :], xi[SCAN_CHUNK - 1:SCAN_CHUNK, :]

        zero = jnp.zeros((1, _SCAN_W), F32)
        lax.fori_loop(0, n_chunks, chunk, (zero, zero))

    spec = pl.BlockSpec((2, lp, _SCAN_W), lambda c: (0, 0, c))
    aspec = pl.BlockSpec((1, _SCAN_W), lambda c: (0, c))
    return pl.pallas_call(
        body, name="s5_scan", grid=(S5_LANES // _SCAN_W,), in_specs=[spec, aspec, aspec], out_specs=spec,
        out_shape=jax.ShapeDtypeStruct((2, lp, S5_LANES), F32), compiler_params=_cparams(),
    )(bu, ab_re, ab_im)


def s5_scan_bwd(ds, s, ab_re, ab_im, lp):
    n_chunks = lp // SCAN_CHUNK

    def body(ds_ref, s_ref, ar_ref, ai_ref, g_ref, da_ref):
        ar, ai = ar_ref[...], -ai_ref[...]
        pw, rows, tr, ti = _scan_powers(ar, ai, True)

        def chunk(k, carry):
            cr, cim, dar, dai = carry
            ci = n_chunks - 1 - k
            r0 = pl.multiple_of(ci * SCAN_CHUNK, SCAN_CHUNK)
            xr = ds_ref[0, pl.ds(r0, SCAN_CHUNK), :]
            xi = ds_ref[1, pl.ds(r0, SCAN_CHUNK), :]
            for j in range(_SCAN_STEPS):
                d = 2 ** j
                mr, mi = _cmul(pw[j][0], pw[j][1], _shift_up(xr, d, rows, SCAN_CHUNK), _shift_up(xi, d, rows, SCAN_CHUNK))
                xr, xi = xr + mr, xi + mi
            mr, mi = _cmul(tr, ti, cr, cim)
            xr, xi = xr + mr, xi + mi
            g_ref[0, pl.ds(r0, SCAN_CHUNK), :] = xr
            g_ref[1, pl.ds(r0, SCAN_CHUNK), :] = xi
            prev0 = pl.multiple_of(jnp.maximum(r0 - 8, 0), 8)
            live = (ci > 0).astype(F32)
            pr = s_ref[0, pl.ds(prev0, 8), :][7:8, :] * live
            pim = s_ref[1, pl.ds(prev0, 8), :][7:8, :] * live
            sr = s_ref[0, pl.ds(r0, SCAN_CHUNK), :]
            si = s_ref[1, pl.ds(r0, SCAN_CHUNK), :]
            sr = jnp.where(rows >= 1, pltpu.roll(sr, 1, 0), pr)
            si = jnp.where(rows >= 1, pltpu.roll(si, 1, 0), pim)
            dar = dar + jnp.sum(xr * sr + xi * si, axis=0, keepdims=True)
            dai = dai + jnp.sum(xi * sr - xr * si, axis=0, keepdims=True)
            return xr[0:1, :], xi[0:1, :], dar, dai

        zero = jnp.zeros((1, _SCAN_W), F32)
        _, _, dar, dai = lax.fori_loop(0, n_chunks, chunk, (zero, zero, zero, zero))
        da_ref[0] = dar
        da_ref[1] = dai

    spec = pl.BlockSpec((2, lp, _SCAN_W), lambda c: (0, 0, c))
    aspec = pl.BlockSpec((1, _SCAN_W), lambda c: (0, c))
    return pl.pallas_call(
        body, name="s5_scan_bwd", grid=(S5_LANES // _SCAN_W,), in_specs=[spec, spec, aspec, aspec],
        out_specs=[spec, pl.BlockSpec((2, 1, _SCAN_W), lambda c: (0, 0, c))],
        out_shape=[jax.ShapeDtypeStruct((2, lp, S5_LANES), F32), jax.ShapeDtypeStruct((2, 1, S5_LANES), F32)],
        compiler_params=_cparams(),
    )(ds, s, ab_re, ab_im)


S5_BLOCKS = 4
_S5_PER = S5_GROUPS // S5_BLOCKS


def _blockdiag(x):
    _, r, c = x.shape
    eye = jnp.eye(_S5_PER, dtype=x.dtype)
    x = x.reshape(S5_BLOCKS, _S5_PER, r, c)
    return (x[:, :, :, None, :] * eye[None, :, None, :, None]).reshape(S5_BLOCKS, _S5_PER * r, _S5_PER * c)


def _blockdiag_extract(m, r, c):
    return jnp.einsum('qgrgc->qgrc', m.reshape(S5_BLOCKS, _S5_PER, r, _S5_PER, c)).reshape(S5_GROUPS, r, c)


def bd_matmul(a, w, *, w_t, reduce, res=None, name):
    _, M, _ = a.shape
    n_w, _, k1, k2 = w.shape
    ka, kout = (k2, k1) if w_t else (k1, k2)
    tm = _row_tile(M)
    n_out, n_red = (1, n_w) if reduce else (n_w, 1)
    has_res = res is not None

    assert n_red <= 2

    def body(*refs):
        a_ref, w_ref = refs[0], refs[1]
        o_ref = refs[3] if has_res else refs[2]
        for q in range(S5_BLOCKS):
            cols = slice(q * kout, (q + 1) * kout)
            part = _dot(a_ref[:, q * ka:(q + 1) * ka].astype(BF16), w_ref[q], 1, 1 if w_t else 0)
            if n_red == 1:
                o_ref[:, cols] = part
            else:
                acc_ref = refs[-1]

                @pl.when(pl.program_id(2) == 0)
                def _():
                    acc_ref[:, cols] = part

                @pl.when(pl.program_id(2) == 1)
                def _():
                    tot = acc_ref[:, cols] + part
                    o_ref[:, cols] = tot + refs[2][:, cols] if has_res else tot

    if reduce:
        a_map, w_map = (lambda o, i, r: (r, i, 0)), (lambda o, i, r: (r, 0, 0, 0))
    else:
        a_map, w_map = (lambda o, i, r: (0, i, 0)), (lambda o, i, r: (o, 0, 0, 0))
    o_map = lambda o, i, r: (o, i, 0)
    in_specs = [pl.BlockSpec((None, tm, S5_BLOCKS * ka), a_map), pl.BlockSpec((None, S5_BLOCKS, k1, k2), w_map)]
    operands = [a, w]
    if has_res:
        in_specs.append(pl.BlockSpec((None, tm, S5_BLOCKS * kout), o_map))
        operands.append(res)
    return pl.pallas_call(
        body, name=name, grid=(n_out, M // tm, n_red), in_specs=in_specs,
        out_specs=pl.BlockSpec((None, tm, S5_BLOCKS * kout), o_map),
        out_shape=jax.ShapeDtypeStruct((n_out, M, S5_BLOCKS * kout), F32),
        scratch_shapes=[pltpu.VMEM((tm, S5_BLOCKS * kout), F32)] if n_red > 1 else [],
        compiler_params=_cparams(),
    )(*operands)


def bd_outer(a, b, name):
    na, M, wa = a.shape
    nb_, _, wb = b.shape
    ka, kb = wa // S5_BLOCKS, wb // S5_BLOCKS
    n_out = max(na, nb_)

    def body(a_ref, b_ref, o_ref):
        o_ref[...] = _dot(a_ref[...].astype(BF16), b_ref[...].astype(BF16), 0, 0)

    return pl.pallas_call(
        body, name=name, grid=(n_out, S5_BLOCKS),
        in_specs=[pl.BlockSpec((None, M, ka), (lambda o, q: (o, 0, q)) if na > 1 else (lambda o, q: (0, 0, q))),
                  pl.BlockSpec((None, M, kb), (lambda o, q: (o, 0, q)) if nb_ > 1 else (lambda o, q: (0, 0, q)))],
        out_specs=pl.BlockSpec((None, None, ka, kb), lambda o, q: (o, q, 0, 0)),
        out_shape=jax.ShapeDtypeStruct((n_out, S5_BLOCKS, ka, kb), F32), compiler_params=_cparams(),
    )(a, b)


def s5_u(proj, lp):
    def fn(row0, rv, pv):
        u, = rv
        return ((jnp.where(_row_mask(row0, u.shape), u, 0.0),), ())

    return rowwise(fn, [(proj, MIX, 5)], [], [(MIX, BF16)], name="s5_u", lp=lp)[0]


def s5_y(ys, proj, d, lp):
    def fn(row0, rv, pv):
        ys_, u = rv
        y = ys_ + pv[0] * u
        return ((y, _gelu(y)), ())

    return rowwise(fn, [(ys, MIX, 0), (proj, MIX, 5)], [d], [(MIX, F32), (MIX, BF16)], name="s5_y", lp=lp)


def s5_glu(z, y, b, lp):
    def fn(row0, rv, pv):
        z_, y_ = rv
        return ((_gelu(y_) * _sigmoid(z_ + pv[0]),), ())

    return rowwise(fn, [(z, MIX, 0), (y, MIX, 0)], [b], [(MIX, BF16)], name="s5_glu", lp=lp)[0]


def s5_glu_bwd(dgl, z, y, b, lp):
    def fn(row0, rv, pv):
        dgl_, z_, y_ = rv
        sg = _sigmoid(z_ + pv[0])
        dz = dgl_ * _gelu(y_) * sg * (1.0 - sg)
        return ((dgl_ * sg, dz), (jnp.sum(dz, axis=0, keepdims=True),))

    return rowwise(fn, [(dgl, MIX, 0), (z, MIX, 0), (y, MIX, 0)], [b], [(MIX, F32), (MIX, BF16)], [((1, MIX), F32)],
                   name="s5_glu_bwd", lp=lp)


def s5_y_bwd(dyg, y, proj, d, lp):
    def fn(row0, rv, pv):
        dyg_, y_, u = rv
        dy = dyg_ * _gelu_grad(y_)
        return ((dy, dy * pv[0]), (jnp.sum(dy * u, axis=0, keepdims=True),))

    return rowwise(fn, [(dyg, MIX, 0), (y, MIX, 0), (proj, MIX, 5)], [d], [(MIX, BF16), (MIX, F32)], [((1, MIX), F32)],
                   name="s5_y_bwd", lp=lp)


def s5_du(du, lp):
    def fn(row0, rv, pv):
        return ((jnp.where(_row_mask(row0, rv[0].shape), rv[0], 0.0),), ())

    return rowwise(fn, [(du, MIX, 0)], [], [(MIX, BF16)], name="s5_du", lp=lp)[0]


def merge_fwd(proj, ya, yb, yc, lp):
    def fn(row0, rv, pv):
        g0, g1, g2, a, b, c = rv
        return ((_sigmoid(g0) * a + _sigmoid(g1) * b + _sigmoid(g2) * c,), ())

    return rowwise(fn, [(proj, D_MODEL, 3), (proj, D_MODEL, 4), (proj, D_MODEL, 5), (ya, D_MODEL, 0), (yb, D_MODEL, 0),
                        (yc, D_MODEL, 0)], [], [(D_MODEL, BF16)], name="merge_fwd", lp=lp)[0]


def merge_bwd(dmix, proj, ya, yb, yc, lp):
    def fn(row0, rv, pv):
        dm, g0, g1, g2, a, b, c = rv
        outs_y, outs_g = [], []
        for g, yv in ((g0, a), (g1, b), (g2, c)):
            sg = _sigmoid(g)
            outs_y.append(dm * sg)
            outs_g.append(dm * yv * sg * (1.0 - sg))
        return (tuple(outs_y) + tuple(outs_g), ())

    return rowwise(fn, [(dmix, D_MODEL, 0), (proj, D_MODEL, 3), (proj, D_MODEL, 4), (proj, D_MODEL, 5),
                        (ya, D_MODEL, 0), (yb, D_MODEL, 0), (yc, D_MODEL, 0)], [], [(D_MODEL, BF16)] * 6,
                   name="merge_bwd", lp=lp)


def loss_head(h, tgt, lp):
    def fn(row0, rv, pv):
        h_, t_ = rv
        live = (row0 + lax.broadcasted_iota(jnp.int32, h_.shape, 0)) >= X0
        diff = jnp.where(live, h_ - t_, 0.0)
        ssq = jnp.sum(jnp.sum(diff * diff, axis=1, keepdims=True), axis=0, keepdims=True)
        return ((diff * (1.0 / D_MODEL),), (ssq * (0.5 / D_MODEL),))

    return rowwise(fn, [(h, D_MODEL, 0), (tgt, D_MODEL, 0)], [], [(D_MODEL, F32)], [((1, 1), F32)], name="loss_head", lp=lp)


def _s5_consts(W):
    ab_re_rep, ab_im_rep, bb_re, bb_im = s5_prep(W['s5_a_re'], W['s5_a_im'], W['s5_log_dt'], W['s5_b_re'], W['s5_b_im'])
    pick = lambda t: t.reshape(S5_GROUPS, S5_GROUP, S5_STATE)[:, 0].reshape(1, S5_LANES)
    bb = jnp.stack([_blockdiag(bb_re.reshape(S5_GROUPS, S5_GROUP, S5_STATE)),
                    _blockdiag(bb_im.reshape(S5_GROUPS, S5_GROUP, S5_STATE))]).astype(BF16)
    return pick(ab_re_rep), pick(ab_im_rep), bb


def layer_fwd(h, hb, W, tabs, lp, ffn1=None, ffn2=True):
    cos, sin, rot = tabs
    h1, h1b, sv1 = ffn1 if ffn1 is not None else ffn_fwd(h, hb, W['wg1'], W['wu1'], W['wd1'], W['ln1_g'], W['ln1_b'], lp)
    proj = matmul(h1b[None], W['w_in'][None], tb=True, name="proj")[0]
    cqn, ckvn, kr = mla_prep(proj, cos, sin, rot, W['q_norm_g'], W['kv_norm_g'], lp)
    qn, qr, kn, v = mla_heads(cqn, ckvn, cos, sin, rot, W['wqn'], W['wqr'], W['wkn'], W['wv'], lp)
    o = attn_fwd(qn, qr, kn, v, kr, lp)
    ya = matmul(o, W['mla_wo'], ab='r', bb='r', name="mla_out")[0]
    vconv = conv_fwd(proj, W['conv_w'], W['conv_b'], lp)
    yb = matmul(vconv[None], W['conv_wout'][None], name="conv_out")[0]
    ub = s5_u(proj, lp)
    ab_re, ab_im, bb = _s5_consts(W)
    bu = bd_matmul(ub[None], bb, w_t=False, reduce=False, name="s5_bu")
    s = s5_scan(bu, ab_re, ab_im, lp)
    ys = bd_matmul(s, W['s5_ct'], w_t=False, reduce=True, name="s5_cs")[0]
    y, ygb = s5_y(ys, proj, W['s5_d'], lp)
    zg = matmul(ygb[None], W['s5_wglu'][None], name="s5_glu_mm")[0]
    glb = s5_glu(zg, y, W['s5_b_glu'], lp)
    yc = matmul(glb[None], W['s5_wout'][None], name="s5_out")[0]
    mixed = merge_fwd(proj, ya, yb, yc, lp)
    z2, h2, h2b = mm_res_ln(mixed[None], W['w_o'][None], h1, W['ln2_g'], W['ln2_b'], scale=1.0, name="wo_ln", lp=lp)
    sv = dict(sv1=sv1, h1b=h1b, proj=proj, cqn=cqn, ckvn=ckvn, kr=kr, qn=qn, qr=qr, kn=kn, v=v, o=o, ya=ya,
              vconv=vconv, yb=yb, ub=ub, ab_re=ab_re, ab_im=ab_im, bb=bb, s=s, y=y, ygb=ygb, zg=zg, glb=glb, yc=yc,
              mixed=mixed, z2=z2)
    if not ffn2:
        return h2, h2b, sv
    h3, h3b, sv['sv3'] = ffn_fwd(h2, h2b, W['wg2'], W['wu2'], W['wd2'], W['ln3_g'], W['ln3_b'], lp)
    return h3, h3b, sv


def layer_bwd(dh3, sv, W, tabs, lp, ffn1=True, ffn2=True):
    cos, sin, rot = tabs
    proj = sv['proj']
    G = {}
    dh2 = dh3
    if ffn2:
        dh2, g3 = ffn_bwd(dh3, sv['sv3'], W['wg2'], W['wu2'], W['wd2'], W['ln3_g'], lp)
        G.update(wg2=g3['wg'], wu2=g3['wu'], wd2=g3['wd'], ln3_g=g3['ln_g'], ln3_b=g3['ln_b'])
    dz2, dz2b, G['ln2_g'], G['ln2_b'] = ln_bwd(dh2, sv['z2'], W['ln2_g'], fscale=1.0, name="wo_ln_bwd", lp=lp)
    dmix = matmul(dz2b[None], W['w_o'][None], tb=True, name="wo_dx")[0]
    G['w_o'] = matmul(sv['mixed'][None], dz2b[None], ta=True, out_dtype=WGRAD, name="wo_dw")[0]
    dya, dyb, dyc, dg0, dg1, dg2 = merge_bwd(dmix, proj, sv['ya'], sv['yb'], sv['yc'], lp)
    dgl = matmul(dyc[None], W['s5_wout'][None], tb=True, name="s5_out_dx")[0]
    G['s5_wout'] = matmul(sv['glb'][None], dyc[None], ta=True, out_dtype=WGRAD, name="s5_out_dw")[0]
    t1, dzb, G['s5_b_glu'] = s5_glu_bwd(dgl, sv['zg'], sv['y'], W['s5_b_glu'], lp)
    dyg = matmul(dzb[None], W['s5_wglu'][None], tb=True, res=t1[None], name="s5_glu_dx")[0]
    G['s5_wglu'] = matmul(sv['ygb'][None], dzb[None], ta=True, out_dtype=WGRAD, name="s5_glu_dw")[0]
    dyb_, du_d, G['s5_d'] = s5_y_bwd(dyg, sv['y'], proj, W['s5_d'], lp)
    ds = bd_matmul(dyb_[None], W['s5_ct'], w_t=True, reduce=False, name="s5_cs_dx")
    G['s5_ct'] = bd_outer(sv['s'], dyb_[None], "s5_cs_dw")
    g_adj, d_ab = s5_scan_bwd(ds, sv['s'], sv['ab_re'], sv['ab_im'], lp)
    du = bd_matmul(g_adj, sv['bb'], w_t=True, reduce=True, res=du_d[None], name="s5_bu_dx")[0]
    d_bb = bd_outer(sv['ub'][None], g_adj, "s5_bu_dw")
    du_b = s5_du(du, lp)
    onehot = (jnp.arange(S5_GROUP) == 0).astype(F32)
    spread = lambda t: (t.reshape(S5_GROUPS, 1, S5_STATE) * onehot[None, :, None]).reshape(_S5_ROWS, S5_STATE)
    take = lambda t: _blockdiag_extract(t, S5_GROUP, S5_STATE).reshape(_S5_ROWS, S5_STATE)
    sel = jnp.kron(jnp.eye(S5_GROUPS, dtype=F32), jnp.ones((1, S5_GROUP), F32))
    (G['s5_a_re'], G['s5_a_im'], G['s5_log_dt'], G['s5_b_re'], G['s5_b_im']) = s5_prep_bwd(
        W['s5_a_re'], W['s5_a_im'], W['s5_log_dt'], W['s5_b_re'], W['s5_b_im'],
        spread(d_ab[0]), spread(d_ab[1]), take(d_bb[0]), take(d_bb[1]), sel)
    dv = matmul(dyb[None], W['conv_wout'][None], tb=True, name="conv_out_dx")[0]
    G['conv_wout'] = matmul(sv['vconv'][None], dyb[None], ta=True, out_dtype=WGRAD, name="conv_out_dw")[0]
    dxbar, dbg, dcg, G['conv_w'], G['conv_b'] = conv_bwd(dv, proj, W['conv_w'], W['conv_b'], lp)
    do = matmul(dya[None], W['mla_wo'], tb=True, bb='o', out_dtype=BF16, name="mla_out_dx")
    G['mla_wo'] = matmul(sv['o'], dya[None], ta=True, ab='o', out_dtype=WGRAD, name="mla_out_dw")
    dqn, dqr, dkn, dvv, dkr = attn_bwd(sv['qn'], sv['qr'], sv['kn'], sv['v'], sv['kr'], do, lp)
    dcq, dckv, dqrp = mla_heads_bwd(dqn, dqr, dkn, dvv, cos, sin, rot, W['wqn'], W['wqr'], W['wkn'], W['wv'], lp)
    G['wqn'] = matmul(sv['cqn'][None], dqn, ta=True, bb='o', out_dtype=WGRAD, name="mla_dwqn")
    G['wqr'] = matmul(sv['cqn'][None], dqrp, ta=True, bb='o', out_dtype=WGRAD, name="mla_dwqr")
    G['wkn'] = matmul(sv['ckvn'][None], dkn, ta=True, bb='o', out_dtype=WGRAD, name="mla_dwkn")
    G['wv'] = matmul(sv['ckvn'][None], dvv, ta=True, bb='o', out_dtype=WGRAD, name="mla_dwv")
    dcq_raw, dckv_raw, dkr_raw, G['q_norm_g'], G['kv_norm_g'] = mla_prep_bwd(
        dcq, dckv, dkr, proj, cos, sin, rot, W['q_norm_g'], W['kv_norm_g'], lp)
    zeros = lambda n: jnp.zeros((lp, n), BF16)
    dproj = jnp.concatenate([dcq_raw, dkr_raw, zeros(96), dckv_raw, zeros(256), dxbar, dbg, dcg, du_b, dg0, dg1, dg2], axis=1)
    dh1 = matmul(dproj[None], W['w_in'][None], res=dz2[None], res_scale=ALPHA, name="proj_dx")[0]
    G['w_in'] = matmul(dproj[None], sv['h1b'][None], ta=True, out_dtype=WGRAD, name="proj_dw")[0]
    if not ffn1:
        return dh1, G
    dh0, g1 = ffn_bwd(dh1, sv['sv1'], W['wg1'], W['wu1'], W['wd1'], W['ln1_g'], lp)
    G.update(wg1=g1['wg'], wu1=g1['wu'], wd1=g1['wd'], ln1_g=g1['ln_g'], ln1_b=g1['ln_b'])
    return dh0, G


def _nat_cols(st):
    return jnp.transpose(st, (1, 0, 2)).reshape(st.shape[1], -1)


def _shard_cols(nat):
    k, n = nat.shape
    return jnp.transpose(nat.reshape(k, N_SHARD, n // N_SHARD), (1, 0, 2))


def _win_pad(wt):
    z = lambda n: jnp.zeros((n, wt.shape[1]), wt.dtype)
    return jnp.concatenate([wt[0:384], wt[640:672], z(96), wt[384:640], z(256), wt[672:]], axis=0)


def _win_unpad(wp):
    return jnp.concatenate([wp[0:384], wp[512:768], wp[384:416], wp[1024:]], axis=0)


_BIG = [('ffn1_w_gate', 'T'), ('ffn1_w_up', 'T'), ('ffn1_w_down', 0), ('w_in', 'T'), ('mla_w_uq', 1), ('mla_w_ukv', 1),
        ('mla_w_o', 1), ('conv_w_out', 1), ('s5_w_glu', 0), ('s5_w_out', 1), ('w_o', 0),
        ('ffn2_w_gate', 'T'), ('ffn2_w_up', 'T'), ('ffn2_w_down', 0)]
_REPL = ['ln1_g', 'ln1_b', 'mla_q_norm_g', 'mla_kv_norm_g', 'conv_b', 's5_a_re', 's5_a_im', 's5_log_dt', 's5_b_re',
         's5_b_im', 's5_c_re', 's5_c_im', 's5_d', 's5_b_glu', 'ln2_g', 'ln2_b', 'ln3_g', 'ln3_b']


def compute_weights(st, small):
    W = {}
    for t in ('1', '2'):
        if 'ffn%s_w_gate' % t in st:
            W['wg' + t], W['wu' + t], W['wd' + t] = (st['ffn%s_w_%s' % (t, p)] for p in ('gate', 'up', 'down'))
    if 'w_in' in st:
        W.update(_mixer_weights(st))
    if small is not None:
        W.update(_small_weights(small))
    return W


def _mixer_weights(st):
    W = {}
    W['w_in'] = _win_pad(st['w_in'].reshape(D_IN, D_MODEL))
    uq = jnp.transpose(_nat_cols(st['mla_w_uq']).reshape(Q_RANK, N_HEADS, D_NOPE + D_ROPE), (1, 0, 2))
    W['wqn'], W['wqr'] = uq[:, :, :D_NOPE], uq[:, :, D_NOPE:]
    ukv = jnp.transpose(_nat_cols(st['mla_w_ukv']).reshape(KV_RANK, N_HEADS, D_NOPE + D_V), (1, 0, 2))
    W['wkn'], W['wv'] = ukv[:, :, :D_NOPE], ukv[:, :, D_NOPE:]
    W['mla_wo'] = _nat_cols(st['mla_w_o']).reshape(N_HEADS, D_V, D_MODEL)
    W['conv_wout'] = _nat_cols(st['conv_w_out'])
    W['s5_wglu'] = st['s5_w_glu'].reshape(MIX, MIX)
    W['s5_wout'] = _nat_cols(st['s5_w_out'])
    W['w_o'] = st['w_o'].reshape(D_MODEL, D_MODEL)
    return W


def _small_weights(small):
    W = {}
    W['conv_w'] = small['conv_w']
    for n in ('ln1_g', 'ln1_b', 'ln2_g', 'ln2_b', 'ln3_g', 'ln3_b', 'conv_b', 's5_b_glu'):
        W[n] = small[n].reshape(1, -1)
    W['q_norm_g'] = small['mla_q_norm_g'].reshape(1, -1)
    W['kv_norm_g'] = small['mla_kv_norm_g'].reshape(1, -1)
    W['s5_d'] = small['s5_d'].reshape(1, MIX)
    rep = lambda t: jnp.repeat(t, S5_GROUP, axis=0)
    W['s5_a_re'], W['s5_a_im'] = rep(small['s5_a_re']), rep(small['s5_a_im'])
    W['s5_log_dt'] = jnp.broadcast_to(rep(small['s5_log_dt'].reshape(S5_GROUPS, 1)), (_S5_ROWS, S5_STATE))
    tr = lambda t: jnp.transpose(t, (0, 2, 1)).reshape(_S5_ROWS, S5_STATE)
    W['s5_b_re'], W['s5_b_im'] = tr(small['s5_b_re']), tr(small['s5_b_im'])
    ct = lambda t: _blockdiag(jnp.transpose(t, (0, 2, 1)))
    W['s5_ct'] = jnp.stack([ct(small['s5_c_re']), -ct(small['s5_c_im'])]).astype(BF16)
    return W


def reference_grads(G, ffn=True):
    R = {}
    for t in ('1', '2') if ffn else ():
        R['ffn%s_w_gate' % t] = G['wg' + t].reshape(D_FF, D_MODEL).T
        R['ffn%s_w_up' % t] = G['wu' + t].reshape(D_FF, D_MODEL).T
        R['ffn%s_w_down' % t] = G['wd' + t].reshape(D_FF, D_MODEL)
    R['w_in_t'] = _win_unpad(G['w_in'])
    if ffn:
        R['w_in'] = R['w_in_t'].T
    R['mla_w_uq'] = jnp.transpose(jnp.concatenate([G['wqn'], G['wqr']], axis=2), (1, 0, 2)).reshape(Q_RANK, -1)
    R['mla_w_ukv'] = jnp.transpose(jnp.concatenate([G['wkn'], G['wv']], axis=2), (1, 0, 2)).reshape(KV_RANK, -1)
    R['mla_w_o'] = G['mla_wo'].reshape(N_HEADS * D_V, D_MODEL)
    R['conv_w'], R['conv_w_out'] = G['conv_w'], G['conv_wout']
    R['s5_w_glu'], R['s5_w_out'], R['w_o'] = G['s5_wglu'], G['s5_wout'], G['w_o']
    for n in ('ln1_g', 'ln1_b', 'ln2_g', 'ln2_b', 'ln3_g', 'ln3_b', 'conv_b', 's5_b_glu'):
        if n in G:
            R[n] = G[n].reshape(-1)
    R['mla_q_norm_g'], R['mla_kv_norm_g'] = G['q_norm_g'].reshape(-1), G['kv_norm_g'].reshape(-1)
    R['s5_d'] = G['s5_d'].reshape(S5_GROUPS, S5_GROUP)
    R['s5_a_re'], R['s5_a_im'], R['s5_log_dt'] = G['s5_a_re'], G['s5_a_im'], G['s5_log_dt'].reshape(-1)
    untr = lambda t: jnp.transpose(t.reshape(S5_GROUPS, S5_GROUP, S5_STATE), (0, 2, 1))
    R['s5_b_re'], R['s5_b_im'] = untr(G['s5_b_re']), untr(G['s5_b_im'])
    unct = lambda t: jnp.transpose(_blockdiag_extract(t, S5_STATE, S5_GROUP), (0, 2, 1))
    R['s5_c_re'], R['s5_c_im'] = unct(G['s5_ct'][0]), -unct(G['s5_ct'][1])
    return R


_ANY = pl.BlockSpec(memory_space=pl.ANY)
LANES = 1024


def _place():
    x, y, c = lax.axis_index("x"), lax.axis_index("y"), lax.axis_index("c")
    chips = [(1 - x, y), (x, 1 - y), (1 - x, 1 - y)]
    return x, y, c, chips


def _rows_of(c, half):
    return pl.ds(pl.multiple_of(c * half, 8), half)


def all_gather_shards(srcs, exact):
    n, m = len(srcs), len(exact)
    halves = [s.shape[0] // 2 for s in srcs]

    def body(*refs):
        s_refs, e_refs = refs[:n], refs[n:n + m]
        o_refs, eo_refs = refs[n + m:2 * n + m], refs[2 * n + m:2 * n + 2 * m]
        send, recv, esend, erecv, osend, orecv, lsem = refs[2 * n + 2 * m:]
        x, y, c, chips = _place()
        me = 2 * x + y
        sibling = (x, y, 1 - c)
        own = [pltpu.make_async_remote_copy(src_ref=s_refs[k], dst_ref=o_refs[k].at[me], send_sem=osend.at[k],
                                            recv_sem=orecv.at[k], device_id=sibling, device_id_type=MESH) for k in range(n)]
        local = [pltpu.make_async_copy(e_refs[k], eo_refs[k].at[me], lsem.at[k]) for k in range(m)]
        for cp in own + local:
            cp.start()

        def copy(k, s, src, idx, half_c, to):
            return pltpu.make_async_remote_copy(
                src_ref=src, dst_ref=o_refs[k].at[idx, _rows_of(half_c, halves[k])], send_sem=send.at[6 * k + s],
                recv_sem=recv.at[6 * k + s], device_id=to, device_id_type=MESH)

        def ecopy(k, j, idx, to):
            return pltpu.make_async_remote_copy(src_ref=e_refs[k], dst_ref=eo_refs[k].at[idx], send_sem=esend.at[3 * k + j],
                                                recv_sem=erecv.at[3 * k + j], device_id=to, device_id_type=MESH)

        sends = []
        for k in range(n):
            mine = s_refs[k].at[_rows_of(c, halves[k])]
            sends += [copy(k, j, mine, me, c, (*chip, c)) for j, chip in enumerate(chips)]
        for k in range(m):
            sends += [ecopy(k, j, me, (*chip, c)) for j, chip in enumerate(chips)]
        for cp in sends:
            cp.start()
        for j, chip in enumerate(chips):
            idx = 2 * chip[0] + chip[1]
            for k in range(n):
                landed = o_refs[k].at[idx, _rows_of(c, halves[k])]
                copy(k, j, landed, idx, c, sibling).wait_recv()
                fwd = copy(k, 3 + j, landed, idx, c, sibling)
                fwd.start()
                sends.append(fwd)
        for j, chip in enumerate(chips):
            idx = 2 * chip[0] + chip[1]
            for k in range(n):
                copy(k, 3 + j, s_refs[k].at[_rows_of(c, halves[k])], idx, 1 - c, sibling).wait_recv()
            for k in range(m):
                ecopy(k, j, idx, sibling).wait_recv()
        for cp in sends:
            cp.wait_send()
        for cp in own + local:
            cp.wait()

    outs = pl.pallas_call(
        body, name="all_gather_weights", in_specs=[_ANY] * (n + m), out_specs=[_ANY] * (n + m),
        out_shape=[jax.ShapeDtypeStruct((N_SHARD,) + a.shape, a.dtype) for a in list(srcs) + list(exact)],
        scratch_shapes=[pltpu.SemaphoreType.DMA((6 * n,)), pltpu.SemaphoreType.DMA((6 * n,)),
                        pltpu.SemaphoreType.DMA((3 * m,)), pltpu.SemaphoreType.DMA((3 * m,)),
                        pltpu.SemaphoreType.DMA((n,)), pltpu.SemaphoreType.DMA((n,)), pltpu.SemaphoreType.DMA((m,))],
    )(*srcs, *exact)
    return outs[:n], outs[n:]


def rs_pair_swap(gs):
    n = len(gs)

    def body(*refs):
        g_refs, r_refs, send, recv = refs[:n], refs[n:2 * n], refs[2 * n], refs[2 * n + 1]
        x, y, c, _ = _place()
        copies = [pltpu.make_async_remote_copy(
            src_ref=g_refs[k].at[pl.ds(0, N_SHARD), _rows_of(1 - c, gs[k].shape[1] // 2)], dst_ref=r_refs[k],
            send_sem=send.at[k], recv_sem=recv.at[k], device_id=(x, y, 1 - c), device_id_type=MESH) for k in range(n)]
        for cp in copies:
            cp.start()
        for cp in copies:
            cp.wait()

    return pl.pallas_call(
        body, name="grad_pair_swap", in_specs=[_ANY] * n, out_specs=[_ANY] * n,
        out_shape=[jax.ShapeDtypeStruct((N_SHARD, g.shape[1] // 2, g.shape[2]), g.dtype) for g in gs],
        scratch_shapes=[pltpu.SemaphoreType.DMA((n,)), pltpu.SemaphoreType.DMA((n,))],
    )(*gs)


def _group_tile(half, n_cols, n_arrays):
    budget = (20 * 2 ** 20) // (6 * n_arrays)
    fits = [t for t in range(8, half + 1, 8) if half % t == 0 and t * n_cols * 4 <= budget]
    return max(fits) if fits else 8


def rs_pair_add(gs, rs, cidx, out_dtype, name):
    n = len(gs)
    _, K, cols = gs[0].shape
    half = K // 2
    tr = _group_tile(half, cols, n)
    nb = half // tr

    def body(c_ref, *refs):
        for g_ref, r_ref, o_ref in zip(refs[:n], refs[n:2 * n], refs[2 * n:]):
            o_ref[...] = (g_ref[...].astype(F32) + r_ref[...].astype(F32)).astype(out_dtype)

    gspec = pl.BlockSpec((None, tr, cols), lambda j, i, c: (j, c[0] * nb + i, 0))
    rspec = pl.BlockSpec((None, tr, cols), lambda j, i, c: (j, i, 0))
    return pl.pallas_call(
        body, name=name,
        grid_spec=pltpu.PrefetchScalarGridSpec(num_scalar_prefetch=1, grid=(N_SHARD, nb), in_specs=[gspec] * n + [rspec] * n,
                                               out_specs=[rspec] * n),
        out_shape=[jax.ShapeDtypeStruct((N_SHARD, half, cols), out_dtype)] * n,
        compiler_params=_cparams(),
    )(cidx, *gs, *rs)


def rs_chip_sum(qs, nl, cidx, name):
    n = len(qs)
    _, half, cols = qs[0].shape
    tr = _group_tile(half, cols, n)
    nb = half // tr

    def body(c_ref, *refs):
        for k, q_ref in enumerate(refs[:n]):
            o_ref = refs[n + k // nl]
            o_ref[k % nl] = ((q_ref[0].astype(F32) + q_ref[1].astype(F32)) + q_ref[2].astype(F32)) + q_ref[3].astype(F32)

    return pl.pallas_call(
        body, name=name,
        grid_spec=pltpu.PrefetchScalarGridSpec(
            num_scalar_prefetch=1, grid=(nb,),
            in_specs=[pl.BlockSpec((N_SHARD, tr, cols), lambda i, c: (0, i, 0))] * n,
            out_specs=[pl.BlockSpec((nl, tr, cols), lambda i, c: (0, c[0] * nb + i, 0))] * (n // nl)),
        out_shape=[jax.ShapeDtypeStruct((nl, 2 * half, cols), F32)] * (n // nl),
        compiler_params=_cparams(),
    )(cidx, *qs)


def rs_pair_gather(fs, name):
    n = len(fs)

    def body(*refs):
        f_refs, send, recv = refs[n:2 * n], refs[2 * n], refs[2 * n + 1]
        x, y, c, _ = _place()
        copies = []
        for k in range(n):
            rows = f_refs[k].at[pl.ds(0, fs[k].shape[0]), _rows_of(c, fs[k].shape[1] // 2)]
            copies.append(pltpu.make_async_remote_copy(src_ref=rows, dst_ref=rows, send_sem=send.at[k], recv_sem=recv.at[k],
                                                       device_id=(x, y, 1 - c), device_id_type=MESH))
        for cp in copies:
            cp.start()
        for cp in copies:
            cp.wait()

    return pl.pallas_call(
        body, name=name, in_specs=[_ANY] * n, out_specs=[_ANY] * n,
        out_shape=[jax.ShapeDtypeStruct(f.shape, f.dtype) for f in fs],
        input_output_aliases={k: k for k in range(n)},
        scratch_shapes=[pltpu.SemaphoreType.DMA((n,)), pltpu.SemaphoreType.DMA((n,))],
    )(*fs)


_HBM = pl.BlockSpec(memory_space=pltpu.HBM)
_SEM = pl.BlockSpec(memory_space=pltpu.SEMAPHORE)
_EFFECT = pltpu.SideEffectType.DATAFLOW_SIDE_EFFECTING


def _in_hbm(a):
    return pltpu.with_memory_space_constraint(a, pltpu.HBM)


def split_start(name, srcs, lands, after, copies_fn, n_copies):
    n = len(srcs)

    def body(*refs):
        for cp in copies_fn(refs[:n], refs[n:2 * n], refs[2 * n + 1], refs[2 * n + 2]):
            cp.start()
        refs[-1][...] = jnp.zeros_like(refs[-1])

    bufs = list(srcs) + list(lands)
    outs = pl.pallas_call(
        body, name=name,
        out_shape=(pltpu.SemaphoreType.DMA((n_copies,)), pltpu.SemaphoreType.DMA((n_copies,)),
                   *[pltpu.HBM(a.shape, a.dtype) for a in bufs], jax.ShapeDtypeStruct((8, 128), F32)),
        in_specs=[_HBM] * (2 * n) + [_ANY],
        out_specs=(_SEM, _SEM, *[_HBM] * (2 * n), pl.BlockSpec(memory_space=pltpu.VMEM)),
        input_output_aliases={i: 2 + i for i in range(2 * n)},
        compiler_params=pltpu.CompilerParams(has_side_effects=_EFFECT),
    )(*[_in_hbm(a) for a in bufs], after)
    return outs[0], outs[1], outs[2:2 + n], outs[2 + n:2 + 2 * n], outs[-1]


def split_wait(name, send, recv, srcs, lands, after, copies_fn, which=None):
    n = len(srcs)

    def body(*refs):
        copies = copies_fn(refs[:n], refs[n:2 * n], refs[2 * n], refs[2 * n + 1], which)
        for cp in copies:
            cp.wait_send()
        for cp in copies:
            cp.wait_recv()

    bufs = list(srcs) + list(lands)
    outs = pl.pallas_call(
        body, name=name, out_shape=tuple(pltpu.HBM(a.shape, a.dtype) for a in bufs),
        in_specs=[_HBM] * (2 * n) + [_SEM, _SEM, _ANY], out_specs=tuple([_HBM] * (2 * n)),
        input_output_aliases={i: i for i in range(2 * n)},
        compiler_params=pltpu.CompilerParams(has_side_effects=_EFFECT),
    )(*bufs, send, recv, after)
    return list(outs[:n]), list(outs[n:])


def _gather_copies(s_refs, l_refs, send, recv, which=None):
    x, y, c, chips = _place()
    me = 2 * x + y
    out = []
    for k in (range(len(s_refs)) if which is None else which):
        s, l = s_refs[k], l_refs[k]
        rows = _rows_of(c, s.shape[0] // 2)
        for j, chip in enumerate(chips):
            out.append(pltpu.make_async_remote_copy(src_ref=s.at[rows], dst_ref=l.at[me, rows], send_sem=send.at[4 * k + j],
                                                    recv_sem=recv.at[4 * k + j], device_id=(*chip, c), device_id_type=MESH))
        out.append(pltpu.make_async_remote_copy(src_ref=s, dst_ref=l.at[me], send_sem=send.at[4 * k + 3],
                                                recv_sem=recv.at[4 * k + 3], device_id=(x, y, 1 - c), device_id_type=MESH))
    return out


def _scatter_copies(s_refs, l_refs, send, recv, which=None):
    x, y, c, chips = _place()
    me = 2 * x + y
    return [pltpu.make_async_remote_copy(src_ref=s_refs[k].at[2 * chip[0] + chip[1]], dst_ref=l_refs[k].at[me],
                                         send_sem=send.at[3 * k + j], recv_sem=recv.at[3 * k + j], device_id=(*chip, c),
                                         device_id_type=MESH)
            for k in (range(len(s_refs)) if which is None else which) for j, chip in enumerate(chips)]


def gather_forward(lands, name):
    n = len(lands)

    def body(*refs):
        l_refs, send, recv = refs[n:2 * n], refs[2 * n], refs[2 * n + 1]
        x, y, c, chips = _place()
        copies = []
        for k in range(n):
            rows = _rows_of(c, lands[k].shape[1] // 2)
            for j, chip in enumerate(chips):
                part = l_refs[k].at[2 * chip[0] + chip[1], rows]
                copies.append(pltpu.make_async_remote_copy(src_ref=part, dst_ref=part, send_sem=send.at[3 * k + j],
                                                           recv_sem=recv.at[3 * k + j], device_id=(x, y, 1 - c),
                                                           device_id_type=MESH))
        for cp in copies:
            cp.start()
        for cp in copies:
            cp.wait()

    return pl.pallas_call(
        body, name=name, in_specs=[_ANY] * n, out_specs=[_ANY] * n,
        out_shape=[jax.ShapeDtypeStruct(a.shape, a.dtype) for a in lands],
        input_output_aliases={k: k for k in range(n)},
        scratch_shapes=[pltpu.SemaphoreType.DMA((3 * n,)), pltpu.SemaphoreType.DMA((3 * n,))],
    )(*lands)


def rs_partials(gs, wire, cidx, tag):
    rs = rs_pair_swap(gs)
    groups = {}
    for k, g in enumerate(gs):
        groups.setdefault((g.shape, jnp.dtype(wire[k]).name), []).append(k)
    ps = [None] * len(gs)
    for gi, ks in enumerate(groups.values()):
        outs = rs_pair_add([gs[k] for k in ks], [rs[k] for k in ks], cidx, wire[ks[0]], "grad_pair_add_%s%d" % (tag, gi))
        for k, o in zip(ks, outs):
            ps[k] = o
    return ps


def rs_finish(items, tag):
    cidx = lax.axis_index("c").astype(jnp.int32).reshape(1)
    groups = {}
    for i, it in enumerate(items):
        groups.setdefault((it[0].shape, len(it), it[0].dtype.name), []).append(i)
    fs = [None] * len(items)
    for gi, ids in enumerate(groups.values()):
        outs = rs_chip_sum([q for i in ids for q in items[i]], len(items[ids[0]]), cidx, "grad_chip_sum_%s%d" % (tag, gi))
        for i, o in zip(ids, outs):
            fs[i] = o
    return rs_pair_gather(fs, "grad_pair_gather_" + tag)


def adamw(w, g, m, v, name):
    shape = w.shape
    if w.ndim == 2:
        block, grid, index = shape, (1,), (lambda i: (0, 0))
    else:
        slab = shape[2:]
        unit = 4 * int(np.prod(slab[:-2] or (1,))) * (-(-slab[-1] // 128) * 128)
        if len(slab) >= 2:
            unit *= -(-slab[-2] // 8) * 8
        k = shape[1]
        tr = k
        if k * unit > 2 ** 21:
            tr = max(t for t in range(8, k, 8) if k % t == 0 and t * unit <= 2 ** 21)
        block, grid = (None, tr) + tuple(slab), (shape[0], k // tr)
        index = lambda l, i: (l, i) + (0,) * len(slab)
        if tr < min(k, 64) and len(slab) == 1:
            tc = max(t for t in range(128, slab[0] + 1, 128) if slab[0] % t == 0 and k * t * 4 <= 2 ** 21)
            block, grid = (None, k, tc), (shape[0], slab[0] // tc)
            index = lambda l, i: (l, 0, i)

    def body(w_ref, g_ref, m_ref, v_ref, d_ref, nm_ref, nv_ref):
        g_ = g_ref[...]
        m_new = ADAM_B1 * m_ref[...] + (1.0 - ADAM_B1) * g_
        v_new = ADAM_B2 * v_ref[...] + (1.0 - ADAM_B2) * (g_ * g_)
        m_hat = m_new / (1.0 - ADAM_B1 ** ADAM_STEP)
        v_hat = v_new / (1.0 - ADAM_B2 ** ADAM_STEP)
        d_ref[...] = -ADAM_LR * (m_hat / (jnp.sqrt(v_hat) + ADAM_EPS) + ADAM_WD * w_ref[...])
        nm_ref[...] = m_new
        nv_ref[...] = v_new

    spec = pl.BlockSpec(block, index)
    return pl.pallas_call(
        body, name=name, grid=grid, in_specs=[spec] * 4, out_specs=[spec] * 3,
        out_shape=[jax.ShapeDtypeStruct(shape, F32)] * 3, compiler_params=_cparams(),
    )(w, g, m, v)


_WEIGHTS = ['meta', 'ffn1_w_gate', 'ffn1_w_up', 'ffn1_w_down', 'ln1_g', 'ln1_b', 'w_in', 'mla_q_norm_g', 'mla_w_uq',
            'mla_kv_norm_g', 'mla_w_ukv', 'mla_w_o', 'conv_w', 'conv_b', 'conv_w_out', 's5_a_re', 's5_a_im', 's5_log_dt',
            's5_b_re', 's5_b_im', 's5_c_re', 's5_c_im', 's5_d', 's5_w_glu', 's5_b_glu', 's5_w_out', 'w_o', 'ln2_g', 'ln2_b',
            'ffn2_w_gate', 'ffn2_w_up', 'ffn2_w_down', 'ln3_g', 'ln3_b']


def _pad_to(flat, n):
    return jnp.concatenate([flat, jnp.zeros((n - flat.shape[0],), flat.dtype)])


def _shard_of(full, axis):
    if axis == 1:
        return _shard_cols(full)
    if axis == 'T':
        return full.T.reshape(N_SHARD, full.shape[1] // N_SHARD, full.shape[0])
    return full.reshape(N_SHARD, full.shape[0] // N_SHARD, full.shape[1])


_FFN_KEY = {'gate': 'wg', 'up': 'wu', 'down': 'wd'}


def _pad_rows(a, axis):
    k = a.shape[axis]
    extra = -k % 32
    if not extra:
        return a
    return jnp.pad(a, [(0, extra) if d == axis else (0, 0) for d in range(a.ndim)])


def _step(env):
    w = {n: env[n] for n in _WEIGHTS}
    mom = {n: env['m_' + n] for n in _WEIGHTS}
    var = {n: env['v_' + n] for n in _WEIGHTS}
    cidx = lax.axis_index("c").astype(jnp.int32).reshape(1)
    chip = 2 * lax.axis_index("x") + lax.axis_index("y")
    big_names = [n for n, _ in _BIG]
    nb = len(big_names)

    kept_t = [n for n, a in _BIG if a == 'T']
    own = {n: (jnp.swapaxes(w[n], 1, 2) if n in kept_t else w[n]) for n in big_names}
    first = [n for n in big_names if n.startswith('ffn1')]
    mix = [n for n in big_names if not n.startswith('ffn')]
    last = [n for n in big_names if n.startswith('ffn2')]
    rest = mix + last
    nm, nr = len(mix), len(mix) + len(last)
    src = lambda n, li: _pad_rows(own[n][li].astype(BF16), 0)
    gathered_first, (conv_w_st, meta_st) = all_gather_shards([src(n, 0) for n in first], [w['conv_w'], w['meta']])
    later = [src(n, 0) for n in rest] + [src(n, 1) for n in big_names]
    lands = [lax.empty((N_SHARD,) + s.shape, BF16) for s in later]
    g_send, g_recv, later_t, lands_t, token = split_start("gather_start", later, lands, gathered_first[0], _gather_copies,
                                                          4 * len(later))

    def weights_of(names, st, li, with_small):
        small = None
        if with_small:
            small = {n: w[n][li] for n in _REPL}
            small['conv_w'] = _nat_cols(conv_w_st[:, li])
        return compute_weights({n: a[:, :own[n].shape[1]] for n, a in zip(names, st)}, small)

    x2d = env['x'][0]
    lp = x2d.shape[0] + X0
    tabs = _rope_tables(lp)
    h = jnp.concatenate([jnp.zeros((PAD, D_MODEL), F32), _nat_cols(meta_st), x2d], axis=0) + token[0, 0]
    W0 = weights_of(first, gathered_first, 0, True)
    ffn1 = ffn_fwd(h, h.astype(BF16), W0['wg1'], W0['wu1'], W0['wd1'], W0['ln1_g'], W0['ln1_b'], lp)
    later_t, lands_t = split_wait("gather0_wait", g_send, g_recv, later_t, lands_t, ffn1[0], _gather_copies, range(nm))
    W0.update(weights_of(mix, gather_forward(lands_t[:nm], "gather0_forward"), 0, False))
    h, hb, sv0 = layer_fwd(None, None, W0, tabs, lp, ffn1=ffn1, ffn2=False)
    later_t, lands_t = split_wait("gather0b_wait", g_send, g_recv, later_t, lands_t, h, _gather_copies, range(nm, nr))
    W0.update(weights_of(last, gather_forward(lands_t[nm:nr], "gather0b_forward"), 0, False))
    h, hb, sv0['sv3'] = ffn_fwd(h, hb, W0['wg2'], W0['wu2'], W0['wd2'], W0['ln3_g'], W0['ln3_b'], lp)
    _, lands_t = split_wait("gather1_wait", g_send, g_recv, later_t, lands_t, h, _gather_copies, range(nr, len(later)))
    W1 = weights_of(big_names, gather_forward(lands_t[nr:], "gather1_forward"), 1, True)
    h, hb, sv1 = layer_fwd(h, hb, W1, tabs, lp)
    tgt = jnp.concatenate([jnp.zeros((X0, D_MODEL), F32), env['loss_target'][0]], axis=0)
    dh, loss_part = loss_head(h, tgt, lp)
    loss = lax.psum(loss_part[0, 0], ("x", "y", "c"))

    def shards(G, names):
        full = None if all(n.startswith('ffn') for n in names) else reference_grads(G, ffn=False)

        def one(n, a):
            if n.startswith('ffn'):
                return G[_FFN_KEY[n.split('_')[-1]] + n[3]]
            if n == 'w_in':
                return full['w_in_t'].reshape(N_SHARD, D_IN // N_SHARD, D_MODEL)
            return _shard_of(full[n], a)

        return [_pad_rows(one(n, a), 1) for n, a in _BIG if n in names]

    def scatter_start(name, ps, after):
        qs = [lax.dynamic_update_slice_in_dim(jnp.zeros_like(p), lax.dynamic_slice_in_dim(p, chip, 1, axis=0), chip, axis=0)
              for p in ps]
        return split_start(name, ps, qs, after, _scatter_copies, 3 * len(ps))

    dh, G1 = layer_bwd(dh, sv1, W1, tabs, lp)
    p1 = rs_partials(shards(G1, big_names), [BF16] * nb, cidx, "b")
    s1_send, s1_recv, p1_t, q1_t, token1 = scatter_start("scatter1_start", p1, dh)
    dh, g3 = ffn_bwd(dh + token1[0, 0], sv0['sv3'], W0['wg2'], W0['wu2'], W0['wd2'], W0['ln3_g'], lp)
    G0 = dict(wg2=g3['wg'], wu2=g3['wu'], wd2=g3['wd'])
    p0l = rs_partials(shards(G0, last), [BF16] * len(last), cidx, "c")
    sl_send, sl_recv, p0l_t, q0l_t, token0l = scatter_start("scatter0b_start", p0l, dh)
    dh, Gm = layer_bwd(dh + token0l[0, 0], sv0, W0, tabs, lp, ffn1=False, ffn2=False)
    G0.update(Gm, ln3_g=g3['ln_g'], ln3_b=g3['ln_b'])
    p0m = rs_partials(shards(G0, mix), [BF16] * nm, cidx, "d")
    sm_send, sm_recv, p0m_t, q0m_t, token0m = scatter_start("scatter0_start", p0m, dh)
    dh, g1 = ffn_bwd(dh + token0m[0, 0], sv0['sv1'], W0['wg1'], W0['wu1'], W0['wd1'], W0['ln1_g'], lp)
    G0.update(wg1=g1['wg'], wu1=g1['wu'], wd1=g1['wd'], ln1_g=g1['ln_g'], ln1_b=g1['ln_b'])
    _, q1 = split_wait("scatter1_wait", s1_send, s1_recv, p1_t, q1_t, dh, _scatter_copies)
    _, q0_last = split_wait("scatter0b_wait", sl_send, sl_recv, p0l_t, q0l_t, dh, _scatter_copies)
    _, q0_mix = split_wait("scatter0_wait", sm_send, sm_recv, p0m_t, q0m_t, dh, _scatter_copies)
    q0_rest = list(q0_mix) + list(q0_last)
    full = [reference_grads(G0, ffn=False), reference_grads(G1, ffn=False)]

    s_parts = [jnp.stack([full[li][n] for li in range(DEPTH)]).reshape(-1) for n in _REPL + ['conv_w']]
    s_parts.append(dh[PAD:X0].reshape(-1))
    s_sizes = [int(p.shape[0]) for p in s_parts]
    s_rows = -(-sum(s_sizes) // (16 * LANES)) * 16
    g_small = _pad_to(jnp.concatenate(s_parts), s_rows * LANES).reshape(1, s_rows, LANES)
    g_small = jnp.broadcast_to(g_small, (N_SHARD, s_rows, LANES))
    p_last = rs_partials(shards(G0, first) + [g_small], [BF16] * len(first) + [F32], cidx, "a")
    z_send, z_recv, pz_t, qz_t, token_z = scatter_start("scatter_last_start", p_last, dh)

    def step_weights(names, grad):
        out = {}
        for n in names:
            if n in kept_t:
                res = adamw(own[n], grad[n], jnp.swapaxes(mom[n], 1, 2), jnp.swapaxes(var[n], 1, 2), "adamw_" + n)
                out[n] = [jnp.swapaxes(t, 1, 2) for t in [grad[n]] + list(res)]
            else:
                out[n] = [grad[n]] + list(adamw(w[n], grad[n], mom[n], var[n], "adamw_" + n))
        return out

    q1 = dict(zip(big_names, q1))
    q0 = dict(zip(rest, q0_rest))
    q0[rest[0]] = q0[rest[0]] + token_z[0, 0].astype(BF16)
    red = rs_finish([[q0[n], q1[n]] for n in rest], "a")
    done = step_weights(rest, {n: r[:, :own[n].shape[1]] for n, r in zip(rest, red)})
    _, q_last = split_wait("scatter_last_wait", z_send, z_recv, pz_t, qz_t, done[rest[-1]][3], _scatter_copies)
    red = rs_finish([[q, q1[n]] for n, q in zip(first, q_last)] + [[q_last[-1]]], "b")
    f_small = red[-1].reshape(-1)

    grad = {n: r[:, :own[n].shape[1]] for n, r in zip(first, red)}
    off = 0
    for n, sz in zip(_REPL + ['conv_w', 'meta'], s_sizes):
        grad[n] = f_small[off:off + sz]
        off += sz
    for n in _REPL:
        grad[n] = grad[n].reshape(w[n].shape)
    cw = grad['conv_w'].reshape(DEPTH, 3, MIX)
    grad['conv_w'] = lax.dynamic_slice_in_dim(cw, chip * (MIX // N_SHARD), MIX // N_SHARD, axis=2)
    gm = grad['meta'].reshape(N_META, D_MODEL)
    grad['meta'] = lax.dynamic_slice_in_dim(gm, chip * (D_MODEL // N_SHARD), D_MODEL // N_SHARD, axis=1)

    done.update(step_weights([n for n in _WEIGHTS if n not in done], grad))
    return (loss, dh[X0:][None], *[done[n][k] for k in range(4) for n in _WEIGHTS])


def kernel(x, meta, ffn1_w_gate, ffn1_w_up, ffn1_w_down, ln1_g, ln1_b, w_in, mla_q_norm_g, mla_w_uq, mla_kv_norm_g, mla_w_ukv, mla_w_o, conv_w, conv_b, conv_w_out, s5_a_re, s5_a_im, s5_log_dt, s5_b_re, s5_b_im, s5_c_re, s5_c_im, s5_d, s5_w_glu, s5_b_glu, s5_w_out, w_o, ln2_g, ln2_b, ffn2_w_gate, ffn2_w_up, ffn2_w_down, ln3_g, ln3_b, loss_target, m_meta, m_ffn1_w_gate, m_ffn1_w_up, m_ffn1_w_down, m_ln1_g, m_ln1_b, m_w_in, m_mla_q_norm_g, m_mla_w_uq, m_mla_kv_norm_g, m_mla_w_ukv, m_mla_w_o, m_conv_w, m_conv_b, m_conv_w_out, m_s5_a_re, m_s5_a_im, m_s5_log_dt, m_s5_b_re, m_s5_b_im, m_s5_c_re, m_s5_c_im, m_s5_d, m_s5_w_glu, m_s5_b_glu, m_s5_w_out, m_w_o, m_ln2_g, m_ln2_b, m_ffn2_w_gate, m_ffn2_w_up, m_ffn2_w_down, m_ln3_g, m_ln3_b, v_meta, v_ffn1_w_gate, v_ffn1_w_up, v_ffn1_w_down, v_ln1_g, v_ln1_b, v_w_in, v_mla_q_norm_g, v_mla_w_uq, v_mla_kv_norm_g, v_mla_w_ukv, v_mla_w_o, v_conv_w, v_conv_b, v_conv_w_out, v_s5_a_re, v_s5_a_im, v_s5_log_dt, v_s5_b_re, v_s5_b_im, v_s5_c_re, v_s5_c_im, v_s5_d, v_s5_w_glu, v_s5_b_glu, v_s5_w_out, v_w_o, v_ln2_g, v_ln2_b, v_ffn2_w_gate, v_ffn2_w_up, v_ffn2_w_down, v_ln3_g, v_ln3_b):
    return _step(dict(locals()))
```

```python
import functools
import math

import numpy as np
import jax
import jax.numpy as jnp
from jax import lax
from jax.experimental import pallas as pl
from jax.experimental.pallas import tpu as pltpu

F32 = jnp.float32
BF16 = jnp.bfloat16

D_MODEL = 1024
DEPTH = 2
N_META = 16
PAD = 112
X0 = PAD + N_META
N_HEADS = 8
D_NOPE = 64
D_ROPE = 32
D_V = 64
Q_RANK = 384
KV_RANK = 256
MIX = 512
S5_GROUPS = 32
S5_GROUP = 16
S5_STATE = 64
S5_LANES = S5_GROUPS * S5_STATE
D_FF = 2816
N_SHARD = 4
FF_SHARD = D_FF // N_SHARD
D_IN = 5792
P_IN = 6144
ALPHA = (2.0 * DEPTH) ** 0.25
LN_EPS = 1e-5
RMS_EPS = 1e-6
ATT_SCALE = (D_NOPE + D_ROPE) ** -0.5
ROPE_BASE = 10000.0
ADAM_LR, ADAM_B1, ADAM_B2, ADAM_EPS, ADAM_WD, ADAM_STEP = 0.001, 0.9, 0.999, 1e-08, 0.01, 10
SCAN_CHUNK = 128
VMEM_LIMIT = 52 * 2 ** 20
WGRAD = BF16
MESH = pl.DeviceIdType.MESH


def _cparams(**kw):
    return pltpu.CompilerParams(vmem_limit_bytes=VMEM_LIMIT, **kw)


def _tile(n):
    if n <= 1088:
        return n
    for t in (1024, 544, 512, 272, 256, 128):
        if n % t == 0:
            return t
    return n


def _row_tile(lp):
    for t in (544, 272, 128):
        if lp % t == 0:
            return t
    return lp


def _sigmoid(x):
    return 1.0 / (1.0 + jnp.exp(-x))


_GELU_C = math.sqrt(2.0 / math.pi)


def _gelu(x):
    return 0.5 * x * (1.0 + jnp.tanh(_GELU_C * (x + 0.044715 * x * x * x)))


def _gelu_grad(x):
    t = jnp.tanh(_GELU_C * (x + 0.044715 * x * x * x))
    return 0.5 * (1.0 + t) + 0.5 * x * (1.0 - t * t) * _GELU_C * (1.0 + 3.0 * 0.044715 * x * x)


def _dot(a, b, ca, cb, precision=None):
    return lax.dot_general(a, b, (((ca,), (cb,)), ((), ())), preferred_element_type=F32, precision=precision)


def matmul(a, b, *, name, ta=False, tb=False, ab='n', bb='n', res=None, res_scale=1.0, scale=1.0, out_dtype=F32):
    if ta:
        _, K, M = a.shape
    else:
        _, M, K = a.shape
    if tb:
        _, N, K2 = b.shape
    else:
        _, K2, N = b.shape
    assert K == K2, (a.shape, b.shape)
    n_out = max(a.shape[0] if ab == 'o' else 1, b.shape[0] if bb == 'o' else 1)
    n_red = max(a.shape[0] if ab == 'r' else 1, b.shape[0] if bb == 'r' else 1)
    tm, tn = _tile(M), _tile(N)
    tk = K if K <= 2304 else _tile(K)
    nkt = K // tk
    n_steps = n_red * nkt

    def bsel(mode, o, r):
        if mode == 'o':
            return o
        if mode == 'r':
            return r // nkt if nkt > 1 else r
        return 0

    def ksel(r):
        if nkt == 1:
            return 0
        return r % nkt if n_red > 1 else r

    a_map = (lambda o, i, j, r: (bsel(ab, o, r), ksel(r), i)) if ta else (lambda o, i, j, r: (bsel(ab, o, r), i, ksel(r)))
    b_map = (lambda o, i, j, r: (bsel(bb, o, r), j, ksel(r))) if tb else (lambda o, i, j, r: (bsel(bb, o, r), ksel(r), j))
    o_map = lambda o, i, j, r: (o, i, j)
    in_specs = [pl.BlockSpec((None, tk, tm) if ta else (None, tm, tk), a_map),
                pl.BlockSpec((None, tn, tk) if tb else (None, tk, tn), b_map)]
    operands = [a, b]
    if res is not None:
        in_specs.append(pl.BlockSpec((None, tm, tn), o_map))
        operands.append(res)
    has_res = res is not None

    def body(*refs):
        a_ref, b_ref = refs[0], refs[1]
        res_ref = refs[2] if has_res else None
        o_ref = refs[3] if has_res else refs[2]
        part = _dot(a_ref[...].astype(BF16), b_ref[...].astype(BF16), 0 if ta else 1, 1 if tb else 0)

        def finish(acc):
            v = acc if scale == 1.0 else acc * scale
            if has_res:
                v = v + res_scale * res_ref[...].astype(F32)
            o_ref[...] = v.astype(o_ref.dtype)

        if n_steps == 1:
            finish(part)
        else:
            acc_ref = refs[-1]
            r = pl.program_id(3)

            @pl.when(r == 0)
            def _():
                acc_ref[...] = part

            @pl.when(r > 0)
            def _():
                acc_ref[...] += part

            @pl.when(r == n_steps - 1)
            def _():
                finish(acc_ref[...])

    return pl.pallas_call(
        body, name=name,
        grid=(n_out, M // tm, N // tn, n_steps),
        in_specs=in_specs,
        out_specs=pl.BlockSpec((None, tm, tn), o_map),
        out_shape=jax.ShapeDtypeStruct((n_out, M, N), out_dtype),
        scratch_shapes=[pltpu.VMEM((tm, tn), F32)] if n_steps > 1 else [],
        compiler_params=_cparams(),
    )(*operands)


def rowwise(fn, rows, pars, outs, accs=(), *, name, lp):
    tm = _row_tile(lp)
    n_rows, n_pars, n_outs, n_accs = len(rows), len(pars), len(outs), len(accs)
    in_specs = [pl.BlockSpec((tm, w), functools.partial(lambda i, cb: (i, cb), cb=cb)) for _, w, cb in rows]
    in_specs += [pl.BlockSpec(p.shape, functools.partial(lambda i, nd: (0,) * nd, nd=p.ndim)) for p in pars]
    out_specs = [pl.BlockSpec((tm, w), lambda i: (i, 0)) for w, _ in outs]
    out_specs += [pl.BlockSpec(s, functools.partial(lambda i, nd: (0,) * nd, nd=len(s))) for s, _ in accs]
    out_shape = [jax.ShapeDtypeStruct((lp, w), dt) for w, dt in outs]
    out_shape += [jax.ShapeDtypeStruct(s, dt) for s, dt in accs]

    def body(*refs):
        i = pl.program_id(0)
        rv = [r[...] for r in refs[:n_rows]]
        pv = [r[...] for r in refs[n_rows:n_rows + n_pars]]
        o_refs = refs[n_rows + n_pars:n_rows + n_pars + n_outs]
        a_refs = refs[n_rows + n_pars + n_outs:]
        ov, av = fn(i * tm, rv, pv)
        for r, v in zip(o_refs, ov):
            r[...] = v.astype(r.dtype)
        if n_accs:
            @pl.when(i == 0)
            def _():
                for r, v in zip(a_refs, av):
                    r[...] = v.astype(r.dtype)

            @pl.when(i > 0)
            def _():
                for r, v in zip(a_refs, av):
                    r[...] += v.astype(r.dtype)

    res = pl.pallas_call(
        body, name=name, grid=(lp // tm,), in_specs=in_specs, out_specs=out_specs, out_shape=out_shape,
        compiler_params=_cparams(),
    )(*[r[0] for r in rows], *pars)
    return res


def _row_mask(row0, shape):
    return (row0 + lax.broadcasted_iota(jnp.int32, shape, 0)) >= PAD


def ffn_up(hb, wg, wu, lp):
    tm = _row_tile(lp)

    def body(h_ref, wg_ref, wu_ref, ab_ref, hid_ref):
        h = h_ref[...]
        a = _dot(h, wg_ref[...], 1, 1)
        b = _dot(h, wu_ref[...], 1, 1)
        ab_ref[0] = a.astype(BF16)
        ab_ref[1] = b.astype(BF16)
        hid_ref[...] = (a * _sigmoid(a) * b).astype(BF16)

    wspec = pl.BlockSpec((None, FF_SHARD, D_MODEL), lambda j, i: (j, 0, 0))
    return pl.pallas_call(
        body, name="ffn_up", grid=(N_SHARD, lp // tm),
        in_specs=[pl.BlockSpec((tm, D_MODEL), lambda j, i: (i, 0)), wspec, wspec],
        out_specs=[pl.BlockSpec((None, 2, tm, FF_SHARD), lambda j, i: (j, 0, i, 0)),
                   pl.BlockSpec((None, tm, FF_SHARD), lambda j, i: (j, i, 0))],
        out_shape=[jax.ShapeDtypeStruct((N_SHARD, 2, lp, FF_SHARD), BF16),
                   jax.ShapeDtypeStruct((N_SHARD, lp, FF_SHARD), BF16)],
        compiler_params=_cparams(),
    )(hb, wg, wu)


def _layer_norm(z, g, b):
    mu = jnp.mean(z, axis=-1, keepdims=True)
    zc = z - mu
    var = jnp.mean(zc * zc, axis=-1, keepdims=True)
    return zc * lax.rsqrt(var + LN_EPS) * g + b


def mm_res_ln(a, w, res, g, b, *, scale, name, lp):
    n_red, _, K = a.shape
    tm = _row_tile(lp)

    def body(a_ref, w_ref, res_ref, g_ref, b_ref, z_ref, h_ref, hb_ref, acc_ref):
        r = pl.program_id(1)
        part = _dot(a_ref[...].astype(BF16), w_ref[...], 1, 0)

        @pl.when(r == 0)
        def _():
            acc_ref[...] = part

        @pl.when(r > 0)
        def _():
            acc_ref[...] += part

        @pl.when(r == n_red - 1)
        def _():
            z = ALPHA * res_ref[...] + scale * acc_ref[...]
            z_ref[...] = z
            hn = _layer_norm(z, g_ref[...], b_ref[...])
            h_ref[...] = hn
            hb_ref[...] = hn.astype(BF16)

    row = pl.BlockSpec((tm, D_MODEL), lambda i, r: (i, 0))
    par = pl.BlockSpec((1, D_MODEL), lambda i, r: (0, 0))
    return pl.pallas_call(
        body, name=name, grid=(lp // tm, n_red),
        in_specs=[pl.BlockSpec((None, tm, K), lambda i, r: (r, i, 0)),
                  pl.BlockSpec((None, K, D_MODEL), lambda i, r: (r, 0, 0)), row, par, par],
        out_specs=[row, row, row],
        out_shape=[jax.ShapeDtypeStruct((lp, D_MODEL), F32), jax.ShapeDtypeStruct((lp, D_MODEL), F32),
                   jax.ShapeDtypeStruct((lp, D_MODEL), BF16)],
        scratch_shapes=[pltpu.VMEM((tm, D_MODEL), F32)],
        compiler_params=_cparams(),
    )(a, w, res, g, b)


def ln_bwd(dh, z, g, *, fscale, name, lp):
    def fn(row0, rv, pv):
        dh_, z_ = rv
        g_, = pv
        mu = jnp.mean(z_, axis=-1, keepdims=True)
        zc = z_ - mu
        rstd = lax.rsqrt(jnp.mean(zc * zc, axis=-1, keepdims=True) + LN_EPS)
        xh = zc * rstd
        dxh = dh_ * g_
        m1 = jnp.mean(dxh, axis=-1, keepdims=True)
        m2 = jnp.mean(dxh * xh, axis=-1, keepdims=True)
        dz = rstd * (dxh - m1 - xh * m2)
        return ((dz, fscale * dz),
                (jnp.sum(dh_ * xh, axis=0, keepdims=True), jnp.sum(dh_, axis=0, keepdims=True)))

    return rowwise(fn, [(dh, D_MODEL, 0), (z, D_MODEL, 0)], [g], [(D_MODEL, F32), (D_MODEL, BF16)],
                   [((1, D_MODEL), F32), ((1, D_MODEL), F32)], name=name, lp=lp)


def ffn_down_bwd(dfb, wd, ab, lp):
    tm = _row_tile(lp)

    def body(df_ref, w_ref, ab_ref, da_ref, db_ref):
        dhid = _dot(df_ref[...], w_ref[...], 1, 1)
        a = ab_ref[0].astype(F32)
        b = ab_ref[1].astype(F32)
        sg = _sigmoid(a)
        da_ref[...] = (dhid * b * (sg * (1.0 + a * (1.0 - sg)))).astype(BF16)
        db_ref[...] = (dhid * (a * sg)).astype(BF16)

    ospec = pl.BlockSpec((None, tm, FF_SHARD), lambda j, i: (j, i, 0))
    return pl.pallas_call(
        body, name="ffn_down_bwd", grid=(N_SHARD, lp // tm),
        in_specs=[pl.BlockSpec((tm, D_MODEL), lambda j, i: (i, 0)),
                  pl.BlockSpec((None, FF_SHARD, D_MODEL), lambda j, i: (j, 0, 0)),
                  pl.BlockSpec((None, 2, tm, FF_SHARD), lambda j, i: (j, 0, i, 0))],
        out_specs=[ospec, ospec],
        out_shape=[jax.ShapeDtypeStruct((N_SHARD, lp, FF_SHARD), BF16)] * 2,
        compiler_params=_cparams(),
    )(dfb, wd, ab)


def ffn_dx(da, db, wg, wu, dz, lp):
    tm = _row_tile(lp)

    def body(da_ref, db_ref, wg_ref, wu_ref, dz_ref, o_ref, acc_ref):
        j = pl.program_id(1)
        part = _dot(da_ref[...], wg_ref[...], 1, 0) + _dot(db_ref[...], wu_ref[...], 1, 0)

        @pl.when(j == 0)
        def _():
            acc_ref[...] = part

        @pl.when(j > 0)
        def _():
            acc_ref[...] += part

        @pl.when(j == N_SHARD - 1)
        def _():
            o_ref[...] = acc_ref[...] + ALPHA * dz_ref[...]

    aspec = pl.BlockSpec((None, tm, FF_SHARD), lambda i, j: (j, i, 0))
    wspec = pl.BlockSpec((None, FF_SHARD, D_MODEL), lambda i, j: (j, 0, 0))
    row = pl.BlockSpec((tm, D_MODEL), lambda i, j: (i, 0))
    return pl.pallas_call(
        body, name="ffn_dx", grid=(lp // tm, N_SHARD), in_specs=[aspec, aspec, wspec, wspec, row], out_specs=row,
        out_shape=jax.ShapeDtypeStruct((lp, D_MODEL), F32), scratch_shapes=[pltpu.VMEM((tm, D_MODEL), F32)],
        compiler_params=_cparams(),
    )(da, db, wg, wu, dz)


def ffn_fwd(h, hb, wg, wu, wd, g, b, lp):
    ab, hid = ffn_up(hb, wg, wu, lp)
    z, hn, hnb = mm_res_ln(hid, wd, h, g, b, scale=0.5, name="ffn_down_ln", lp=lp)
    return hn, hnb, dict(hb=hb, ab=ab, hid=hid, z=z)


def ffn_bwd(dh, sv, wg, wu, wd, g, lp):
    dz, dfb, dg, db = ln_bwd(dh, sv['z'], g, fscale=0.5, name="ffn_ln_bwd", lp=lp)
    da, dbb = ffn_down_bwd(dfb, wd, sv['ab'], lp)
    d_wd = matmul(sv['hid'], dfb[None], ta=True, ab='o', out_dtype=WGRAD, name="ffn_dwd")
    d_wg = matmul(da, sv['hb'][None], ta=True, ab='o', out_dtype=WGRAD, name="ffn_dwg")
    d_wu = matmul(dbb, sv['hb'][None], ta=True, ab='o', out_dtype=WGRAD, name="ffn_dwu")
    dh_in = ffn_dx(da, dbb, wg, wu, dz, lp)
    return dh_in, dict(wg=d_wg, wu=d_wu, wd=d_wd, ln_g=dg, ln_b=db)


def _rope_tables(lp):
    pos = np.arange(lp, dtype=np.float32) - PAD
    inv = ROPE_BASE ** (-np.arange(0, D_ROPE, 2, dtype=np.float32) / D_ROPE)
    ang = pos[:, None] * inv[None, :]
    cos = np.concatenate([np.cos(ang), np.cos(ang)], axis=1).astype(np.float32)
    sin = np.concatenate([np.sin(ang), np.sin(ang)], axis=1).astype(np.float32)
    rot = np.zeros((D_ROPE, D_ROPE), np.float32)
    half = D_ROPE // 2
    for j in range(half):
        rot[j + half, j] = -1.0
        rot[j, j + half] = 1.0
    return jnp.asarray(cos), jnp.asarray(sin), jnp.asarray(rot)


def _rot(x, rot):
    return _dot(x, rot, 1, 0, precision=lax.Precision.HIGHEST)


def _rms(x, g):
    r = lax.rsqrt(jnp.mean(x * x, axis=-1, keepdims=True) + RMS_EPS)
    return x * r * g


def mla_prep(proj, cos, sin, rot, qg, kvg, lp):
    def fn(row0, rv, pv):
        cq, krb, ckv, c, s = rv
        qg_, kvg_, rot_ = pv
        kr = krb[:, :D_ROPE]
        return ((_rms(cq, qg_), _rms(ckv, kvg_), kr * c + _rot(kr, rot_) * s), ())

    return rowwise(fn, [(proj, Q_RANK, 0), (proj, 128, 3), (proj, KV_RANK, 2), (cos, D_ROPE, 0), (sin, D_ROPE, 0)],
                   [qg, kvg, rot], [(Q_RANK, BF16), (KV_RANK, BF16), (D_ROPE, BF16)], name="mla_prep", lp=lp)


def mla_heads(cqn, ckvn, cos, sin, rot, wqn, wqr, wkn, wv, lp):
    tm = _row_tile(lp)

    def body(cq_ref, ckv_ref, c_ref, s_ref, rot_ref, wqn_ref, wqr_ref, wkn_ref, wv_ref, qn_ref, qr_ref, kn_ref, v_ref):
        cq = cq_ref[...]
        ckv = ckv_ref[...]
        qn_ref[...] = _dot(cq, wqn_ref[...], 1, 0).astype(BF16)
        qr = _dot(cq, wqr_ref[...], 1, 0)
        qr_ref[...] = (qr * c_ref[...] + _rot(qr, rot_ref[...]) * s_ref[...]).astype(BF16)
        kn_ref[...] = _dot(ckv, wkn_ref[...], 1, 0).astype(BF16)
        v_ref[...] = _dot(ckv, wv_ref[...], 1, 0).astype(BF16)

    def row(w):
        return pl.BlockSpec((tm, w), lambda h, i: (i, 0))

    def wspec(k, n):
        return pl.BlockSpec((None, k, n), lambda h, i: (h, 0, 0))

    def ospec(n):
        return pl.BlockSpec((None, tm, n), lambda h, i: (h, i, 0))

    return pl.pallas_call(
        body, name="mla_heads", grid=(N_HEADS, lp // tm),
        in_specs=[row(Q_RANK), row(KV_RANK), row(D_ROPE), row(D_ROPE),
                  pl.BlockSpec((D_ROPE, D_ROPE), lambda h, i: (0, 0)),
                  wspec(Q_RANK, D_NOPE), wspec(Q_RANK, D_ROPE), wspec(KV_RANK, D_NOPE), wspec(KV_RANK, D_V)],
        out_specs=[ospec(D_NOPE), ospec(D_ROPE), ospec(D_NOPE), ospec(D_V)],
        out_shape=[jax.ShapeDtypeStruct((N_HEADS, lp, D_NOPE), BF16), jax.ShapeDtypeStruct((N_HEADS, lp, D_ROPE), BF16),
                   jax.ShapeDtypeStruct((N_HEADS, lp, D_NOPE), BF16), jax.ShapeDtypeStruct((N_HEADS, lp, D_V), BF16)],
        compiler_params=_cparams(),
    )(cqn, ckvn, cos, sin, rot, wqn, wqr, wkn, wv)


def _att_probs(qn, qr, kn, kr, row0, tq, lp):
    s = (_dot(qn, kn, 1, 1) + _dot(qr, kr, 1, 1)) * ATT_SCALE
    qi = row0 + lax.broadcasted_iota(jnp.int32, (tq, lp), 0)
    ki = lax.broadcasted_iota(jnp.int32, (tq, lp), 1)
    s = jnp.where((ki <= qi) & (ki >= PAD), s, -1e30)
    p = jnp.exp(s - jnp.max(s, axis=-1, keepdims=True))
    return p / jnp.sum(p, axis=-1, keepdims=True)


def _att_specs(lp):
    def hspec(n):
        return pl.BlockSpec((None, lp, n), lambda h, i: (h, 0, 0))

    return hspec, pl.BlockSpec((lp, D_ROPE), lambda h, i: (0, 0))


def _att_tiles(lp):
    tiles, r = [(0, X0)], X0
    while r < lp:
        tiles.append((r, 256))
        r += 256
    assert r == lp
    return tiles


def attn_fwd(qn, qr, kn, v, kr, lp):
    kspec, krspec = _att_specs(lp)

    def body(qn_ref, qr_ref, kn_ref, v_ref, kr_ref, o_ref):
        for r0, rows in _att_tiles(lp):
            ke, q = r0 + rows, slice(r0, r0 + rows)
            p = _att_probs(qn_ref[q, :], qr_ref[q, :], kn_ref[0:ke, :], kr_ref[0:ke, :], r0, rows, ke)
            o_ref[q, :] = _dot(p.astype(BF16), v_ref[0:ke, :], 1, 0).astype(BF16)

    return pl.pallas_call(
        body, name="attn_fwd", grid=(N_HEADS, 1),
        in_specs=[kspec(D_NOPE), kspec(D_ROPE), kspec(D_NOPE), kspec(D_V), krspec],
        out_specs=kspec(D_V), out_shape=jax.ShapeDtypeStruct((N_HEADS, lp, D_V), BF16),
        compiler_params=_cparams(),
    )(qn, qr, kn, v, kr)


def attn_bwd(qn, qr, kn, v, kr, do, lp):
    kspec, krspec = _att_specs(lp)

    def body(qn_ref, qr_ref, kn_ref, v_ref, kr_ref, do_ref, dqn_ref, dqr_ref, dkn_ref, dv_ref, dkr_ref):
        dkn_ref[...] = jnp.zeros_like(dkn_ref)
        dv_ref[...] = jnp.zeros_like(dv_ref)

        @pl.when(pl.program_id(0) == 0)
        def _():
            dkr_ref[...] = jnp.zeros_like(dkr_ref)

        for r0, rows in _att_tiles(lp):
            ke, q = r0 + rows, slice(r0, r0 + rows)
            qn_, qr_, do_ = qn_ref[q, :], qr_ref[q, :], do_ref[q, :]
            kn_, v_, kr_ = kn_ref[0:ke, :], v_ref[0:ke, :], kr_ref[0:ke, :]
            p = _att_probs(qn_, qr_, kn_, kr_, r0, rows, ke)
            dp = _dot(do_, v_, 1, 1)
            delta = jnp.sum(p * dp, axis=-1, keepdims=True)
            ds = (p * (dp - delta) * ATT_SCALE).astype(BF16)
            dqn_ref[q, :] = _dot(ds, kn_, 1, 0).astype(BF16)
            dqr_ref[q, :] = _dot(ds, kr_, 1, 0)
            dkn_ref[0:ke, :] += _dot(ds, qn_, 0, 0)
            dv_ref[0:ke, :] += _dot(p.astype(BF16), do_, 0, 0)
            dkr_ref[0:ke, :] += _dot(ds, qr_, 0, 0)

    return pl.pallas_call(
        body, name="attn_bwd", grid=(N_HEADS, 1),
        in_specs=[kspec(D_NOPE), kspec(D_ROPE), kspec(D_NOPE), kspec(D_V), krspec, kspec(D_V)],
        out_specs=[kspec(D_NOPE), kspec(D_ROPE), kspec(D_NOPE), kspec(D_V), krspec],
        out_shape=[jax.ShapeDtypeStruct((N_HEADS, lp, D_NOPE), BF16), jax.ShapeDtypeStruct((N_HEADS, lp, D_ROPE), F32),
                   jax.ShapeDtypeStruct((N_HEADS, lp, D_NOPE), F32), jax.ShapeDtypeStruct((N_HEADS, lp, D_V), F32),
                   jax.ShapeDtypeStruct((lp, D_ROPE), F32)],
        compiler_params=_cparams(),
    )(qn, qr, kn, v, kr, do)


def mla_heads_bwd(dqn, dqr, dkn, dv, cos, sin, rot, wqn, wqr, wkn, wv, lp):
    tm = _row_tile(lp)

    def body(dqn_ref, dqr_ref, dkn_ref, dv_ref, c_ref, s_ref, rot_ref, wqn_ref, wqr_ref, wkn_ref, wv_ref,
             dcq_ref, dckv_ref, dqrp_ref):
        h = pl.program_id(1)
        dqr_ = dqr_ref[...]
        dqrp = (dqr_ * c_ref[...] - _rot(dqr_ * s_ref[...], rot_ref[...])).astype(BF16)
        dqrp_ref[...] = dqrp
        dcq = _dot(dqn_ref[...], wqn_ref[...], 1, 1) + _dot(dqrp, wqr_ref[...], 1, 1)
        dckv = _dot(dkn_ref[...].astype(BF16), wkn_ref[...], 1, 1) + _dot(dv_ref[...].astype(BF16), wv_ref[...], 1, 1)

        @pl.when(h == 0)
        def _():
            dcq_ref[...] = dcq
            dckv_ref[...] = dckv

        @pl.when(h > 0)
        def _():
            dcq_ref[...] += dcq
            dckv_ref[...] += dckv

    def hspec(n):
        return pl.BlockSpec((None, tm, n), lambda i, h: (h, i, 0))

    def row(w):
        return pl.BlockSpec((tm, w), lambda i, h: (i, 0))

    def wspec(k, n):
        return pl.BlockSpec((None, k, n), lambda i, h: (h, 0, 0))

    return pl.pallas_call(
        body, name="mla_heads_bwd", grid=(lp // tm, N_HEADS),
        in_specs=[hspec(D_NOPE), hspec(D_ROPE), hspec(D_NOPE), hspec(D_V), row(D_ROPE), row(D_ROPE),
                  pl.BlockSpec((D_ROPE, D_ROPE), lambda i, h: (0, 0)),
                  wspec(Q_RANK, D_NOPE), wspec(Q_RANK, D_ROPE), wspec(KV_RANK, D_NOPE), wspec(KV_RANK, D_V)],
        out_specs=[row(Q_RANK), row(KV_RANK), hspec(D_ROPE)],
        out_shape=[jax.ShapeDtypeStruct((lp, Q_RANK), F32), jax.ShapeDtypeStruct((lp, KV_RANK), F32),
                   jax.ShapeDtypeStruct((N_HEADS, lp, D_ROPE), BF16)],
        compiler_params=_cparams(),
    )(dqn, dqr, dkn, dv, cos, sin, rot, wqn, wqr, wkn, wv)


def _rms_bwd(dy, x, g):
    r = lax.rsqrt(jnp.mean(x * x, axis=-1, keepdims=True) + RMS_EPS)
    n = x * r
    dn = dy * g
    dx = r * (dn - n * jnp.mean(dn * n, axis=-1, keepdims=True))
    return dx, jnp.sum(dy * n, axis=0, keepdims=True)


def mla_prep_bwd(dcq, dckv, dkr, proj, cos, sin, rot, qg, kvg, lp):
    def fn(row0, rv, pv):
        dcq_, dckv_, dkr_, cq, ckv, c, s = rv
        qg_, kvg_, rot_ = pv
        dxq, dgq = _rms_bwd(dcq_, cq, qg_)
        dxkv, dgkv = _rms_bwd(dckv_, ckv, kvg_)
        dkr_raw = dkr_ * c - _rot(dkr_ * s, rot_)
        return ((dxq, dxkv, dkr_raw), (dgq, dgkv))

    return rowwise(fn, [(dcq, Q_RANK, 0), (dckv, KV_RANK, 0), (dkr, D_ROPE, 0), (proj, Q_RANK, 0), (proj, KV_RANK, 2),
                        (cos, D_ROPE, 0), (sin, D_ROPE, 0)], [qg, kvg, rot],
                   [(Q_RANK, BF16), (KV_RANK, BF16), (D_ROPE, BF16)], [((1, Q_RANK), F32), ((1, KV_RANK), F32)],
                   name="mla_prep_bwd", lp=lp)


def _shift_down(x, d, rows):
    return jnp.where(rows >= d, pltpu.roll(x, d, 0), 0.0)


def _shift_up(x, d, rows, n):
    return jnp.where(rows < n - d, pltpu.roll(x, n - d, 0), 0.0)


_CONV_W = 128
_XB, _BG, _CG = 1024 // _CONV_W, 1536 // _CONV_W, 2048 // _CONV_W


def _conv_specs(lp):
    def pspec(base):
        return pl.BlockSpec((lp, _CONV_W), functools.partial(lambda c, base: (0, base + c), base=base))

    col = pl.BlockSpec((lp, _CONV_W), lambda c: (0, c))
    wspec = pl.BlockSpec((3, _CONV_W), lambda c: (0, c))
    bspec = pl.BlockSpec((1, _CONV_W), lambda c: (0, c))
    return pspec, col, wspec, bspec


def _conv_core(xbar, cg, w, bias, lp):
    rows = lax.broadcasted_iota(jnp.int32, (lp, _CONV_W), 0)
    u = jnp.where(rows >= PAD, cg * xbar, 0.0)
    u1 = _shift_down(u, 1, rows)
    u2 = _shift_down(u, 2, rows)
    y = bias + w[0:1] * u2 + w[1:2] * u1 + w[2:3] * u
    return rows, u, u1, u2, y


def conv_fwd(proj, w, bias, lp):
    pspec, col, wspec, bspec = _conv_specs(lp)

    def body(x_ref, b_ref, c_ref, w_ref, bias_ref, v_ref):
        _, _, _, _, y = _conv_core(x_ref[...], c_ref[...], w_ref[...], bias_ref[...], lp)
        v_ref[...] = (b_ref[...] * y).astype(BF16)

    return pl.pallas_call(
        body, name="conv_fwd", grid=(MIX // _CONV_W,),
        in_specs=[pspec(_XB), pspec(_BG), pspec(_CG), wspec, bspec], out_specs=col,
        out_shape=jax.ShapeDtypeStruct((lp, MIX), BF16), compiler_params=_cparams(),
    )(proj, proj, proj, w, bias)


def conv_bwd(dv, proj, w, bias, lp):
    pspec, col, wspec, bspec = _conv_specs(lp)

    def body(dv_ref, x_ref, b_ref, c_ref, w_ref, bias_ref, dx_ref, db_ref, dc_ref, dw_ref, dbias_ref):
        xbar, cg, w_ = x_ref[...], c_ref[...], w_ref[...]
        rows, u, u1, u2, y = _conv_core(xbar, cg, w_, bias_ref[...], lp)
        dv_ = dv_ref[...]
        db_ref[...] = (dv_ * y).astype(BF16)
        dy = dv_ * b_ref[...]
        dbias_ref[...] = jnp.sum(dy, axis=0, keepdims=True)
        dw_ref[0:1, :] = jnp.sum(dy * u2, axis=0, keepdims=True)
        dw_ref[1:2, :] = jnp.sum(dy * u1, axis=0, keepdims=True)
        dw_ref[2:3, :] = jnp.sum(dy * u, axis=0, keepdims=True)
        du = w_[2:3] * dy + w_[1:2] * _shift_up(dy, 1, rows, lp) + w_[0:1] * _shift_up(dy, 2, rows, lp)
        du = jnp.where(rows >= PAD, du, 0.0)
        dc_ref[...] = (du * xbar).astype(BF16)
        dx_ref[...] = (du * cg).astype(BF16)

    return pl.pallas_call(
        body, name="conv_bwd", grid=(MIX // _CONV_W,),
        in_specs=[col, pspec(_XB), pspec(_BG), pspec(_CG), wspec, bspec],
        out_specs=[col, col, col, wspec, bspec],
        out_shape=[jax.ShapeDtypeStruct((lp, MIX), BF16)] * 3 + [jax.ShapeDtypeStruct((3, MIX), F32),
                                                                jax.ShapeDtypeStruct((1, MIX), F32)],
        compiler_params=_cparams(),
    )(dv, proj, proj, proj, w, bias)


def _s5_disc(a_re, a_im, log_dt, b_re, b_im):
    dt = jnp.exp(log_dt)
    mag = jnp.exp(dt * a_re)
    ab_re, ab_im = mag * jnp.cos(dt * a_im), mag * jnp.sin(dt * a_im)
    den = a_re * a_re + a_im * a_im
    nr, ni = ab_re - 1.0, ab_im
    coef_re = (nr * a_re + ni * a_im) / den
    coef_im = (ni * a_re - nr * a_im) / den
    return ab_re, ab_im, coef_re * b_re - coef_im * b_im, coef_re * b_im + coef_im * b_re


_S5_ROWS = S5_GROUPS * S5_GROUP


def s5_prep(a_re, a_im, log_dt, b_re, b_im):
    def body(ar, ai, ld, br, bi, o0, o1, o2, o3):
        for o, v in zip((o0, o1, o2, o3), _s5_disc(ar[...], ai[...], ld[...], br[...], bi[...])):
            o[...] = v

    return pl.pallas_call(body, name="s5_prep",
                          out_shape=[jax.ShapeDtypeStruct((_S5_ROWS, S5_STATE), F32)] * 4)(a_re, a_im, log_dt, b_re, b_im)


def s5_prep_bwd(a_re, a_im, log_dt, b_re, b_im, d_ab_re, d_ab_im, d_bb_re, d_bb_im, sel):
    def body(ar, ai, ld, br, bi, g0, g1, g2, g3, sel_ref, da_re, da_im, dld, dbr, dbi):
        _, vjp = jax.vjp(_s5_disc, ar[...], ai[...], ld[...], br[...], bi[...])
        c_ar, c_ai, c_ld, c_br, c_bi = vjp((g0[...], g1[...], g2[...], g3[...]))
        s = sel_ref[...]
        hi = lax.Precision.HIGHEST
        da_re[...] = _dot(s, c_ar, 1, 0, precision=hi)
        da_im[...] = _dot(s, c_ai, 1, 0, precision=hi)
        dld[...] = jnp.sum(_dot(s, c_ld, 1, 0, precision=hi), axis=-1, keepdims=True)
        dbr[...] = c_br
        dbi[...] = c_bi

    g = jax.ShapeDtypeStruct((S5_GROUPS, S5_STATE), F32)
    full = jax.ShapeDtypeStruct((_S5_ROWS, S5_STATE), F32)
    return pl.pallas_call(body, name="s5_prep_bwd",
                          out_shape=[g, g, jax.ShapeDtypeStruct((S5_GROUPS, 1), F32), full, full],
                          )(a_re, a_im, log_dt, b_re, b_im, d_ab_re, d_ab_im, d_bb_re, d_bb_im, sel)


_SCAN_W = 128
_SCAN_STEPS = int(math.log2(SCAN_CHUNK))


def _cmul(ar, ai, br, bi):
    return ar * br - ai * bi, ar * bi + ai * br


def _scan_powers(ar, ai, reverse):
    pw = [(ar, ai)]
    for _ in range(_SCAN_STEPS):
        pw.append(_cmul(*pw[-1], *pw[-1]))
    rows = lax.broadcasted_iota(jnp.int32, (SCAN_CHUNK, ar.shape[-1]), 0)
    tr = jnp.broadcast_to(ar, rows.shape)
    ti = jnp.broadcast_to(ai, rows.shape)
    for k in range(_SCAN_STEPS):
        d = 2 ** k
        if reverse:
            live = rows < SCAN_CHUNK - d
            mr, mi = _cmul(tr, ti, _shift_up(tr, d, rows, SCAN_CHUNK), _shift_up(ti, d, rows, SCAN_CHUNK))
        else:
            live = rows >= d
            mr, mi = _cmul(tr, ti, _shift_down(tr, d, rows), _shift_down(ti, d, rows))
        tr = jnp.where(live, mr, tr)
        ti = jnp.where(live, mi, ti)
    return pw, rows, tr, ti


def s5_scan(bu, ab_re, ab_im, lp):
    n_chunks = lp // SCAN_CHUNK

    def body(bu_ref, ar_ref, ai_ref, s_ref):
        ar, ai = ar_ref[...], ai_ref[...]
        pw, rows, tr, ti = _scan_powers(ar, ai, False)

        def chunk(ci, carry):
            cr, cim = carry
            r0 = pl.multiple_of(ci * SCAN_CHUNK, SCAN_CHUNK)
            xr = bu_ref[0, pl.ds(r0, SCAN_CHUNK), :]
            xi = bu_ref[1, pl.ds(r0, SCAN_CHUNK), :]
            for k in range(_SCAN_STEPS):
                d = 2 ** k
                mr, mi = _cmul(pw[k][0], pw[k][1], _shift_down(xr, d, rows), _shift_down(xi, d, rows))
                xr, xi = xr + mr, xi + mi
            mr, mi = _cmul(tr, ti, cr, cim)
            xr, xi = xr + mr, xi + mi
            s_ref[0, pl.ds(r0, SCAN_CHUNK), :] = xr
            s_ref[1, pl.ds(r0, SCAN_CHUNK), :] = xi
            return xr[SCAN_CHUNK - 1:SCAN_CHUNK, :], xi[SCAN_CHUNK - 1:SCAN_CHUNK, :]

        zero = jnp.zeros((1, _SCAN_W), F32)
        lax.fori_loop(0, n_chunks, chunk, (zero, zero))

    spec = pl.BlockSpec((2, lp, _SCAN_W), lambda c: (0, 0, c))
    aspec = pl.BlockSpec((1, _SCAN_W), lambda c: (0, c))
    return pl.pallas_call(
        body, name="s5_scan", grid=(S5_LANES // _SCAN_W,), in_specs=[spec, aspec, aspec], out_specs=spec,
        out_shape=jax.ShapeDtypeStruct((2, lp, S5_LANES), F32), compiler_params=_cparams(),
    )(bu, ab_re, ab_im)


def s5_scan_bwd(ds, s, ab_re, ab_im, lp):
    n_chunks = lp // SCAN_CHUNK

    def body(ds_ref, s_ref, ar_ref, ai_ref, g_ref, da_ref):
        ar, ai = ar_ref[...], -ai_ref[...]
        pw, rows, tr, ti = _scan_powers(ar, ai, True)

        def chunk(k, carry):
            cr, cim, dar, dai = carry
            ci = n_chunks - 1 - k
            r0 = pl.multiple_of(ci * SCAN_CHUNK, SCAN_CHUNK)
            xr = ds_ref[0, pl.ds(r0, SCAN_CHUNK), :]
            xi = ds_ref[1, pl.ds(r0, SCAN_CHUNK), :]
            for j in range(_SCAN_STEPS):
                d = 2 ** j
                mr, mi = _cmul(pw[j][0], pw[j][1], _shift_up(xr, d, rows, SCAN_CHUNK), _shift_up(xi, d, rows, SCAN_CHUNK))
                xr, xi = xr + mr, xi + mi
            mr, mi = _cmul(tr, ti, cr, cim)
            xr, xi = xr + mr, xi + mi
            g_ref[0, pl.ds(r0, SCAN_CHUNK), :] = xr
            g_ref[1, pl.ds(r0, SCAN_CHUNK), :] = xi
            prev0 = pl.multiple_of(jnp.maximum(r0 - 8, 0), 8)
            live = (ci > 0).astype(F32)
            pr = s_ref[0, pl.ds(prev0, 8), :][7:8, :] * live
            pim = s_ref[1, pl.ds(prev0, 8), :][7:8, :] * live
            sr = s_ref[0, pl.ds(r0, SCAN_CHUNK), :]
            si = s_ref[1, pl.ds(r0, SCAN_CHUNK), :]
            sr = jnp.where(rows >= 1, pltpu.roll(sr, 1, 0), pr)
            si = jnp.where(rows >= 1, pltpu.roll(si, 1, 0), pim)
            dar = dar + jnp.sum(xr * sr + xi * si, axis=0, keepdims=True)
            dai = dai + jnp.sum(xi * sr - xr * si, axis=0, keepdims=True)
            return xr[0:1, :], xi[0:1, :], dar, dai

        zero = jnp.zeros((1, _SCAN_W), F32)
        _, _, dar, dai = lax.fori_loop(0, n_chunks, chunk, (zero, zero, zero, zero))
        da_ref[0] = dar
        da_ref[1] = dai

    spec = pl.BlockSpec((2, lp, _SCAN_W), lambda c: (0, 0, c))
    aspec = pl.BlockSpec((1, _SCAN_W), lambda c: (0, c))
    return pl.pallas_call(
        body, name="s5_scan_bwd", grid=(S5_LANES // _SCAN_W,), in_specs=[spec, spec, aspec, aspec],
        out_specs=[spec, pl.BlockSpec((2, 1, _SCAN_W), lambda c: (0, 0, c))],
        out_shape=[jax.ShapeDtypeStruct((2, lp, S5_LANES), F32), jax.ShapeDtypeStruct((2, 1, S5_LANES), F32)],
        compiler_params=_cparams(),
    )(ds, s, ab_re, ab_im)


S5_BLOCKS = 4
_S5_PER = S5_GROUPS // S5_BLOCKS


def _blockdiag(x):
    _, r, c = x.shape
    eye = jnp.eye(_S5_PER, dtype=x.dtype)
    x = x.reshape(S5_BLOCKS, _S5_PER, r, c)
    return (x[:, :, :, None, :] * eye[None, :, None, :, None]).reshape(S5_BLOCKS, _S5_PER * r, _S5_PER * c)


def _blockdiag_extract(m, r, c):
    return jnp.einsum('qgrgc->qgrc', m.reshape(S5_BLOCKS, _S5_PER, r, _S5_PER, c)).reshape(S5_GROUPS, r, c)


def bd_matmul(a, w, *, w_t, reduce, res=None, name):
    _, M, _ = a.shape
    n_w, _, k1, k2 = w.shape
    ka, kout = (k2, k1) if w_t else (k1, k2)
    tm = _row_tile(M)
    n_out, n_red = (1, n_w) if reduce else (n_w, 1)
    has_res = res is not None

    assert n_red <= 2

    def body(*refs):
        a_ref, w_ref = refs[0], refs[1]
        o_ref = refs[3] if has_res else refs[2]
        for q in range(S5_BLOCKS):
            cols = slice(q * kout, (q + 1) * kout)
            part = _dot(a_ref[:, q * ka:(q + 1) * ka].astype(BF16), w_ref[q], 1, 1 if w_t else 0)
            if n_red == 1:
                o_ref[:, cols] = part
            else:
                acc_ref = refs[-1]

                @pl.when(pl.program_id(2) == 0)
                def _():
                    acc_ref[:, cols] = part

                @pl.when(pl.program_id(2) == 1)
                def _():
                    tot = acc_ref[:, cols] + part
                    o_ref[:, cols] = tot + refs[2][:, cols] if has_res else tot

    if reduce:
        a_map, w_map = (lambda o, i, r: (r, i, 0)), (lambda o, i, r: (r, 0, 0, 0))
    else:
        a_map, w_map = (lambda o, i, r: (0, i, 0)), (lambda o, i, r: (o, 0, 0, 0))
    o_map = lambda o, i, r: (o, i, 0)
    in_specs = [pl.BlockSpec((None, tm, S5_BLOCKS * ka), a_map), pl.BlockSpec((None, S5_BLOCKS, k1, k2), w_map)]
    operands = [a, w]
    if has_res:
        in_specs.append(pl.BlockSpec((None, tm, S5_BLOCKS * kout), o_map))
        operands.append(res)
    return pl.pallas_call(
        body, name=name, grid=(n_out, M // tm, n_red), in_specs=in_specs,
        out_specs=pl.BlockSpec((None, tm, S5_BLOCKS * kout), o_map),
        out_shape=jax.ShapeDtypeStruct((n_out, M, S5_BLOCKS * kout), F32),
        scratch_shapes=[pltpu.VMEM((tm, S5_BLOCKS * kout), F32)] if n_red > 1 else [],
        compiler_params=_cparams(),
    )(*operands)


def bd_outer(a, b, name):
    na, M, wa = a.shape
    nb_, _, wb = b.shape
    ka, kb = wa // S5_BLOCKS, wb // S5_BLOCKS
    n_out = max(na, nb_)

    def body(a_ref, b_ref, o_ref):
        o_ref[...] = _dot(a_ref[...].astype(BF16), b_ref[...].astype(BF16), 0, 0)

    return pl.pallas_call(
        body, name=name, grid=(n_out, S5_BLOCKS),
        in_specs=[pl.BlockSpec((None, M, ka), (lambda o, q: (o, 0, q)) if na > 1 else (lambda o, q: (0, 0, q))),
                  pl.BlockSpec((None, M, kb), (lambda o, q: (o, 0, q)) if nb_ > 1 else (lambda o, q: (0, 0, q)))],
        out_specs=pl.BlockSpec((None, None, ka, kb), lambda o, q: (o, q, 0, 0)),
        out_shape=jax.ShapeDtypeStruct((n_out, S5_BLOCKS, ka, kb), F32), compiler_params=_cparams(),
    )(a, b)


def s5_u(proj, lp):
    def fn(row0, rv, pv):
        u, = rv
        return ((jnp.where(_row_mask(row0, u.shape), u, 0.0),), ())

    return rowwise(fn, [(proj, MIX, 5)], [], [(MIX, BF16)], name="s5_u", lp=lp)[0]


def s5_y(ys, proj, d, lp):
    def fn(row0, rv, pv):
        ys_, u = rv
        y = ys_ + pv[0] * u
        return ((y, _gelu(y)), ())

    return rowwise(fn, [(ys, MIX, 0), (proj, MIX, 5)], [d], [(MIX, F32), (MIX, BF16)], name="s5_y", lp=lp)


def s5_glu(z, y, b, lp):
    def fn(row0, rv, pv):
        z_, y_ = rv
        return ((_gelu(y_) * _sigmoid(z_ + pv[0]),), ())

    return rowwise(fn, [(z, MIX, 0), (y, MIX, 0)], [b], [(MIX, BF16)], name="s5_glu", lp=lp)[0]


def s5_glu_bwd(dgl, z, y, b, lp):
    def fn(row0, rv, pv):
        dgl_, z_, y_ = rv
        sg = _sigmoid(z_ + pv[0])
        dz = dgl_ * _gelu(y_) * sg * (1.0 - sg)
        return ((dgl_ * sg, dz), (jnp.sum(dz, axis=0, keepdims=True),))

    return rowwise(fn, [(dgl, MIX, 0), (z, MIX, 0), (y, MIX, 0)], [b], [(MIX, F32), (MIX, BF16)], [((1, MIX), F32)],
                   name="s5_glu_bwd", lp=lp)


def s5_y_bwd(dyg, y, proj, d, lp):
    def fn(row0, rv, pv):
        dyg_, y_, u = rv
        dy = dyg_ * _gelu_grad(y_)
        return ((dy, dy * pv[0]), (jnp.sum(dy * u, axis=0, keepdims=True),))

    return rowwise(fn, [(dyg, MIX, 0), (y, MIX, 0), (proj, MIX, 5)], [d], [(MIX, BF16), (MIX, F32)], [((1, MIX), F32)],
                   name="s5_y_bwd", lp=lp)


def s5_du(du, lp):
    def fn(row0, rv, pv):
        return ((jnp.where(_row_mask(row0, rv[0].shape), rv[0], 0.0),), ())

    return rowwise(fn, [(du, MIX, 0)], [], [(MIX, BF16)], name="s5_du", lp=lp)[0]


def merge_fwd(proj, ya, yb, yc, lp):
    def fn(row0, rv, pv):
        g0, g1, g2, a, b, c = rv
        return ((_sigmoid(g0) * a + _sigmoid(g1) * b + _sigmoid(g2) * c,), ())

    return rowwise(fn, [(proj, D_MODEL, 3), (proj, D_MODEL, 4), (proj, D_MODEL, 5), (ya, D_MODEL, 0), (yb, D_MODEL, 0),
                        (yc, D_MODEL, 0)], [], [(D_MODEL, BF16)], name="merge_fwd", lp=lp)[0]


def merge_bwd(dmix, proj, ya, yb, yc, lp):
    def fn(row0, rv, pv):
        dm, g0, g1, g2, a, b, c = rv
        outs_y, outs_g = [], []
        for g, yv in ((g0, a), (g1, b), (g2, c)):
            sg = _sigmoid(g)
            outs_y.append(dm * sg)
            outs_g.append(dm * yv * sg * (1.0 - sg))
        return (tuple(outs_y) + tuple(outs_g), ())

    return rowwise(fn, [(dmix, D_MODEL, 0), (proj, D_MODEL, 3), (proj, D_MODEL, 4), (proj, D_MODEL, 5),
                        (ya, D_MODEL, 0), (yb, D_MODEL, 0), (yc, D_MODEL, 0)], [], [(D_MODEL, BF16)] * 6,
                   name="merge_bwd", lp=lp)


def loss_head(h, tgt, lp):
    def fn(row0, rv, pv):
        h_, t_ = rv
        live = (row0 + lax.broadcasted_iota(jnp.int32, h_.shape, 0)) >= X0
        diff = jnp.where(live, h_ - t_, 0.0)
        ssq = jnp.sum(jnp.sum(diff * diff, axis=1, keepdims=True), axis=0, keepdims=True)
        return ((diff * (1.0 / D_MODEL),), (ssq * (0.5 / D_MODEL),))

    return rowwise(fn, [(h, D_MODEL, 0), (tgt, D_MODEL, 0)], [], [(D_MODEL, F32)], [((1, 1), F32)], name="loss_head", lp=lp)


def _s5_consts(W):
    ab_re_rep, ab_im_rep, bb_re, bb_im = s5_prep(W['s5_a_re'], W['s5_a_im'], W['s5_log_dt'], W['s5_b_re'], W['s5_b_im'])
    pick = lambda t: t.reshape(S5_GROUPS, S5_GROUP, S5_STATE)[:, 0].reshape(1, S5_LANES)
    bb = jnp.stack([_blockdiag(bb_re.reshape(S5_GROUPS, S5_GROUP, S5_STATE)),
                    _blockdiag(bb_im.reshape(S5_GROUPS, S5_GROUP, S5_STATE))]).astype(BF16)
    return pick(ab_re_rep), pick(ab_im_rep), bb


def layer_fwd(h, hb, W, tabs, lp, ffn1=None, ffn2=True):
    cos, sin, rot = tabs
    h1, h1b, sv1 = ffn1 if ffn1 is not None else ffn_fwd(h, hb, W['wg1'], W['wu1'], W['wd1'], W['ln1_g'], W['ln1_b'], lp)
    proj = matmul(h1b[None], W['w_in'][None], tb=True, name="proj")[0]
    cqn, ckvn, kr = mla_prep(proj, cos, sin, rot, W['q_norm_g'], W['kv_norm_g'], lp)
    qn, qr, kn, v = mla_heads(cqn, ckvn, cos, sin, rot, W['wqn'], W['wqr'], W['wkn'], W['wv'], lp)
    o = attn_fwd(qn, qr, kn, v, kr, lp)
    ya = matmul(o, W['mla_wo'], ab='r', bb='r', name="mla_out")[0]
    vconv = conv_fwd(proj, W['conv_w'], W['conv_b'], lp)
    yb = matmul(vconv[None], W['conv_wout'][None], name="conv_out")[0]
    ub = s5_u(proj, lp)
    ab_re, ab_im, bb = _s5_consts(W)
    bu = bd_matmul(ub[None], bb, w_t=False, reduce=False, name="s5_bu")
    s = s5_scan(bu, ab_re, ab_im, lp)
    ys = bd_matmul(s, W['s5_ct'], w_t=False, reduce=True, name="s5_cs")[0]
    y, ygb = s5_y(ys, proj, W['s5_d'], lp)
    zg = matmul(ygb[None], W['s5_wglu'][None], name="s5_glu_mm")[0]
    glb = s5_glu(zg, y, W['s5_b_glu'], lp)
    yc = matmul(glb[None], W['s5_wout'][None], name="s5_out")[0]
    mixed = merge_fwd(proj, ya, yb, yc, lp)
    z2, h2, h2b = mm_res_ln(mixed[None], W['w_o'][None], h1, W['ln2_g'], W['ln2_b'], scale=1.0, name="wo_ln", lp=lp)
    sv = dict(sv1=sv1, h1b=h1b, proj=proj, cqn=cqn, ckvn=ckvn, kr=kr, qn=qn, qr=qr, kn=kn, v=v, o=o, ya=ya,
              vconv=vconv, yb=yb, ub=ub, ab_re=ab_re, ab_im=ab_im, bb=bb, s=s, y=y, ygb=ygb, zg=zg, glb=glb, yc=yc,
              mixed=mixed, z2=z2)
    if not ffn2:
        return h2, h2b, sv
    h3, h3b, sv['sv3'] = ffn_fwd(h2, h2b, W['wg2'], W['wu2'], W['wd2'], W['ln3_g'], W['ln3_b'], lp)
    return h3, h3b, sv


def layer_bwd(dh3, sv, W, tabs, lp, ffn1=True, ffn2=True):
    cos, sin, rot = tabs
    proj = sv['proj']
    G = {}
    dh2 = dh3
    if ffn2:
        dh2, g3 = ffn_bwd(dh3, sv['sv3'], W['wg2'], W['wu2'], W['wd2'], W['ln3_g'], lp)
        G.update(wg2=g3['wg'], wu2=g3['wu'], wd2=g3['wd'], ln3_g=g3['ln_g'], ln3_b=g3['ln_b'])
    dz2, dz2b, G['ln2_g'], G['ln2_b'] = ln_bwd(dh2, sv['z2'], W['ln2_g'], fscale=1.0, name="wo_ln_bwd", lp=lp)
    dmix = matmul(dz2b[None], W['w_o'][None], tb=True, name="wo_dx")[0]
    G['w_o'] = matmul(sv['mixed'][None], dz2b[None], ta=True, out_dtype=WGRAD, name="wo_dw")[0]
    dya, dyb, dyc, dg0, dg1, dg2 = merge_bwd(dmix, proj, sv['ya'], sv['yb'], sv['yc'], lp)
    dgl = matmul(dyc[None], W['s5_wout'][None], tb=True, name="s5_out_dx")[0]
    G['s5_wout'] = matmul(sv['glb'][None], dyc[None], ta=True, out_dtype=WGRAD, name="s5_out_dw")[0]
    t1, dzb, G['s5_b_glu'] = s5_glu_bwd(dgl, sv['zg'], sv['y'], W['s5_b_glu'], lp)
    dyg = matmul(dzb[None], W['s5_wglu'][None], tb=True, res=t1[None], name="s5_glu_dx")[0]
    G['s5_wglu'] = matmul(sv['ygb'][None], dzb[None], ta=True, out_dtype=WGRAD, name="s5_glu_dw")[0]
    dyb_, du_d, G['s5_d'] = s5_y_bwd(dyg, sv['y'], proj, W['s5_d'], lp)
    ds = bd_matmul(dyb_[None], W['s5_ct'], w_t=True, reduce=False, name="s5_cs_dx")
    G['s5_ct'] = bd_outer(sv['s'], dyb_[None], "s5_cs_dw")
    g_adj, d_ab = s5_scan_bwd(ds, sv['s'], sv['ab_re'], sv['ab_im'], lp)
    du = bd_matmul(g_adj, sv['bb'], w_t=True, reduce=True, res=du_d[None], name="s5_bu_dx")[0]
    d_bb = bd_outer(sv['ub'][None], g_adj, "s5_bu_dw")
    du_b = s5_du(du, lp)
    onehot = (jnp.arange(S5_GROUP) == 0).astype(F32)
    spread = lambda t: (t.reshape(S5_GROUPS, 1, S5_STATE) * onehot[None, :, None]).reshape(_S5_ROWS, S5_STATE)
    take = lambda t: _blockdiag_extract(t, S5_GROUP, S5_STATE).reshape(_S5_ROWS, S5_STATE)
    sel = jnp.kron(jnp.eye(S5_GROUPS, dtype=F32), jnp.ones((1, S5_GROUP), F32))
    (G['s5_a_re'], G['s5_a_im'], G['s5_log_dt'], G['s5_b_re'], G['s5_b_im']) = s5_prep_bwd(
        W['s5_a_re'], W['s5_a_im'], W['s5_log_dt'], W['s5_b_re'], W['s5_b_im'],
        spread(d_ab[0]), spread(d_ab[1]), take(d_bb[0]), take(d_bb[1]), sel)
    dv = matmul(dyb[None], W['conv_wout'][None], tb=True, name="conv_out_dx")[0]
    G['conv_wout'] = matmul(sv['vconv'][None], dyb[None], ta=True, out_dtype=WGRAD, name="conv_out_dw")[0]
    dxbar, dbg, dcg, G['conv_w'], G['conv_b'] = conv_bwd(dv, proj, W['conv_w'], W['conv_b'], lp)
    do = matmul(dya[None], W['mla_wo'], tb=True, bb='o', out_dtype=BF16, name="mla_out_dx")
    G['mla_wo'] = matmul(sv['o'], dya[None], ta=True, ab='o', out_dtype=WGRAD, name="mla_out_dw")
    dqn, dqr, dkn, dvv, dkr = attn_bwd(sv['qn'], sv['qr'], sv['kn'], sv['v'], sv['kr'], do, lp)
    dcq, dckv, dqrp = mla_heads_bwd(dqn, dqr, dkn, dvv, cos, sin, rot, W['wqn'], W['wqr'], W['wkn'], W['wv'], lp)
    G['wqn'] = matmul(sv['cqn'][None], dqn, ta=True, bb='o', out_dtype=WGRAD, name="mla_dwqn")
    G['wqr'] = matmul(sv['cqn'][None], dqrp, ta=True, bb='o', out_dtype=WGRAD, name="mla_dwqr")
    G['wkn'] = matmul(sv['ckvn'][None], dkn, ta=True, bb='o', out_dtype=WGRAD, name="mla_dwkn")
    G['wv'] = matmul(sv['ckvn'][None], dvv, ta=True, bb='o', out_dtype=WGRAD, name="mla_dwv")
    dcq_raw, dckv_raw, dkr_raw, G['q_norm_g'], G['kv_norm_g'] = mla_prep_bwd(
        dcq, dckv, dkr, proj, cos, sin, rot, W['q_norm_g'], W['kv_norm_g'], lp)
    zeros = lambda n: jnp.zeros((lp, n), BF16)
    dproj = jnp.concatenate([dcq_raw, dkr_raw, zeros(96), dckv_raw, zeros(256), dxbar, dbg, dcg, du_b, dg0, dg1, dg2], axis=1)
    dh1 = matmul(dproj[None], W['w_in'][None], res=dz2[None], res_scale=ALPHA, name="proj_dx")[0]
    G['w_in'] = matmul(dproj[None], sv['h1b'][None], ta=True, out_dtype=WGRAD, name="proj_dw")[0]
    if not ffn1:
        return dh1, G
    dh0, g1 = ffn_bwd(dh1, sv['sv1'], W['wg1'], W['wu1'], W['wd1'], W['ln1_g'], lp)
    G.update(wg1=g1['wg'], wu1=g1['wu'], wd1=g1['wd'], ln1_g=g1['ln_g'], ln1_b=g1['ln_b'])
    return dh0, G


def _nat_cols(st):
    return jnp.transpose(st, (1, 0, 2)).reshape(st.shape[1], -1)


def _shard_cols(nat):
    k, n = nat.shape
    return jnp.transpose(nat.reshape(k, N_SHARD, n // N_SHARD), (1, 0, 2))


def _win_pad(wt):
    z = lambda n: jnp.zeros((n, wt.shape[1]), wt.dtype)
    return jnp.concatenate([wt[0:384], wt[640:672], z(96), wt[384:640], z(256), wt[672:]], axis=0)


def _win_unpad(wp):
    return jnp.concatenate([wp[0:384], wp[512:768], wp[384:416], wp[1024:]], axis=0)


_BIG = [('ffn1_w_gate', 'T'), ('ffn1_w_up', 'T'), ('ffn1_w_down', 0), ('w_in', 'T'), ('mla_w_uq', 1), ('mla_w_ukv', 1),
        ('mla_w_o', 1), ('conv_w_out', 1), ('s5_w_glu', 0), ('s5_w_out', 1), ('w_o', 0),
        ('ffn2_w_gate', 'T'), ('ffn2_w_up', 'T'), ('ffn2_w_down', 0)]
_REPL = ['ln1_g', 'ln1_b', 'mla_q_norm_g', 'mla_kv_norm_g', 'conv_b', 's5_a_re', 's5_a_im', 's5_log_dt', 's5_b_re',
         's5_b_im', 's5_c_re', 's5_c_im', 's5_d', 's5_b_glu', 'ln2_g', 'ln2_b', 'ln3_g', 'ln3_b']


def compute_weights(st, small):
    W = {}
    for t in ('1', '2'):
        if 'ffn%s_w_gate' % t in st:
            W['wg' + t], W['wu' + t], W['wd' + t] = (st['ffn%s_w_%s' % (t, p)] for p in ('gate', 'up', 'down'))
    if 'w_in' in st:
        W.update(_mixer_weights(st))
    if small is not None:
        W.update(_small_weights(small))
    return W


def _mixer_weights(st):
    W = {}
    W['w_in'] = _win_pad(st['w_in'].reshape(D_IN, D_MODEL))
    uq = jnp.transpose(_nat_cols(st['mla_w_uq']).reshape(Q_RANK, N_HEADS, D_NOPE + D_ROPE), (1, 0, 2))
    W['wqn'], W['wqr'] = uq[:, :, :D_NOPE], uq[:, :, D_NOPE:]
    ukv = jnp.transpose(_nat_cols(st['mla_w_ukv']).reshape(KV_RANK, N_HEADS, D_NOPE + D_V), (1, 0, 2))
    W['wkn'], W['wv'] = ukv[:, :, :D_NOPE], ukv[:, :, D_NOPE:]
    W['mla_wo'] = _nat_cols(st['mla_w_o']).reshape(N_HEADS, D_V, D_MODEL)
    W['conv_wout'] = _nat_cols(st['conv_w_out'])
    W['s5_wglu'] = st['s5_w_glu'].reshape(MIX, MIX)
    W['s5_wout'] = _nat_cols(st['s5_w_out'])
    W['w_o'] = st['w_o'].reshape(D_MODEL, D_MODEL)
    return W


def _small_weights(small):
    W = {}
    W['conv_w'] = small['conv_w']
    for n in ('ln1_g', 'ln1_b', 'ln2_g', 'ln2_b', 'ln3_g', 'ln3_b', 'conv_b', 's5_b_glu'):
        W[n] = small[n].reshape(1, -1)
    W['q_norm_g'] = small['mla_q_norm_g'].reshape(1, -1)
    W['kv_norm_g'] = small['mla_kv_norm_g'].reshape(1, -1)
    W['s5_d'] = small['s5_d'].reshape(1, MIX)
    rep = lambda t: jnp.repeat(t, S5_GROUP, axis=0)
    W['s5_a_re'], W['s5_a_im'] = rep(small['s5_a_re']), rep(small['s5_a_im'])
    W['s5_log_dt'] = jnp.broadcast_to(rep(small['s5_log_dt'].reshape(S5_GROUPS, 1)), (_S5_ROWS, S5_STATE))
    tr = lambda t: jnp.transpose(t, (0, 2, 1)).reshape(_S5_ROWS, S5_STATE)
    W['s5_b_re'], W['s5_b_im'] = tr(small['s5_b_re']), tr(small['s5_b_im'])
    ct = lambda t: _blockdiag(jnp.transpose(t, (0, 2, 1)))
    W['s5_ct'] = jnp.stack([ct(small['s5_c_re']), -ct(small['s5_c_im'])]).astype(BF16)
    return W


def reference_grads(G, ffn=True):
    R = {}
    for t in ('1', '2') if ffn else ():
        R['ffn%s_w_gate' % t] = G['wg' + t].reshape(D_FF, D_MODEL).T
        R['ffn%s_w_up' % t] = G['wu' + t].reshape(D_FF, D_MODEL).T
        R['ffn%s_w_down' % t] = G['wd' + t].reshape(D_FF, D_MODEL)
    R['w_in_t'] = _win_unpad(G['w_in'])
    if ffn:
        R['w_in'] = R['w_in_t'].T
    R['mla_w_uq'] = jnp.transpose(jnp.concatenate([G['wqn'], G['wqr']], axis=2), (1, 0, 2)).reshape(Q_RANK, -1)
    R['mla_w_ukv'] = jnp.transpose(jnp.concatenate([G['wkn'], G['wv']], axis=2), (1, 0, 2)).reshape(KV_RANK, -1)
    R['mla_w_o'] = G['mla_wo'].reshape(N_HEADS * D_V, D_MODEL)
    R['conv_w'], R['conv_w_out'] = G['conv_w'], G['conv_wout']
    R['s5_w_glu'], R['s5_w_out'], R['w_o'] = G['s5_wglu'], G['s5_wout'], G['w_o']
    for n in ('ln1_g', 'ln1_b', 'ln2_g', 'ln2_b', 'ln3_g', 'ln3_b', 'conv_b', 's5_b_glu'):
        if n in G:
            R[n] = G[n].reshape(-1)
    R['mla_q_norm_g'], R['mla_kv_norm_g'] = G['q_norm_g'].reshape(-1), G['kv_norm_g'].reshape(-1)
    R['s5_d'] = G['s5_d'].reshape(S5_GROUPS, S5_GROUP)
    R['s5_a_re'], R['s5_a_im'], R['s5_log_dt'] = G['s5_a_re'], G['s5_a_im'], G['s5_log_dt'].reshape(-1)
    untr = lambda t: jnp.transpose(t.reshape(S5_GROUPS, S5_GROUP, S5_STATE), (0, 2, 1))
    R['s5_b_re'], R['s5_b_im'] = untr(G['s5_b_re']), untr(G['s5_b_im'])
    unct = lambda t: jnp.transpose(_blockdiag_extract(t, S5_STATE, S5_GROUP), (0, 2, 1))
    R['s5_c_re'], R['s5_c_im'] = unct(G['s5_ct'][0]), -unct(G['s5_ct'][1])
    return R


_ANY = pl.BlockSpec(memory_space=pl.ANY)
LANES = 1024


def _place():
    x, y, c = lax.axis_index("x"), lax.axis_index("y"), lax.axis_index("c")
    chips = [(1 - x, y), (x, 1 - y), (1 - x, 1 - y)]
    return x, y, c, chips


def _rows_of(c, half):
    return pl.ds(pl.multiple_of(c * half, 8), half)


def all_gather_shards(srcs, exact):
    n, m = len(srcs), len(exact)
    halves = [s.shape[0] // 2 for s in srcs]

    def body(*refs):
        s_refs, e_refs = refs[:n], refs[n:n + m]
        o_refs, eo_refs = refs[n + m:2 * n + m], refs[2 * n + m:2 * n + 2 * m]
        send, recv, esend, erecv, osend, orecv, lsem = refs[2 * n + 2 * m:]
        x, y, c, chips = _place()
        me = 2 * x + y
        sibling = (x, y, 1 - c)
        own = [pltpu.make_async_remote_copy(src_ref=s_refs[k], dst_ref=o_refs[k].at[me], send_sem=osend.at[k],
                                            recv_sem=orecv.at[k], device_id=sibling, device_id_type=MESH) for k in range(n)]
        local = [pltpu.make_async_copy(e_refs[k], eo_refs[k].at[me], lsem.at[k]) for k in range(m)]
        for cp in own + local:
            cp.start()

        def copy(k, s, src, idx, half_c, to):
            return pltpu.make_async_remote_copy(
                src_ref=src, dst_ref=o_refs[k].at[idx, _rows_of(half_c, halves[k])], send_sem=send.at[6 * k + s],
                recv_sem=recv.at[6 * k + s], device_id=to, device_id_type=MESH)

        def ecopy(k, j, idx, to):
            return pltpu.make_async_remote_copy(src_ref=e_refs[k], dst_ref=eo_refs[k].at[idx], send_sem=esend.at[3 * k + j],
                                                recv_sem=erecv.at[3 * k + j], device_id=to, device_id_type=MESH)

        sends = []
        for k in range(n):
            mine = s_refs[k].at[_rows_of(c, halves[k])]
            sends += [copy(k, j, mine, me, c, (*chip, c)) for j, chip in enumerate(chips)]
        for k in range(m):
            sends += [ecopy(k, j, me, (*chip, c)) for j, chip in enumerate(chips)]
        for cp in sends:
            cp.start()
        for j, chip in enumerate(chips):
            idx = 2 * chip[0] + chip[1]
            for k in range(n):
                landed = o_refs[k].at[idx, _rows_of(c, halves[k])]
                copy(k, j, landed, idx, c, sibling).wait_recv()
                fwd = copy(k, 3 + j, landed, idx, c, sibling)
                fwd.start()
                sends.append(fwd)
        for j, chip in enumerate(chips):
            idx = 2 * chip[0] + chip[1]
            for k in range(n):
                copy(k, 3 + j, s_refs[k].at[_rows_of(c, halves[k])], idx, 1 - c, sibling).wait_recv()
            for k in range(m):
                ecopy(k, j, idx, sibling).wait_recv()
        for cp in sends:
            cp.wait_send()
        for cp in own + local:
            cp.wait()

    outs = pl.pallas_call(
        body, name="all_gather_weights", in_specs=[_ANY] * (n + m), out_specs=[_ANY] * (n + m),
        out_shape=[jax.ShapeDtypeStruct((N_SHARD,) + a.shape, a.dtype) for a in list(srcs) + list(exact)],
        scratch_shapes=[pltpu.SemaphoreType.DMA((6 * n,)), pltpu.SemaphoreType.DMA((6 * n,)),
                        pltpu.SemaphoreType.DMA((3 * m,)), pltpu.SemaphoreType.DMA((3 * m,)),
                        pltpu.SemaphoreType.DMA((n,)), pltpu.SemaphoreType.DMA((n,)), pltpu.SemaphoreType.DMA((m,))],
    )(*srcs, *exact)
    return outs[:n], outs[n:]


def rs_pair_swap(gs):
    n = len(gs)

    def body(*refs):
        g_refs, r_refs, send, recv = refs[:n], refs[n:2 * n], refs[2 * n], refs[2 * n + 1]
        x, y, c, _ = _place()
        copies = [pltpu.make_async_remote_copy(
            src_ref=g_refs[k].at[pl.ds(0, N_SHARD), _rows_of(1 - c, gs[k].shape[1] // 2)], dst_ref=r_refs[k],
            send_sem=send.at[k], recv_sem=recv.at[k], device_id=(x, y, 1 - c), device_id_type=MESH) for k in range(n)]
        for cp in copies:
            cp.start()
        for cp in copies:
            cp.wait()

    return pl.pallas_call(
        body, name="grad_pair_swap", in_specs=[_ANY] * n, out_specs=[_ANY] * n,
        out_shape=[jax.ShapeDtypeStruct((N_SHARD, g.shape[1] // 2, g.shape[2]), g.dtype) for g in gs],
        scratch_shapes=[pltpu.SemaphoreType.DMA((n,)), pltpu.SemaphoreType.DMA((n,))],
    )(*gs)


def _group_tile(half, n_cols, n_arrays):
    budget = (20 * 2 ** 20) // (6 * n_arrays)
    fits = [t for t in range(8, half + 1, 8) if half % t == 0 and t * n_cols * 4 <= budget]
    return max(fits) if fits else 8


def rs_pair_add(gs, rs, cidx, out_dtype, name):
    n = len(gs)
    _, K, cols = gs[0].shape
    half = K // 2
    tr = _group_tile(half, cols, n)
    nb = half // tr

    def body(c_ref, *refs):
        for g_ref, r_ref, o_ref in zip(refs[:n], refs[n:2 * n], refs[2 * n:]):
            o_ref[...] = (g_ref[...].astype(F32) + r_ref[...].astype(F32)).astype(out_dtype)

    gspec = pl.BlockSpec((None, tr, cols), lambda j, i, c: (j, c[0] * nb + i, 0))
    rspec = pl.BlockSpec((None, tr, cols), lambda j, i, c: (j, i, 0))
    return pl.pallas_call(
        body, name=name,
        grid_spec=pltpu.PrefetchScalarGridSpec(num_scalar_prefetch=1, grid=(N_SHARD, nb), in_specs=[gspec] * n + [rspec] * n,
                                               out_specs=[rspec] * n),
        out_shape=[jax.ShapeDtypeStruct((N_SHARD, half, cols), out_dtype)] * n,
        compiler_params=_cparams(),
    )(cidx, *gs, *rs)


def rs_chip_sum(qs, nl, cidx, name):
    n = len(qs)
    _, half, cols = qs[0].shape
    tr = _group_tile(half, cols, n)
    nb = half // tr

    def body(c_ref, *refs):
        for k, q_ref in enumerate(refs[:n]):
            o_ref = refs[n + k // nl]
            o_ref[k % nl] = ((q_ref[0].astype(F32) + q_ref[1].astype(F32)) + q_ref[2].astype(F32)) + q_ref[3].astype(F32)

    return pl.pallas_call(
        body, name=name,
        grid_spec=pltpu.PrefetchScalarGridSpec(
            num_scalar_prefetch=1, grid=(nb,),
            in_specs=[pl.BlockSpec((N_SHARD, tr, cols), lambda i, c: (0, i, 0))] * n,
            out_specs=[pl.BlockSpec((nl, tr, cols), lambda i, c: (0, c[0] * nb + i, 0))] * (n // nl)),
        out_shape=[jax.ShapeDtypeStruct((nl, 2 * half, cols), F32)] * (n // nl),
        compiler_params=_cparams(),
    )(cidx, *qs)


def rs_pair_gather(fs, name):
    n = len(fs)

    def body(*refs):
        f_refs, send, recv = refs[n:2 * n], refs[2 * n], refs[2 * n + 1]
        x, y, c, _ = _place()
        copies = []
        for k in range(n):
            rows = f_refs[k].at[pl.ds(0, fs[k].shape[0]), _rows_of(c, fs[k].shape[1] // 2)]
            copies.append(pltpu.make_async_remote_copy(src_ref=rows, dst_ref=rows, send_sem=send.at[k], recv_sem=recv.at[k],
                                                       device_id=(x, y, 1 - c), device_id_type=MESH))
        for cp in copies:
            cp.start()
        for cp in copies:
            cp.wait()

    return pl.pallas_call(
        body, name=name, in_specs=[_ANY] * n, out_specs=[_ANY] * n,
        out_shape=[jax.ShapeDtypeStruct(f.shape, f.dtype) for f in fs],
        input_output_aliases={k: k for k in range(n)},
        scratch_shapes=[pltpu.SemaphoreType.DMA((n,)), pltpu.SemaphoreType.DMA((n,))],
    )(*fs)


_HBM = pl.BlockSpec(memory_space=pltpu.HBM)
_SEM = pl.BlockSpec(memory_space=pltpu.SEMAPHORE)
_EFFECT = pltpu.SideEffectType.DATAFLOW_SIDE_EFFECTING


def _in_hbm(a):
    return pltpu.with_memory_space_constraint(a, pltpu.HBM)


def split_start(name, srcs, lands, after, copies_fn, n_copies):
    n = len(srcs)

    def body(*refs):
        for cp in copies_fn(refs[:n], refs[n:2 * n], refs[2 * n + 1], refs[2 * n + 2]):
            cp.start()
        refs[-1][...] = jnp.zeros_like(refs[-1])

    bufs = list(srcs) + list(lands)
    outs = pl.pallas_call(
        body, name=name,
        out_shape=(pltpu.SemaphoreType.DMA((n_copies,)), pltpu.SemaphoreType.DMA((n_copies,)),
                   *[pltpu.HBM(a.shape, a.dtype) for a in bufs], jax.ShapeDtypeStruct((8, 128), F32)),
        in_specs=[_HBM] * (2 * n) + [_ANY],
        out_specs=(_SEM, _SEM, *[_HBM] * (2 * n), pl.BlockSpec(memory_space=pltpu.VMEM)),
        input_output_aliases={i: 2 + i for i in range(2 * n)},
        compiler_params=pltpu.CompilerParams(has_side_effects=_EFFECT),
    )(*[_in_hbm(a) for a in bufs], after)
    return outs[0], outs[1], outs[2:2 + n], outs[2 + n:2 + 2 * n], outs[-1]


def split_wait(name, send, recv, srcs, lands, after, copies_fn, which=None):
    n = len(srcs)

    def body(*refs):
        copies = copies_fn(refs[:n], refs[n:2 * n], refs[2 * n], refs[2 * n + 1], which)
        for cp in copies:
            cp.wait_send()
        for cp in copies:
            cp.wait_recv()

    bufs = list(srcs) + list(lands)
    outs = pl.pallas_call(
        body, name=name, out_shape=tuple(pltpu.HBM(a.shape, a.dtype) for a in bufs),
        in_specs=[_HBM] * (2 * n) + [_SEM, _SEM, _ANY], out_specs=tuple([_HBM] * (2 * n)),
        input_output_aliases={i: i for i in range(2 * n)},
        compiler_params=pltpu.CompilerParams(has_side_effects=_EFFECT),
    )(*bufs, send, recv, after)
    return list(outs[:n]), list(outs[n:])


def _gather_copies(s_refs, l_refs, send, recv, which=None):
    x, y, c, chips = _place()
    me = 2 * x + y
    out = []
    for k in (range(len(s_refs)) if which is None else which):
        s, l = s_refs[k], l_refs[k]
        rows = _rows_of(c, s.shape[0] // 2)
        for j, chip in enumerate(chips):
            out.append(pltpu.make_async_remote_copy(src_ref=s.at[rows], dst_ref=l.at[me, rows], send_sem=send.at[4 * k + j],
                                                    recv_sem=recv.at[4 * k + j], device_id=(*chip, c), device_id_type=MESH))
        out.append(pltpu.make_async_remote_copy(src_ref=s, dst_ref=l.at[me], send_sem=send.at[4 * k + 3],
                                                recv_sem=recv.at[4 * k + 3], device_id=(x, y, 1 - c), device_id_type=MESH))
    return out


def _scatter_copies(s_refs, l_refs, send, recv, which=None):
    x, y, c, chips = _place()
    me = 2 * x + y
    return [pltpu.make_async_remote_copy(src_ref=s_refs[k].at[2 * chip[0] + chip[1]], dst_ref=l_refs[k].at[me],
                                         send_sem=send.at[3 * k + j], recv_sem=recv.at[3 * k + j], device_id=(*chip, c),
                                         device_id_type=MESH)
            for k in (range(len(s_refs)) if which is None else which) for j, chip in enumerate(chips)]


def gather_forward(lands, name):
    n = len(lands)

    def body(*refs):
        l_refs, send, recv = refs[n:2 * n], refs[2 * n], refs[2 * n + 1]
        x, y, c, chips = _place()
        copies = []
        for k in range(n):
            rows = _rows_of(c, lands[k].shape[1] // 2)
            for j, chip in enumerate(chips):
                part = l_refs[k].at[2 * chip[0] + chip[1], rows]
                copies.append(pltpu.make_async_remote_copy(src_ref=part, dst_ref=part, send_sem=send.at[3 * k + j],
                                                           recv_sem=recv.at[3 * k + j], device_id=(x, y, 1 - c),
                                                           device_id_type=MESH))
        for cp in copies:
            cp.start()
        for cp in copies:
            cp.wait()

    return pl.pallas_call(
        body, name=name, in_specs=[_ANY] * n, out_specs=[_ANY] * n,
        out_shape=[jax.ShapeDtypeStruct(a.shape, a.dtype) for a in lands],
        input_output_aliases={k: k for k in range(n)},
        scratch_shapes=[pltpu.SemaphoreType.DMA((3 * n,)), pltpu.SemaphoreType.DMA((3 * n,))],
    )(*lands)


def rs_partials(gs, wire, cidx, tag):
    rs = rs_pair_swap(gs)
    groups = {}
    for k, g in enumerate(gs):
        groups.setdefault((g.shape, jnp.dtype(wire[k]).name), []).append(k)
    ps = [None] * len(gs)
    for gi, ks in enumerate(groups.values()):
        outs = rs_pair_add([gs[k] for k in ks], [rs[k] for k in ks], cidx, wire[ks[0]], "grad_pair_add_%s%d" % (tag, gi))
        for k, o in zip(ks, outs):
            ps[k] = o
    return ps


def rs_finish(items, tag):
    cidx = lax.axis_index("c").astype(jnp.int32).reshape(1)
    groups = {}
    for i, it in enumerate(items):
        groups.setdefault((it[0].shape, len(it), it[0].dtype.name), []).append(i)
    fs = [None] * len(items)
    for gi, ids in enumerate(groups.values()):
        outs = rs_chip_sum([q for i in ids for q in items[i]], len(items[ids[0]]), cidx, "grad_chip_sum_%s%d" % (tag, gi))
        for i, o in zip(ids, outs):
            fs[i] = o
    return rs_pair_gather(fs, "grad_pair_gather_" + tag)


def adamw(w, g, m, v, name):
    shape = w.shape
    if w.ndim == 2:
        block, grid, index = shape, (1,), (lambda i: (0, 0))
    else:
        slab = shape[2:]
        unit = 4 * int(np.prod(slab[:-2] or (1,))) * (-(-slab[-1] // 128) * 128)
        if len(slab) >= 2:
            unit *= -(-slab[-2] // 8) * 8
        k = shape[1]
        tr = k
        if k * unit > 2 ** 21:
            tr = max(t for t in range(8, k, 8) if k % t == 0 and t * unit <= 2 ** 21)
        block, grid = (None, tr) + tuple(slab), (shape[0], k // tr)
        index = lambda l, i: (l, i) + (0,) * len(slab)
        if tr < min(k, 64) and len(slab) == 1:
            tc = max(t for t in range(128, slab[0] + 1, 128) if slab[0] % t == 0 and k * t * 4 <= 2 ** 21)
            block, grid = (None, k, tc), (shape[0], slab[0] // tc)
            index = lambda l, i: (l, 0, i)

    def body(w_ref, g_ref, m_ref, v_ref, d_ref, nm_ref, nv_ref):
        g_ = g_ref[...]
        m_new = ADAM_B1 * m_ref[...] + (1.0 - ADAM_B1) * g_
        v_new = ADAM_B2 * v_ref[...] + (1.0 - ADAM_B2) * (g_ * g_)
        m_hat = m_new / (1.0 - ADAM_B1 ** ADAM_STEP)
        v_hat = v_new / (1.0 - ADAM_B2 ** ADAM_STEP)
        d_ref[...] = -ADAM_LR * (m_hat / (jnp.sqrt(v_hat) + ADAM_EPS) + ADAM_WD * w_ref[...])
        nm_ref[...] = m_new
        nv_ref[...] = v_new

    spec = pl.BlockSpec(block, index)
    return pl.pallas_call(
        body, name=name, grid=grid, in_specs=[spec] * 4, out_specs=[spec] * 3,
        out_shape=[jax.ShapeDtypeStruct(shape, F32)] * 3, compiler_params=_cparams(),
    )(w, g, m, v)


_WEIGHTS = ['meta', 'ffn1_w_gate', 'ffn1_w_up', 'ffn1_w_down', 'ln1_g', 'ln1_b', 'w_in', 'mla_q_norm_g', 'mla_w_uq',
            'mla_kv_norm_g', 'mla_w_ukv', 'mla_w_o', 'conv_w', 'conv_b', 'conv_w_out', 's5_a_re', 's5_a_im', 's5_log_dt',
            's5_b_re', 's5_b_im', 's5_c_re', 's5_c_im', 's5_d', 's5_w_glu', 's5_b_glu', 's5_w_out', 'w_o', 'ln2_g', 'ln2_b',
            'ffn2_w_gate', 'ffn2_w_up', 'ffn2_w_down', 'ln3_g', 'ln3_b']


def _pad_to(flat, n):
    return jnp.concatenate([flat, jnp.zeros((n - flat.shape[0],), flat.dtype)])


def _shard_of(full, axis):
    if axis == 1:
        return _shard_cols(full)
    if axis == 'T':
        return full.T.reshape(N_SHARD, full.shape[1] // N_SHARD, full.shape[0])
    return full.reshape(N_SHARD, full.shape[0] // N_SHARD, full.shape[1])


_FFN_KEY = {'gate': 'wg', 'up': 'wu', 'down': 'wd'}


def _pad_rows(a, axis):
    k = a.shape[axis]
    extra = -k % 32
    if not extra:
        return a
    return jnp.pad(a, [(0, extra) if d == axis else (0, 0) for d in range(a.ndim)])


def _step(env):
    w = {n: env[n] for n in _WEIGHTS}
    mom = {n: env['m_' + n] for n in _WEIGHTS}
    var = {n: env['v_' + n] for n in _WEIGHTS}
    cidx = lax.axis_index("c").astype(jnp.int32).reshape(1)
    chip = 2 * lax.axis_index("x") + lax.axis_index("y")
    big_names = [n for n, _ in _BIG]
    nb = len(big_names)

    kept_t = [n for n, a in _BIG if a == 'T']
    own = {n: (jnp.swapaxes(w[n], 1, 2) if n in kept_t else w[n]) for n in big_names}
    first = [n for n in big_names if n.startswith('ffn1')]
    mix = [n for n in big_names if not n.startswith('ffn')]
    last = [n for n in big_names if n.startswith('ffn2')]
    rest = mix + last
    nm, nr = len(mix), len(mix) + len(last)
    src = lambda n, li: _pad_rows(own[n][li].astype(BF16), 0)
    gathered_first, (conv_w_st, meta_st) = all_gather_shards([src(n, 0) for n in first], [w['conv_w'], w['meta']])
    later = [src(n, 0) for n in rest] + [src(n, 1) for n in big_names]
    lands = [lax.empty((N_SHARD,) + s.shape, BF16) for s in later]
    g_send, g_recv, later_t, lands_t, token = split_start("gather_start", later, lands, gathered_first[0], _gather_copies,
                                                          4 * len(later))

    def weights_of(names, st, li, with_small):
        small = None
        if with_small:
            small = {n: w[n][li] for n in _REPL}
            small['conv_w'] = _nat_cols(conv_w_st[:, li])
        return compute_weights({n: a[:, :own[n].shape[1]] for n, a in zip(names, st)}, small)

    x2d = env['x'][0]
    lp = x2d.shape[0] + X0
    tabs = _rope_tables(lp)
    h = jnp.concatenate([jnp.zeros((PAD, D_MODEL), F32), _nat_cols(meta_st), x2d], axis=0) + token[0, 0]
    W0 = weights_of(first, gathered_first, 0, True)
    ffn1 = ffn_fwd(h, h.astype(BF16), W0['wg1'], W0['wu1'], W0['wd1'], W0['ln1_g'], W0['ln1_b'], lp)
    later_t, lands_t = split_wait("gather0_wait", g_send, g_recv, later_t, lands_t, ffn1[0], _gather_copies, range(nm))
    W0.update(weights_of(mix, gather_forward(lands_t[:nm], "gather0_forward"), 0, False))
    h, hb, sv0 = layer_fwd(None, None, W0, tabs, lp, ffn1=ffn1, ffn2=False)
    later_t, lands_t = split_wait("gather0b_wait", g_send, g_recv, later_t, lands_t, h, _gather_copies, range(nm, nr))
    W0.update(weights_of(last, gather_forward(lands_t[nm:nr], "gather0b_forward"), 0, False))
    h, hb, sv0['sv3'] = ffn_fwd(h, hb, W0['wg2'], W0['wu2'], W0['wd2'], W0['ln3_g'], W0['ln3_b'], lp)
    _, lands_t = split_wait("gather1_wait", g_send, g_recv, later_t, lands_t, h, _gather_copies, range(nr, len(later)))
    W1 = weights_of(big_names, gather_forward(lands_t[nr:], "gather1_forward"), 1, True)
    h, hb, sv1 = layer_fwd(h, hb, W1, tabs, lp)
    tgt = jnp.concatenate([jnp.zeros((X0, D_MODEL), F32), env['loss_target'][0]], axis=0)
    dh, loss_part = loss_head(h, tgt, lp)
    loss = lax.psum(loss_part[0, 0], ("x", "y", "c"))

    def shards(G, names):
        full = None if all(n.startswith('ffn') for n in names) else reference_grads(G, ffn=False)

        def one(n, a):
            if n.startswith('ffn'):
                return G[_FFN_KEY[n.split('_')[-1]] + n[3]]
            if n == 'w_in':
                return full['w_in_t'].reshape(N_SHARD, D_IN // N_SHARD, D_MODEL)
            return _shard_of(full[n], a)

        return [_pad_rows(one(n, a), 1) for n, a in _BIG if n in names]

    def scatter_start(name, ps, after):
        qs = [lax.dynamic_update_slice_in_dim(jnp.zeros_like(p), lax.dynamic_slice_in_dim(p, chip, 1, axis=0), chip, axis=0)
              for p in ps]
        return split_start(name, ps, qs, after, _scatter_copies, 3 * len(ps))

    dh, G1 = layer_bwd(dh, sv1, W1, tabs, lp)
    p1 = rs_partials(shards(G1, big_names), [BF16] * nb, cidx, "b")
    s1_send, s1_recv, p1_t, q1_t, token1 = scatter_start("scatter1_start", p1, dh)
    dh, g3 = ffn_bwd(dh, sv0['sv3'], W0['wg2'], W0['wu2'], W0['wd2'], W0['ln3_g'] + token1[0, 0], lp)
    G0 = dict(wg2=g3['wg'], wu2=g3['wu'], wd2=g3['wd'])
    p0l = rs_partials(shards(G0, last), [BF16] * len(last), cidx, "c")
    sl_send, sl_recv, p0l_t, q0l_t, token0l = scatter_start("scatter0b_start", p0l, dh)
    dh, Gm = layer_bwd(dh, sv0, dict(W0, ln2_g=W0['ln2_g'] + token0l[0, 0]), tabs, lp, ffn1=False, ffn2=False)
    G0.update(Gm, ln3_g=g3['ln_g'], ln3_b=g3['ln_b'])
    p0m = rs_partials(shards(G0, mix), [BF16] * nm, cidx, "d")
    sm_send, sm_recv, p0m_t, q0m_t, token0m = scatter_start("scatter0_start", p0m, dh)
    dh, g1 = ffn_bwd(dh, sv0['sv1'], W0['wg1'], W0['wu1'], W0['wd1'], W0['ln1_g'] + token0m[0, 0], lp)
    G0.update(wg1=g1['wg'], wu1=g1['wu'], wd1=g1['wd'], ln1_g=g1['ln_g'], ln1_b=g1['ln_b'])
    _, q1 = split_wait("scatter1_wait", s1_send, s1_recv, p1_t, q1_t, dh, _scatter_copies)
    _, q0_last = split_wait("scatter0b_wait", sl_send, sl_recv, p0l_t, q0l_t, dh, _scatter_copies)
    _, q0_mix = split_wait("scatter0_wait", sm_send, sm_recv, p0m_t, q0m_t, dh, _scatter_copies)
    q0_rest = list(q0_mix) + list(q0_last)
    full = [reference_grads(G0, ffn=False), reference_grads(G1, ffn=False)]

    s_parts = [jnp.stack([full[li][n] for li in range(DEPTH)]).reshape(-1) for n in _REPL + ['conv_w']]
    s_parts.append(dh[PAD:X0].reshape(-1))
    s_sizes = [int(p.shape[0]) for p in s_parts]
    s_rows = -(-sum(s_sizes) // (16 * LANES)) * 16
    g_small = _pad_to(jnp.concatenate(s_parts), s_rows * LANES).reshape(1, s_rows, LANES)
    g_small = jnp.broadcast_to(g_small, (N_SHARD, s_rows, LANES))
    p_last = rs_partials(shards(G0, first) + [g_small], [BF16] * len(first) + [F32], cidx, "a")
    z_send, z_recv, pz_t, qz_t, token_z = scatter_start("scatter_last_start", p_last, dh)

    def step_weights(names, grad):
        out = {}
        for n in names:
            if n in kept_t:
                res = adamw(own[n], grad[n], jnp.swapaxes(mom[n], 1, 2), jnp.swapaxes(var[n], 1, 2), "adamw_" + n)
                out[n] = [jnp.swapaxes(t, 1, 2) for t in [grad[n]] + list(res)]
            else:
                out[n] = [grad[n]] + list(adamw(w[n], grad[n], mom[n], var[n], "adamw_" + n))
        return out

    q1 = dict(zip(big_names, q1))
    q0 = dict(zip(rest, q0_rest))
    q0[rest[0]] = q0[rest[0]] + token_z[0, 0].astype(BF16)
    red = rs_finish([[q0[n], q1[n]] for n in rest], "a")
    done = step_weights(rest, {n: r[:, :own[n].shape[1]] for n, r in zip(rest, red)})
    all_done = jnp.stack([done[n][3][(0,) * done[n][3].ndim] for n in rest])
    _, q_last = split_wait("scatter_last_wait", z_send, z_recv, pz_t, qz_t, all_done, _scatter_copies)
    red = rs_finish([[q, q1[n]] for n, q in zip(first, q_last)] + [[q_last[-1]]], "b")
    f_small = red[-1].reshape(-1)

    grad = {n: r[:, :own[n].shape[1]] for n, r in zip(first, red)}
    off = 0
    for n, sz in zip(_REPL + ['conv_w', 'meta'], s_sizes):
        grad[n] = f_small[off:off + sz]
        off += sz
    for n in _REPL:
        grad[n] = grad[n].reshape(w[n].shape)
    cw = grad['conv_w'].reshape(DEPTH, 3, MIX)
    grad['conv_w'] = lax.dynamic_slice_in_dim(cw, chip * (MIX // N_SHARD), MIX // N_SHARD, axis=2)
    gm = grad['meta'].reshape(N_META, D_MODEL)
    grad['meta'] = lax.dynamic_slice_in_dim(gm, chip * (D_MODEL // N_SHARD), D_MODEL // N_SHARD, axis=1)

    done.update(step_weights([n for n in _WEIGHTS if n not in done], grad))
    return (loss, dh[X0:][None], *[done[n][k] for k in range(4) for n in _WEIGHTS])


def kernel(x, meta, ffn1_w_gate, ffn1_w_up, ffn1_w_down, ln1_g, ln1_b, w_in, mla_q_norm_g, mla_w_uq, mla_kv_norm_g, mla_w_ukv, mla_w_o, conv_w, conv_b, conv_w_out, s5_a_re, s5_a_im, s5_log_dt, s5_b_re, s5_b_im, s5_c_re, s5_c_im, s5_d, s5_w_glu, s5_b_glu, s5_w_out, w_o, ln2_g, ln2_b, ffn2_w_gate, ffn2_w_up, ffn2_w_down, ln3_g, ln3_b, loss_target, m_meta, m_ffn1_w_gate, m_ffn1_w_up, m_ffn1_w_down, m_ln1_g, m_ln1_b, m_w_in, m_mla_q_norm_g, m_mla_w_uq, m_mla_kv_norm_g, m_mla_w_ukv, m_mla_w_o, m_conv_w, m_conv_b, m_conv_w_out, m_s5_a_re, m_s5_a_im, m_s5_log_dt, m_s5_b_re, m_s5_b_im, m_s5_c_re, m_s5_c_im, m_s5_d, m_s5_w_glu, m_s5_b_glu, m_s5_w_out, m_w_o, m_ln2_g, m_ln2_b, m_ffn2_w_gate, m_ffn2_w_up, m_ffn2_w_down, m_ln3_g, m_ln3_b, v_meta, v_ffn1_w_gate, v_ffn1_w_up, v_ffn1_w_down, v_ln1_g, v_ln1_b, v_w_in, v_mla_q_norm_g, v_mla_w_uq, v_mla_kv_norm_g, v_mla_w_ukv, v_mla_w_o, v_conv_w, v_conv_b, v_conv_w_out, v_s5_a_re, v_s5_a_im, v_s5_log_dt, v_s5_b_re, v_s5_b_im, v_s5_c_re, v_s5_c_im, v_s5_d, v_s5_w_glu, v_s5_b_glu, v_s5_w_out, v_w_o, v_ln2_g, v_ln2_b, v_ffn2_w_gate, v_ffn2_w_up, v_ffn2_w_down, v_ln3_g, v_ln3_b):
    return _step(dict(locals()))
```

```python
import functools
import math

import numpy as np
import jax
import jax.numpy as jnp
from jax import lax
from jax.experimental import pallas as pl
from jax.experimental.pallas import tpu as pltpu

F32 = jnp.float32
BF16 = jnp.bfloat16

D_MODEL = 1024
DEPTH = 2
N_META = 16
PAD = 112
X0 = PAD + N_META
N_HEADS = 8
D_NOPE = 64
D_ROPE = 32
D_V = 64
Q_RANK = 384
KV_RANK = 256
MIX = 512
S5_GROUPS = 32
S5_GROUP = 16
S5_STATE = 64
S5_LANES = S5_GROUPS * S5_STATE
D_FF = 2816
N_SHARD = 4
FF_SHARD = D_FF // N_SHARD
D_IN = 5792
P_IN = 6144
ALPHA = (2.0 * DEPTH) ** 0.25
LN_EPS = 1e-5
RMS_EPS = 1e-6
ATT_SCALE = (D_NOPE + D_ROPE) ** -0.5
ROPE_BASE = 10000.0
ADAM_LR, ADAM_B1, ADAM_B2, ADAM_EPS, ADAM_WD, ADAM_STEP = 0.001, 0.9, 0.999, 1e-08, 0.01, 10
SCAN_CHUNK = 128
VMEM_LIMIT = 52 * 2 ** 20
WGRAD = BF16
MESH = pl.DeviceIdType.MESH


def _cparams(**kw):
    return pltpu.CompilerParams(vmem_limit_bytes=VMEM_LIMIT, **kw)


def _tile(n):
    if n <= 1088:
        return n
    for t in (1024, 544, 512, 272, 256, 128):
        if n % t == 0:
            return t
    return n


def _row_tile(lp):
    for t in (544, 272, 128):
        if lp % t == 0:
            return t
    return lp


def _sigmoid(x):
    return 1.0 / (1.0 + jnp.exp(-x))


_GELU_C = math.sqrt(2.0 / math.pi)


def _gelu(x):
    return 0.5 * x * (1.0 + jnp.tanh(_GELU_C * (x + 0.044715 * x * x * x)))


def _gelu_grad(x):
    t = jnp.tanh(_GELU_C * (x + 0.044715 * x * x * x))
    return 0.5 * (1.0 + t) + 0.5 * x * (1.0 - t * t) * _GELU_C * (1.0 + 3.0 * 0.044715 * x * x)


def _dot(a, b, ca, cb, precision=None):
    return lax.dot_general(a, b, (((ca,), (cb,)), ((), ())), preferred_element_type=F32, precision=precision)


def matmul(a, b, *, name, ta=False, tb=False, ab='n', bb='n', res=None, res_scale=1.0, scale=1.0, out_dtype=F32):
    if ta:
        _, K, M = a.shape
    else:
        _, M, K = a.shape
    if tb:
        _, N, K2 = b.shape
    else:
        _, K2, N = b.shape
    assert K == K2, (a.shape, b.shape)
    n_out = max(a.shape[0] if ab == 'o' else 1, b.shape[0] if bb == 'o' else 1)
    n_red = max(a.shape[0] if ab == 'r' else 1, b.shape[0] if bb == 'r' else 1)
    tm, tn = _tile(M), _tile(N)
    tk = K if K <= 2304 else _tile(K)
    nkt = K // tk
    n_steps = n_red * nkt

    def bsel(mode, o, r):
        if mode == 'o':
            return o
        if mode == 'r':
            return r // nkt if nkt > 1 else r
        return 0

    def ksel(r):
        if nkt == 1:
            return 0
        return r % nkt if n_red > 1 else r

    a_map = (lambda o, i, j, r: (bsel(ab, o, r), ksel(r), i)) if ta else (lambda o, i, j, r: (bsel(ab, o, r), i, ksel(r)))
    b_map = (lambda o, i, j, r: (bsel(bb, o, r), j, ksel(r))) if tb else (lambda o, i, j, r: (bsel(bb, o, r), ksel(r), j))
    o_map = lambda o, i, j, r: (o, i, j)
    in_specs = [pl.BlockSpec((None, tk, tm) if ta else (None, tm, tk), a_map),
                pl.BlockSpec((None, tn, tk) if tb else (None, tk, tn), b_map)]
    operands = [a, b]
    if res is not None:
        in_specs.append(pl.BlockSpec((None, tm, tn), o_map))
        operands.append(res)
    has_res = res is not None

    def body(*refs):
        a_ref, b_ref = refs[0], refs[1]
        res_ref = refs[2] if has_res else None
        o_ref = refs[3] if has_res else refs[2]
        part = _dot(a_ref[...].astype(BF16), b_ref[...].astype(BF16), 0 if ta else 1, 1 if tb else 0)

        def finish(acc):
            v = acc if scale == 1.0 else acc * scale
            if has_res:
                v = v + res_scale * res_ref[...].astype(F32)
            o_ref[...] = v.astype(o_ref.dtype)

        if n_steps == 1:
            finish(part)
        else:
            acc_ref = refs[-1]
            r = pl.program_id(3)

            @pl.when(r == 0)
            def _():
                acc_ref[...] = part

            @pl.when(r > 0)
            def _():
                acc_ref[...] += part

            @pl.when(r == n_steps - 1)
            def _():
                finish(acc_ref[...])

    return pl.pallas_call(
        body, name=name,
        grid=(n_out, M // tm, N // tn, n_steps),
        in_specs=in_specs,
        out_specs=pl.BlockSpec((None, tm, tn), o_map),
        out_shape=jax.ShapeDtypeStruct((n_out, M, N), out_dtype),
        scratch_shapes=[pltpu.VMEM((tm, tn), F32)] if n_steps > 1 else [],
        compiler_params=_cparams(),
    )(*operands)


def rowwise(fn, rows, pars, outs, accs=(), *, name, lp):
    tm = _row_tile(lp)
    n_rows, n_pars, n_outs, n_accs = len(rows), len(pars), len(outs), len(accs)
    in_specs = [pl.BlockSpec((tm, w), functools.partial(lambda i, cb: (i, cb), cb=cb)) for _, w, cb in rows]
    in_specs += [pl.BlockSpec(p.shape, functools.partial(lambda i, nd: (0,) * nd, nd=p.ndim)) for p in pars]
    out_specs = [pl.BlockSpec((tm, w), lambda i: (i, 0)) for w, _ in outs]
    out_specs += [pl.BlockSpec(s, functools.partial(lambda i, nd: (0,) * nd, nd=len(s))) for s, _ in accs]
    out_shape = [jax.ShapeDtypeStruct((lp, w), dt) for w, dt in outs]
    out_shape += [jax.ShapeDtypeStruct(s, dt) for s, dt in accs]

    def body(*refs):
        i = pl.program_id(0)
        rv = [r[...] for r in refs[:n_rows]]
        pv = [r[...] for r in refs[n_rows:n_rows + n_pars]]
        o_refs = refs[n_rows + n_pars:n_rows + n_pars + n_outs]
        a_refs = refs[n_rows + n_pars + n_outs:]
        ov, av = fn(i * tm, rv, pv)
        for r, v in zip(o_refs, ov):
            r[...] = v.astype(r.dtype)
        if n_accs:
            @pl.when(i == 0)
            def _():
                for r, v in zip(a_refs, av):
                    r[...] = v.astype(r.dtype)

            @pl.when(i > 0)
            def _():
                for r, v in zip(a_refs, av):
                    r[...] += v.astype(r.dtype)

    res = pl.pallas_call(
        body, name=name, grid=(lp // tm,), in_specs=in_specs, out_specs=out_specs, out_shape=out_shape,
        compiler_params=_cparams(),
    )(*[r[0] for r in rows], *pars)
    return res


def _row_mask(row0, shape):
    return (row0 + lax.broadcasted_iota(jnp.int32, shape, 0)) >= PAD


def ffn_up(hb, wg, wu, lp):
    tm = _row_tile(lp)

    def body(h_ref, wg_ref, wu_ref, ab_ref, hid_ref):
        h = h_ref[...]
        a = _dot(h, wg_ref[...], 1, 1)
        b = _dot(h, wu_ref[...], 1, 1)
        ab_ref[0] = a.astype(BF16)
        ab_ref[1] = b.astype(BF16)
        hid_ref[...] = (a * _sigmoid(a) * b).astype(BF16)

    wspec = pl.BlockSpec((None, FF_SHARD, D_MODEL), lambda j, i: (j, 0, 0))
    return pl.pallas_call(
        body, name="ffn_up", grid=(N_SHARD, lp // tm),
        in_specs=[pl.BlockSpec((tm, D_MODEL), lambda j, i: (i, 0)), wspec, wspec],
        out_specs=[pl.BlockSpec((None, 2, tm, FF_SHARD), lambda j, i: (j, 0, i, 0)),
                   pl.BlockSpec((None, tm, FF_SHARD), lambda j, i: (j, i, 0))],
        out_shape=[jax.ShapeDtypeStruct((N_SHARD, 2, lp, FF_SHARD), BF16),
                   jax.ShapeDtypeStruct((N_SHARD, lp, FF_SHARD), BF16)],
        compiler_params=_cparams(),
    )(hb, wg, wu)


def _layer_norm(z, g, b):
    mu = jnp.mean(z, axis=-1, keepdims=True)
    zc = z - mu
    var = jnp.mean(zc * zc, axis=-1, keepdims=True)
    return zc * lax.rsqrt(var + LN_EPS) * g + b


def mm_res_ln(a, w, res, g, b, *, scale, name, lp):
    n_red, _, K = a.shape
    tm = _row_tile(lp)

    def body(a_ref, w_ref, res_ref, g_ref, b_ref, z_ref, h_ref, hb_ref, acc_ref):
        r = pl.program_id(1)
        part = _dot(a_ref[...].astype(BF16), w_ref[...], 1, 0)

        @pl.when(r == 0)
        def _():
            acc_ref[...] = part

        @pl.when(r > 0)
        def _():
            acc_ref[...] += part

        @pl.when(r == n_red - 1)
        def _():
            z = ALPHA * res_ref[...] + scale * acc_ref[...]
            z_ref[...] = z
            hn = _layer_norm(z, g_ref[...], b_ref[...])
            h_ref[...] = hn
            hb_ref[...] = hn.astype(BF16)

    row = pl.BlockSpec((tm, D_MODEL), lambda i, r: (i, 0))
    par = pl.BlockSpec((1, D_MODEL), lambda i, r: (0, 0))
    return pl.pallas_call(
        body, name=name, grid=(lp // tm, n_red),
        in_specs=[pl.BlockSpec((None, tm, K), lambda i, r: (r, i, 0)),
                  pl.BlockSpec((None, K, D_MODEL), lambda i, r: (r, 0, 0)), row, par, par],
        out_specs=[row, row, row],
        out_shape=[jax.ShapeDtypeStruct((lp, D_MODEL), F32), jax.ShapeDtypeStruct((lp, D_MODEL), F32),
                   jax.ShapeDtypeStruct((lp, D_MODEL), BF16)],
        scratch_shapes=[pltpu.VMEM((tm, D_MODEL), F32)],
        compiler_params=_cparams(),
    )(a, w, res, g, b)


def ln_bwd(dh, z, g, *, fscale, name, lp):
    def fn(row0, rv, pv):
        dh_, z_ = rv
        g_, = pv
        mu = jnp.mean(z_, axis=-1, keepdims=True)
        zc = z_ - mu
        rstd = lax.rsqrt(jnp.mean(zc * zc, axis=-1, keepdims=True) + LN_EPS)
        xh = zc * rstd
        dxh = dh_ * g_
        m1 = jnp.mean(dxh, axis=-1, keepdims=True)
        m2 = jnp.mean(dxh * xh, axis=-1, keepdims=True)
        dz = rstd * (dxh - m1 - xh * m2)
        return ((dz, fscale * dz),
                (jnp.sum(dh_ * xh, axis=0, keepdims=True), jnp.sum(dh_, axis=0, keepdims=True)))

    return rowwise(fn, [(dh, D_MODEL, 0), (z, D_MODEL, 0)], [g], [(D_MODEL, F32), (D_MODEL, BF16)],
                   [((1, D_MODEL), F32), ((1, D_MODEL), F32)], name=name, lp=lp)


def ffn_down_bwd(dfb, wd, ab, lp):
    tm = _row_tile(lp)

    def body(df_ref, w_ref, ab_ref, da_ref, db_ref):
        dhid = _dot(df_ref[...], w_ref[...], 1, 1)
        a = ab_ref[0].astype(F32)
        b = ab_ref[1].astype(F32)
        sg = _sigmoid(a)
        da_ref[...] = (dhid * b * (sg * (1.0 + a * (1.0 - sg)))).astype(BF16)
        db_ref[...] = (dhid * (a * sg)).astype(BF16)

    ospec = pl.BlockSpec((None, tm, FF_SHARD), lambda j, i: (j, i, 0))
    return pl.pallas_call(
        body, name="ffn_down_bwd", grid=(N_SHARD, lp // tm),
        in_specs=[pl.BlockSpec((tm, D_MODEL), lambda j, i: (i, 0)),
                  pl.BlockSpec((None, FF_SHARD, D_MODEL), lambda j, i: (j, 0, 0)),
                  pl.BlockSpec((None, 2, tm, FF_SHARD), lambda j, i: (j, 0, i, 0))],
        out_specs=[ospec, ospec],
        out_shape=[jax.ShapeDtypeStruct((N_SHARD, lp, FF_SHARD), BF16)] * 2,
        compiler_params=_cparams(),
    )(dfb, wd, ab)


def ffn_dx(da, db, wg, wu, dz, lp):
    tm = _row_tile(lp)

    def body(da_ref, db_ref, wg_ref, wu_ref, dz_ref, o_ref, acc_ref):
        j = pl.program_id(1)
        part = _dot(da_ref[...], wg_ref[...], 1, 0) + _dot(db_ref[...], wu_ref[...], 1, 0)

        @pl.when(j == 0)
        def _():
            acc_ref[...] = part

        @pl.when(j > 0)
        def _():
            acc_ref[...] += part

        @pl.when(j == N_SHARD - 1)
        def _():
            o_ref[...] = acc_ref[...] + ALPHA * dz_ref[...]

    aspec = pl.BlockSpec((None, tm, FF_SHARD), lambda i, j: (j, i, 0))
    wspec = pl.BlockSpec((None, FF_SHARD, D_MODEL), lambda i, j: (j, 0, 0))
    row = pl.BlockSpec((tm, D_MODEL), lambda i, j: (i, 0))
    return pl.pallas_call(
        body, name="ffn_dx", grid=(lp // tm, N_SHARD), in_specs=[aspec, aspec, wspec, wspec, row], out_specs=row,
        out_shape=jax.ShapeDtypeStruct((lp, D_MODEL), F32), scratch_shapes=[pltpu.VMEM((tm, D_MODEL), F32)],
        compiler_params=_cparams(),
    )(da, db, wg, wu, dz)


def ffn_fwd(h, hb, wg, wu, wd, g, b, lp):
    ab, hid = ffn_up(hb, wg, wu, lp)
    z, hn, hnb = mm_res_ln(hid, wd, h, g, b, scale=0.5, name="ffn_down_ln", lp=lp)
    return hn, hnb, dict(hb=hb, ab=ab, hid=hid, z=z)


def ffn_bwd(dh, sv, wg, wu, wd, g, lp):
    dz, dfb, dg, db = ln_bwd(dh, sv['z'], g, fscale=0.5, name="ffn_ln_bwd", lp=lp)
    da, dbb = ffn_down_bwd(dfb, wd, sv['ab'], lp)
    d_wd = matmul(sv['hid'], dfb[None], ta=True, ab='o', out_dtype=WGRAD, name="ffn_dwd")
    d_wg = matmul(da, sv['hb'][None], ta=True, ab='o', out_dtype=WGRAD, name="ffn_dwg")
    d_wu = matmul(dbb, sv['hb'][None], ta=True, ab='o', out_dtype=WGRAD, name="ffn_dwu")
    dh_in = ffn_dx(da, dbb, wg, wu, dz, lp)
    return dh_in, dict(wg=d_wg, wu=d_wu, wd=d_wd, ln_g=dg, ln_b=db)


def _rope_tables(lp):
    pos = np.arange(lp, dtype=np.float32) - PAD
    inv = ROPE_BASE ** (-np.arange(0, D_ROPE, 2, dtype=np.float32) / D_ROPE)
    ang = pos[:, None] * inv[None, :]
    cos = np.concatenate([np.cos(ang), np.cos(ang)], axis=1).astype(np.float32)
    sin = np.concatenate([np.sin(ang), np.sin(ang)], axis=1).astype(np.float32)
    rot = np.zeros((D_ROPE, D_ROPE), np.float32)
    half = D_ROPE // 2
    for j in range(half):
        rot[j + half, j] = -1.0
        rot[j, j + half] = 1.0
    return jnp.asarray(cos), jnp.asarray(sin), jnp.asarray(rot)


def _rot(x, rot):
    return _dot(x, rot, 1, 0, precision=lax.Precision.HIGHEST)


def _rms(x, g):
    r = lax.rsqrt(jnp.mean(x * x, axis=-1, keepdims=True) + RMS_EPS)
    return x * r * g


def mla_prep(proj, cos, sin, rot, qg, kvg, lp):
    def fn(row0, rv, pv):
        cq, krb, ckv, c, s = rv
        qg_, kvg_, rot_ = pv
        kr = krb[:, :D_ROPE]
        return ((_rms(cq, qg_), _rms(ckv, kvg_), kr * c + _rot(kr, rot_) * s), ())

    return rowwise(fn, [(proj, Q_RANK, 0), (proj, 128, 3), (proj, KV_RANK, 2), (cos, D_ROPE, 0), (sin, D_ROPE, 0)],
                   [qg, kvg, rot], [(Q_RANK, BF16), (KV_RANK, BF16), (D_ROPE, BF16)], name="mla_prep", lp=lp)


def mla_heads(cqn, ckvn, cos, sin, rot, wqn, wqr, wkn, wv, lp):
    tm = _row_tile(lp)

    def body(cq_ref, ckv_ref, c_ref, s_ref, rot_ref, wqn_ref, wqr_ref, wkn_ref, wv_ref, qn_ref, qr_ref, kn_ref, v_ref):
        cq = cq_ref[...]
        ckv = ckv_ref[...]
        for h in range(N_HEADS):
            qn_ref[h] = _dot(cq, wqn_ref[h], 1, 0).astype(BF16)
            qr = _dot(cq, wqr_ref[h], 1, 0)
            qr_ref[h] = (qr * c_ref[...] + _rot(qr, rot_ref[...]) * s_ref[...]).astype(BF16)
            kn_ref[h] = _dot(ckv, wkn_ref[h], 1, 0).astype(BF16)
            v_ref[h] = _dot(ckv, wv_ref[h], 1, 0).astype(BF16)

    def row(w):
        return pl.BlockSpec((tm, w), lambda i: (i, 0))

    def wspec(k, n):
        return pl.BlockSpec((N_HEADS, k, n), lambda i: (0, 0, 0))

    def ospec(n):
        return pl.BlockSpec((N_HEADS, tm, n), lambda i: (0, i, 0))

    return pl.pallas_call(
        body, name="mla_heads", grid=(lp // tm,),
        in_specs=[row(Q_RANK), row(KV_RANK), row(D_ROPE), row(D_ROPE),
                  pl.BlockSpec((D_ROPE, D_ROPE), lambda i: (0, 0)),
                  wspec(Q_RANK, D_NOPE), wspec(Q_RANK, D_ROPE), wspec(KV_RANK, D_NOPE), wspec(KV_RANK, D_V)],
        out_specs=[ospec(D_NOPE), ospec(D_ROPE), ospec(D_NOPE), ospec(D_V)],
        out_shape=[jax.ShapeDtypeStruct((N_HEADS, lp, D_NOPE), BF16), jax.ShapeDtypeStruct((N_HEADS, lp, D_ROPE), BF16),
                   jax.ShapeDtypeStruct((N_HEADS, lp, D_NOPE), BF16), jax.ShapeDtypeStruct((N_HEADS, lp, D_V), BF16)],
        compiler_params=_cparams(),
    )(cqn, ckvn, cos, sin, rot, wqn, wqr, wkn, wv)


def _att_probs(qn, qr, kn, kr, row0, tq, lp):
    s = (_dot(qn, kn, 1, 1) + _dot(qr, kr, 1, 1)) * ATT_SCALE
    qi = row0 + lax.broadcasted_iota(jnp.int32, (tq, lp), 0)
    ki = lax.broadcasted_iota(jnp.int32, (tq, lp), 1)
    s = jnp.where((ki <= qi) & (ki >= PAD), s, -1e30)
    p = jnp.exp(s - jnp.max(s, axis=-1, keepdims=True))
    return p / jnp.sum(p, axis=-1, keepdims=True)


def _att_specs(lp):
    def hspec(n):
        return pl.BlockSpec((None, lp, n), lambda h, i: (h, 0, 0))

    return hspec, pl.BlockSpec((lp, D_ROPE), lambda h, i: (0, 0))


def _att_tiles(lp):
    tiles, r = [(0, X0)], X0
    while r < lp:
        tiles.append((r, 256))
        r += 256
    assert r == lp
    return tiles


def attn_fwd(qn, qr, kn, v, kr, lp):
    kspec, krspec = _att_specs(lp)

    def body(qn_ref, qr_ref, kn_ref, v_ref, kr_ref, o_ref):
        for r0, rows in _att_tiles(lp):
            ke, q = r0 + rows, slice(r0, r0 + rows)
            p = _att_probs(qn_ref[q, :], qr_ref[q, :], kn_ref[0:ke, :], kr_ref[0:ke, :], r0, rows, ke)
            o_ref[q, :] = _dot(p.astype(BF16), v_ref[0:ke, :], 1, 0).astype(BF16)

    return pl.pallas_call(
        body, name="attn_fwd", grid=(N_HEADS, 1),
        in_specs=[kspec(D_NOPE), kspec(D_ROPE), kspec(D_NOPE), kspec(D_V), krspec],
        out_specs=kspec(D_V), out_shape=jax.ShapeDtypeStruct((N_HEADS, lp, D_V), BF16),
        compiler_params=_cparams(),
    )(qn, qr, kn, v, kr)


def attn_bwd(qn, qr, kn, v, kr, do, lp):
    kspec, krspec = _att_specs(lp)

    def body(qn_ref, qr_ref, kn_ref, v_ref, kr_ref, do_ref, dqn_ref, dqr_ref, dkn_ref, dv_ref, dkr_ref):
        dkn_ref[...] = jnp.zeros_like(dkn_ref)
        dv_ref[...] = jnp.zeros_like(dv_ref)

        @pl.when(pl.program_id(0) == 0)
        def _():
            dkr_ref[...] = jnp.zeros_like(dkr_ref)

        for r0, rows in _att_tiles(lp):
            ke, q = r0 + rows, slice(r0, r0 + rows)
            qn_, qr_, do_ = qn_ref[q, :], qr_ref[q, :], do_ref[q, :]
            kn_, v_, kr_ = kn_ref[0:ke, :], v_ref[0:ke, :], kr_ref[0:ke, :]
            p = _att_probs(qn_, qr_, kn_, kr_, r0, rows, ke)
            dp = _dot(do_, v_, 1, 1)
            delta = jnp.sum(p * dp, axis=-1, keepdims=True)
            ds = (p * (dp - delta) * ATT_SCALE).astype(BF16)
            dqn_ref[q, :] = _dot(ds, kn_, 1, 0).astype(BF16)
            dqr_ref[q, :] = _dot(ds, kr_, 1, 0)
            dkn_ref[0:ke, :] += _dot(ds, qn_, 0, 0)
            dv_ref[0:ke, :] += _dot(p.astype(BF16), do_, 0, 0)
            dkr_ref[0:ke, :] += _dot(ds, qr_, 0, 0)

    return pl.pallas_call(
        body, name="attn_bwd", grid=(N_HEADS, 1),
        in_specs=[kspec(D_NOPE), kspec(D_ROPE), kspec(D_NOPE), kspec(D_V), krspec, kspec(D_V)],
        out_specs=[kspec(D_NOPE), kspec(D_ROPE), kspec(D_NOPE), kspec(D_V), krspec],
        out_shape=[jax.ShapeDtypeStruct((N_HEADS, lp, D_NOPE), BF16), jax.ShapeDtypeStruct((N_HEADS, lp, D_ROPE), F32),
                   jax.ShapeDtypeStruct((N_HEADS, lp, D_NOPE), F32), jax.ShapeDtypeStruct((N_HEADS, lp, D_V), F32),
                   jax.ShapeDtypeStruct((lp, D_ROPE), F32)],
        compiler_params=_cparams(),
    )(qn, qr, kn, v, kr, do)


def mla_heads_bwd(dqn, dqr, dkn, dv, cos, sin, rot, wqn, wqr, wkn, wv, lp):
    tm = _row_tile(lp)

    def body(dqn_ref, dqr_ref, dkn_ref, dv_ref, c_ref, s_ref, rot_ref, wqn_ref, wqr_ref, wkn_ref, wv_ref,
             dcq_ref, dckv_ref, dqrp_ref):
        dcq = jnp.zeros(dcq_ref.shape, F32)
        dckv = jnp.zeros(dckv_ref.shape, F32)
        for h in range(N_HEADS):
            dqr_ = dqr_ref[h]
            dqrp = (dqr_ * c_ref[...] - _rot(dqr_ * s_ref[...], rot_ref[...])).astype(BF16)
            dqrp_ref[h] = dqrp
            dcq = dcq + _dot(dqn_ref[h], wqn_ref[h], 1, 1) + _dot(dqrp, wqr_ref[h], 1, 1)
            dckv = dckv + _dot(dkn_ref[h].astype(BF16), wkn_ref[h], 1, 1) + _dot(dv_ref[h].astype(BF16), wv_ref[h], 1, 1)
        dcq_ref[...] = dcq
        dckv_ref[...] = dckv

    def hspec(n):
        return pl.BlockSpec((N_HEADS, tm, n), lambda i: (0, i, 0))

    def row(w):
        return pl.BlockSpec((tm, w), lambda i: (i, 0))

    def wspec(k, n):
        return pl.BlockSpec((N_HEADS, k, n), lambda i: (0, 0, 0))

    return pl.pallas_call(
        body, name="mla_heads_bwd", grid=(lp // tm,),
        in_specs=[hspec(D_NOPE), hspec(D_ROPE), hspec(D_NOPE), hspec(D_V), row(D_ROPE), row(D_ROPE),
                  pl.BlockSpec((D_ROPE, D_ROPE), lambda i: (0, 0)),
                  wspec(Q_RANK, D_NOPE), wspec(Q_RANK, D_ROPE), wspec(KV_RANK, D_NOPE), wspec(KV_RANK, D_V)],
        out_specs=[row(Q_RANK), row(KV_RANK), hspec(D_ROPE)],
        out_shape=[jax.ShapeDtypeStruct((lp, Q_RANK), F32), jax.ShapeDtypeStruct((lp, KV_RANK), F32),
                   jax.ShapeDtypeStruct((N_HEADS, lp, D_ROPE), BF16)],
        compiler_params=_cparams(),
    )(dqn, dqr, dkn, dv, cos, sin, rot, wqn, wqr, wkn, wv)


def _rms_bwd(dy, x, g):
    r = lax.rsqrt(jnp.mean(x * x, axis=-1, keepdims=True) + RMS_EPS)
    n = x * r
    dn = dy * g
    dx = r * (dn - n * jnp.mean(dn * n, axis=-1, keepdims=True))
    return dx, jnp.sum(dy * n, axis=0, keepdims=True)


def mla_prep_bwd(dcq, dckv, dkr, proj, cos, sin, rot, qg, kvg, lp):
    def fn(row0, rv, pv):
        dcq_, dckv_, dkr_, cq, ckv, c, s = rv
        qg_, kvg_, rot_ = pv
        dxq, dgq = _rms_bwd(dcq_, cq, qg_)
        dxkv, dgkv = _rms_bwd(dckv_, ckv, kvg_)
        dkr_raw = dkr_ * c - _rot(dkr_ * s, rot_)
        return ((dxq, dxkv, dkr_raw), (dgq, dgkv))

    return rowwise(fn, [(dcq, Q_RANK, 0), (dckv, KV_RANK, 0), (dkr, D_ROPE, 0), (proj, Q_RANK, 0), (proj, KV_RANK, 2),
                        (cos, D_ROPE, 0), (sin, D_ROPE, 0)], [qg, kvg, rot],
                   [(Q_RANK, BF16), (KV_RANK, BF16), (D_ROPE, BF16)], [((1, Q_RANK), F32), ((1, KV_RANK), F32)],
                   name="mla_prep_bwd", lp=lp)


def _shift_down(x, d, rows):
    return jnp.where(rows >= d, pltpu.roll(x, d, 0), 0.0)


def _shift_up(x, d, rows, n):
    return jnp.where(rows < n - d, pltpu.roll(x, n - d, 0), 0.0)


_CONV_W = 128
_XB, _BG, _CG = 1024 // _CONV_W, 1536 // _CONV_W, 2048 // _CONV_W


def _conv_specs(lp):
    def pspec(base):
        return pl.BlockSpec((lp, _CONV_W), functools.partial(lambda c, base: (0, base + c), base=base))

    col = pl.BlockSpec((lp, _CONV_W), lambda c: (0, c))
    wspec = pl.BlockSpec((3, _CONV_W), lambda c: (0, c))
    bspec = pl.BlockSpec((1, _CONV_W), lambda c: (0, c))
    return pspec, col, wspec, bspec


def _conv_core(xbar, cg, w, bias, lp):
    rows = lax.broadcasted_iota(jnp.int32, (lp, _CONV_W), 0)
    u = jnp.where(rows >= PAD, cg * xbar, 0.0)
    u1 = _shift_down(u, 1, rows)
    u2 = _shift_down(u, 2, rows)
    y = bias + w[0:1] * u2 + w[1:2] * u1 + w[2:3] * u
    return rows, u, u1, u2, y


def conv_fwd(proj, w, bias, lp):
    pspec, col, wspec, bspec = _conv_specs(lp)

    def body(x_ref, b_ref, c_ref, w_ref, bias_ref, v_ref):
        _, _, _, _, y = _conv_core(x_ref[...], c_ref[...], w_ref[...], bias_ref[...], lp)
        v_ref[...] = (b_ref[...] * y).astype(BF16)

    return pl.pallas_call(
        body, name="conv_fwd", grid=(MIX // _CONV_W,),
        in_specs=[pspec(_XB), pspec(_BG), pspec(_CG), wspec, bspec], out_specs=col,
        out_shape=jax.ShapeDtypeStruct((lp, MIX), BF16), compiler_params=_cparams(),
    )(proj, proj, proj, w, bias)


def conv_bwd(dv, proj, w, bias, lp):
    pspec, col, wspec, bspec = _conv_specs(lp)

    def body(dv_ref, x_ref, b_ref, c_ref, w_ref, bias_ref, dx_ref, db_ref, dc_ref, dw_ref, dbias_ref):
        xbar, cg, w_ = x_ref[...], c_ref[...], w_ref[...]
        rows, u, u1, u2, y = _conv_core(xbar, cg, w_, bias_ref[...], lp)
        dv_ = dv_ref[...]
        db_ref[...] = (dv_ * y).astype(BF16)
        dy = dv_ * b_ref[...]
        dbias_ref[...] = jnp.sum(dy, axis=0, keepdims=True)
        dw_ref[0:1, :] = jnp.sum(dy * u2, axis=0, keepdims=True)
        dw_ref[1:2, :] = jnp.sum(dy * u1, axis=0, keepdims=True)
        dw_ref[2:3, :] = jnp.sum(dy * u, axis=0, keepdims=True)
        du = w_[2:3] * dy + w_[1:2] * _shift_up(dy, 1, rows, lp) + w_[0:1] * _shift_up(dy, 2, rows, lp)
        du = jnp.where(rows >= PAD, du, 0.0)
        dc_ref[...] = (du * xbar).astype(BF16)
        dx_ref[...] = (du * cg).astype(BF16)

    return pl.pallas_call(
        body, name="conv_bwd", grid=(MIX // _CONV_W,),
        in_specs=[col, pspec(_XB), pspec(_BG), pspec(_CG), wspec, bspec],
        out_specs=[col, col, col, wspec, bspec],
        out_shape=[jax.ShapeDtypeStruct((lp, MIX), BF16)] * 3 + [jax.ShapeDtypeStruct((3, MIX), F32),
                                                                jax.ShapeDtypeStruct((1, MIX), F32)],
        compiler_params=_cparams(),
    )(dv, proj, proj, proj, w, bias)


def _s5_disc(a_re, a_im, log_dt, b_re, b_im):
    dt = jnp.exp(log_dt)
    mag = jnp.exp(dt * a_re)
    ab_re, ab_im = mag * jnp.cos(dt * a_im), mag * jnp.sin(dt * a_im)
    den = a_re * a_re + a_im * a_im
    nr, ni = ab_re - 1.0, ab_im
    coef_re = (nr * a_re + ni * a_im) / den
    coef_im = (ni * a_re - nr * a_im) / den
    return ab_re, ab_im, coef_re * b_re - coef_im * b_im, coef_re * b_im + coef_im * b_re


_S5_ROWS = S5_GROUPS * S5_GROUP


def s5_prep(a_re, a_im, log_dt, b_re, b_im):
    def body(ar, ai, ld, br, bi, o0, o1, o2, o3):
        for o, v in zip((o0, o1, o2, o3), _s5_disc(ar[...], ai[...], ld[...], br[...], bi[...])):
            o[...] = v

    return pl.pallas_call(body, name="s5_prep",
                          out_shape=[jax.ShapeDtypeStruct((_S5_ROWS, S5_STATE), F32)] * 4)(a_re, a_im, log_dt, b_re, b_im)


def s5_prep_bwd(a_re, a_im, log_dt, b_re, b_im, d_ab_re, d_ab_im, d_bb_re, d_bb_im, sel):
    def body(ar, ai, ld, br, bi, g0, g1, g2, g3, sel_ref, da_re, da_im, dld, dbr, dbi):
        _, vjp = jax.vjp(_s5_disc, ar[...], ai[...], ld[...], br[...], bi[...])
        c_ar, c_ai, c_ld, c_br, c_bi = vjp((g0[...], g1[...], g2[...], g3[...]))
        s = sel_ref[...]
        hi = lax.Precision.HIGHEST
        da_re[...] = _dot(s, c_ar, 1, 0, precision=hi)
        da_im[...] = _dot(s, c_ai, 1, 0, precision=hi)
        dld[...] = jnp.sum(_dot(s, c_ld, 1, 0, precision=hi), axis=-1, keepdims=True)
        dbr[...] = c_br
        dbi[...] = c_bi

    g = jax.ShapeDtypeStruct((S5_GROUPS, S5_STATE), F32)
    full = jax.ShapeDtypeStruct((_S5_ROWS, S5_STATE), F32)
    return pl.pallas_call(body, name="s5_prep_bwd",
                          out_shape=[g, g, jax.ShapeDtypeStruct((S5_GROUPS, 1), F32), full, full],
                          )(a_re, a_im, log_dt, b_re, b_im, d_ab_re, d_ab_im, d_bb_re, d_bb_im, sel)


_SCAN_W = 128
_SCAN_STEPS = int(math.log2(SCAN_CHUNK))


def _cmul(ar, ai, br, bi):
    return ar * br - ai * bi, ar * bi + ai * br


def _scan_powers(ar, ai, reverse):
    pw = [(ar, ai)]
    for _ in range(_SCAN_STEPS):
        pw.append(_cmul(*pw[-1], *pw[-1]))
    rows = lax.broadcasted_iota(jnp.int32, (SCAN_CHUNK, ar.shape[-1]), 0)
    tr = jnp.broadcast_to(ar, rows.shape)
    ti = jnp.broadcast_to(ai, rows.shape)
    for k in range(_SCAN_STEPS):
        d = 2 ** k
        if reverse:
            live = rows < SCAN_CHUNK - d
            mr, mi = _cmul(tr, ti, _shift_up(tr, d, rows, SCAN_CHUNK), _shift_up(ti, d, rows, SCAN_CHUNK))
        else:
            live = rows >= d
            mr, mi = _cmul(tr, ti, _shift_down(tr, d, rows), _shift_down(ti, d, rows))
        tr = jnp.where(live, mr, tr)
        ti = jnp.where(live, mi, ti)
    return pw, rows, tr, ti


def s5_scan(bu, ab_re, ab_im, lp):
    n_chunks = lp // SCAN_CHUNK

    def body(bu_ref, ar_ref, ai_ref, s_ref):
        ar, ai = ar_ref[...], ai_ref[...]
        pw, rows, tr, ti = _scan_powers(ar, ai, False)

        def chunk(ci, carry):
            cr, cim = carry
            r0 = pl.multiple_of(ci * SCAN_CHUNK, SCAN_CHUNK)
            xr = bu_ref[0, pl.ds(r0, SCAN_CHUNK), :]
            xi = bu_ref[1, pl.ds(r0, SCAN_CHUNK), :]
            for k in range(_SCAN_STEPS):
                d = 2 ** k
                mr, mi = _cmul(pw[k][0], pw[k][1], _shift_down(xr, d, rows), _shift_down(xi, d, rows))
                xr, xi = xr + mr, xi + mi
            mr, mi = _cmul(tr, ti, cr, cim)
            xr, xi = xr + mr, xi + mi
            s_ref[0, pl.ds(r0, SCAN_CHUNK), :] = xr
            s_ref[1, pl.ds(r0, SCAN_CHUNK), :] = xi
            return xr[SCAN_CHUNK - 1:SCAN_CHUNK, :], xi[SCAN_CHUNK - 1:SCAN_CHUNK, :]

        zero = jnp.zeros((1, _SCAN_W), F32)
        lax.fori_loop(0, n_chunks, chunk, (zero, zero))

    spec = pl.BlockSpec((2, lp, _SCAN_W), lambda c: (0, 0, c))
    aspec = pl.BlockSpec((1, _SCAN_W), lambda c: (0, c))
    return pl.pallas_call(
        body, name="s5_scan", grid=(S5_LANES // _SCAN_W,), in_specs=[spec, aspec, aspec], out_specs=spec,
        out_shape=jax.ShapeDtypeStruct((2, lp, S5_LANES), F32), compiler_params=_cparams(),
    )(bu, ab_re, ab_im)


def s5_scan_bwd(ds, s, ab_re, ab_im, lp):
    n_chunks = lp // SCAN_CHUNK

    def body(ds_ref, s_ref, ar_ref, ai_ref, g_ref, da_ref):
        ar, ai = ar_ref[...], -ai_ref[...]
        pw, rows, tr, ti = _scan_powers(ar, ai, True)

        def chunk(k, carry):
            cr, cim, dar, dai = carry
            ci = n_chunks - 1 - k
            r0 = pl.multiple_of(ci * SCAN_CHUNK, SCAN_CHUNK)
            xr = ds_ref[0, pl.ds(r0, SCAN_CHUNK), :]
            xi = ds_ref[1, pl.ds(r0, SCAN_CHUNK), :]
            for j in range(_SCAN_STEPS):
                d = 2 ** j
                mr, mi = _cmul(pw[j][0], pw[j][1], _shift_up(xr, d, rows, SCAN_CHUNK), _shift_up(xi, d, rows, SCAN_CHUNK))
                xr, xi = xr + mr, xi + mi
            mr, mi = _cmul(tr, ti, cr, cim)
            xr, xi = xr + mr, xi + mi
            g_ref[0, pl.ds(r0, SCAN_CHUNK), :] = xr
            g_ref[1, pl.ds(r0, SCAN_CHUNK), :] = xi
            prev0 = pl.multiple_of(jnp.maximum(r0 - 8, 0), 8)
            live = (ci > 0).astype(F32)
            pr = s_ref[0, pl.ds(prev0, 8), :][7:8, :] * live
            pim = s_ref[1, pl.ds(prev0, 8), :][7:8, :] * live
            sr = s_ref[0, pl.ds(r0, SCAN_CHUNK), :]
            si = s_ref[1, pl.ds(r0, SCAN_CHUNK), :]
            sr = jnp.where(rows >= 1, pltpu.roll(sr, 1, 0), pr)
            si = jnp.where(rows >= 1, pltpu.roll(si, 1, 0), pim)
            dar = dar + jnp.sum(xr * sr + xi * si, axis=0, keepdims=True)
            dai = dai + jnp.sum(xi * sr - xr * si, axis=0, keepdims=True)
            return xr[0:1, :], xi[0:1, :], dar, dai

        zero = jnp.zeros((1, _SCAN_W), F32)
        _, _, dar, dai = lax.fori_loop(0, n_chunks, chunk, (zero, zero, zero, zero))
        da_ref[0] = dar
        da_ref[1] = dai

    spec = pl.BlockSpec((2, lp, _SCAN_W), lambda c: (0, 0, c))
    aspec = pl.BlockSpec((1, _SCAN_W), lambda c: (0, c))
    return pl.pallas_call(
        body, name="s5_scan_bwd", grid=(S5_LANES // _SCAN_W,), in_specs=[spec, spec, aspec, aspec],
        out_specs=[spec, pl.BlockSpec((2, 1, _SCAN_W), lambda c: (0, 0, c))],
        out_shape=[jax.ShapeDtypeStruct((2, lp, S5_LANES), F32), jax.ShapeDtypeStruct((2, 1, S5_LANES), F32)],
        compiler_params=_cparams(),
    )(ds, s, ab_re, ab_im)


S5_BLOCKS = 4
_S5_PER = S5_GROUPS // S5_BLOCKS


def _blockdiag(x):
    _, r, c = x.shape
    eye = jnp.eye(_S5_PER, dtype=x.dtype)
    x = x.reshape(S5_BLOCKS, _S5_PER, r, c)
    return (x[:, :, :, None, :] * eye[None, :, None, :, None]).reshape(S5_BLOCKS, _S5_PER * r, _S5_PER * c)


def _blockdiag_extract(m, r, c):
    return jnp.einsum('qgrgc->qgrc', m.reshape(S5_BLOCKS, _S5_PER, r, _S5_PER, c)).reshape(S5_GROUPS, r, c)


def bd_matmul(a, w, *, w_t, reduce, res=None, name):
    _, M, _ = a.shape
    n_w, _, k1, k2 = w.shape
    ka, kout = (k2, k1) if w_t else (k1, k2)
    tm = _row_tile(M)
    n_out, n_red = (1, n_w) if reduce else (n_w, 1)
    has_res = res is not None

    assert n_red <= 2

    def body(*refs):
        a_ref, w_ref = refs[0], refs[1]
        o_ref = refs[3] if has_res else refs[2]
        for q in range(S5_BLOCKS):
            cols = slice(q * kout, (q + 1) * kout)
            part = _dot(a_ref[:, q * ka:(q + 1) * ka].astype(BF16), w_ref[q], 1, 1 if w_t else 0)
            if n_red == 1:
                o_ref[:, cols] = part
            else:
                acc_ref = refs[-1]

                @pl.when(pl.program_id(2) == 0)
                def _():
                    acc_ref[:, cols] = part

                @pl.when(pl.program_id(2) == 1)
                def _():
                    tot = acc_ref[:, cols] + part
                    o_ref[:, cols] = tot + refs[2][:, cols] if has_res else tot

    if reduce:
        a_map, w_map = (lambda o, i, r: (r, i, 0)), (lambda o, i, r: (r, 0, 0, 0))
    else:
        a_map, w_map = (lambda o, i, r: (0, i, 0)), (lambda o, i, r: (o, 0, 0, 0))
    o_map = lambda o, i, r: (o, i, 0)
    in_specs = [pl.BlockSpec((None, tm, S5_BLOCKS * ka), a_map), pl.BlockSpec((None, S5_BLOCKS, k1, k2), w_map)]
    operands = [a, w]
    if has_res:
        in_specs.append(pl.BlockSpec((None, tm, S5_BLOCKS * kout), o_map))
        operands.append(res)
    return pl.pallas_call(
        body, name=name, grid=(n_out, M // tm, n_red), in_specs=in_specs,
        out_specs=pl.BlockSpec((None, tm, S5_BLOCKS * kout), o_map),
        out_shape=jax.ShapeDtypeStruct((n_out, M, S5_BLOCKS * kout), F32),
        scratch_shapes=[pltpu.VMEM((tm, S5_BLOCKS * kout), F32)] if n_red > 1 else [],
        compiler_params=_cparams(),
    )(*operands)


def bd_outer(a, b, name):
    na, M, wa = a.shape
    nb_, _, wb = b.shape
    ka, kb = wa // S5_BLOCKS, wb // S5_BLOCKS
    n_out = max(na, nb_)

    def body(a_ref, b_ref, o_ref):
        o_ref[...] = _dot(a_ref[...].astype(BF16), b_ref[...].astype(BF16), 0, 0)

    return pl.pallas_call(
        body, name=name, grid=(n_out, S5_BLOCKS),
        in_specs=[pl.BlockSpec((None, M, ka), (lambda o, q: (o, 0, q)) if na > 1 else (lambda o, q: (0, 0, q))),
                  pl.BlockSpec((None, M, kb), (lambda o, q: (o, 0, q)) if nb_ > 1 else (lambda o, q: (0, 0, q)))],
        out_specs=pl.BlockSpec((None, None, ka, kb), lambda o, q: (o, q, 0, 0)),
        out_shape=jax.ShapeDtypeStruct((n_out, S5_BLOCKS, ka, kb), F32), compiler_params=_cparams(),
    )(a, b)


def s5_u(proj, lp):
    def fn(row0, rv, pv):
        u, = rv
        return ((jnp.where(_row_mask(row0, u.shape), u, 0.0),), ())

    return rowwise(fn, [(proj, MIX, 5)], [], [(MIX, BF16)], name="s5_u", lp=lp)[0]


def s5_y(ys, proj, d, lp):
    def fn(row0, rv, pv):
        ys_, u = rv
        y = ys_ + pv[0] * u
        return ((y, _gelu(y)), ())

    return rowwise(fn, [(ys, MIX, 0), (proj, MIX, 5)], [d], [(MIX, F32), (MIX, BF16)], name="s5_y", lp=lp)


def s5_glu(z, y, b, lp):
    def fn(row0, rv, pv):
        z_, y_ = rv
        return ((_gelu(y_) * _sigmoid(z_ + pv[0]),), ())

    return rowwise(fn, [(z, MIX, 0), (y, MIX, 0)], [b], [(MIX, BF16)], name="s5_glu", lp=lp)[0]


def s5_glu_bwd(dgl, z, y, b, lp):
    def fn(row0, rv, pv):
        dgl_, z_, y_ = rv
        sg = _sigmoid(z_ + pv[0])
        dz = dgl_ * _gelu(y_) * sg * (1.0 - sg)
        return ((dgl_ * sg, dz), (jnp.sum(dz, axis=0, keepdims=True),))

    return rowwise(fn, [(dgl, MIX, 0), (z, MIX, 0), (y, MIX, 0)], [b], [(MIX, F32), (MIX, BF16)], [((1, MIX), F32)],
                   name="s5_glu_bwd", lp=lp)


def s5_y_bwd(dyg, y, proj, d, lp):
    def fn(row0, rv, pv):
        dyg_, y_, u = rv
        dy = dyg_ * _gelu_grad(y_)
        return ((dy, dy * pv[0]), (jnp.sum(dy * u, axis=0, keepdims=True),))

    return rowwise(fn, [(dyg, MIX, 0), (y, MIX, 0), (proj, MIX, 5)], [d], [(MIX, BF16), (MIX, F32)], [((1, MIX), F32)],
                   name="s5_y_bwd", lp=lp)


def s5_du(du, lp):
    def fn(row0, rv, pv):
        return ((jnp.where(_row_mask(row0, rv[0].shape), rv[0], 0.0),), ())

    return rowwise(fn, [(du, MIX, 0)], [], [(MIX, BF16)], name="s5_du", lp=lp)[0]


def merge_fwd(proj, ya, yb, yc, lp):
    def fn(row0, rv, pv):
        g0, g1, g2, a, b, c = rv
        return ((_sigmoid(g0) * a + _sigmoid(g1) * b + _sigmoid(g2) * c,), ())

    return rowwise(fn, [(proj, D_MODEL, 3), (proj, D_MODEL, 4), (proj, D_MODEL, 5), (ya, D_MODEL, 0), (yb, D_MODEL, 0),
                        (yc, D_MODEL, 0)], [], [(D_MODEL, BF16)], name="merge_fwd", lp=lp)[0]


def merge_bwd(dmix, proj, ya, yb, yc, lp):
    def fn(row0, rv, pv):
        dm, g0, g1, g2, a, b, c = rv
        outs_y, outs_g = [], []
        for g, yv in ((g0, a), (g1, b), (g2, c)):
            sg = _sigmoid(g)
            outs_y.append(dm * sg)
            outs_g.append(dm * yv * sg * (1.0 - sg))
        return (tuple(outs_y) + tuple(outs_g), ())

    return rowwise(fn, [(dmix, D_MODEL, 0), (proj, D_MODEL, 3), (proj, D_MODEL, 4), (proj, D_MODEL, 5),
                        (ya, D_MODEL, 0), (yb, D_MODEL, 0), (yc, D_MODEL, 0)], [], [(D_MODEL, BF16)] * 6,
                   name="merge_bwd", lp=lp)


def loss_head(h, tgt, lp):
    def fn(row0, rv, pv):
        h_, t_ = rv
        live = (row0 + lax.broadcasted_iota(jnp.int32, h_.shape, 0)) >= X0
        diff = jnp.where(live, h_ - t_, 0.0)
        ssq = jnp.sum(jnp.sum(diff * diff, axis=1, keepdims=True), axis=0, keepdims=True)
        return ((diff * (1.0 / D_MODEL),), (ssq * (0.5 / D_MODEL),))

    return rowwise(fn, [(h, D_MODEL, 0), (tgt, D_MODEL, 0)], [], [(D_MODEL, F32)], [((1, 1), F32)], name="loss_head", lp=lp)


def _s5_consts(W):
    ab_re_rep, ab_im_rep, bb_re, bb_im = s5_prep(W['s5_a_re'], W['s5_a_im'], W['s5_log_dt'], W['s5_b_re'], W['s5_b_im'])
    pick = lambda t: t.reshape(S5_GROUPS, S5_GROUP, S5_STATE)[:, 0].reshape(1, S5_LANES)
    bb = jnp.stack([_blockdiag(bb_re.reshape(S5_GROUPS, S5_GROUP, S5_STATE)),
                    _blockdiag(bb_im.reshape(S5_GROUPS, S5_GROUP, S5_STATE))]).astype(BF16)
    return pick(ab_re_rep), pick(ab_im_rep), bb


def layer_fwd(h, hb, W, tabs, lp, ffn1=None, ffn2=True):
    cos, sin, rot = tabs
    h1, h1b, sv1 = ffn1 if ffn1 is not None else ffn_fwd(h, hb, W['wg1'], W['wu1'], W['wd1'], W['ln1_g'], W['ln1_b'], lp)
    proj = matmul(h1b[None], W['w_in'][None], tb=True, name="proj")[0]
    cqn, ckvn, kr = mla_prep(proj, cos, sin, rot, W['q_norm_g'], W['kv_norm_g'], lp)
    qn, qr, kn, v = mla_heads(cqn, ckvn, cos, sin, rot, W['wqn'], W['wqr'], W['wkn'], W['wv'], lp)
    o = attn_fwd(qn, qr, kn, v, kr, lp)
    ya = matmul(o, W['mla_wo'], ab='r', bb='r', name="mla_out")[0]
    vconv = conv_fwd(proj, W['conv_w'], W['conv_b'], lp)
    yb = matmul(vconv[None], W['conv_wout'][None], name="conv_out")[0]
    ub = s5_u(proj, lp)
    ab_re, ab_im, bb = _s5_consts(W)
    bu = bd_matmul(ub[None], bb, w_t=False, reduce=False, name="s5_bu")
    s = s5_scan(bu, ab_re, ab_im, lp)
    ys = bd_matmul(s, W['s5_ct'], w_t=False, reduce=True, name="s5_cs")[0]
    y, ygb = s5_y(ys, proj, W['s5_d'], lp)
    zg = matmul(ygb[None], W['s5_wglu'][None], name="s5_glu_mm")[0]
    glb = s5_glu(zg, y, W['s5_b_glu'], lp)
    yc = matmul(glb[None], W['s5_wout'][None], name="s5_out")[0]
    mixed = merge_fwd(proj, ya, yb, yc, lp)
    z2, h2, h2b = mm_res_ln(mixed[None], W['w_o'][None], h1, W['ln2_g'], W['ln2_b'], scale=1.0, name="wo_ln", lp=lp)
    sv = dict(sv1=sv1, h1b=h1b, proj=proj, cqn=cqn, ckvn=ckvn, kr=kr, qn=qn, qr=qr, kn=kn, v=v, o=o, ya=ya,
              vconv=vconv, yb=yb, ub=ub, ab_re=ab_re, ab_im=ab_im, bb=bb, s=s, y=y, ygb=ygb, zg=zg, glb=glb, yc=yc,
              mixed=mixed, z2=z2)
    if not ffn2:
        return h2, h2b, sv
    h3, h3b, sv['sv3'] = ffn_fwd(h2, h2b, W['wg2'], W['wu2'], W['wd2'], W['ln3_g'], W['ln3_b'], lp)
    return h3, h3b, sv


def layer_bwd(dh3, sv, W, tabs, lp, ffn1=True, ffn2=True):
    cos, sin, rot = tabs
    proj = sv['proj']
    G = {}
    dh2 = dh3
    if ffn2:
        dh2, g3 = ffn_bwd(dh3, sv['sv3'], W['wg2'], W['wu2'], W['wd2'], W['ln3_g'], lp)
        G.update(wg2=g3['wg'], wu2=g3['wu'], wd2=g3['wd'], ln3_g=g3['ln_g'], ln3_b=g3['ln_b'])
    dz2, dz2b, G['ln2_g'], G['ln2_b'] = ln_bwd(dh2, sv['z2'], W['ln2_g'], fscale=1.0, name="wo_ln_bwd", lp=lp)
    dmix = matmul(dz2b[None], W['w_o'][None], tb=True, name="wo_dx")[0]
    G['w_o'] = matmul(sv['mixed'][None], dz2b[None], ta=True, out_dtype=WGRAD, name="wo_dw")[0]
    dya, dyb, dyc, dg0, dg1, dg2 = merge_bwd(dmix, proj, sv['ya'], sv['yb'], sv['yc'], lp)
    dgl = matmul(dyc[None], W['s5_wout'][None], tb=True, name="s5_out_dx")[0]
    G['s5_wout'] = matmul(sv['glb'][None], dyc[None], ta=True, out_dtype=WGRAD, name="s5_out_dw")[0]
    t1, dzb, G['s5_b_glu'] = s5_glu_bwd(dgl, sv['zg'], sv['y'], W['s5_b_glu'], lp)
    dyg = matmul(dzb[None], W['s5_wglu'][None], tb=True, res=t1[None], name="s5_glu_dx")[0]
    G['s5_wglu'] = matmul(sv['ygb'][None], dzb[None], ta=True, out_dtype=WGRAD, name="s5_glu_dw")[0]
    dyb_, du_d, G['s5_d'] = s5_y_bwd(dyg, sv['y'], proj, W['s5_d'], lp)
    ds = bd_matmul(dyb_[None], W['s5_ct'], w_t=True, reduce=False, name="s5_cs_dx")
    G['s5_ct'] = bd_outer(sv['s'], dyb_[None], "s5_cs_dw")
    g_adj, d_ab = s5_scan_bwd(ds, sv['s'], sv['ab_re'], sv['ab_im'], lp)
    du = bd_matmul(g_adj, sv['bb'], w_t=True, reduce=True, res=du_d[None], name="s5_bu_dx")[0]
    d_bb = bd_outer(sv['ub'][None], g_adj, "s5_bu_dw")
    du_b = s5_du(du, lp)
    onehot = (jnp.arange(S5_GROUP) == 0).astype(F32)
    spread = lambda t: (t.reshape(S5_GROUPS, 1, S5_STATE) * onehot[None, :, None]).reshape(_S5_ROWS, S5_STATE)
    take = lambda t: _blockdiag_extract(t, S5_GROUP, S5_STATE).reshape(_S5_ROWS, S5_STATE)
    sel = jnp.kron(jnp.eye(S5_GROUPS, dtype=F32), jnp.ones((1, S5_GROUP), F32))
    (G['s5_a_re'], G['s5_a_im'], G['s5_log_dt'], G['s5_b_re'], G['s5_b_im']) = s5_prep_bwd(
        W['s5_a_re'], W['s5_a_im'], W['s5_log_dt'], W['s5_b_re'], W['s5_b_im'],
        spread(d_ab[0]), spread(d_ab[1]), take(d_bb[0]), take(d_bb[1]), sel)
    dv = matmul(dyb[None], W['conv_wout'][None], tb=True, name="conv_out_dx")[0]
    G['conv_wout'] = matmul(sv['vconv'][None], dyb[None], ta=True, out_dtype=WGRAD, name="conv_out_dw")[0]
    dxbar, dbg, dcg, G['conv_w'], G['conv_b'] = conv_bwd(dv, proj, W['conv_w'], W['conv_b'], lp)
    do = matmul(dya[None], W['mla_wo'], tb=True, bb='o', out_dtype=BF16, name="mla_out_dx")
    G['mla_wo'] = matmul(sv['o'], dya[None], ta=True, ab='o', out_dtype=WGRAD, name="mla_out_dw")
    dqn, dqr, dkn, dvv, dkr = attn_bwd(sv['qn'], sv['qr'], sv['kn'], sv['v'], sv['kr'], do, lp)
    dcq, dckv, dqrp = mla_heads_bwd(dqn, dqr, dkn, dvv, cos, sin, rot, W['wqn'], W['wqr'], W['wkn'], W['wv'], lp)
    G['wqn'] = matmul(sv['cqn'][None], dqn, ta=True, bb='o', out_dtype=WGRAD, name="mla_dwqn")
    G['wqr'] = matmul(sv['cqn'][None], dqrp, ta=True, bb='o', out_dtype=WGRAD, name="mla_dwqr")
    G['wkn'] = matmul(sv['ckvn'][None], dkn, ta=True, bb='o', out_dtype=WGRAD, name="mla_dwkn")
    G['wv'] = matmul(sv['ckvn'][None], dvv, ta=True, bb='o', out_dtype=WGRAD, name="mla_dwv")
    dcq_raw, dckv_raw, dkr_raw, G['q_norm_g'], G['kv_norm_g'] = mla_prep_bwd(
        dcq, dckv, dkr, proj, cos, sin, rot, W['q_norm_g'], W['kv_norm_g'], lp)
    zeros = lambda n: jnp.zeros((lp, n), BF16)
    dproj = jnp.concatenate([dcq_raw, dkr_raw, zeros(96), dckv_raw, zeros(256), dxbar, dbg, dcg, du_b, dg0, dg1, dg2], axis=1)
    dh1 = matmul(dproj[None], W['w_in'][None], res=dz2[None], res_scale=ALPHA, name="proj_dx")[0]
    G['w_in'] = matmul(dproj[None], sv['h1b'][None], ta=True, out_dtype=WGRAD, name="proj_dw")[0]
    if not ffn1:
        return dh1, G
    dh0, g1 = ffn_bwd(dh1, sv['sv1'], W['wg1'], W['wu1'], W['wd1'], W['ln1_g'], lp)
    G.update(wg1=g1['wg'], wu1=g1['wu'], wd1=g1['wd'], ln1_g=g1['ln_g'], ln1_b=g1['ln_b'])
    return dh0, G


def _nat_cols(st):
    return jnp.transpose(st, (1, 0, 2)).reshape(st.shape[1], -1)


def _shard_cols(nat):
    k, n = nat.shape
    return jnp.transpose(nat.reshape(k, N_SHARD, n // N_SHARD), (1, 0, 2))


def _win_pad(wt):
    z = lambda n: jnp.zeros((n, wt.shape[1]), wt.dtype)
    return jnp.concatenate([wt[0:384], wt[640:672], z(96), wt[384:640], z(256), wt[672:]], axis=0)


def _win_unpad(wp):
    return jnp.concatenate([wp[0:384], wp[512:768], wp[384:416], wp[1024:]], axis=0)


_BIG = [('ffn1_w_gate', 'T'), ('ffn1_w_up', 'T'), ('ffn1_w_down', 0), ('w_in', 'T'), ('mla_w_uq', 1), ('mla_w_ukv', 1),
        ('mla_w_o', 1), ('conv_w_out', 1), ('s5_w_glu', 0), ('s5_w_out', 1), ('w_o', 0),
        ('ffn2_w_gate', 'T'), ('ffn2_w_up', 'T'), ('ffn2_w_down', 0)]
_REPL = ['ln1_g', 'ln1_b', 'mla_q_norm_g', 'mla_kv_norm_g', 'conv_b', 's5_a_re', 's5_a_im', 's5_log_dt', 's5_b_re',
         's5_b_im', 's5_c_re', 's5_c_im', 's5_d', 's5_b_glu', 'ln2_g', 'ln2_b', 'ln3_g', 'ln3_b']


def compute_weights(st, small):
    W = {}
    for t in ('1', '2'):
        if 'ffn%s_w_gate' % t in st:
            W['wg' + t], W['wu' + t], W['wd' + t] = (st['ffn%s_w_%s' % (t, p)] for p in ('gate', 'up', 'down'))
    if 'w_in' in st:
        W.update(_mixer_weights(st))
    if small is not None:
        W.update(_small_weights(small))
    return W


def _mixer_weights(st):
    W = {}
    W['w_in'] = _win_pad(st['w_in'].reshape(D_IN, D_MODEL))
    uq = jnp.transpose(_nat_cols(st['mla_w_uq']).reshape(Q_RANK, N_HEADS, D_NOPE + D_ROPE), (1, 0, 2))
    W['wqn'], W['wqr'] = uq[:, :, :D_NOPE], uq[:, :, D_NOPE:]
    ukv = jnp.transpose(_nat_cols(st['mla_w_ukv']).reshape(KV_RANK, N_HEADS, D_NOPE + D_V), (1, 0, 2))
    W['wkn'], W['wv'] = ukv[:, :, :D_NOPE], ukv[:, :, D_NOPE:]
    W['mla_wo'] = _nat_cols(st['mla_w_o']).reshape(N_HEADS, D_V, D_MODEL)
    W['conv_wout'] = _nat_cols(st['conv_w_out'])
    W['s5_wglu'] = st['s5_w_glu'].reshape(MIX, MIX)
    W['s5_wout'] = _nat_cols(st['s5_w_out'])
    W['w_o'] = st['w_o'].reshape(D_MODEL, D_MODEL)
    return W


def _small_weights(small):
    W = {}
    W['conv_w'] = small['conv_w']
    for n in ('ln1_g', 'ln1_b', 'ln2_g', 'ln2_b', 'ln3_g', 'ln3_b', 'conv_b', 's5_b_glu'):
        W[n] = small[n].reshape(1, -1)
    W['q_norm_g'] = small['mla_q_norm_g'].reshape(1, -1)
    W['kv_norm_g'] = small['mla_kv_norm_g'].reshape(1, -1)
    W['s5_d'] = small['s5_d'].reshape(1, MIX)
    rep = lambda t: jnp.repeat(t, S5_GROUP, axis=0)
    W['s5_a_re'], W['s5_a_im'] = rep(small['s5_a_re']), rep(small['s5_a_im'])
    W['s5_log_dt'] = jnp.broadcast_to(rep(small['s5_log_dt'].reshape(S5_GROUPS, 1)), (_S5_ROWS, S5_STATE))
    tr = lambda t: jnp.transpose(t, (0, 2, 1)).reshape(_S5_ROWS, S5_STATE)
    W['s5_b_re'], W['s5_b_im'] = tr(small['s5_b_re']), tr(small['s5_b_im'])
    ct = lambda t: _blockdiag(jnp.transpose(t, (0, 2, 1)))
    W['s5_ct'] = jnp.stack([ct(small['s5_c_re']), -ct(small['s5_c_im'])]).astype(BF16)
    return W


def reference_grads(G, ffn=True):
    R = {}
    for t in ('1', '2') if ffn else ():
        R['ffn%s_w_gate' % t] = G['wg' + t].reshape(D_FF, D_MODEL).T
        R['ffn%s_w_up' % t] = G['wu' + t].reshape(D_FF, D_MODEL).T
        R['ffn%s_w_down' % t] = G['wd' + t].reshape(D_FF, D_MODEL)
    R['w_in_t'] = _win_unpad(G['w_in'])
    if ffn:
        R['w_in'] = R['w_in_t'].T
    R['mla_w_uq'] = jnp.transpose(jnp.concatenate([G['wqn'], G['wqr']], axis=2), (1, 0, 2)).reshape(Q_RANK, -1)
    R['mla_w_ukv'] = jnp.transpose(jnp.concatenate([G['wkn'], G['wv']], axis=2), (1, 0, 2)).reshape(KV_RANK, -1)
    R['mla_w_o'] = G['mla_wo'].reshape(N_HEADS * D_V, D_MODEL)
    R['conv_w'], R['conv_w_out'] = G['conv_w'], G['conv_wout']
    R['s5_w_glu'], R['s5_w_out'], R['w_o'] = G['s5_wglu'], G['s5_wout'], G['w_o']
    for n in ('ln1_g', 'ln1_b', 'ln2_g', 'ln2_b', 'ln3_g', 'ln3_b', 'conv_b', 's5_b_glu'):
        if n in G:
            R[n] = G[n].reshape(-1)
    R['mla_q_norm_g'], R['mla_kv_norm_g'] = G['q_norm_g'].reshape(-1), G['kv_norm_g'].reshape(-1)
    R['s5_d'] = G['s5_d'].reshape(S5_GROUPS, S5_GROUP)
    R['s5_a_re'], R['s5_a_im'], R['s5_log_dt'] = G['s5_a_re'], G['s5_a_im'], G['s5_log_dt'].reshape(-1)
    untr = lambda t: jnp.transpose(t.reshape(S5_GROUPS, S5_GROUP, S5_STATE), (0, 2, 1))
    R['s5_b_re'], R['s5_b_im'] = untr(G['s5_b_re']), untr(G['s5_b_im'])
    unct = lambda t: jnp.transpose(_blockdiag_extract(t, S5_STATE, S5_GROUP), (0, 2, 1))
    R['s5_c_re'], R['s5_c_im'] = unct(G['s5_ct'][0]), -unct(G['s5_ct'][1])
    return R


_ANY = pl.BlockSpec(memory_space=pl.ANY)
LANES = 1024


def _place():
    x, y, c = lax.axis_index("x"), lax.axis_index("y"), lax.axis_index("c")
    chips = [(1 - x, y), (x, 1 - y), (1 - x, 1 - y)]
    return x, y, c, chips


def _rows_of(c, half):
    return pl.ds(pl.multiple_of(c * half, 8), half)


def all_gather_shards(srcs, exact):
    n, m = len(srcs), len(exact)
    halves = [s.shape[0] // 2 for s in srcs]

    def body(*refs):
        s_refs, e_refs = refs[:n], refs[n:n + m]
        o_refs, eo_refs = refs[n + m:2 * n + m], refs[2 * n + m:2 * n + 2 * m]
        send, recv, esend, erecv, osend, orecv, lsem = refs[2 * n + 2 * m:]
        x, y, c, chips = _place()
        me = 2 * x + y
        sibling = (x, y, 1 - c)
        own = [pltpu.make_async_remote_copy(src_ref=s_refs[k], dst_ref=o_refs[k].at[me], send_sem=osend.at[k],
                                            recv_sem=orecv.at[k], device_id=sibling, device_id_type=MESH) for k in range(n)]
        local = [pltpu.make_async_copy(e_refs[k], eo_refs[k].at[me], lsem.at[k]) for k in range(m)]
        for cp in own + local:
            cp.start()

        def copy(k, s, src, idx, half_c, to):
            return pltpu.make_async_remote_copy(
                src_ref=src, dst_ref=o_refs[k].at[idx, _rows_of(half_c, halves[k])], send_sem=send.at[6 * k + s],
                recv_sem=recv.at[6 * k + s], device_id=to, device_id_type=MESH)

        def ecopy(k, j, idx, to):
            return pltpu.make_async_remote_copy(src_ref=e_refs[k], dst_ref=eo_refs[k].at[idx], send_sem=esend.at[3 * k + j],
                                                recv_sem=erecv.at[3 * k + j], device_id=to, device_id_type=MESH)

        sends = []
        for k in range(n):
            mine = s_refs[k].at[_rows_of(c, halves[k])]
            sends += [copy(k, j, mine, me, c, (*chip, c)) for j, chip in enumerate(chips)]
        for k in range(m):
            sends += [ecopy(k, j, me, (*chip, c)) for j, chip in enumerate(chips)]
        for cp in sends:
            cp.start()
        for j, chip in enumerate(chips):
            idx = 2 * chip[0] + chip[1]
            for k in range(n):
                landed = o_refs[k].at[idx, _rows_of(c, halves[k])]
                copy(k, j, landed, idx, c, sibling).wait_recv()
                fwd = copy(k, 3 + j, landed, idx, c, sibling)
                fwd.start()
                sends.append(fwd)
        for j, chip in enumerate(chips):
            idx = 2 * chip[0] + chip[1]
            for k in range(n):
                copy(k, 3 + j, s_refs[k].at[_rows_of(c, halves[k])], idx, 1 - c, sibling).wait_recv()
            for k in range(m):
                ecopy(k, j, idx, sibling).wait_recv()
        for cp in sends:
            cp.wait_send()
        for cp in own + local:
            cp.wait()

    outs = pl.pallas_call(
        body, name="all_gather_weights", in_specs=[_ANY] * (n + m), out_specs=[_ANY] * (n + m),
        out_shape=[jax.ShapeDtypeStruct((N_SHARD,) + a.shape, a.dtype) for a in list(srcs) + list(exact)],
        scratch_shapes=[pltpu.SemaphoreType.DMA((6 * n,)), pltpu.SemaphoreType.DMA((6 * n,)),
                        pltpu.SemaphoreType.DMA((3 * m,)), pltpu.SemaphoreType.DMA((3 * m,)),
                        pltpu.SemaphoreType.DMA((n,)), pltpu.SemaphoreType.DMA((n,)), pltpu.SemaphoreType.DMA((m,))],
    )(*srcs, *exact)
    return outs[:n], outs[n:]


def rs_pair_swap(gs):
    n = len(gs)

    def body(*refs):
        g_refs, r_refs, send, recv = refs[:n], refs[n:2 * n], refs[2 * n], refs[2 * n + 1]
        x, y, c, _ = _place()
        copies = [pltpu.make_async_remote_copy(
            src_ref=g_refs[k].at[pl.ds(0, N_SHARD), _rows_of(1 - c, gs[k].shape[1] // 2)], dst_ref=r_refs[k],
            send_sem=send.at[k], recv_sem=recv.at[k], device_id=(x, y, 1 - c), device_id_type=MESH) for k in range(n)]
        for cp in copies:
            cp.start()
        for cp in copies:
            cp.wait()

    return pl.pallas_call(
        body, name="grad_pair_swap", in_specs=[_ANY] * n, out_specs=[_ANY] * n,
        out_shape=[jax.ShapeDtypeStruct((N_SHARD, g.shape[1] // 2, g.shape[2]), g.dtype) for g in gs],
        scratch_shapes=[pltpu.SemaphoreType.DMA((n,)), pltpu.SemaphoreType.DMA((n,))],
    )(*gs)


def _group_tile(half, n_cols, n_arrays):
    budget = (20 * 2 ** 20) // (6 * n_arrays)
    fits = [t for t in range(8, half + 1, 8) if half % t == 0 and t * n_cols * 4 <= budget]
    return max(fits) if fits else 8


def rs_pair_add(gs, rs, cidx, out_dtype, name):
    n = len(gs)
    _, K, cols = gs[0].shape
    half = K // 2
    tr = _group_tile(half, cols, n)
    nb = half // tr

    def body(c_ref, *refs):
        for g_ref, r_ref, o_ref in zip(refs[:n], refs[n:2 * n], refs[2 * n:]):
            o_ref[...] = (g_ref[...].astype(F32) + r_ref[...].astype(F32)).astype(out_dtype)

    gspec = pl.BlockSpec((None, tr, cols), lambda j, i, c: (j, c[0] * nb + i, 0))
    rspec = pl.BlockSpec((None, tr, cols), lambda j, i, c: (j, i, 0))
    return pl.pallas_call(
        body, name=name,
        grid_spec=pltpu.PrefetchScalarGridSpec(num_scalar_prefetch=1, grid=(N_SHARD, nb), in_specs=[gspec] * n + [rspec] * n,
                                               out_specs=[rspec] * n),
        out_shape=[jax.ShapeDtypeStruct((N_SHARD, half, cols), out_dtype)] * n,
        compiler_params=_cparams(),
    )(cidx, *gs, *rs)


def rs_chip_sum(qs, nl, cidx, name):
    n = len(qs)
    _, half, cols = qs[0].shape
    tr = _group_tile(half, cols, n)
    nb = half // tr

    def body(c_ref, *refs):
        for k, q_ref in enumerate(refs[:n]):
            o_ref = refs[n + k // nl]
            o_ref[k % nl] = ((q_ref[0].astype(F32) + q_ref[1].astype(F32)) + q_ref[2].astype(F32)) + q_ref[3].astype(F32)

    return pl.pallas_call(
        body, name=name,
        grid_spec=pltpu.PrefetchScalarGridSpec(
            num_scalar_prefetch=1, grid=(nb,),
            in_specs=[pl.BlockSpec((N_SHARD, tr, cols), lambda i, c: (0, i, 0))] * n,
            out_specs=[pl.BlockSpec((nl, tr, cols), lambda i, c: (0, c[0] * nb + i, 0))] * (n // nl)),
        out_shape=[jax.ShapeDtypeStruct((nl, 2 * half, cols), F32)] * (n // nl),
        compiler_params=_cparams(),
    )(cidx, *qs)


def rs_pair_gather(fs, name):
    n = len(fs)

    def body(*refs):
        f_refs, send, recv = refs[n:2 * n], refs[2 * n], refs[2 * n + 1]
        x, y, c, _ = _place()
        copies = []
        for k in range(n):
            rows = f_refs[k].at[pl.ds(0, fs[k].shape[0]), _rows_of(c, fs[k].shape[1] // 2)]
            copies.append(pltpu.make_async_remote_copy(src_ref=rows, dst_ref=rows, send_sem=send.at[k], recv_sem=recv.at[k],
                                                       device_id=(x, y, 1 - c), device_id_type=MESH))
        for cp in copies:
            cp.start()
        for cp in copies:
            cp.wait()

    return pl.pallas_call(
        body, name=name, in_specs=[_ANY] * n, out_specs=[_ANY] * n,
        out_shape=[jax.ShapeDtypeStruct(f.shape, f.dtype) for f in fs],
        input_output_aliases={k: k for k in range(n)},
        scratch_shapes=[pltpu.SemaphoreType.DMA((n,)), pltpu.SemaphoreType.DMA((n,))],
    )(*fs)


_HBM = pl.BlockSpec(memory_space=pltpu.HBM)
_SEM = pl.BlockSpec(memory_space=pltpu.SEMAPHORE)
_EFFECT = pltpu.SideEffectType.DATAFLOW_SIDE_EFFECTING


def _in_hbm(a):
    return pltpu.with_memory_space_constraint(a, pltpu.HBM)


def split_start(name, srcs, lands, after, copies_fn, n_copies):
    n = len(srcs)

    def body(*refs):
        for cp in copies_fn(refs[:n], refs[n:2 * n], refs[2 * n + 1], refs[2 * n + 2]):
            cp.start()
        refs[-1][...] = jnp.zeros_like(refs[-1])

    bufs = list(srcs) + list(lands)
    outs = pl.pallas_call(
        body, name=name,
        out_shape=(pltpu.SemaphoreType.DMA((n_copies,)), pltpu.SemaphoreType.DMA((n_copies,)),
                   *[pltpu.HBM(a.shape, a.dtype) for a in bufs], jax.ShapeDtypeStruct((8, 128), F32)),
        in_specs=[_HBM] * (2 * n) + [_ANY],
        out_specs=(_SEM, _SEM, *[_HBM] * (2 * n), pl.BlockSpec(memory_space=pltpu.VMEM)),
        input_output_aliases={i: 2 + i for i in range(2 * n)},
        compiler_params=pltpu.CompilerParams(has_side_effects=_EFFECT),
    )(*[_in_hbm(a) for a in bufs], after)
    return outs[0], outs[1], outs[2:2 + n], outs[2 + n:2 + 2 * n], outs[-1]


def split_wait(name, send, recv, srcs, lands, after, copies_fn, which=None):
    n = len(srcs)

    def body(*refs):
        copies = copies_fn(refs[:n], refs[n:2 * n], refs[2 * n], refs[2 * n + 1], which)
        for cp in copies:
            cp.wait_send()
        for cp in copies:
            cp.wait_recv()

    bufs = list(srcs) + list(lands)
    outs = pl.pallas_call(
        body, name=name, out_shape=tuple(pltpu.HBM(a.shape, a.dtype) for a in bufs),
        in_specs=[_HBM] * (2 * n) + [_SEM, _SEM, _ANY], out_specs=tuple([_HBM] * (2 * n)),
        input_output_aliases={i: i for i in range(2 * n)},
        compiler_params=pltpu.CompilerParams(has_side_effects=_EFFECT),
    )(*bufs, send, recv, after)
    return list(outs[:n]), list(outs[n:])


def _gather_copies(s_refs, l_refs, send, recv, which=None):
    x, y, c, chips = _place()
    me = 2 * x + y
    out = []
    for k in (range(len(s_refs)) if which is None else which):
        s, l = s_refs[k], l_refs[k]
        rows = _rows_of(c, s.shape[0] // 2)
        for j, chip in enumerate(chips):
            out.append(pltpu.make_async_remote_copy(src_ref=s.at[rows], dst_ref=l.at[me, rows], send_sem=send.at[4 * k + j],
                                                    recv_sem=recv.at[4 * k + j], device_id=(*chip, c), device_id_type=MESH))
        out.append(pltpu.make_async_remote_copy(src_ref=s, dst_ref=l.at[me], send_sem=send.at[4 * k + 3],
                                                recv_sem=recv.at[4 * k + 3], device_id=(x, y, 1 - c), device_id_type=MESH))
    return out


def _scatter_copies(s_refs, l_refs, send, recv, which=None):
    x, y, c, chips = _place()
    me = 2 * x + y
    return [pltpu.make_async_remote_copy(src_ref=s_refs[k].at[2 * chip[0] + chip[1]], dst_ref=l_refs[k].at[me],
                                         send_sem=send.at[3 * k + j], recv_sem=recv.at[3 * k + j], device_id=(*chip, c),
                                         device_id_type=MESH)
            for k in (range(len(s_refs)) if which is None else which) for j, chip in enumerate(chips)]


def gather_forward(lands, name):
    n = len(lands)

    def body(*refs):
        l_refs, send, recv = refs[n:2 * n], refs[2 * n], refs[2 * n + 1]
        x, y, c, chips = _place()
        copies = []
        for k in range(n):
            rows = _rows_of(c, lands[k].shape[1] // 2)
            for j, chip in enumerate(chips):
                part = l_refs[k].at[2 * chip[0] + chip[1], rows]
                copies.append(pltpu.make_async_remote_copy(src_ref=part, dst_ref=part, send_sem=send.at[3 * k + j],
                                                           recv_sem=recv.at[3 * k + j], device_id=(x, y, 1 - c),
                                                           device_id_type=MESH))
        for cp in copies:
            cp.start()
        for cp in copies:
            cp.wait()

    return pl.pallas_call(
        body, name=name, in_specs=[_ANY] * n, out_specs=[_ANY] * n,
        out_shape=[jax.ShapeDtypeStruct(a.shape, a.dtype) for a in lands],
        input_output_aliases={k: k for k in range(n)},
        scratch_shapes=[pltpu.SemaphoreType.DMA((3 * n,)), pltpu.SemaphoreType.DMA((3 * n,))],
    )(*lands)


def rs_partials(gs, wire, cidx, tag):
    rs = rs_pair_swap(gs)
    groups = {}
    for k, g in enumerate(gs):
        groups.setdefault((g.shape, jnp.dtype(wire[k]).name), []).append(k)
    ps = [None] * len(gs)
    for gi, ks in enumerate(groups.values()):
        outs = rs_pair_add([gs[k] for k in ks], [rs[k] for k in ks], cidx, wire[ks[0]], "grad_pair_add_%s%d" % (tag, gi))
        for k, o in zip(ks, outs):
            ps[k] = o
    return ps


def rs_finish(items, tag):
    cidx = lax.axis_index("c").astype(jnp.int32).reshape(1)
    groups = {}
    for i, it in enumerate(items):
        groups.setdefault((it[0].shape, len(it), it[0].dtype.name), []).append(i)
    fs = [None] * len(items)
    for gi, ids in enumerate(groups.values()):
        outs = rs_chip_sum([q for i in ids for q in items[i]], len(items[ids[0]]), cidx, "grad_chip_sum_%s%d" % (tag, gi))
        for i, o in zip(ids, outs):
            fs[i] = o
    return rs_pair_gather(fs, "grad_pair_gather_" + tag)


def adamw(w, g, m, v, name):
    shape = w.shape
    if w.ndim == 2:
        block, grid, index = shape, (1,), (lambda i: (0, 0))
    else:
        slab = shape[2:]
        unit = 4 * int(np.prod(slab[:-2] or (1,))) * (-(-slab[-1] // 128) * 128)
        if len(slab) >= 2:
            unit *= -(-slab[-2] // 8) * 8
        k = shape[1]
        tr = k
        if k * unit > 2 ** 21:
            tr = max(t for t in range(8, k, 8) if k % t == 0 and t * unit <= 2 ** 21)
        block, grid = (None, tr) + tuple(slab), (shape[0], k // tr)
        index = lambda l, i: (l, i) + (0,) * len(slab)
        if tr < min(k, 64) and len(slab) == 1:
            tc = max(t for t in range(128, slab[0] + 1, 128) if slab[0] % t == 0 and k * t * 4 <= 2 ** 21)
            block, grid = (None, k, tc), (shape[0], slab[0] // tc)
            index = lambda l, i: (l, 0, i)

    def body(w_ref, g_ref, m_ref, v_ref, d_ref, nm_ref, nv_ref):
        g_ = g_ref[...]
        m_new = ADAM_B1 * m_ref[...] + (1.0 - ADAM_B1) * g_
        v_new = ADAM_B2 * v_ref[...] + (1.0 - ADAM_B2) * (g_ * g_)
        m_hat = m_new / (1.0 - ADAM_B1 ** ADAM_STEP)
        v_hat = v_new / (1.0 - ADAM_B2 ** ADAM_STEP)
        d_ref[...] = -ADAM_LR * (m_hat / (jnp.sqrt(v_hat) + ADAM_EPS) + ADAM_WD * w_ref[...])
        nm_ref[...] = m_new
        nv_ref[...] = v_new

    spec = pl.BlockSpec(block, index)
    return pl.pallas_call(
        body, name=name, grid=grid, in_specs=[spec] * 4, out_specs=[spec] * 3,
        out_shape=[jax.ShapeDtypeStruct(shape, F32)] * 3, compiler_params=_cparams(),
    )(w, g, m, v)


_WEIGHTS = ['meta', 'ffn1_w_gate', 'ffn1_w_up', 'ffn1_w_down', 'ln1_g', 'ln1_b', 'w_in', 'mla_q_norm_g', 'mla_w_uq',
            'mla_kv_norm_g', 'mla_w_ukv', 'mla_w_o', 'conv_w', 'conv_b', 'conv_w_out', 's5_a_re', 's5_a_im', 's5_log_dt',
            's5_b_re', 's5_b_im', 's5_c_re', 's5_c_im', 's5_d', 's5_w_glu', 's5_b_glu', 's5_w_out', 'w_o', 'ln2_g', 'ln2_b',
            'ffn2_w_gate', 'ffn2_w_up', 'ffn2_w_down', 'ln3_g', 'ln3_b']


def _pad_to(flat, n):
    return jnp.concatenate([flat, jnp.zeros((n - flat.shape[0],), flat.dtype)])


def _shard_of(full, axis):
    if axis == 1:
        return _shard_cols(full)
    if axis == 'T':
        return full.T.reshape(N_SHARD, full.shape[1] // N_SHARD, full.shape[0])
    return full.reshape(N_SHARD, full.shape[0] // N_SHARD, full.shape[1])


_FFN_KEY = {'gate': 'wg', 'up': 'wu', 'down': 'wd'}


def _pad_rows(a, axis):
    k = a.shape[axis]
    extra = -k % 32
    if not extra:
        return a
    return jnp.pad(a, [(0, extra) if d == axis else (0, 0) for d in range(a.ndim)])


def _step(env):
    w = {n: env[n] for n in _WEIGHTS}
    mom = {n: env['m_' + n] for n in _WEIGHTS}
    var = {n: env['v_' + n] for n in _WEIGHTS}
    cidx = lax.axis_index("c").astype(jnp.int32).reshape(1)
    chip = 2 * lax.axis_index("x") + lax.axis_index("y")
    big_names = [n for n, _ in _BIG]
    nb = len(big_names)

    kept_t = [n for n, a in _BIG if a == 'T']
    own = {n: (jnp.swapaxes(w[n], 1, 2) if n in kept_t else w[n]) for n in big_names}
    first = [n for n in big_names if n.startswith('ffn1')]
    mix = [n for n in big_names if not n.startswith('ffn')]
    last = [n for n in big_names if n.startswith('ffn2')]
    rest = mix + last
    nm, nr = len(mix), len(mix) + len(last)
    src = lambda n, li: _pad_rows(own[n][li].astype(BF16), 0)
    gathered_first, (conv_w_st, meta_st) = all_gather_shards([src(n, 0) for n in first], [w['conv_w'], w['meta']])
    later = [src(n, 0) for n in rest] + [src(n, 1) for n in big_names]
    lands = [lax.empty((N_SHARD,) + s.shape, BF16) for s in later]
    g_send, g_recv, later_t, lands_t, token = split_start("gather_start", later, lands, gathered_first[0], _gather_copies,
                                                          4 * len(later))

    def weights_of(names, st, li, with_small):
        small = None
        if with_small:
            small = {n: w[n][li] for n in _REPL}
            small['conv_w'] = _nat_cols(conv_w_st[:, li])
        return compute_weights({n: a[:, :own[n].shape[1]] for n, a in zip(names, st)}, small)

    x2d = env['x'][0]
    lp = x2d.shape[0] + X0
    tabs = _rope_tables(lp)
    h = jnp.concatenate([jnp.zeros((PAD, D_MODEL), F32), _nat_cols(meta_st), x2d], axis=0) + token[0, 0]
    W0 = weights_of(first, gathered_first, 0, True)
    ffn1 = ffn_fwd(h, h.astype(BF16), W0['wg1'], W0['wu1'], W0['wd1'], W0['ln1_g'], W0['ln1_b'], lp)
    later_t, lands_t = split_wait("gather0_wait", g_send, g_recv, later_t, lands_t, ffn1[0], _gather_copies, range(nm))
    W0.update(weights_of(mix, gather_forward(lands_t[:nm], "gather0_forward"), 0, False))
    h, hb, sv0 = layer_fwd(None, None, W0, tabs, lp, ffn1=ffn1, ffn2=False)
    later_t, lands_t = split_wait("gather0b_wait", g_send, g_recv, later_t, lands_t, h, _gather_copies, range(nm, nr))
    W0.update(weights_of(last, gather_forward(lands_t[nm:nr], "gather0b_forward"), 0, False))
    h, hb, sv0['sv3'] = ffn_fwd(h, hb, W0['wg2'], W0['wu2'], W0['wd2'], W0['ln3_g'], W0['ln3_b'], lp)
    _, lands_t = split_wait("gather1_wait", g_send, g_recv, later_t, lands_t, h, _gather_copies, range(nr, len(later)))
    W1 = weights_of(big_names, gather_forward(lands_t[nr:], "gather1_forward"), 1, True)
    h, hb, sv1 = layer_fwd(h, hb, W1, tabs, lp)
    tgt = jnp.concatenate([jnp.zeros((X0, D_MODEL), F32), env['loss_target'][0]], axis=0)
    dh, loss_part = loss_head(h, tgt, lp)
    loss = lax.psum(loss_part[0, 0], ("x", "y", "c"))

    def shards(G, names):
        full = None if all(n.startswith('ffn') for n in names) else reference_grads(G, ffn=False)

        def one(n, a):
            if n.startswith('ffn'):
                return G[_FFN_KEY[n.split('_')[-1]] + n[3]]
            if n == 'w_in':
                return full['w_in_t'].reshape(N_SHARD, D_IN // N_SHARD, D_MODEL)
            return _shard_of(full[n], a)

        return [_pad_rows(one(n, a), 1) for n, a in _BIG if n in names]

    def scatter_start(name, ps, after):
        qs = [lax.dynamic_update_slice_in_dim(jnp.zeros_like(p), lax.dynamic_slice_in_dim(p, chip, 1, axis=0), chip, axis=0)
              for p in ps]
        return split_start(name, ps, qs, after, _scatter_copies, 3 * len(ps))

    dh, G1 = layer_bwd(dh, sv1, W1, tabs, lp)
    p1 = rs_partials(shards(G1, big_names), [BF16] * nb, cidx, "b")
    s1_send, s1_recv, p1_t, q1_t, token1 = scatter_start("scatter1_start", p1, dh)
    dh, g3 = ffn_bwd(dh, sv0['sv3'], W0['wg2'], W0['wu2'], W0['wd2'], W0['ln3_g'] + token1[0, 0], lp)
    G0 = dict(wg2=g3['wg'], wu2=g3['wu'], wd2=g3['wd'])
    p0l = rs_partials(shards(G0, last), [BF16] * len(last), cidx, "c")
    sl_send, sl_recv, p0l_t, q0l_t, token0l = scatter_start("scatter0b_start", p0l, dh)
    dh, Gm = layer_bwd(dh, sv0, dict(W0, ln2_g=W0['ln2_g'] + token0l[0, 0]), tabs, lp, ffn1=False, ffn2=False)
    G0.update(Gm, ln3_g=g3['ln_g'], ln3_b=g3['ln_b'])
    p0m = rs_partials(shards(G0, mix), [BF16] * nm, cidx, "d")
    sm_send, sm_recv, p0m_t, q0m_t, token0m = scatter_start("scatter0_start", p0m, dh)
    dh, g1 = ffn_bwd(dh, sv0['sv1'], W0['wg1'], W0['wu1'], W0['wd1'], W0['ln1_g'] + token0m[0, 0], lp)
    G0.update(wg1=g1['wg'], wu1=g1['wu'], wd1=g1['wd'], ln1_g=g1['ln_g'], ln1_b=g1['ln_b'])
    _, q1 = split_wait("scatter1_wait", s1_send, s1_recv, p1_t, q1_t, dh, _scatter_copies)
    _, q0_last = split_wait("scatter0b_wait", sl_send, sl_recv, p0l_t, q0l_t, dh, _scatter_copies)
    _, q0_mix = split_wait("scatter0_wait", sm_send, sm_recv, p0m_t, q0m_t, dh, _scatter_copies)
    q0_rest = list(q0_mix) + list(q0_last)
    full = [reference_grads(G0, ffn=False), reference_grads(G1, ffn=False)]

    s_parts = [jnp.stack([full[li][n] for li in range(DEPTH)]).reshape(-1) for n in _REPL + ['conv_w']]
    s_parts.append(dh[PAD:X0].reshape(-1))
    s_sizes = [int(p.shape[0]) for p in s_parts]
    s_rows = -(-sum(s_sizes) // (16 * LANES)) * 16
    g_small = _pad_to(jnp.concatenate(s_parts), s_rows * LANES).reshape(1, s_rows, LANES)
    g_small = jnp.broadcast_to(g_small, (N_SHARD, s_rows, LANES))
    p_last = rs_partials(shards(G0, first) + [g_small], [BF16] * len(first) + [F32], cidx, "a")
    z_send, z_recv, pz_t, qz_t, token_z = scatter_start("scatter_last_start", p_last, dh)

    def step_weights(names, grad):
        out = {}
        for n in names:
            if n in kept_t:
                res = adamw(own[n], grad[n], jnp.swapaxes(mom[n], 1, 2), jnp.swapaxes(var[n], 1, 2), "adamw_" + n)
                out[n] = [jnp.swapaxes(t, 1, 2) for t in [grad[n]] + list(res)]
            else:
                out[n] = [grad[n]] + list(adamw(w[n], grad[n], mom[n], var[n], "adamw_" + n))
        return out

    q1 = dict(zip(big_names, q1))
    q0 = dict(zip(rest, q0_rest))
    q0[rest[0]] = q0[rest[0]] + token_z[0, 0].astype(BF16)
    red = rs_finish([[q0[n], q1[n]] for n in rest], "a")
    done = step_weights(rest, {n: r[:, :own[n].shape[1]] for n, r in zip(rest, red)})
    all_done = jnp.stack([done[n][3][(0,) * done[n][3].ndim] for n in rest])
    _, q_last = split_wait("scatter_last_wait", z_send, z_recv, pz_t, qz_t, all_done, _scatter_copies)
    red = rs_finish([[q, q1[n]] for n, q in zip(first, q_last)] + [[q_last[-1]]], "b")
    f_small = red[-1].reshape(-1)

    grad = {n: r[:, :own[n].shape[1]] for n, r in zip(first, red)}
    off = 0
    for n, sz in zip(_REPL + ['conv_w', 'meta'], s_sizes):
        grad[n] = f_small[off:off + sz]
        off += sz
    for n in _REPL:
        grad[n] = grad[n].reshape(w[n].shape)
    cw = grad['conv_w'].reshape(DEPTH, 3, MIX)
    grad['conv_w'] = lax.dynamic_slice_in_dim(cw, chip * (MIX // N_SHARD), MIX // N_SHARD, axis=2)
    gm = grad['meta'].reshape(N_META, D_MODEL)
    grad['meta'] = lax.dynamic_slice_in_dim(gm, chip * (D_MODEL // N_SHARD), D_MODEL // N_SHARD, axis=1)

    done.update(step_weights([n for n in _WEIGHTS if n not in done], grad))
    return (loss, dh[X0:][None], *[done[n][k] for k in range(4) for n in _WEIGHTS])


def kernel(x, meta, ffn1_w_gate, ffn1_w_up, ffn1_w_down, ln1_g, ln1_b, w_in, mla_q_norm_g, mla_w_uq, mla_kv_norm_g, mla_w_ukv, mla_w_o, conv_w, conv_b, conv_w_out, s5_a_re, s5_a_im, s5_log_dt, s5_b_re, s5_b_im, s5_c_re, s5_c_im, s5_d, s5_w_glu, s5_b_glu, s5_w_out, w_o, ln2_g, ln2_b, ffn2_w_gate, ffn2_w_up, ffn2_w_down, ln3_g, ln3_b, loss_target, m_meta, m_ffn1_w_gate, m_ffn1_w_up, m_ffn1_w_down, m_ln1_g, m_ln1_b, m_w_in, m_mla_q_norm_g, m_mla_w_uq, m_mla_kv_norm_g, m_mla_w_ukv, m_mla_w_o, m_conv_w, m_conv_b, m_conv_w_out, m_s5_a_re, m_s5_a_im, m_s5_log_dt, m_s5_b_re, m_s5_b_im, m_s5_c_re, m_s5_c_im, m_s5_d, m_s5_w_glu, m_s5_b_glu, m_s5_w_out, m_w_o, m_ln2_g, m_ln2_b, m_ffn2_w_gate, m_ffn2_w_up, m_ffn2_w_down, m_ln3_g, m_ln3_b, v_meta, v_ffn1_w_gate, v_ffn1_w_up, v_ffn1_w_down, v_ln1_g, v_ln1_b, v_w_in, v_mla_q_norm_g, v_mla_w_uq, v_mla_kv_norm_g, v_mla_w_ukv, v_mla_w_o, v_conv_w, v_conv_b, v_conv_w_out, v_s5_a_re, v_s5_a_im, v_s5_log_dt, v_s5_b_re, v_s5_b_im, v_s5_c_re, v_s5_c_im, v_s5_d, v_s5_w_glu, v_s5_b_glu, v_s5_w_out, v_w_o, v_ln2_g, v_ln2_b, v_ffn2_w_gate, v_ffn2_w_up, v_ffn2_w_down, v_ln3_g, v_ln3_b):
    return _step(dict(locals()))
```

```python
import functools
import math

import numpy as np
import jax
import jax.numpy as jnp
from jax import lax
from jax.experimental import pallas as pl
from jax.experimental.pallas import tpu as pltpu

F32 = jnp.float32
BF16 = jnp.bfloat16

D_MODEL = 1024
DEPTH = 2
N_META = 16
PAD = 112
X0 = PAD + N_META
N_HEADS = 8
D_NOPE = 64
D_ROPE = 32
D_V = 64
Q_RANK = 384
KV_RANK = 256
MIX = 512
S5_GROUPS = 32
S5_GROUP = 16
S5_STATE = 64
S5_LANES = S5_GROUPS * S5_STATE
D_FF = 2816
N_SHARD = 4
FF_SHARD = D_FF // N_SHARD
D_IN = 5792
P_IN = 6144
ALPHA = (2.0 * DEPTH) ** 0.25
LN_EPS = 1e-5
RMS_EPS = 1e-6
ATT_SCALE = (D_NOPE + D_ROPE) ** -0.5
ROPE_BASE = 10000.0
ADAM_LR, ADAM_B1, ADAM_B2, ADAM_EPS, ADAM_WD, ADAM_STEP = 0.001, 0.9, 0.999, 1e-08, 0.01, 10
SCAN_CHUNK = 128
VMEM_LIMIT = 52 * 2 ** 20
WGRAD = BF16
MESH = pl.DeviceIdType.MESH


def _cparams(**kw):
    return pltpu.CompilerParams(vmem_limit_bytes=VMEM_LIMIT, **kw)


def _tile(n):
    if n <= 1088:
        return n
    for t in (1024, 544, 512, 272, 256, 128):
        if n % t == 0:
            return t
    return n


def _row_tile(lp):
    for t in (544, 272, 128):
        if lp % t == 0:
            return t
    return lp


def _ffn_tile(lp):
    return 1088 if lp % 1088 == 0 else _row_tile(lp)


def _sigmoid(x):
    return 1.0 / (1.0 + jnp.exp(-x))


_GELU_C = math.sqrt(2.0 / math.pi)


def _gelu(x):
    return 0.5 * x * (1.0 + jnp.tanh(_GELU_C * (x + 0.044715 * x * x * x)))


def _gelu_grad(x):
    t = jnp.tanh(_GELU_C * (x + 0.044715 * x * x * x))
    return 0.5 * (1.0 + t) + 0.5 * x * (1.0 - t * t) * _GELU_C * (1.0 + 3.0 * 0.044715 * x * x)


def _dot(a, b, ca, cb, precision=None):
    return lax.dot_general(a, b, (((ca,), (cb,)), ((), ())), preferred_element_type=F32, precision=precision)


def matmul(a, b, *, name, ta=False, tb=False, ab='n', bb='n', res=None, res_scale=1.0, scale=1.0, out_dtype=F32):
    if ta:
        _, K, M = a.shape
    else:
        _, M, K = a.shape
    if tb:
        _, N, K2 = b.shape
    else:
        _, K2, N = b.shape
    assert K == K2, (a.shape, b.shape)
    n_out = max(a.shape[0] if ab == 'o' else 1, b.shape[0] if bb == 'o' else 1)
    n_red = max(a.shape[0] if ab == 'r' else 1, b.shape[0] if bb == 'r' else 1)
    tm, tn = _tile(M), _tile(N)
    tk = K if K <= 2304 else _tile(K)
    nkt = K // tk
    n_steps = n_red * nkt

    def bsel(mode, o, r):
        if mode == 'o':
            return o
        if mode == 'r':
            return r // nkt if nkt > 1 else r
        return 0

    def ksel(r):
        if nkt == 1:
            return 0
        return r % nkt if n_red > 1 else r

    a_map = (lambda o, i, j, r: (bsel(ab, o, r), ksel(r), i)) if ta else (lambda o, i, j, r: (bsel(ab, o, r), i, ksel(r)))
    b_map = (lambda o, i, j, r: (bsel(bb, o, r), j, ksel(r))) if tb else (lambda o, i, j, r: (bsel(bb, o, r), ksel(r), j))
    o_map = lambda o, i, j, r: (o, i, j)
    in_specs = [pl.BlockSpec((None, tk, tm) if ta else (None, tm, tk), a_map),
                pl.BlockSpec((None, tn, tk) if tb else (None, tk, tn), b_map)]
    operands = [a, b]
    if res is not None:
        in_specs.append(pl.BlockSpec((None, tm, tn), o_map))
        operands.append(res)
    has_res = res is not None

    def body(*refs):
        a_ref, b_ref = refs[0], refs[1]
        res_ref = refs[2] if has_res else None
        o_ref = refs[3] if has_res else refs[2]
        part = _dot(a_ref[...].astype(BF16), b_ref[...].astype(BF16), 0 if ta else 1, 1 if tb else 0)

        def finish(acc):
            v = acc if scale == 1.0 else acc * scale
            if has_res:
                v = v + res_scale * res_ref[...].astype(F32)
            o_ref[...] = v.astype(o_ref.dtype)

        if n_steps == 1:
            finish(part)
        else:
            acc_ref = refs[-1]
            r = pl.program_id(3)

            @pl.when(r == 0)
            def _():
                acc_ref[...] = part

            @pl.when(r > 0)
            def _():
                acc_ref[...] += part

            @pl.when(r == n_steps - 1)
            def _():
                finish(acc_ref[...])

    return pl.pallas_call(
        body, name=name,
        grid=(n_out, M // tm, N // tn, n_steps),
        in_specs=in_specs,
        out_specs=pl.BlockSpec((None, tm, tn), o_map),
        out_shape=jax.ShapeDtypeStruct((n_out, M, N), out_dtype),
        scratch_shapes=[pltpu.VMEM((tm, tn), F32)] if n_steps > 1 else [],
        compiler_params=_cparams(),
    )(*operands)


def rowwise(fn, rows, pars, outs, accs=(), *, name, lp):
    tm = _row_tile(lp)
    n_rows, n_pars, n_outs, n_accs = len(rows), len(pars), len(outs), len(accs)
    in_specs = [pl.BlockSpec((tm, w), functools.partial(lambda i, cb: (i, cb), cb=cb)) for _, w, cb in rows]
    in_specs += [pl.BlockSpec(p.shape, functools.partial(lambda i, nd: (0,) * nd, nd=p.ndim)) for p in pars]
    out_specs = [pl.BlockSpec((tm, w), lambda i: (i, 0)) for w, _ in outs]
    out_specs += [pl.BlockSpec(s, functools.partial(lambda i, nd: (0,) * nd, nd=len(s))) for s, _ in accs]
    out_shape = [jax.ShapeDtypeStruct((lp, w), dt) for w, dt in outs]
    out_shape += [jax.ShapeDtypeStruct(s, dt) for s, dt in accs]

    def body(*refs):
        i = pl.program_id(0)
        rv = [r[...] for r in refs[:n_rows]]
        pv = [r[...] for r in refs[n_rows:n_rows + n_pars]]
        o_refs = refs[n_rows + n_pars:n_rows + n_pars + n_outs]
        a_refs = refs[n_rows + n_pars + n_outs:]
        ov, av = fn(i * tm, rv, pv)
        for r, v in zip(o_refs, ov):
            r[...] = v.astype(r.dtype)
        if n_accs:
            @pl.when(i == 0)
            def _():
                for r, v in zip(a_refs, av):
                    r[...] = v.astype(r.dtype)

            @pl.when(i > 0)
            def _():
                for r, v in zip(a_refs, av):
                    r[...] += v.astype(r.dtype)

    res = pl.pallas_call(
        body, name=name, grid=(lp // tm,), in_specs=in_specs, out_specs=out_specs, out_shape=out_shape,
        compiler_params=_cparams(),
    )(*[r[0] for r in rows], *pars)
    return res


def _row_mask(row0, shape):
    return (row0 + lax.broadcasted_iota(jnp.int32, shape, 0)) >= PAD


def ffn_up(hb, wg, wu, lp):
    tm = _ffn_tile(lp)

    def body(h_ref, wg_ref, wu_ref, ab_ref, hid_ref):
        h = h_ref[...]
        a = _dot(h, wg_ref[...], 1, 1)
        b = _dot(h, wu_ref[...], 1, 1)
        ab_ref[0] = a.astype(BF16)
        ab_ref[1] = b.astype(BF16)
        hid_ref[...] = (a * _sigmoid(a) * b).astype(BF16)

    wspec = pl.BlockSpec((None, FF_SHARD, D_MODEL), lambda j, i: (j, 0, 0))
    return pl.pallas_call(
        body, name="ffn_up", grid=(N_SHARD, lp // tm),
        in_specs=[pl.BlockSpec((tm, D_MODEL), lambda j, i: (i, 0)), wspec, wspec],
        out_specs=[pl.BlockSpec((None, 2, tm, FF_SHARD), lambda j, i: (j, 0, i, 0)),
                   pl.BlockSpec((None, tm, FF_SHARD), lambda j, i: (j, i, 0))],
        out_shape=[jax.ShapeDtypeStruct((N_SHARD, 2, lp, FF_SHARD), BF16),
                   jax.ShapeDtypeStruct((N_SHARD, lp, FF_SHARD), BF16)],
        compiler_params=_cparams(),
    )(hb, wg, wu)


def _layer_norm(z, g, b):
    mu = jnp.mean(z, axis=-1, keepdims=True)
    zc = z - mu
    var = jnp.mean(zc * zc, axis=-1, keepdims=True)
    return zc * lax.rsqrt(var + LN_EPS) * g + b


def mm_res_ln(a, w, res, g, b, *, scale, name, lp):
    n_red, _, K = a.shape
    tm = _row_tile(lp)

    def body(a_ref, w_ref, res_ref, g_ref, b_ref, z_ref, h_ref, hb_ref, acc_ref):
        r = pl.program_id(1)
        part = _dot(a_ref[...].astype(BF16), w_ref[...], 1, 0)

        @pl.when(r == 0)
        def _():
            acc_ref[...] = part

        @pl.when(r > 0)
        def _():
            acc_ref[...] += part

        @pl.when(r == n_red - 1)
        def _():
            z = ALPHA * res_ref[...] + scale * acc_ref[...]
            z_ref[...] = z
            hn = _layer_norm(z, g_ref[...], b_ref[...])
            h_ref[...] = hn
            hb_ref[...] = hn.astype(BF16)

    row = pl.BlockSpec((tm, D_MODEL), lambda i, r: (i, 0))
    par = pl.BlockSpec((1, D_MODEL), lambda i, r: (0, 0))
    return pl.pallas_call(
        body, name=name, grid=(lp // tm, n_red),
        in_specs=[pl.BlockSpec((None, tm, K), lambda i, r: (r, i, 0)),
                  pl.BlockSpec((None, K, D_MODEL), lambda i, r: (r, 0, 0)), row, par, par],
        out_specs=[row, row, row],
        out_shape=[jax.ShapeDtypeStruct((lp, D_MODEL), F32), jax.ShapeDtypeStruct((lp, D_MODEL), F32),
                   jax.ShapeDtypeStruct((lp, D_MODEL), BF16)],
        scratch_shapes=[pltpu.VMEM((tm, D_MODEL), F32)],
        compiler_params=_cparams(),
    )(a, w, res, g, b)


def ln_bwd(dh, z, g, *, fscale, name, lp):
    def fn(row0, rv, pv):
        dh_, z_ = rv
        g_, = pv
        mu = jnp.mean(z_, axis=-1, keepdims=True)
        zc = z_ - mu
        rstd = lax.rsqrt(jnp.mean(zc * zc, axis=-1, keepdims=True) + LN_EPS)
        xh = zc * rstd
        dxh = dh_ * g_
        m1 = jnp.mean(dxh, axis=-1, keepdims=True)
        m2 = jnp.mean(dxh * xh, axis=-1, keepdims=True)
        dz = rstd * (dxh - m1 - xh * m2)
        return ((dz, fscale * dz),
                (jnp.sum(dh_ * xh, axis=0, keepdims=True), jnp.sum(dh_, axis=0, keepdims=True)))

    return rowwise(fn, [(dh, D_MODEL, 0), (z, D_MODEL, 0)], [g], [(D_MODEL, F32), (D_MODEL, BF16)],
                   [((1, D_MODEL), F32), ((1, D_MODEL), F32)], name=name, lp=lp)


def ffn_down_bwd(dfb, wd, ab, lp):
    tm = _ffn_tile(lp)

    def body(df_ref, w_ref, ab_ref, da_ref, db_ref):
        dhid = _dot(df_ref[...], w_ref[...], 1, 1)
        a = ab_ref[0].astype(F32)
        b = ab_ref[1].astype(F32)
        sg = _sigmoid(a)
        da_ref[...] = (dhid * b * (sg * (1.0 + a * (1.0 - sg)))).astype(BF16)
        db_ref[...] = (dhid * (a * sg)).astype(BF16)

    ospec = pl.BlockSpec((None, tm, FF_SHARD), lambda j, i: (j, i, 0))
    return pl.pallas_call(
        body, name="ffn_down_bwd", grid=(N_SHARD, lp // tm),
        in_specs=[pl.BlockSpec((tm, D_MODEL), lambda j, i: (i, 0)),
                  pl.BlockSpec((None, FF_SHARD, D_MODEL), lambda j, i: (j, 0, 0)),
                  pl.BlockSpec((None, 2, tm, FF_SHARD), lambda j, i: (j, 0, i, 0))],
        out_specs=[ospec, ospec],
        out_shape=[jax.ShapeDtypeStruct((N_SHARD, lp, FF_SHARD), BF16)] * 2,
        compiler_params=_cparams(),
    )(dfb, wd, ab)


def ffn_dx(da, db, wg, wu, dz, lp):
    tm = _ffn_tile(lp)

    def body(da_ref, db_ref, wg_ref, wu_ref, dz_ref, o_ref, acc_ref):
        j = pl.program_id(1)
        part = _dot(da_ref[...], wg_ref[...], 1, 0) + _dot(db_ref[...], wu_ref[...], 1, 0)

        @pl.when(j == 0)
        def _():
            acc_ref[...] = part

        @pl.when(j > 0)
        def _():
            acc_ref[...] += part

        @pl.when(j == N_SHARD - 1)
        def _():
            o_ref[...] = acc_ref[...] + ALPHA * dz_ref[...]

    aspec = pl.BlockSpec((None, tm, FF_SHARD), lambda i, j: (j, i, 0))
    wspec = pl.BlockSpec((None, FF_SHARD, D_MODEL), lambda i, j: (j, 0, 0))
    row = pl.BlockSpec((tm, D_MODEL), lambda i, j: (i, 0))
    return pl.pallas_call(
        body, name="ffn_dx", grid=(lp // tm, N_SHARD), in_specs=[aspec, aspec, wspec, wspec, row], out_specs=row,
        out_shape=jax.ShapeDtypeStruct((lp, D_MODEL), F32), scratch_shapes=[pltpu.VMEM((tm, D_MODEL), F32)],
        compiler_params=_cparams(),
    )(da, db, wg, wu, dz)


def ffn_fwd(h, hb, wg, wu, wd, g, b, lp):
    ab, hid = ffn_up(hb, wg, wu, lp)
    z, hn, hnb = mm_res_ln(hid, wd, h, g, b, scale=0.5, name="ffn_down_ln", lp=lp)
    return hn, hnb, dict(hb=hb, ab=ab, hid=hid, z=z)


def ffn_bwd(dh, sv, wg, wu, wd, g, lp):
    dz, dfb, dg, db = ln_bwd(dh, sv['z'], g, fscale=0.5, name="ffn_ln_bwd", lp=lp)
    da, dbb = ffn_down_bwd(dfb, wd, sv['ab'], lp)
    d_wd = matmul(sv['hid'], dfb[None], ta=True, ab='o', out_dtype=WGRAD, name="ffn_dwd")
    d_wg = matmul(da, sv['hb'][None], ta=True, ab='o', out_dtype=WGRAD, name="ffn_dwg")
    d_wu = matmul(dbb, sv['hb'][None], ta=True, ab='o', out_dtype=WGRAD, name="ffn_dwu")
    dh_in = ffn_dx(da, dbb, wg, wu, dz, lp)
    return dh_in, dict(wg=d_wg, wu=d_wu, wd=d_wd, ln_g=dg, ln_b=db)


def _rope_tables(lp):
    pos = np.arange(lp, dtype=np.float32) - PAD
    inv = ROPE_BASE ** (-np.arange(0, D_ROPE, 2, dtype=np.float32) / D_ROPE)
    ang = pos[:, None] * inv[None, :]
    cos = np.concatenate([np.cos(ang), np.cos(ang)], axis=1).astype(np.float32)
    sin = np.concatenate([np.sin(ang), np.sin(ang)], axis=1).astype(np.float32)
    rot = np.zeros((D_ROPE, D_ROPE), np.float32)
    half = D_ROPE // 2
    for j in range(half):
        rot[j + half, j] = -1.0
        rot[j, j + half] = 1.0
    return jnp.asarray(cos), jnp.asarray(sin), jnp.asarray(rot)


def _rot(x, rot):
    return _dot(x, rot, 1, 0, precision=lax.Precision.HIGHEST)


def _rms(x, g):
    r = lax.rsqrt(jnp.mean(x * x, axis=-1, keepdims=True) + RMS_EPS)
    return x * r * g


def mla_prep(proj, cos, sin, rot, qg, kvg, lp):
    def fn(row0, rv, pv):
        cq, krb, ckv, c, s = rv
        qg_, kvg_, rot_ = pv
        kr = krb[:, :D_ROPE]
        return ((_rms(cq, qg_), _rms(ckv, kvg_), kr * c + _rot(kr, rot_) * s), ())

    return rowwise(fn, [(proj, Q_RANK, 0), (proj, 128, 3), (proj, KV_RANK, 2), (cos, D_ROPE, 0), (sin, D_ROPE, 0)],
                   [qg, kvg, rot], [(Q_RANK, BF16), (KV_RANK, BF16), (D_ROPE, BF16)], name="mla_prep", lp=lp)


def mla_heads(cqn, ckvn, cos, sin, rot, wqn, wqr, wkn, wv, lp):
    tm = _row_tile(lp)

    def body(cq_ref, ckv_ref, c_ref, s_ref, rot_ref, wqn_ref, wqr_ref, wkn_ref, wv_ref, qn_ref, qr_ref, kn_ref, v_ref):
        cq = cq_ref[...]
        ckv = ckv_ref[...]
        for h in range(N_HEADS):
            qn_ref[h] = _dot(cq, wqn_ref[h], 1, 0).astype(BF16)
            qr = _dot(cq, wqr_ref[h], 1, 0)
            qr_ref[h] = (qr * c_ref[...] + _rot(qr, rot_ref[...]) * s_ref[...]).astype(BF16)
            kn_ref[h] = _dot(ckv, wkn_ref[h], 1, 0).astype(BF16)
            v_ref[h] = _dot(ckv, wv_ref[h], 1, 0).astype(BF16)

    def row(w):
        return pl.BlockSpec((tm, w), lambda i: (i, 0))

    def wspec(k, n):
        return pl.BlockSpec((N_HEADS, k, n), lambda i: (0, 0, 0))

    def ospec(n):
        return pl.BlockSpec((N_HEADS, tm, n), lambda i: (0, i, 0))

    return pl.pallas_call(
        body, name="mla_heads", grid=(lp // tm,),
        in_specs=[row(Q_RANK), row(KV_RANK), row(D_ROPE), row(D_ROPE),
                  pl.BlockSpec((D_ROPE, D_ROPE), lambda i: (0, 0)),
                  wspec(Q_RANK, D_NOPE), wspec(Q_RANK, D_ROPE), wspec(KV_RANK, D_NOPE), wspec(KV_RANK, D_V)],
        out_specs=[ospec(D_NOPE), ospec(D_ROPE), ospec(D_NOPE), ospec(D_V)],
        out_shape=[jax.ShapeDtypeStruct((N_HEADS, lp, D_NOPE), BF16), jax.ShapeDtypeStruct((N_HEADS, lp, D_ROPE), BF16),
                   jax.ShapeDtypeStruct((N_HEADS, lp, D_NOPE), BF16), jax.ShapeDtypeStruct((N_HEADS, lp, D_V), BF16)],
        compiler_params=_cparams(),
    )(cqn, ckvn, cos, sin, rot, wqn, wqr, wkn, wv)


D_QK = D_NOPE + D_ROPE


def _att_probs(q, k, row0, tq, lp):
    s = _dot(q, k, 1, 1) * ATT_SCALE
    qi = row0 + lax.broadcasted_iota(jnp.int32, (tq, lp), 0)
    ki = lax.broadcasted_iota(jnp.int32, (tq, lp), 1)
    s = jnp.where((ki <= qi) & (ki >= PAD), s, -1e30)
    p = jnp.exp(s - jnp.max(s, axis=-1, keepdims=True))
    return p / jnp.sum(p, axis=-1, keepdims=True)


def _att_spec(lp, n):
    return pl.BlockSpec((None, lp, n), lambda h: (h, 0, 0))


def _att_tiles(lp):
    tiles, r = [(0, X0)], X0
    while r < lp:
        tiles.append((r, 256))
        r += 256
    assert r == lp
    return tiles


def attn_fwd(q, k, v, lp):
    def body(q_ref, k_ref, v_ref, o_ref):
        for r0, rows in _att_tiles(lp):
            ke, rq = r0 + rows, slice(r0, r0 + rows)
            p = _att_probs(q_ref[rq, :], k_ref[0:ke, :], r0, rows, ke)
            o_ref[rq, :] = _dot(p.astype(BF16), v_ref[0:ke, :], 1, 0).astype(BF16)

    return pl.pallas_call(
        body, name="attn_fwd", grid=(N_HEADS,),
        in_specs=[_att_spec(lp, D_QK), _att_spec(lp, D_QK), _att_spec(lp, D_V)],
        out_specs=_att_spec(lp, D_V), out_shape=jax.ShapeDtypeStruct((N_HEADS, lp, D_V), BF16),
        compiler_params=_cparams(),
    )(q, k, v)


def attn_bwd(q, k, v, do, lp):
    def body(q_ref, k_ref, v_ref, do_ref, dq_ref, dk_ref, dv_ref):
        dk_ref[...] = jnp.zeros_like(dk_ref)
        dv_ref[...] = jnp.zeros_like(dv_ref)
        for r0, rows in _att_tiles(lp):
            ke, rq = r0 + rows, slice(r0, r0 + rows)
            q_, do_, k_, v_ = q_ref[rq, :], do_ref[rq, :], k_ref[0:ke, :], v_ref[0:ke, :]
            p = _att_probs(q_, k_, r0, rows, ke)
            dp = _dot(do_, v_, 1, 1)
            delta = jnp.sum(p * dp, axis=-1, keepdims=True)
            ds = (p * (dp - delta) * ATT_SCALE).astype(BF16)
            dq_ref[rq, :] = _dot(ds, k_, 1, 0)
            dk_ref[0:ke, :] += _dot(ds, q_, 0, 0)
            dv_ref[0:ke, :] += _dot(p.astype(BF16), do_, 0, 0)

    qk, vv = _att_spec(lp, D_QK), _att_spec(lp, D_V)
    return pl.pallas_call(
        body, name="attn_bwd", grid=(N_HEADS,), in_specs=[qk, qk, vv, vv], out_specs=[qk, qk, vv],
        out_shape=[jax.ShapeDtypeStruct((N_HEADS, lp, D_QK), F32), jax.ShapeDtypeStruct((N_HEADS, lp, D_QK), F32),
                   jax.ShapeDtypeStruct((N_HEADS, lp, D_V), F32)],
        compiler_params=_cparams(),
    )(q, k, v, do)


def mla_heads_bwd(dqn, dqr, dkn, dkr, dv, cos, sin, rot, wqn, wqr, wkn, wv, lp):
    tm = _row_tile(lp)

    def body(dqn_ref, dqr_ref, dkn_ref, dkr_ref, dv_ref, c_ref, s_ref, rot_ref, wqn_ref, wqr_ref, wkn_ref, wv_ref,
             dcq_ref, dckv_ref, dqrp_ref, dkrs_ref):
        dcq = jnp.zeros(dcq_ref.shape, F32)
        dckv = jnp.zeros(dckv_ref.shape, F32)
        dkrs = jnp.zeros(dkrs_ref.shape, F32)
        for h in range(N_HEADS):
            dkrs = dkrs + dkr_ref[h]
            dqr_ = dqr_ref[h]
            dqrp = (dqr_ * c_ref[...] - _rot(dqr_ * s_ref[...], rot_ref[...])).astype(BF16)
            dqrp_ref[h] = dqrp
            dcq = dcq + _dot(dqn_ref[h], wqn_ref[h], 1, 1) + _dot(dqrp, wqr_ref[h], 1, 1)
            dckv = dckv + _dot(dkn_ref[h].astype(BF16), wkn_ref[h], 1, 1) + _dot(dv_ref[h].astype(BF16), wv_ref[h], 1, 1)
        dcq_ref[...] = dcq
        dckv_ref[...] = dckv
        dkrs_ref[...] = dkrs

    def hspec(n):
        return pl.BlockSpec((N_HEADS, tm, n), lambda i: (0, i, 0))

    def row(w):
        return pl.BlockSpec((tm, w), lambda i: (i, 0))

    def wspec(k, n):
        return pl.BlockSpec((N_HEADS, k, n), lambda i: (0, 0, 0))

    return pl.pallas_call(
        body, name="mla_heads_bwd", grid=(lp // tm,),
        in_specs=[hspec(D_NOPE), hspec(D_ROPE), hspec(D_NOPE), hspec(D_ROPE), hspec(D_V), row(D_ROPE), row(D_ROPE),
                  pl.BlockSpec((D_ROPE, D_ROPE), lambda i: (0, 0)),
                  wspec(Q_RANK, D_NOPE), wspec(Q_RANK, D_ROPE), wspec(KV_RANK, D_NOPE), wspec(KV_RANK, D_V)],
        out_specs=[row(Q_RANK), row(KV_RANK), hspec(D_ROPE), row(D_ROPE)],
        out_shape=[jax.ShapeDtypeStruct((lp, Q_RANK), F32), jax.ShapeDtypeStruct((lp, KV_RANK), F32),
                   jax.ShapeDtypeStruct((N_HEADS, lp, D_ROPE), BF16), jax.ShapeDtypeStruct((lp, D_ROPE), F32)],
        compiler_params=_cparams(),
    )(dqn, dqr, dkn, dkr, dv, cos, sin, rot, wqn, wqr, wkn, wv)


def _rms_bwd(dy, x, g):
    r = lax.rsqrt(jnp.mean(x * x, axis=-1, keepdims=True) + RMS_EPS)
    n = x * r
    dn = dy * g
    dx = r * (dn - n * jnp.mean(dn * n, axis=-1, keepdims=True))
    return dx, jnp.sum(dy * n, axis=0, keepdims=True)


def mla_prep_bwd(dcq, dckv, dkr, proj, cos, sin, rot, qg, kvg, lp):
    def fn(row0, rv, pv):
        dcq_, dckv_, dkr_, cq, ckv, c, s = rv
        qg_, kvg_, rot_ = pv
        dxq, dgq = _rms_bwd(dcq_, cq, qg_)
        dxkv, dgkv = _rms_bwd(dckv_, ckv, kvg_)
        dkr_raw = dkr_ * c - _rot(dkr_ * s, rot_)
        return ((dxq, dxkv, dkr_raw), (dgq, dgkv))

    return rowwise(fn, [(dcq, Q_RANK, 0), (dckv, KV_RANK, 0), (dkr, D_ROPE, 0), (proj, Q_RANK, 0), (proj, KV_RANK, 2),
                        (cos, D_ROPE, 0), (sin, D_ROPE, 0)], [qg, kvg, rot],
                   [(Q_RANK, BF16), (KV_RANK, BF16), (D_ROPE, BF16)], [((1, Q_RANK), F32), ((1, KV_RANK), F32)],
                   name="mla_prep_bwd", lp=lp)


def _shift_down(x, d, rows):
    return jnp.where(rows >= d, pltpu.roll(x, d, 0), 0.0)


def _shift_up(x, d, rows, n):
    return jnp.where(rows < n - d, pltpu.roll(x, n - d, 0), 0.0)


_CONV_W = 128
_XB, _BG, _CG = 1024 // _CONV_W, 1536 // _CONV_W, 2048 // _CONV_W


def _conv_specs(lp):
    def pspec(base):
        return pl.BlockSpec((lp, _CONV_W), functools.partial(lambda c, base: (0, base + c), base=base))

    col = pl.BlockSpec((lp, _CONV_W), lambda c: (0, c))
    wspec = pl.BlockSpec((3, _CONV_W), lambda c: (0, c))
    bspec = pl.BlockSpec((1, _CONV_W), lambda c: (0, c))
    return pspec, col, wspec, bspec


def _conv_core(xbar, cg, w, bias, lp):
    rows = lax.broadcasted_iota(jnp.int32, (lp, _CONV_W), 0)
    u = jnp.where(rows >= PAD, cg * xbar, 0.0)
    u1 = _shift_down(u, 1, rows)
    u2 = _shift_down(u, 2, rows)
    y = bias + w[0:1] * u2 + w[1:2] * u1 + w[2:3] * u
    return rows, u, u1, u2, y


def conv_fwd(proj, w, bias, lp):
    pspec, col, wspec, bspec = _conv_specs(lp)

    def body(x_ref, b_ref, c_ref, w_ref, bias_ref, v_ref):
        _, _, _, _, y = _conv_core(x_ref[...], c_ref[...], w_ref[...], bias_ref[...], lp)
        v_ref[...] = (b_ref[...] * y).astype(BF16)

    return pl.pallas_call(
        body, name="conv_fwd", grid=(MIX // _CONV_W,),
        in_specs=[pspec(_XB), pspec(_BG), pspec(_CG), wspec, bspec], out_specs=col,
        out_shape=jax.ShapeDtypeStruct((lp, MIX), BF16), compiler_params=_cparams(),
    )(proj, proj, proj, w, bias)


def conv_bwd(dv, proj, w, bias, lp):
    pspec, col, wspec, bspec = _conv_specs(lp)

    def body(dv_ref, x_ref, b_ref, c_ref, w_ref, bias_ref, dx_ref, db_ref, dc_ref, dw_ref, dbias_ref):
        xbar, cg, w_ = x_ref[...], c_ref[...], w_ref[...]
        rows, u, u1, u2, y = _conv_core(xbar, cg, w_, bias_ref[...], lp)
        dv_ = dv_ref[...]
        db_ref[...] = (dv_ * y).astype(BF16)
        dy = dv_ * b_ref[...]
        dbias_ref[...] = jnp.sum(dy, axis=0, keepdims=True)
        dw_ref[0:1, :] = jnp.sum(dy * u2, axis=0, keepdims=True)
        dw_ref[1:2, :] = jnp.sum(dy * u1, axis=0, keepdims=True)
        dw_ref[2:3, :] = jnp.sum(dy * u, axis=0, keepdims=True)
        du = w_[2:3] * dy + w_[1:2] * _shift_up(dy, 1, rows, lp) + w_[0:1] * _shift_up(dy, 2, rows, lp)
        du = jnp.where(rows >= PAD, du, 0.0)
        dc_ref[...] = (du * xbar).astype(BF16)
        dx_ref[...] = (du * cg).astype(BF16)

    return pl.pallas_call(
        body, name="conv_bwd", grid=(MIX // _CONV_W,),
        in_specs=[col, pspec(_XB), pspec(_BG), pspec(_CG), wspec, bspec],
        out_specs=[col, col, col, wspec, bspec],
        out_shape=[jax.ShapeDtypeStruct((lp, MIX), BF16)] * 3 + [jax.ShapeDtypeStruct((3, MIX), F32),
                                                                jax.ShapeDtypeStruct((1, MIX), F32)],
        compiler_params=_cparams(),
    )(dv, proj, proj, proj, w, bias)


def _s5_disc(a_re, a_im, log_dt, b_re, b_im):
    dt = jnp.exp(log_dt)
    mag = jnp.exp(dt * a_re)
    ab_re, ab_im = mag * jnp.cos(dt * a_im), mag * jnp.sin(dt * a_im)
    den = a_re * a_re + a_im * a_im
    nr, ni = ab_re - 1.0, ab_im
    coef_re = (nr * a_re + ni * a_im) / den
    coef_im = (ni * a_re - nr * a_im) / den
    return ab_re, ab_im, coef_re * b_re - coef_im * b_im, coef_re * b_im + coef_im * b_re


_S5_ROWS = S5_GROUPS * S5_GROUP


def s5_prep(a_re, a_im, log_dt, b_re, b_im):
    def body(ar, ai, ld, br, bi, o0, o1, o2, o3):
        for o, v in zip((o0, o1, o2, o3), _s5_disc(ar[...], ai[...], ld[...], br[...], bi[...])):
            o[...] = v

    return pl.pallas_call(body, name="s5_prep",
                          out_shape=[jax.ShapeDtypeStruct((_S5_ROWS, S5_STATE), F32)] * 4)(a_re, a_im, log_dt, b_re, b_im)


def s5_prep_bwd(a_re, a_im, log_dt, b_re, b_im, d_ab_re, d_ab_im, d_bb_re, d_bb_im, sel):
    def body(ar, ai, ld, br, bi, g0, g1, g2, g3, sel_ref, da_re, da_im, dld, dbr, dbi):
        _, vjp = jax.vjp(_s5_disc, ar[...], ai[...], ld[...], br[...], bi[...])
        c_ar, c_ai, c_ld, c_br, c_bi = vjp((g0[...], g1[...], g2[...], g3[...]))
        s = sel_ref[...]
        hi = lax.Precision.HIGHEST
        da_re[...] = _dot(s, c_ar, 1, 0, precision=hi)
        da_im[...] = _dot(s, c_ai, 1, 0, precision=hi)
        dld[...] = jnp.sum(_dot(s, c_ld, 1, 0, precision=hi), axis=-1, keepdims=True)
        dbr[...] = c_br
        dbi[...] = c_bi

    g = jax.ShapeDtypeStruct((S5_GROUPS, S5_STATE), F32)
    full = jax.ShapeDtypeStruct((_S5_ROWS, S5_STATE), F32)
    return pl.pallas_call(body, name="s5_prep_bwd",
                          out_shape=[g, g, jax.ShapeDtypeStruct((S5_GROUPS, 1), F32), full, full],
                          )(a_re, a_im, log_dt, b_re, b_im, d_ab_re, d_ab_im, d_bb_re, d_bb_im, sel)


_SCAN_W = 128
_SCAN_STEPS = int(math.log2(SCAN_CHUNK))


def _cmul(ar, ai, br, bi):
    return ar * br - ai * bi, ar * bi + ai * br


def _scan_powers(ar, ai, reverse):
    pw = [(ar, ai)]
    for _ in range(_SCAN_STEPS):
        pw.append(_cmul(*pw[-1], *pw[-1]))
    rows = lax.broadcasted_iota(jnp.int32, (SCAN_CHUNK, ar.shape[-1]), 0)
    tr = jnp.broadcast_to(ar, rows.shape)
    ti = jnp.broadcast_to(ai, rows.shape)
    for k in range(_SCAN_STEPS):
        d = 2 ** k
        if reverse:
            live = rows < SCAN_CHUNK - d
            mr, mi = _cmul(tr, ti, _shift_up(tr, d, rows, SCAN_CHUNK), _shift_up(ti, d, rows, SCAN_CHUNK))
        else:
            live = rows >= d
            mr, mi = _cmul(tr, ti, _shift_down(tr, d, rows), _shift_down(ti, d, rows))
        tr = jnp.where(live, mr, tr)
        ti = jnp.where(live, mi, ti)
    return pw, rows, tr, ti


def s5_scan(bu, ab_re, ab_im, lp):
    n_chunks = lp // SCAN_CHUNK

    def body(bu_ref, ar_ref, ai_ref, s_ref):
        ar, ai = ar_ref[...], ai_ref[...]
        pw, rows, tr, ti = _scan_powers(ar, ai, False)

        def chunk(ci, carry):
            cr, cim = carry
            r0 = pl.multiple_of(ci * SCAN_CHUNK, SCAN_CHUNK)
            xr = bu_ref[0, pl.ds(r0, SCAN_CHUNK), :]
            xi = bu_ref[1, pl.ds(r0, SCAN_CHUNK), :]
            for k in range(_SCAN_STEPS):
                d = 2 ** k
                mr, mi = _cmul(pw[k][0], pw[k][1], _shift_down(xr, d, rows), _shift_down(xi, d, rows))
                xr, xi = xr + mr, xi + mi
            mr, mi = _cmul(tr, ti, cr, cim)
            xr, xi = xr + mr, xi + mi
            s_ref[0, pl.ds(r0, SCAN_CHUNK), :] = xr
            s_ref[1, pl.ds(r0, SCAN_CHUNK), :] = xi
            return xr[SCAN_CHUNK - 1:SCAN_CHUNK, :], xi[SCAN_CHUNK - 1:SCAN_CHUNK, :]

        zero = jnp.zeros((1, _SCAN_W), F32)
        lax.fori_loop(0, n_chunks, chunk, (zero, zero))

    spec = pl.BlockSpec((2, lp, _SCAN_W), lambda c: (0, 0, c))
    aspec = pl.BlockSpec((1, _SCAN_W), lambda c: (0, c))
    return pl.pallas_call(
        body, name="s5_scan", grid=(S5_LANES // _SCAN_W,), in_specs=[spec, aspec, aspec], out_specs=spec,
        out_shape=jax.ShapeDtypeStruct((2, lp, S5_LANES), F32), compiler_params=_cparams(),
    )(bu, ab_re, ab_im)


def s5_scan_bwd(ds, s, ab_re, ab_im, lp):
    n_chunks = lp // SCAN_CHUNK

    def body(ds_ref, s_ref, ar_ref, ai_ref, g_ref, da_ref):
        ar, ai = ar_ref[...], -ai_ref[...]
        pw, rows, tr, ti = _scan_powers(ar, ai, True)

        def chunk(k, carry):
            cr, cim, dar, dai = carry
            ci = n_chunks - 1 - k
            r0 = pl.multiple_of(ci * SCAN_CHUNK, SCAN_CHUNK)
            xr = ds_ref[0, pl.ds(r0, SCAN_CHUNK), :]
            xi = ds_ref[1, pl.ds(r0, SCAN_CHUNK), :]
            for j in range(_SCAN_STEPS):
                d = 2 ** j
                mr, mi = _cmul(pw[j][0], pw[j][1], _shift_up(xr, d, rows, SCAN_CHUNK), _shift_up(xi, d, rows, SCAN_CHUNK))
                xr, xi = xr + mr, xi + mi
            mr, mi = _cmul(tr, ti, cr, cim)
            xr, xi = xr + mr, xi + mi
            g_ref[0, pl.ds(r0, SCAN_CHUNK), :] = xr
            g_ref[1, pl.ds(r0, SCAN_CHUNK), :] = xi
            prev0 = pl.multiple_of(jnp.maximum(r0 - 8, 0), 8)
            live = (ci > 0).astype(F32)
            pr = s_ref[0, pl.ds(prev0, 8), :][7:8, :] * live
            pim = s_ref[1, pl.ds(prev0, 8), :][7:8, :] * live
            sr = s_ref[0, pl.ds(r0, SCAN_CHUNK), :]
            si = s_ref[1, pl.ds(r0, SCAN_CHUNK), :]
            sr = jnp.where(rows >= 1, pltpu.roll(sr, 1, 0), pr)
            si = jnp.where(rows >= 1, pltpu.roll(si, 1, 0), pim)
            dar = dar + jnp.sum(xr * sr + xi * si, axis=0, keepdims=True)
            dai = dai + jnp.sum(xi * sr - xr * si, axis=0, keepdims=True)
            return xr[0:1, :], xi[0:1, :], dar, dai

        zero = jnp.zeros((1, _SCAN_W), F32)
        _, _, dar, dai = lax.fori_loop(0, n_chunks, chunk, (zero, zero, zero, zero))
        da_ref[0] = dar
        da_ref[1] = dai

    spec = pl.BlockSpec((2, lp, _SCAN_W), lambda c: (0, 0, c))
    aspec = pl.BlockSpec((1, _SCAN_W), lambda c: (0, c))
    return pl.pallas_call(
        body, name="s5_scan_bwd", grid=(S5_LANES // _SCAN_W,), in_specs=[spec, spec, aspec, aspec],
        out_specs=[spec, pl.BlockSpec((2, 1, _SCAN_W), lambda c: (0, 0, c))],
        out_shape=[jax.ShapeDtypeStruct((2, lp, S5_LANES), F32), jax.ShapeDtypeStruct((2, 1, S5_LANES), F32)],
        compiler_params=_cparams(),
    )(ds, s, ab_re, ab_im)


S5_BLOCKS = 4
_S5_PER = S5_GROUPS // S5_BLOCKS


def _blockdiag(x):
    _, r, c = x.shape
    eye = jnp.eye(_S5_PER, dtype=x.dtype)
    x = x.reshape(S5_BLOCKS, _S5_PER, r, c)
    return (x[:, :, :, None, :] * eye[None, :, None, :, None]).reshape(S5_BLOCKS, _S5_PER * r, _S5_PER * c)


def _blockdiag_extract(m, r, c):
    return jnp.einsum('qgrgc->qgrc', m.reshape(S5_BLOCKS, _S5_PER, r, _S5_PER, c)).reshape(S5_GROUPS, r, c)


def bd_matmul(a, w, *, w_t, reduce, res=None, name):
    _, M, _ = a.shape
    n_w, _, k1, k2 = w.shape
    ka, kout = (k2, k1) if w_t else (k1, k2)
    tm = _row_tile(M)
    n_out, n_red = (1, n_w) if reduce else (n_w, 1)
    has_res = res is not None

    assert n_red <= 2

    def body(*refs):
        a_ref, w_ref = refs[0], refs[1]
        o_ref = refs[3] if has_res else refs[2]
        for q in range(S5_BLOCKS):
            cols = slice(q * kout, (q + 1) * kout)
            part = _dot(a_ref[:, q * ka:(q + 1) * ka].astype(BF16), w_ref[q], 1, 1 if w_t else 0)
            if n_red == 1:
                o_ref[:, cols] = part
            else:
                acc_ref = refs[-1]

                @pl.when(pl.program_id(2) == 0)
                def _():
                    acc_ref[:, cols] = part

                @pl.when(pl.program_id(2) == 1)
                def _():
                    tot = acc_ref[:, cols] + part
                    o_ref[:, cols] = tot + refs[2][:, cols] if has_res else tot

    if reduce:
        a_map, w_map = (lambda o, i, r: (r, i, 0)), (lambda o, i, r: (r, 0, 0, 0))
    else:
        a_map, w_map = (lambda o, i, r: (0, i, 0)), (lambda o, i, r: (o, 0, 0, 0))
    o_map = lambda o, i, r: (o, i, 0)
    in_specs = [pl.BlockSpec((None, tm, S5_BLOCKS * ka), a_map), pl.BlockSpec((None, S5_BLOCKS, k1, k2), w_map)]
    operands = [a, w]
    if has_res:
        in_specs.append(pl.BlockSpec((None, tm, S5_BLOCKS * kout), o_map))
        operands.append(res)
    return pl.pallas_call(
        body, name=name, grid=(n_out, M // tm, n_red), in_specs=in_specs,
        out_specs=pl.BlockSpec((None, tm, S5_BLOCKS * kout), o_map),
        out_shape=jax.ShapeDtypeStruct((n_out, M, S5_BLOCKS * kout), F32),
        scratch_shapes=[pltpu.VMEM((tm, S5_BLOCKS * kout), F32)] if n_red > 1 else [],
        compiler_params=_cparams(),
    )(*operands)


def bd_outer(a, b, name):
    na, M, wa = a.shape
    nb_, _, wb = b.shape
    ka, kb = wa // S5_BLOCKS, wb // S5_BLOCKS
    n_out = max(na, nb_)

    def body(a_ref, b_ref, o_ref):
        o_ref[...] = _dot(a_ref[...].astype(BF16), b_ref[...].astype(BF16), 0, 0)

    return pl.pallas_call(
        body, name=name, grid=(n_out, S5_BLOCKS),
        in_specs=[pl.BlockSpec((None, M, ka), (lambda o, q: (o, 0, q)) if na > 1 else (lambda o, q: (0, 0, q))),
                  pl.BlockSpec((None, M, kb), (lambda o, q: (o, 0, q)) if nb_ > 1 else (lambda o, q: (0, 0, q)))],
        out_specs=pl.BlockSpec((None, None, ka, kb), lambda o, q: (o, q, 0, 0)),
        out_shape=jax.ShapeDtypeStruct((n_out, S5_BLOCKS, ka, kb), F32), compiler_params=_cparams(),
    )(a, b)


def s5_u(proj, lp):
    def fn(row0, rv, pv):
        u, = rv
        return ((jnp.where(_row_mask(row0, u.shape), u, 0.0),), ())

    return rowwise(fn, [(proj, MIX, 5)], [], [(MIX, BF16)], name="s5_u", lp=lp)[0]


def s5_y(ys, proj, d, lp):
    def fn(row0, rv, pv):
        ys_, u = rv
        y = ys_ + pv[0] * u
        return ((y, _gelu(y)), ())

    return rowwise(fn, [(ys, MIX, 0), (proj, MIX, 5)], [d], [(MIX, F32), (MIX, BF16)], name="s5_y", lp=lp)


def s5_glu(z, y, b, lp):
    def fn(row0, rv, pv):
        z_, y_ = rv
        return ((_gelu(y_) * _sigmoid(z_ + pv[0]),), ())

    return rowwise(fn, [(z, MIX, 0), (y, MIX, 0)], [b], [(MIX, BF16)], name="s5_glu", lp=lp)[0]


def s5_glu_bwd(dgl, z, y, b, lp):
    def fn(row0, rv, pv):
        dgl_, z_, y_ = rv
        sg = _sigmoid(z_ + pv[0])
        dz = dgl_ * _gelu(y_) * sg * (1.0 - sg)
        return ((dgl_ * sg, dz), (jnp.sum(dz, axis=0, keepdims=True),))

    return rowwise(fn, [(dgl, MIX, 0), (z, MIX, 0), (y, MIX, 0)], [b], [(MIX, F32), (MIX, BF16)], [((1, MIX), F32)],
                   name="s5_glu_bwd", lp=lp)


def s5_y_bwd(dyg, y, proj, d, lp):
    def fn(row0, rv, pv):
        dyg_, y_, u = rv
        dy = dyg_ * _gelu_grad(y_)
        return ((dy, dy * pv[0]), (jnp.sum(dy * u, axis=0, keepdims=True),))

    return rowwise(fn, [(dyg, MIX, 0), (y, MIX, 0), (proj, MIX, 5)], [d], [(MIX, BF16), (MIX, F32)], [((1, MIX), F32)],
                   name="s5_y_bwd", lp=lp)


def s5_du(du, lp):
    def fn(row0, rv, pv):
        return ((jnp.where(_row_mask(row0, rv[0].shape), rv[0], 0.0),), ())

    return rowwise(fn, [(du, MIX, 0)], [], [(MIX, BF16)], name="s5_du", lp=lp)[0]


def merge_fwd(proj, ya, yb, yc, lp):
    def fn(row0, rv, pv):
        g0, g1, g2, a, b, c = rv
        return ((_sigmoid(g0) * a + _sigmoid(g1) * b + _sigmoid(g2) * c,), ())

    return rowwise(fn, [(proj, D_MODEL, 3), (proj, D_MODEL, 4), (proj, D_MODEL, 5), (ya, D_MODEL, 0), (yb, D_MODEL, 0),
                        (yc, D_MODEL, 0)], [], [(D_MODEL, BF16)], name="merge_fwd", lp=lp)[0]


def merge_bwd(dmix, proj, ya, yb, yc, lp):
    def fn(row0, rv, pv):
        dm, g0, g1, g2, a, b, c = rv
        outs_y, outs_g = [], []
        for g, yv in ((g0, a), (g1, b), (g2, c)):
            sg = _sigmoid(g)
            outs_y.append(dm * sg)
            outs_g.append(dm * yv * sg * (1.0 - sg))
        return (tuple(outs_y) + tuple(outs_g), ())

    return rowwise(fn, [(dmix, D_MODEL, 0), (proj, D_MODEL, 3), (proj, D_MODEL, 4), (proj, D_MODEL, 5),
                        (ya, D_MODEL, 0), (yb, D_MODEL, 0), (yc, D_MODEL, 0)], [], [(D_MODEL, BF16)] * 6,
                   name="merge_bwd", lp=lp)


def loss_head(h, tgt, lp):
    def fn(row0, rv, pv):
        h_, t_ = rv
        live = (row0 + lax.broadcasted_iota(jnp.int32, h_.shape, 0)) >= X0
        diff = jnp.where(live, h_ - t_, 0.0)
        ssq = jnp.sum(jnp.sum(diff * diff, axis=1, keepdims=True), axis=0, keepdims=True)
        return ((diff * (1.0 / D_MODEL),), (ssq * (0.5 / D_MODEL),))

    return rowwise(fn, [(h, D_MODEL, 0), (tgt, D_MODEL, 0)], [], [(D_MODEL, F32)], [((1, 1), F32)], name="loss_head", lp=lp)


def _s5_consts(W):
    ab_re_rep, ab_im_rep, bb_re, bb_im = s5_prep(W['s5_a_re'], W['s5_a_im'], W['s5_log_dt'], W['s5_b_re'], W['s5_b_im'])
    pick = lambda t: t.reshape(S5_GROUPS, S5_GROUP, S5_STATE)[:, 0].reshape(1, S5_LANES)
    bb = jnp.stack([_blockdiag(bb_re.reshape(S5_GROUPS, S5_GROUP, S5_STATE)),
                    _blockdiag(bb_im.reshape(S5_GROUPS, S5_GROUP, S5_STATE))]).astype(BF16)
    return pick(ab_re_rep), pick(ab_im_rep), bb


def layer_fwd(h, hb, W, tabs, lp, ffn1=None, ffn2=True):
    cos, sin, rot = tabs
    h1, h1b, sv1 = ffn1 if ffn1 is not None else ffn_fwd(h, hb, W['wg1'], W['wu1'], W['wd1'], W['ln1_g'], W['ln1_b'], lp)
    proj = matmul(h1b[None], W['w_in'][None], tb=True, name="proj")[0]
    cqn, ckvn, kr = mla_prep(proj, cos, sin, rot, W['q_norm_g'], W['kv_norm_g'], lp)
    qn, qr, kn, v = mla_heads(cqn, ckvn, cos, sin, rot, W['wqn'], W['wqr'], W['wkn'], W['wv'], lp)
    q96 = jnp.concatenate([qn, qr], axis=2)
    k96 = jnp.concatenate([kn, jnp.broadcast_to(kr[None], (N_HEADS, lp, D_ROPE))], axis=2)
    o = attn_fwd(q96, k96, v, lp)
    ya = matmul(o, W['mla_wo'], ab='r', bb='r', name="mla_out")[0]
    vconv = conv_fwd(proj, W['conv_w'], W['conv_b'], lp)
    yb = matmul(vconv[None], W['conv_wout'][None], name="conv_out")[0]
    ub = s5_u(proj, lp)
    ab_re, ab_im, bb = _s5_consts(W)
    bu = bd_matmul(ub[None], bb, w_t=False, reduce=False, name="s5_bu")
    s = s5_scan(bu, ab_re, ab_im, lp)
    ys = bd_matmul(s, W['s5_ct'], w_t=False, reduce=True, name="s5_cs")[0]
    y, ygb = s5_y(ys, proj, W['s5_d'], lp)
    zg = matmul(ygb[None], W['s5_wglu'][None], name="s5_glu_mm")[0]
    glb = s5_glu(zg, y, W['s5_b_glu'], lp)
    yc = matmul(glb[None], W['s5_wout'][None], name="s5_out")[0]
    mixed = merge_fwd(proj, ya, yb, yc, lp)
    z2, h2, h2b = mm_res_ln(mixed[None], W['w_o'][None], h1, W['ln2_g'], W['ln2_b'], scale=1.0, name="wo_ln", lp=lp)
    sv = dict(sv1=sv1, h1b=h1b, proj=proj, cqn=cqn, ckvn=ckvn, q96=q96, k96=k96, v=v, o=o, ya=ya,
              vconv=vconv, yb=yb, ub=ub, ab_re=ab_re, ab_im=ab_im, bb=bb, s=s, y=y, ygb=ygb, zg=zg, glb=glb, yc=yc,
              mixed=mixed, z2=z2)
    if not ffn2:
        return h2, h2b, sv
    h3, h3b, sv['sv3'] = ffn_fwd(h2, h2b, W['wg2'], W['wu2'], W['wd2'], W['ln3_g'], W['ln3_b'], lp)
    return h3, h3b, sv


def layer_bwd(dh3, sv, W, tabs, lp, ffn1=True, ffn2=True):
    cos, sin, rot = tabs
    proj = sv['proj']
    G = {}
    dh2 = dh3
    if ffn2:
        dh2, g3 = ffn_bwd(dh3, sv['sv3'], W['wg2'], W['wu2'], W['wd2'], W['ln3_g'], lp)
        G.update(wg2=g3['wg'], wu2=g3['wu'], wd2=g3['wd'], ln3_g=g3['ln_g'], ln3_b=g3['ln_b'])
    dz2, dz2b, G['ln2_g'], G['ln2_b'] = ln_bwd(dh2, sv['z2'], W['ln2_g'], fscale=1.0, name="wo_ln_bwd", lp=lp)
    dmix = matmul(dz2b[None], W['w_o'][None], tb=True, name="wo_dx")[0]
    G['w_o'] = matmul(sv['mixed'][None], dz2b[None], ta=True, out_dtype=WGRAD, name="wo_dw")[0]
    dya, dyb, dyc, dg0, dg1, dg2 = merge_bwd(dmix, proj, sv['ya'], sv['yb'], sv['yc'], lp)
    dgl = matmul(dyc[None], W['s5_wout'][None], tb=True, name="s5_out_dx")[0]
    G['s5_wout'] = matmul(sv['glb'][None], dyc[None], ta=True, out_dtype=WGRAD, name="s5_out_dw")[0]
    t1, dzb, G['s5_b_glu'] = s5_glu_bwd(dgl, sv['zg'], sv['y'], W['s5_b_glu'], lp)
    dyg = matmul(dzb[None], W['s5_wglu'][None], tb=True, res=t1[None], name="s5_glu_dx")[0]
    G['s5_wglu'] = matmul(sv['ygb'][None], dzb[None], ta=True, out_dtype=WGRAD, name="s5_glu_dw")[0]
    dyb_, du_d, G['s5_d'] = s5_y_bwd(dyg, sv['y'], proj, W['s5_d'], lp)
    ds = bd_matmul(dyb_[None], W['s5_ct'], w_t=True, reduce=False, name="s5_cs_dx")
    G['s5_ct'] = bd_outer(sv['s'], dyb_[None], "s5_cs_dw")
    g_adj, d_ab = s5_scan_bwd(ds, sv['s'], sv['ab_re'], sv['ab_im'], lp)
    du = bd_matmul(g_adj, sv['bb'], w_t=True, reduce=True, res=du_d[None], name="s5_bu_dx")[0]
    d_bb = bd_outer(sv['ub'][None], g_adj, "s5_bu_dw")
    du_b = s5_du(du, lp)
    onehot = (jnp.arange(S5_GROUP) == 0).astype(F32)
    spread = lambda t: (t.reshape(S5_GROUPS, 1, S5_STATE) * onehot[None, :, None]).reshape(_S5_ROWS, S5_STATE)
    take = lambda t: _blockdiag_extract(t, S5_GROUP, S5_STATE).reshape(_S5_ROWS, S5_STATE)
    sel = jnp.kron(jnp.eye(S5_GROUPS, dtype=F32), jnp.ones((1, S5_GROUP), F32))
    (G['s5_a_re'], G['s5_a_im'], G['s5_log_dt'], G['s5_b_re'], G['s5_b_im']) = s5_prep_bwd(
        W['s5_a_re'], W['s5_a_im'], W['s5_log_dt'], W['s5_b_re'], W['s5_b_im'],
        spread(d_ab[0]), spread(d_ab[1]), take(d_bb[0]), take(d_bb[1]), sel)
    dv = matmul(dyb[None], W['conv_wout'][None], tb=True, name="conv_out_dx")[0]
    G['conv_wout'] = matmul(sv['vconv'][None], dyb[None], ta=True, out_dtype=WGRAD, name="conv_out_dw")[0]
    dxbar, dbg, dcg, G['conv_w'], G['conv_b'] = conv_bwd(dv, proj, W['conv_w'], W['conv_b'], lp)
    do = matmul(dya[None], W['mla_wo'], tb=True, bb='o', out_dtype=BF16, name="mla_out_dx")
    G['mla_wo'] = matmul(sv['o'], dya[None], ta=True, ab='o', out_dtype=WGRAD, name="mla_out_dw")
    dq96, dk96, dvv = attn_bwd(sv['q96'], sv['k96'], sv['v'], do, lp)
    dqn, dqr = dq96[:, :, :D_NOPE].astype(BF16), dq96[:, :, D_NOPE:]
    dkn, dkr_heads = dk96[:, :, :D_NOPE], dk96[:, :, D_NOPE:]
    dcq, dckv, dqrp, dkr = mla_heads_bwd(dqn, dqr, dkn, dkr_heads, dvv, cos, sin, rot, W['wqn'], W['wqr'], W['wkn'], W['wv'], lp)
    G['wqn'] = matmul(sv['cqn'][None], dqn, ta=True, bb='o', out_dtype=WGRAD, name="mla_dwqn")
    G['wqr'] = matmul(sv['cqn'][None], dqrp, ta=True, bb='o', out_dtype=WGRAD, name="mla_dwqr")
    G['wkn'] = matmul(sv['ckvn'][None], dkn, ta=True, bb='o', out_dtype=WGRAD, name="mla_dwkn")
    G['wv'] = matmul(sv['ckvn'][None], dvv, ta=True, bb='o', out_dtype=WGRAD, name="mla_dwv")
    dcq_raw, dckv_raw, dkr_raw, G['q_norm_g'], G['kv_norm_g'] = mla_prep_bwd(
        dcq, dckv, dkr, proj, cos, sin, rot, W['q_norm_g'], W['kv_norm_g'], lp)
    zeros = lambda n: jnp.zeros((lp, n), BF16)
    dproj = jnp.concatenate([dcq_raw, dkr_raw, zeros(96), dckv_raw, zeros(256), dxbar, dbg, dcg, du_b, dg0, dg1, dg2], axis=1)
    dh1 = matmul(dproj[None], W['w_in'][None], res=dz2[None], res_scale=ALPHA, name="proj_dx")[0]
    G['w_in'] = matmul(dproj[None], sv['h1b'][None], ta=True, out_dtype=WGRAD, name="proj_dw")[0]
    if not ffn1:
        return dh1, G
    dh0, g1 = ffn_bwd(dh1, sv['sv1'], W['wg1'], W['wu1'], W['wd1'], W['ln1_g'], lp)
    G.update(wg1=g1['wg'], wu1=g1['wu'], wd1=g1['wd'], ln1_g=g1['ln_g'], ln1_b=g1['ln_b'])
    return dh0, G


def _nat_cols(st):
    return jnp.transpose(st, (1, 0, 2)).reshape(st.shape[1], -1)


def _shard_cols(nat):
    k, n = nat.shape
    return jnp.transpose(nat.reshape(k, N_SHARD, n // N_SHARD), (1, 0, 2))


def _win_pad(wt):
    z = lambda n: jnp.zeros((n, wt.shape[1]), wt.dtype)
    return jnp.concatenate([wt[0:384], wt[640:672], z(96), wt[384:640], z(256), wt[672:]], axis=0)


def _win_unpad(wp):
    return jnp.concatenate([wp[0:384], wp[512:768], wp[384:416], wp[1024:]], axis=0)


_BIG = [('ffn1_w_gate', 'T'), ('ffn1_w_up', 'T'), ('ffn1_w_down', 0), ('w_in', 'T'), ('mla_w_uq', 1), ('mla_w_ukv', 1),
        ('mla_w_o', 1), ('conv_w_out', 1), ('s5_w_glu', 0), ('s5_w_out', 1), ('w_o', 0),
        ('ffn2_w_gate', 'T'), ('ffn2_w_up', 'T'), ('ffn2_w_down', 0)]
_REPL = ['ln1_g', 'ln1_b', 'mla_q_norm_g', 'mla_kv_norm_g', 'conv_b', 's5_a_re', 's5_a_im', 's5_log_dt', 's5_b_re',
         's5_b_im', 's5_c_re', 's5_c_im', 's5_d', 's5_b_glu', 'ln2_g', 'ln2_b', 'ln3_g', 'ln3_b']


def compute_weights(st, small):
    W = {}
    for t in ('1', '2'):
        if 'ffn%s_w_gate' % t in st:
            W['wg' + t], W['wu' + t], W['wd' + t] = (st['ffn%s_w_%s' % (t, p)] for p in ('gate', 'up', 'down'))
    if 'w_in' in st:
        W.update(_mixer_weights(st))
    if small is not None:
        W.update(_small_weights(small))
    return W


def _mixer_weights(st):
    W = {}
    W['w_in'] = _win_pad(st['w_in'].reshape(D_IN, D_MODEL))
    uq = jnp.transpose(_nat_cols(st['mla_w_uq']).reshape(Q_RANK, N_HEADS, D_NOPE + D_ROPE), (1, 0, 2))
    W['wqn'], W['wqr'] = uq[:, :, :D_NOPE], uq[:, :, D_NOPE:]
    ukv = jnp.transpose(_nat_cols(st['mla_w_ukv']).reshape(KV_RANK, N_HEADS, D_NOPE + D_V), (1, 0, 2))
    W['wkn'], W['wv'] = ukv[:, :, :D_NOPE], ukv[:, :, D_NOPE:]
    W['mla_wo'] = _nat_cols(st['mla_w_o']).reshape(N_HEADS, D_V, D_MODEL)
    W['conv_wout'] = _nat_cols(st['conv_w_out'])
    W['s5_wglu'] = st['s5_w_glu'].reshape(MIX, MIX)
    W['s5_wout'] = _nat_cols(st['s5_w_out'])
    W['w_o'] = st['w_o'].reshape(D_MODEL, D_MODEL)
    return W


def _small_weights(small):
    W = {}
    W['conv_w'] = small['conv_w']
    for n in ('ln1_g', 'ln1_b', 'ln2_g', 'ln2_b', 'ln3_g', 'ln3_b', 'conv_b', 's5_b_glu'):
        W[n] = small[n].reshape(1, -1)
    W['q_norm_g'] = small['mla_q_norm_g'].reshape(1, -1)
    W['kv_norm_g'] = small['mla_kv_norm_g'].reshape(1, -1)
    W['s5_d'] = small['s5_d'].reshape(1, MIX)
    rep = lambda t: jnp.repeat(t, S5_GROUP, axis=0)
    W['s5_a_re'], W['s5_a_im'] = rep(small['s5_a_re']), rep(small['s5_a_im'])
    W['s5_log_dt'] = jnp.broadcast_to(rep(small['s5_log_dt'].reshape(S5_GROUPS, 1)), (_S5_ROWS, S5_STATE))
    tr = lambda t: jnp.transpose(t, (0, 2, 1)).reshape(_S5_ROWS, S5_STATE)
    W['s5_b_re'], W['s5_b_im'] = tr(small['s5_b_re']), tr(small['s5_b_im'])
    ct = lambda t: _blockdiag(jnp.transpose(t, (0, 2, 1)))
    W['s5_ct'] = jnp.stack([ct(small['s5_c_re']), -ct(small['s5_c_im'])]).astype(BF16)
    return W


def reference_grads(G, ffn=True):
    R = {}
    for t in ('1', '2') if ffn else ():
        R['ffn%s_w_gate' % t] = G['wg' + t].reshape(D_FF, D_MODEL).T
        R['ffn%s_w_up' % t] = G['wu' + t].reshape(D_FF, D_MODEL).T
        R['ffn%s_w_down' % t] = G['wd' + t].reshape(D_FF, D_MODEL)
    R['w_in_t'] = _win_unpad(G['w_in'])
    if ffn:
        R['w_in'] = R['w_in_t'].T
    R['mla_w_uq'] = jnp.transpose(jnp.concatenate([G['wqn'], G['wqr']], axis=2), (1, 0, 2)).reshape(Q_RANK, -1)
    R['mla_w_ukv'] = jnp.transpose(jnp.concatenate([G['wkn'], G['wv']], axis=2), (1, 0, 2)).reshape(KV_RANK, -1)
    R['mla_w_o'] = G['mla_wo'].reshape(N_HEADS * D_V, D_MODEL)
    R['conv_w'], R['conv_w_out'] = G['conv_w'], G['conv_wout']
    R['s5_w_glu'], R['s5_w_out'], R['w_o'] = G['s5_wglu'], G['s5_wout'], G['w_o']
    for n in ('ln1_g', 'ln1_b', 'ln2_g', 'ln2_b', 'ln3_g', 'ln3_b', 'conv_b', 's5_b_glu'):
        if n in G:
            R[n] = G[n].reshape(-1)
    R['mla_q_norm_g'], R['mla_kv_norm_g'] = G['q_norm_g'].reshape(-1), G['kv_norm_g'].reshape(-1)
    R['s5_d'] = G['s5_d'].reshape(S5_GROUPS, S5_GROUP)
    R['s5_a_re'], R['s5_a_im'], R['s5_log_dt'] = G['s5_a_re'], G['s5_a_im'], G['s5_log_dt'].reshape(-1)
    untr = lambda t: jnp.transpose(t.reshape(S5_GROUPS, S5_GROUP, S5_STATE), (0, 2, 1))
    R['s5_b_re'], R['s5_b_im'] = untr(G['s5_b_re']), untr(G['s5_b_im'])
    unct = lambda t: jnp.transpose(_blockdiag_extract(t, S5_STATE, S5_GROUP), (0, 2, 1))
    R['s5_c_re'], R['s5_c_im'] = unct(G['s5_ct'][0]), -unct(G['s5_ct'][1])
    return R


_ANY = pl.BlockSpec(memory_space=pl.ANY)
LANES = 1024


def _place():
    x, y, c = lax.axis_index("x"), lax.axis_index("y"), lax.axis_index("c")
    chips = [(1 - x, y), (x, 1 - y), (1 - x, 1 - y)]
    return x, y, c, chips


def _rows_of(c, half):
    return pl.ds(pl.multiple_of(c * half, 8), half)


def all_gather_shards(srcs, exact):
    n, m = len(srcs), len(exact)
    halves = [s.shape[0] // 2 for s in srcs]

    def body(*refs):
        s_refs, e_refs = refs[:n], refs[n:n + m]
        o_refs, eo_refs = refs[n + m:2 * n + m], refs[2 * n + m:2 * n + 2 * m]
        send, recv, esend, erecv, osend, orecv, lsem = refs[2 * n + 2 * m:]
        x, y, c, chips = _place()
        me = 2 * x + y
        sibling = (x, y, 1 - c)
        own = [pltpu.make_async_remote_copy(src_ref=s_refs[k], dst_ref=o_refs[k].at[me], send_sem=osend.at[k],
                                            recv_sem=orecv.at[k], device_id=sibling, device_id_type=MESH) for k in range(n)]
        local = [pltpu.make_async_copy(e_refs[k], eo_refs[k].at[me], lsem.at[k]) for k in range(m)]
        for cp in own + local:
            cp.start()

        def copy(k, s, src, idx, half_c, to):
            return pltpu.make_async_remote_copy(
                src_ref=src, dst_ref=o_refs[k].at[idx, _rows_of(half_c, halves[k])], send_sem=send.at[6 * k + s],
                recv_sem=recv.at[6 * k + s], device_id=to, device_id_type=MESH)

        def ecopy(k, j, idx, to):
            return pltpu.make_async_remote_copy(src_ref=e_refs[k], dst_ref=eo_refs[k].at[idx], send_sem=esend.at[3 * k + j],
                                                recv_sem=erecv.at[3 * k + j], device_id=to, device_id_type=MESH)

        sends = []
        for k in range(n):
            mine = s_refs[k].at[_rows_of(c, halves[k])]
            sends += [copy(k, j, mine, me, c, (*chip, c)) for j, chip in enumerate(chips)]
        for k in range(m):
            sends += [ecopy(k, j, me, (*chip, c)) for j, chip in enumerate(chips)]
        for cp in sends:
            cp.start()
        for j, chip in enumerate(chips):
            idx = 2 * chip[0] + chip[1]
            for k in range(n):
                landed = o_refs[k].at[idx, _rows_of(c, halves[k])]
                copy(k, j, landed, idx, c, sibling).wait_recv()
                fwd = copy(k, 3 + j, landed, idx, c, sibling)
                fwd.start()
                sends.append(fwd)
        for j, chip in enumerate(chips):
            idx = 2 * chip[0] + chip[1]
            for k in range(n):
                copy(k, 3 + j, s_refs[k].at[_rows_of(c, halves[k])], idx, 1 - c, sibling).wait_recv()
            for k in range(m):
                ecopy(k, j, idx, sibling).wait_recv()
        for cp in sends:
            cp.wait_send()
        for cp in own + local:
            cp.wait()

    outs = pl.pallas_call(
        body, name="all_gather_weights", in_specs=[_ANY] * (n + m), out_specs=[_ANY] * (n + m),
        out_shape=[jax.ShapeDtypeStruct((N_SHARD,) + a.shape, a.dtype) for a in list(srcs) + list(exact)],
        scratch_shapes=[pltpu.SemaphoreType.DMA((6 * n,)), pltpu.SemaphoreType.DMA((6 * n,)),
                        pltpu.SemaphoreType.DMA((3 * m,)), pltpu.SemaphoreType.DMA((3 * m,)),
                        pltpu.SemaphoreType.DMA((n,)), pltpu.SemaphoreType.DMA((n,)), pltpu.SemaphoreType.DMA((m,))],
    )(*srcs, *exact)
    return outs[:n], outs[n:]


def rs_pair_swap(gs):
    n = len(gs)

    def body(*refs):
        g_refs, r_refs, send, recv = refs[:n], refs[n:2 * n], refs[2 * n], refs[2 * n + 1]
        x, y, c, _ = _place()
        copies = [pltpu.make_async_remote_copy(
            src_ref=g_refs[k].at[pl.ds(0, N_SHARD), _rows_of(1 - c, gs[k].shape[1] // 2)], dst_ref=r_refs[k],
            send_sem=send.at[k], recv_sem=recv.at[k], device_id=(x, y, 1 - c), device_id_type=MESH) for k in range(n)]
        for cp in copies:
            cp.start()
        for cp in copies:
            cp.wait()

    return pl.pallas_call(
        body, name="grad_pair_swap", in_specs=[_ANY] * n, out_specs=[_ANY] * n,
        out_shape=[jax.ShapeDtypeStruct((N_SHARD, g.shape[1] // 2, g.shape[2]), g.dtype) for g in gs],
        scratch_shapes=[pltpu.SemaphoreType.DMA((n,)), pltpu.SemaphoreType.DMA((n,))],
    )(*gs)


def _group_tile(half, n_cols, n_arrays):
    budget = (20 * 2 ** 20) // (6 * n_arrays)
    fits = [t for t in range(8, half + 1, 8) if half % t == 0 and t * n_cols * 4 <= budget]
    return max(fits) if fits else 8


def rs_pair_add(gs, rs, cidx, out_dtype, name):
    n = len(gs)
    _, K, cols = gs[0].shape
    half = K // 2
    tr = _group_tile(half, cols, n)
    nb = half // tr

    def body(c_ref, *refs):
        for g_ref, r_ref, o_ref in zip(refs[:n], refs[n:2 * n], refs[2 * n:]):
            o_ref[...] = (g_ref[...].astype(F32) + r_ref[...].astype(F32)).astype(out_dtype)

    gspec = pl.BlockSpec((None, tr, cols), lambda j, i, c: (j, c[0] * nb + i, 0))
    rspec = pl.BlockSpec((None, tr, cols), lambda j, i, c: (j, i, 0))
    return pl.pallas_call(
        body, name=name,
        grid_spec=pltpu.PrefetchScalarGridSpec(num_scalar_prefetch=1, grid=(N_SHARD, nb), in_specs=[gspec] * n + [rspec] * n,
                                               out_specs=[rspec] * n),
        out_shape=[jax.ShapeDtypeStruct((N_SHARD, half, cols), out_dtype)] * n,
        compiler_params=_cparams(),
    )(cidx, *gs, *rs)


def rs_chip_sum(qs, nl, cidx, name):
    n = len(qs)
    _, half, cols = qs[0].shape
    tr = _group_tile(half, cols, n)
    nb = half // tr

    def body(c_ref, *refs):
        for k, q_ref in enumerate(refs[:n]):
            o_ref = refs[n + k // nl]
            o_ref[k % nl] = ((q_ref[0].astype(F32) + q_ref[1].astype(F32)) + q_ref[2].astype(F32)) + q_ref[3].astype(F32)

    return pl.pallas_call(
        body, name=name,
        grid_spec=pltpu.PrefetchScalarGridSpec(
            num_scalar_prefetch=1, grid=(nb,),
            in_specs=[pl.BlockSpec((N_SHARD, tr, cols), lambda i, c: (0, i, 0))] * n,
            out_specs=[pl.BlockSpec((nl, tr, cols), lambda i, c: (0, c[0] * nb + i, 0))] * (n // nl)),
        out_shape=[jax.ShapeDtypeStruct((nl, 2 * half, cols), F32)] * (n // nl),
        compiler_params=_cparams(),
    )(cidx, *qs)


def rs_pair_gather(fs, name):
    n = len(fs)

    def body(*refs):
        f_refs, send, recv = refs[n:2 * n], refs[2 * n], refs[2 * n + 1]
        x, y, c, _ = _place()
        copies = []
        for k in range(n):
            rows = f_refs[k].at[pl.ds(0, fs[k].shape[0]), _rows_of(c, fs[k].shape[1] // 2)]
            copies.append(pltpu.make_async_remote_copy(src_ref=rows, dst_ref=rows, send_sem=send.at[k], recv_sem=recv.at[k],
                                                       device_id=(x, y, 1 - c), device_id_type=MESH))
        for cp in copies:
            cp.start()
        for cp in copies:
            cp.wait()

    return pl.pallas_call(
        body, name=name, in_specs=[_ANY] * n, out_specs=[_ANY] * n,
        out_shape=[jax.ShapeDtypeStruct(f.shape, f.dtype) for f in fs],
        input_output_aliases={k: k for k in range(n)},
        scratch_shapes=[pltpu.SemaphoreType.DMA((n,)), pltpu.SemaphoreType.DMA((n,))],
    )(*fs)


_HBM = pl.BlockSpec(memory_space=pltpu.HBM)
_SEM = pl.BlockSpec(memory_space=pltpu.SEMAPHORE)
_EFFECT = pltpu.SideEffectType.DATAFLOW_SIDE_EFFECTING


def _in_hbm(a):
    return pltpu.with_memory_space_constraint(a, pltpu.HBM)


def split_start(name, srcs, lands, after, copies_fn, n_copies):
    n = len(srcs)

    def body(*refs):
        for cp in copies_fn(refs[:n], refs[n:2 * n], refs[2 * n + 1], refs[2 * n + 2]):
            cp.start()
        refs[-1][...] = jnp.zeros_like(refs[-1])

    bufs = list(srcs) + list(lands)
    outs = pl.pallas_call(
        body, name=name,
        out_shape=(pltpu.SemaphoreType.DMA((n_copies,)), pltpu.SemaphoreType.DMA((n_copies,)),
                   *[pltpu.HBM(a.shape, a.dtype) for a in bufs], jax.ShapeDtypeStruct((8, 128), F32)),
        in_specs=[_HBM] * (2 * n) + [_ANY],
        out_specs=(_SEM, _SEM, *[_HBM] * (2 * n), pl.BlockSpec(memory_space=pltpu.VMEM)),
        input_output_aliases={i: 2 + i for i in range(2 * n)},
        compiler_params=pltpu.CompilerParams(has_side_effects=_EFFECT),
    )(*[_in_hbm(a) for a in bufs], after)
    return outs[0], outs[1], outs[2:2 + n], outs[2 + n:2 + 2 * n], outs[-1]


def split_wait(name, send, recv, srcs, lands, after, copies_fn, which=None):
    n = len(srcs)

    def body(*refs):
        copies = copies_fn(refs[:n], refs[n:2 * n], refs[2 * n], refs[2 * n + 1], which)
        for cp in copies:
            cp.wait_send()
        for cp in copies:
            cp.wait_recv()

    bufs = list(srcs) + list(lands)
    outs = pl.pallas_call(
        body, name=name, out_shape=tuple(pltpu.HBM(a.shape, a.dtype) for a in bufs),
        in_specs=[_HBM] * (2 * n) + [_SEM, _SEM, _ANY], out_specs=tuple([_HBM] * (2 * n)),
        input_output_aliases={i: i for i in range(2 * n)},
        compiler_params=pltpu.CompilerParams(has_side_effects=_EFFECT),
    )(*bufs, send, recv, after)
    return list(outs[:n]), list(outs[n:])


def _gather_copies(s_refs, l_refs, send, recv, which=None):
    x, y, c, chips = _place()
    me = 2 * x + y
    out = []
    for k in (range(len(s_refs)) if which is None else which):
        s, l = s_refs[k], l_refs[k]
        rows = _rows_of(c, s.shape[0] // 2)
        for j, chip in enumerate(chips):
            out.append(pltpu.make_async_remote_copy(src_ref=s.at[rows], dst_ref=l.at[me, rows], send_sem=send.at[4 * k + j],
                                                    recv_sem=recv.at[4 * k + j], device_id=(*chip, c), device_id_type=MESH))
        out.append(pltpu.make_async_remote_copy(src_ref=s, dst_ref=l.at[me], send_sem=send.at[4 * k + 3],
                                                recv_sem=recv.at[4 * k + 3], device_id=(x, y, 1 - c), device_id_type=MESH))
    return out


def _scatter_copies(s_refs, l_refs, send, recv, which=None):
    x, y, c, chips = _place()
    me = 2 * x + y
    return [pltpu.make_async_remote_copy(src_ref=s_refs[k].at[2 * chip[0] + chip[1]], dst_ref=l_refs[k].at[me],
                                         send_sem=send.at[3 * k + j], recv_sem=recv.at[3 * k + j], device_id=(*chip, c),
                                         device_id_type=MESH)
            for k in (range(len(s_refs)) if which is None else which) for j, chip in enumerate(chips)]


def gather_forward(lands, name):
    n = len(lands)

    def body(*refs):
        l_refs, send, recv = refs[n:2 * n], refs[2 * n], refs[2 * n + 1]
        x, y, c, chips = _place()
        copies = []
        for k in range(n):
            rows = _rows_of(c, lands[k].shape[1] // 2)
            for j, chip in enumerate(chips):
                part = l_refs[k].at[2 * chip[0] + chip[1], rows]
                copies.append(pltpu.make_async_remote_copy(src_ref=part, dst_ref=part, send_sem=send.at[3 * k + j],
                                                           recv_sem=recv.at[3 * k + j], device_id=(x, y, 1 - c),
                                                           device_id_type=MESH))
        for cp in copies:
            cp.start()
        for cp in copies:
            cp.wait()

    return pl.pallas_call(
        body, name=name, in_specs=[_ANY] * n, out_specs=[_ANY] * n,
        out_shape=[jax.ShapeDtypeStruct(a.shape, a.dtype) for a in lands],
        input_output_aliases={k: k for k in range(n)},
        scratch_shapes=[pltpu.SemaphoreType.DMA((3 * n,)), pltpu.SemaphoreType.DMA((3 * n,))],
    )(*lands)


def rs_partials(gs, wire, cidx, tag):
    rs = rs_pair_swap(gs)
    groups = {}
    for k, g in enumerate(gs):
        groups.setdefault((g.shape, jnp.dtype(wire[k]).name), []).append(k)
    ps = [None] * len(gs)
    for gi, ks in enumerate(groups.values()):
        outs = rs_pair_add([gs[k] for k in ks], [rs[k] for k in ks], cidx, wire[ks[0]], "grad_pair_add_%s%d" % (tag, gi))
        for k, o in zip(ks, outs):
            ps[k] = o
    return ps


def rs_finish(items, tag):
    cidx = lax.axis_index("c").astype(jnp.int32).reshape(1)
    groups = {}
    for i, it in enumerate(items):
        groups.setdefault((it[0].shape, len(it), it[0].dtype.name), []).append(i)
    fs = [None] * len(items)
    for gi, ids in enumerate(groups.values()):
        outs = rs_chip_sum([q for i in ids for q in items[i]], len(items[ids[0]]), cidx, "grad_chip_sum_%s%d" % (tag, gi))
        for i, o in zip(ids, outs):
            fs[i] = o
    return rs_pair_gather(fs, "grad_pair_gather_" + tag)


def adamw(w, g, m, v, name):
    shape = w.shape
    if w.ndim == 2:
        block, grid, index = shape, (1,), (lambda i: (0, 0))
    else:
        slab = shape[2:]
        unit = 4 * int(np.prod(slab[:-2] or (1,))) * (-(-slab[-1] // 128) * 128)
        if len(slab) >= 2:
            unit *= -(-slab[-2] // 8) * 8
        k = shape[1]
        tr = k
        if k * unit > 2 ** 21:
            tr = max(t for t in range(8, k, 8) if k % t == 0 and t * unit <= 2 ** 21)
        block, grid = (None, tr) + tuple(slab), (shape[0], k // tr)
        index = lambda l, i: (l, i) + (0,) * len(slab)
        if tr < min(k, 64) and len(slab) == 1:
            tc = max(t for t in range(128, slab[0] + 1, 128) if slab[0] % t == 0 and k * t * 4 <= 2 ** 21)
            block, grid = (None, k, tc), (shape[0], slab[0] // tc)
            index = lambda l, i: (l, 0, i)

    def body(w_ref, g_ref, m_ref, v_ref, d_ref, nm_ref, nv_ref):
        g_ = g_ref[...]
        m_new = ADAM_B1 * m_ref[...] + (1.0 - ADAM_B1) * g_
        v_new = ADAM_B2 * v_ref[...] + (1.0 - ADAM_B2) * (g_ * g_)
        m_hat = m_new / (1.0 - ADAM_B1 ** ADAM_STEP)
        v_hat = v_new / (1.0 - ADAM_B2 ** ADAM_STEP)
        d_ref[...] = -ADAM_LR * (m_hat / (jnp.sqrt(v_hat) + ADAM_EPS) + ADAM_WD * w_ref[...])
        nm_ref[...] = m_new
        nv_ref[...] = v_new

    spec = pl.BlockSpec(block, index)
    return pl.pallas_call(
        body, name=name, grid=grid, in_specs=[spec] * 4, out_specs=[spec] * 3,
        out_shape=[jax.ShapeDtypeStruct(shape, F32)] * 3, compiler_params=_cparams(),
    )(w, g, m, v)


_WEIGHTS = ['meta', 'ffn1_w_gate', 'ffn1_w_up', 'ffn1_w_down', 'ln1_g', 'ln1_b', 'w_in', 'mla_q_norm_g', 'mla_w_uq',
            'mla_kv_norm_g', 'mla_w_ukv', 'mla_w_o', 'conv_w', 'conv_b', 'conv_w_out', 's5_a_re', 's5_a_im', 's5_log_dt',
            's5_b_re', 's5_b_im', 's5_c_re', 's5_c_im', 's5_d', 's5_w_glu', 's5_b_glu', 's5_w_out', 'w_o', 'ln2_g', 'ln2_b',
            'ffn2_w_gate', 'ffn2_w_up', 'ffn2_w_down', 'ln3_g', 'ln3_b']


def _pad_to(flat, n):
    return jnp.concatenate([flat, jnp.zeros((n - flat.shape[0],), flat.dtype)])


def _shard_of(full, axis):
    if axis == 1:
        return _shard_cols(full)
    if axis == 'T':
        return full.T.reshape(N_SHARD, full.shape[1] // N_SHARD, full.shape[0])
    return full.reshape(N_SHARD, full.shape[0] // N_SHARD, full.shape[1])


_FFN_KEY = {'gate': 'wg', 'up': 'wu', 'down': 'wd'}


def _pad_rows(a, axis):
    k = a.shape[axis]
    extra = -k % 32
    if not extra:
        return a
    return jnp.pad(a, [(0, extra) if d == axis else (0, 0) for d in range(a.ndim)])


def _step(env):
    w = {n: env[n] for n in _WEIGHTS}
    mom = {n: env['m_' + n] for n in _WEIGHTS}
    var = {n: env['v_' + n] for n in _WEIGHTS}
    cidx = lax.axis_index("c").astype(jnp.int32).reshape(1)
    chip = 2 * lax.axis_index("x") + lax.axis_index("y")
    big_names = [n for n, _ in _BIG]
    nb = len(big_names)

    kept_t = [n for n, a in _BIG if a == 'T']
    own = {n: (jnp.swapaxes(w[n], 1, 2) if n in kept_t else w[n]) for n in big_names}
    first = [n for n in big_names if n.startswith('ffn1')]
    mix = [n for n in big_names if not n.startswith('ffn')]
    last = [n for n in big_names if n.startswith('ffn2')]
    rest = mix + last
    nm, nr = len(mix), len(mix) + len(last)
    src = lambda n, li: _pad_rows(own[n][li].astype(BF16), 0)
    gathered_first, (conv_w_st, meta_st) = all_gather_shards([src(n, 0) for n in first], [w['conv_w'], w['meta']])
    later = [src(n, 0) for n in rest] + [src(n, 1) for n in big_names]
    lands = [lax.empty((N_SHARD,) + s.shape, BF16) for s in later]
    g_send, g_recv, later_t, lands_t, token = split_start("gather_start", later, lands, gathered_first[0], _gather_copies,
                                                          4 * len(later))

    def weights_of(names, st, li, with_small):
        small = None
        if with_small:
            small = {n: w[n][li] for n in _REPL}
            small['conv_w'] = _nat_cols(conv_w_st[:, li])
        return compute_weights({n: a[:, :own[n].shape[1]] for n, a in zip(names, st)}, small)

    x2d = env['x'][0]
    lp = x2d.shape[0] + X0
    tabs = _rope_tables(lp)
    h = jnp.concatenate([jnp.zeros((PAD, D_MODEL), F32), _nat_cols(meta_st), x2d], axis=0) + token[0, 0]
    W0 = weights_of(first, gathered_first, 0, True)
    ffn1 = ffn_fwd(h, h.astype(BF16), W0['wg1'], W0['wu1'], W0['wd1'], W0['ln1_g'], W0['ln1_b'], lp)
    later_t, lands_t = split_wait("gather0_wait", g_send, g_recv, later_t, lands_t, ffn1[0], _gather_copies, range(nm))
    W0.update(weights_of(mix, gather_forward(lands_t[:nm], "gather0_forward"), 0, False))
    h, hb, sv0 = layer_fwd(None, None, W0, tabs, lp, ffn1=ffn1, ffn2=False)
    later_t, lands_t = split_wait("gather0b_wait", g_send, g_recv, later_t, lands_t, h, _gather_copies, range(nm, nr))
    W0.update(weights_of(last, gather_forward(lands_t[nm:nr], "gather0b_forward"), 0, False))
    h, hb, sv0['sv3'] = ffn_fwd(h, hb, W0['wg2'], W0['wu2'], W0['wd2'], W0['ln3_g'], W0['ln3_b'], lp)
    _, lands_t = split_wait("gather1_wait", g_send, g_recv, later_t, lands_t, h, _gather_copies, range(nr, len(later)))
    W1 = weights_of(big_names, gather_forward(lands_t[nr:], "gather1_forward"), 1, True)
    h, hb, sv1 = layer_fwd(h, hb, W1, tabs, lp)
    tgt = jnp.concatenate([jnp.zeros((X0, D_MODEL), F32), env['loss_target'][0]], axis=0)
    dh, loss_part = loss_head(h, tgt, lp)
    loss = lax.psum(loss_part[0, 0], ("x", "y", "c"))

    def shards(G, names):
        full = None if all(n.startswith('ffn') for n in names) else reference_grads(G, ffn=False)

        def one(n, a):
            if n.startswith('ffn'):
                return G[_FFN_KEY[n.split('_')[-1]] + n[3]]
            if n == 'w_in':
                return full['w_in_t'].reshape(N_SHARD, D_IN // N_SHARD, D_MODEL)
            return _shard_of(full[n], a)

        return [_pad_rows(one(n, a), 1) for n, a in _BIG if n in names]

    def scatter_start(name, ps, after):
        qs = [lax.dynamic_update_slice_in_dim(jnp.zeros_like(p), lax.dynamic_slice_in_dim(p, chip, 1, axis=0), chip, axis=0)
              for p in ps]
        return split_start(name, ps, qs, after, _scatter_copies, 3 * len(ps))

    dh, G1 = layer_bwd(dh, sv1, W1, tabs, lp)
    p1 = rs_partials(shards(G1, big_names), [BF16] * nb, cidx, "b")
    s1_send, s1_recv, p1_t, q1_t, token1 = scatter_start("scatter1_start", p1, dh)
    dh, g3 = ffn_bwd(dh, sv0['sv3'], W0['wg2'], W0['wu2'], W0['wd2'], W0['ln3_g'] + token1[0, 0], lp)
    G0 = dict(wg2=g3['wg'], wu2=g3['wu'], wd2=g3['wd'])
    p0l = rs_partials(shards(G0, last), [BF16] * len(last), cidx, "c")
    sl_send, sl_recv, p0l_t, q0l_t, token0l = scatter_start("scatter0b_start", p0l, dh)
    dh, Gm = layer_bwd(dh, sv0, dict(W0, ln2_g=W0['ln2_g'] + token0l[0, 0]), tabs, lp, ffn1=False, ffn2=False)
    G0.update(Gm, ln3_g=g3['ln_g'], ln3_b=g3['ln_b'])
    p0m = rs_partials(shards(G0, mix), [BF16] * nm, cidx, "d")
    sm_send, sm_recv, p0m_t, q0m_t, token0m = scatter_start("scatter0_start", p0m, dh)
    dh, g1 = ffn_bwd(dh, sv0['sv1'], W0['wg1'], W0['wu1'], W0['wd1'], W0['ln1_g'] + token0m[0, 0], lp)
    G0.update(wg1=g1['wg'], wu1=g1['wu'], wd1=g1['wd'], ln1_g=g1['ln_g'], ln1_b=g1['ln_b'])
    _, q1 = split_wait("scatter1_wait", s1_send, s1_recv, p1_t, q1_t, dh, _scatter_copies)
    _, q0_last = split_wait("scatter0b_wait", sl_send, sl_recv, p0l_t, q0l_t, dh, _scatter_copies)
    _, q0_mix = split_wait("scatter0_wait", sm_send, sm_recv, p0m_t, q0m_t, dh, _scatter_copies)
    q0_rest = list(q0_mix) + list(q0_last)
    full = [reference_grads(G0, ffn=False), reference_grads(G1, ffn=False)]

    s_parts = [jnp.stack([full[li][n] for li in range(DEPTH)]).reshape(-1) for n in _REPL + ['conv_w']]
    s_parts.append(dh[PAD:X0].reshape(-1))
    s_sizes = [int(p.shape[0]) for p in s_parts]
    s_rows = -(-sum(s_sizes) // (16 * LANES)) * 16
    g_small = _pad_to(jnp.concatenate(s_parts), s_rows * LANES).reshape(1, s_rows, LANES)
    g_small = jnp.broadcast_to(g_small, (N_SHARD, s_rows, LANES))
    p_last = rs_partials(shards(G0, first) + [g_small], [BF16] * len(first) + [F32], cidx, "a")
    z_send, z_recv, pz_t, qz_t, token_z = scatter_start("scatter_last_start", p_last, dh)

    def step_weights(names, grad):
        out = {}
        for n in names:
            if n in kept_t:
                res = adamw(own[n], grad[n], jnp.swapaxes(mom[n], 1, 2), jnp.swapaxes(var[n], 1, 2), "adamw_" + n)
                out[n] = [jnp.swapaxes(t, 1, 2) for t in [grad[n]] + list(res)]
            else:
                out[n] = [grad[n]] + list(adamw(w[n], grad[n], mom[n], var[n], "adamw_" + n))
        return out

    q1 = dict(zip(big_names, q1))
    q0 = dict(zip(rest, q0_rest))
    q0[rest[0]] = q0[rest[0]] + token_z[0, 0].astype(BF16)
    red = rs_finish([[q0[n], q1[n]] for n in rest], "a")
    done = step_weights(rest, {n: r[:, :own[n].shape[1]] for n, r in zip(rest, red)})
    all_done = jnp.stack([done[n][3][(0,) * done[n][3].ndim] for n in rest])
    _, q_last = split_wait("scatter_last_wait", z_send, z_recv, pz_t, qz_t, all_done, _scatter_copies)
    red = rs_finish([[q, q1[n]] for n, q in zip(first, q_last)] + [[q_last[-1]]], "b")
    f_small = red[-1].reshape(-1)

    grad = {n: r[:, :own[n].shape[1]] for n, r in zip(first, red)}
    off = 0
    for n, sz in zip(_REPL + ['conv_w', 'meta'], s_sizes):
        grad[n] = f_small[off:off + sz]
        off += sz
    for n in _REPL:
        grad[n] = grad[n].reshape(w[n].shape)
    cw = grad['conv_w'].reshape(DEPTH, 3, MIX)
    grad['conv_w'] = lax.dynamic_slice_in_dim(cw, chip * (MIX // N_SHARD), MIX // N_SHARD, axis=2)
    gm = grad['meta'].reshape(N_META, D_MODEL)
    grad['meta'] = lax.dynamic_slice_in_dim(gm, chip * (D_MODEL // N_SHARD), D_MODEL // N_SHARD, axis=1)

    done.update(step_weights([n for n in _WEIGHTS if n not in done], grad))
    return (loss, dh[X0:][None], *[done[n][k] for k in range(4) for n in _WEIGHTS])


def kernel(x, meta, ffn1_w_gate, ffn1_w_up, ffn1_w_down, ln1_g, ln1_b, w_in, mla_q_norm_g, mla_w_uq, mla_kv_norm_g, mla_w_ukv, mla_w_o, conv_w, conv_b, conv_w_out, s5_a_re, s5_a_im, s5_log_dt, s5_b_re, s5_b_im, s5_c_re, s5_c_im, s5_d, s5_w_glu, s5_b_glu, s5_w_out, w_o, ln2_g, ln2_b, ffn2_w_gate, ffn2_w_up, ffn2_w_down, ln3_g, ln3_b, loss_target, m_meta, m_ffn1_w_gate, m_ffn1_w_up, m_ffn1_w_down, m_ln1_g, m_ln1_b, m_w_in, m_mla_q_norm_g, m_mla_w_uq, m_mla_kv_norm_g, m_mla_w_ukv, m_mla_w_o, m_conv_w, m_conv_b, m_conv_w_out, m_s5_a_re, m_s5_a_im, m_s5_log_dt, m_s5_b_re, m_s5_b_im, m_s5_c_re, m_s5_c_im, m_s5_d, m_s5_w_glu, m_s5_b_glu, m_s5_w_out, m_w_o, m_ln2_g, m_ln2_b, m_ffn2_w_gate, m_ffn2_w_up, m_ffn2_w_down, m_ln3_g, m_ln3_b, v_meta, v_ffn1_w_gate, v_ffn1_w_up, v_ffn1_w_down, v_ln1_g, v_ln1_b, v_w_in, v_mla_q_norm_g, v_mla_w_uq, v_mla_kv_norm_g, v_mla_w_ukv, v_mla_w_o, v_conv_w, v_conv_b, v_conv_w_out, v_s5_a_re, v_s5_a_im, v_s5_log_dt, v_s5_b_re, v_s5_b_im, v_s5_c_re, v_s5_c_im, v_s5_d, v_s5_w_glu, v_s5_b_glu, v_s5_w_out, v_w_o, v_ln2_g, v_ln2_b, v_ffn2_w_gate, v_ffn2_w_up, v_ffn2_w_down, v_ln3_g, v_ln3_b):
    return _step(dict(locals()))
```

```python
import functools
import math

import numpy as np
import jax
import jax.numpy as jnp
from jax import lax
from jax.experimental import pallas as pl
from jax.experimental.pallas import tpu as pltpu

F32 = jnp.float32
BF16 = jnp.bfloat16

D_MODEL = 1024
DEPTH = 2
N_META = 16
PAD = 112
X0 = PAD + N_META
N_HEADS = 8
D_NOPE = 64
D_ROPE = 32
D_V = 64
Q_RANK = 384
KV_RANK = 256
MIX = 512
S5_GROUPS = 32
S5_GROUP = 16
S5_STATE = 64
S5_LANES = S5_GROUPS * S5_STATE
D_FF = 2816
N_SHARD = 4
FF_SHARD = D_FF // N_SHARD
D_IN = 5792
P_IN = 6144
ALPHA = (2.0 * DEPTH) ** 0.25
LN_EPS = 1e-5
RMS_EPS = 1e-6
ATT_SCALE = (D_NOPE + D_ROPE) ** -0.5
ROPE_BASE = 10000.0
ADAM_LR, ADAM_B1, ADAM_B2, ADAM_EPS, ADAM_WD, ADAM_STEP = 0.001, 0.9, 0.999, 1e-08, 0.01, 10
SCAN_CHUNK = 128
VMEM_LIMIT = 52 * 2 ** 20
WGRAD = BF16
MESH = pl.DeviceIdType.MESH


def _cparams(**kw):
    return pltpu.CompilerParams(vmem_limit_bytes=VMEM_LIMIT, **kw)


def _tile(n):
    if n <= 1088:
        return n
    for t in (1024, 544, 512, 272, 256, 128):
        if n % t == 0:
            return t
    return n


def _row_tile(lp):
    for t in (544, 272, 128):
        if lp % t == 0:
            return t
    return lp


def _ffn_tile(lp):
    return 1088 if lp % 1088 == 0 else _row_tile(lp)


def _sigmoid(x):
    return 1.0 / (1.0 + jnp.exp(-x))


_GELU_C = math.sqrt(2.0 / math.pi)


def _gelu(x):
    return 0.5 * x * (1.0 + jnp.tanh(_GELU_C * (x + 0.044715 * x * x * x)))


def _gelu_grad(x):
    t = jnp.tanh(_GELU_C * (x + 0.044715 * x * x * x))
    return 0.5 * (1.0 + t) + 0.5 * x * (1.0 - t * t) * _GELU_C * (1.0 + 3.0 * 0.044715 * x * x)


def _dot(a, b, ca, cb, precision=None):
    return lax.dot_general(a, b, (((ca,), (cb,)), ((), ())), preferred_element_type=F32, precision=precision)


def matmul(a, b, *, name, ta=False, tb=False, ab='n', bb='n', res=None, res_scale=1.0, scale=1.0, out_dtype=F32):
    if ta:
        _, K, M = a.shape
    else:
        _, M, K = a.shape
    if tb:
        _, N, K2 = b.shape
    else:
        _, K2, N = b.shape
    assert K == K2, (a.shape, b.shape)
    n_out = max(a.shape[0] if ab == 'o' else 1, b.shape[0] if bb == 'o' else 1)
    n_red = max(a.shape[0] if ab == 'r' else 1, b.shape[0] if bb == 'r' else 1)
    tm, tn = _tile(M), _tile(N)
    tk = K if K <= 2304 else _tile(K)
    nkt = K // tk
    n_steps = n_red * nkt

    def bsel(mode, o, r):
        if mode == 'o':
            return o
        if mode == 'r':
            return r // nkt if nkt > 1 else r
        return 0

    def ksel(r):
        if nkt == 1:
            return 0
        return r % nkt if n_red > 1 else r

    a_map = (lambda o, i, j, r: (bsel(ab, o, r), ksel(r), i)) if ta else (lambda o, i, j, r: (bsel(ab, o, r), i, ksel(r)))
    b_map = (lambda o, i, j, r: (bsel(bb, o, r), j, ksel(r))) if tb else (lambda o, i, j, r: (bsel(bb, o, r), ksel(r), j))
    o_map = lambda o, i, j, r: (o, i, j)
    in_specs = [pl.BlockSpec((None, tk, tm) if ta else (None, tm, tk), a_map),
                pl.BlockSpec((None, tn, tk) if tb else (None, tk, tn), b_map)]
    operands = [a, b]
    if res is not None:
        in_specs.append(pl.BlockSpec((None, tm, tn), o_map))
        operands.append(res)
    has_res = res is not None

    def body(*refs):
        a_ref, b_ref = refs[0], refs[1]
        res_ref = refs[2] if has_res else None
        o_ref = refs[3] if has_res else refs[2]
        part = _dot(a_ref[...].astype(BF16), b_ref[...].astype(BF16), 0 if ta else 1, 1 if tb else 0)

        def finish(acc):
            v = acc if scale == 1.0 else acc * scale
            if has_res:
                v = v + res_scale * res_ref[...].astype(F32)
            o_ref[...] = v.astype(o_ref.dtype)

        if n_steps == 1:
            finish(part)
        else:
            acc_ref = refs[-1]
            r = pl.program_id(3)

            @pl.when(r == 0)
            def _():
                acc_ref[...] = part

            @pl.when(r > 0)
            def _():
                acc_ref[...] += part

            @pl.when(r == n_steps - 1)
            def _():
                finish(acc_ref[...])

    return pl.pallas_call(
        body, name=name,
        grid=(n_out, M // tm, N // tn, n_steps),
        in_specs=in_specs,
        out_specs=pl.BlockSpec((None, tm, tn), o_map),
        out_shape=jax.ShapeDtypeStruct((n_out, M, N), out_dtype),
        scratch_shapes=[pltpu.VMEM((tm, tn), F32)] if n_steps > 1 else [],
        compiler_params=_cparams(),
    )(*operands)


def rowwise(fn, rows, pars, outs, accs=(), *, name, lp):
    tm = _row_tile(lp)
    n_rows, n_pars, n_outs, n_accs = len(rows), len(pars), len(outs), len(accs)
    in_specs = [pl.BlockSpec((tm, w), functools.partial(lambda i, cb: (i, cb), cb=cb)) for _, w, cb in rows]
    in_specs += [pl.BlockSpec(p.shape, functools.partial(lambda i, nd: (0,) * nd, nd=p.ndim)) for p in pars]
    out_specs = [pl.BlockSpec((tm, w), lambda i: (i, 0)) for w, _ in outs]
    out_specs += [pl.BlockSpec(s, functools.partial(lambda i, nd: (0,) * nd, nd=len(s))) for s, _ in accs]
    out_shape = [jax.ShapeDtypeStruct((lp, w), dt) for w, dt in outs]
    out_shape += [jax.ShapeDtypeStruct(s, dt) for s, dt in accs]

    def body(*refs):
        i = pl.program_id(0)
        rv = [r[...] for r in refs[:n_rows]]
        pv = [r[...] for r in refs[n_rows:n_rows + n_pars]]
        o_refs = refs[n_rows + n_pars:n_rows + n_pars + n_outs]
        a_refs = refs[n_rows + n_pars + n_outs:]
        ov, av = fn(i * tm, rv, pv)
        for r, v in zip(o_refs, ov):
            r[...] = v.astype(r.dtype)
        if n_accs:
            @pl.when(i == 0)
            def _():
                for r, v in zip(a_refs, av):
                    r[...] = v.astype(r.dtype)

            @pl.when(i > 0)
            def _():
                for r, v in zip(a_refs, av):
                    r[...] += v.astype(r.dtype)

    res = pl.pallas_call(
        body, name=name, grid=(lp // tm,), in_specs=in_specs, out_specs=out_specs, out_shape=out_shape,
        compiler_params=_cparams(),
    )(*[r[0] for r in rows], *pars)
    return res


def _row_mask(row0, shape):
    return (row0 + lax.broadcasted_iota(jnp.int32, shape, 0)) >= PAD


def ffn_up(hb, wg, wu, lp):
    tm = _ffn_tile(lp)

    def body(h_ref, wg_ref, wu_ref, ab_ref, hid_ref):
        h = h_ref[...]
        a = _dot(h, wg_ref[...], 1, 1)
        b = _dot(h, wu_ref[...], 1, 1)
        ab_ref[0] = a.astype(BF16)
        ab_ref[1] = b.astype(BF16)
        hid_ref[...] = (a * _sigmoid(a) * b).astype(BF16)

    wspec = pl.BlockSpec((None, FF_SHARD, D_MODEL), lambda j, i: (j, 0, 0))
    return pl.pallas_call(
        body, name="ffn_up", grid=(N_SHARD, lp // tm),
        in_specs=[pl.BlockSpec((tm, D_MODEL), lambda j, i: (i, 0)), wspec, wspec],
        out_specs=[pl.BlockSpec((None, 2, tm, FF_SHARD), lambda j, i: (j, 0, i, 0)),
                   pl.BlockSpec((None, tm, FF_SHARD), lambda j, i: (j, i, 0))],
        out_shape=[jax.ShapeDtypeStruct((N_SHARD, 2, lp, FF_SHARD), BF16),
                   jax.ShapeDtypeStruct((N_SHARD, lp, FF_SHARD), BF16)],
        compiler_params=_cparams(),
    )(hb, wg, wu)


def _layer_norm(z, g, b):
    mu = jnp.mean(z, axis=-1, keepdims=True)
    zc = z - mu
    var = jnp.mean(zc * zc, axis=-1, keepdims=True)
    return zc * lax.rsqrt(var + LN_EPS) * g + b


def mm_res_ln(a, w, res, g, b, *, scale, name, lp):
    n_red, _, K = a.shape
    tm = _row_tile(lp)

    def body(a_ref, w_ref, res_ref, g_ref, b_ref, z_ref, h_ref, hb_ref, acc_ref):
        r = pl.program_id(1)
        part = _dot(a_ref[...].astype(BF16), w_ref[...], 1, 0)

        @pl.when(r == 0)
        def _():
            acc_ref[...] = part

        @pl.when(r > 0)
        def _():
            acc_ref[...] += part

        @pl.when(r == n_red - 1)
        def _():
            z = ALPHA * res_ref[...] + scale * acc_ref[...]
            z_ref[...] = z
            hn = _layer_norm(z, g_ref[...], b_ref[...])
            h_ref[...] = hn
            hb_ref[...] = hn.astype(BF16)

    row = pl.BlockSpec((tm, D_MODEL), lambda i, r: (i, 0))
    par = pl.BlockSpec((1, D_MODEL), lambda i, r: (0, 0))
    return pl.pallas_call(
        body, name=name, grid=(lp // tm, n_red),
        in_specs=[pl.BlockSpec((None, tm, K), lambda i, r: (r, i, 0)),
                  pl.BlockSpec((None, K, D_MODEL), lambda i, r: (r, 0, 0)), row, par, par],
        out_specs=[row, row, row],
        out_shape=[jax.ShapeDtypeStruct((lp, D_MODEL), F32), jax.ShapeDtypeStruct((lp, D_MODEL), F32),
                   jax.ShapeDtypeStruct((lp, D_MODEL), BF16)],
        scratch_shapes=[pltpu.VMEM((tm, D_MODEL), F32)],
        compiler_params=_cparams(),
    )(a, w, res, g, b)


def ln_bwd(dh, z, g, *, fscale, name, lp):
    def fn(row0, rv, pv):
        dh_, z_ = rv
        g_, = pv
        mu = jnp.mean(z_, axis=-1, keepdims=True)
        zc = z_ - mu
        rstd = lax.rsqrt(jnp.mean(zc * zc, axis=-1, keepdims=True) + LN_EPS)
        xh = zc * rstd
        dxh = dh_ * g_
        m1 = jnp.mean(dxh, axis=-1, keepdims=True)
        m2 = jnp.mean(dxh * xh, axis=-1, keepdims=True)
        dz = rstd * (dxh - m1 - xh * m2)
        return ((dz, fscale * dz),
                (jnp.sum(dh_ * xh, axis=0, keepdims=True), jnp.sum(dh_, axis=0, keepdims=True)))

    return rowwise(fn, [(dh, D_MODEL, 0), (z, D_MODEL, 0)], [g], [(D_MODEL, F32), (D_MODEL, BF16)],
                   [((1, D_MODEL), F32), ((1, D_MODEL), F32)], name=name, lp=lp)


def ffn_down_bwd(dfb, wd, ab, lp):
    tm = _ffn_tile(lp)

    def body(df_ref, w_ref, ab_ref, da_ref, db_ref):
        dhid = _dot(df_ref[...], w_ref[...], 1, 1)
        a = ab_ref[0].astype(F32)
        b = ab_ref[1].astype(F32)
        sg = _sigmoid(a)
        da_ref[...] = (dhid * b * (sg * (1.0 + a * (1.0 - sg)))).astype(BF16)
        db_ref[...] = (dhid * (a * sg)).astype(BF16)

    ospec = pl.BlockSpec((None, tm, FF_SHARD), lambda j, i: (j, i, 0))
    return pl.pallas_call(
        body, name="ffn_down_bwd", grid=(N_SHARD, lp // tm),
        in_specs=[pl.BlockSpec((tm, D_MODEL), lambda j, i: (i, 0)),
                  pl.BlockSpec((None, FF_SHARD, D_MODEL), lambda j, i: (j, 0, 0)),
                  pl.BlockSpec((None, 2, tm, FF_SHARD), lambda j, i: (j, 0, i, 0))],
        out_specs=[ospec, ospec],
        out_shape=[jax.ShapeDtypeStruct((N_SHARD, lp, FF_SHARD), BF16)] * 2,
        compiler_params=_cparams(),
    )(dfb, wd, ab)


def ffn_dx(da, db, wg, wu, dz, lp):
    tm = _ffn_tile(lp)

    def body(da_ref, db_ref, wg_ref, wu_ref, dz_ref, o_ref, acc_ref):
        j = pl.program_id(1)
        part = _dot(da_ref[...], wg_ref[...], 1, 0) + _dot(db_ref[...], wu_ref[...], 1, 0)

        @pl.when(j == 0)
        def _():
            acc_ref[...] = part

        @pl.when(j > 0)
        def _():
            acc_ref[...] += part

        @pl.when(j == N_SHARD - 1)
        def _():
            o_ref[...] = acc_ref[...] + ALPHA * dz_ref[...]

    aspec = pl.BlockSpec((None, tm, FF_SHARD), lambda i, j: (j, i, 0))
    wspec = pl.BlockSpec((None, FF_SHARD, D_MODEL), lambda i, j: (j, 0, 0))
    row = pl.BlockSpec((tm, D_MODEL), lambda i, j: (i, 0))
    return pl.pallas_call(
        body, name="ffn_dx", grid=(lp // tm, N_SHARD), in_specs=[aspec, aspec, wspec, wspec, row], out_specs=row,
        out_shape=jax.ShapeDtypeStruct((lp, D_MODEL), F32), scratch_shapes=[pltpu.VMEM((tm, D_MODEL), F32)],
        compiler_params=_cparams(),
    )(da, db, wg, wu, dz)


def ffn_fwd(h, hb, wg, wu, wd, g, b, lp):
    ab, hid = ffn_up(hb, wg, wu, lp)
    z, hn, hnb = mm_res_ln(hid, wd, h, g, b, scale=0.5, name="ffn_down_ln", lp=lp)
    return hn, hnb, dict(hb=hb, ab=ab, hid=hid, z=z)


def ffn_bwd(dh, sv, wg, wu, wd, g, lp):
    dz, dfb, dg, db = ln_bwd(dh, sv['z'], g, fscale=0.5, name="ffn_ln_bwd", lp=lp)
    da, dbb = ffn_down_bwd(dfb, wd, sv['ab'], lp)
    d_wd = matmul(sv['hid'], dfb[None], ta=True, ab='o', out_dtype=WGRAD, name="ffn_dwd")
    d_wg = matmul(da, sv['hb'][None], ta=True, ab='o', out_dtype=WGRAD, name="ffn_dwg")
    d_wu = matmul(dbb, sv['hb'][None], ta=True, ab='o', out_dtype=WGRAD, name="ffn_dwu")
    dh_in = ffn_dx(da, dbb, wg, wu, dz, lp)
    return dh_in, dict(wg=d_wg, wu=d_wu, wd=d_wd, ln_g=dg, ln_b=db)


def _rope_tables(lp):
    pos = np.arange(lp, dtype=np.float32) - PAD
    inv = ROPE_BASE ** (-np.arange(0, D_ROPE, 2, dtype=np.float32) / D_ROPE)
    ang = pos[:, None] * inv[None, :]
    cos = np.concatenate([np.cos(ang), np.cos(ang)], axis=1).astype(np.float32)
    sin = np.concatenate([np.sin(ang), np.sin(ang)], axis=1).astype(np.float32)
    rot = np.zeros((D_ROPE, D_ROPE), np.float32)
    half = D_ROPE // 2
    for j in range(half):
        rot[j + half, j] = -1.0
        rot[j, j + half] = 1.0
    d_qk = D_NOPE + D_ROPE
    cos_qk = np.concatenate([np.ones((lp, D_NOPE), np.float32), cos], axis=1)
    sin_qk = np.concatenate([np.zeros((lp, D_NOPE), np.float32), sin], axis=1)
    rot_qk = np.zeros((d_qk, d_qk), np.float32)
    rot_qk[D_NOPE:, D_NOPE:] = rot
    place = np.zeros((D_ROPE, d_qk), np.float32)
    place[:, D_NOPE:] = np.eye(D_ROPE, dtype=np.float32)
    return tuple(jnp.asarray(t) for t in (cos, sin, rot, cos_qk, sin_qk, rot_qk, place))


def _rot(x, rot):
    return _dot(x, rot, 1, 0, precision=lax.Precision.HIGHEST)


def _rms(x, g):
    r = lax.rsqrt(jnp.mean(x * x, axis=-1, keepdims=True) + RMS_EPS)
    return x * r * g


def mla_prep(proj, cos, sin, rot, qg, kvg, lp):
    def fn(row0, rv, pv):
        cq, krb, ckv, c, s = rv
        qg_, kvg_, rot_ = pv
        kr = krb[:, :D_ROPE]
        return ((_rms(cq, qg_), _rms(ckv, kvg_), kr * c + _rot(kr, rot_) * s), ())

    return rowwise(fn, [(proj, Q_RANK, 0), (proj, 128, 3), (proj, KV_RANK, 2), (cos, D_ROPE, 0), (sin, D_ROPE, 0)],
                   [qg, kvg, rot], [(Q_RANK, BF16), (KV_RANK, BF16), (D_ROPE, BF16)], name="mla_prep", lp=lp)


def mla_heads(cqn, ckvn, kr, cos_qk, sin_qk, rot_qk, place, wq, wk, wv, lp):
    tm = _row_tile(lp)

    def body(cq_ref, ckv_ref, kr_ref, c_ref, s_ref, rot_ref, place_ref, wq_ref, wk_ref, wv_ref, q_ref, k_ref, v_ref):
        cq = cq_ref[...]
        ckv = ckv_ref[...]
        kr_placed = _dot(kr_ref[...], place_ref[...].astype(BF16), 1, 0)
        for h in range(N_HEADS):
            q = _dot(cq, wq_ref[h], 1, 0)
            q_ref[h] = (q * c_ref[...] + _rot(q, rot_ref[...]) * s_ref[...]).astype(BF16)
            k_ref[h] = (_dot(ckv, wk_ref[h], 1, 0) + kr_placed).astype(BF16)
            v_ref[h] = _dot(ckv, wv_ref[h], 1, 0).astype(BF16)

    def row(w):
        return pl.BlockSpec((tm, w), lambda i: (i, 0))

    def whole(a):
        return pl.BlockSpec(a.shape, functools.partial(lambda i, nd: (0,) * nd, nd=a.ndim))

    def ospec(n):
        return pl.BlockSpec((N_HEADS, tm, n), lambda i: (0, i, 0))

    return pl.pallas_call(
        body, name="mla_heads", grid=(lp // tm,),
        in_specs=[row(Q_RANK), row(KV_RANK), row(D_ROPE), row(D_QK), row(D_QK), whole(rot_qk), whole(place),
                  whole(wq), whole(wk), whole(wv)],
        out_specs=[ospec(D_QK), ospec(D_QK), ospec(D_V)],
        out_shape=[jax.ShapeDtypeStruct((N_HEADS, lp, D_QK), BF16), jax.ShapeDtypeStruct((N_HEADS, lp, D_QK), BF16),
                   jax.ShapeDtypeStruct((N_HEADS, lp, D_V), BF16)],
        compiler_params=_cparams(),
    )(cqn, ckvn, kr, cos_qk, sin_qk, rot_qk, place, wq, wk, wv)


D_QK = D_NOPE + D_ROPE


def _att_probs(q, k, row0, tq, lp):
    s = _dot(q, k, 1, 1) * ATT_SCALE
    qi = row0 + lax.broadcasted_iota(jnp.int32, (tq, lp), 0)
    ki = lax.broadcasted_iota(jnp.int32, (tq, lp), 1)
    s = jnp.where((ki <= qi) & (ki >= PAD), s, -1e30)
    p = jnp.exp(s - jnp.max(s, axis=-1, keepdims=True))
    return p / jnp.sum(p, axis=-1, keepdims=True)


def _att_spec(lp, n):
    return pl.BlockSpec((None, lp, n), lambda h: (h, 0, 0))


def _att_tiles(lp):
    tiles, r = [(0, X0)], X0
    while r < lp:
        tiles.append((r, 256))
        r += 256
    assert r == lp
    return tiles


def attn_fwd(q, k, v, lp):
    def body(q_ref, k_ref, v_ref, o_ref):
        for r0, rows in _att_tiles(lp):
            ke, rq = r0 + rows, slice(r0, r0 + rows)
            p = _att_probs(q_ref[rq, :], k_ref[0:ke, :], r0, rows, ke)
            o_ref[rq, :] = _dot(p.astype(BF16), v_ref[0:ke, :], 1, 0).astype(BF16)

    return pl.pallas_call(
        body, name="attn_fwd", grid=(N_HEADS,),
        in_specs=[_att_spec(lp, D_QK), _att_spec(lp, D_QK), _att_spec(lp, D_V)],
        out_specs=_att_spec(lp, D_V), out_shape=jax.ShapeDtypeStruct((N_HEADS, lp, D_V), BF16),
        compiler_params=_cparams(),
    )(q, k, v)


def attn_bwd(q, k, v, do, lp):
    def body(q_ref, k_ref, v_ref, do_ref, dq_ref, dk_ref, dv_ref):
        dk_ref[...] = jnp.zeros_like(dk_ref)
        dv_ref[...] = jnp.zeros_like(dv_ref)
        for r0, rows in _att_tiles(lp):
            ke, rq = r0 + rows, slice(r0, r0 + rows)
            q_, do_, k_, v_ = q_ref[rq, :], do_ref[rq, :], k_ref[0:ke, :], v_ref[0:ke, :]
            p = _att_probs(q_, k_, r0, rows, ke)
            dp = _dot(do_, v_, 1, 1)
            delta = jnp.sum(p * dp, axis=-1, keepdims=True)
            ds = (p * (dp - delta) * ATT_SCALE).astype(BF16)
            dq_ref[rq, :] = _dot(ds, k_, 1, 0)
            dk_ref[0:ke, :] += _dot(ds, q_, 0, 0)
            dv_ref[0:ke, :] += _dot(p.astype(BF16), do_, 0, 0)

    qk, vv = _att_spec(lp, D_QK), _att_spec(lp, D_V)
    return pl.pallas_call(
        body, name="attn_bwd", grid=(N_HEADS,), in_specs=[qk, qk, vv, vv], out_specs=[qk, qk, vv],
        out_shape=[jax.ShapeDtypeStruct((N_HEADS, lp, D_QK), F32), jax.ShapeDtypeStruct((N_HEADS, lp, D_QK), F32),
                   jax.ShapeDtypeStruct((N_HEADS, lp, D_V), F32)],
        compiler_params=_cparams(),
    )(q, k, v, do)


def mla_heads_bwd(dq, dk, dv, cos_qk, sin_qk, rot_qk, place, wq, wk, wv, lp):
    tm = _row_tile(lp)

    def body(dq_ref, dk_ref, dv_ref, c_ref, s_ref, rot_ref, place_ref, wq_ref, wk_ref, wv_ref,
             dcq_ref, dckv_ref, dqp_ref, dkr_ref):
        dcq = jnp.zeros(dcq_ref.shape, F32)
        dckv = jnp.zeros(dckv_ref.shape, F32)
        dkr = jnp.zeros(dkr_ref.shape, F32)
        for h in range(N_HEADS):
            g = dq_ref[h]
            dqp = (g * c_ref[...] - _rot(g * s_ref[...], rot_ref[...])).astype(BF16)
            dqp_ref[h] = dqp
            dcq = dcq + _dot(dqp, wq_ref[h], 1, 1)
            dk_ = dk_ref[h]
            dckv = dckv + _dot(dk_.astype(BF16), wk_ref[h], 1, 1) + _dot(dv_ref[h].astype(BF16), wv_ref[h], 1, 1)
            dkr = dkr + _dot(dk_, place_ref[...], 1, 1, precision=lax.Precision.HIGHEST)
        dcq_ref[...] = dcq
        dckv_ref[...] = dckv
        dkr_ref[...] = dkr

    def hspec(n):
        return pl.BlockSpec((N_HEADS, tm, n), lambda i: (0, i, 0))

    def row(w):
        return pl.BlockSpec((tm, w), lambda i: (i, 0))

    def whole(a):
        return pl.BlockSpec(a.shape, functools.partial(lambda i, nd: (0,) * nd, nd=a.ndim))

    return pl.pallas_call(
        body, name="mla_heads_bwd", grid=(lp // tm,),
        in_specs=[hspec(D_QK), hspec(D_QK), hspec(D_V), row(D_QK), row(D_QK), whole(rot_qk), whole(place),
                  whole(wq), whole(wk), whole(wv)],
        out_specs=[row(Q_RANK), row(KV_RANK), hspec(D_QK), row(D_ROPE)],
        out_shape=[jax.ShapeDtypeStruct((lp, Q_RANK), F32), jax.ShapeDtypeStruct((lp, KV_RANK), F32),
                   jax.ShapeDtypeStruct((N_HEADS, lp, D_QK), BF16), jax.ShapeDtypeStruct((lp, D_ROPE), F32)],
        compiler_params=_cparams(),
    )(dq, dk, dv, cos_qk, sin_qk, rot_qk, place, wq, wk, wv)


def _rms_bwd(dy, x, g):
    r = lax.rsqrt(jnp.mean(x * x, axis=-1, keepdims=True) + RMS_EPS)
    n = x * r
    dn = dy * g
    dx = r * (dn - n * jnp.mean(dn * n, axis=-1, keepdims=True))
    return dx, jnp.sum(dy * n, axis=0, keepdims=True)


def mla_prep_bwd(dcq, dckv, dkr, proj, cos, sin, rot, qg, kvg, lp):
    def fn(row0, rv, pv):
        dcq_, dckv_, dkr_, cq, ckv, c, s = rv
        qg_, kvg_, rot_ = pv
        dxq, dgq = _rms_bwd(dcq_, cq, qg_)
        dxkv, dgkv = _rms_bwd(dckv_, ckv, kvg_)
        dkr_raw = dkr_ * c - _rot(dkr_ * s, rot_)
        return ((dxq, dxkv, dkr_raw), (dgq, dgkv))

    return rowwise(fn, [(dcq, Q_RANK, 0), (dckv, KV_RANK, 0), (dkr, D_ROPE, 0), (proj, Q_RANK, 0), (proj, KV_RANK, 2),
                        (cos, D_ROPE, 0), (sin, D_ROPE, 0)], [qg, kvg, rot],
                   [(Q_RANK, BF16), (KV_RANK, BF16), (D_ROPE, BF16)], [((1, Q_RANK), F32), ((1, KV_RANK), F32)],
                   name="mla_prep_bwd", lp=lp)


def _shift_down(x, d, rows):
    return jnp.where(rows >= d, pltpu.roll(x, d, 0), 0.0)


def _shift_up(x, d, rows, n):
    return jnp.where(rows < n - d, pltpu.roll(x, n - d, 0), 0.0)


_CONV_W = 128
_XB, _BG, _CG = 1024 // _CONV_W, 1536 // _CONV_W, 2048 // _CONV_W


def _conv_specs(lp):
    def pspec(base):
        return pl.BlockSpec((lp, _CONV_W), functools.partial(lambda c, base: (0, base + c), base=base))

    col = pl.BlockSpec((lp, _CONV_W), lambda c: (0, c))
    wspec = pl.BlockSpec((3, _CONV_W), lambda c: (0, c))
    bspec = pl.BlockSpec((1, _CONV_W), lambda c: (0, c))
    return pspec, col, wspec, bspec


def _conv_core(xbar, cg, w, bias, lp):
    rows = lax.broadcasted_iota(jnp.int32, (lp, _CONV_W), 0)
    u = jnp.where(rows >= PAD, cg * xbar, 0.0)
    u1 = _shift_down(u, 1, rows)
    u2 = _shift_down(u, 2, rows)
    y = bias + w[0:1] * u2 + w[1:2] * u1 + w[2:3] * u
    return rows, u, u1, u2, y


def conv_fwd(proj, w, bias, lp):
    pspec, col, wspec, bspec = _conv_specs(lp)

    def body(x_ref, b_ref, c_ref, w_ref, bias_ref, v_ref):
        _, _, _, _, y = _conv_core(x_ref[...], c_ref[...], w_ref[...], bias_ref[...], lp)
        v_ref[...] = (b_ref[...] * y).astype(BF16)

    return pl.pallas_call(
        body, name="conv_fwd", grid=(MIX // _CONV_W,),
        in_specs=[pspec(_XB), pspec(_BG), pspec(_CG), wspec, bspec], out_specs=col,
        out_shape=jax.ShapeDtypeStruct((lp, MIX), BF16), compiler_params=_cparams(),
    )(proj, proj, proj, w, bias)


def conv_bwd(dv, proj, w, bias, lp):
    pspec, col, wspec, bspec = _conv_specs(lp)

    def body(dv_ref, x_ref, b_ref, c_ref, w_ref, bias_ref, dx_ref, db_ref, dc_ref, dw_ref, dbias_ref):
        xbar, cg, w_ = x_ref[...], c_ref[...], w_ref[...]
        rows, u, u1, u2, y = _conv_core(xbar, cg, w_, bias_ref[...], lp)
        dv_ = dv_ref[...]
        db_ref[...] = (dv_ * y).astype(BF16)
        dy = dv_ * b_ref[...]
        dbias_ref[...] = jnp.sum(dy, axis=0, keepdims=True)
        dw_ref[0:1, :] = jnp.sum(dy * u2, axis=0, keepdims=True)
        dw_ref[1:2, :] = jnp.sum(dy * u1, axis=0, keepdims=True)
        dw_ref[2:3, :] = jnp.sum(dy * u, axis=0, keepdims=True)
        du = w_[2:3] * dy + w_[1:2] * _shift_up(dy, 1, rows, lp) + w_[0:1] * _shift_up(dy, 2, rows, lp)
        du = jnp.where(rows >= PAD, du, 0.0)
        dc_ref[...] = (du * xbar).astype(BF16)
        dx_ref[...] = (du * cg).astype(BF16)

    return pl.pallas_call(
        body, name="conv_bwd", grid=(MIX // _CONV_W,),
        in_specs=[col, pspec(_XB), pspec(_BG), pspec(_CG), wspec, bspec],
        out_specs=[col, col, col, wspec, bspec],
        out_shape=[jax.ShapeDtypeStruct((lp, MIX), BF16)] * 3 + [jax.ShapeDtypeStruct((3, MIX), F32),
                                                                jax.ShapeDtypeStruct((1, MIX), F32)],
        compiler_params=_cparams(),
    )(dv, proj, proj, proj, w, bias)


def _s5_disc(a_re, a_im, log_dt, b_re, b_im):
    dt = jnp.exp(log_dt)
    mag = jnp.exp(dt * a_re)
    ab_re, ab_im = mag * jnp.cos(dt * a_im), mag * jnp.sin(dt * a_im)
    den = a_re * a_re + a_im * a_im
    nr, ni = ab_re - 1.0, ab_im
    coef_re = (nr * a_re + ni * a_im) / den
    coef_im = (ni * a_re - nr * a_im) / den
    return ab_re, ab_im, coef_re * b_re - coef_im * b_im, coef_re * b_im + coef_im * b_re


_S5_ROWS = S5_GROUPS * S5_GROUP


def s5_prep(a_re, a_im, log_dt, b_re, b_im):
    def body(ar, ai, ld, br, bi, o0, o1, o2, o3):
        for o, v in zip((o0, o1, o2, o3), _s5_disc(ar[...], ai[...], ld[...], br[...], bi[...])):
            o[...] = v

    return pl.pallas_call(body, name="s5_prep",
                          out_shape=[jax.ShapeDtypeStruct((_S5_ROWS, S5_STATE), F32)] * 4)(a_re, a_im, log_dt, b_re, b_im)


def s5_prep_bwd(a_re, a_im, log_dt, b_re, b_im, d_ab_re, d_ab_im, d_bb_re, d_bb_im, sel):
    def body(ar, ai, ld, br, bi, g0, g1, g2, g3, sel_ref, da_re, da_im, dld, dbr, dbi):
        _, vjp = jax.vjp(_s5_disc, ar[...], ai[...], ld[...], br[...], bi[...])
        c_ar, c_ai, c_ld, c_br, c_bi = vjp((g0[...], g1[...], g2[...], g3[...]))
        s = sel_ref[...]
        hi = lax.Precision.HIGHEST
        da_re[...] = _dot(s, c_ar, 1, 0, precision=hi)
        da_im[...] = _dot(s, c_ai, 1, 0, precision=hi)
        dld[...] = jnp.sum(_dot(s, c_ld, 1, 0, precision=hi), axis=-1, keepdims=True)
        dbr[...] = c_br
        dbi[...] = c_bi

    g = jax.ShapeDtypeStruct((S5_GROUPS, S5_STATE), F32)
    full = jax.ShapeDtypeStruct((_S5_ROWS, S5_STATE), F32)
    return pl.pallas_call(body, name="s5_prep_bwd",
                          out_shape=[g, g, jax.ShapeDtypeStruct((S5_GROUPS, 1), F32), full, full],
                          )(a_re, a_im, log_dt, b_re, b_im, d_ab_re, d_ab_im, d_bb_re, d_bb_im, sel)


_SCAN_W = 128
_SCAN_STEPS = int(math.log2(SCAN_CHUNK))


def _cmul(ar, ai, br, bi):
    return ar * br - ai * bi, ar * bi + ai * br


def _scan_powers(ar, ai, reverse):
    pw = [(ar, ai)]
    for _ in range(_SCAN_STEPS):
        pw.append(_cmul(*pw[-1], *pw[-1]))
    rows = lax.broadcasted_iota(jnp.int32, (SCAN_CHUNK, ar.shape[-1]), 0)
    tr = jnp.broadcast_to(ar, rows.shape)
    ti = jnp.broadcast_to(ai, rows.shape)
    for k in range(_SCAN_STEPS):
        d = 2 ** k
        if reverse:
            live = rows < SCAN_CHUNK - d
            mr, mi = _cmul(tr, ti, _shift_up(tr, d, rows, SCAN_CHUNK), _shift_up(ti, d, rows, SCAN_CHUNK))
        else:
            live = rows >= d
            mr, mi = _cmul(tr, ti, _shift_down(tr, d, rows), _shift_down(ti, d, rows))
        tr = jnp.where(live, mr, tr)
        ti = jnp.where(live, mi, ti)
    return pw, rows, tr, ti


def s5_scan(bu, ab_re, ab_im, lp):
    n_chunks = lp // SCAN_CHUNK

    def body(bu_ref, ar_ref, ai_ref, s_ref):
        ar, ai = ar_ref[...], ai_ref[...]
        pw, rows, tr, ti = _scan_powers(ar, ai, False)

        def chunk(ci, carry):
            cr, cim = carry
            r0 = pl.multiple_of(ci * SCAN_CHUNK, SCAN_CHUNK)
            xr = bu_ref[0, pl.ds(r0, SCAN_CHUNK), :]
            xi = bu_ref[1, pl.ds(r0, SCAN_CHUNK), :]
            for k in range(_SCAN_STEPS):
                d = 2 ** k
                mr, mi = _cmul(pw[k][0], pw[k][1], _shift_down(xr, d, rows), _shift_down(xi, d, rows))
                xr, xi = xr + mr, xi + mi
            mr, mi = _cmul(tr, ti, cr, cim)
            xr, xi = xr + mr, xi + mi
            s_ref[0, pl.ds(r0, SCAN_CHUNK), :] = xr
            s_ref[1, pl.ds(r0, SCAN_CHUNK), :] = xi
            return xr[SCAN_CHUNK - 1:SCAN_CHUNK, :], xi[SCAN_CHUNK - 1:SCAN_CHUNK, :]

        zero = jnp.zeros((1, _SCAN_W), F32)
        lax.fori_loop(0, n_chunks, chunk, (zero, zero))

    spec = pl.BlockSpec((2, lp, _SCAN_W), lambda c: (0, 0, c))
    aspec = pl.BlockSpec((1, _SCAN_W), lambda c: (0, c))
    return pl.pallas_call(
        body, name="s5_scan", grid=(S5_LANES // _SCAN_W,), in_specs=[spec, aspec, aspec], out_specs=spec,
        out_shape=jax.ShapeDtypeStruct((2, lp, S5_LANES), F32), compiler_params=_cparams(),
    )(bu, ab_re, ab_im)


def s5_scan_bwd(ds, s, ab_re, ab_im, lp):
    n_chunks = lp // SCAN_CHUNK

    def body(ds_ref, s_ref, ar_ref, ai_ref, g_ref, da_ref):
        ar, ai = ar_ref[...], -ai_ref[...]
        pw, rows, tr, ti = _scan_powers(ar, ai, True)

        def chunk(k, carry):
            cr, cim, dar, dai = carry
            ci = n_chunks - 1 - k
            r0 = pl.multiple_of(ci * SCAN_CHUNK, SCAN_CHUNK)
            xr = ds_ref[0, pl.ds(r0, SCAN_CHUNK), :]
            xi = ds_ref[1, pl.ds(r0, SCAN_CHUNK), :]
            for j in range(_SCAN_STEPS):
                d = 2 ** j
                mr, mi = _cmul(pw[j][0], pw[j][1], _shift_up(xr, d, rows, SCAN_CHUNK), _shift_up(xi, d, rows, SCAN_CHUNK))
                xr, xi = xr + mr, xi + mi
            mr, mi = _cmul(tr, ti, cr, cim)
            xr, xi = xr + mr, xi + mi
            g_ref[0, pl.ds(r0, SCAN_CHUNK), :] = xr
            g_ref[1, pl.ds(r0, SCAN_CHUNK), :] = xi
            prev0 = pl.multiple_of(jnp.maximum(r0 - 8, 0), 8)
            live = (ci > 0).astype(F32)
            pr = s_ref[0, pl.ds(prev0, 8), :][7:8, :] * live
            pim = s_ref[1, pl.ds(prev0, 8), :][7:8, :] * live
            sr = s_ref[0, pl.ds(r0, SCAN_CHUNK), :]
            si = s_ref[1, pl.ds(r0, SCAN_CHUNK), :]
            sr = jnp.where(rows >= 1, pltpu.roll(sr, 1, 0), pr)
            si = jnp.where(rows >= 1, pltpu.roll(si, 1, 0), pim)
            dar = dar + jnp.sum(xr * sr + xi * si, axis=0, keepdims=True)
            dai = dai + jnp.sum(xi * sr - xr * si, axis=0, keepdims=True)
            return xr[0:1, :], xi[0:1, :], dar, dai

        zero = jnp.zeros((1, _SCAN_W), F32)
        _, _, dar, dai = lax.fori_loop(0, n_chunks, chunk, (zero, zero, zero, zero))
        da_ref[0] = dar
        da_ref[1] = dai

    spec = pl.BlockSpec((2, lp, _SCAN_W), lambda c: (0, 0, c))
    aspec = pl.BlockSpec((1, _SCAN_W), lambda c: (0, c))
    return pl.pallas_call(
        body, name="s5_scan_bwd", grid=(S5_LANES // _SCAN_W,), in_specs=[spec, spec, aspec, aspec],
        out_specs=[spec, pl.BlockSpec((2, 1, _SCAN_W), lambda c: (0, 0, c))],
        out_shape=[jax.ShapeDtypeStruct((2, lp, S5_LANES), F32), jax.ShapeDtypeStruct((2, 1, S5_LANES), F32)],
        compiler_params=_cparams(),
    )(ds, s, ab_re, ab_im)


S5_BLOCKS = 4
_S5_PER = S5_GROUPS // S5_BLOCKS


def _blockdiag(x):
    _, r, c = x.shape
    eye = jnp.eye(_S5_PER, dtype=x.dtype)
    x = x.reshape(S5_BLOCKS, _S5_PER, r, c)
    return (x[:, :, :, None, :] * eye[None, :, None, :, None]).reshape(S5_BLOCKS, _S5_PER * r, _S5_PER * c)


def _blockdiag_extract(m, r, c):
    return jnp.einsum('qgrgc->qgrc', m.reshape(S5_BLOCKS, _S5_PER, r, _S5_PER, c)).reshape(S5_GROUPS, r, c)


def bd_matmul(a, w, *, w_t, reduce, res=None, name):
    _, M, _ = a.shape
    n_w, _, k1, k2 = w.shape
    ka, kout = (k2, k1) if w_t else (k1, k2)
    tm = _row_tile(M)
    n_out, n_red = (1, n_w) if reduce else (n_w, 1)
    has_res = res is not None

    assert n_red <= 2

    def body(*refs):
        a_ref, w_ref = refs[0], refs[1]
        o_ref = refs[3] if has_res else refs[2]
        for q in range(S5_BLOCKS):
            cols = slice(q * kout, (q + 1) * kout)
            part = _dot(a_ref[:, q * ka:(q + 1) * ka].astype(BF16), w_ref[q], 1, 1 if w_t else 0)
            if n_red == 1:
                o_ref[:, cols] = part
            else:
                acc_ref = refs[-1]

                @pl.when(pl.program_id(2) == 0)
                def _():
                    acc_ref[:, cols] = part

                @pl.when(pl.program_id(2) == 1)
                def _():
                    tot = acc_ref[:, cols] + part
                    o_ref[:, cols] = tot + refs[2][:, cols] if has_res else tot

    if reduce:
        a_map, w_map = (lambda o, i, r: (r, i, 0)), (lambda o, i, r: (r, 0, 0, 0))
    else:
        a_map, w_map = (lambda o, i, r: (0, i, 0)), (lambda o, i, r: (o, 0, 0, 0))
    o_map = lambda o, i, r: (o, i, 0)
    in_specs = [pl.BlockSpec((None, tm, S5_BLOCKS * ka), a_map), pl.BlockSpec((None, S5_BLOCKS, k1, k2), w_map)]
    operands = [a, w]
    if has_res:
        in_specs.append(pl.BlockSpec((None, tm, S5_BLOCKS * kout), o_map))
        operands.append(res)
    return pl.pallas_call(
        body, name=name, grid=(n_out, M // tm, n_red), in_specs=in_specs,
        out_specs=pl.BlockSpec((None, tm, S5_BLOCKS * kout), o_map),
        out_shape=jax.ShapeDtypeStruct((n_out, M, S5_BLOCKS * kout), F32),
        scratch_shapes=[pltpu.VMEM((tm, S5_BLOCKS * kout), F32)] if n_red > 1 else [],
        compiler_params=_cparams(),
    )(*operands)


def bd_outer(a, b, name):
    na, M, wa = a.shape
    nb_, _, wb = b.shape
    ka, kb = wa // S5_BLOCKS, wb // S5_BLOCKS
    n_out = max(na, nb_)

    def body(a_ref, b_ref, o_ref):
        o_ref[...] = _dot(a_ref[...].astype(BF16), b_ref[...].astype(BF16), 0, 0)

    return pl.pallas_call(
        body, name=name, grid=(n_out, S5_BLOCKS),
        in_specs=[pl.BlockSpec((None, M, ka), (lambda o, q: (o, 0, q)) if na > 1 else (lambda o, q: (0, 0, q))),
                  pl.BlockSpec((None, M, kb), (lambda o, q: (o, 0, q)) if nb_ > 1 else (lambda o, q: (0, 0, q)))],
        out_specs=pl.BlockSpec((None, None, ka, kb), lambda o, q: (o, q, 0, 0)),
        out_shape=jax.ShapeDtypeStruct((n_out, S5_BLOCKS, ka, kb), F32), compiler_params=_cparams(),
    )(a, b)


def s5_u(proj, lp):
    def fn(row0, rv, pv):
        u, = rv
        return ((jnp.where(_row_mask(row0, u.shape), u, 0.0),), ())

    return rowwise(fn, [(proj, MIX, 5)], [], [(MIX, BF16)], name="s5_u", lp=lp)[0]


def s5_y(ys, proj, d, lp):
    def fn(row0, rv, pv):
        ys_, u = rv
        y = ys_ + pv[0] * u
        return ((y, _gelu(y)), ())

    return rowwise(fn, [(ys, MIX, 0), (proj, MIX, 5)], [d], [(MIX, F32), (MIX, BF16)], name="s5_y", lp=lp)


def s5_glu(z, y, b, lp):
    def fn(row0, rv, pv):
        z_, y_ = rv
        return ((_gelu(y_) * _sigmoid(z_ + pv[0]),), ())

    return rowwise(fn, [(z, MIX, 0), (y, MIX, 0)], [b], [(MIX, BF16)], name="s5_glu", lp=lp)[0]


def s5_glu_bwd(dgl, z, y, b, lp):
    def fn(row0, rv, pv):
        dgl_, z_, y_ = rv
        sg = _sigmoid(z_ + pv[0])
        dz = dgl_ * _gelu(y_) * sg * (1.0 - sg)
        return ((dgl_ * sg, dz), (jnp.sum(dz, axis=0, keepdims=True),))

    return rowwise(fn, [(dgl, MIX, 0), (z, MIX, 0), (y, MIX, 0)], [b], [(MIX, F32), (MIX, BF16)], [((1, MIX), F32)],
                   name="s5_glu_bwd", lp=lp)


def s5_y_bwd(dyg, y, proj, d, lp):
    def fn(row0, rv, pv):
        dyg_, y_, u = rv
        dy = dyg_ * _gelu_grad(y_)
        return ((dy, dy * pv[0]), (jnp.sum(dy * u, axis=0, keepdims=True),))

    return rowwise(fn, [(dyg, MIX, 0), (y, MIX, 0), (proj, MIX, 5)], [d], [(MIX, BF16), (MIX, F32)], [((1, MIX), F32)],
                   name="s5_y_bwd", lp=lp)


def s5_du(du, lp):
    def fn(row0, rv, pv):
        return ((jnp.where(_row_mask(row0, rv[0].shape), rv[0], 0.0),), ())

    return rowwise(fn, [(du, MIX, 0)], [], [(MIX, BF16)], name="s5_du", lp=lp)[0]


def merge_fwd(proj, ya, yb, yc, lp):
    def fn(row0, rv, pv):
        g0, g1, g2, a, b, c = rv
        return ((_sigmoid(g0) * a + _sigmoid(g1) * b + _sigmoid(g2) * c,), ())

    return rowwise(fn, [(proj, D_MODEL, 3), (proj, D_MODEL, 4), (proj, D_MODEL, 5), (ya, D_MODEL, 0), (yb, D_MODEL, 0),
                        (yc, D_MODEL, 0)], [], [(D_MODEL, BF16)], name="merge_fwd", lp=lp)[0]


def merge_bwd(dmix, proj, ya, yb, yc, lp):
    def fn(row0, rv, pv):
        dm, g0, g1, g2, a, b, c = rv
        outs_y, outs_g = [], []
        for g, yv in ((g0, a), (g1, b), (g2, c)):
            sg = _sigmoid(g)
            outs_y.append(dm * sg)
            outs_g.append(dm * yv * sg * (1.0 - sg))
        return (tuple(outs_y) + tuple(outs_g), ())

    return rowwise(fn, [(dmix, D_MODEL, 0), (proj, D_MODEL, 3), (proj, D_MODEL, 4), (proj, D_MODEL, 5),
                        (ya, D_MODEL, 0), (yb, D_MODEL, 0), (yc, D_MODEL, 0)], [], [(D_MODEL, BF16)] * 6,
                   name="merge_bwd", lp=lp)


def loss_head(h, tgt, lp):
    def fn(row0, rv, pv):
        h_, t_ = rv
        live = (row0 + lax.broadcasted_iota(jnp.int32, h_.shape, 0)) >= X0
        diff = jnp.where(live, h_ - t_, 0.0)
        ssq = jnp.sum(jnp.sum(diff * diff, axis=1, keepdims=True), axis=0, keepdims=True)
        return ((diff * (1.0 / D_MODEL),), (ssq * (0.5 / D_MODEL),))

    return rowwise(fn, [(h, D_MODEL, 0), (tgt, D_MODEL, 0)], [], [(D_MODEL, F32)], [((1, 1), F32)], name="loss_head", lp=lp)


def _s5_consts(W):
    ab_re_rep, ab_im_rep, bb_re, bb_im = s5_prep(W['s5_a_re'], W['s5_a_im'], W['s5_log_dt'], W['s5_b_re'], W['s5_b_im'])
    pick = lambda t: t.reshape(S5_GROUPS, S5_GROUP, S5_STATE)[:, 0].reshape(1, S5_LANES)
    bb = jnp.stack([_blockdiag(bb_re.reshape(S5_GROUPS, S5_GROUP, S5_STATE)),
                    _blockdiag(bb_im.reshape(S5_GROUPS, S5_GROUP, S5_STATE))]).astype(BF16)
    return pick(ab_re_rep), pick(ab_im_rep), bb


def layer_fwd(h, hb, W, tabs, lp, ffn1=None, ffn2=True):
    cos, sin, rot = tabs[:3]
    h1, h1b, sv1 = ffn1 if ffn1 is not None else ffn_fwd(h, hb, W['wg1'], W['wu1'], W['wd1'], W['ln1_g'], W['ln1_b'], lp)
    proj = matmul(h1b[None], W['w_in'][None], tb=True, name="proj")[0]
    cqn, ckvn, kr = mla_prep(proj, cos, sin, rot, W['q_norm_g'], W['kv_norm_g'], lp)
    q96, k96, v = mla_heads(cqn, ckvn, kr, *tabs[3:], W['wq'], W['wk'], W['wv'], lp)
    o = attn_fwd(q96, k96, v, lp)
    ya = matmul(o, W['mla_wo'], ab='r', bb='r', name="mla_out")[0]
    vconv = conv_fwd(proj, W['conv_w'], W['conv_b'], lp)
    yb = matmul(vconv[None], W['conv_wout'][None], name="conv_out")[0]
    ub = s5_u(proj, lp)
    ab_re, ab_im, bb = _s5_consts(W)
    bu = bd_matmul(ub[None], bb, w_t=False, reduce=False, name="s5_bu")
    s = s5_scan(bu, ab_re, ab_im, lp)
    ys = bd_matmul(s, W['s5_ct'], w_t=False, reduce=True, name="s5_cs")[0]
    y, ygb = s5_y(ys, proj, W['s5_d'], lp)
    zg = matmul(ygb[None], W['s5_wglu'][None], name="s5_glu_mm")[0]
    glb = s5_glu(zg, y, W['s5_b_glu'], lp)
    yc = matmul(glb[None], W['s5_wout'][None], name="s5_out")[0]
    mixed = merge_fwd(proj, ya, yb, yc, lp)
    z2, h2, h2b = mm_res_ln(mixed[None], W['w_o'][None], h1, W['ln2_g'], W['ln2_b'], scale=1.0, name="wo_ln", lp=lp)
    sv = dict(sv1=sv1, h1b=h1b, proj=proj, cqn=cqn, ckvn=ckvn, q96=q96, k96=k96, v=v, o=o, ya=ya,
              vconv=vconv, yb=yb, ub=ub, ab_re=ab_re, ab_im=ab_im, bb=bb, s=s, y=y, ygb=ygb, zg=zg, glb=glb, yc=yc,
              mixed=mixed, z2=z2)
    if not ffn2:
        return h2, h2b, sv
    h3, h3b, sv['sv3'] = ffn_fwd(h2, h2b, W['wg2'], W['wu2'], W['wd2'], W['ln3_g'], W['ln3_b'], lp)
    return h3, h3b, sv


def layer_bwd(dh3, sv, W, tabs, lp, ffn1=True, ffn2=True):
    cos, sin, rot = tabs[:3]
    proj = sv['proj']
    G = {}
    dh2 = dh3
    if ffn2:
        dh2, g3 = ffn_bwd(dh3, sv['sv3'], W['wg2'], W['wu2'], W['wd2'], W['ln3_g'], lp)
        G.update(wg2=g3['wg'], wu2=g3['wu'], wd2=g3['wd'], ln3_g=g3['ln_g'], ln3_b=g3['ln_b'])
    dz2, dz2b, G['ln2_g'], G['ln2_b'] = ln_bwd(dh2, sv['z2'], W['ln2_g'], fscale=1.0, name="wo_ln_bwd", lp=lp)
    dmix = matmul(dz2b[None], W['w_o'][None], tb=True, name="wo_dx")[0]
    G['w_o'] = matmul(sv['mixed'][None], dz2b[None], ta=True, out_dtype=WGRAD, name="wo_dw")[0]
    dya, dyb, dyc, dg0, dg1, dg2 = merge_bwd(dmix, proj, sv['ya'], sv['yb'], sv['yc'], lp)
    dgl = matmul(dyc[None], W['s5_wout'][None], tb=True, name="s5_out_dx")[0]
    G['s5_wout'] = matmul(sv['glb'][None], dyc[None], ta=True, out_dtype=WGRAD, name="s5_out_dw")[0]
    t1, dzb, G['s5_b_glu'] = s5_glu_bwd(dgl, sv['zg'], sv['y'], W['s5_b_glu'], lp)
    dyg = matmul(dzb[None], W['s5_wglu'][None], tb=True, res=t1[None], name="s5_glu_dx")[0]
    G['s5_wglu'] = matmul(sv['ygb'][None], dzb[None], ta=True, out_dtype=WGRAD, name="s5_glu_dw")[0]
    dyb_, du_d, G['s5_d'] = s5_y_bwd(dyg, sv['y'], proj, W['s5_d'], lp)
    ds = bd_matmul(dyb_[None], W['s5_ct'], w_t=True, reduce=False, name="s5_cs_dx")
    G['s5_ct'] = bd_outer(sv['s'], dyb_[None], "s5_cs_dw")
    g_adj, d_ab = s5_scan_bwd(ds, sv['s'], sv['ab_re'], sv['ab_im'], lp)
    du = bd_matmul(g_adj, sv['bb'], w_t=True, reduce=True, res=du_d[None], name="s5_bu_dx")[0]
    d_bb = bd_outer(sv['ub'][None], g_adj, "s5_bu_dw")
    du_b = s5_du(du, lp)
    onehot = (jnp.arange(S5_GROUP) == 0).astype(F32)
    spread = lambda t: (t.reshape(S5_GROUPS, 1, S5_STATE) * onehot[None, :, None]).reshape(_S5_ROWS, S5_STATE)
    take = lambda t: _blockdiag_extract(t, S5_GROUP, S5_STATE).reshape(_S5_ROWS, S5_STATE)
    sel = jnp.kron(jnp.eye(S5_GROUPS, dtype=F32), jnp.ones((1, S5_GROUP), F32))
    (G['s5_a_re'], G['s5_a_im'], G['s5_log_dt'], G['s5_b_re'], G['s5_b_im']) = s5_prep_bwd(
        W['s5_a_re'], W['s5_a_im'], W['s5_log_dt'], W['s5_b_re'], W['s5_b_im'],
        spread(d_ab[0]), spread(d_ab[1]), take(d_bb[0]), take(d_bb[1]), sel)
    dv = matmul(dyb[None], W['conv_wout'][None], tb=True, name="conv_out_dx")[0]
    G['conv_wout'] = matmul(sv['vconv'][None], dyb[None], ta=True, out_dtype=WGRAD, name="conv_out_dw")[0]
    dxbar, dbg, dcg, G['conv_w'], G['conv_b'] = conv_bwd(dv, proj, W['conv_w'], W['conv_b'], lp)
    do = matmul(dya[None], W['mla_wo'], tb=True, bb='o', out_dtype=BF16, name="mla_out_dx")
    G['mla_wo'] = matmul(sv['o'], dya[None], ta=True, ab='o', out_dtype=WGRAD, name="mla_out_dw")
    dq96, dk96, dvv = attn_bwd(sv['q96'], sv['k96'], sv['v'], do, lp)
    dcq, dckv, dqp, dkr = mla_heads_bwd(dq96, dk96, dvv, *tabs[3:], W['wq'], W['wk'], W['wv'], lp)
    G['wq'] = matmul(sv['cqn'][None], dqp, ta=True, bb='o', out_dtype=WGRAD, name="mla_dwq")
    G['wk'] = matmul(sv['ckvn'][None], dk96, ta=True, bb='o', out_dtype=WGRAD, name="mla_dwk")
    G['wv'] = matmul(sv['ckvn'][None], dvv, ta=True, bb='o', out_dtype=WGRAD, name="mla_dwv")
    dcq_raw, dckv_raw, dkr_raw, G['q_norm_g'], G['kv_norm_g'] = mla_prep_bwd(
        dcq, dckv, dkr, proj, cos, sin, rot, W['q_norm_g'], W['kv_norm_g'], lp)
    zeros = lambda n: jnp.zeros((lp, n), BF16)
    dproj = jnp.concatenate([dcq_raw, dkr_raw, zeros(96), dckv_raw, zeros(256), dxbar, dbg, dcg, du_b, dg0, dg1, dg2], axis=1)
    dh1 = matmul(dproj[None], W['w_in'][None], res=dz2[None], res_scale=ALPHA, name="proj_dx")[0]
    G['w_in'] = matmul(dproj[None], sv['h1b'][None], ta=True, out_dtype=WGRAD, name="proj_dw")[0]
    if not ffn1:
        return dh1, G
    dh0, g1 = ffn_bwd(dh1, sv['sv1'], W['wg1'], W['wu1'], W['wd1'], W['ln1_g'], lp)
    G.update(wg1=g1['wg'], wu1=g1['wu'], wd1=g1['wd'], ln1_g=g1['ln_g'], ln1_b=g1['ln_b'])
    return dh0, G


def _nat_cols(st):
    return jnp.transpose(st, (1, 0, 2)).reshape(st.shape[1], -1)


def _shard_cols(nat):
    k, n = nat.shape
    return jnp.transpose(nat.reshape(k, N_SHARD, n // N_SHARD), (1, 0, 2))


def _win_pad(wt):
    z = lambda n: jnp.zeros((n, wt.shape[1]), wt.dtype)
    return jnp.concatenate([wt[0:384], wt[640:672], z(96), wt[384:640], z(256), wt[672:]], axis=0)


def _win_unpad(wp):
    return jnp.concatenate([wp[0:384], wp[512:768], wp[384:416], wp[1024:]], axis=0)


_BIG = [('ffn1_w_gate', 'T'), ('ffn1_w_up', 'T'), ('ffn1_w_down', 0), ('w_in', 'T'), ('mla_w_uq', 1), ('mla_w_ukv', 1),
        ('mla_w_o', 1), ('conv_w_out', 1), ('s5_w_glu', 0), ('s5_w_out', 1), ('w_o', 0),
        ('ffn2_w_gate', 'T'), ('ffn2_w_up', 'T'), ('ffn2_w_down', 0)]
_REPL = ['ln1_g', 'ln1_b', 'mla_q_norm_g', 'mla_kv_norm_g', 'conv_b', 's5_a_re', 's5_a_im', 's5_log_dt', 's5_b_re',
         's5_b_im', 's5_c_re', 's5_c_im', 's5_d', 's5_b_glu', 'ln2_g', 'ln2_b', 'ln3_g', 'ln3_b']


def compute_weights(st, small):
    W = {}
    for t in ('1', '2'):
        if 'ffn%s_w_gate' % t in st:
            W['wg' + t], W['wu' + t], W['wd' + t] = (st['ffn%s_w_%s' % (t, p)] for p in ('gate', 'up', 'down'))
    if 'w_in' in st:
        W.update(_mixer_weights(st))
    if small is not None:
        W.update(_small_weights(small))
    return W


def _mixer_weights(st):
    W = {}
    W['w_in'] = _win_pad(st['w_in'].reshape(D_IN, D_MODEL))
    W['wq'] = jnp.transpose(_nat_cols(st['mla_w_uq']).reshape(Q_RANK, N_HEADS, D_QK), (1, 0, 2))
    ukv = jnp.transpose(_nat_cols(st['mla_w_ukv']).reshape(KV_RANK, N_HEADS, D_NOPE + D_V), (1, 0, 2))
    W['wk'] = jnp.concatenate([ukv[:, :, :D_NOPE], jnp.zeros((N_HEADS, KV_RANK, D_ROPE), ukv.dtype)], axis=2)
    W['wv'] = ukv[:, :, D_NOPE:]
    W['mla_wo'] = _nat_cols(st['mla_w_o']).reshape(N_HEADS, D_V, D_MODEL)
    W['conv_wout'] = _nat_cols(st['conv_w_out'])
    W['s5_wglu'] = st['s5_w_glu'].reshape(MIX, MIX)
    W['s5_wout'] = _nat_cols(st['s5_w_out'])
    W['w_o'] = st['w_o'].reshape(D_MODEL, D_MODEL)
    return W


def _small_weights(small):
    W = {}
    W['conv_w'] = small['conv_w']
    for n in ('ln1_g', 'ln1_b', 'ln2_g', 'ln2_b', 'ln3_g', 'ln3_b', 'conv_b', 's5_b_glu'):
        W[n] = small[n].reshape(1, -1)
    W['q_norm_g'] = small['mla_q_norm_g'].reshape(1, -1)
    W['kv_norm_g'] = small['mla_kv_norm_g'].reshape(1, -1)
    W['s5_d'] = small['s5_d'].reshape(1, MIX)
    rep = lambda t: jnp.repeat(t, S5_GROUP, axis=0)
    W['s5_a_re'], W['s5_a_im'] = rep(small['s5_a_re']), rep(small['s5_a_im'])
    W['s5_log_dt'] = jnp.broadcast_to(rep(small['s5_log_dt'].reshape(S5_GROUPS, 1)), (_S5_ROWS, S5_STATE))
    tr = lambda t: jnp.transpose(t, (0, 2, 1)).reshape(_S5_ROWS, S5_STATE)
    W['s5_b_re'], W['s5_b_im'] = tr(small['s5_b_re']), tr(small['s5_b_im'])
    ct = lambda t: _blockdiag(jnp.transpose(t, (0, 2, 1)))
    W['s5_ct'] = jnp.stack([ct(small['s5_c_re']), -ct(small['s5_c_im'])]).astype(BF16)
    return W


def reference_grads(G, ffn=True):
    R = {}
    for t in ('1', '2') if ffn else ():
        R['ffn%s_w_gate' % t] = G['wg' + t].reshape(D_FF, D_MODEL).T
        R['ffn%s_w_up' % t] = G['wu' + t].reshape(D_FF, D_MODEL).T
        R['ffn%s_w_down' % t] = G['wd' + t].reshape(D_FF, D_MODEL)
    R['w_in_t'] = _win_unpad(G['w_in'])
    if ffn:
        R['w_in'] = R['w_in_t'].T
    R['mla_w_uq'] = jnp.transpose(G['wq'], (1, 0, 2)).reshape(Q_RANK, -1)
    R['mla_w_ukv'] = jnp.transpose(jnp.concatenate([G['wk'][:, :, :D_NOPE], G['wv']], axis=2), (1, 0, 2)).reshape(KV_RANK, -1)
    R['mla_w_o'] = G['mla_wo'].reshape(N_HEADS * D_V, D_MODEL)
    R['conv_w'], R['conv_w_out'] = G['conv_w'], G['conv_wout']
    R['s5_w_glu'], R['s5_w_out'], R['w_o'] = G['s5_wglu'], G['s5_wout'], G['w_o']
    for n in ('ln1_g', 'ln1_b', 'ln2_g', 'ln2_b', 'ln3_g', 'ln3_b', 'conv_b', 's5_b_glu'):
        if n in G:
            R[n] = G[n].reshape(-1)
    R['mla_q_norm_g'], R['mla_kv_norm_g'] = G['q_norm_g'].reshape(-1), G['kv_norm_g'].reshape(-1)
    R['s5_d'] = G['s5_d'].reshape(S5_GROUPS, S5_GROUP)
    R['s5_a_re'], R['s5_a_im'], R['s5_log_dt'] = G['s5_a_re'], G['s5_a_im'], G['s5_log_dt'].reshape(-1)
    untr = lambda t: jnp.transpose(t.reshape(S5_GROUPS, S5_GROUP, S5_STATE), (0, 2, 1))
    R['s5_b_re'], R['s5_b_im'] = untr(G['s5_b_re']), untr(G['s5_b_im'])
    unct = lambda t: jnp.transpose(_blockdiag_extract(t, S5_STATE, S5_GROUP), (0, 2, 1))
    R['s5_c_re'], R['s5_c_im'] = unct(G['s5_ct'][0]), -unct(G['s5_ct'][1])
    return R


_ANY = pl.BlockSpec(memory_space=pl.ANY)
LANES = 1024


def _place():
    x, y, c = lax.axis_index("x"), lax.axis_index("y"), lax.axis_index("c")
    chips = [(1 - x, y), (x, 1 - y), (1 - x, 1 - y)]
    return x, y, c, chips


def _rows_of(c, half):
    return pl.ds(pl.multiple_of(c * half, 8), half)


def all_gather_shards(srcs, exact):
    n, m = len(srcs), len(exact)
    halves = [s.shape[0] // 2 for s in srcs]

    def body(*refs):
        s_refs, e_refs = refs[:n], refs[n:n + m]
        o_refs, eo_refs = refs[n + m:2 * n + m], refs[2 * n + m:2 * n + 2 * m]
        send, recv, esend, erecv, osend, orecv, lsem = refs[2 * n + 2 * m:]
        x, y, c, chips = _place()
        me = 2 * x + y
        sibling = (x, y, 1 - c)
        own = [pltpu.make_async_remote_copy(src_ref=s_refs[k], dst_ref=o_refs[k].at[me], send_sem=osend.at[k],
                                            recv_sem=orecv.at[k], device_id=sibling, device_id_type=MESH) for k in range(n)]
        local = [pltpu.make_async_copy(e_refs[k], eo_refs[k].at[me], lsem.at[k]) for k in range(m)]
        for cp in own + local:
            cp.start()

        def copy(k, s, src, idx, half_c, to):
            return pltpu.make_async_remote_copy(
                src_ref=src, dst_ref=o_refs[k].at[idx, _rows_of(half_c, halves[k])], send_sem=send.at[6 * k + s],
                recv_sem=recv.at[6 * k + s], device_id=to, device_id_type=MESH)

        def ecopy(k, j, idx, to):
            return pltpu.make_async_remote_copy(src_ref=e_refs[k], dst_ref=eo_refs[k].at[idx], send_sem=esend.at[3 * k + j],
                                                recv_sem=erecv.at[3 * k + j], device_id=to, device_id_type=MESH)

        sends = []
        for k in range(n):
            mine = s_refs[k].at[_rows_of(c, halves[k])]
            sends += [copy(k, j, mine, me, c, (*chip, c)) for j, chip in enumerate(chips)]
        for k in range(m):
            sends += [ecopy(k, j, me, (*chip, c)) for j, chip in enumerate(chips)]
        for cp in sends:
            cp.start()
        for j, chip in enumerate(chips):
            idx = 2 * chip[0] + chip[1]
            for k in range(n):
                landed = o_refs[k].at[idx, _rows_of(c, halves[k])]
                copy(k, j, landed, idx, c, sibling).wait_recv()
                fwd = copy(k, 3 + j, landed, idx, c, sibling)
                fwd.start()
                sends.append(fwd)
        for j, chip in enumerate(chips):
            idx = 2 * chip[0] + chip[1]
            for k in range(n):
                copy(k, 3 + j, s_refs[k].at[_rows_of(c, halves[k])], idx, 1 - c, sibling).wait_recv()
            for k in range(m):
                ecopy(k, j, idx, sibling).wait_recv()
        for cp in sends:
            cp.wait_send()
        for cp in own + local:
            cp.wait()

    outs = pl.pallas_call(
        body, name="all_gather_weights", in_specs=[_ANY] * (n + m), out_specs=[_ANY] * (n + m),
        out_shape=[jax.ShapeDtypeStruct((N_SHARD,) + a.shape, a.dtype) for a in list(srcs) + list(exact)],
        scratch_shapes=[pltpu.SemaphoreType.DMA((6 * n,)), pltpu.SemaphoreType.DMA((6 * n,)),
                        pltpu.SemaphoreType.DMA((3 * m,)), pltpu.SemaphoreType.DMA((3 * m,)),
                        pltpu.SemaphoreType.DMA((n,)), pltpu.SemaphoreType.DMA((n,)), pltpu.SemaphoreType.DMA((m,))],
    )(*srcs, *exact)
    return outs[:n], outs[n:]


def rs_pair_swap(gs):
    n = len(gs)

    def body(*refs):
        g_refs, r_refs, send, recv = refs[:n], refs[n:2 * n], refs[2 * n], refs[2 * n + 1]
        x, y, c, _ = _place()
        copies = [pltpu.make_async_remote_copy(
            src_ref=g_refs[k].at[pl.ds(0, N_SHARD), _rows_of(1 - c, gs[k].shape[1] // 2)], dst_ref=r_refs[k],
            send_sem=send.at[k], recv_sem=recv.at[k], device_id=(x, y, 1 - c), device_id_type=MESH) for k in range(n)]
        for cp in copies:
            cp.start()
        for cp in copies:
            cp.wait()

    return pl.pallas_call(
        body, name="grad_pair_swap", in_specs=[_ANY] * n, out_specs=[_ANY] * n,
        out_shape=[jax.ShapeDtypeStruct((N_SHARD, g.shape[1] // 2, g.shape[2]), g.dtype) for g in gs],
        scratch_shapes=[pltpu.SemaphoreType.DMA((n,)), pltpu.SemaphoreType.DMA((n,))],
    )(*gs)


def _group_tile(half, n_cols, n_arrays):
    budget = (20 * 2 ** 20) // (6 * n_arrays)
    fits = [t for t in range(8, half + 1, 8) if half % t == 0 and t * n_cols * 4 <= budget]
    return max(fits) if fits else 8


def rs_pair_add(gs, rs, cidx, out_dtype, name):
    n = len(gs)
    _, K, cols = gs[0].shape
    half = K // 2
    tr = _group_tile(half, cols, n)
    nb = half // tr

    def body(c_ref, *refs):
        for g_ref, r_ref, o_ref in zip(refs[:n], refs[n:2 * n], refs[2 * n:]):
            o_ref[...] = (g_ref[...].astype(F32) + r_ref[...].astype(F32)).astype(out_dtype)

    gspec = pl.BlockSpec((None, tr, cols), lambda j, i, c: (j, c[0] * nb + i, 0))
    rspec = pl.BlockSpec((None, tr, cols), lambda j, i, c: (j, i, 0))
    return pl.pallas_call(
        body, name=name,
        grid_spec=pltpu.PrefetchScalarGridSpec(num_scalar_prefetch=1, grid=(N_SHARD, nb), in_specs=[gspec] * n + [rspec] * n,
                                               out_specs=[rspec] * n),
        out_shape=[jax.ShapeDtypeStruct((N_SHARD, half, cols), out_dtype)] * n,
        compiler_params=_cparams(),
    )(cidx, *gs, *rs)


def rs_chip_sum(qs, nl, cidx, name):
    n = len(qs)
    _, half, cols = qs[0].shape
    tr = _group_tile(half, cols, n)
    nb = half // tr

    def body(c_ref, *refs):
        for k, q_ref in enumerate(refs[:n]):
            o_ref = refs[n + k // nl]
            o_ref[k % nl] = ((q_ref[0].astype(F32) + q_ref[1].astype(F32)) + q_ref[2].astype(F32)) + q_ref[3].astype(F32)

    return pl.pallas_call(
        body, name=name,
        grid_spec=pltpu.PrefetchScalarGridSpec(
            num_scalar_prefetch=1, grid=(nb,),
            in_specs=[pl.BlockSpec((N_SHARD, tr, cols), lambda i, c: (0, i, 0))] * n,
            out_specs=[pl.BlockSpec((nl, tr, cols), lambda i, c: (0, c[0] * nb + i, 0))] * (n // nl)),
        out_shape=[jax.ShapeDtypeStruct((nl, 2 * half, cols), F32)] * (n // nl),
        compiler_params=_cparams(),
    )(cidx, *qs)


def rs_pair_gather(fs, name):
    n = len(fs)

    def body(*refs):
        f_refs, send, recv = refs[n:2 * n], refs[2 * n], refs[2 * n + 1]
        x, y, c, _ = _place()
        copies = []
        for k in range(n):
            rows = f_refs[k].at[pl.ds(0, fs[k].shape[0]), _rows_of(c, fs[k].shape[1] // 2)]
            copies.append(pltpu.make_async_remote_copy(src_ref=rows, dst_ref=rows, send_sem=send.at[k], recv_sem=recv.at[k],
                                                       device_id=(x, y, 1 - c), device_id_type=MESH))
        for cp in copies:
            cp.start()
        for cp in copies:
            cp.wait()

    return pl.pallas_call(
        body, name=name, in_specs=[_ANY] * n, out_specs=[_ANY] * n,
        out_shape=[jax.ShapeDtypeStruct(f.shape, f.dtype) for f in fs],
        input_output_aliases={k: k for k in range(n)},
        scratch_shapes=[pltpu.SemaphoreType.DMA((n,)), pltpu.SemaphoreType.DMA((n,))],
    )(*fs)


_HBM = pl.BlockSpec(memory_space=pltpu.HBM)
_SEM = pl.BlockSpec(memory_space=pltpu.SEMAPHORE)
_EFFECT = pltpu.SideEffectType.DATAFLOW_SIDE_EFFECTING


def _in_hbm(a):
    return pltpu.with_memory_space_constraint(a, pltpu.HBM)


def split_start(name, srcs, lands, after, copies_fn, n_copies):
    n = len(srcs)

    def body(*refs):
        for cp in copies_fn(refs[:n], refs[n:2 * n], refs[2 * n + 1], refs[2 * n + 2]):
            cp.start()
        refs[-1][...] = jnp.zeros_like(refs[-1])

    bufs = list(srcs) + list(lands)
    outs = pl.pallas_call(
        body, name=name,
        out_shape=(pltpu.SemaphoreType.DMA((n_copies,)), pltpu.SemaphoreType.DMA((n_copies,)),
                   *[pltpu.HBM(a.shape, a.dtype) for a in bufs], jax.ShapeDtypeStruct((8, 128), F32)),
        in_specs=[_HBM] * (2 * n) + [_ANY],
        out_specs=(_SEM, _SEM, *[_HBM] * (2 * n), pl.BlockSpec(memory_space=pltpu.VMEM)),
        input_output_aliases={i: 2 + i for i in range(2 * n)},
        compiler_params=pltpu.CompilerParams(has_side_effects=_EFFECT),
    )(*[_in_hbm(a) for a in bufs], after)
    return outs[0], outs[1], outs[2:2 + n], outs[2 + n:2 + 2 * n], outs[-1]


def split_wait(name, send, recv, srcs, lands, after, copies_fn, which=None):
    n = len(srcs)

    def body(*refs):
        copies = copies_fn(refs[:n], refs[n:2 * n], refs[2 * n], refs[2 * n + 1], which)
        for cp in copies:
            cp.wait_send()
        for cp in copies:
            cp.wait_recv()

    bufs = list(srcs) + list(lands)
    outs = pl.pallas_call(
        body, name=name, out_shape=tuple(pltpu.HBM(a.shape, a.dtype) for a in bufs),
        in_specs=[_HBM] * (2 * n) + [_SEM, _SEM, _ANY], out_specs=tuple([_HBM] * (2 * n)),
        input_output_aliases={i: i for i in range(2 * n)},
        compiler_params=pltpu.CompilerParams(has_side_effects=_EFFECT),
    )(*bufs, send, recv, after)
    return list(outs[:n]), list(outs[n:])


def _gather_copies(s_refs, l_refs, send, recv, which=None):
    x, y, c, chips = _place()
    me = 2 * x + y
    out = []
    for k in (range(len(s_refs)) if which is None else which):
        s, l = s_refs[k], l_refs[k]
        rows = _rows_of(c, s.shape[0] // 2)
        for j, chip in enumerate(chips):
            out.append(pltpu.make_async_remote_copy(src_ref=s.at[rows], dst_ref=l.at[me, rows], send_sem=send.at[4 * k + j],
                                                    recv_sem=recv.at[4 * k + j], device_id=(*chip, c), device_id_type=MESH))
        out.append(pltpu.make_async_remote_copy(src_ref=s, dst_ref=l.at[me], send_sem=send.at[4 * k + 3],
                                                recv_sem=recv.at[4 * k + 3], device_id=(x, y, 1 - c), device_id_type=MESH))
    return out


def _scatter_copies(s_refs, l_refs, send, recv, which=None):
    x, y, c, chips = _place()
    me = 2 * x + y
    return [pltpu.make_async_remote_copy(src_ref=s_refs[k].at[2 * chip[0] + chip[1]], dst_ref=l_refs[k].at[me],
                                         send_sem=send.at[3 * k + j], recv_sem=recv.at[3 * k + j], device_id=(*chip, c),
                                         device_id_type=MESH)
            for k in (range(len(s_refs)) if which is None else which) for j, chip in enumerate(chips)]


def gather_forward(lands, name):
    n = len(lands)

    def body(*refs):
        l_refs, send, recv = refs[n:2 * n], refs[2 * n], refs[2 * n + 1]
        x, y, c, chips = _place()
        copies = []
        for k in range(n):
            rows = _rows_of(c, lands[k].shape[1] // 2)
            for j, chip in enumerate(chips):
                part = l_refs[k].at[2 * chip[0] + chip[1], rows]
                copies.append(pltpu.make_async_remote_copy(src_ref=part, dst_ref=part, send_sem=send.at[3 * k + j],
                                                           recv_sem=recv.at[3 * k + j], device_id=(x, y, 1 - c),
                                                           device_id_type=MESH))
        for cp in copies:
            cp.start()
        for cp in copies:
            cp.wait()

    return pl.pallas_call(
        body, name=name, in_specs=[_ANY] * n, out_specs=[_ANY] * n,
        out_shape=[jax.ShapeDtypeStruct(a.shape, a.dtype) for a in lands],
        input_output_aliases={k: k for k in range(n)},
        scratch_shapes=[pltpu.SemaphoreType.DMA((3 * n,)), pltpu.SemaphoreType.DMA((3 * n,))],
    )(*lands)


def rs_partials(gs, wire, cidx, tag):
    rs = rs_pair_swap(gs)
    groups = {}
    for k, g in enumerate(gs):
        groups.setdefault((g.shape, jnp.dtype(wire[k]).name), []).append(k)
    ps = [None] * len(gs)
    for gi, ks in enumerate(groups.values()):
        outs = rs_pair_add([gs[k] for k in ks], [rs[k] for k in ks], cidx, wire[ks[0]], "grad_pair_add_%s%d" % (tag, gi))
        for k, o in zip(ks, outs):
            ps[k] = o
    return ps


def rs_finish(items, tag):
    cidx = lax.axis_index("c").astype(jnp.int32).reshape(1)
    groups = {}
    for i, it in enumerate(items):
        groups.setdefault((it[0].shape, len(it), it[0].dtype.name), []).append(i)
    fs = [None] * len(items)
    for gi, ids in enumerate(groups.values()):
        outs = rs_chip_sum([q for i in ids for q in items[i]], len(items[ids[0]]), cidx, "grad_chip_sum_%s%d" % (tag, gi))
        for i, o in zip(ids, outs):
            fs[i] = o
    return rs_pair_gather(fs, "grad_pair_gather_" + tag)


def adamw(w, g, m, v, name):
    shape = w.shape
    if w.ndim == 2:
        block, grid, index = shape, (1,), (lambda i: (0, 0))
    else:
        slab = shape[2:]
        unit = 4 * int(np.prod(slab[:-2] or (1,))) * (-(-slab[-1] // 128) * 128)
        if len(slab) >= 2:
            unit *= -(-slab[-2] // 8) * 8
        k = shape[1]
        tr = k
        if k * unit > 2 ** 21:
            tr = max(t for t in range(8, k, 8) if k % t == 0 and t * unit <= 2 ** 21)
        block, grid = (None, tr) + tuple(slab), (shape[0], k // tr)
        index = lambda l, i: (l, i) + (0,) * len(slab)
        if tr < min(k, 64) and len(slab) == 1:
            tc = max(t for t in range(128, slab[0] + 1, 128) if slab[0] % t == 0 and k * t * 4 <= 2 ** 21)
            block, grid = (None, k, tc), (shape[0], slab[0] // tc)
            index = lambda l, i: (l, 0, i)

    def body(w_ref, g_ref, m_ref, v_ref, d_ref, nm_ref, nv_ref):
        g_ = g_ref[...]
        m_new = ADAM_B1 * m_ref[...] + (1.0 - ADAM_B1) * g_
        v_new = ADAM_B2 * v_ref[...] + (1.0 - ADAM_B2) * (g_ * g_)
        m_hat = m_new / (1.0 - ADAM_B1 ** ADAM_STEP)
        v_hat = v_new / (1.0 - ADAM_B2 ** ADAM_STEP)
        d_ref[...] = -ADAM_LR * (m_hat / (jnp.sqrt(v_hat) + ADAM_EPS) + ADAM_WD * w_ref[...])
        nm_ref[...] = m_new
        nv_ref[...] = v_new

    spec = pl.BlockSpec(block, index)
    return pl.pallas_call(
        body, name=name, grid=grid, in_specs=[spec] * 4, out_specs=[spec] * 3,
        out_shape=[jax.ShapeDtypeStruct(shape, F32)] * 3, compiler_params=_cparams(),
    )(w, g, m, v)


_WEIGHTS = ['meta', 'ffn1_w_gate', 'ffn1_w_up', 'ffn1_w_down', 'ln1_g', 'ln1_b', 'w_in', 'mla_q_norm_g', 'mla_w_uq',
            'mla_kv_norm_g', 'mla_w_ukv', 'mla_w_o', 'conv_w', 'conv_b', 'conv_w_out', 's5_a_re', 's5_a_im', 's5_log_dt',
            's5_b_re', 's5_b_im', 's5_c_re', 's5_c_im', 's5_d', 's5_w_glu', 's5_b_glu', 's5_w_out', 'w_o', 'ln2_g', 'ln2_b',
            'ffn2_w_gate', 'ffn2_w_up', 'ffn2_w_down', 'ln3_g', 'ln3_b']


def _pad_to(flat, n):
    return jnp.concatenate([flat, jnp.zeros((n - flat.shape[0],), flat.dtype)])


def _shard_of(full, axis):
    if axis == 1:
        return _shard_cols(full)
    if axis == 'T':
        return full.T.reshape(N_SHARD, full.shape[1] // N_SHARD, full.shape[0])
    return full.reshape(N_SHARD, full.shape[0] // N_SHARD, full.shape[1])


_FFN_KEY = {'gate': 'wg', 'up': 'wu', 'down': 'wd'}


def _pad_rows(a, axis):
    k = a.shape[axis]
    extra = -k % 32
    if not extra:
        return a
    return jnp.pad(a, [(0, extra) if d == axis else (0, 0) for d in range(a.ndim)])


def _step(env):
    w = {n: env[n] for n in _WEIGHTS}
    mom = {n: env['m_' + n] for n in _WEIGHTS}
    var = {n: env['v_' + n] for n in _WEIGHTS}
    cidx = lax.axis_index("c").astype(jnp.int32).reshape(1)
    chip = 2 * lax.axis_index("x") + lax.axis_index("y")
    big_names = [n for n, _ in _BIG]
    nb = len(big_names)

    kept_t = [n for n, a in _BIG if a == 'T']
    own = {n: (jnp.swapaxes(w[n], 1, 2) if n in kept_t else w[n]) for n in big_names}
    first = [n for n in big_names if n.startswith('ffn1')]
    mix = [n for n in big_names if not n.startswith('ffn')]
    last = [n for n in big_names if n.startswith('ffn2')]
    rest = mix + last
    nm, nr = len(mix), len(mix) + len(last)
    src = lambda n, li: _pad_rows(own[n][li].astype(BF16), 0)
    gathered_first, (conv_w_st, meta_st) = all_gather_shards([src(n, 0) for n in first], [w['conv_w'], w['meta']])
    later = [src(n, 0) for n in rest] + [src(n, 1) for n in big_names]
    lands = [lax.empty((N_SHARD,) + s.shape, BF16) for s in later]
    g_send, g_recv, later_t, lands_t, token = split_start("gather_start", later, lands, gathered_first[0], _gather_copies,
                                                          4 * len(later))

    def weights_of(names, st, li, with_small):
        small = None
        if with_small:
            small = {n: w[n][li] for n in _REPL}
            small['conv_w'] = _nat_cols(conv_w_st[:, li])
        return compute_weights({n: a[:, :own[n].shape[1]] for n, a in zip(names, st)}, small)

    x2d = env['x'][0]
    lp = x2d.shape[0] + X0
    tabs = _rope_tables(lp)
    h = jnp.concatenate([jnp.zeros((PAD, D_MODEL), F32), _nat_cols(meta_st), x2d], axis=0) + token[0, 0]
    W0 = weights_of(first, gathered_first, 0, True)
    ffn1 = ffn_fwd(h, h.astype(BF16), W0['wg1'], W0['wu1'], W0['wd1'], W0['ln1_g'], W0['ln1_b'], lp)
    later_t, lands_t = split_wait("gather0_wait", g_send, g_recv, later_t, lands_t, ffn1[0], _gather_copies, range(nm))
    W0.update(weights_of(mix, gather_forward(lands_t[:nm], "gather0_forward"), 0, False))
    h, hb, sv0 = layer_fwd(None, None, W0, tabs, lp, ffn1=ffn1, ffn2=False)
    later_t, lands_t = split_wait("gather0b_wait", g_send, g_recv, later_t, lands_t, h, _gather_copies, range(nm, nr))
    W0.update(weights_of(last, gather_forward(lands_t[nm:nr], "gather0b_forward"), 0, False))
    h, hb, sv0['sv3'] = ffn_fwd(h, hb, W0['wg2'], W0['wu2'], W0['wd2'], W0['ln3_g'], W0['ln3_b'], lp)
    _, lands_t = split_wait("gather1_wait", g_send, g_recv, later_t, lands_t, h, _gather_copies, range(nr, len(later)))
    W1 = weights_of(big_names, gather_forward(lands_t[nr:], "gather1_forward"), 1, True)
    h, hb, sv1 = layer_fwd(h, hb, W1, tabs, lp)
    tgt = jnp.concatenate([jnp.zeros((X0, D_MODEL), F32), env['loss_target'][0]], axis=0)
    dh, loss_part = loss_head(h, tgt, lp)
    loss = lax.psum(loss_part[0, 0], ("x", "y", "c"))

    def shards(G, names):
        full = None if all(n.startswith('ffn') for n in names) else reference_grads(G, ffn=False)

        def one(n, a):
            if n.startswith('ffn'):
                return G[_FFN_KEY[n.split('_')[-1]] + n[3]]
            if n == 'w_in':
                return full['w_in_t'].reshape(N_SHARD, D_IN // N_SHARD, D_MODEL)
            return _shard_of(full[n], a)

        return [_pad_rows(one(n, a), 1) for n, a in _BIG if n in names]

    def scatter_start(name, ps, after):
        qs = [lax.dynamic_update_slice_in_dim(jnp.zeros_like(p), lax.dynamic_slice_in_dim(p, chip, 1, axis=0), chip, axis=0)
              for p in ps]
        return split_start(name, ps, qs, after, _scatter_copies, 3 * len(ps))

    dh, G1 = layer_bwd(dh, sv1, W1, tabs, lp)
    p1 = rs_partials(shards(G1, big_names), [BF16] * nb, cidx, "b")
    s1_send, s1_recv, p1_t, q1_t, token1 = scatter_start("scatter1_start", p1, dh)
    dh, g3 = ffn_bwd(dh, sv0['sv3'], W0['wg2'], W0['wu2'], W0['wd2'], W0['ln3_g'] + token1[0, 0], lp)
    G0 = dict(wg2=g3['wg'], wu2=g3['wu'], wd2=g3['wd'])
    p0l = rs_partials(shards(G0, last), [BF16] * len(last), cidx, "c")
    sl_send, sl_recv, p0l_t, q0l_t, token0l = scatter_start("scatter0b_start", p0l, dh)
    dh, Gm = layer_bwd(dh, sv0, dict(W0, ln2_g=W0['ln2_g'] + token0l[0, 0]), tabs, lp, ffn1=False, ffn2=False)
    G0.update(Gm, ln3_g=g3['ln_g'], ln3_b=g3['ln_b'])
    p0m = rs_partials(shards(G0, mix), [BF16] * nm, cidx, "d")
    sm_send, sm_recv, p0m_t, q0m_t, token0m = scatter_start("scatter0_start", p0m, dh)
    dh, g1 = ffn_bwd(dh, sv0['sv1'], W0['wg1'], W0['wu1'], W0['wd1'], W0['ln1_g'] + token0m[0, 0], lp)
    G0.update(wg1=g1['wg'], wu1=g1['wu'], wd1=g1['wd'], ln1_g=g1['ln_g'], ln1_b=g1['ln_b'])
    _, q1 = split_wait("scatter1_wait", s1_send, s1_recv, p1_t, q1_t, dh, _scatter_copies)
    _, q0_last = split_wait("scatter0b_wait", sl_send, sl_recv, p0l_t, q0l_t, dh, _scatter_copies)
    _, q0_mix = split_wait("scatter0_wait", sm_send, sm_recv, p0m_t, q0m_t, dh, _scatter_copies)
    q0_rest = list(q0_mix) + list(q0_last)
    full = [reference_grads(G0, ffn=False), reference_grads(G1, ffn=False)]

    s_parts = [jnp.stack([full[li][n] for li in range(DEPTH)]).reshape(-1) for n in _REPL + ['conv_w']]
    s_parts.append(dh[PAD:X0].reshape(-1))
    s_sizes = [int(p.shape[0]) for p in s_parts]
    s_rows = -(-sum(s_sizes) // (16 * LANES)) * 16
    g_small = _pad_to(jnp.concatenate(s_parts), s_rows * LANES).reshape(1, s_rows, LANES)
    g_small = jnp.broadcast_to(g_small, (N_SHARD, s_rows, LANES))
    p_last = rs_partials(shards(G0, first) + [g_small], [BF16] * len(first) + [F32], cidx, "a")
    z_send, z_recv, pz_t, qz_t, token_z = scatter_start("scatter_last_start", p_last, dh)

    def step_weights(names, grad):
        out = {}
        for n in names:
            if n in kept_t:
                res = adamw(own[n], grad[n], jnp.swapaxes(mom[n], 1, 2), jnp.swapaxes(var[n], 1, 2), "adamw_" + n)
                out[n] = [jnp.swapaxes(t, 1, 2) for t in [grad[n]] + list(res)]
            else:
                out[n] = [grad[n]] + list(adamw(w[n], grad[n], mom[n], var[n], "adamw_" + n))
        return out

    q1 = dict(zip(big_names, q1))
    q0 = dict(zip(rest, q0_rest))
    q0[rest[0]] = q0[rest[0]] + token_z[0, 0].astype(BF16)
    red = rs_finish([[q0[n], q1[n]] for n in rest], "a")
    done = step_weights(rest, {n: r[:, :own[n].shape[1]] for n, r in zip(rest, red)})
    all_done = jnp.stack([done[n][3][(0,) * done[n][3].ndim] for n in rest])
    _, q_last = split_wait("scatter_last_wait", z_send, z_recv, pz_t, qz_t, all_done, _scatter_copies)
    red = rs_finish([[q, q1[n]] for n, q in zip(first, q_last)] + [[q_last[-1]]], "b")
    f_small = red[-1].reshape(-1)

    grad = {n: r[:, :own[n].shape[1]] for n, r in zip(first, red)}
    off = 0
    for n, sz in zip(_REPL + ['conv_w', 'meta'], s_sizes):
        grad[n] = f_small[off:off + sz]
        off += sz
    for n in _REPL:
        grad[n] = grad[n].reshape(w[n].shape)
    cw = grad['conv_w'].reshape(DEPTH, 3, MIX)
    grad['conv_w'] = lax.dynamic_slice_in_dim(cw, chip * (MIX // N_SHARD), MIX // N_SHARD, axis=2)
    gm = grad['meta'].reshape(N_META, D_MODEL)
    grad['meta'] = lax.dynamic_slice_in_dim(gm, chip * (D_MODEL // N_SHARD), D_MODEL // N_SHARD, axis=1)

    done.update(step_weights([n for n in _WEIGHTS if n not in done], grad))
    return (loss, dh[X0:][None], *[done[n][k] for k in range(4) for n in _WEIGHTS])


def kernel(x, meta, ffn1_w_gate, ffn1_w_up, ffn1_w_down, ln1_g, ln1_b, w_in, mla_q_norm_g, mla_w_uq, mla_kv_norm_g, mla_w_ukv, mla_w_o, conv_w, conv_b, conv_w_out, s5_a_re, s5_a_im, s5_log_dt, s5_b_re, s5_b_im, s5_c_re, s5_c_im, s5_d, s5_w_glu, s5_b_glu, s5_w_out, w_o, ln2_g, ln2_b, ffn2_w_gate, ffn2_w_up, ffn2_w_down, ln3_g, ln3_b, loss_target, m_meta, m_ffn1_w_gate, m_ffn1_w_up, m_ffn1_w_down, m_ln1_g, m_ln1_b, m_w_in, m_mla_q_norm_g, m_mla_w_uq, m_mla_kv_norm_g, m_mla_w_ukv, m_mla_w_o, m_conv_w, m_conv_b, m_conv_w_out, m_s5_a_re, m_s5_a_im, m_s5_log_dt, m_s5_b_re, m_s5_b_im, m_s5_c_re, m_s5_c_im, m_s5_d, m_s5_w_glu, m_s5_b_glu, m_s5_w_out, m_w_o, m_ln2_g, m_ln2_b, m_ffn2_w_gate, m_ffn2_w_up, m_ffn2_w_down, m_ln3_g, m_ln3_b, v_meta, v_ffn1_w_gate, v_ffn1_w_up, v_ffn1_w_down, v_ln1_g, v_ln1_b, v_w_in, v_mla_q_norm_g, v_mla_w_uq, v_mla_kv_norm_g, v_mla_w_ukv, v_mla_w_o, v_conv_w, v_conv_b, v_conv_w_out, v_s5_a_re, v_s5_a_im, v_s5_log_dt, v_s5_b_re, v_s5_b_im, v_s5_c_re, v_s5_c_im, v_s5_d, v_s5_w_glu, v_s5_b_glu, v_s5_w_out, v_w_o, v_ln2_g, v_ln2_b, v_ffn2_w_gate, v_ffn2_w_up, v_ffn2_w_down, v_ln3_g, v_ln3_b):
    return _step(dict(locals()))
```

```python
import functools
import math

import numpy as np
import jax
import jax.numpy as jnp
from jax import lax
from jax.experimental import pallas as pl
from jax.experimental.pallas import tpu as pltpu

F32 = jnp.float32
BF16 = jnp.bfloat16

D_MODEL = 1024
DEPTH = 2
N_META = 16
PAD = 112
X0 = PAD + N_META
N_HEADS = 8
D_NOPE = 64
D_ROPE = 32
D_V = 64
Q_RANK = 384
KV_RANK = 256
MIX = 512
S5_GROUPS = 32
S5_GROUP = 16
S5_STATE = 64
S5_LANES = S5_GROUPS * S5_STATE
D_FF = 2816
N_SHARD = 4
FF_SHARD = D_FF // N_SHARD
D_IN = 5792
P_IN = 6144
ALPHA = (2.0 * DEPTH) ** 0.25
LN_EPS = 1e-5
RMS_EPS = 1e-6
ATT_SCALE = (D_NOPE + D_ROPE) ** -0.5
ROPE_BASE = 10000.0
ADAM_LR, ADAM_B1, ADAM_B2, ADAM_EPS, ADAM_WD, ADAM_STEP = 0.001, 0.9, 0.999, 1e-08, 0.01, 10
SCAN_CHUNK = 128
VMEM_LIMIT = 52 * 2 ** 20
WGRAD = BF16
MESH = pl.DeviceIdType.MESH


def _cparams(**kw):
    return pltpu.CompilerParams(vmem_limit_bytes=VMEM_LIMIT, **kw)


def _tile(n):
    if n <= 1088:
        return n
    for t in (1024, 544, 512, 272, 256, 128):
        if n % t == 0:
            return t
    return n


def _row_tile(lp):
    for t in (544, 272, 128):
        if lp % t == 0:
            return t
    return lp


def _ffn_tile(lp):
    return 1088 if lp % 1088 == 0 else _row_tile(lp)


def _sigmoid(x):
    return 1.0 / (1.0 + jnp.exp(-x))


_GELU_C = math.sqrt(2.0 / math.pi)


def _gelu(x):
    return 0.5 * x * (1.0 + jnp.tanh(_GELU_C * (x + 0.044715 * x * x * x)))


def _gelu_grad(x):
    t = jnp.tanh(_GELU_C * (x + 0.044715 * x * x * x))
    return 0.5 * (1.0 + t) + 0.5 * x * (1.0 - t * t) * _GELU_C * (1.0 + 3.0 * 0.044715 * x * x)


def _dot(a, b, ca, cb, precision=None):
    return lax.dot_general(a, b, (((ca,), (cb,)), ((), ())), preferred_element_type=F32, precision=precision)


def matmul(a, b, *, name, ta=False, tb=False, ab='n', bb='n', res=None, res_scale=1.0, scale=1.0, out_dtype=F32):
    if ta:
        _, K, M = a.shape
    else:
        _, M, K = a.shape
    if tb:
        _, N, K2 = b.shape
    else:
        _, K2, N = b.shape
    assert K == K2, (a.shape, b.shape)
    n_out = max(a.shape[0] if ab == 'o' else 1, b.shape[0] if bb == 'o' else 1)
    n_red = max(a.shape[0] if ab == 'r' else 1, b.shape[0] if bb == 'r' else 1)
    tm, tn = _tile(M), _tile(N)
    tk = K if K <= 2304 else _tile(K)
    nkt = K // tk
    n_steps = n_red * nkt

    def bsel(mode, o, r):
        if mode == 'o':
            return o
        if mode == 'r':
            return r // nkt if nkt > 1 else r
        return 0

    def ksel(r):
        if nkt == 1:
            return 0
        return r % nkt if n_red > 1 else r

    a_map = (lambda o, i, j, r: (bsel(ab, o, r), ksel(r), i)) if ta else (lambda o, i, j, r: (bsel(ab, o, r), i, ksel(r)))
    b_map = (lambda o, i, j, r: (bsel(bb, o, r), j, ksel(r))) if tb else (lambda o, i, j, r: (bsel(bb, o, r), ksel(r), j))
    o_map = lambda o, i, j, r: (o, i, j)
    in_specs = [pl.BlockSpec((None, tk, tm) if ta else (None, tm, tk), a_map),
                pl.BlockSpec((None, tn, tk) if tb else (None, tk, tn), b_map)]
    operands = [a, b]
    if res is not None:
        in_specs.append(pl.BlockSpec((None, tm, tn), o_map))
        operands.append(res)
    has_res = res is not None

    def body(*refs):
        a_ref, b_ref = refs[0], refs[1]
        res_ref = refs[2] if has_res else None
        o_ref = refs[3] if has_res else refs[2]
        part = _dot(a_ref[...].astype(BF16), b_ref[...].astype(BF16), 0 if ta else 1, 1 if tb else 0)

        def finish(acc):
            v = acc if scale == 1.0 else acc * scale
            if has_res:
                v = v + res_scale * res_ref[...].astype(F32)
            o_ref[...] = v.astype(o_ref.dtype)

        if n_steps == 1:
            finish(part)
        else:
            acc_ref = refs[-1]
            r = pl.program_id(3)

            @pl.when(r == 0)
            def _():
                acc_ref[...] = part

            @pl.when(r > 0)
            def _():
                acc_ref[...] += part

            @pl.when(r == n_steps - 1)
            def _():
                finish(acc_ref[...])

    return pl.pallas_call(
        body, name=name,
        grid=(n_out, M // tm, N // tn, n_steps),
        in_specs=in_specs,
        out_specs=pl.BlockSpec((None, tm, tn), o_map),
        out_shape=jax.ShapeDtypeStruct((n_out, M, N), out_dtype),
        scratch_shapes=[pltpu.VMEM((tm, tn), F32)] if n_steps > 1 else [],
        compiler_params=_cparams(),
    )(*operands)


def rowwise(fn, rows, pars, outs, accs=(), *, name, lp):
    tm = _row_tile(lp)
    n_rows, n_pars, n_outs, n_accs = len(rows), len(pars), len(outs), len(accs)
    outs = [tuple(o) + (o[0], 0) if len(o) == 2 else tuple(o) for o in outs]
    in_specs = [pl.BlockSpec((tm, w), functools.partial(lambda i, cb: (i, cb), cb=cb)) for _, w, cb in rows]
    in_specs += [pl.BlockSpec(p.shape, functools.partial(lambda i, nd: (0,) * nd, nd=p.ndim)) for p in pars]
    out_specs = [pl.BlockSpec((tm, w), functools.partial(lambda i, cb: (i, cb), cb=cb)) for w, _, _, cb in outs]
    out_specs += [pl.BlockSpec(s, functools.partial(lambda i, nd: (0,) * nd, nd=len(s))) for s, _ in accs]
    out_shape = [jax.ShapeDtypeStruct((lp, total), dt) for _, dt, total, _ in outs]
    out_shape += [jax.ShapeDtypeStruct(s, dt) for s, dt in accs]

    def body(*refs):
        i = pl.program_id(0)
        rv = [r[...] for r in refs[:n_rows]]
        pv = [r[...] for r in refs[n_rows:n_rows + n_pars]]
        o_refs = refs[n_rows + n_pars:n_rows + n_pars + n_outs]
        a_refs = refs[n_rows + n_pars + n_outs:]
        ov, av = fn(i * tm, rv, pv)
        for r, v in zip(o_refs, ov):
            r[...] = v.astype(r.dtype)
        if n_accs:
            @pl.when(i == 0)
            def _():
                for r, v in zip(a_refs, av):
                    r[...] = v.astype(r.dtype)

            @pl.when(i > 0)
            def _():
                for r, v in zip(a_refs, av):
                    r[...] += v.astype(r.dtype)

    res = pl.pallas_call(
        body, name=name, grid=(lp // tm,), in_specs=in_specs, out_specs=out_specs, out_shape=out_shape,
        compiler_params=_cparams(),
    )(*[r[0] for r in rows], *pars)
    return res


def _row_mask(row0, shape):
    return (row0 + lax.broadcasted_iota(jnp.int32, shape, 0)) >= PAD


def ffn_up(hb, wg, wu, lp):
    tm = _ffn_tile(lp)

    def body(h_ref, wg_ref, wu_ref, ab_ref, hid_ref):
        h = h_ref[...]
        a = _dot(h, wg_ref[...], 1, 1)
        b = _dot(h, wu_ref[...], 1, 1)
        ab_ref[0] = a.astype(BF16)
        ab_ref[1] = b.astype(BF16)
        hid_ref[...] = (a * _sigmoid(a) * b).astype(BF16)

    wspec = pl.BlockSpec((None, FF_SHARD, D_MODEL), lambda j, i: (j, 0, 0))
    return pl.pallas_call(
        body, name="ffn_up", grid=(N_SHARD, lp // tm),
        in_specs=[pl.BlockSpec((tm, D_MODEL), lambda j, i: (i, 0)), wspec, wspec],
        out_specs=[pl.BlockSpec((None, 2, tm, FF_SHARD), lambda j, i: (j, 0, i, 0)),
                   pl.BlockSpec((None, tm, FF_SHARD), lambda j, i: (j, i, 0))],
        out_shape=[jax.ShapeDtypeStruct((N_SHARD, 2, lp, FF_SHARD), BF16),
                   jax.ShapeDtypeStruct((N_SHARD, lp, FF_SHARD), BF16)],
        compiler_params=_cparams(),
    )(hb, wg, wu)


def _layer_norm(z, g, b):
    mu = jnp.mean(z, axis=-1, keepdims=True)
    zc = z - mu
    var = jnp.mean(zc * zc, axis=-1, keepdims=True)
    return zc * lax.rsqrt(var + LN_EPS) * g + b


def mm_res_ln(a, w, res, g, b, *, scale, name, lp):
    n_red, _, K = a.shape
    tm = _row_tile(lp)

    def body(a_ref, w_ref, res_ref, g_ref, b_ref, z_ref, h_ref, hb_ref, acc_ref):
        r = pl.program_id(1)
        part = _dot(a_ref[...].astype(BF16), w_ref[...], 1, 0)

        @pl.when(r == 0)
        def _():
            acc_ref[...] = part

        @pl.when(r > 0)
        def _():
            acc_ref[...] += part

        @pl.when(r == n_red - 1)
        def _():
            z = ALPHA * res_ref[...] + scale * acc_ref[...]
            z_ref[...] = z
            hn = _layer_norm(z, g_ref[...], b_ref[...])
            h_ref[...] = hn
            hb_ref[...] = hn.astype(BF16)

    row = pl.BlockSpec((tm, D_MODEL), lambda i, r: (i, 0))
    par = pl.BlockSpec((1, D_MODEL), lambda i, r: (0, 0))
    return pl.pallas_call(
        body, name=name, grid=(lp // tm, n_red),
        in_specs=[pl.BlockSpec((None, tm, K), lambda i, r: (r, i, 0)),
                  pl.BlockSpec((None, K, D_MODEL), lambda i, r: (r, 0, 0)), row, par, par],
        out_specs=[row, row, row],
        out_shape=[jax.ShapeDtypeStruct((lp, D_MODEL), F32), jax.ShapeDtypeStruct((lp, D_MODEL), F32),
                   jax.ShapeDtypeStruct((lp, D_MODEL), BF16)],
        scratch_shapes=[pltpu.VMEM((tm, D_MODEL), F32)],
        compiler_params=_cparams(),
    )(a, w, res, g, b)


def ln_bwd(dh, z, g, *, fscale, name, lp):
    def fn(row0, rv, pv):
        dh_, z_ = rv
        g_, = pv
        mu = jnp.mean(z_, axis=-1, keepdims=True)
        zc = z_ - mu
        rstd = lax.rsqrt(jnp.mean(zc * zc, axis=-1, keepdims=True) + LN_EPS)
        xh = zc * rstd
        dxh = dh_ * g_
        m1 = jnp.mean(dxh, axis=-1, keepdims=True)
        m2 = jnp.mean(dxh * xh, axis=-1, keepdims=True)
        dz = rstd * (dxh - m1 - xh * m2)
        return ((dz, fscale * dz),
                (jnp.sum(dh_ * xh, axis=0, keepdims=True), jnp.sum(dh_, axis=0, keepdims=True)))

    return rowwise(fn, [(dh, D_MODEL, 0), (z, D_MODEL, 0)], [g], [(D_MODEL, F32), (D_MODEL, BF16)],
                   [((1, D_MODEL), F32), ((1, D_MODEL), F32)], name=name, lp=lp)


def ffn_down_bwd(dfb, wd, ab, lp):
    tm = _ffn_tile(lp)

    def body(df_ref, w_ref, ab_ref, da_ref, db_ref):
        dhid = _dot(df_ref[...], w_ref[...], 1, 1)
        a = ab_ref[0].astype(F32)
        b = ab_ref[1].astype(F32)
        sg = _sigmoid(a)
        da_ref[...] = (dhid * b * (sg * (1.0 + a * (1.0 - sg)))).astype(BF16)
        db_ref[...] = (dhid * (a * sg)).astype(BF16)

    ospec = pl.BlockSpec((None, tm, FF_SHARD), lambda j, i: (j, i, 0))
    return pl.pallas_call(
        body, name="ffn_down_bwd", grid=(N_SHARD, lp // tm),
        in_specs=[pl.BlockSpec((tm, D_MODEL), lambda j, i: (i, 0)),
                  pl.BlockSpec((None, FF_SHARD, D_MODEL), lambda j, i: (j, 0, 0)),
                  pl.BlockSpec((None, 2, tm, FF_SHARD), lambda j, i: (j, 0, i, 0))],
        out_specs=[ospec, ospec],
        out_shape=[jax.ShapeDtypeStruct((N_SHARD, lp, FF_SHARD), BF16)] * 2,
        compiler_params=_cparams(),
    )(dfb, wd, ab)


def ffn_dx(da, db, wg, wu, dz, lp):
    tm = _ffn_tile(lp)

    def body(da_ref, db_ref, wg_ref, wu_ref, dz_ref, o_ref, acc_ref):
        j = pl.program_id(1)
        part = _dot(da_ref[...], wg_ref[...], 1, 0) + _dot(db_ref[...], wu_ref[...], 1, 0)

        @pl.when(j == 0)
        def _():
            acc_ref[...] = part

        @pl.when(j > 0)
        def _():
            acc_ref[...] += part

        @pl.when(j == N_SHARD - 1)
        def _():
            o_ref[...] = acc_ref[...] + ALPHA * dz_ref[...]

    aspec = pl.BlockSpec((None, tm, FF_SHARD), lambda i, j: (j, i, 0))
    wspec = pl.BlockSpec((None, FF_SHARD, D_MODEL), lambda i, j: (j, 0, 0))
    row = pl.BlockSpec((tm, D_MODEL), lambda i, j: (i, 0))
    return pl.pallas_call(
        body, name="ffn_dx", grid=(lp // tm, N_SHARD), in_specs=[aspec, aspec, wspec, wspec, row], out_specs=row,
        out_shape=jax.ShapeDtypeStruct((lp, D_MODEL), F32), scratch_shapes=[pltpu.VMEM((tm, D_MODEL), F32)],
        compiler_params=_cparams(),
    )(da, db, wg, wu, dz)


def ffn_fwd(h, hb, wg, wu, wd, g, b, lp):
    ab, hid = ffn_up(hb, wg, wu, lp)
    z, hn, hnb = mm_res_ln(hid, wd, h, g, b, scale=0.5, name="ffn_down_ln", lp=lp)
    return hn, hnb, dict(hb=hb, ab=ab, hid=hid, z=z)


def ffn_bwd(dh, sv, wg, wu, wd, g, lp):
    dz, dfb, dg, db = ln_bwd(dh, sv['z'], g, fscale=0.5, name="ffn_ln_bwd", lp=lp)
    da, dbb = ffn_down_bwd(dfb, wd, sv['ab'], lp)
    d_wd = matmul(sv['hid'], dfb[None], ta=True, ab='o', out_dtype=WGRAD, name="ffn_dwd")
    d_wg = matmul(da, sv['hb'][None], ta=True, ab='o', out_dtype=WGRAD, name="ffn_dwg")
    d_wu = matmul(dbb, sv['hb'][None], ta=True, ab='o', out_dtype=WGRAD, name="ffn_dwu")
    dh_in = ffn_dx(da, dbb, wg, wu, dz, lp)
    return dh_in, dict(wg=d_wg, wu=d_wu, wd=d_wd, ln_g=dg, ln_b=db)


def _rope_tables(lp):
    pos = np.arange(lp, dtype=np.float32) - PAD
    inv = ROPE_BASE ** (-np.arange(0, D_ROPE, 2, dtype=np.float32) / D_ROPE)
    ang = pos[:, None] * inv[None, :]
    cos = np.concatenate([np.cos(ang), np.cos(ang)], axis=1).astype(np.float32)
    sin = np.concatenate([np.sin(ang), np.sin(ang)], axis=1).astype(np.float32)
    rot = np.zeros((D_ROPE, D_ROPE), np.float32)
    half = D_ROPE // 2
    for j in range(half):
        rot[j + half, j] = -1.0
        rot[j, j + half] = 1.0
    d_qk = D_NOPE + D_ROPE
    cos_qk = np.concatenate([np.ones((lp, D_NOPE), np.float32), cos], axis=1)
    sin_qk = np.concatenate([np.zeros((lp, D_NOPE), np.float32), sin], axis=1)
    rot_qk = np.zeros((d_qk, d_qk), np.float32)
    rot_qk[D_NOPE:, D_NOPE:] = rot
    place = np.zeros((D_ROPE, d_qk), np.float32)
    place[:, D_NOPE:] = np.eye(D_ROPE, dtype=np.float32)
    return tuple(jnp.asarray(t) for t in (cos, sin, rot, cos_qk, sin_qk, rot_qk, place))


def _rot(x, rot):
    return _dot(x, rot, 1, 0, precision=lax.Precision.HIGHEST)


def _rms(x, g):
    r = lax.rsqrt(jnp.mean(x * x, axis=-1, keepdims=True) + RMS_EPS)
    return x * r * g


def mla_prep(proj, cos, sin, rot, qg, kvg, lp):
    def fn(row0, rv, pv):
        cq, krb, ckv, c, s = rv
        qg_, kvg_, rot_ = pv
        kr = krb[:, :D_ROPE]
        return ((_rms(cq, qg_), _rms(ckv, kvg_), kr * c + _rot(kr, rot_) * s), ())

    return rowwise(fn, [(proj, Q_RANK, 0), (proj, 128, 3), (proj, KV_RANK, 2), (cos, D_ROPE, 0), (sin, D_ROPE, 0)],
                   [qg, kvg, rot], [(Q_RANK, BF16), (KV_RANK, BF16), (D_ROPE, BF16)], name="mla_prep", lp=lp)


def mla_heads(cqn, ckvn, kr, cos_qk, sin_qk, rot_qk, place, wq, wk, wv, lp):
    tm = _row_tile(lp)

    def body(cq_ref, ckv_ref, kr_ref, c_ref, s_ref, rot_ref, place_ref, wq_ref, wk_ref, wv_ref, q_ref, k_ref, v_ref):
        cq = cq_ref[...]
        ckv = ckv_ref[...]
        kr_placed = _dot(kr_ref[...], place_ref[...].astype(BF16), 1, 0)
        for h in range(N_HEADS):
            q = _dot(cq, wq_ref[h], 1, 0)
            q_ref[h] = (q * c_ref[...] + _rot(q, rot_ref[...]) * s_ref[...]).astype(BF16)
            k_ref[h] = (_dot(ckv, wk_ref[h], 1, 0) + kr_placed).astype(BF16)
            v_ref[h] = _dot(ckv, wv_ref[h], 1, 0).astype(BF16)

    def row(w):
        return pl.BlockSpec((tm, w), lambda i: (i, 0))

    def whole(a):
        return pl.BlockSpec(a.shape, functools.partial(lambda i, nd: (0,) * nd, nd=a.ndim))

    def ospec(n):
        return pl.BlockSpec((N_HEADS, tm, n), lambda i: (0, i, 0))

    return pl.pallas_call(
        body, name="mla_heads", grid=(lp // tm,),
        in_specs=[row(Q_RANK), row(KV_RANK), row(D_ROPE), row(D_QK), row(D_QK), whole(rot_qk), whole(place),
                  whole(wq), whole(wk), whole(wv)],
        out_specs=[ospec(D_QK), ospec(D_QK), ospec(D_V)],
        out_shape=[jax.ShapeDtypeStruct((N_HEADS, lp, D_QK), BF16), jax.ShapeDtypeStruct((N_HEADS, lp, D_QK), BF16),
                   jax.ShapeDtypeStruct((N_HEADS, lp, D_V), BF16)],
        compiler_params=_cparams(),
    )(cqn, ckvn, kr, cos_qk, sin_qk, rot_qk, place, wq, wk, wv)


D_QK = D_NOPE + D_ROPE


def _att_probs(q, k, row0, tq, lp):
    s = _dot(q, k, 1, 1) * ATT_SCALE
    qi = row0 + lax.broadcasted_iota(jnp.int32, (tq, lp), 0)
    ki = lax.broadcasted_iota(jnp.int32, (tq, lp), 1)
    s = jnp.where((ki <= qi) & (ki >= PAD), s, -1e30)
    p = jnp.exp(s - jnp.max(s, axis=-1, keepdims=True))
    return p / jnp.sum(p, axis=-1, keepdims=True)


def _att_spec(lp, n):
    return pl.BlockSpec((None, lp, n), lambda h: (h, 0, 0))


def _att_tiles(lp):
    tiles, r = [(0, X0)], X0
    while r < lp:
        tiles.append((r, 256))
        r += 256
    assert r == lp
    return tiles


def attn_fwd(q, k, v, lp):
    def body(q_ref, k_ref, v_ref, o_ref):
        for r0, rows in _att_tiles(lp):
            ke, rq = r0 + rows, slice(r0, r0 + rows)
            p = _att_probs(q_ref[rq, :], k_ref[0:ke, :], r0, rows, ke)
            o_ref[rq, :] = _dot(p.astype(BF16), v_ref[0:ke, :], 1, 0).astype(BF16)

    return pl.pallas_call(
        body, name="attn_fwd", grid=(N_HEADS,),
        in_specs=[_att_spec(lp, D_QK), _att_spec(lp, D_QK), _att_spec(lp, D_V)],
        out_specs=_att_spec(lp, D_V), out_shape=jax.ShapeDtypeStruct((N_HEADS, lp, D_V), BF16),
        compiler_params=_cparams(),
    )(q, k, v)


def attn_bwd(q, k, v, do, lp):
    def body(q_ref, k_ref, v_ref, do_ref, dq_ref, dk_ref, dv_ref):
        dk_ref[...] = jnp.zeros_like(dk_ref)
        dv_ref[...] = jnp.zeros_like(dv_ref)
        for r0, rows in _att_tiles(lp):
            ke, rq = r0 + rows, slice(r0, r0 + rows)
            q_, do_, k_, v_ = q_ref[rq, :], do_ref[rq, :], k_ref[0:ke, :], v_ref[0:ke, :]
            p = _att_probs(q_, k_, r0, rows, ke)
            dp = _dot(do_, v_, 1, 1)
            delta = jnp.sum(p * dp, axis=-1, keepdims=True)
            ds = (p * (dp - delta) * ATT_SCALE).astype(BF16)
            dq_ref[rq, :] = _dot(ds, k_, 1, 0)
            dk_ref[0:ke, :] += _dot(ds, q_, 0, 0)
            dv_ref[0:ke, :] += _dot(p.astype(BF16), do_, 0, 0)

    qk, vv = _att_spec(lp, D_QK), _att_spec(lp, D_V)
    return pl.pallas_call(
        body, name="attn_bwd", grid=(N_HEADS,), in_specs=[qk, qk, vv, vv], out_specs=[qk, qk, vv],
        out_shape=[jax.ShapeDtypeStruct((N_HEADS, lp, D_QK), F32), jax.ShapeDtypeStruct((N_HEADS, lp, D_QK), F32),
                   jax.ShapeDtypeStruct((N_HEADS, lp, D_V), F32)],
        compiler_params=_cparams(),
    )(q, k, v, do)


def mla_heads_bwd(dq, dk, dv, cos_qk, sin_qk, rot_qk, place, wq, wk, wv, lp):
    tm = _row_tile(lp)

    def body(dq_ref, dk_ref, dv_ref, c_ref, s_ref, rot_ref, place_ref, wq_ref, wk_ref, wv_ref,
             dcq_ref, dckv_ref, dqp_ref, dkr_ref):
        dcq = jnp.zeros(dcq_ref.shape, F32)
        dckv = jnp.zeros(dckv_ref.shape, F32)
        dkr = jnp.zeros(dkr_ref.shape, F32)
        for h in range(N_HEADS):
            g = dq_ref[h]
            dqp = (g * c_ref[...] - _rot(g * s_ref[...], rot_ref[...])).astype(BF16)
            dqp_ref[h] = dqp
            dcq = dcq + _dot(dqp, wq_ref[h], 1, 1)
            dk_ = dk_ref[h]
            dckv = dckv + _dot(dk_.astype(BF16), wk_ref[h], 1, 1) + _dot(dv_ref[h].astype(BF16), wv_ref[h], 1, 1)
            dkr = dkr + _dot(dk_, place_ref[...], 1, 1, precision=lax.Precision.HIGHEST)
        dcq_ref[...] = dcq
        dckv_ref[...] = dckv
        dkr_ref[...] = dkr

    def hspec(n):
        return pl.BlockSpec((N_HEADS, tm, n), lambda i: (0, i, 0))

    def row(w):
        return pl.BlockSpec((tm, w), lambda i: (i, 0))

    def whole(a):
        return pl.BlockSpec(a.shape, functools.partial(lambda i, nd: (0,) * nd, nd=a.ndim))

    return pl.pallas_call(
        body, name="mla_heads_bwd", grid=(lp // tm,),
        in_specs=[hspec(D_QK), hspec(D_QK), hspec(D_V), row(D_QK), row(D_QK), whole(rot_qk), whole(place),
                  whole(wq), whole(wk), whole(wv)],
        out_specs=[row(Q_RANK), row(KV_RANK), hspec(D_QK), row(D_ROPE)],
        out_shape=[jax.ShapeDtypeStruct((lp, Q_RANK), F32), jax.ShapeDtypeStruct((lp, KV_RANK), F32),
                   jax.ShapeDtypeStruct((N_HEADS, lp, D_QK), BF16), jax.ShapeDtypeStruct((lp, D_ROPE), F32)],
        compiler_params=_cparams(),
    )(dq, dk, dv, cos_qk, sin_qk, rot_qk, place, wq, wk, wv)


def _rms_bwd(dy, x, g):
    r = lax.rsqrt(jnp.mean(x * x, axis=-1, keepdims=True) + RMS_EPS)
    n = x * r
    dn = dy * g
    dx = r * (dn - n * jnp.mean(dn * n, axis=-1, keepdims=True))
    return dx, jnp.sum(dy * n, axis=0, keepdims=True)


def mla_prep_bwd(dcq, dckv, dkr, proj, cos, sin, rot, qg, kvg, lp):
    def fn(row0, rv, pv):
        dcq_, dckv_, dkr_, cq, ckv, c, s = rv
        qg_, kvg_, rot_ = pv
        dxq, dgq = _rms_bwd(dcq_, cq, qg_)
        dxkv, dgkv = _rms_bwd(dckv_, ckv, kvg_)
        dkr_raw = dkr_ * c - _rot(dkr_ * s, rot_)
        return ((dxq, dxkv, dkr_raw), (dgq, dgkv))

    return rowwise(fn, [(dcq, Q_RANK, 0), (dckv, KV_RANK, 0), (dkr, D_ROPE, 0), (proj, Q_RANK, 0), (proj, KV_RANK, 2),
                        (cos, D_ROPE, 0), (sin, D_ROPE, 0)], [qg, kvg, rot],
                   [(Q_RANK, BF16), (KV_RANK, BF16), (D_ROPE, BF16)], [((1, Q_RANK), F32), ((1, KV_RANK), F32)],
                   name="mla_prep_bwd", lp=lp)


def _shift_down(x, d, rows):
    return jnp.where(rows >= d, pltpu.roll(x, d, 0), 0.0)


def _shift_up(x, d, rows, n):
    return jnp.where(rows < n - d, pltpu.roll(x, n - d, 0), 0.0)


_CONV_W = 128
_XB, _BG, _CG = 1024 // _CONV_W, 1536 // _CONV_W, 2048 // _CONV_W


def _conv_specs(lp):
    def pspec(base):
        return pl.BlockSpec((lp, _CONV_W), functools.partial(lambda c, base: (0, base + c), base=base))

    col = pl.BlockSpec((lp, _CONV_W), lambda c: (0, c))
    wspec = pl.BlockSpec((3, _CONV_W), lambda c: (0, c))
    bspec = pl.BlockSpec((1, _CONV_W), lambda c: (0, c))
    return pspec, col, wspec, bspec


def _conv_core(xbar, cg, w, bias, lp):
    rows = lax.broadcasted_iota(jnp.int32, (lp, _CONV_W), 0)
    u = jnp.where(rows >= PAD, cg * xbar, 0.0)
    u1 = _shift_down(u, 1, rows)
    u2 = _shift_down(u, 2, rows)
    y = bias + w[0:1] * u2 + w[1:2] * u1 + w[2:3] * u
    return rows, u, u1, u2, y


def conv_fwd(proj, w, bias, lp):
    pspec, col, wspec, bspec = _conv_specs(lp)

    def body(x_ref, b_ref, c_ref, w_ref, bias_ref, v_ref):
        _, _, _, _, y = _conv_core(x_ref[...], c_ref[...], w_ref[...], bias_ref[...], lp)
        v_ref[...] = (b_ref[...] * y).astype(BF16)

    return pl.pallas_call(
        body, name="conv_fwd", grid=(MIX // _CONV_W,),
        in_specs=[pspec(_XB), pspec(_BG), pspec(_CG), wspec, bspec], out_specs=col,
        out_shape=jax.ShapeDtypeStruct((lp, MIX), BF16), compiler_params=_cparams(),
    )(proj, proj, proj, w, bias)


def conv_bwd(dv, proj, w, bias, lp):
    pspec, col, wspec, bspec = _conv_specs(lp)

    def body(dv_ref, x_ref, b_ref, c_ref, w_ref, bias_ref, dx_ref, db_ref, dc_ref, dw_ref, dbias_ref):
        xbar, cg, w_ = x_ref[...], c_ref[...], w_ref[...]
        rows, u, u1, u2, y = _conv_core(xbar, cg, w_, bias_ref[...], lp)
        dv_ = dv_ref[...]
        db_ref[...] = (dv_ * y).astype(BF16)
        dy = dv_ * b_ref[...]
        dbias_ref[...] = jnp.sum(dy, axis=0, keepdims=True)
        dw_ref[0:1, :] = jnp.sum(dy * u2, axis=0, keepdims=True)
        dw_ref[1:2, :] = jnp.sum(dy * u1, axis=0, keepdims=True)
        dw_ref[2:3, :] = jnp.sum(dy * u, axis=0, keepdims=True)
        du = w_[2:3] * dy + w_[1:2] * _shift_up(dy, 1, rows, lp) + w_[0:1] * _shift_up(dy, 2, rows, lp)
        du = jnp.where(rows >= PAD, du, 0.0)
        dc_ref[...] = (du * xbar).astype(BF16)
        dx_ref[...] = (du * cg).astype(BF16)

    return pl.pallas_call(
        body, name="conv_bwd", grid=(MIX // _CONV_W,),
        in_specs=[col, pspec(_XB), pspec(_BG), pspec(_CG), wspec, bspec],
        out_specs=[col, col, col, wspec, bspec],
        out_shape=[jax.ShapeDtypeStruct((lp, MIX), BF16)] * 3 + [jax.ShapeDtypeStruct((3, MIX), F32),
                                                                jax.ShapeDtypeStruct((1, MIX), F32)],
        compiler_params=_cparams(),
    )(dv, proj, proj, proj, w, bias)


def _s5_disc(a_re, a_im, log_dt, b_re, b_im):
    dt = jnp.exp(log_dt)
    mag = jnp.exp(dt * a_re)
    ab_re, ab_im = mag * jnp.cos(dt * a_im), mag * jnp.sin(dt * a_im)
    den = a_re * a_re + a_im * a_im
    nr, ni = ab_re - 1.0, ab_im
    coef_re = (nr * a_re + ni * a_im) / den
    coef_im = (ni * a_re - nr * a_im) / den
    return ab_re, ab_im, coef_re * b_re - coef_im * b_im, coef_re * b_im + coef_im * b_re


_S5_ROWS = S5_GROUPS * S5_GROUP


def s5_prep(a_re, a_im, log_dt, b_re, b_im):
    def body(ar, ai, ld, br, bi, o0, o1, o2, o3):
        for o, v in zip((o0, o1, o2, o3), _s5_disc(ar[...], ai[...], ld[...], br[...], bi[...])):
            o[...] = v

    return pl.pallas_call(body, name="s5_prep",
                          out_shape=[jax.ShapeDtypeStruct((_S5_ROWS, S5_STATE), F32)] * 4)(a_re, a_im, log_dt, b_re, b_im)


def s5_prep_bwd(a_re, a_im, log_dt, b_re, b_im, d_ab_re, d_ab_im, d_bb_re, d_bb_im, sel):
    def body(ar, ai, ld, br, bi, g0, g1, g2, g3, sel_ref, da_re, da_im, dld, dbr, dbi):
        _, vjp = jax.vjp(_s5_disc, ar[...], ai[...], ld[...], br[...], bi[...])
        c_ar, c_ai, c_ld, c_br, c_bi = vjp((g0[...], g1[...], g2[...], g3[...]))
        s = sel_ref[...]
        hi = lax.Precision.HIGHEST
        da_re[...] = _dot(s, c_ar, 1, 0, precision=hi)
        da_im[...] = _dot(s, c_ai, 1, 0, precision=hi)
        dld[...] = jnp.sum(_dot(s, c_ld, 1, 0, precision=hi), axis=-1, keepdims=True)
        dbr[...] = c_br
        dbi[...] = c_bi

    g = jax.ShapeDtypeStruct((S5_GROUPS, S5_STATE), F32)
    full = jax.ShapeDtypeStruct((_S5_ROWS, S5_STATE), F32)
    return pl.pallas_call(body, name="s5_prep_bwd",
                          out_shape=[g, g, jax.ShapeDtypeStruct((S5_GROUPS, 1), F32), full, full],
                          )(a_re, a_im, log_dt, b_re, b_im, d_ab_re, d_ab_im, d_bb_re, d_bb_im, sel)


_SCAN_W = 128
_SCAN_STEPS = int(math.log2(SCAN_CHUNK))


def _cmul(ar, ai, br, bi):
    return ar * br - ai * bi, ar * bi + ai * br


def _scan_powers(ar, ai, reverse):
    pw = [(ar, ai)]
    for _ in range(_SCAN_STEPS):
        pw.append(_cmul(*pw[-1], *pw[-1]))
    rows = lax.broadcasted_iota(jnp.int32, (SCAN_CHUNK, ar.shape[-1]), 0)
    tr = jnp.broadcast_to(ar, rows.shape)
    ti = jnp.broadcast_to(ai, rows.shape)
    for k in range(_SCAN_STEPS):
        d = 2 ** k
        if reverse:
            live = rows < SCAN_CHUNK - d
            mr, mi = _cmul(tr, ti, _shift_up(tr, d, rows, SCAN_CHUNK), _shift_up(ti, d, rows, SCAN_CHUNK))
        else:
            live = rows >= d
            mr, mi = _cmul(tr, ti, _shift_down(tr, d, rows), _shift_down(ti, d, rows))
        tr = jnp.where(live, mr, tr)
        ti = jnp.where(live, mi, ti)
    return pw, rows, tr, ti


def s5_scan(bu, ab_re, ab_im, lp):
    n_chunks = lp // SCAN_CHUNK

    def body(bu_ref, ar_ref, ai_ref, s_ref):
        ar, ai = ar_ref[...], ai_ref[...]
        pw, rows, tr, ti = _scan_powers(ar, ai, False)

        def chunk(ci, carry):
            cr, cim = carry
            r0 = pl.multiple_of(ci * SCAN_CHUNK, SCAN_CHUNK)
            xr = bu_ref[0, pl.ds(r0, SCAN_CHUNK), :]
            xi = bu_ref[1, pl.ds(r0, SCAN_CHUNK), :]
            for k in range(_SCAN_STEPS):
                d = 2 ** k
                mr, mi = _cmul(pw[k][0], pw[k][1], _shift_down(xr, d, rows), _shift_down(xi, d, rows))
                xr, xi = xr + mr, xi + mi
            mr, mi = _cmul(tr, ti, cr, cim)
            xr, xi = xr + mr, xi + mi
            s_ref[0, pl.ds(r0, SCAN_CHUNK), :] = xr
            s_ref[1, pl.ds(r0, SCAN_CHUNK), :] = xi
            return xr[SCAN_CHUNK - 1:SCAN_CHUNK, :], xi[SCAN_CHUNK - 1:SCAN_CHUNK, :]

        zero = jnp.zeros((1, _SCAN_W), F32)
        lax.fori_loop(0, n_chunks, chunk, (zero, zero))

    spec = pl.BlockSpec((2, lp, _SCAN_W), lambda c: (0, 0, c))
    aspec = pl.BlockSpec((1, _SCAN_W), lambda c: (0, c))
    return pl.pallas_call(
        body, name="s5_scan", grid=(S5_LANES // _SCAN_W,), in_specs=[spec, aspec, aspec], out_specs=spec,
        out_shape=jax.ShapeDtypeStruct((2, lp, S5_LANES), F32), compiler_params=_cparams(),
    )(bu, ab_re, ab_im)


def s5_scan_bwd(ds, s, ab_re, ab_im, lp):
    n_chunks = lp // SCAN_CHUNK

    def body(ds_ref, s_ref, ar_ref, ai_ref, g_ref, da_ref):
        ar, ai = ar_ref[...], -ai_ref[...]
        pw, rows, tr, ti = _scan_powers(ar, ai, True)

        def chunk(k, carry):
            cr, cim, dar, dai = carry
            ci = n_chunks - 1 - k
            r0 = pl.multiple_of(ci * SCAN_CHUNK, SCAN_CHUNK)
            xr = ds_ref[0, pl.ds(r0, SCAN_CHUNK), :]
            xi = ds_ref[1, pl.ds(r0, SCAN_CHUNK), :]
            for j in range(_SCAN_STEPS):
                d = 2 ** j
                mr, mi = _cmul(pw[j][0], pw[j][1], _shift_up(xr, d, rows, SCAN_CHUNK), _shift_up(xi, d, rows, SCAN_CHUNK))
                xr, xi = xr + mr, xi + mi
            mr, mi = _cmul(tr, ti, cr, cim)
            xr, xi = xr + mr, xi + mi
            g_ref[0, pl.ds(r0, SCAN_CHUNK), :] = xr
            g_ref[1, pl.ds(r0, SCAN_CHUNK), :] = xi
            prev0 = pl.multiple_of(jnp.maximum(r0 - 8, 0), 8)
            live = (ci > 0).astype(F32)
            pr = s_ref[0, pl.ds(prev0, 8), :][7:8, :] * live
            pim = s_ref[1, pl.ds(prev0, 8), :][7:8, :] * live
            sr = s_ref[0, pl.ds(r0, SCAN_CHUNK), :]
            si = s_ref[1, pl.ds(r0, SCAN_CHUNK), :]
            sr = jnp.where(rows >= 1, pltpu.roll(sr, 1, 0), pr)
            si = jnp.where(rows >= 1, pltpu.roll(si, 1, 0), pim)
            dar = dar + jnp.sum(xr * sr + xi * si, axis=0, keepdims=True)
            dai = dai + jnp.sum(xi * sr - xr * si, axis=0, keepdims=True)
            return xr[0:1, :], xi[0:1, :], dar, dai

        zero = jnp.zeros((1, _SCAN_W), F32)
        _, _, dar, dai = lax.fori_loop(0, n_chunks, chunk, (zero, zero, zero, zero))
        da_ref[0] = dar
        da_ref[1] = dai

    spec = pl.BlockSpec((2, lp, _SCAN_W), lambda c: (0, 0, c))
    aspec = pl.BlockSpec((1, _SCAN_W), lambda c: (0, c))
    return pl.pallas_call(
        body, name="s5_scan_bwd", grid=(S5_LANES // _SCAN_W,), in_specs=[spec, spec, aspec, aspec],
        out_specs=[spec, pl.BlockSpec((2, 1, _SCAN_W), lambda c: (0, 0, c))],
        out_shape=[jax.ShapeDtypeStruct((2, lp, S5_LANES), F32), jax.ShapeDtypeStruct((2, 1, S5_LANES), F32)],
        compiler_params=_cparams(),
    )(ds, s, ab_re, ab_im)


S5_BLOCKS = 4
_S5_PER = S5_GROUPS // S5_BLOCKS


def _blockdiag(x):
    _, r, c = x.shape
    eye = jnp.eye(_S5_PER, dtype=x.dtype)
    x = x.reshape(S5_BLOCKS, _S5_PER, r, c)
    return (x[:, :, :, None, :] * eye[None, :, None, :, None]).reshape(S5_BLOCKS, _S5_PER * r, _S5_PER * c)


def _blockdiag_extract(m, r, c):
    return jnp.einsum('qgrgc->qgrc', m.reshape(S5_BLOCKS, _S5_PER, r, _S5_PER, c)).reshape(S5_GROUPS, r, c)


def bd_matmul(a, w, *, w_t, reduce, res=None, name):
    _, M, _ = a.shape
    n_w, _, k1, k2 = w.shape
    ka, kout = (k2, k1) if w_t else (k1, k2)
    tm = _row_tile(M)
    n_out, n_red = (1, n_w) if reduce else (n_w, 1)
    has_res = res is not None

    assert n_red <= 2

    def body(*refs):
        a_ref, w_ref = refs[0], refs[1]
        o_ref = refs[3] if has_res else refs[2]
        for q in range(S5_BLOCKS):
            cols = slice(q * kout, (q + 1) * kout)
            part = _dot(a_ref[:, q * ka:(q + 1) * ka].astype(BF16), w_ref[q], 1, 1 if w_t else 0)
            if n_red == 1:
                o_ref[:, cols] = part
            else:
                acc_ref = refs[-1]

                @pl.when(pl.program_id(2) == 0)
                def _():
                    acc_ref[:, cols] = part

                @pl.when(pl.program_id(2) == 1)
                def _():
                    tot = acc_ref[:, cols] + part
                    o_ref[:, cols] = tot + refs[2][:, cols] if has_res else tot

    if reduce:
        a_map, w_map = (lambda o, i, r: (r, i, 0)), (lambda o, i, r: (r, 0, 0, 0))
    else:
        a_map, w_map = (lambda o, i, r: (0, i, 0)), (lambda o, i, r: (o, 0, 0, 0))
    o_map = lambda o, i, r: (o, i, 0)
    in_specs = [pl.BlockSpec((None, tm, S5_BLOCKS * ka), a_map), pl.BlockSpec((None, S5_BLOCKS, k1, k2), w_map)]
    operands = [a, w]
    if has_res:
        in_specs.append(pl.BlockSpec((None, tm, S5_BLOCKS * kout), o_map))
        operands.append(res)
    return pl.pallas_call(
        body, name=name, grid=(n_out, M // tm, n_red), in_specs=in_specs,
        out_specs=pl.BlockSpec((None, tm, S5_BLOCKS * kout), o_map),
        out_shape=jax.ShapeDtypeStruct((n_out, M, S5_BLOCKS * kout), F32),
        scratch_shapes=[pltpu.VMEM((tm, S5_BLOCKS * kout), F32)] if n_red > 1 else [],
        compiler_params=_cparams(),
    )(*operands)


def bd_outer(a, b, name):
    na, M, wa = a.shape
    nb_, _, wb = b.shape
    ka, kb = wa // S5_BLOCKS, wb // S5_BLOCKS
    n_out = max(na, nb_)

    def body(a_ref, b_ref, o_ref):
        o_ref[...] = _dot(a_ref[...].astype(BF16), b_ref[...].astype(BF16), 0, 0)

    return pl.pallas_call(
        body, name=name, grid=(n_out, S5_BLOCKS),
        in_specs=[pl.BlockSpec((None, M, ka), (lambda o, q: (o, 0, q)) if na > 1 else (lambda o, q: (0, 0, q))),
                  pl.BlockSpec((None, M, kb), (lambda o, q: (o, 0, q)) if nb_ > 1 else (lambda o, q: (0, 0, q)))],
        out_specs=pl.BlockSpec((None, None, ka, kb), lambda o, q: (o, q, 0, 0)),
        out_shape=jax.ShapeDtypeStruct((n_out, S5_BLOCKS, ka, kb), F32), compiler_params=_cparams(),
    )(a, b)


def s5_u(proj, lp):
    def fn(row0, rv, pv):
        u, = rv
        return ((jnp.where(_row_mask(row0, u.shape), u, 0.0),), ())

    return rowwise(fn, [(proj, MIX, 5)], [], [(MIX, BF16)], name="s5_u", lp=lp)[0]


def s5_y(ys, proj, d, lp):
    def fn(row0, rv, pv):
        ys_, u = rv
        y = ys_ + pv[0] * u
        return ((y, _gelu(y)), ())

    return rowwise(fn, [(ys, MIX, 0), (proj, MIX, 5)], [d], [(MIX, F32), (MIX, BF16)], name="s5_y", lp=lp)


def s5_glu(z, y, b, lp):
    def fn(row0, rv, pv):
        z_, y_ = rv
        return ((_gelu(y_) * _sigmoid(z_ + pv[0]),), ())

    return rowwise(fn, [(z, MIX, 0), (y, MIX, 0)], [b], [(MIX, BF16)], name="s5_glu", lp=lp)[0]


def s5_glu_bwd(dgl, z, y, b, lp):
    def fn(row0, rv, pv):
        dgl_, z_, y_ = rv
        sg = _sigmoid(z_ + pv[0])
        dz = dgl_ * _gelu(y_) * sg * (1.0 - sg)
        return ((dgl_ * sg, dz), (jnp.sum(dz, axis=0, keepdims=True),))

    return rowwise(fn, [(dgl, MIX, 0), (z, MIX, 0), (y, MIX, 0)], [b], [(MIX, F32), (MIX, BF16)], [((1, MIX), F32)],
                   name="s5_glu_bwd", lp=lp)


def s5_y_bwd(dyg, y, proj, d, lp):
    def fn(row0, rv, pv):
        dyg_, y_, u = rv
        dy = dyg_ * _gelu_grad(y_)
        return ((dy, dy * pv[0]), (jnp.sum(dy * u, axis=0, keepdims=True),))

    return rowwise(fn, [(dyg, MIX, 0), (y, MIX, 0), (proj, MIX, 5)], [d], [(MIX, BF16), (MIX, F32)], [((1, MIX), F32)],
                   name="s5_y_bwd", lp=lp)


def s5_du(du, lp):
    def fn(row0, rv, pv):
        return ((jnp.where(_row_mask(row0, rv[0].shape), rv[0], 0.0),), ())

    return rowwise(fn, [(du, MIX, 0)], [], [(MIX, BF16)], name="s5_du", lp=lp)[0]


def merge_fwd(proj, ya, yb, yc, lp):
    def fn(row0, rv, pv):
        g0, g1, g2, a, b, c = rv
        return ((_sigmoid(g0) * a + _sigmoid(g1) * b + _sigmoid(g2) * c,), ())

    return rowwise(fn, [(proj, D_MODEL, 3), (proj, D_MODEL, 4), (proj, D_MODEL, 5), (ya, D_MODEL, 0), (yb, D_MODEL, 0),
                        (yc, D_MODEL, 0)], [], [(D_MODEL, BF16)], name="merge_fwd", lp=lp)[0]


def merge_bwd(dmix, proj, ya, yb, yc, lp):
    def fn(row0, rv, pv):
        dm, g0, g1, g2, a, b, c = rv
        outs_y, outs_g = [], []
        for g, yv in ((g0, a), (g1, b), (g2, c)):
            sg = _sigmoid(g)
            outs_y.append(dm * sg)
            outs_g.append(dm * yv * sg * (1.0 - sg))
        return (tuple(outs_y) + (jnp.concatenate(outs_g, axis=1),), ())

    return rowwise(fn, [(dmix, D_MODEL, 0), (proj, D_MODEL, 3), (proj, D_MODEL, 4), (proj, D_MODEL, 5),
                        (ya, D_MODEL, 0), (yb, D_MODEL, 0), (yc, D_MODEL, 0)], [],
                   [(D_MODEL, BF16)] * 3 + [(P_IN // 2, BF16, P_IN, 1)], name="merge_bwd", lp=lp)


def dproj_fill(dproj, dcq, dkr, dckv, dxbar, dbg, dcg, du, place, lp):
    tm = _row_tile(lp)

    def body(dproj_ref, dcq_ref, dkr_ref, dckv_ref, dx_ref, db_ref, dc_ref, du_ref, place_ref, o_ref):
        o_ref[:, 0:384] = dcq_ref[...]
        o_ref[:, 384:512] = _dot(dkr_ref[...], place_ref[...], 1, 0).astype(BF16)
        o_ref[:, 512:768] = dckv_ref[...]
        o_ref[:, 768:1024] = jnp.zeros((tm, 256), BF16)
        o_ref[:, 1024:1536] = dx_ref[...]
        o_ref[:, 1536:2048] = db_ref[...]
        o_ref[:, 2048:2560] = dc_ref[...]
        o_ref[:, 2560:3072] = du_ref[...]

    def row(w):
        return pl.BlockSpec((tm, w), lambda i: (i, 0))

    return pl.pallas_call(
        body, name="dproj_fill", grid=(lp // tm,),
        in_specs=[pl.BlockSpec(memory_space=pl.ANY), row(Q_RANK), row(D_ROPE), row(KV_RANK), row(MIX), row(MIX), row(MIX),
                  row(MIX), pl.BlockSpec(place.shape, lambda i: (0, 0))],
        out_specs=row(P_IN // 2), out_shape=jax.ShapeDtypeStruct((lp, P_IN), BF16),
        input_output_aliases={0: 0}, compiler_params=_cparams(),
    )(dproj, dcq, dkr, dckv, dxbar, dbg, dcg, du, place)


def loss_head(h, tgt, lp):
    def fn(row0, rv, pv):
        h_, t_ = rv
        live = (row0 + lax.broadcasted_iota(jnp.int32, h_.shape, 0)) >= X0
        diff = jnp.where(live, h_ - t_, 0.0)
        ssq = jnp.sum(jnp.sum(diff * diff, axis=1, keepdims=True), axis=0, keepdims=True)
        return ((diff * (1.0 / D_MODEL),), (ssq * (0.5 / D_MODEL),))

    return rowwise(fn, [(h, D_MODEL, 0), (tgt, D_MODEL, 0)], [], [(D_MODEL, F32)], [((1, 1), F32)], name="loss_head", lp=lp)


def _s5_consts(W):
    ab_re_rep, ab_im_rep, bb_re, bb_im = s5_prep(W['s5_a_re'], W['s5_a_im'], W['s5_log_dt'], W['s5_b_re'], W['s5_b_im'])
    pick = lambda t: t.reshape(S5_GROUPS, S5_GROUP, S5_STATE)[:, 0].reshape(1, S5_LANES)
    bb = jnp.stack([_blockdiag(bb_re.reshape(S5_GROUPS, S5_GROUP, S5_STATE)),
                    _blockdiag(bb_im.reshape(S5_GROUPS, S5_GROUP, S5_STATE))]).astype(BF16)
    return pick(ab_re_rep), pick(ab_im_rep), bb


def layer_fwd(h, hb, W, tabs, lp, ffn1=None, ffn2=True):
    cos, sin, rot = tabs[:3]
    h1, h1b, sv1 = ffn1 if ffn1 is not None else ffn_fwd(h, hb, W['wg1'], W['wu1'], W['wd1'], W['ln1_g'], W['ln1_b'], lp)
    proj = matmul(h1b[None], W['w_in'][None], tb=True, name="proj")[0]
    cqn, ckvn, kr = mla_prep(proj, cos, sin, rot, W['q_norm_g'], W['kv_norm_g'], lp)
    q96, k96, v = mla_heads(cqn, ckvn, kr, *tabs[3:], W['wq'], W['wk'], W['wv'], lp)
    o = attn_fwd(q96, k96, v, lp)
    ya = matmul(o, W['mla_wo'], ab='r', bb='r', name="mla_out")[0]
    vconv = conv_fwd(proj, W['conv_w'], W['conv_b'], lp)
    yb = matmul(vconv[None], W['conv_wout'][None], name="conv_out")[0]
    ub = s5_u(proj, lp)
    ab_re, ab_im, bb = _s5_consts(W)
    bu = bd_matmul(ub[None], bb, w_t=False, reduce=False, name="s5_bu")
    s = s5_scan(bu, ab_re, ab_im, lp)
    ys = bd_matmul(s, W['s5_ct'], w_t=False, reduce=True, name="s5_cs")[0]
    y, ygb = s5_y(ys, proj, W['s5_d'], lp)
    zg = matmul(ygb[None], W['s5_wglu'][None], name="s5_glu_mm")[0]
    glb = s5_glu(zg, y, W['s5_b_glu'], lp)
    yc = matmul(glb[None], W['s5_wout'][None], name="s5_out")[0]
    mixed = merge_fwd(proj, ya, yb, yc, lp)
    z2, h2, h2b = mm_res_ln(mixed[None], W['w_o'][None], h1, W['ln2_g'], W['ln2_b'], scale=1.0, name="wo_ln", lp=lp)
    sv = dict(sv1=sv1, h1b=h1b, proj=proj, cqn=cqn, ckvn=ckvn, q96=q96, k96=k96, v=v, o=o, ya=ya,
              vconv=vconv, yb=yb, ub=ub, ab_re=ab_re, ab_im=ab_im, bb=bb, s=s, y=y, ygb=ygb, zg=zg, glb=glb, yc=yc,
              mixed=mixed, z2=z2)
    if not ffn2:
        return h2, h2b, sv
    h3, h3b, sv['sv3'] = ffn_fwd(h2, h2b, W['wg2'], W['wu2'], W['wd2'], W['ln3_g'], W['ln3_b'], lp)
    return h3, h3b, sv


def layer_bwd(dh3, sv, W, tabs, lp, ffn1=True, ffn2=True):
    cos, sin, rot = tabs[:3]
    proj = sv['proj']
    G = {}
    dh2 = dh3
    if ffn2:
        dh2, g3 = ffn_bwd(dh3, sv['sv3'], W['wg2'], W['wu2'], W['wd2'], W['ln3_g'], lp)
        G.update(wg2=g3['wg'], wu2=g3['wu'], wd2=g3['wd'], ln3_g=g3['ln_g'], ln3_b=g3['ln_b'])
    dz2, dz2b, G['ln2_g'], G['ln2_b'] = ln_bwd(dh2, sv['z2'], W['ln2_g'], fscale=1.0, name="wo_ln_bwd", lp=lp)
    dmix = matmul(dz2b[None], W['w_o'][None], tb=True, name="wo_dx")[0]
    G['w_o'] = matmul(sv['mixed'][None], dz2b[None], ta=True, out_dtype=WGRAD, name="wo_dw")[0]
    dya, dyb, dyc, dproj = merge_bwd(dmix, proj, sv['ya'], sv['yb'], sv['yc'], lp)
    dgl = matmul(dyc[None], W['s5_wout'][None], tb=True, name="s5_out_dx")[0]
    G['s5_wout'] = matmul(sv['glb'][None], dyc[None], ta=True, out_dtype=WGRAD, name="s5_out_dw")[0]
    t1, dzb, G['s5_b_glu'] = s5_glu_bwd(dgl, sv['zg'], sv['y'], W['s5_b_glu'], lp)
    dyg = matmul(dzb[None], W['s5_wglu'][None], tb=True, res=t1[None], name="s5_glu_dx")[0]
    G['s5_wglu'] = matmul(sv['ygb'][None], dzb[None], ta=True, out_dtype=WGRAD, name="s5_glu_dw")[0]
    dyb_, du_d, G['s5_d'] = s5_y_bwd(dyg, sv['y'], proj, W['s5_d'], lp)
    ds = bd_matmul(dyb_[None], W['s5_ct'], w_t=True, reduce=False, name="s5_cs_dx")
    G['s5_ct'] = bd_outer(sv['s'], dyb_[None], "s5_cs_dw")
    g_adj, d_ab = s5_scan_bwd(ds, sv['s'], sv['ab_re'], sv['ab_im'], lp)
    du = bd_matmul(g_adj, sv['bb'], w_t=True, reduce=True, res=du_d[None], name="s5_bu_dx")[0]
    d_bb = bd_outer(sv['ub'][None], g_adj, "s5_bu_dw")
    du_b = s5_du(du, lp)
    onehot = (jnp.arange(S5_GROUP) == 0).astype(F32)
    spread = lambda t: (t.reshape(S5_GROUPS, 1, S5_STATE) * onehot[None, :, None]).reshape(_S5_ROWS, S5_STATE)
    take = lambda t: _blockdiag_extract(t, S5_GROUP, S5_STATE).reshape(_S5_ROWS, S5_STATE)
    sel = jnp.kron(jnp.eye(S5_GROUPS, dtype=F32), jnp.ones((1, S5_GROUP), F32))
    (G['s5_a_re'], G['s5_a_im'], G['s5_log_dt'], G['s5_b_re'], G['s5_b_im']) = s5_prep_bwd(
        W['s5_a_re'], W['s5_a_im'], W['s5_log_dt'], W['s5_b_re'], W['s5_b_im'],
        spread(d_ab[0]), spread(d_ab[1]), take(d_bb[0]), take(d_bb[1]), sel)
    dv = matmul(dyb[None], W['conv_wout'][None], tb=True, name="conv_out_dx")[0]
    G['conv_wout'] = matmul(sv['vconv'][None], dyb[None], ta=True, out_dtype=WGRAD, name="conv_out_dw")[0]
    dxbar, dbg, dcg, G['conv_w'], G['conv_b'] = conv_bwd(dv, proj, W['conv_w'], W['conv_b'], lp)
    do = matmul(dya[None], W['mla_wo'], tb=True, bb='o', out_dtype=BF16, name="mla_out_dx")
    G['mla_wo'] = matmul(sv['o'], dya[None], ta=True, ab='o', out_dtype=WGRAD, name="mla_out_dw")
    dq96, dk96, dvv = attn_bwd(sv['q96'], sv['k96'], sv['v'], do, lp)
    dcq, dckv, dqp, dkr = mla_heads_bwd(dq96, dk96, dvv, *tabs[3:], W['wq'], W['wk'], W['wv'], lp)
    G['wq'] = matmul(sv['cqn'][None], dqp, ta=True, bb='o', out_dtype=WGRAD, name="mla_dwq")
    G['wk'] = matmul(sv['ckvn'][None], dk96, ta=True, bb='o', out_dtype=WGRAD, name="mla_dwk")
    G['wv'] = matmul(sv['ckvn'][None], dvv, ta=True, bb='o', out_dtype=WGRAD, name="mla_dwv")
    dcq_raw, dckv_raw, dkr_raw, G['q_norm_g'], G['kv_norm_g'] = mla_prep_bwd(
        dcq, dckv, dkr, proj, cos, sin, rot, W['q_norm_g'], W['kv_norm_g'], lp)
    dproj = dproj_fill(dproj, dcq_raw, dkr_raw, dckv_raw, dxbar, dbg, dcg, du_b, jnp.eye(D_ROPE, 128, dtype=BF16), lp)
    dh1 = matmul(dproj[None], W['w_in'][None], res=dz2[None], res_scale=ALPHA, name="proj_dx")[0]
    G['w_in'] = matmul(dproj[None], sv['h1b'][None], ta=True, out_dtype=WGRAD, name="proj_dw")[0]
    if not ffn1:
        return dh1, G
    dh0, g1 = ffn_bwd(dh1, sv['sv1'], W['wg1'], W['wu1'], W['wd1'], W['ln1_g'], lp)
    G.update(wg1=g1['wg'], wu1=g1['wu'], wd1=g1['wd'], ln1_g=g1['ln_g'], ln1_b=g1['ln_b'])
    return dh0, G


def _nat_cols(st):
    return jnp.transpose(st, (1, 0, 2)).reshape(st.shape[1], -1)


def _shard_cols(nat):
    k, n = nat.shape
    return jnp.transpose(nat.reshape(k, N_SHARD, n // N_SHARD), (1, 0, 2))


def _win_pad(wt):
    z = lambda n: jnp.zeros((n, wt.shape[1]), wt.dtype)
    return jnp.concatenate([wt[0:384], wt[640:672], z(96), wt[384:640], z(256), wt[672:]], axis=0)


def _win_unpad(wp):
    return jnp.concatenate([wp[0:384], wp[512:768], wp[384:416], wp[1024:]], axis=0)


_BIG = [('ffn1_w_gate', 'T'), ('ffn1_w_up', 'T'), ('ffn1_w_down', 0), ('w_in', 'T'), ('mla_w_uq', 1), ('mla_w_ukv', 1),
        ('mla_w_o', 1), ('conv_w_out', 1), ('s5_w_glu', 0), ('s5_w_out', 1), ('w_o', 0),
        ('ffn2_w_gate', 'T'), ('ffn2_w_up', 'T'), ('ffn2_w_down', 0)]
_REPL = ['ln1_g', 'ln1_b', 'mla_q_norm_g', 'mla_kv_norm_g', 'conv_b', 's5_a_re', 's5_a_im', 's5_log_dt', 's5_b_re',
         's5_b_im', 's5_c_re', 's5_c_im', 's5_d', 's5_b_glu', 'ln2_g', 'ln2_b', 'ln3_g', 'ln3_b']


def compute_weights(st, small):
    W = {}
    for t in ('1', '2'):
        if 'ffn%s_w_gate' % t in st:
            W['wg' + t], W['wu' + t], W['wd' + t] = (st['ffn%s_w_%s' % (t, p)] for p in ('gate', 'up', 'down'))
    if 'w_in' in st:
        W.update(_mixer_weights(st))
    if small is not None:
        W.update(_small_weights(small))
    return W


def _mixer_weights(st):
    W = {}
    W['w_in'] = _win_pad(st['w_in'].reshape(D_IN, D_MODEL))
    W['wq'] = jnp.transpose(_nat_cols(st['mla_w_uq']).reshape(Q_RANK, N_HEADS, D_QK), (1, 0, 2))
    ukv = jnp.transpose(_nat_cols(st['mla_w_ukv']).reshape(KV_RANK, N_HEADS, D_NOPE + D_V), (1, 0, 2))
    W['wk'] = jnp.concatenate([ukv[:, :, :D_NOPE], jnp.zeros((N_HEADS, KV_RANK, D_ROPE), ukv.dtype)], axis=2)
    W['wv'] = ukv[:, :, D_NOPE:]
    W['mla_wo'] = _nat_cols(st['mla_w_o']).reshape(N_HEADS, D_V, D_MODEL)
    W['conv_wout'] = _nat_cols(st['conv_w_out'])
    W['s5_wglu'] = st['s5_w_glu'].reshape(MIX, MIX)
    W['s5_wout'] = _nat_cols(st['s5_w_out'])
    W['w_o'] = st['w_o'].reshape(D_MODEL, D_MODEL)
    return W


def _small_weights(small):
    W = {}
    W['conv_w'] = small['conv_w']
    for n in ('ln1_g', 'ln1_b', 'ln2_g', 'ln2_b', 'ln3_g', 'ln3_b', 'conv_b', 's5_b_glu'):
        W[n] = small[n].reshape(1, -1)
    W['q_norm_g'] = small['mla_q_norm_g'].reshape(1, -1)
    W['kv_norm_g'] = small['mla_kv_norm_g'].reshape(1, -1)
    W['s5_d'] = small['s5_d'].reshape(1, MIX)
    rep = lambda t: jnp.repeat(t, S5_GROUP, axis=0)
    W['s5_a_re'], W['s5_a_im'] = rep(small['s5_a_re']), rep(small['s5_a_im'])
    W['s5_log_dt'] = jnp.broadcast_to(rep(small['s5_log_dt'].reshape(S5_GROUPS, 1)), (_S5_ROWS, S5_STATE))
    tr = lambda t: jnp.transpose(t, (0, 2, 1)).reshape(_S5_ROWS, S5_STATE)
    W['s5_b_re'], W['s5_b_im'] = tr(small['s5_b_re']), tr(small['s5_b_im'])
    ct = lambda t: _blockdiag(jnp.transpose(t, (0, 2, 1)))
    W['s5_ct'] = jnp.stack([ct(small['s5_c_re']), -ct(small['s5_c_im'])]).astype(BF16)
    return W


def reference_grads(G, ffn=True):
    R = {}
    for t in ('1', '2') if ffn else ():
        R['ffn%s_w_gate' % t] = G['wg' + t].reshape(D_FF, D_MODEL).T
        R['ffn%s_w_up' % t] = G['wu' + t].reshape(D_FF, D_MODEL).T
        R['ffn%s_w_down' % t] = G['wd' + t].reshape(D_FF, D_MODEL)
    R['w_in_t'] = _win_unpad(G['w_in'])
    if ffn:
        R['w_in'] = R['w_in_t'].T
    R['mla_w_uq'] = jnp.transpose(G['wq'], (1, 0, 2)).reshape(Q_RANK, -1)
    R['mla_w_ukv'] = jnp.transpose(jnp.concatenate([G['wk'][:, :, :D_NOPE], G['wv']], axis=2), (1, 0, 2)).reshape(KV_RANK, -1)
    R['mla_w_o'] = G['mla_wo'].reshape(N_HEADS * D_V, D_MODEL)
    R['conv_w'], R['conv_w_out'] = G['conv_w'], G['conv_wout']
    R['s5_w_glu'], R['s5_w_out'], R['w_o'] = G['s5_wglu'], G['s5_wout'], G['w_o']
    for n in ('ln1_g', 'ln1_b', 'ln2_g', 'ln2_b', 'ln3_g', 'ln3_b', 'conv_b', 's5_b_glu'):
        if n in G:
            R[n] = G[n].reshape(-1)
    R['mla_q_norm_g'], R['mla_kv_norm_g'] = G['q_norm_g'].reshape(-1), G['kv_norm_g'].reshape(-1)
    R['s5_d'] = G['s5_d'].reshape(S5_GROUPS, S5_GROUP)
    R['s5_a_re'], R['s5_a_im'], R['s5_log_dt'] = G['s5_a_re'], G['s5_a_im'], G['s5_log_dt'].reshape(-1)
    untr = lambda t: jnp.transpose(t.reshape(S5_GROUPS, S5_GROUP, S5_STATE), (0, 2, 1))
    R['s5_b_re'], R['s5_b_im'] = untr(G['s5_b_re']), untr(G['s5_b_im'])
    unct = lambda t: jnp.transpose(_blockdiag_extract(t, S5_STATE, S5_GROUP), (0, 2, 1))
    R['s5_c_re'], R['s5_c_im'] = unct(G['s5_ct'][0]), -unct(G['s5_ct'][1])
    return R


_ANY = pl.BlockSpec(memory_space=pl.ANY)
LANES = 1024


def _place():
    x, y, c = lax.axis_index("x"), lax.axis_index("y"), lax.axis_index("c")
    chips = [(1 - x, y), (x, 1 - y), (1 - x, 1 - y)]
    return x, y, c, chips


def _rows_of(c, half):
    return pl.ds(pl.multiple_of(c * half, 8), half)


def all_gather_shards(srcs, exact):
    n, m = len(srcs), len(exact)
    halves = [s.shape[0] // 2 for s in srcs]

    def body(*refs):
        s_refs, e_refs = refs[:n], refs[n:n + m]
        o_refs, eo_refs = refs[n + m:2 * n + m], refs[2 * n + m:2 * n + 2 * m]
        send, recv, esend, erecv, osend, orecv, lsem = refs[2 * n + 2 * m:]
        x, y, c, chips = _place()
        me = 2 * x + y
        sibling = (x, y, 1 - c)
        own = [pltpu.make_async_remote_copy(src_ref=s_refs[k], dst_ref=o_refs[k].at[me], send_sem=osend.at[k],
                                            recv_sem=orecv.at[k], device_id=sibling, device_id_type=MESH) for k in range(n)]
        local = [pltpu.make_async_copy(e_refs[k], eo_refs[k].at[me], lsem.at[k]) for k in range(m)]
        for cp in own + local:
            cp.start()

        def copy(k, s, src, idx, half_c, to):
            return pltpu.make_async_remote_copy(
                src_ref=src, dst_ref=o_refs[k].at[idx, _rows_of(half_c, halves[k])], send_sem=send.at[6 * k + s],
                recv_sem=recv.at[6 * k + s], device_id=to, device_id_type=MESH)

        def ecopy(k, j, idx, to):
            return pltpu.make_async_remote_copy(src_ref=e_refs[k], dst_ref=eo_refs[k].at[idx], send_sem=esend.at[3 * k + j],
                                                recv_sem=erecv.at[3 * k + j], device_id=to, device_id_type=MESH)

        sends = []
        for k in range(n):
            mine = s_refs[k].at[_rows_of(c, halves[k])]
            sends += [copy(k, j, mine, me, c, (*chip, c)) for j, chip in enumerate(chips)]
        for k in range(m):
            sends += [ecopy(k, j, me, (*chip, c)) for j, chip in enumerate(chips)]
        for cp in sends:
            cp.start()
        for j, chip in enumerate(chips):
            idx = 2 * chip[0] + chip[1]
            for k in range(n):
                landed = o_refs[k].at[idx, _rows_of(c, halves[k])]
                copy(k, j, landed, idx, c, sibling).wait_recv()
                fwd = copy(k, 3 + j, landed, idx, c, sibling)
                fwd.start()
                sends.append(fwd)
        for j, chip in enumerate(chips):
            idx = 2 * chip[0] + chip[1]
            for k in range(n):
                copy(k, 3 + j, s_refs[k].at[_rows_of(c, halves[k])], idx, 1 - c, sibling).wait_recv()
            for k in range(m):
                ecopy(k, j, idx, sibling).wait_recv()
        for cp in sends:
            cp.wait_send()
        for cp in own + local:
            cp.wait()

    outs = pl.pallas_call(
        body, name="all_gather_weights", in_specs=[_ANY] * (n + m), out_specs=[_ANY] * (n + m),
        out_shape=[jax.ShapeDtypeStruct((N_SHARD,) + a.shape, a.dtype) for a in list(srcs) + list(exact)],
        scratch_shapes=[pltpu.SemaphoreType.DMA((6 * n,)), pltpu.SemaphoreType.DMA((6 * n,)),
                        pltpu.SemaphoreType.DMA((3 * m,)), pltpu.SemaphoreType.DMA((3 * m,)),
                        pltpu.SemaphoreType.DMA((n,)), pltpu.SemaphoreType.DMA((n,)), pltpu.SemaphoreType.DMA((m,))],
    )(*srcs, *exact)
    return outs[:n], outs[n:]


def rs_pair_swap(gs):
    n = len(gs)

    def body(*refs):
        g_refs, r_refs, send, recv = refs[:n], refs[n:2 * n], refs[2 * n], refs[2 * n + 1]
        x, y, c, _ = _place()
        copies = [pltpu.make_async_remote_copy(
            src_ref=g_refs[k].at[pl.ds(0, N_SHARD), _rows_of(1 - c, gs[k].shape[1] // 2)], dst_ref=r_refs[k],
            send_sem=send.at[k], recv_sem=recv.at[k], device_id=(x, y, 1 - c), device_id_type=MESH) for k in range(n)]
        for cp in copies:
            cp.start()
        for cp in copies:
            cp.wait()

    return pl.pallas_call(
        body, name="grad_pair_swap", in_specs=[_ANY] * n, out_specs=[_ANY] * n,
        out_shape=[jax.ShapeDtypeStruct((N_SHARD, g.shape[1] // 2, g.shape[2]), g.dtype) for g in gs],
        scratch_shapes=[pltpu.SemaphoreType.DMA((n,)), pltpu.SemaphoreType.DMA((n,))],
    )(*gs)


def _group_tile(half, n_cols, n_arrays):
    budget = (20 * 2 ** 20) // (6 * n_arrays)
    fits = [t for t in range(8, half + 1, 8) if half % t == 0 and t * n_cols * 4 <= budget]
    return max(fits) if fits else 8


def rs_pair_add(gs, rs, cidx, out_dtype, name):
    n = len(gs)
    _, K, cols = gs[0].shape
    half = K // 2
    tr = _group_tile(half, cols, n)
    nb = half // tr

    def body(c_ref, *refs):
        for g_ref, r_ref, o_ref in zip(refs[:n], refs[n:2 * n], refs[2 * n:]):
            o_ref[...] = (g_ref[...].astype(F32) + r_ref[...].astype(F32)).astype(out_dtype)

    gspec = pl.BlockSpec((None, tr, cols), lambda j, i, c: (j, c[0] * nb + i, 0))
    rspec = pl.BlockSpec((None, tr, cols), lambda j, i, c: (j, i, 0))
    return pl.pallas_call(
        body, name=name,
        grid_spec=pltpu.PrefetchScalarGridSpec(num_scalar_prefetch=1, grid=(N_SHARD, nb), in_specs=[gspec] * n + [rspec] * n,
                                               out_specs=[rspec] * n),
        out_shape=[jax.ShapeDtypeStruct((N_SHARD, half, cols), out_dtype)] * n,
        compiler_params=_cparams(),
    )(cidx, *gs, *rs)


def rs_chip_sum(qs, nl, cidx, name):
    n = len(qs)
    _, half, cols = qs[0].shape
    tr = _group_tile(half, cols, n)
    nb = half // tr

    def body(c_ref, *refs):
        for k, q_ref in enumerate(refs[:n]):
            o_ref = refs[n + k // nl]
            o_ref[k % nl] = ((q_ref[0].astype(F32) + q_ref[1].astype(F32)) + q_ref[2].astype(F32)) + q_ref[3].astype(F32)

    return pl.pallas_call(
        body, name=name,
        grid_spec=pltpu.PrefetchScalarGridSpec(
            num_scalar_prefetch=1, grid=(nb,),
            in_specs=[pl.BlockSpec((N_SHARD, tr, cols), lambda i, c: (0, i, 0))] * n,
            out_specs=[pl.BlockSpec((nl, tr, cols), lambda i, c: (0, c[0] * nb + i, 0))] * (n // nl)),
        out_shape=[jax.ShapeDtypeStruct((nl, 2 * half, cols), F32)] * (n // nl),
        compiler_params=_cparams(),
    )(cidx, *qs)


def rs_pair_gather(fs, name):
    n = len(fs)

    def body(*refs):
        f_refs, send, recv = refs[n:2 * n], refs[2 * n], refs[2 * n + 1]
        x, y, c, _ = _place()
        copies = []
        for k in range(n):
            rows = f_refs[k].at[pl.ds(0, fs[k].shape[0]), _rows_of(c, fs[k].shape[1] // 2)]
            copies.append(pltpu.make_async_remote_copy(src_ref=rows, dst_ref=rows, send_sem=send.at[k], recv_sem=recv.at[k],
                                                       device_id=(x, y, 1 - c), device_id_type=MESH))
        for cp in copies:
            cp.start()
        for cp in copies:
            cp.wait()

    return pl.pallas_call(
        body, name=name, in_specs=[_ANY] * n, out_specs=[_ANY] * n,
        out_shape=[jax.ShapeDtypeStruct(f.shape, f.dtype) for f in fs],
        input_output_aliases={k: k for k in range(n)},
        scratch_shapes=[pltpu.SemaphoreType.DMA((n,)), pltpu.SemaphoreType.DMA((n,))],
    )(*fs)


_HBM = pl.BlockSpec(memory_space=pltpu.HBM)
_SEM = pl.BlockSpec(memory_space=pltpu.SEMAPHORE)
_EFFECT = pltpu.SideEffectType.DATAFLOW_SIDE_EFFECTING


def _in_hbm(a):
    return pltpu.with_memory_space_constraint(a, pltpu.HBM)


def split_start(name, srcs, lands, after, copies_fn, n_copies):
    n = len(srcs)

    def body(*refs):
        for cp in copies_fn(refs[:n], refs[n:2 * n], refs[2 * n + 1], refs[2 * n + 2]):
            cp.start()
        refs[-1][...] = jnp.zeros_like(refs[-1])

    bufs = list(srcs) + list(lands)
    outs = pl.pallas_call(
        body, name=name,
        out_shape=(pltpu.SemaphoreType.DMA((n_copies,)), pltpu.SemaphoreType.DMA((n_copies,)),
                   *[pltpu.HBM(a.shape, a.dtype) for a in bufs], jax.ShapeDtypeStruct((8, 128), F32)),
        in_specs=[_HBM] * (2 * n) + [_ANY],
        out_specs=(_SEM, _SEM, *[_HBM] * (2 * n), pl.BlockSpec(memory_space=pltpu.VMEM)),
        input_output_aliases={i: 2 + i for i in range(2 * n)},
        compiler_params=pltpu.CompilerParams(has_side_effects=_EFFECT),
    )(*[_in_hbm(a) for a in bufs], after)
    return outs[0], outs[1], outs[2:2 + n], outs[2 + n:2 + 2 * n], outs[-1]


def split_wait(name, send, recv, srcs, lands, after, copies_fn, which=None):
    n = len(srcs)

    def body(*refs):
        copies = copies_fn(refs[:n], refs[n:2 * n], refs[2 * n], refs[2 * n + 1], which)
        for cp in copies:
            cp.wait_send()
        for cp in copies:
            cp.wait_recv()

    bufs = list(srcs) + list(lands)
    outs = pl.pallas_call(
        body, name=name, out_shape=tuple(pltpu.HBM(a.shape, a.dtype) for a in bufs),
        in_specs=[_HBM] * (2 * n) + [_SEM, _SEM, _ANY], out_specs=tuple([_HBM] * (2 * n)),
        input_output_aliases={i: i for i in range(2 * n)},
        compiler_params=pltpu.CompilerParams(has_side_effects=_EFFECT),
    )(*bufs, send, recv, after)
    return list(outs[:n]), list(outs[n:])


def _gather_copies(s_refs, l_refs, send, recv, which=None):
    x, y, c, chips = _place()
    me = 2 * x + y
    out = []
    for k in (range(len(s_refs)) if which is None else which):
        s, l = s_refs[k], l_refs[k]
        rows = _rows_of(c, s.shape[0] // 2)
        for j, chip in enumerate(chips):
            out.append(pltpu.make_async_remote_copy(src_ref=s.at[rows], dst_ref=l.at[me, rows], send_sem=send.at[4 * k + j],
                                                    recv_sem=recv.at[4 * k + j], device_id=(*chip, c), device_id_type=MESH))
        out.append(pltpu.make_async_remote_copy(src_ref=s, dst_ref=l.at[me], send_sem=send.at[4 * k + 3],
                                                recv_sem=recv.at[4 * k + 3], device_id=(x, y, 1 - c), device_id_type=MESH))
    return out


def _scatter_copies(s_refs, l_refs, send, recv, which=None):
    x, y, c, chips = _place()
    me = 2 * x + y
    return [pltpu.make_async_remote_copy(src_ref=s_refs[k].at[2 * chip[0] + chip[1]], dst_ref=l_refs[k].at[me],
                                         send_sem=send.at[3 * k + j], recv_sem=recv.at[3 * k + j], device_id=(*chip, c),
                                         device_id_type=MESH)
            for k in (range(len(s_refs)) if which is None else which) for j, chip in enumerate(chips)]


def gather_forward(lands, name):
    n = len(lands)

    def body(*refs):
        l_refs, send, recv = refs[n:2 * n], refs[2 * n], refs[2 * n + 1]
        x, y, c, chips = _place()
        copies = []
        for k in range(n):
            rows = _rows_of(c, lands[k].shape[1] // 2)
            for j, chip in enumerate(chips):
                part = l_refs[k].at[2 * chip[0] + chip[1], rows]
                copies.append(pltpu.make_async_remote_copy(src_ref=part, dst_ref=part, send_sem=send.at[3 * k + j],
                                                           recv_sem=recv.at[3 * k + j], device_id=(x, y, 1 - c),
                                                           device_id_type=MESH))
        for cp in copies:
            cp.start()
        for cp in copies:
            cp.wait()

    return pl.pallas_call(
        body, name=name, in_specs=[_ANY] * n, out_specs=[_ANY] * n,
        out_shape=[jax.ShapeDtypeStruct(a.shape, a.dtype) for a in lands],
        input_output_aliases={k: k for k in range(n)},
        scratch_shapes=[pltpu.SemaphoreType.DMA((3 * n,)), pltpu.SemaphoreType.DMA((3 * n,))],
    )(*lands)


def rs_partials(gs, wire, cidx, tag):
    rs = rs_pair_swap(gs)
    groups = {}
    for k, g in enumerate(gs):
        groups.setdefault((g.shape, jnp.dtype(wire[k]).name), []).append(k)
    ps = [None] * len(gs)
    for gi, ks in enumerate(groups.values()):
        outs = rs_pair_add([gs[k] for k in ks], [rs[k] for k in ks], cidx, wire[ks[0]], "grad_pair_add_%s%d" % (tag, gi))
        for k, o in zip(ks, outs):
            ps[k] = o
    return ps


def rs_finish(items, tag):
    cidx = lax.axis_index("c").astype(jnp.int32).reshape(1)
    groups = {}
    for i, it in enumerate(items):
        groups.setdefault((it[0].shape, len(it), it[0].dtype.name), []).append(i)
    fs = [None] * len(items)
    for gi, ids in enumerate(groups.values()):
        outs = rs_chip_sum([q for i in ids for q in items[i]], len(items[ids[0]]), cidx, "grad_chip_sum_%s%d" % (tag, gi))
        for i, o in zip(ids, outs):
            fs[i] = o
    return rs_pair_gather(fs, "grad_pair_gather_" + tag)


def adamw(w, g, m, v, name):
    shape = w.shape
    if w.ndim == 2:
        block, grid, index = shape, (1,), (lambda i: (0, 0))
    else:
        slab = shape[2:]
        unit = 4 * int(np.prod(slab[:-2] or (1,))) * (-(-slab[-1] // 128) * 128)
        if len(slab) >= 2:
            unit *= -(-slab[-2] // 8) * 8
        k = shape[1]
        tr = k
        if k * unit > 2 ** 21:
            tr = max(t for t in range(8, k, 8) if k % t == 0 and t * unit <= 2 ** 21)
        block, grid = (None, tr) + tuple(slab), (shape[0], k // tr)
        index = lambda l, i: (l, i) + (0,) * len(slab)
        if tr < min(k, 64) and len(slab) == 1:
            tc = max(t for t in range(128, slab[0] + 1, 128) if slab[0] % t == 0 and k * t * 4 <= 2 ** 21)
            block, grid = (None, k, tc), (shape[0], slab[0] // tc)
            index = lambda l, i: (l, 0, i)

    def body(w_ref, g_ref, m_ref, v_ref, d_ref, nm_ref, nv_ref):
        g_ = g_ref[...]
        m_new = ADAM_B1 * m_ref[...] + (1.0 - ADAM_B1) * g_
        v_new = ADAM_B2 * v_ref[...] + (1.0 - ADAM_B2) * (g_ * g_)
        m_hat = m_new / (1.0 - ADAM_B1 ** ADAM_STEP)
        v_hat = v_new / (1.0 - ADAM_B2 ** ADAM_STEP)
        d_ref[...] = -ADAM_LR * (m_hat / (jnp.sqrt(v_hat) + ADAM_EPS) + ADAM_WD * w_ref[...])
        nm_ref[...] = m_new
        nv_ref[...] = v_new

    spec = pl.BlockSpec(block, index)
    return pl.pallas_call(
        body, name=name, grid=grid, in_specs=[spec] * 4, out_specs=[spec] * 3,
        out_shape=[jax.ShapeDtypeStruct(shape, F32)] * 3, compiler_params=_cparams(),
    )(w, g, m, v)


_WEIGHTS = ['meta', 'ffn1_w_gate', 'ffn1_w_up', 'ffn1_w_down', 'ln1_g', 'ln1_b', 'w_in', 'mla_q_norm_g', 'mla_w_uq',
            'mla_kv_norm_g', 'mla_w_ukv', 'mla_w_o', 'conv_w', 'conv_b', 'conv_w_out', 's5_a_re', 's5_a_im', 's5_log_dt',
            's5_b_re', 's5_b_im', 's5_c_re', 's5_c_im', 's5_d', 's5_w_glu', 's5_b_glu', 's5_w_out', 'w_o', 'ln2_g', 'ln2_b',
            'ffn2_w_gate', 'ffn2_w_up', 'ffn2_w_down', 'ln3_g', 'ln3_b']


def _pad_to(flat, n):
    return jnp.concatenate([flat, jnp.zeros((n - flat.shape[0],), flat.dtype)])


def _shard_of(full, axis):
    if axis == 1:
        return _shard_cols(full)
    if axis == 'T':
        return full.T.reshape(N_SHARD, full.shape[1] // N_SHARD, full.shape[0])
    return full.reshape(N_SHARD, full.shape[0] // N_SHARD, full.shape[1])


_FFN_KEY = {'gate': 'wg', 'up': 'wu', 'down': 'wd'}


def _pad_rows(a, axis):
    k = a.shape[axis]
    extra = -k % 32
    if not extra:
        return a
    return jnp.pad(a, [(0, extra) if d == axis else (0, 0) for d in range(a.ndim)])


def _step(env):
    w = {n: env[n] for n in _WEIGHTS}
    mom = {n: env['m_' + n] for n in _WEIGHTS}
    var = {n: env['v_' + n] for n in _WEIGHTS}
    cidx = lax.axis_index("c").astype(jnp.int32).reshape(1)
    chip = 2 * lax.axis_index("x") + lax.axis_index("y")
    big_names = [n for n, _ in _BIG]
    nb = len(big_names)

    kept_t = [n for n, a in _BIG if a == 'T']
    own = {n: (jnp.swapaxes(w[n], 1, 2) if n in kept_t else w[n]) for n in big_names}
    first = [n for n in big_names if n.startswith('ffn1')]
    mix = [n for n in big_names if not n.startswith('ffn')]
    last = [n for n in big_names if n.startswith('ffn2')]
    rest = mix + last
    nm, nr = len(mix), len(mix) + len(last)
    src = lambda n, li: _pad_rows(own[n][li].astype(BF16), 0)
    gathered_first, (conv_w_st, meta_st) = all_gather_shards([src(n, 0) for n in first], [w['conv_w'], w['meta']])
    later = [src(n, 0) for n in rest] + [src(n, 1) for n in big_names]
    lands = [lax.empty((N_SHARD,) + s.shape, BF16) for s in later]
    g_send, g_recv, later_t, lands_t, token = split_start("gather_start", later, lands, gathered_first[0], _gather_copies,
                                                          4 * len(later))

    def weights_of(names, st, li, with_small):
        small = None
        if with_small:
            small = {n: w[n][li] for n in _REPL}
            small['conv_w'] = _nat_cols(conv_w_st[:, li])
        return compute_weights({n: a[:, :own[n].shape[1]] for n, a in zip(names, st)}, small)

    x2d = env['x'][0]
    lp = x2d.shape[0] + X0
    tabs = _rope_tables(lp)
    h = jnp.concatenate([jnp.zeros((PAD, D_MODEL), F32), _nat_cols(meta_st), x2d], axis=0) + token[0, 0]
    W0 = weights_of(first, gathered_first, 0, True)
    ffn1 = ffn_fwd(h, h.astype(BF16), W0['wg1'], W0['wu1'], W0['wd1'], W0['ln1_g'], W0['ln1_b'], lp)
    later_t, lands_t = split_wait("gather0_wait", g_send, g_recv, later_t, lands_t, ffn1[0], _gather_copies, range(nm))
    W0.update(weights_of(mix, gather_forward(lands_t[:nm], "gather0_forward"), 0, False))
    h, hb, sv0 = layer_fwd(None, None, W0, tabs, lp, ffn1=ffn1, ffn2=False)
    later_t, lands_t = split_wait("gather0b_wait", g_send, g_recv, later_t, lands_t, h, _gather_copies, range(nm, nr))
    W0.update(weights_of(last, gather_forward(lands_t[nm:nr], "gather0b_forward"), 0, False))
    h, hb, sv0['sv3'] = ffn_fwd(h, hb, W0['wg2'], W0['wu2'], W0['wd2'], W0['ln3_g'], W0['ln3_b'], lp)
    _, lands_t = split_wait("gather1_wait", g_send, g_recv, later_t, lands_t, h, _gather_copies, range(nr, len(later)))
    W1 = weights_of(big_names, gather_forward(lands_t[nr:], "gather1_forward"), 1, True)
    h, hb, sv1 = layer_fwd(h, hb, W1, tabs, lp)
    tgt = jnp.concatenate([jnp.zeros((X0, D_MODEL), F32), env['loss_target'][0]], axis=0)
    dh, loss_part = loss_head(h, tgt, lp)
    loss = lax.psum(loss_part[0, 0], ("x", "y", "c"))

    def shards(G, names):
        full = None if all(n.startswith('ffn') for n in names) else reference_grads(G, ffn=False)

        def one(n, a):
            if n.startswith('ffn'):
                return G[_FFN_KEY[n.split('_')[-1]] + n[3]]
            if n == 'w_in':
                return full['w_in_t'].reshape(N_SHARD, D_IN // N_SHARD, D_MODEL)
            return _shard_of(full[n], a)

        return [_pad_rows(one(n, a), 1) for n, a in _BIG if n in names]

    def scatter_start(name, ps, after):
        qs = [lax.dynamic_update_slice_in_dim(jnp.zeros_like(p), lax.dynamic_slice_in_dim(p, chip, 1, axis=0), chip, axis=0)
              for p in ps]
        return split_start(name, ps, qs, after, _scatter_copies, 3 * len(ps))

    dh, G1 = layer_bwd(dh, sv1, W1, tabs, lp)
    p1 = rs_partials(shards(G1, big_names), [BF16] * nb, cidx, "b")
    s1_send, s1_recv, p1_t, q1_t, token1 = scatter_start("scatter1_start", p1, dh)
    dh, g3 = ffn_bwd(dh, sv0['sv3'], W0['wg2'], W0['wu2'], W0['wd2'], W0['ln3_g'] + token1[0, 0], lp)
    G0 = dict(wg2=g3['wg'], wu2=g3['wu'], wd2=g3['wd'])
    p0l = rs_partials(shards(G0, last), [BF16] * len(last), cidx, "c")
    sl_send, sl_recv, p0l_t, q0l_t, token0l = scatter_start("scatter0b_start", p0l, dh)
    dh, Gm = layer_bwd(dh, sv0, dict(W0, ln2_g=W0['ln2_g'] + token0l[0, 0]), tabs, lp, ffn1=False, ffn2=False)
    G0.update(Gm, ln3_g=g3['ln_g'], ln3_b=g3['ln_b'])
    p0m = rs_partials(shards(G0, mix), [BF16] * nm, cidx, "d")
    sm_send, sm_recv, p0m_t, q0m_t, token0m = scatter_start("scatter0_start", p0m, dh)
    dh, g1 = ffn_bwd(dh, sv0['sv1'], W0['wg1'], W0['wu1'], W0['wd1'], W0['ln1_g'] + token0m[0, 0], lp)
    G0.update(wg1=g1['wg'], wu1=g1['wu'], wd1=g1['wd'], ln1_g=g1['ln_g'], ln1_b=g1['ln_b'])
    _, q1 = split_wait("scatter1_wait", s1_send, s1_recv, p1_t, q1_t, dh, _scatter_copies)
    _, q0_last = split_wait("scatter0b_wait", sl_send, sl_recv, p0l_t, q0l_t, dh, _scatter_copies)
    _, q0_mix = split_wait("scatter0_wait", sm_send, sm_recv, p0m_t, q0m_t, dh, _scatter_copies)
    q0_rest = list(q0_mix) + list(q0_last)
    full = [reference_grads(G0, ffn=False), reference_grads(G1, ffn=False)]

    s_parts = [jnp.stack([full[li][n] for li in range(DEPTH)]).reshape(-1) for n in _REPL + ['conv_w']]
    s_parts.append(dh[PAD:X0].reshape(-1))
    s_sizes = [int(p.shape[0]) for p in s_parts]
    s_rows = -(-sum(s_sizes) // (16 * LANES)) * 16
    g_small = _pad_to(jnp.concatenate(s_parts), s_rows * LANES).reshape(1, s_rows, LANES)
    g_small = jnp.broadcast_to(g_small, (N_SHARD, s_rows, LANES))
    p_last = rs_partials(shards(G0, first) + [g_small], [BF16] * len(first) + [F32], cidx, "a")
    z_send, z_recv, pz_t, qz_t, token_z = scatter_start("scatter_last_start", p_last, dh)

    def step_weights(names, grad):
        out = {}
        for n in names:
            if n in kept_t:
                res = adamw(own[n], grad[n], jnp.swapaxes(mom[n], 1, 2), jnp.swapaxes(var[n], 1, 2), "adamw_" + n)
                out[n] = [jnp.swapaxes(t, 1, 2) for t in [grad[n]] + list(res)]
            else:
                out[n] = [grad[n]] + list(adamw(w[n], grad[n], mom[n], var[n], "adamw_" + n))
        return out

    q1 = dict(zip(big_names, q1))
    q0 = dict(zip(rest, q0_rest))
    q0[rest[0]] = q0[rest[0]] + token_z[0, 0].astype(BF16)
    red = rs_finish([[q0[n], q1[n]] for n in rest], "a")
    done = step_weights(rest, {n: r[:, :own[n].shape[1]] for n, r in zip(rest, red)})
    all_done = jnp.stack([done[n][3][(0,) * done[n][3].ndim] for n in rest])
    _, q_last = split_wait("scatter_last_wait", z_send, z_recv, pz_t, qz_t, all_done, _scatter_copies)
    red = rs_finish([[q, q1[n]] for n, q in zip(first, q_last)] + [[q_last[-1]]], "b")
    f_small = red[-1].reshape(-1)

    grad = {n: r[:, :own[n].shape[1]] for n, r in zip(first, red)}
    off = 0
    for n, sz in zip(_REPL + ['conv_w', 'meta'], s_sizes):
        grad[n] = f_small[off:off + sz]
        off += sz
    for n in _REPL:
        grad[n] = grad[n].reshape(w[n].shape)
    cw = grad['conv_w'].reshape(DEPTH, 3, MIX)
    grad['conv_w'] = lax.dynamic_slice_in_dim(cw, chip * (MIX // N_SHARD), MIX // N_SHARD, axis=2)
    gm = grad['meta'].reshape(N_META, D_MODEL)
    grad['meta'] = lax.dynamic_slice_in_dim(gm, chip * (D_MODEL // N_SHARD), D_MODEL // N_SHARD, axis=1)

    done.update(step_weights([n for n in _WEIGHTS if n not in done], grad))
    return (loss, dh[X0:][None], *[done[n][k] for k in range(4) for n in _WEIGHTS])


def kernel(x, meta, ffn1_w_gate, ffn1_w_up, ffn1_w_down, ln1_g, ln1_b, w_in, mla_q_norm_g, mla_w_uq, mla_kv_norm_g, mla_w_ukv, mla_w_o, conv_w, conv_b, conv_w_out, s5_a_re, s5_a_im, s5_log_dt, s5_b_re, s5_b_im, s5_c_re, s5_c_im, s5_d, s5_w_glu, s5_b_glu, s5_w_out, w_o, ln2_g, ln2_b, ffn2_w_gate, ffn2_w_up, ffn2_w_down, ln3_g, ln3_b, loss_target, m_meta, m_ffn1_w_gate, m_ffn1_w_up, m_ffn1_w_down, m_ln1_g, m_ln1_b, m_w_in, m_mla_q_norm_g, m_mla_w_uq, m_mla_kv_norm_g, m_mla_w_ukv, m_mla_w_o, m_conv_w, m_conv_b, m_conv_w_out, m_s5_a_re, m_s5_a_im, m_s5_log_dt, m_s5_b_re, m_s5_b_im, m_s5_c_re, m_s5_c_im, m_s5_d, m_s5_w_glu, m_s5_b_glu, m_s5_w_out, m_w_o, m_ln2_g, m_ln2_b, m_ffn2_w_gate, m_ffn2_w_up, m_ffn2_w_down, m_ln3_g, m_ln3_b, v_meta, v_ffn1_w_gate, v_ffn1_w_up, v_ffn1_w_down, v_ln1_g, v_ln1_b, v_w_in, v_mla_q_norm_g, v_mla_w_uq, v_mla_kv_norm_g, v_mla_w_ukv, v_mla_w_o, v_conv_w, v_conv_b, v_conv_w_out, v_s5_a_re, v_s5_a_im, v_s5_log_dt, v_s5_b_re, v_s5_b_im, v_s5_c_re, v_s5_c_im, v_s5_d, v_s5_w_glu, v_s5_b_glu, v_s5_w_out, v_w_o, v_ln2_g, v_ln2_b, v_ffn2_w_gate, v_ffn2_w_up, v_ffn2_w_down, v_ln3_g, v_ln3_b):
    return _step(dict(locals()))
```

```python
import functools
import math

import numpy as np
import jax
import jax.numpy as jnp
from jax import lax
from jax.experimental import pallas as pl
from jax.experimental.pallas import tpu as pltpu

F32 = jnp.float32
BF16 = jnp.bfloat16

D_MODEL = 1024
DEPTH = 2
N_META = 16
PAD = 112
X0 = PAD + N_META
N_HEADS = 8
D_NOPE = 64
D_ROPE = 32
D_V = 64
Q_RANK = 384
KV_RANK = 256
MIX = 512
S5_GROUPS = 32
S5_GROUP = 16
S5_STATE = 64
S5_LANES = S5_GROUPS * S5_STATE
D_FF = 2816
N_SHARD = 4
FF_SHARD = D_FF // N_SHARD
D_IN = 5792
P_IN = 6144
ALPHA = (2.0 * DEPTH) ** 0.25
LN_EPS = 1e-5
RMS_EPS = 1e-6
ATT_SCALE = (D_NOPE + D_ROPE) ** -0.5
ROPE_BASE = 10000.0
ADAM_LR, ADAM_B1, ADAM_B2, ADAM_EPS, ADAM_WD, ADAM_STEP = 0.001, 0.9, 0.999, 1e-08, 0.01, 10
SCAN_CHUNK = 64
VMEM_LIMIT = 52 * 2 ** 20
WGRAD = BF16
MESH = pl.DeviceIdType.MESH


def _cparams(**kw):
    return pltpu.CompilerParams(vmem_limit_bytes=VMEM_LIMIT, **kw)


def _tile(n):
    if n <= 1088:
        return n
    for t in (1024, 544, 512, 272, 256, 128):
        if n % t == 0:
            return t
    return n


def _row_tile(lp):
    for t in (544, 272, 128):
        if lp % t == 0:
            return t
    return lp


def _ffn_tile(lp):
    return 1088 if lp % 1088 == 0 else _row_tile(lp)


def _sigmoid(x):
    return 1.0 / (1.0 + jnp.exp(-x))


_GELU_C = math.sqrt(2.0 / math.pi)


def _gelu(x):
    return 0.5 * x * (1.0 + jnp.tanh(_GELU_C * (x + 0.044715 * x * x * x)))


def _gelu_grad(x):
    t = jnp.tanh(_GELU_C * (x + 0.044715 * x * x * x))
    return 0.5 * (1.0 + t) + 0.5 * x * (1.0 - t * t) * _GELU_C * (1.0 + 3.0 * 0.044715 * x * x)


def _dot(a, b, ca, cb, precision=None):
    return lax.dot_general(a, b, (((ca,), (cb,)), ((), ())), preferred_element_type=F32, precision=precision)


def matmul(a, b, *, name, ta=False, tb=False, ab='n', bb='n', res=None, res_scale=1.0, scale=1.0, out_dtype=F32):
    if ta:
        _, K, M = a.shape
    else:
        _, M, K = a.shape
    if tb:
        _, N, K2 = b.shape
    else:
        _, K2, N = b.shape
    assert K == K2, (a.shape, b.shape)
    n_out = max(a.shape[0] if ab == 'o' else 1, b.shape[0] if bb == 'o' else 1)
    n_red = max(a.shape[0] if ab == 'r' else 1, b.shape[0] if bb == 'r' else 1)
    tm, tn = _tile(M), _tile(N)
    tk = K if K <= 2304 else _tile(K)
    nkt = K // tk
    n_steps = n_red * nkt

    def bsel(mode, o, r):
        if mode == 'o':
            return o
        if mode == 'r':
            return r // nkt if nkt > 1 else r
        return 0

    def ksel(r):
        if nkt == 1:
            return 0
        return r % nkt if n_red > 1 else r

    a_map = (lambda o, i, j, r: (bsel(ab, o, r), ksel(r), i)) if ta else (lambda o, i, j, r: (bsel(ab, o, r), i, ksel(r)))
    b_map = (lambda o, i, j, r: (bsel(bb, o, r), j, ksel(r))) if tb else (lambda o, i, j, r: (bsel(bb, o, r), ksel(r), j))
    o_map = lambda o, i, j, r: (o, i, j)
    in_specs = [pl.BlockSpec((None, tk, tm) if ta else (None, tm, tk), a_map),
                pl.BlockSpec((None, tn, tk) if tb else (None, tk, tn), b_map)]
    operands = [a, b]
    if res is not None:
        in_specs.append(pl.BlockSpec((None, tm, tn), o_map))
        operands.append(res)
    has_res = res is not None

    def body(*refs):
        a_ref, b_ref = refs[0], refs[1]
        res_ref = refs[2] if has_res else None
        o_ref = refs[3] if has_res else refs[2]
        part = _dot(a_ref[...].astype(BF16), b_ref[...].astype(BF16), 0 if ta else 1, 1 if tb else 0)

        def finish(acc):
            v = acc if scale == 1.0 else acc * scale
            if has_res:
                v = v + res_scale * res_ref[...].astype(F32)
            o_ref[...] = v.astype(o_ref.dtype)

        if n_steps == 1:
            finish(part)
        else:
            acc_ref = refs[-1]
            r = pl.program_id(3)

            @pl.when(r == 0)
            def _():
                acc_ref[...] = part

            @pl.when(r > 0)
            def _():
                acc_ref[...] += part

            @pl.when(r == n_steps - 1)
            def _():
                finish(acc_ref[...])

    return pl.pallas_call(
        body, name=name,
        grid=(n_out, M // tm, N // tn, n_steps),
        in_specs=in_specs,
        out_specs=pl.BlockSpec((None, tm, tn), o_map),
        out_shape=jax.ShapeDtypeStruct((n_out, M, N), out_dtype),
        scratch_shapes=[pltpu.VMEM((tm, tn), F32)] if n_steps > 1 else [],
        compiler_params=_cparams(),
    )(*operands)


def rowwise(fn, rows, pars, outs, accs=(), *, name, lp):
    tm = _row_tile(lp)
    n_rows, n_pars, n_outs, n_accs = len(rows), len(pars), len(outs), len(accs)
    outs = [tuple(o) + (o[0], 0) if len(o) == 2 else tuple(o) for o in outs]
    in_specs = [pl.BlockSpec((tm, w), functools.partial(lambda i, cb: (i, cb), cb=cb)) for _, w, cb in rows]
    in_specs += [pl.BlockSpec(p.shape, functools.partial(lambda i, nd: (0,) * nd, nd=p.ndim)) for p in pars]
    out_specs = [pl.BlockSpec((tm, w), functools.partial(lambda i, cb: (i, cb), cb=cb)) for w, _, _, cb in outs]
    out_specs += [pl.BlockSpec(s, functools.partial(lambda i, nd: (0,) * nd, nd=len(s))) for s, _ in accs]
    out_shape = [jax.ShapeDtypeStruct((lp, total), dt) for _, dt, total, _ in outs]
    out_shape += [jax.ShapeDtypeStruct(s, dt) for s, dt in accs]

    def body(*refs):
        i = pl.program_id(0)
        rv = [r[...] for r in refs[:n_rows]]
        pv = [r[...] for r in refs[n_rows:n_rows + n_pars]]
        o_refs = refs[n_rows + n_pars:n_rows + n_pars + n_outs]
        a_refs = refs[n_rows + n_pars + n_outs:]
        ov, av = fn(i * tm, rv, pv)
        for r, v in zip(o_refs, ov):
            r[...] = v.astype(r.dtype)
        if n_accs:
            @pl.when(i == 0)
            def _():
                for r, v in zip(a_refs, av):
                    r[...] = v.astype(r.dtype)

            @pl.when(i > 0)
            def _():
                for r, v in zip(a_refs, av):
                    r[...] += v.astype(r.dtype)

    res = pl.pallas_call(
        body, name=name, grid=(lp // tm,), in_specs=in_specs, out_specs=out_specs, out_shape=out_shape,
        compiler_params=_cparams(),
    )(*[r[0] for r in rows], *pars)
    return res


def _row_mask(row0, shape):
    return (row0 + lax.broadcasted_iota(jnp.int32, shape, 0)) >= PAD


def ffn_up(hb, wg, wu, lp):
    tm = _ffn_tile(lp)

    def body(h_ref, wg_ref, wu_ref, ab_ref, hid_ref):
        h = h_ref[...]
        a = _dot(h, wg_ref[...], 1, 1)
        b = _dot(h, wu_ref[...], 1, 1)
        ab_ref[0] = a.astype(BF16)
        ab_ref[1] = b.astype(BF16)
        hid_ref[...] = (a * _sigmoid(a) * b).astype(BF16)

    wspec = pl.BlockSpec((None, FF_SHARD, D_MODEL), lambda j, i: (j, 0, 0))
    return pl.pallas_call(
        body, name="ffn_up", grid=(N_SHARD, lp // tm),
        in_specs=[pl.BlockSpec((tm, D_MODEL), lambda j, i: (i, 0)), wspec, wspec],
        out_specs=[pl.BlockSpec((None, 2, tm, FF_SHARD), lambda j, i: (j, 0, i, 0)),
                   pl.BlockSpec((None, tm, FF_SHARD), lambda j, i: (j, i, 0))],
        out_shape=[jax.ShapeDtypeStruct((N_SHARD, 2, lp, FF_SHARD), BF16),
                   jax.ShapeDtypeStruct((N_SHARD, lp, FF_SHARD), BF16)],
        compiler_params=_cparams(),
    )(hb, wg, wu)


def _layer_norm(z, g, b):
    mu = jnp.mean(z, axis=-1, keepdims=True)
    zc = z - mu
    var = jnp.mean(zc * zc, axis=-1, keepdims=True)
    return zc * lax.rsqrt(var + LN_EPS) * g + b


def mm_res_ln(a, w, res, g, b, *, scale, name, lp):
    n_red, _, K = a.shape
    tm = _row_tile(lp)

    def body(a_ref, w_ref, res_ref, g_ref, b_ref, z_ref, h_ref, hb_ref, acc_ref):
        r = pl.program_id(1)
        part = _dot(a_ref[...].astype(BF16), w_ref[...], 1, 0)

        @pl.when(r == 0)
        def _():
            acc_ref[...] = part

        @pl.when(r > 0)
        def _():
            acc_ref[...] += part

        @pl.when(r == n_red - 1)
        def _():
            z = ALPHA * res_ref[...] + scale * acc_ref[...]
            z_ref[...] = z
            hn = _layer_norm(z, g_ref[...], b_ref[...])
            h_ref[...] = hn
            hb_ref[...] = hn.astype(BF16)

    row = pl.BlockSpec((tm, D_MODEL), lambda i, r: (i, 0))
    par = pl.BlockSpec((1, D_MODEL), lambda i, r: (0, 0))
    return pl.pallas_call(
        body, name=name, grid=(lp // tm, n_red),
        in_specs=[pl.BlockSpec((None, tm, K), lambda i, r: (r, i, 0)),
                  pl.BlockSpec((None, K, D_MODEL), lambda i, r: (r, 0, 0)), row, par, par],
        out_specs=[row, row, row],
        out_shape=[jax.ShapeDtypeStruct((lp, D_MODEL), F32), jax.ShapeDtypeStruct((lp, D_MODEL), F32),
                   jax.ShapeDtypeStruct((lp, D_MODEL), BF16)],
        scratch_shapes=[pltpu.VMEM((tm, D_MODEL), F32)],
        compiler_params=_cparams(),
    )(a, w, res, g, b)


def ln_bwd(dh, z, g, *, fscale, name, lp):
    def fn(row0, rv, pv):
        dh_, z_ = rv
        g_, = pv
        mu = jnp.mean(z_, axis=-1, keepdims=True)
        zc = z_ - mu
        rstd = lax.rsqrt(jnp.mean(zc * zc, axis=-1, keepdims=True) + LN_EPS)
        xh = zc * rstd
        dxh = dh_ * g_
        m1 = jnp.mean(dxh, axis=-1, keepdims=True)
        m2 = jnp.mean(dxh * xh, axis=-1, keepdims=True)
        dz = rstd * (dxh - m1 - xh * m2)
        return ((dz, fscale * dz),
                (jnp.sum(dh_ * xh, axis=0, keepdims=True), jnp.sum(dh_, axis=0, keepdims=True)))

    return rowwise(fn, [(dh, D_MODEL, 0), (z, D_MODEL, 0)], [g], [(D_MODEL, F32), (D_MODEL, BF16)],
                   [((1, D_MODEL), F32), ((1, D_MODEL), F32)], name=name, lp=lp)


def ffn_down_bwd(dfb, wd, ab, lp):
    tm = _ffn_tile(lp)

    def body(df_ref, w_ref, ab_ref, da_ref, db_ref):
        dhid = _dot(df_ref[...], w_ref[...], 1, 1)
        a = ab_ref[0].astype(F32)
        b = ab_ref[1].astype(F32)
        sg = _sigmoid(a)
        da_ref[...] = (dhid * b * (sg * (1.0 + a * (1.0 - sg)))).astype(BF16)
        db_ref[...] = (dhid * (a * sg)).astype(BF16)

    ospec = pl.BlockSpec((None, tm, FF_SHARD), lambda j, i: (j, i, 0))
    return pl.pallas_call(
        body, name="ffn_down_bwd", grid=(N_SHARD, lp // tm),
        in_specs=[pl.BlockSpec((tm, D_MODEL), lambda j, i: (i, 0)),
                  pl.BlockSpec((None, FF_SHARD, D_MODEL), lambda j, i: (j, 0, 0)),
                  pl.BlockSpec((None, 2, tm, FF_SHARD), lambda j, i: (j, 0, i, 0))],
        out_specs=[ospec, ospec],
        out_shape=[jax.ShapeDtypeStruct((N_SHARD, lp, FF_SHARD), BF16)] * 2,
        compiler_params=_cparams(),
    )(dfb, wd, ab)


def ffn_dx(da, db, wg, wu, dz, lp):
    tm = _ffn_tile(lp)

    def body(da_ref, db_ref, wg_ref, wu_ref, dz_ref, o_ref, acc_ref):
        j = pl.program_id(1)
        part = _dot(da_ref[...], wg_ref[...], 1, 0) + _dot(db_ref[...], wu_ref[...], 1, 0)

        @pl.when(j == 0)
        def _():
            acc_ref[...] = part

        @pl.when(j > 0)
        def _():
            acc_ref[...] += part

        @pl.when(j == N_SHARD - 1)
        def _():
            o_ref[...] = acc_ref[...] + ALPHA * dz_ref[...]

    aspec = pl.BlockSpec((None, tm, FF_SHARD), lambda i, j: (j, i, 0))
    wspec = pl.BlockSpec((None, FF_SHARD, D_MODEL), lambda i, j: (j, 0, 0))
    row = pl.BlockSpec((tm, D_MODEL), lambda i, j: (i, 0))
    return pl.pallas_call(
        body, name="ffn_dx", grid=(lp // tm, N_SHARD), in_specs=[aspec, aspec, wspec, wspec, row], out_specs=row,
        out_shape=jax.ShapeDtypeStruct((lp, D_MODEL), F32), scratch_shapes=[pltpu.VMEM((tm, D_MODEL), F32)],
        compiler_params=_cparams(),
    )(da, db, wg, wu, dz)


def ffn_fwd(h, hb, wg, wu, wd, g, b, lp):
    ab, hid = ffn_up(hb, wg, wu, lp)
    z, hn, hnb = mm_res_ln(hid, wd, h, g, b, scale=0.5, name="ffn_down_ln", lp=lp)
    return hn, hnb, dict(hb=hb, ab=ab, hid=hid, z=z)


def ffn_bwd(dh, sv, wg, wu, wd, g, lp):
    dz, dfb, dg, db = ln_bwd(dh, sv['z'], g, fscale=0.5, name="ffn_ln_bwd", lp=lp)
    da, dbb = ffn_down_bwd(dfb, wd, sv['ab'], lp)
    d_wd = matmul(sv['hid'], dfb[None], ta=True, ab='o', out_dtype=WGRAD, name="ffn_dwd")
    d_wg = matmul(da, sv['hb'][None], ta=True, ab='o', out_dtype=WGRAD, name="ffn_dwg")
    d_wu = matmul(dbb, sv['hb'][None], ta=True, ab='o', out_dtype=WGRAD, name="ffn_dwu")
    dh_in = ffn_dx(da, dbb, wg, wu, dz, lp)
    return dh_in, dict(wg=d_wg, wu=d_wu, wd=d_wd, ln_g=dg, ln_b=db)


def _rope_tables(lp):
    pos = np.arange(lp, dtype=np.float32) - PAD
    inv = ROPE_BASE ** (-np.arange(0, D_ROPE, 2, dtype=np.float32) / D_ROPE)
    ang = pos[:, None] * inv[None, :]
    cos = np.concatenate([np.cos(ang), np.cos(ang)], axis=1).astype(np.float32)
    sin = np.concatenate([np.sin(ang), np.sin(ang)], axis=1).astype(np.float32)
    rot = np.zeros((D_ROPE, D_ROPE), np.float32)
    half = D_ROPE // 2
    for j in range(half):
        rot[j + half, j] = -1.0
        rot[j, j + half] = 1.0
    d_qk = D_NOPE + D_ROPE
    cos_qk = np.concatenate([np.ones((lp, D_NOPE), np.float32), cos], axis=1)
    sin_qk = np.concatenate([np.zeros((lp, D_NOPE), np.float32), sin], axis=1)
    rot_qk = np.zeros((d_qk, d_qk), np.float32)
    rot_qk[D_NOPE:, D_NOPE:] = rot
    place = np.zeros((D_ROPE, d_qk), np.float32)
    place[:, D_NOPE:] = np.eye(D_ROPE, dtype=np.float32)
    return tuple(jnp.asarray(t) for t in (cos, sin, rot, cos_qk, sin_qk, rot_qk, place))


def _rot(x, rot):
    return _dot(x, rot, 1, 0, precision=lax.Precision.HIGHEST)


def _rms(x, g):
    r = lax.rsqrt(jnp.mean(x * x, axis=-1, keepdims=True) + RMS_EPS)
    return x * r * g


def mla_prep(proj, cos, sin, rot, qg, kvg, lp):
    def fn(row0, rv, pv):
        cq, krb, ckv, c, s = rv
        qg_, kvg_, rot_ = pv
        kr = krb[:, :D_ROPE]
        return ((_rms(cq, qg_), _rms(ckv, kvg_), kr * c + _rot(kr, rot_) * s), ())

    return rowwise(fn, [(proj, Q_RANK, 0), (proj, 128, 3), (proj, KV_RANK, 2), (cos, D_ROPE, 0), (sin, D_ROPE, 0)],
                   [qg, kvg, rot], [(Q_RANK, BF16), (KV_RANK, BF16), (D_ROPE, BF16)], name="mla_prep", lp=lp)


def mla_heads(cqn, ckvn, kr, cos_qk, sin_qk, rot_qk, place, wq, wk, wv, lp):
    tm = _row_tile(lp)

    def body(cq_ref, ckv_ref, kr_ref, c_ref, s_ref, rot_ref, place_ref, wq_ref, wk_ref, wv_ref, q_ref, k_ref, v_ref):
        cq = cq_ref[...]
        ckv = ckv_ref[...]
        kr_placed = _dot(kr_ref[...], place_ref[...].astype(BF16), 1, 0)
        for h in range(N_HEADS):
            q = _dot(cq, wq_ref[h], 1, 0)
            q_ref[h] = (q * c_ref[...] + _rot(q, rot_ref[...]) * s_ref[...]).astype(BF16)
            k_ref[h] = (_dot(ckv, wk_ref[h], 1, 0) + kr_placed).astype(BF16)
            v_ref[h] = _dot(ckv, wv_ref[h], 1, 0).astype(BF16)

    def row(w):
        return pl.BlockSpec((tm, w), lambda i: (i, 0))

    def whole(a):
        return pl.BlockSpec(a.shape, functools.partial(lambda i, nd: (0,) * nd, nd=a.ndim))

    def ospec(n):
        return pl.BlockSpec((N_HEADS, tm, n), lambda i: (0, i, 0))

    return pl.pallas_call(
        body, name="mla_heads", grid=(lp // tm,),
        in_specs=[row(Q_RANK), row(KV_RANK), row(D_ROPE), row(D_QK), row(D_QK), whole(rot_qk), whole(place),
                  whole(wq), whole(wk), whole(wv)],
        out_specs=[ospec(D_QK), ospec(D_QK), ospec(D_V)],
        out_shape=[jax.ShapeDtypeStruct((N_HEADS, lp, D_QK), BF16), jax.ShapeDtypeStruct((N_HEADS, lp, D_QK), BF16),
                   jax.ShapeDtypeStruct((N_HEADS, lp, D_V), BF16)],
        compiler_params=_cparams(),
    )(cqn, ckvn, kr, cos_qk, sin_qk, rot_qk, place, wq, wk, wv)


D_QK = D_NOPE + D_ROPE


def _att_probs(q, k, row0, tq, lp):
    s = _dot(q, k, 1, 1) * ATT_SCALE
    qi = row0 + lax.broadcasted_iota(jnp.int32, (tq, lp), 0)
    ki = lax.broadcasted_iota(jnp.int32, (tq, lp), 1)
    s = jnp.where((ki <= qi) & (ki >= PAD), s, -1e30)
    p = jnp.exp(s - jnp.max(s, axis=-1, keepdims=True))
    return p / jnp.sum(p, axis=-1, keepdims=True)


def _att_spec(lp, n):
    return pl.BlockSpec((None, lp, n), lambda h: (h, 0, 0))


def _att_tiles(lp):
    tiles, r = [(0, X0)], X0
    while r < lp:
        tiles.append((r, 256))
        r += 256
    assert r == lp
    return tiles


def attn_fwd(q, k, v, lp):
    def body(q_ref, k_ref, v_ref, o_ref):
        for r0, rows in _att_tiles(lp):
            ke, rq = r0 + rows, slice(r0, r0 + rows)
            p = _att_probs(q_ref[rq, :], k_ref[0:ke, :], r0, rows, ke)
            o_ref[rq, :] = _dot(p.astype(BF16), v_ref[0:ke, :], 1, 0).astype(BF16)

    return pl.pallas_call(
        body, name="attn_fwd", grid=(N_HEADS,),
        in_specs=[_att_spec(lp, D_QK), _att_spec(lp, D_QK), _att_spec(lp, D_V)],
        out_specs=_att_spec(lp, D_V), out_shape=jax.ShapeDtypeStruct((N_HEADS, lp, D_V), BF16),
        compiler_params=_cparams(),
    )(q, k, v)


def attn_bwd(q, k, v, do, lp):
    def body(q_ref, k_ref, v_ref, do_ref, dq_ref, dk_ref, dv_ref):
        dk_ref[...] = jnp.zeros_like(dk_ref)
        dv_ref[...] = jnp.zeros_like(dv_ref)
        for r0, rows in _att_tiles(lp):
            ke, rq = r0 + rows, slice(r0, r0 + rows)
            q_, do_, k_, v_ = q_ref[rq, :], do_ref[rq, :], k_ref[0:ke, :], v_ref[0:ke, :]
            p = _att_probs(q_, k_, r0, rows, ke)
            dp = _dot(do_, v_, 1, 1)
            delta = jnp.sum(p * dp, axis=-1, keepdims=True)
            ds = (p * (dp - delta) * ATT_SCALE).astype(BF16)
            dq_ref[rq, :] = _dot(ds, k_, 1, 0)
            dk_ref[0:ke, :] += _dot(ds, q_, 0, 0)
            dv_ref[0:ke, :] += _dot(p.astype(BF16), do_, 0, 0)

    qk, vv = _att_spec(lp, D_QK), _att_spec(lp, D_V)
    return pl.pallas_call(
        body, name="attn_bwd", grid=(N_HEADS,), in_specs=[qk, qk, vv, vv], out_specs=[qk, qk, vv],
        out_shape=[jax.ShapeDtypeStruct((N_HEADS, lp, D_QK), F32), jax.ShapeDtypeStruct((N_HEADS, lp, D_QK), F32),
                   jax.ShapeDtypeStruct((N_HEADS, lp, D_V), F32)],
        compiler_params=_cparams(),
    )(q, k, v, do)


def mla_heads_bwd(dq, dk, dv, cos_qk, sin_qk, rot_qk, place, wq, wk, wv, lp):
    tm = _row_tile(lp)

    def body(dq_ref, dk_ref, dv_ref, c_ref, s_ref, rot_ref, place_ref, wq_ref, wk_ref, wv_ref,
             dcq_ref, dckv_ref, dqp_ref, dkr_ref):
        dcq = jnp.zeros(dcq_ref.shape, F32)
        dckv = jnp.zeros(dckv_ref.shape, F32)
        dkr = jnp.zeros(dkr_ref.shape, F32)
        for h in range(N_HEADS):
            g = dq_ref[h]
            dqp = (g * c_ref[...] - _rot(g * s_ref[...], rot_ref[...])).astype(BF16)
            dqp_ref[h] = dqp
            dcq = dcq + _dot(dqp, wq_ref[h], 1, 1)
            dk_ = dk_ref[h]
            dckv = dckv + _dot(dk_.astype(BF16), wk_ref[h], 1, 1) + _dot(dv_ref[h].astype(BF16), wv_ref[h], 1, 1)
            dkr = dkr + _dot(dk_, place_ref[...], 1, 1, precision=lax.Precision.HIGHEST)
        dcq_ref[...] = dcq
        dckv_ref[...] = dckv
        dkr_ref[...] = dkr

    def hspec(n):
        return pl.BlockSpec((N_HEADS, tm, n), lambda i: (0, i, 0))

    def row(w):
        return pl.BlockSpec((tm, w), lambda i: (i, 0))

    def whole(a):
        return pl.BlockSpec(a.shape, functools.partial(lambda i, nd: (0,) * nd, nd=a.ndim))

    return pl.pallas_call(
        body, name="mla_heads_bwd", grid=(lp // tm,),
        in_specs=[hspec(D_QK), hspec(D_QK), hspec(D_V), row(D_QK), row(D_QK), whole(rot_qk), whole(place),
                  whole(wq), whole(wk), whole(wv)],
        out_specs=[row(Q_RANK), row(KV_RANK), hspec(D_QK), row(D_ROPE)],
        out_shape=[jax.ShapeDtypeStruct((lp, Q_RANK), F32), jax.ShapeDtypeStruct((lp, KV_RANK), F32),
                   jax.ShapeDtypeStruct((N_HEADS, lp, D_QK), BF16), jax.ShapeDtypeStruct((lp, D_ROPE), F32)],
        compiler_params=_cparams(),
    )(dq, dk, dv, cos_qk, sin_qk, rot_qk, place, wq, wk, wv)


def heads_out(o, wo, lp):
    tm = _row_tile(lp)

    def body(o_ref, w_ref, y_ref):
        acc = _dot(o_ref[0], w_ref[0], 1, 0)
        for h in range(1, N_HEADS):
            acc = acc + _dot(o_ref[h], w_ref[h], 1, 0)
        y_ref[...] = acc

    return pl.pallas_call(
        body, name="mla_out", grid=(lp // tm,),
        in_specs=[pl.BlockSpec((N_HEADS, tm, D_V), lambda i: (0, i, 0)), pl.BlockSpec(wo.shape, lambda i: (0, 0, 0))],
        out_specs=pl.BlockSpec((tm, D_MODEL), lambda i: (i, 0)), out_shape=jax.ShapeDtypeStruct((lp, D_MODEL), F32),
        compiler_params=_cparams(),
    )(o, wo)


def heads_out_dx(dy, wo, lp):
    tm = _row_tile(lp)

    def body(dy_ref, w_ref, do_ref):
        dy_ = dy_ref[...]
        for h in range(N_HEADS):
            do_ref[h] = _dot(dy_, w_ref[h], 1, 1).astype(BF16)

    return pl.pallas_call(
        body, name="mla_out_dx", grid=(lp // tm,),
        in_specs=[pl.BlockSpec((tm, D_MODEL), lambda i: (i, 0)), pl.BlockSpec(wo.shape, lambda i: (0, 0, 0))],
        out_specs=pl.BlockSpec((N_HEADS, tm, D_V), lambda i: (0, i, 0)),
        out_shape=jax.ShapeDtypeStruct((N_HEADS, lp, D_V), BF16), compiler_params=_cparams(),
    )(dy, wo)


def _rms_bwd(dy, x, g):
    r = lax.rsqrt(jnp.mean(x * x, axis=-1, keepdims=True) + RMS_EPS)
    n = x * r
    dn = dy * g
    dx = r * (dn - n * jnp.mean(dn * n, axis=-1, keepdims=True))
    return dx, jnp.sum(dy * n, axis=0, keepdims=True)


def mla_prep_bwd(dcq, dckv, dkr, proj, cos, sin, rot, qg, kvg, lp):
    def fn(row0, rv, pv):
        dcq_, dckv_, dkr_, cq, ckv, c, s = rv
        qg_, kvg_, rot_ = pv
        dxq, dgq = _rms_bwd(dcq_, cq, qg_)
        dxkv, dgkv = _rms_bwd(dckv_, ckv, kvg_)
        dkr_raw = dkr_ * c - _rot(dkr_ * s, rot_)
        return ((dxq, dxkv, dkr_raw), (dgq, dgkv))

    return rowwise(fn, [(dcq, Q_RANK, 0), (dckv, KV_RANK, 0), (dkr, D_ROPE, 0), (proj, Q_RANK, 0), (proj, KV_RANK, 2),
                        (cos, D_ROPE, 0), (sin, D_ROPE, 0)], [qg, kvg, rot],
                   [(Q_RANK, BF16), (KV_RANK, BF16), (D_ROPE, BF16)], [((1, Q_RANK), F32), ((1, KV_RANK), F32)],
                   name="mla_prep_bwd", lp=lp)


def _shift_down(x, d, rows):
    return jnp.where(rows >= d, pltpu.roll(x, d, 0), 0.0)


def _shift_up(x, d, rows, n):
    return jnp.where(rows < n - d, pltpu.roll(x, n - d, 0), 0.0)


_CONV_W = 128
_XB, _BG, _CG = 1024 // _CONV_W, 1536 // _CONV_W, 2048 // _CONV_W


def _conv_specs(lp):
    def pspec(base):
        return pl.BlockSpec((lp, _CONV_W), functools.partial(lambda c, base: (0, base + c), base=base))

    col = pl.BlockSpec((lp, _CONV_W), lambda c: (0, c))
    wspec = pl.BlockSpec((3, _CONV_W), lambda c: (0, c))
    bspec = pl.BlockSpec((1, _CONV_W), lambda c: (0, c))
    return pspec, col, wspec, bspec


def _conv_core(xbar, cg, w, bias, lp):
    rows = lax.broadcasted_iota(jnp.int32, (lp, _CONV_W), 0)
    u = jnp.where(rows >= PAD, cg * xbar, 0.0)
    u1 = _shift_down(u, 1, rows)
    u2 = _shift_down(u, 2, rows)
    y = bias + w[0:1] * u2 + w[1:2] * u1 + w[2:3] * u
    return rows, u, u1, u2, y


def conv_fwd(proj, w, bias, lp):
    pspec, col, wspec, bspec = _conv_specs(lp)

    def body(x_ref, b_ref, c_ref, w_ref, bias_ref, v_ref):
        _, _, _, _, y = _conv_core(x_ref[...], c_ref[...], w_ref[...], bias_ref[...], lp)
        v_ref[...] = (b_ref[...] * y).astype(BF16)

    return pl.pallas_call(
        body, name="conv_fwd", grid=(MIX // _CONV_W,),
        in_specs=[pspec(_XB), pspec(_BG), pspec(_CG), wspec, bspec], out_specs=col,
        out_shape=jax.ShapeDtypeStruct((lp, MIX), BF16), compiler_params=_cparams(),
    )(proj, proj, proj, w, bias)


def conv_bwd(dv, proj, w, bias, lp):
    pspec, col, wspec, bspec = _conv_specs(lp)

    def body(dv_ref, x_ref, b_ref, c_ref, w_ref, bias_ref, dx_ref, db_ref, dc_ref, dw_ref, dbias_ref):
        xbar, cg, w_ = x_ref[...], c_ref[...], w_ref[...]
        rows, u, u1, u2, y = _conv_core(xbar, cg, w_, bias_ref[...], lp)
        dv_ = dv_ref[...]
        db_ref[...] = (dv_ * y).astype(BF16)
        dy = dv_ * b_ref[...]
        dbias_ref[...] = jnp.sum(dy, axis=0, keepdims=True)
        dw_ref[0:1, :] = jnp.sum(dy * u2, axis=0, keepdims=True)
        dw_ref[1:2, :] = jnp.sum(dy * u1, axis=0, keepdims=True)
        dw_ref[2:3, :] = jnp.sum(dy * u, axis=0, keepdims=True)
        du = w_[2:3] * dy + w_[1:2] * _shift_up(dy, 1, rows, lp) + w_[0:1] * _shift_up(dy, 2, rows, lp)
        du = jnp.where(rows >= PAD, du, 0.0)
        dc_ref[...] = (du * xbar).astype(BF16)
        dx_ref[...] = (du * cg).astype(BF16)

    return pl.pallas_call(
        body, name="conv_bwd", grid=(MIX // _CONV_W,),
        in_specs=[col, pspec(_XB), pspec(_BG), pspec(_CG), wspec, bspec],
        out_specs=[col, col, col, wspec, bspec],
        out_shape=[jax.ShapeDtypeStruct((lp, MIX), BF16)] * 3 + [jax.ShapeDtypeStruct((3, MIX), F32),
                                                                jax.ShapeDtypeStruct((1, MIX), F32)],
        compiler_params=_cparams(),
    )(dv, proj, proj, proj, w, bias)


def _s5_disc(a_re, a_im, log_dt, b_re, b_im):
    dt = jnp.exp(log_dt)
    mag = jnp.exp(dt * a_re)
    ab_re, ab_im = mag * jnp.cos(dt * a_im), mag * jnp.sin(dt * a_im)
    den = a_re * a_re + a_im * a_im
    nr, ni = ab_re - 1.0, ab_im
    coef_re = (nr * a_re + ni * a_im) / den
    coef_im = (ni * a_re - nr * a_im) / den
    return ab_re, ab_im, coef_re * b_re - coef_im * b_im, coef_re * b_im + coef_im * b_re


_S5_ROWS = S5_GROUPS * S5_GROUP


def s5_prep(a_re, a_im, log_dt, b_re, b_im):
    def body(ar, ai, ld, br, bi, o0, o1, o2, o3):
        for o, v in zip((o0, o1, o2, o3), _s5_disc(ar[...], ai[...], ld[...], br[...], bi[...])):
            o[...] = v

    return pl.pallas_call(body, name="s5_prep",
                          out_shape=[jax.ShapeDtypeStruct((_S5_ROWS, S5_STATE), F32)] * 4)(a_re, a_im, log_dt, b_re, b_im)


def s5_prep_bwd(a_re, a_im, log_dt, b_re, b_im, d_ab_re, d_ab_im, d_bb_re, d_bb_im, sel):
    def body(ar, ai, ld, br, bi, g0, g1, g2, g3, sel_ref, da_re, da_im, dld, dbr, dbi):
        _, vjp = jax.vjp(_s5_disc, ar[...], ai[...], ld[...], br[...], bi[...])
        c_ar, c_ai, c_ld, c_br, c_bi = vjp((g0[...], g1[...], g2[...], g3[...]))
        s = sel_ref[...]
        hi = lax.Precision.HIGHEST
        da_re[...] = _dot(s, c_ar, 1, 0, precision=hi)
        da_im[...] = _dot(s, c_ai, 1, 0, precision=hi)
        dld[...] = jnp.sum(_dot(s, c_ld, 1, 0, precision=hi), axis=-1, keepdims=True)
        dbr[...] = c_br
        dbi[...] = c_bi

    g = jax.ShapeDtypeStruct((S5_GROUPS, S5_STATE), F32)
    full = jax.ShapeDtypeStruct((_S5_ROWS, S5_STATE), F32)
    return pl.pallas_call(body, name="s5_prep_bwd",
                          out_shape=[g, g, jax.ShapeDtypeStruct((S5_GROUPS, 1), F32), full, full],
                          )(a_re, a_im, log_dt, b_re, b_im, d_ab_re, d_ab_im, d_bb_re, d_bb_im, sel)


_SCAN_W = 128
_SCAN_STEPS = int(math.log2(SCAN_CHUNK))


def _cmul(ar, ai, br, bi):
    return ar * br - ai * bi, ar * bi + ai * br


def _scan_powers(ar, ai, reverse):
    pw = [(ar, ai)]
    for _ in range(_SCAN_STEPS):
        pw.append(_cmul(*pw[-1], *pw[-1]))
    rows = lax.broadcasted_iota(jnp.int32, (SCAN_CHUNK, ar.shape[-1]), 0)
    tr = jnp.broadcast_to(ar, rows.shape)
    ti = jnp.broadcast_to(ai, rows.shape)
    for k in range(_SCAN_STEPS):
        d = 2 ** k
        if reverse:
            live = rows < SCAN_CHUNK - d
            mr, mi = _cmul(tr, ti, _shift_up(tr, d, rows, SCAN_CHUNK), _shift_up(ti, d, rows, SCAN_CHUNK))
        else:
            live = rows >= d
            mr, mi = _cmul(tr, ti, _shift_down(tr, d, rows), _shift_down(ti, d, rows))
        tr = jnp.where(live, mr, tr)
        ti = jnp.where(live, mi, ti)
    return pw, rows, tr, ti


def s5_scan(bu, ab_re, ab_im, lp):
    n_chunks = lp // SCAN_CHUNK

    def body(bu_ref, ar_ref, ai_ref, s_ref):
        ar, ai = ar_ref[...], ai_ref[...]
        pw, rows, tr, ti = _scan_powers(ar, ai, False)

        def chunk(ci, carry):
            cr, cim = carry
            r0 = pl.multiple_of(ci * SCAN_CHUNK, SCAN_CHUNK)
            xr = bu_ref[0, pl.ds(r0, SCAN_CHUNK), :]
            xi = bu_ref[1, pl.ds(r0, SCAN_CHUNK), :]
            for k in range(_SCAN_STEPS):
                d = 2 ** k
                mr, mi = _cmul(pw[k][0], pw[k][1], _shift_down(xr, d, rows), _shift_down(xi, d, rows))
                xr, xi = xr + mr, xi + mi
            mr, mi = _cmul(tr, ti, cr, cim)
            xr, xi = xr + mr, xi + mi
            s_ref[0, pl.ds(r0, SCAN_CHUNK), :] = xr
            s_ref[1, pl.ds(r0, SCAN_CHUNK), :] = xi
            return xr[SCAN_CHUNK - 1:SCAN_CHUNK, :], xi[SCAN_CHUNK - 1:SCAN_CHUNK, :]

        zero = jnp.zeros((1, _SCAN_W), F32)
        lax.fori_loop(0, n_chunks, chunk, (zero, zero))

    spec = pl.BlockSpec((2, lp, _SCAN_W), lambda c: (0, 0, c))
    aspec = pl.BlockSpec((1, _SCAN_W), lambda c: (0, c))
    return pl.pallas_call(
        body, name="s5_scan", grid=(S5_LANES // _SCAN_W,), in_specs=[spec, aspec, aspec], out_specs=spec,
        out_shape=jax.ShapeDtypeStruct((2, lp, S5_LANES), F32), compiler_params=_cparams(),
    )(bu, ab_re, ab_im)


def s5_scan_bwd(ds, s, ab_re, ab_im, lp):
    n_chunks = lp // SCAN_CHUNK

    def body(ds_ref, s_ref, ar_ref, ai_ref, g_ref, da_ref):
        ar, ai = ar_ref[...], -ai_ref[...]
        pw, rows, tr, ti = _scan_powers(ar, ai, True)

        def chunk(k, carry):
            cr, cim, dar, dai = carry
            ci = n_chunks - 1 - k
            r0 = pl.multiple_of(ci * SCAN_CHUNK, SCAN_CHUNK)
            xr = ds_ref[0, pl.ds(r0, SCAN_CHUNK), :]
            xi = ds_ref[1, pl.ds(r0, SCAN_CHUNK), :]
            for j in range(_SCAN_STEPS):
                d = 2 ** j
                mr, mi = _cmul(pw[j][0], pw[j][1], _shift_up(xr, d, rows, SCAN_CHUNK), _shift_up(xi, d, rows, SCAN_CHUNK))
                xr, xi = xr + mr, xi + mi
            mr, mi = _cmul(tr, ti, cr, cim)
            xr, xi = xr + mr, xi + mi
            g_ref[0, pl.ds(r0, SCAN_CHUNK), :] = xr
            g_ref[1, pl.ds(r0, SCAN_CHUNK), :] = xi
            prev0 = pl.multiple_of(jnp.maximum(r0 - 8, 0), 8)
            live = (ci > 0).astype(F32)
            pr = s_ref[0, pl.ds(prev0, 8), :][7:8, :] * live
            pim = s_ref[1, pl.ds(prev0, 8), :][7:8, :] * live
            sr = s_ref[0, pl.ds(r0, SCAN_CHUNK), :]
            si = s_ref[1, pl.ds(r0, SCAN_CHUNK), :]
            sr = jnp.where(rows >= 1, pltpu.roll(sr, 1, 0), pr)
            si = jnp.where(rows >= 1, pltpu.roll(si, 1, 0), pim)
            dar = dar + jnp.sum(xr * sr + xi * si, axis=0, keepdims=True)
            dai = dai + jnp.sum(xi * sr - xr * si, axis=0, keepdims=True)
            return xr[0:1, :], xi[0:1, :], dar, dai

        zero = jnp.zeros((1, _SCAN_W), F32)
        _, _, dar, dai = lax.fori_loop(0, n_chunks, chunk, (zero, zero, zero, zero))
        da_ref[0] = dar
        da_ref[1] = dai

    spec = pl.BlockSpec((2, lp, _SCAN_W), lambda c: (0, 0, c))
    aspec = pl.BlockSpec((1, _SCAN_W), lambda c: (0, c))
    return pl.pallas_call(
        body, name="s5_scan_bwd", grid=(S5_LANES // _SCAN_W,), in_specs=[spec, spec, aspec, aspec],
        out_specs=[spec, pl.BlockSpec((2, 1, _SCAN_W), lambda c: (0, 0, c))],
        out_shape=[jax.ShapeDtypeStruct((2, lp, S5_LANES), F32), jax.ShapeDtypeStruct((2, 1, S5_LANES), F32)],
        compiler_params=_cparams(),
    )(ds, s, ab_re, ab_im)


S5_BLOCKS = 4
_S5_PER = S5_GROUPS // S5_BLOCKS


def _blockdiag(x):
    _, r, c = x.shape
    eye = jnp.eye(_S5_PER, dtype=x.dtype)
    x = x.reshape(S5_BLOCKS, _S5_PER, r, c)
    return (x[:, :, :, None, :] * eye[None, :, None, :, None]).reshape(S5_BLOCKS, _S5_PER * r, _S5_PER * c)


def _blockdiag_extract(m, r, c):
    return jnp.einsum('qgrgc->qgrc', m.reshape(S5_BLOCKS, _S5_PER, r, _S5_PER, c)).reshape(S5_GROUPS, r, c)


def bd_matmul(a, w, *, w_t, reduce, res=None, name):
    _, M, _ = a.shape
    n_w, _, k1, k2 = w.shape
    ka, kout = (k2, k1) if w_t else (k1, k2)
    tm = _row_tile(M)
    n_out, n_red = (1, n_w) if reduce else (n_w, 1)
    has_res = res is not None

    assert n_red <= 2

    def body(*refs):
        a_ref, w_ref = refs[0], refs[1]
        o_ref = refs[3] if has_res else refs[2]
        for q in range(S5_BLOCKS):
            cols = slice(q * kout, (q + 1) * kout)
            part = _dot(a_ref[:, q * ka:(q + 1) * ka].astype(BF16), w_ref[q], 1, 1 if w_t else 0)
            if n_red == 1:
                o_ref[:, cols] = part
            else:
                acc_ref = refs[-1]

                @pl.when(pl.program_id(2) == 0)
                def _():
                    acc_ref[:, cols] = part

                @pl.when(pl.program_id(2) == 1)
                def _():
                    tot = acc_ref[:, cols] + part
                    o_ref[:, cols] = tot + refs[2][:, cols] if has_res else tot

    if reduce:
        a_map, w_map = (lambda o, i, r: (r, i, 0)), (lambda o, i, r: (r, 0, 0, 0))
    else:
        a_map, w_map = (lambda o, i, r: (0, i, 0)), (lambda o, i, r: (o, 0, 0, 0))
    o_map = lambda o, i, r: (o, i, 0)
    in_specs = [pl.BlockSpec((None, tm, S5_BLOCKS * ka), a_map), pl.BlockSpec((None, S5_BLOCKS, k1, k2), w_map)]
    operands = [a, w]
    if has_res:
        in_specs.append(pl.BlockSpec((None, tm, S5_BLOCKS * kout), o_map))
        operands.append(res)
    return pl.pallas_call(
        body, name=name, grid=(n_out, M // tm, n_red), in_specs=in_specs,
        out_specs=pl.BlockSpec((None, tm, S5_BLOCKS * kout), o_map),
        out_shape=jax.ShapeDtypeStruct((n_out, M, S5_BLOCKS * kout), F32),
        scratch_shapes=[pltpu.VMEM((tm, S5_BLOCKS * kout), F32)] if n_red > 1 else [],
        compiler_params=_cparams(),
    )(*operands)


def bd_outer(a, b, name):
    na, M, wa = a.shape
    nb_, _, wb = b.shape
    ka, kb = wa // S5_BLOCKS, wb // S5_BLOCKS
    n_out = max(na, nb_)

    def body(a_ref, b_ref, o_ref):
        o_ref[...] = _dot(a_ref[...].astype(BF16), b_ref[...].astype(BF16), 0, 0)

    return pl.pallas_call(
        body, name=name, grid=(n_out, S5_BLOCKS),
        in_specs=[pl.BlockSpec((None, M, ka), (lambda o, q: (o, 0, q)) if na > 1 else (lambda o, q: (0, 0, q))),
                  pl.BlockSpec((None, M, kb), (lambda o, q: (o, 0, q)) if nb_ > 1 else (lambda o, q: (0, 0, q)))],
        out_specs=pl.BlockSpec((None, None, ka, kb), lambda o, q: (o, q, 0, 0)),
        out_shape=jax.ShapeDtypeStruct((n_out, S5_BLOCKS, ka, kb), F32), compiler_params=_cparams(),
    )(a, b)


def s5_u(proj, lp):
    def fn(row0, rv, pv):
        u, = rv
        return ((jnp.where(_row_mask(row0, u.shape), u, 0.0),), ())

    return rowwise(fn, [(proj, MIX, 5)], [], [(MIX, BF16)], name="s5_u", lp=lp)[0]


def s5_y(ys, proj, d, lp):
    def fn(row0, rv, pv):
        ys_, u = rv
        y = ys_ + pv[0] * u
        return ((y, _gelu(y)), ())

    return rowwise(fn, [(ys, MIX, 0), (proj, MIX, 5)], [d], [(MIX, F32), (MIX, BF16)], name="s5_y", lp=lp)


def s5_glu(z, y, b, lp):
    def fn(row0, rv, pv):
        z_, y_ = rv
        return ((_gelu(y_) * _sigmoid(z_ + pv[0]),), ())

    return rowwise(fn, [(z, MIX, 0), (y, MIX, 0)], [b], [(MIX, BF16)], name="s5_glu", lp=lp)[0]


def s5_glu_bwd(dgl, z, y, b, lp):
    def fn(row0, rv, pv):
        dgl_, z_, y_ = rv
        sg = _sigmoid(z_ + pv[0])
        dz = dgl_ * _gelu(y_) * sg * (1.0 - sg)
        return ((dgl_ * sg, dz), (jnp.sum(dz, axis=0, keepdims=True),))

    return rowwise(fn, [(dgl, MIX, 0), (z, MIX, 0), (y, MIX, 0)], [b], [(MIX, F32), (MIX, BF16)], [((1, MIX), F32)],
                   name="s5_glu_bwd", lp=lp)


def s5_y_bwd(dyg, y, proj, d, lp):
    def fn(row0, rv, pv):
        dyg_, y_, u = rv
        dy = dyg_ * _gelu_grad(y_)
        return ((dy, dy * pv[0]), (jnp.sum(dy * u, axis=0, keepdims=True),))

    return rowwise(fn, [(dyg, MIX, 0), (y, MIX, 0), (proj, MIX, 5)], [d], [(MIX, BF16), (MIX, F32)], [((1, MIX), F32)],
                   name="s5_y_bwd", lp=lp)


def s5_du(du, lp):
    def fn(row0, rv, pv):
        return ((jnp.where(_row_mask(row0, rv[0].shape), rv[0], 0.0),), ())

    return rowwise(fn, [(du, MIX, 0)], [], [(MIX, BF16)], name="s5_du", lp=lp)[0]


def merge_fwd(proj, ya, yb, yc, lp):
    def fn(row0, rv, pv):
        g0, g1, g2, a, b, c = rv
        return ((_sigmoid(g0) * a + _sigmoid(g1) * b + _sigmoid(g2) * c,), ())

    return rowwise(fn, [(proj, D_MODEL, 3), (proj, D_MODEL, 4), (proj, D_MODEL, 5), (ya, D_MODEL, 0), (yb, D_MODEL, 0),
                        (yc, D_MODEL, 0)], [], [(D_MODEL, BF16)], name="merge_fwd", lp=lp)[0]


def merge_bwd(dmix, proj, ya, yb, yc, lp):
    def fn(row0, rv, pv):
        dm, g0, g1, g2, a, b, c = rv
        outs_y, outs_g = [], []
        for g, yv in ((g0, a), (g1, b), (g2, c)):
            sg = _sigmoid(g)
            outs_y.append(dm * sg)
            outs_g.append(dm * yv * sg * (1.0 - sg))
        return (tuple(outs_y) + (jnp.concatenate(outs_g, axis=1),), ())

    return rowwise(fn, [(dmix, D_MODEL, 0), (proj, D_MODEL, 3), (proj, D_MODEL, 4), (proj, D_MODEL, 5),
                        (ya, D_MODEL, 0), (yb, D_MODEL, 0), (yc, D_MODEL, 0)], [],
                   [(D_MODEL, BF16)] * 3 + [(P_IN // 2, BF16, P_IN, 1)], name="merge_bwd", lp=lp)


def dproj_fill(dproj, dcq, dkr, dckv, dxbar, dbg, dcg, du, place, lp):
    tm = _row_tile(lp)

    def body(dproj_ref, dcq_ref, dkr_ref, dckv_ref, dx_ref, db_ref, dc_ref, du_ref, place_ref, o_ref):
        o_ref[:, 0:384] = dcq_ref[...]
        o_ref[:, 384:512] = _dot(dkr_ref[...], place_ref[...], 1, 0).astype(BF16)
        o_ref[:, 512:768] = dckv_ref[...]
        o_ref[:, 768:1024] = jnp.zeros((tm, 256), BF16)
        o_ref[:, 1024:1536] = dx_ref[...]
        o_ref[:, 1536:2048] = db_ref[...]
        o_ref[:, 2048:2560] = dc_ref[...]
        o_ref[:, 2560:3072] = du_ref[...]

    def row(w):
        return pl.BlockSpec((tm, w), lambda i: (i, 0))

    return pl.pallas_call(
        body, name="dproj_fill", grid=(lp // tm,),
        in_specs=[pl.BlockSpec(memory_space=pl.ANY), row(Q_RANK), row(D_ROPE), row(KV_RANK), row(MIX), row(MIX), row(MIX),
                  row(MIX), pl.BlockSpec(place.shape, lambda i: (0, 0))],
        out_specs=row(P_IN // 2), out_shape=jax.ShapeDtypeStruct((lp, P_IN), BF16),
        input_output_aliases={0: 0}, compiler_params=_cparams(),
    )(dproj, dcq, dkr, dckv, dxbar, dbg, dcg, du, place)


def loss_head(h, tgt, lp):
    def fn(row0, rv, pv):
        h_, t_ = rv
        live = (row0 + lax.broadcasted_iota(jnp.int32, h_.shape, 0)) >= X0
        diff = jnp.where(live, h_ - t_, 0.0)
        ssq = jnp.sum(jnp.sum(diff * diff, axis=1, keepdims=True), axis=0, keepdims=True)
        return ((diff * (1.0 / D_MODEL),), (ssq * (0.5 / D_MODEL),))

    return rowwise(fn, [(h, D_MODEL, 0), (tgt, D_MODEL, 0)], [], [(D_MODEL, F32)], [((1, 1), F32)], name="loss_head", lp=lp)


def _s5_consts(W):
    ab_re_rep, ab_im_rep, bb_re, bb_im = s5_prep(W['s5_a_re'], W['s5_a_im'], W['s5_log_dt'], W['s5_b_re'], W['s5_b_im'])
    pick = lambda t: t.reshape(S5_GROUPS, S5_GROUP, S5_STATE)[:, 0].reshape(1, S5_LANES)
    bb = jnp.stack([_blockdiag(bb_re.reshape(S5_GROUPS, S5_GROUP, S5_STATE)),
                    _blockdiag(bb_im.reshape(S5_GROUPS, S5_GROUP, S5_STATE))]).astype(BF16)
    return pick(ab_re_rep), pick(ab_im_rep), bb


def layer_fwd(h, hb, W, tabs, lp, ffn1=None, ffn2=True):
    cos, sin, rot = tabs[:3]
    h1, h1b, sv1 = ffn1 if ffn1 is not None else ffn_fwd(h, hb, W['wg1'], W['wu1'], W['wd1'], W['ln1_g'], W['ln1_b'], lp)
    proj = matmul(h1b[None], W['w_in'][None], tb=True, name="proj")[0]
    cqn, ckvn, kr = mla_prep(proj, cos, sin, rot, W['q_norm_g'], W['kv_norm_g'], lp)
    q96, k96, v = mla_heads(cqn, ckvn, kr, *tabs[3:], W['wq'], W['wk'], W['wv'], lp)
    o = attn_fwd(q96, k96, v, lp)
    ya = heads_out(o, W['mla_wo'], lp)
    vconv = conv_fwd(proj, W['conv_w'], W['conv_b'], lp)
    yb = matmul(vconv[None], W['conv_wout'][None], name="conv_out")[0]
    ub = s5_u(proj, lp)
    ab_re, ab_im, bb = _s5_consts(W)
    bu = bd_matmul(ub[None], bb, w_t=False, reduce=False, name="s5_bu")
    s = s5_scan(bu, ab_re, ab_im, lp)
    ys = bd_matmul(s, W['s5_ct'], w_t=False, reduce=True, name="s5_cs")[0]
    y, ygb = s5_y(ys, proj, W['s5_d'], lp)
    zg = matmul(ygb[None], W['s5_wglu'][None], name="s5_glu_mm")[0]
    glb = s5_glu(zg, y, W['s5_b_glu'], lp)
    yc = matmul(glb[None], W['s5_wout'][None], name="s5_out")[0]
    mixed = merge_fwd(proj, ya, yb, yc, lp)
    z2, h2, h2b = mm_res_ln(mixed[None], W['w_o'][None], h1, W['ln2_g'], W['ln2_b'], scale=1.0, name="wo_ln", lp=lp)
    sv = dict(sv1=sv1, h1b=h1b, proj=proj, cqn=cqn, ckvn=ckvn, q96=q96, k96=k96, v=v, o=o, ya=ya,
              vconv=vconv, yb=yb, ub=ub, ab_re=ab_re, ab_im=ab_im, bb=bb, s=s, y=y, ygb=ygb, zg=zg, glb=glb, yc=yc,
              mixed=mixed, z2=z2)
    if not ffn2:
        return h2, h2b, sv
    h3, h3b, sv['sv3'] = ffn_fwd(h2, h2b, W['wg2'], W['wu2'], W['wd2'], W['ln3_g'], W['ln3_b'], lp)
    return h3, h3b, sv


def layer_bwd(dh3, sv, W, tabs, lp, ffn1=True, ffn2=True):
    cos, sin, rot = tabs[:3]
    proj = sv['proj']
    G = {}
    dh2 = dh3
    if ffn2:
        dh2, g3 = ffn_bwd(dh3, sv['sv3'], W['wg2'], W['wu2'], W['wd2'], W['ln3_g'], lp)
        G.update(wg2=g3['wg'], wu2=g3['wu'], wd2=g3['wd'], ln3_g=g3['ln_g'], ln3_b=g3['ln_b'])
    dz2, dz2b, G['ln2_g'], G['ln2_b'] = ln_bwd(dh2, sv['z2'], W['ln2_g'], fscale=1.0, name="wo_ln_bwd", lp=lp)
    dmix = matmul(dz2b[None], W['w_o'][None], tb=True, name="wo_dx")[0]
    G['w_o'] = matmul(sv['mixed'][None], dz2b[None], ta=True, out_dtype=WGRAD, name="wo_dw")[0]
    dya, dyb, dyc, dproj = merge_bwd(dmix, proj, sv['ya'], sv['yb'], sv['yc'], lp)
    dgl = matmul(dyc[None], W['s5_wout'][None], tb=True, name="s5_out_dx")[0]
    G['s5_wout'] = matmul(sv['glb'][None], dyc[None], ta=True, out_dtype=WGRAD, name="s5_out_dw")[0]
    t1, dzb, G['s5_b_glu'] = s5_glu_bwd(dgl, sv['zg'], sv['y'], W['s5_b_glu'], lp)
    dyg = matmul(dzb[None], W['s5_wglu'][None], tb=True, res=t1[None], name="s5_glu_dx")[0]
    G['s5_wglu'] = matmul(sv['ygb'][None], dzb[None], ta=True, out_dtype=WGRAD, name="s5_glu_dw")[0]
    dyb_, du_d, G['s5_d'] = s5_y_bwd(dyg, sv['y'], proj, W['s5_d'], lp)
    ds = bd_matmul(dyb_[None], W['s5_ct'], w_t=True, reduce=False, name="s5_cs_dx")
    G['s5_ct'] = bd_outer(sv['s'], dyb_[None], "s5_cs_dw")
    g_adj, d_ab = s5_scan_bwd(ds, sv['s'], sv['ab_re'], sv['ab_im'], lp)
    du = bd_matmul(g_adj, sv['bb'], w_t=True, reduce=True, res=du_d[None], name="s5_bu_dx")[0]
    d_bb = bd_outer(sv['ub'][None], g_adj, "s5_bu_dw")
    du_b = s5_du(du, lp)
    onehot = (jnp.arange(S5_GROUP) == 0).astype(F32)
    spread = lambda t: (t.reshape(S5_GROUPS, 1, S5_STATE) * onehot[None, :, None]).reshape(_S5_ROWS, S5_STATE)
    take = lambda t: _blockdiag_extract(t, S5_GROUP, S5_STATE).reshape(_S5_ROWS, S5_STATE)
    sel = jnp.kron(jnp.eye(S5_GROUPS, dtype=F32), jnp.ones((1, S5_GROUP), F32))
    (G['s5_a_re'], G['s5_a_im'], G['s5_log_dt'], G['s5_b_re'], G['s5_b_im']) = s5_prep_bwd(
        W['s5_a_re'], W['s5_a_im'], W['s5_log_dt'], W['s5_b_re'], W['s5_b_im'],
        spread(d_ab[0]), spread(d_ab[1]), take(d_bb[0]), take(d_bb[1]), sel)
    dv = matmul(dyb[None], W['conv_wout'][None], tb=True, name="conv_out_dx")[0]
    G['conv_wout'] = matmul(sv['vconv'][None], dyb[None], ta=True, out_dtype=WGRAD, name="conv_out_dw")[0]
    dxbar, dbg, dcg, G['conv_w'], G['conv_b'] = conv_bwd(dv, proj, W['conv_w'], W['conv_b'], lp)
    do = heads_out_dx(dya, W['mla_wo'], lp)
    G['mla_wo'] = matmul(sv['o'], dya[None], ta=True, ab='o', out_dtype=WGRAD, name="mla_out_dw")
    dq96, dk96, dvv = attn_bwd(sv['q96'], sv['k96'], sv['v'], do, lp)
    dcq, dckv, dqp, dkr = mla_heads_bwd(dq96, dk96, dvv, *tabs[3:], W['wq'], W['wk'], W['wv'], lp)
    G['wq'] = matmul(sv['cqn'][None], dqp, ta=True, bb='o', out_dtype=WGRAD, name="mla_dwq")
    G['wk'] = matmul(sv['ckvn'][None], dk96, ta=True, bb='o', out_dtype=WGRAD, name="mla_dwk")
    G['wv'] = matmul(sv['ckvn'][None], dvv, ta=True, bb='o', out_dtype=WGRAD, name="mla_dwv")
    dcq_raw, dckv_raw, dkr_raw, G['q_norm_g'], G['kv_norm_g'] = mla_prep_bwd(
        dcq, dckv, dkr, proj, cos, sin, rot, W['q_norm_g'], W['kv_norm_g'], lp)
    dproj = dproj_fill(dproj, dcq_raw, dkr_raw, dckv_raw, dxbar, dbg, dcg, du_b, jnp.eye(D_ROPE, 128, dtype=BF16), lp)
    dh1 = matmul(dproj[None], W['w_in'][None], res=dz2[None], res_scale=ALPHA, name="proj_dx")[0]
    G['w_in'] = matmul(dproj[None], sv['h1b'][None], ta=True, out_dtype=WGRAD, name="proj_dw")[0]
    if not ffn1:
        return dh1, G
    dh0, g1 = ffn_bwd(dh1, sv['sv1'], W['wg1'], W['wu1'], W['wd1'], W['ln1_g'], lp)
    G.update(wg1=g1['wg'], wu1=g1['wu'], wd1=g1['wd'], ln1_g=g1['ln_g'], ln1_b=g1['ln_b'])
    return dh0, G


def _nat_cols(st):
    return jnp.transpose(st, (1, 0, 2)).reshape(st.shape[1], -1)


def _shard_cols(nat):
    k, n = nat.shape
    return jnp.transpose(nat.reshape(k, N_SHARD, n // N_SHARD), (1, 0, 2))


def _win_pad(wt):
    z = lambda n: jnp.zeros((n, wt.shape[1]), wt.dtype)
    return jnp.concatenate([wt[0:384], wt[640:672], z(96), wt[384:640], z(256), wt[672:]], axis=0)


def _win_unpad(wp):
    return jnp.concatenate([wp[0:384], wp[512:768], wp[384:416], wp[1024:]], axis=0)


_BIG = [('ffn1_w_gate', 'T'), ('ffn1_w_up', 'T'), ('ffn1_w_down', 0), ('w_in', 'T'), ('mla_w_uq', 1), ('mla_w_ukv', 1),
        ('mla_w_o', 1), ('conv_w_out', 1), ('s5_w_glu', 0), ('s5_w_out', 1), ('w_o', 0),
        ('ffn2_w_gate', 'T'), ('ffn2_w_up', 'T'), ('ffn2_w_down', 0)]
_REPL = ['ln1_g', 'ln1_b', 'mla_q_norm_g', 'mla_kv_norm_g', 'conv_b', 's5_a_re', 's5_a_im', 's5_log_dt', 's5_b_re',
         's5_b_im', 's5_c_re', 's5_c_im', 's5_d', 's5_b_glu', 'ln2_g', 'ln2_b', 'ln3_g', 'ln3_b']


def compute_weights(st, small):
    W = {}
    for t in ('1', '2'):
        if 'ffn%s_w_gate' % t in st:
            W['wg' + t], W['wu' + t], W['wd' + t] = (st['ffn%s_w_%s' % (t, p)] for p in ('gate', 'up', 'down'))
    if 'w_in' in st:
        W.update(_mixer_weights(st))
    if small is not None:
        W.update(_small_weights(small))
    return W


def _mixer_weights(st):
    W = {}
    W['w_in'] = _win_pad(st['w_in'].reshape(D_IN, D_MODEL))
    W['wq'] = jnp.transpose(_nat_cols(st['mla_w_uq']).reshape(Q_RANK, N_HEADS, D_QK), (1, 0, 2))
    ukv = jnp.transpose(_nat_cols(st['mla_w_ukv']).reshape(KV_RANK, N_HEADS, D_NOPE + D_V), (1, 0, 2))
    W['wk'] = jnp.concatenate([ukv[:, :, :D_NOPE], jnp.zeros((N_HEADS, KV_RANK, D_ROPE), ukv.dtype)], axis=2)
    W['wv'] = ukv[:, :, D_NOPE:]
    W['mla_wo'] = _nat_cols(st['mla_w_o']).reshape(N_HEADS, D_V, D_MODEL)
    W['conv_wout'] = _nat_cols(st['conv_w_out'])
    W['s5_wglu'] = st['s5_w_glu'].reshape(MIX, MIX)
    W['s5_wout'] = _nat_cols(st['s5_w_out'])
    W['w_o'] = st['w_o'].reshape(D_MODEL, D_MODEL)
    return W


def _small_weights(small):
    W = {}
    W['conv_w'] = small['conv_w']
    for n in ('ln1_g', 'ln1_b', 'ln2_g', 'ln2_b', 'ln3_g', 'ln3_b', 'conv_b', 's5_b_glu'):
        W[n] = small[n].reshape(1, -1)
    W['q_norm_g'] = small['mla_q_norm_g'].reshape(1, -1)
    W['kv_norm_g'] = small['mla_kv_norm_g'].reshape(1, -1)
    W['s5_d'] = small['s5_d'].reshape(1, MIX)
    rep = lambda t: jnp.repeat(t, S5_GROUP, axis=0)
    W['s5_a_re'], W['s5_a_im'] = rep(small['s5_a_re']), rep(small['s5_a_im'])
    W['s5_log_dt'] = jnp.broadcast_to(rep(small['s5_log_dt'].reshape(S5_GROUPS, 1)), (_S5_ROWS, S5_STATE))
    tr = lambda t: jnp.transpose(t, (0, 2, 1)).reshape(_S5_ROWS, S5_STATE)
    W['s5_b_re'], W['s5_b_im'] = tr(small['s5_b_re']), tr(small['s5_b_im'])
    ct = lambda t: _blockdiag(jnp.transpose(t, (0, 2, 1)))
    W['s5_ct'] = jnp.stack([ct(small['s5_c_re']), -ct(small['s5_c_im'])]).astype(BF16)
    return W


def reference_grads(G, ffn=True):
    R = {}
    for t in ('1', '2') if ffn else ():
        R['ffn%s_w_gate' % t] = G['wg' + t].reshape(D_FF, D_MODEL).T
        R['ffn%s_w_up' % t] = G['wu' + t].reshape(D_FF, D_MODEL).T
        R['ffn%s_w_down' % t] = G['wd' + t].reshape(D_FF, D_MODEL)
    R['w_in_t'] = _win_unpad(G['w_in'])
    if ffn:
        R['w_in'] = R['w_in_t'].T
    R['mla_w_uq'] = jnp.transpose(G['wq'], (1, 0, 2)).reshape(Q_RANK, -1)
    R['mla_w_ukv'] = jnp.transpose(jnp.concatenate([G['wk'][:, :, :D_NOPE], G['wv']], axis=2), (1, 0, 2)).reshape(KV_RANK, -1)
    R['mla_w_o'] = G['mla_wo'].reshape(N_HEADS * D_V, D_MODEL)
    R['conv_w'], R['conv_w_out'] = G['conv_w'], G['conv_wout']
    R['s5_w_glu'], R['s5_w_out'], R['w_o'] = G['s5_wglu'], G['s5_wout'], G['w_o']
    for n in ('ln1_g', 'ln1_b', 'ln2_g', 'ln2_b', 'ln3_g', 'ln3_b', 'conv_b', 's5_b_glu'):
        if n in G:
            R[n] = G[n].reshape(-1)
    R['mla_q_norm_g'], R['mla_kv_norm_g'] = G['q_norm_g'].reshape(-1), G['kv_norm_g'].reshape(-1)
    R['s5_d'] = G['s5_d'].reshape(S5_GROUPS, S5_GROUP)
    R['s5_a_re'], R['s5_a_im'], R['s5_log_dt'] = G['s5_a_re'], G['s5_a_im'], G['s5_log_dt'].reshape(-1)
    untr = lambda t: jnp.transpose(t.reshape(S5_GROUPS, S5_GROUP, S5_STATE), (0, 2, 1))
    R['s5_b_re'], R['s5_b_im'] = untr(G['s5_b_re']), untr(G['s5_b_im'])
    unct = lambda t: jnp.transpose(_blockdiag_extract(t, S5_STATE, S5_GROUP), (0, 2, 1))
    R['s5_c_re'], R['s5_c_im'] = unct(G['s5_ct'][0]), -unct(G['s5_ct'][1])
    return R


_ANY = pl.BlockSpec(memory_space=pl.ANY)
LANES = 1024


def _place():
    x, y, c = lax.axis_index("x"), lax.axis_index("y"), lax.axis_index("c")
    chips = [(1 - x, y), (x, 1 - y), (1 - x, 1 - y)]
    return x, y, c, chips


def _rows_of(c, half):
    return pl.ds(pl.multiple_of(c * half, 8), half)


def all_gather_shards(srcs, exact):
    n, m = len(srcs), len(exact)
    halves = [s.shape[0] // 2 for s in srcs]

    def body(*refs):
        s_refs, e_refs = refs[:n], refs[n:n + m]
        o_refs, eo_refs = refs[n + m:2 * n + m], refs[2 * n + m:2 * n + 2 * m]
        send, recv, esend, erecv, osend, orecv, lsem = refs[2 * n + 2 * m:]
        x, y, c, chips = _place()
        me = 2 * x + y
        sibling = (x, y, 1 - c)
        own = [pltpu.make_async_remote_copy(src_ref=s_refs[k], dst_ref=o_refs[k].at[me], send_sem=osend.at[k],
                                            recv_sem=orecv.at[k], device_id=sibling, device_id_type=MESH) for k in range(n)]
        local = [pltpu.make_async_copy(e_refs[k], eo_refs[k].at[me], lsem.at[k]) for k in range(m)]
        for cp in own + local:
            cp.start()

        def copy(k, s, src, idx, half_c, to):
            return pltpu.make_async_remote_copy(
                src_ref=src, dst_ref=o_refs[k].at[idx, _rows_of(half_c, halves[k])], send_sem=send.at[6 * k + s],
                recv_sem=recv.at[6 * k + s], device_id=to, device_id_type=MESH)

        def ecopy(k, j, idx, to):
            return pltpu.make_async_remote_copy(src_ref=e_refs[k], dst_ref=eo_refs[k].at[idx], send_sem=esend.at[3 * k + j],
                                                recv_sem=erecv.at[3 * k + j], device_id=to, device_id_type=MESH)

        sends = []
        for k in range(n):
            mine = s_refs[k].at[_rows_of(c, halves[k])]
            sends += [copy(k, j, mine, me, c, (*chip, c)) for j, chip in enumerate(chips)]
        for k in range(m):
            sends += [ecopy(k, j, me, (*chip, c)) for j, chip in enumerate(chips)]
        for cp in sends:
            cp.start()
        for j, chip in enumerate(chips):
            idx = 2 * chip[0] + chip[1]
            for k in range(n):
                landed = o_refs[k].at[idx, _rows_of(c, halves[k])]
                copy(k, j, landed, idx, c, sibling).wait_recv()
                fwd = copy(k, 3 + j, landed, idx, c, sibling)
                fwd.start()
                sends.append(fwd)
        for j, chip in enumerate(chips):
            idx = 2 * chip[0] + chip[1]
            for k in range(n):
                copy(k, 3 + j, s_refs[k].at[_rows_of(c, halves[k])], idx, 1 - c, sibling).wait_recv()
            for k in range(m):
                ecopy(k, j, idx, sibling).wait_recv()
        for cp in sends:
            cp.wait_send()
        for cp in own + local:
            cp.wait()

    outs = pl.pallas_call(
        body, name="all_gather_weights", in_specs=[_ANY] * (n + m), out_specs=[_ANY] * (n + m),
        out_shape=[jax.ShapeDtypeStruct((N_SHARD,) + a.shape, a.dtype) for a in list(srcs) + list(exact)],
        scratch_shapes=[pltpu.SemaphoreType.DMA((6 * n,)), pltpu.SemaphoreType.DMA((6 * n,)),
                        pltpu.SemaphoreType.DMA((3 * m,)), pltpu.SemaphoreType.DMA((3 * m,)),
                        pltpu.SemaphoreType.DMA((n,)), pltpu.SemaphoreType.DMA((n,)), pltpu.SemaphoreType.DMA((m,))],
    )(*srcs, *exact)
    return outs[:n], outs[n:]


def rs_pair_swap(gs):
    n = len(gs)

    def body(*refs):
        g_refs, r_refs, send, recv = refs[:n], refs[n:2 * n], refs[2 * n], refs[2 * n + 1]
        x, y, c, _ = _place()
        copies = [pltpu.make_async_remote_copy(
            src_ref=g_refs[k].at[pl.ds(0, N_SHARD), _rows_of(1 - c, gs[k].shape[1] // 2)], dst_ref=r_refs[k],
            send_sem=send.at[k], recv_sem=recv.at[k], device_id=(x, y, 1 - c), device_id_type=MESH) for k in range(n)]
        for cp in copies:
            cp.start()
        for cp in copies:
            cp.wait()

    return pl.pallas_call(
        body, name="grad_pair_swap", in_specs=[_ANY] * n, out_specs=[_ANY] * n,
        out_shape=[jax.ShapeDtypeStruct((N_SHARD, g.shape[1] // 2, g.shape[2]), g.dtype) for g in gs],
        scratch_shapes=[pltpu.SemaphoreType.DMA((n,)), pltpu.SemaphoreType.DMA((n,))],
    )(*gs)


def _group_tile(half, n_cols, n_arrays):
    budget = (20 * 2 ** 20) // (6 * n_arrays)
    fits = [t for t in range(8, half + 1, 8) if half % t == 0 and t * n_cols * 4 <= budget]
    return max(fits) if fits else 8


def rs_pair_add(gs, rs, cidx, out_dtype, name):
    n = len(gs)
    _, K, cols = gs[0].shape
    half = K // 2
    tr = _group_tile(half, cols, n)
    nb = half // tr

    def body(c_ref, *refs):
        for g_ref, r_ref, o_ref in zip(refs[:n], refs[n:2 * n], refs[2 * n:]):
            o_ref[...] = (g_ref[...].astype(F32) + r_ref[...].astype(F32)).astype(out_dtype)

    gspec = pl.BlockSpec((None, tr, cols), lambda j, i, c: (j, c[0] * nb + i, 0))
    rspec = pl.BlockSpec((None, tr, cols), lambda j, i, c: (j, i, 0))
    return pl.pallas_call(
        body, name=name,
        grid_spec=pltpu.PrefetchScalarGridSpec(num_scalar_prefetch=1, grid=(N_SHARD, nb), in_specs=[gspec] * n + [rspec] * n,
                                               out_specs=[rspec] * n),
        out_shape=[jax.ShapeDtypeStruct((N_SHARD, half, cols), out_dtype)] * n,
        compiler_params=_cparams(),
    )(cidx, *gs, *rs)


def rs_chip_sum(qs, nl, cidx, name):
    n = len(qs)
    _, half, cols = qs[0].shape
    tr = _group_tile(half, cols, n)
    nb = half // tr

    def body(c_ref, *refs):
        for k, q_ref in enumerate(refs[:n]):
            o_ref = refs[n + k // nl]
            o_ref[k % nl] = ((q_ref[0].astype(F32) + q_ref[1].astype(F32)) + q_ref[2].astype(F32)) + q_ref[3].astype(F32)

    return pl.pallas_call(
        body, name=name,
        grid_spec=pltpu.PrefetchScalarGridSpec(
            num_scalar_prefetch=1, grid=(nb,),
            in_specs=[pl.BlockSpec((N_SHARD, tr, cols), lambda i, c: (0, i, 0))] * n,
            out_specs=[pl.BlockSpec((nl, tr, cols), lambda i, c: (0, c[0] * nb + i, 0))] * (n // nl)),
        out_shape=[jax.ShapeDtypeStruct((nl, 2 * half, cols), F32)] * (n // nl),
        compiler_params=_cparams(),
    )(cidx, *qs)


def rs_pair_gather(fs, name):
    n = len(fs)

    def body(*refs):
        f_refs, send, recv = refs[n:2 * n], refs[2 * n], refs[2 * n + 1]
        x, y, c, _ = _place()
        copies = []
        for k in range(n):
            rows = f_refs[k].at[pl.ds(0, fs[k].shape[0]), _rows_of(c, fs[k].shape[1] // 2)]
            copies.append(pltpu.make_async_remote_copy(src_ref=rows, dst_ref=rows, send_sem=send.at[k], recv_sem=recv.at[k],
                                                       device_id=(x, y, 1 - c), device_id_type=MESH))
        for cp in copies:
            cp.start()
        for cp in copies:
            cp.wait()

    return pl.pallas_call(
        body, name=name, in_specs=[_ANY] * n, out_specs=[_ANY] * n,
        out_shape=[jax.ShapeDtypeStruct(f.shape, f.dtype) for f in fs],
        input_output_aliases={k: k for k in range(n)},
        scratch_shapes=[pltpu.SemaphoreType.DMA((n,)), pltpu.SemaphoreType.DMA((n,))],
    )(*fs)


_HBM = pl.BlockSpec(memory_space=pltpu.HBM)
_SEM = pl.BlockSpec(memory_space=pltpu.SEMAPHORE)
_EFFECT = pltpu.SideEffectType.DATAFLOW_SIDE_EFFECTING


def _in_hbm(a):
    return pltpu.with_memory_space_constraint(a, pltpu.HBM)


def split_start(name, srcs, lands, after, copies_fn, n_copies):
    n = len(srcs)

    def body(*refs):
        for cp in copies_fn(refs[:n], refs[n:2 * n], refs[2 * n + 1], refs[2 * n + 2]):
            cp.start()
        refs[-1][...] = jnp.zeros_like(refs[-1])

    bufs = list(srcs) + list(lands)
    outs = pl.pallas_call(
        body, name=name,
        out_shape=(pltpu.SemaphoreType.DMA((n_copies,)), pltpu.SemaphoreType.DMA((n_copies,)),
                   *[pltpu.HBM(a.shape, a.dtype) for a in bufs], jax.ShapeDtypeStruct((8, 128), F32)),
        in_specs=[_HBM] * (2 * n) + [_ANY],
        out_specs=(_SEM, _SEM, *[_HBM] * (2 * n), pl.BlockSpec(memory_space=pltpu.VMEM)),
        input_output_aliases={i: 2 + i for i in range(2 * n)},
        compiler_params=pltpu.CompilerParams(has_side_effects=_EFFECT),
    )(*[_in_hbm(a) for a in bufs], after)
    return outs[0], outs[1], outs[2:2 + n], outs[2 + n:2 + 2 * n], outs[-1]


def split_wait(name, send, recv, srcs, lands, after, copies_fn, which=None):
    n = len(srcs)

    def body(*refs):
        copies = copies_fn(refs[:n], refs[n:2 * n], refs[2 * n], refs[2 * n + 1], which)
        for cp in copies:
            cp.wait_send()
        for cp in copies:
            cp.wait_recv()

    bufs = list(srcs) + list(lands)
    outs = pl.pallas_call(
        body, name=name, out_shape=tuple(pltpu.HBM(a.shape, a.dtype) for a in bufs),
        in_specs=[_HBM] * (2 * n) + [_SEM, _SEM, _ANY], out_specs=tuple([_HBM] * (2 * n)),
        input_output_aliases={i: i for i in range(2 * n)},
        compiler_params=pltpu.CompilerParams(has_side_effects=_EFFECT),
    )(*bufs, send, recv, after)
    return list(outs[:n]), list(outs[n:])


def _gather_copies(s_refs, l_refs, send, recv, which=None):
    x, y, c, chips = _place()
    me = 2 * x + y
    out = []
    for k in (range(len(s_refs)) if which is None else which):
        s, l = s_refs[k], l_refs[k]
        rows = _rows_of(c, s.shape[0] // 2)
        for j, chip in enumerate(chips):
            out.append(pltpu.make_async_remote_copy(src_ref=s.at[rows], dst_ref=l.at[me, rows], send_sem=send.at[4 * k + j],
                                                    recv_sem=recv.at[4 * k + j], device_id=(*chip, c), device_id_type=MESH))
        out.append(pltpu.make_async_remote_copy(src_ref=s, dst_ref=l.at[me], send_sem=send.at[4 * k + 3],
                                                recv_sem=recv.at[4 * k + 3], device_id=(x, y, 1 - c), device_id_type=MESH))
    return out


def _scatter_copies(s_refs, l_refs, send, recv, which=None):
    x, y, c, chips = _place()
    me = 2 * x + y
    return [pltpu.make_async_remote_copy(src_ref=s_refs[k].at[2 * chip[0] + chip[1]], dst_ref=l_refs[k].at[me],
                                         send_sem=send.at[3 * k + j], recv_sem=recv.at[3 * k + j], device_id=(*chip, c),
                                         device_id_type=MESH)
            for k in (range(len(s_refs)) if which is None else which) for j, chip in enumerate(chips)]


def gather_forward(lands, name):
    n = len(lands)

    def body(*refs):
        l_refs, send, recv = refs[n:2 * n], refs[2 * n], refs[2 * n + 1]
        x, y, c, chips = _place()
        copies = []
        for k in range(n):
            rows = _rows_of(c, lands[k].shape[1] // 2)
            for j, chip in enumerate(chips):
                part = l_refs[k].at[2 * chip[0] + chip[1], rows]
                copies.append(pltpu.make_async_remote_copy(src_ref=part, dst_ref=part, send_sem=send.at[3 * k + j],
                                                           recv_sem=recv.at[3 * k + j], device_id=(x, y, 1 - c),
                                                           device_id_type=MESH))
        for cp in copies:
            cp.start()
        for cp in copies:
            cp.wait()

    return pl.pallas_call(
        body, name=name, in_specs=[_ANY] * n, out_specs=[_ANY] * n,
        out_shape=[jax.ShapeDtypeStruct(a.shape, a.dtype) for a in lands],
        input_output_aliases={k: k for k in range(n)},
        scratch_shapes=[pltpu.SemaphoreType.DMA((3 * n,)), pltpu.SemaphoreType.DMA((3 * n,))],
    )(*lands)


def rs_partials(gs, wire, cidx, tag):
    rs = rs_pair_swap(gs)
    groups = {}
    for k, g in enumerate(gs):
        groups.setdefault((g.shape, jnp.dtype(wire[k]).name), []).append(k)
    ps = [None] * len(gs)
    for gi, ks in enumerate(groups.values()):
        outs = rs_pair_add([gs[k] for k in ks], [rs[k] for k in ks], cidx, wire[ks[0]], "grad_pair_add_%s%d" % (tag, gi))
        for k, o in zip(ks, outs):
            ps[k] = o
    return ps


def rs_finish(items, tag):
    cidx = lax.axis_index("c").astype(jnp.int32).reshape(1)
    groups = {}
    for i, it in enumerate(items):
        groups.setdefault((it[0].shape, len(it), it[0].dtype.name), []).append(i)
    fs = [None] * len(items)
    for gi, ids in enumerate(groups.values()):
        outs = rs_chip_sum([q for i in ids for q in items[i]], len(items[ids[0]]), cidx, "grad_chip_sum_%s%d" % (tag, gi))
        for i, o in zip(ids, outs):
            fs[i] = o
    return rs_pair_gather(fs, "grad_pair_gather_" + tag)


def adamw(w, g, m, v, name):
    shape = w.shape
    if w.ndim == 2:
        block, grid, index = shape, (1,), (lambda i: (0, 0))
    else:
        slab = shape[2:]
        unit = 4 * int(np.prod(slab[:-2] or (1,))) * (-(-slab[-1] // 128) * 128)
        if len(slab) >= 2:
            unit *= -(-slab[-2] // 8) * 8
        k = shape[1]
        tr = k
        if k * unit > 2 ** 21:
            tr = max(t for t in range(8, k, 8) if k % t == 0 and t * unit <= 2 ** 21)
        block, grid = (None, tr) + tuple(slab), (shape[0], k // tr)
        index = lambda l, i: (l, i) + (0,) * len(slab)
        if tr < min(k, 64) and len(slab) == 1:
            tc = max(t for t in range(128, slab[0] + 1, 128) if slab[0] % t == 0 and k * t * 4 <= 2 ** 21)
            block, grid = (None, k, tc), (shape[0], slab[0] // tc)
            index = lambda l, i: (l, 0, i)

    def body(w_ref, g_ref, m_ref, v_ref, d_ref, nm_ref, nv_ref):
        g_ = g_ref[...]
        m_new = ADAM_B1 * m_ref[...] + (1.0 - ADAM_B1) * g_
        v_new = ADAM_B2 * v_ref[...] + (1.0 - ADAM_B2) * (g_ * g_)
        m_hat = m_new / (1.0 - ADAM_B1 ** ADAM_STEP)
        v_hat = v_new / (1.0 - ADAM_B2 ** ADAM_STEP)
        d_ref[...] = -ADAM_LR * (m_hat / (jnp.sqrt(v_hat) + ADAM_EPS) + ADAM_WD * w_ref[...])
        nm_ref[...] = m_new
        nv_ref[...] = v_new

    spec = pl.BlockSpec(block, index)
    return pl.pallas_call(
        body, name=name, grid=grid, in_specs=[spec] * 4, out_specs=[spec] * 3,
        out_shape=[jax.ShapeDtypeStruct(shape, F32)] * 3, compiler_params=_cparams(),
    )(w, g, m, v)


_WEIGHTS = ['meta', 'ffn1_w_gate', 'ffn1_w_up', 'ffn1_w_down', 'ln1_g', 'ln1_b', 'w_in', 'mla_q_norm_g', 'mla_w_uq',
            'mla_kv_norm_g', 'mla_w_ukv', 'mla_w_o', 'conv_w', 'conv_b', 'conv_w_out', 's5_a_re', 's5_a_im', 's5_log_dt',
            's5_b_re', 's5_b_im', 's5_c_re', 's5_c_im', 's5_d', 's5_w_glu', 's5_b_glu', 's5_w_out', 'w_o', 'ln2_g', 'ln2_b',
            'ffn2_w_gate', 'ffn2_w_up', 'ffn2_w_down', 'ln3_g', 'ln3_b']


def _pad_to(flat, n):
    return jnp.concatenate([flat, jnp.zeros((n - flat.shape[0],), flat.dtype)])


def _shard_of(full, axis):
    if axis == 1:
        return _shard_cols(full)
    if axis == 'T':
        return full.T.reshape(N_SHARD, full.shape[1] // N_SHARD, full.shape[0])
    return full.reshape(N_SHARD, full.shape[0] // N_SHARD, full.shape[1])


_FFN_KEY = {'gate': 'wg', 'up': 'wu', 'down': 'wd'}


def _pad_rows(a, axis):
    k = a.shape[axis]
    extra = -k % 32
    if not extra:
        return a
    return jnp.pad(a, [(0, extra) if d == axis else (0, 0) for d in range(a.ndim)])


def _step(env):
    w = {n: env[n] for n in _WEIGHTS}
    mom = {n: env['m_' + n] for n in _WEIGHTS}
    var = {n: env['v_' + n] for n in _WEIGHTS}
    cidx = lax.axis_index("c").astype(jnp.int32).reshape(1)
    chip = 2 * lax.axis_index("x") + lax.axis_index("y")
    big_names = [n for n, _ in _BIG]
    nb = len(big_names)

    kept_t = [n for n, a in _BIG if a == 'T']
    own = {n: (jnp.swapaxes(w[n], 1, 2) if n in kept_t else w[n]) for n in big_names}
    first = [n for n in big_names if n.startswith('ffn1')]
    mix = [n for n in big_names if not n.startswith('ffn')]
    last = [n for n in big_names if n.startswith('ffn2')]
    rest = mix + last
    nm, nr = len(mix), len(mix) + len(last)
    src = lambda n, li: _pad_rows(own[n][li].astype(BF16), 0)
    gathered_first, (conv_w_st, meta_st) = all_gather_shards([src(n, 0) for n in first], [w['conv_w'], w['meta']])
    later = [src(n, 0) for n in rest] + [src(n, 1) for n in big_names]
    lands = [lax.empty((N_SHARD,) + s.shape, BF16) for s in later]
    g_send, g_recv, later_t, lands_t, token = split_start("gather_start", later, lands, gathered_first[0], _gather_copies,
                                                          4 * len(later))

    def weights_of(names, st, li, with_small):
        small = None
        if with_small:
            small = {n: w[n][li] for n in _REPL}
            small['conv_w'] = _nat_cols(conv_w_st[:, li])
        return compute_weights({n: a[:, :own[n].shape[1]] for n, a in zip(names, st)}, small)

    x2d = env['x'][0]
    lp = x2d.shape[0] + X0
    tabs = _rope_tables(lp)
    h = jnp.concatenate([jnp.zeros((PAD, D_MODEL), F32), _nat_cols(meta_st), x2d], axis=0) + token[0, 0]
    W0 = weights_of(first, gathered_first, 0, True)
    ffn1 = ffn_fwd(h, h.astype(BF16), W0['wg1'], W0['wu1'], W0['wd1'], W0['ln1_g'], W0['ln1_b'], lp)
    later_t, lands_t = split_wait("gather0_wait", g_send, g_recv, later_t, lands_t, ffn1[0], _gather_copies, range(nm))
    W0.update(weights_of(mix, gather_forward(lands_t[:nm], "gather0_forward"), 0, False))
    h, hb, sv0 = layer_fwd(None, None, W0, tabs, lp, ffn1=ffn1, ffn2=False)
    later_t, lands_t = split_wait("gather0b_wait", g_send, g_recv, later_t, lands_t, h, _gather_copies, range(nm, nr))
    W0.update(weights_of(last, gather_forward(lands_t[nm:nr], "gather0b_forward"), 0, False))
    h, hb, sv0['sv3'] = ffn_fwd(h, hb, W0['wg2'], W0['wu2'], W0['wd2'], W0['ln3_g'], W0['ln3_b'], lp)
    _, lands_t = split_wait("gather1_wait", g_send, g_recv, later_t, lands_t, h, _gather_copies, range(nr, len(later)))
    W1 = weights_of(big_names, gather_forward(lands_t[nr:], "gather1_forward"), 1, True)
    h, hb, sv1 = layer_fwd(h, hb, W1, tabs, lp)
    tgt = jnp.concatenate([jnp.zeros((X0, D_MODEL), F32), env['loss_target'][0]], axis=0)
    dh, loss_part = loss_head(h, tgt, lp)
    loss = lax.psum(loss_part[0, 0], ("x", "y", "c"))

    def shards(G, names):
        full = None if all(n.startswith('ffn') for n in names) else reference_grads(G, ffn=False)

        def one(n, a):
            if n.startswith('ffn'):
                return G[_FFN_KEY[n.split('_')[-1]] + n[3]]
            if n == 'w_in':
                return full['w_in_t'].reshape(N_SHARD, D_IN // N_SHARD, D_MODEL)
            return _shard_of(full[n], a)

        return [_pad_rows(one(n, a), 1) for n, a in _BIG if n in names]

    def scatter_start(name, ps, after):
        qs = [lax.dynamic_update_slice_in_dim(jnp.zeros_like(p), lax.dynamic_slice_in_dim(p, chip, 1, axis=0), chip, axis=0)
              for p in ps]
        return split_start(name, ps, qs, after, _scatter_copies, 3 * len(ps))

    dh, G1 = layer_bwd(dh, sv1, W1, tabs, lp)
    p1 = rs_partials(shards(G1, big_names), [BF16] * nb, cidx, "b")
    s1_send, s1_recv, p1_t, q1_t, token1 = scatter_start("scatter1_start", p1, dh)
    dh, g3 = ffn_bwd(dh, sv0['sv3'], W0['wg2'], W0['wu2'], W0['wd2'], W0['ln3_g'] + token1[0, 0], lp)
    G0 = dict(wg2=g3['wg'], wu2=g3['wu'], wd2=g3['wd'])
    p0l = rs_partials(shards(G0, last), [BF16] * len(last), cidx, "c")
    sl_send, sl_recv, p0l_t, q0l_t, token0l = scatter_start("scatter0b_start", p0l, dh)
    dh, Gm = layer_bwd(dh, sv0, dict(W0, ln2_g=W0['ln2_g'] + token0l[0, 0]), tabs, lp, ffn1=False, ffn2=False)
    G0.update(Gm, ln3_g=g3['ln_g'], ln3_b=g3['ln_b'])
    p0m = rs_partials(shards(G0, mix), [BF16] * nm, cidx, "d")
    sm_send, sm_recv, p0m_t, q0m_t, token0m = scatter_start("scatter0_start", p0m, dh)
    dh, g1 = ffn_bwd(dh, sv0['sv1'], W0['wg1'], W0['wu1'], W0['wd1'], W0['ln1_g'] + token0m[0, 0], lp)
    G0.update(wg1=g1['wg'], wu1=g1['wu'], wd1=g1['wd'], ln1_g=g1['ln_g'], ln1_b=g1['ln_b'])
    _, q1 = split_wait("scatter1_wait", s1_send, s1_recv, p1_t, q1_t, dh, _scatter_copies)
    _, q0_last = split_wait("scatter0b_wait", sl_send, sl_recv, p0l_t, q0l_t, dh, _scatter_copies)
    _, q0_mix = split_wait("scatter0_wait", sm_send, sm_recv, p0m_t, q0m_t, dh, _scatter_copies)
    q0_rest = list(q0_mix) + list(q0_last)
    full = [reference_grads(G0, ffn=False), reference_grads(G1, ffn=False)]

    s_parts = [jnp.stack([full[li][n] for li in range(DEPTH)]).reshape(-1) for n in _REPL + ['conv_w']]
    s_parts.append(dh[PAD:X0].reshape(-1))
    s_sizes = [int(p.shape[0]) for p in s_parts]
    s_rows = -(-sum(s_sizes) // (16 * LANES)) * 16
    g_small = _pad_to(jnp.concatenate(s_parts), s_rows * LANES).reshape(1, s_rows, LANES)
    g_small = jnp.broadcast_to(g_small, (N_SHARD, s_rows, LANES))
    p_last = rs_partials(shards(G0, first) + [g_small], [BF16] * len(first) + [F32], cidx, "a")
    z_send, z_recv, pz_t, qz_t, token_z = scatter_start("scatter_last_start", p_last, dh)

    def step_weights(names, grad):
        out = {}
        for n in names:
            if n in kept_t:
                res = adamw(own[n], grad[n], jnp.swapaxes(mom[n], 1, 2), jnp.swapaxes(var[n], 1, 2), "adamw_" + n)
                out[n] = [jnp.swapaxes(t, 1, 2) for t in [grad[n]] + list(res)]
            else:
                out[n] = [grad[n]] + list(adamw(w[n], grad[n], mom[n], var[n], "adamw_" + n))
        return out

    q1 = dict(zip(big_names, q1))
    q0 = dict(zip(rest, q0_rest))
    q0[rest[0]] = q0[rest[0]] + token_z[0, 0].astype(BF16)
    red = rs_finish([[q0[n], q1[n]] for n in rest], "a")
    done = step_weights(rest, {n: r[:, :own[n].shape[1]] for n, r in zip(rest, red)})
    all_done = jnp.stack([done[n][3][(0,) * done[n][3].ndim] for n in rest])
    _, q_last = split_wait("scatter_last_wait", z_send, z_recv, pz_t, qz_t, all_done, _scatter_copies)
    red = rs_finish([[q, q1[n]] for n, q in zip(first, q_last)] + [[q_last[-1]]], "b")
    f_small = red[-1].reshape(-1)

    grad = {n: r[:, :own[n].shape[1]] for n, r in zip(first, red)}
    off = 0
    for n, sz in zip(_REPL + ['conv_w', 'meta'], s_sizes):
        grad[n] = f_small[off:off + sz]
        off += sz
    for n in _REPL:
        grad[n] = grad[n].reshape(w[n].shape)
    cw = grad['conv_w'].reshape(DEPTH, 3, MIX)
    grad['conv_w'] = lax.dynamic_slice_in_dim(cw, chip * (MIX // N_SHARD), MIX // N_SHARD, axis=2)
    gm = grad['meta'].reshape(N_META, D_MODEL)
    grad['meta'] = lax.dynamic_slice_in_dim(gm, chip * (D_MODEL // N_SHARD), D_MODEL // N_SHARD, axis=1)

    done.update(step_weights([n for n in _WEIGHTS if n not in done], grad))
    return (loss, dh[X0:][None], *[done[n][k] for k in range(4) for n in _WEIGHTS])


def kernel(x, meta, ffn1_w_gate, ffn1_w_up, ffn1_w_down, ln1_g, ln1_b, w_in, mla_q_norm_g, mla_w_uq, mla_kv_norm_g, mla_w_ukv, mla_w_o, conv_w, conv_b, conv_w_out, s5_a_re, s5_a_im, s5_log_dt, s5_b_re, s5_b_im, s5_c_re, s5_c_im, s5_d, s5_w_glu, s5_b_glu, s5_w_out, w_o, ln2_g, ln2_b, ffn2_w_gate, ffn2_w_up, ffn2_w_down, ln3_g, ln3_b, loss_target, m_meta, m_ffn1_w_gate, m_ffn1_w_up, m_ffn1_w_down, m_ln1_g, m_ln1_b, m_w_in, m_mla_q_norm_g, m_mla_w_uq, m_mla_kv_norm_g, m_mla_w_ukv, m_mla_w_o, m_conv_w, m_conv_b, m_conv_w_out, m_s5_a_re, m_s5_a_im, m_s5_log_dt, m_s5_b_re, m_s5_b_im, m_s5_c_re, m_s5_c_im, m_s5_d, m_s5_w_glu, m_s5_b_glu, m_s5_w_out, m_w_o, m_ln2_g, m_ln2_b, m_ffn2_w_gate, m_ffn2_w_up, m_ffn2_w_down, m_ln3_g, m_ln3_b, v_meta, v_ffn1_w_gate, v_ffn1_w_up, v_ffn1_w_down, v_ln1_g, v_ln1_b, v_w_in, v_mla_q_norm_g, v_mla_w_uq, v_mla_kv_norm_g, v_mla_w_ukv, v_mla_w_o, v_conv_w, v_conv_b, v_conv_w_out, v_s5_a_re, v_s5_a_im, v_s5_log_dt, v_s5_b_re, v_s5_b_im, v_s5_c_re, v_s5_c_im, v_s5_d, v_s5_w_glu, v_s5_b_glu, v_s5_w_out, v_w_o, v_ln2_g, v_ln2_b, v_ffn2_w_gate, v_ffn2_w_up, v_ffn2_w_down, v_ln3_g, v_ln3_b):
    return _step(dict(locals()))
```

```python
import functools
import math

import numpy as np
import jax
import jax.numpy as jnp
from jax import lax
from jax.experimental import pallas as pl
from jax.experimental.pallas import tpu as pltpu

F32 = jnp.float32
BF16 = jnp.bfloat16

D_MODEL = 1024
DEPTH = 2
N_META = 16
PAD = 112
X0 = PAD + N_META
N_HEADS = 8
D_NOPE = 64
D_ROPE = 32
D_V = 64
Q_RANK = 384
KV_RANK = 256
MIX = 512
S5_GROUPS = 32
S5_GROUP = 16
S5_STATE = 64
S5_LANES = S5_GROUPS * S5_STATE
D_FF = 2816
N_SHARD = 4
FF_SHARD = D_FF // N_SHARD
D_IN = 5792
P_IN = 6144
ALPHA = (2.0 * DEPTH) ** 0.25
LN_EPS = 1e-5
RMS_EPS = 1e-6
ATT_SCALE = (D_NOPE + D_ROPE) ** -0.5
ROPE_BASE = 10000.0
ADAM_LR, ADAM_B1, ADAM_B2, ADAM_EPS, ADAM_WD, ADAM_STEP = 0.001, 0.9, 0.999, 1e-08, 0.01, 10
SCAN_CHUNK = 64
VMEM_LIMIT = 52 * 2 ** 20
WGRAD = BF16
MESH = pl.DeviceIdType.MESH


def _cparams(**kw):
    return pltpu.CompilerParams(vmem_limit_bytes=VMEM_LIMIT, **kw)


def _tile(n):
    if n <= 1088:
        return n
    for t in (1088, 1024, 544, 512, 272, 256, 128):
        if n % t == 0:
            return t
    return n


def _row_tile(lp):
    for t in (544, 272, 128):
        if lp % t == 0:
            return t
    return lp


def _ffn_tile(lp):
    return 1088 if lp % 1088 == 0 else _row_tile(lp)


def _sigmoid(x):
    return 1.0 / (1.0 + jnp.exp(-x))


_GELU_C = math.sqrt(2.0 / math.pi)


def _gelu(x):
    return 0.5 * x * (1.0 + jnp.tanh(_GELU_C * (x + 0.044715 * x * x * x)))


def _gelu_grad(x):
    t = jnp.tanh(_GELU_C * (x + 0.044715 * x * x * x))
    return 0.5 * (1.0 + t) + 0.5 * x * (1.0 - t * t) * _GELU_C * (1.0 + 3.0 * 0.044715 * x * x)


def _dot(a, b, ca, cb, precision=None):
    return lax.dot_general(a, b, (((ca,), (cb,)), ((), ())), preferred_element_type=F32, precision=precision)


def matmul(a, b, *, name, ta=False, tb=False, ab='n', bb='n', res=None, res_scale=1.0, scale=1.0, out_dtype=F32):
    if ta:
        _, K, M = a.shape
    else:
        _, M, K = a.shape
    if tb:
        _, N, K2 = b.shape
    else:
        _, K2, N = b.shape
    assert K == K2, (a.shape, b.shape)
    n_out = max(a.shape[0] if ab == 'o' else 1, b.shape[0] if bb == 'o' else 1)
    n_red = max(a.shape[0] if ab == 'r' else 1, b.shape[0] if bb == 'r' else 1)
    tm, tn = _tile(M), _tile(N)
    tk = K if K <= 2304 else _tile(K)
    nkt = K // tk
    n_steps = n_red * nkt

    def bsel(mode, o, r):
        if mode == 'o':
            return o
        if mode == 'r':
            return r // nkt if nkt > 1 else r
        return 0

    def ksel(r):
        if nkt == 1:
            return 0
        return r % nkt if n_red > 1 else r

    a_map = (lambda o, i, j, r: (bsel(ab, o, r), ksel(r), i)) if ta else (lambda o, i, j, r: (bsel(ab, o, r), i, ksel(r)))
    b_map = (lambda o, i, j, r: (bsel(bb, o, r), j, ksel(r))) if tb else (lambda o, i, j, r: (bsel(bb, o, r), ksel(r), j))
    o_map = lambda o, i, j, r: (o, i, j)
    in_specs = [pl.BlockSpec((None, tk, tm) if ta else (None, tm, tk), a_map),
                pl.BlockSpec((None, tn, tk) if tb else (None, tk, tn), b_map)]
    operands = [a, b]
    if res is not None:
        in_specs.append(pl.BlockSpec((None, tm, tn), o_map))
        operands.append(res)
    has_res = res is not None

    def body(*refs):
        a_ref, b_ref = refs[0], refs[1]
        res_ref = refs[2] if has_res else None
        o_ref = refs[3] if has_res else refs[2]
        part = _dot(a_ref[...].astype(BF16), b_ref[...].astype(BF16), 0 if ta else 1, 1 if tb else 0)

        def finish(acc):
            v = acc if scale == 1.0 else acc * scale
            if has_res:
                v = v + res_scale * res_ref[...].astype(F32)
            o_ref[...] = v.astype(o_ref.dtype)

        if n_steps == 1:
            finish(part)
        else:
            acc_ref = refs[-1]
            r = pl.program_id(3)

            @pl.when(r == 0)
            def _():
                acc_ref[...] = part

            @pl.when(r > 0)
            def _():
                acc_ref[...] += part

            @pl.when(r == n_steps - 1)
            def _():
                finish(acc_ref[...])

    return pl.pallas_call(
        body, name=name,
        grid=(n_out, M // tm, N // tn, n_steps),
        in_specs=in_specs,
        out_specs=pl.BlockSpec((None, tm, tn), o_map),
        out_shape=jax.ShapeDtypeStruct((n_out, M, N), out_dtype),
        scratch_shapes=[pltpu.VMEM((tm, tn), F32)] if n_steps > 1 else [],
        compiler_params=_cparams(),
    )(*operands)


def rowwise(fn, rows, pars, outs, accs=(), *, name, lp):
    tm = _row_tile(lp)
    n_rows, n_pars, n_outs, n_accs = len(rows), len(pars), len(outs), len(accs)
    outs = [tuple(o) + (o[0], 0) if len(o) == 2 else tuple(o) for o in outs]
    in_specs = [pl.BlockSpec((tm, w), functools.partial(lambda i, cb: (i, cb), cb=cb)) for _, w, cb in rows]
    in_specs += [pl.BlockSpec(p.shape, functools.partial(lambda i, nd: (0,) * nd, nd=p.ndim)) for p in pars]
    out_specs = [pl.BlockSpec((tm, w), functools.partial(lambda i, cb: (i, cb), cb=cb)) for w, _, _, cb in outs]
    out_specs += [pl.BlockSpec(s, functools.partial(lambda i, nd: (0,) * nd, nd=len(s))) for s, _ in accs]
    out_shape = [jax.ShapeDtypeStruct((lp, total), dt) for _, dt, total, _ in outs]
    out_shape += [jax.ShapeDtypeStruct(s, dt) for s, dt in accs]

    def body(*refs):
        i = pl.program_id(0)
        rv = [r[...] for r in refs[:n_rows]]
        pv = [r[...] for r in refs[n_rows:n_rows + n_pars]]
        o_refs = refs[n_rows + n_pars:n_rows + n_pars + n_outs]
        a_refs = refs[n_rows + n_pars + n_outs:]
        ov, av = fn(i * tm, rv, pv)
        for r, v in zip(o_refs, ov):
            r[...] = v.astype(r.dtype)
        if n_accs:
            @pl.when(i == 0)
            def _():
                for r, v in zip(a_refs, av):
                    r[...] = v.astype(r.dtype)

            @pl.when(i > 0)
            def _():
                for r, v in zip(a_refs, av):
                    r[...] += v.astype(r.dtype)

    res = pl.pallas_call(
        body, name=name, grid=(lp // tm,), in_specs=in_specs, out_specs=out_specs, out_shape=out_shape,
        compiler_params=_cparams(),
    )(*[r[0] for r in rows], *pars)
    return res


def _row_mask(row0, shape):
    return (row0 + lax.broadcasted_iota(jnp.int32, shape, 0)) >= PAD


def ffn_up(hb, wg, wu, lp):
    tm = _ffn_tile(lp)

    def body(h_ref, wg_ref, wu_ref, ab_ref, hid_ref):
        h = h_ref[...]
        a = _dot(h, wg_ref[...], 1, 1)
        b = _dot(h, wu_ref[...], 1, 1)
        ab_ref[0] = a.astype(BF16)
        ab_ref[1] = b.astype(BF16)
        hid_ref[...] = (a * _sigmoid(a) * b).astype(BF16)

    wspec = pl.BlockSpec((None, FF_SHARD, D_MODEL), lambda j, i: (j, 0, 0))
    return pl.pallas_call(
        body, name="ffn_up", grid=(N_SHARD, lp // tm),
        in_specs=[pl.BlockSpec((tm, D_MODEL), lambda j, i: (i, 0)), wspec, wspec],
        out_specs=[pl.BlockSpec((None, 2, tm, FF_SHARD), lambda j, i: (j, 0, i, 0)),
                   pl.BlockSpec((None, tm, FF_SHARD), lambda j, i: (j, i, 0))],
        out_shape=[jax.ShapeDtypeStruct((N_SHARD, 2, lp, FF_SHARD), BF16),
                   jax.ShapeDtypeStruct((N_SHARD, lp, FF_SHARD), BF16)],
        compiler_params=_cparams(),
    )(hb, wg, wu)


def _layer_norm(z, g, b):
    mu = jnp.mean(z, axis=-1, keepdims=True)
    zc = z - mu
    var = jnp.mean(zc * zc, axis=-1, keepdims=True)
    return zc * lax.rsqrt(var + LN_EPS) * g + b


def mm_res_ln(a, w, res, g, b, *, scale, name, lp):
    n_red, _, K = a.shape
    tm = _row_tile(lp)

    def body(a_ref, w_ref, res_ref, g_ref, b_ref, z_ref, h_ref, hb_ref, acc_ref):
        r = pl.program_id(1)
        part = _dot(a_ref[...].astype(BF16), w_ref[...], 1, 0)

        @pl.when(r == 0)
        def _():
            acc_ref[...] = part

        @pl.when(r > 0)
        def _():
            acc_ref[...] += part

        @pl.when(r == n_red - 1)
        def _():
            z = ALPHA * res_ref[...] + scale * acc_ref[...]
            z_ref[...] = z
            hn = _layer_norm(z, g_ref[...], b_ref[...])
            h_ref[...] = hn
            hb_ref[...] = hn.astype(BF16)

    row = pl.BlockSpec((tm, D_MODEL), lambda i, r: (i, 0))
    par = pl.BlockSpec((1, D_MODEL), lambda i, r: (0, 0))
    return pl.pallas_call(
        body, name=name, grid=(lp // tm, n_red),
        in_specs=[pl.BlockSpec((None, tm, K), lambda i, r: (r, i, 0)),
                  pl.BlockSpec((None, K, D_MODEL), lambda i, r: (r, 0, 0)), row, par, par],
        out_specs=[row, row, row],
        out_shape=[jax.ShapeDtypeStruct((lp, D_MODEL), F32), jax.ShapeDtypeStruct((lp, D_MODEL), F32),
                   jax.ShapeDtypeStruct((lp, D_MODEL), BF16)],
        scratch_shapes=[pltpu.VMEM((tm, D_MODEL), F32)],
        compiler_params=_cparams(),
    )(a, w, res, g, b)


def ln_bwd(dh, z, g, *, fscale, name, lp):
    def fn(row0, rv, pv):
        dh_, z_ = rv
        g_, = pv
        mu = jnp.mean(z_, axis=-1, keepdims=True)
        zc = z_ - mu
        rstd = lax.rsqrt(jnp.mean(zc * zc, axis=-1, keepdims=True) + LN_EPS)
        xh = zc * rstd
        dxh = dh_ * g_
        m1 = jnp.mean(dxh, axis=-1, keepdims=True)
        m2 = jnp.mean(dxh * xh, axis=-1, keepdims=True)
        dz = rstd * (dxh - m1 - xh * m2)
        return ((dz, fscale * dz),
                (jnp.sum(dh_ * xh, axis=0, keepdims=True), jnp.sum(dh_, axis=0, keepdims=True)))

    return rowwise(fn, [(dh, D_MODEL, 0), (z, D_MODEL, 0)], [g], [(D_MODEL, F32), (D_MODEL, BF16)],
                   [((1, D_MODEL), F32), ((1, D_MODEL), F32)], name=name, lp=lp)


def ffn_down_bwd(dfb, wd, ab, lp):
    tm = _ffn_tile(lp)

    def body(df_ref, w_ref, ab_ref, da_ref, db_ref):
        dhid = _dot(df_ref[...], w_ref[...], 1, 1)
        a = ab_ref[0].astype(F32)
        b = ab_ref[1].astype(F32)
        sg = _sigmoid(a)
        da_ref[...] = (dhid * b * (sg * (1.0 + a * (1.0 - sg)))).astype(BF16)
        db_ref[...] = (dhid * (a * sg)).astype(BF16)

    ospec = pl.BlockSpec((None, tm, FF_SHARD), lambda j, i: (j, i, 0))
    return pl.pallas_call(
        body, name="ffn_down_bwd", grid=(N_SHARD, lp // tm),
        in_specs=[pl.BlockSpec((tm, D_MODEL), lambda j, i: (i, 0)),
                  pl.BlockSpec((None, FF_SHARD, D_MODEL), lambda j, i: (j, 0, 0)),
                  pl.BlockSpec((None, 2, tm, FF_SHARD), lambda j, i: (j, 0, i, 0))],
        out_specs=[ospec, ospec],
        out_shape=[jax.ShapeDtypeStruct((N_SHARD, lp, FF_SHARD), BF16)] * 2,
        compiler_params=_cparams(),
    )(dfb, wd, ab)


def ffn_dx(da, db, wg, wu, dz, lp):
    tm = _ffn_tile(lp)

    def body(da_ref, db_ref, wg_ref, wu_ref, dz_ref, o_ref, acc_ref):
        j = pl.program_id(1)
        part = _dot(da_ref[...], wg_ref[...], 1, 0) + _dot(db_ref[...], wu_ref[...], 1, 0)

        @pl.when(j == 0)
        def _():
            acc_ref[...] = part

        @pl.when(j > 0)
        def _():
            acc_ref[...] += part

        @pl.when(j == N_SHARD - 1)
        def _():
            o_ref[...] = acc_ref[...] + ALPHA * dz_ref[...]

    aspec = pl.BlockSpec((None, tm, FF_SHARD), lambda i, j: (j, i, 0))
    wspec = pl.BlockSpec((None, FF_SHARD, D_MODEL), lambda i, j: (j, 0, 0))
    row = pl.BlockSpec((tm, D_MODEL), lambda i, j: (i, 0))
    return pl.pallas_call(
        body, name="ffn_dx", grid=(lp // tm, N_SHARD), in_specs=[aspec, aspec, wspec, wspec, row], out_specs=row,
        out_shape=jax.ShapeDtypeStruct((lp, D_MODEL), F32), scratch_shapes=[pltpu.VMEM((tm, D_MODEL), F32)],
        compiler_params=_cparams(),
    )(da, db, wg, wu, dz)


def ffn_fwd(h, hb, wg, wu, wd, g, b, lp):
    ab, hid = ffn_up(hb, wg, wu, lp)
    z, hn, hnb = mm_res_ln(hid, wd, h, g, b, scale=0.5, name="ffn_down_ln", lp=lp)
    return hn, hnb, dict(hb=hb, ab=ab, hid=hid, z=z)


def ffn_bwd(dh, sv, wg, wu, wd, g, lp):
    dz, dfb, dg, db = ln_bwd(dh, sv['z'], g, fscale=0.5, name="ffn_ln_bwd", lp=lp)
    da, dbb = ffn_down_bwd(dfb, wd, sv['ab'], lp)
    d_wd = matmul(sv['hid'], dfb[None], ta=True, ab='o', out_dtype=WGRAD, name="ffn_dwd")
    d_wg = matmul(da, sv['hb'][None], ta=True, ab='o', out_dtype=WGRAD, name="ffn_dwg")
    d_wu = matmul(dbb, sv['hb'][None], ta=True, ab='o', out_dtype=WGRAD, name="ffn_dwu")
    dh_in = ffn_dx(da, dbb, wg, wu, dz, lp)
    return dh_in, dict(wg=d_wg, wu=d_wu, wd=d_wd, ln_g=dg, ln_b=db)


def _rope_tables(lp):
    pos = np.arange(lp, dtype=np.float32) - PAD
    inv = ROPE_BASE ** (-np.arange(0, D_ROPE, 2, dtype=np.float32) / D_ROPE)
    ang = pos[:, None] * inv[None, :]
    cos = np.concatenate([np.cos(ang), np.cos(ang)], axis=1).astype(np.float32)
    sin = np.concatenate([np.sin(ang), np.sin(ang)], axis=1).astype(np.float32)
    rot = np.zeros((D_ROPE, D_ROPE), np.float32)
    half = D_ROPE // 2
    for j in range(half):
        rot[j + half, j] = -1.0
        rot[j, j + half] = 1.0
    d_qk = D_NOPE + D_ROPE
    cos_qk = np.concatenate([np.ones((lp, D_NOPE), np.float32), cos], axis=1)
    sin_qk = np.concatenate([np.zeros((lp, D_NOPE), np.float32), sin], axis=1)
    rot_qk = np.zeros((d_qk, d_qk), np.float32)
    rot_qk[D_NOPE:, D_NOPE:] = rot
    place = np.zeros((D_ROPE, d_qk), np.float32)
    place[:, D_NOPE:] = np.eye(D_ROPE, dtype=np.float32)
    return tuple(jnp.asarray(t) for t in (cos, sin, rot, cos_qk, sin_qk, rot_qk, place))


def _rot(x, rot):
    return _dot(x, rot, 1, 0, precision=lax.Precision.HIGHEST)


def _rms(x, g):
    r = lax.rsqrt(jnp.mean(x * x, axis=-1, keepdims=True) + RMS_EPS)
    return x * r * g


def mla_prep(proj, cos, sin, rot, qg, kvg, lp):
    def fn(row0, rv, pv):
        cq, krb, ckv, c, s = rv
        qg_, kvg_, rot_ = pv
        kr = krb[:, :D_ROPE]
        return ((_rms(cq, qg_), _rms(ckv, kvg_), kr * c + _rot(kr, rot_) * s), ())

    return rowwise(fn, [(proj, Q_RANK, 0), (proj, 128, 3), (proj, KV_RANK, 2), (cos, D_ROPE, 0), (sin, D_ROPE, 0)],
                   [qg, kvg, rot], [(Q_RANK, BF16), (KV_RANK, BF16), (D_ROPE, BF16)], name="mla_prep", lp=lp)


def mla_heads(cqn, ckvn, kr, cos_qk, sin_qk, rot_qk, place, wq, wk, wv, lp):
    tm = _row_tile(lp)

    def body(cq_ref, ckv_ref, kr_ref, c_ref, s_ref, rot_ref, place_ref, wq_ref, wk_ref, wv_ref, q_ref, k_ref, v_ref):
        cq = cq_ref[...]
        ckv = ckv_ref[...]
        kr_placed = _dot(kr_ref[...], place_ref[...].astype(BF16), 1, 0)
        for h in range(N_HEADS):
            q = _dot(cq, wq_ref[h], 1, 0)
            q_ref[h] = (q * c_ref[...] + _rot(q, rot_ref[...]) * s_ref[...]).astype(BF16)
            k_ref[h] = (_dot(ckv, wk_ref[h], 1, 0) + kr_placed).astype(BF16)
            v_ref[h] = _dot(ckv, wv_ref[h], 1, 0).astype(BF16)

    def row(w):
        return pl.BlockSpec((tm, w), lambda i: (i, 0))

    def whole(a):
        return pl.BlockSpec(a.shape, functools.partial(lambda i, nd: (0,) * nd, nd=a.ndim))

    def ospec(n):
        return pl.BlockSpec((N_HEADS, tm, n), lambda i: (0, i, 0))

    return pl.pallas_call(
        body, name="mla_heads", grid=(lp // tm,),
        in_specs=[row(Q_RANK), row(KV_RANK), row(D_ROPE), row(D_QK), row(D_QK), whole(rot_qk), whole(place),
                  whole(wq), whole(wk), whole(wv)],
        out_specs=[ospec(D_QK), ospec(D_QK), ospec(D_V)],
        out_shape=[jax.ShapeDtypeStruct((N_HEADS, lp, D_QK), BF16), jax.ShapeDtypeStruct((N_HEADS, lp, D_QK), BF16),
                   jax.ShapeDtypeStruct((N_HEADS, lp, D_V), BF16)],
        compiler_params=_cparams(),
    )(cqn, ckvn, kr, cos_qk, sin_qk, rot_qk, place, wq, wk, wv)


D_QK = D_NOPE + D_ROPE


def _att_probs(q, k, row0, tq, lp):
    s = _dot(q, k, 1, 1) * ATT_SCALE
    qi = row0 + lax.broadcasted_iota(jnp.int32, (tq, lp), 0)
    ki = lax.broadcasted_iota(jnp.int32, (tq, lp), 1)
    s = jnp.where((ki <= qi) & (ki >= PAD), s, -1e30)
    p = jnp.exp(s - jnp.max(s, axis=-1, keepdims=True))
    return p / jnp.sum(p, axis=-1, keepdims=True)


def _att_spec(lp, n):
    return pl.BlockSpec((None, lp, n), lambda h: (h, 0, 0))


def _att_tiles(lp):
    tiles, r = [(0, X0)], X0
    while r < lp:
        tiles.append((r, 256))
        r += 256
    assert r == lp
    return tiles


def attn_fwd(q, k, v, lp):
    def body(q_ref, k_ref, v_ref, o_ref):
        for r0, rows in _att_tiles(lp):
            ke, rq = r0 + rows, slice(r0, r0 + rows)
            p = _att_probs(q_ref[rq, :], k_ref[0:ke, :], r0, rows, ke)
            o_ref[rq, :] = _dot(p.astype(BF16), v_ref[0:ke, :], 1, 0).astype(BF16)

    return pl.pallas_call(
        body, name="attn_fwd", grid=(N_HEADS,),
        in_specs=[_att_spec(lp, D_QK), _att_spec(lp, D_QK), _att_spec(lp, D_V)],
        out_specs=_att_spec(lp, D_V), out_shape=jax.ShapeDtypeStruct((N_HEADS, lp, D_V), BF16),
        compiler_params=_cparams(),
    )(q, k, v)


def attn_bwd(q, k, v, do, lp):
    def body(q_ref, k_ref, v_ref, do_ref, dq_ref, dk_ref, dv_ref):
        dk_ref[...] = jnp.zeros_like(dk_ref)
        dv_ref[...] = jnp.zeros_like(dv_ref)
        for r0, rows in _att_tiles(lp):
            ke, rq = r0 + rows, slice(r0, r0 + rows)
            q_, do_, k_, v_ = q_ref[rq, :], do_ref[rq, :], k_ref[0:ke, :], v_ref[0:ke, :]
            p = _att_probs(q_, k_, r0, rows, ke)
            dp = _dot(do_, v_, 1, 1)
            delta = jnp.sum(p * dp, axis=-1, keepdims=True)
            ds = (p * (dp - delta) * ATT_SCALE).astype(BF16)
            dq_ref[rq, :] = _dot(ds, k_, 1, 0)
            dk_ref[0:ke, :] += _dot(ds, q_, 0, 0)
            dv_ref[0:ke, :] += _dot(p.astype(BF16), do_, 0, 0)

    qk, vv = _att_spec(lp, D_QK), _att_spec(lp, D_V)
    return pl.pallas_call(
        body, name="attn_bwd", grid=(N_HEADS,), in_specs=[qk, qk, vv, vv], out_specs=[qk, qk, vv],
        out_shape=[jax.ShapeDtypeStruct((N_HEADS, lp, D_QK), F32), jax.ShapeDtypeStruct((N_HEADS, lp, D_QK), F32),
                   jax.ShapeDtypeStruct((N_HEADS, lp, D_V), F32)],
        compiler_params=_cparams(),
    )(q, k, v, do)


def mla_heads_bwd(dq, dk, dv, cos_qk, sin_qk, rot_qk, place, wq, wk, wv, lp):
    tm = _row_tile(lp)

    def body(dq_ref, dk_ref, dv_ref, c_ref, s_ref, rot_ref, place_ref, wq_ref, wk_ref, wv_ref,
             dcq_ref, dckv_ref, dqp_ref, dkr_ref):
        dcq = jnp.zeros(dcq_ref.shape, F32)
        dckv = jnp.zeros(dckv_ref.shape, F32)
        dkr = jnp.zeros(dkr_ref.shape, F32)
        for h in range(N_HEADS):
            g = dq_ref[h]
            dqp = (g * c_ref[...] - _rot(g * s_ref[...], rot_ref[...])).astype(BF16)
            dqp_ref[h] = dqp
            dcq = dcq + _dot(dqp, wq_ref[h], 1, 1)
            dk_ = dk_ref[h]
            dckv = dckv + _dot(dk_.astype(BF16), wk_ref[h], 1, 1) + _dot(dv_ref[h].astype(BF16), wv_ref[h], 1, 1)
            dkr = dkr + _dot(dk_, place_ref[...], 1, 1, precision=lax.Precision.HIGHEST)
        dcq_ref[...] = dcq
        dckv_ref[...] = dckv
        dkr_ref[...] = dkr

    def hspec(n):
        return pl.BlockSpec((N_HEADS, tm, n), lambda i: (0, i, 0))

    def row(w):
        return pl.BlockSpec((tm, w), lambda i: (i, 0))

    def whole(a):
        return pl.BlockSpec(a.shape, functools.partial(lambda i, nd: (0,) * nd, nd=a.ndim))

    return pl.pallas_call(
        body, name="mla_heads_bwd", grid=(lp // tm,),
        in_specs=[hspec(D_QK), hspec(D_QK), hspec(D_V), row(D_QK), row(D_QK), whole(rot_qk), whole(place),
                  whole(wq), whole(wk), whole(wv)],
        out_specs=[row(Q_RANK), row(KV_RANK), hspec(D_QK), row(D_ROPE)],
        out_shape=[jax.ShapeDtypeStruct((lp, Q_RANK), F32), jax.ShapeDtypeStruct((lp, KV_RANK), F32),
                   jax.ShapeDtypeStruct((N_HEADS, lp, D_QK), BF16), jax.ShapeDtypeStruct((lp, D_ROPE), F32)],
        compiler_params=_cparams(),
    )(dq, dk, dv, cos_qk, sin_qk, rot_qk, place, wq, wk, wv)


def heads_out(o, wo, lp):
    tm = _row_tile(lp)

    def body(o_ref, w_ref, y_ref):
        acc = _dot(o_ref[0], w_ref[0], 1, 0)
        for h in range(1, N_HEADS):
            acc = acc + _dot(o_ref[h], w_ref[h], 1, 0)
        y_ref[...] = acc

    return pl.pallas_call(
        body, name="mla_out", grid=(lp // tm,),
        in_specs=[pl.BlockSpec((N_HEADS, tm, D_V), lambda i: (0, i, 0)), pl.BlockSpec(wo.shape, lambda i: (0, 0, 0))],
        out_specs=pl.BlockSpec((tm, D_MODEL), lambda i: (i, 0)), out_shape=jax.ShapeDtypeStruct((lp, D_MODEL), F32),
        compiler_params=_cparams(),
    )(o, wo)


def heads_out_dx(dy, wo, lp):
    tm = _row_tile(lp)

    def body(dy_ref, w_ref, do_ref):
        dy_ = dy_ref[...]
        for h in range(N_HEADS):
            do_ref[h] = _dot(dy_, w_ref[h], 1, 1).astype(BF16)

    return pl.pallas_call(
        body, name="mla_out_dx", grid=(lp // tm,),
        in_specs=[pl.BlockSpec((tm, D_MODEL), lambda i: (i, 0)), pl.BlockSpec(wo.shape, lambda i: (0, 0, 0))],
        out_specs=pl.BlockSpec((N_HEADS, tm, D_V), lambda i: (0, i, 0)),
        out_shape=jax.ShapeDtypeStruct((N_HEADS, lp, D_V), BF16), compiler_params=_cparams(),
    )(dy, wo)


def _rms_bwd(dy, x, g):
    r = lax.rsqrt(jnp.mean(x * x, axis=-1, keepdims=True) + RMS_EPS)
    n = x * r
    dn = dy * g
    dx = r * (dn - n * jnp.mean(dn * n, axis=-1, keepdims=True))
    return dx, jnp.sum(dy * n, axis=0, keepdims=True)


def mla_prep_bwd(dcq, dckv, dkr, proj, cos, sin, rot, qg, kvg, lp):
    def fn(row0, rv, pv):
        dcq_, dckv_, dkr_, cq, ckv, c, s = rv
        qg_, kvg_, rot_ = pv
        dxq, dgq = _rms_bwd(dcq_, cq, qg_)
        dxkv, dgkv = _rms_bwd(dckv_, ckv, kvg_)
        dkr_raw = dkr_ * c - _rot(dkr_ * s, rot_)
        return ((dxq, dxkv, dkr_raw), (dgq, dgkv))

    return rowwise(fn, [(dcq, Q_RANK, 0), (dckv, KV_RANK, 0), (dkr, D_ROPE, 0), (proj, Q_RANK, 0), (proj, KV_RANK, 2),
                        (cos, D_ROPE, 0), (sin, D_ROPE, 0)], [qg, kvg, rot],
                   [(Q_RANK, BF16), (KV_RANK, BF16), (D_ROPE, BF16)], [((1, Q_RANK), F32), ((1, KV_RANK), F32)],
                   name="mla_prep_bwd", lp=lp)


def _shift_down(x, d, rows):
    return jnp.where(rows >= d, pltpu.roll(x, d, 0), 0.0)


def _shift_up(x, d, rows, n):
    return jnp.where(rows < n - d, pltpu.roll(x, n - d, 0), 0.0)


_CONV_W = 128
_XB, _BG, _CG = 1024 // _CONV_W, 1536 // _CONV_W, 2048 // _CONV_W


def _conv_specs(lp):
    def pspec(base):
        return pl.BlockSpec((lp, _CONV_W), functools.partial(lambda c, base: (0, base + c), base=base))

    col = pl.BlockSpec((lp, _CONV_W), lambda c: (0, c))
    wspec = pl.BlockSpec((3, _CONV_W), lambda c: (0, c))
    bspec = pl.BlockSpec((1, _CONV_W), lambda c: (0, c))
    return pspec, col, wspec, bspec


def _conv_core(xbar, cg, w, bias, lp):
    rows = lax.broadcasted_iota(jnp.int32, (lp, _CONV_W), 0)
    u = jnp.where(rows >= PAD, cg * xbar, 0.0)
    u1 = _shift_down(u, 1, rows)
    u2 = _shift_down(u, 2, rows)
    y = bias + w[0:1] * u2 + w[1:2] * u1 + w[2:3] * u
    return rows, u, u1, u2, y


def conv_fwd(proj, w, bias, lp):
    pspec, col, wspec, bspec = _conv_specs(lp)

    def body(x_ref, b_ref, c_ref, w_ref, bias_ref, v_ref):
        _, _, _, _, y = _conv_core(x_ref[...], c_ref[...], w_ref[...], bias_ref[...], lp)
        v_ref[...] = (b_ref[...] * y).astype(BF16)

    return pl.pallas_call(
        body, name="conv_fwd", grid=(MIX // _CONV_W,),
        in_specs=[pspec(_XB), pspec(_BG), pspec(_CG), wspec, bspec], out_specs=col,
        out_shape=jax.ShapeDtypeStruct((lp, MIX), BF16), compiler_params=_cparams(),
    )(proj, proj, proj, w, bias)


def conv_bwd(dv, proj, w, bias, lp):
    pspec, col, wspec, bspec = _conv_specs(lp)

    def body(dv_ref, x_ref, b_ref, c_ref, w_ref, bias_ref, dx_ref, db_ref, dc_ref, dw_ref, dbias_ref):
        xbar, cg, w_ = x_ref[...], c_ref[...], w_ref[...]
        rows, u, u1, u2, y = _conv_core(xbar, cg, w_, bias_ref[...], lp)
        dv_ = dv_ref[...]
        db_ref[...] = (dv_ * y).astype(BF16)
        dy = dv_ * b_ref[...]
        dbias_ref[...] = jnp.sum(dy, axis=0, keepdims=True)
        dw_ref[0:1, :] = jnp.sum(dy * u2, axis=0, keepdims=True)
        dw_ref[1:2, :] = jnp.sum(dy * u1, axis=0, keepdims=True)
        dw_ref[2:3, :] = jnp.sum(dy * u, axis=0, keepdims=True)
        du = w_[2:3] * dy + w_[1:2] * _shift_up(dy, 1, rows, lp) + w_[0:1] * _shift_up(dy, 2, rows, lp)
        du = jnp.where(rows >= PAD, du, 0.0)
        dc_ref[...] = (du * xbar).astype(BF16)
        dx_ref[...] = (du * cg).astype(BF16)

    return pl.pallas_call(
        body, name="conv_bwd", grid=(MIX // _CONV_W,),
        in_specs=[col, pspec(_XB), pspec(_BG), pspec(_CG), wspec, bspec],
        out_specs=[col, col, col, wspec, bspec],
        out_shape=[jax.ShapeDtypeStruct((lp, MIX), BF16)] * 3 + [jax.ShapeDtypeStruct((3, MIX), F32),
                                                                jax.ShapeDtypeStruct((1, MIX), F32)],
        compiler_params=_cparams(),
    )(dv, proj, proj, proj, w, bias)


def _s5_disc(a_re, a_im, log_dt, b_re, b_im):
    dt = jnp.exp(log_dt)
    mag = jnp.exp(dt * a_re)
    ab_re, ab_im = mag * jnp.cos(dt * a_im), mag * jnp.sin(dt * a_im)
    den = a_re * a_re + a_im * a_im
    nr, ni = ab_re - 1.0, ab_im
    coef_re = (nr * a_re + ni * a_im) / den
    coef_im = (ni * a_re - nr * a_im) / den
    return ab_re, ab_im, coef_re * b_re - coef_im * b_im, coef_re * b_im + coef_im * b_re


_S5_ROWS = S5_GROUPS * S5_GROUP


def s5_prep(a_re, a_im, log_dt, b_re, b_im):
    def body(ar, ai, ld, br, bi, o0, o1, o2, o3):
        for o, v in zip((o0, o1, o2, o3), _s5_disc(ar[...], ai[...], ld[...], br[...], bi[...])):
            o[...] = v

    return pl.pallas_call(body, name="s5_prep",
                          out_shape=[jax.ShapeDtypeStruct((_S5_ROWS, S5_STATE), F32)] * 4)(a_re, a_im, log_dt, b_re, b_im)


def s5_prep_bwd(a_re, a_im, log_dt, b_re, b_im, d_ab_re, d_ab_im, d_bb_re, d_bb_im, sel):
    def body(ar, ai, ld, br, bi, g0, g1, g2, g3, sel_ref, da_re, da_im, dld, dbr, dbi):
        _, vjp = jax.vjp(_s5_disc, ar[...], ai[...], ld[...], br[...], bi[...])
        c_ar, c_ai, c_ld, c_br, c_bi = vjp((g0[...], g1[...], g2[...], g3[...]))
        s = sel_ref[...]
        hi = lax.Precision.HIGHEST
        da_re[...] = _dot(s, c_ar, 1, 0, precision=hi)
        da_im[...] = _dot(s, c_ai, 1, 0, precision=hi)
        dld[...] = jnp.sum(_dot(s, c_ld, 1, 0, precision=hi), axis=-1, keepdims=True)
        dbr[...] = c_br
        dbi[...] = c_bi

    g = jax.ShapeDtypeStruct((S5_GROUPS, S5_STATE), F32)
    full = jax.ShapeDtypeStruct((_S5_ROWS, S5_STATE), F32)
    return pl.pallas_call(body, name="s5_prep_bwd",
                          out_shape=[g, g, jax.ShapeDtypeStruct((S5_GROUPS, 1), F32), full, full],
                          )(a_re, a_im, log_dt, b_re, b_im, d_ab_re, d_ab_im, d_bb_re, d_bb_im, sel)


_SCAN_W = 128
_SCAN_STEPS = int(math.log2(SCAN_CHUNK))


def _cmul(ar, ai, br, bi):
    return ar * br - ai * bi, ar * bi + ai * br


def _scan_powers(ar, ai, reverse):
    pw = [(ar, ai)]
    for _ in range(_SCAN_STEPS):
        pw.append(_cmul(*pw[-1], *pw[-1]))
    rows = lax.broadcasted_iota(jnp.int32, (SCAN_CHUNK, ar.shape[-1]), 0)
    tr = jnp.broadcast_to(ar, rows.shape)
    ti = jnp.broadcast_to(ai, rows.shape)
    for k in range(_SCAN_STEPS):
        d = 2 ** k
        if reverse:
            live = rows < SCAN_CHUNK - d
            mr, mi = _cmul(tr, ti, _shift_up(tr, d, rows, SCAN_CHUNK), _shift_up(ti, d, rows, SCAN_CHUNK))
        else:
            live = rows >= d
            mr, mi = _cmul(tr, ti, _shift_down(tr, d, rows), _shift_down(ti, d, rows))
        tr = jnp.where(live, mr, tr)
        ti = jnp.where(live, mi, ti)
    return pw, rows, tr, ti


def s5_scan(bu, ab_re, ab_im, lp):
    n_chunks = lp // SCAN_CHUNK

    def body(bu_ref, ar_ref, ai_ref, s_ref):
        ar, ai = ar_ref[...], ai_ref[...]
        pw, rows, tr, ti = _scan_powers(ar, ai, False)

        def chunk(ci, carry):
            cr, cim = carry
            r0 = pl.multiple_of(ci * SCAN_CHUNK, SCAN_CHUNK)
            xr = bu_ref[0, pl.ds(r0, SCAN_CHUNK), :]
            xi = bu_ref[1, pl.ds(r0, SCAN_CHUNK), :]
            for k in range(_SCAN_STEPS):
                d = 2 ** k
                mr, mi = _cmul(pw[k][0], pw[k][1], _shift_down(xr, d, rows), _shift_down(xi, d, rows))
                xr, xi = xr + mr, xi + mi
            mr, mi = _cmul(tr, ti, cr, cim)
            xr, xi = xr + mr, xi + mi
            s_ref[0, pl.ds(r0, SCAN_CHUNK), :] = xr
            s_ref[1, pl.ds(r0, SCAN_CHUNK), :] = xi
            return xr[SCAN_CHUNK - 1:SCAN_CHUNK, :], xi[SCAN_CHUNK - 1:SCAN_CHUNK, :]

        zero = jnp.zeros((1, _SCAN_W), F32)
        lax.fori_loop(0, n_chunks, chunk, (zero, zero))

    spec = pl.BlockSpec((2, lp, _SCAN_W), lambda c: (0, 0, c))
    aspec = pl.BlockSpec((1, _SCAN_W), lambda c: (0, c))
    return pl.pallas_call(
        body, name="s5_scan", grid=(S5_LANES // _SCAN_W,), in_specs=[spec, aspec, aspec], out_specs=spec,
        out_shape=jax.ShapeDtypeStruct((2, lp, S5_LANES), F32), compiler_params=_cparams(),
    )(bu, ab_re, ab_im)


def s5_scan_bwd(ds, s, ab_re, ab_im, lp):
    n_chunks = lp // SCAN_CHUNK

    def body(ds_ref, s_ref, ar_ref, ai_ref, g_ref, da_ref):
        ar, ai = ar_ref[...], -ai_ref[...]
        pw, rows, tr, ti = _scan_powers(ar, ai, True)

        def chunk(k, carry):
            cr, cim, dar, dai = carry
            ci = n_chunks - 1 - k
            r0 = pl.multiple_of(ci * SCAN_CHUNK, SCAN_CHUNK)
            xr = ds_ref[0, pl.ds(r0, SCAN_CHUNK), :]
            xi = ds_ref[1, pl.ds(r0, SCAN_CHUNK), :]
            for j in range(_SCAN_STEPS):
                d = 2 ** j
                mr, mi = _cmul(pw[j][0], pw[j][1], _shift_up(xr, d, rows, SCAN_CHUNK), _shift_up(xi, d, rows, SCAN_CHUNK))
                xr, xi = xr + mr, xi + mi
            mr, mi = _cmul(tr, ti, cr, cim)
            xr, xi = xr + mr, xi + mi
            g_ref[0, pl.ds(r0, SCAN_CHUNK), :] = xr
            g_ref[1, pl.ds(r0, SCAN_CHUNK), :] = xi
            prev0 = pl.multiple_of(jnp.maximum(r0 - 8, 0), 8)
            live = (ci > 0).astype(F32)
            pr = s_ref[0, pl.ds(prev0, 8), :][7:8, :] * live
            pim = s_ref[1, pl.ds(prev0, 8), :][7:8, :] * live
            sr = s_ref[0, pl.ds(r0, SCAN_CHUNK), :]
            si = s_ref[1, pl.ds(r0, SCAN_CHUNK), :]
            sr = jnp.where(rows >= 1, pltpu.roll(sr, 1, 0), pr)
            si = jnp.where(rows >= 1, pltpu.roll(si, 1, 0), pim)
            dar = dar + jnp.sum(xr * sr + xi * si, axis=0, keepdims=True)
            dai = dai + jnp.sum(xi * sr - xr * si, axis=0, keepdims=True)
            return xr[0:1, :], xi[0:1, :], dar, dai

        zero = jnp.zeros((1, _SCAN_W), F32)
        _, _, dar, dai = lax.fori_loop(0, n_chunks, chunk, (zero, zero, zero, zero))
        da_ref[0] = dar
        da_ref[1] = dai

    spec = pl.BlockSpec((2, lp, _SCAN_W), lambda c: (0, 0, c))
    aspec = pl.BlockSpec((1, _SCAN_W), lambda c: (0, c))
    return pl.pallas_call(
        body, name="s5_scan_bwd", grid=(S5_LANES // _SCAN_W,), in_specs=[spec, spec, aspec, aspec],
        out_specs=[spec, pl.BlockSpec((2, 1, _SCAN_W), lambda c: (0, 0, c))],
        out_shape=[jax.ShapeDtypeStruct((2, lp, S5_LANES), F32), jax.ShapeDtypeStruct((2, 1, S5_LANES), F32)],
        compiler_params=_cparams(),
    )(ds, s, ab_re, ab_im)


S5_BLOCKS = 4
_S5_PER = S5_GROUPS // S5_BLOCKS


def _blockdiag(x):
    _, r, c = x.shape
    eye = jnp.eye(_S5_PER, dtype=x.dtype)
    x = x.reshape(S5_BLOCKS, _S5_PER, r, c)
    return (x[:, :, :, None, :] * eye[None, :, None, :, None]).reshape(S5_BLOCKS, _S5_PER * r, _S5_PER * c)


def _blockdiag_extract(m, r, c):
    return jnp.einsum('qgrgc->qgrc', m.reshape(S5_BLOCKS, _S5_PER, r, _S5_PER, c)).reshape(S5_GROUPS, r, c)


def bd_matmul(a, w, *, w_t, reduce, res=None, name):
    _, M, _ = a.shape
    n_w, _, k1, k2 = w.shape
    ka, kout = (k2, k1) if w_t else (k1, k2)
    tm = _row_tile(M)
    n_out, n_red = (1, n_w) if reduce else (n_w, 1)
    has_res = res is not None

    assert n_red <= 2

    def body(*refs):
        a_ref, w_ref = refs[0], refs[1]
        o_ref = refs[3] if has_res else refs[2]
        for q in range(S5_BLOCKS):
            cols = slice(q * kout, (q + 1) * kout)
            part = _dot(a_ref[:, q * ka:(q + 1) * ka].astype(BF16), w_ref[q], 1, 1 if w_t else 0)
            if n_red == 1:
                o_ref[:, cols] = part
            else:
                acc_ref = refs[-1]

                @pl.when(pl.program_id(2) == 0)
                def _():
                    acc_ref[:, cols] = part

                @pl.when(pl.program_id(2) == 1)
                def _():
                    tot = acc_ref[:, cols] + part
                    o_ref[:, cols] = tot + refs[2][:, cols] if has_res else tot

    if reduce:
        a_map, w_map = (lambda o, i, r: (r, i, 0)), (lambda o, i, r: (r, 0, 0, 0))
    else:
        a_map, w_map = (lambda o, i, r: (0, i, 0)), (lambda o, i, r: (o, 0, 0, 0))
    o_map = lambda o, i, r: (o, i, 0)
    in_specs = [pl.BlockSpec((None, tm, S5_BLOCKS * ka), a_map), pl.BlockSpec((None, S5_BLOCKS, k1, k2), w_map)]
    operands = [a, w]
    if has_res:
        in_specs.append(pl.BlockSpec((None, tm, S5_BLOCKS * kout), o_map))
        operands.append(res)
    return pl.pallas_call(
        body, name=name, grid=(n_out, M // tm, n_red), in_specs=in_specs,
        out_specs=pl.BlockSpec((None, tm, S5_BLOCKS * kout), o_map),
        out_shape=jax.ShapeDtypeStruct((n_out, M, S5_BLOCKS * kout), F32),
        scratch_shapes=[pltpu.VMEM((tm, S5_BLOCKS * kout), F32)] if n_red > 1 else [],
        compiler_params=_cparams(),
    )(*operands)


def bd_outer(a, b, name):
    na, M, wa = a.shape
    nb_, _, wb = b.shape
    ka, kb = wa // S5_BLOCKS, wb // S5_BLOCKS
    n_out = max(na, nb_)

    def body(a_ref, b_ref, o_ref):
        o_ref[...] = _dot(a_ref[...].astype(BF16), b_ref[...].astype(BF16), 0, 0)

    return pl.pallas_call(
        body, name=name, grid=(n_out, S5_BLOCKS),
        in_specs=[pl.BlockSpec((None, M, ka), (lambda o, q: (o, 0, q)) if na > 1 else (lambda o, q: (0, 0, q))),
                  pl.BlockSpec((None, M, kb), (lambda o, q: (o, 0, q)) if nb_ > 1 else (lambda o, q: (0, 0, q)))],
        out_specs=pl.BlockSpec((None, None, ka, kb), lambda o, q: (o, q, 0, 0)),
        out_shape=jax.ShapeDtypeStruct((n_out, S5_BLOCKS, ka, kb), F32), compiler_params=_cparams(),
    )(a, b)


def s5_u(proj, lp):
    def fn(row0, rv, pv):
        u, = rv
        return ((jnp.where(_row_mask(row0, u.shape), u, 0.0),), ())

    return rowwise(fn, [(proj, MIX, 5)], [], [(MIX, BF16)], name="s5_u", lp=lp)[0]


def s5_y(ys, proj, d, lp):
    def fn(row0, rv, pv):
        ys_, u = rv
        y = ys_ + pv[0] * u
        return ((y, _gelu(y)), ())

    return rowwise(fn, [(ys, MIX, 0), (proj, MIX, 5)], [d], [(MIX, F32), (MIX, BF16)], name="s5_y", lp=lp)


def s5_glu(z, y, b, lp):
    def fn(row0, rv, pv):
        z_, y_ = rv
        return ((_gelu(y_) * _sigmoid(z_ + pv[0]),), ())

    return rowwise(fn, [(z, MIX, 0), (y, MIX, 0)], [b], [(MIX, BF16)], name="s5_glu", lp=lp)[0]


def s5_glu_bwd(dgl, z, y, b, lp):
    def fn(row0, rv, pv):
        dgl_, z_, y_ = rv
        sg = _sigmoid(z_ + pv[0])
        dz = dgl_ * _gelu(y_) * sg * (1.0 - sg)
        return ((dgl_ * sg, dz), (jnp.sum(dz, axis=0, keepdims=True),))

    return rowwise(fn, [(dgl, MIX, 0), (z, MIX, 0), (y, MIX, 0)], [b], [(MIX, F32), (MIX, BF16)], [((1, MIX), F32)],
                   name="s5_glu_bwd", lp=lp)


def s5_y_bwd(dyg, y, proj, d, lp):
    def fn(row0, rv, pv):
        dyg_, y_, u = rv
        dy = dyg_ * _gelu_grad(y_)
        return ((dy, dy * pv[0]), (jnp.sum(dy * u, axis=0, keepdims=True),))

    return rowwise(fn, [(dyg, MIX, 0), (y, MIX, 0), (proj, MIX, 5)], [d], [(MIX, BF16), (MIX, F32)], [((1, MIX), F32)],
                   name="s5_y_bwd", lp=lp)


def s5_du(du, lp):
    def fn(row0, rv, pv):
        return ((jnp.where(_row_mask(row0, rv[0].shape), rv[0], 0.0),), ())

    return rowwise(fn, [(du, MIX, 0)], [], [(MIX, BF16)], name="s5_du", lp=lp)[0]


def merge_fwd(proj, ya, yb, yc, lp):
    def fn(row0, rv, pv):
        g0, g1, g2, a, b, c = rv
        return ((_sigmoid(g0) * a + _sigmoid(g1) * b + _sigmoid(g2) * c,), ())

    return rowwise(fn, [(proj, D_MODEL, 3), (proj, D_MODEL, 4), (proj, D_MODEL, 5), (ya, D_MODEL, 0), (yb, D_MODEL, 0),
                        (yc, D_MODEL, 0)], [], [(D_MODEL, BF16)], name="merge_fwd", lp=lp)[0]


def merge_bwd(dmix, proj, ya, yb, yc, lp):
    def fn(row0, rv, pv):
        dm, g0, g1, g2, a, b, c = rv
        outs_y, outs_g = [], []
        for g, yv in ((g0, a), (g1, b), (g2, c)):
            sg = _sigmoid(g)
            outs_y.append(dm * sg)
            outs_g.append(dm * yv * sg * (1.0 - sg))
        return (tuple(outs_y) + (jnp.concatenate(outs_g, axis=1),), ())

    return rowwise(fn, [(dmix, D_MODEL, 0), (proj, D_MODEL, 3), (proj, D_MODEL, 4), (proj, D_MODEL, 5),
                        (ya, D_MODEL, 0), (yb, D_MODEL, 0), (yc, D_MODEL, 0)], [],
                   [(D_MODEL, BF16)] * 3 + [(P_IN // 2, BF16, P_IN, 1)], name="merge_bwd", lp=lp)


def dproj_fill(dproj, dcq, dkr, dckv, dxbar, dbg, dcg, du, place, lp):
    tm = _row_tile(lp)

    def body(dproj_ref, dcq_ref, dkr_ref, dckv_ref, dx_ref, db_ref, dc_ref, du_ref, place_ref, o_ref):
        o_ref[:, 0:384] = dcq_ref[...]
        o_ref[:, 384:512] = _dot(dkr_ref[...], place_ref[...], 1, 0).astype(BF16)
        o_ref[:, 512:768] = dckv_ref[...]
        o_ref[:, 768:1024] = jnp.zeros((tm, 256), BF16)
        o_ref[:, 1024:1536] = dx_ref[...]
        o_ref[:, 1536:2048] = db_ref[...]
        o_ref[:, 2048:2560] = dc_ref[...]
        o_ref[:, 2560:3072] = du_ref[...]

    def row(w):
        return pl.BlockSpec((tm, w), lambda i: (i, 0))

    return pl.pallas_call(
        body, name="dproj_fill", grid=(lp // tm,),
        in_specs=[pl.BlockSpec(memory_space=pl.ANY), row(Q_RANK), row(D_ROPE), row(KV_RANK), row(MIX), row(MIX), row(MIX),
                  row(MIX), pl.BlockSpec(place.shape, lambda i: (0, 0))],
        out_specs=row(P_IN // 2), out_shape=jax.ShapeDtypeStruct((lp, P_IN), BF16),
        input_output_aliases={0: 0}, compiler_params=_cparams(),
    )(dproj, dcq, dkr, dckv, dxbar, dbg, dcg, du, place)


def loss_head(h, tgt, lp):
    def fn(row0, rv, pv):
        h_, t_ = rv
        live = (row0 + lax.broadcasted_iota(jnp.int32, h_.shape, 0)) >= X0
        diff = jnp.where(live, h_ - t_, 0.0)
        ssq = jnp.sum(jnp.sum(diff * diff, axis=1, keepdims=True), axis=0, keepdims=True)
        return ((diff * (1.0 / D_MODEL),), (ssq * (0.5 / D_MODEL),))

    return rowwise(fn, [(h, D_MODEL, 0), (tgt, D_MODEL, 0)], [], [(D_MODEL, F32)], [((1, 1), F32)], name="loss_head", lp=lp)


def _s5_consts(W):
    ab_re_rep, ab_im_rep, bb_re, bb_im = s5_prep(W['s5_a_re'], W['s5_a_im'], W['s5_log_dt'], W['s5_b_re'], W['s5_b_im'])
    pick = lambda t: t.reshape(S5_GROUPS, S5_GROUP, S5_STATE)[:, 0].reshape(1, S5_LANES)
    bb = jnp.stack([_blockdiag(bb_re.reshape(S5_GROUPS, S5_GROUP, S5_STATE)),
                    _blockdiag(bb_im.reshape(S5_GROUPS, S5_GROUP, S5_STATE))]).astype(BF16)
    return pick(ab_re_rep), pick(ab_im_rep), bb


def layer_fwd(h, hb, W, tabs, lp, ffn1=None, ffn2=True):
    cos, sin, rot = tabs[:3]
    h1, h1b, sv1 = ffn1 if ffn1 is not None else ffn_fwd(h, hb, W['wg1'], W['wu1'], W['wd1'], W['ln1_g'], W['ln1_b'], lp)
    proj = matmul(h1b[None], W['w_in'][None], tb=True, name="proj")[0]
    cqn, ckvn, kr = mla_prep(proj, cos, sin, rot, W['q_norm_g'], W['kv_norm_g'], lp)
    q96, k96, v = mla_heads(cqn, ckvn, kr, *tabs[3:], W['wq'], W['wk'], W['wv'], lp)
    o = attn_fwd(q96, k96, v, lp)
    ya = heads_out(o, W['mla_wo'], lp)
    vconv = conv_fwd(proj, W['conv_w'], W['conv_b'], lp)
    yb = matmul(vconv[None], W['conv_wout'][None], name="conv_out")[0]
    ub = s5_u(proj, lp)
    ab_re, ab_im, bb = _s5_consts(W)
    bu = bd_matmul(ub[None], bb, w_t=False, reduce=False, name="s5_bu")
    s = s5_scan(bu, ab_re, ab_im, lp)
    ys = bd_matmul(s, W['s5_ct'], w_t=False, reduce=True, name="s5_cs")[0]
    y, ygb = s5_y(ys, proj, W['s5_d'], lp)
    zg = matmul(ygb[None], W['s5_wglu'][None], name="s5_glu_mm")[0]
    glb = s5_glu(zg, y, W['s5_b_glu'], lp)
    yc = matmul(glb[None], W['s5_wout'][None], name="s5_out")[0]
    mixed = merge_fwd(proj, ya, yb, yc, lp)
    z2, h2, h2b = mm_res_ln(mixed[None], W['w_o'][None], h1, W['ln2_g'], W['ln2_b'], scale=1.0, name="wo_ln", lp=lp)
    sv = dict(sv1=sv1, h1b=h1b, proj=proj, cqn=cqn, ckvn=ckvn, q96=q96, k96=k96, v=v, o=o, ya=ya,
              vconv=vconv, yb=yb, ub=ub, ab_re=ab_re, ab_im=ab_im, bb=bb, s=s, y=y, ygb=ygb, zg=zg, glb=glb, yc=yc,
              mixed=mixed, z2=z2)
    if not ffn2:
        return h2, h2b, sv
    h3, h3b, sv['sv3'] = ffn_fwd(h2, h2b, W['wg2'], W['wu2'], W['wd2'], W['ln3_g'], W['ln3_b'], lp)
    return h3, h3b, sv


def layer_bwd(dh3, sv, W, tabs, lp, ffn1=True, ffn2=True):
    cos, sin, rot = tabs[:3]
    proj = sv['proj']
    G = {}
    dh2 = dh3
    if ffn2:
        dh2, g3 = ffn_bwd(dh3, sv['sv3'], W['wg2'], W['wu2'], W['wd2'], W['ln3_g'], lp)
        G.update(wg2=g3['wg'], wu2=g3['wu'], wd2=g3['wd'], ln3_g=g3['ln_g'], ln3_b=g3['ln_b'])
    dz2, dz2b, G['ln2_g'], G['ln2_b'] = ln_bwd(dh2, sv['z2'], W['ln2_g'], fscale=1.0, name="wo_ln_bwd", lp=lp)
    dmix = matmul(dz2b[None], W['w_o'][None], tb=True, name="wo_dx")[0]
    G['w_o'] = matmul(sv['mixed'][None], dz2b[None], ta=True, out_dtype=WGRAD, name="wo_dw")[0]
    dya, dyb, dyc, dproj = merge_bwd(dmix, proj, sv['ya'], sv['yb'], sv['yc'], lp)
    dgl = matmul(dyc[None], W['s5_wout'][None], tb=True, name="s5_out_dx")[0]
    G['s5_wout'] = matmul(sv['glb'][None], dyc[None], ta=True, out_dtype=WGRAD, name="s5_out_dw")[0]
    t1, dzb, G['s5_b_glu'] = s5_glu_bwd(dgl, sv['zg'], sv['y'], W['s5_b_glu'], lp)
    dyg = matmul(dzb[None], W['s5_wglu'][None], tb=True, res=t1[None], name="s5_glu_dx")[0]
    G['s5_wglu'] = matmul(sv['ygb'][None], dzb[None], ta=True, out_dtype=WGRAD, name="s5_glu_dw")[0]
    dyb_, du_d, G['s5_d'] = s5_y_bwd(dyg, sv['y'], proj, W['s5_d'], lp)
    ds = bd_matmul(dyb_[None], W['s5_ct'], w_t=True, reduce=False, name="s5_cs_dx")
    G['s5_ct'] = bd_outer(sv['s'], dyb_[None], "s5_cs_dw")
    g_adj, d_ab = s5_scan_bwd(ds, sv['s'], sv['ab_re'], sv['ab_im'], lp)
    du = bd_matmul(g_adj, sv['bb'], w_t=True, reduce=True, res=du_d[None], name="s5_bu_dx")[0]
    d_bb = bd_outer(sv['ub'][None], g_adj, "s5_bu_dw")
    du_b = s5_du(du, lp)
    onehot = (jnp.arange(S5_GROUP) == 0).astype(F32)
    spread = lambda t: (t.reshape(S5_GROUPS, 1, S5_STATE) * onehot[None, :, None]).reshape(_S5_ROWS, S5_STATE)
    take = lambda t: _blockdiag_extract(t, S5_GROUP, S5_STATE).reshape(_S5_ROWS, S5_STATE)
    sel = jnp.kron(jnp.eye(S5_GROUPS, dtype=F32), jnp.ones((1, S5_GROUP), F32))
    (G['s5_a_re'], G['s5_a_im'], G['s5_log_dt'], G['s5_b_re'], G['s5_b_im']) = s5_prep_bwd(
        W['s5_a_re'], W['s5_a_im'], W['s5_log_dt'], W['s5_b_re'], W['s5_b_im'],
        spread(d_ab[0]), spread(d_ab[1]), take(d_bb[0]), take(d_bb[1]), sel)
    dv = matmul(dyb[None], W['conv_wout'][None], tb=True, name="conv_out_dx")[0]
    G['conv_wout'] = matmul(sv['vconv'][None], dyb[None], ta=True, out_dtype=WGRAD, name="conv_out_dw")[0]
    dxbar, dbg, dcg, G['conv_w'], G['conv_b'] = conv_bwd(dv, proj, W['conv_w'], W['conv_b'], lp)
    do = heads_out_dx(dya, W['mla_wo'], lp)
    G['mla_wo'] = matmul(sv['o'], dya[None], ta=True, ab='o', out_dtype=WGRAD, name="mla_out_dw")
    dq96, dk96, dvv = attn_bwd(sv['q96'], sv['k96'], sv['v'], do, lp)
    dcq, dckv, dqp, dkr = mla_heads_bwd(dq96, dk96, dvv, *tabs[3:], W['wq'], W['wk'], W['wv'], lp)
    G['wq'] = matmul(sv['cqn'][None], dqp, ta=True, bb='o', out_dtype=WGRAD, name="mla_dwq")
    G['wk'] = matmul(sv['ckvn'][None], dk96, ta=True, bb='o', out_dtype=WGRAD, name="mla_dwk")
    G['wv'] = matmul(sv['ckvn'][None], dvv, ta=True, bb='o', out_dtype=WGRAD, name="mla_dwv")
    dcq_raw, dckv_raw, dkr_raw, G['q_norm_g'], G['kv_norm_g'] = mla_prep_bwd(
        dcq, dckv, dkr, proj, cos, sin, rot, W['q_norm_g'], W['kv_norm_g'], lp)
    dproj = dproj_fill(dproj, dcq_raw, dkr_raw, dckv_raw, dxbar, dbg, dcg, du_b, jnp.eye(D_ROPE, 128, dtype=BF16), lp)
    dh1 = matmul(dproj[None], W['w_in'][None], res=dz2[None], res_scale=ALPHA, name="proj_dx")[0]
    G['w_in'] = matmul(dproj[None], sv['h1b'][None], ta=True, out_dtype=WGRAD, name="proj_dw")[0]
    if not ffn1:
        return dh1, G
    dh0, g1 = ffn_bwd(dh1, sv['sv1'], W['wg1'], W['wu1'], W['wd1'], W['ln1_g'], lp)
    G.update(wg1=g1['wg'], wu1=g1['wu'], wd1=g1['wd'], ln1_g=g1['ln_g'], ln1_b=g1['ln_b'])
    return dh0, G


def _nat_cols(st):
    return jnp.transpose(st, (1, 0, 2)).reshape(st.shape[1], -1)


def _shard_cols(nat):
    k, n = nat.shape
    return jnp.transpose(nat.reshape(k, N_SHARD, n // N_SHARD), (1, 0, 2))


def _win_pad(wt):
    z = lambda n: jnp.zeros((n, wt.shape[1]), wt.dtype)
    return jnp.concatenate([wt[0:384], wt[640:672], z(96), wt[384:640], z(256), wt[672:]], axis=0)


def _win_unpad(wp):
    return jnp.concatenate([wp[0:384], wp[512:768], wp[384:416], wp[1024:]], axis=0)


_BIG = [('ffn1_w_gate', 'T'), ('ffn1_w_up', 'T'), ('ffn1_w_down', 0), ('w_in', 'T'), ('mla_w_uq', 1), ('mla_w_ukv', 1),
        ('mla_w_o', 1), ('conv_w_out', 1), ('s5_w_glu', 0), ('s5_w_out', 1), ('w_o', 0),
        ('ffn2_w_gate', 'T'), ('ffn2_w_up', 'T'), ('ffn2_w_down', 0)]
_REPL = ['ln1_g', 'ln1_b', 'mla_q_norm_g', 'mla_kv_norm_g', 'conv_b', 's5_a_re', 's5_a_im', 's5_log_dt', 's5_b_re',
         's5_b_im', 's5_c_re', 's5_c_im', 's5_d', 's5_b_glu', 'ln2_g', 'ln2_b', 'ln3_g', 'ln3_b']


def compute_weights(st, small):
    W = {}
    for t in ('1', '2'):
        if 'ffn%s_w_gate' % t in st:
            W['wg' + t], W['wu' + t], W['wd' + t] = (st['ffn%s_w_%s' % (t, p)] for p in ('gate', 'up', 'down'))
    if 'w_in' in st:
        W.update(_mixer_weights(st))
    if small is not None:
        W.update(_small_weights(small))
    return W


def _mixer_weights(st):
    W = {}
    W['w_in'] = _win_pad(st['w_in'].reshape(D_IN, D_MODEL))
    W['wq'] = jnp.transpose(_nat_cols(st['mla_w_uq']).reshape(Q_RANK, N_HEADS, D_QK), (1, 0, 2))
    ukv = jnp.transpose(_nat_cols(st['mla_w_ukv']).reshape(KV_RANK, N_HEADS, D_NOPE + D_V), (1, 0, 2))
    W['wk'] = jnp.concatenate([ukv[:, :, :D_NOPE], jnp.zeros((N_HEADS, KV_RANK, D_ROPE), ukv.dtype)], axis=2)
    W['wv'] = ukv[:, :, D_NOPE:]
    W['mla_wo'] = _nat_cols(st['mla_w_o']).reshape(N_HEADS, D_V, D_MODEL)
    W['conv_wout'] = _nat_cols(st['conv_w_out'])
    W['s5_wglu'] = st['s5_w_glu'].reshape(MIX, MIX)
    W['s5_wout'] = _nat_cols(st['s5_w_out'])
    W['w_o'] = st['w_o'].reshape(D_MODEL, D_MODEL)
    return W


def _small_weights(small):
    W = {}
    W['conv_w'] = small['conv_w']
    for n in ('ln1_g', 'ln1_b', 'ln2_g', 'ln2_b', 'ln3_g', 'ln3_b', 'conv_b', 's5_b_glu'):
        W[n] = small[n].reshape(1, -1)
    W['q_norm_g'] = small['mla_q_norm_g'].reshape(1, -1)
    W['kv_norm_g'] = small['mla_kv_norm_g'].reshape(1, -1)
    W['s5_d'] = small['s5_d'].reshape(1, MIX)
    rep = lambda t: jnp.repeat(t, S5_GROUP, axis=0)
    W['s5_a_re'], W['s5_a_im'] = rep(small['s5_a_re']), rep(small['s5_a_im'])
    W['s5_log_dt'] = jnp.broadcast_to(rep(small['s5_log_dt'].reshape(S5_GROUPS, 1)), (_S5_ROWS, S5_STATE))
    tr = lambda t: jnp.transpose(t, (0, 2, 1)).reshape(_S5_ROWS, S5_STATE)
    W['s5_b_re'], W['s5_b_im'] = tr(small['s5_b_re']), tr(small['s5_b_im'])
    ct = lambda t: _blockdiag(jnp.transpose(t, (0, 2, 1)))
    W['s5_ct'] = jnp.stack([ct(small['s5_c_re']), -ct(small['s5_c_im'])]).astype(BF16)
    return W


def reference_grads(G, ffn=True):
    R = {}
    for t in ('1', '2') if ffn else ():
        R['ffn%s_w_gate' % t] = G['wg' + t].reshape(D_FF, D_MODEL).T
        R['ffn%s_w_up' % t] = G['wu' + t].reshape(D_FF, D_MODEL).T
        R['ffn%s_w_down' % t] = G['wd' + t].reshape(D_FF, D_MODEL)
    R['w_in_t'] = _win_unpad(G['w_in'])
    if ffn:
        R['w_in'] = R['w_in_t'].T
    R['mla_w_uq'] = jnp.transpose(G['wq'], (1, 0, 2)).reshape(Q_RANK, -1)
    R['mla_w_ukv'] = jnp.transpose(jnp.concatenate([G['wk'][:, :, :D_NOPE], G['wv']], axis=2), (1, 0, 2)).reshape(KV_RANK, -1)
    R['mla_w_o'] = G['mla_wo'].reshape(N_HEADS * D_V, D_MODEL)
    R['conv_w'], R['conv_w_out'] = G['conv_w'], G['conv_wout']
    R['s5_w_glu'], R['s5_w_out'], R['w_o'] = G['s5_wglu'], G['s5_wout'], G['w_o']
    for n in ('ln1_g', 'ln1_b', 'ln2_g', 'ln2_b', 'ln3_g', 'ln3_b', 'conv_b', 's5_b_glu'):
        if n in G:
            R[n] = G[n].reshape(-1)
    R['mla_q_norm_g'], R['mla_kv_norm_g'] = G['q_norm_g'].reshape(-1), G['kv_norm_g'].reshape(-1)
    R['s5_d'] = G['s5_d'].reshape(S5_GROUPS, S5_GROUP)
    R['s5_a_re'], R['s5_a_im'], R['s5_log_dt'] = G['s5_a_re'], G['s5_a_im'], G['s5_log_dt'].reshape(-1)
    untr = lambda t: jnp.transpose(t.reshape(S5_GROUPS, S5_GROUP, S5_STATE), (0, 2, 1))
    R['s5_b_re'], R['s5_b_im'] = untr(G['s5_b_re']), untr(G['s5_b_im'])
    unct = lambda t: jnp.transpose(_blockdiag_extract(t, S5_STATE, S5_GROUP), (0, 2, 1))
    R['s5_c_re'], R['s5_c_im'] = unct(G['s5_ct'][0]), -unct(G['s5_ct'][1])
    return R


_ANY = pl.BlockSpec(memory_space=pl.ANY)
LANES = 1024


def _place():
    x, y, c = lax.axis_index("x"), lax.axis_index("y"), lax.axis_index("c")
    chips = [(1 - x, y), (x, 1 - y), (1 - x, 1 - y)]
    return x, y, c, chips


def _rows_of(c, half):
    return pl.ds(pl.multiple_of(c * half, 8), half)


def all_gather_shards(srcs, exact):
    n, m = len(srcs), len(exact)
    halves = [s.shape[0] // 2 for s in srcs]

    def body(*refs):
        s_refs, e_refs = refs[:n], refs[n:n + m]
        o_refs, eo_refs = refs[n + m:2 * n + m], refs[2 * n + m:2 * n + 2 * m]
        send, recv, esend, erecv, osend, orecv, lsem = refs[2 * n + 2 * m:]
        x, y, c, chips = _place()
        me = 2 * x + y
        sibling = (x, y, 1 - c)
        own = [pltpu.make_async_remote_copy(src_ref=s_refs[k], dst_ref=o_refs[k].at[me], send_sem=osend.at[k],
                                            recv_sem=orecv.at[k], device_id=sibling, device_id_type=MESH) for k in range(n)]
        local = [pltpu.make_async_copy(e_refs[k], eo_refs[k].at[me], lsem.at[k]) for k in range(m)]
        for cp in own + local:
            cp.start()

        def copy(k, s, src, idx, half_c, to):
            return pltpu.make_async_remote_copy(
                src_ref=src, dst_ref=o_refs[k].at[idx, _rows_of(half_c, halves[k])], send_sem=send.at[6 * k + s],
                recv_sem=recv.at[6 * k + s], device_id=to, device_id_type=MESH)

        def ecopy(k, j, idx, to):
            return pltpu.make_async_remote_copy(src_ref=e_refs[k], dst_ref=eo_refs[k].at[idx], send_sem=esend.at[3 * k + j],
                                                recv_sem=erecv.at[3 * k + j], device_id=to, device_id_type=MESH)

        sends = []
        for k in range(n):
            mine = s_refs[k].at[_rows_of(c, halves[k])]
            sends += [copy(k, j, mine, me, c, (*chip, c)) for j, chip in enumerate(chips)]
        for k in range(m):
            sends += [ecopy(k, j, me, (*chip, c)) for j, chip in enumerate(chips)]
        for cp in sends:
            cp.start()
        for j, chip in enumerate(chips):
            idx = 2 * chip[0] + chip[1]
            for k in range(n):
                landed = o_refs[k].at[idx, _rows_of(c, halves[k])]
                copy(k, j, landed, idx, c, sibling).wait_recv()
                fwd = copy(k, 3 + j, landed, idx, c, sibling)
                fwd.start()
                sends.append(fwd)
        for j, chip in enumerate(chips):
            idx = 2 * chip[0] + chip[1]
            for k in range(n):
                copy(k, 3 + j, s_refs[k].at[_rows_of(c, halves[k])], idx, 1 - c, sibling).wait_recv()
            for k in range(m):
                ecopy(k, j, idx, sibling).wait_recv()
        for cp in sends:
            cp.wait_send()
        for cp in own + local:
            cp.wait()

    outs = pl.pallas_call(
        body, name="all_gather_weights", in_specs=[_ANY] * (n + m), out_specs=[_ANY] * (n + m),
        out_shape=[jax.ShapeDtypeStruct((N_SHARD,) + a.shape, a.dtype) for a in list(srcs) + list(exact)],
        scratch_shapes=[pltpu.SemaphoreType.DMA((6 * n,)), pltpu.SemaphoreType.DMA((6 * n,)),
                        pltpu.SemaphoreType.DMA((3 * m,)), pltpu.SemaphoreType.DMA((3 * m,)),
                        pltpu.SemaphoreType.DMA((n,)), pltpu.SemaphoreType.DMA((n,)), pltpu.SemaphoreType.DMA((m,))],
    )(*srcs, *exact)
    return outs[:n], outs[n:]


def rs_pair_swap(gs):
    n = len(gs)

    def body(*refs):
        g_refs, r_refs, send, recv = refs[:n], refs[n:2 * n], refs[2 * n], refs[2 * n + 1]
        x, y, c, _ = _place()
        copies = [pltpu.make_async_remote_copy(
            src_ref=g_refs[k].at[pl.ds(0, N_SHARD), _rows_of(1 - c, gs[k].shape[1] // 2)], dst_ref=r_refs[k],
            send_sem=send.at[k], recv_sem=recv.at[k], device_id=(x, y, 1 - c), device_id_type=MESH) for k in range(n)]
        for cp in copies:
            cp.start()
        for cp in copies:
            cp.wait()

    return pl.pallas_call(
        body, name="grad_pair_swap", in_specs=[_ANY] * n, out_specs=[_ANY] * n,
        out_shape=[jax.ShapeDtypeStruct((N_SHARD, g.shape[1] // 2, g.shape[2]), g.dtype) for g in gs],
        scratch_shapes=[pltpu.SemaphoreType.DMA((n,)), pltpu.SemaphoreType.DMA((n,))],
    )(*gs)


def _group_tile(half, n_cols, n_arrays):
    budget = (20 * 2 ** 20) // (6 * n_arrays)
    fits = [t for t in range(8, half + 1, 8) if half % t == 0 and t * n_cols * 4 <= budget]
    return max(fits) if fits else 8


def rs_pair_add(gs, rs, cidx, out_dtype, name):
    n = len(gs)
    _, K, cols = gs[0].shape
    half = K // 2
    tr = _group_tile(half, cols, n)
    nb = half // tr

    def body(c_ref, *refs):
        for g_ref, r_ref, o_ref in zip(refs[:n], refs[n:2 * n], refs[2 * n:]):
            o_ref[...] = (g_ref[...].astype(F32) + r_ref[...].astype(F32)).astype(out_dtype)

    gspec = pl.BlockSpec((None, tr, cols), lambda j, i, c: (j, c[0] * nb + i, 0))
    rspec = pl.BlockSpec((None, tr, cols), lambda j, i, c: (j, i, 0))
    return pl.pallas_call(
        body, name=name,
        grid_spec=pltpu.PrefetchScalarGridSpec(num_scalar_prefetch=1, grid=(N_SHARD, nb), in_specs=[gspec] * n + [rspec] * n,
                                               out_specs=[rspec] * n),
        out_shape=[jax.ShapeDtypeStruct((N_SHARD, half, cols), out_dtype)] * n,
        compiler_params=_cparams(),
    )(cidx, *gs, *rs)


def rs_chip_sum(qs, nl, cidx, name):
    n = len(qs)
    _, half, cols = qs[0].shape
    tr = _group_tile(half, cols, n)
    nb = half // tr

    def body(c_ref, *refs):
        for k, q_ref in enumerate(refs[:n]):
            o_ref = refs[n + k // nl]
            o_ref[k % nl] = ((q_ref[0].astype(F32) + q_ref[1].astype(F32)) + q_ref[2].astype(F32)) + q_ref[3].astype(F32)

    return pl.pallas_call(
        body, name=name,
        grid_spec=pltpu.PrefetchScalarGridSpec(
            num_scalar_prefetch=1, grid=(nb,),
            in_specs=[pl.BlockSpec((N_SHARD, tr, cols), lambda i, c: (0, i, 0))] * n,
            out_specs=[pl.BlockSpec((nl, tr, cols), lambda i, c: (0, c[0] * nb + i, 0))] * (n // nl)),
        out_shape=[jax.ShapeDtypeStruct((nl, 2 * half, cols), F32)] * (n // nl),
        compiler_params=_cparams(),
    )(cidx, *qs)


def rs_pair_gather(fs, name):
    n = len(fs)

    def body(*refs):
        f_refs, send, recv = refs[n:2 * n], refs[2 * n], refs[2 * n + 1]
        x, y, c, _ = _place()
        copies = []
        for k in range(n):
            rows = f_refs[k].at[pl.ds(0, fs[k].shape[0]), _rows_of(c, fs[k].shape[1] // 2)]
            copies.append(pltpu.make_async_remote_copy(src_ref=rows, dst_ref=rows, send_sem=send.at[k], recv_sem=recv.at[k],
                                                       device_id=(x, y, 1 - c), device_id_type=MESH))
        for cp in copies:
            cp.start()
        for cp in copies:
            cp.wait()

    return pl.pallas_call(
        body, name=name, in_specs=[_ANY] * n, out_specs=[_ANY] * n,
        out_shape=[jax.ShapeDtypeStruct(f.shape, f.dtype) for f in fs],
        input_output_aliases={k: k for k in range(n)},
        scratch_shapes=[pltpu.SemaphoreType.DMA((n,)), pltpu.SemaphoreType.DMA((n,))],
    )(*fs)


_HBM = pl.BlockSpec(memory_space=pltpu.HBM)
_SEM = pl.BlockSpec(memory_space=pltpu.SEMAPHORE)
_EFFECT = pltpu.SideEffectType.DATAFLOW_SIDE_EFFECTING


def _in_hbm(a):
    return pltpu.with_memory_space_constraint(a, pltpu.HBM)


def split_start(name, srcs, lands, after, copies_fn, n_copies):
    n = len(srcs)

    def body(*refs):
        for cp in copies_fn(refs[:n], refs[n:2 * n], refs[2 * n + 1], refs[2 * n + 2]):
            cp.start()
        refs[-1][...] = jnp.zeros_like(refs[-1])

    bufs = list(srcs) + list(lands)
    outs = pl.pallas_call(
        body, name=name,
        out_shape=(pltpu.SemaphoreType.DMA((n_copies,)), pltpu.SemaphoreType.DMA((n_copies,)),
                   *[pltpu.HBM(a.shape, a.dtype) for a in bufs], jax.ShapeDtypeStruct((8, 128), F32)),
        in_specs=[_HBM] * (2 * n) + [_ANY],
        out_specs=(_SEM, _SEM, *[_HBM] * (2 * n), pl.BlockSpec(memory_space=pltpu.VMEM)),
        input_output_aliases={i: 2 + i for i in range(2 * n)},
        compiler_params=pltpu.CompilerParams(has_side_effects=_EFFECT),
    )(*[_in_hbm(a) for a in bufs], after)
    return outs[0], outs[1], outs[2:2 + n], outs[2 + n:2 + 2 * n], outs[-1]


def split_wait(name, send, recv, srcs, lands, after, copies_fn, which=None):
    n = len(srcs)

    def body(*refs):
        copies = copies_fn(refs[:n], refs[n:2 * n], refs[2 * n], refs[2 * n + 1], which)
        for cp in copies:
            cp.wait_send()
        for cp in copies:
            cp.wait_recv()

    bufs = list(srcs) + list(lands)
    outs = pl.pallas_call(
        body, name=name, out_shape=tuple(pltpu.HBM(a.shape, a.dtype) for a in bufs),
        in_specs=[_HBM] * (2 * n) + [_SEM, _SEM, _ANY], out_specs=tuple([_HBM] * (2 * n)),
        input_output_aliases={i: i for i in range(2 * n)},
        compiler_params=pltpu.CompilerParams(has_side_effects=_EFFECT),
    )(*bufs, send, recv, after)
    return list(outs[:n]), list(outs[n:])


def _gather_copies(s_refs, l_refs, send, recv, which=None):
    x, y, c, chips = _place()
    me = 2 * x + y
    out = []
    for k in (range(len(s_refs)) if which is None else which):
        s, l = s_refs[k], l_refs[k]
        rows = _rows_of(c, s.shape[0] // 2)
        for j, chip in enumerate(chips):
            out.append(pltpu.make_async_remote_copy(src_ref=s.at[rows], dst_ref=l.at[me, rows], send_sem=send.at[4 * k + j],
                                                    recv_sem=recv.at[4 * k + j], device_id=(*chip, c), device_id_type=MESH))
        out.append(pltpu.make_async_remote_copy(src_ref=s, dst_ref=l.at[me], send_sem=send.at[4 * k + 3],
                                                recv_sem=recv.at[4 * k + 3], device_id=(x, y, 1 - c), device_id_type=MESH))
    return out


def _scatter_copies(s_refs, l_refs, send, recv, which=None):
    x, y, c, chips = _place()
    me = 2 * x + y
    return [pltpu.make_async_remote_copy(src_ref=s_refs[k].at[2 * chip[0] + chip[1]], dst_ref=l_refs[k].at[me],
                                         send_sem=send.at[3 * k + j], recv_sem=recv.at[3 * k + j], device_id=(*chip, c),
                                         device_id_type=MESH)
            for k in (range(len(s_refs)) if which is None else which) for j, chip in enumerate(chips)]


def gather_forward(lands, name):
    n = len(lands)

    def body(*refs):
        l_refs, send, recv = refs[n:2 * n], refs[2 * n], refs[2 * n + 1]
        x, y, c, chips = _place()
        copies = []
        for k in range(n):
            rows = _rows_of(c, lands[k].shape[1] // 2)
            for j, chip in enumerate(chips):
                part = l_refs[k].at[2 * chip[0] + chip[1], rows]
                copies.append(pltpu.make_async_remote_copy(src_ref=part, dst_ref=part, send_sem=send.at[3 * k + j],
                                                           recv_sem=recv.at[3 * k + j], device_id=(x, y, 1 - c),
                                                           device_id_type=MESH))
        for cp in copies:
            cp.start()
        for cp in copies:
            cp.wait()

    return pl.pallas_call(
        body, name=name, in_specs=[_ANY] * n, out_specs=[_ANY] * n,
        out_shape=[jax.ShapeDtypeStruct(a.shape, a.dtype) for a in lands],
        input_output_aliases={k: k for k in range(n)},
        scratch_shapes=[pltpu.SemaphoreType.DMA((3 * n,)), pltpu.SemaphoreType.DMA((3 * n,))],
    )(*lands)


def rs_partials(gs, wire, cidx, tag):
    rs = rs_pair_swap(gs)
    groups = {}
    for k, g in enumerate(gs):
        groups.setdefault((g.shape, jnp.dtype(wire[k]).name), []).append(k)
    ps = [None] * len(gs)
    for gi, ks in enumerate(groups.values()):
        outs = rs_pair_add([gs[k] for k in ks], [rs[k] for k in ks], cidx, wire[ks[0]], "grad_pair_add_%s%d" % (tag, gi))
        for k, o in zip(ks, outs):
            ps[k] = o
    return ps


def rs_finish(items, tag):
    cidx = lax.axis_index("c").astype(jnp.int32).reshape(1)
    groups = {}
    for i, it in enumerate(items):
        groups.setdefault((it[0].shape, len(it), it[0].dtype.name), []).append(i)
    fs = [None] * len(items)
    for gi, ids in enumerate(groups.values()):
        outs = rs_chip_sum([q for i in ids for q in items[i]], len(items[ids[0]]), cidx, "grad_chip_sum_%s%d" % (tag, gi))
        for i, o in zip(ids, outs):
            fs[i] = o
    return rs_pair_gather(fs, "grad_pair_gather_" + tag)


def adamw(w, g, m, v, name):
    shape = w.shape
    if w.ndim == 2:
        block, grid, index = shape, (1,), (lambda i: (0, 0))
    else:
        slab = shape[2:]
        unit = 4 * int(np.prod(slab[:-2] or (1,))) * (-(-slab[-1] // 128) * 128)
        if len(slab) >= 2:
            unit *= -(-slab[-2] // 8) * 8
        k = shape[1]
        tr = k
        if k * unit > 2 ** 21:
            tr = max(t for t in range(8, k, 8) if k % t == 0 and t * unit <= 2 ** 21)
        block, grid = (None, tr) + tuple(slab), (shape[0], k // tr)
        index = lambda l, i: (l, i) + (0,) * len(slab)
        if tr < min(k, 64) and len(slab) == 1:
            tc = max(t for t in range(128, slab[0] + 1, 128) if slab[0] % t == 0 and k * t * 4 <= 2 ** 21)
            block, grid = (None, k, tc), (shape[0], slab[0] // tc)
            index = lambda l, i: (l, 0, i)

    def body(w_ref, g_ref, m_ref, v_ref, d_ref, nm_ref, nv_ref):
        g_ = g_ref[...]
        m_new = ADAM_B1 * m_ref[...] + (1.0 - ADAM_B1) * g_
        v_new = ADAM_B2 * v_ref[...] + (1.0 - ADAM_B2) * (g_ * g_)
        m_hat = m_new / (1.0 - ADAM_B1 ** ADAM_STEP)
        v_hat = v_new / (1.0 - ADAM_B2 ** ADAM_STEP)
        d_ref[...] = -ADAM_LR * (m_hat / (jnp.sqrt(v_hat) + ADAM_EPS) + ADAM_WD * w_ref[...])
        nm_ref[...] = m_new
        nv_ref[...] = v_new

    spec = pl.BlockSpec(block, index)
    return pl.pallas_call(
        body, name=name, grid=grid, in_specs=[spec] * 4, out_specs=[spec] * 3,
        out_shape=[jax.ShapeDtypeStruct(shape, F32)] * 3, compiler_params=_cparams(),
    )(w, g, m, v)


_WEIGHTS = ['meta', 'ffn1_w_gate', 'ffn1_w_up', 'ffn1_w_down', 'ln1_g', 'ln1_b', 'w_in', 'mla_q_norm_g', 'mla_w_uq',
            'mla_kv_norm_g', 'mla_w_ukv', 'mla_w_o', 'conv_w', 'conv_b', 'conv_w_out', 's5_a_re', 's5_a_im', 's5_log_dt',
            's5_b_re', 's5_b_im', 's5_c_re', 's5_c_im', 's5_d', 's5_w_glu', 's5_b_glu', 's5_w_out', 'w_o', 'ln2_g', 'ln2_b',
            'ffn2_w_gate', 'ffn2_w_up', 'ffn2_w_down', 'ln3_g', 'ln3_b']


def _pad_to(flat, n):
    return jnp.concatenate([flat, jnp.zeros((n - flat.shape[0],), flat.dtype)])


def _shard_of(full, axis):
    if axis == 1:
        return _shard_cols(full)
    if axis == 'T':
        return full.T.reshape(N_SHARD, full.shape[1] // N_SHARD, full.shape[0])
    return full.reshape(N_SHARD, full.shape[0] // N_SHARD, full.shape[1])


_FFN_KEY = {'gate': 'wg', 'up': 'wu', 'down': 'wd'}


def _pad_rows(a, axis):
    k = a.shape[axis]
    extra = -k % 32
    if not extra:
        return a
    return jnp.pad(a, [(0, extra) if d == axis else (0, 0) for d in range(a.ndim)])


def _step(env):
    w = {n: env[n] for n in _WEIGHTS}
    mom = {n: env['m_' + n] for n in _WEIGHTS}
    var = {n: env['v_' + n] for n in _WEIGHTS}
    cidx = lax.axis_index("c").astype(jnp.int32).reshape(1)
    chip = 2 * lax.axis_index("x") + lax.axis_index("y")
    big_names = [n for n, _ in _BIG]
    nb = len(big_names)

    kept_t = [n for n, a in _BIG if a == 'T']
    own = {n: (jnp.swapaxes(w[n], 1, 2) if n in kept_t else w[n]) for n in big_names}
    first = [n for n in big_names if n.startswith('ffn1')]
    mix = [n for n in big_names if not n.startswith('ffn')]
    last = [n for n in big_names if n.startswith('ffn2')]
    rest = mix + last
    nm, nr = len(mix), len(mix) + len(last)
    src = lambda n, li: _pad_rows(own[n][li].astype(BF16), 0)
    gathered_first, (conv_w_st, meta_st) = all_gather_shards([src(n, 0) for n in first], [w['conv_w'], w['meta']])
    later = [src(n, 0) for n in rest] + [src(n, 1) for n in big_names]
    lands = [lax.empty((N_SHARD,) + s.shape, BF16) for s in later]
    g_send, g_recv, later_t, lands_t, token = split_start("gather_start", later, lands, gathered_first[0], _gather_copies,
                                                          4 * len(later))

    def weights_of(names, st, li, with_small):
        small = None
        if with_small:
            small = {n: w[n][li] for n in _REPL}
            small['conv_w'] = _nat_cols(conv_w_st[:, li])
        return compute_weights({n: a[:, :own[n].shape[1]] for n, a in zip(names, st)}, small)

    x2d = env['x'][0]
    lp = x2d.shape[0] + X0
    tabs = _rope_tables(lp)
    h = jnp.concatenate([jnp.zeros((PAD, D_MODEL), F32), _nat_cols(meta_st), x2d], axis=0) + token[0, 0]
    W0 = weights_of(first, gathered_first, 0, True)
    ffn1 = ffn_fwd(h, h.astype(BF16), W0['wg1'], W0['wu1'], W0['wd1'], W0['ln1_g'], W0['ln1_b'], lp)
    later_t, lands_t = split_wait("gather0_wait", g_send, g_recv, later_t, lands_t, ffn1[0], _gather_copies, range(nm))
    W0.update(weights_of(mix, gather_forward(lands_t[:nm], "gather0_forward"), 0, False))
    h, hb, sv0 = layer_fwd(None, None, W0, tabs, lp, ffn1=ffn1, ffn2=False)
    later_t, lands_t = split_wait("gather0b_wait", g_send, g_recv, later_t, lands_t, h, _gather_copies, range(nm, nr))
    W0.update(weights_of(last, gather_forward(lands_t[nm:nr], "gather0b_forward"), 0, False))
    h, hb, sv0['sv3'] = ffn_fwd(h, hb, W0['wg2'], W0['wu2'], W0['wd2'], W0['ln3_g'], W0['ln3_b'], lp)
    _, lands_t = split_wait("gather1_wait", g_send, g_recv, later_t, lands_t, h, _gather_copies, range(nr, len(later)))
    W1 = weights_of(big_names, gather_forward(lands_t[nr:], "gather1_forward"), 1, True)
    h, hb, sv1 = layer_fwd(h, hb, W1, tabs, lp)
    tgt = jnp.concatenate([jnp.zeros((X0, D_MODEL), F32), env['loss_target'][0]], axis=0)
    dh, loss_part = loss_head(h, tgt, lp)
    loss = lax.psum(loss_part[0, 0], ("x", "y", "c"))

    def shards(G, names):
        full = None if all(n.startswith('ffn') for n in names) else reference_grads(G, ffn=False)

        def one(n, a):
            if n.startswith('ffn'):
                return G[_FFN_KEY[n.split('_')[-1]] + n[3]]
            if n == 'w_in':
                return full['w_in_t'].reshape(N_SHARD, D_IN // N_SHARD, D_MODEL)
            return _shard_of(full[n], a)

        return [_pad_rows(one(n, a), 1) for n, a in _BIG if n in names]

    def scatter_start(name, ps, after):
        qs = [lax.dynamic_update_slice_in_dim(jnp.zeros_like(p), lax.dynamic_slice_in_dim(p, chip, 1, axis=0), chip, axis=0)
              for p in ps]
        return split_start(name, ps, qs, after, _scatter_copies, 3 * len(ps))

    dh, G1 = layer_bwd(dh, sv1, W1, tabs, lp)
    p1 = rs_partials(shards(G1, big_names), [BF16] * nb, cidx, "b")
    s1_send, s1_recv, p1_t, q1_t, token1 = scatter_start("scatter1_start", p1, dh)
    dh, g3 = ffn_bwd(dh, sv0['sv3'], W0['wg2'], W0['wu2'], W0['wd2'], W0['ln3_g'] + token1[0, 0], lp)
    G0 = dict(wg2=g3['wg'], wu2=g3['wu'], wd2=g3['wd'])
    p0l = rs_partials(shards(G0, last), [BF16] * len(last), cidx, "c")
    sl_send, sl_recv, p0l_t, q0l_t, token0l = scatter_start("scatter0b_start", p0l, dh)
    dh, Gm = layer_bwd(dh, sv0, dict(W0, ln2_g=W0['ln2_g'] + token0l[0, 0]), tabs, lp, ffn1=False, ffn2=False)
    G0.update(Gm, ln3_g=g3['ln_g'], ln3_b=g3['ln_b'])
    p0m = rs_partials(shards(G0, mix), [BF16] * nm, cidx, "d")
    sm_send, sm_recv, p0m_t, q0m_t, token0m = scatter_start("scatter0_start", p0m, dh)
    dh, g1 = ffn_bwd(dh, sv0['sv1'], W0['wg1'], W0['wu1'], W0['wd1'], W0['ln1_g'] + token0m[0, 0], lp)
    G0.update(wg1=g1['wg'], wu1=g1['wu'], wd1=g1['wd'], ln1_g=g1['ln_g'], ln1_b=g1['ln_b'])
    _, q1 = split_wait("scatter1_wait", s1_send, s1_recv, p1_t, q1_t, dh, _scatter_copies)
    _, q0_last = split_wait("scatter0b_wait", sl_send, sl_recv, p0l_t, q0l_t, dh, _scatter_copies)
    _, q0_mix = split_wait("scatter0_wait", sm_send, sm_recv, p0m_t, q0m_t, dh, _scatter_copies)
    q0_rest = list(q0_mix) + list(q0_last)
    full = [reference_grads(G0, ffn=False), reference_grads(G1, ffn=False)]

    s_parts = [jnp.stack([full[li][n] for li in range(DEPTH)]).reshape(-1) for n in _REPL + ['conv_w']]
    s_parts.append(dh[PAD:X0].reshape(-1))
    s_sizes = [int(p.shape[0]) for p in s_parts]
    s_rows = -(-sum(s_sizes) // (16 * LANES)) * 16
    g_small = _pad_to(jnp.concatenate(s_parts), s_rows * LANES).reshape(1, s_rows, LANES)
    g_small = jnp.broadcast_to(g_small, (N_SHARD, s_rows, LANES))
    p_last = rs_partials(shards(G0, first) + [g_small], [BF16] * len(first) + [F32], cidx, "a")
    z_send, z_recv, pz_t, qz_t, token_z = scatter_start("scatter_last_start", p_last, dh)

    def step_weights(names, grad):
        out = {}
        for n in names:
            if n in kept_t:
                res = adamw(own[n], grad[n], jnp.swapaxes(mom[n], 1, 2), jnp.swapaxes(var[n], 1, 2), "adamw_" + n)
                out[n] = [jnp.swapaxes(t, 1, 2) for t in [grad[n]] + list(res)]
            else:
                out[n] = [grad[n]] + list(adamw(w[n], grad[n], mom[n], var[n], "adamw_" + n))
        return out

    q1 = dict(zip(big_names, q1))
    q0 = dict(zip(rest, q0_rest))
    q0[rest[0]] = q0[rest[0]] + token_z[0, 0].astype(BF16)
    red = rs_finish([[q0[n], q1[n]] for n in rest], "a")
    done = step_weights(rest, {n: r[:, :own[n].shape[1]] for n, r in zip(rest, red)})
    all_done = jnp.stack([done[n][3][(0,) * done[n][3].ndim] for n in rest])
    _, q_last = split_wait("scatter_last_wait", z_send, z_recv, pz_t, qz_t, all_done, _scatter_copies)
    red = rs_finish([[q, q1[n]] for n, q in zip(first, q_last)] + [[q_last[-1]]], "b")
    f_small = red[-1].reshape(-1)

    grad = {n: r[:, :own[n].shape[1]] for n, r in zip(first, red)}
    off = 0
    for n, sz in zip(_REPL + ['conv_w', 'meta'], s_sizes):
        grad[n] = f_small[off:off + sz]
        off += sz
    for n in _REPL:
        grad[n] = grad[n].reshape(w[n].shape)
    cw = grad['conv_w'].reshape(DEPTH, 3, MIX)
    grad['conv_w'] = lax.dynamic_slice_in_dim(cw, chip * (MIX // N_SHARD), MIX // N_SHARD, axis=2)
    gm = grad['meta'].reshape(N_META, D_MODEL)
    grad['meta'] = lax.dynamic_slice_in_dim(gm, chip * (D_MODEL // N_SHARD), D_MODEL // N_SHARD, axis=1)

    done.update(step_weights([n for n in _WEIGHTS if n not in done], grad))
    return (loss, dh[X0:][None], *[done[n][k] for k in range(4) for n in _WEIGHTS])


def kernel(x, meta, ffn1_w_gate, ffn1_w_up, ffn1_w_down, ln1_g, ln1_b, w_in, mla_q_norm_g, mla_w_uq, mla_kv_norm_g, mla_w_ukv, mla_w_o, conv_w, conv_b, conv_w_out, s5_a_re, s5_a_im, s5_log_dt, s5_b_re, s5_b_im, s5_c_re, s5_c_im, s5_d, s5_w_glu, s5_b_glu, s5_w_out, w_o, ln2_g, ln2_b, ffn2_w_gate, ffn2_w_up, ffn2_w_down, ln3_g, ln3_b, loss_target, m_meta, m_ffn1_w_gate, m_ffn1_w_up, m_ffn1_w_down, m_ln1_g, m_ln1_b, m_w_in, m_mla_q_norm_g, m_mla_w_uq, m_mla_kv_norm_g, m_mla_w_ukv, m_mla_w_o, m_conv_w, m_conv_b, m_conv_w_out, m_s5_a_re, m_s5_a_im, m_s5_log_dt, m_s5_b_re, m_s5_b_im, m_s5_c_re, m_s5_c_im, m_s5_d, m_s5_w_glu, m_s5_b_glu, m_s5_w_out, m_w_o, m_ln2_g, m_ln2_b, m_ffn2_w_gate, m_ffn2_w_up, m_ffn2_w_down, m_ln3_g, m_ln3_b, v_meta, v_ffn1_w_gate, v_ffn1_w_up, v_ffn1_w_down, v_ln1_g, v_ln1_b, v_w_in, v_mla_q_norm_g, v_mla_w_uq, v_mla_kv_norm_g, v_mla_w_ukv, v_mla_w_o, v_conv_w, v_conv_b, v_conv_w_out, v_s5_a_re, v_s5_a_im, v_s5_log_dt, v_s5_b_re, v_s5_b_im, v_s5_c_re, v_s5_c_im, v_s5_d, v_s5_w_glu, v_s5_b_glu, v_s5_w_out, v_w_o, v_ln2_g, v_ln2_b, v_ffn2_w_gate, v_ffn2_w_up, v_ffn2_w_down, v_ln3_g, v_ln3_b):
    return _step(dict(locals()))
```

```python
import functools
import math

import numpy as np
import jax
import jax.numpy as jnp
from jax import lax
from jax.experimental import pallas as pl
from jax.experimental.pallas import tpu as pltpu

F32 = jnp.float32
BF16 = jnp.bfloat16

D_MODEL = 1024
DEPTH = 2
N_META = 16
PAD = 112
X0 = PAD + N_META
N_HEADS = 8
D_NOPE = 64
D_ROPE = 32
D_V = 64
Q_RANK = 384
KV_RANK = 256
MIX = 512
S5_GROUPS = 32
S5_GROUP = 16
S5_STATE = 64
S5_LANES = S5_GROUPS * S5_STATE
D_FF = 2816
N_SHARD = 4
FF_SHARD = D_FF // N_SHARD
D_IN = 5792
P_IN = 6144
ALPHA = (2.0 * DEPTH) ** 0.25
LN_EPS = 1e-5
RMS_EPS = 1e-6
ATT_SCALE = (D_NOPE + D_ROPE) ** -0.5
ROPE_BASE = 10000.0
ADAM_LR, ADAM_B1, ADAM_B2, ADAM_EPS, ADAM_WD, ADAM_STEP = 0.001, 0.9, 0.999, 1e-08, 0.01, 10
SCAN_CHUNK = 64
VMEM_LIMIT = 52 * 2 ** 20
WGRAD = BF16
MESH = pl.DeviceIdType.MESH


def _cparams(**kw):
    return pltpu.CompilerParams(vmem_limit_bytes=VMEM_LIMIT, **kw)


def _tile(n):
    if n <= 1088:
        return n
    for t in (1088, 1024, 544, 512, 272, 256, 128):
        if n % t == 0:
            return t
    return n


def _row_tile(lp):
    for t in (544, 272, 128):
        if lp % t == 0:
            return t
    return lp


def _ffn_tile(lp):
    return 1088 if lp % 1088 == 0 else _row_tile(lp)


def _sigmoid(x):
    return 1.0 / (1.0 + jnp.exp(-x))


_GELU_C = math.sqrt(2.0 / math.pi)


def _gelu(x):
    return 0.5 * x * (1.0 + jnp.tanh(_GELU_C * (x + 0.044715 * x * x * x)))


def _gelu_grad(x):
    t = jnp.tanh(_GELU_C * (x + 0.044715 * x * x * x))
    return 0.5 * (1.0 + t) + 0.5 * x * (1.0 - t * t) * _GELU_C * (1.0 + 3.0 * 0.044715 * x * x)


def _dot(a, b, ca, cb, precision=None):
    return lax.dot_general(a, b, (((ca,), (cb,)), ((), ())), preferred_element_type=F32, precision=precision)


def matmul(a, b, *, name, ta=False, tb=False, ab='n', bb='n', res=None, res_scale=1.0, scale=1.0, out_dtype=F32):
    if ta:
        _, K, M = a.shape
    else:
        _, M, K = a.shape
    if tb:
        _, N, K2 = b.shape
    else:
        _, K2, N = b.shape
    assert K == K2, (a.shape, b.shape)
    n_out = max(a.shape[0] if ab == 'o' else 1, b.shape[0] if bb == 'o' else 1)
    n_red = max(a.shape[0] if ab == 'r' else 1, b.shape[0] if bb == 'r' else 1)
    tm, tn = _tile(M), _tile(N)
    tk = K if K <= 2304 else _tile(K)
    nkt = K // tk
    n_steps = n_red * nkt

    def bsel(mode, o, r):
        if mode == 'o':
            return o
        if mode == 'r':
            return r // nkt if nkt > 1 else r
        return 0

    def ksel(r):
        if nkt == 1:
            return 0
        return r % nkt if n_red > 1 else r

    a_map = (lambda o, i, j, r: (bsel(ab, o, r), ksel(r), i)) if ta else (lambda o, i, j, r: (bsel(ab, o, r), i, ksel(r)))
    b_map = (lambda o, i, j, r: (bsel(bb, o, r), j, ksel(r))) if tb else (lambda o, i, j, r: (bsel(bb, o, r), ksel(r), j))
    o_map = lambda o, i, j, r: (o, i, j)
    in_specs = [pl.BlockSpec((None, tk, tm) if ta else (None, tm, tk), a_map),
                pl.BlockSpec((None, tn, tk) if tb else (None, tk, tn), b_map)]
    operands = [a, b]
    if res is not None:
        in_specs.append(pl.BlockSpec((None, tm, tn), o_map))
        operands.append(res)
    has_res = res is not None

    def body(*refs):
        a_ref, b_ref = refs[0], refs[1]
        res_ref = refs[2] if has_res else None
        o_ref = refs[3] if has_res else refs[2]
        part = _dot(a_ref[...].astype(BF16), b_ref[...].astype(BF16), 0 if ta else 1, 1 if tb else 0)

        def finish(acc):
            v = acc if scale == 1.0 else acc * scale
            if has_res:
                v = v + res_scale * res_ref[...].astype(F32)
            o_ref[...] = v.astype(o_ref.dtype)

        if n_steps == 1:
            finish(part)
        else:
            acc_ref = refs[-1]
            r = pl.program_id(3)

            @pl.when(r == 0)
            def _():
                acc_ref[...] = part

            @pl.when(r > 0)
            def _():
                acc_ref[...] += part

            @pl.when(r == n_steps - 1)
            def _():
                finish(acc_ref[...])

    return pl.pallas_call(
        body, name=name,
        grid=(n_out, M // tm, N // tn, n_steps),
        in_specs=in_specs,
        out_specs=pl.BlockSpec((None, tm, tn), o_map),
        out_shape=jax.ShapeDtypeStruct((n_out, M, N), out_dtype),
        scratch_shapes=[pltpu.VMEM((tm, tn), F32)] if n_steps > 1 else [],
        compiler_params=_cparams(),
    )(*operands)


def rowwise(fn, rows, pars, outs, accs=(), *, name, lp):
    tm = _row_tile(lp)
    n_rows, n_pars, n_outs, n_accs = len(rows), len(pars), len(outs), len(accs)
    outs = [tuple(o) + (o[0], 0) if len(o) == 2 else tuple(o) for o in outs]
    in_specs = [pl.BlockSpec((tm, w), functools.partial(lambda i, cb: (i, cb), cb=cb)) for _, w, cb in rows]
    in_specs += [pl.BlockSpec(p.shape, functools.partial(lambda i, nd: (0,) * nd, nd=p.ndim)) for p in pars]
    out_specs = [pl.BlockSpec((tm, w), functools.partial(lambda i, cb: (i, cb), cb=cb)) for w, _, _, cb in outs]
    out_specs += [pl.BlockSpec(s, functools.partial(lambda i, nd: (0,) * nd, nd=len(s))) for s, _ in accs]
    out_shape = [jax.ShapeDtypeStruct((lp, total), dt) for _, dt, total, _ in outs]
    out_shape += [jax.ShapeDtypeStruct(s, dt) for s, dt in accs]

    def body(*refs):
        i = pl.program_id(0)
        rv = [r[...] for r in refs[:n_rows]]
        pv = [r[...] for r in refs[n_rows:n_rows + n_pars]]
        o_refs = refs[n_rows + n_pars:n_rows + n_pars + n_outs]
        a_refs = refs[n_rows + n_pars + n_outs:]
        ov, av = fn(i * tm, rv, pv)
        for r, v in zip(o_refs, ov):
            r[...] = v.astype(r.dtype)
        if n_accs:
            @pl.when(i == 0)
            def _():
                for r, v in zip(a_refs, av):
                    r[...] = v.astype(r.dtype)

            @pl.when(i > 0)
            def _():
                for r, v in zip(a_refs, av):
                    r[...] += v.astype(r.dtype)

    res = pl.pallas_call(
        body, name=name, grid=(lp // tm,), in_specs=in_specs, out_specs=out_specs, out_shape=out_shape,
        compiler_params=_cparams(),
    )(*[r[0] for r in rows], *pars)
    return res


def _row_mask(row0, shape):
    return (row0 + lax.broadcasted_iota(jnp.int32, shape, 0)) >= PAD


def ffn_up(hb, wg, wu, lp):
    tm = _ffn_tile(lp)

    def body(h_ref, wg_ref, wu_ref, ab_ref, hid_ref):
        h = h_ref[...]
        a = _dot(h, wg_ref[...], 1, 1)
        b = _dot(h, wu_ref[...], 1, 1)
        ab_ref[0] = a.astype(BF16)
        ab_ref[1] = b.astype(BF16)
        hid_ref[...] = (a * _sigmoid(a) * b).astype(BF16)

    wspec = pl.BlockSpec((None, FF_SHARD, D_MODEL), lambda j, i: (j, 0, 0))
    return pl.pallas_call(
        body, name="ffn_up", grid=(N_SHARD, lp // tm),
        in_specs=[pl.BlockSpec((tm, D_MODEL), lambda j, i: (i, 0)), wspec, wspec],
        out_specs=[pl.BlockSpec((None, 2, tm, FF_SHARD), lambda j, i: (j, 0, i, 0)),
                   pl.BlockSpec((None, tm, FF_SHARD), lambda j, i: (j, i, 0))],
        out_shape=[jax.ShapeDtypeStruct((N_SHARD, 2, lp, FF_SHARD), BF16),
                   jax.ShapeDtypeStruct((N_SHARD, lp, FF_SHARD), BF16)],
        compiler_params=_cparams(),
    )(hb, wg, wu)


def _layer_norm(z, g, b):
    mu = jnp.mean(z, axis=-1, keepdims=True)
    zc = z - mu
    var = jnp.mean(zc * zc, axis=-1, keepdims=True)
    return zc * lax.rsqrt(var + LN_EPS) * g + b


def mm_res_ln(a, w, res, g, b, *, scale, name, lp):
    n_red, _, K = a.shape
    tm = _row_tile(lp)

    def body(a_ref, w_ref, res_ref, g_ref, b_ref, z_ref, h_ref, hb_ref, acc_ref):
        r = pl.program_id(1)
        part = _dot(a_ref[...].astype(BF16), w_ref[...], 1, 0)

        @pl.when(r == 0)
        def _():
            acc_ref[...] = part

        @pl.when(r > 0)
        def _():
            acc_ref[...] += part

        @pl.when(r == n_red - 1)
        def _():
            z = ALPHA * res_ref[...] + scale * acc_ref[...]
            z_ref[...] = z
            hn = _layer_norm(z, g_ref[...], b_ref[...])
            h_ref[...] = hn
            hb_ref[...] = hn.astype(BF16)

    row = pl.BlockSpec((tm, D_MODEL), lambda i, r: (i, 0))
    par = pl.BlockSpec((1, D_MODEL), lambda i, r: (0, 0))
    return pl.pallas_call(
        body, name=name, grid=(lp // tm, n_red),
        in_specs=[pl.BlockSpec((None, tm, K), lambda i, r: (r, i, 0)),
                  pl.BlockSpec((None, K, D_MODEL), lambda i, r: (r, 0, 0)), row, par, par],
        out_specs=[row, row, row],
        out_shape=[jax.ShapeDtypeStruct((lp, D_MODEL), F32), jax.ShapeDtypeStruct((lp, D_MODEL), F32),
                   jax.ShapeDtypeStruct((lp, D_MODEL), BF16)],
        scratch_shapes=[pltpu.VMEM((tm, D_MODEL), F32)],
        compiler_params=_cparams(),
    )(a, w, res, g, b)


def ln_bwd(dh, z, g, *, fscale, name, lp):
    def fn(row0, rv, pv):
        dh_, z_ = rv
        g_, = pv
        mu = jnp.mean(z_, axis=-1, keepdims=True)
        zc = z_ - mu
        rstd = lax.rsqrt(jnp.mean(zc * zc, axis=-1, keepdims=True) + LN_EPS)
        xh = zc * rstd
        dxh = dh_ * g_
        m1 = jnp.mean(dxh, axis=-1, keepdims=True)
        m2 = jnp.mean(dxh * xh, axis=-1, keepdims=True)
        dz = rstd * (dxh - m1 - xh * m2)
        return ((dz, fscale * dz),
                (jnp.sum(dh_ * xh, axis=0, keepdims=True), jnp.sum(dh_, axis=0, keepdims=True)))

    return rowwise(fn, [(dh, D_MODEL, 0), (z, D_MODEL, 0)], [g], [(D_MODEL, F32), (D_MODEL, BF16)],
                   [((1, D_MODEL), F32), ((1, D_MODEL), F32)], name=name, lp=lp)


def ffn_down_bwd(dfb, wd, ab, lp):
    tm = _ffn_tile(lp)

    def body(df_ref, w_ref, ab_ref, da_ref, db_ref):
        dhid = _dot(df_ref[...], w_ref[...], 1, 1)
        a = ab_ref[0].astype(F32)
        b = ab_ref[1].astype(F32)
        sg = _sigmoid(a)
        da_ref[...] = (dhid * b * (sg * (1.0 + a * (1.0 - sg)))).astype(BF16)
        db_ref[...] = (dhid * (a * sg)).astype(BF16)

    ospec = pl.BlockSpec((None, tm, FF_SHARD), lambda j, i: (j, i, 0))
    return pl.pallas_call(
        body, name="ffn_down_bwd", grid=(N_SHARD, lp // tm),
        in_specs=[pl.BlockSpec((tm, D_MODEL), lambda j, i: (i, 0)),
                  pl.BlockSpec((None, FF_SHARD, D_MODEL), lambda j, i: (j, 0, 0)),
                  pl.BlockSpec((None, 2, tm, FF_SHARD), lambda j, i: (j, 0, i, 0))],
        out_specs=[ospec, ospec],
        out_shape=[jax.ShapeDtypeStruct((N_SHARD, lp, FF_SHARD), BF16)] * 2,
        compiler_params=_cparams(),
    )(dfb, wd, ab)


def ffn_dx(da, db, wg, wu, dz, lp):
    tm = _ffn_tile(lp)

    def body(da_ref, db_ref, wg_ref, wu_ref, dz_ref, o_ref, acc_ref):
        j = pl.program_id(1)
        part = _dot(da_ref[...], wg_ref[...], 1, 0) + _dot(db_ref[...], wu_ref[...], 1, 0)

        @pl.when(j == 0)
        def _():
            acc_ref[...] = part

        @pl.when(j > 0)
        def _():
            acc_ref[...] += part

        @pl.when(j == N_SHARD - 1)
        def _():
            o_ref[...] = acc_ref[...] + ALPHA * dz_ref[...]

    aspec = pl.BlockSpec((None, tm, FF_SHARD), lambda i, j: (j, i, 0))
    wspec = pl.BlockSpec((None, FF_SHARD, D_MODEL), lambda i, j: (j, 0, 0))
    row = pl.BlockSpec((tm, D_MODEL), lambda i, j: (i, 0))
    return pl.pallas_call(
        body, name="ffn_dx", grid=(lp // tm, N_SHARD), in_specs=[aspec, aspec, wspec, wspec, row], out_specs=row,
        out_shape=jax.ShapeDtypeStruct((lp, D_MODEL), F32), scratch_shapes=[pltpu.VMEM((tm, D_MODEL), F32)],
        compiler_params=_cparams(),
    )(da, db, wg, wu, dz)


def ffn_fwd(h, hb, wg, wu, wd, g, b, lp):
    ab, hid = ffn_up(hb, wg, wu, lp)
    z, hn, hnb = mm_res_ln(hid, wd, h, g, b, scale=0.5, name="ffn_down_ln", lp=lp)
    return hn, hnb, dict(hb=hb, ab=ab, hid=hid, z=z)


def ffn_bwd(dh, sv, wg, wu, wd, g, lp):
    dz, dfb, dg, db = ln_bwd(dh, sv['z'], g, fscale=0.5, name="ffn_ln_bwd", lp=lp)
    da, dbb = ffn_down_bwd(dfb, wd, sv['ab'], lp)
    d_wd = matmul(sv['hid'], dfb[None], ta=True, ab='o', out_dtype=WGRAD, name="ffn_dwd")
    d_wg = matmul(da, sv['hb'][None], ta=True, ab='o', out_dtype=WGRAD, name="ffn_dwg")
    d_wu = matmul(dbb, sv['hb'][None], ta=True, ab='o', out_dtype=WGRAD, name="ffn_dwu")
    dh_in = ffn_dx(da, dbb, wg, wu, dz, lp)
    return dh_in, dict(wg=d_wg, wu=d_wu, wd=d_wd, ln_g=dg, ln_b=db)


def _rope_tables(lp):
    pos = np.arange(lp, dtype=np.float32) - PAD
    inv = ROPE_BASE ** (-np.arange(0, D_ROPE, 2, dtype=np.float32) / D_ROPE)
    ang = pos[:, None] * inv[None, :]
    cos = np.concatenate([np.cos(ang), np.cos(ang)], axis=1).astype(np.float32)
    sin = np.concatenate([np.sin(ang), np.sin(ang)], axis=1).astype(np.float32)
    rot = np.zeros((D_ROPE, D_ROPE), np.float32)
    half = D_ROPE // 2
    for j in range(half):
        rot[j + half, j] = -1.0
        rot[j, j + half] = 1.0
    d_qk = D_NOPE + D_ROPE
    cos_qk = np.concatenate([np.ones((lp, D_NOPE), np.float32), cos], axis=1)
    sin_qk = np.concatenate([np.zeros((lp, D_NOPE), np.float32), sin], axis=1)
    rot_qk = np.zeros((d_qk, d_qk), np.float32)
    rot_qk[D_NOPE:, D_NOPE:] = rot
    place = np.zeros((D_ROPE, d_qk), np.float32)
    place[:, D_NOPE:] = np.eye(D_ROPE, dtype=np.float32)
    return tuple(jnp.asarray(t) for t in (cos, sin, rot, cos_qk, sin_qk, rot_qk, place))


def _rot(x, rot):
    return _dot(x, rot, 1, 0, precision=lax.Precision.HIGHEST)


def _rms(x, g):
    r = lax.rsqrt(jnp.mean(x * x, axis=-1, keepdims=True) + RMS_EPS)
    return x * r * g


def mla_prep(proj, cos, sin, rot, qg, kvg, lp):
    def fn(row0, rv, pv):
        cq, krb, ckv, c, s = rv
        qg_, kvg_, rot_ = pv
        kr = krb[:, :D_ROPE]
        return ((_rms(cq, qg_), _rms(ckv, kvg_), kr * c + _rot(kr, rot_) * s), ())

    return rowwise(fn, [(proj, Q_RANK, 0), (proj, 128, 3), (proj, KV_RANK, 2), (cos, D_ROPE, 0), (sin, D_ROPE, 0)],
                   [qg, kvg, rot], [(Q_RANK, BF16), (KV_RANK, BF16), (D_ROPE, BF16)], name="mla_prep", lp=lp)


def mla_heads(cqn, ckvn, kr, cos_qk, sin_qk, rot_qk, place, wq, wk, wv, lp):
    tm = _row_tile(lp)

    def body(cq_ref, ckv_ref, kr_ref, c_ref, s_ref, rot_ref, place_ref, wq_ref, wk_ref, wv_ref, q_ref, k_ref, v_ref):
        cq = cq_ref[...]
        ckv = ckv_ref[...]
        kr_placed = _dot(kr_ref[...], place_ref[...].astype(BF16), 1, 0)
        for h in range(N_HEADS):
            q = _dot(cq, wq_ref[h], 1, 0)
            q_ref[h] = (q * c_ref[...] + _rot(q, rot_ref[...]) * s_ref[...]).astype(BF16)
            k_ref[h] = (_dot(ckv, wk_ref[h], 1, 0) + kr_placed).astype(BF16)
            v_ref[h] = _dot(ckv, wv_ref[h], 1, 0).astype(BF16)

    def row(w):
        return pl.BlockSpec((tm, w), lambda i: (i, 0))

    def whole(a):
        return pl.BlockSpec(a.shape, functools.partial(lambda i, nd: (0,) * nd, nd=a.ndim))

    def ospec(n):
        return pl.BlockSpec((N_HEADS, tm, n), lambda i: (0, i, 0))

    return pl.pallas_call(
        body, name="mla_heads", grid=(lp // tm,),
        in_specs=[row(Q_RANK), row(KV_RANK), row(D_ROPE), row(D_QK), row(D_QK), whole(rot_qk), whole(place),
                  whole(wq), whole(wk), whole(wv)],
        out_specs=[ospec(D_QK), ospec(D_QK), ospec(D_V)],
        out_shape=[jax.ShapeDtypeStruct((N_HEADS, lp, D_QK), BF16), jax.ShapeDtypeStruct((N_HEADS, lp, D_QK), BF16),
                   jax.ShapeDtypeStruct((N_HEADS, lp, D_V), BF16)],
        compiler_params=_cparams(),
    )(cqn, ckvn, kr, cos_qk, sin_qk, rot_qk, place, wq, wk, wv)


D_QK = D_NOPE + D_ROPE


def _att_probs(q, k, row0, tq, lp):
    s = _dot(q, k, 1, 1) * ATT_SCALE
    qi = row0 + lax.broadcasted_iota(jnp.int32, (tq, lp), 0)
    ki = lax.broadcasted_iota(jnp.int32, (tq, lp), 1)
    s = jnp.where((ki <= qi) & (ki >= PAD), s, -1e30)
    p = jnp.exp(s - jnp.max(s, axis=-1, keepdims=True))
    return p / jnp.sum(p, axis=-1, keepdims=True)


def _att_spec(lp, n):
    return pl.BlockSpec((None, lp, n), lambda h: (h, 0, 0))


def _att_tiles(lp):
    tiles, r = [(0, X0)], X0
    while r < lp:
        tiles.append((r, 256))
        r += 256
    assert r == lp
    return tiles


def attn_fwd(q, k, v, lp):
    def body(q_ref, k_ref, v_ref, o_ref):
        for r0, rows in _att_tiles(lp):
            ke, rq = r0 + rows, slice(r0, r0 + rows)
            p = _att_probs(q_ref[rq, :], k_ref[0:ke, :], r0, rows, ke)
            o_ref[rq, :] = _dot(p.astype(BF16), v_ref[0:ke, :], 1, 0).astype(BF16)

    return pl.pallas_call(
        body, name="attn_fwd", grid=(N_HEADS,),
        in_specs=[_att_spec(lp, D_QK), _att_spec(lp, D_QK), _att_spec(lp, D_V)],
        out_specs=_att_spec(lp, D_V), out_shape=jax.ShapeDtypeStruct((N_HEADS, lp, D_V), BF16),
        compiler_params=_cparams(),
    )(q, k, v)


def attn_bwd(q, k, v, do, lp):
    def body(q_ref, k_ref, v_ref, do_ref, dq_ref, dk_ref, dv_ref):
        dk_ref[...] = jnp.zeros_like(dk_ref)
        dv_ref[...] = jnp.zeros_like(dv_ref)
        for r0, rows in _att_tiles(lp):
            ke, rq = r0 + rows, slice(r0, r0 + rows)
            q_, do_, k_, v_ = q_ref[rq, :], do_ref[rq, :], k_ref[0:ke, :], v_ref[0:ke, :]
            p = _att_probs(q_, k_, r0, rows, ke)
            dp = _dot(do_, v_, 1, 1)
            delta = jnp.sum(p * dp, axis=-1, keepdims=True)
            ds = (p * (dp - delta) * ATT_SCALE).astype(BF16)
            dq_ref[rq, :] = _dot(ds, k_, 1, 0)
            dk_ref[0:ke, :] += _dot(ds, q_, 0, 0)
            dv_ref[0:ke, :] += _dot(p.astype(BF16), do_, 0, 0)

    qk, vv = _att_spec(lp, D_QK), _att_spec(lp, D_V)
    return pl.pallas_call(
        body, name="attn_bwd", grid=(N_HEADS,), in_specs=[qk, qk, vv, vv], out_specs=[qk, qk, vv],
        out_shape=[jax.ShapeDtypeStruct((N_HEADS, lp, D_QK), F32), jax.ShapeDtypeStruct((N_HEADS, lp, D_QK), F32),
                   jax.ShapeDtypeStruct((N_HEADS, lp, D_V), F32)],
        compiler_params=_cparams(),
    )(q, k, v, do)


def mla_heads_bwd(dq, dk, dv, cos_qk, sin_qk, rot_qk, place, wq, wk, wv, lp):
    tm = _row_tile(lp)

    def body(dq_ref, dk_ref, dv_ref, c_ref, s_ref, rot_ref, place_ref, wq_ref, wk_ref, wv_ref,
             dcq_ref, dckv_ref, dqp_ref, dkr_ref):
        dcq = jnp.zeros(dcq_ref.shape, F32)
        dckv = jnp.zeros(dckv_ref.shape, F32)
        dkr = jnp.zeros(dkr_ref.shape, F32)
        for h in range(N_HEADS):
            g = dq_ref[h]
            dqp = (g * c_ref[...] - _rot(g * s_ref[...], rot_ref[...])).astype(BF16)
            dqp_ref[h] = dqp
            dcq = dcq + _dot(dqp, wq_ref[h], 1, 1)
            dk_ = dk_ref[h]
            dckv = dckv + _dot(dk_.astype(BF16), wk_ref[h], 1, 1) + _dot(dv_ref[h].astype(BF16), wv_ref[h], 1, 1)
            dkr = dkr + _dot(dk_, place_ref[...], 1, 1, precision=lax.Precision.HIGHEST)
        dcq_ref[...] = dcq
        dckv_ref[...] = dckv
        dkr_ref[...] = dkr

    def hspec(n):
        return pl.BlockSpec((N_HEADS, tm, n), lambda i: (0, i, 0))

    def row(w):
        return pl.BlockSpec((tm, w), lambda i: (i, 0))

    def whole(a):
        return pl.BlockSpec(a.shape, functools.partial(lambda i, nd: (0,) * nd, nd=a.ndim))

    return pl.pallas_call(
        body, name="mla_heads_bwd", grid=(lp // tm,),
        in_specs=[hspec(D_QK), hspec(D_QK), hspec(D_V), row(D_QK), row(D_QK), whole(rot_qk), whole(place),
                  whole(wq), whole(wk), whole(wv)],
        out_specs=[row(Q_RANK), row(KV_RANK), hspec(D_QK), row(D_ROPE)],
        out_shape=[jax.ShapeDtypeStruct((lp, Q_RANK), F32), jax.ShapeDtypeStruct((lp, KV_RANK), F32),
                   jax.ShapeDtypeStruct((N_HEADS, lp, D_QK), BF16), jax.ShapeDtypeStruct((lp, D_ROPE), F32)],
        compiler_params=_cparams(),
    )(dq, dk, dv, cos_qk, sin_qk, rot_qk, place, wq, wk, wv)


def heads_out(o, wo, lp):
    tm = _row_tile(lp)

    def body(o_ref, w_ref, y_ref):
        acc = _dot(o_ref[0], w_ref[0], 1, 0)
        for h in range(1, N_HEADS):
            acc = acc + _dot(o_ref[h], w_ref[h], 1, 0)
        y_ref[...] = acc.astype(BF16)

    return pl.pallas_call(
        body, name="mla_out", grid=(lp // tm,),
        in_specs=[pl.BlockSpec((N_HEADS, tm, D_V), lambda i: (0, i, 0)), pl.BlockSpec(wo.shape, lambda i: (0, 0, 0))],
        out_specs=pl.BlockSpec((tm, D_MODEL), lambda i: (i, 0)), out_shape=jax.ShapeDtypeStruct((lp, D_MODEL), BF16),
        compiler_params=_cparams(),
    )(o, wo)


def heads_out_dx(dy, wo, lp):
    tm = _row_tile(lp)

    def body(dy_ref, w_ref, do_ref):
        dy_ = dy_ref[...]
        for h in range(N_HEADS):
            do_ref[h] = _dot(dy_, w_ref[h], 1, 1).astype(BF16)

    return pl.pallas_call(
        body, name="mla_out_dx", grid=(lp // tm,),
        in_specs=[pl.BlockSpec((tm, D_MODEL), lambda i: (i, 0)), pl.BlockSpec(wo.shape, lambda i: (0, 0, 0))],
        out_specs=pl.BlockSpec((N_HEADS, tm, D_V), lambda i: (0, i, 0)),
        out_shape=jax.ShapeDtypeStruct((N_HEADS, lp, D_V), BF16), compiler_params=_cparams(),
    )(dy, wo)


def _rms_bwd(dy, x, g):
    r = lax.rsqrt(jnp.mean(x * x, axis=-1, keepdims=True) + RMS_EPS)
    n = x * r
    dn = dy * g
    dx = r * (dn - n * jnp.mean(dn * n, axis=-1, keepdims=True))
    return dx, jnp.sum(dy * n, axis=0, keepdims=True)


def mla_prep_bwd(dcq, dckv, dkr, proj, cos, sin, rot, qg, kvg, lp):
    def fn(row0, rv, pv):
        dcq_, dckv_, dkr_, cq, ckv, c, s = rv
        qg_, kvg_, rot_ = pv
        dxq, dgq = _rms_bwd(dcq_, cq, qg_)
        dxkv, dgkv = _rms_bwd(dckv_, ckv, kvg_)
        dkr_raw = dkr_ * c - _rot(dkr_ * s, rot_)
        return ((dxq, dxkv, dkr_raw), (dgq, dgkv))

    return rowwise(fn, [(dcq, Q_RANK, 0), (dckv, KV_RANK, 0), (dkr, D_ROPE, 0), (proj, Q_RANK, 0), (proj, KV_RANK, 2),
                        (cos, D_ROPE, 0), (sin, D_ROPE, 0)], [qg, kvg, rot],
                   [(Q_RANK, BF16), (KV_RANK, BF16), (D_ROPE, BF16)], [((1, Q_RANK), F32), ((1, KV_RANK), F32)],
                   name="mla_prep_bwd", lp=lp)


def _shift_down(x, d, rows):
    return jnp.where(rows >= d, pltpu.roll(x, d, 0), 0.0)


def _shift_up(x, d, rows, n):
    return jnp.where(rows < n - d, pltpu.roll(x, n - d, 0), 0.0)


_CONV_W = 128
_XB, _BG, _CG = 1024 // _CONV_W, 1536 // _CONV_W, 2048 // _CONV_W


def _conv_specs(lp):
    def pspec(base):
        return pl.BlockSpec((lp, _CONV_W), functools.partial(lambda c, base: (0, base + c), base=base))

    col = pl.BlockSpec((lp, _CONV_W), lambda c: (0, c))
    wspec = pl.BlockSpec((3, _CONV_W), lambda c: (0, c))
    bspec = pl.BlockSpec((1, _CONV_W), lambda c: (0, c))
    return pspec, col, wspec, bspec


def _conv_core(xbar, cg, w, bias, lp):
    rows = lax.broadcasted_iota(jnp.int32, (lp, _CONV_W), 0)
    u = jnp.where(rows >= PAD, cg * xbar, 0.0)
    u1 = _shift_down(u, 1, rows)
    u2 = _shift_down(u, 2, rows)
    y = bias + w[0:1] * u2 + w[1:2] * u1 + w[2:3] * u
    return rows, u, u1, u2, y


def conv_fwd(proj, w, bias, lp):
    pspec, col, wspec, bspec = _conv_specs(lp)

    def body(x_ref, b_ref, c_ref, w_ref, bias_ref, v_ref):
        _, _, _, _, y = _conv_core(x_ref[...], c_ref[...], w_ref[...], bias_ref[...], lp)
        v_ref[...] = (b_ref[...] * y).astype(BF16)

    return pl.pallas_call(
        body, name="conv_fwd", grid=(MIX // _CONV_W,),
        in_specs=[pspec(_XB), pspec(_BG), pspec(_CG), wspec, bspec], out_specs=col,
        out_shape=jax.ShapeDtypeStruct((lp, MIX), BF16), compiler_params=_cparams(),
    )(proj, proj, proj, w, bias)


def conv_bwd(dv, proj, w, bias, lp):
    pspec, col, wspec, bspec = _conv_specs(lp)

    def body(dv_ref, x_ref, b_ref, c_ref, w_ref, bias_ref, dx_ref, db_ref, dc_ref, dw_ref, dbias_ref):
        xbar, cg, w_ = x_ref[...], c_ref[...], w_ref[...]
        rows, u, u1, u2, y = _conv_core(xbar, cg, w_, bias_ref[...], lp)
        dv_ = dv_ref[...]
        db_ref[...] = (dv_ * y).astype(BF16)
        dy = dv_ * b_ref[...]
        dbias_ref[...] = jnp.sum(dy, axis=0, keepdims=True)
        dw_ref[0:1, :] = jnp.sum(dy * u2, axis=0, keepdims=True)
        dw_ref[1:2, :] = jnp.sum(dy * u1, axis=0, keepdims=True)
        dw_ref[2:3, :] = jnp.sum(dy * u, axis=0, keepdims=True)
        du = w_[2:3] * dy + w_[1:2] * _shift_up(dy, 1, rows, lp) + w_[0:1] * _shift_up(dy, 2, rows, lp)
        du = jnp.where(rows >= PAD, du, 0.0)
        dc_ref[...] = (du * xbar).astype(BF16)
        dx_ref[...] = (du * cg).astype(BF16)

    return pl.pallas_call(
        body, name="conv_bwd", grid=(MIX // _CONV_W,),
        in_specs=[col, pspec(_XB), pspec(_BG), pspec(_CG), wspec, bspec],
        out_specs=[col, col, col, wspec, bspec],
        out_shape=[jax.ShapeDtypeStruct((lp, MIX), BF16)] * 3 + [jax.ShapeDtypeStruct((3, MIX), F32),
                                                                jax.ShapeDtypeStruct((1, MIX), F32)],
        compiler_params=_cparams(),
    )(dv, proj, proj, proj, w, bias)


def _s5_disc(a_re, a_im, log_dt, b_re, b_im):
    dt = jnp.exp(log_dt)
    mag = jnp.exp(dt * a_re)
    ab_re, ab_im = mag * jnp.cos(dt * a_im), mag * jnp.sin(dt * a_im)
    den = a_re * a_re + a_im * a_im
    nr, ni = ab_re - 1.0, ab_im
    coef_re = (nr * a_re + ni * a_im) / den
    coef_im = (ni * a_re - nr * a_im) / den
    return ab_re, ab_im, coef_re * b_re - coef_im * b_im, coef_re * b_im + coef_im * b_re


_S5_ROWS = S5_GROUPS * S5_GROUP


def s5_prep(a_re, a_im, log_dt, b_re, b_im):
    def body(ar, ai, ld, br, bi, o0, o1, o2, o3):
        for o, v in zip((o0, o1, o2, o3), _s5_disc(ar[...], ai[...], ld[...], br[...], bi[...])):
            o[...] = v

    return pl.pallas_call(body, name="s5_prep",
                          out_shape=[jax.ShapeDtypeStruct((_S5_ROWS, S5_STATE), F32)] * 4)(a_re, a_im, log_dt, b_re, b_im)


def s5_prep_bwd(a_re, a_im, log_dt, b_re, b_im, d_ab_re, d_ab_im, d_bb_re, d_bb_im, sel):
    def body(ar, ai, ld, br, bi, g0, g1, g2, g3, sel_ref, da_re, da_im, dld, dbr, dbi):
        _, vjp = jax.vjp(_s5_disc, ar[...], ai[...], ld[...], br[...], bi[...])
        c_ar, c_ai, c_ld, c_br, c_bi = vjp((g0[...], g1[...], g2[...], g3[...]))
        s = sel_ref[...]
        hi = lax.Precision.HIGHEST
        da_re[...] = _dot(s, c_ar, 1, 0, precision=hi)
        da_im[...] = _dot(s, c_ai, 1, 0, precision=hi)
        dld[...] = jnp.sum(_dot(s, c_ld, 1, 0, precision=hi), axis=-1, keepdims=True)
        dbr[...] = c_br
        dbi[...] = c_bi

    g = jax.ShapeDtypeStruct((S5_GROUPS, S5_STATE), F32)
    full = jax.ShapeDtypeStruct((_S5_ROWS, S5_STATE), F32)
    return pl.pallas_call(body, name="s5_prep_bwd",
                          out_shape=[g, g, jax.ShapeDtypeStruct((S5_GROUPS, 1), F32), full, full],
                          )(a_re, a_im, log_dt, b_re, b_im, d_ab_re, d_ab_im, d_bb_re, d_bb_im, sel)


_SCAN_W = 128
_SCAN_STEPS = int(math.log2(SCAN_CHUNK))


def _cmul(ar, ai, br, bi):
    return ar * br - ai * bi, ar * bi + ai * br


def _scan_powers(ar, ai, reverse):
    pw = [(ar, ai)]
    for _ in range(_SCAN_STEPS):
        pw.append(_cmul(*pw[-1], *pw[-1]))
    rows = lax.broadcasted_iota(jnp.int32, (SCAN_CHUNK, ar.shape[-1]), 0)
    tr = jnp.broadcast_to(ar, rows.shape)
    ti = jnp.broadcast_to(ai, rows.shape)
    for k in range(_SCAN_STEPS):
        d = 2 ** k
        if reverse:
            live = rows < SCAN_CHUNK - d
            mr, mi = _cmul(tr, ti, _shift_up(tr, d, rows, SCAN_CHUNK), _shift_up(ti, d, rows, SCAN_CHUNK))
        else:
            live = rows >= d
            mr, mi = _cmul(tr, ti, _shift_down(tr, d, rows), _shift_down(ti, d, rows))
        tr = jnp.where(live, mr, tr)
        ti = jnp.where(live, mi, ti)
    return pw, rows, tr, ti


def s5_scan(bu, ab_re, ab_im, lp):
    n_chunks = lp // SCAN_CHUNK

    def body(bu_ref, ar_ref, ai_ref, s_ref):
        ar, ai = ar_ref[...], ai_ref[...]
        pw, rows, tr, ti = _scan_powers(ar, ai, False)

        def chunk(ci, carry):
            cr, cim = carry
            r0 = pl.multiple_of(ci * SCAN_CHUNK, SCAN_CHUNK)
            xr = bu_ref[0, pl.ds(r0, SCAN_CHUNK), :]
            xi = bu_ref[1, pl.ds(r0, SCAN_CHUNK), :]
            for k in range(_SCAN_STEPS):
                d = 2 ** k
                mr, mi = _cmul(pw[k][0], pw[k][1], _shift_down(xr, d, rows), _shift_down(xi, d, rows))
                xr, xi = xr + mr, xi + mi
            mr, mi = _cmul(tr, ti, cr, cim)
            xr, xi = xr + mr, xi + mi
            s_ref[0, pl.ds(r0, SCAN_CHUNK), :] = xr
            s_ref[1, pl.ds(r0, SCAN_CHUNK), :] = xi
            return xr[SCAN_CHUNK - 1:SCAN_CHUNK, :], xi[SCAN_CHUNK - 1:SCAN_CHUNK, :]

        zero = jnp.zeros((1, _SCAN_W), F32)
        lax.fori_loop(0, n_chunks, chunk, (zero, zero))

    spec = pl.BlockSpec((2, lp, _SCAN_W), lambda c: (0, 0, c))
    aspec = pl.BlockSpec((1, _SCAN_W), lambda c: (0, c))
    return pl.pallas_call(
        body, name="s5_scan", grid=(S5_LANES // _SCAN_W,), in_specs=[spec, aspec, aspec], out_specs=spec,
        out_shape=jax.ShapeDtypeStruct((2, lp, S5_LANES), F32), compiler_params=_cparams(),
    )(bu, ab_re, ab_im)


def s5_scan_bwd(ds, s, ab_re, ab_im, lp):
    n_chunks = lp // SCAN_CHUNK

    def body(ds_ref, s_ref, ar_ref, ai_ref, g_ref, da_ref):
        ar, ai = ar_ref[...], -ai_ref[...]
        pw, rows, tr, ti = _scan_powers(ar, ai, True)

        def chunk(k, carry):
            cr, cim, dar, dai = carry
            ci = n_chunks - 1 - k
            r0 = pl.multiple_of(ci * SCAN_CHUNK, SCAN_CHUNK)
            xr = ds_ref[0, pl.ds(r0, SCAN_CHUNK), :]
            xi = ds_ref[1, pl.ds(r0, SCAN_CHUNK), :]
            for j in range(_SCAN_STEPS):
                d = 2 ** j
                mr, mi = _cmul(pw[j][0], pw[j][1], _shift_up(xr, d, rows, SCAN_CHUNK), _shift_up(xi, d, rows, SCAN_CHUNK))
                xr, xi = xr + mr, xi + mi
            mr, mi = _cmul(tr, ti, cr, cim)
            xr, xi = xr + mr, xi + mi
            g_ref[0, pl.ds(r0, SCAN_CHUNK), :] = xr
            g_ref[1, pl.ds(r0, SCAN_CHUNK), :] = xi
            prev0 = pl.multiple_of(jnp.maximum(r0 - 8, 0), 8)
            live = (ci > 0).astype(F32)
            pr = s_ref[0, pl.ds(prev0, 8), :][7:8, :] * live
            pim = s_ref[1, pl.ds(prev0, 8), :][7:8, :] * live
            sr = s_ref[0, pl.ds(r0, SCAN_CHUNK), :]
            si = s_ref[1, pl.ds(r0, SCAN_CHUNK), :]
            sr = jnp.where(rows >= 1, pltpu.roll(sr, 1, 0), pr)
            si = jnp.where(rows >= 1, pltpu.roll(si, 1, 0), pim)
            dar = dar + jnp.sum(xr * sr + xi * si, axis=0, keepdims=True)
            dai = dai + jnp.sum(xi * sr - xr * si, axis=0, keepdims=True)
            return xr[0:1, :], xi[0:1, :], dar, dai

        zero = jnp.zeros((1, _SCAN_W), F32)
        _, _, dar, dai = lax.fori_loop(0, n_chunks, chunk, (zero, zero, zero, zero))
        da_ref[0] = dar
        da_ref[1] = dai

    spec = pl.BlockSpec((2, lp, _SCAN_W), lambda c: (0, 0, c))
    aspec = pl.BlockSpec((1, _SCAN_W), lambda c: (0, c))
    return pl.pallas_call(
        body, name="s5_scan_bwd", grid=(S5_LANES // _SCAN_W,), in_specs=[spec, spec, aspec, aspec],
        out_specs=[spec, pl.BlockSpec((2, 1, _SCAN_W), lambda c: (0, 0, c))],
        out_shape=[jax.ShapeDtypeStruct((2, lp, S5_LANES), F32), jax.ShapeDtypeStruct((2, 1, S5_LANES), F32)],
        compiler_params=_cparams(),
    )(ds, s, ab_re, ab_im)


S5_BLOCKS = 4
_S5_PER = S5_GROUPS // S5_BLOCKS


def _blockdiag(x):
    _, r, c = x.shape
    eye = jnp.eye(_S5_PER, dtype=x.dtype)
    x = x.reshape(S5_BLOCKS, _S5_PER, r, c)
    return (x[:, :, :, None, :] * eye[None, :, None, :, None]).reshape(S5_BLOCKS, _S5_PER * r, _S5_PER * c)


def _blockdiag_extract(m, r, c):
    return jnp.einsum('qgrgc->qgrc', m.reshape(S5_BLOCKS, _S5_PER, r, _S5_PER, c)).reshape(S5_GROUPS, r, c)


def bd_matmul(a, w, *, w_t, reduce, res=None, name):
    _, M, _ = a.shape
    n_w, _, k1, k2 = w.shape
    ka, kout = (k2, k1) if w_t else (k1, k2)
    tm = _row_tile(M)
    n_out, n_red = (1, n_w) if reduce else (n_w, 1)
    has_res = res is not None

    assert n_red <= 2

    def body(*refs):
        a_ref, w_ref = refs[0], refs[1]
        o_ref = refs[3] if has_res else refs[2]
        for q in range(S5_BLOCKS):
            cols = slice(q * kout, (q + 1) * kout)
            part = _dot(a_ref[:, q * ka:(q + 1) * ka].astype(BF16), w_ref[q], 1, 1 if w_t else 0)
            if n_red == 1:
                o_ref[:, cols] = part
            else:
                acc_ref = refs[-1]

                @pl.when(pl.program_id(2) == 0)
                def _():
                    acc_ref[:, cols] = part

                @pl.when(pl.program_id(2) == 1)
                def _():
                    tot = acc_ref[:, cols] + part
                    o_ref[:, cols] = tot + refs[2][:, cols] if has_res else tot

    if reduce:
        a_map, w_map = (lambda o, i, r: (r, i, 0)), (lambda o, i, r: (r, 0, 0, 0))
    else:
        a_map, w_map = (lambda o, i, r: (0, i, 0)), (lambda o, i, r: (o, 0, 0, 0))
    o_map = lambda o, i, r: (o, i, 0)
    in_specs = [pl.BlockSpec((None, tm, S5_BLOCKS * ka), a_map), pl.BlockSpec((None, S5_BLOCKS, k1, k2), w_map)]
    operands = [a, w]
    if has_res:
        in_specs.append(pl.BlockSpec((None, tm, S5_BLOCKS * kout), o_map))
        operands.append(res)
    return pl.pallas_call(
        body, name=name, grid=(n_out, M // tm, n_red), in_specs=in_specs,
        out_specs=pl.BlockSpec((None, tm, S5_BLOCKS * kout), o_map),
        out_shape=jax.ShapeDtypeStruct((n_out, M, S5_BLOCKS * kout), F32),
        scratch_shapes=[pltpu.VMEM((tm, S5_BLOCKS * kout), F32)] if n_red > 1 else [],
        compiler_params=_cparams(),
    )(*operands)


def bd_outer(a, b, name):
    na, M, wa = a.shape
    nb_, _, wb = b.shape
    ka, kb = wa // S5_BLOCKS, wb // S5_BLOCKS
    n_out = max(na, nb_)

    def body(a_ref, b_ref, o_ref):
        o_ref[...] = _dot(a_ref[...].astype(BF16), b_ref[...].astype(BF16), 0, 0)

    return pl.pallas_call(
        body, name=name, grid=(n_out, S5_BLOCKS),
        in_specs=[pl.BlockSpec((None, M, ka), (lambda o, q: (o, 0, q)) if na > 1 else (lambda o, q: (0, 0, q))),
                  pl.BlockSpec((None, M, kb), (lambda o, q: (o, 0, q)) if nb_ > 1 else (lambda o, q: (0, 0, q)))],
        out_specs=pl.BlockSpec((None, None, ka, kb), lambda o, q: (o, q, 0, 0)),
        out_shape=jax.ShapeDtypeStruct((n_out, S5_BLOCKS, ka, kb), F32), compiler_params=_cparams(),
    )(a, b)


def s5_u(proj, lp):
    def fn(row0, rv, pv):
        u, = rv
        return ((jnp.where(_row_mask(row0, u.shape), u, 0.0),), ())

    return rowwise(fn, [(proj, MIX, 5)], [], [(MIX, BF16)], name="s5_u", lp=lp)[0]


def s5_y(ys, proj, d, lp):
    def fn(row0, rv, pv):
        ys_, u = rv
        y = ys_ + pv[0] * u
        return ((y, _gelu(y)), ())

    return rowwise(fn, [(ys, MIX, 0), (proj, MIX, 5)], [d], [(MIX, F32), (MIX, BF16)], name="s5_y", lp=lp)


def s5_glu(z, y, b, lp):
    def fn(row0, rv, pv):
        z_, y_ = rv
        return ((_gelu(y_) * _sigmoid(z_ + pv[0]),), ())

    return rowwise(fn, [(z, MIX, 0), (y, MIX, 0)], [b], [(MIX, BF16)], name="s5_glu", lp=lp)[0]


def s5_glu_bwd(dgl, z, y, b, lp):
    def fn(row0, rv, pv):
        dgl_, z_, y_ = rv
        sg = _sigmoid(z_ + pv[0])
        dz = dgl_ * _gelu(y_) * sg * (1.0 - sg)
        return ((dgl_ * sg, dz), (jnp.sum(dz, axis=0, keepdims=True),))

    return rowwise(fn, [(dgl, MIX, 0), (z, MIX, 0), (y, MIX, 0)], [b], [(MIX, F32), (MIX, BF16)], [((1, MIX), F32)],
                   name="s5_glu_bwd", lp=lp)


def s5_y_bwd(dyg, y, proj, d, lp):
    def fn(row0, rv, pv):
        dyg_, y_, u = rv
        dy = dyg_ * _gelu_grad(y_)
        return ((dy, dy * pv[0]), (jnp.sum(dy * u, axis=0, keepdims=True),))

    return rowwise(fn, [(dyg, MIX, 0), (y, MIX, 0), (proj, MIX, 5)], [d], [(MIX, BF16), (MIX, F32)], [((1, MIX), F32)],
                   name="s5_y_bwd", lp=lp)


def s5_du(du, lp):
    def fn(row0, rv, pv):
        return ((jnp.where(_row_mask(row0, rv[0].shape), rv[0], 0.0),), ())

    return rowwise(fn, [(du, MIX, 0)], [], [(MIX, BF16)], name="s5_du", lp=lp)[0]


def merge_fwd(proj, ya, yb, yc, lp):
    def fn(row0, rv, pv):
        g0, g1, g2, a, b, c = rv
        return ((_sigmoid(g0) * a + _sigmoid(g1) * b + _sigmoid(g2) * c,), ())

    return rowwise(fn, [(proj, D_MODEL, 3), (proj, D_MODEL, 4), (proj, D_MODEL, 5), (ya, D_MODEL, 0), (yb, D_MODEL, 0),
                        (yc, D_MODEL, 0)], [], [(D_MODEL, BF16)], name="merge_fwd", lp=lp)[0]


def merge_bwd(dmix, proj, ya, yb, yc, lp):
    def fn(row0, rv, pv):
        dm, g0, g1, g2, a, b, c = rv
        outs_y, outs_g = [], []
        for g, yv in ((g0, a), (g1, b), (g2, c)):
            sg = _sigmoid(g)
            outs_y.append(dm * sg)
            outs_g.append(dm * yv * sg * (1.0 - sg))
        return (tuple(outs_y) + (jnp.concatenate(outs_g, axis=1),), ())

    return rowwise(fn, [(dmix, D_MODEL, 0), (proj, D_MODEL, 3), (proj, D_MODEL, 4), (proj, D_MODEL, 5),
                        (ya, D_MODEL, 0), (yb, D_MODEL, 0), (yc, D_MODEL, 0)], [],
                   [(D_MODEL, BF16)] * 3 + [(P_IN // 2, BF16, P_IN, 1)], name="merge_bwd", lp=lp)


def dproj_fill(dproj, dcq, dkr, dckv, dxbar, dbg, dcg, du, place, lp):
    tm = _row_tile(lp)

    def body(dproj_ref, dcq_ref, dkr_ref, dckv_ref, dx_ref, db_ref, dc_ref, du_ref, place_ref, o_ref):
        o_ref[:, 0:384] = dcq_ref[...]
        o_ref[:, 384:512] = _dot(dkr_ref[...], place_ref[...], 1, 0).astype(BF16)
        o_ref[:, 512:768] = dckv_ref[...]
        o_ref[:, 768:1024] = jnp.zeros((tm, 256), BF16)
        o_ref[:, 1024:1536] = dx_ref[...]
        o_ref[:, 1536:2048] = db_ref[...]
        o_ref[:, 2048:2560] = dc_ref[...]
        o_ref[:, 2560:3072] = du_ref[...]

    def row(w):
        return pl.BlockSpec((tm, w), lambda i: (i, 0))

    return pl.pallas_call(
        body, name="dproj_fill", grid=(lp // tm,),
        in_specs=[pl.BlockSpec(memory_space=pl.ANY), row(Q_RANK), row(D_ROPE), row(KV_RANK), row(MIX), row(MIX), row(MIX),
                  row(MIX), pl.BlockSpec(place.shape, lambda i: (0, 0))],
        out_specs=row(P_IN // 2), out_shape=jax.ShapeDtypeStruct((lp, P_IN), BF16),
        input_output_aliases={0: 0}, compiler_params=_cparams(),
    )(dproj, dcq, dkr, dckv, dxbar, dbg, dcg, du, place)


def loss_head(h, tgt, lp):
    def fn(row0, rv, pv):
        h_, t_ = rv
        live = (row0 + lax.broadcasted_iota(jnp.int32, h_.shape, 0)) >= X0
        diff = jnp.where(live, h_ - t_, 0.0)
        ssq = jnp.sum(jnp.sum(diff * diff, axis=1, keepdims=True), axis=0, keepdims=True)
        return ((diff * (1.0 / D_MODEL),), (ssq * (0.5 / D_MODEL),))

    return rowwise(fn, [(h, D_MODEL, 0), (tgt, D_MODEL, 0)], [], [(D_MODEL, F32)], [((1, 1), F32)], name="loss_head", lp=lp)


def _s5_consts(W):
    ab_re_rep, ab_im_rep, bb_re, bb_im = s5_prep(W['s5_a_re'], W['s5_a_im'], W['s5_log_dt'], W['s5_b_re'], W['s5_b_im'])
    pick = lambda t: t.reshape(S5_GROUPS, S5_GROUP, S5_STATE)[:, 0].reshape(1, S5_LANES)
    bb = jnp.stack([_blockdiag(bb_re.reshape(S5_GROUPS, S5_GROUP, S5_STATE)),
                    _blockdiag(bb_im.reshape(S5_GROUPS, S5_GROUP, S5_STATE))]).astype(BF16)
    return pick(ab_re_rep), pick(ab_im_rep), bb


def layer_fwd(h, hb, W, tabs, lp, ffn1=None, ffn2=True):
    cos, sin, rot = tabs[:3]
    h1, h1b, sv1 = ffn1 if ffn1 is not None else ffn_fwd(h, hb, W['wg1'], W['wu1'], W['wd1'], W['ln1_g'], W['ln1_b'], lp)
    proj = matmul(h1b[None], W['w_in'][None], tb=True, name="proj")[0]
    cqn, ckvn, kr = mla_prep(proj, cos, sin, rot, W['q_norm_g'], W['kv_norm_g'], lp)
    q96, k96, v = mla_heads(cqn, ckvn, kr, *tabs[3:], W['wq'], W['wk'], W['wv'], lp)
    o = attn_fwd(q96, k96, v, lp)
    ya = heads_out(o, W['mla_wo'], lp)
    vconv = conv_fwd(proj, W['conv_w'], W['conv_b'], lp)
    yb = matmul(vconv[None], W['conv_wout'][None], out_dtype=BF16, name="conv_out")[0]
    ub = s5_u(proj, lp)
    ab_re, ab_im, bb = _s5_consts(W)
    bu = bd_matmul(ub[None], bb, w_t=False, reduce=False, name="s5_bu")
    s = s5_scan(bu, ab_re, ab_im, lp)
    ys = bd_matmul(s, W['s5_ct'], w_t=False, reduce=True, name="s5_cs")[0]
    y, ygb = s5_y(ys, proj, W['s5_d'], lp)
    zg = matmul(ygb[None], W['s5_wglu'][None], name="s5_glu_mm")[0]
    glb = s5_glu(zg, y, W['s5_b_glu'], lp)
    yc = matmul(glb[None], W['s5_wout'][None], out_dtype=BF16, name="s5_out")[0]
    mixed = merge_fwd(proj, ya, yb, yc, lp)
    z2, h2, h2b = mm_res_ln(mixed[None], W['w_o'][None], h1, W['ln2_g'], W['ln2_b'], scale=1.0, name="wo_ln", lp=lp)
    sv = dict(sv1=sv1, h1b=h1b, proj=proj, cqn=cqn, ckvn=ckvn, q96=q96, k96=k96, v=v, o=o, ya=ya,
              vconv=vconv, yb=yb, ub=ub, ab_re=ab_re, ab_im=ab_im, bb=bb, s=s, y=y, ygb=ygb, zg=zg, glb=glb, yc=yc,
              mixed=mixed, z2=z2)
    if not ffn2:
        return h2, h2b, sv
    h3, h3b, sv['sv3'] = ffn_fwd(h2, h2b, W['wg2'], W['wu2'], W['wd2'], W['ln3_g'], W['ln3_b'], lp)
    return h3, h3b, sv


def layer_bwd(dh3, sv, W, tabs, lp, ffn1=True, ffn2=True):
    cos, sin, rot = tabs[:3]
    proj = sv['proj']
    G = {}
    dh2 = dh3
    if ffn2:
        dh2, g3 = ffn_bwd(dh3, sv['sv3'], W['wg2'], W['wu2'], W['wd2'], W['ln3_g'], lp)
        G.update(wg2=g3['wg'], wu2=g3['wu'], wd2=g3['wd'], ln3_g=g3['ln_g'], ln3_b=g3['ln_b'])
    dz2, dz2b, G['ln2_g'], G['ln2_b'] = ln_bwd(dh2, sv['z2'], W['ln2_g'], fscale=1.0, name="wo_ln_bwd", lp=lp)
    dmix = matmul(dz2b[None], W['w_o'][None], tb=True, name="wo_dx")[0]
    G['w_o'] = matmul(sv['mixed'][None], dz2b[None], ta=True, out_dtype=WGRAD, name="wo_dw")[0]
    dya, dyb, dyc, dproj = merge_bwd(dmix, proj, sv['ya'], sv['yb'], sv['yc'], lp)
    dgl = matmul(dyc[None], W['s5_wout'][None], tb=True, name="s5_out_dx")[0]
    G['s5_wout'] = matmul(sv['glb'][None], dyc[None], ta=True, out_dtype=WGRAD, name="s5_out_dw")[0]
    t1, dzb, G['s5_b_glu'] = s5_glu_bwd(dgl, sv['zg'], sv['y'], W['s5_b_glu'], lp)
    dyg = matmul(dzb[None], W['s5_wglu'][None], tb=True, res=t1[None], name="s5_glu_dx")[0]
    G['s5_wglu'] = matmul(sv['ygb'][None], dzb[None], ta=True, out_dtype=WGRAD, name="s5_glu_dw")[0]
    dyb_, du_d, G['s5_d'] = s5_y_bwd(dyg, sv['y'], proj, W['s5_d'], lp)
    ds = bd_matmul(dyb_[None], W['s5_ct'], w_t=True, reduce=False, name="s5_cs_dx")
    G['s5_ct'] = bd_outer(sv['s'], dyb_[None], "s5_cs_dw")
    g_adj, d_ab = s5_scan_bwd(ds, sv['s'], sv['ab_re'], sv['ab_im'], lp)
    du = bd_matmul(g_adj, sv['bb'], w_t=True, reduce=True, res=du_d[None], name="s5_bu_dx")[0]
    d_bb = bd_outer(sv['ub'][None], g_adj, "s5_bu_dw")
    du_b = s5_du(du, lp)
    onehot = (jnp.arange(S5_GROUP) == 0).astype(F32)
    spread = lambda t: (t.reshape(S5_GROUPS, 1, S5_STATE) * onehot[None, :, None]).reshape(_S5_ROWS, S5_STATE)
    take = lambda t: _blockdiag_extract(t, S5_GROUP, S5_STATE).reshape(_S5_ROWS, S5_STATE)
    sel = jnp.kron(jnp.eye(S5_GROUPS, dtype=F32), jnp.ones((1, S5_GROUP), F32))
    (G['s5_a_re'], G['s5_a_im'], G['s5_log_dt'], G['s5_b_re'], G['s5_b_im']) = s5_prep_bwd(
        W['s5_a_re'], W['s5_a_im'], W['s5_log_dt'], W['s5_b_re'], W['s5_b_im'],
        spread(d_ab[0]), spread(d_ab[1]), take(d_bb[0]), take(d_bb[1]), sel)
    dv = matmul(dyb[None], W['conv_wout'][None], tb=True, name="conv_out_dx")[0]
    G['conv_wout'] = matmul(sv['vconv'][None], dyb[None], ta=True, out_dtype=WGRAD, name="conv_out_dw")[0]
    dxbar, dbg, dcg, G['conv_w'], G['conv_b'] = conv_bwd(dv, proj, W['conv_w'], W['conv_b'], lp)
    do = heads_out_dx(dya, W['mla_wo'], lp)
    G['mla_wo'] = matmul(sv['o'], dya[None], ta=True, ab='o', out_dtype=WGRAD, name="mla_out_dw")
    dq96, dk96, dvv = attn_bwd(sv['q96'], sv['k96'], sv['v'], do, lp)
    dcq, dckv, dqp, dkr = mla_heads_bwd(dq96, dk96, dvv, *tabs[3:], W['wq'], W['wk'], W['wv'], lp)
    G['wq'] = matmul(sv['cqn'][None], dqp, ta=True, bb='o', out_dtype=WGRAD, name="mla_dwq")
    G['wk'] = matmul(sv['ckvn'][None], dk96, ta=True, bb='o', out_dtype=WGRAD, name="mla_dwk")
    G['wv'] = matmul(sv['ckvn'][None], dvv, ta=True, bb='o', out_dtype=WGRAD, name="mla_dwv")
    dcq_raw, dckv_raw, dkr_raw, G['q_norm_g'], G['kv_norm_g'] = mla_prep_bwd(
        dcq, dckv, dkr, proj, cos, sin, rot, W['q_norm_g'], W['kv_norm_g'], lp)
    dproj = dproj_fill(dproj, dcq_raw, dkr_raw, dckv_raw, dxbar, dbg, dcg, du_b, jnp.eye(D_ROPE, 128, dtype=BF16), lp)
    dh1 = matmul(dproj[None], W['w_in'][None], res=dz2[None], res_scale=ALPHA, name="proj_dx")[0]
    G['w_in'] = matmul(dproj[None], sv['h1b'][None], ta=True, out_dtype=WGRAD, name="proj_dw")[0]
    if not ffn1:
        return dh1, G
    dh0, g1 = ffn_bwd(dh1, sv['sv1'], W['wg1'], W['wu1'], W['wd1'], W['ln1_g'], lp)
    G.update(wg1=g1['wg'], wu1=g1['wu'], wd1=g1['wd'], ln1_g=g1['ln_g'], ln1_b=g1['ln_b'])
    return dh0, G


def _nat_cols(st):
    return jnp.transpose(st, (1, 0, 2)).reshape(st.shape[1], -1)


def _shard_cols(nat):
    k, n = nat.shape
    return jnp.transpose(nat.reshape(k, N_SHARD, n // N_SHARD), (1, 0, 2))


def _win_pad(wt):
    z = lambda n: jnp.zeros((n, wt.shape[1]), wt.dtype)
    return jnp.concatenate([wt[0:384], wt[640:672], z(96), wt[384:640], z(256), wt[672:]], axis=0)


def _win_unpad(wp):
    return jnp.concatenate([wp[0:384], wp[512:768], wp[384:416], wp[1024:]], axis=0)


_BIG = [('ffn1_w_gate', 'T'), ('ffn1_w_up', 'T'), ('ffn1_w_down', 0), ('w_in', 'T'), ('mla_w_uq', 1), ('mla_w_ukv', 1),
        ('mla_w_o', 1), ('conv_w_out', 1), ('s5_w_glu', 0), ('s5_w_out', 1), ('w_o', 0),
        ('ffn2_w_gate', 'T'), ('ffn2_w_up', 'T'), ('ffn2_w_down', 0)]
_REPL = ['ln1_g', 'ln1_b', 'mla_q_norm_g', 'mla_kv_norm_g', 'conv_b', 's5_a_re', 's5_a_im', 's5_log_dt', 's5_b_re',
         's5_b_im', 's5_c_re', 's5_c_im', 's5_d', 's5_b_glu', 'ln2_g', 'ln2_b', 'ln3_g', 'ln3_b']


def compute_weights(st, small):
    W = {}
    for t in ('1', '2'):
        if 'ffn%s_w_gate' % t in st:
            W['wg' + t], W['wu' + t], W['wd' + t] = (st['ffn%s_w_%s' % (t, p)] for p in ('gate', 'up', 'down'))
    if 'w_in' in st:
        W.update(_mixer_weights(st))
    if small is not None:
        W.update(_small_weights(small))
    return W


def _mixer_weights(st):
    W = {}
    W['w_in'] = _win_pad(st['w_in'].reshape(D_IN, D_MODEL))
    W['wq'] = jnp.transpose(_nat_cols(st['mla_w_uq']).reshape(Q_RANK, N_HEADS, D_QK), (1, 0, 2))
    ukv = jnp.transpose(_nat_cols(st['mla_w_ukv']).reshape(KV_RANK, N_HEADS, D_NOPE + D_V), (1, 0, 2))
    W['wk'] = jnp.concatenate([ukv[:, :, :D_NOPE], jnp.zeros((N_HEADS, KV_RANK, D_ROPE), ukv.dtype)], axis=2)
    W['wv'] = ukv[:, :, D_NOPE:]
    W['mla_wo'] = _nat_cols(st['mla_w_o']).reshape(N_HEADS, D_V, D_MODEL)
    W['conv_wout'] = _nat_cols(st['conv_w_out'])
    W['s5_wglu'] = st['s5_w_glu'].reshape(MIX, MIX)
    W['s5_wout'] = _nat_cols(st['s5_w_out'])
    W['w_o'] = st['w_o'].reshape(D_MODEL, D_MODEL)
    return W


def _small_weights(small):
    W = {}
    W['conv_w'] = small['conv_w']
    for n in ('ln1_g', 'ln1_b', 'ln2_g', 'ln2_b', 'ln3_g', 'ln3_b', 'conv_b', 's5_b_glu'):
        W[n] = small[n].reshape(1, -1)
    W['q_norm_g'] = small['mla_q_norm_g'].reshape(1, -1)
    W['kv_norm_g'] = small['mla_kv_norm_g'].reshape(1, -1)
    W['s5_d'] = small['s5_d'].reshape(1, MIX)
    rep = lambda t: jnp.repeat(t, S5_GROUP, axis=0)
    W['s5_a_re'], W['s5_a_im'] = rep(small['s5_a_re']), rep(small['s5_a_im'])
    W['s5_log_dt'] = jnp.broadcast_to(rep(small['s5_log_dt'].reshape(S5_GROUPS, 1)), (_S5_ROWS, S5_STATE))
    tr = lambda t: jnp.transpose(t, (0, 2, 1)).reshape(_S5_ROWS, S5_STATE)
    W['s5_b_re'], W['s5_b_im'] = tr(small['s5_b_re']), tr(small['s5_b_im'])
    ct = lambda t: _blockdiag(jnp.transpose(t, (0, 2, 1)))
    W['s5_ct'] = jnp.stack([ct(small['s5_c_re']), -ct(small['s5_c_im'])]).astype(BF16)
    return W


def reference_grads(G, ffn=True):
    R = {}
    for t in ('1', '2') if ffn else ():
        R['ffn%s_w_gate' % t] = G['wg' + t].reshape(D_FF, D_MODEL).T
        R['ffn%s_w_up' % t] = G['wu' + t].reshape(D_FF, D_MODEL).T
        R['ffn%s_w_down' % t] = G['wd' + t].reshape(D_FF, D_MODEL)
    R['w_in_t'] = _win_unpad(G['w_in'])
    if ffn:
        R['w_in'] = R['w_in_t'].T
    R['mla_w_uq'] = jnp.transpose(G['wq'], (1, 0, 2)).reshape(Q_RANK, -1)
    R['mla_w_ukv'] = jnp.transpose(jnp.concatenate([G['wk'][:, :, :D_NOPE], G['wv']], axis=2), (1, 0, 2)).reshape(KV_RANK, -1)
    R['mla_w_o'] = G['mla_wo'].reshape(N_HEADS * D_V, D_MODEL)
    R['conv_w'], R['conv_w_out'] = G['conv_w'], G['conv_wout']
    R['s5_w_glu'], R['s5_w_out'], R['w_o'] = G['s5_wglu'], G['s5_wout'], G['w_o']
    for n in ('ln1_g', 'ln1_b', 'ln2_g', 'ln2_b', 'ln3_g', 'ln3_b', 'conv_b', 's5_b_glu'):
        if n in G:
            R[n] = G[n].reshape(-1)
    R['mla_q_norm_g'], R['mla_kv_norm_g'] = G['q_norm_g'].reshape(-1), G['kv_norm_g'].reshape(-1)
    R['s5_d'] = G['s5_d'].reshape(S5_GROUPS, S5_GROUP)
    R['s5_a_re'], R['s5_a_im'], R['s5_log_dt'] = G['s5_a_re'], G['s5_a_im'], G['s5_log_dt'].reshape(-1)
    untr = lambda t: jnp.transpose(t.reshape(S5_GROUPS, S5_GROUP, S5_STATE), (0, 2, 1))
    R['s5_b_re'], R['s5_b_im'] = untr(G['s5_b_re']), untr(G['s5_b_im'])
    unct = lambda t: jnp.transpose(_blockdiag_extract(t, S5_STATE, S5_GROUP), (0, 2, 1))
    R['s5_c_re'], R['s5_c_im'] = unct(G['s5_ct'][0]), -unct(G['s5_ct'][1])
    return R


_ANY = pl.BlockSpec(memory_space=pl.ANY)
LANES = 1024


def _place():
    x, y, c = lax.axis_index("x"), lax.axis_index("y"), lax.axis_index("c")
    chips = [(1 - x, y), (x, 1 - y), (1 - x, 1 - y)]
    return x, y, c, chips


def _rows_of(c, half):
    return pl.ds(pl.multiple_of(c * half, 8), half)


def all_gather_shards(srcs, exact):
    n, m = len(srcs), len(exact)
    halves = [s.shape[0] // 2 for s in srcs]

    def body(*refs):
        s_refs, e_refs = refs[:n], refs[n:n + m]
        o_refs, eo_refs = refs[n + m:2 * n + m], refs[2 * n + m:2 * n + 2 * m]
        send, recv, esend, erecv, osend, orecv, lsem = refs[2 * n + 2 * m:]
        x, y, c, chips = _place()
        me = 2 * x + y
        sibling = (x, y, 1 - c)
        own = [pltpu.make_async_remote_copy(src_ref=s_refs[k], dst_ref=o_refs[k].at[me], send_sem=osend.at[k],
                                            recv_sem=orecv.at[k], device_id=sibling, device_id_type=MESH) for k in range(n)]
        local = [pltpu.make_async_copy(e_refs[k], eo_refs[k].at[me], lsem.at[k]) for k in range(m)]
        for cp in own + local:
            cp.start()

        def copy(k, s, src, idx, half_c, to):
            return pltpu.make_async_remote_copy(
                src_ref=src, dst_ref=o_refs[k].at[idx, _rows_of(half_c, halves[k])], send_sem=send.at[6 * k + s],
                recv_sem=recv.at[6 * k + s], device_id=to, device_id_type=MESH)

        def ecopy(k, j, idx, to):
            return pltpu.make_async_remote_copy(src_ref=e_refs[k], dst_ref=eo_refs[k].at[idx], send_sem=esend.at[3 * k + j],
                                                recv_sem=erecv.at[3 * k + j], device_id=to, device_id_type=MESH)

        sends = []
        for k in range(n):
            mine = s_refs[k].at[_rows_of(c, halves[k])]
            sends += [copy(k, j, mine, me, c, (*chip, c)) for j, chip in enumerate(chips)]
        for k in range(m):
            sends += [ecopy(k, j, me, (*chip, c)) for j, chip in enumerate(chips)]
        for cp in sends:
            cp.start()
        for j, chip in enumerate(chips):
            idx = 2 * chip[0] + chip[1]
            for k in range(n):
                landed = o_refs[k].at[idx, _rows_of(c, halves[k])]
                copy(k, j, landed, idx, c, sibling).wait_recv()
                fwd = copy(k, 3 + j, landed, idx, c, sibling)
                fwd.start()
                sends.append(fwd)
        for j, chip in enumerate(chips):
            idx = 2 * chip[0] + chip[1]
            for k in range(n):
                copy(k, 3 + j, s_refs[k].at[_rows_of(c, halves[k])], idx, 1 - c, sibling).wait_recv()
            for k in range(m):
                ecopy(k, j, idx, sibling).wait_recv()
        for cp in sends:
            cp.wait_send()
        for cp in own + local:
            cp.wait()

    outs = pl.pallas_call(
        body, name="all_gather_weights", in_specs=[_ANY] * (n + m), out_specs=[_ANY] * (n + m),
        out_shape=[jax.ShapeDtypeStruct((N_SHARD,) + a.shape, a.dtype) for a in list(srcs) + list(exact)],
        scratch_shapes=[pltpu.SemaphoreType.DMA((6 * n,)), pltpu.SemaphoreType.DMA((6 * n,)),
                        pltpu.SemaphoreType.DMA((3 * m,)), pltpu.SemaphoreType.DMA((3 * m,)),
                        pltpu.SemaphoreType.DMA((n,)), pltpu.SemaphoreType.DMA((n,)), pltpu.SemaphoreType.DMA((m,))],
    )(*srcs, *exact)
    return outs[:n], outs[n:]


def rs_pair_swap(gs):
    n = len(gs)

    def body(*refs):
        g_refs, r_refs, send, recv = refs[:n], refs[n:2 * n], refs[2 * n], refs[2 * n + 1]
        x, y, c, _ = _place()
        copies = [pltpu.make_async_remote_copy(
            src_ref=g_refs[k].at[pl.ds(0, N_SHARD), _rows_of(1 - c, gs[k].shape[1] // 2)], dst_ref=r_refs[k],
            send_sem=send.at[k], recv_sem=recv.at[k], device_id=(x, y, 1 - c), device_id_type=MESH) for k in range(n)]
        for cp in copies:
            cp.start()
        for cp in copies:
            cp.wait()

    return pl.pallas_call(
        body, name="grad_pair_swap", in_specs=[_ANY] * n, out_specs=[_ANY] * n,
        out_shape=[jax.ShapeDtypeStruct((N_SHARD, g.shape[1] // 2, g.shape[2]), g.dtype) for g in gs],
        scratch_shapes=[pltpu.SemaphoreType.DMA((n,)), pltpu.SemaphoreType.DMA((n,))],
    )(*gs)


def _group_tile(half, n_cols, n_arrays):
    budget = (20 * 2 ** 20) // (6 * n_arrays)
    fits = [t for t in range(8, half + 1, 8) if half % t == 0 and t * n_cols * 4 <= budget]
    return max(fits) if fits else 8


def rs_pair_add(gs, rs, cidx, out_dtype, name):
    n = len(gs)
    _, K, cols = gs[0].shape
    half = K // 2
    tr = _group_tile(half, cols, n)
    nb = half // tr

    def body(c_ref, *refs):
        for g_ref, r_ref, o_ref in zip(refs[:n], refs[n:2 * n], refs[2 * n:]):
            o_ref[...] = (g_ref[...].astype(F32) + r_ref[...].astype(F32)).astype(out_dtype)

    gspec = pl.BlockSpec((None, tr, cols), lambda j, i, c: (j, c[0] * nb + i, 0))
    rspec = pl.BlockSpec((None, tr, cols), lambda j, i, c: (j, i, 0))
    return pl.pallas_call(
        body, name=name,
        grid_spec=pltpu.PrefetchScalarGridSpec(num_scalar_prefetch=1, grid=(N_SHARD, nb), in_specs=[gspec] * n + [rspec] * n,
                                               out_specs=[rspec] * n),
        out_shape=[jax.ShapeDtypeStruct((N_SHARD, half, cols), out_dtype)] * n,
        compiler_params=_cparams(),
    )(cidx, *gs, *rs)


def rs_chip_sum(qs, nl, cidx, name):
    n = len(qs)
    _, half, cols = qs[0].shape
    tr = _group_tile(half, cols, n)
    nb = half // tr

    def body(c_ref, *refs):
        for k, q_ref in enumerate(refs[:n]):
            o_ref = refs[n + k // nl]
            o_ref[k % nl] = ((q_ref[0].astype(F32) + q_ref[1].astype(F32)) + q_ref[2].astype(F32)) + q_ref[3].astype(F32)

    return pl.pallas_call(
        body, name=name,
        grid_spec=pltpu.PrefetchScalarGridSpec(
            num_scalar_prefetch=1, grid=(nb,),
            in_specs=[pl.BlockSpec((N_SHARD, tr, cols), lambda i, c: (0, i, 0))] * n,
            out_specs=[pl.BlockSpec((nl, tr, cols), lambda i, c: (0, c[0] * nb + i, 0))] * (n // nl)),
        out_shape=[jax.ShapeDtypeStruct((nl, 2 * half, cols), F32)] * (n // nl),
        compiler_params=_cparams(),
    )(cidx, *qs)


def rs_pair_gather(fs, name):
    n = len(fs)

    def body(*refs):
        f_refs, send, recv = refs[n:2 * n], refs[2 * n], refs[2 * n + 1]
        x, y, c, _ = _place()
        copies = []
        for k in range(n):
            rows = f_refs[k].at[pl.ds(0, fs[k].shape[0]), _rows_of(c, fs[k].shape[1] // 2)]
            copies.append(pltpu.make_async_remote_copy(src_ref=rows, dst_ref=rows, send_sem=send.at[k], recv_sem=recv.at[k],
                                                       device_id=(x, y, 1 - c), device_id_type=MESH))
        for cp in copies:
            cp.start()
        for cp in copies:
            cp.wait()

    return pl.pallas_call(
        body, name=name, in_specs=[_ANY] * n, out_specs=[_ANY] * n,
        out_shape=[jax.ShapeDtypeStruct(f.shape, f.dtype) for f in fs],
        input_output_aliases={k: k for k in range(n)},
        scratch_shapes=[pltpu.SemaphoreType.DMA((n,)), pltpu.SemaphoreType.DMA((n,))],
    )(*fs)


_HBM = pl.BlockSpec(memory_space=pltpu.HBM)
_SEM = pl.BlockSpec(memory_space=pltpu.SEMAPHORE)
_EFFECT = pltpu.SideEffectType.DATAFLOW_SIDE_EFFECTING


def _in_hbm(a):
    return pltpu.with_memory_space_constraint(a, pltpu.HBM)


def split_start(name, srcs, lands, after, copies_fn, n_copies):
    n = len(srcs)

    def body(*refs):
        for cp in copies_fn(refs[:n], refs[n:2 * n], refs[2 * n + 1], refs[2 * n + 2]):
            cp.start()
        refs[-1][...] = jnp.zeros_like(refs[-1])

    bufs = list(srcs) + list(lands)
    outs = pl.pallas_call(
        body, name=name,
        out_shape=(pltpu.SemaphoreType.DMA((n_copies,)), pltpu.SemaphoreType.DMA((n_copies,)),
                   *[pltpu.HBM(a.shape, a.dtype) for a in bufs], jax.ShapeDtypeStruct((8, 128), F32)),
        in_specs=[_HBM] * (2 * n) + [_ANY],
        out_specs=(_SEM, _SEM, *[_HBM] * (2 * n), pl.BlockSpec(memory_space=pltpu.VMEM)),
        input_output_aliases={i: 2 + i for i in range(2 * n)},
        compiler_params=pltpu.CompilerParams(has_side_effects=_EFFECT),
    )(*[_in_hbm(a) for a in bufs], after)
    return outs[0], outs[1], outs[2:2 + n], outs[2 + n:2 + 2 * n], outs[-1]


def split_wait(name, send, recv, srcs, lands, after, copies_fn, which=None):
    n = len(srcs)

    def body(*refs):
        copies = copies_fn(refs[:n], refs[n:2 * n], refs[2 * n], refs[2 * n + 1], which)
        for cp in copies:
            cp.wait_send()
        for cp in copies:
            cp.wait_recv()

    bufs = list(srcs) + list(lands)
    outs = pl.pallas_call(
        body, name=name, out_shape=tuple(pltpu.HBM(a.shape, a.dtype) for a in bufs),
        in_specs=[_HBM] * (2 * n) + [_SEM, _SEM, _ANY], out_specs=tuple([_HBM] * (2 * n)),
        input_output_aliases={i: i for i in range(2 * n)},
        compiler_params=pltpu.CompilerParams(has_side_effects=_EFFECT),
    )(*bufs, send, recv, after)
    return list(outs[:n]), list(outs[n:])


def _gather_copies(s_refs, l_refs, send, recv, which=None):
    x, y, c, chips = _place()
    me = 2 * x + y
    out = []
    for k in (range(len(s_refs)) if which is None else which):
        s, l = s_refs[k], l_refs[k]
        rows = _rows_of(c, s.shape[0] // 2)
        for j, chip in enumerate(chips):
            out.append(pltpu.make_async_remote_copy(src_ref=s.at[rows], dst_ref=l.at[me, rows], send_sem=send.at[4 * k + j],
                                                    recv_sem=recv.at[4 * k + j], device_id=(*chip, c), device_id_type=MESH))
        out.append(pltpu.make_async_remote_copy(src_ref=s, dst_ref=l.at[me], send_sem=send.at[4 * k + 3],
                                                recv_sem=recv.at[4 * k + 3], device_id=(x, y, 1 - c), device_id_type=MESH))
    return out


def _scatter_copies(s_refs, l_refs, send, recv, which=None):
    x, y, c, chips = _place()
    me = 2 * x + y
    return [pltpu.make_async_remote_copy(src_ref=s_refs[k].at[2 * chip[0] + chip[1]], dst_ref=l_refs[k].at[me],
                                         send_sem=send.at[3 * k + j], recv_sem=recv.at[3 * k + j], device_id=(*chip, c),
                                         device_id_type=MESH)
            for k in (range(len(s_refs)) if which is None else which) for j, chip in enumerate(chips)]


def gather_forward(lands, name):
    n = len(lands)

    def body(*refs):
        l_refs, send, recv = refs[n:2 * n], refs[2 * n], refs[2 * n + 1]
        x, y, c, chips = _place()
        copies = []
        for k in range(n):
            rows = _rows_of(c, lands[k].shape[1] // 2)
            for j, chip in enumerate(chips):
                part = l_refs[k].at[2 * chip[0] + chip[1], rows]
                copies.append(pltpu.make_async_remote_copy(src_ref=part, dst_ref=part, send_sem=send.at[3 * k + j],
                                                           recv_sem=recv.at[3 * k + j], device_id=(x, y, 1 - c),
                                                           device_id_type=MESH))
        for cp in copies:
            cp.start()
        for cp in copies:
            cp.wait()

    return pl.pallas_call(
        body, name=name, in_specs=[_ANY] * n, out_specs=[_ANY] * n,
        out_shape=[jax.ShapeDtypeStruct(a.shape, a.dtype) for a in lands],
        input_output_aliases={k: k for k in range(n)},
        scratch_shapes=[pltpu.SemaphoreType.DMA((3 * n,)), pltpu.SemaphoreType.DMA((3 * n,))],
    )(*lands)


def rs_partials(gs, wire, cidx, tag):
    rs = rs_pair_swap(gs)
    groups = {}
    for k, g in enumerate(gs):
        groups.setdefault((g.shape, jnp.dtype(wire[k]).name), []).append(k)
    ps = [None] * len(gs)
    for gi, ks in enumerate(groups.values()):
        outs = rs_pair_add([gs[k] for k in ks], [rs[k] for k in ks], cidx, wire[ks[0]], "grad_pair_add_%s%d" % (tag, gi))
        for k, o in zip(ks, outs):
            ps[k] = o
    return ps


def rs_finish(items, tag):
    cidx = lax.axis_index("c").astype(jnp.int32).reshape(1)
    groups = {}
    for i, it in enumerate(items):
        groups.setdefault((it[0].shape, len(it), it[0].dtype.name), []).append(i)
    fs = [None] * len(items)
    for gi, ids in enumerate(groups.values()):
        outs = rs_chip_sum([q for i in ids for q in items[i]], len(items[ids[0]]), cidx, "grad_chip_sum_%s%d" % (tag, gi))
        for i, o in zip(ids, outs):
            fs[i] = o
    return rs_pair_gather(fs, "grad_pair_gather_" + tag)


def adamw(w, g, m, v, name):
    shape = w.shape
    if w.ndim == 2:
        block, grid, index = shape, (1,), (lambda i: (0, 0))
    else:
        slab = shape[2:]
        unit = 4 * int(np.prod(slab[:-2] or (1,))) * (-(-slab[-1] // 128) * 128)
        if len(slab) >= 2:
            unit *= -(-slab[-2] // 8) * 8
        k = shape[1]
        tr = k
        if k * unit > 2 ** 21:
            tr = max(t for t in range(8, k, 8) if k % t == 0 and t * unit <= 2 ** 21)
        block, grid = (None, tr) + tuple(slab), (shape[0], k // tr)
        index = lambda l, i: (l, i) + (0,) * len(slab)
        if tr < min(k, 64) and len(slab) == 1:
            tc = max(t for t in range(128, slab[0] + 1, 128) if slab[0] % t == 0 and k * t * 4 <= 2 ** 21)
            block, grid = (None, k, tc), (shape[0], slab[0] // tc)
            index = lambda l, i: (l, 0, i)

    def body(w_ref, g_ref, m_ref, v_ref, d_ref, nm_ref, nv_ref):
        g_ = g_ref[...]
        m_new = ADAM_B1 * m_ref[...] + (1.0 - ADAM_B1) * g_
        v_new = ADAM_B2 * v_ref[...] + (1.0 - ADAM_B2) * (g_ * g_)
        m_hat = m_new / (1.0 - ADAM_B1 ** ADAM_STEP)
        v_hat = v_new / (1.0 - ADAM_B2 ** ADAM_STEP)
        d_ref[...] = -ADAM_LR * (m_hat / (jnp.sqrt(v_hat) + ADAM_EPS) + ADAM_WD * w_ref[...])
        nm_ref[...] = m_new
        nv_ref[...] = v_new

    spec = pl.BlockSpec(block, index)
    return pl.pallas_call(
        body, name=name, grid=grid, in_specs=[spec] * 4, out_specs=[spec] * 3,
        out_shape=[jax.ShapeDtypeStruct(shape, F32)] * 3, compiler_params=_cparams(),
    )(w, g, m, v)


_WEIGHTS = ['meta', 'ffn1_w_gate', 'ffn1_w_up', 'ffn1_w_down', 'ln1_g', 'ln1_b', 'w_in', 'mla_q_norm_g', 'mla_w_uq',
            'mla_kv_norm_g', 'mla_w_ukv', 'mla_w_o', 'conv_w', 'conv_b', 'conv_w_out', 's5_a_re', 's5_a_im', 's5_log_dt',
            's5_b_re', 's5_b_im', 's5_c_re', 's5_c_im', 's5_d', 's5_w_glu', 's5_b_glu', 's5_w_out', 'w_o', 'ln2_g', 'ln2_b',
            'ffn2_w_gate', 'ffn2_w_up', 'ffn2_w_down', 'ln3_g', 'ln3_b']


def _pad_to(flat, n):
    return jnp.concatenate([flat, jnp.zeros((n - flat.shape[0],), flat.dtype)])


def _shard_of(full, axis):
    if axis == 1:
        return _shard_cols(full)
    if axis == 'T':
        return full.T.reshape(N_SHARD, full.shape[1] // N_SHARD, full.shape[0])
    return full.reshape(N_SHARD, full.shape[0] // N_SHARD, full.shape[1])


_FFN_KEY = {'gate': 'wg', 'up': 'wu', 'down': 'wd'}


def _pad_rows(a, axis):
    k = a.shape[axis]
    extra = -k % 32
    if not extra:
        return a
    return jnp.pad(a, [(0, extra) if d == axis else (0, 0) for d in range(a.ndim)])


def _step(env):
    w = {n: env[n] for n in _WEIGHTS}
    mom = {n: env['m_' + n] for n in _WEIGHTS}
    var = {n: env['v_' + n] for n in _WEIGHTS}
    cidx = lax.axis_index("c").astype(jnp.int32).reshape(1)
    chip = 2 * lax.axis_index("x") + lax.axis_index("y")
    big_names = [n for n, _ in _BIG]
    nb = len(big_names)

    kept_t = [n for n, a in _BIG if a == 'T']
    own = {n: (jnp.swapaxes(w[n], 1, 2) if n in kept_t else w[n]) for n in big_names}
    first = [n for n in big_names if n.startswith('ffn1')]
    mix = [n for n in big_names if not n.startswith('ffn')]
    last = [n for n in big_names if n.startswith('ffn2')]
    rest = mix + last
    nm, nr = len(mix), len(mix) + len(last)
    src = lambda n, li: _pad_rows(own[n][li].astype(BF16), 0)
    gathered_first, (conv_w_st, meta_st) = all_gather_shards([src(n, 0) for n in first], [w['conv_w'], w['meta']])
    later = [src(n, 0) for n in rest] + [src(n, 1) for n in big_names]
    lands = [lax.empty((N_SHARD,) + s.shape, BF16) for s in later]
    g_send, g_recv, later_t, lands_t, token = split_start("gather_start", later, lands, gathered_first[0], _gather_copies,
                                                          4 * len(later))

    def weights_of(names, st, li, with_small):
        small = None
        if with_small:
            small = {n: w[n][li] for n in _REPL}
            small['conv_w'] = _nat_cols(conv_w_st[:, li])
        return compute_weights({n: a[:, :own[n].shape[1]] for n, a in zip(names, st)}, small)

    x2d = env['x'][0]
    lp = x2d.shape[0] + X0
    tabs = _rope_tables(lp)
    h = jnp.concatenate([jnp.zeros((PAD, D_MODEL), F32), _nat_cols(meta_st), x2d], axis=0) + token[0, 0]
    W0 = weights_of(first, gathered_first, 0, True)
    ffn1 = ffn_fwd(h, h.astype(BF16), W0['wg1'], W0['wu1'], W0['wd1'], W0['ln1_g'], W0['ln1_b'], lp)
    later_t, lands_t = split_wait("gather0_wait", g_send, g_recv, later_t, lands_t, ffn1[0], _gather_copies, range(nm))
    W0.update(weights_of(mix, gather_forward(lands_t[:nm], "gather0_forward"), 0, False))
    h, hb, sv0 = layer_fwd(None, None, W0, tabs, lp, ffn1=ffn1, ffn2=False)
    later_t, lands_t = split_wait("gather0b_wait", g_send, g_recv, later_t, lands_t, h, _gather_copies, range(nm, nr))
    W0.update(weights_of(last, gather_forward(lands_t[nm:nr], "gather0b_forward"), 0, False))
    h, hb, sv0['sv3'] = ffn_fwd(h, hb, W0['wg2'], W0['wu2'], W0['wd2'], W0['ln3_g'], W0['ln3_b'], lp)
    _, lands_t = split_wait("gather1_wait", g_send, g_recv, later_t, lands_t, h, _gather_copies, range(nr, len(later)))
    W1 = weights_of(big_names, gather_forward(lands_t[nr:], "gather1_forward"), 1, True)
    h, hb, sv1 = layer_fwd(h, hb, W1, tabs, lp)
    tgt = jnp.concatenate([jnp.zeros((X0, D_MODEL), F32), env['loss_target'][0]], axis=0)
    dh, loss_part = loss_head(h, tgt, lp)
    loss = lax.psum(loss_part[0, 0], ("x", "y", "c"))

    def shards(G, names):
        full = None if all(n.startswith('ffn') for n in names) else reference_grads(G, ffn=False)

        def one(n, a):
            if n.startswith('ffn'):
                return G[_FFN_KEY[n.split('_')[-1]] + n[3]]
            if n == 'w_in':
                return full['w_in_t'].reshape(N_SHARD, D_IN // N_SHARD, D_MODEL)
            return _shard_of(full[n], a)

        return [_pad_rows(one(n, a), 1) for n, a in _BIG if n in names]

    def scatter_start(name, ps, after):
        qs = [lax.dynamic_update_slice_in_dim(jnp.zeros_like(p), lax.dynamic_slice_in_dim(p, chip, 1, axis=0), chip, axis=0)
              for p in ps]
        return split_start(name, ps, qs, after, _scatter_copies, 3 * len(ps))

    dh, G1 = layer_bwd(dh, sv1, W1, tabs, lp)
    p1 = rs_partials(shards(G1, big_names), [BF16] * nb, cidx, "b")
    s1_send, s1_recv, p1_t, q1_t, token1 = scatter_start("scatter1_start", p1, dh)
    dh, g3 = ffn_bwd(dh, sv0['sv3'], W0['wg2'], W0['wu2'], W0['wd2'], W0['ln3_g'] + token1[0, 0], lp)
    G0 = dict(wg2=g3['wg'], wu2=g3['wu'], wd2=g3['wd'])
    p0l = rs_partials(shards(G0, last), [BF16] * len(last), cidx, "c")
    sl_send, sl_recv, p0l_t, q0l_t, token0l = scatter_start("scatter0b_start", p0l, dh)
    dh, Gm = layer_bwd(dh, sv0, dict(W0, ln2_g=W0['ln2_g'] + token0l[0, 0]), tabs, lp, ffn1=False, ffn2=False)
    G0.update(Gm, ln3_g=g3['ln_g'], ln3_b=g3['ln_b'])
    p0m = rs_partials(shards(G0, mix), [BF16] * nm, cidx, "d")
    sm_send, sm_recv, p0m_t, q0m_t, token0m = scatter_start("scatter0_start", p0m, dh)
    dh, g1 = ffn_bwd(dh, sv0['sv1'], W0['wg1'], W0['wu1'], W0['wd1'], W0['ln1_g'] + token0m[0, 0], lp)
    G0.update(wg1=g1['wg'], wu1=g1['wu'], wd1=g1['wd'], ln1_g=g1['ln_g'], ln1_b=g1['ln_b'])
    _, q1 = split_wait("scatter1_wait", s1_send, s1_recv, p1_t, q1_t, dh, _scatter_copies)
    _, q0_last = split_wait("scatter0b_wait", sl_send, sl_recv, p0l_t, q0l_t, dh, _scatter_copies)
    _, q0_mix = split_wait("scatter0_wait", sm_send, sm_recv, p0m_t, q0m_t, dh, _scatter_copies)
    q0_rest = list(q0_mix) + list(q0_last)
    full = [reference_grads(G0, ffn=False), reference_grads(G1, ffn=False)]

    s_parts = [jnp.stack([full[li][n] for li in range(DEPTH)]).reshape(-1) for n in _REPL + ['conv_w']]
    s_parts.append(dh[PAD:X0].reshape(-1))
    s_sizes = [int(p.shape[0]) for p in s_parts]
    s_rows = -(-sum(s_sizes) // (16 * LANES)) * 16
    g_small = _pad_to(jnp.concatenate(s_parts), s_rows * LANES).reshape(1, s_rows, LANES)
    g_small = jnp.broadcast_to(g_small, (N_SHARD, s_rows, LANES))
    p_last = rs_partials(shards(G0, first) + [g_small], [BF16] * len(first) + [F32], cidx, "a")
    z_send, z_recv, pz_t, qz_t, token_z = scatter_start("scatter_last_start", p_last, dh)

    def step_weights(names, grad):
        out = {}
        for n in names:
            if n in kept_t:
                res = adamw(own[n], grad[n], jnp.swapaxes(mom[n], 1, 2), jnp.swapaxes(var[n], 1, 2), "adamw_" + n)
                out[n] = [jnp.swapaxes(t, 1, 2) for t in [grad[n]] + list(res)]
            else:
                out[n] = [grad[n]] + list(adamw(w[n], grad[n], mom[n], var[n], "adamw_" + n))
        return out

    q1 = dict(zip(big_names, q1))
    q0 = dict(zip(rest, q0_rest))
    q0[rest[0]] = q0[rest[0]] + token_z[0, 0].astype(BF16)
    red = rs_finish([[q0[n], q1[n]] for n in rest], "a")
    done = step_weights(rest, {n: r[:, :own[n].shape[1]] for n, r in zip(rest, red)})
    all_done = jnp.stack([done[n][3][(0,) * done[n][3].ndim] for n in rest])
    _, q_last = split_wait("scatter_last_wait", z_send, z_recv, pz_t, qz_t, all_done, _scatter_copies)
    red = rs_finish([[q, q1[n]] for n, q in zip(first, q_last)] + [[q_last[-1]]], "b")
    f_small = red[-1].reshape(-1)

    grad = {n: r[:, :own[n].shape[1]] for n, r in zip(first, red)}
    off = 0
    for n, sz in zip(_REPL + ['conv_w', 'meta'], s_sizes):
        grad[n] = f_small[off:off + sz]
        off += sz
    for n in _REPL:
        grad[n] = grad[n].reshape(w[n].shape)
    cw = grad['conv_w'].reshape(DEPTH, 3, MIX)
    grad['conv_w'] = lax.dynamic_slice_in_dim(cw, chip * (MIX // N_SHARD), MIX // N_SHARD, axis=2)
    gm = grad['meta'].reshape(N_META, D_MODEL)
    grad['meta'] = lax.dynamic_slice_in_dim(gm, chip * (D_MODEL // N_SHARD), D_MODEL // N_SHARD, axis=1)

    done.update(step_weights([n for n in _WEIGHTS if n not in done], grad))
    return (loss, dh[X0:][None], *[done[n][k] for k in range(4) for n in _WEIGHTS])


def kernel(x, meta, ffn1_w_gate, ffn1_w_up, ffn1_w_down, ln1_g, ln1_b, w_in, mla_q_norm_g, mla_w_uq, mla_kv_norm_g, mla_w_ukv, mla_w_o, conv_w, conv_b, conv_w_out, s5_a_re, s5_a_im, s5_log_dt, s5_b_re, s5_b_im, s5_c_re, s5_c_im, s5_d, s5_w_glu, s5_b_glu, s5_w_out, w_o, ln2_g, ln2_b, ffn2_w_gate, ffn2_w_up, ffn2_w_down, ln3_g, ln3_b, loss_target, m_meta, m_ffn1_w_gate, m_ffn1_w_up, m_ffn1_w_down, m_ln1_g, m_ln1_b, m_w_in, m_mla_q_norm_g, m_mla_w_uq, m_mla_kv_norm_g, m_mla_w_ukv, m_mla_w_o, m_conv_w, m_conv_b, m_conv_w_out, m_s5_a_re, m_s5_a_im, m_s5_log_dt, m_s5_b_re, m_s5_b_im, m_s5_c_re, m_s5_c_im, m_s5_d, m_s5_w_glu, m_s5_b_glu, m_s5_w_out, m_w_o, m_ln2_g, m_ln2_b, m_ffn2_w_gate, m_ffn2_w_up, m_ffn2_w_down, m_ln3_g, m_ln3_b, v_meta, v_ffn1_w_gate, v_ffn1_w_up, v_ffn1_w_down, v_ln1_g, v_ln1_b, v_w_in, v_mla_q_norm_g, v_mla_w_uq, v_mla_kv_norm_g, v_mla_w_ukv, v_mla_w_o, v_conv_w, v_conv_b, v_conv_w_out, v_s5_a_re, v_s5_a_im, v_s5_log_dt, v_s5_b_re, v_s5_b_im, v_s5_c_re, v_s5_c_im, v_s5_d, v_s5_w_glu, v_s5_b_glu, v_s5_w_out, v_w_o, v_ln2_g, v_ln2_b, v_ffn2_w_gate, v_ffn2_w_up, v_ffn2_w_down, v_ln3_g, v_ln3_b):
    return _step(dict(locals()))
```

```python
import functools
import math

import numpy as np
import jax
import jax.numpy as jnp
from jax import lax
from jax.experimental import pallas as pl
from jax.experimental.pallas import tpu as pltpu

F32 = jnp.float32
BF16 = jnp.bfloat16

D_MODEL = 1024
DEPTH = 2
N_META = 16
PAD = 112
X0 = PAD + N_META
N_HEADS = 8
D_NOPE = 64
D_ROPE = 32
D_V = 64
Q_RANK = 384
KV_RANK = 256
MIX = 512
S5_GROUPS = 32
S5_GROUP = 16
S5_STATE = 64
S5_LANES = S5_GROUPS * S5_STATE
D_FF = 2816
N_SHARD = 4
FF_SHARD = D_FF // N_SHARD
D_IN = 5792
P_IN = 6144
ALPHA = (2.0 * DEPTH) ** 0.25
LN_EPS = 1e-5
RMS_EPS = 1e-6
ATT_SCALE = (D_NOPE + D_ROPE) ** -0.5
ROPE_BASE = 10000.0
ADAM_LR, ADAM_B1, ADAM_B2, ADAM_EPS, ADAM_WD, ADAM_STEP = 0.001, 0.9, 0.999, 1e-08, 0.01, 10
SCAN_CHUNK = 64
VMEM_LIMIT = 52 * 2 ** 20
WGRAD = BF16
MESH = pl.DeviceIdType.MESH


def _cparams(**kw):
    return pltpu.CompilerParams(vmem_limit_bytes=VMEM_LIMIT, **kw)


def _tile(n):
    if n <= 1088:
        return n
    for t in (1088, 1024, 544, 512, 272, 256, 128):
        if n % t == 0:
            return t
    return n


def _row_tile(lp):
    for t in (544, 272, 128):
        if lp % t == 0:
            return t
    return lp


def _ffn_tile(lp):
    return 1088 if lp % 1088 == 0 else _row_tile(lp)


def _sigmoid(x):
    return 1.0 / (1.0 + jnp.exp(-x))


_GELU_C = math.sqrt(2.0 / math.pi)


def _gelu(x):
    return 0.5 * x * (1.0 + jnp.tanh(_GELU_C * (x + 0.044715 * x * x * x)))


def _gelu_grad(x):
    t = jnp.tanh(_GELU_C * (x + 0.044715 * x * x * x))
    return 0.5 * (1.0 + t) + 0.5 * x * (1.0 - t * t) * _GELU_C * (1.0 + 3.0 * 0.044715 * x * x)


def _dot(a, b, ca, cb, precision=None):
    return lax.dot_general(a, b, (((ca,), (cb,)), ((), ())), preferred_element_type=F32, precision=precision)


def matmul(a, b, *, name, ta=False, tb=False, ab='n', bb='n', res=None, res_scale=1.0, scale=1.0, out_dtype=F32):
    if ta:
        _, K, M = a.shape
    else:
        _, M, K = a.shape
    if tb:
        _, N, K2 = b.shape
    else:
        _, K2, N = b.shape
    assert K == K2, (a.shape, b.shape)
    n_out = max(a.shape[0] if ab == 'o' else 1, b.shape[0] if bb == 'o' else 1)
    n_red = max(a.shape[0] if ab == 'r' else 1, b.shape[0] if bb == 'r' else 1)
    tm, tn = _tile(M), _tile(N)
    tk = K if K <= 2304 else _tile(K)
    nkt = K // tk
    n_steps = n_red * nkt

    def bsel(mode, o, r):
        if mode == 'o':
            return o
        if mode == 'r':
            return r // nkt if nkt > 1 else r
        return 0

    def ksel(r):
        if nkt == 1:
            return 0
        return r % nkt if n_red > 1 else r

    a_map = (lambda o, i, j, r: (bsel(ab, o, r), ksel(r), i)) if ta else (lambda o, i, j, r: (bsel(ab, o, r), i, ksel(r)))
    b_map = (lambda o, i, j, r: (bsel(bb, o, r), j, ksel(r))) if tb else (lambda o, i, j, r: (bsel(bb, o, r), ksel(r), j))
    o_map = lambda o, i, j, r: (o, i, j)
    in_specs = [pl.BlockSpec((None, tk, tm) if ta else (None, tm, tk), a_map),
                pl.BlockSpec((None, tn, tk) if tb else (None, tk, tn), b_map)]
    operands = [a, b]
    if res is not None:
        in_specs.append(pl.BlockSpec((None, tm, tn), o_map))
        operands.append(res)
    has_res = res is not None

    def body(*refs):
        a_ref, b_ref = refs[0], refs[1]
        res_ref = refs[2] if has_res else None
        o_ref = refs[3] if has_res else refs[2]
        part = _dot(a_ref[...].astype(BF16), b_ref[...].astype(BF16), 0 if ta else 1, 1 if tb else 0)

        def finish(acc):
            v = acc if scale == 1.0 else acc * scale
            if has_res:
                v = v + res_scale * res_ref[...].astype(F32)
            o_ref[...] = v.astype(o_ref.dtype)

        if n_steps == 1:
            finish(part)
        else:
            acc_ref = refs[-1]
            r = pl.program_id(3)

            @pl.when(r == 0)
            def _():
                acc_ref[...] = part

            @pl.when(r > 0)
            def _():
                acc_ref[...] += part

            @pl.when(r == n_steps - 1)
            def _():
                finish(acc_ref[...])

    return pl.pallas_call(
        body, name=name,
        grid=(n_out, M // tm, N // tn, n_steps),
        in_specs=in_specs,
        out_specs=pl.BlockSpec((None, tm, tn), o_map),
        out_shape=jax.ShapeDtypeStruct((n_out, M, N), out_dtype),
        scratch_shapes=[pltpu.VMEM((tm, tn), F32)] if n_steps > 1 else [],
        compiler_params=_cparams(),
    )(*operands)


def rowwise(fn, rows, pars, outs, accs=(), *, name, lp):
    tm = _row_tile(lp)
    n_rows, n_pars, n_outs, n_accs = len(rows), len(pars), len(outs), len(accs)
    outs = [tuple(o) + (o[0], 0) if len(o) == 2 else tuple(o) for o in outs]
    in_specs = [pl.BlockSpec((tm, w), functools.partial(lambda i, cb: (i, cb), cb=cb)) for _, w, cb in rows]
    in_specs += [pl.BlockSpec(p.shape, functools.partial(lambda i, nd: (0,) * nd, nd=p.ndim)) for p in pars]
    out_specs = [pl.BlockSpec((tm, w), functools.partial(lambda i, cb: (i, cb), cb=cb)) for w, _, _, cb in outs]
    out_specs += [pl.BlockSpec(s, functools.partial(lambda i, nd: (0,) * nd, nd=len(s))) for s, _ in accs]
    out_shape = [jax.ShapeDtypeStruct((lp, total), dt) for _, dt, total, _ in outs]
    out_shape += [jax.ShapeDtypeStruct(s, dt) for s, dt in accs]

    def body(*refs):
        i = pl.program_id(0)
        rv = [r[...] for r in refs[:n_rows]]
        pv = [r[...] for r in refs[n_rows:n_rows + n_pars]]
        o_refs = refs[n_rows + n_pars:n_rows + n_pars + n_outs]
        a_refs = refs[n_rows + n_pars + n_outs:]
        ov, av = fn(i * tm, rv, pv)
        for r, v in zip(o_refs, ov):
            r[...] = v.astype(r.dtype)
        if n_accs:
            @pl.when(i == 0)
            def _():
                for r, v in zip(a_refs, av):
                    r[...] = v.astype(r.dtype)

            @pl.when(i > 0)
            def _():
                for r, v in zip(a_refs, av):
                    r[...] += v.astype(r.dtype)

    res = pl.pallas_call(
        body, name=name, grid=(lp // tm,), in_specs=in_specs, out_specs=out_specs, out_shape=out_shape,
        compiler_params=_cparams(),
    )(*[r[0] for r in rows], *pars)
    return res


def _row_mask(row0, shape):
    return (row0 + lax.broadcasted_iota(jnp.int32, shape, 0)) >= PAD


def ffn_up(hb, wg, wu, lp):
    tm = _ffn_tile(lp)

    def body(h_ref, wg_ref, wu_ref, ab_ref, hid_ref):
        h = h_ref[...]
        a = _dot(h, wg_ref[...], 1, 1)
        b = _dot(h, wu_ref[...], 1, 1)
        ab_ref[0] = a.astype(BF16)
        ab_ref[1] = b.astype(BF16)
        hid_ref[...] = (a * _sigmoid(a) * b).astype(BF16)

    wspec = pl.BlockSpec((None, FF_SHARD, D_MODEL), lambda j, i: (j, 0, 0))
    return pl.pallas_call(
        body, name="ffn_up", grid=(N_SHARD, lp // tm),
        in_specs=[pl.BlockSpec((tm, D_MODEL), lambda j, i: (i, 0)), wspec, wspec],
        out_specs=[pl.BlockSpec((None, 2, tm, FF_SHARD), lambda j, i: (j, 0, i, 0)),
                   pl.BlockSpec((None, tm, FF_SHARD), lambda j, i: (j, i, 0))],
        out_shape=[jax.ShapeDtypeStruct((N_SHARD, 2, lp, FF_SHARD), BF16),
                   jax.ShapeDtypeStruct((N_SHARD, lp, FF_SHARD), BF16)],
        compiler_params=_cparams(),
    )(hb, wg, wu)


def _layer_norm(z, g, b):
    mu = jnp.mean(z, axis=-1, keepdims=True)
    zc = z - mu
    var = jnp.mean(zc * zc, axis=-1, keepdims=True)
    return zc * lax.rsqrt(var + LN_EPS) * g + b


def mm_res_ln(a, w, res, g, b, *, scale, name, lp):
    n_red, _, K = a.shape
    tm = _row_tile(lp)

    def body(a_ref, w_ref, res_ref, g_ref, b_ref, z_ref, h_ref, hb_ref, acc_ref):
        r = pl.program_id(1)
        part = _dot(a_ref[...].astype(BF16), w_ref[...], 1, 0)

        @pl.when(r == 0)
        def _():
            acc_ref[...] = part

        @pl.when(r > 0)
        def _():
            acc_ref[...] += part

        @pl.when(r == n_red - 1)
        def _():
            z = ALPHA * res_ref[...] + scale * acc_ref[...]
            z_ref[...] = z
            hn = _layer_norm(z, g_ref[...], b_ref[...])
            h_ref[...] = hn
            hb_ref[...] = hn.astype(BF16)

    row = pl.BlockSpec((tm, D_MODEL), lambda i, r: (i, 0))
    par = pl.BlockSpec((1, D_MODEL), lambda i, r: (0, 0))
    return pl.pallas_call(
        body, name=name, grid=(lp // tm, n_red),
        in_specs=[pl.BlockSpec((None, tm, K), lambda i, r: (r, i, 0)),
                  pl.BlockSpec((None, K, D_MODEL), lambda i, r: (r, 0, 0)), row, par, par],
        out_specs=[row, row, row],
        out_shape=[jax.ShapeDtypeStruct((lp, D_MODEL), F32), jax.ShapeDtypeStruct((lp, D_MODEL), F32),
                   jax.ShapeDtypeStruct((lp, D_MODEL), BF16)],
        scratch_shapes=[pltpu.VMEM((tm, D_MODEL), F32)],
        compiler_params=_cparams(),
    )(a, w, res, g, b)


def ln_bwd(dh, z, g, *, fscale, name, lp):
    def fn(row0, rv, pv):
        dh_, z_ = rv
        g_, = pv
        mu = jnp.mean(z_, axis=-1, keepdims=True)
        zc = z_ - mu
        rstd = lax.rsqrt(jnp.mean(zc * zc, axis=-1, keepdims=True) + LN_EPS)
        xh = zc * rstd
        dxh = dh_ * g_
        m1 = jnp.mean(dxh, axis=-1, keepdims=True)
        m2 = jnp.mean(dxh * xh, axis=-1, keepdims=True)
        dz = rstd * (dxh - m1 - xh * m2)
        return ((dz, fscale * dz),
                (jnp.sum(dh_ * xh, axis=0, keepdims=True), jnp.sum(dh_, axis=0, keepdims=True)))

    return rowwise(fn, [(dh, D_MODEL, 0), (z, D_MODEL, 0)], [g], [(D_MODEL, F32), (D_MODEL, BF16)],
                   [((1, D_MODEL), F32), ((1, D_MODEL), F32)], name=name, lp=lp)


def ffn_down_bwd(dfb, wd, ab, lp):
    tm = _ffn_tile(lp)

    def body(df_ref, w_ref, ab_ref, da_ref, db_ref):
        dhid = _dot(df_ref[...], w_ref[...], 1, 1)
        a = ab_ref[0].astype(F32)
        b = ab_ref[1].astype(F32)
        sg = _sigmoid(a)
        da_ref[...] = (dhid * b * (sg * (1.0 + a * (1.0 - sg)))).astype(BF16)
        db_ref[...] = (dhid * (a * sg)).astype(BF16)

    ospec = pl.BlockSpec((None, tm, FF_SHARD), lambda j, i: (j, i, 0))
    return pl.pallas_call(
        body, name="ffn_down_bwd", grid=(N_SHARD, lp // tm),
        in_specs=[pl.BlockSpec((tm, D_MODEL), lambda j, i: (i, 0)),
                  pl.BlockSpec((None, FF_SHARD, D_MODEL), lambda j, i: (j, 0, 0)),
                  pl.BlockSpec((None, 2, tm, FF_SHARD), lambda j, i: (j, 0, i, 0))],
        out_specs=[ospec, ospec],
        out_shape=[jax.ShapeDtypeStruct((N_SHARD, lp, FF_SHARD), BF16)] * 2,
        compiler_params=_cparams(),
    )(dfb, wd, ab)


def ffn_dx(da, db, wg, wu, dz, lp):
    tm = _ffn_tile(lp)

    def body(da_ref, db_ref, wg_ref, wu_ref, dz_ref, o_ref, acc_ref):
        j = pl.program_id(1)
        part = _dot(da_ref[...], wg_ref[...], 1, 0) + _dot(db_ref[...], wu_ref[...], 1, 0)

        @pl.when(j == 0)
        def _():
            acc_ref[...] = part

        @pl.when(j > 0)
        def _():
            acc_ref[...] += part

        @pl.when(j == N_SHARD - 1)
        def _():
            o_ref[...] = acc_ref[...] + ALPHA * dz_ref[...]

    aspec = pl.BlockSpec((None, tm, FF_SHARD), lambda i, j: (j, i, 0))
    wspec = pl.BlockSpec((None, FF_SHARD, D_MODEL), lambda i, j: (j, 0, 0))
    row = pl.BlockSpec((tm, D_MODEL), lambda i, j: (i, 0))
    return pl.pallas_call(
        body, name="ffn_dx", grid=(lp // tm, N_SHARD), in_specs=[aspec, aspec, wspec, wspec, row], out_specs=row,
        out_shape=jax.ShapeDtypeStruct((lp, D_MODEL), F32), scratch_shapes=[pltpu.VMEM((tm, D_MODEL), F32)],
        compiler_params=_cparams(),
    )(da, db, wg, wu, dz)


def ffn_fwd(h, hb, wg, wu, wd, g, b, lp):
    ab, hid = ffn_up(hb, wg, wu, lp)
    z, hn, hnb = mm_res_ln(hid, wd, h, g, b, scale=0.5, name="ffn_down_ln", lp=lp)
    return hn, hnb, dict(hb=hb, ab=ab, hid=hid, z=z)


def ffn_bwd(dh, sv, wg, wu, wd, g, lp):
    dz, dfb, dg, db = ln_bwd(dh, sv['z'], g, fscale=0.5, name="ffn_ln_bwd", lp=lp)
    da, dbb = ffn_down_bwd(dfb, wd, sv['ab'], lp)
    d_wd = matmul(sv['hid'], dfb[None], ta=True, ab='o', out_dtype=WGRAD, name="ffn_dwd")
    d_wg = matmul(da, sv['hb'][None], ta=True, ab='o', out_dtype=WGRAD, name="ffn_dwg")
    d_wu = matmul(dbb, sv['hb'][None], ta=True, ab='o', out_dtype=WGRAD, name="ffn_dwu")
    dh_in = ffn_dx(da, dbb, wg, wu, dz, lp)
    return dh_in, dict(wg=d_wg, wu=d_wu, wd=d_wd, ln_g=dg, ln_b=db)


def _rope_tables(lp):
    pos = np.arange(lp, dtype=np.float32) - PAD
    inv = ROPE_BASE ** (-np.arange(0, D_ROPE, 2, dtype=np.float32) / D_ROPE)
    ang = pos[:, None] * inv[None, :]
    cos = np.concatenate([np.cos(ang), np.cos(ang)], axis=1).astype(np.float32)
    sin = np.concatenate([np.sin(ang), np.sin(ang)], axis=1).astype(np.float32)
    rot = np.zeros((D_ROPE, D_ROPE), np.float32)
    half = D_ROPE // 2
    for j in range(half):
        rot[j + half, j] = -1.0
        rot[j, j + half] = 1.0
    d_qk = D_NOPE + D_ROPE
    cos_qk = np.concatenate([np.ones((lp, D_NOPE), np.float32), cos], axis=1)
    sin_qk = np.concatenate([np.zeros((lp, D_NOPE), np.float32), sin], axis=1)
    rot_qk = np.zeros((d_qk, d_qk), np.float32)
    rot_qk[D_NOPE:, D_NOPE:] = rot
    place = np.zeros((D_ROPE, d_qk), np.float32)
    place[:, D_NOPE:] = np.eye(D_ROPE, dtype=np.float32)
    return tuple(jnp.asarray(t) for t in (cos, sin, rot, cos_qk, sin_qk, rot_qk, place))


def _rot(x, rot):
    return _dot(x, rot, 1, 0, precision=lax.Precision.HIGHEST)


def _rms(x, g):
    r = lax.rsqrt(jnp.mean(x * x, axis=-1, keepdims=True) + RMS_EPS)
    return x * r * g


def mla_prep(proj, cos, sin, rot, qg, kvg, lp):
    def fn(row0, rv, pv):
        cq, krb, ckv, c, s = rv
        qg_, kvg_, rot_ = pv
        kr = krb[:, :D_ROPE]
        return ((_rms(cq, qg_), _rms(ckv, kvg_), kr * c + _rot(kr, rot_) * s), ())

    return rowwise(fn, [(proj, Q_RANK, 0), (proj, 128, 3), (proj, KV_RANK, 2), (cos, D_ROPE, 0), (sin, D_ROPE, 0)],
                   [qg, kvg, rot], [(Q_RANK, BF16), (KV_RANK, BF16), (D_ROPE, BF16)], name="mla_prep", lp=lp)


def mla_heads(cqn, ckvn, kr, cos_qk, sin_qk, rot_qk, place, wq, wk, wv, lp):
    tm = _row_tile(lp)

    def body(cq_ref, ckv_ref, kr_ref, c_ref, s_ref, rot_ref, place_ref, wq_ref, wk_ref, wv_ref, q_ref, k_ref, v_ref):
        cq = cq_ref[...]
        ckv = ckv_ref[...]
        kr_placed = _dot(kr_ref[...], place_ref[...].astype(BF16), 1, 0)
        for h in range(N_HEADS):
            q = _dot(cq, wq_ref[h], 1, 0)
            q_ref[h] = (q * c_ref[...] + _rot(q, rot_ref[...]) * s_ref[...]).astype(BF16)
            k_ref[h] = (_dot(ckv, wk_ref[h], 1, 0) + kr_placed).astype(BF16)
            v_ref[h] = _dot(ckv, wv_ref[h], 1, 0).astype(BF16)

    def row(w):
        return pl.BlockSpec((tm, w), lambda i: (i, 0))

    def whole(a):
        return pl.BlockSpec(a.shape, functools.partial(lambda i, nd: (0,) * nd, nd=a.ndim))

    def ospec(n):
        return pl.BlockSpec((N_HEADS, tm, n), lambda i: (0, i, 0))

    return pl.pallas_call(
        body, name="mla_heads", grid=(lp // tm,),
        in_specs=[row(Q_RANK), row(KV_RANK), row(D_ROPE), row(D_QK), row(D_QK), whole(rot_qk), whole(place),
                  whole(wq), whole(wk), whole(wv)],
        out_specs=[ospec(D_QK), ospec(D_QK), ospec(D_V)],
        out_shape=[jax.ShapeDtypeStruct((N_HEADS, lp, D_QK), BF16), jax.ShapeDtypeStruct((N_HEADS, lp, D_QK), BF16),
                   jax.ShapeDtypeStruct((N_HEADS, lp, D_V), BF16)],
        compiler_params=_cparams(),
    )(cqn, ckvn, kr, cos_qk, sin_qk, rot_qk, place, wq, wk, wv)


D_QK = D_NOPE + D_ROPE


def _att_probs(q, k, row0, tq, lp):
    s = _dot(q, k, 1, 1) * ATT_SCALE
    qi = row0 + lax.broadcasted_iota(jnp.int32, (tq, lp), 0)
    ki = lax.broadcasted_iota(jnp.int32, (tq, lp), 1)
    s = jnp.where((ki <= qi) & (ki >= PAD), s, -1e30)
    p = jnp.exp(s - jnp.max(s, axis=-1, keepdims=True))
    return p / jnp.sum(p, axis=-1, keepdims=True)


def _att_spec(lp, n):
    return pl.BlockSpec((None, lp, n), lambda h: (h, 0, 0))


def _att_tiles(lp):
    tiles, r = [(0, X0)], X0
    while r < lp:
        tiles.append((r, 256))
        r += 256
    assert r == lp
    return tiles


def attn_fwd(q, k, v, lp):
    def body(q_ref, k_ref, v_ref, o_ref):
        for r0, rows in _att_tiles(lp):
            ke, rq = r0 + rows, slice(r0, r0 + rows)
            p = _att_probs(q_ref[rq, :], k_ref[0:ke, :], r0, rows, ke)
            o_ref[rq, :] = _dot(p.astype(BF16), v_ref[0:ke, :], 1, 0).astype(BF16)

    return pl.pallas_call(
        body, name="attn_fwd", grid=(N_HEADS,),
        in_specs=[_att_spec(lp, D_QK), _att_spec(lp, D_QK), _att_spec(lp, D_V)],
        out_specs=_att_spec(lp, D_V), out_shape=jax.ShapeDtypeStruct((N_HEADS, lp, D_V), BF16),
        compiler_params=_cparams(),
    )(q, k, v)


def attn_bwd(q, k, v, do, lp):
    def body(q_ref, k_ref, v_ref, do_ref, dq_ref, dk_ref, dv_ref):
        dk_ref[...] = jnp.zeros_like(dk_ref)
        dv_ref[...] = jnp.zeros_like(dv_ref)
        for r0, rows in _att_tiles(lp):
            ke, rq = r0 + rows, slice(r0, r0 + rows)
            q_, do_, k_, v_ = q_ref[rq, :], do_ref[rq, :], k_ref[0:ke, :], v_ref[0:ke, :]
            p = _att_probs(q_, k_, r0, rows, ke)
            dp = _dot(do_, v_, 1, 1)
            delta = jnp.sum(p * dp, axis=-1, keepdims=True)
            ds = (p * (dp - delta) * ATT_SCALE).astype(BF16)
            dq_ref[rq, :] = _dot(ds, k_, 1, 0)
            dk_ref[0:ke, :] += _dot(ds, q_, 0, 0)
            dv_ref[0:ke, :] += _dot(p.astype(BF16), do_, 0, 0)

    qk, vv = _att_spec(lp, D_QK), _att_spec(lp, D_V)
    return pl.pallas_call(
        body, name="attn_bwd", grid=(N_HEADS,), in_specs=[qk, qk, vv, vv], out_specs=[qk, qk, vv],
        out_shape=[jax.ShapeDtypeStruct((N_HEADS, lp, D_QK), F32), jax.ShapeDtypeStruct((N_HEADS, lp, D_QK), F32),
                   jax.ShapeDtypeStruct((N_HEADS, lp, D_V), F32)],
        compiler_params=_cparams(),
    )(q, k, v, do)


def mla_heads_bwd(dq, dk, dv, cos_qk, sin_qk, rot_qk, place, wq, wk, wv, lp):
    tm = _row_tile(lp)

    def body(dq_ref, dk_ref, dv_ref, c_ref, s_ref, rot_ref, place_ref, wq_ref, wk_ref, wv_ref,
             dcq_ref, dckv_ref, dqp_ref, dkr_ref):
        dcq = jnp.zeros(dcq_ref.shape, F32)
        dckv = jnp.zeros(dckv_ref.shape, F32)
        dkr = jnp.zeros(dkr_ref.shape, F32)
        for h in range(N_HEADS):
            g = dq_ref[h]
            dqp = (g * c_ref[...] - _rot(g * s_ref[...], rot_ref[...])).astype(BF16)
            dqp_ref[h] = dqp
            dcq = dcq + _dot(dqp, wq_ref[h], 1, 1)
            dk_ = dk_ref[h]
            dckv = dckv + _dot(dk_.astype(BF16), wk_ref[h], 1, 1) + _dot(dv_ref[h].astype(BF16), wv_ref[h], 1, 1)
            dkr = dkr + _dot(dk_, place_ref[...], 1, 1, precision=lax.Precision.HIGHEST)
        dcq_ref[...] = dcq
        dckv_ref[...] = dckv
        dkr_ref[...] = dkr

    def hspec(n):
        return pl.BlockSpec((N_HEADS, tm, n), lambda i: (0, i, 0))

    def row(w):
        return pl.BlockSpec((tm, w), lambda i: (i, 0))

    def whole(a):
        return pl.BlockSpec(a.shape, functools.partial(lambda i, nd: (0,) * nd, nd=a.ndim))

    return pl.pallas_call(
        body, name="mla_heads_bwd", grid=(lp // tm,),
        in_specs=[hspec(D_QK), hspec(D_QK), hspec(D_V), row(D_QK), row(D_QK), whole(rot_qk), whole(place),
                  whole(wq), whole(wk), whole(wv)],
        out_specs=[row(Q_RANK), row(KV_RANK), hspec(D_QK), row(D_ROPE)],
        out_shape=[jax.ShapeDtypeStruct((lp, Q_RANK), F32), jax.ShapeDtypeStruct((lp, KV_RANK), F32),
                   jax.ShapeDtypeStruct((N_HEADS, lp, D_QK), BF16), jax.ShapeDtypeStruct((lp, D_ROPE), F32)],
        compiler_params=_cparams(),
    )(dq, dk, dv, cos_qk, sin_qk, rot_qk, place, wq, wk, wv)


def heads_out(o, wo, lp):
    tm = _row_tile(lp)

    def body(o_ref, w_ref, y_ref):
        acc = _dot(o_ref[0], w_ref[0], 1, 0)
        for h in range(1, N_HEADS):
            acc = acc + _dot(o_ref[h], w_ref[h], 1, 0)
        y_ref[...] = acc.astype(BF16)

    return pl.pallas_call(
        body, name="mla_out", grid=(lp // tm,),
        in_specs=[pl.BlockSpec((N_HEADS, tm, D_V), lambda i: (0, i, 0)), pl.BlockSpec(wo.shape, lambda i: (0, 0, 0))],
        out_specs=pl.BlockSpec((tm, D_MODEL), lambda i: (i, 0)), out_shape=jax.ShapeDtypeStruct((lp, D_MODEL), BF16),
        compiler_params=_cparams(),
    )(o, wo)


def heads_out_dx(dy, wo, lp):
    tm = _row_tile(lp)

    def body(dy_ref, w_ref, do_ref):
        dy_ = dy_ref[...]
        for h in range(N_HEADS):
            do_ref[h] = _dot(dy_, w_ref[h], 1, 1).astype(BF16)

    return pl.pallas_call(
        body, name="mla_out_dx", grid=(lp // tm,),
        in_specs=[pl.BlockSpec((tm, D_MODEL), lambda i: (i, 0)), pl.BlockSpec(wo.shape, lambda i: (0, 0, 0))],
        out_specs=pl.BlockSpec((N_HEADS, tm, D_V), lambda i: (0, i, 0)),
        out_shape=jax.ShapeDtypeStruct((N_HEADS, lp, D_V), BF16), compiler_params=_cparams(),
    )(dy, wo)


def _rms_bwd(dy, x, g):
    r = lax.rsqrt(jnp.mean(x * x, axis=-1, keepdims=True) + RMS_EPS)
    n = x * r
    dn = dy * g
    dx = r * (dn - n * jnp.mean(dn * n, axis=-1, keepdims=True))
    return dx, jnp.sum(dy * n, axis=0, keepdims=True)


def mla_prep_bwd(dcq, dckv, dkr, proj, cos, sin, rot, qg, kvg, lp):
    def fn(row0, rv, pv):
        dcq_, dckv_, dkr_, cq, ckv, c, s = rv
        qg_, kvg_, rot_ = pv
        dxq, dgq = _rms_bwd(dcq_, cq, qg_)
        dxkv, dgkv = _rms_bwd(dckv_, ckv, kvg_)
        dkr_raw = dkr_ * c - _rot(dkr_ * s, rot_)
        return ((dxq, dxkv, dkr_raw), (dgq, dgkv))

    return rowwise(fn, [(dcq, Q_RANK, 0), (dckv, KV_RANK, 0), (dkr, D_ROPE, 0), (proj, Q_RANK, 0), (proj, KV_RANK, 2),
                        (cos, D_ROPE, 0), (sin, D_ROPE, 0)], [qg, kvg, rot],
                   [(Q_RANK, BF16), (KV_RANK, BF16), (D_ROPE, BF16)], [((1, Q_RANK), F32), ((1, KV_RANK), F32)],
                   name="mla_prep_bwd", lp=lp)


def _shift_down(x, d, rows):
    return jnp.where(rows >= d, pltpu.roll(x, d, 0), 0.0)


def _shift_up(x, d, rows, n):
    return jnp.where(rows < n - d, pltpu.roll(x, n - d, 0), 0.0)


_CONV_W = 128
_XB, _BG, _CG = 1024 // _CONV_W, 1536 // _CONV_W, 2048 // _CONV_W


def _conv_specs(lp):
    def pspec(base):
        return pl.BlockSpec((lp, _CONV_W), functools.partial(lambda c, base: (0, base + c), base=base))

    col = pl.BlockSpec((lp, _CONV_W), lambda c: (0, c))
    wspec = pl.BlockSpec((3, _CONV_W), lambda c: (0, c))
    bspec = pl.BlockSpec((1, _CONV_W), lambda c: (0, c))
    return pspec, col, wspec, bspec


def _conv_core(xbar, cg, w, bias, lp):
    rows = lax.broadcasted_iota(jnp.int32, (lp, _CONV_W), 0)
    u = jnp.where(rows >= PAD, cg * xbar, 0.0)
    u1 = _shift_down(u, 1, rows)
    u2 = _shift_down(u, 2, rows)
    y = bias + w[0:1] * u2 + w[1:2] * u1 + w[2:3] * u
    return rows, u, u1, u2, y


def conv_fwd(proj, w, bias, lp):
    pspec, col, wspec, bspec = _conv_specs(lp)

    def body(x_ref, b_ref, c_ref, w_ref, bias_ref, v_ref):
        _, _, _, _, y = _conv_core(x_ref[...], c_ref[...], w_ref[...], bias_ref[...], lp)
        v_ref[...] = (b_ref[...] * y).astype(BF16)

    return pl.pallas_call(
        body, name="conv_fwd", grid=(MIX // _CONV_W,),
        in_specs=[pspec(_XB), pspec(_BG), pspec(_CG), wspec, bspec], out_specs=col,
        out_shape=jax.ShapeDtypeStruct((lp, MIX), BF16), compiler_params=_cparams(),
    )(proj, proj, proj, w, bias)


def conv_bwd(dv, proj, w, bias, lp):
    pspec, col, wspec, bspec = _conv_specs(lp)

    def body(dv_ref, x_ref, b_ref, c_ref, w_ref, bias_ref, dx_ref, db_ref, dc_ref, dw_ref, dbias_ref):
        xbar, cg, w_ = x_ref[...], c_ref[...], w_ref[...]
        rows, u, u1, u2, y = _conv_core(xbar, cg, w_, bias_ref[...], lp)
        dv_ = dv_ref[...]
        db_ref[...] = (dv_ * y).astype(BF16)
        dy = dv_ * b_ref[...]
        dbias_ref[...] = jnp.sum(dy, axis=0, keepdims=True)
        dw_ref[0:1, :] = jnp.sum(dy * u2, axis=0, keepdims=True)
        dw_ref[1:2, :] = jnp.sum(dy * u1, axis=0, keepdims=True)
        dw_ref[2:3, :] = jnp.sum(dy * u, axis=0, keepdims=True)
        du = w_[2:3] * dy + w_[1:2] * _shift_up(dy, 1, rows, lp) + w_[0:1] * _shift_up(dy, 2, rows, lp)
        du = jnp.where(rows >= PAD, du, 0.0)
        dc_ref[...] = (du * xbar).astype(BF16)
        dx_ref[...] = (du * cg).astype(BF16)

    return pl.pallas_call(
        body, name="conv_bwd", grid=(MIX // _CONV_W,),
        in_specs=[col, pspec(_XB), pspec(_BG), pspec(_CG), wspec, bspec],
        out_specs=[col, col, col, wspec, bspec],
        out_shape=[jax.ShapeDtypeStruct((lp, MIX), BF16)] * 3 + [jax.ShapeDtypeStruct((3, MIX), F32),
                                                                jax.ShapeDtypeStruct((1, MIX), F32)],
        compiler_params=_cparams(),
    )(dv, proj, proj, proj, w, bias)


def _s5_disc(a_re, a_im, log_dt, b_re, b_im):
    dt = jnp.exp(log_dt)
    mag = jnp.exp(dt * a_re)
    ab_re, ab_im = mag * jnp.cos(dt * a_im), mag * jnp.sin(dt * a_im)
    den = a_re * a_re + a_im * a_im
    nr, ni = ab_re - 1.0, ab_im
    coef_re = (nr * a_re + ni * a_im) / den
    coef_im = (ni * a_re - nr * a_im) / den
    return ab_re, ab_im, coef_re * b_re - coef_im * b_im, coef_re * b_im + coef_im * b_re


_S5_ROWS = S5_GROUPS * S5_GROUP


def s5_prep(a_re, a_im, log_dt, b_re, b_im):
    def body(ar, ai, ld, br, bi, o0, o1, o2, o3):
        for o, v in zip((o0, o1, o2, o3), _s5_disc(ar[...], ai[...], ld[...], br[...], bi[...])):
            o[...] = v

    return pl.pallas_call(body, name="s5_prep",
                          out_shape=[jax.ShapeDtypeStruct((_S5_ROWS, S5_STATE), F32)] * 4)(a_re, a_im, log_dt, b_re, b_im)


def s5_prep_bwd(a_re, a_im, log_dt, b_re, b_im, d_ab_re, d_ab_im, d_bb_re, d_bb_im, sel):
    def body(ar, ai, ld, br, bi, g0, g1, g2, g3, sel_ref, da_re, da_im, dld, dbr, dbi):
        _, vjp = jax.vjp(_s5_disc, ar[...], ai[...], ld[...], br[...], bi[...])
        c_ar, c_ai, c_ld, c_br, c_bi = vjp((g0[...], g1[...], g2[...], g3[...]))
        s = sel_ref[...]
        hi = lax.Precision.HIGHEST
        da_re[...] = _dot(s, c_ar, 1, 0, precision=hi)
        da_im[...] = _dot(s, c_ai, 1, 0, precision=hi)
        dld[...] = jnp.sum(_dot(s, c_ld, 1, 0, precision=hi), axis=-1, keepdims=True)
        dbr[...] = c_br
        dbi[...] = c_bi

    g = jax.ShapeDtypeStruct((S5_GROUPS, S5_STATE), F32)
    full = jax.ShapeDtypeStruct((_S5_ROWS, S5_STATE), F32)
    return pl.pallas_call(body, name="s5_prep_bwd",
                          out_shape=[g, g, jax.ShapeDtypeStruct((S5_GROUPS, 1), F32), full, full],
                          )(a_re, a_im, log_dt, b_re, b_im, d_ab_re, d_ab_im, d_bb_re, d_bb_im, sel)


_SCAN_W = 128
_SCAN_STEPS = int(math.log2(SCAN_CHUNK))


def _cmul(ar, ai, br, bi):
    return ar * br - ai * bi, ar * bi + ai * br


def _scan_powers(ar, ai, reverse):
    pw = [(ar, ai)]
    for _ in range(_SCAN_STEPS):
        pw.append(_cmul(*pw[-1], *pw[-1]))
    rows = lax.broadcasted_iota(jnp.int32, (SCAN_CHUNK, ar.shape[-1]), 0)
    tr = jnp.broadcast_to(ar, rows.shape)
    ti = jnp.broadcast_to(ai, rows.shape)
    for k in range(_SCAN_STEPS):
        d = 2 ** k
        if reverse:
            live = rows < SCAN_CHUNK - d
            mr, mi = _cmul(tr, ti, _shift_up(tr, d, rows, SCAN_CHUNK), _shift_up(ti, d, rows, SCAN_CHUNK))
        else:
            live = rows >= d
            mr, mi = _cmul(tr, ti, _shift_down(tr, d, rows), _shift_down(ti, d, rows))
        tr = jnp.where(live, mr, tr)
        ti = jnp.where(live, mi, ti)
    return pw, rows, tr, ti


def s5_scan(bu, ab_re, ab_im, lp):
    n_chunks = lp // SCAN_CHUNK

    def body(bu_ref, ar_ref, ai_ref, s_ref):
        ar, ai = ar_ref[...], ai_ref[...]
        pw, rows, tr, ti = _scan_powers(ar, ai, False)

        def chunk(ci, carry):
            cr, cim = carry
            r0 = pl.multiple_of(ci * SCAN_CHUNK, SCAN_CHUNK)
            xr = bu_ref[0, pl.ds(r0, SCAN_CHUNK), :].astype(F32)
            xi = bu_ref[1, pl.ds(r0, SCAN_CHUNK), :].astype(F32)
            for k in range(_SCAN_STEPS):
                d = 2 ** k
                mr, mi = _cmul(pw[k][0], pw[k][1], _shift_down(xr, d, rows), _shift_down(xi, d, rows))
                xr, xi = xr + mr, xi + mi
            mr, mi = _cmul(tr, ti, cr, cim)
            xr, xi = xr + mr, xi + mi
            s_ref[0, pl.ds(r0, SCAN_CHUNK), :] = xr
            s_ref[1, pl.ds(r0, SCAN_CHUNK), :] = xi
            return xr[SCAN_CHUNK - 1:SCAN_CHUNK, :], xi[SCAN_CHUNK - 1:SCAN_CHUNK, :]

        zero = jnp.zeros((1, _SCAN_W), F32)
        lax.fori_loop(0, n_chunks, chunk, (zero, zero))

    spec = pl.BlockSpec((2, lp, _SCAN_W), lambda c: (0, 0, c))
    aspec = pl.BlockSpec((1, _SCAN_W), lambda c: (0, c))
    return pl.pallas_call(
        body, name="s5_scan", grid=(S5_LANES // _SCAN_W,), in_specs=[spec, aspec, aspec], out_specs=spec,
        out_shape=jax.ShapeDtypeStruct((2, lp, S5_LANES), F32), compiler_params=_cparams(),
    )(bu, ab_re, ab_im)


def s5_scan_bwd(ds, s, ab_re, ab_im, lp):
    n_chunks = lp // SCAN_CHUNK

    def body(ds_ref, s_ref, ar_ref, ai_ref, g_ref, da_ref):
        ar, ai = ar_ref[...], -ai_ref[...]
        pw, rows, tr, ti = _scan_powers(ar, ai, True)

        def chunk(k, carry):
            cr, cim, dar, dai = carry
            ci = n_chunks - 1 - k
            r0 = pl.multiple_of(ci * SCAN_CHUNK, SCAN_CHUNK)
            xr = ds_ref[0, pl.ds(r0, SCAN_CHUNK), :].astype(F32)
            xi = ds_ref[1, pl.ds(r0, SCAN_CHUNK), :].astype(F32)
            for j in range(_SCAN_STEPS):
                d = 2 ** j
                mr, mi = _cmul(pw[j][0], pw[j][1], _shift_up(xr, d, rows, SCAN_CHUNK), _shift_up(xi, d, rows, SCAN_CHUNK))
                xr, xi = xr + mr, xi + mi
            mr, mi = _cmul(tr, ti, cr, cim)
            xr, xi = xr + mr, xi + mi
            g_ref[0, pl.ds(r0, SCAN_CHUNK), :] = xr
            g_ref[1, pl.ds(r0, SCAN_CHUNK), :] = xi
            prev0 = pl.multiple_of(jnp.maximum(r0 - 8, 0), 8)
            live = (ci > 0).astype(F32)
            pr = s_ref[0, pl.ds(prev0, 8), :][7:8, :] * live
            pim = s_ref[1, pl.ds(prev0, 8), :][7:8, :] * live
            sr = s_ref[0, pl.ds(r0, SCAN_CHUNK), :]
            si = s_ref[1, pl.ds(r0, SCAN_CHUNK), :]
            sr = jnp.where(rows >= 1, pltpu.roll(sr, 1, 0), pr)
            si = jnp.where(rows >= 1, pltpu.roll(si, 1, 0), pim)
            dar = dar + jnp.sum(xr * sr + xi * si, axis=0, keepdims=True)
            dai = dai + jnp.sum(xi * sr - xr * si, axis=0, keepdims=True)
            return xr[0:1, :], xi[0:1, :], dar, dai

        zero = jnp.zeros((1, _SCAN_W), F32)
        _, _, dar, dai = lax.fori_loop(0, n_chunks, chunk, (zero, zero, zero, zero))
        da_ref[0] = dar
        da_ref[1] = dai

    spec = pl.BlockSpec((2, lp, _SCAN_W), lambda c: (0, 0, c))
    aspec = pl.BlockSpec((1, _SCAN_W), lambda c: (0, c))
    return pl.pallas_call(
        body, name="s5_scan_bwd", grid=(S5_LANES // _SCAN_W,), in_specs=[spec, spec, aspec, aspec],
        out_specs=[spec, pl.BlockSpec((2, 1, _SCAN_W), lambda c: (0, 0, c))],
        out_shape=[jax.ShapeDtypeStruct((2, lp, S5_LANES), F32), jax.ShapeDtypeStruct((2, 1, S5_LANES), F32)],
        compiler_params=_cparams(),
    )(ds, s, ab_re, ab_im)


S5_BLOCKS = 4
_S5_PER = S5_GROUPS // S5_BLOCKS


def _blockdiag(x):
    _, r, c = x.shape
    eye = jnp.eye(_S5_PER, dtype=x.dtype)
    x = x.reshape(S5_BLOCKS, _S5_PER, r, c)
    return (x[:, :, :, None, :] * eye[None, :, None, :, None]).reshape(S5_BLOCKS, _S5_PER * r, _S5_PER * c)


def _blockdiag_extract(m, r, c):
    return jnp.einsum('qgrgc->qgrc', m.reshape(S5_BLOCKS, _S5_PER, r, _S5_PER, c)).reshape(S5_GROUPS, r, c)


def bd_matmul(a, w, *, w_t, reduce, res=None, out_dtype=F32, name):
    _, M, _ = a.shape
    n_w, _, k1, k2 = w.shape
    ka, kout = (k2, k1) if w_t else (k1, k2)
    tm = _row_tile(M)
    n_out, n_red = (1, n_w) if reduce else (n_w, 1)
    has_res = res is not None

    assert n_red <= 2

    def body(*refs):
        a_ref, w_ref = refs[0], refs[1]
        o_ref = refs[3] if has_res else refs[2]
        for q in range(S5_BLOCKS):
            cols = slice(q * kout, (q + 1) * kout)
            part = _dot(a_ref[:, q * ka:(q + 1) * ka].astype(BF16), w_ref[q], 1, 1 if w_t else 0)
            if n_red == 1:
                o_ref[:, cols] = part.astype(o_ref.dtype)
            else:
                acc_ref = refs[-1]

                @pl.when(pl.program_id(2) == 0)
                def _():
                    acc_ref[:, cols] = part

                @pl.when(pl.program_id(2) == 1)
                def _():
                    tot = acc_ref[:, cols] + part
                    o_ref[:, cols] = tot + refs[2][:, cols] if has_res else tot

    if reduce:
        a_map, w_map = (lambda o, i, r: (r, i, 0)), (lambda o, i, r: (r, 0, 0, 0))
    else:
        a_map, w_map = (lambda o, i, r: (0, i, 0)), (lambda o, i, r: (o, 0, 0, 0))
    o_map = lambda o, i, r: (o, i, 0)
    in_specs = [pl.BlockSpec((None, tm, S5_BLOCKS * ka), a_map), pl.BlockSpec((None, S5_BLOCKS, k1, k2), w_map)]
    operands = [a, w]
    if has_res:
        in_specs.append(pl.BlockSpec((None, tm, S5_BLOCKS * kout), o_map))
        operands.append(res)
    return pl.pallas_call(
        body, name=name, grid=(n_out, M // tm, n_red), in_specs=in_specs,
        out_specs=pl.BlockSpec((None, tm, S5_BLOCKS * kout), o_map),
        out_shape=jax.ShapeDtypeStruct((n_out, M, S5_BLOCKS * kout), out_dtype),
        scratch_shapes=[pltpu.VMEM((tm, S5_BLOCKS * kout), F32)] if n_red > 1 else [],
        compiler_params=_cparams(),
    )(*operands)


def bd_outer(a, b, name):
    na, M, wa = a.shape
    nb_, _, wb = b.shape
    ka, kb = wa // S5_BLOCKS, wb // S5_BLOCKS
    n_out = max(na, nb_)

    def body(a_ref, b_ref, o_ref):
        o_ref[...] = _dot(a_ref[...].astype(BF16), b_ref[...].astype(BF16), 0, 0)

    return pl.pallas_call(
        body, name=name, grid=(n_out, S5_BLOCKS),
        in_specs=[pl.BlockSpec((None, M, ka), (lambda o, q: (o, 0, q)) if na > 1 else (lambda o, q: (0, 0, q))),
                  pl.BlockSpec((None, M, kb), (lambda o, q: (o, 0, q)) if nb_ > 1 else (lambda o, q: (0, 0, q)))],
        out_specs=pl.BlockSpec((None, None, ka, kb), lambda o, q: (o, q, 0, 0)),
        out_shape=jax.ShapeDtypeStruct((n_out, S5_BLOCKS, ka, kb), F32), compiler_params=_cparams(),
    )(a, b)


def s5_u(proj, lp):
    def fn(row0, rv, pv):
        u, = rv
        return ((jnp.where(_row_mask(row0, u.shape), u, 0.0),), ())

    return rowwise(fn, [(proj, MIX, 5)], [], [(MIX, BF16)], name="s5_u", lp=lp)[0]


def s5_y(ys, proj, d, lp):
    def fn(row0, rv, pv):
        ys_, u = rv
        y = ys_ + pv[0] * u
        return ((y, _gelu(y)), ())

    return rowwise(fn, [(ys, MIX, 0), (proj, MIX, 5)], [d], [(MIX, F32), (MIX, BF16)], name="s5_y", lp=lp)


def s5_glu(z, y, b, lp):
    def fn(row0, rv, pv):
        z_, y_ = rv
        return ((_gelu(y_) * _sigmoid(z_ + pv[0]),), ())

    return rowwise(fn, [(z, MIX, 0), (y, MIX, 0)], [b], [(MIX, BF16)], name="s5_glu", lp=lp)[0]


def s5_glu_bwd(dgl, z, y, b, lp):
    def fn(row0, rv, pv):
        dgl_, z_, y_ = rv
        sg = _sigmoid(z_ + pv[0])
        dz = dgl_ * _gelu(y_) * sg * (1.0 - sg)
        return ((dgl_ * sg, dz), (jnp.sum(dz, axis=0, keepdims=True),))

    return rowwise(fn, [(dgl, MIX, 0), (z, MIX, 0), (y, MIX, 0)], [b], [(MIX, F32), (MIX, BF16)], [((1, MIX), F32)],
                   name="s5_glu_bwd", lp=lp)


def s5_y_bwd(dyg, y, proj, d, lp):
    def fn(row0, rv, pv):
        dyg_, y_, u = rv
        dy = dyg_ * _gelu_grad(y_)
        return ((dy, dy * pv[0]), (jnp.sum(dy * u, axis=0, keepdims=True),))

    return rowwise(fn, [(dyg, MIX, 0), (y, MIX, 0), (proj, MIX, 5)], [d], [(MIX, BF16), (MIX, F32)], [((1, MIX), F32)],
                   name="s5_y_bwd", lp=lp)


def s5_du(du, lp):
    def fn(row0, rv, pv):
        return ((jnp.where(_row_mask(row0, rv[0].shape), rv[0], 0.0),), ())

    return rowwise(fn, [(du, MIX, 0)], [], [(MIX, BF16)], name="s5_du", lp=lp)[0]


def merge_fwd(proj, ya, yb, yc, lp):
    def fn(row0, rv, pv):
        g0, g1, g2, a, b, c = rv
        return ((_sigmoid(g0) * a + _sigmoid(g1) * b + _sigmoid(g2) * c,), ())

    return rowwise(fn, [(proj, D_MODEL, 3), (proj, D_MODEL, 4), (proj, D_MODEL, 5), (ya, D_MODEL, 0), (yb, D_MODEL, 0),
                        (yc, D_MODEL, 0)], [], [(D_MODEL, BF16)], name="merge_fwd", lp=lp)[0]


def merge_bwd(dmix, proj, ya, yb, yc, lp):
    def fn(row0, rv, pv):
        dm, g0, g1, g2, a, b, c = rv
        outs_y, outs_g = [], []
        for g, yv in ((g0, a), (g1, b), (g2, c)):
            sg = _sigmoid(g)
            outs_y.append(dm * sg)
            outs_g.append(dm * yv * sg * (1.0 - sg))
        return (tuple(outs_y) + (jnp.concatenate(outs_g, axis=1),), ())

    return rowwise(fn, [(dmix, D_MODEL, 0), (proj, D_MODEL, 3), (proj, D_MODEL, 4), (proj, D_MODEL, 5),
                        (ya, D_MODEL, 0), (yb, D_MODEL, 0), (yc, D_MODEL, 0)], [],
                   [(D_MODEL, BF16)] * 3 + [(P_IN // 2, BF16, P_IN, 1)], name="merge_bwd", lp=lp)


def dproj_fill(dproj, dcq, dkr, dckv, dxbar, dbg, dcg, du, place, lp):
    tm = _row_tile(lp)

    def body(dproj_ref, dcq_ref, dkr_ref, dckv_ref, dx_ref, db_ref, dc_ref, du_ref, place_ref, o_ref):
        o_ref[:, 0:384] = dcq_ref[...]
        o_ref[:, 384:512] = _dot(dkr_ref[...], place_ref[...], 1, 0).astype(BF16)
        o_ref[:, 512:768] = dckv_ref[...]
        o_ref[:, 768:1024] = jnp.zeros((tm, 256), BF16)
        o_ref[:, 1024:1536] = dx_ref[...]
        o_ref[:, 1536:2048] = db_ref[...]
        o_ref[:, 2048:2560] = dc_ref[...]
        o_ref[:, 2560:3072] = du_ref[...]

    def row(w):
        return pl.BlockSpec((tm, w), lambda i: (i, 0))

    return pl.pallas_call(
        body, name="dproj_fill", grid=(lp // tm,),
        in_specs=[pl.BlockSpec(memory_space=pl.ANY), row(Q_RANK), row(D_ROPE), row(KV_RANK), row(MIX), row(MIX), row(MIX),
                  row(MIX), pl.BlockSpec(place.shape, lambda i: (0, 0))],
        out_specs=row(P_IN // 2), out_shape=jax.ShapeDtypeStruct((lp, P_IN), BF16),
        input_output_aliases={0: 0}, compiler_params=_cparams(),
    )(dproj, dcq, dkr, dckv, dxbar, dbg, dcg, du, place)


def loss_head(h, tgt, lp):
    def fn(row0, rv, pv):
        h_, t_ = rv
        live = (row0 + lax.broadcasted_iota(jnp.int32, h_.shape, 0)) >= X0
        diff = jnp.where(live, h_ - t_, 0.0)
        ssq = jnp.sum(jnp.sum(diff * diff, axis=1, keepdims=True), axis=0, keepdims=True)
        return ((diff * (1.0 / D_MODEL),), (ssq * (0.5 / D_MODEL),))

    return rowwise(fn, [(h, D_MODEL, 0), (tgt, D_MODEL, 0)], [], [(D_MODEL, F32)], [((1, 1), F32)], name="loss_head", lp=lp)


def _s5_consts(W):
    ab_re_rep, ab_im_rep, bb_re, bb_im = s5_prep(W['s5_a_re'], W['s5_a_im'], W['s5_log_dt'], W['s5_b_re'], W['s5_b_im'])
    pick = lambda t: t.reshape(S5_GROUPS, S5_GROUP, S5_STATE)[:, 0].reshape(1, S5_LANES)
    bb = jnp.stack([_blockdiag(bb_re.reshape(S5_GROUPS, S5_GROUP, S5_STATE)),
                    _blockdiag(bb_im.reshape(S5_GROUPS, S5_GROUP, S5_STATE))]).astype(BF16)
    return pick(ab_re_rep), pick(ab_im_rep), bb


def layer_fwd(h, hb, W, tabs, lp, ffn1=None, ffn2=True):
    cos, sin, rot = tabs[:3]
    h1, h1b, sv1 = ffn1 if ffn1 is not None else ffn_fwd(h, hb, W['wg1'], W['wu1'], W['wd1'], W['ln1_g'], W['ln1_b'], lp)
    proj = matmul(h1b[None], W['w_in'][None], tb=True, name="proj")[0]
    cqn, ckvn, kr = mla_prep(proj, cos, sin, rot, W['q_norm_g'], W['kv_norm_g'], lp)
    q96, k96, v = mla_heads(cqn, ckvn, kr, *tabs[3:], W['wq'], W['wk'], W['wv'], lp)
    o = attn_fwd(q96, k96, v, lp)
    ya = heads_out(o, W['mla_wo'], lp)
    vconv = conv_fwd(proj, W['conv_w'], W['conv_b'], lp)
    yb = matmul(vconv[None], W['conv_wout'][None], out_dtype=BF16, name="conv_out")[0]
    ub = s5_u(proj, lp)
    ab_re, ab_im, bb = _s5_consts(W)
    bu = bd_matmul(ub[None], bb, w_t=False, reduce=False, out_dtype=BF16, name="s5_bu")
    s = s5_scan(bu, ab_re, ab_im, lp)
    ys = bd_matmul(s, W['s5_ct'], w_t=False, reduce=True, name="s5_cs")[0]
    y, ygb = s5_y(ys, proj, W['s5_d'], lp)
    zg = matmul(ygb[None], W['s5_wglu'][None], name="s5_glu_mm")[0]
    glb = s5_glu(zg, y, W['s5_b_glu'], lp)
    yc = matmul(glb[None], W['s5_wout'][None], out_dtype=BF16, name="s5_out")[0]
    mixed = merge_fwd(proj, ya, yb, yc, lp)
    z2, h2, h2b = mm_res_ln(mixed[None], W['w_o'][None], h1, W['ln2_g'], W['ln2_b'], scale=1.0, name="wo_ln", lp=lp)
    sv = dict(sv1=sv1, h1b=h1b, proj=proj, cqn=cqn, ckvn=ckvn, q96=q96, k96=k96, v=v, o=o, ya=ya,
              vconv=vconv, yb=yb, ub=ub, ab_re=ab_re, ab_im=ab_im, bb=bb, s=s, y=y, ygb=ygb, zg=zg, glb=glb, yc=yc,
              mixed=mixed, z2=z2)
    if not ffn2:
        return h2, h2b, sv
    h3, h3b, sv['sv3'] = ffn_fwd(h2, h2b, W['wg2'], W['wu2'], W['wd2'], W['ln3_g'], W['ln3_b'], lp)
    return h3, h3b, sv


def layer_bwd(dh3, sv, W, tabs, lp, ffn1=True, ffn2=True):
    cos, sin, rot = tabs[:3]
    proj = sv['proj']
    G = {}
    dh2 = dh3
    if ffn2:
        dh2, g3 = ffn_bwd(dh3, sv['sv3'], W['wg2'], W['wu2'], W['wd2'], W['ln3_g'], lp)
        G.update(wg2=g3['wg'], wu2=g3['wu'], wd2=g3['wd'], ln3_g=g3['ln_g'], ln3_b=g3['ln_b'])
    dz2, dz2b, G['ln2_g'], G['ln2_b'] = ln_bwd(dh2, sv['z2'], W['ln2_g'], fscale=1.0, name="wo_ln_bwd", lp=lp)
    dmix = matmul(dz2b[None], W['w_o'][None], tb=True, name="wo_dx")[0]
    G['w_o'] = matmul(sv['mixed'][None], dz2b[None], ta=True, out_dtype=WGRAD, name="wo_dw")[0]
    dya, dyb, dyc, dproj = merge_bwd(dmix, proj, sv['ya'], sv['yb'], sv['yc'], lp)
    dgl = matmul(dyc[None], W['s5_wout'][None], tb=True, name="s5_out_dx")[0]
    G['s5_wout'] = matmul(sv['glb'][None], dyc[None], ta=True, out_dtype=WGRAD, name="s5_out_dw")[0]
    t1, dzb, G['s5_b_glu'] = s5_glu_bwd(dgl, sv['zg'], sv['y'], W['s5_b_glu'], lp)
    dyg = matmul(dzb[None], W['s5_wglu'][None], tb=True, res=t1[None], name="s5_glu_dx")[0]
    G['s5_wglu'] = matmul(sv['ygb'][None], dzb[None], ta=True, out_dtype=WGRAD, name="s5_glu_dw")[0]
    dyb_, du_d, G['s5_d'] = s5_y_bwd(dyg, sv['y'], proj, W['s5_d'], lp)
    ds = bd_matmul(dyb_[None], W['s5_ct'], w_t=True, reduce=False, out_dtype=BF16, name="s5_cs_dx")
    G['s5_ct'] = bd_outer(sv['s'], dyb_[None], "s5_cs_dw")
    g_adj, d_ab = s5_scan_bwd(ds, sv['s'], sv['ab_re'], sv['ab_im'], lp)
    du = bd_matmul(g_adj, sv['bb'], w_t=True, reduce=True, res=du_d[None], name="s5_bu_dx")[0]
    d_bb = bd_outer(sv['ub'][None], g_adj, "s5_bu_dw")
    du_b = s5_du(du, lp)
    onehot = (jnp.arange(S5_GROUP) == 0).astype(F32)
    spread = lambda t: (t.reshape(S5_GROUPS, 1, S5_STATE) * onehot[None, :, None]).reshape(_S5_ROWS, S5_STATE)
    take = lambda t: _blockdiag_extract(t, S5_GROUP, S5_STATE).reshape(_S5_ROWS, S5_STATE)
    sel = jnp.kron(jnp.eye(S5_GROUPS, dtype=F32), jnp.ones((1, S5_GROUP), F32))
    (G['s5_a_re'], G['s5_a_im'], G['s5_log_dt'], G['s5_b_re'], G['s5_b_im']) = s5_prep_bwd(
        W['s5_a_re'], W['s5_a_im'], W['s5_log_dt'], W['s5_b_re'], W['s5_b_im'],
        spread(d_ab[0]), spread(d_ab[1]), take(d_bb[0]), take(d_bb[1]), sel)
    dv = matmul(dyb[None], W['conv_wout'][None], tb=True, name="conv_out_dx")[0]
    G['conv_wout'] = matmul(sv['vconv'][None], dyb[None], ta=True, out_dtype=WGRAD, name="conv_out_dw")[0]
    dxbar, dbg, dcg, G['conv_w'], G['conv_b'] = conv_bwd(dv, proj, W['conv_w'], W['conv_b'], lp)
    do = heads_out_dx(dya, W['mla_wo'], lp)
    G['mla_wo'] = matmul(sv['o'], dya[None], ta=True, ab='o', out_dtype=WGRAD, name="mla_out_dw")
    dq96, dk96, dvv = attn_bwd(sv['q96'], sv['k96'], sv['v'], do, lp)
    dcq, dckv, dqp, dkr = mla_heads_bwd(dq96, dk96, dvv, *tabs[3:], W['wq'], W['wk'], W['wv'], lp)
    G['wq'] = matmul(sv['cqn'][None], dqp, ta=True, bb='o', out_dtype=WGRAD, name="mla_dwq")
    G['wk'] = matmul(sv['ckvn'][None], dk96, ta=True, bb='o', out_dtype=WGRAD, name="mla_dwk")
    G['wv'] = matmul(sv['ckvn'][None], dvv, ta=True, bb='o', out_dtype=WGRAD, name="mla_dwv")
    dcq_raw, dckv_raw, dkr_raw, G['q_norm_g'], G['kv_norm_g'] = mla_prep_bwd(
        dcq, dckv, dkr, proj, cos, sin, rot, W['q_norm_g'], W['kv_norm_g'], lp)
    dproj = dproj_fill(dproj, dcq_raw, dkr_raw, dckv_raw, dxbar, dbg, dcg, du_b, jnp.eye(D_ROPE, 128, dtype=BF16), lp)
    dh1 = matmul(dproj[None], W['w_in'][None], res=dz2[None], res_scale=ALPHA, name="proj_dx")[0]
    G['w_in'] = matmul(dproj[None], sv['h1b'][None], ta=True, out_dtype=WGRAD, name="proj_dw")[0]
    if not ffn1:
        return dh1, G
    dh0, g1 = ffn_bwd(dh1, sv['sv1'], W['wg1'], W['wu1'], W['wd1'], W['ln1_g'], lp)
    G.update(wg1=g1['wg'], wu1=g1['wu'], wd1=g1['wd'], ln1_g=g1['ln_g'], ln1_b=g1['ln_b'])
    return dh0, G


def _nat_cols(st):
    return jnp.transpose(st, (1, 0, 2)).reshape(st.shape[1], -1)


def _shard_cols(nat):
    k, n = nat.shape
    return jnp.transpose(nat.reshape(k, N_SHARD, n // N_SHARD), (1, 0, 2))


def _win_pad(wt):
    z = lambda n: jnp.zeros((n, wt.shape[1]), wt.dtype)
    return jnp.concatenate([wt[0:384], wt[640:672], z(96), wt[384:640], z(256), wt[672:]], axis=0)


def _win_unpad(wp):
    return jnp.concatenate([wp[0:384], wp[512:768], wp[384:416], wp[1024:]], axis=0)


_BIG = [('ffn1_w_gate', 'T'), ('ffn1_w_up', 'T'), ('ffn1_w_down', 0), ('w_in', 'T'), ('mla_w_uq', 1), ('mla_w_ukv', 1),
        ('mla_w_o', 1), ('conv_w_out', 1), ('s5_w_glu', 0), ('s5_w_out', 1), ('w_o', 0),
        ('ffn2_w_gate', 'T'), ('ffn2_w_up', 'T'), ('ffn2_w_down', 0)]
_REPL = ['ln1_g', 'ln1_b', 'mla_q_norm_g', 'mla_kv_norm_g', 'conv_b', 's5_a_re', 's5_a_im', 's5_log_dt', 's5_b_re',
         's5_b_im', 's5_c_re', 's5_c_im', 's5_d', 's5_b_glu', 'ln2_g', 'ln2_b', 'ln3_g', 'ln3_b']


def compute_weights(st, small):
    W = {}
    for t in ('1', '2'):
        if 'ffn%s_w_gate' % t in st:
            W['wg' + t], W['wu' + t], W['wd' + t] = (st['ffn%s_w_%s' % (t, p)] for p in ('gate', 'up', 'down'))
    if 'w_in' in st:
        W.update(_mixer_weights(st))
    if small is not None:
        W.update(_small_weights(small))
    return W


def _mixer_weights(st):
    W = {}
    W['w_in'] = _win_pad(st['w_in'].reshape(D_IN, D_MODEL))
    W['wq'] = jnp.transpose(_nat_cols(st['mla_w_uq']).reshape(Q_RANK, N_HEADS, D_QK), (1, 0, 2))
    ukv = jnp.transpose(_nat_cols(st['mla_w_ukv']).reshape(KV_RANK, N_HEADS, D_NOPE + D_V), (1, 0, 2))
    W['wk'] = jnp.concatenate([ukv[:, :, :D_NOPE], jnp.zeros((N_HEADS, KV_RANK, D_ROPE), ukv.dtype)], axis=2)
    W['wv'] = ukv[:, :, D_NOPE:]
    W['mla_wo'] = _nat_cols(st['mla_w_o']).reshape(N_HEADS, D_V, D_MODEL)
    W['conv_wout'] = _nat_cols(st['conv_w_out'])
    W['s5_wglu'] = st['s5_w_glu'].reshape(MIX, MIX)
    W['s5_wout'] = _nat_cols(st['s5_w_out'])
    W['w_o'] = st['w_o'].reshape(D_MODEL, D_MODEL)
    return W


def _small_weights(small):
    W = {}
    W['conv_w'] = small['conv_w']
    for n in ('ln1_g', 'ln1_b', 'ln2_g', 'ln2_b', 'ln3_g', 'ln3_b', 'conv_b', 's5_b_glu'):
        W[n] = small[n].reshape(1, -1)
    W['q_norm_g'] = small['mla_q_norm_g'].reshape(1, -1)
    W['kv_norm_g'] = small['mla_kv_norm_g'].reshape(1, -1)
    W['s5_d'] = small['s5_d'].reshape(1, MIX)
    rep = lambda t: jnp.repeat(t, S5_GROUP, axis=0)
    W['s5_a_re'], W['s5_a_im'] = rep(small['s5_a_re']), rep(small['s5_a_im'])
    W['s5_log_dt'] = jnp.broadcast_to(rep(small['s5_log_dt'].reshape(S5_GROUPS, 1)), (_S5_ROWS, S5_STATE))
    tr = lambda t: jnp.transpose(t, (0, 2, 1)).reshape(_S5_ROWS, S5_STATE)
    W['s5_b_re'], W['s5_b_im'] = tr(small['s5_b_re']), tr(small['s5_b_im'])
    ct = lambda t: _blockdiag(jnp.transpose(t, (0, 2, 1)))
    W['s5_ct'] = jnp.stack([ct(small['s5_c_re']), -ct(small['s5_c_im'])]).astype(BF16)
    return W


def reference_grads(G, ffn=True):
    R = {}
    for t in ('1', '2') if ffn else ():
        R['ffn%s_w_gate' % t] = G['wg' + t].reshape(D_FF, D_MODEL).T
        R['ffn%s_w_up' % t] = G['wu' + t].reshape(D_FF, D_MODEL).T
        R['ffn%s_w_down' % t] = G['wd' + t].reshape(D_FF, D_MODEL)
    R['w_in_t'] = _win_unpad(G['w_in'])
    if ffn:
        R['w_in'] = R['w_in_t'].T
    R['mla_w_uq'] = jnp.transpose(G['wq'], (1, 0, 2)).reshape(Q_RANK, -1)
    R['mla_w_ukv'] = jnp.transpose(jnp.concatenate([G['wk'][:, :, :D_NOPE], G['wv']], axis=2), (1, 0, 2)).reshape(KV_RANK, -1)
    R['mla_w_o'] = G['mla_wo'].reshape(N_HEADS * D_V, D_MODEL)
    R['conv_w'], R['conv_w_out'] = G['conv_w'], G['conv_wout']
    R['s5_w_glu'], R['s5_w_out'], R['w_o'] = G['s5_wglu'], G['s5_wout'], G['w_o']
    for n in ('ln1_g', 'ln1_b', 'ln2_g', 'ln2_b', 'ln3_g', 'ln3_b', 'conv_b', 's5_b_glu'):
        if n in G:
            R[n] = G[n].reshape(-1)
    R['mla_q_norm_g'], R['mla_kv_norm_g'] = G['q_norm_g'].reshape(-1), G['kv_norm_g'].reshape(-1)
    R['s5_d'] = G['s5_d'].reshape(S5_GROUPS, S5_GROUP)
    R['s5_a_re'], R['s5_a_im'], R['s5_log_dt'] = G['s5_a_re'], G['s5_a_im'], G['s5_log_dt'].reshape(-1)
    untr = lambda t: jnp.transpose(t.reshape(S5_GROUPS, S5_GROUP, S5_STATE), (0, 2, 1))
    R['s5_b_re'], R['s5_b_im'] = untr(G['s5_b_re']), untr(G['s5_b_im'])
    unct = lambda t: jnp.transpose(_blockdiag_extract(t, S5_STATE, S5_GROUP), (0, 2, 1))
    R['s5_c_re'], R['s5_c_im'] = unct(G['s5_ct'][0]), -unct(G['s5_ct'][1])
    return R


_ANY = pl.BlockSpec(memory_space=pl.ANY)
LANES = 1024


def _place():
    x, y, c = lax.axis_index("x"), lax.axis_index("y"), lax.axis_index("c")
    chips = [(1 - x, y), (x, 1 - y), (1 - x, 1 - y)]
    return x, y, c, chips


def _rows_of(c, half):
    return pl.ds(pl.multiple_of(c * half, 8), half)


def all_gather_shards(srcs, exact):
    n, m = len(srcs), len(exact)
    halves = [s.shape[0] // 2 for s in srcs]

    def body(*refs):
        s_refs, e_refs = refs[:n], refs[n:n + m]
        o_refs, eo_refs = refs[n + m:2 * n + m], refs[2 * n + m:2 * n + 2 * m]
        send, recv, esend, erecv, osend, orecv, lsem = refs[2 * n + 2 * m:]
        x, y, c, chips = _place()
        me = 2 * x + y
        sibling = (x, y, 1 - c)
        own = [pltpu.make_async_remote_copy(src_ref=s_refs[k], dst_ref=o_refs[k].at[me], send_sem=osend.at[k],
                                            recv_sem=orecv.at[k], device_id=sibling, device_id_type=MESH) for k in range(n)]
        local = [pltpu.make_async_copy(e_refs[k], eo_refs[k].at[me], lsem.at[k]) for k in range(m)]
        for cp in own + local:
            cp.start()

        def copy(k, s, src, idx, half_c, to):
            return pltpu.make_async_remote_copy(
                src_ref=src, dst_ref=o_refs[k].at[idx, _rows_of(half_c, halves[k])], send_sem=send.at[6 * k + s],
                recv_sem=recv.at[6 * k + s], device_id=to, device_id_type=MESH)

        def ecopy(k, j, idx, to):
            return pltpu.make_async_remote_copy(src_ref=e_refs[k], dst_ref=eo_refs[k].at[idx], send_sem=esend.at[3 * k + j],
                                                recv_sem=erecv.at[3 * k + j], device_id=to, device_id_type=MESH)

        sends = []
        for k in range(n):
            mine = s_refs[k].at[_rows_of(c, halves[k])]
            sends += [copy(k, j, mine, me, c, (*chip, c)) for j, chip in enumerate(chips)]
        for k in range(m):
            sends += [ecopy(k, j, me, (*chip, c)) for j, chip in enumerate(chips)]
        for cp in sends:
            cp.start()
        for j, chip in enumerate(chips):
            idx = 2 * chip[0] + chip[1]
            for k in range(n):
                landed = o_refs[k].at[idx, _rows_of(c, halves[k])]
                copy(k, j, landed, idx, c, sibling).wait_recv()
                fwd = copy(k, 3 + j, landed, idx, c, sibling)
                fwd.start()
                sends.append(fwd)
        for j, chip in enumerate(chips):
            idx = 2 * chip[0] + chip[1]
            for k in range(n):
                copy(k, 3 + j, s_refs[k].at[_rows_of(c, halves[k])], idx, 1 - c, sibling).wait_recv()
            for k in range(m):
                ecopy(k, j, idx, sibling).wait_recv()
        for cp in sends:
            cp.wait_send()
        for cp in own + local:
            cp.wait()

    outs = pl.pallas_call(
        body, name="all_gather_weights", in_specs=[_ANY] * (n + m), out_specs=[_ANY] * (n + m),
        out_shape=[jax.ShapeDtypeStruct((N_SHARD,) + a.shape, a.dtype) for a in list(srcs) + list(exact)],
        scratch_shapes=[pltpu.SemaphoreType.DMA((6 * n,)), pltpu.SemaphoreType.DMA((6 * n,)),
                        pltpu.SemaphoreType.DMA((3 * m,)), pltpu.SemaphoreType.DMA((3 * m,)),
                        pltpu.SemaphoreType.DMA((n,)), pltpu.SemaphoreType.DMA((n,)), pltpu.SemaphoreType.DMA((m,))],
    )(*srcs, *exact)
    return outs[:n], outs[n:]


def rs_pair_swap(gs):
    n = len(gs)

    def body(*refs):
        g_refs, r_refs, send, recv = refs[:n], refs[n:2 * n], refs[2 * n], refs[2 * n + 1]
        x, y, c, _ = _place()
        copies = [pltpu.make_async_remote_copy(
            src_ref=g_refs[k].at[pl.ds(0, N_SHARD), _rows_of(1 - c, gs[k].shape[1] // 2)], dst_ref=r_refs[k],
            send_sem=send.at[k], recv_sem=recv.at[k], device_id=(x, y, 1 - c), device_id_type=MESH) for k in range(n)]
        for cp in copies:
            cp.start()
        for cp in copies:
            cp.wait()

    return pl.pallas_call(
        body, name="grad_pair_swap", in_specs=[_ANY] * n, out_specs=[_ANY] * n,
        out_shape=[jax.ShapeDtypeStruct((N_SHARD, g.shape[1] // 2, g.shape[2]), g.dtype) for g in gs],
        scratch_shapes=[pltpu.SemaphoreType.DMA((n,)), pltpu.SemaphoreType.DMA((n,))],
    )(*gs)


def _group_tile(half, n_cols, n_arrays):
    budget = (20 * 2 ** 20) // (6 * n_arrays)
    fits = [t for t in range(8, half + 1, 8) if half % t == 0 and t * n_cols * 4 <= budget]
    return max(fits) if fits else 8


def rs_pair_add(gs, rs, cidx, out_dtype, name):
    n = len(gs)
    _, K, cols = gs[0].shape
    half = K // 2
    tr = _group_tile(half, cols, n)
    nb = half // tr

    def body(c_ref, *refs):
        for g_ref, r_ref, o_ref in zip(refs[:n], refs[n:2 * n], refs[2 * n:]):
            o_ref[...] = (g_ref[...].astype(F32) + r_ref[...].astype(F32)).astype(out_dtype)

    gspec = pl.BlockSpec((None, tr, cols), lambda j, i, c: (j, c[0] * nb + i, 0))
    rspec = pl.BlockSpec((None, tr, cols), lambda j, i, c: (j, i, 0))
    return pl.pallas_call(
        body, name=name,
        grid_spec=pltpu.PrefetchScalarGridSpec(num_scalar_prefetch=1, grid=(N_SHARD, nb), in_specs=[gspec] * n + [rspec] * n,
                                               out_specs=[rspec] * n),
        out_shape=[jax.ShapeDtypeStruct((N_SHARD, half, cols), out_dtype)] * n,
        compiler_params=_cparams(),
    )(cidx, *gs, *rs)


def rs_chip_sum(qs, nl, cidx, name):
    n = len(qs)
    _, half, cols = qs[0].shape
    tr = _group_tile(half, cols, n)
    nb = half // tr

    def body(c_ref, *refs):
        for k, q_ref in enumerate(refs[:n]):
            o_ref = refs[n + k // nl]
            o_ref[k % nl] = ((q_ref[0].astype(F32) + q_ref[1].astype(F32)) + q_ref[2].astype(F32)) + q_ref[3].astype(F32)

    return pl.pallas_call(
        body, name=name,
        grid_spec=pltpu.PrefetchScalarGridSpec(
            num_scalar_prefetch=1, grid=(nb,),
            in_specs=[pl.BlockSpec((N_SHARD, tr, cols), lambda i, c: (0, i, 0))] * n,
            out_specs=[pl.BlockSpec((nl, tr, cols), lambda i, c: (0, c[0] * nb + i, 0))] * (n // nl)),
        out_shape=[jax.ShapeDtypeStruct((nl, 2 * half, cols), F32)] * (n // nl),
        compiler_params=_cparams(),
    )(cidx, *qs)


def rs_pair_gather(fs, name):
    n = len(fs)

    def body(*refs):
        f_refs, send, recv = refs[n:2 * n], refs[2 * n], refs[2 * n + 1]
        x, y, c, _ = _place()
        copies = []
        for k in range(n):
            rows = f_refs[k].at[pl.ds(0, fs[k].shape[0]), _rows_of(c, fs[k].shape[1] // 2)]
            copies.append(pltpu.make_async_remote_copy(src_ref=rows, dst_ref=rows, send_sem=send.at[k], recv_sem=recv.at[k],
                                                       device_id=(x, y, 1 - c), device_id_type=MESH))
        for cp in copies:
            cp.start()
        for cp in copies:
            cp.wait()

    return pl.pallas_call(
        body, name=name, in_specs=[_ANY] * n, out_specs=[_ANY] * n,
        out_shape=[jax.ShapeDtypeStruct(f.shape, f.dtype) for f in fs],
        input_output_aliases={k: k for k in range(n)},
        scratch_shapes=[pltpu.SemaphoreType.DMA((n,)), pltpu.SemaphoreType.DMA((n,))],
    )(*fs)


_HBM = pl.BlockSpec(memory_space=pltpu.HBM)
_SEM = pl.BlockSpec(memory_space=pltpu.SEMAPHORE)
_EFFECT = pltpu.SideEffectType.DATAFLOW_SIDE_EFFECTING


def _in_hbm(a):
    return pltpu.with_memory_space_constraint(a, pltpu.HBM)


def split_start(name, srcs, lands, after, copies_fn, n_copies):
    n = len(srcs)

    def body(*refs):
        for cp in copies_fn(refs[:n], refs[n:2 * n], refs[2 * n + 1], refs[2 * n + 2]):
            cp.start()
        refs[-1][...] = jnp.zeros_like(refs[-1])

    bufs = list(srcs) + list(lands)
    outs = pl.pallas_call(
        body, name=name,
        out_shape=(pltpu.SemaphoreType.DMA((n_copies,)), pltpu.SemaphoreType.DMA((n_copies,)),
                   *[pltpu.HBM(a.shape, a.dtype) for a in bufs], jax.ShapeDtypeStruct((8, 128), F32)),
        in_specs=[_HBM] * (2 * n) + [_ANY],
        out_specs=(_SEM, _SEM, *[_HBM] * (2 * n), pl.BlockSpec(memory_space=pltpu.VMEM)),
        input_output_aliases={i: 2 + i for i in range(2 * n)},
        compiler_params=pltpu.CompilerParams(has_side_effects=_EFFECT),
    )(*[_in_hbm(a) for a in bufs], after)
    return outs[0], outs[1], outs[2:2 + n], outs[2 + n:2 + 2 * n], outs[-1]


def split_wait(name, send, recv, srcs, lands, after, copies_fn, which=None):
    n = len(srcs)

    def body(*refs):
        copies = copies_fn(refs[:n], refs[n:2 * n], refs[2 * n], refs[2 * n + 1], which)
        for cp in copies:
            cp.wait_send()
        for cp in copies:
            cp.wait_recv()

    bufs = list(srcs) + list(lands)
    outs = pl.pallas_call(
        body, name=name, out_shape=tuple(pltpu.HBM(a.shape, a.dtype) for a in bufs),
        in_specs=[_HBM] * (2 * n) + [_SEM, _SEM, _ANY], out_specs=tuple([_HBM] * (2 * n)),
        input_output_aliases={i: i for i in range(2 * n)},
        compiler_params=pltpu.CompilerParams(has_side_effects=_EFFECT),
    )(*bufs, send, recv, after)
    return list(outs[:n]), list(outs[n:])


def _gather_copies(s_refs, l_refs, send, recv, which=None):
    x, y, c, chips = _place()
    me = 2 * x + y
    out = []
    for k in (range(len(s_refs)) if which is None else which):
        s, l = s_refs[k], l_refs[k]
        rows = _rows_of(c, s.shape[0] // 2)
        for j, chip in enumerate(chips):
            out.append(pltpu.make_async_remote_copy(src_ref=s.at[rows], dst_ref=l.at[me, rows], send_sem=send.at[4 * k + j],
                                                    recv_sem=recv.at[4 * k + j], device_id=(*chip, c), device_id_type=MESH))
        out.append(pltpu.make_async_remote_copy(src_ref=s, dst_ref=l.at[me], send_sem=send.at[4 * k + 3],
                                                recv_sem=recv.at[4 * k + 3], device_id=(x, y, 1 - c), device_id_type=MESH))
    return out


def _scatter_copies(s_refs, l_refs, send, recv, which=None):
    x, y, c, chips = _place()
    me = 2 * x + y
    return [pltpu.make_async_remote_copy(src_ref=s_refs[k].at[2 * chip[0] + chip[1]], dst_ref=l_refs[k].at[me],
                                         send_sem=send.at[3 * k + j], recv_sem=recv.at[3 * k + j], device_id=(*chip, c),
                                         device_id_type=MESH)
            for k in (range(len(s_refs)) if which is None else which) for j, chip in enumerate(chips)]


def gather_forward(lands, name):
    n = len(lands)

    def body(*refs):
        l_refs, send, recv = refs[n:2 * n], refs[2 * n], refs[2 * n + 1]
        x, y, c, chips = _place()
        copies = []
        for k in range(n):
            rows = _rows_of(c, lands[k].shape[1] // 2)
            for j, chip in enumerate(chips):
                part = l_refs[k].at[2 * chip[0] + chip[1], rows]
                copies.append(pltpu.make_async_remote_copy(src_ref=part, dst_ref=part, send_sem=send.at[3 * k + j],
                                                           recv_sem=recv.at[3 * k + j], device_id=(x, y, 1 - c),
                                                           device_id_type=MESH))
        for cp in copies:
            cp.start()
        for cp in copies:
            cp.wait()

    return pl.pallas_call(
        body, name=name, in_specs=[_ANY] * n, out_specs=[_ANY] * n,
        out_shape=[jax.ShapeDtypeStruct(a.shape, a.dtype) for a in lands],
        input_output_aliases={k: k for k in range(n)},
        scratch_shapes=[pltpu.SemaphoreType.DMA((3 * n,)), pltpu.SemaphoreType.DMA((3 * n,))],
    )(*lands)


def rs_partials(gs, wire, cidx, tag):
    rs = rs_pair_swap(gs)
    groups = {}
    for k, g in enumerate(gs):
        groups.setdefault((g.shape, jnp.dtype(wire[k]).name), []).append(k)
    ps = [None] * len(gs)
    for gi, ks in enumerate(groups.values()):
        outs = rs_pair_add([gs[k] for k in ks], [rs[k] for k in ks], cidx, wire[ks[0]], "grad_pair_add_%s%d" % (tag, gi))
        for k, o in zip(ks, outs):
            ps[k] = o
    return ps


def rs_finish(items, tag):
    cidx = lax.axis_index("c").astype(jnp.int32).reshape(1)
    groups = {}
    for i, it in enumerate(items):
        groups.setdefault((it[0].shape, len(it), it[0].dtype.name), []).append(i)
    fs = [None] * len(items)
    for gi, ids in enumerate(groups.values()):
        outs = rs_chip_sum([q for i in ids for q in items[i]], len(items[ids[0]]), cidx, "grad_chip_sum_%s%d" % (tag, gi))
        for i, o in zip(ids, outs):
            fs[i] = o
    return rs_pair_gather(fs, "grad_pair_gather_" + tag)


def adamw(w, g, m, v, name):
    shape = w.shape
    if w.ndim == 2:
        block, grid, index = shape, (1,), (lambda i: (0, 0))
    else:
        slab = shape[2:]
        unit = 4 * int(np.prod(slab[:-2] or (1,))) * (-(-slab[-1] // 128) * 128)
        if len(slab) >= 2:
            unit *= -(-slab[-2] // 8) * 8
        k = shape[1]
        tr = k
        if k * unit > 2 ** 21:
            tr = max(t for t in range(8, k, 8) if k % t == 0 and t * unit <= 2 ** 21)
        block, grid = (None, tr) + tuple(slab), (shape[0], k // tr)
        index = lambda l, i: (l, i) + (0,) * len(slab)
        if tr < min(k, 64) and len(slab) == 1:
            tc = max(t for t in range(128, slab[0] + 1, 128) if slab[0] % t == 0 and k * t * 4 <= 2 ** 21)
            block, grid = (None, k, tc), (shape[0], slab[0] // tc)
            index = lambda l, i: (l, 0, i)

    def body(w_ref, g_ref, m_ref, v_ref, d_ref, nm_ref, nv_ref):
        g_ = g_ref[...]
        m_new = ADAM_B1 * m_ref[...] + (1.0 - ADAM_B1) * g_
        v_new = ADAM_B2 * v_ref[...] + (1.0 - ADAM_B2) * (g_ * g_)
        m_hat = m_new / (1.0 - ADAM_B1 ** ADAM_STEP)
        v_hat = v_new / (1.0 - ADAM_B2 ** ADAM_STEP)
        d_ref[...] = -ADAM_LR * (m_hat / (jnp.sqrt(v_hat) + ADAM_EPS) + ADAM_WD * w_ref[...])
        nm_ref[...] = m_new
        nv_ref[...] = v_new

    spec = pl.BlockSpec(block, index)
    return pl.pallas_call(
        body, name=name, grid=grid, in_specs=[spec] * 4, out_specs=[spec] * 3,
        out_shape=[jax.ShapeDtypeStruct(shape, F32)] * 3, compiler_params=_cparams(),
    )(w, g, m, v)


_WEIGHTS = ['meta', 'ffn1_w_gate', 'ffn1_w_up', 'ffn1_w_down', 'ln1_g', 'ln1_b', 'w_in', 'mla_q_norm_g', 'mla_w_uq',
            'mla_kv_norm_g', 'mla_w_ukv', 'mla_w_o', 'conv_w', 'conv_b', 'conv_w_out', 's5_a_re', 's5_a_im', 's5_log_dt',
            's5_b_re', 's5_b_im', 's5_c_re', 's5_c_im', 's5_d', 's5_w_glu', 's5_b_glu', 's5_w_out', 'w_o', 'ln2_g', 'ln2_b',
            'ffn2_w_gate', 'ffn2_w_up', 'ffn2_w_down', 'ln3_g', 'ln3_b']


def _pad_to(flat, n):
    return jnp.concatenate([flat, jnp.zeros((n - flat.shape[0],), flat.dtype)])


def _shard_of(full, axis):
    if axis == 1:
        return _shard_cols(full)
    if axis == 'T':
        return full.T.reshape(N_SHARD, full.shape[1] // N_SHARD, full.shape[0])
    return full.reshape(N_SHARD, full.shape[0] // N_SHARD, full.shape[1])


_FFN_KEY = {'gate': 'wg', 'up': 'wu', 'down': 'wd'}


def _pad_rows(a, axis):
    k = a.shape[axis]
    extra = -k % 32
    if not extra:
        return a
    return jnp.pad(a, [(0, extra) if d == axis else (0, 0) for d in range(a.ndim)])


def _step(env):
    w = {n: env[n] for n in _WEIGHTS}
    mom = {n: env['m_' + n] for n in _WEIGHTS}
    var = {n: env['v_' + n] for n in _WEIGHTS}
    cidx = lax.axis_index("c").astype(jnp.int32).reshape(1)
    chip = 2 * lax.axis_index("x") + lax.axis_index("y")
    big_names = [n for n, _ in _BIG]
    nb = len(big_names)

    kept_t = [n for n, a in _BIG if a == 'T']
    own = {n: (jnp.swapaxes(w[n], 1, 2) if n in kept_t else w[n]) for n in big_names}
    first = [n for n in big_names if n.startswith('ffn1')]
    mix = [n for n in big_names if not n.startswith('ffn')]
    last = [n for n in big_names if n.startswith('ffn2')]
    rest = mix + last
    nm, nr = len(mix), len(mix) + len(last)
    src = lambda n, li: _pad_rows(own[n][li].astype(BF16), 0)
    gathered_first, (conv_w_st, meta_st) = all_gather_shards([src(n, 0) for n in first], [w['conv_w'], w['meta']])
    later = [src(n, 0) for n in rest] + [src(n, 1) for n in big_names]
    lands = [lax.empty((N_SHARD,) + s.shape, BF16) for s in later]
    g_send, g_recv, later_t, lands_t, token = split_start("gather_start", later, lands, gathered_first[0], _gather_copies,
                                                          4 * len(later))

    def weights_of(names, st, li, with_small):
        small = None
        if with_small:
            small = {n: w[n][li] for n in _REPL}
            small['conv_w'] = _nat_cols(conv_w_st[:, li])
        return compute_weights({n: a[:, :own[n].shape[1]] for n, a in zip(names, st)}, small)

    x2d = env['x'][0]
    lp = x2d.shape[0] + X0
    tabs = _rope_tables(lp)
    h = jnp.concatenate([jnp.zeros((PAD, D_MODEL), F32), _nat_cols(meta_st), x2d], axis=0) + token[0, 0]
    W0 = weights_of(first, gathered_first, 0, True)
    ffn1 = ffn_fwd(h, h.astype(BF16), W0['wg1'], W0['wu1'], W0['wd1'], W0['ln1_g'], W0['ln1_b'], lp)
    later_t, lands_t = split_wait("gather0_wait", g_send, g_recv, later_t, lands_t, ffn1[0], _gather_copies, range(nm))
    W0.update(weights_of(mix, gather_forward(lands_t[:nm], "gather0_forward"), 0, False))
    h, hb, sv0 = layer_fwd(None, None, W0, tabs, lp, ffn1=ffn1, ffn2=False)
    later_t, lands_t = split_wait("gather0b_wait", g_send, g_recv, later_t, lands_t, h, _gather_copies, range(nm, nr))
    W0.update(weights_of(last, gather_forward(lands_t[nm:nr], "gather0b_forward"), 0, False))
    h, hb, sv0['sv3'] = ffn_fwd(h, hb, W0['wg2'], W0['wu2'], W0['wd2'], W0['ln3_g'], W0['ln3_b'], lp)
    _, lands_t = split_wait("gather1_wait", g_send, g_recv, later_t, lands_t, h, _gather_copies, range(nr, len(later)))
    W1 = weights_of(big_names, gather_forward(lands_t[nr:], "gather1_forward"), 1, True)
    h, hb, sv1 = layer_fwd(h, hb, W1, tabs, lp)
    tgt = jnp.concatenate([jnp.zeros((X0, D_MODEL), F32), env['loss_target'][0]], axis=0)
    dh, loss_part = loss_head(h, tgt, lp)
    loss = lax.psum(loss_part[0, 0], ("x", "y", "c"))

    def shards(G, names):
        full = None if all(n.startswith('ffn') for n in names) else reference_grads(G, ffn=False)

        def one(n, a):
            if n.startswith('ffn'):
                return G[_FFN_KEY[n.split('_')[-1]] + n[3]]
            if n == 'w_in':
                return full['w_in_t'].reshape(N_SHARD, D_IN // N_SHARD, D_MODEL)
            return _shard_of(full[n], a)

        return [_pad_rows(one(n, a), 1) for n, a in _BIG if n in names]

    def scatter_start(name, ps, after):
        qs = [lax.dynamic_update_slice_in_dim(jnp.zeros_like(p), lax.dynamic_slice_in_dim(p, chip, 1, axis=0), chip, axis=0)
              for p in ps]
        return split_start(name, ps, qs, after, _scatter_copies, 3 * len(ps))

    dh, G1 = layer_bwd(dh, sv1, W1, tabs, lp)
    p1 = rs_partials(shards(G1, big_names), [BF16] * nb, cidx, "b")
    s1_send, s1_recv, p1_t, q1_t, token1 = scatter_start("scatter1_start", p1, dh)
    dh, g3 = ffn_bwd(dh, sv0['sv3'], W0['wg2'], W0['wu2'], W0['wd2'], W0['ln3_g'] + token1[0, 0], lp)
    G0 = dict(wg2=g3['wg'], wu2=g3['wu'], wd2=g3['wd'])
    p0l = rs_partials(shards(G0, last), [BF16] * len(last), cidx, "c")
    sl_send, sl_recv, p0l_t, q0l_t, token0l = scatter_start("scatter0b_start", p0l, dh)
    dh, Gm = layer_bwd(dh, sv0, dict(W0, ln2_g=W0['ln2_g'] + token0l[0, 0]), tabs, lp, ffn1=False, ffn2=False)
    G0.update(Gm, ln3_g=g3['ln_g'], ln3_b=g3['ln_b'])
    p0m = rs_partials(shards(G0, mix), [BF16] * nm, cidx, "d")
    sm_send, sm_recv, p0m_t, q0m_t, token0m = scatter_start("scatter0_start", p0m, dh)
    dh, g1 = ffn_bwd(dh, sv0['sv1'], W0['wg1'], W0['wu1'], W0['wd1'], W0['ln1_g'] + token0m[0, 0], lp)
    G0.update(wg1=g1['wg'], wu1=g1['wu'], wd1=g1['wd'], ln1_g=g1['ln_g'], ln1_b=g1['ln_b'])
    _, q1 = split_wait("scatter1_wait", s1_send, s1_recv, p1_t, q1_t, dh, _scatter_copies)
    _, q0_last = split_wait("scatter0b_wait", sl_send, sl_recv, p0l_t, q0l_t, dh, _scatter_copies)
    _, q0_mix = split_wait("scatter0_wait", sm_send, sm_recv, p0m_t, q0m_t, dh, _scatter_copies)
    q0_rest = list(q0_mix) + list(q0_last)
    full = [reference_grads(G0, ffn=False), reference_grads(G1, ffn=False)]

    s_parts = [jnp.stack([full[li][n] for li in range(DEPTH)]).reshape(-1) for n in _REPL + ['conv_w']]
    s_parts.append(dh[PAD:X0].reshape(-1))
    s_sizes = [int(p.shape[0]) for p in s_parts]
    s_rows = -(-sum(s_sizes) // (16 * LANES)) * 16
    g_small = _pad_to(jnp.concatenate(s_parts), s_rows * LANES).reshape(1, s_rows, LANES)
    g_small = jnp.broadcast_to(g_small, (N_SHARD, s_rows, LANES))
    p_last = rs_partials(shards(G0, first) + [g_small], [BF16] * len(first) + [F32], cidx, "a")
    z_send, z_recv, pz_t, qz_t, token_z = scatter_start("scatter_last_start", p_last, dh)

    def step_weights(names, grad):
        out = {}
        for n in names:
            if n in kept_t:
                res = adamw(own[n], grad[n], jnp.swapaxes(mom[n], 1, 2), jnp.swapaxes(var[n], 1, 2), "adamw_" + n)
                out[n] = [jnp.swapaxes(t, 1, 2) for t in [grad[n]] + list(res)]
            else:
                out[n] = [grad[n]] + list(adamw(w[n], grad[n], mom[n], var[n], "adamw_" + n))
        return out

    q1 = dict(zip(big_names, q1))
    q0 = dict(zip(rest, q0_rest))
    q0[rest[0]] = q0[rest[0]] + token_z[0, 0].astype(BF16)
    red = rs_finish([[q0[n], q1[n]] for n in rest], "a")
    done = step_weights(rest, {n: r[:, :own[n].shape[1]] for n, r in zip(rest, red)})
    all_done = jnp.stack([done[n][3][(0,) * done[n][3].ndim] for n in rest])
    _, q_last = split_wait("scatter_last_wait", z_send, z_recv, pz_t, qz_t, all_done, _scatter_copies)
    red = rs_finish([[q, q1[n]] for n, q in zip(first, q_last)] + [[q_last[-1]]], "b")
    f_small = red[-1].reshape(-1)

    grad = {n: r[:, :own[n].shape[1]] for n, r in zip(first, red)}
    off = 0
    for n, sz in zip(_REPL + ['conv_w', 'meta'], s_sizes):
        grad[n] = f_small[off:off + sz]
        off += sz
    for n in _REPL:
        grad[n] = grad[n].reshape(w[n].shape)
    cw = grad['conv_w'].reshape(DEPTH, 3, MIX)
    grad['conv_w'] = lax.dynamic_slice_in_dim(cw, chip * (MIX // N_SHARD), MIX // N_SHARD, axis=2)
    gm = grad['meta'].reshape(N_META, D_MODEL)
    grad['meta'] = lax.dynamic_slice_in_dim(gm, chip * (D_MODEL // N_SHARD), D_MODEL // N_SHARD, axis=1)

    done.update(step_weights([n for n in _WEIGHTS if n not in done], grad))
    return (loss, dh[X0:][None], *[done[n][k] for k in range(4) for n in _WEIGHTS])


def kernel(x, meta, ffn1_w_gate, ffn1_w_up, ffn1_w_down, ln1_g, ln1_b, w_in, mla_q_norm_g, mla_w_uq, mla_kv_norm_g, mla_w_ukv, mla_w_o, conv_w, conv_b, conv_w_out, s5_a_re, s5_a_im, s5_log_dt, s5_b_re, s5_b_im, s5_c_re, s5_c_im, s5_d, s5_w_glu, s5_b_glu, s5_w_out, w_o, ln2_g, ln2_b, ffn2_w_gate, ffn2_w_up, ffn2_w_down, ln3_g, ln3_b, loss_target, m_meta, m_ffn1_w_gate, m_ffn1_w_up, m_ffn1_w_down, m_ln1_g, m_ln1_b, m_w_in, m_mla_q_norm_g, m_mla_w_uq, m_mla_kv_norm_g, m_mla_w_ukv, m_mla_w_o, m_conv_w, m_conv_b, m_conv_w_out, m_s5_a_re, m_s5_a_im, m_s5_log_dt, m_s5_b_re, m_s5_b_im, m_s5_c_re, m_s5_c_im, m_s5_d, m_s5_w_glu, m_s5_b_glu, m_s5_w_out, m_w_o, m_ln2_g, m_ln2_b, m_ffn2_w_gate, m_ffn2_w_up, m_ffn2_w_down, m_ln3_g, m_ln3_b, v_meta, v_ffn1_w_gate, v_ffn1_w_up, v_ffn1_w_down, v_ln1_g, v_ln1_b, v_w_in, v_mla_q_norm_g, v_mla_w_uq, v_mla_kv_norm_g, v_mla_w_ukv, v_mla_w_o, v_conv_w, v_conv_b, v_conv_w_out, v_s5_a_re, v_s5_a_im, v_s5_log_dt, v_s5_b_re, v_s5_b_im, v_s5_c_re, v_s5_c_im, v_s5_d, v_s5_w_glu, v_s5_b_glu, v_s5_w_out, v_w_o, v_ln2_g, v_ln2_b, v_ffn2_w_gate, v_ffn2_w_up, v_ffn2_w_down, v_ln3_g, v_ln3_b):
    return _step(dict(locals()))
```

```python
import functools
import math

import numpy as np
import jax
import jax.numpy as jnp
from jax import lax
from jax.experimental import pallas as pl
from jax.experimental.pallas import tpu as pltpu

F32 = jnp.float32
BF16 = jnp.bfloat16

D_MODEL = 1024
DEPTH = 2
N_META = 16
PAD = 112
X0 = PAD + N_META
N_HEADS = 8
D_NOPE = 64
D_ROPE = 32
D_V = 64
Q_RANK = 384
KV_RANK = 256
MIX = 512
S5_GROUPS = 32
S5_GROUP = 16
S5_STATE = 64
S5_LANES = S5_GROUPS * S5_STATE
D_FF = 2816
N_SHARD = 4
FF_SHARD = D_FF // N_SHARD
D_IN = 5792
P_IN = 6144
ALPHA = (2.0 * DEPTH) ** 0.25
LN_EPS = 1e-5
RMS_EPS = 1e-6
ATT_SCALE = (D_NOPE + D_ROPE) ** -0.5
ROPE_BASE = 10000.0
ADAM_LR, ADAM_B1, ADAM_B2, ADAM_EPS, ADAM_WD, ADAM_STEP = 0.001, 0.9, 0.999, 1e-08, 0.01, 10
SCAN_CHUNK = 64
VMEM_LIMIT = 52 * 2 ** 20
WGRAD = BF16
MESH = pl.DeviceIdType.MESH


def _cparams(**kw):
    return pltpu.CompilerParams(vmem_limit_bytes=VMEM_LIMIT, **kw)


def _tile(n):
    if n <= 1088:
        return n
    for t in (1088, 1024, 544, 512, 272, 256, 128):
        if n % t == 0:
            return t
    return n


def _row_tile(lp):
    for t in (544, 272, 128):
        if lp % t == 0:
            return t
    return lp


def _ffn_tile(lp):
    return 1088 if lp % 1088 == 0 else _row_tile(lp)


def _sigmoid(x):
    return 1.0 / (1.0 + jnp.exp(-x))


_GELU_C = math.sqrt(2.0 / math.pi)


def _gelu(x):
    return 0.5 * x * (1.0 + jnp.tanh(_GELU_C * (x + 0.044715 * x * x * x)))


def _gelu_grad(x):
    t = jnp.tanh(_GELU_C * (x + 0.044715 * x * x * x))
    return 0.5 * (1.0 + t) + 0.5 * x * (1.0 - t * t) * _GELU_C * (1.0 + 3.0 * 0.044715 * x * x)


def _dot(a, b, ca, cb, precision=None):
    return lax.dot_general(a, b, (((ca,), (cb,)), ((), ())), preferred_element_type=F32, precision=precision)


def matmul(a, b, *, name, ta=False, tb=False, ab='n', bb='n', res=None, res_scale=1.0, scale=1.0, out_dtype=F32):
    if ta:
        _, K, M = a.shape
    else:
        _, M, K = a.shape
    if tb:
        _, N, K2 = b.shape
    else:
        _, K2, N = b.shape
    assert K == K2, (a.shape, b.shape)
    n_out = max(a.shape[0] if ab == 'o' else 1, b.shape[0] if bb == 'o' else 1)
    n_red = max(a.shape[0] if ab == 'r' else 1, b.shape[0] if bb == 'r' else 1)
    tm, tn = _tile(M), _tile(N)
    tk = K if K <= 2304 else _tile(K)
    nkt = K // tk
    n_steps = n_red * nkt

    def bsel(mode, o, r):
        if mode == 'o':
            return o
        if mode == 'r':
            return r // nkt if nkt > 1 else r
        return 0

    def ksel(r):
        if nkt == 1:
            return 0
        return r % nkt if n_red > 1 else r

    a_map = (lambda o, i, j, r: (bsel(ab, o, r), ksel(r), i)) if ta else (lambda o, i, j, r: (bsel(ab, o, r), i, ksel(r)))
    b_map = (lambda o, i, j, r: (bsel(bb, o, r), j, ksel(r))) if tb else (lambda o, i, j, r: (bsel(bb, o, r), ksel(r), j))
    o_map = lambda o, i, j, r: (o, i, j)
    in_specs = [pl.BlockSpec((None, tk, tm) if ta else (None, tm, tk), a_map),
                pl.BlockSpec((None, tn, tk) if tb else (None, tk, tn), b_map)]
    operands = [a, b]
    if res is not None:
        in_specs.append(pl.BlockSpec((None, tm, tn), o_map))
        operands.append(res)
    has_res = res is not None

    def body(*refs):
        a_ref, b_ref = refs[0], refs[1]
        res_ref = refs[2] if has_res else None
        o_ref = refs[3] if has_res else refs[2]
        part = _dot(a_ref[...].astype(BF16), b_ref[...].astype(BF16), 0 if ta else 1, 1 if tb else 0)

        def finish(acc):
            v = acc if scale == 1.0 else acc * scale
            if has_res:
                v = v + res_scale * res_ref[...].astype(F32)
            o_ref[...] = v.astype(o_ref.dtype)

        if n_steps == 1:
            finish(part)
        else:
            acc_ref = refs[-1]
            r = pl.program_id(3)

            @pl.when(r == 0)
            def _():
                acc_ref[...] = part

            @pl.when(r > 0)
            def _():
                acc_ref[...] += part

            @pl.when(r == n_steps - 1)
            def _():
                finish(acc_ref[...])

    return pl.pallas_call(
        body, name=name,
        grid=(n_out, M // tm, N // tn, n_steps),
        in_specs=in_specs,
        out_specs=pl.BlockSpec((None, tm, tn), o_map),
        out_shape=jax.ShapeDtypeStruct((n_out, M, N), out_dtype),
        scratch_shapes=[pltpu.VMEM((tm, tn), F32)] if n_steps > 1 else [],
        compiler_params=_cparams(),
    )(*operands)


def rowwise(fn, rows, pars, outs, accs=(), *, name, lp):
    tm = _row_tile(lp)
    n_rows, n_pars, n_outs, n_accs = len(rows), len(pars), len(outs), len(accs)
    outs = [tuple(o) + (o[0], 0) if len(o) == 2 else tuple(o) for o in outs]
    in_specs = [pl.BlockSpec((tm, w), functools.partial(lambda i, cb: (i, cb), cb=cb)) for _, w, cb in rows]
    in_specs += [pl.BlockSpec(p.shape, functools.partial(lambda i, nd: (0,) * nd, nd=p.ndim)) for p in pars]
    out_specs = [pl.BlockSpec((tm, w), functools.partial(lambda i, cb: (i, cb), cb=cb)) for w, _, _, cb in outs]
    out_specs += [pl.BlockSpec(s, functools.partial(lambda i, nd: (0,) * nd, nd=len(s))) for s, _ in accs]
    out_shape = [jax.ShapeDtypeStruct((lp, total), dt) for _, dt, total, _ in outs]
    out_shape += [jax.ShapeDtypeStruct(s, dt) for s, dt in accs]

    def body(*refs):
        i = pl.program_id(0)
        rv = [r[...] for r in refs[:n_rows]]
        pv = [r[...] for r in refs[n_rows:n_rows + n_pars]]
        o_refs = refs[n_rows + n_pars:n_rows + n_pars + n_outs]
        a_refs = refs[n_rows + n_pars + n_outs:]
        ov, av = fn(i * tm, rv, pv)
        for r, v in zip(o_refs, ov):
            r[...] = v.astype(r.dtype)
        if n_accs:
            @pl.when(i == 0)
            def _():
                for r, v in zip(a_refs, av):
                    r[...] = v.astype(r.dtype)

            @pl.when(i > 0)
            def _():
                for r, v in zip(a_refs, av):
                    r[...] += v.astype(r.dtype)

    res = pl.pallas_call(
        body, name=name, grid=(lp // tm,), in_specs=in_specs, out_specs=out_specs, out_shape=out_shape,
        compiler_params=_cparams(),
    )(*[r[0] for r in rows], *pars)
    return res


def _row_mask(row0, shape):
    return (row0 + lax.broadcasted_iota(jnp.int32, shape, 0)) >= PAD


def ffn_up(hb, wg, wu, lp):
    tm = _ffn_tile(lp)

    def body(h_ref, wg_ref, wu_ref, ab_ref, hid_ref):
        h = h_ref[...]
        a = _dot(h, wg_ref[...], 1, 1)
        b = _dot(h, wu_ref[...], 1, 1)
        ab_ref[0] = a.astype(BF16)
        ab_ref[1] = b.astype(BF16)
        hid_ref[...] = (a * _sigmoid(a) * b).astype(BF16)

    wspec = pl.BlockSpec((None, FF_SHARD, D_MODEL), lambda j, i: (j, 0, 0))
    return pl.pallas_call(
        body, name="ffn_up", grid=(N_SHARD, lp // tm),
        in_specs=[pl.BlockSpec((tm, D_MODEL), lambda j, i: (i, 0)), wspec, wspec],
        out_specs=[pl.BlockSpec((None, 2, tm, FF_SHARD), lambda j, i: (j, 0, i, 0)),
                   pl.BlockSpec((None, tm, FF_SHARD), lambda j, i: (j, i, 0))],
        out_shape=[jax.ShapeDtypeStruct((N_SHARD, 2, lp, FF_SHARD), BF16),
                   jax.ShapeDtypeStruct((N_SHARD, lp, FF_SHARD), BF16)],
        compiler_params=_cparams(),
    )(hb, wg, wu)


def _layer_norm(z, g, b):
    mu = jnp.mean(z, axis=-1, keepdims=True)
    zc = z - mu
    var = jnp.mean(zc * zc, axis=-1, keepdims=True)
    return zc * lax.rsqrt(var + LN_EPS) * g + b


def mm_res_ln(a, w, res, g, b, *, scale, name, lp):
    n_red, _, K = a.shape
    tm = _row_tile(lp)

    def body(a_ref, w_ref, res_ref, g_ref, b_ref, z_ref, h_ref, hb_ref):
        acc = _dot(a_ref[0].astype(BF16), w_ref[0], 1, 0)
        for r in range(1, n_red):
            acc = acc + _dot(a_ref[r].astype(BF16), w_ref[r], 1, 0)
        z = ALPHA * res_ref[...] + scale * acc
        z_ref[...] = z
        hn = _layer_norm(z, g_ref[...], b_ref[...])
        h_ref[...] = hn
        hb_ref[...] = hn.astype(BF16)

    row = pl.BlockSpec((tm, D_MODEL), lambda i: (i, 0))
    par = pl.BlockSpec((1, D_MODEL), lambda i: (0, 0))
    return pl.pallas_call(
        body, name=name, grid=(lp // tm,),
        in_specs=[pl.BlockSpec((n_red, tm, K), lambda i: (0, i, 0)),
                  pl.BlockSpec((n_red, K, D_MODEL), lambda i: (0, 0, 0)), row, par, par],
        out_specs=[row, row, row],
        out_shape=[jax.ShapeDtypeStruct((lp, D_MODEL), F32), jax.ShapeDtypeStruct((lp, D_MODEL), F32),
                   jax.ShapeDtypeStruct((lp, D_MODEL), BF16)],
        compiler_params=_cparams(),
    )(a, w, res, g, b)


def ln_bwd(dh, z, g, *, fscale, name, lp):
    def fn(row0, rv, pv):
        dh_, z_ = rv
        g_, = pv
        mu = jnp.mean(z_, axis=-1, keepdims=True)
        zc = z_ - mu
        rstd = lax.rsqrt(jnp.mean(zc * zc, axis=-1, keepdims=True) + LN_EPS)
        xh = zc * rstd
        dxh = dh_ * g_
        m1 = jnp.mean(dxh, axis=-1, keepdims=True)
        m2 = jnp.mean(dxh * xh, axis=-1, keepdims=True)
        dz = rstd * (dxh - m1 - xh * m2)
        return ((dz, fscale * dz),
                (jnp.sum(dh_ * xh, axis=0, keepdims=True), jnp.sum(dh_, axis=0, keepdims=True)))

    return rowwise(fn, [(dh, D_MODEL, 0), (z, D_MODEL, 0)], [g], [(D_MODEL, F32), (D_MODEL, BF16)],
                   [((1, D_MODEL), F32), ((1, D_MODEL), F32)], name=name, lp=lp)


def ffn_down_bwd(dfb, wd, ab, lp):
    tm = _ffn_tile(lp)

    def body(df_ref, w_ref, ab_ref, da_ref, db_ref):
        dhid = _dot(df_ref[...], w_ref[...], 1, 1)
        a = ab_ref[0].astype(F32)
        b = ab_ref[1].astype(F32)
        sg = _sigmoid(a)
        da_ref[...] = (dhid * b * (sg * (1.0 + a * (1.0 - sg)))).astype(BF16)
        db_ref[...] = (dhid * (a * sg)).astype(BF16)

    ospec = pl.BlockSpec((None, tm, FF_SHARD), lambda j, i: (j, i, 0))
    return pl.pallas_call(
        body, name="ffn_down_bwd", grid=(N_SHARD, lp // tm),
        in_specs=[pl.BlockSpec((tm, D_MODEL), lambda j, i: (i, 0)),
                  pl.BlockSpec((None, FF_SHARD, D_MODEL), lambda j, i: (j, 0, 0)),
                  pl.BlockSpec((None, 2, tm, FF_SHARD), lambda j, i: (j, 0, i, 0))],
        out_specs=[ospec, ospec],
        out_shape=[jax.ShapeDtypeStruct((N_SHARD, lp, FF_SHARD), BF16)] * 2,
        compiler_params=_cparams(),
    )(dfb, wd, ab)


def ffn_dx(da, db, wg, wu, dz, lp):
    tm = _ffn_tile(lp)

    def body(da_ref, db_ref, wg_ref, wu_ref, dz_ref, o_ref, acc_ref):
        j = pl.program_id(1)
        part = _dot(da_ref[...], wg_ref[...], 1, 0) + _dot(db_ref[...], wu_ref[...], 1, 0)

        @pl.when(j == 0)
        def _():
            acc_ref[...] = part

        @pl.when(j > 0)
        def _():
            acc_ref[...] += part

        @pl.when(j == N_SHARD - 1)
        def _():
            o_ref[...] = acc_ref[...] + ALPHA * dz_ref[...]

    aspec = pl.BlockSpec((None, tm, FF_SHARD), lambda i, j: (j, i, 0))
    wspec = pl.BlockSpec((None, FF_SHARD, D_MODEL), lambda i, j: (j, 0, 0))
    row = pl.BlockSpec((tm, D_MODEL), lambda i, j: (i, 0))
    return pl.pallas_call(
        body, name="ffn_dx", grid=(lp // tm, N_SHARD), in_specs=[aspec, aspec, wspec, wspec, row], out_specs=row,
        out_shape=jax.ShapeDtypeStruct((lp, D_MODEL), F32), scratch_shapes=[pltpu.VMEM((tm, D_MODEL), F32)],
        compiler_params=_cparams(),
    )(da, db, wg, wu, dz)


def ffn_fwd(h, hb, wg, wu, wd, g, b, lp):
    ab, hid = ffn_up(hb, wg, wu, lp)
    z, hn, hnb = mm_res_ln(hid, wd, h, g, b, scale=0.5, name="ffn_down_ln", lp=lp)
    return hn, hnb, dict(hb=hb, ab=ab, hid=hid, z=z)


def ffn_bwd(dh, sv, wg, wu, wd, g, lp):
    dz, dfb, dg, db = ln_bwd(dh, sv['z'], g, fscale=0.5, name="ffn_ln_bwd", lp=lp)
    da, dbb = ffn_down_bwd(dfb, wd, sv['ab'], lp)
    d_wd = matmul(sv['hid'], dfb[None], ta=True, ab='o', out_dtype=WGRAD, name="ffn_dwd")
    d_wg = matmul(da, sv['hb'][None], ta=True, ab='o', out_dtype=WGRAD, name="ffn_dwg")
    d_wu = matmul(dbb, sv['hb'][None], ta=True, ab='o', out_dtype=WGRAD, name="ffn_dwu")
    dh_in = ffn_dx(da, dbb, wg, wu, dz, lp)
    return dh_in, dict(wg=d_wg, wu=d_wu, wd=d_wd, ln_g=dg, ln_b=db)


def _rope_tables(lp):
    pos = np.arange(lp, dtype=np.float32) - PAD
    inv = ROPE_BASE ** (-np.arange(0, D_ROPE, 2, dtype=np.float32) / D_ROPE)
    ang = pos[:, None] * inv[None, :]
    cos = np.concatenate([np.cos(ang), np.cos(ang)], axis=1).astype(np.float32)
    sin = np.concatenate([np.sin(ang), np.sin(ang)], axis=1).astype(np.float32)
    rot = np.zeros((D_ROPE, D_ROPE), np.float32)
    half = D_ROPE // 2
    for j in range(half):
        rot[j + half, j] = -1.0
        rot[j, j + half] = 1.0
    d_qk = D_NOPE + D_ROPE
    cos_qk = np.concatenate([np.ones((lp, D_NOPE), np.float32), cos], axis=1)
    sin_qk = np.concatenate([np.zeros((lp, D_NOPE), np.float32), sin], axis=1)
    rot_qk = np.zeros((d_qk, d_qk), np.float32)
    rot_qk[D_NOPE:, D_NOPE:] = rot
    place = np.zeros((D_ROPE, d_qk), np.float32)
    place[:, D_NOPE:] = np.eye(D_ROPE, dtype=np.float32)
    return tuple(jnp.asarray(t) for t in (cos, sin, rot, cos_qk, sin_qk, rot_qk, place))


def _rot(x, rot):
    return _dot(x, rot, 1, 0, precision=lax.Precision.HIGHEST)


def _rms(x, g):
    r = lax.rsqrt(jnp.mean(x * x, axis=-1, keepdims=True) + RMS_EPS)
    return x * r * g


def mla_prep(proj, cos, sin, rot, qg, kvg, lp):
    def fn(row0, rv, pv):
        cq, krb, ckv, c, s = rv
        qg_, kvg_, rot_ = pv
        kr = krb[:, :D_ROPE]
        return ((_rms(cq, qg_), _rms(ckv, kvg_), kr * c + _rot(kr, rot_) * s), ())

    return rowwise(fn, [(proj, Q_RANK, 0), (proj, 128, 3), (proj, KV_RANK, 2), (cos, D_ROPE, 0), (sin, D_ROPE, 0)],
                   [qg, kvg, rot], [(Q_RANK, BF16), (KV_RANK, BF16), (D_ROPE, BF16)], name="mla_prep", lp=lp)


def mla_heads(cqn, ckvn, kr, cos_qk, sin_qk, rot_qk, place, wq, wk, wv, lp):
    tm = _row_tile(lp)

    def body(cq_ref, ckv_ref, kr_ref, c_ref, s_ref, rot_ref, place_ref, wq_ref, wk_ref, wv_ref, q_ref, k_ref, v_ref):
        cq = cq_ref[...]
        ckv = ckv_ref[...]
        kr_placed = _dot(kr_ref[...], place_ref[...].astype(BF16), 1, 0)
        for h in range(N_HEADS):
            q = _dot(cq, wq_ref[h], 1, 0)
            q_ref[h] = (q * c_ref[...] + _rot(q, rot_ref[...]) * s_ref[...]).astype(BF16)
            k_ref[h] = (_dot(ckv, wk_ref[h], 1, 0) + kr_placed).astype(BF16)
            v_ref[h] = _dot(ckv, wv_ref[h], 1, 0).astype(BF16)

    def row(w):
        return pl.BlockSpec((tm, w), lambda i: (i, 0))

    def whole(a):
        return pl.BlockSpec(a.shape, functools.partial(lambda i, nd: (0,) * nd, nd=a.ndim))

    def ospec(n):
        return pl.BlockSpec((N_HEADS, tm, n), lambda i: (0, i, 0))

    return pl.pallas_call(
        body, name="mla_heads", grid=(lp // tm,),
        in_specs=[row(Q_RANK), row(KV_RANK), row(D_ROPE), row(D_QK), row(D_QK), whole(rot_qk), whole(place),
                  whole(wq), whole(wk), whole(wv)],
        out_specs=[ospec(D_QK), ospec(D_QK), ospec(D_V)],
        out_shape=[jax.ShapeDtypeStruct((N_HEADS, lp, D_QK), BF16), jax.ShapeDtypeStruct((N_HEADS, lp, D_QK), BF16),
                   jax.ShapeDtypeStruct((N_HEADS, lp, D_V), BF16)],
        compiler_params=_cparams(),
    )(cqn, ckvn, kr, cos_qk, sin_qk, rot_qk, place, wq, wk, wv)


D_QK = D_NOPE + D_ROPE


def _att_probs(q, k, row0, tq, lp):
    s = _dot(q, k, 1, 1) * ATT_SCALE
    qi = row0 + lax.broadcasted_iota(jnp.int32, (tq, lp), 0)
    ki = lax.broadcasted_iota(jnp.int32, (tq, lp), 1)
    s = jnp.where((ki <= qi) & (ki >= PAD), s, -1e30)
    p = jnp.exp(s - jnp.max(s, axis=-1, keepdims=True))
    return p / jnp.sum(p, axis=-1, keepdims=True)


def _att_spec(lp, n):
    return pl.BlockSpec((None, lp, n), lambda h: (h, 0, 0))


def _att_tiles(lp):
    tiles, r = [(0, X0)], X0
    while r < lp:
        tiles.append((r, 256))
        r += 256
    assert r == lp
    return tiles


def attn_fwd(q, k, v, lp):
    def body(q_ref, k_ref, v_ref, o_ref):
        for r0, rows in _att_tiles(lp):
            ke, rq = r0 + rows, slice(r0, r0 + rows)
            p = _att_probs(q_ref[rq, :], k_ref[0:ke, :], r0, rows, ke)
            o_ref[rq, :] = _dot(p.astype(BF16), v_ref[0:ke, :], 1, 0).astype(BF16)

    return pl.pallas_call(
        body, name="attn_fwd", grid=(N_HEADS,),
        in_specs=[_att_spec(lp, D_QK), _att_spec(lp, D_QK), _att_spec(lp, D_V)],
        out_specs=_att_spec(lp, D_V), out_shape=jax.ShapeDtypeStruct((N_HEADS, lp, D_V), BF16),
        compiler_params=_cparams(),
    )(q, k, v)


def attn_bwd(q, k, v, do, lp):
    def body(q_ref, k_ref, v_ref, do_ref, dq_ref, dk_ref, dv_ref):
        dk_ref[...] = jnp.zeros_like(dk_ref)
        dv_ref[...] = jnp.zeros_like(dv_ref)
        for r0, rows in _att_tiles(lp):
            ke, rq = r0 + rows, slice(r0, r0 + rows)
            q_, do_, k_, v_ = q_ref[rq, :], do_ref[rq, :], k_ref[0:ke, :], v_ref[0:ke, :]
            p = _att_probs(q_, k_, r0, rows, ke)
            dp = _dot(do_, v_, 1, 1)
            delta = jnp.sum(p * dp, axis=-1, keepdims=True)
            ds = (p * (dp - delta) * ATT_SCALE).astype(BF16)
            dq_ref[rq, :] = _dot(ds, k_, 1, 0)
            dk_ref[0:ke, :] += _dot(ds, q_, 0, 0)
            dv_ref[0:ke, :] += _dot(p.astype(BF16), do_, 0, 0)

    qk, vv = _att_spec(lp, D_QK), _att_spec(lp, D_V)
    return pl.pallas_call(
        body, name="attn_bwd", grid=(N_HEADS,), in_specs=[qk, qk, vv, vv], out_specs=[qk, qk, vv],
        out_shape=[jax.ShapeDtypeStruct((N_HEADS, lp, D_QK), F32), jax.ShapeDtypeStruct((N_HEADS, lp, D_QK), F32),
                   jax.ShapeDtypeStruct((N_HEADS, lp, D_V), F32)],
        compiler_params=_cparams(),
    )(q, k, v, do)


def mla_heads_bwd(dq, dk, dv, cos_qk, sin_qk, rot_qk, place, wq, wk, wv, lp):
    tm = _row_tile(lp)

    def body(dq_ref, dk_ref, dv_ref, c_ref, s_ref, rot_ref, place_ref, wq_ref, wk_ref, wv_ref,
             dcq_ref, dckv_ref, dqp_ref, dkr_ref):
        dcq = jnp.zeros(dcq_ref.shape, F32)
        dckv = jnp.zeros(dckv_ref.shape, F32)
        dkr = jnp.zeros(dkr_ref.shape, F32)
        for h in range(N_HEADS):
            g = dq_ref[h]
            dqp = (g * c_ref[...] - _rot(g * s_ref[...], rot_ref[...])).astype(BF16)
            dqp_ref[h] = dqp
            dcq = dcq + _dot(dqp, wq_ref[h], 1, 1)
            dk_ = dk_ref[h]
            dckv = dckv + _dot(dk_.astype(BF16), wk_ref[h], 1, 1) + _dot(dv_ref[h].astype(BF16), wv_ref[h], 1, 1)
            dkr = dkr + _dot(dk_, place_ref[...], 1, 1, precision=lax.Precision.HIGHEST)
        dcq_ref[...] = dcq
        dckv_ref[...] = dckv
        dkr_ref[...] = dkr

    def hspec(n):
        return pl.BlockSpec((N_HEADS, tm, n), lambda i: (0, i, 0))

    def row(w):
        return pl.BlockSpec((tm, w), lambda i: (i, 0))

    def whole(a):
        return pl.BlockSpec(a.shape, functools.partial(lambda i, nd: (0,) * nd, nd=a.ndim))

    return pl.pallas_call(
        body, name="mla_heads_bwd", grid=(lp // tm,),
        in_specs=[hspec(D_QK), hspec(D_QK), hspec(D_V), row(D_QK), row(D_QK), whole(rot_qk), whole(place),
                  whole(wq), whole(wk), whole(wv)],
        out_specs=[row(Q_RANK), row(KV_RANK), hspec(D_QK), row(D_ROPE)],
        out_shape=[jax.ShapeDtypeStruct((lp, Q_RANK), F32), jax.ShapeDtypeStruct((lp, KV_RANK), F32),
                   jax.ShapeDtypeStruct((N_HEADS, lp, D_QK), BF16), jax.ShapeDtypeStruct((lp, D_ROPE), F32)],
        compiler_params=_cparams(),
    )(dq, dk, dv, cos_qk, sin_qk, rot_qk, place, wq, wk, wv)


def heads_out(o, wo, lp):
    tm = _row_tile(lp)

    def body(o_ref, w_ref, y_ref):
        acc = _dot(o_ref[0], w_ref[0], 1, 0)
        for h in range(1, N_HEADS):
            acc = acc + _dot(o_ref[h], w_ref[h], 1, 0)
        y_ref[...] = acc.astype(BF16)

    return pl.pallas_call(
        body, name="mla_out", grid=(lp // tm,),
        in_specs=[pl.BlockSpec((N_HEADS, tm, D_V), lambda i: (0, i, 0)), pl.BlockSpec(wo.shape, lambda i: (0, 0, 0))],
        out_specs=pl.BlockSpec((tm, D_MODEL), lambda i: (i, 0)), out_shape=jax.ShapeDtypeStruct((lp, D_MODEL), BF16),
        compiler_params=_cparams(),
    )(o, wo)


def heads_out_dx(dy, wo, lp):
    tm = _row_tile(lp)

    def body(dy_ref, w_ref, do_ref):
        dy_ = dy_ref[...]
        for h in range(N_HEADS):
            do_ref[h] = _dot(dy_, w_ref[h], 1, 1).astype(BF16)

    return pl.pallas_call(
        body, name="mla_out_dx", grid=(lp // tm,),
        in_specs=[pl.BlockSpec((tm, D_MODEL), lambda i: (i, 0)), pl.BlockSpec(wo.shape, lambda i: (0, 0, 0))],
        out_specs=pl.BlockSpec((N_HEADS, tm, D_V), lambda i: (0, i, 0)),
        out_shape=jax.ShapeDtypeStruct((N_HEADS, lp, D_V), BF16), compiler_params=_cparams(),
    )(dy, wo)


def _rms_bwd(dy, x, g):
    r = lax.rsqrt(jnp.mean(x * x, axis=-1, keepdims=True) + RMS_EPS)
    n = x * r
    dn = dy * g
    dx = r * (dn - n * jnp.mean(dn * n, axis=-1, keepdims=True))
    return dx, jnp.sum(dy * n, axis=0, keepdims=True)


def mla_prep_bwd(dcq, dckv, dkr, proj, cos, sin, rot, qg, kvg, lp):
    def fn(row0, rv, pv):
        dcq_, dckv_, dkr_, cq, ckv, c, s = rv
        qg_, kvg_, rot_ = pv
        dxq, dgq = _rms_bwd(dcq_, cq, qg_)
        dxkv, dgkv = _rms_bwd(dckv_, ckv, kvg_)
        dkr_raw = dkr_ * c - _rot(dkr_ * s, rot_)
        return ((dxq, dxkv, dkr_raw), (dgq, dgkv))

    return rowwise(fn, [(dcq, Q_RANK, 0), (dckv, KV_RANK, 0), (dkr, D_ROPE, 0), (proj, Q_RANK, 0), (proj, KV_RANK, 2),
                        (cos, D_ROPE, 0), (sin, D_ROPE, 0)], [qg, kvg, rot],
                   [(Q_RANK, BF16), (KV_RANK, BF16), (D_ROPE, BF16)], [((1, Q_RANK), F32), ((1, KV_RANK), F32)],
                   name="mla_prep_bwd", lp=lp)


def _shift_down(x, d, rows):
    return jnp.where(rows >= d, pltpu.roll(x, d, 0), 0.0)


def _shift_up(x, d, rows, n):
    return jnp.where(rows < n - d, pltpu.roll(x, n - d, 0), 0.0)


_CONV_W = 128
_XB, _BG, _CG = 1024 // _CONV_W, 1536 // _CONV_W, 2048 // _CONV_W


def _conv_specs(lp):
    def pspec(base):
        return pl.BlockSpec((lp, _CONV_W), functools.partial(lambda c, base: (0, base + c), base=base))

    col = pl.BlockSpec((lp, _CONV_W), lambda c: (0, c))
    wspec = pl.BlockSpec((3, _CONV_W), lambda c: (0, c))
    bspec = pl.BlockSpec((1, _CONV_W), lambda c: (0, c))
    return pspec, col, wspec, bspec


def _conv_core(xbar, cg, w, bias, lp):
    rows = lax.broadcasted_iota(jnp.int32, (lp, _CONV_W), 0)
    u = jnp.where(rows >= PAD, cg * xbar, 0.0)
    u1 = _shift_down(u, 1, rows)
    u2 = _shift_down(u, 2, rows)
    y = bias + w[0:1] * u2 + w[1:2] * u1 + w[2:3] * u
    return rows, u, u1, u2, y


def conv_fwd(proj, w, bias, lp):
    pspec, col, wspec, bspec = _conv_specs(lp)

    def body(x_ref, b_ref, c_ref, w_ref, bias_ref, v_ref):
        _, _, _, _, y = _conv_core(x_ref[...], c_ref[...], w_ref[...], bias_ref[...], lp)
        v_ref[...] = (b_ref[...] * y).astype(BF16)

    return pl.pallas_call(
        body, name="conv_fwd", grid=(MIX // _CONV_W,),
        in_specs=[pspec(_XB), pspec(_BG), pspec(_CG), wspec, bspec], out_specs=col,
        out_shape=jax.ShapeDtypeStruct((lp, MIX), BF16), compiler_params=_cparams(),
    )(proj, proj, proj, w, bias)


def conv_bwd(dv, proj, w, bias, lp):
    pspec, col, wspec, bspec = _conv_specs(lp)

    def body(dv_ref, x_ref, b_ref, c_ref, w_ref, bias_ref, dx_ref, db_ref, dc_ref, dw_ref, dbias_ref):
        xbar, cg, w_ = x_ref[...], c_ref[...], w_ref[...]
        rows, u, u1, u2, y = _conv_core(xbar, cg, w_, bias_ref[...], lp)
        dv_ = dv_ref[...]
        db_ref[...] = (dv_ * y).astype(BF16)
        dy = dv_ * b_ref[...]
        dbias_ref[...] = jnp.sum(dy, axis=0, keepdims=True)
        dw_ref[0:1, :] = jnp.sum(dy * u2, axis=0, keepdims=True)
        dw_ref[1:2, :] = jnp.sum(dy * u1, axis=0, keepdims=True)
        dw_ref[2:3, :] = jnp.sum(dy * u, axis=0, keepdims=True)
        du = w_[2:3] * dy + w_[1:2] * _shift_up(dy, 1, rows, lp) + w_[0:1] * _shift_up(dy, 2, rows, lp)
        du = jnp.where(rows >= PAD, du, 0.0)
        dc_ref[...] = (du * xbar).astype(BF16)
        dx_ref[...] = (du * cg).astype(BF16)

    return pl.pallas_call(
        body, name="conv_bwd", grid=(MIX // _CONV_W,),
        in_specs=[col, pspec(_XB), pspec(_BG), pspec(_CG), wspec, bspec],
        out_specs=[col, col, col, wspec, bspec],
        out_shape=[jax.ShapeDtypeStruct((lp, MIX), BF16)] * 3 + [jax.ShapeDtypeStruct((3, MIX), F32),
                                                                jax.ShapeDtypeStruct((1, MIX), F32)],
        compiler_params=_cparams(),
    )(dv, proj, proj, proj, w, bias)


def _s5_disc(a_re, a_im, log_dt, b_re, b_im):
    dt = jnp.exp(log_dt)
    mag = jnp.exp(dt * a_re)
    ab_re, ab_im = mag * jnp.cos(dt * a_im), mag * jnp.sin(dt * a_im)
    den = a_re * a_re + a_im * a_im
    nr, ni = ab_re - 1.0, ab_im
    coef_re = (nr * a_re + ni * a_im) / den
    coef_im = (ni * a_re - nr * a_im) / den
    return ab_re, ab_im, coef_re * b_re - coef_im * b_im, coef_re * b_im + coef_im * b_re


_S5_ROWS = S5_GROUPS * S5_GROUP


def s5_prep(a_re, a_im, log_dt, b_re, b_im):
    def body(ar, ai, ld, br, bi, o0, o1, o2, o3):
        for o, v in zip((o0, o1, o2, o3), _s5_disc(ar[...], ai[...], ld[...], br[...], bi[...])):
            o[...] = v

    return pl.pallas_call(body, name="s5_prep",
                          out_shape=[jax.ShapeDtypeStruct((_S5_ROWS, S5_STATE), F32)] * 4)(a_re, a_im, log_dt, b_re, b_im)


def s5_prep_bwd(a_re, a_im, log_dt, b_re, b_im, d_ab_re, d_ab_im, d_bb_re, d_bb_im, sel):
    def body(ar, ai, ld, br, bi, g0, g1, g2, g3, sel_ref, da_re, da_im, dld, dbr, dbi):
        _, vjp = jax.vjp(_s5_disc, ar[...], ai[...], ld[...], br[...], bi[...])
        c_ar, c_ai, c_ld, c_br, c_bi = vjp((g0[...], g1[...], g2[...], g3[...]))
        s = sel_ref[...]
        hi = lax.Precision.HIGHEST
        da_re[...] = _dot(s, c_ar, 1, 0, precision=hi)
        da_im[...] = _dot(s, c_ai, 1, 0, precision=hi)
        dld[...] = jnp.sum(_dot(s, c_ld, 1, 0, precision=hi), axis=-1, keepdims=True)
        dbr[...] = c_br
        dbi[...] = c_bi

    g = jax.ShapeDtypeStruct((S5_GROUPS, S5_STATE), F32)
    full = jax.ShapeDtypeStruct((_S5_ROWS, S5_STATE), F32)
    return pl.pallas_call(body, name="s5_prep_bwd",
                          out_shape=[g, g, jax.ShapeDtypeStruct((S5_GROUPS, 1), F32), full, full],
                          )(a_re, a_im, log_dt, b_re, b_im, d_ab_re, d_ab_im, d_bb_re, d_bb_im, sel)


_SCAN_W = 128
_SCAN_STEPS = int(math.log2(SCAN_CHUNK))


def _cmul(ar, ai, br, bi):
    return ar * br - ai * bi, ar * bi + ai * br


def _scan_powers(ar, ai, reverse):
    pw = [(ar, ai)]
    for _ in range(_SCAN_STEPS):
        pw.append(_cmul(*pw[-1], *pw[-1]))
    rows = lax.broadcasted_iota(jnp.int32, (SCAN_CHUNK, ar.shape[-1]), 0)
    tr = jnp.broadcast_to(ar, rows.shape)
    ti = jnp.broadcast_to(ai, rows.shape)
    for k in range(_SCAN_STEPS):
        d = 2 ** k
        if reverse:
            live = rows < SCAN_CHUNK - d
            mr, mi = _cmul(tr, ti, _shift_up(tr, d, rows, SCAN_CHUNK), _shift_up(ti, d, rows, SCAN_CHUNK))
        else:
            live = rows >= d
            mr, mi = _cmul(tr, ti, _shift_down(tr, d, rows), _shift_down(ti, d, rows))
        tr = jnp.where(live, mr, tr)
        ti = jnp.where(live, mi, ti)
    return pw, rows, tr, ti


def s5_scan(bu, ab_re, ab_im, lp):
    n_chunks = lp // SCAN_CHUNK

    def body(bu_ref, ar_ref, ai_ref, s_ref):
        ar, ai = ar_ref[...], ai_ref[...]
        pw, rows, tr, ti = _scan_powers(ar, ai, False)

        def chunk(ci, carry):
            cr, cim = carry
            r0 = pl.multiple_of(ci * SCAN_CHUNK, SCAN_CHUNK)
            xr = bu_ref[0, pl.ds(r0, SCAN_CHUNK), :]
            xi = bu_ref[1, pl.ds(r0, SCAN_CHUNK), :]
            for k in range(_SCAN_STEPS):
                d = 2 ** k
                mr, mi = _cmul(pw[k][0], pw[k][1], _shift_down(xr, d, rows), _shift_down(xi, d, rows))
                xr, xi = xr + mr, xi + mi
            mr, mi = _cmul(tr, ti, cr, cim)
            xr, xi = xr + mr, xi + mi
            s_ref[0, pl.ds(r0, SCAN_CHUNK), :] = xr
            s_ref[1, pl.ds(r0, SCAN_CHUNK), :] = xi
            return xr[SCAN_CHUNK - 1:SCAN_CHUNK, :], xi[SCAN_CHUNK - 1:SCAN_CHUNK, :]

        zero = jnp.zeros((1, _SCAN_W), F32)
        lax.fori_loop(0, n_chunks, chunk, (zero, zero))

    spec = pl.BlockSpec((2, lp, _SCAN_W), lambda c: (0, 0, c))
    aspec = pl.BlockSpec((1, _SCAN_W), lambda c: (0, c))
    return pl.pallas_call(
        body, name="s5_scan", grid=(S5_LANES // _SCAN_W,), in_specs=[spec, aspec, aspec], out_specs=spec,
        out_shape=jax.ShapeDtypeStruct((2, lp, S5_LANES), F32), compiler_params=_cparams(),
    )(bu, ab_re, ab_im)


def s5_scan_bwd(ds, s, ab_re, ab_im, lp):
    n_chunks = lp // SCAN_CHUNK

    def body(ds_ref, s_ref, ar_ref, ai_ref, g_ref, da_ref):
        ar, ai = ar_ref[...], -ai_ref[...]
        pw, rows, tr, ti = _scan_powers(ar, ai, True)

        def chunk(k, carry):
            cr, cim, dar, dai = carry
            ci = n_chunks - 1 - k
            r0 = pl.multiple_of(ci * SCAN_CHUNK, SCAN_CHUNK)
            xr = ds_ref[0, pl.ds(r0, SCAN_CHUNK), :]
            xi = ds_ref[1, pl.ds(r0, SCAN_CHUNK), :]
            for j in range(_SCAN_STEPS):
                d = 2 ** j
                mr, mi = _cmul(pw[j][0], pw[j][1], _shift_up(xr, d, rows, SCAN_CHUNK), _shift_up(xi, d, rows, SCAN_CHUNK))
                xr, xi = xr + mr, xi + mi
            mr, mi = _cmul(tr, ti, cr, cim)
            xr, xi = xr + mr, xi + mi
            g_ref[0, pl.ds(r0, SCAN_CHUNK), :] = xr
            g_ref[1, pl.ds(r0, SCAN_CHUNK), :] = xi
            prev0 = pl.multiple_of(jnp.maximum(r0 - 8, 0), 8)
            live = (ci > 0).astype(F32)
            pr = s_ref[0, pl.ds(prev0, 8), :][7:8, :] * live
            pim = s_ref[1, pl.ds(prev0, 8), :][7:8, :] * live
            sr = s_ref[0, pl.ds(r0, SCAN_CHUNK), :]
            si = s_ref[1, pl.ds(r0, SCAN_CHUNK), :]
            sr = jnp.where(rows >= 1, pltpu.roll(sr, 1, 0), pr)
            si = jnp.where(rows >= 1, pltpu.roll(si, 1, 0), pim)
            dar = dar + jnp.sum(xr * sr + xi * si, axis=0, keepdims=True)
            dai = dai + jnp.sum(xi * sr - xr * si, axis=0, keepdims=True)
            return xr[0:1, :], xi[0:1, :], dar, dai

        zero = jnp.zeros((1, _SCAN_W), F32)
        _, _, dar, dai = lax.fori_loop(0, n_chunks, chunk, (zero, zero, zero, zero))
        da_ref[0] = dar
        da_ref[1] = dai

    spec = pl.BlockSpec((2, lp, _SCAN_W), lambda c: (0, 0, c))
    aspec = pl.BlockSpec((1, _SCAN_W), lambda c: (0, c))
    return pl.pallas_call(
        body, name="s5_scan_bwd", grid=(S5_LANES // _SCAN_W,), in_specs=[spec, spec, aspec, aspec],
        out_specs=[spec, pl.BlockSpec((2, 1, _SCAN_W), lambda c: (0, 0, c))],
        out_shape=[jax.ShapeDtypeStruct((2, lp, S5_LANES), F32), jax.ShapeDtypeStruct((2, 1, S5_LANES), F32)],
        compiler_params=_cparams(),
    )(ds, s, ab_re, ab_im)


S5_BLOCKS = 4
_S5_PER = S5_GROUPS // S5_BLOCKS


def _blockdiag(x):
    _, r, c = x.shape
    eye = jnp.eye(_S5_PER, dtype=x.dtype)
    x = x.reshape(S5_BLOCKS, _S5_PER, r, c)
    return (x[:, :, :, None, :] * eye[None, :, None, :, None]).reshape(S5_BLOCKS, _S5_PER * r, _S5_PER * c)


def _blockdiag_extract(m, r, c):
    return jnp.einsum('qgrgc->qgrc', m.reshape(S5_BLOCKS, _S5_PER, r, _S5_PER, c)).reshape(S5_GROUPS, r, c)


def bd_matmul(a, w, *, w_t, reduce, res=None, name):
    _, M, _ = a.shape
    n_w, _, k1, k2 = w.shape
    ka, kout = (k2, k1) if w_t else (k1, k2)
    tm = _row_tile(M)
    n_out, n_red = (1, n_w) if reduce else (n_w, 1)
    has_res = res is not None

    assert n_red <= 2

    def body(*refs):
        a_ref, w_ref = refs[0], refs[1]
        o_ref = refs[3] if has_res else refs[2]
        for q in range(S5_BLOCKS):
            cols = slice(q * kout, (q + 1) * kout)
            part = _dot(a_ref[:, q * ka:(q + 1) * ka].astype(BF16), w_ref[q], 1, 1 if w_t else 0)
            if n_red == 1:
                o_ref[:, cols] = part
            else:
                acc_ref = refs[-1]

                @pl.when(pl.program_id(2) == 0)
                def _():
                    acc_ref[:, cols] = part

                @pl.when(pl.program_id(2) == 1)
                def _():
                    tot = acc_ref[:, cols] + part
                    o_ref[:, cols] = tot + refs[2][:, cols] if has_res else tot

    if reduce:
        a_map, w_map = (lambda o, i, r: (r, i, 0)), (lambda o, i, r: (r, 0, 0, 0))
    else:
        a_map, w_map = (lambda o, i, r: (0, i, 0)), (lambda o, i, r: (o, 0, 0, 0))
    o_map = lambda o, i, r: (o, i, 0)
    in_specs = [pl.BlockSpec((None, tm, S5_BLOCKS * ka), a_map), pl.BlockSpec((None, S5_BLOCKS, k1, k2), w_map)]
    operands = [a, w]
    if has_res:
        in_specs.append(pl.BlockSpec((None, tm, S5_BLOCKS * kout), o_map))
        operands.append(res)
    return pl.pallas_call(
        body, name=name, grid=(n_out, M // tm, n_red), in_specs=in_specs,
        out_specs=pl.BlockSpec((None, tm, S5_BLOCKS * kout), o_map),
        out_shape=jax.ShapeDtypeStruct((n_out, M, S5_BLOCKS * kout), F32),
        scratch_shapes=[pltpu.VMEM((tm, S5_BLOCKS * kout), F32)] if n_red > 1 else [],
        compiler_params=_cparams(),
    )(*operands)


def bd_outer(a, b, name):
    na, M, wa = a.shape
    nb_, _, wb = b.shape
    ka, kb = wa // S5_BLOCKS, wb // S5_BLOCKS
    n_out = max(na, nb_)

    def body(a_ref, b_ref, o_ref):
        o_ref[...] = _dot(a_ref[...].astype(BF16), b_ref[...].astype(BF16), 0, 0)

    return pl.pallas_call(
        body, name=name, grid=(n_out, S5_BLOCKS),
        in_specs=[pl.BlockSpec((None, M, ka), (lambda o, q: (o, 0, q)) if na > 1 else (lambda o, q: (0, 0, q))),
                  pl.BlockSpec((None, M, kb), (lambda o, q: (o, 0, q)) if nb_ > 1 else (lambda o, q: (0, 0, q)))],
        out_specs=pl.BlockSpec((None, None, ka, kb), lambda o, q: (o, q, 0, 0)),
        out_shape=jax.ShapeDtypeStruct((n_out, S5_BLOCKS, ka, kb), F32), compiler_params=_cparams(),
    )(a, b)


def s5_u(proj, lp):
    def fn(row0, rv, pv):
        u, = rv
        return ((jnp.where(_row_mask(row0, u.shape), u, 0.0),), ())

    return rowwise(fn, [(proj, MIX, 5)], [], [(MIX, BF16)], name="s5_u", lp=lp)[0]


def s5_y(ys, proj, d, lp):
    def fn(row0, rv, pv):
        ys_, u = rv
        y = ys_ + pv[0] * u
        return ((y, _gelu(y)), ())

    return rowwise(fn, [(ys, MIX, 0), (proj, MIX, 5)], [d], [(MIX, F32), (MIX, BF16)], name="s5_y", lp=lp)


def s5_glu(z, y, b, lp):
    def fn(row0, rv, pv):
        z_, y_ = rv
        return ((_gelu(y_) * _sigmoid(z_ + pv[0]),), ())

    return rowwise(fn, [(z, MIX, 0), (y, MIX, 0)], [b], [(MIX, BF16)], name="s5_glu", lp=lp)[0]


def s5_glu_bwd(dgl, z, y, b, lp):
    def fn(row0, rv, pv):
        dgl_, z_, y_ = rv
        sg = _sigmoid(z_ + pv[0])
        dz = dgl_ * _gelu(y_) * sg * (1.0 - sg)
        return ((dgl_ * sg, dz), (jnp.sum(dz, axis=0, keepdims=True),))

    return rowwise(fn, [(dgl, MIX, 0), (z, MIX, 0), (y, MIX, 0)], [b], [(MIX, F32), (MIX, BF16)], [((1, MIX), F32)],
                   name="s5_glu_bwd", lp=lp)


def s5_y_bwd(dyg, y, proj, d, lp):
    def fn(row0, rv, pv):
        dyg_, y_, u = rv
        dy = dyg_ * _gelu_grad(y_)
        return ((dy, dy * pv[0]), (jnp.sum(dy * u, axis=0, keepdims=True),))

    return rowwise(fn, [(dyg, MIX, 0), (y, MIX, 0), (proj, MIX, 5)], [d], [(MIX, BF16), (MIX, F32)], [((1, MIX), F32)],
                   name="s5_y_bwd", lp=lp)


def s5_du(du, lp):
    def fn(row0, rv, pv):
        return ((jnp.where(_row_mask(row0, rv[0].shape), rv[0], 0.0),), ())

    return rowwise(fn, [(du, MIX, 0)], [], [(MIX, BF16)], name="s5_du", lp=lp)[0]


def merge_fwd(proj, ya, yb, yc, lp):
    def fn(row0, rv, pv):
        g0, g1, g2, a, b, c = rv
        return ((_sigmoid(g0) * a + _sigmoid(g1) * b + _sigmoid(g2) * c,), ())

    return rowwise(fn, [(proj, D_MODEL, 3), (proj, D_MODEL, 4), (proj, D_MODEL, 5), (ya, D_MODEL, 0), (yb, D_MODEL, 0),
                        (yc, D_MODEL, 0)], [], [(D_MODEL, BF16)], name="merge_fwd", lp=lp)[0]


def merge_bwd(dmix, proj, ya, yb, yc, lp):
    def fn(row0, rv, pv):
        dm, g0, g1, g2, a, b, c = rv
        outs_y, outs_g = [], []
        for g, yv in ((g0, a), (g1, b), (g2, c)):
            sg = _sigmoid(g)
            outs_y.append(dm * sg)
            outs_g.append(dm * yv * sg * (1.0 - sg))
        return (tuple(outs_y) + (jnp.concatenate(outs_g, axis=1),), ())

    return rowwise(fn, [(dmix, D_MODEL, 0), (proj, D_MODEL, 3), (proj, D_MODEL, 4), (proj, D_MODEL, 5),
                        (ya, D_MODEL, 0), (yb, D_MODEL, 0), (yc, D_MODEL, 0)], [],
                   [(D_MODEL, BF16)] * 3 + [(P_IN // 2, BF16, P_IN, 1)], name="merge_bwd", lp=lp)


def dproj_fill(dproj, dcq, dkr, dckv, dxbar, dbg, dcg, du, place, lp):
    tm = _row_tile(lp)

    def body(dproj_ref, dcq_ref, dkr_ref, dckv_ref, dx_ref, db_ref, dc_ref, du_ref, place_ref, o_ref):
        o_ref[:, 0:384] = dcq_ref[...]
        o_ref[:, 384:512] = _dot(dkr_ref[...], place_ref[...], 1, 0).astype(BF16)
        o_ref[:, 512:768] = dckv_ref[...]
        o_ref[:, 768:1024] = jnp.zeros((tm, 256), BF16)
        o_ref[:, 1024:1536] = dx_ref[...]
        o_ref[:, 1536:2048] = db_ref[...]
        o_ref[:, 2048:2560] = dc_ref[...]
        o_ref[:, 2560:3072] = du_ref[...]

    def row(w):
        return pl.BlockSpec((tm, w), lambda i: (i, 0))

    return pl.pallas_call(
        body, name="dproj_fill", grid=(lp // tm,),
        in_specs=[pl.BlockSpec(memory_space=pl.ANY), row(Q_RANK), row(D_ROPE), row(KV_RANK), row(MIX), row(MIX), row(MIX),
                  row(MIX), pl.BlockSpec(place.shape, lambda i: (0, 0))],
        out_specs=row(P_IN // 2), out_shape=jax.ShapeDtypeStruct((lp, P_IN), BF16),
        input_output_aliases={0: 0}, compiler_params=_cparams(),
    )(dproj, dcq, dkr, dckv, dxbar, dbg, dcg, du, place)


def loss_head(h, tgt, lp):
    def fn(row0, rv, pv):
        h_, t_ = rv
        live = (row0 + lax.broadcasted_iota(jnp.int32, h_.shape, 0)) >= X0
        diff = jnp.where(live, h_ - t_, 0.0)
        ssq = jnp.sum(jnp.sum(diff * diff, axis=1, keepdims=True), axis=0, keepdims=True)
        return ((diff * (1.0 / D_MODEL),), (ssq * (0.5 / D_MODEL),))

    return rowwise(fn, [(h, D_MODEL, 0), (tgt, D_MODEL, 0)], [], [(D_MODEL, F32)], [((1, 1), F32)], name="loss_head", lp=lp)


def _s5_consts(W):
    ab_re_rep, ab_im_rep, bb_re, bb_im = s5_prep(W['s5_a_re'], W['s5_a_im'], W['s5_log_dt'], W['s5_b_re'], W['s5_b_im'])
    pick = lambda t: t.reshape(S5_GROUPS, S5_GROUP, S5_STATE)[:, 0].reshape(1, S5_LANES)
    bb = jnp.stack([_blockdiag(bb_re.reshape(S5_GROUPS, S5_GROUP, S5_STATE)),
                    _blockdiag(bb_im.reshape(S5_GROUPS, S5_GROUP, S5_STATE))]).astype(BF16)
    return pick(ab_re_rep), pick(ab_im_rep), bb


def layer_fwd(h, hb, W, tabs, lp, ffn1=None, ffn2=True):
    cos, sin, rot = tabs[:3]
    h1, h1b, sv1 = ffn1 if ffn1 is not None else ffn_fwd(h, hb, W['wg1'], W['wu1'], W['wd1'], W['ln1_g'], W['ln1_b'], lp)
    proj = matmul(h1b[None], W['w_in'][None], tb=True, name="proj")[0]
    cqn, ckvn, kr = mla_prep(proj, cos, sin, rot, W['q_norm_g'], W['kv_norm_g'], lp)
    q96, k96, v = mla_heads(cqn, ckvn, kr, *tabs[3:], W['wq'], W['wk'], W['wv'], lp)
    o = attn_fwd(q96, k96, v, lp)
    ya = heads_out(o, W['mla_wo'], lp)
    vconv = conv_fwd(proj, W['conv_w'], W['conv_b'], lp)
    yb = matmul(vconv[None], W['conv_wout'][None], out_dtype=BF16, name="conv_out")[0]
    ub = s5_u(proj, lp)
    ab_re, ab_im, bb = _s5_consts(W)
    bu = bd_matmul(ub[None], bb, w_t=False, reduce=False, name="s5_bu")
    s = s5_scan(bu, ab_re, ab_im, lp)
    ys = bd_matmul(s, W['s5_ct'], w_t=False, reduce=True, name="s5_cs")[0]
    y, ygb = s5_y(ys, proj, W['s5_d'], lp)
    zg = matmul(ygb[None], W['s5_wglu'][None], name="s5_glu_mm")[0]
    glb = s5_glu(zg, y, W['s5_b_glu'], lp)
    yc = matmul(glb[None], W['s5_wout'][None], out_dtype=BF16, name="s5_out")[0]
    mixed = merge_fwd(proj, ya, yb, yc, lp)
    z2, h2, h2b = mm_res_ln(mixed[None], W['w_o'][None], h1, W['ln2_g'], W['ln2_b'], scale=1.0, name="wo_ln", lp=lp)
    sv = dict(sv1=sv1, h1b=h1b, proj=proj, cqn=cqn, ckvn=ckvn, q96=q96, k96=k96, v=v, o=o, ya=ya,
              vconv=vconv, yb=yb, ub=ub, ab_re=ab_re, ab_im=ab_im, bb=bb, s=s, y=y, ygb=ygb, zg=zg, glb=glb, yc=yc,
              mixed=mixed, z2=z2)
    if not ffn2:
        return h2, h2b, sv
    h3, h3b, sv['sv3'] = ffn_fwd(h2, h2b, W['wg2'], W['wu2'], W['wd2'], W['ln3_g'], W['ln3_b'], lp)
    return h3, h3b, sv


def layer_bwd(dh3, sv, W, tabs, lp, ffn1=True, ffn2=True):
    cos, sin, rot = tabs[:3]
    proj = sv['proj']
    G = {}
    dh2 = dh3
    if ffn2:
        dh2, g3 = ffn_bwd(dh3, sv['sv3'], W['wg2'], W['wu2'], W['wd2'], W['ln3_g'], lp)
        G.update(wg2=g3['wg'], wu2=g3['wu'], wd2=g3['wd'], ln3_g=g3['ln_g'], ln3_b=g3['ln_b'])
    dz2, dz2b, G['ln2_g'], G['ln2_b'] = ln_bwd(dh2, sv['z2'], W['ln2_g'], fscale=1.0, name="wo_ln_bwd", lp=lp)
    dmix = matmul(dz2b[None], W['w_o'][None], tb=True, name="wo_dx")[0]
    G['w_o'] = matmul(sv['mixed'][None], dz2b[None], ta=True, out_dtype=WGRAD, name="wo_dw")[0]
    dya, dyb, dyc, dproj = merge_bwd(dmix, proj, sv['ya'], sv['yb'], sv['yc'], lp)
    dgl = matmul(dyc[None], W['s5_wout'][None], tb=True, name="s5_out_dx")[0]
    G['s5_wout'] = matmul(sv['glb'][None], dyc[None], ta=True, out_dtype=WGRAD, name="s5_out_dw")[0]
    t1, dzb, G['s5_b_glu'] = s5_glu_bwd(dgl, sv['zg'], sv['y'], W['s5_b_glu'], lp)
    dyg = matmul(dzb[None], W['s5_wglu'][None], tb=True, res=t1[None], name="s5_glu_dx")[0]
    G['s5_wglu'] = matmul(sv['ygb'][None], dzb[None], ta=True, out_dtype=WGRAD, name="s5_glu_dw")[0]
    dyb_, du_d, G['s5_d'] = s5_y_bwd(dyg, sv['y'], proj, W['s5_d'], lp)
    ds = bd_matmul(dyb_[None], W['s5_ct'], w_t=True, reduce=False, name="s5_cs_dx")
    G['s5_ct'] = bd_outer(sv['s'], dyb_[None], "s5_cs_dw")
    g_adj, d_ab = s5_scan_bwd(ds, sv['s'], sv['ab_re'], sv['ab_im'], lp)
    du = bd_matmul(g_adj, sv['bb'], w_t=True, reduce=True, res=du_d[None], name="s5_bu_dx")[0]
    d_bb = bd_outer(sv['ub'][None], g_adj, "s5_bu_dw")
    du_b = s5_du(du, lp)
    onehot = (jnp.arange(S5_GROUP) == 0).astype(F32)
    spread = lambda t: (t.reshape(S5_GROUPS, 1, S5_STATE) * onehot[None, :, None]).reshape(_S5_ROWS, S5_STATE)
    take = lambda t: _blockdiag_extract(t, S5_GROUP, S5_STATE).reshape(_S5_ROWS, S5_STATE)
    sel = jnp.kron(jnp.eye(S5_GROUPS, dtype=F32), jnp.ones((1, S5_GROUP), F32))
    (G['s5_a_re'], G['s5_a_im'], G['s5_log_dt'], G['s5_b_re'], G['s5_b_im']) = s5_prep_bwd(
        W['s5_a_re'], W['s5_a_im'], W['s5_log_dt'], W['s5_b_re'], W['s5_b_im'],
        spread(d_ab[0]), spread(d_ab[1]), take(d_bb[0]), take(d_bb[1]), sel)
    dv = matmul(dyb[None], W['conv_wout'][None], tb=True, name="conv_out_dx")[0]
    G['conv_wout'] = matmul(sv['vconv'][None], dyb[None], ta=True, out_dtype=WGRAD, name="conv_out_dw")[0]
    dxbar, dbg, dcg, G['conv_w'], G['conv_b'] = conv_bwd(dv, proj, W['conv_w'], W['conv_b'], lp)
    do = heads_out_dx(dya, W['mla_wo'], lp)
    G['mla_wo'] = matmul(sv['o'], dya[None], ta=True, ab='o', out_dtype=WGRAD, name="mla_out_dw")
    dq96, dk96, dvv = attn_bwd(sv['q96'], sv['k96'], sv['v'], do, lp)
    dcq, dckv, dqp, dkr = mla_heads_bwd(dq96, dk96, dvv, *tabs[3:], W['wq'], W['wk'], W['wv'], lp)
    G['wq'] = matmul(sv['cqn'][None], dqp, ta=True, bb='o', out_dtype=WGRAD, name="mla_dwq")
    G['wk'] = matmul(sv['ckvn'][None], dk96, ta=True, bb='o', out_dtype=WGRAD, name="mla_dwk")
    G['wv'] = matmul(sv['ckvn'][None], dvv, ta=True, bb='o', out_dtype=WGRAD, name="mla_dwv")
    dcq_raw, dckv_raw, dkr_raw, G['q_norm_g'], G['kv_norm_g'] = mla_prep_bwd(
        dcq, dckv, dkr, proj, cos, sin, rot, W['q_norm_g'], W['kv_norm_g'], lp)
    dproj = dproj_fill(dproj, dcq_raw, dkr_raw, dckv_raw, dxbar, dbg, dcg, du_b, jnp.eye(D_ROPE, 128, dtype=BF16), lp)
    dh1 = matmul(dproj[None], W['w_in'][None], res=dz2[None], res_scale=ALPHA, name="proj_dx")[0]
    G['w_in'] = matmul(dproj[None], sv['h1b'][None], ta=True, out_dtype=WGRAD, name="proj_dw")[0]
    if not ffn1:
        return dh1, G
    dh0, g1 = ffn_bwd(dh1, sv['sv1'], W['wg1'], W['wu1'], W['wd1'], W['ln1_g'], lp)
    G.update(wg1=g1['wg'], wu1=g1['wu'], wd1=g1['wd'], ln1_g=g1['ln_g'], ln1_b=g1['ln_b'])
    return dh0, G


def _nat_cols(st):
    return jnp.transpose(st, (1, 0, 2)).reshape(st.shape[1], -1)


def _shard_cols(nat):
    k, n = nat.shape
    return jnp.transpose(nat.reshape(k, N_SHARD, n // N_SHARD), (1, 0, 2))


def _win_pad(wt):
    z = lambda n: jnp.zeros((n, wt.shape[1]), wt.dtype)
    return jnp.concatenate([wt[0:384], wt[640:672], z(96), wt[384:640], z(256), wt[672:]], axis=0)


def _win_unpad(wp):
    return jnp.concatenate([wp[0:384], wp[512:768], wp[384:416], wp[1024:]], axis=0)


_BIG = [('ffn1_w_gate', 'T'), ('ffn1_w_up', 'T'), ('ffn1_w_down', 0), ('w_in', 'T'), ('mla_w_uq', 1), ('mla_w_ukv', 1),
        ('mla_w_o', 1), ('conv_w_out', 1), ('s5_w_glu', 0), ('s5_w_out', 1), ('w_o', 0),
        ('ffn2_w_gate', 'T'), ('ffn2_w_up', 'T'), ('ffn2_w_down', 0)]
_REPL = ['ln1_g', 'ln1_b', 'mla_q_norm_g', 'mla_kv_norm_g', 'conv_b', 's5_a_re', 's5_a_im', 's5_log_dt', 's5_b_re',
         's5_b_im', 's5_c_re', 's5_c_im', 's5_d', 's5_b_glu', 'ln2_g', 'ln2_b', 'ln3_g', 'ln3_b']


def compute_weights(st, small):
    W = {}
    for t in ('1', '2'):
        if 'ffn%s_w_gate' % t in st:
            W['wg' + t], W['wu' + t], W['wd' + t] = (st['ffn%s_w_%s' % (t, p)] for p in ('gate', 'up', 'down'))
    if 'w_in' in st:
        W.update(_mixer_weights(st))
    if small is not None:
        W.update(_small_weights(small))
    return W


def _mixer_weights(st):
    W = {}
    W['w_in'] = _win_pad(st['w_in'].reshape(D_IN, D_MODEL))
    W['wq'] = jnp.transpose(_nat_cols(st['mla_w_uq']).reshape(Q_RANK, N_HEADS, D_QK), (1, 0, 2))
    ukv = jnp.transpose(_nat_cols(st['mla_w_ukv']).reshape(KV_RANK, N_HEADS, D_NOPE + D_V), (1, 0, 2))
    W['wk'] = jnp.concatenate([ukv[:, :, :D_NOPE], jnp.zeros((N_HEADS, KV_RANK, D_ROPE), ukv.dtype)], axis=2)
    W['wv'] = ukv[:, :, D_NOPE:]
    W['mla_wo'] = _nat_cols(st['mla_w_o']).reshape(N_HEADS, D_V, D_MODEL)
    W['conv_wout'] = _nat_cols(st['conv_w_out'])
    W['s5_wglu'] = st['s5_w_glu'].reshape(MIX, MIX)
    W['s5_wout'] = _nat_cols(st['s5_w_out'])
    W['w_o'] = st['w_o'].reshape(D_MODEL, D_MODEL)
    return W


def _small_weights(small):
    W = {}
    W['conv_w'] = small['conv_w']
    for n in ('ln1_g', 'ln1_b', 'ln2_g', 'ln2_b', 'ln3_g', 'ln3_b', 'conv_b', 's5_b_glu'):
        W[n] = small[n].reshape(1, -1)
    W['q_norm_g'] = small['mla_q_norm_g'].reshape(1, -1)
    W['kv_norm_g'] = small['mla_kv_norm_g'].reshape(1, -1)
    W['s5_d'] = small['s5_d'].reshape(1, MIX)
    rep = lambda t: jnp.repeat(t, S5_GROUP, axis=0)
    W['s5_a_re'], W['s5_a_im'] = rep(small['s5_a_re']), rep(small['s5_a_im'])
    W['s5_log_dt'] = jnp.broadcast_to(rep(small['s5_log_dt'].reshape(S5_GROUPS, 1)), (_S5_ROWS, S5_STATE))
    tr = lambda t: jnp.transpose(t, (0, 2, 1)).reshape(_S5_ROWS, S5_STATE)
    W['s5_b_re'], W['s5_b_im'] = tr(small['s5_b_re']), tr(small['s5_b_im'])
    ct = lambda t: _blockdiag(jnp.transpose(t, (0, 2, 1)))
    W['s5_ct'] = jnp.stack([ct(small['s5_c_re']), -ct(small['s5_c_im'])]).astype(BF16)
    return W


def reference_grads(G, ffn=True):
    R = {}
    for t in ('1', '2') if ffn else ():
        R['ffn%s_w_gate' % t] = G['wg' + t].reshape(D_FF, D_MODEL).T
        R['ffn%s_w_up' % t] = G['wu' + t].reshape(D_FF, D_MODEL).T
        R['ffn%s_w_down' % t] = G['wd' + t].reshape(D_FF, D_MODEL)
    R['w_in_t'] = _win_unpad(G['w_in'])
    if ffn:
        R['w_in'] = R['w_in_t'].T
    R['mla_w_uq'] = jnp.transpose(G['wq'], (1, 0, 2)).reshape(Q_RANK, -1)
    R['mla_w_ukv'] = jnp.transpose(jnp.concatenate([G['wk'][:, :, :D_NOPE], G['wv']], axis=2), (1, 0, 2)).reshape(KV_RANK, -1)
    R['mla_w_o'] = G['mla_wo'].reshape(N_HEADS * D_V, D_MODEL)
    R['conv_w'], R['conv_w_out'] = G['conv_w'], G['conv_wout']
    R['s5_w_glu'], R['s5_w_out'], R['w_o'] = G['s5_wglu'], G['s5_wout'], G['w_o']
    for n in ('ln1_g', 'ln1_b', 'ln2_g', 'ln2_b', 'ln3_g', 'ln3_b', 'conv_b', 's5_b_glu'):
        if n in G:
            R[n] = G[n].reshape(-1)
    R['mla_q_norm_g'], R['mla_kv_norm_g'] = G['q_norm_g'].reshape(-1), G['kv_norm_g'].reshape(-1)
    R['s5_d'] = G['s5_d'].reshape(S5_GROUPS, S5_GROUP)
    R['s5_a_re'], R['s5_a_im'], R['s5_log_dt'] = G['s5_a_re'], G['s5_a_im'], G['s5_log_dt'].reshape(-1)
    untr = lambda t: jnp.transpose(t.reshape(S5_GROUPS, S5_GROUP, S5_STATE), (0, 2, 1))
    R['s5_b_re'], R['s5_b_im'] = untr(G['s5_b_re']), untr(G['s5_b_im'])
    unct = lambda t: jnp.transpose(_blockdiag_extract(t, S5_STATE, S5_GROUP), (0, 2, 1))
    R['s5_c_re'], R['s5_c_im'] = unct(G['s5_ct'][0]), -unct(G['s5_ct'][1])
    return R


_ANY = pl.BlockSpec(memory_space=pl.ANY)
LANES = 1024


def _place():
    x, y, c = lax.axis_index("x"), lax.axis_index("y"), lax.axis_index("c")
    chips = [(1 - x, y), (x, 1 - y), (1 - x, 1 - y)]
    return x, y, c, chips


def _rows_of(c, half):
    return pl.ds(pl.multiple_of(c * half, 8), half)


def all_gather_shards(srcs, exact):
    n, m = len(srcs), len(exact)
    halves = [s.shape[0] // 2 for s in srcs]

    def body(*refs):
        s_refs, e_refs = refs[:n], refs[n:n + m]
        o_refs, eo_refs = refs[n + m:2 * n + m], refs[2 * n + m:2 * n + 2 * m]
        send, recv, esend, erecv, osend, orecv, lsem = refs[2 * n + 2 * m:]
        x, y, c, chips = _place()
        me = 2 * x + y
        sibling = (x, y, 1 - c)
        own = [pltpu.make_async_remote_copy(src_ref=s_refs[k], dst_ref=o_refs[k].at[me], send_sem=osend.at[k],
                                            recv_sem=orecv.at[k], device_id=sibling, device_id_type=MESH) for k in range(n)]
        local = [pltpu.make_async_copy(e_refs[k], eo_refs[k].at[me], lsem.at[k]) for k in range(m)]
        for cp in own + local:
            cp.start()

        def copy(k, s, src, idx, half_c, to):
            return pltpu.make_async_remote_copy(
                src_ref=src, dst_ref=o_refs[k].at[idx, _rows_of(half_c, halves[k])], send_sem=send.at[6 * k + s],
                recv_sem=recv.at[6 * k + s], device_id=to, device_id_type=MESH)

        def ecopy(k, j, idx, to):
            return pltpu.make_async_remote_copy(src_ref=e_refs[k], dst_ref=eo_refs[k].at[idx], send_sem=esend.at[3 * k + j],
                                                recv_sem=erecv.at[3 * k + j], device_id=to, device_id_type=MESH)

        sends = []
        for k in range(n):
            mine = s_refs[k].at[_rows_of(c, halves[k])]
            sends += [copy(k, j, mine, me, c, (*chip, c)) for j, chip in enumerate(chips)]
        for k in range(m):
            sends += [ecopy(k, j, me, (*chip, c)) for j, chip in enumerate(chips)]
        for cp in sends:
            cp.start()
        for j, chip in enumerate(chips):
            idx = 2 * chip[0] + chip[1]
            for k in range(n):
                landed = o_refs[k].at[idx, _rows_of(c, halves[k])]
                copy(k, j, landed, idx, c, sibling).wait_recv()
                fwd = copy(k, 3 + j, landed, idx, c, sibling)
                fwd.start()
                sends.append(fwd)
        for j, chip in enumerate(chips):
            idx = 2 * chip[0] + chip[1]
            for k in range(n):
                copy(k, 3 + j, s_refs[k].at[_rows_of(c, halves[k])], idx, 1 - c, sibling).wait_recv()
            for k in range(m):
                ecopy(k, j, idx, sibling).wait_recv()
        for cp in sends:
            cp.wait_send()
        for cp in own + local:
            cp.wait()

    outs = pl.pallas_call(
        body, name="all_gather_weights", in_specs=[_ANY] * (n + m), out_specs=[_ANY] * (n + m),
        out_shape=[jax.ShapeDtypeStruct((N_SHARD,) + a.shape, a.dtype) for a in list(srcs) + list(exact)],
        scratch_shapes=[pltpu.SemaphoreType.DMA((6 * n,)), pltpu.SemaphoreType.DMA((6 * n,)),
                        pltpu.SemaphoreType.DMA((3 * m,)), pltpu.SemaphoreType.DMA((3 * m,)),
                        pltpu.SemaphoreType.DMA((n,)), pltpu.SemaphoreType.DMA((n,)), pltpu.SemaphoreType.DMA((m,))],
    )(*srcs, *exact)
    return outs[:n], outs[n:]


def rs_pair_swap(gs):
    n = len(gs)

    def body(*refs):
        g_refs, r_refs, send, recv = refs[:n], refs[n:2 * n], refs[2 * n], refs[2 * n + 1]
        x, y, c, _ = _place()
        copies = [pltpu.make_async_remote_copy(
            src_ref=g_refs[k].at[pl.ds(0, N_SHARD), _rows_of(1 - c, gs[k].shape[1] // 2)], dst_ref=r_refs[k],
            send_sem=send.at[k], recv_sem=recv.at[k], device_id=(x, y, 1 - c), device_id_type=MESH) for k in range(n)]
        for cp in copies:
            cp.start()
        for cp in copies:
            cp.wait()

    return pl.pallas_call(
        body, name="grad_pair_swap", in_specs=[_ANY] * n, out_specs=[_ANY] * n,
        out_shape=[jax.ShapeDtypeStruct((N_SHARD, g.shape[1] // 2, g.shape[2]), g.dtype) for g in gs],
        scratch_shapes=[pltpu.SemaphoreType.DMA((n,)), pltpu.SemaphoreType.DMA((n,))],
    )(*gs)


def _group_tile(half, n_cols, n_arrays):
    budget = (20 * 2 ** 20) // (6 * n_arrays)
    fits = [t for t in range(8, half + 1, 8) if half % t == 0 and t * n_cols * 4 <= budget]
    return max(fits) if fits else 8


def rs_pair_add(gs, rs, cidx, out_dtype, name):
    n = len(gs)
    _, K, cols = gs[0].shape
    half = K // 2
    tr = _group_tile(half, cols, n)
    nb = half // tr

    def body(c_ref, *refs):
        for g_ref, r_ref, o_ref in zip(refs[:n], refs[n:2 * n], refs[2 * n:]):
            o_ref[...] = (g_ref[...].astype(F32) + r_ref[...].astype(F32)).astype(out_dtype)

    gspec = pl.BlockSpec((None, tr, cols), lambda j, i, c: (j, c[0] * nb + i, 0))
    rspec = pl.BlockSpec((None, tr, cols), lambda j, i, c: (j, i, 0))
    return pl.pallas_call(
        body, name=name,
        grid_spec=pltpu.PrefetchScalarGridSpec(num_scalar_prefetch=1, grid=(N_SHARD, nb), in_specs=[gspec] * n + [rspec] * n,
                                               out_specs=[rspec] * n),
        out_shape=[jax.ShapeDtypeStruct((N_SHARD, half, cols), out_dtype)] * n,
        compiler_params=_cparams(),
    )(cidx, *gs, *rs)


def rs_chip_sum(qs, nl, cidx, name):
    n = len(qs)
    _, half, cols = qs[0].shape
    tr = _group_tile(half, cols, n)
    nb = half // tr

    def body(c_ref, *refs):
        for k, q_ref in enumerate(refs[:n]):
            o_ref = refs[n + k // nl]
            o_ref[k % nl] = ((q_ref[0].astype(F32) + q_ref[1].astype(F32)) + q_ref[2].astype(F32)) + q_ref[3].astype(F32)

    return pl.pallas_call(
        body, name=name,
        grid_spec=pltpu.PrefetchScalarGridSpec(
            num_scalar_prefetch=1, grid=(nb,),
            in_specs=[pl.BlockSpec((N_SHARD, tr, cols), lambda i, c: (0, i, 0))] * n,
            out_specs=[pl.BlockSpec((nl, tr, cols), lambda i, c: (0, c[0] * nb + i, 0))] * (n // nl)),
        out_shape=[jax.ShapeDtypeStruct((nl, 2 * half, cols), F32)] * (n // nl),
        compiler_params=_cparams(),
    )(cidx, *qs)


def rs_pair_gather(fs, name):
    n = len(fs)

    def body(*refs):
        f_refs, send, recv = refs[n:2 * n], refs[2 * n], refs[2 * n + 1]
        x, y, c, _ = _place()
        copies = []
        for k in range(n):
            rows = f_refs[k].at[pl.ds(0, fs[k].shape[0]), _rows_of(c, fs[k].shape[1] // 2)]
            copies.append(pltpu.make_async_remote_copy(src_ref=rows, dst_ref=rows, send_sem=send.at[k], recv_sem=recv.at[k],
                                                       device_id=(x, y, 1 - c), device_id_type=MESH))
        for cp in copies:
            cp.start()
        for cp in copies:
            cp.wait()

    return pl.pallas_call(
        body, name=name, in_specs=[_ANY] * n, out_specs=[_ANY] * n,
        out_shape=[jax.ShapeDtypeStruct(f.shape, f.dtype) for f in fs],
        input_output_aliases={k: k for k in range(n)},
        scratch_shapes=[pltpu.SemaphoreType.DMA((n,)), pltpu.SemaphoreType.DMA((n,))],
    )(*fs)


_HBM = pl.BlockSpec(memory_space=pltpu.HBM)
_SEM = pl.BlockSpec(memory_space=pltpu.SEMAPHORE)
_EFFECT = pltpu.SideEffectType.DATAFLOW_SIDE_EFFECTING


def _in_hbm(a):
    return pltpu.with_memory_space_constraint(a, pltpu.HBM)


def split_start(name, srcs, lands, after, copies_fn, n_copies):
    n = len(srcs)

    def body(*refs):
        for cp in copies_fn(refs[:n], refs[n:2 * n], refs[2 * n + 1], refs[2 * n + 2]):
            cp.start()
        refs[-1][...] = jnp.zeros_like(refs[-1])

    bufs = list(srcs) + list(lands)
    outs = pl.pallas_call(
        body, name=name,
        out_shape=(pltpu.SemaphoreType.DMA((n_copies,)), pltpu.SemaphoreType.DMA((n_copies,)),
                   *[pltpu.HBM(a.shape, a.dtype) for a in bufs], jax.ShapeDtypeStruct((8, 128), F32)),
        in_specs=[_HBM] * (2 * n) + [_ANY],
        out_specs=(_SEM, _SEM, *[_HBM] * (2 * n), pl.BlockSpec(memory_space=pltpu.VMEM)),
        input_output_aliases={i: 2 + i for i in range(2 * n)},
        compiler_params=pltpu.CompilerParams(has_side_effects=_EFFECT),
    )(*[_in_hbm(a) for a in bufs], after)
    return outs[0], outs[1], outs[2:2 + n], outs[2 + n:2 + 2 * n], outs[-1]


def split_wait(name, send, recv, srcs, lands, after, copies_fn, which=None):
    n = len(srcs)

    def body(*refs):
        copies = copies_fn(refs[:n], refs[n:2 * n], refs[2 * n], refs[2 * n + 1], which)
        for cp in copies:
            cp.wait_send()
        for cp in copies:
            cp.wait_recv()

    bufs = list(srcs) + list(lands)
    outs = pl.pallas_call(
        body, name=name, out_shape=tuple(pltpu.HBM(a.shape, a.dtype) for a in bufs),
        in_specs=[_HBM] * (2 * n) + [_SEM, _SEM, _ANY], out_specs=tuple([_HBM] * (2 * n)),
        input_output_aliases={i: i for i in range(2 * n)},
        compiler_params=pltpu.CompilerParams(has_side_effects=_EFFECT),
    )(*bufs, send, recv, after)
    return list(outs[:n]), list(outs[n:])


def _gather_copies(s_refs, l_refs, send, recv, which=None):
    x, y, c, chips = _place()
    me = 2 * x + y
    out = []
    for k in (range(len(s_refs)) if which is None else which):
        s, l = s_refs[k], l_refs[k]
        rows = _rows_of(c, s.shape[0] // 2)
        for j, chip in enumerate(chips):
            out.append(pltpu.make_async_remote_copy(src_ref=s.at[rows], dst_ref=l.at[me, rows], send_sem=send.at[4 * k + j],
                                                    recv_sem=recv.at[4 * k + j], device_id=(*chip, c), device_id_type=MESH))
        out.append(pltpu.make_async_remote_copy(src_ref=s, dst_ref=l.at[me], send_sem=send.at[4 * k + 3],
                                                recv_sem=recv.at[4 * k + 3], device_id=(x, y, 1 - c), device_id_type=MESH))
    return out


def _scatter_copies(s_refs, l_refs, send, recv, which=None):
    x, y, c, chips = _place()
    me = 2 * x + y
    return [pltpu.make_async_remote_copy(src_ref=s_refs[k].at[2 * chip[0] + chip[1]], dst_ref=l_refs[k].at[me],
                                         send_sem=send.at[3 * k + j], recv_sem=recv.at[3 * k + j], device_id=(*chip, c),
                                         device_id_type=MESH)
            for k in (range(len(s_refs)) if which is None else which) for j, chip in enumerate(chips)]


def gather_forward(lands, name):
    n = len(lands)

    def body(*refs):
        l_refs, send, recv = refs[n:2 * n], refs[2 * n], refs[2 * n + 1]
        x, y, c, chips = _place()
        copies = []
        for k in range(n):
            rows = _rows_of(c, lands[k].shape[1] // 2)
            for j, chip in enumerate(chips):
                part = l_refs[k].at[2 * chip[0] + chip[1], rows]
                copies.append(pltpu.make_async_remote_copy(src_ref=part, dst_ref=part, send_sem=send.at[3 * k + j],
                                                           recv_sem=recv.at[3 * k + j], device_id=(x, y, 1 - c),
                                                           device_id_type=MESH))
        for cp in copies:
            cp.start()
        for cp in copies:
            cp.wait()

    return pl.pallas_call(
        body, name=name, in_specs=[_ANY] * n, out_specs=[_ANY] * n,
        out_shape=[jax.ShapeDtypeStruct(a.shape, a.dtype) for a in lands],
        input_output_aliases={k: k for k in range(n)},
        scratch_shapes=[pltpu.SemaphoreType.DMA((3 * n,)), pltpu.SemaphoreType.DMA((3 * n,))],
    )(*lands)


def rs_partials(gs, wire, cidx, tag):
    rs = rs_pair_swap(gs)
    groups = {}
    for k, g in enumerate(gs):
        groups.setdefault((g.shape, jnp.dtype(wire[k]).name), []).append(k)
    ps = [None] * len(gs)
    for gi, ks in enumerate(groups.values()):
        outs = rs_pair_add([gs[k] for k in ks], [rs[k] for k in ks], cidx, wire[ks[0]], "grad_pair_add_%s%d" % (tag, gi))
        for k, o in zip(ks, outs):
            ps[k] = o
    return ps


def rs_finish(items, tag):
    cidx = lax.axis_index("c").astype(jnp.int32).reshape(1)
    groups = {}
    for i, it in enumerate(items):
        groups.setdefault((it[0].shape, len(it), it[0].dtype.name), []).append(i)
    fs = [None] * len(items)
    for gi, ids in enumerate(groups.values()):
        outs = rs_chip_sum([q for i in ids for q in items[i]], len(items[ids[0]]), cidx, "grad_chip_sum_%s%d" % (tag, gi))
        for i, o in zip(ids, outs):
            fs[i] = o
    return rs_pair_gather(fs, "grad_pair_gather_" + tag)


def adamw(w, g, m, v, name):
    shape = w.shape
    if w.ndim == 2:
        block, grid, index = shape, (1,), (lambda i: (0, 0))
    else:
        slab = shape[2:]
        unit = 4 * int(np.prod(slab[:-2] or (1,))) * (-(-slab[-1] // 128) * 128)
        if len(slab) >= 2:
            unit *= -(-slab[-2] // 8) * 8
        k = shape[1]
        tr = k
        if k * unit > 2 ** 21:
            tr = max(t for t in range(8, k, 8) if k % t == 0 and t * unit <= 2 ** 21)
        block, grid = (None, tr) + tuple(slab), (shape[0], k // tr)
        index = lambda l, i: (l, i) + (0,) * len(slab)
        if tr < min(k, 64) and len(slab) == 1:
            tc = max(t for t in range(128, slab[0] + 1, 128) if slab[0] % t == 0 and k * t * 4 <= 2 ** 21)
            block, grid = (None, k, tc), (shape[0], slab[0] // tc)
            index = lambda l, i: (l, 0, i)

    def body(w_ref, g_ref, m_ref, v_ref, d_ref, nm_ref, nv_ref):
        g_ = g_ref[...]
        m_new = ADAM_B1 * m_ref[...] + (1.0 - ADAM_B1) * g_
        v_new = ADAM_B2 * v_ref[...] + (1.0 - ADAM_B2) * (g_ * g_)
        m_hat = m_new / (1.0 - ADAM_B1 ** ADAM_STEP)
        v_hat = v_new / (1.0 - ADAM_B2 ** ADAM_STEP)
        d_ref[...] = -ADAM_LR * (m_hat / (jnp.sqrt(v_hat) + ADAM_EPS) + ADAM_WD * w_ref[...])
        nm_ref[...] = m_new
        nv_ref[...] = v_new

    spec = pl.BlockSpec(block, index)
    return pl.pallas_call(
        body, name=name, grid=grid, in_specs=[spec] * 4, out_specs=[spec] * 3,
        out_shape=[jax.ShapeDtypeStruct(shape, F32)] * 3, compiler_params=_cparams(),
    )(w, g, m, v)


_WEIGHTS = ['meta', 'ffn1_w_gate', 'ffn1_w_up', 'ffn1_w_down', 'ln1_g', 'ln1_b', 'w_in', 'mla_q_norm_g', 'mla_w_uq',
            'mla_kv_norm_g', 'mla_w_ukv', 'mla_w_o', 'conv_w', 'conv_b', 'conv_w_out', 's5_a_re', 's5_a_im', 's5_log_dt',
            's5_b_re', 's5_b_im', 's5_c_re', 's5_c_im', 's5_d', 's5_w_glu', 's5_b_glu', 's5_w_out', 'w_o', 'ln2_g', 'ln2_b',
            'ffn2_w_gate', 'ffn2_w_up', 'ffn2_w_down', 'ln3_g', 'ln3_b']


def _pad_to(flat, n):
    return jnp.concatenate([flat, jnp.zeros((n - flat.shape[0],), flat.dtype)])


def _shard_of(full, axis):
    if axis == 1:
        return _shard_cols(full)
    if axis == 'T':
        return full.T.reshape(N_SHARD, full.shape[1] // N_SHARD, full.shape[0])
    return full.reshape(N_SHARD, full.shape[0] // N_SHARD, full.shape[1])


_FFN_KEY = {'gate': 'wg', 'up': 'wu', 'down': 'wd'}


def _pad_rows(a, axis):
    k = a.shape[axis]
    extra = -k % 32
    if not extra:
        return a
    return jnp.pad(a, [(0, extra) if d == axis else (0, 0) for d in range(a.ndim)])


def _step(env):
    w = {n: env[n] for n in _WEIGHTS}
    mom = {n: env['m_' + n] for n in _WEIGHTS}
    var = {n: env['v_' + n] for n in _WEIGHTS}
    cidx = lax.axis_index("c").astype(jnp.int32).reshape(1)
    chip = 2 * lax.axis_index("x") + lax.axis_index("y")
    big_names = [n for n, _ in _BIG]
    nb = len(big_names)

    kept_t = [n for n, a in _BIG if a == 'T']
    own = {n: (jnp.swapaxes(w[n], 1, 2) if n in kept_t else w[n]) for n in big_names}
    first = [n for n in big_names if n.startswith('ffn1')]
    mix = [n for n in big_names if not n.startswith('ffn')]
    last = [n for n in big_names if n.startswith('ffn2')]
    rest = mix + last
    nm, nr = len(mix), len(mix) + len(last)
    src = lambda n, li: _pad_rows(own[n][li].astype(BF16), 0)
    gathered_first, (conv_w_st, meta_st) = all_gather_shards([src(n, 0) for n in first], [w['conv_w'], w['meta']])
    later = [src(n, 0) for n in rest] + [src(n, 1) for n in big_names]
    lands = [lax.empty((N_SHARD,) + s.shape, BF16) for s in later]
    g_send, g_recv, later_t, lands_t, token = split_start("gather_start", later, lands, gathered_first[0], _gather_copies,
                                                          4 * len(later))

    def weights_of(names, st, li, with_small):
        small = None
        if with_small:
            small = {n: w[n][li] for n in _REPL}
            small['conv_w'] = _nat_cols(conv_w_st[:, li])
        return compute_weights({n: a[:, :own[n].shape[1]] for n, a in zip(names, st)}, small)

    x2d = env['x'][0]
    lp = x2d.shape[0] + X0
    tabs = _rope_tables(lp)
    h = jnp.concatenate([jnp.zeros((PAD, D_MODEL), F32), _nat_cols(meta_st), x2d], axis=0) + token[0, 0]
    W0 = weights_of(first, gathered_first, 0, True)
    ffn1 = ffn_fwd(h, h.astype(BF16), W0['wg1'], W0['wu1'], W0['wd1'], W0['ln1_g'], W0['ln1_b'], lp)
    later_t, lands_t = split_wait("gather0_wait", g_send, g_recv, later_t, lands_t, ffn1[0], _gather_copies, range(nm))
    W0.update(weights_of(mix, gather_forward(lands_t[:nm], "gather0_forward"), 0, False))
    h, hb, sv0 = layer_fwd(None, None, W0, tabs, lp, ffn1=ffn1, ffn2=False)
    later_t, lands_t = split_wait("gather0b_wait", g_send, g_recv, later_t, lands_t, h, _gather_copies, range(nm, nr))
    W0.update(weights_of(last, gather_forward(lands_t[nm:nr], "gather0b_forward"), 0, False))
    h, hb, sv0['sv3'] = ffn_fwd(h, hb, W0['wg2'], W0['wu2'], W0['wd2'], W0['ln3_g'], W0['ln3_b'], lp)
    _, lands_t = split_wait("gather1_wait", g_send, g_recv, later_t, lands_t, h, _gather_copies, range(nr, len(later)))
    W1 = weights_of(big_names, gather_forward(lands_t[nr:], "gather1_forward"), 1, True)
    h, hb, sv1 = layer_fwd(h, hb, W1, tabs, lp)
    tgt = jnp.concatenate([jnp.zeros((X0, D_MODEL), F32), env['loss_target'][0]], axis=0)
    dh, loss_part = loss_head(h, tgt, lp)
    loss = lax.psum(loss_part[0, 0], ("x", "y", "c"))

    def shards(G, names):
        full = None if all(n.startswith('ffn') for n in names) else reference_grads(G, ffn=False)

        def one(n, a):
            if n.startswith('ffn'):
                return G[_FFN_KEY[n.split('_')[-1]] + n[3]]
            if n == 'w_in':
                return full['w_in_t'].reshape(N_SHARD, D_IN // N_SHARD, D_MODEL)
            return _shard_of(full[n], a)

        return [_pad_rows(one(n, a), 1) for n, a in _BIG if n in names]

    def scatter_start(name, ps, after):
        qs = [lax.dynamic_update_slice_in_dim(jnp.zeros_like(p), lax.dynamic_slice_in_dim(p, chip, 1, axis=0), chip, axis=0)
              for p in ps]
        return split_start(name, ps, qs, after, _scatter_copies, 3 * len(ps))

    dh, G1 = layer_bwd(dh, sv1, W1, tabs, lp)
    p1 = rs_partials(shards(G1, big_names), [BF16] * nb, cidx, "b")
    s1_send, s1_recv, p1_t, q1_t, token1 = scatter_start("scatter1_start", p1, dh)
    dh, g3 = ffn_bwd(dh, sv0['sv3'], W0['wg2'], W0['wu2'], W0['wd2'], W0['ln3_g'] + token1[0, 0], lp)
    G0 = dict(wg2=g3['wg'], wu2=g3['wu'], wd2=g3['wd'])
    p0l = rs_partials(shards(G0, last), [BF16] * len(last), cidx, "c")
    sl_send, sl_recv, p0l_t, q0l_t, token0l = scatter_start("scatter0b_start", p0l, dh)
    dh, Gm = layer_bwd(dh, sv0, dict(W0, ln2_g=W0['ln2_g'] + token0l[0, 0]), tabs, lp, ffn1=False, ffn2=False)
    G0.update(Gm, ln3_g=g3['ln_g'], ln3_b=g3['ln_b'])
    p0m = rs_partials(shards(G0, mix), [BF16] * nm, cidx, "d")
    sm_send, sm_recv, p0m_t, q0m_t, token0m = scatter_start("scatter0_start", p0m, dh)
    dh, g1 = ffn_bwd(dh, sv0['sv1'], W0['wg1'], W0['wu1'], W0['wd1'], W0['ln1_g'] + token0m[0, 0], lp)
    G0.update(wg1=g1['wg'], wu1=g1['wu'], wd1=g1['wd'], ln1_g=g1['ln_g'], ln1_b=g1['ln_b'])
    _, q1 = split_wait("scatter1_wait", s1_send, s1_recv, p1_t, q1_t, dh, _scatter_copies)
    _, q0_last = split_wait("scatter0b_wait", sl_send, sl_recv, p0l_t, q0l_t, dh, _scatter_copies)
    _, q0_mix = split_wait("scatter0_wait", sm_send, sm_recv, p0m_t, q0m_t, dh, _scatter_copies)
    q0_rest = list(q0_mix) + list(q0_last)
    full = [reference_grads(G0, ffn=False), reference_grads(G1, ffn=False)]

    s_parts = [jnp.stack([full[li][n] for li in range(DEPTH)]).reshape(-1) for n in _REPL + ['conv_w']]
    s_parts.append(dh[PAD:X0].reshape(-1))
    s_sizes = [int(p.shape[0]) for p in s_parts]
    s_rows = -(-sum(s_sizes) // (16 * LANES)) * 16
    g_small = _pad_to(jnp.concatenate(s_parts), s_rows * LANES).reshape(1, s_rows, LANES)
    g_small = jnp.broadcast_to(g_small, (N_SHARD, s_rows, LANES))
    p_last = rs_partials(shards(G0, first) + [g_small], [BF16] * len(first) + [F32], cidx, "a")
    z_send, z_recv, pz_t, qz_t, token_z = scatter_start("scatter_last_start", p_last, dh)

    def step_weights(names, grad):
        out = {}
        for n in names:
            if n in kept_t:
                res = adamw(own[n], grad[n], jnp.swapaxes(mom[n], 1, 2), jnp.swapaxes(var[n], 1, 2), "adamw_" + n)
                out[n] = [jnp.swapaxes(t, 1, 2) for t in [grad[n]] + list(res)]
            else:
                out[n] = [grad[n]] + list(adamw(w[n], grad[n], mom[n], var[n], "adamw_" + n))
        return out

    q1 = dict(zip(big_names, q1))
    q0 = dict(zip(rest, q0_rest))
    q0[rest[0]] = q0[rest[0]] + token_z[0, 0].astype(BF16)
    red = rs_finish([[q0[n], q1[n]] for n in rest], "a")
    done = step_weights(rest, {n: r[:, :own[n].shape[1]] for n, r in zip(rest, red)})
    all_done = jnp.stack([done[n][3][(0,) * done[n][3].ndim] for n in rest])
    _, q_last = split_wait("scatter_last_wait", z_send, z_recv, pz_t, qz_t, all_done, _scatter_copies)
    red = rs_finish([[q, q1[n]] for n, q in zip(first, q_last)] + [[q_last[-1]]], "b")
    f_small = red[-1].reshape(-1)

    grad = {n: r[:, :own[n].shape[1]] for n, r in zip(first, red)}
    off = 0
    for n, sz in zip(_REPL + ['conv_w', 'meta'], s_sizes):
        grad[n] = f_small[off:off + sz]
        off += sz
    for n in _REPL:
        grad[n] = grad[n].reshape(w[n].shape)
    cw = grad['conv_w'].reshape(DEPTH, 3, MIX)
    grad['conv_w'] = lax.dynamic_slice_in_dim(cw, chip * (MIX // N_SHARD), MIX // N_SHARD, axis=2)
    gm = grad['meta'].reshape(N_META, D_MODEL)
    grad['meta'] = lax.dynamic_slice_in_dim(gm, chip * (D_MODEL // N_SHARD), D_MODEL // N_SHARD, axis=1)

    done.update(step_weights([n for n in _WEIGHTS if n not in done], grad))
    return (loss, dh[X0:][None], *[done[n][k] for k in range(4) for n in _WEIGHTS])


def kernel(x, meta, ffn1_w_gate, ffn1_w_up, ffn1_w_down, ln1_g, ln1_b, w_in, mla_q_norm_g, mla_w_uq, mla_kv_norm_g, mla_w_ukv, mla_w_o, conv_w, conv_b, conv_w_out, s5_a_re, s5_a_im, s5_log_dt, s5_b_re, s5_b_im, s5_c_re, s5_c_im, s5_d, s5_w_glu, s5_b_glu, s5_w_out, w_o, ln2_g, ln2_b, ffn2_w_gate, ffn2_w_up, ffn2_w_down, ln3_g, ln3_b, loss_target, m_meta, m_ffn1_w_gate, m_ffn1_w_up, m_ffn1_w_down, m_ln1_g, m_ln1_b, m_w_in, m_mla_q_norm_g, m_mla_w_uq, m_mla_kv_norm_g, m_mla_w_ukv, m_mla_w_o, m_conv_w, m_conv_b, m_conv_w_out, m_s5_a_re, m_s5_a_im, m_s5_log_dt, m_s5_b_re, m_s5_b_im, m_s5_c_re, m_s5_c_im, m_s5_d, m_s5_w_glu, m_s5_b_glu, m_s5_w_out, m_w_o, m_ln2_g, m_ln2_b, m_ffn2_w_gate, m_ffn2_w_up, m_ffn2_w_down, m_ln3_g, m_ln3_b, v_meta, v_ffn1_w_gate, v_ffn1_w_up, v_ffn1_w_down, v_ln1_g, v_ln1_b, v_w_in, v_mla_q_norm_g, v_mla_w_uq, v_mla_kv_norm_g, v_mla_w_ukv, v_mla_w_o, v_conv_w, v_conv_b, v_conv_w_out, v_s5_a_re, v_s5_a_im, v_s5_log_dt, v_s5_b_re, v_s5_b_im, v_s5_c_re, v_s5_c_im, v_s5_d, v_s5_w_glu, v_s5_b_glu, v_s5_w_out, v_w_o, v_ln2_g, v_ln2_b, v_ffn2_w_gate, v_ffn2_w_up, v_ffn2_w_down, v_ln3_g, v_ln3_b):
    return _step(dict(locals()))
```

```python
import functools
import math

import numpy as np
import jax
import jax.numpy as jnp
from jax import lax
from jax.experimental import pallas as pl
from jax.experimental.pallas import tpu as pltpu

F32 = jnp.float32
BF16 = jnp.bfloat16

D_MODEL = 1024
DEPTH = 2
N_META = 16
PAD = 112
X0 = PAD + N_META
N_HEADS = 8
D_NOPE = 64
D_ROPE = 32
D_V = 64
Q_RANK = 384
KV_RANK = 256
MIX = 512
S5_GROUPS = 32
S5_GROUP = 16
S5_STATE = 64
S5_LANES = S5_GROUPS * S5_STATE
D_FF = 2816
N_SHARD = 4
FF_SHARD = D_FF // N_SHARD
D_IN = 5792
P_IN = 6144
ALPHA = (2.0 * DEPTH) ** 0.25
LN_EPS = 1e-5
RMS_EPS = 1e-6
ATT_SCALE = (D_NOPE + D_ROPE) ** -0.5
ROPE_BASE = 10000.0
ADAM_LR, ADAM_B1, ADAM_B2, ADAM_EPS, ADAM_WD, ADAM_STEP = 0.001, 0.9, 0.999, 1e-08, 0.01, 10
SCAN_CHUNK = 64
VMEM_LIMIT = 52 * 2 ** 20
WGRAD = BF16
MESH = pl.DeviceIdType.MESH


def _cparams(**kw):
    return pltpu.CompilerParams(vmem_limit_bytes=VMEM_LIMIT, **kw)


def _tile(n):
    if n <= 1088:
        return n
    for t in (1088, 1024, 544, 512, 272, 256, 128):
        if n % t == 0:
            return t
    return n


def _row_tile(lp):
    for t in (544, 272, 128):
        if lp % t == 0:
            return t
    return lp


def _ffn_tile(lp):
    return 1088 if lp % 1088 == 0 else _row_tile(lp)


def _sigmoid(x):
    return 1.0 / (1.0 + jnp.exp(-x))


_GELU_C = math.sqrt(2.0 / math.pi)


def _gelu(x):
    return 0.5 * x * (1.0 + jnp.tanh(_GELU_C * (x + 0.044715 * x * x * x)))


def _gelu_grad(x):
    t = jnp.tanh(_GELU_C * (x + 0.044715 * x * x * x))
    return 0.5 * (1.0 + t) + 0.5 * x * (1.0 - t * t) * _GELU_C * (1.0 + 3.0 * 0.044715 * x * x)


def _dot(a, b, ca, cb, precision=None):
    return lax.dot_general(a, b, (((ca,), (cb,)), ((), ())), preferred_element_type=F32, precision=precision)


def matmul(a, b, *, name, ta=False, tb=False, ab='n', bb='n', res=None, res_scale=1.0, scale=1.0, out_dtype=F32):
    if ta:
        _, K, M = a.shape
    else:
        _, M, K = a.shape
    if tb:
        _, N, K2 = b.shape
    else:
        _, K2, N = b.shape
    assert K == K2, (a.shape, b.shape)
    n_out = max(a.shape[0] if ab == 'o' else 1, b.shape[0] if bb == 'o' else 1)
    n_red = max(a.shape[0] if ab == 'r' else 1, b.shape[0] if bb == 'r' else 1)
    tm, tn = _tile(M), _tile(N)
    tk = K if K <= 2304 else _tile(K)
    nkt = K // tk
    n_steps = n_red * nkt

    def bsel(mode, o, r):
        if mode == 'o':
            return o
        if mode == 'r':
            return r // nkt if nkt > 1 else r
        return 0

    def ksel(r):
        if nkt == 1:
            return 0
        return r % nkt if n_red > 1 else r

    a_map = (lambda o, i, j, r: (bsel(ab, o, r), ksel(r), i)) if ta else (lambda o, i, j, r: (bsel(ab, o, r), i, ksel(r)))
    b_map = (lambda o, i, j, r: (bsel(bb, o, r), j, ksel(r))) if tb else (lambda o, i, j, r: (bsel(bb, o, r), ksel(r), j))
    o_map = lambda o, i, j, r: (o, i, j)
    in_specs = [pl.BlockSpec((None, tk, tm) if ta else (None, tm, tk), a_map),
                pl.BlockSpec((None, tn, tk) if tb else (None, tk, tn), b_map)]
    operands = [a, b]
    if res is not None:
        in_specs.append(pl.BlockSpec((None, tm, tn), o_map))
        operands.append(res)
    has_res = res is not None

    def body(*refs):
        a_ref, b_ref = refs[0], refs[1]
        res_ref = refs[2] if has_res else None
        o_ref = refs[3] if has_res else refs[2]
        part = _dot(a_ref[...].astype(BF16), b_ref[...].astype(BF16), 0 if ta else 1, 1 if tb else 0)

        def finish(acc):
            v = acc if scale == 1.0 else acc * scale
            if has_res:
                v = v + res_scale * res_ref[...].astype(F32)
            o_ref[...] = v.astype(o_ref.dtype)

        if n_steps == 1:
            finish(part)
        else:
            acc_ref = refs[-1]
            r = pl.program_id(3)

            @pl.when(r == 0)
            def _():
                acc_ref[...] = part

            @pl.when(r > 0)
            def _():
                acc_ref[...] += part

            @pl.when(r == n_steps - 1)
            def _():
                finish(acc_ref[...])

    return pl.pallas_call(
        body, name=name,
        grid=(n_out, M // tm, N // tn, n_steps),
        in_specs=in_specs,
        out_specs=pl.BlockSpec((None, tm, tn), o_map),
        out_shape=jax.ShapeDtypeStruct((n_out, M, N), out_dtype),
        scratch_shapes=[pltpu.VMEM((tm, tn), F32)] if n_steps > 1 else [],
        compiler_params=_cparams(),
    )(*operands)


def rowwise(fn, rows, pars, outs, accs=(), *, name, lp):
    tm = _row_tile(lp)
    n_rows, n_pars, n_outs, n_accs = len(rows), len(pars), len(outs), len(accs)
    outs = [tuple(o) + (o[0], 0) if len(o) == 2 else tuple(o) for o in outs]
    in_specs = [pl.BlockSpec((tm, w), functools.partial(lambda i, cb: (i, cb), cb=cb)) for _, w, cb in rows]
    in_specs += [pl.BlockSpec(p.shape, functools.partial(lambda i, nd: (0,) * nd, nd=p.ndim)) for p in pars]
    out_specs = [pl.BlockSpec((tm, w), functools.partial(lambda i, cb: (i, cb), cb=cb)) for w, _, _, cb in outs]
    out_specs += [pl.BlockSpec(s, functools.partial(lambda i, nd: (0,) * nd, nd=len(s))) for s, _ in accs]
    out_shape = [jax.ShapeDtypeStruct((lp, total), dt) for _, dt, total, _ in outs]
    out_shape += [jax.ShapeDtypeStruct(s, dt) for s, dt in accs]

    def body(*refs):
        i = pl.program_id(0)
        rv = [r[...] for r in refs[:n_rows]]
        pv = [r[...] for r in refs[n_rows:n_rows + n_pars]]
        o_refs = refs[n_rows + n_pars:n_rows + n_pars + n_outs]
        a_refs = refs[n_rows + n_pars + n_outs:]
        ov, av = fn(i * tm, rv, pv)
        for r, v in zip(o_refs, ov):
            r[...] = v.astype(r.dtype)
        if n_accs:
            @pl.when(i == 0)
            def _():
                for r, v in zip(a_refs, av):
                    r[...] = v.astype(r.dtype)

            @pl.when(i > 0)
            def _():
                for r, v in zip(a_refs, av):
                    r[...] += v.astype(r.dtype)

    res = pl.pallas_call(
        body, name=name, grid=(lp // tm,), in_specs=in_specs, out_specs=out_specs, out_shape=out_shape,
        compiler_params=_cparams(),
    )(*[r[0] for r in rows], *pars)
    return res


def _row_mask(row0, shape):
    return (row0 + lax.broadcasted_iota(jnp.int32, shape, 0)) >= PAD


def ffn_up(hb, wg, wu, lp):
    tm = _ffn_tile(lp)

    def body(h_ref, wg_ref, wu_ref, ab_ref, hid_ref):
        h = h_ref[...]
        a = _dot(h, wg_ref[...], 1, 1)
        b = _dot(h, wu_ref[...], 1, 1)
        ab_ref[0] = a.astype(BF16)
        ab_ref[1] = b.astype(BF16)
        hid_ref[...] = (a * _sigmoid(a) * b).astype(BF16)

    wspec = pl.BlockSpec((None, FF_SHARD, D_MODEL), lambda j, i: (j, 0, 0))
    return pl.pallas_call(
        body, name="ffn_up", grid=(N_SHARD, lp // tm),
        in_specs=[pl.BlockSpec((tm, D_MODEL), lambda j, i: (i, 0)), wspec, wspec],
        out_specs=[pl.BlockSpec((None, 2, tm, FF_SHARD), lambda j, i: (j, 0, i, 0)),
                   pl.BlockSpec((None, tm, FF_SHARD), lambda j, i: (j, i, 0))],
        out_shape=[jax.ShapeDtypeStruct((N_SHARD, 2, lp, FF_SHARD), BF16),
                   jax.ShapeDtypeStruct((N_SHARD, lp, FF_SHARD), BF16)],
        compiler_params=_cparams(),
    )(hb, wg, wu)


def _layer_norm(z, g, b):
    mu = jnp.mean(z, axis=-1, keepdims=True)
    zc = z - mu
    var = jnp.mean(zc * zc, axis=-1, keepdims=True)
    return zc * lax.rsqrt(var + LN_EPS) * g + b


def mm_res_ln(a, w, res, g, b, *, scale, name, lp):
    n_red, _, K = a.shape
    tm = _row_tile(lp)

    def body(a_ref, w_ref, res_ref, g_ref, b_ref, z_ref, h_ref, hb_ref):
        acc = _dot(a_ref[0].astype(BF16), w_ref[0], 1, 0)
        for r in range(1, n_red):
            acc = acc + _dot(a_ref[r].astype(BF16), w_ref[r], 1, 0)
        z = ALPHA * res_ref[...] + scale * acc
        z_ref[...] = z
        hn = _layer_norm(z, g_ref[...], b_ref[...])
        h_ref[...] = hn
        hb_ref[...] = hn.astype(BF16)

    row = pl.BlockSpec((tm, D_MODEL), lambda i: (i, 0))
    par = pl.BlockSpec((1, D_MODEL), lambda i: (0, 0))
    return pl.pallas_call(
        body, name=name, grid=(lp // tm,),
        in_specs=[pl.BlockSpec((n_red, tm, K), lambda i: (0, i, 0)),
                  pl.BlockSpec((n_red, K, D_MODEL), lambda i: (0, 0, 0)), row, par, par],
        out_specs=[row, row, row],
        out_shape=[jax.ShapeDtypeStruct((lp, D_MODEL), F32), jax.ShapeDtypeStruct((lp, D_MODEL), F32),
                   jax.ShapeDtypeStruct((lp, D_MODEL), BF16)],
        compiler_params=_cparams(),
    )(a, w, res, g, b)


def ln_bwd(dh, z, g, *, fscale, name, lp):
    def fn(row0, rv, pv):
        dh_, z_ = rv
        g_, = pv
        mu = jnp.mean(z_, axis=-1, keepdims=True)
        zc = z_ - mu
        rstd = lax.rsqrt(jnp.mean(zc * zc, axis=-1, keepdims=True) + LN_EPS)
        xh = zc * rstd
        dxh = dh_ * g_
        m1 = jnp.mean(dxh, axis=-1, keepdims=True)
        m2 = jnp.mean(dxh * xh, axis=-1, keepdims=True)
        dz = rstd * (dxh - m1 - xh * m2)
        return ((dz, fscale * dz),
                (jnp.sum(dh_ * xh, axis=0, keepdims=True), jnp.sum(dh_, axis=0, keepdims=True)))

    return rowwise(fn, [(dh, D_MODEL, 0), (z, D_MODEL, 0)], [g], [(D_MODEL, F32), (D_MODEL, BF16)],
                   [((1, D_MODEL), F32), ((1, D_MODEL), F32)], name=name, lp=lp)


def ffn_down_bwd(dfb, wd, ab, lp):
    tm = _ffn_tile(lp)

    def body(df_ref, w_ref, ab_ref, da_ref, db_ref):
        dhid = _dot(df_ref[...], w_ref[...], 1, 1)
        a = ab_ref[0].astype(F32)
        b = ab_ref[1].astype(F32)
        sg = _sigmoid(a)
        da_ref[...] = (dhid * b * (sg * (1.0 + a * (1.0 - sg)))).astype(BF16)
        db_ref[...] = (dhid * (a * sg)).astype(BF16)

    ospec = pl.BlockSpec((None, tm, FF_SHARD), lambda j, i: (j, i, 0))
    return pl.pallas_call(
        body, name="ffn_down_bwd", grid=(N_SHARD, lp // tm),
        in_specs=[pl.BlockSpec((tm, D_MODEL), lambda j, i: (i, 0)),
                  pl.BlockSpec((None, FF_SHARD, D_MODEL), lambda j, i: (j, 0, 0)),
                  pl.BlockSpec((None, 2, tm, FF_SHARD), lambda j, i: (j, 0, i, 0))],
        out_specs=[ospec, ospec],
        out_shape=[jax.ShapeDtypeStruct((N_SHARD, lp, FF_SHARD), BF16)] * 2,
        compiler_params=_cparams(),
    )(dfb, wd, ab)


def ffn_dx(da, db, wg, wu, dz, lp):
    tm = _row_tile(lp)

    def body(da_ref, db_ref, wg_ref, wu_ref, dz_ref, o_ref):
        acc = ALPHA * dz_ref[...]
        for j in range(N_SHARD):
            acc = acc + _dot(da_ref[j], wg_ref[j], 1, 0) + _dot(db_ref[j], wu_ref[j], 1, 0)
        o_ref[...] = acc

    aspec = pl.BlockSpec((N_SHARD, tm, FF_SHARD), lambda i: (0, i, 0))
    wspec = pl.BlockSpec((N_SHARD, FF_SHARD, D_MODEL), lambda i: (0, 0, 0))
    row = pl.BlockSpec((tm, D_MODEL), lambda i: (i, 0))
    return pl.pallas_call(
        body, name="ffn_dx", grid=(lp // tm,), in_specs=[aspec, aspec, wspec, wspec, row], out_specs=row,
        out_shape=jax.ShapeDtypeStruct((lp, D_MODEL), F32), compiler_params=_cparams(),
    )(da, db, wg, wu, dz)


def ffn_fwd(h, hb, wg, wu, wd, g, b, lp):
    ab, hid = ffn_up(hb, wg, wu, lp)
    z, hn, hnb = mm_res_ln(hid, wd, h, g, b, scale=0.5, name="ffn_down_ln", lp=lp)
    return hn, hnb, dict(hb=hb, ab=ab, hid=hid, z=z)


def ffn_bwd(dh, sv, wg, wu, wd, g, lp):
    dz, dfb, dg, db = ln_bwd(dh, sv['z'], g, fscale=0.5, name="ffn_ln_bwd", lp=lp)
    da, dbb = ffn_down_bwd(dfb, wd, sv['ab'], lp)
    d_wd = matmul(sv['hid'], dfb[None], ta=True, ab='o', out_dtype=WGRAD, name="ffn_dwd")
    d_wg = matmul(da, sv['hb'][None], ta=True, ab='o', out_dtype=WGRAD, name="ffn_dwg")
    d_wu = matmul(dbb, sv['hb'][None], ta=True, ab='o', out_dtype=WGRAD, name="ffn_dwu")
    dh_in = ffn_dx(da, dbb, wg, wu, dz, lp)
    return dh_in, dict(wg=d_wg, wu=d_wu, wd=d_wd, ln_g=dg, ln_b=db)


def _rope_tables(lp):
    pos = np.arange(lp, dtype=np.float32) - PAD
    inv = ROPE_BASE ** (-np.arange(0, D_ROPE, 2, dtype=np.float32) / D_ROPE)
    ang = pos[:, None] * inv[None, :]
    cos = np.concatenate([np.cos(ang), np.cos(ang)], axis=1).astype(np.float32)
    sin = np.concatenate([np.sin(ang), np.sin(ang)], axis=1).astype(np.float32)
    rot = np.zeros((D_ROPE, D_ROPE), np.float32)
    half = D_ROPE // 2
    for j in range(half):
        rot[j + half, j] = -1.0
        rot[j, j + half] = 1.0
    d_qk = D_NOPE + D_ROPE
    cos_qk = np.concatenate([np.ones((lp, D_NOPE), np.float32), cos], axis=1)
    sin_qk = np.concatenate([np.zeros((lp, D_NOPE), np.float32), sin], axis=1)
    rot_qk = np.zeros((d_qk, d_qk), np.float32)
    rot_qk[D_NOPE:, D_NOPE:] = rot
    place = np.zeros((D_ROPE, d_qk), np.float32)
    place[:, D_NOPE:] = np.eye(D_ROPE, dtype=np.float32)
    return tuple(jnp.asarray(t) for t in (cos, sin, rot, cos_qk, sin_qk, rot_qk, place))


def _rot(x, rot):
    return _dot(x, rot, 1, 0, precision=lax.Precision.HIGHEST)


def _rms(x, g):
    r = lax.rsqrt(jnp.mean(x * x, axis=-1, keepdims=True) + RMS_EPS)
    return x * r * g


def mla_prep(proj, cos, sin, rot, qg, kvg, lp):
    def fn(row0, rv, pv):
        cq, krb, ckv, c, s = rv
        qg_, kvg_, rot_ = pv
        kr = krb[:, :D_ROPE]
        return ((_rms(cq, qg_), _rms(ckv, kvg_), kr * c + _rot(kr, rot_) * s), ())

    return rowwise(fn, [(proj, Q_RANK, 0), (proj, 128, 3), (proj, KV_RANK, 2), (cos, D_ROPE, 0), (sin, D_ROPE, 0)],
                   [qg, kvg, rot], [(Q_RANK, BF16), (KV_RANK, BF16), (D_ROPE, BF16)], name="mla_prep", lp=lp)


def mla_heads(cqn, ckvn, kr, cos_qk, sin_qk, rot_qk, place, wq, wk, wv, lp):
    tm = _row_tile(lp)

    def body(cq_ref, ckv_ref, kr_ref, c_ref, s_ref, rot_ref, place_ref, wq_ref, wk_ref, wv_ref, q_ref, k_ref, v_ref):
        cq = cq_ref[...]
        ckv = ckv_ref[...]
        kr_placed = _dot(kr_ref[...], place_ref[...].astype(BF16), 1, 0)
        for h in range(N_HEADS):
            q = _dot(cq, wq_ref[h], 1, 0)
            q_ref[h] = (q * c_ref[...] + _rot(q, rot_ref[...]) * s_ref[...]).astype(BF16)
            k_ref[h] = (_dot(ckv, wk_ref[h], 1, 0) + kr_placed).astype(BF16)
            v_ref[h] = _dot(ckv, wv_ref[h], 1, 0).astype(BF16)

    def row(w):
        return pl.BlockSpec((tm, w), lambda i: (i, 0))

    def whole(a):
        return pl.BlockSpec(a.shape, functools.partial(lambda i, nd: (0,) * nd, nd=a.ndim))

    def ospec(n):
        return pl.BlockSpec((N_HEADS, tm, n), lambda i: (0, i, 0))

    return pl.pallas_call(
        body, name="mla_heads", grid=(lp // tm,),
        in_specs=[row(Q_RANK), row(KV_RANK), row(D_ROPE), row(D_QK), row(D_QK), whole(rot_qk), whole(place),
                  whole(wq), whole(wk), whole(wv)],
        out_specs=[ospec(D_QK), ospec(D_QK), ospec(D_V)],
        out_shape=[jax.ShapeDtypeStruct((N_HEADS, lp, D_QK), BF16), jax.ShapeDtypeStruct((N_HEADS, lp, D_QK), BF16),
                   jax.ShapeDtypeStruct((N_HEADS, lp, D_V), BF16)],
        compiler_params=_cparams(),
    )(cqn, ckvn, kr, cos_qk, sin_qk, rot_qk, place, wq, wk, wv)


D_QK = D_NOPE + D_ROPE


def _att_probs(q, k, row0, tq, lp):
    s = _dot(q, k, 1, 1) * ATT_SCALE
    qi = row0 + lax.broadcasted_iota(jnp.int32, (tq, lp), 0)
    ki = lax.broadcasted_iota(jnp.int32, (tq, lp), 1)
    s = jnp.where((ki <= qi) & (ki >= PAD), s, -1e30)
    p = jnp.exp(s - jnp.max(s, axis=-1, keepdims=True))
    return p / jnp.sum(p, axis=-1, keepdims=True)


def _att_spec(lp, n):
    return pl.BlockSpec((None, lp, n), lambda h: (h, 0, 0))


def _att_tiles(lp):
    tiles, r = [(0, X0)], X0
    while r < lp:
        tiles.append((r, 256))
        r += 256
    assert r == lp
    return tiles


def attn_fwd(q, k, v, lp):
    def body(q_ref, k_ref, v_ref, o_ref):
        for r0, rows in _att_tiles(lp):
            ke, rq = r0 + rows, slice(r0, r0 + rows)
            p = _att_probs(q_ref[rq, :], k_ref[0:ke, :], r0, rows, ke)
            o_ref[rq, :] = _dot(p.astype(BF16), v_ref[0:ke, :], 1, 0).astype(BF16)

    return pl.pallas_call(
        body, name="attn_fwd", grid=(N_HEADS,),
        in_specs=[_att_spec(lp, D_QK), _att_spec(lp, D_QK), _att_spec(lp, D_V)],
        out_specs=_att_spec(lp, D_V), out_shape=jax.ShapeDtypeStruct((N_HEADS, lp, D_V), BF16),
        compiler_params=_cparams(),
    )(q, k, v)


def attn_bwd(q, k, v, do, lp):
    def body(q_ref, k_ref, v_ref, do_ref, dq_ref, dk_ref, dv_ref):
        dk_ref[...] = jnp.zeros_like(dk_ref)
        dv_ref[...] = jnp.zeros_like(dv_ref)
        for r0, rows in _att_tiles(lp):
            ke, rq = r0 + rows, slice(r0, r0 + rows)
            q_, do_, k_, v_ = q_ref[rq, :], do_ref[rq, :], k_ref[0:ke, :], v_ref[0:ke, :]
            p = _att_probs(q_, k_, r0, rows, ke)
            dp = _dot(do_, v_, 1, 1)
            delta = jnp.sum(p * dp, axis=-1, keepdims=True)
            ds = (p * (dp - delta) * ATT_SCALE).astype(BF16)
            dq_ref[rq, :] = _dot(ds, k_, 1, 0)
            dk_ref[0:ke, :] += _dot(ds, q_, 0, 0)
            dv_ref[0:ke, :] += _dot(p.astype(BF16), do_, 0, 0)

    qk, vv = _att_spec(lp, D_QK), _att_spec(lp, D_V)
    return pl.pallas_call(
        body, name="attn_bwd", grid=(N_HEADS,), in_specs=[qk, qk, vv, vv], out_specs=[qk, qk, vv],
        out_shape=[jax.ShapeDtypeStruct((N_HEADS, lp, D_QK), F32), jax.ShapeDtypeStruct((N_HEADS, lp, D_QK), F32),
                   jax.ShapeDtypeStruct((N_HEADS, lp, D_V), F32)],
        compiler_params=_cparams(),
    )(q, k, v, do)


def mla_heads_bwd(dq, dk, dv, cos_qk, sin_qk, rot_qk, place, wq, wk, wv, lp):
    tm = _row_tile(lp)

    def body(dq_ref, dk_ref, dv_ref, c_ref, s_ref, rot_ref, place_ref, wq_ref, wk_ref, wv_ref,
             dcq_ref, dckv_ref, dqp_ref, dkr_ref):
        dcq = jnp.zeros(dcq_ref.shape, F32)
        dckv = jnp.zeros(dckv_ref.shape, F32)
        dkr = jnp.zeros(dkr_ref.shape, F32)
        for h in range(N_HEADS):
            g = dq_ref[h]
            dqp = (g * c_ref[...] - _rot(g * s_ref[...], rot_ref[...])).astype(BF16)
            dqp_ref[h] = dqp
            dcq = dcq + _dot(dqp, wq_ref[h], 1, 1)
            dk_ = dk_ref[h]
            dckv = dckv + _dot(dk_.astype(BF16), wk_ref[h], 1, 1) + _dot(dv_ref[h].astype(BF16), wv_ref[h], 1, 1)
            dkr = dkr + _dot(dk_, place_ref[...], 1, 1, precision=lax.Precision.HIGHEST)
        dcq_ref[...] = dcq
        dckv_ref[...] = dckv
        dkr_ref[...] = dkr

    def hspec(n):
        return pl.BlockSpec((N_HEADS, tm, n), lambda i: (0, i, 0))

    def row(w):
        return pl.BlockSpec((tm, w), lambda i: (i, 0))

    def whole(a):
        return pl.BlockSpec(a.shape, functools.partial(lambda i, nd: (0,) * nd, nd=a.ndim))

    return pl.pallas_call(
        body, name="mla_heads_bwd", grid=(lp // tm,),
        in_specs=[hspec(D_QK), hspec(D_QK), hspec(D_V), row(D_QK), row(D_QK), whole(rot_qk), whole(place),
                  whole(wq), whole(wk), whole(wv)],
        out_specs=[row(Q_RANK), row(KV_RANK), hspec(D_QK), row(D_ROPE)],
        out_shape=[jax.ShapeDtypeStruct((lp, Q_RANK), F32), jax.ShapeDtypeStruct((lp, KV_RANK), F32),
                   jax.ShapeDtypeStruct((N_HEADS, lp, D_QK), BF16), jax.ShapeDtypeStruct((lp, D_ROPE), F32)],
        compiler_params=_cparams(),
    )(dq, dk, dv, cos_qk, sin_qk, rot_qk, place, wq, wk, wv)


def heads_out(o, wo, lp):
    tm = _row_tile(lp)

    def body(o_ref, w_ref, y_ref):
        acc = _dot(o_ref[0], w_ref[0], 1, 0)
        for h in range(1, N_HEADS):
            acc = acc + _dot(o_ref[h], w_ref[h], 1, 0)
        y_ref[...] = acc.astype(BF16)

    return pl.pallas_call(
        body, name="mla_out", grid=(lp // tm,),
        in_specs=[pl.BlockSpec((N_HEADS, tm, D_V), lambda i: (0, i, 0)), pl.BlockSpec(wo.shape, lambda i: (0, 0, 0))],
        out_specs=pl.BlockSpec((tm, D_MODEL), lambda i: (i, 0)), out_shape=jax.ShapeDtypeStruct((lp, D_MODEL), BF16),
        compiler_params=_cparams(),
    )(o, wo)


def heads_out_dx(dy, wo, lp):
    tm = _row_tile(lp)

    def body(dy_ref, w_ref, do_ref):
        dy_ = dy_ref[...]
        for h in range(N_HEADS):
            do_ref[h] = _dot(dy_, w_ref[h], 1, 1).astype(BF16)

    return pl.pallas_call(
        body, name="mla_out_dx", grid=(lp // tm,),
        in_specs=[pl.BlockSpec((tm, D_MODEL), lambda i: (i, 0)), pl.BlockSpec(wo.shape, lambda i: (0, 0, 0))],
        out_specs=pl.BlockSpec((N_HEADS, tm, D_V), lambda i: (0, i, 0)),
        out_shape=jax.ShapeDtypeStruct((N_HEADS, lp, D_V), BF16), compiler_params=_cparams(),
    )(dy, wo)


def _rms_bwd(dy, x, g):
    r = lax.rsqrt(jnp.mean(x * x, axis=-1, keepdims=True) + RMS_EPS)
    n = x * r
    dn = dy * g
    dx = r * (dn - n * jnp.mean(dn * n, axis=-1, keepdims=True))
    return dx, jnp.sum(dy * n, axis=0, keepdims=True)


def mla_prep_bwd(dcq, dckv, dkr, proj, cos, sin, rot, qg, kvg, lp):
    def fn(row0, rv, pv):
        dcq_, dckv_, dkr_, cq, ckv, c, s = rv
        qg_, kvg_, rot_ = pv
        dxq, dgq = _rms_bwd(dcq_, cq, qg_)
        dxkv, dgkv = _rms_bwd(dckv_, ckv, kvg_)
        dkr_raw = dkr_ * c - _rot(dkr_ * s, rot_)
        return ((dxq, dxkv, dkr_raw), (dgq, dgkv))

    return rowwise(fn, [(dcq, Q_RANK, 0), (dckv, KV_RANK, 0), (dkr, D_ROPE, 0), (proj, Q_RANK, 0), (proj, KV_RANK, 2),
                        (cos, D_ROPE, 0), (sin, D_ROPE, 0)], [qg, kvg, rot],
                   [(Q_RANK, BF16), (KV_RANK, BF16), (D_ROPE, BF16)], [((1, Q_RANK), F32), ((1, KV_RANK), F32)],
                   name="mla_prep_bwd", lp=lp)


def _shift_down(x, d, rows):
    return jnp.where(rows >= d, pltpu.roll(x, d, 0), 0.0)


def _shift_up(x, d, rows, n):
    return jnp.where(rows < n - d, pltpu.roll(x, n - d, 0), 0.0)


_CONV_W = 128
_XB, _BG, _CG = 1024 // _CONV_W, 1536 // _CONV_W, 2048 // _CONV_W


def _conv_specs(lp):
    def pspec(base):
        return pl.BlockSpec((lp, _CONV_W), functools.partial(lambda c, base: (0, base + c), base=base))

    col = pl.BlockSpec((lp, _CONV_W), lambda c: (0, c))
    wspec = pl.BlockSpec((3, _CONV_W), lambda c: (0, c))
    bspec = pl.BlockSpec((1, _CONV_W), lambda c: (0, c))
    return pspec, col, wspec, bspec


def _conv_core(xbar, cg, w, bias, lp):
    rows = lax.broadcasted_iota(jnp.int32, (lp, _CONV_W), 0)
    u = jnp.where(rows >= PAD, cg * xbar, 0.0)
    u1 = _shift_down(u, 1, rows)
    u2 = _shift_down(u, 2, rows)
    y = bias + w[0:1] * u2 + w[1:2] * u1 + w[2:3] * u
    return rows, u, u1, u2, y


def conv_fwd(proj, w, bias, lp):
    pspec, col, wspec, bspec = _conv_specs(lp)

    def body(x_ref, b_ref, c_ref, w_ref, bias_ref, v_ref):
        _, _, _, _, y = _conv_core(x_ref[...], c_ref[...], w_ref[...], bias_ref[...], lp)
        v_ref[...] = (b_ref[...] * y).astype(BF16)

    return pl.pallas_call(
        body, name="conv_fwd", grid=(MIX // _CONV_W,),
        in_specs=[pspec(_XB), pspec(_BG), pspec(_CG), wspec, bspec], out_specs=col,
        out_shape=jax.ShapeDtypeStruct((lp, MIX), BF16), compiler_params=_cparams(),
    )(proj, proj, proj, w, bias)


def conv_bwd(dv, proj, w, bias, lp):
    pspec, col, wspec, bspec = _conv_specs(lp)

    def body(dv_ref, x_ref, b_ref, c_ref, w_ref, bias_ref, dx_ref, db_ref, dc_ref, dw_ref, dbias_ref):
        xbar, cg, w_ = x_ref[...], c_ref[...], w_ref[...]
        rows, u, u1, u2, y = _conv_core(xbar, cg, w_, bias_ref[...], lp)
        dv_ = dv_ref[...]
        db_ref[...] = (dv_ * y).astype(BF16)
        dy = dv_ * b_ref[...]
        dbias_ref[...] = jnp.sum(dy, axis=0, keepdims=True)
        dw_ref[0:1, :] = jnp.sum(dy * u2, axis=0, keepdims=True)
        dw_ref[1:2, :] = jnp.sum(dy * u1, axis=0, keepdims=True)
        dw_ref[2:3, :] = jnp.sum(dy * u, axis=0, keepdims=True)
        du = w_[2:3] * dy + w_[1:2] * _shift_up(dy, 1, rows, lp) + w_[0:1] * _shift_up(dy, 2, rows, lp)
        du = jnp.where(rows >= PAD, du, 0.0)
        dc_ref[...] = (du * xbar).astype(BF16)
        dx_ref[...] = (du * cg).astype(BF16)

    return pl.pallas_call(
        body, name="conv_bwd", grid=(MIX // _CONV_W,),
        in_specs=[col, pspec(_XB), pspec(_BG), pspec(_CG), wspec, bspec],
        out_specs=[col, col, col, wspec, bspec],
        out_shape=[jax.ShapeDtypeStruct((lp, MIX), BF16)] * 3 + [jax.ShapeDtypeStruct((3, MIX), F32),
                                                                jax.ShapeDtypeStruct((1, MIX), F32)],
        compiler_params=_cparams(),
    )(dv, proj, proj, proj, w, bias)


def _s5_disc(a_re, a_im, log_dt, b_re, b_im):
    dt = jnp.exp(log_dt)
    mag = jnp.exp(dt * a_re)
    ab_re, ab_im = mag * jnp.cos(dt * a_im), mag * jnp.sin(dt * a_im)
    den = a_re * a_re + a_im * a_im
    nr, ni = ab_re - 1.0, ab_im
    coef_re = (nr * a_re + ni * a_im) / den
    coef_im = (ni * a_re - nr * a_im) / den
    return ab_re, ab_im, coef_re * b_re - coef_im * b_im, coef_re * b_im + coef_im * b_re


_S5_ROWS = S5_GROUPS * S5_GROUP


def s5_prep(a_re, a_im, log_dt, b_re, b_im):
    def body(ar, ai, ld, br, bi, o0, o1, o2, o3):
        for o, v in zip((o0, o1, o2, o3), _s5_disc(ar[...], ai[...], ld[...], br[...], bi[...])):
            o[...] = v

    return pl.pallas_call(body, name="s5_prep",
                          out_shape=[jax.ShapeDtypeStruct((_S5_ROWS, S5_STATE), F32)] * 4)(a_re, a_im, log_dt, b_re, b_im)


def s5_prep_bwd(a_re, a_im, log_dt, b_re, b_im, d_ab_re, d_ab_im, d_bb_re, d_bb_im, sel):
    def body(ar, ai, ld, br, bi, g0, g1, g2, g3, sel_ref, da_re, da_im, dld, dbr, dbi):
        _, vjp = jax.vjp(_s5_disc, ar[...], ai[...], ld[...], br[...], bi[...])
        c_ar, c_ai, c_ld, c_br, c_bi = vjp((g0[...], g1[...], g2[...], g3[...]))
        s = sel_ref[...]
        hi = lax.Precision.HIGHEST
        da_re[...] = _dot(s, c_ar, 1, 0, precision=hi)
        da_im[...] = _dot(s, c_ai, 1, 0, precision=hi)
        dld[...] = jnp.sum(_dot(s, c_ld, 1, 0, precision=hi), axis=-1, keepdims=True)
        dbr[...] = c_br
        dbi[...] = c_bi

    g = jax.ShapeDtypeStruct((S5_GROUPS, S5_STATE), F32)
    full = jax.ShapeDtypeStruct((_S5_ROWS, S5_STATE), F32)
    return pl.pallas_call(body, name="s5_prep_bwd",
                          out_shape=[g, g, jax.ShapeDtypeStruct((S5_GROUPS, 1), F32), full, full],
                          )(a_re, a_im, log_dt, b_re, b_im, d_ab_re, d_ab_im, d_bb_re, d_bb_im, sel)


_SCAN_W = 128
_SCAN_STEPS = int(math.log2(SCAN_CHUNK))


def _cmul(ar, ai, br, bi):
    return ar * br - ai * bi, ar * bi + ai * br


def _scan_powers(ar, ai, reverse):
    pw = [(ar, ai)]
    for _ in range(_SCAN_STEPS):
        pw.append(_cmul(*pw[-1], *pw[-1]))
    rows = lax.broadcasted_iota(jnp.int32, (SCAN_CHUNK, ar.shape[-1]), 0)
    tr = jnp.broadcast_to(ar, rows.shape)
    ti = jnp.broadcast_to(ai, rows.shape)
    for k in range(_SCAN_STEPS):
        d = 2 ** k
        if reverse:
            live = rows < SCAN_CHUNK - d
            mr, mi = _cmul(tr, ti, _shift_up(tr, d, rows, SCAN_CHUNK), _shift_up(ti, d, rows, SCAN_CHUNK))
        else:
            live = rows >= d
            mr, mi = _cmul(tr, ti, _shift_down(tr, d, rows), _shift_down(ti, d, rows))
        tr = jnp.where(live, mr, tr)
        ti = jnp.where(live, mi, ti)
    return pw, rows, tr, ti


def s5_scan(bu, ab_re, ab_im, lp):
    n_chunks = lp // SCAN_CHUNK

    def body(bu_ref, ar_ref, ai_ref, s_ref):
        ar, ai = ar_ref[...], ai_ref[...]
        pw, rows, tr, ti = _scan_powers(ar, ai, False)

        def chunk(ci, carry):
            cr, cim = carry
            r0 = pl.multiple_of(ci * SCAN_CHUNK, SCAN_CHUNK)
            xr = bu_ref[0, pl.ds(r0, SCAN_CHUNK), :]
            xi = bu_ref[1, pl.ds(r0, SCAN_CHUNK), :]
            for k in range(_SCAN_STEPS):
                d = 2 ** k
                mr, mi = _cmul(pw[k][0], pw[k][1], _shift_down(xr, d, rows), _shift_down(xi, d, rows))
                xr, xi = xr + mr, xi + mi
            mr, mi = _cmul(tr, ti, cr, cim)
            xr, xi = xr + mr, xi + mi
            s_ref[0, pl.ds(r0, SCAN_CHUNK), :] = xr
            s_ref[1, pl.ds(r0, SCAN_CHUNK), :] = xi
            return xr[SCAN_CHUNK - 1:SCAN_CHUNK, :], xi[SCAN_CHUNK - 1:SCAN_CHUNK, :]

        zero = jnp.zeros((1, _SCAN_W), F32)
        lax.fori_loop(0, n_chunks, chunk, (zero, zero))

    spec = pl.BlockSpec((2, lp, _SCAN_W), lambda c: (0, 0, c))
    aspec = pl.BlockSpec((1, _SCAN_W), lambda c: (0, c))
    return pl.pallas_call(
        body, name="s5_scan", grid=(S5_LANES // _SCAN_W,), in_specs=[spec, aspec, aspec], out_specs=spec,
        out_shape=jax.ShapeDtypeStruct((2, lp, S5_LANES), F32), compiler_params=_cparams(),
    )(bu, ab_re, ab_im)


def s5_scan_bwd(ds, s, ab_re, ab_im, lp):
    n_chunks = lp // SCAN_CHUNK

    def body(ds_ref, s_ref, ar_ref, ai_ref, g_ref, da_ref):
        ar, ai = ar_ref[...], -ai_ref[...]
        pw, rows, tr, ti = _scan_powers(ar, ai, True)

        def chunk(k, carry):
            cr, cim, dar, dai = carry
            ci = n_chunks - 1 - k
            r0 = pl.multiple_of(ci * SCAN_CHUNK, SCAN_CHUNK)
            xr = ds_ref[0, pl.ds(r0, SCAN_CHUNK), :]
            xi = ds_ref[1, pl.ds(r0, SCAN_CHUNK), :]
            for j in range(_SCAN_STEPS):
                d = 2 ** j
                mr, mi = _cmul(pw[j][0], pw[j][1], _shift_up(xr, d, rows, SCAN_CHUNK), _shift_up(xi, d, rows, SCAN_CHUNK))
                xr, xi = xr + mr, xi + mi
            mr, mi = _cmul(tr, ti, cr, cim)
            xr, xi = xr + mr, xi + mi
            g_ref[0, pl.ds(r0, SCAN_CHUNK), :] = xr
            g_ref[1, pl.ds(r0, SCAN_CHUNK), :] = xi
            prev0 = pl.multiple_of(jnp.maximum(r0 - 8, 0), 8)
            live = (ci > 0).astype(F32)
            pr = s_ref[0, pl.ds(prev0, 8), :][7:8, :] * live
            pim = s_ref[1, pl.ds(prev0, 8), :][7:8, :] * live
            sr = s_ref[0, pl.ds(r0, SCAN_CHUNK), :]
            si = s_ref[1, pl.ds(r0, SCAN_CHUNK), :]
            sr = jnp.where(rows >= 1, pltpu.roll(sr, 1, 0), pr)
            si = jnp.where(rows >= 1, pltpu.roll(si, 1, 0), pim)
            dar = dar + jnp.sum(xr * sr + xi * si, axis=0, keepdims=True)
            dai = dai + jnp.sum(xi * sr - xr * si, axis=0, keepdims=True)
            return xr[0:1, :], xi[0:1, :], dar, dai

        zero = jnp.zeros((1, _SCAN_W), F32)
        _, _, dar, dai = lax.fori_loop(0, n_chunks, chunk, (zero, zero, zero, zero))
        da_ref[0] = dar
        da_ref[1] = dai

    spec = pl.BlockSpec((2, lp, _SCAN_W), lambda c: (0, 0, c))
    aspec = pl.BlockSpec((1, _SCAN_W), lambda c: (0, c))
    return pl.pallas_call(
        body, name="s5_scan_bwd", grid=(S5_LANES // _SCAN_W,), in_specs=[spec, spec, aspec, aspec],
        out_specs=[spec, pl.BlockSpec((2, 1, _SCAN_W), lambda c: (0, 0, c))],
        out_shape=[jax.ShapeDtypeStruct((2, lp, S5_LANES), F32), jax.ShapeDtypeStruct((2, 1, S5_LANES), F32)],
        compiler_params=_cparams(),
    )(ds, s, ab_re, ab_im)


S5_BLOCKS = 4
_S5_PER = S5_GROUPS // S5_BLOCKS


def _blockdiag(x):
    _, r, c = x.shape
    eye = jnp.eye(_S5_PER, dtype=x.dtype)
    x = x.reshape(S5_BLOCKS, _S5_PER, r, c)
    return (x[:, :, :, None, :] * eye[None, :, None, :, None]).reshape(S5_BLOCKS, _S5_PER * r, _S5_PER * c)


def _blockdiag_extract(m, r, c):
    return jnp.einsum('qgrgc->qgrc', m.reshape(S5_BLOCKS, _S5_PER, r, _S5_PER, c)).reshape(S5_GROUPS, r, c)


def bd_matmul(a, w, *, w_t, reduce, res=None, name):
    _, M, _ = a.shape
    n_w, _, k1, k2 = w.shape
    ka, kout = (k2, k1) if w_t else (k1, k2)
    tm = _row_tile(M)
    n_out, n_red = (1, n_w) if reduce else (n_w, 1)
    has_res = res is not None

    assert n_red <= 2

    def body(*refs):
        a_ref, w_ref = refs[0], refs[1]
        o_ref = refs[3] if has_res else refs[2]
        for q in range(S5_BLOCKS):
            cols = slice(q * kout, (q + 1) * kout)
            part = _dot(a_ref[:, q * ka:(q + 1) * ka].astype(BF16), w_ref[q], 1, 1 if w_t else 0)
            if n_red == 1:
                o_ref[:, cols] = part
            else:
                acc_ref = refs[-1]

                @pl.when(pl.program_id(2) == 0)
                def _():
                    acc_ref[:, cols] = part

                @pl.when(pl.program_id(2) == 1)
                def _():
                    tot = acc_ref[:, cols] + part
                    o_ref[:, cols] = tot + refs[2][:, cols] if has_res else tot

    if reduce:
        a_map, w_map = (lambda o, i, r: (r, i, 0)), (lambda o, i, r: (r, 0, 0, 0))
    else:
        a_map, w_map = (lambda o, i, r: (0, i, 0)), (lambda o, i, r: (o, 0, 0, 0))
    o_map = lambda o, i, r: (o, i, 0)
    in_specs = [pl.BlockSpec((None, tm, S5_BLOCKS * ka), a_map), pl.BlockSpec((None, S5_BLOCKS, k1, k2), w_map)]
    operands = [a, w]
    if has_res:
        in_specs.append(pl.BlockSpec((None, tm, S5_BLOCKS * kout), o_map))
        operands.append(res)
    return pl.pallas_call(
        body, name=name, grid=(n_out, M // tm, n_red), in_specs=in_specs,
        out_specs=pl.BlockSpec((None, tm, S5_BLOCKS * kout), o_map),
        out_shape=jax.ShapeDtypeStruct((n_out, M, S5_BLOCKS * kout), F32),
        scratch_shapes=[pltpu.VMEM((tm, S5_BLOCKS * kout), F32)] if n_red > 1 else [],
        compiler_params=_cparams(),
    )(*operands)


def bd_outer(a, b, name):
    na, M, wa = a.shape
    nb_, _, wb = b.shape
    ka, kb = wa // S5_BLOCKS, wb // S5_BLOCKS
    n_out = max(na, nb_)

    def body(a_ref, b_ref, o_ref):
        o_ref[...] = _dot(a_ref[...].astype(BF16), b_ref[...].astype(BF16), 0, 0)

    return pl.pallas_call(
        body, name=name, grid=(n_out, S5_BLOCKS),
        in_specs=[pl.BlockSpec((None, M, ka), (lambda o, q: (o, 0, q)) if na > 1 else (lambda o, q: (0, 0, q))),
                  pl.BlockSpec((None, M, kb), (lambda o, q: (o, 0, q)) if nb_ > 1 else (lambda o, q: (0, 0, q)))],
        out_specs=pl.BlockSpec((None, None, ka, kb), lambda o, q: (o, q, 0, 0)),
        out_shape=jax.ShapeDtypeStruct((n_out, S5_BLOCKS, ka, kb), F32), compiler_params=_cparams(),
    )(a, b)


def s5_u(proj, lp):
    def fn(row0, rv, pv):
        u, = rv
        return ((jnp.where(_row_mask(row0, u.shape), u, 0.0),), ())

    return rowwise(fn, [(proj, MIX, 5)], [], [(MIX, BF16)], name="s5_u", lp=lp)[0]


def s5_y(ys, proj, d, lp):
    def fn(row0, rv, pv):
        ys_, u = rv
        y = ys_ + pv[0] * u
        return ((y, _gelu(y)), ())

    return rowwise(fn, [(ys, MIX, 0), (proj, MIX, 5)], [d], [(MIX, F32), (MIX, BF16)], name="s5_y", lp=lp)


def s5_glu(z, y, b, lp):
    def fn(row0, rv, pv):
        z_, y_ = rv
        return ((_gelu(y_) * _sigmoid(z_ + pv[0]),), ())

    return rowwise(fn, [(z, MIX, 0), (y, MIX, 0)], [b], [(MIX, BF16)], name="s5_glu", lp=lp)[0]


def s5_glu_bwd(dgl, z, y, b, lp):
    def fn(row0, rv, pv):
        dgl_, z_, y_ = rv
        sg = _sigmoid(z_ + pv[0])
        dz = dgl_ * _gelu(y_) * sg * (1.0 - sg)
        return ((dgl_ * sg, dz), (jnp.sum(dz, axis=0, keepdims=True),))

    return rowwise(fn, [(dgl, MIX, 0), (z, MIX, 0), (y, MIX, 0)], [b], [(MIX, F32), (MIX, BF16)], [((1, MIX), F32)],
                   name="s5_glu_bwd", lp=lp)


def s5_y_bwd(dyg, y, proj, d, lp):
    def fn(row0, rv, pv):
        dyg_, y_, u = rv
        dy = dyg_ * _gelu_grad(y_)
        return ((dy, dy * pv[0]), (jnp.sum(dy * u, axis=0, keepdims=True),))

    return rowwise(fn, [(dyg, MIX, 0), (y, MIX, 0), (proj, MIX, 5)], [d], [(MIX, BF16), (MIX, F32)], [((1, MIX), F32)],
                   name="s5_y_bwd", lp=lp)


def s5_du(du, lp):
    def fn(row0, rv, pv):
        return ((jnp.where(_row_mask(row0, rv[0].shape), rv[0], 0.0),), ())

    return rowwise(fn, [(du, MIX, 0)], [], [(MIX, BF16)], name="s5_du", lp=lp)[0]


def merge_fwd(proj, ya, yb, yc, lp):
    def fn(row0, rv, pv):
        g0, g1, g2, a, b, c = rv
        return ((_sigmoid(g0) * a + _sigmoid(g1) * b + _sigmoid(g2) * c,), ())

    return rowwise(fn, [(proj, D_MODEL, 3), (proj, D_MODEL, 4), (proj, D_MODEL, 5), (ya, D_MODEL, 0), (yb, D_MODEL, 0),
                        (yc, D_MODEL, 0)], [], [(D_MODEL, BF16)], name="merge_fwd", lp=lp)[0]


def merge_bwd(dmix, proj, ya, yb, yc, lp):
    def fn(row0, rv, pv):
        dm, g0, g1, g2, a, b, c = rv
        outs_y, outs_g = [], []
        for g, yv in ((g0, a), (g1, b), (g2, c)):
            sg = _sigmoid(g)
            outs_y.append(dm * sg)
            outs_g.append(dm * yv * sg * (1.0 - sg))
        return (tuple(outs_y) + (jnp.concatenate(outs_g, axis=1),), ())

    return rowwise(fn, [(dmix, D_MODEL, 0), (proj, D_MODEL, 3), (proj, D_MODEL, 4), (proj, D_MODEL, 5),
                        (ya, D_MODEL, 0), (yb, D_MODEL, 0), (yc, D_MODEL, 0)], [],
                   [(D_MODEL, BF16)] * 3 + [(P_IN // 2, BF16, P_IN, 1)], name="merge_bwd", lp=lp)


def dproj_fill(dproj, dcq, dkr, dckv, dxbar, dbg, dcg, du, place, lp):
    tm = _row_tile(lp)

    def body(dproj_ref, dcq_ref, dkr_ref, dckv_ref, dx_ref, db_ref, dc_ref, du_ref, place_ref, o_ref):
        o_ref[:, 0:384] = dcq_ref[...]
        o_ref[:, 384:512] = _dot(dkr_ref[...], place_ref[...], 1, 0).astype(BF16)
        o_ref[:, 512:768] = dckv_ref[...]
        o_ref[:, 768:1024] = jnp.zeros((tm, 256), BF16)
        o_ref[:, 1024:1536] = dx_ref[...]
        o_ref[:, 1536:2048] = db_ref[...]
        o_ref[:, 2048:2560] = dc_ref[...]
        o_ref[:, 2560:3072] = du_ref[...]

    def row(w):
        return pl.BlockSpec((tm, w), lambda i: (i, 0))

    return pl.pallas_call(
        body, name="dproj_fill", grid=(lp // tm,),
        in_specs=[pl.BlockSpec(memory_space=pl.ANY), row(Q_RANK), row(D_ROPE), row(KV_RANK), row(MIX), row(MIX), row(MIX),
                  row(MIX), pl.BlockSpec(place.shape, lambda i: (0, 0))],
        out_specs=row(P_IN // 2), out_shape=jax.ShapeDtypeStruct((lp, P_IN), BF16),
        input_output_aliases={0: 0}, compiler_params=_cparams(),
    )(dproj, dcq, dkr, dckv, dxbar, dbg, dcg, du, place)


def loss_head(h, tgt, lp):
    def fn(row0, rv, pv):
        h_, t_ = rv
        live = (row0 + lax.broadcasted_iota(jnp.int32, h_.shape, 0)) >= X0
        diff = jnp.where(live, h_ - t_, 0.0)
        ssq = jnp.sum(jnp.sum(diff * diff, axis=1, keepdims=True), axis=0, keepdims=True)
        return ((diff * (1.0 / D_MODEL),), (ssq * (0.5 / D_MODEL),))

    return rowwise(fn, [(h, D_MODEL, 0), (tgt, D_MODEL, 0)], [], [(D_MODEL, F32)], [((1, 1), F32)], name="loss_head", lp=lp)


def _s5_consts(W):
    ab_re_rep, ab_im_rep, bb_re, bb_im = s5_prep(W['s5_a_re'], W['s5_a_im'], W['s5_log_dt'], W['s5_b_re'], W['s5_b_im'])
    pick = lambda t: t.reshape(S5_GROUPS, S5_GROUP, S5_STATE)[:, 0].reshape(1, S5_LANES)
    bb = jnp.stack([_blockdiag(bb_re.reshape(S5_GROUPS, S5_GROUP, S5_STATE)),
                    _blockdiag(bb_im.reshape(S5_GROUPS, S5_GROUP, S5_STATE))]).astype(BF16)
    return pick(ab_re_rep), pick(ab_im_rep), bb


def layer_fwd(h, hb, W, tabs, lp, ffn1=None, ffn2=True):
    cos, sin, rot = tabs[:3]
    h1, h1b, sv1 = ffn1 if ffn1 is not None else ffn_fwd(h, hb, W['wg1'], W['wu1'], W['wd1'], W['ln1_g'], W['ln1_b'], lp)
    proj = matmul(h1b[None], W['w_in'][None], tb=True, name="proj")[0]
    cqn, ckvn, kr = mla_prep(proj, cos, sin, rot, W['q_norm_g'], W['kv_norm_g'], lp)
    q96, k96, v = mla_heads(cqn, ckvn, kr, *tabs[3:], W['wq'], W['wk'], W['wv'], lp)
    o = attn_fwd(q96, k96, v, lp)
    ya = heads_out(o, W['mla_wo'], lp)
    vconv = conv_fwd(proj, W['conv_w'], W['conv_b'], lp)
    yb = matmul(vconv[None], W['conv_wout'][None], out_dtype=BF16, name="conv_out")[0]
    ub = s5_u(proj, lp)
    ab_re, ab_im, bb = _s5_consts(W)
    bu = bd_matmul(ub[None], bb, w_t=False, reduce=False, name="s5_bu")
    s = s5_scan(bu, ab_re, ab_im, lp)
    ys = bd_matmul(s, W['s5_ct'], w_t=False, reduce=True, name="s5_cs")[0]
    y, ygb = s5_y(ys, proj, W['s5_d'], lp)
    zg = matmul(ygb[None], W['s5_wglu'][None], name="s5_glu_mm")[0]
    glb = s5_glu(zg, y, W['s5_b_glu'], lp)
    yc = matmul(glb[None], W['s5_wout'][None], out_dtype=BF16, name="s5_out")[0]
    mixed = merge_fwd(proj, ya, yb, yc, lp)
    z2, h2, h2b = mm_res_ln(mixed[None], W['w_o'][None], h1, W['ln2_g'], W['ln2_b'], scale=1.0, name="wo_ln", lp=lp)
    sv = dict(sv1=sv1, h1b=h1b, proj=proj, cqn=cqn, ckvn=ckvn, q96=q96, k96=k96, v=v, o=o, ya=ya,
              vconv=vconv, yb=yb, ub=ub, ab_re=ab_re, ab_im=ab_im, bb=bb, s=s, y=y, ygb=ygb, zg=zg, glb=glb, yc=yc,
              mixed=mixed, z2=z2)
    if not ffn2:
        return h2, h2b, sv
    h3, h3b, sv['sv3'] = ffn_fwd(h2, h2b, W['wg2'], W['wu2'], W['wd2'], W['ln3_g'], W['ln3_b'], lp)
    return h3, h3b, sv


def layer_bwd(dh3, sv, W, tabs, lp, ffn1=True, ffn2=True):
    cos, sin, rot = tabs[:3]
    proj = sv['proj']
    G = {}
    dh2 = dh3
    if ffn2:
        dh2, g3 = ffn_bwd(dh3, sv['sv3'], W['wg2'], W['wu2'], W['wd2'], W['ln3_g'], lp)
        G.update(wg2=g3['wg'], wu2=g3['wu'], wd2=g3['wd'], ln3_g=g3['ln_g'], ln3_b=g3['ln_b'])
    dz2, dz2b, G['ln2_g'], G['ln2_b'] = ln_bwd(dh2, sv['z2'], W['ln2_g'], fscale=1.0, name="wo_ln_bwd", lp=lp)
    dmix = matmul(dz2b[None], W['w_o'][None], tb=True, name="wo_dx")[0]
    G['w_o'] = matmul(sv['mixed'][None], dz2b[None], ta=True, out_dtype=WGRAD, name="wo_dw")[0]
    dya, dyb, dyc, dproj = merge_bwd(dmix, proj, sv['ya'], sv['yb'], sv['yc'], lp)
    dgl = matmul(dyc[None], W['s5_wout'][None], tb=True, name="s5_out_dx")[0]
    G['s5_wout'] = matmul(sv['glb'][None], dyc[None], ta=True, out_dtype=WGRAD, name="s5_out_dw")[0]
    t1, dzb, G['s5_b_glu'] = s5_glu_bwd(dgl, sv['zg'], sv['y'], W['s5_b_glu'], lp)
    dyg = matmul(dzb[None], W['s5_wglu'][None], tb=True, res=t1[None], name="s5_glu_dx")[0]
    G['s5_wglu'] = matmul(sv['ygb'][None], dzb[None], ta=True, out_dtype=WGRAD, name="s5_glu_dw")[0]
    dyb_, du_d, G['s5_d'] = s5_y_bwd(dyg, sv['y'], proj, W['s5_d'], lp)
    ds = bd_matmul(dyb_[None], W['s5_ct'], w_t=True, reduce=False, name="s5_cs_dx")
    G['s5_ct'] = bd_outer(sv['s'], dyb_[None], "s5_cs_dw")
    g_adj, d_ab = s5_scan_bwd(ds, sv['s'], sv['ab_re'], sv['ab_im'], lp)
    du = bd_matmul(g_adj, sv['bb'], w_t=True, reduce=True, res=du_d[None], name="s5_bu_dx")[0]
    d_bb = bd_outer(sv['ub'][None], g_adj, "s5_bu_dw")
    du_b = s5_du(du, lp)
    onehot = (jnp.arange(S5_GROUP) == 0).astype(F32)
    spread = lambda t: (t.reshape(S5_GROUPS, 1, S5_STATE) * onehot[None, :, None]).reshape(_S5_ROWS, S5_STATE)
    take = lambda t: _blockdiag_extract(t, S5_GROUP, S5_STATE).reshape(_S5_ROWS, S5_STATE)
    sel = jnp.kron(jnp.eye(S5_GROUPS, dtype=F32), jnp.ones((1, S5_GROUP), F32))
    (G['s5_a_re'], G['s5_a_im'], G['s5_log_dt'], G['s5_b_re'], G['s5_b_im']) = s5_prep_bwd(
        W['s5_a_re'], W['s5_a_im'], W['s5_log_dt'], W['s5_b_re'], W['s5_b_im'],
        spread(d_ab[0]), spread(d_ab[1]), take(d_bb[0]), take(d_bb[1]), sel)
    dv = matmul(dyb[None], W['conv_wout'][None], tb=True, name="conv_out_dx")[0]
    G['conv_wout'] = matmul(sv['vconv'][None], dyb[None], ta=True, out_dtype=WGRAD, name="conv_out_dw")[0]
    dxbar, dbg, dcg, G['conv_w'], G['conv_b'] = conv_bwd(dv, proj, W['conv_w'], W['conv_b'], lp)
    do = heads_out_dx(dya, W['mla_wo'], lp)
    G['mla_wo'] = matmul(sv['o'], dya[None], ta=True, ab='o', out_dtype=WGRAD, name="mla_out_dw")
    dq96, dk96, dvv = attn_bwd(sv['q96'], sv['k96'], sv['v'], do, lp)
    dcq, dckv, dqp, dkr = mla_heads_bwd(dq96, dk96, dvv, *tabs[3:], W['wq'], W['wk'], W['wv'], lp)
    G['wq'] = matmul(sv['cqn'][None], dqp, ta=True, bb='o', out_dtype=WGRAD, name="mla_dwq")
    G['wk'] = matmul(sv['ckvn'][None], dk96, ta=True, bb='o', out_dtype=WGRAD, name="mla_dwk")
    G['wv'] = matmul(sv['ckvn'][None], dvv, ta=True, bb='o', out_dtype=WGRAD, name="mla_dwv")
    dcq_raw, dckv_raw, dkr_raw, G['q_norm_g'], G['kv_norm_g'] = mla_prep_bwd(
        dcq, dckv, dkr, proj, cos, sin, rot, W['q_norm_g'], W['kv_norm_g'], lp)
    dproj = dproj_fill(dproj, dcq_raw, dkr_raw, dckv_raw, dxbar, dbg, dcg, du_b, jnp.eye(D_ROPE, 128, dtype=BF16), lp)
    dh1 = matmul(dproj[None], W['w_in'][None], res=dz2[None], res_scale=ALPHA, name="proj_dx")[0]
    G['w_in'] = matmul(dproj[None], sv['h1b'][None], ta=True, out_dtype=WGRAD, name="proj_dw")[0]
    if not ffn1:
        return dh1, G
    dh0, g1 = ffn_bwd(dh1, sv['sv1'], W['wg1'], W['wu1'], W['wd1'], W['ln1_g'], lp)
    G.update(wg1=g1['wg'], wu1=g1['wu'], wd1=g1['wd'], ln1_g=g1['ln_g'], ln1_b=g1['ln_b'])
    return dh0, G


def _nat_cols(st):
    return jnp.transpose(st, (1, 0, 2)).reshape(st.shape[1], -1)


def _shard_cols(nat):
    k, n = nat.shape
    return jnp.transpose(nat.reshape(k, N_SHARD, n // N_SHARD), (1, 0, 2))


def _win_pad(wt):
    z = lambda n: jnp.zeros((n, wt.shape[1]), wt.dtype)
    return jnp.concatenate([wt[0:384], wt[640:672], z(96), wt[384:640], z(256), wt[672:]], axis=0)


def _win_unpad(wp):
    return jnp.concatenate([wp[0:384], wp[512:768], wp[384:416], wp[1024:]], axis=0)


_BIG = [('ffn1_w_gate', 'T'), ('ffn1_w_up', 'T'), ('ffn1_w_down', 0), ('w_in', 'T'), ('mla_w_uq', 1), ('mla_w_ukv', 1),
        ('mla_w_o', 1), ('conv_w_out', 1), ('s5_w_glu', 0), ('s5_w_out', 1), ('w_o', 0),
        ('ffn2_w_gate', 'T'), ('ffn2_w_up', 'T'), ('ffn2_w_down', 0)]
_REPL = ['ln1_g', 'ln1_b', 'mla_q_norm_g', 'mla_kv_norm_g', 'conv_b', 's5_a_re', 's5_a_im', 's5_log_dt', 's5_b_re',
         's5_b_im', 's5_c_re', 's5_c_im', 's5_d', 's5_b_glu', 'ln2_g', 'ln2_b', 'ln3_g', 'ln3_b']


def compute_weights(st, small):
    W = {}
    for t in ('1', '2'):
        if 'ffn%s_w_gate' % t in st:
            W['wg' + t], W['wu' + t], W['wd' + t] = (st['ffn%s_w_%s' % (t, p)] for p in ('gate', 'up', 'down'))
    if 'w_in' in st:
        W.update(_mixer_weights(st))
    if small is not None:
        W.update(_small_weights(small))
    return W


def _mixer_weights(st):
    W = {}
    W['w_in'] = _win_pad(st['w_in'].reshape(D_IN, D_MODEL))
    W['wq'] = jnp.transpose(_nat_cols(st['mla_w_uq']).reshape(Q_RANK, N_HEADS, D_QK), (1, 0, 2))
    ukv = jnp.transpose(_nat_cols(st['mla_w_ukv']).reshape(KV_RANK, N_HEADS, D_NOPE + D_V), (1, 0, 2))
    W['wk'] = jnp.concatenate([ukv[:, :, :D_NOPE], jnp.zeros((N_HEADS, KV_RANK, D_ROPE), ukv.dtype)], axis=2)
    W['wv'] = ukv[:, :, D_NOPE:]
    W['mla_wo'] = _nat_cols(st['mla_w_o']).reshape(N_HEADS, D_V, D_MODEL)
    W['conv_wout'] = _nat_cols(st['conv_w_out'])
    W['s5_wglu'] = st['s5_w_glu'].reshape(MIX, MIX)
    W['s5_wout'] = _nat_cols(st['s5_w_out'])
    W['w_o'] = st['w_o'].reshape(D_MODEL, D_MODEL)
    return W


def _small_weights(small):
    W = {}
    W['conv_w'] = small['conv_w']
    for n in ('ln1_g', 'ln1_b', 'ln2_g', 'ln2_b', 'ln3_g', 'ln3_b', 'conv_b', 's5_b_glu'):
        W[n] = small[n].reshape(1, -1)
    W['q_norm_g'] = small['mla_q_norm_g'].reshape(1, -1)
    W['kv_norm_g'] = small['mla_kv_norm_g'].reshape(1, -1)
    W['s5_d'] = small['s5_d'].reshape(1, MIX)
    rep = lambda t: jnp.repeat(t, S5_GROUP, axis=0)
    W['s5_a_re'], W['s5_a_im'] = rep(small['s5_a_re']), rep(small['s5_a_im'])
    W['s5_log_dt'] = jnp.broadcast_to(rep(small['s5_log_dt'].reshape(S5_GROUPS, 1)), (_S5_ROWS, S5_STATE))
    tr = lambda t: jnp.transpose(t, (0, 2, 1)).reshape(_S5_ROWS, S5_STATE)
    W['s5_b_re'], W['s5_b_im'] = tr(small['s5_b_re']), tr(small['s5_b_im'])
    ct = lambda t: _blockdiag(jnp.transpose(t, (0, 2, 1)))
    W['s5_ct'] = jnp.stack([ct(small['s5_c_re']), -ct(small['s5_c_im'])]).astype(BF16)
    return W


def reference_grads(G, ffn=True):
    R = {}
    for t in ('1', '2') if ffn else ():
        R['ffn%s_w_gate' % t] = G['wg' + t].reshape(D_FF, D_MODEL).T
        R['ffn%s_w_up' % t] = G['wu' + t].reshape(D_FF, D_MODEL).T
        R['ffn%s_w_down' % t] = G['wd' + t].reshape(D_FF, D_MODEL)
    R['w_in_t'] = _win_unpad(G['w_in'])
    if ffn:
        R['w_in'] = R['w_in_t'].T
    R['mla_w_uq'] = jnp.transpose(G['wq'], (1, 0, 2)).reshape(Q_RANK, -1)
    R['mla_w_ukv'] = jnp.transpose(jnp.concatenate([G['wk'][:, :, :D_NOPE], G['wv']], axis=2), (1, 0, 2)).reshape(KV_RANK, -1)
    R['mla_w_o'] = G['mla_wo'].reshape(N_HEADS * D_V, D_MODEL)
    R['conv_w'], R['conv_w_out'] = G['conv_w'], G['conv_wout']
    R['s5_w_glu'], R['s5_w_out'], R['w_o'] = G['s5_wglu'], G['s5_wout'], G['w_o']
    for n in ('ln1_g', 'ln1_b', 'ln2_g', 'ln2_b', 'ln3_g', 'ln3_b', 'conv_b', 's5_b_glu'):
        if n in G:
            R[n] = G[n].reshape(-1)
    R['mla_q_norm_g'], R['mla_kv_norm_g'] = G['q_norm_g'].reshape(-1), G['kv_norm_g'].reshape(-1)
    R['s5_d'] = G['s5_d'].reshape(S5_GROUPS, S5_GROUP)
    R['s5_a_re'], R['s5_a_im'], R['s5_log_dt'] = G['s5_a_re'], G['s5_a_im'], G['s5_log_dt'].reshape(-1)
    untr = lambda t: jnp.transpose(t.reshape(S5_GROUPS, S5_GROUP, S5_STATE), (0, 2, 1))
    R['s5_b_re'], R['s5_b_im'] = untr(G['s5_b_re']), untr(G['s5_b_im'])
    unct = lambda t: jnp.transpose(_blockdiag_extract(t, S5_STATE, S5_GROUP), (0, 2, 1))
    R['s5_c_re'], R['s5_c_im'] = unct(G['s5_ct'][0]), -unct(G['s5_ct'][1])
    return R


_ANY = pl.BlockSpec(memory_space=pl.ANY)
LANES = 1024


def _place():
    x, y, c = lax.axis_index("x"), lax.axis_index("y"), lax.axis_index("c")
    chips = [(1 - x, y), (x, 1 - y), (1 - x, 1 - y)]
    return x, y, c, chips


def _rows_of(c, half):
    return pl.ds(pl.multiple_of(c * half, 8), half)


def all_gather_shards(srcs, exact):
    n, m = len(srcs), len(exact)
    halves = [s.shape[0] // 2 for s in srcs]

    def body(*refs):
        s_refs, e_refs = refs[:n], refs[n:n + m]
        o_refs, eo_refs = refs[n + m:2 * n + m], refs[2 * n + m:2 * n + 2 * m]
        send, recv, esend, erecv, osend, orecv, lsem = refs[2 * n + 2 * m:]
        x, y, c, chips = _place()
        me = 2 * x + y
        sibling = (x, y, 1 - c)
        own = [pltpu.make_async_remote_copy(src_ref=s_refs[k], dst_ref=o_refs[k].at[me], send_sem=osend.at[k],
                                            recv_sem=orecv.at[k], device_id=sibling, device_id_type=MESH) for k in range(n)]
        local = [pltpu.make_async_copy(e_refs[k], eo_refs[k].at[me], lsem.at[k]) for k in range(m)]
        for cp in own + local:
            cp.start()

        def copy(k, s, src, idx, half_c, to):
            return pltpu.make_async_remote_copy(
                src_ref=src, dst_ref=o_refs[k].at[idx, _rows_of(half_c, halves[k])], send_sem=send.at[6 * k + s],
                recv_sem=recv.at[6 * k + s], device_id=to, device_id_type=MESH)

        def ecopy(k, j, idx, to):
            return pltpu.make_async_remote_copy(src_ref=e_refs[k], dst_ref=eo_refs[k].at[idx], send_sem=esend.at[3 * k + j],
                                                recv_sem=erecv.at[3 * k + j], device_id=to, device_id_type=MESH)

        sends = []
        for k in range(n):
            mine = s_refs[k].at[_rows_of(c, halves[k])]
            sends += [copy(k, j, mine, me, c, (*chip, c)) for j, chip in enumerate(chips)]
        for k in range(m):
            sends += [ecopy(k, j, me, (*chip, c)) for j, chip in enumerate(chips)]
        for cp in sends:
            cp.start()
        for j, chip in enumerate(chips):
            idx = 2 * chip[0] + chip[1]
            for k in range(n):
                landed = o_refs[k].at[idx, _rows_of(c, halves[k])]
                copy(k, j, landed, idx, c, sibling).wait_recv()
                fwd = copy(k, 3 + j, landed, idx, c, sibling)
                fwd.start()
                sends.append(fwd)
        for j, chip in enumerate(chips):
            idx = 2 * chip[0] + chip[1]
            for k in range(n):
                copy(k, 3 + j, s_refs[k].at[_rows_of(c, halves[k])], idx, 1 - c, sibling).wait_recv()
            for k in range(m):
                ecopy(k, j, idx, sibling).wait_recv()
        for cp in sends:
            cp.wait_send()
        for cp in own + local:
            cp.wait()

    outs = pl.pallas_call(
        body, name="all_gather_weights", in_specs=[_ANY] * (n + m), out_specs=[_ANY] * (n + m),
        out_shape=[jax.ShapeDtypeStruct((N_SHARD,) + a.shape, a.dtype) for a in list(srcs) + list(exact)],
        scratch_shapes=[pltpu.SemaphoreType.DMA((6 * n,)), pltpu.SemaphoreType.DMA((6 * n,)),
                        pltpu.SemaphoreType.DMA((3 * m,)), pltpu.SemaphoreType.DMA((3 * m,)),
                        pltpu.SemaphoreType.DMA((n,)), pltpu.SemaphoreType.DMA((n,)), pltpu.SemaphoreType.DMA((m,))],
    )(*srcs, *exact)
    return outs[:n], outs[n:]


def rs_pair_swap(gs):
    n = len(gs)

    def body(*refs):
        g_refs, r_refs, send, recv = refs[:n], refs[n:2 * n], refs[2 * n], refs[2 * n + 1]
        x, y, c, _ = _place()
        copies = [pltpu.make_async_remote_copy(
            src_ref=g_refs[k].at[pl.ds(0, N_SHARD), _rows_of(1 - c, gs[k].shape[1] // 2)], dst_ref=r_refs[k],
            send_sem=send.at[k], recv_sem=recv.at[k], device_id=(x, y, 1 - c), device_id_type=MESH) for k in range(n)]
        for cp in copies:
            cp.start()
        for cp in copies:
            cp.wait()

    return pl.pallas_call(
        body, name="grad_pair_swap", in_specs=[_ANY] * n, out_specs=[_ANY] * n,
        out_shape=[jax.ShapeDtypeStruct((N_SHARD, g.shape[1] // 2, g.shape[2]), g.dtype) for g in gs],
        scratch_shapes=[pltpu.SemaphoreType.DMA((n,)), pltpu.SemaphoreType.DMA((n,))],
    )(*gs)


def _group_tile(half, n_cols, n_arrays):
    budget = (20 * 2 ** 20) // (6 * n_arrays)
    fits = [t for t in range(8, half + 1, 8) if half % t == 0 and t * n_cols * 4 <= budget]
    return max(fits) if fits else 8


def rs_pair_add(gs, rs, cidx, out_dtype, name):
    n = len(gs)
    _, K, cols = gs[0].shape
    half = K // 2
    tr = _group_tile(half, cols, n)
    nb = half // tr

    def body(c_ref, *refs):
        for g_ref, r_ref, o_ref in zip(refs[:n], refs[n:2 * n], refs[2 * n:]):
            o_ref[...] = (g_ref[...].astype(F32) + r_ref[...].astype(F32)).astype(out_dtype)

    gspec = pl.BlockSpec((None, tr, cols), lambda j, i, c: (j, c[0] * nb + i, 0))
    rspec = pl.BlockSpec((None, tr, cols), lambda j, i, c: (j, i, 0))
    return pl.pallas_call(
        body, name=name,
        grid_spec=pltpu.PrefetchScalarGridSpec(num_scalar_prefetch=1, grid=(N_SHARD, nb), in_specs=[gspec] * n + [rspec] * n,
                                               out_specs=[rspec] * n),
        out_shape=[jax.ShapeDtypeStruct((N_SHARD, half, cols), out_dtype)] * n,
        compiler_params=_cparams(),
    )(cidx, *gs, *rs)


def rs_chip_sum(qs, nl, cidx, name):
    n = len(qs)
    _, half, cols = qs[0].shape
    tr = _group_tile(half, cols, n)
    nb = half // tr

    def body(c_ref, *refs):
        for k, q_ref in enumerate(refs[:n]):
            o_ref = refs[n + k // nl]
            o_ref[k % nl] = ((q_ref[0].astype(F32) + q_ref[1].astype(F32)) + q_ref[2].astype(F32)) + q_ref[3].astype(F32)

    return pl.pallas_call(
        body, name=name,
        grid_spec=pltpu.PrefetchScalarGridSpec(
            num_scalar_prefetch=1, grid=(nb,),
            in_specs=[pl.BlockSpec((N_SHARD, tr, cols), lambda i, c: (0, i, 0))] * n,
            out_specs=[pl.BlockSpec((nl, tr, cols), lambda i, c: (0, c[0] * nb + i, 0))] * (n // nl)),
        out_shape=[jax.ShapeDtypeStruct((nl, 2 * half, cols), F32)] * (n // nl),
        compiler_params=_cparams(),
    )(cidx, *qs)


def rs_pair_gather(fs, name):
    n = len(fs)

    def body(*refs):
        f_refs, send, recv = refs[n:2 * n], refs[2 * n], refs[2 * n + 1]
        x, y, c, _ = _place()
        copies = []
        for k in range(n):
            rows = f_refs[k].at[pl.ds(0, fs[k].shape[0]), _rows_of(c, fs[k].shape[1] // 2)]
            copies.append(pltpu.make_async_remote_copy(src_ref=rows, dst_ref=rows, send_sem=send.at[k], recv_sem=recv.at[k],
                                                       device_id=(x, y, 1 - c), device_id_type=MESH))
        for cp in copies:
            cp.start()
        for cp in copies:
            cp.wait()

    return pl.pallas_call(
        body, name=name, in_specs=[_ANY] * n, out_specs=[_ANY] * n,
        out_shape=[jax.ShapeDtypeStruct(f.shape, f.dtype) for f in fs],
        input_output_aliases={k: k for k in range(n)},
        scratch_shapes=[pltpu.SemaphoreType.DMA((n,)), pltpu.SemaphoreType.DMA((n,))],
    )(*fs)


_HBM = pl.BlockSpec(memory_space=pltpu.HBM)
_SEM = pl.BlockSpec(memory_space=pltpu.SEMAPHORE)
_EFFECT = pltpu.SideEffectType.DATAFLOW_SIDE_EFFECTING


def _in_hbm(a):
    return pltpu.with_memory_space_constraint(a, pltpu.HBM)


def split_start(name, srcs, lands, after, copies_fn, n_copies):
    n = len(srcs)

    def body(*refs):
        for cp in copies_fn(refs[:n], refs[n:2 * n], refs[2 * n + 1], refs[2 * n + 2]):
            cp.start()
        refs[-1][...] = jnp.zeros_like(refs[-1])

    bufs = list(srcs) + list(lands)
    outs = pl.pallas_call(
        body, name=name,
        out_shape=(pltpu.SemaphoreType.DMA((n_copies,)), pltpu.SemaphoreType.DMA((n_copies,)),
                   *[pltpu.HBM(a.shape, a.dtype) for a in bufs], jax.ShapeDtypeStruct((8, 128), F32)),
        in_specs=[_HBM] * (2 * n) + [_ANY],
        out_specs=(_SEM, _SEM, *[_HBM] * (2 * n), pl.BlockSpec(memory_space=pltpu.VMEM)),
        input_output_aliases={i: 2 + i for i in range(2 * n)},
        compiler_params=pltpu.CompilerParams(has_side_effects=_EFFECT),
    )(*[_in_hbm(a) for a in bufs], after)
    return outs[0], outs[1], outs[2:2 + n], outs[2 + n:2 + 2 * n], outs[-1]


def split_wait(name, send, recv, srcs, lands, after, copies_fn, which=None):
    n = len(srcs)

    def body(*refs):
        copies = copies_fn(refs[:n], refs[n:2 * n], refs[2 * n], refs[2 * n + 1], which)
        for cp in copies:
            cp.wait_send()
        for cp in copies:
            cp.wait_recv()

    bufs = list(srcs) + list(lands)
    outs = pl.pallas_call(
        body, name=name, out_shape=tuple(pltpu.HBM(a.shape, a.dtype) for a in bufs),
        in_specs=[_HBM] * (2 * n) + [_SEM, _SEM, _ANY], out_specs=tuple([_HBM] * (2 * n)),
        input_output_aliases={i: i for i in range(2 * n)},
        compiler_params=pltpu.CompilerParams(has_side_effects=_EFFECT),
    )(*bufs, send, recv, after)
    return list(outs[:n]), list(outs[n:])


def _gather_copies(s_refs, l_refs, send, recv, which=None):
    x, y, c, chips = _place()
    me = 2 * x + y
    out = []
    for k in (range(len(s_refs)) if which is None else which):
        s, l = s_refs[k], l_refs[k]
        rows = _rows_of(c, s.shape[0] // 2)
        for j, chip in enumerate(chips):
            out.append(pltpu.make_async_remote_copy(src_ref=s.at[rows], dst_ref=l.at[me, rows], send_sem=send.at[4 * k + j],
                                                    recv_sem=recv.at[4 * k + j], device_id=(*chip, c), device_id_type=MESH))
        out.append(pltpu.make_async_remote_copy(src_ref=s, dst_ref=l.at[me], send_sem=send.at[4 * k + 3],
                                                recv_sem=recv.at[4 * k + 3], device_id=(x, y, 1 - c), device_id_type=MESH))
    return out


def _scatter_copies(s_refs, l_refs, send, recv, which=None):
    x, y, c, chips = _place()
    me = 2 * x + y
    return [pltpu.make_async_remote_copy(src_ref=s_refs[k].at[2 * chip[0] + chip[1]], dst_ref=l_refs[k].at[me],
                                         send_sem=send.at[3 * k + j], recv_sem=recv.at[3 * k + j], device_id=(*chip, c),
                                         device_id_type=MESH)
            for k in (range(len(s_refs)) if which is None else which) for j, chip in enumerate(chips)]


def gather_forward(lands, name):
    n = len(lands)

    def body(*refs):
        l_refs, send, recv = refs[n:2 * n], refs[2 * n], refs[2 * n + 1]
        x, y, c, chips = _place()
        copies = []
        for k in range(n):
            rows = _rows_of(c, lands[k].shape[1] // 2)
            for j, chip in enumerate(chips):
                part = l_refs[k].at[2 * chip[0] + chip[1], rows]
                copies.append(pltpu.make_async_remote_copy(src_ref=part, dst_ref=part, send_sem=send.at[3 * k + j],
                                                           recv_sem=recv.at[3 * k + j], device_id=(x, y, 1 - c),
                                                           device_id_type=MESH))
        for cp in copies:
            cp.start()
        for cp in copies:
            cp.wait()

    return pl.pallas_call(
        body, name=name, in_specs=[_ANY] * n, out_specs=[_ANY] * n,
        out_shape=[jax.ShapeDtypeStruct(a.shape, a.dtype) for a in lands],
        input_output_aliases={k: k for k in range(n)},
        scratch_shapes=[pltpu.SemaphoreType.DMA((3 * n,)), pltpu.SemaphoreType.DMA((3 * n,))],
    )(*lands)


def rs_partials(gs, wire, cidx, tag):
    rs = rs_pair_swap(gs)
    groups = {}
    for k, g in enumerate(gs):
        groups.setdefault((g.shape, jnp.dtype(wire[k]).name), []).append(k)
    ps = [None] * len(gs)
    for gi, ks in enumerate(groups.values()):
        outs = rs_pair_add([gs[k] for k in ks], [rs[k] for k in ks], cidx, wire[ks[0]], "grad_pair_add_%s%d" % (tag, gi))
        for k, o in zip(ks, outs):
            ps[k] = o
    return ps


def rs_finish(items, tag):
    cidx = lax.axis_index("c").astype(jnp.int32).reshape(1)
    groups = {}
    for i, it in enumerate(items):
        groups.setdefault((it[0].shape, len(it), it[0].dtype.name), []).append(i)
    fs = [None] * len(items)
    for gi, ids in enumerate(groups.values()):
        outs = rs_chip_sum([q for i in ids for q in items[i]], len(items[ids[0]]), cidx, "grad_chip_sum_%s%d" % (tag, gi))
        for i, o in zip(ids, outs):
            fs[i] = o
    return rs_pair_gather(fs, "grad_pair_gather_" + tag)


def adamw(w, g, m, v, name):
    shape = w.shape
    if w.ndim == 2:
        block, grid, index = shape, (1,), (lambda i: (0, 0))
    else:
        slab = shape[2:]
        unit = 4 * int(np.prod(slab[:-2] or (1,))) * (-(-slab[-1] // 128) * 128)
        if len(slab) >= 2:
            unit *= -(-slab[-2] // 8) * 8
        k = shape[1]
        tr = k
        if k * unit > 2 ** 21:
            tr = max(t for t in range(8, k, 8) if k % t == 0 and t * unit <= 2 ** 21)
        block, grid = (None, tr) + tuple(slab), (shape[0], k // tr)
        index = lambda l, i: (l, i) + (0,) * len(slab)
        if tr < min(k, 64) and len(slab) == 1:
            tc = max(t for t in range(128, slab[0] + 1, 128) if slab[0] % t == 0 and k * t * 4 <= 2 ** 21)
            block, grid = (None, k, tc), (shape[0], slab[0] // tc)
            index = lambda l, i: (l, 0, i)

    def body(w_ref, g_ref, m_ref, v_ref, d_ref, nm_ref, nv_ref):
        g_ = g_ref[...]
        m_new = ADAM_B1 * m_ref[...] + (1.0 - ADAM_B1) * g_
        v_new = ADAM_B2 * v_ref[...] + (1.0 - ADAM_B2) * (g_ * g_)
        m_hat = m_new / (1.0 - ADAM_B1 ** ADAM_STEP)
        v_hat = v_new / (1.0 - ADAM_B2 ** ADAM_STEP)
        d_ref[...] = -ADAM_LR * (m_hat / (jnp.sqrt(v_hat) + ADAM_EPS) + ADAM_WD * w_ref[...])
        nm_ref[...] = m_new
        nv_ref[...] = v_new

    spec = pl.BlockSpec(block, index)
    return pl.pallas_call(
        body, name=name, grid=grid, in_specs=[spec] * 4, out_specs=[spec] * 3,
        out_shape=[jax.ShapeDtypeStruct(shape, F32)] * 3, compiler_params=_cparams(),
    )(w, g, m, v)


_WEIGHTS = ['meta', 'ffn1_w_gate', 'ffn1_w_up', 'ffn1_w_down', 'ln1_g', 'ln1_b', 'w_in', 'mla_q_norm_g', 'mla_w_uq',
            'mla_kv_norm_g', 'mla_w_ukv', 'mla_w_o', 'conv_w', 'conv_b', 'conv_w_out', 's5_a_re', 's5_a_im', 's5_log_dt',
            's5_b_re', 's5_b_im', 's5_c_re', 's5_c_im', 's5_d', 's5_w_glu', 's5_b_glu', 's5_w_out', 'w_o', 'ln2_g', 'ln2_b',
            'ffn2_w_gate', 'ffn2_w_up', 'ffn2_w_down', 'ln3_g', 'ln3_b']


def _pad_to(flat, n):
    return jnp.concatenate([flat, jnp.zeros((n - flat.shape[0],), flat.dtype)])


def _shard_of(full, axis):
    if axis == 1:
        return _shard_cols(full)
    if axis == 'T':
        return full.T.reshape(N_SHARD, full.shape[1] // N_SHARD, full.shape[0])
    return full.reshape(N_SHARD, full.shape[0] // N_SHARD, full.shape[1])


_FFN_KEY = {'gate': 'wg', 'up': 'wu', 'down': 'wd'}


def _pad_rows(a, axis):
    k = a.shape[axis]
    extra = -k % 32
    if not extra:
        return a
    return jnp.pad(a, [(0, extra) if d == axis else (0, 0) for d in range(a.ndim)])


def _step(env):
    w = {n: env[n] for n in _WEIGHTS}
    mom = {n: env['m_' + n] for n in _WEIGHTS}
    var = {n: env['v_' + n] for n in _WEIGHTS}
    cidx = lax.axis_index("c").astype(jnp.int32).reshape(1)
    chip = 2 * lax.axis_index("x") + lax.axis_index("y")
    big_names = [n for n, _ in _BIG]
    nb = len(big_names)

    kept_t = [n for n, a in _BIG if a == 'T']
    own = {n: (jnp.swapaxes(w[n], 1, 2) if n in kept_t else w[n]) for n in big_names}
    first = [n for n in big_names if n.startswith('ffn1')]
    mix = [n for n in big_names if not n.startswith('ffn')]
    last = [n for n in big_names if n.startswith('ffn2')]
    rest = mix + last
    nm, nr = len(mix), len(mix) + len(last)
    src = lambda n, li: _pad_rows(own[n][li].astype(BF16), 0)
    gathered_first, (conv_w_st, meta_st) = all_gather_shards([src(n, 0) for n in first], [w['conv_w'], w['meta']])
    later = [src(n, 0) for n in rest] + [src(n, 1) for n in big_names]
    lands = [lax.empty((N_SHARD,) + s.shape, BF16) for s in later]
    g_send, g_recv, later_t, lands_t, token = split_start("gather_start", later, lands, gathered_first[0], _gather_copies,
                                                          4 * len(later))

    def weights_of(names, st, li, with_small):
        small = None
        if with_small:
            small = {n: w[n][li] for n in _REPL}
            small['conv_w'] = _nat_cols(conv_w_st[:, li])
        return compute_weights({n: a[:, :own[n].shape[1]] for n, a in zip(names, st)}, small)

    x2d = env['x'][0]
    lp = x2d.shape[0] + X0
    tabs = _rope_tables(lp)
    h = jnp.concatenate([jnp.zeros((PAD, D_MODEL), F32), _nat_cols(meta_st), x2d], axis=0) + token[0, 0]
    W0 = weights_of(first, gathered_first, 0, True)
    ffn1 = ffn_fwd(h, h.astype(BF16), W0['wg1'], W0['wu1'], W0['wd1'], W0['ln1_g'], W0['ln1_b'], lp)
    later_t, lands_t = split_wait("gather0_wait", g_send, g_recv, later_t, lands_t, ffn1[0], _gather_copies, range(nm))
    W0.update(weights_of(mix, gather_forward(lands_t[:nm], "gather0_forward"), 0, False))
    h, hb, sv0 = layer_fwd(None, None, W0, tabs, lp, ffn1=ffn1, ffn2=False)
    later_t, lands_t = split_wait("gather0b_wait", g_send, g_recv, later_t, lands_t, h, _gather_copies, range(nm, nr))
    W0.update(weights_of(last, gather_forward(lands_t[nm:nr], "gather0b_forward"), 0, False))
    h, hb, sv0['sv3'] = ffn_fwd(h, hb, W0['wg2'], W0['wu2'], W0['wd2'], W0['ln3_g'], W0['ln3_b'], lp)
    _, lands_t = split_wait("gather1_wait", g_send, g_recv, later_t, lands_t, h, _gather_copies, range(nr, len(later)))
    W1 = weights_of(big_names, gather_forward(lands_t[nr:], "gather1_forward"), 1, True)
    h, hb, sv1 = layer_fwd(h, hb, W1, tabs, lp)
    tgt = jnp.concatenate([jnp.zeros((X0, D_MODEL), F32), env['loss_target'][0]], axis=0)
    dh, loss_part = loss_head(h, tgt, lp)
    loss = lax.psum(loss_part[0, 0], ("x", "y", "c"))

    def shards(G, names):
        full = None if all(n.startswith('ffn') for n in names) else reference_grads(G, ffn=False)

        def one(n, a):
            if n.startswith('ffn'):
                return G[_FFN_KEY[n.split('_')[-1]] + n[3]]
            if n == 'w_in':
                return full['w_in_t'].reshape(N_SHARD, D_IN // N_SHARD, D_MODEL)
            return _shard_of(full[n], a)

        return [_pad_rows(one(n, a), 1) for n, a in _BIG if n in names]

    def scatter_start(name, ps, after):
        qs = [lax.dynamic_update_slice_in_dim(jnp.zeros_like(p), lax.dynamic_slice_in_dim(p, chip, 1, axis=0), chip, axis=0)
              for p in ps]
        return split_start(name, ps, qs, after, _scatter_copies, 3 * len(ps))

    dh, G1 = layer_bwd(dh, sv1, W1, tabs, lp)
    p1 = rs_partials(shards(G1, big_names), [BF16] * nb, cidx, "b")
    s1_send, s1_recv, p1_t, q1_t, token1 = scatter_start("scatter1_start", p1, dh)
    dh, g3 = ffn_bwd(dh, sv0['sv3'], W0['wg2'], W0['wu2'], W0['wd2'], W0['ln3_g'] + token1[0, 0], lp)
    G0 = dict(wg2=g3['wg'], wu2=g3['wu'], wd2=g3['wd'])
    p0l = rs_partials(shards(G0, last), [BF16] * len(last), cidx, "c")
    sl_send, sl_recv, p0l_t, q0l_t, token0l = scatter_start("scatter0b_start", p0l, dh)
    dh, Gm = layer_bwd(dh, sv0, dict(W0, ln2_g=W0['ln2_g'] + token0l[0, 0]), tabs, lp, ffn1=False, ffn2=False)
    G0.update(Gm, ln3_g=g3['ln_g'], ln3_b=g3['ln_b'])
    p0m = rs_partials(shards(G0, mix), [BF16] * nm, cidx, "d")
    sm_send, sm_recv, p0m_t, q0m_t, token0m = scatter_start("scatter0_start", p0m, dh)
    dh, g1 = ffn_bwd(dh, sv0['sv1'], W0['wg1'], W0['wu1'], W0['wd1'], W0['ln1_g'] + token0m[0, 0], lp)
    G0.update(wg1=g1['wg'], wu1=g1['wu'], wd1=g1['wd'], ln1_g=g1['ln_g'], ln1_b=g1['ln_b'])
    _, q1 = split_wait("scatter1_wait", s1_send, s1_recv, p1_t, q1_t, dh, _scatter_copies)
    _, q0_last = split_wait("scatter0b_wait", sl_send, sl_recv, p0l_t, q0l_t, dh, _scatter_copies)
    _, q0_mix = split_wait("scatter0_wait", sm_send, sm_recv, p0m_t, q0m_t, dh, _scatter_copies)
    q0_rest = list(q0_mix) + list(q0_last)
    full = [reference_grads(G0, ffn=False), reference_grads(G1, ffn=False)]

    s_parts = [jnp.stack([full[li][n] for li in range(DEPTH)]).reshape(-1) for n in _REPL + ['conv_w']]
    s_parts.append(dh[PAD:X0].reshape(-1))
    s_sizes = [int(p.shape[0]) for p in s_parts]
    s_rows = -(-sum(s_sizes) // (16 * LANES)) * 16
    g_small = _pad_to(jnp.concatenate(s_parts), s_rows * LANES).reshape(1, s_rows, LANES)
    g_small = jnp.broadcast_to(g_small, (N_SHARD, s_rows, LANES))
    p_last = rs_partials(shards(G0, first) + [g_small], [BF16] * len(first) + [F32], cidx, "a")
    z_send, z_recv, pz_t, qz_t, token_z = scatter_start("scatter_last_start", p_last, dh)

    def step_weights(names, grad):
        out = {}
        for n in names:
            if n in kept_t:
                res = adamw(own[n], grad[n], jnp.swapaxes(mom[n], 1, 2), jnp.swapaxes(var[n], 1, 2), "adamw_" + n)
                out[n] = [jnp.swapaxes(t, 1, 2) for t in [grad[n]] + list(res)]
            else:
                out[n] = [grad[n]] + list(adamw(w[n], grad[n], mom[n], var[n], "adamw_" + n))
        return out

    q1 = dict(zip(big_names, q1))
    q0 = dict(zip(rest, q0_rest))
    q0[rest[0]] = q0[rest[0]] + token_z[0, 0].astype(BF16)
    red = rs_finish([[q0[n], q1[n]] for n in rest], "a")
    done = step_weights(rest, {n: r[:, :own[n].shape[1]] for n, r in zip(rest, red)})
    all_done = jnp.stack([done[n][3][(0,) * done[n][3].ndim] for n in rest])
    _, q_last = split_wait("scatter_last_wait", z_send, z_recv, pz_t, qz_t, all_done, _scatter_copies)
    red = rs_finish([[q, q1[n]] for n, q in zip(first, q_last)] + [[q_last[-1]]], "b")
    f_small = red[-1].reshape(-1)

    grad = {n: r[:, :own[n].shape[1]] for n, r in zip(first, red)}
    off = 0
    for n, sz in zip(_REPL + ['conv_w', 'meta'], s_sizes):
        grad[n] = f_small[off:off + sz]
        off += sz
    for n in _REPL:
        grad[n] = grad[n].reshape(w[n].shape)
    cw = grad['conv_w'].reshape(DEPTH, 3, MIX)
    grad['conv_w'] = lax.dynamic_slice_in_dim(cw, chip * (MIX // N_SHARD), MIX // N_SHARD, axis=2)
    gm = grad['meta'].reshape(N_META, D_MODEL)
    grad['meta'] = lax.dynamic_slice_in_dim(gm, chip * (D_MODEL // N_SHARD), D_MODEL // N_SHARD, axis=1)

    done.update(step_weights([n for n in _WEIGHTS if n not in done], grad))
    return (loss, dh[X0:][None], *[done[n][k] for k in range(4) for n in _WEIGHTS])


def kernel(x, meta, ffn1_w_gate, ffn1_w_up, ffn1_w_down, ln1_g, ln1_b, w_in, mla_q_norm_g, mla_w_uq, mla_kv_norm_g, mla_w_ukv, mla_w_o, conv_w, conv_b, conv_w_out, s5_a_re, s5_a_im, s5_log_dt, s5_b_re, s5_b_im, s5_c_re, s5_c_im, s5_d, s5_w_glu, s5_b_glu, s5_w_out, w_o, ln2_g, ln2_b, ffn2_w_gate, ffn2_w_up, ffn2_w_down, ln3_g, ln3_b, loss_target, m_meta, m_ffn1_w_gate, m_ffn1_w_up, m_ffn1_w_down, m_ln1_g, m_ln1_b, m_w_in, m_mla_q_norm_g, m_mla_w_uq, m_mla_kv_norm_g, m_mla_w_ukv, m_mla_w_o, m_conv_w, m_conv_b, m_conv_w_out, m_s5_a_re, m_s5_a_im, m_s5_log_dt, m_s5_b_re, m_s5_b_im, m_s5_c_re, m_s5_c_im, m_s5_d, m_s5_w_glu, m_s5_b_glu, m_s5_w_out, m_w_o, m_ln2_g, m_ln2_b, m_ffn2_w_gate, m_ffn2_w_up, m_ffn2_w_down, m_ln3_g, m_ln3_b, v_meta, v_ffn1_w_gate, v_ffn1_w_up, v_ffn1_w_down, v_ln1_g, v_ln1_b, v_w_in, v_mla_q_norm_g, v_mla_w_uq, v_mla_kv_norm_g, v_mla_w_ukv, v_mla_w_o, v_conv_w, v_conv_b, v_conv_w_out, v_s5_a_re, v_s5_a_im, v_s5_log_dt, v_s5_b_re, v_s5_b_im, v_s5_c_re, v_s5_c_im, v_s5_d, v_s5_w_glu, v_s5_b_glu, v_s5_w_out, v_w_o, v_ln2_g, v_ln2_b, v_ffn2_w_gate, v_ffn2_w_up, v_ffn2_w_down, v_ln3_g, v_ln3_b):
    return _step(dict(locals()))
```
